```python
import math
import jax, jax.numpy as jnp
from jax import lax
import numpy as np

D_MODEL = 1024
BATCH = 8
SEQ = 2048
DEPTH = 1

EPS = 1e-6
PLE_DIM = 256
S5_GROUP_CH = 16
S5_GROUPS = D_MODEL // 32
S5_WIDTH = S5_GROUPS * S5_GROUP_CH
S5_STATE = 64
LRU_HEAD_DIM = 64
LRU_WIDTH = D_MODEL
LRU_HEADS = LRU_WIDTH // LRU_HEAD_DIM
LRU_C = 8.0
CONV_WIDTH = 4
FFN_HIDDEN = -(-8 * D_MODEL // (3 * 256)) * 256
IN_COLS = S5_WIDTH + LRU_WIDTH + 2 * D_MODEL

kernel_name = "hybrid_s5_rglru_gated_block"


def rms_norm(x, g):
    xf = x.astype(jnp.float32)
    y = xf * lax.rsqrt(jnp.mean(xf * xf, axis=-1, keepdims=True) + EPS)
    return (y * g.astype(jnp.float32)).astype(x.dtype)


def s5_mixer(u, lam_re, lam_im, log_dt, b_re, b_im, c_re, c_im, d_skip, w_glu, b_glu):
    f32 = jnp.float32
    bsz, L, _ = u.shape
    uf = u.astype(f32).reshape(bsz, L, S5_GROUPS, S5_GROUP_CH)
    lr = lam_re.astype(f32)
    li = lam_im.astype(f32)
    dt = jnp.exp(log_dt.astype(f32))[:, None]
    mag = jnp.exp(lr * dt)
    ar = mag * jnp.cos(li * dt)
    ai = mag * jnp.sin(li * dt)
    den = lr * lr + li * li
    nr = ar - 1.0
    fr = (nr * lr + ai * li) / den
    fi = (ai * lr - nr * li) / den
    br = b_re.astype(f32)
    bi = b_im.astype(f32)
    bbr = fr[..., None] * br - fi[..., None] * bi
    bbi = fr[..., None] * bi + fi[..., None] * br
    xr = jnp.einsum('blgp,gnp->blgn', uf, bbr)
    xi = jnp.einsum('blgp,gnp->blgn', uf, bbi)
    a_r = jnp.broadcast_to(ar, (1, L) + ar.shape)
    a_i = jnp.broadcast_to(ai, (1, L) + ai.shape)

    def combine(e1, e2):
        a1r, a1i, b1r, b1i = e1
        a2r, a2i, b2r, b2i = e2
        return (a2r * a1r - a2i * a1i,
                a2r * a1i + a2i * a1r,
                a2r * b1r - a2i * b1i + b2r,
                a2r * b1i + a2i * b1r + b2i)

    _, _, sr, si = lax.associative_scan(combine, (a_r, a_i, xr, xi), axis=1)
    y = (jnp.einsum('blgn,gpn->blgp', sr, c_re.astype(f32))
         - jnp.einsum('blgn,gpn->blgp', si, c_im.astype(f32))
         + d_skip.astype(f32) * uf)
    y = y.reshape(bsz, L, S5_WIDTH)
    z = jax.nn.gelu(y)
    out = z * jax.nn.sigmoid(z @ w_glu.astype(f32) + b_glu.astype(f32))
    return out.astype(u.dtype)


def rglru_mixer(u, conv_w, conv_b, w_r, b_r, w_i, b_i, lru_lambda):
    f32 = jnp.float32
    xc = lax.conv_general_dilated(
        u, conv_w[:, None, :].astype(u.dtype), window_strides=(1,),
        padding=[(CONV_WIDTH - 1, 0)], dimension_numbers=('NWC', 'WIO', 'NWC'),
        feature_group_count=LRU_WIDTH) + conv_b
    bsz, L, _ = xc.shape
    xh = xc.astype(f32).reshape(bsz, L, LRU_HEADS, LRU_HEAD_DIM)
    r = jax.nn.sigmoid(jnp.einsum('blhi,hij->blhj', xh, w_r.astype(f32)) + b_r.astype(f32))
    ig = jax.nn.sigmoid(jnp.einsum('blhi,hij->blhj', xh, w_i.astype(f32)) + b_i.astype(f32))
    log_a = -LRU_C * r * jax.nn.softplus(-lru_lambda.astype(f32).reshape(LRU_HEADS, LRU_HEAD_DIM))
    a = jnp.exp(log_a)
    mult = jnp.sqrt(-jnp.expm1(2.0 * log_a))
    bx = mult * ig * xh

    def combine(e1, e2):
        a1, b1 = e1
        a2, b2 = e2
        return a2 * a1, a2 * b1 + b2

    _, h = lax.associative_scan(combine, (a, bx), axis=1)
    return h.reshape(bsz, L, LRU_WIDTH).astype(u.dtype)


def _fwd_setup_inputs(seed: int = 0) -> dict:
    key = jax.random.key(seed)
    ks = jax.random.split(key, 40)
    f32 = jnp.float32

    def nrm(k, shape, scale):
        return jax.random.normal(k, shape, f32) * scale

    def gain(k, shape):
        return 1.0 + 0.01 * jax.random.normal(k, shape, f32)

    G, N, P = S5_GROUPS, S5_STATE, S5_GROUP_CH
    lam_re = -0.5 + 0.01 * jax.random.normal(ks[3], (DEPTH, G, N), f32)
    lam_im = jnp.pi * jnp.arange(N, dtype=f32)[None, None, :] + 0.01 * jax.random.normal(ks[4], (DEPTH, G, N), f32)
    log_dt = jax.random.uniform(ks[5], (DEPTH, G), f32, math.log(1e-3), math.log(1e-1))
    u_a = jax.random.uniform(ks[16], (DEPTH, LRU_WIDTH), f32, 0.9, 0.999)
    a_base = u_a ** (1.0 / LRU_C)
    lru_lambda = jnp.log(a_base) - jnp.log1p(-a_base)

    return {
        "x": nrm(ks[0], (BATCH, SEQ, D_MODEL), 1.0),
        "p": nrm(ks[1], (DEPTH, BATCH, SEQ, PLE_DIM), 1.0),
        "g_mix": gain(ks[2], (DEPTH, D_MODEL)),
        "w_in": nrm(ks[6], (DEPTH, D_MODEL, IN_COLS), D_MODEL ** -0.5),
        "b_in": nrm(ks[7], (DEPTH, IN_COLS), 0.01),
        "lam_re": lam_re,
        "lam_im": lam_im,
        "log_dt": log_dt,
        "s5_b_re": nrm(ks[8], (DEPTH, G, N, P), (2 * P) ** -0.5),
        "s5_b_im": nrm(ks[9], (DEPTH, G, N, P), (2 * P) ** -0.5),
        "s5_c_re": nrm(ks[10], (DEPTH, G, P, N), N ** -0.5),
        "s5_c_im": nrm(ks[11], (DEPTH, G, P, N), N ** -0.5),
        "s5_d": nrm(ks[12], (DEPTH, G, P), 1.0),
        "w_glu": nrm(ks[13], (DEPTH, S5_WIDTH, S5_WIDTH), S5_WIDTH ** -0.5),
        "b_glu": nrm(ks[14], (DEPTH, S5_WIDTH), 0.01),
        "conv_w": nrm(ks[15], (DEPTH, CONV_WIDTH, LRU_WIDTH), CONV_WIDTH ** -0.5),
        "conv_b": nrm(ks[17], (DEPTH, LRU_WIDTH), 0.01),
        "w_r": nrm(ks[18], (DEPTH, LRU_HEADS, LRU_HEAD_DIM, LRU_HEAD_DIM), LRU_HEAD_DIM ** -0.5),
        "b_r": nrm(ks[19], (DEPTH, LRU_HEADS, LRU_HEAD_DIM), 0.01),
        "w_i": nrm(ks[20], (DEPTH, LRU_HEADS, LRU_HEAD_DIM, LRU_HEAD_DIM), LRU_HEAD_DIM ** -0.5),
        "b_i": nrm(ks[21], (DEPTH, LRU_HEADS, LRU_HEAD_DIM), 0.01),
        "lru_lambda": lru_lambda,
        "w_a_out": nrm(ks[22], (DEPTH, S5_WIDTH, D_MODEL), S5_WIDTH ** -0.5),
        "w_b_out": nrm(ks[23], (DEPTH, LRU_WIDTH, D_MODEL), LRU_WIDTH ** -0.5),
        "w_o": nrm(ks[24], (DEPTH, D_MODEL, D_MODEL), D_MODEL ** -0.5),
        "g_ffn": gain(ks[25], (DEPTH, D_MODEL)),
        "w_ffn_gate": nrm(ks[26], (DEPTH, D_MODEL, FFN_HIDDEN), D_MODEL ** -0.5),
        "w_ffn_up": nrm(ks[27], (DEPTH, D_MODEL, FFN_HIDDEN), D_MODEL ** -0.5),
        "w_ffn_down": nrm(ks[28], (DEPTH, FFN_HIDDEN, D_MODEL), FFN_HIDDEN ** -0.5),
        "g_ple_gate": gain(ks[29], (DEPTH, D_MODEL)),
        "w_ple_gate": nrm(ks[30], (DEPTH, D_MODEL, D_MODEL), D_MODEL ** -0.5),
        "b_ple_gate": nrm(ks[31], (DEPTH, D_MODEL), 0.01),
        "w_ple": nrm(ks[32], (DEPTH, PLE_DIM, D_MODEL), PLE_DIM ** -0.5),
        "g_ple": gain(ks[33], (DEPTH, D_MODEL)),
        "g_final": gain(ks[34], (D_MODEL,)),
    }


def _fwd_reference(x, p, g_mix, w_in, b_in, lam_re, lam_im, log_dt, s5_b_re, s5_b_im,
              s5_c_re, s5_c_im, s5_d, w_glu, b_glu, conv_w, conv_b, w_r, b_r, w_i, b_i,
              lru_lambda, w_a_out, w_b_out, w_o, g_ffn, w_ffn_gate, w_ffn_up, w_ffn_down,
              g_ple_gate, w_ple_gate, b_ple_gate, w_ple, g_ple, g_final):
    s_a = S5_WIDTH
    s_b = S5_WIDTH + LRU_WIDTH
    s_g = s_b + D_MODEL
    for i in range(DEPTH):
        h = rms_norm(x, g_mix[i])
        z = h @ w_in[i] + b_in[i]
        u_a = z[..., :s_a]
        u_b = z[..., s_a:s_b]
        gate_a = jax.nn.sigmoid(z[..., s_b:s_g])
        gate_b = jax.nn.sigmoid(z[..., s_g:])
        y_a = s5_mixer(u_a, lam_re[i], lam_im[i], log_dt[i], s5_b_re[i], s5_b_im[i],
                       s5_c_re[i], s5_c_im[i], s5_d[i], w_glu[i], b_glu[i])
        y_b = rglru_mixer(u_b, conv_w[i], conv_b[i], w_r[i], b_r[i], w_i[i], b_i[i],
                          lru_lambda[i])
        merged = gate_a * (y_a @ w_a_out[i]) + gate_b * (y_b @ w_b_out[i])
        x = x + merged @ w_o[i]
        h2 = rms_norm(x, g_ffn[i])
        x = x + (jax.nn.silu(h2 @ w_ffn_gate[i]) * (h2 @ w_ffn_up[i])) @ w_ffn_down[i]
        gate_p = jax.nn.sigmoid(rms_norm(x, g_ple_gate[i]) @ w_ple_gate[i] + b_ple_gate[i])
        e = rms_norm(p[i] @ w_ple[i], g_ple[i])
        x = x + gate_p * e
    return rms_norm(x, g_final)


import jax as _jax
import jax.numpy as _jnp

TWIN_FORMAT = 'train_step'
FWD_PARAMS = ['x', 'p', 'g_mix', 'w_in', 'b_in', 'lam_re', 'lam_im', 'log_dt', 's5_b_re', 's5_b_im', 's5_c_re', 's5_c_im', 's5_d', 'w_glu', 'b_glu', 'conv_w', 'conv_b', 'w_r', 'b_r', 'w_i', 'b_i', 'lru_lambda', 'w_a_out', 'w_b_out', 'w_o', 'g_ffn', 'w_ffn_gate', 'w_ffn_up', 'w_ffn_down', 'g_ple_gate', 'w_ple_gate', 'b_ple_gate', 'w_ple', 'g_ple', 'g_final']
TWIN_WEIGHTS = ['g_mix', 'w_in', 'b_in', 'lam_re', 'lam_im', 'log_dt', 's5_b_re', 's5_b_im', 's5_c_re', 's5_c_im', 's5_d', 'w_glu', 'b_glu', 'conv_w', 'conv_b', 'w_r', 'b_r', 'w_i', 'b_i', 'lru_lambda', 'w_a_out', 'w_b_out', 'w_o', 'g_ffn', 'w_ffn_gate', 'w_ffn_up', 'w_ffn_down', 'g_ple_gate', 'w_ple_gate', 'b_ple_gate', 'w_ple', 'g_ple', 'g_final']
TWIN_DIFF_INPUT = 'x'
TWIN_INPUTS = ['x', 'p', 'g_mix', 'w_in', 'b_in', 'lam_re', 'lam_im', 'log_dt', 's5_b_re', 's5_b_im', 's5_c_re', 's5_c_im', 's5_d', 'w_glu', 'b_glu', 'conv_w', 'conv_b', 'w_r', 'b_r', 'w_i', 'b_i', 'lru_lambda', 'w_a_out', 'w_b_out', 'w_o', 'g_ffn', 'w_ffn_gate', 'w_ffn_up', 'w_ffn_down', 'g_ple_gate', 'w_ple_gate', 'b_ple_gate', 'w_ple', 'g_ple', 'g_final', 'loss_target', 'm_g_mix', 'm_w_in', 'm_b_in', 'm_lam_re', 'm_lam_im', 'm_log_dt', 'm_s5_b_re', 'm_s5_b_im', 'm_s5_c_re', 'm_s5_c_im', 'm_s5_d', 'm_w_glu', 'm_b_glu', 'm_conv_w', 'm_conv_b', 'm_w_r', 'm_b_r', 'm_w_i', 'm_b_i', 'm_lru_lambda', 'm_w_a_out', 'm_w_b_out', 'm_w_o', 'm_g_ffn', 'm_w_ffn_gate', 'm_w_ffn_up', 'm_w_ffn_down', 'm_g_ple_gate', 'm_w_ple_gate', 'm_b_ple_gate', 'm_w_ple', 'm_g_ple', 'm_g_final', 'v_g_mix', 'v_w_in', 'v_b_in', 'v_lam_re', 'v_lam_im', 'v_log_dt', 'v_s5_b_re', 'v_s5_b_im', 'v_s5_c_re', 'v_s5_c_im', 'v_s5_d', 'v_w_glu', 'v_b_glu', 'v_conv_w', 'v_conv_b', 'v_w_r', 'v_b_r', 'v_w_i', 'v_b_i', 'v_lru_lambda', 'v_w_a_out', 'v_w_b_out', 'v_w_o', 'v_g_ffn', 'v_w_ffn_gate', 'v_w_ffn_up', 'v_w_ffn_down', 'v_g_ple_gate', 'v_w_ple_gate', 'v_b_ple_gate', 'v_w_ple', 'v_g_ple', 'v_g_final']
TWIN_OUTPUTS = ['loss', 'grad_x', 'grad_g_mix', 'grad_w_in', 'grad_b_in', 'grad_lam_re', 'grad_lam_im', 'grad_log_dt', 'grad_s5_b_re', 'grad_s5_b_im', 'grad_s5_c_re', 'grad_s5_c_im', 'grad_s5_d', 'grad_w_glu', 'grad_b_glu', 'grad_conv_w', 'grad_conv_b', 'grad_w_r', 'grad_b_r', 'grad_w_i', 'grad_b_i', 'grad_lru_lambda', 'grad_w_a_out', 'grad_w_b_out', 'grad_w_o', 'grad_g_ffn', 'grad_w_ffn_gate', 'grad_w_ffn_up', 'grad_w_ffn_down', 'grad_g_ple_gate', 'grad_w_ple_gate', 'grad_b_ple_gate', 'grad_w_ple', 'grad_g_ple', 'grad_g_final', 'delta_g_mix', 'delta_w_in', 'delta_b_in', 'delta_lam_re', 'delta_lam_im', 'delta_log_dt', 'delta_s5_b_re', 'delta_s5_b_im', 'delta_s5_c_re', 'delta_s5_c_im', 'delta_s5_d', 'delta_w_glu', 'delta_b_glu', 'delta_conv_w', 'delta_conv_b', 'delta_w_r', 'delta_b_r', 'delta_w_i', 'delta_b_i', 'delta_lru_lambda', 'delta_w_a_out', 'delta_w_b_out', 'delta_w_o', 'delta_g_ffn', 'delta_w_ffn_gate', 'delta_w_ffn_up', 'delta_w_ffn_down', 'delta_g_ple_gate', 'delta_w_ple_gate', 'delta_b_ple_gate', 'delta_w_ple', 'delta_g_ple', 'delta_g_final', 'new_m_g_mix', 'new_m_w_in', 'new_m_b_in', 'new_m_lam_re', 'new_m_lam_im', 'new_m_log_dt', 'new_m_s5_b_re', 'new_m_s5_b_im', 'new_m_s5_c_re', 'new_m_s5_c_im', 'new_m_s5_d', 'new_m_w_glu', 'new_m_b_glu', 'new_m_conv_w', 'new_m_conv_b', 'new_m_w_r', 'new_m_b_r', 'new_m_w_i', 'new_m_b_i', 'new_m_lru_lambda', 'new_m_w_a_out', 'new_m_w_b_out', 'new_m_w_o', 'new_m_g_ffn', 'new_m_w_ffn_gate', 'new_m_w_ffn_up', 'new_m_w_ffn_down', 'new_m_g_ple_gate', 'new_m_w_ple_gate', 'new_m_b_ple_gate', 'new_m_w_ple', 'new_m_g_ple', 'new_m_g_final', 'new_v_g_mix', 'new_v_w_in', 'new_v_b_in', 'new_v_lam_re', 'new_v_lam_im', 'new_v_log_dt', 'new_v_s5_b_re', 'new_v_s5_b_im', 'new_v_s5_c_re', 'new_v_s5_c_im', 'new_v_s5_d', 'new_v_w_glu', 'new_v_b_glu', 'new_v_conv_w', 'new_v_conv_b', 'new_v_w_r', 'new_v_b_r', 'new_v_w_i', 'new_v_b_i', 'new_v_lru_lambda', 'new_v_w_a_out', 'new_v_w_b_out', 'new_v_w_o', 'new_v_g_ffn', 'new_v_w_ffn_gate', 'new_v_w_ffn_up', 'new_v_w_ffn_down', 'new_v_g_ple_gate', 'new_v_w_ple_gate', 'new_v_b_ple_gate', 'new_v_w_ple', 'new_v_g_ple', 'new_v_g_final']
TWIN_LEAF_KINDS = {'loss': 'loss', 'grad_x': 'grad_x', 'grad_g_mix': 'grad_w', 'grad_w_in': 'grad_w', 'grad_b_in': 'grad_w', 'grad_lam_re': 'grad_w', 'grad_lam_im': 'grad_w', 'grad_log_dt': 'grad_w', 'grad_s5_b_re': 'grad_w', 'grad_s5_b_im': 'grad_w', 'grad_s5_c_re': 'grad_w', 'grad_s5_c_im': 'grad_w', 'grad_s5_d': 'grad_w', 'grad_w_glu': 'grad_w', 'grad_b_glu': 'grad_w', 'grad_conv_w': 'grad_w', 'grad_conv_b': 'grad_w', 'grad_w_r': 'grad_w', 'grad_b_r': 'grad_w', 'grad_w_i': 'grad_w', 'grad_b_i': 'grad_w', 'grad_lru_lambda': 'grad_w', 'grad_w_a_out': 'grad_w', 'grad_w_b_out': 'grad_w', 'grad_w_o': 'grad_w', 'grad_g_ffn': 'grad_w', 'grad_w_ffn_gate': 'grad_w', 'grad_w_ffn_up': 'grad_w', 'grad_w_ffn_down': 'grad_w', 'grad_g_ple_gate': 'grad_w', 'grad_w_ple_gate': 'grad_w', 'grad_b_ple_gate': 'grad_w', 'grad_w_ple': 'grad_w', 'grad_g_ple': 'grad_w', 'grad_g_final': 'grad_w', 'delta_g_mix': 'delta_w', 'delta_w_in': 'delta_w', 'delta_b_in': 'delta_w', 'delta_lam_re': 'delta_w', 'delta_lam_im': 'delta_w', 'delta_log_dt': 'delta_w', 'delta_s5_b_re': 'delta_w', 'delta_s5_b_im': 'delta_w', 'delta_s5_c_re': 'delta_w', 'delta_s5_c_im': 'delta_w', 'delta_s5_d': 'delta_w', 'delta_w_glu': 'delta_w', 'delta_b_glu': 'delta_w', 'delta_conv_w': 'delta_w', 'delta_conv_b': 'delta_w', 'delta_w_r': 'delta_w', 'delta_b_r': 'delta_w', 'delta_w_i': 'delta_w', 'delta_b_i': 'delta_w', 'delta_lru_lambda': 'delta_w', 'delta_w_a_out': 'delta_w', 'delta_w_b_out': 'delta_w', 'delta_w_o': 'delta_w', 'delta_g_ffn': 'delta_w', 'delta_w_ffn_gate': 'delta_w', 'delta_w_ffn_up': 'delta_w', 'delta_w_ffn_down': 'delta_w', 'delta_g_ple_gate': 'delta_w', 'delta_w_ple_gate': 'delta_w', 'delta_b_ple_gate': 'delta_w', 'delta_w_ple': 'delta_w', 'delta_g_ple': 'delta_w', 'delta_g_final': 'delta_w', 'new_m_g_mix': 'new_m', 'new_m_w_in': 'new_m', 'new_m_b_in': 'new_m', 'new_m_lam_re': 'new_m', 'new_m_lam_im': 'new_m', 'new_m_log_dt': 'new_m', 'new_m_s5_b_re': 'new_m', 'new_m_s5_b_im': 'new_m', 'new_m_s5_c_re': 'new_m', 'new_m_s5_c_im': 'new_m', 'new_m_s5_d': 'new_m', 'new_m_w_glu': 'new_m', 'new_m_b_glu': 'new_m', 'new_m_conv_w': 'new_m', 'new_m_conv_b': 'new_m', 'new_m_w_r': 'new_m', 'new_m_b_r': 'new_m', 'new_m_w_i': 'new_m', 'new_m_b_i': 'new_m', 'new_m_lru_lambda': 'new_m', 'new_m_w_a_out': 'new_m', 'new_m_w_b_out': 'new_m', 'new_m_w_o': 'new_m', 'new_m_g_ffn': 'new_m', 'new_m_w_ffn_gate': 'new_m', 'new_m_w_ffn_up': 'new_m', 'new_m_w_ffn_down': 'new_m', 'new_m_g_ple_gate': 'new_m', 'new_m_w_ple_gate': 'new_m', 'new_m_b_ple_gate': 'new_m', 'new_m_w_ple': 'new_m', 'new_m_g_ple': 'new_m', 'new_m_g_final': 'new_m', 'new_v_g_mix': 'new_v', 'new_v_w_in': 'new_v', 'new_v_b_in': 'new_v', 'new_v_lam_re': 'new_v', 'new_v_lam_im': 'new_v', 'new_v_log_dt': 'new_v', 'new_v_s5_b_re': 'new_v', 'new_v_s5_b_im': 'new_v', 'new_v_s5_c_re': 'new_v', 'new_v_s5_c_im': 'new_v', 'new_v_s5_d': 'new_v', 'new_v_w_glu': 'new_v', 'new_v_b_glu': 'new_v', 'new_v_conv_w': 'new_v', 'new_v_conv_b': 'new_v', 'new_v_w_r': 'new_v', 'new_v_b_r': 'new_v', 'new_v_w_i': 'new_v', 'new_v_b_i': 'new_v', 'new_v_lru_lambda': 'new_v', 'new_v_w_a_out': 'new_v', 'new_v_w_b_out': 'new_v', 'new_v_w_o': 'new_v', 'new_v_g_ffn': 'new_v', 'new_v_w_ffn_gate': 'new_v', 'new_v_w_ffn_up': 'new_v', 'new_v_w_ffn_down': 'new_v', 'new_v_g_ple_gate': 'new_v', 'new_v_w_ple_gate': 'new_v', 'new_v_b_ple_gate': 'new_v', 'new_v_w_ple': 'new_v', 'new_v_g_ple': 'new_v', 'new_v_g_final': 'new_v'}


def _forward(args):
    return _fwd_reference(*[args[k] for k in FWD_PARAMS])


def _output_shape():
    out = _jax.eval_shape(lambda: _forward(_fwd_setup_inputs(0)))
    return out.shape, out.dtype

N_MICROBATCH = 1
ADAM_LR = 0.001
ADAM_B1 = 0.9
ADAM_B2 = 0.999
ADAM_EPS = 1e-08
ADAM_WD = 0.01
ADAM_STEP = 10
PER_EXAMPLE_BATCH_AXIS = {'x': 0, 'p': 1, 'loss_target': 0}
SHARED_INPUTS = []
_WEIGHT_DTYPES = {'g_mix': _jnp.float32, 'w_in': _jnp.float32, 'b_in': _jnp.float32, 'lam_re': _jnp.float32, 'lam_im': _jnp.float32, 'log_dt': _jnp.float32, 's5_b_re': _jnp.float32, 's5_b_im': _jnp.float32, 's5_c_re': _jnp.float32, 's5_c_im': _jnp.float32, 's5_d': _jnp.float32, 'w_glu': _jnp.float32, 'b_glu': _jnp.float32, 'conv_w': _jnp.float32, 'conv_b': _jnp.float32, 'w_r': _jnp.float32, 'b_r': _jnp.float32, 'w_i': _jnp.float32, 'b_i': _jnp.float32, 'lru_lambda': _jnp.float32, 'w_a_out': _jnp.float32, 'w_b_out': _jnp.float32, 'w_o': _jnp.float32, 'g_ffn': _jnp.float32, 'w_ffn_gate': _jnp.float32, 'w_ffn_up': _jnp.float32, 'w_ffn_down': _jnp.float32, 'g_ple_gate': _jnp.float32, 'w_ple_gate': _jnp.float32, 'b_ple_gate': _jnp.float32, 'w_ple': _jnp.float32, 'g_ple': _jnp.float32, 'g_final': _jnp.float32}
MOMENT_SCALE = {'g_mix': 5.309134e-02, 'w_in': 2.724340e-02, 'b_in': 2.482677e-01, 'lam_re': 2.563027e-03, 'lam_im': 2.499794e-03, 'log_dt': 2.376857e+00, 's5_b_re': 1.549400e-03, 's5_b_im': 1.578940e-03, 's5_c_re': 2.326306e-03, 's5_c_im': 2.311685e-03, 's5_d': 3.731297e-02, 'w_glu': 9.447616e-03, 'b_glu': 1.460212e-02, 'conv_w': 4.497382e-02, 'conv_b': 4.483759e-01, 'w_r': 1.706180e-02, 'b_r': 1.144853e-02, 'w_i': 3.096564e-02, 'b_i': 1.463651e-02, 'lru_lambda': 2.043537e-02, 'w_a_out': 2.292843e-02, 'w_b_out': 4.959520e-02, 'w_o': 5.424451e-02, 'g_ffn': 8.363985e-02, 'w_ffn_gate': 3.531220e-02, 'w_ffn_up': 3.425550e-02, 'w_ffn_down': 5.686946e-02, 'g_ple_gate': 1.986730e-02, 'w_ple_gate': 1.986352e-02, 'b_ple_gate': 2.136963e-02, 'w_ple': 5.105287e-02, 'g_ple': 5.540657e-02, 'g_final': 1.600078e+01}


def _to_microbatches(a, axis):
    t = _jnp.moveaxis(a, axis, 0)
    t = t.reshape((N_MICROBATCH, t.shape[0] // N_MICROBATCH) + t.shape[1:])
    return _jnp.moveaxis(t, 1, axis + 1)


def setup_inputs(seed: int = 0) -> dict:
    inp = _fwd_setup_inputs(seed)
    key = _jax.random.fold_in(_jax.random.key(seed), 7919)
    shape, _ = _output_shape()
    out = dict(inp)
    out["loss_target"] = _jax.random.normal(_jax.random.fold_in(key, 0), shape, _jnp.float32)
    for i, name in enumerate(TWIN_WEIGHTS):
        w = inp[name].astype(_jnp.float32)
        if MOMENT_SCALE is None:
            s = _jnp.sqrt(_jnp.mean(_jnp.square(w)) + 1e-30)
        else:
            s = MOMENT_SCALE[name]
        km, kv = _jax.random.split(_jax.random.fold_in(key, i + 1))
        out[name] = w
        out["m_" + name] = s * _jax.random.normal(km, w.shape, _jnp.float32)
        out["v_" + name] = (s * s) * _jax.random.uniform(kv, w.shape, _jnp.float32, 0.5, 1.5)
    if N_MICROBATCH > 1:
        for name, axis in PER_EXAMPLE_BATCH_AXIS.items():
            out[name] = _to_microbatches(out[name], axis)
    return {'x': out['x'], 'p': out['p'], 'g_mix': out['g_mix'], 'w_in': out['w_in'], 'b_in': out['b_in'], 'lam_re': out['lam_re'], 'lam_im': out['lam_im'], 'log_dt': out['log_dt'], 's5_b_re': out['s5_b_re'], 's5_b_im': out['s5_b_im'], 's5_c_re': out['s5_c_re'], 's5_c_im': out['s5_c_im'], 's5_d': out['s5_d'], 'w_glu': out['w_glu'], 'b_glu': out['b_glu'], 'conv_w': out['conv_w'], 'conv_b': out['conv_b'], 'w_r': out['w_r'], 'b_r': out['b_r'], 'w_i': out['w_i'], 'b_i': out['b_i'], 'lru_lambda': out['lru_lambda'], 'w_a_out': out['w_a_out'], 'w_b_out': out['w_b_out'], 'w_o': out['w_o'], 'g_ffn': out['g_ffn'], 'w_ffn_gate': out['w_ffn_gate'], 'w_ffn_up': out['w_ffn_up'], 'w_ffn_down': out['w_ffn_down'], 'g_ple_gate': out['g_ple_gate'], 'w_ple_gate': out['w_ple_gate'], 'b_ple_gate': out['b_ple_gate'], 'w_ple': out['w_ple'], 'g_ple': out['g_ple'], 'g_final': out['g_final'], 'loss_target': out['loss_target'], 'm_g_mix': out['m_g_mix'], 'm_w_in': out['m_w_in'], 'm_b_in': out['m_b_in'], 'm_lam_re': out['m_lam_re'], 'm_lam_im': out['m_lam_im'], 'm_log_dt': out['m_log_dt'], 'm_s5_b_re': out['m_s5_b_re'], 'm_s5_b_im': out['m_s5_b_im'], 'm_s5_c_re': out['m_s5_c_re'], 'm_s5_c_im': out['m_s5_c_im'], 'm_s5_d': out['m_s5_d'], 'm_w_glu': out['m_w_glu'], 'm_b_glu': out['m_b_glu'], 'm_conv_w': out['m_conv_w'], 'm_conv_b': out['m_conv_b'], 'm_w_r': out['m_w_r'], 'm_b_r': out['m_b_r'], 'm_w_i': out['m_w_i'], 'm_b_i': out['m_b_i'], 'm_lru_lambda': out['m_lru_lambda'], 'm_w_a_out': out['m_w_a_out'], 'm_w_b_out': out['m_w_b_out'], 'm_w_o': out['m_w_o'], 'm_g_ffn': out['m_g_ffn'], 'm_w_ffn_gate': out['m_w_ffn_gate'], 'm_w_ffn_up': out['m_w_ffn_up'], 'm_w_ffn_down': out['m_w_ffn_down'], 'm_g_ple_gate': out['m_g_ple_gate'], 'm_w_ple_gate': out['m_w_ple_gate'], 'm_b_ple_gate': out['m_b_ple_gate'], 'm_w_ple': out['m_w_ple'], 'm_g_ple': out['m_g_ple'], 'm_g_final': out['m_g_final'], 'v_g_mix': out['v_g_mix'], 'v_w_in': out['v_w_in'], 'v_b_in': out['v_b_in'], 'v_lam_re': out['v_lam_re'], 'v_lam_im': out['v_lam_im'], 'v_log_dt': out['v_log_dt'], 'v_s5_b_re': out['v_s5_b_re'], 'v_s5_b_im': out['v_s5_b_im'], 'v_s5_c_re': out['v_s5_c_re'], 'v_s5_c_im': out['v_s5_c_im'], 'v_s5_d': out['v_s5_d'], 'v_w_glu': out['v_w_glu'], 'v_b_glu': out['v_b_glu'], 'v_conv_w': out['v_conv_w'], 'v_conv_b': out['v_conv_b'], 'v_w_r': out['v_w_r'], 'v_b_r': out['v_b_r'], 'v_w_i': out['v_w_i'], 'v_b_i': out['v_b_i'], 'v_lru_lambda': out['v_lru_lambda'], 'v_w_a_out': out['v_w_a_out'], 'v_w_b_out': out['v_w_b_out'], 'v_w_o': out['v_w_o'], 'v_g_ffn': out['v_g_ffn'], 'v_w_ffn_gate': out['v_w_ffn_gate'], 'v_w_ffn_up': out['v_w_ffn_up'], 'v_w_ffn_down': out['v_w_ffn_down'], 'v_g_ple_gate': out['v_g_ple_gate'], 'v_w_ple_gate': out['v_w_ple_gate'], 'v_b_ple_gate': out['v_b_ple_gate'], 'v_w_ple': out['v_w_ple'], 'v_g_ple': out['v_g_ple'], 'v_g_final': out['v_g_final']}


def _loss(weights, diff, rest, loss_target):
    with _jax.named_scope("forward"):
        args = {**rest, TWIN_DIFF_INPUT: diff, **{k: w.astype(_WEIGHT_DTYPES[k]) for k, w in weights.items()}}
        y = _forward(args)
    with _jax.named_scope("loss_head"):
        err = _jnp.square(y.astype(_jnp.float32) - loss_target)
        return 0.5 * _jnp.sum(_jnp.mean(err, axis=-1)) if err.ndim else 0.5 * err


def _adamw(w, g, m, v):
    m = ADAM_B1 * m + (1.0 - ADAM_B1) * g
    v = ADAM_B2 * v + (1.0 - ADAM_B2) * _jnp.square(g)
    m_hat = m / (1.0 - ADAM_B1 ** ADAM_STEP)
    v_hat = v / (1.0 - ADAM_B2 ** ADAM_STEP)
    delta = -ADAM_LR * (m_hat / (_jnp.sqrt(v_hat) + ADAM_EPS) + ADAM_WD * w)
    return delta, m, v


def reference(x, p, g_mix, w_in, b_in, lam_re, lam_im, log_dt, s5_b_re, s5_b_im, s5_c_re, s5_c_im, s5_d, w_glu, b_glu, conv_w, conv_b, w_r, b_r, w_i, b_i, lru_lambda, w_a_out, w_b_out, w_o, g_ffn, w_ffn_gate, w_ffn_up, w_ffn_down, g_ple_gate, w_ple_gate, b_ple_gate, w_ple, g_ple, g_final, loss_target, m_g_mix, m_w_in, m_b_in, m_lam_re, m_lam_im, m_log_dt, m_s5_b_re, m_s5_b_im, m_s5_c_re, m_s5_c_im, m_s5_d, m_w_glu, m_b_glu, m_conv_w, m_conv_b, m_w_r, m_b_r, m_w_i, m_b_i, m_lru_lambda, m_w_a_out, m_w_b_out, m_w_o, m_g_ffn, m_w_ffn_gate, m_w_ffn_up, m_w_ffn_down, m_g_ple_gate, m_w_ple_gate, m_b_ple_gate, m_w_ple, m_g_ple, m_g_final, v_g_mix, v_w_in, v_b_in, v_lam_re, v_lam_im, v_log_dt, v_s5_b_re, v_s5_b_im, v_s5_c_re, v_s5_c_im, v_s5_d, v_w_glu, v_b_glu, v_conv_w, v_conv_b, v_w_r, v_b_r, v_w_i, v_b_i, v_lru_lambda, v_w_a_out, v_w_b_out, v_w_o, v_g_ffn, v_w_ffn_gate, v_w_ffn_up, v_w_ffn_down, v_g_ple_gate, v_w_ple_gate, v_b_ple_gate, v_w_ple, v_g_ple, v_g_final):
    given = dict(x=x, p=p, g_mix=g_mix, w_in=w_in, b_in=b_in, lam_re=lam_re, lam_im=lam_im, log_dt=log_dt, s5_b_re=s5_b_re, s5_b_im=s5_b_im, s5_c_re=s5_c_re, s5_c_im=s5_c_im, s5_d=s5_d, w_glu=w_glu, b_glu=b_glu, conv_w=conv_w, conv_b=conv_b, w_r=w_r, b_r=b_r, w_i=w_i, b_i=b_i, lru_lambda=lru_lambda, w_a_out=w_a_out, w_b_out=w_b_out, w_o=w_o, g_ffn=g_ffn, w_ffn_gate=w_ffn_gate, w_ffn_up=w_ffn_up, w_ffn_down=w_ffn_down, g_ple_gate=g_ple_gate, w_ple_gate=w_ple_gate, b_ple_gate=b_ple_gate, w_ple=w_ple, g_ple=g_ple, g_final=g_final, loss_target=loss_target, m_g_mix=m_g_mix, m_w_in=m_w_in, m_b_in=m_b_in, m_lam_re=m_lam_re, m_lam_im=m_lam_im, m_log_dt=m_log_dt, m_s5_b_re=m_s5_b_re, m_s5_b_im=m_s5_b_im, m_s5_c_re=m_s5_c_re, m_s5_c_im=m_s5_c_im, m_s5_d=m_s5_d, m_w_glu=m_w_glu, m_b_glu=m_b_glu, m_conv_w=m_conv_w, m_conv_b=m_conv_b, m_w_r=m_w_r, m_b_r=m_b_r, m_w_i=m_w_i, m_b_i=m_b_i, m_lru_lambda=m_lru_lambda, m_w_a_out=m_w_a_out, m_w_b_out=m_w_b_out, m_w_o=m_w_o, m_g_ffn=m_g_ffn, m_w_ffn_gate=m_w_ffn_gate, m_w_ffn_up=m_w_ffn_up, m_w_ffn_down=m_w_ffn_down, m_g_ple_gate=m_g_ple_gate, m_w_ple_gate=m_w_ple_gate, m_b_ple_gate=m_b_ple_gate, m_w_ple=m_w_ple, m_g_ple=m_g_ple, m_g_final=m_g_final, v_g_mix=v_g_mix, v_w_in=v_w_in, v_b_in=v_b_in, v_lam_re=v_lam_re, v_lam_im=v_lam_im, v_log_dt=v_log_dt, v_s5_b_re=v_s5_b_re, v_s5_b_im=v_s5_b_im, v_s5_c_re=v_s5_c_re, v_s5_c_im=v_s5_c_im, v_s5_d=v_s5_d, v_w_glu=v_w_glu, v_b_glu=v_b_glu, v_conv_w=v_conv_w, v_conv_b=v_conv_b, v_w_r=v_w_r, v_b_r=v_b_r, v_w_i=v_w_i, v_b_i=v_b_i, v_lru_lambda=v_lru_lambda, v_w_a_out=v_w_a_out, v_w_b_out=v_w_b_out, v_w_o=v_w_o, v_g_ffn=v_g_ffn, v_w_ffn_gate=v_w_ffn_gate, v_w_ffn_up=v_w_ffn_up, v_w_ffn_down=v_w_ffn_down, v_g_ple_gate=v_g_ple_gate, v_w_ple_gate=v_w_ple_gate, v_b_ple_gate=v_b_ple_gate, v_w_ple=v_w_ple, v_g_ple=v_g_ple, v_g_final=v_g_final)
    weights = {n: given[n] for n in TWIN_WEIGHTS}
    shared = {n: given[n] for n in SHARED_INPUTS}
    per_example = {n: given[n] for n in ['x', 'p']}
    grad_fn = _jax.value_and_grad(_loss, argnums=(0, 1))

    def one_microbatch(ex, loss_target):
        ex = dict(ex)
        diff = ex.pop(TWIN_DIFF_INPUT)
        return grad_fn(weights, diff, {**shared, **ex}, loss_target)

    if N_MICROBATCH == 1:
        loss, (grad_w, grad_x) = one_microbatch(per_example, given["loss_target"])
    else:
        def body(carry, xs):
            loss_sum, grad_sum = carry
            l_k, (gw_k, gx_k) = one_microbatch(xs[0], xs[1])
            with _jax.named_scope("update"):
                return (loss_sum + l_k, _jax.tree.map(_jnp.add, grad_sum, gw_k)), gx_k

        init = (_jnp.zeros((), _jnp.float32), _jax.tree.map(_jnp.zeros_like, weights))
        (loss, grad_w), grad_x = _jax.lax.scan(body, init, (per_example, given["loss_target"]))
    with _jax.named_scope("update"):
        delta_w, new_m, new_v = {}, {}, {}
        for n in TWIN_WEIGHTS:
            delta_w[n], new_m[n], new_v[n] = _adamw(weights[n], grad_w[n], given["m_" + n], given["v_" + n])
    return (loss, grad_x, *[grad_w[n] for n in TWIN_WEIGHTS], *[delta_w[n] for n in TWIN_WEIGHTS],
            *[new_m[n] for n in TWIN_WEIGHTS], *[new_v[n] for n in TWIN_WEIGHTS])
```

```python
import functools
import math

import jax
import jax.numpy as jnp
from jax import lax
from jax.experimental import pallas as pl
from jax.experimental.pallas import tpu as pltpu

F32 = jnp.float32
BF16 = jnp.bfloat16

EPS = 1e-6
LRU_C = 8.0
CONV_WIDTH = 4
ADAM_LR = 0.001
ADAM_B1 = 0.9
ADAM_B2 = 0.999
ADAM_EPS = 1e-08
ADAM_WD = 0.01
ADAM_STEP = 10

N_DEV = 8
MESH = pl.DeviceIdType.MESH
SUBLANES = 8
LANES = 128
VMEM_LIMIT = 56 * 1024 * 1024
TOKEN_TILE = 256
TIME_CHUNK = 256
PACK_COLS = 1024


def _dot(a, b):
    return jnp.dot(a.astype(BF16), b.astype(BF16), preferred_element_type=F32)


def _dot_nt(a, b):
    return lax.dot_general(a.astype(BF16), b.astype(BF16), (((1,), (1,)), ((), ())), preferred_element_type=F32)


def _dot_tn(a, b):
    return lax.dot_general(a.astype(BF16), b.astype(BF16), (((0,), (0,)), ((), ())), preferred_element_type=F32)


def _sigmoid(x):
    return jax.nn.sigmoid(x)


def _rms_stats(x):
    r = lax.rsqrt(jnp.mean(x * x, axis=-1, keepdims=True) + EPS)
    return x * r, r


def _rms_bwd(dy, xhat, r, g):
    dxn = dy * g
    dx = r * (dxn - xhat * jnp.mean(dxn * xhat, axis=-1, keepdims=True))
    return dx, dy * xhat


def _rowsum(v):
    return jnp.sum(v, axis=0, keepdims=True)


def _expm1(x):
    u = jnp.exp(x)
    um1 = u - 1.0
    safe = jnp.where(um1 == 0.0, 1.0, jnp.log(u))
    return jnp.where(um1 == 0.0, x, um1 * x / safe)


def _softplus(x):
    e = jnp.exp(-jnp.abs(x))
    u = 1.0 + e
    um1 = u - 1.0
    safe = jnp.where(um1 == 0.0, 1.0, um1)
    log1p_e = jnp.where(um1 == 0.0, e, jnp.log(u) * e / safe)
    return jnp.maximum(x, 0.0) + log1p_e


_GELU_K = math.sqrt(2.0 / math.pi)
_GELU_C = 0.044715


def _gelu(x):
    return 0.5 * x * (1.0 + jnp.tanh(_GELU_K * (x + _GELU_C * x * x * x)))


def _gelu_grad(x):
    th = jnp.tanh(_GELU_K * (x + _GELU_C * x * x * x))
    return 0.5 * (1.0 + th) + 0.5 * x * (1.0 - th * th) * _GELU_K * (1.0 + 3.0 * _GELU_C * x * x)


def _params(*sem):
    return pltpu.CompilerParams(dimension_semantics=sem, vmem_limit_bytes=VMEM_LIMIT)


def _rows(tm, n):
    return pl.BlockSpec((tm, n), lambda i: (i, 0))


def _rows_rev(tm, n, steps):
    return pl.BlockSpec((tm, n), lambda i: (steps - 1 - i, 0))


def _whole(shape):
    nd = len(shape)
    return pl.BlockSpec(shape, lambda i: (0,) * nd, pipeline_mode=pl.Buffered(1))


def _acc(shape):
    nd = len(shape)
    return pl.BlockSpec(shape, lambda i: (0,) * nd)


def _zero_on_first(*refs):
    @pl.when(pl.program_id(0) == 0)
    def _():
        for r in refs:
            r[...] = jnp.zeros_like(r)


def _inproj_fwd(x, g_mix, w_in, b_in, widths):
    t, d = x.shape
    n = w_in.shape[1]
    tm = min(TOKEN_TILE, t)
    offs = [sum(widths[:i]) for i in range(len(widths) + 1)]

    def body(x_ref, g_ref, w_ref, b_ref, *outs):
        xhat, _ = _rms_stats(x_ref[...])
        h = (xhat * g_ref[...]).astype(BF16)
        for k, o_ref in enumerate(outs):
            lo, hi = offs[k], offs[k + 1]
            o_ref[...] = jnp.dot(h, w_ref[:, lo:hi], preferred_element_type=F32) + b_ref[:, lo:hi]

    return pl.pallas_call(
        body, name="inproj_fwd", grid=(t // tm,),
        out_shape=[jax.ShapeDtypeStruct((t, w), F32) for w in widths],
        in_specs=[_rows(tm, d), _whole((1, d)), _whole((d, n)), _whole((1, n))],
        out_specs=[_rows(tm, w) for w in widths],
        compiler_params=_params("parallel"),
    )(x, g_mix, w_in, b_in)


def _s5_fwd(u, bbr_blk, bbi_blk, ar, ai, cre_blk, cimn_blk, d_skip, w_glu, b_glu):
    t, sa = u.shape
    gn = bbr_blk.shape[1]
    tc = min(TIME_CHUNK, t)

    def body(u_ref, bbr_ref, bbi_ref, ar_ref, ai_ref, cre_ref, cim_ref, d_ref, wg_ref, bg_ref,
             sr_ref, si_ref, y_ref, ya_ref, cr_s, ci_s):
        _zero_on_first(cr_s, ci_s)
        uv = u_ref[...]
        ub = uv.astype(BF16)
        sr_ref[...] = jnp.dot(ub, bbr_ref[...], preferred_element_type=F32)
        si_ref[...] = jnp.dot(ub, bbi_ref[...], preferred_element_type=F32)
        a_r = ar_ref[...]
        a_i = ai_ref[...]

        def step(row, carry):
            c_r, c_i = carry
            at = pl.ds(row, 1)
            n_r = a_r * c_r - a_i * c_i + sr_ref[at, :]
            n_i = a_r * c_i + a_i * c_r + si_ref[at, :]
            sr_ref[at, :] = n_r
            si_ref[at, :] = n_i
            return n_r, n_i

        c_r, c_i = lax.fori_loop(0, tc, step, (cr_s[0:1, :], ci_s[0:1, :]), unroll=8)
        cr_s[0:1, :] = c_r
        ci_s[0:1, :] = c_i
        y = _dot(sr_ref[...], cre_ref[...]) + _dot(si_ref[...], cim_ref[...]) + d_ref[...] * uv
        y_ref[...] = y
        zz = _gelu(y)
        q = _dot(zz, wg_ref[...]) + bg_ref[...]
        ya_ref[...] = zz * _sigmoid(q)

    return pl.pallas_call(
        body, name="s5_fwd", grid=(t // tc,),
        out_shape=[jax.ShapeDtypeStruct((t, gn), F32), jax.ShapeDtypeStruct((t, gn), F32),
                   jax.ShapeDtypeStruct((t, sa), F32), jax.ShapeDtypeStruct((t, sa), F32)],
        in_specs=[_rows(tc, sa), _whole((sa, gn)), _whole((sa, gn)), _whole((1, gn)), _whole((1, gn)),
                  _whole((gn, sa)), _whole((gn, sa)), _whole((1, sa)), _whole((sa, sa)), _whole((1, sa))],
        out_specs=[_rows(tc, gn), _rows(tc, gn), _rows(tc, sa), _rows(tc, sa)],
        scratch_shapes=[pltpu.VMEM((SUBLANES, gn), F32), pltpu.VMEM((SUBLANES, gn), F32)],
        compiler_params=_params("arbitrary"),
    )(u, bbr_blk, bbi_blk, ar, ai, cre_blk, cimn_blk, d_skip, w_glu, b_glu)


def _lru_gates(xc, wr_ref, br_ref, wi_ref, bi_ref, lam_ref):
    r = _sigmoid(_dot(xc, wr_ref[...]) + br_ref[...])
    ig = _sigmoid(_dot(xc, wi_ref[...]) + bi_ref[...])
    sp = _softplus(-lam_ref[...])
    log_a = (-LRU_C * r) * sp
    return r, ig, sp, log_a


def _lru_fwd(u, conv_w, conv_b, wr_blk, b_r, wi_blk, b_i, lru_lambda):
    t, w = u.shape
    tc = min(TIME_CHUNK, t)
    halo = SUBLANES

    def body(u_ref, cw_ref, cb_ref, wr_ref, br_ref, wi_ref, bi_ref, lam_ref,
             xc_ref, h_ref, hp_ref, ext_s, a_s, carry_s):
        @pl.when(pl.program_id(0) == 0)
        def _():
            ext_s[0:halo, :] = jnp.zeros((halo, w), F32)
            carry_s[...] = jnp.zeros_like(carry_s)

        ext_s[halo:halo + tc, :] = u_ref[...]
        xc = cb_ref[...]
        for k in range(CONV_WIDTH):
            off = halo - (CONV_WIDTH - 1) + k
            xc = xc + cw_ref[k:k + 1, :] * ext_s[off:off + tc, :]
        ext_s[0:halo, :] = ext_s[tc:tc + halo, :]
        xc_ref[...] = xc
        r, ig, sp, log_a = _lru_gates(xc, wr_ref, br_ref, wi_ref, bi_ref, lam_ref)
        a_s[...] = jnp.exp(log_a)
        h_ref[...] = jnp.sqrt(-_expm1(2.0 * log_a)) * ig * xc

        def step(row, carry):
            at = pl.ds(row, 1)
            hp_ref[at, :] = carry
            nxt = a_s[at, :] * carry + h_ref[at, :]
            h_ref[at, :] = nxt
            return nxt

        carry_s[0:1, :] = lax.fori_loop(0, tc, step, carry_s[0:1, :], unroll=8)

    return pl.pallas_call(
        body, name="lru_fwd", grid=(t // tc,),
        out_shape=[jax.ShapeDtypeStruct((t, w), F32)] * 3,
        in_specs=[_rows(tc, w), _whole((CONV_WIDTH, w)), _whole((1, w)), _whole((w, w)), _whole((1, w)),
                  _whole((w, w)), _whole((1, w)), _whole((1, w))],
        out_specs=[_rows(tc, w)] * 3,
        scratch_shapes=[pltpu.VMEM((halo + tc, w), F32), pltpu.VMEM((tc, w), F32), pltpu.VMEM((SUBLANES, w), F32)],
        compiler_params=_params("arbitrary"),
    )(u, conv_w, conv_b, wr_blk, b_r, wi_blk, b_i, lru_lambda)


def _merge_fwd(y_a, h, za, zb, x, w_a_out, w_b_out, w_o):
    t, d = x.shape
    sa, lw = y_a.shape[1], h.shape[1]
    tm = min(TOKEN_TILE, t)

    def body(ya_ref, h_ref, za_ref, zb_ref, x_ref, wa_ref, wb_ref, wo_ref, x1_ref, mg_ref, ma_ref, mb_ref):
        ma = _dot(ya_ref[...], wa_ref[...])
        mb = _dot(h_ref[...], wb_ref[...])
        merged = _sigmoid(za_ref[...]) * ma + _sigmoid(zb_ref[...]) * mb
        ma_ref[...] = ma
        mb_ref[...] = mb
        mg_ref[...] = merged.astype(mg_ref.dtype)
        x1_ref[...] = x_ref[...] + _dot(merged, wo_ref[...])

    return pl.pallas_call(
        body, name="merge_fwd", grid=(t // tm,),
        out_shape=[jax.ShapeDtypeStruct((t, d), F32), jax.ShapeDtypeStruct((t, d), BF16),
                   jax.ShapeDtypeStruct((t, d), F32), jax.ShapeDtypeStruct((t, d), F32)],
        in_specs=[_rows(tm, sa), _rows(tm, lw), _rows(tm, d), _rows(tm, d), _rows(tm, d),
                  _whole((sa, d)), _whole((lw, d)), _whole((d, d))],
        out_specs=[_rows(tm, d)] * 4,
        compiler_params=_params("parallel"),
    )(y_a, h, za, zb, x, w_a_out, w_b_out, w_o)


def _ffn_up_fwd(x1, g_ffn, w_gate, w_up):
    t, d = x1.shape
    f = w_gate.shape[1]
    tm = min(TOKEN_TILE, t)

    def body(x_ref, g_ref, wg_ref, wu_ref, fg_ref, fu_ref):
        xhat, _ = _rms_stats(x_ref[...])
        h2 = (xhat * g_ref[...]).astype(BF16)
        fg_ref[...] = jnp.dot(h2, wg_ref[...], preferred_element_type=F32)
        fu_ref[...] = jnp.dot(h2, wu_ref[...], preferred_element_type=F32)

    return pl.pallas_call(
        body, name="ffn_up_fwd", grid=(t // tm,),
        out_shape=[jax.ShapeDtypeStruct((t, f), F32)] * 2,
        in_specs=[_rows(tm, d), _whole((1, d)), _whole((d, f)), _whole((d, f))],
        out_specs=[_rows(tm, f)] * 2,
        compiler_params=_params("parallel"),
    )(x1, g_ffn, w_gate, w_up)


def _ffn_down_fwd(fg, fu, x1, w_down):
    t, d = x1.shape
    f = fg.shape[1]
    tm = min(TOKEN_TILE, t)

    def body(fg_ref, fu_ref, x_ref, wd_ref, x2_ref):
        fgv = fg_ref[...]
        act = fgv * _sigmoid(fgv) * fu_ref[...]
        x2_ref[...] = x_ref[...] + _dot(act, wd_ref[...])

    return pl.pallas_call(
        body, name="ffn_down_fwd", grid=(t // tm,),
        out_shape=jax.ShapeDtypeStruct((t, d), F32),
        in_specs=[_rows(tm, f), _rows(tm, f), _rows(tm, d), _whole((f, d))],
        out_specs=_rows(tm, d),
        compiler_params=_params("parallel"),
    )(fg, fu, x1, w_down)


def _tail_fwd_bwd(x2, p, target, g_pg, w_pg, b_pg, w_ple, g_ple, g_final):
    t, d = x2.shape
    pd = p.shape[1]
    tm = min(TOKEN_TILE, t)

    def body(x2_ref, p_ref, tg_ref, gpg_ref, wpg_ref, bpg_ref, wple_ref, gple_ref, gfin_ref,
             dx2_ref, loss_ref, dwpg_ref, dwple_ref, vec_ref):
        _zero_on_first(loss_ref, dwpg_ref, dwple_ref, vec_ref)
        x2v = x2_ref[...]
        xh2, r2 = _rms_stats(x2v)
        h3 = xh2 * gpg_ref[...]
        gp = _sigmoid(_dot(h3, wpg_ref[...]) + bpg_ref[...])
        pe = _dot(p_ref[...], wple_ref[...])
        peh, r3 = _rms_stats(pe)
        e = peh * gple_ref[...]
        x3 = x2v + gp * e
        xh3, r4 = _rms_stats(x3)
        diff = xh3 * gfin_ref[...] - tg_ref[...]
        loss_ref[...] += 0.5 * jnp.sum(jnp.mean(diff * diff, axis=-1, keepdims=True))
        dy = diff * (1.0 / d)
        dx3, dgfin = _rms_bwd(dy, xh3, r4, gfin_ref[...])
        d_gp = dx3 * e
        d_e = dx3 * gp
        dpe, dgple = _rms_bwd(d_e, peh, r3, gple_ref[...])
        dwple_ref[...] += _dot_tn(p_ref[...], dpe)
        dpre = d_gp * gp * (1.0 - gp)
        dwpg_ref[...] += _dot_tn(h3, dpre)
        dh3 = _dot_nt(dpre, wpg_ref[...])
        dx2n, dgpg = _rms_bwd(dh3, xh2, r2, gpg_ref[...])
        dx2_ref[...] = dx3 + dx2n
        vec_ref[0:1, :] += _rowsum(dpre)
        vec_ref[1:2, :] += _rowsum(dgpg)
        vec_ref[2:3, :] += _rowsum(dgple)
        vec_ref[3:4, :] += _rowsum(dgfin)

    return pl.pallas_call(
        body, name="tail_fwd_bwd", grid=(t // tm,),
        out_shape=[jax.ShapeDtypeStruct((t, d), F32), jax.ShapeDtypeStruct((SUBLANES, LANES), F32),
                   jax.ShapeDtypeStruct((d, d), F32), jax.ShapeDtypeStruct((pd, d), F32),
                   jax.ShapeDtypeStruct((SUBLANES, d), F32)],
        in_specs=[_rows(tm, d), _rows(tm, pd), _rows(tm, d), _whole((1, d)), _whole((d, d)), _whole((1, d)),
                  _whole((pd, d)), _whole((1, d)), _whole((1, d))],
        out_specs=[_rows(tm, d), _acc((SUBLANES, LANES)), _acc((d, d)), _acc((pd, d)), _acc((SUBLANES, d))],
        compiler_params=_params("arbitrary"),
    )(x2, p, target, g_pg, w_pg, b_pg, w_ple, g_ple, g_final)


def _ffn_bwd_a(dx2, fg, fu, w_down):
    t, d = dx2.shape
    f = fg.shape[1]
    tm = min(TOKEN_TILE, t)

    def body(dx_ref, fg_ref, fu_ref, wd_ref, dfg_ref, dfu_ref, act_ref):
        dact = _dot_nt(dx_ref[...], wd_ref[...])
        fgv = fg_ref[...]
        fuv = fu_ref[...]
        sg = _sigmoid(fgv)
        silu = fgv * sg
        dfu_ref[...] = (dact * silu).astype(dfu_ref.dtype)
        dfg_ref[...] = (dact * fuv * (sg * (1.0 + fgv * (1.0 - sg)))).astype(dfg_ref.dtype)
        act_ref[...] = (silu * fuv).astype(act_ref.dtype)

    return pl.pallas_call(
        body, name="ffn_bwd_a", grid=(t // tm,),
        out_shape=[jax.ShapeDtypeStruct((t, f), BF16)] * 3,
        in_specs=[_rows(tm, d), _rows(tm, f), _rows(tm, f), _whole((f, d))],
        out_specs=[_rows(tm, f)] * 3,
        compiler_params=_params("parallel"),
    )(dx2, fg, fu, w_down)


def _ffn_bwd_b(dfg, dfu, x1, dx2, g_ffn, w_gate, w_up):
    t, d = x1.shape
    f = dfg.shape[1]
    tm = min(TOKEN_TILE, t)

    def body(dfg_ref, dfu_ref, x_ref, dx2_ref, g_ref, wg_ref, wu_ref, dx1_ref, h2_ref, vec_ref):
        _zero_on_first(vec_ref)
        dh2 = _dot_nt(dfg_ref[...], wg_ref[...]) + _dot_nt(dfu_ref[...], wu_ref[...])
        xhat, r = _rms_stats(x_ref[...])
        h2_ref[...] = (xhat * g_ref[...]).astype(h2_ref.dtype)
        dxn, dg = _rms_bwd(dh2, xhat, r, g_ref[...])
        dx1_ref[...] = dx2_ref[...] + dxn
        vec_ref[0:1, :] += _rowsum(dg)

    return pl.pallas_call(
        body, name="ffn_bwd_b", grid=(t // tm,),
        out_shape=[jax.ShapeDtypeStruct((t, d), F32), jax.ShapeDtypeStruct((t, d), BF16),
                   jax.ShapeDtypeStruct((SUBLANES, d), F32)],
        in_specs=[_rows(tm, f), _rows(tm, f), _rows(tm, d), _rows(tm, d), _whole((1, d)), _whole((d, f)), _whole((d, f))],
        out_specs=[_rows(tm, d), _rows(tm, d), _acc((SUBLANES, d))],
        compiler_params=_params("arbitrary"),
    )(dfg, dfu, x1, dx2, g_ffn, w_gate, w_up)


def _matmul_tn(a, b, tn, name):
    t, k = a.shape
    n = b.shape[1]

    def body(a_ref, b_ref, o_ref):
        o_ref[...] = _dot_tn(a_ref[...], b_ref[...])

    return pl.pallas_call(
        body, name=name, grid=(n // tn,),
        out_shape=jax.ShapeDtypeStruct((k, n), F32),
        in_specs=[_whole((t, k)), pl.BlockSpec((t, tn), lambda j: (0, j))],
        out_specs=pl.BlockSpec((k, tn), lambda j: (0, j)),
        compiler_params=_params("parallel"),
    )(a, b)


def _merge_bwd(dx1, merged, ma, mb, za, zb, y_a, h, w_o, w_a_out, w_b_out):
    t, d = dx1.shape
    sa, lw = y_a.shape[1], h.shape[1]
    tm = min(TOKEN_TILE, t)

    def body(dx1_ref, mg_ref, ma_ref, mb_ref, za_ref, zb_ref, ya_ref, h_ref, wo_ref, wa_ref, wb_ref,
             dza_ref, dzb_ref, dya_ref, dyb_ref, dwo_ref, dwa_ref, dwb_ref):
        _zero_on_first(dwo_ref, dwa_ref, dwb_ref)
        dx1v = dx1_ref[...].astype(BF16)
        dmg = _dot_nt(dx1v, wo_ref[...])
        ga = _sigmoid(za_ref[...])
        gb = _sigmoid(zb_ref[...])
        dza_ref[...] = dmg * ma_ref[...] * ga * (1.0 - ga)
        dzb_ref[...] = dmg * mb_ref[...] * gb * (1.0 - gb)
        dma = (dmg * ga).astype(BF16)
        dmb = (dmg * gb).astype(BF16)
        dya_ref[...] = _dot_nt(dma, wa_ref[...])
        dyb_ref[...] = _dot_nt(dmb, wb_ref[...])
        dwo_ref[...] += _dot_tn(mg_ref[...], dx1v)
        dwa_ref[...] += _dot_tn(ya_ref[...], dma)
        dwb_ref[...] += _dot_tn(h_ref[...], dmb)

    return pl.pallas_call(
        body, name="merge_bwd", grid=(t // tm,),
        out_shape=[jax.ShapeDtypeStruct((t, d), F32), jax.ShapeDtypeStruct((t, d), F32),
                   jax.ShapeDtypeStruct((t, sa), F32), jax.ShapeDtypeStruct((t, lw), F32),
                   jax.ShapeDtypeStruct((d, d), F32), jax.ShapeDtypeStruct((sa, d), F32),
                   jax.ShapeDtypeStruct((lw, d), F32)],
        in_specs=[_rows(tm, d), _rows(tm, d), _rows(tm, d), _rows(tm, d), _rows(tm, d), _rows(tm, d),
                  _rows(tm, sa), _rows(tm, lw), _whole((d, d)), _whole((sa, d)), _whole((lw, d))],
        out_specs=[_rows(tm, d), _rows(tm, d), _rows(tm, sa), _rows(tm, lw), _acc((d, d)), _acc((sa, d)), _acc((lw, d))],
        compiler_params=_params("arbitrary"),
    )(dx1, merged, ma, mb, za, zb, y_a, h, w_o, w_a_out, w_b_out)


def _lru_bwd(dh, xc, hprev, u, conv_w, wr_blk, b_r, wi_blk, b_i, lru_lambda):
    t, w = dh.shape
    tc = min(TIME_CHUNK, t)
    steps = t // tc
    halo = SUBLANES
    sub_per_chunk = tc // halo

    def body(dh_ref, xc_ref, hp_ref, u_ref, uh_ref, cw_ref, wr_ref, br_ref, wi_ref, bi_ref, lam_ref,
             du_ref, dwr_ref, dwi_ref, vec_ref, lam_s, a_s, dxc_s, uext_s, carry_s):
        chunk = steps - 1 - pl.program_id(0)

        @pl.when(pl.program_id(0) == 0)
        def _():
            carry_s[...] = jnp.zeros_like(carry_s)
            dxc_s[tc:tc + halo, :] = jnp.zeros((halo, w), F32)
            dwr_ref[...] = jnp.zeros_like(dwr_ref)
            dwi_ref[...] = jnp.zeros_like(dwi_ref)
            vec_ref[...] = jnp.zeros_like(vec_ref)

        xc = xc_ref[...]
        r, ig, sp, log_a = _lru_gates(xc, wr_ref, br_ref, wi_ref, bi_ref, lam_ref)
        a = jnp.exp(log_a)
        a_s[...] = a

        def step(i, q):
            at = pl.ds(tc - 1 - i, 1)
            lam_row = dh_ref[at, :] + q
            lam_s[at, :] = lam_row
            return a_s[at, :] * lam_row

        carry_s[0:1, :] = lax.fori_loop(0, tc, step, carry_s[0:1, :], unroll=8)
        lam = lam_s[...]
        mult = jnp.sqrt(-_expm1(2.0 * log_a))
        d_log_a = lam * hp_ref[...] * a - (lam * ig * xc) * (a * a) / mult
        d_ig = lam * mult * xc
        dpre_r = (d_log_a * (-LRU_C * sp)) * r * (1.0 - r)
        dpre_i = d_ig * ig * (1.0 - ig)
        dxc = lam * mult * ig + _dot_nt(dpre_r, wr_ref[...]) + _dot_nt(dpre_i, wi_ref[...])
        dwr_ref[...] += _dot_tn(xc, dpre_r)
        dwi_ref[...] += _dot_tn(xc, dpre_i)
        vec_ref[0:1, :] += _rowsum(dxc)
        vec_ref[1:2, :] += _rowsum(dpre_r)
        vec_ref[2:3, :] += _rowsum(dpre_i)
        vec_ref[3:4, :] += _rowsum(d_log_a * (-LRU_C * r)) * (-_sigmoid(-lam_ref[...]))
        dxc_s[0:tc, :] = dxc
        du = cw_ref[CONV_WIDTH - 1:CONV_WIDTH, :] * dxc
        for k in range(CONV_WIDTH - 1):
            off = CONV_WIDTH - 1 - k
            du = du + cw_ref[k:k + 1, :] * dxc_s[off:off + tc, :]
        du_ref[...] = du
        dxc_s[tc:tc + halo, :] = dxc_s[0:halo, :]
        uext_s[0:halo, :] = jnp.where(chunk > 0, uh_ref[...], 0.0)
        uext_s[halo:halo + tc, :] = u_ref[...]
        for k in range(CONV_WIDTH):
            off = halo - (CONV_WIDTH - 1) + k
            vec_ref[4 + k:5 + k, :] += _rowsum(dxc * uext_s[off:off + tc, :])

    halo_spec = pl.BlockSpec((halo, w), lambda i: (jnp.maximum((steps - 1 - i) * sub_per_chunk - 1, 0), 0))
    return pl.pallas_call(
        body, name="lru_bwd", grid=(steps,),
        out_shape=[jax.ShapeDtypeStruct((t, w), F32), jax.ShapeDtypeStruct((w, w), F32),
                   jax.ShapeDtypeStruct((w, w), F32), jax.ShapeDtypeStruct((SUBLANES, w), F32)],
        in_specs=[_rows_rev(tc, w, steps)] * 4 + [halo_spec, _whole((CONV_WIDTH, w)), _whole((w, w)), _whole((1, w)),
                                                  _whole((w, w)), _whole((1, w)), _whole((1, w))],
        out_specs=[_rows_rev(tc, w, steps), _acc((w, w)), _acc((w, w)), _acc((SUBLANES, w))],
        scratch_shapes=[pltpu.VMEM((tc, w), F32), pltpu.VMEM((tc, w), F32), pltpu.VMEM((tc + halo, w), F32),
                        pltpu.VMEM((halo + tc, w), F32), pltpu.VMEM((SUBLANES, w), F32)],
        compiler_params=_params("arbitrary"),
    )(dh, xc, hprev, u, u, conv_w, wr_blk, b_r, wi_blk, b_i, lru_lambda)


def _s5_bwd(dya, y, sr, si, u, w_glu, b_glu, cre_blk, cimn_blk, bbr_blk, bbi_blk, ar, ai, d_skip):
    t, sa = dya.shape
    gn = sr.shape[1]
    tc = min(TIME_CHUNK, t)
    steps = t // tc
    halo = SUBLANES

    def body(dya_ref, y_ref, sr_ref, si_ref, u_ref, wg_ref, bg_ref, cre_ref, cim_ref, bbr_ref, bbi_ref,
             ar_ref, ai_ref, d_ref, du_ref, lr_ref, li_ref, dy_ref, dwg_ref, vsa_ref, vgn_ref, gr_s, gi_s, cr_s, ci_s):
        @pl.when(pl.program_id(0) == 0)
        def _():
            cr_s[...] = jnp.zeros_like(cr_s)
            ci_s[...] = jnp.zeros_like(ci_s)
            gr_s[tc:tc + halo, :] = jnp.zeros((halo, gn), F32)
            gi_s[tc:tc + halo, :] = jnp.zeros((halo, gn), F32)
            dwg_ref[...] = jnp.zeros_like(dwg_ref)
            vsa_ref[...] = jnp.zeros_like(vsa_ref)
            vgn_ref[...] = jnp.zeros_like(vgn_ref)

        yv = y_ref[...]
        uv = u_ref[...]
        zz = _gelu(yv)
        sg = _sigmoid(_dot(zz, wg_ref[...]) + bg_ref[...])
        dyav = dya_ref[...]
        dq = dyav * zz * sg * (1.0 - sg)
        dzz = dyav * sg + _dot_nt(dq, wg_ref[...])
        dwg_ref[...] += _dot_tn(zz, dq)
        dy = dzz * _gelu_grad(yv)
        dyb = dy.astype(BF16)
        dy_ref[...] = dyb.astype(dy_ref.dtype)
        vsa_ref[0:1, :] += _rowsum(dq)
        vsa_ref[1:2, :] += _rowsum(dy * uv)
        gr_s[0:tc, :] = _dot_nt(dyb, cre_ref[...])
        gi_s[0:tc, :] = _dot_nt(dyb, cim_ref[...])
        a_r = ar_ref[...]
        a_i = ai_ref[...]

        def step(i, carry):
            l_r, l_i = carry
            at = pl.ds(tc - 1 - i, 1)
            n_r = gr_s[at, :] + a_r * l_r + a_i * l_i
            n_i = gi_s[at, :] + a_r * l_i - a_i * l_r
            gr_s[at, :] = n_r
            gi_s[at, :] = n_i
            return n_r, n_i

        l_r, l_i = lax.fori_loop(0, tc, step, (cr_s[0:1, :], ci_s[0:1, :]), unroll=8)
        cr_s[0:1, :] = l_r
        ci_s[0:1, :] = l_i
        nxt_r = gr_s[1:tc + 1, :]
        nxt_i = gi_s[1:tc + 1, :]
        srv = sr_ref[...]
        siv = si_ref[...]
        vgn_ref[0:1, :] += _rowsum(nxt_r * srv + nxt_i * siv)
        vgn_ref[1:2, :] += _rowsum(nxt_i * srv - nxt_r * siv)
        lam_r = gr_s[0:tc, :]
        lam_i = gi_s[0:tc, :]
        gr_s[tc:tc + halo, :] = gr_s[0:halo, :]
        gi_s[tc:tc + halo, :] = gi_s[0:halo, :]
        lrb = lam_r.astype(BF16)
        lib = lam_i.astype(BF16)
        lr_ref[...] = lrb.astype(lr_ref.dtype)
        li_ref[...] = lib.astype(li_ref.dtype)
        du_ref[...] = _dot_nt(lrb, bbr_ref[...]) + _dot_nt(lib, bbi_ref[...]) + dy * d_ref[...]

    return pl.pallas_call(
        body, name="s5_bwd", grid=(steps,),
        out_shape=[jax.ShapeDtypeStruct((t, sa), F32), jax.ShapeDtypeStruct((t, gn), BF16),
                   jax.ShapeDtypeStruct((t, gn), BF16), jax.ShapeDtypeStruct((t, sa), BF16),
                   jax.ShapeDtypeStruct((sa, sa), F32), jax.ShapeDtypeStruct((SUBLANES, sa), F32),
                   jax.ShapeDtypeStruct((SUBLANES, gn), F32)],
        in_specs=[_rows_rev(tc, sa, steps), _rows_rev(tc, sa, steps), _rows_rev(tc, gn, steps), _rows_rev(tc, gn, steps),
                  _rows_rev(tc, sa, steps), _whole((sa, sa)), _whole((1, sa)), _whole((gn, sa)), _whole((gn, sa)),
                  _whole((sa, gn)), _whole((sa, gn)), _whole((1, gn)), _whole((1, gn)), _whole((1, sa))],
        out_specs=[_rows_rev(tc, sa, steps), _rows_rev(tc, gn, steps), _rows_rev(tc, gn, steps), _rows_rev(tc, sa, steps),
                   _acc((sa, sa)), _acc((SUBLANES, sa)), _acc((SUBLANES, gn))],
        scratch_shapes=[pltpu.VMEM((tc + halo, gn), F32), pltpu.VMEM((tc + halo, gn), F32),
                        pltpu.VMEM((SUBLANES, gn), F32), pltpu.VMEM((SUBLANES, gn), F32)],
        compiler_params=_params("arbitrary"),
    )(dya, y, sr, si, u, w_glu, b_glu, cre_blk, cimn_blk, bbr_blk, bbi_blk, ar, ai, d_skip)


def _inproj_bwd(dparts, x, dx1, g_mix, w_in):
    t, d = x.shape
    n = w_in.shape[1]
    widths = [p.shape[1] for p in dparts]
    offs = [sum(widths[:i]) for i in range(len(widths) + 1)]
    tm = min(TOKEN_TILE, t)
    np_ = len(dparts)

    def body(*refs):
        dz_refs = refs[:np_]
        x_ref, dx1_ref, g_ref, w_ref, gx_ref, h_ref, dz_ref, vd_ref, vn_ref = refs[np_:]
        _zero_on_first(vd_ref, vn_ref)
        dh = jnp.zeros((tm, d), F32)
        for k, r in enumerate(dz_refs):
            lo, hi = offs[k], offs[k + 1]
            dzk = r[...]
            dh = dh + _dot_nt(dzk, w_ref[:, lo:hi])
            dz_ref[:, lo:hi] = dzk.astype(dz_ref.dtype)
            vn_ref[0:1, lo:hi] += _rowsum(dzk)
        xhat, r0 = _rms_stats(x_ref[...])
        h_ref[...] = (xhat * g_ref[...]).astype(h_ref.dtype)
        dxn, dg = _rms_bwd(dh, xhat, r0, g_ref[...])
        gx_ref[...] = dx1_ref[...] + dxn
        vd_ref[0:1, :] += _rowsum(dg)

    return pl.pallas_call(
        body, name="inproj_bwd", grid=(t // tm,),
        out_shape=[jax.ShapeDtypeStruct((t, d), F32), jax.ShapeDtypeStruct((t, d), BF16),
                   jax.ShapeDtypeStruct((t, n), BF16), jax.ShapeDtypeStruct((SUBLANES, d), F32),
                   jax.ShapeDtypeStruct((SUBLANES, n), F32)],
        in_specs=[_rows(tm, w) for w in widths] + [_rows(tm, d), _rows(tm, d), _whole((1, d)), _whole((d, n))],
        out_specs=[_rows(tm, d), _rows(tm, d), _rows(tm, n), _acc((SUBLANES, d)), _acc((SUBLANES, n))],
        compiler_params=_params("arbitrary"),
    )(*dparts, x, dx1, g_mix, w_in)


def _adamw_math(w, g, m, v):
    m_new = ADAM_B1 * m + (1.0 - ADAM_B1) * g
    v_new = ADAM_B2 * v + (1.0 - ADAM_B2) * (g * g)
    m_hat = m_new / (1.0 - ADAM_B1 ** ADAM_STEP)
    v_hat = v_new / (1.0 - ADAM_B2 ** ADAM_STEP)
    delta = -ADAM_LR * (m_hat / (jnp.sqrt(v_hat) + ADAM_EPS) + ADAM_WD * w)
    return delta, m_new, v_new


def _row_tile(rows):
    for cand in (256, 128, 64, 32, 16, 8):
        if rows % cand == 0:
            return cand
    return rows


def _adamw(parts, w, m, v, name):
    npart, rows, cols = parts.shape
    tr = _row_tile(rows)

    def body(p_ref, w_ref, m_ref, v_ref, g_ref, d_ref, mo_ref, vo_ref):
        g = p_ref[0].astype(F32)
        for k in range(1, npart):
            g = g + p_ref[k].astype(F32)
        delta, m_new, v_new = _adamw_math(w_ref[...], g, m_ref[...], v_ref[...])
        g_ref[...] = g
        d_ref[...] = delta
        mo_ref[...] = m_new
        vo_ref[...] = v_new

    return pl.pallas_call(
        body, name=name, grid=(rows // tr,),
        out_shape=[jax.ShapeDtypeStruct((rows, cols), F32)] * 4,
        in_specs=[pl.BlockSpec((npart, tr, cols), lambda i: (0, i, 0))] + [_rows(tr, cols)] * 3,
        out_specs=[_rows(tr, cols)] * 4,
        compiler_params=_params("parallel"),
    )(parts, w, m, v)


def _sum_parts(parts, name):
    npart, rows, cols = parts.shape
    tr = _row_tile(rows)

    def body(p_ref, o_ref):
        g = p_ref[0].astype(F32)
        for k in range(1, npart):
            g = g + p_ref[k].astype(F32)
        o_ref[...] = g

    return pl.pallas_call(
        body, name=name, grid=(rows // tr,),
        out_shape=jax.ShapeDtypeStruct((rows, cols), F32),
        in_specs=[pl.BlockSpec((npart, tr, cols), lambda i: (0, i, 0))],
        out_specs=_rows(tr, cols),
        compiler_params=_params("parallel"),
    )(parts)


def _mesh_position():
    return lax.axis_index("x"), lax.axis_index("y"), lax.axis_index("c")


def _flip(pos, rel):
    x, y, c = pos
    return (1 - x if rel & 4 else x, 1 - y if rel & 2 else y, 1 - c if rel & 1 else c)


def _index(pos):
    return 4 * pos[0] + 2 * pos[1] + pos[2]


_ANY = pl.BlockSpec(memory_space=pl.ANY)


def _all_gather(shards, name):
    n = len(shards)
    chips = (4, 2, 6)

    def body(*refs):
        ins, outs = refs[:n], refs[n:2 * n]
        send_sems, recv_sems, local_sems = refs[2 * n:]
        me = _mesh_position()
        sibling = _flip(me, 1)

        def copy(i, k, block, to, src=None):
            dst = outs[i].at[_index(block)]
            return pltpu.make_async_remote_copy(
                src_ref=dst if src is None else src, dst_ref=dst,
                send_sem=send_sems.at[i, k], recv_sem=recv_sems.at[i, k], device_id=to, device_id_type=MESH)

        mine, first, passed = [], [], []
        for i in range(n):
            cp = pltpu.make_async_copy(ins[i], outs[i].at[_index(me)], local_sems.at[i])
            cp.start()
            mine.append(cp)
            first.append(copy(i, 0, me, sibling, src=ins[i]))
            first += [copy(i, 1 + j, me, _flip(me, rel), src=ins[i]) for j, rel in enumerate(chips)]
        for cp in first:
            cp.start()
        for j, rel in enumerate(chips):
            for i in range(n):
                copy(i, 1 + j, _flip(me, rel), me).wait_recv()
                fwd = copy(i, 4 + j, _flip(me, rel), sibling)
                fwd.start()
                passed.append(fwd)
        for i in range(n):
            copy(i, 0, sibling, me).wait_recv()
            for j, rel in enumerate(chips):
                copy(i, 4 + j, _flip(sibling, rel), me).wait_recv()
        for cp in first + passed:
            cp.wait_send()
        for cp in mine:
            cp.wait()

    return pl.pallas_call(
        body, name=name,
        out_shape=[jax.ShapeDtypeStruct((N_DEV,) + s.shape, s.dtype) for s in shards],
        in_specs=[_ANY] * n, out_specs=[_ANY] * n,
        scratch_shapes=[pltpu.SemaphoreType.DMA((n, 7)), pltpu.SemaphoreType.DMA((n, 7)), pltpu.SemaphoreType.DMA((n,))],
    )(*shards)


def _exchange_blocks(blocked, name):
    n = len(blocked)

    def body(*refs):
        ins, outs = refs[:n], refs[n:2 * n]
        send_sems, recv_sems, local_sems = refs[2 * n:]
        me = _mesh_position()
        sends, mine = [], []
        for i in range(n):
            cp = pltpu.make_async_copy(ins[i].at[_index(me)], outs[i].at[_index(me)], local_sems.at[i])
            cp.start()
            mine.append(cp)
            for rel in range(1, N_DEV):
                peer = _flip(me, rel)
                cp = pltpu.make_async_remote_copy(
                    src_ref=ins[i].at[_index(peer)], dst_ref=outs[i].at[_index(me)],
                    send_sem=send_sems.at[i, rel - 1], recv_sem=recv_sems.at[i, rel - 1], device_id=peer, device_id_type=MESH)
                cp.start()
                sends.append(cp)
        for i in range(n):
            for rel in range(1, N_DEV):
                peer = _flip(me, rel)
                pltpu.make_async_remote_copy(
                    src_ref=ins[i].at[_index(me)], dst_ref=outs[i].at[_index(peer)],
                    send_sem=send_sems.at[i, rel - 1], recv_sem=recv_sems.at[i, rel - 1], device_id=peer,
                    device_id_type=MESH).wait_recv()
        for cp in sends:
            cp.wait_send()
        for cp in mine:
            cp.wait()

    return pl.pallas_call(
        body, name=name,
        out_shape=[jax.ShapeDtypeStruct(b.shape, b.dtype) for b in blocked],
        in_specs=[_ANY] * n, out_specs=[_ANY] * n,
        scratch_shapes=[pltpu.SemaphoreType.DMA((n, 7)), pltpu.SemaphoreType.DMA((n, 7)), pltpu.SemaphoreType.DMA((n,))],
    )(*blocked)


SHARDED = {"w_in": 1, "w_glu": 0, "conv_w": 1, "w_a_out": 1, "w_b_out": 0, "w_o": 0, "w_ffn_gate": 1, "w_ffn_up": 1,
           "w_ffn_down": 0, "w_ple_gate": 0, "w_ple": 1}
SMALL = ["g_mix", "b_in", "lam_re", "lam_im", "log_dt", "s5_b_re", "s5_b_im", "s5_c_re", "s5_c_im", "s5_d", "b_glu",
         "conv_b", "w_r", "b_r", "w_i", "b_i", "lru_lambda", "g_ffn", "g_ple_gate", "b_ple_gate", "g_ple", "g_final"]
WEIGHTS = ["g_mix", "w_in", "b_in", "lam_re", "lam_im", "log_dt", "s5_b_re", "s5_b_im", "s5_c_re", "s5_c_im", "s5_d",
           "w_glu", "b_glu", "conv_w", "conv_b", "w_r", "b_r", "w_i", "b_i", "lru_lambda", "w_a_out", "w_b_out", "w_o",
           "g_ffn", "w_ffn_gate", "w_ffn_up", "w_ffn_down", "g_ple_gate", "w_ple_gate", "b_ple_gate", "w_ple", "g_ple",
           "g_final"]


def _unblock(gathered, axis):
    nb, r, c = gathered.shape
    if axis == 0:
        return gathered.reshape(nb * r, c)
    return jnp.transpose(gathered, (1, 0, 2)).reshape(r, nb * c)


def _block(full, axis):
    r, c = full.shape
    if axis == 0:
        return full.reshape(N_DEV, r // N_DEV, c)
    return jnp.transpose(full.reshape(r, N_DEV, c // N_DEV), (1, 0, 2))


def _block_diag(w):
    h, i, j = w.shape
    eye = jnp.eye(h, dtype=w.dtype)
    return (w[:, :, None, :] * eye[:, None, :, None]).reshape(h * i, h * j)


def _diag_blocks(dense, h):
    hi, hj = dense.shape
    i, j = hi // h, hj // h
    return jnp.einsum("hihj->hij", dense.reshape(h, i, h, j))


def _s5_discretise(lam_re, lam_im, log_dt, b_re, b_im):
    dt = jnp.exp(log_dt)[:, None]
    mag = jnp.exp(lam_re * dt)
    ar = mag * jnp.cos(lam_im * dt)
    ai = mag * jnp.sin(lam_im * dt)
    den = lam_re * lam_re + lam_im * lam_im
    nr = ar - 1.0
    fr = (nr * lam_re + ai * lam_im) / den
    fi = (ai * lam_re - nr * lam_im) / den
    bbr = fr[..., None] * b_re - fi[..., None] * b_im
    bbi = fr[..., None] * b_im + fi[..., None] * b_re
    return ar, ai, bbr, bbi


def _pack(arrs, rows):
    flat = jnp.concatenate([a.reshape(-1).astype(F32) for a in arrs])
    return jnp.pad(flat, (0, rows * PACK_COLS - flat.shape[0])).reshape(rows, PACK_COLS)


def _unpack(packed, shapes):
    flat = packed.reshape(-1)
    out, off = [], 0
    for s in shapes:
        size = math.prod(s)
        out.append(flat[off:off + size].reshape(s))
        off += size
    return out


def _local_step(x, p, target, full, small):
    d = x.shape[1]
    g, n, pch = small["s5_b_re"].shape
    heads = small["w_r"].shape[0]
    sa, lw = g * pch, small["lru_lambda"].shape[-1]
    widths = [sa, lw, d, d]
    row = lambda v: v.reshape(1, -1)

    (ar, ai, bbr, bbi), disc_vjp = jax.vjp(_s5_discretise, small["lam_re"], small["lam_im"], small["log_dt"],
                                           small["s5_b_re"], small["s5_b_im"])
    bbr_blk = _block_diag(jnp.transpose(bbr, (0, 2, 1))).astype(BF16)
    bbi_blk = _block_diag(jnp.transpose(bbi, (0, 2, 1))).astype(BF16)
    cre_blk = _block_diag(jnp.transpose(small["s5_c_re"], (0, 2, 1))).astype(BF16)
    cimn_blk = _block_diag(jnp.transpose(-small["s5_c_im"], (0, 2, 1))).astype(BF16)
    wr_blk = _block_diag(small["w_r"]).astype(BF16)
    wi_blk = _block_diag(small["w_i"]).astype(BF16)
    ar_row, ai_row = row(ar), row(ai)
    d_row = row(small["s5_d"])
    conv_w = full["conv_w"].astype(F32)

    u_a, u_b, za, zb = _inproj_fwd(x, row(small["g_mix"]), full["w_in"], row(small["b_in"]), widths)
    sr, si, y, y_a = _s5_fwd(u_a, bbr_blk, bbi_blk, ar_row, ai_row, cre_blk, cimn_blk, d_row, full["w_glu"],
                             row(small["b_glu"]))
    xc, h, hprev = _lru_fwd(u_b, conv_w, row(small["conv_b"]), wr_blk, row(small["b_r"]), wi_blk, row(small["b_i"]),
                            row(small["lru_lambda"]))
    x1, merged, ma, mb = _merge_fwd(y_a, h, za, zb, x, full["w_a_out"], full["w_b_out"], full["w_o"])
    fg, fu = _ffn_up_fwd(x1, row(small["g_ffn"]), full["w_ffn_gate"], full["w_ffn_up"])
    x2 = _ffn_down_fwd(fg, fu, x1, full["w_ffn_down"])

    dx2, loss_blk, dw_pg, dw_ple, vec_tail = _tail_fwd_bwd(
        x2, p, target, row(small["g_ple_gate"]), full["w_ple_gate"], row(small["b_ple_gate"]), full["w_ple"],
        row(small["g_ple"]), row(small["g_final"]))
    dfg, dfu, act = _ffn_bwd_a(dx2, fg, fu, full["w_ffn_down"])
    dx1, h2, vec_ffn = _ffn_bwd_b(dfg, dfu, x1, dx2, row(small["g_ffn"]), full["w_ffn_gate"], full["w_ffn_up"])
    f = fg.shape[1]
    tn_f = f // 2 if (f // 2) % LANES == 0 else f
    dw_down = _matmul_tn(act, dx2, min(d, 512), "dw_ffn_down")
    dw_gate = _matmul_tn(h2, dfg, tn_f, "dw_ffn_gate")
    dw_up = _matmul_tn(h2, dfu, tn_f, "dw_ffn_up")
    dza, dzb, dya, dyb, dw_o, dw_a, dw_b = _merge_bwd(dx1, merged, ma, mb, za, zb, y_a, h, full["w_o"], full["w_a_out"],
                                                       full["w_b_out"])
    du_b, dwr_dense, dwi_dense, vec_lru = _lru_bwd(dyb, xc, hprev, u_b, conv_w, wr_blk, row(small["b_r"]), wi_blk,
                                                   row(small["b_i"]), row(small["lru_lambda"]))
    du_a, lam_r, lam_i, dy16, dw_glu, vec_sa, vec_gn = _s5_bwd(
        dya, y, sr, si, u_a, full["w_glu"], row(small["b_glu"]), cre_blk, cimn_blk, bbr_blk, bbi_blk, ar_row, ai_row, d_row)
    tn_s = min(sa, 512)
    dbbr_dense = _matmul_tn(lam_r, u_a, tn_s, "dbb_re")
    dbbi_dense = _matmul_tn(lam_i, u_a, tn_s, "dbb_im")
    dcre_dense = _matmul_tn(sr, dy16, tn_s, "dc_re")
    dcimn_dense = _matmul_tn(si, dy16, tn_s, "dc_im")
    grad_x, h0, dz16, vec_mix, vec_bin = _inproj_bwd([du_a, du_b, dza, dzb], x, dx1, row(small["g_mix"]), full["w_in"])
    nz = dz16.shape[1]
    tn_z = 512 if nz % 512 == 0 else nz
    dw_in = _matmul_tn(h0, dz16, tn_z, "dw_in")

    d_bbr = _diag_blocks(dbbr_dense, g)
    d_bbi = _diag_blocks(dbbi_dense, g)
    d_cre = jnp.transpose(_diag_blocks(dcre_dense, g), (0, 2, 1))
    d_cim = -jnp.transpose(_diag_blocks(dcimn_dense, g), (0, 2, 1))
    pre = {"d_ar": vec_gn[0].reshape(g, n), "d_ai": vec_gn[1].reshape(g, n), "d_bbr": d_bbr, "d_bbi": d_bbi}
    gs = {
        "g_mix": vec_mix[0], "b_in": vec_bin[0], "s5_c_re": d_cre, "s5_c_im": d_cim, "s5_d": vec_sa[1].reshape(g, pch),
        "b_glu": vec_sa[0], "conv_b": vec_lru[0], "w_r": _diag_blocks(dwr_dense, heads), "b_r": vec_lru[1].reshape(heads, -1),
        "w_i": _diag_blocks(dwi_dense, heads), "b_i": vec_lru[2].reshape(heads, -1), "lru_lambda": vec_lru[3],
        "g_ffn": vec_ffn[0], "g_ple_gate": vec_tail[1], "b_ple_gate": vec_tail[0], "g_ple": vec_tail[2],
        "g_final": vec_tail[3],
    }
    gfull = {"w_in": dw_in, "w_glu": dw_glu, "conv_w": vec_lru[4:4 + CONV_WIDTH], "w_a_out": dw_a, "w_b_out": dw_b,
             "w_o": dw_o, "w_ffn_gate": dw_gate, "w_ffn_up": dw_up, "w_ffn_down": dw_down, "w_ple_gate": dw_pg,
             "w_ple": dw_ple}
    return loss_blk[0, 0], grad_x, gfull, gs, pre, disc_vjp


_PRE_KEYS = ["d_ar", "d_ai", "d_bbr", "d_bbi"]
_DIRECT_SMALL = [k for k in SMALL if k not in ("lam_re", "lam_im", "log_dt", "s5_b_re", "s5_b_im")]


def kernel(x, p, g_mix, w_in, b_in, lam_re, lam_im, log_dt, s5_b_re, s5_b_im, s5_c_re, s5_c_im, s5_d, w_glu, b_glu, conv_w, conv_b, w_r, b_r, w_i, b_i, lru_lambda, w_a_out, w_b_out, w_o, g_ffn, w_ffn_gate, w_ffn_up, w_ffn_down, g_ple_gate, w_ple_gate, b_ple_gate, w_ple, g_ple, g_final, loss_target, m_g_mix, m_w_in, m_b_in, m_lam_re, m_lam_im, m_log_dt, m_s5_b_re, m_s5_b_im, m_s5_c_re, m_s5_c_im, m_s5_d, m_w_glu, m_b_glu, m_conv_w, m_conv_b, m_w_r, m_b_r, m_w_i, m_b_i, m_lru_lambda, m_w_a_out, m_w_b_out, m_w_o, m_g_ffn, m_w_ffn_gate, m_w_ffn_up, m_w_ffn_down, m_g_ple_gate, m_w_ple_gate, m_b_ple_gate, m_w_ple, m_g_ple, m_g_final, v_g_mix, v_w_in, v_b_in, v_lam_re, v_lam_im, v_log_dt, v_s5_b_re, v_s5_b_im, v_s5_c_re, v_s5_c_im, v_s5_d, v_w_glu, v_b_glu, v_conv_w, v_conv_b, v_w_r, v_b_r, v_w_i, v_b_i, v_lru_lambda, v_w_a_out, v_w_b_out, v_w_o, v_g_ffn, v_w_ffn_gate, v_w_ffn_up, v_w_ffn_down, v_g_ple_gate, v_w_ple_gate, v_b_ple_gate, v_w_ple, v_g_ple, v_g_final):
    given = dict(locals())
    wts = {k: given[k] for k in WEIGHTS}
    moms = {k: given["m_" + k] for k in WEIGHTS}
    vels = {k: given["v_" + k] for k in WEIGHTS}

    def drop_depth(k, a):
        return a if k == "g_final" else a[0]

    small = {k: drop_depth(k, wts[k]) for k in SMALL}
    shard = {k: wts[k][0] for k in SHARDED}
    names = list(SHARDED)

    gathered = _all_gather([shard[k] if k == "conv_w" else shard[k].astype(BF16) for k in names], "gather_weights")
    full = {k: _unblock(gw, SHARDED[k]) for k, gw in zip(names, gathered)}

    loss_part, grad_x, gfull, gs, pre, disc_vjp = _local_step(x[0], p[0, 0], loss_target[0], full, small)
    loss = lax.psum(loss_part, ("x", "y", "c"))

    pack_shapes = [pre[k].shape for k in _PRE_KEYS] + [small[k].shape for k in _DIRECT_SMALL]
    pack_size = sum(math.prod(s) for s in pack_shapes)
    slice_rows = -(-pack_size // (N_DEV * PACK_COLS * SUBLANES)) * SUBLANES
    packed = _pack([pre[k] for k in _PRE_KEYS] + [gs[k] for k in _DIRECT_SMALL], N_DEV * slice_rows)
    blocked = [_block(gfull[k], SHARDED[k]).astype(BF16) for k in names]
    blocked.append(packed.reshape(N_DEV, slice_rows, PACK_COLS))
    received = _exchange_blocks(blocked, "exchange_grads")

    my_slice = _sum_parts(received[-1], "reduce_small")
    (reduced,) = _all_gather([my_slice], "gather_small")
    red = _unpack(reduced.reshape(N_DEV * slice_rows, PACK_COLS), pack_shapes)
    red_pre = dict(zip(_PRE_KEYS, red[:len(_PRE_KEYS)]))
    g_small = dict(zip(_DIRECT_SMALL, red[len(_PRE_KEYS):]))
    d_lr, d_li, d_ldt, d_bre, d_bim = disc_vjp((red_pre["d_ar"], red_pre["d_ai"], red_pre["d_bbr"], red_pre["d_bbi"]))
    g_small.update(lam_re=d_lr, lam_im=d_li, log_dt=d_ldt, s5_b_re=d_bre, s5_b_im=d_bim)
    small_shapes = [small[k].shape for k in SMALL]
    small_rows = -(-sum(math.prod(s) for s in small_shapes) // (PACK_COLS * SUBLANES)) * SUBLANES
    g_pk = _pack([g_small[k] for k in SMALL], small_rows)
    w_pk = _pack([small[k] for k in SMALL], small_rows)
    m_pk = _pack([drop_depth(k, moms[k]) for k in SMALL], small_rows)
    v_pk = _pack([drop_depth(k, vels[k]) for k in SMALL], small_rows)
    small_out = _adamw(g_pk[None], w_pk, m_pk, v_pk, "adamw_small")
    small_out = [dict(zip(SMALL, _unpack(o, small_shapes))) for o in small_out]

    big_out = {}
    for k, parts in zip(names, received[:-1]):
        big_out[k] = _adamw(parts, shard[k], moms[k][0], vels[k][0], "adamw_" + k)

    outs = [loss, grad_x[None]]
    for slot in range(4):
        for k in WEIGHTS:
            if k in SHARDED:
                outs.append(big_out[k][slot][None])
            else:
                a = small_out[slot][k]
                outs.append(a if k == "g_final" else a[None])
    return tuple(outs)
```

```python
import functools
import math

import jax
import jax.numpy as jnp
from jax import lax
from jax.experimental import pallas as pl
from jax.experimental.pallas import tpu as pltpu

F32 = jnp.float32
BF16 = jnp.bfloat16

EPS = 1e-6
LRU_C = 8.0
CONV_WIDTH = 4
ADAM_LR = 0.001
ADAM_B1 = 0.9
ADAM_B2 = 0.999
ADAM_EPS = 1e-08
ADAM_WD = 0.01
ADAM_STEP = 10

N_DEV = 8
MESH = pl.DeviceIdType.MESH
SUBLANES = 8
LANES = 128
VMEM_LIMIT = 56 * 1024 * 1024
TOKEN_TILE = 256
TIME_CHUNK = 256
PACK_COLS = 1024


def _dot(a, b):
    return jnp.dot(a.astype(BF16), b.astype(BF16), preferred_element_type=F32)


def _dot_nt(a, b):
    return lax.dot_general(a.astype(BF16), b.astype(BF16), (((1,), (1,)), ((), ())), preferred_element_type=F32)


def _dot_tn(a, b):
    return lax.dot_general(a.astype(BF16), b.astype(BF16), (((0,), (0,)), ((), ())), preferred_element_type=F32)


def _sigmoid(x):
    return jax.nn.sigmoid(x)


def _rms_stats(x):
    r = lax.rsqrt(jnp.mean(x * x, axis=-1, keepdims=True) + EPS)
    return x * r, r


def _rms_bwd(dy, xhat, r, g):
    dxn = dy * g
    dx = r * (dxn - xhat * jnp.mean(dxn * xhat, axis=-1, keepdims=True))
    return dx, dy * xhat


def _rowsum(v):
    return jnp.sum(v, axis=0, keepdims=True)


def _expm1(x):
    u = jnp.exp(x)
    um1 = u - 1.0
    safe = jnp.where(um1 == 0.0, 1.0, jnp.log(u))
    return jnp.where(um1 == 0.0, x, um1 * x / safe)


def _softplus(x):
    e = jnp.exp(-jnp.abs(x))
    u = 1.0 + e
    um1 = u - 1.0
    safe = jnp.where(um1 == 0.0, 1.0, um1)
    log1p_e = jnp.where(um1 == 0.0, e, jnp.log(u) * e / safe)
    return jnp.maximum(x, 0.0) + log1p_e


_GELU_K = math.sqrt(2.0 / math.pi)
_GELU_C = 0.044715


def _gelu(x):
    return 0.5 * x * (1.0 + jnp.tanh(_GELU_K * (x + _GELU_C * x * x * x)))


def _gelu_grad(x):
    th = jnp.tanh(_GELU_K * (x + _GELU_C * x * x * x))
    return 0.5 * (1.0 + th) + 0.5 * x * (1.0 - th * th) * _GELU_K * (1.0 + 3.0 * _GELU_C * x * x)


def _params(*sem):
    return pltpu.CompilerParams(dimension_semantics=sem, vmem_limit_bytes=VMEM_LIMIT)


def _rows(tm, n):
    return pl.BlockSpec((tm, n), lambda i: (i, 0))


def _rows_rev(tm, n, steps):
    return pl.BlockSpec((tm, n), lambda i: (steps - 1 - i, 0))


def _whole(shape):
    nd = len(shape)
    return pl.BlockSpec(shape, lambda i: (0,) * nd, pipeline_mode=pl.Buffered(1))


def _acc(shape):
    nd = len(shape)
    return pl.BlockSpec(shape, lambda i: (0,) * nd)


def _zero_on_first(*refs):
    @pl.when(pl.program_id(0) == 0)
    def _():
        for r in refs:
            r[...] = jnp.zeros_like(r)


def _inproj_fwd(x, g_mix, w_in_t, b_in, widths):
    t, d = x.shape
    n = w_in_t.shape[0]
    tm = min(TOKEN_TILE, t)
    offs = [sum(widths[:i]) for i in range(len(widths) + 1)]

    def body(x_ref, g_ref, w_ref, b_ref, *outs):
        xhat, _ = _rms_stats(x_ref[...])
        h = (xhat * g_ref[...]).astype(BF16)
        for k, o_ref in enumerate(outs):
            lo, hi = offs[k], offs[k + 1]
            o_ref[...] = _dot_nt(h, w_ref[lo:hi, :]) + b_ref[:, lo:hi]

    return pl.pallas_call(
        body, name="inproj_fwd", grid=(t // tm,),
        out_shape=[jax.ShapeDtypeStruct((t, w), F32) for w in widths],
        in_specs=[_rows(tm, d), _whole((1, d)), _whole((n, d)), _whole((1, n))],
        out_specs=[_rows(tm, w) for w in widths],
        compiler_params=_params("parallel"),
    )(x, g_mix, w_in_t, b_in)


def _s5_fwd(u, bbr_blk, bbi_blk, ar, ai, cre_blk, cimn_blk, d_skip, w_glu, b_glu):
    t, sa = u.shape
    gn = bbr_blk.shape[1]
    tc = min(TIME_CHUNK, t)

    def body(u_ref, bbr_ref, bbi_ref, ar_ref, ai_ref, cre_ref, cim_ref, d_ref, wg_ref, bg_ref,
             sr_ref, si_ref, y_ref, ya_ref, cr_s, ci_s):
        _zero_on_first(cr_s, ci_s)
        uv = u_ref[...]
        ub = uv.astype(BF16)
        sr_ref[...] = jnp.dot(ub, bbr_ref[...], preferred_element_type=F32)
        si_ref[...] = jnp.dot(ub, bbi_ref[...], preferred_element_type=F32)
        a_r = ar_ref[...]
        a_i = ai_ref[...]

        def step(row, carry):
            c_r, c_i = carry
            at = pl.ds(row, 1)
            n_r = a_r * c_r - a_i * c_i + sr_ref[at, :]
            n_i = a_r * c_i + a_i * c_r + si_ref[at, :]
            sr_ref[at, :] = n_r
            si_ref[at, :] = n_i
            return n_r, n_i

        c_r, c_i = lax.fori_loop(0, tc, step, (cr_s[0:1, :], ci_s[0:1, :]), unroll=8)
        cr_s[0:1, :] = c_r
        ci_s[0:1, :] = c_i
        y = _dot(sr_ref[...], cre_ref[...]) + _dot(si_ref[...], cim_ref[...]) + d_ref[...] * uv
        y_ref[...] = y
        zz = _gelu(y)
        q = _dot(zz, wg_ref[...]) + bg_ref[...]
        ya_ref[...] = zz * _sigmoid(q)

    return pl.pallas_call(
        body, name="s5_fwd", grid=(t // tc,),
        out_shape=[jax.ShapeDtypeStruct((t, gn), F32), jax.ShapeDtypeStruct((t, gn), F32),
                   jax.ShapeDtypeStruct((t, sa), F32), jax.ShapeDtypeStruct((t, sa), F32)],
        in_specs=[_rows(tc, sa), _whole((sa, gn)), _whole((sa, gn)), _whole((1, gn)), _whole((1, gn)),
                  _whole((gn, sa)), _whole((gn, sa)), _whole((1, sa)), _whole((sa, sa)), _whole((1, sa))],
        out_specs=[_rows(tc, gn), _rows(tc, gn), _rows(tc, sa), _rows(tc, sa)],
        scratch_shapes=[pltpu.VMEM((SUBLANES, gn), F32), pltpu.VMEM((SUBLANES, gn), F32)],
        compiler_params=_params("arbitrary"),
    )(u, bbr_blk, bbi_blk, ar, ai, cre_blk, cimn_blk, d_skip, w_glu, b_glu)


def _lru_gates(xc, wr_ref, br_ref, wi_ref, bi_ref, lam_ref):
    r = _sigmoid(_dot(xc, wr_ref[...]) + br_ref[...])
    ig = _sigmoid(_dot(xc, wi_ref[...]) + bi_ref[...])
    sp = _softplus(-lam_ref[...])
    log_a = (-LRU_C * r) * sp
    return r, ig, sp, log_a


def _lru_fwd(u, conv_w, conv_b, wr_blk, b_r, wi_blk, b_i, lru_lambda):
    t, w = u.shape
    tc = min(TIME_CHUNK, t)
    halo = SUBLANES

    def body(u_ref, cw_ref, cb_ref, wr_ref, br_ref, wi_ref, bi_ref, lam_ref,
             xc_ref, h_ref, hp_ref, ext_s, a_s, carry_s):
        @pl.when(pl.program_id(0) == 0)
        def _():
            ext_s[0:halo, :] = jnp.zeros((halo, w), F32)
            carry_s[...] = jnp.zeros_like(carry_s)

        ext_s[halo:halo + tc, :] = u_ref[...]
        xc = cb_ref[...]
        for k in range(CONV_WIDTH):
            off = halo - (CONV_WIDTH - 1) + k
            xc = xc + cw_ref[k:k + 1, :] * ext_s[off:off + tc, :]
        ext_s[0:halo, :] = ext_s[tc:tc + halo, :]
        xc_ref[...] = xc
        r, ig, sp, log_a = _lru_gates(xc, wr_ref, br_ref, wi_ref, bi_ref, lam_ref)
        a_s[...] = jnp.exp(log_a)
        h_ref[...] = jnp.sqrt(-_expm1(2.0 * log_a)) * ig * xc

        def step(row, carry):
            at = pl.ds(row, 1)
            hp_ref[at, :] = carry
            nxt = a_s[at, :] * carry + h_ref[at, :]
            h_ref[at, :] = nxt
            return nxt

        carry_s[0:1, :] = lax.fori_loop(0, tc, step, carry_s[0:1, :], unroll=8)

    return pl.pallas_call(
        body, name="lru_fwd", grid=(t // tc,),
        out_shape=[jax.ShapeDtypeStruct((t, w), F32)] * 3,
        in_specs=[_rows(tc, w), _whole((CONV_WIDTH, w)), _whole((1, w)), _whole((w, w)), _whole((1, w)),
                  _whole((w, w)), _whole((1, w)), _whole((1, w))],
        out_specs=[_rows(tc, w)] * 3,
        scratch_shapes=[pltpu.VMEM((halo + tc, w), F32), pltpu.VMEM((tc, w), F32), pltpu.VMEM((SUBLANES, w), F32)],
        compiler_params=_params("arbitrary"),
    )(u, conv_w, conv_b, wr_blk, b_r, wi_blk, b_i, lru_lambda)


def _merge_fwd(y_a, h, za, zb, x, w_a_out_t, w_b_out, w_o):
    t, d = x.shape
    sa, lw = y_a.shape[1], h.shape[1]
    tm = min(TOKEN_TILE, t)

    def body(ya_ref, h_ref, za_ref, zb_ref, x_ref, wa_ref, wb_ref, wo_ref, x1_ref, mg_ref, ma_ref, mb_ref):
        ma = _dot_nt(ya_ref[...], wa_ref[...])
        mb = _dot(h_ref[...], wb_ref[...])
        merged = _sigmoid(za_ref[...]) * ma + _sigmoid(zb_ref[...]) * mb
        ma_ref[...] = ma
        mb_ref[...] = mb
        mg_ref[...] = merged.astype(mg_ref.dtype)
        x1_ref[...] = x_ref[...] + _dot(merged, wo_ref[...])

    return pl.pallas_call(
        body, name="merge_fwd", grid=(t // tm,),
        out_shape=[jax.ShapeDtypeStruct((t, d), F32), jax.ShapeDtypeStruct((t, d), BF16),
                   jax.ShapeDtypeStruct((t, d), F32), jax.ShapeDtypeStruct((t, d), F32)],
        in_specs=[_rows(tm, sa), _rows(tm, lw), _rows(tm, d), _rows(tm, d), _rows(tm, d),
                  _whole((d, sa)), _whole((lw, d)), _whole((d, d))],
        out_specs=[_rows(tm, d)] * 4,
        compiler_params=_params("parallel"),
    )(y_a, h, za, zb, x, w_a_out_t, w_b_out, w_o)


def _ffn_up_fwd(x1, g_ffn, w_gate_t, w_up_t):
    t, d = x1.shape
    f = w_gate_t.shape[0]
    tm = min(TOKEN_TILE, t)

    def body(x_ref, g_ref, wg_ref, wu_ref, fg_ref, fu_ref):
        xhat, _ = _rms_stats(x_ref[...])
        h2 = (xhat * g_ref[...]).astype(BF16)
        fg_ref[...] = _dot_nt(h2, wg_ref[...])
        fu_ref[...] = _dot_nt(h2, wu_ref[...])

    return pl.pallas_call(
        body, name="ffn_up_fwd", grid=(t // tm,),
        out_shape=[jax.ShapeDtypeStruct((t, f), F32)] * 2,
        in_specs=[_rows(tm, d), _whole((1, d)), _whole((f, d)), _whole((f, d))],
        out_specs=[_rows(tm, f)] * 2,
        compiler_params=_params("parallel"),
    )(x1, g_ffn, w_gate_t, w_up_t)


def _ffn_down_fwd(fg, fu, x1, w_down):
    t, d = x1.shape
    f = fg.shape[1]
    tm = min(TOKEN_TILE, t)

    def body(fg_ref, fu_ref, x_ref, wd_ref, x2_ref):
        fgv = fg_ref[...]
        act = fgv * _sigmoid(fgv) * fu_ref[...]
        x2_ref[...] = x_ref[...] + _dot(act, wd_ref[...])

    return pl.pallas_call(
        body, name="ffn_down_fwd", grid=(t // tm,),
        out_shape=jax.ShapeDtypeStruct((t, d), F32),
        in_specs=[_rows(tm, f), _rows(tm, f), _rows(tm, d), _whole((f, d))],
        out_specs=_rows(tm, d),
        compiler_params=_params("parallel"),
    )(fg, fu, x1, w_down)


def _store_on_last(pairs):
    @pl.when(pl.program_id(0) == pl.num_programs(0) - 1)
    def _():
        for acc, out in pairs:
            out[...] = acc[...].astype(out.dtype)


def _tail_fwd_bwd(x2, p, target, g_pg, w_pg, b_pg, w_ple_t, g_ple, g_final):
    t, d = x2.shape
    pd = p.shape[1]
    tm = min(TOKEN_TILE, t)

    def body(x2_ref, p_ref, tg_ref, gpg_ref, wpg_ref, bpg_ref, wple_ref, gple_ref, gfin_ref,
             dx2_ref, loss_ref, dwpg_out, dwple_out, vec_ref, dwpg_ref, dwple_ref):
        _zero_on_first(loss_ref, dwpg_ref, dwple_ref, vec_ref)
        x2v = x2_ref[...]
        xh2, r2 = _rms_stats(x2v)
        h3 = xh2 * gpg_ref[...]
        gp = _sigmoid(_dot(h3, wpg_ref[...]) + bpg_ref[...])
        pe = _dot_nt(p_ref[...], wple_ref[...])
        peh, r3 = _rms_stats(pe)
        e = peh * gple_ref[...]
        x3 = x2v + gp * e
        xh3, r4 = _rms_stats(x3)
        diff = xh3 * gfin_ref[...] - tg_ref[...]
        loss_ref[...] += 0.5 * jnp.sum(jnp.mean(diff * diff, axis=-1, keepdims=True))
        dy = diff * (1.0 / d)
        dx3, dgfin = _rms_bwd(dy, xh3, r4, gfin_ref[...])
        d_gp = dx3 * e
        d_e = dx3 * gp
        dpe, dgple = _rms_bwd(d_e, peh, r3, gple_ref[...])
        dwple_ref[...] += _dot_tn(dpe, p_ref[...])
        dpre = d_gp * gp * (1.0 - gp)
        dwpg_ref[...] += _dot_tn(h3, dpre)
        dh3 = _dot_nt(dpre, wpg_ref[...])
        dx2n, dgpg = _rms_bwd(dh3, xh2, r2, gpg_ref[...])
        dx2_ref[...] = dx3 + dx2n
        vec_ref[0:1, :] += _rowsum(dpre)
        vec_ref[1:2, :] += _rowsum(dgpg)
        vec_ref[2:3, :] += _rowsum(dgple)
        vec_ref[3:4, :] += _rowsum(dgfin)
        _store_on_last([(dwpg_ref, dwpg_out), (dwple_ref, dwple_out)])

    return pl.pallas_call(
        body, name="tail_fwd_bwd", grid=(t // tm,),
        out_shape=[jax.ShapeDtypeStruct((t, d), F32), jax.ShapeDtypeStruct((SUBLANES, LANES), F32),
                   jax.ShapeDtypeStruct((d, d), BF16), jax.ShapeDtypeStruct((d, pd), BF16),
                   jax.ShapeDtypeStruct((SUBLANES, d), F32)],
        in_specs=[_rows(tm, d), _rows(tm, pd), _rows(tm, d), _whole((1, d)), _whole((d, d)), _whole((1, d)),
                  _whole((d, pd)), _whole((1, d)), _whole((1, d))],
        out_specs=[_rows(tm, d), _acc((SUBLANES, LANES)), _acc((d, d)), _acc((d, pd)), _acc((SUBLANES, d))],
        scratch_shapes=[pltpu.VMEM((d, d), F32), pltpu.VMEM((d, pd), F32)],
        compiler_params=_params("arbitrary"),
    )(x2, p, target, g_pg, w_pg, b_pg, w_ple_t, g_ple, g_final)


def _ffn_bwd_a(dx2, fg, fu, w_down):
    t, d = dx2.shape
    f = fg.shape[1]
    tm = min(TOKEN_TILE, t)

    def body(dx_ref, fg_ref, fu_ref, wd_ref, dfg_ref, dfu_ref, act_ref):
        dact = _dot_nt(dx_ref[...], wd_ref[...])
        fgv = fg_ref[...]
        fuv = fu_ref[...]
        sg = _sigmoid(fgv)
        silu = fgv * sg
        dfu_ref[...] = (dact * silu).astype(dfu_ref.dtype)
        dfg_ref[...] = (dact * fuv * (sg * (1.0 + fgv * (1.0 - sg)))).astype(dfg_ref.dtype)
        act_ref[...] = (silu * fuv).astype(act_ref.dtype)

    return pl.pallas_call(
        body, name="ffn_bwd_a", grid=(t // tm,),
        out_shape=[jax.ShapeDtypeStruct((t, f), BF16)] * 3,
        in_specs=[_rows(tm, d), _rows(tm, f), _rows(tm, f), _whole((f, d))],
        out_specs=[_rows(tm, f)] * 3,
        compiler_params=_params("parallel"),
    )(dx2, fg, fu, w_down)


def _ffn_bwd_b(dfg, dfu, x1, dx2, g_ffn, w_gate_t, w_up_t):
    t, d = x1.shape
    f = dfg.shape[1]
    tm = min(TOKEN_TILE, t)

    def body(dfg_ref, dfu_ref, x_ref, dx2_ref, g_ref, wg_ref, wu_ref, dx1_ref, h2_ref, vec_ref):
        _zero_on_first(vec_ref)
        dh2 = _dot(dfg_ref[...], wg_ref[...]) + _dot(dfu_ref[...], wu_ref[...])
        xhat, r = _rms_stats(x_ref[...])
        h2_ref[...] = (xhat * g_ref[...]).astype(h2_ref.dtype)
        dxn, dg = _rms_bwd(dh2, xhat, r, g_ref[...])
        dx1_ref[...] = dx2_ref[...] + dxn
        vec_ref[0:1, :] += _rowsum(dg)

    return pl.pallas_call(
        body, name="ffn_bwd_b", grid=(t // tm,),
        out_shape=[jax.ShapeDtypeStruct((t, d), F32), jax.ShapeDtypeStruct((t, d), BF16),
                   jax.ShapeDtypeStruct((SUBLANES, d), F32)],
        in_specs=[_rows(tm, f), _rows(tm, f), _rows(tm, d), _rows(tm, d), _whole((1, d)), _whole((f, d)), _whole((f, d))],
        out_specs=[_rows(tm, d), _rows(tm, d), _acc((SUBLANES, d))],
        compiler_params=_params("arbitrary"),
    )(dfg, dfu, x1, dx2, g_ffn, w_gate_t, w_up_t)


def _matmul_tn(a, b, tn, name, dtype=F32):
    t, k = a.shape
    n = b.shape[1]

    def body(a_ref, b_ref, o_ref):
        o_ref[...] = _dot_tn(a_ref[...], b_ref[...]).astype(o_ref.dtype)

    return pl.pallas_call(
        body, name=name, grid=(n // tn,),
        out_shape=jax.ShapeDtypeStruct((k, n), dtype),
        in_specs=[_whole((t, k)), pl.BlockSpec((t, tn), lambda j: (0, j))],
        out_specs=pl.BlockSpec((k, tn), lambda j: (0, j)),
        compiler_params=_params("parallel"),
    )(a, b)


def _merge_bwd(dx1, merged, ma, mb, za, zb, y_a, h, w_o, w_a_out_t, w_b_out):
    t, d = dx1.shape
    sa, lw = y_a.shape[1], h.shape[1]
    tm = min(TOKEN_TILE, t)

    def body(dx1_ref, mg_ref, ma_ref, mb_ref, za_ref, zb_ref, ya_ref, h_ref, wo_ref, wa_ref, wb_ref,
             dza_ref, dzb_ref, dya_ref, dyb_ref, dwo_out, dwa_out, dwb_out, dwo_ref, dwa_ref, dwb_ref):
        _zero_on_first(dwo_ref, dwa_ref, dwb_ref)
        dx1v = dx1_ref[...].astype(BF16)
        dmg = _dot_nt(dx1v, wo_ref[...])
        ga = _sigmoid(za_ref[...])
        gb = _sigmoid(zb_ref[...])
        dza_ref[...] = dmg * ma_ref[...] * ga * (1.0 - ga)
        dzb_ref[...] = dmg * mb_ref[...] * gb * (1.0 - gb)
        dma = (dmg * ga).astype(BF16)
        dmb = (dmg * gb).astype(BF16)
        dya_ref[...] = _dot(dma, wa_ref[...])
        dyb_ref[...] = _dot_nt(dmb, wb_ref[...])
        dwo_ref[...] += _dot_tn(mg_ref[...], dx1v)
        dwa_ref[...] += _dot_tn(dma, ya_ref[...])
        dwb_ref[...] += _dot_tn(h_ref[...], dmb)
        _store_on_last([(dwo_ref, dwo_out), (dwa_ref, dwa_out), (dwb_ref, dwb_out)])

    return pl.pallas_call(
        body, name="merge_bwd", grid=(t // tm,),
        out_shape=[jax.ShapeDtypeStruct((t, d), F32), jax.ShapeDtypeStruct((t, d), F32),
                   jax.ShapeDtypeStruct((t, sa), F32), jax.ShapeDtypeStruct((t, lw), F32),
                   jax.ShapeDtypeStruct((d, d), BF16), jax.ShapeDtypeStruct((d, sa), BF16),
                   jax.ShapeDtypeStruct((lw, d), BF16)],
        in_specs=[_rows(tm, d), _rows(tm, d), _rows(tm, d), _rows(tm, d), _rows(tm, d), _rows(tm, d),
                  _rows(tm, sa), _rows(tm, lw), _whole((d, d)), _whole((d, sa)), _whole((lw, d))],
        out_specs=[_rows(tm, d), _rows(tm, d), _rows(tm, sa), _rows(tm, lw), _acc((d, d)), _acc((d, sa)), _acc((lw, d))],
        scratch_shapes=[pltpu.VMEM((d, d), F32), pltpu.VMEM((d, sa), F32), pltpu.VMEM((lw, d), F32)],
        compiler_params=_params("arbitrary"),
    )(dx1, merged, ma, mb, za, zb, y_a, h, w_o, w_a_out_t, w_b_out)


def _lru_bwd(dh, xc, hprev, u, conv_w, wr_blk, b_r, wi_blk, b_i, lru_lambda):
    t, w = dh.shape
    tc = min(TIME_CHUNK, t)
    steps = t // tc
    halo = SUBLANES
    sub_per_chunk = tc // halo

    def body(dh_ref, xc_ref, hp_ref, u_ref, uh_ref, cw_ref, wr_ref, br_ref, wi_ref, bi_ref, lam_ref,
             du_ref, dwr_ref, dwi_ref, vec_ref, lam_s, a_s, dxc_s, uext_s, carry_s):
        chunk = steps - 1 - pl.program_id(0)

        @pl.when(pl.program_id(0) == 0)
        def _():
            carry_s[...] = jnp.zeros_like(carry_s)
            dxc_s[tc:tc + halo, :] = jnp.zeros((halo, w), F32)
            dwr_ref[...] = jnp.zeros_like(dwr_ref)
            dwi_ref[...] = jnp.zeros_like(dwi_ref)
            vec_ref[...] = jnp.zeros_like(vec_ref)

        xc = xc_ref[...]
        r, ig, sp, log_a = _lru_gates(xc, wr_ref, br_ref, wi_ref, bi_ref, lam_ref)
        a = jnp.exp(log_a)
        a_s[...] = a

        def step(i, q):
            at = pl.ds(tc - 1 - i, 1)
            lam_row = dh_ref[at, :] + q
            lam_s[at, :] = lam_row
            return a_s[at, :] * lam_row

        carry_s[0:1, :] = lax.fori_loop(0, tc, step, carry_s[0:1, :], unroll=8)
        lam = lam_s[...]
        mult = jnp.sqrt(-_expm1(2.0 * log_a))
        d_log_a = lam * hp_ref[...] * a - (lam * ig * xc) * (a * a) / mult
        d_ig = lam * mult * xc
        dpre_r = (d_log_a * (-LRU_C * sp)) * r * (1.0 - r)
        dpre_i = d_ig * ig * (1.0 - ig)
        dxc = lam * mult * ig + _dot_nt(dpre_r, wr_ref[...]) + _dot_nt(dpre_i, wi_ref[...])
        dwr_ref[...] += _dot_tn(xc, dpre_r)
        dwi_ref[...] += _dot_tn(xc, dpre_i)
        vec_ref[0:1, :] += _rowsum(dxc)
        vec_ref[1:2, :] += _rowsum(dpre_r)
        vec_ref[2:3, :] += _rowsum(dpre_i)
        vec_ref[3:4, :] += _rowsum(d_log_a * (-LRU_C * r)) * (-_sigmoid(-lam_ref[...]))
        dxc_s[0:tc, :] = dxc
        du = cw_ref[CONV_WIDTH - 1:CONV_WIDTH, :] * dxc
        for k in range(CONV_WIDTH - 1):
            off = CONV_WIDTH - 1 - k
            du = du + cw_ref[k:k + 1, :] * dxc_s[off:off + tc, :]
        du_ref[...] = du
        dxc_s[tc:tc + halo, :] = dxc_s[0:halo, :]
        uext_s[0:halo, :] = jnp.where(chunk > 0, uh_ref[...], 0.0)
        uext_s[halo:halo + tc, :] = u_ref[...]
        for k in range(CONV_WIDTH):
            off = halo - (CONV_WIDTH - 1) + k
            vec_ref[4 + k:5 + k, :] += _rowsum(dxc * uext_s[off:off + tc, :])

    halo_spec = pl.BlockSpec((halo, w), lambda i: (jnp.maximum((steps - 1 - i) * sub_per_chunk - 1, 0), 0))
    return pl.pallas_call(
        body, name="lru_bwd", grid=(steps,),
        out_shape=[jax.ShapeDtypeStruct((t, w), F32), jax.ShapeDtypeStruct((w, w), F32),
                   jax.ShapeDtypeStruct((w, w), F32), jax.ShapeDtypeStruct((SUBLANES, w), F32)],
        in_specs=[_rows_rev(tc, w, steps)] * 4 + [halo_spec, _whole((CONV_WIDTH, w)), _whole((w, w)), _whole((1, w)),
                                                  _whole((w, w)), _whole((1, w)), _whole((1, w))],
        out_specs=[_rows_rev(tc, w, steps), _acc((w, w)), _acc((w, w)), _acc((SUBLANES, w))],
        scratch_shapes=[pltpu.VMEM((tc, w), F32), pltpu.VMEM((tc, w), F32), pltpu.VMEM((tc + halo, w), F32),
                        pltpu.VMEM((halo + tc, w), F32), pltpu.VMEM((SUBLANES, w), F32)],
        compiler_params=_params("arbitrary"),
    )(dh, xc, hprev, u, u, conv_w, wr_blk, b_r, wi_blk, b_i, lru_lambda)


def _s5_bwd(dya, y, sr, si, u, w_glu, b_glu, cre_blk, cimn_blk, bbr_blk, bbi_blk, ar, ai, d_skip):
    t, sa = dya.shape
    gn = sr.shape[1]
    tc = min(TIME_CHUNK, t)
    steps = t // tc
    halo = SUBLANES

    def body(dya_ref, y_ref, sr_ref, si_ref, u_ref, wg_ref, bg_ref, cre_ref, cim_ref, bbr_ref, bbi_ref,
             ar_ref, ai_ref, d_ref, du_ref, lr_ref, li_ref, dy_ref, dwg_out, vsa_ref, vgn_ref, gr_s, gi_s, cr_s, ci_s,
             dwg_ref):
        @pl.when(pl.program_id(0) == 0)
        def _():
            cr_s[...] = jnp.zeros_like(cr_s)
            ci_s[...] = jnp.zeros_like(ci_s)
            gr_s[tc:tc + halo, :] = jnp.zeros((halo, gn), F32)
            gi_s[tc:tc + halo, :] = jnp.zeros((halo, gn), F32)
            dwg_ref[...] = jnp.zeros_like(dwg_ref)
            vsa_ref[...] = jnp.zeros_like(vsa_ref)
            vgn_ref[...] = jnp.zeros_like(vgn_ref)

        yv = y_ref[...]
        uv = u_ref[...]
        zz = _gelu(yv)
        sg = _sigmoid(_dot(zz, wg_ref[...]) + bg_ref[...])
        dyav = dya_ref[...]
        dq = dyav * zz * sg * (1.0 - sg)
        dzz = dyav * sg + _dot_nt(dq, wg_ref[...])
        dwg_ref[...] += _dot_tn(zz, dq)
        dy = dzz * _gelu_grad(yv)
        dyb = dy.astype(BF16)
        dy_ref[...] = dyb.astype(dy_ref.dtype)
        vsa_ref[0:1, :] += _rowsum(dq)
        vsa_ref[1:2, :] += _rowsum(dy * uv)
        gr_s[0:tc, :] = _dot_nt(dyb, cre_ref[...])
        gi_s[0:tc, :] = _dot_nt(dyb, cim_ref[...])
        a_r = ar_ref[...]
        a_i = ai_ref[...]

        def step(i, carry):
            l_r, l_i = carry
            at = pl.ds(tc - 1 - i, 1)
            n_r = gr_s[at, :] + a_r * l_r + a_i * l_i
            n_i = gi_s[at, :] + a_r * l_i - a_i * l_r
            gr_s[at, :] = n_r
            gi_s[at, :] = n_i
            return n_r, n_i

        l_r, l_i = lax.fori_loop(0, tc, step, (cr_s[0:1, :], ci_s[0:1, :]), unroll=8)
        cr_s[0:1, :] = l_r
        ci_s[0:1, :] = l_i
        nxt_r = gr_s[1:tc + 1, :]
        nxt_i = gi_s[1:tc + 1, :]
        srv = sr_ref[...]
        siv = si_ref[...]
        vgn_ref[0:1, :] += _rowsum(nxt_r * srv + nxt_i * siv)
        vgn_ref[1:2, :] += _rowsum(nxt_i * srv - nxt_r * siv)
        lam_r = gr_s[0:tc, :]
        lam_i = gi_s[0:tc, :]
        gr_s[tc:tc + halo, :] = gr_s[0:halo, :]
        gi_s[tc:tc + halo, :] = gi_s[0:halo, :]
        lrb = lam_r.astype(BF16)
        lib = lam_i.astype(BF16)
        lr_ref[...] = lrb.astype(lr_ref.dtype)
        li_ref[...] = lib.astype(li_ref.dtype)
        du_ref[...] = _dot_nt(lrb, bbr_ref[...]) + _dot_nt(lib, bbi_ref[...]) + dy * d_ref[...]
        _store_on_last([(dwg_ref, dwg_out)])

    return pl.pallas_call(
        body, name="s5_bwd", grid=(steps,),
        out_shape=[jax.ShapeDtypeStruct((t, sa), F32), jax.ShapeDtypeStruct((t, gn), BF16),
                   jax.ShapeDtypeStruct((t, gn), BF16), jax.ShapeDtypeStruct((t, sa), BF16),
                   jax.ShapeDtypeStruct((sa, sa), BF16), jax.ShapeDtypeStruct((SUBLANES, sa), F32),
                   jax.ShapeDtypeStruct((SUBLANES, gn), F32)],
        in_specs=[_rows_rev(tc, sa, steps), _rows_rev(tc, sa, steps), _rows_rev(tc, gn, steps), _rows_rev(tc, gn, steps),
                  _rows_rev(tc, sa, steps), _whole((sa, sa)), _whole((1, sa)), _whole((gn, sa)), _whole((gn, sa)),
                  _whole((sa, gn)), _whole((sa, gn)), _whole((1, gn)), _whole((1, gn)), _whole((1, sa))],
        out_specs=[_rows_rev(tc, sa, steps), _rows_rev(tc, gn, steps), _rows_rev(tc, gn, steps), _rows_rev(tc, sa, steps),
                   _acc((sa, sa)), _acc((SUBLANES, sa)), _acc((SUBLANES, gn))],
        scratch_shapes=[pltpu.VMEM((tc + halo, gn), F32), pltpu.VMEM((tc + halo, gn), F32),
                        pltpu.VMEM((SUBLANES, gn), F32), pltpu.VMEM((SUBLANES, gn), F32), pltpu.VMEM((sa, sa), F32)],
        compiler_params=_params("arbitrary"),
    )(dya, y, sr, si, u, w_glu, b_glu, cre_blk, cimn_blk, bbr_blk, bbi_blk, ar, ai, d_skip)


def _inproj_bwd(dparts, x, dx1, g_mix, w_in_t):
    t, d = x.shape
    n = w_in_t.shape[0]
    widths = [p.shape[1] for p in dparts]
    offs = [sum(widths[:i]) for i in range(len(widths) + 1)]
    tm = min(TOKEN_TILE, t)
    np_ = len(dparts)

    def body(*refs):
        dz_refs = refs[:np_]
        x_ref, dx1_ref, g_ref, w_ref, gx_ref, h_ref, dz_ref, vd_ref, vn_ref = refs[np_:]
        _zero_on_first(vd_ref, vn_ref)
        dh = jnp.zeros((tm, d), F32)
        for k, r in enumerate(dz_refs):
            lo, hi = offs[k], offs[k + 1]
            dzk = r[...]
            dh = dh + _dot(dzk, w_ref[lo:hi, :])
            dz_ref[:, lo:hi] = dzk.astype(dz_ref.dtype)
            vn_ref[0:1, lo:hi] += _rowsum(dzk)
        xhat, r0 = _rms_stats(x_ref[...])
        h_ref[...] = (xhat * g_ref[...]).astype(h_ref.dtype)
        dxn, dg = _rms_bwd(dh, xhat, r0, g_ref[...])
        gx_ref[...] = dx1_ref[...] + dxn
        vd_ref[0:1, :] += _rowsum(dg)

    return pl.pallas_call(
        body, name="inproj_bwd", grid=(t // tm,),
        out_shape=[jax.ShapeDtypeStruct((t, d), F32), jax.ShapeDtypeStruct((t, d), BF16),
                   jax.ShapeDtypeStruct((t, n), BF16), jax.ShapeDtypeStruct((SUBLANES, d), F32),
                   jax.ShapeDtypeStruct((SUBLANES, n), F32)],
        in_specs=[_rows(tm, w) for w in widths] + [_rows(tm, d), _rows(tm, d), _whole((1, d)), _whole((n, d))],
        out_specs=[_rows(tm, d), _rows(tm, d), _rows(tm, n), _acc((SUBLANES, d)), _acc((SUBLANES, n))],
        compiler_params=_params("arbitrary"),
    )(*dparts, x, dx1, g_mix, w_in_t)


def _adamw_math(w, g, m, v):
    m_new = ADAM_B1 * m + (1.0 - ADAM_B1) * g
    v_new = ADAM_B2 * v + (1.0 - ADAM_B2) * (g * g)
    m_hat = m_new / (1.0 - ADAM_B1 ** ADAM_STEP)
    v_hat = v_new / (1.0 - ADAM_B2 ** ADAM_STEP)
    delta = -ADAM_LR * (m_hat / (jnp.sqrt(v_hat) + ADAM_EPS) + ADAM_WD * w)
    return delta, m_new, v_new


def _row_tile(rows):
    for cand in (256, 128, 64, 32, 16, 8):
        if rows % cand == 0:
            return cand
    return rows


def _adamw(parts, w, m, v, name, transposed=False):
    rows, cols = w.shape
    npart = parts.shape[0]
    tr = _row_tile(rows)
    if transposed:
        parts_spec = pl.BlockSpec((npart, cols, tr), lambda i: (0, 0, i))
    else:
        parts_spec = pl.BlockSpec((npart, tr, cols), lambda i: (0, i, 0))

    def body(p_ref, w_ref, m_ref, v_ref, g_ref, d_ref, mo_ref, vo_ref):
        g = p_ref[0].astype(F32)
        for k in range(1, npart):
            g = g + p_ref[k].astype(F32)
        if transposed:
            g = g.T
        delta, m_new, v_new = _adamw_math(w_ref[...], g, m_ref[...], v_ref[...])
        g_ref[...] = g
        d_ref[...] = delta
        mo_ref[...] = m_new
        vo_ref[...] = v_new

    return pl.pallas_call(
        body, name=name, grid=(rows // tr,),
        out_shape=[jax.ShapeDtypeStruct((rows, cols), F32)] * 4,
        in_specs=[parts_spec] + [_rows(tr, cols)] * 3,
        out_specs=[_rows(tr, cols)] * 4,
        compiler_params=_params("parallel"),
    )(parts, w, m, v)


def _sum_parts(parts, name):
    npart, rows, cols = parts.shape
    tr = _row_tile(rows)

    def body(p_ref, o_ref):
        g = p_ref[0].astype(F32)
        for k in range(1, npart):
            g = g + p_ref[k].astype(F32)
        o_ref[...] = g

    return pl.pallas_call(
        body, name=name, grid=(rows // tr,),
        out_shape=jax.ShapeDtypeStruct((rows, cols), F32),
        in_specs=[pl.BlockSpec((npart, tr, cols), lambda i: (0, i, 0))],
        out_specs=_rows(tr, cols),
        compiler_params=_params("parallel"),
    )(parts)


def _mesh_position():
    return lax.axis_index("x"), lax.axis_index("y"), lax.axis_index("c")


def _flip(pos, rel):
    x, y, c = pos
    return (1 - x if rel & 4 else x, 1 - y if rel & 2 else y, 1 - c if rel & 1 else c)


def _index(pos):
    return 4 * pos[0] + 2 * pos[1] + pos[2]


_ANY = pl.BlockSpec(memory_space=pl.ANY)


def _all_gather(shards, name):
    n = len(shards)
    chips = (4, 2, 6)

    def body(*refs):
        ins, outs = refs[:n], refs[n:2 * n]
        send_sems, recv_sems, local_sems = refs[2 * n:]
        me = _mesh_position()
        sibling = _flip(me, 1)

        def copy(i, k, block, to, src=None):
            dst = outs[i].at[_index(block)]
            return pltpu.make_async_remote_copy(
                src_ref=dst if src is None else src, dst_ref=dst,
                send_sem=send_sems.at[i, k], recv_sem=recv_sems.at[i, k], device_id=to, device_id_type=MESH)

        mine, first, passed = [], [], []
        for i in range(n):
            cp = pltpu.make_async_copy(ins[i], outs[i].at[_index(me)], local_sems.at[i])
            cp.start()
            mine.append(cp)
            first.append(copy(i, 0, me, sibling, src=ins[i]))
            first += [copy(i, 1 + j, me, _flip(me, rel), src=ins[i]) for j, rel in enumerate(chips)]
        for cp in first:
            cp.start()
        for j, rel in enumerate(chips):
            for i in range(n):
                copy(i, 1 + j, _flip(me, rel), me).wait_recv()
                fwd = copy(i, 4 + j, _flip(me, rel), sibling)
                fwd.start()
                passed.append(fwd)
        for i in range(n):
            copy(i, 0, sibling, me).wait_recv()
            for j, rel in enumerate(chips):
                copy(i, 4 + j, _flip(sibling, rel), me).wait_recv()
        for cp in first + passed:
            cp.wait_send()
        for cp in mine:
            cp.wait()

    return pl.pallas_call(
        body, name=name,
        out_shape=[jax.ShapeDtypeStruct((N_DEV,) + s.shape, s.dtype) for s in shards],
        in_specs=[_ANY] * n, out_specs=[_ANY] * n,
        scratch_shapes=[pltpu.SemaphoreType.DMA((n, 7)), pltpu.SemaphoreType.DMA((n, 7)), pltpu.SemaphoreType.DMA((n,))],
    )(*shards)


def _exchange_blocks(blocked, name):
    n = len(blocked)

    def body(*refs):
        ins, outs = refs[:n], refs[n:2 * n]
        send_sems, recv_sems, local_sems = refs[2 * n:]
        me = _mesh_position()
        sends, mine = [], []
        for i in range(n):
            cp = pltpu.make_async_copy(ins[i].at[_index(me)], outs[i].at[_index(me)], local_sems.at[i])
            cp.start()
            mine.append(cp)
            for rel in range(1, N_DEV):
                peer = _flip(me, rel)
                cp = pltpu.make_async_remote_copy(
                    src_ref=ins[i].at[_index(peer)], dst_ref=outs[i].at[_index(me)],
                    send_sem=send_sems.at[i, rel - 1], recv_sem=recv_sems.at[i, rel - 1], device_id=peer, device_id_type=MESH)
                cp.start()
                sends.append(cp)
        for i in range(n):
            for rel in range(1, N_DEV):
                peer = _flip(me, rel)
                pltpu.make_async_remote_copy(
                    src_ref=ins[i].at[_index(me)], dst_ref=outs[i].at[_index(peer)],
                    send_sem=send_sems.at[i, rel - 1], recv_sem=recv_sems.at[i, rel - 1], device_id=peer,
                    device_id_type=MESH).wait_recv()
        for cp in sends:
            cp.wait_send()
        for cp in mine:
            cp.wait()

    return pl.pallas_call(
        body, name=name,
        out_shape=[jax.ShapeDtypeStruct(b.shape, b.dtype) for b in blocked],
        in_specs=[_ANY] * n, out_specs=[_ANY] * n,
        scratch_shapes=[pltpu.SemaphoreType.DMA((n, 7)), pltpu.SemaphoreType.DMA((n, 7)), pltpu.SemaphoreType.DMA((n,))],
    )(*blocked)


SHARDED = {"w_in": 1, "w_glu": 0, "conv_w": 1, "w_a_out": 1, "w_b_out": 0, "w_o": 0, "w_ffn_gate": 1, "w_ffn_up": 1,
           "w_ffn_down": 0, "w_ple_gate": 0, "w_ple": 1}
TRANSPOSED = ("w_in", "w_a_out", "w_ffn_gate", "w_ffn_up", "w_ple")
SMALL = ["g_mix", "b_in", "lam_re", "lam_im", "log_dt", "s5_b_re", "s5_b_im", "s5_c_re", "s5_c_im", "s5_d", "b_glu",
         "conv_b", "w_r", "b_r", "w_i", "b_i", "lru_lambda", "g_ffn", "g_ple_gate", "b_ple_gate", "g_ple", "g_final"]
WEIGHTS = ["g_mix", "w_in", "b_in", "lam_re", "lam_im", "log_dt", "s5_b_re", "s5_b_im", "s5_c_re", "s5_c_im", "s5_d",
           "w_glu", "b_glu", "conv_w", "conv_b", "w_r", "b_r", "w_i", "b_i", "lru_lambda", "w_a_out", "w_b_out", "w_o",
           "g_ffn", "w_ffn_gate", "w_ffn_up", "w_ffn_down", "g_ple_gate", "w_ple_gate", "b_ple_gate", "w_ple", "g_ple",
           "g_final"]


def _unblock(gathered, axis):
    nb, r, c = gathered.shape
    if axis == 0:
        return gathered.reshape(nb * r, c)
    return jnp.transpose(gathered, (1, 0, 2)).reshape(r, nb * c)


def _block(full, axis):
    r, c = full.shape
    if axis == 0:
        return full.reshape(N_DEV, r // N_DEV, c)
    return jnp.transpose(full.reshape(r, N_DEV, c // N_DEV), (1, 0, 2))


def _block_diag(w):
    h, i, j = w.shape
    eye = jnp.eye(h, dtype=w.dtype)
    return (w[:, :, None, :] * eye[:, None, :, None]).reshape(h * i, h * j)


def _diag_blocks(dense, h):
    hi, hj = dense.shape
    i, j = hi // h, hj // h
    return jnp.einsum("hihj->hij", dense.reshape(h, i, h, j))


def _s5_discretise(lam_re, lam_im, log_dt, b_re, b_im):
    dt = jnp.exp(log_dt)[:, None]
    mag = jnp.exp(lam_re * dt)
    ar = mag * jnp.cos(lam_im * dt)
    ai = mag * jnp.sin(lam_im * dt)
    den = lam_re * lam_re + lam_im * lam_im
    nr = ar - 1.0
    fr = (nr * lam_re + ai * lam_im) / den
    fi = (ai * lam_re - nr * lam_im) / den
    bbr = fr[..., None] * b_re - fi[..., None] * b_im
    bbi = fr[..., None] * b_im + fi[..., None] * b_re
    return ar, ai, bbr, bbi


def _pack(arrs, rows):
    flat = jnp.concatenate([a.reshape(-1).astype(F32) for a in arrs])
    return jnp.pad(flat, (0, rows * PACK_COLS - flat.shape[0])).reshape(rows, PACK_COLS)


def _unpack(packed, shapes):
    flat = packed.reshape(-1)
    out, off = [], 0
    for s in shapes:
        size = math.prod(s)
        out.append(flat[off:off + size].reshape(s))
        off += size
    return out


def _local_step(x, p, target, full, small):
    d = x.shape[1]
    g, n, pch = small["s5_b_re"].shape
    heads = small["w_r"].shape[0]
    sa, lw = g * pch, small["lru_lambda"].shape[-1]
    widths = [sa, lw, d, d]
    row = lambda v: v.reshape(1, -1)

    (ar, ai, bbr, bbi), disc_vjp = jax.vjp(_s5_discretise, small["lam_re"], small["lam_im"], small["log_dt"],
                                           small["s5_b_re"], small["s5_b_im"])
    bbr_blk = _block_diag(jnp.transpose(bbr, (0, 2, 1))).astype(BF16)
    bbi_blk = _block_diag(jnp.transpose(bbi, (0, 2, 1))).astype(BF16)
    cre_blk = _block_diag(jnp.transpose(small["s5_c_re"], (0, 2, 1))).astype(BF16)
    cimn_blk = _block_diag(jnp.transpose(-small["s5_c_im"], (0, 2, 1))).astype(BF16)
    wr_blk = _block_diag(small["w_r"]).astype(BF16)
    wi_blk = _block_diag(small["w_i"]).astype(BF16)
    ar_row, ai_row = row(ar), row(ai)
    d_row = row(small["s5_d"])
    conv_w = full["conv_w"].astype(F32)

    u_a, u_b, za, zb = _inproj_fwd(x, row(small["g_mix"]), full["w_in"], row(small["b_in"]), widths)
    sr, si, y, y_a = _s5_fwd(u_a, bbr_blk, bbi_blk, ar_row, ai_row, cre_blk, cimn_blk, d_row, full["w_glu"],
                             row(small["b_glu"]))
    xc, h, hprev = _lru_fwd(u_b, conv_w, row(small["conv_b"]), wr_blk, row(small["b_r"]), wi_blk, row(small["b_i"]),
                            row(small["lru_lambda"]))
    x1, merged, ma, mb = _merge_fwd(y_a, h, za, zb, x, full["w_a_out"], full["w_b_out"], full["w_o"])
    fg, fu = _ffn_up_fwd(x1, row(small["g_ffn"]), full["w_ffn_gate"], full["w_ffn_up"])
    x2 = _ffn_down_fwd(fg, fu, x1, full["w_ffn_down"])

    dx2, loss_blk, dw_pg, dw_ple, vec_tail = _tail_fwd_bwd(
        x2, p, target, row(small["g_ple_gate"]), full["w_ple_gate"], row(small["b_ple_gate"]), full["w_ple"],
        row(small["g_ple"]), row(small["g_final"]))
    dfg, dfu, act = _ffn_bwd_a(dx2, fg, fu, full["w_ffn_down"])
    dx1, h2, vec_ffn = _ffn_bwd_b(dfg, dfu, x1, dx2, row(small["g_ffn"]), full["w_ffn_gate"], full["w_ffn_up"])
    tn_d = min(d, 512)
    dw_down = _matmul_tn(act, dx2, tn_d, "dw_ffn_down", BF16)
    dw_gate = _matmul_tn(dfg, h2, tn_d, "dw_ffn_gate", BF16)
    dw_up = _matmul_tn(dfu, h2, tn_d, "dw_ffn_up", BF16)
    dza, dzb, dya, dyb, dw_o, dw_a, dw_b = _merge_bwd(dx1, merged, ma, mb, za, zb, y_a, h, full["w_o"], full["w_a_out"],
                                                       full["w_b_out"])
    du_b, dwr_dense, dwi_dense, vec_lru = _lru_bwd(dyb, xc, hprev, u_b, conv_w, wr_blk, row(small["b_r"]), wi_blk,
                                                   row(small["b_i"]), row(small["lru_lambda"]))
    du_a, lam_r, lam_i, dy16, dw_glu, vec_sa, vec_gn = _s5_bwd(
        dya, y, sr, si, u_a, full["w_glu"], row(small["b_glu"]), cre_blk, cimn_blk, bbr_blk, bbi_blk, ar_row, ai_row, d_row)
    tn_s = min(sa, 512)
    dbbr_dense = _matmul_tn(lam_r, u_a, tn_s, "dbb_re")
    dbbi_dense = _matmul_tn(lam_i, u_a, tn_s, "dbb_im")
    tn_n = min(sr.shape[1], 512)
    dcre_dense = _matmul_tn(dy16, sr, tn_n, "dc_re")
    dcimn_dense = _matmul_tn(dy16, si, tn_n, "dc_im")
    grad_x, h0, dz16, vec_mix, vec_bin = _inproj_bwd([du_a, du_b, dza, dzb], x, dx1, row(small["g_mix"]), full["w_in"])
    dw_in = _matmul_tn(dz16, h0, tn_d, "dw_in", BF16)

    d_bbr = _diag_blocks(dbbr_dense, g)
    d_bbi = _diag_blocks(dbbi_dense, g)
    d_cre = _diag_blocks(dcre_dense, g)
    d_cim = -_diag_blocks(dcimn_dense, g)
    pre = {"d_ar": vec_gn[0].reshape(g, n), "d_ai": vec_gn[1].reshape(g, n), "d_bbr": d_bbr, "d_bbi": d_bbi}
    gs = {
        "g_mix": vec_mix[0], "b_in": vec_bin[0], "s5_c_re": d_cre, "s5_c_im": d_cim, "s5_d": vec_sa[1].reshape(g, pch),
        "b_glu": vec_sa[0], "conv_b": vec_lru[0], "w_r": _diag_blocks(dwr_dense, heads), "b_r": vec_lru[1].reshape(heads, -1),
        "w_i": _diag_blocks(dwi_dense, heads), "b_i": vec_lru[2].reshape(heads, -1), "lru_lambda": vec_lru[3],
        "g_ffn": vec_ffn[0], "g_ple_gate": vec_tail[1], "b_ple_gate": vec_tail[0], "g_ple": vec_tail[2],
        "g_final": vec_tail[3],
    }
    gfull = {"w_in": dw_in, "w_glu": dw_glu, "conv_w": vec_lru[4:4 + CONV_WIDTH], "w_a_out": dw_a, "w_b_out": dw_b,
             "w_o": dw_o, "w_ffn_gate": dw_gate, "w_ffn_up": dw_up, "w_ffn_down": dw_down, "w_ple_gate": dw_pg,
             "w_ple": dw_ple}
    return loss_blk[0, 0], grad_x, gfull, gs, pre, disc_vjp


_PRE_KEYS = ["d_ar", "d_ai", "d_bbr", "d_bbi"]
_DIRECT_SMALL = [k for k in SMALL if k not in ("lam_re", "lam_im", "log_dt", "s5_b_re", "s5_b_im")]


def kernel(x, p, g_mix, w_in, b_in, lam_re, lam_im, log_dt, s5_b_re, s5_b_im, s5_c_re, s5_c_im, s5_d, w_glu, b_glu, conv_w, conv_b, w_r, b_r, w_i, b_i, lru_lambda, w_a_out, w_b_out, w_o, g_ffn, w_ffn_gate, w_ffn_up, w_ffn_down, g_ple_gate, w_ple_gate, b_ple_gate, w_ple, g_ple, g_final, loss_target, m_g_mix, m_w_in, m_b_in, m_lam_re, m_lam_im, m_log_dt, m_s5_b_re, m_s5_b_im, m_s5_c_re, m_s5_c_im, m_s5_d, m_w_glu, m_b_glu, m_conv_w, m_conv_b, m_w_r, m_b_r, m_w_i, m_b_i, m_lru_lambda, m_w_a_out, m_w_b_out, m_w_o, m_g_ffn, m_w_ffn_gate, m_w_ffn_up, m_w_ffn_down, m_g_ple_gate, m_w_ple_gate, m_b_ple_gate, m_w_ple, m_g_ple, m_g_final, v_g_mix, v_w_in, v_b_in, v_lam_re, v_lam_im, v_log_dt, v_s5_b_re, v_s5_b_im, v_s5_c_re, v_s5_c_im, v_s5_d, v_w_glu, v_b_glu, v_conv_w, v_conv_b, v_w_r, v_b_r, v_w_i, v_b_i, v_lru_lambda, v_w_a_out, v_w_b_out, v_w_o, v_g_ffn, v_w_ffn_gate, v_w_ffn_up, v_w_ffn_down, v_g_ple_gate, v_w_ple_gate, v_b_ple_gate, v_w_ple, v_g_ple, v_g_final):
    given = dict(locals())
    wts = {k: given[k] for k in WEIGHTS}
    moms = {k: given["m_" + k] for k in WEIGHTS}
    vels = {k: given["v_" + k] for k in WEIGHTS}

    def drop_depth(k, a):
        return a if k == "g_final" else a[0]

    small = {k: drop_depth(k, wts[k]) for k in SMALL}
    shard = {k: wts[k][0] for k in SHARDED}
    names = list(SHARDED)

    def wire(k):
        if k == "conv_w":
            return shard[k]
        return (shard[k].T if k in TRANSPOSED else shard[k]).astype(BF16)

    gathered = _all_gather([wire(k) for k in names], "gather_weights")
    full = {k: _unblock(gw, 1 if k == "conv_w" else 0) for k, gw in zip(names, gathered)}

    loss_part, grad_x, gfull, gs, pre, disc_vjp = _local_step(x[0], p[0, 0], loss_target[0], full, small)

    pack_shapes = [pre[k].shape for k in _PRE_KEYS] + [small[k].shape for k in _DIRECT_SMALL] + [(1,)]
    pack_size = sum(math.prod(s) for s in pack_shapes)
    slice_rows = -(-pack_size // (N_DEV * PACK_COLS * SUBLANES)) * SUBLANES
    packed = _pack([pre[k] for k in _PRE_KEYS] + [gs[k] for k in _DIRECT_SMALL] + [loss_part], N_DEV * slice_rows)
    blocked = [_block(gfull[k], 1 if k == "conv_w" else 0) for k in names]
    blocked.append(packed.reshape(N_DEV, slice_rows, PACK_COLS))
    received = _exchange_blocks(blocked, "exchange_grads")

    my_slice = _sum_parts(received[-1], "reduce_small")
    (reduced,) = _all_gather([my_slice], "gather_small")
    red = _unpack(reduced.reshape(N_DEV * slice_rows, PACK_COLS), pack_shapes)
    red_pre = dict(zip(_PRE_KEYS, red[:len(_PRE_KEYS)]))
    g_small = dict(zip(_DIRECT_SMALL, red[len(_PRE_KEYS):-1]))
    loss = red[-1].reshape(())
    d_lr, d_li, d_ldt, d_bre, d_bim = disc_vjp((red_pre["d_ar"], red_pre["d_ai"], red_pre["d_bbr"], red_pre["d_bbi"]))
    g_small.update(lam_re=d_lr, lam_im=d_li, log_dt=d_ldt, s5_b_re=d_bre, s5_b_im=d_bim)
    small_shapes = [small[k].shape for k in SMALL]
    small_rows = -(-sum(math.prod(s) for s in small_shapes) // (PACK_COLS * SUBLANES)) * SUBLANES
    g_pk = _pack([g_small[k] for k in SMALL], small_rows)
    w_pk = _pack([small[k] for k in SMALL], small_rows)
    m_pk = _pack([drop_depth(k, moms[k]) for k in SMALL], small_rows)
    v_pk = _pack([drop_depth(k, vels[k]) for k in SMALL], small_rows)
    small_out = _adamw(g_pk[None], w_pk, m_pk, v_pk, "adamw_small")
    small_out = [dict(zip(SMALL, _unpack(o, small_shapes))) for o in small_out]

    big_out = {}
    for k, parts in zip(names, received[:-1]):
        big_out[k] = _adamw(parts, shard[k], moms[k][0], vels[k][0], "adamw_" + k, transposed=k in TRANSPOSED)

    outs = [loss, grad_x[None]]
    for slot in range(4):
        for k in WEIGHTS:
            if k in SHARDED:
                outs.append(big_out[k][slot][None])
            else:
                a = small_out[slot][k]
                outs.append(a if k == "g_final" else a[None])
    return tuple(outs)
```

```python
import functools
import math

import jax
import jax.numpy as jnp
from jax import lax
from jax.experimental import pallas as pl
from jax.experimental.pallas import tpu as pltpu

F32 = jnp.float32
BF16 = jnp.bfloat16

EPS = 1e-6
LRU_C = 8.0
CONV_WIDTH = 4
ADAM_LR = 0.001
ADAM_B1 = 0.9
ADAM_B2 = 0.999
ADAM_EPS = 1e-08
ADAM_WD = 0.01
ADAM_STEP = 10

N_DEV = 8
MESH = pl.DeviceIdType.MESH
SUBLANES = 8
LANES = 128
VMEM_LIMIT = 56 * 1024 * 1024
TOKEN_TILE = 256
TIME_CHUNK = 256
PACK_COLS = 1024


def _dot(a, b):
    return jnp.dot(a.astype(BF16), b.astype(BF16), preferred_element_type=F32)


def _dot_nt(a, b):
    return lax.dot_general(a.astype(BF16), b.astype(BF16), (((1,), (1,)), ((), ())), preferred_element_type=F32)


def _dot_tn(a, b):
    return lax.dot_general(a.astype(BF16), b.astype(BF16), (((0,), (0,)), ((), ())), preferred_element_type=F32)


def _sigmoid(x):
    return jax.nn.sigmoid(x)


def _rms_stats(x):
    r = lax.rsqrt(jnp.mean(x * x, axis=-1, keepdims=True) + EPS)
    return x * r, r


def _rms_bwd(dy, xhat, r, g):
    dxn = dy * g
    dx = r * (dxn - xhat * jnp.mean(dxn * xhat, axis=-1, keepdims=True))
    return dx, dy * xhat


def _rowsum(v):
    return jnp.sum(v, axis=0, keepdims=True)


def _expm1(x):
    u = jnp.exp(x)
    um1 = u - 1.0
    safe = jnp.where(um1 == 0.0, 1.0, jnp.log(u))
    return jnp.where(um1 == 0.0, x, um1 * x / safe)


def _softplus(x):
    e = jnp.exp(-jnp.abs(x))
    u = 1.0 + e
    um1 = u - 1.0
    safe = jnp.where(um1 == 0.0, 1.0, um1)
    log1p_e = jnp.where(um1 == 0.0, e, jnp.log(u) * e / safe)
    return jnp.maximum(x, 0.0) + log1p_e


_GELU_K = math.sqrt(2.0 / math.pi)
_GELU_C = 0.044715


def _gelu(x):
    return 0.5 * x * (1.0 + jnp.tanh(_GELU_K * (x + _GELU_C * x * x * x)))


def _gelu_grad(x):
    th = jnp.tanh(_GELU_K * (x + _GELU_C * x * x * x))
    return 0.5 * (1.0 + th) + 0.5 * x * (1.0 - th * th) * _GELU_K * (1.0 + 3.0 * _GELU_C * x * x)


def _params(*sem):
    return pltpu.CompilerParams(dimension_semantics=sem, vmem_limit_bytes=VMEM_LIMIT)


def _rows(tm, n):
    return pl.BlockSpec((tm, n), lambda i: (i, 0))


def _rows_rev(tm, n, steps):
    return pl.BlockSpec((tm, n), lambda i: (steps - 1 - i, 0))


def _whole(shape):
    nd = len(shape)
    return pl.BlockSpec(shape, lambda i: (0,) * nd, pipeline_mode=pl.Buffered(1))


def _acc(shape):
    nd = len(shape)
    return pl.BlockSpec(shape, lambda i: (0,) * nd)


def _zero_on_first(*refs):
    @pl.when(pl.program_id(0) == 0)
    def _():
        for r in refs:
            r[...] = jnp.zeros_like(r)


def _inproj_fwd(x, g_mix, w_in_t, b_in, widths, comm=None):
    t, d = x.shape
    n = w_in_t.shape[0]
    tm = min(TOKEN_TILE, t)
    offs = [sum(widths[:i]) for i in range(len(widths) + 1)]

    def body(x_ref, g_ref, w_ref, b_ref, *outs):
        xhat, _ = _rms_stats(x_ref[...])
        h = (xhat * g_ref[...]).astype(BF16)
        for k, o_ref in enumerate(outs):
            lo, hi = offs[k], offs[k + 1]
            o_ref[...] = _dot_nt(h, w_ref[lo:hi, :]) + b_ref[:, lo:hi]

    return _run(
        body, comm, name="inproj_fwd", grid=(t // tm,),
        out_shape=[jax.ShapeDtypeStruct((t, w), F32) for w in widths],
        in_specs=[_rows(tm, d), _whole((1, d)), _whole((n, d)), _whole((1, n))],
        out_specs=[_rows(tm, w) for w in widths],
        semantics="parallel",
    )(x, g_mix, w_in_t, b_in)


def _s5_fwd(u, bbr_blk, bbi_blk, ar, ai, cre_blk, cimn_blk, d_skip, w_glu, b_glu, comm=None):
    t, sa = u.shape
    gn = bbr_blk.shape[1]
    tc = min(TIME_CHUNK, t)

    def body(u_ref, bbr_ref, bbi_ref, ar_ref, ai_ref, cre_ref, cim_ref, d_ref, wg_ref, bg_ref,
             sr_ref, si_ref, y_ref, ya_ref, cr_s, ci_s):
        _zero_on_first(cr_s, ci_s)
        uv = u_ref[...]
        ub = uv.astype(BF16)
        sr_ref[...] = jnp.dot(ub, bbr_ref[...], preferred_element_type=F32)
        si_ref[...] = jnp.dot(ub, bbi_ref[...], preferred_element_type=F32)
        a_r = ar_ref[...]
        a_i = ai_ref[...]

        def step(row, carry):
            c_r, c_i = carry
            at = pl.ds(row, 1)
            n_r = a_r * c_r - a_i * c_i + sr_ref[at, :]
            n_i = a_r * c_i + a_i * c_r + si_ref[at, :]
            sr_ref[at, :] = n_r
            si_ref[at, :] = n_i
            return n_r, n_i

        c_r, c_i = lax.fori_loop(0, tc, step, (cr_s[0:1, :], ci_s[0:1, :]), unroll=8)
        cr_s[0:1, :] = c_r
        ci_s[0:1, :] = c_i
        y = _dot(sr_ref[...], cre_ref[...]) + _dot(si_ref[...], cim_ref[...]) + d_ref[...] * uv
        y_ref[...] = y
        zz = _gelu(y)
        q = _dot(zz, wg_ref[...]) + bg_ref[...]
        ya_ref[...] = zz * _sigmoid(q)

    return _run(
        body, comm, name="s5_fwd", grid=(t // tc,),
        out_shape=[jax.ShapeDtypeStruct((t, gn), F32), jax.ShapeDtypeStruct((t, gn), F32),
                   jax.ShapeDtypeStruct((t, sa), F32), jax.ShapeDtypeStruct((t, sa), F32)],
        in_specs=[_rows(tc, sa), _whole((sa, gn)), _whole((sa, gn)), _whole((1, gn)), _whole((1, gn)),
                  _whole((gn, sa)), _whole((gn, sa)), _whole((1, sa)), _whole((sa, sa)), _whole((1, sa))],
        out_specs=[_rows(tc, gn), _rows(tc, gn), _rows(tc, sa), _rows(tc, sa)],
        scratch_shapes=[pltpu.VMEM((SUBLANES, gn), F32), pltpu.VMEM((SUBLANES, gn), F32)],
        semantics="arbitrary",
    )(u, bbr_blk, bbi_blk, ar, ai, cre_blk, cimn_blk, d_skip, w_glu, b_glu)


def _lru_gates(xc, wr_ref, br_ref, wi_ref, bi_ref, lam_ref):
    r = _sigmoid(_dot(xc, wr_ref[...]) + br_ref[...])
    ig = _sigmoid(_dot(xc, wi_ref[...]) + bi_ref[...])
    sp = _softplus(-lam_ref[...])
    log_a = (-LRU_C * r) * sp
    return r, ig, sp, log_a


def _lru_fwd(u, conv_w, conv_b, wr_blk, b_r, wi_blk, b_i, lru_lambda, comm=None):
    t, w = u.shape
    tc = min(TIME_CHUNK, t)
    halo = SUBLANES

    def body(u_ref, cw_ref, cb_ref, wr_ref, br_ref, wi_ref, bi_ref, lam_ref,
             xc_ref, h_ref, hp_ref, ext_s, a_s, carry_s):
        @pl.when(pl.program_id(0) == 0)
        def _():
            ext_s[0:halo, :] = jnp.zeros((halo, w), F32)
            carry_s[...] = jnp.zeros_like(carry_s)

        ext_s[halo:halo + tc, :] = u_ref[...]
        xc = cb_ref[...]
        for k in range(CONV_WIDTH):
            off = halo - (CONV_WIDTH - 1) + k
            xc = xc + cw_ref[k:k + 1, :] * ext_s[off:off + tc, :]
        ext_s[0:halo, :] = ext_s[tc:tc + halo, :]
        xc_ref[...] = xc
        r, ig, sp, log_a = _lru_gates(xc, wr_ref, br_ref, wi_ref, bi_ref, lam_ref)
        a_s[...] = jnp.exp(log_a)
        h_ref[...] = jnp.sqrt(-_expm1(2.0 * log_a)) * ig * xc

        def step(row, carry):
            at = pl.ds(row, 1)
            hp_ref[at, :] = carry
            nxt = a_s[at, :] * carry + h_ref[at, :]
            h_ref[at, :] = nxt
            return nxt

        carry_s[0:1, :] = lax.fori_loop(0, tc, step, carry_s[0:1, :], unroll=8)

    return _run(
        body, comm, name="lru_fwd", grid=(t // tc,),
        out_shape=[jax.ShapeDtypeStruct((t, w), F32)] * 3,
        in_specs=[_rows(tc, w), _whole((CONV_WIDTH, w)), _whole((1, w)), _whole((w, w)), _whole((1, w)),
                  _whole((w, w)), _whole((1, w)), _whole((1, w))],
        out_specs=[_rows(tc, w)] * 3,
        scratch_shapes=[pltpu.VMEM((halo + tc, w), F32), pltpu.VMEM((tc, w), F32), pltpu.VMEM((SUBLANES, w), F32)],
        semantics="arbitrary",
    )(u, conv_w, conv_b, wr_blk, b_r, wi_blk, b_i, lru_lambda)


def _merge_fwd(y_a, h, za, zb, x, w_a_out_t, w_b_out, w_o, comm=None):
    t, d = x.shape
    sa, lw = y_a.shape[1], h.shape[1]
    tm = min(TOKEN_TILE, t)

    def body(ya_ref, h_ref, za_ref, zb_ref, x_ref, wa_ref, wb_ref, wo_ref, x1_ref, mg_ref, ma_ref, mb_ref):
        ma = _dot_nt(ya_ref[...], wa_ref[...])
        mb = _dot(h_ref[...], wb_ref[...])
        merged = _sigmoid(za_ref[...]) * ma + _sigmoid(zb_ref[...]) * mb
        ma_ref[...] = ma
        mb_ref[...] = mb
        mg_ref[...] = merged.astype(mg_ref.dtype)
        x1_ref[...] = x_ref[...] + _dot(merged, wo_ref[...])

    return _run(
        body, comm, name="merge_fwd", grid=(t // tm,),
        out_shape=[jax.ShapeDtypeStruct((t, d), F32), jax.ShapeDtypeStruct((t, d), BF16),
                   jax.ShapeDtypeStruct((t, d), F32), jax.ShapeDtypeStruct((t, d), F32)],
        in_specs=[_rows(tm, sa), _rows(tm, lw), _rows(tm, d), _rows(tm, d), _rows(tm, d),
                  _whole((d, sa)), _whole((lw, d)), _whole((d, d))],
        out_specs=[_rows(tm, d)] * 4,
        semantics="parallel",
    )(y_a, h, za, zb, x, w_a_out_t, w_b_out, w_o)


def _ffn_up_fwd(x1, g_ffn, w_gate_t, w_up_t, comm=None):
    t, d = x1.shape
    f = w_gate_t.shape[0]
    tm = min(TOKEN_TILE, t)

    def body(x_ref, g_ref, wg_ref, wu_ref, fg_ref, fu_ref):
        xhat, _ = _rms_stats(x_ref[...])
        h2 = (xhat * g_ref[...]).astype(BF16)
        fg_ref[...] = _dot_nt(h2, wg_ref[...])
        fu_ref[...] = _dot_nt(h2, wu_ref[...])

    return _run(
        body, comm, name="ffn_up_fwd", grid=(t // tm,),
        out_shape=[jax.ShapeDtypeStruct((t, f), F32)] * 2,
        in_specs=[_rows(tm, d), _whole((1, d)), _whole((f, d)), _whole((f, d))],
        out_specs=[_rows(tm, f)] * 2,
        semantics="parallel",
    )(x1, g_ffn, w_gate_t, w_up_t)


def _ffn_down_fwd(fg, fu, x1, w_down):
    t, d = x1.shape
    f = fg.shape[1]
    tm = min(TOKEN_TILE, t)

    def body(fg_ref, fu_ref, x_ref, wd_ref, x2_ref):
        fgv = fg_ref[...]
        act = fgv * _sigmoid(fgv) * fu_ref[...]
        x2_ref[...] = x_ref[...] + _dot(act, wd_ref[...])

    return pl.pallas_call(
        body, name="ffn_down_fwd", grid=(t // tm,),
        out_shape=jax.ShapeDtypeStruct((t, d), F32),
        in_specs=[_rows(tm, f), _rows(tm, f), _rows(tm, d), _whole((f, d))],
        out_specs=_rows(tm, d),
        compiler_params=_params("parallel"),
    )(fg, fu, x1, w_down)


def _store_on_last(pairs):
    @pl.when(pl.program_id(0) == pl.num_programs(0) - 1)
    def _():
        for acc, out in pairs:
            out[...] = acc[...].astype(out.dtype)


def _tail_fwd_bwd(x2, p, target, g_pg, w_pg, b_pg, w_ple_t, g_ple, g_final):
    t, d = x2.shape
    pd = p.shape[1]
    tm = min(TOKEN_TILE, t)

    def body(x2_ref, p_ref, tg_ref, gpg_ref, wpg_ref, bpg_ref, wple_ref, gple_ref, gfin_ref,
             dx2_ref, loss_ref, dwpg_out, dwple_out, vec_ref, dwpg_ref, dwple_ref):
        _zero_on_first(loss_ref, dwpg_ref, dwple_ref, vec_ref)
        x2v = x2_ref[...]
        xh2, r2 = _rms_stats(x2v)
        h3 = xh2 * gpg_ref[...]
        gp = _sigmoid(_dot(h3, wpg_ref[...]) + bpg_ref[...])
        pe = _dot_nt(p_ref[...], wple_ref[...])
        peh, r3 = _rms_stats(pe)
        e = peh * gple_ref[...]
        x3 = x2v + gp * e
        xh3, r4 = _rms_stats(x3)
        diff = xh3 * gfin_ref[...] - tg_ref[...]
        loss_ref[...] += 0.5 * jnp.sum(jnp.mean(diff * diff, axis=-1, keepdims=True))
        dy = diff * (1.0 / d)
        dx3, dgfin = _rms_bwd(dy, xh3, r4, gfin_ref[...])
        d_gp = dx3 * e
        d_e = dx3 * gp
        dpe, dgple = _rms_bwd(d_e, peh, r3, gple_ref[...])
        dwple_ref[...] += _dot_tn(dpe, p_ref[...])
        dpre = d_gp * gp * (1.0 - gp)
        dwpg_ref[...] += _dot_tn(h3, dpre)
        dh3 = _dot_nt(dpre, wpg_ref[...])
        dx2n, dgpg = _rms_bwd(dh3, xh2, r2, gpg_ref[...])
        dx2_ref[...] = dx3 + dx2n
        vec_ref[0:1, :] += _rowsum(dpre)
        vec_ref[1:2, :] += _rowsum(dgpg)
        vec_ref[2:3, :] += _rowsum(dgple)
        vec_ref[3:4, :] += _rowsum(dgfin)
        _store_on_last([(dwpg_ref, dwpg_out), (dwple_ref, dwple_out)])

    return pl.pallas_call(
        body, name="tail_fwd_bwd", grid=(t // tm,),
        out_shape=[jax.ShapeDtypeStruct((t, d), F32), jax.ShapeDtypeStruct((SUBLANES, LANES), F32),
                   jax.ShapeDtypeStruct((d, d), BF16), jax.ShapeDtypeStruct((d, pd), BF16),
                   jax.ShapeDtypeStruct((SUBLANES, d), F32)],
        in_specs=[_rows(tm, d), _rows(tm, pd), _rows(tm, d), _whole((1, d)), _whole((d, d)), _whole((1, d)),
                  _whole((d, pd)), _whole((1, d)), _whole((1, d))],
        out_specs=[_rows(tm, d), _acc((SUBLANES, LANES)), _acc((d, d)), _acc((d, pd)), _acc((SUBLANES, d))],
        scratch_shapes=[pltpu.VMEM((d, d), F32), pltpu.VMEM((d, pd), F32)],
        compiler_params=_params("arbitrary"),
    )(x2, p, target, g_pg, w_pg, b_pg, w_ple_t, g_ple, g_final)


def _ffn_bwd_a(dx2, fg, fu, w_down, comm=None):
    t, d = dx2.shape
    f = fg.shape[1]
    tm = min(TOKEN_TILE, t)

    def body(dx_ref, fg_ref, fu_ref, wd_ref, dfg_ref, dfu_ref, act_ref):
        dact = _dot_nt(dx_ref[...], wd_ref[...])
        fgv = fg_ref[...]
        fuv = fu_ref[...]
        sg = _sigmoid(fgv)
        silu = fgv * sg
        dfu_ref[...] = (dact * silu).astype(dfu_ref.dtype)
        dfg_ref[...] = (dact * fuv * (sg * (1.0 + fgv * (1.0 - sg)))).astype(dfg_ref.dtype)
        act_ref[...] = (silu * fuv).astype(act_ref.dtype)

    return _run(
        body, comm, name="ffn_bwd_a", grid=(t // tm,),
        out_shape=[jax.ShapeDtypeStruct((t, f), BF16)] * 3,
        in_specs=[_rows(tm, d), _rows(tm, f), _rows(tm, f), _whole((f, d))],
        out_specs=[_rows(tm, f)] * 3,
        semantics="parallel",
    )(dx2, fg, fu, w_down)


def _ffn_bwd_b(dfg, dfu, x1, dx2, g_ffn, w_gate_t, w_up_t):
    t, d = x1.shape
    f = dfg.shape[1]
    tm = min(TOKEN_TILE, t)

    def body(dfg_ref, dfu_ref, x_ref, dx2_ref, g_ref, wg_ref, wu_ref, dx1_ref, h2_ref, vec_ref):
        _zero_on_first(vec_ref)
        dh2 = _dot(dfg_ref[...], wg_ref[...]) + _dot(dfu_ref[...], wu_ref[...])
        xhat, r = _rms_stats(x_ref[...])
        h2_ref[...] = (xhat * g_ref[...]).astype(h2_ref.dtype)
        dxn, dg = _rms_bwd(dh2, xhat, r, g_ref[...])
        dx1_ref[...] = dx2_ref[...] + dxn
        vec_ref[0:1, :] += _rowsum(dg)

    return pl.pallas_call(
        body, name="ffn_bwd_b", grid=(t // tm,),
        out_shape=[jax.ShapeDtypeStruct((t, d), F32), jax.ShapeDtypeStruct((t, d), BF16),
                   jax.ShapeDtypeStruct((SUBLANES, d), F32)],
        in_specs=[_rows(tm, f), _rows(tm, f), _rows(tm, d), _rows(tm, d), _whole((1, d)), _whole((f, d)), _whole((f, d))],
        out_specs=[_rows(tm, d), _rows(tm, d), _acc((SUBLANES, d))],
        compiler_params=_params("arbitrary"),
    )(dfg, dfu, x1, dx2, g_ffn, w_gate_t, w_up_t)


def _matmul_tn(a, b, tn, name, dtype=F32):
    t, k = a.shape
    n = b.shape[1]

    def body(a_ref, b_ref, o_ref):
        o_ref[...] = _dot_tn(a_ref[...], b_ref[...]).astype(o_ref.dtype)

    return pl.pallas_call(
        body, name=name, grid=(n // tn,),
        out_shape=jax.ShapeDtypeStruct((k, n), dtype),
        in_specs=[_whole((t, k)), pl.BlockSpec((t, tn), lambda j: (0, j))],
        out_specs=pl.BlockSpec((k, tn), lambda j: (0, j)),
        compiler_params=_params("parallel"),
    )(a, b)


def _merge_bwd(dx1, merged, ma, mb, za, zb, y_a, h, w_o, w_a_out_t, w_b_out, comm=None):
    t, d = dx1.shape
    sa, lw = y_a.shape[1], h.shape[1]
    tm = min(TOKEN_TILE, t)

    def body(dx1_ref, mg_ref, ma_ref, mb_ref, za_ref, zb_ref, ya_ref, h_ref, wo_ref, wa_ref, wb_ref,
             dza_ref, dzb_ref, dya_ref, dyb_ref, dwo_out, dwa_out, dwb_out, dwo_ref, dwa_ref, dwb_ref):
        _zero_on_first(dwo_ref, dwa_ref, dwb_ref)
        dx1v = dx1_ref[...].astype(BF16)
        dmg = _dot_nt(dx1v, wo_ref[...])
        ga = _sigmoid(za_ref[...])
        gb = _sigmoid(zb_ref[...])
        dza_ref[...] = dmg * ma_ref[...] * ga * (1.0 - ga)
        dzb_ref[...] = dmg * mb_ref[...] * gb * (1.0 - gb)
        dma = (dmg * ga).astype(BF16)
        dmb = (dmg * gb).astype(BF16)
        dya_ref[...] = _dot(dma, wa_ref[...])
        dyb_ref[...] = _dot_nt(dmb, wb_ref[...])
        dwo_ref[...] += _dot_tn(mg_ref[...], dx1v)
        dwa_ref[...] += _dot_tn(dma, ya_ref[...])
        dwb_ref[...] += _dot_tn(h_ref[...], dmb)
        _store_on_last([(dwo_ref, dwo_out), (dwa_ref, dwa_out), (dwb_ref, dwb_out)])

    return _run(
        body, comm, name="merge_bwd", grid=(t // tm,),
        out_shape=[jax.ShapeDtypeStruct((t, d), F32), jax.ShapeDtypeStruct((t, d), F32),
                   jax.ShapeDtypeStruct((t, sa), F32), jax.ShapeDtypeStruct((t, lw), F32),
                   jax.ShapeDtypeStruct((d, d), BF16), jax.ShapeDtypeStruct((d, sa), BF16),
                   jax.ShapeDtypeStruct((lw, d), BF16)],
        in_specs=[_rows(tm, d), _rows(tm, d), _rows(tm, d), _rows(tm, d), _rows(tm, d), _rows(tm, d),
                  _rows(tm, sa), _rows(tm, lw), _whole((d, d)), _whole((d, sa)), _whole((lw, d))],
        out_specs=[_rows(tm, d), _rows(tm, d), _rows(tm, sa), _rows(tm, lw), _acc((d, d)), _acc((d, sa)), _acc((lw, d))],
        scratch_shapes=[pltpu.VMEM((d, d), F32), pltpu.VMEM((d, sa), F32), pltpu.VMEM((lw, d), F32)],
        semantics="arbitrary",
    )(dx1, merged, ma, mb, za, zb, y_a, h, w_o, w_a_out_t, w_b_out)


def _lru_bwd(dh, xc, hprev, u, conv_w, wr_blk, b_r, wi_blk, b_i, lru_lambda, comm=None):
    t, w = dh.shape
    tc = min(TIME_CHUNK, t)
    steps = t // tc
    halo = SUBLANES
    sub_per_chunk = tc // halo

    def body(dh_ref, xc_ref, hp_ref, u_ref, uh_ref, cw_ref, wr_ref, br_ref, wi_ref, bi_ref, lam_ref,
             du_ref, dwr_ref, dwi_ref, vec_ref, lam_s, a_s, dxc_s, uext_s, carry_s):
        chunk = steps - 1 - pl.program_id(0)

        @pl.when(pl.program_id(0) == 0)
        def _():
            carry_s[...] = jnp.zeros_like(carry_s)
            dxc_s[tc:tc + halo, :] = jnp.zeros((halo, w), F32)
            dwr_ref[...] = jnp.zeros_like(dwr_ref)
            dwi_ref[...] = jnp.zeros_like(dwi_ref)
            vec_ref[...] = jnp.zeros_like(vec_ref)

        xc = xc_ref[...]
        r, ig, sp, log_a = _lru_gates(xc, wr_ref, br_ref, wi_ref, bi_ref, lam_ref)
        a = jnp.exp(log_a)
        a_s[...] = a

        def step(i, q):
            at = pl.ds(tc - 1 - i, 1)
            lam_row = dh_ref[at, :] + q
            lam_s[at, :] = lam_row
            return a_s[at, :] * lam_row

        carry_s[0:1, :] = lax.fori_loop(0, tc, step, carry_s[0:1, :], unroll=8)
        lam = lam_s[...]
        mult = jnp.sqrt(-_expm1(2.0 * log_a))
        d_log_a = lam * hp_ref[...] * a - (lam * ig * xc) * (a * a) / mult
        d_ig = lam * mult * xc
        dpre_r = (d_log_a * (-LRU_C * sp)) * r * (1.0 - r)
        dpre_i = d_ig * ig * (1.0 - ig)
        dxc = lam * mult * ig + _dot_nt(dpre_r, wr_ref[...]) + _dot_nt(dpre_i, wi_ref[...])
        dwr_ref[...] += _dot_tn(xc, dpre_r)
        dwi_ref[...] += _dot_tn(xc, dpre_i)
        vec_ref[0:1, :] += _rowsum(dxc)
        vec_ref[1:2, :] += _rowsum(dpre_r)
        vec_ref[2:3, :] += _rowsum(dpre_i)
        vec_ref[3:4, :] += _rowsum(d_log_a * (-LRU_C * r)) * (-_sigmoid(-lam_ref[...]))
        dxc_s[0:tc, :] = dxc
        du = cw_ref[CONV_WIDTH - 1:CONV_WIDTH, :] * dxc
        for k in range(CONV_WIDTH - 1):
            off = CONV_WIDTH - 1 - k
            du = du + cw_ref[k:k + 1, :] * dxc_s[off:off + tc, :]
        du_ref[...] = du
        dxc_s[tc:tc + halo, :] = dxc_s[0:halo, :]
        uext_s[0:halo, :] = jnp.where(chunk > 0, uh_ref[...], 0.0)
        uext_s[halo:halo + tc, :] = u_ref[...]
        for k in range(CONV_WIDTH):
            off = halo - (CONV_WIDTH - 1) + k
            vec_ref[4 + k:5 + k, :] += _rowsum(dxc * uext_s[off:off + tc, :])

    halo_spec = pl.BlockSpec((halo, w), lambda i: (jnp.maximum((steps - 1 - i) * sub_per_chunk - 1, 0), 0))
    return _run(
        body, comm, name="lru_bwd", grid=(steps,),
        out_shape=[jax.ShapeDtypeStruct((t, w), F32), jax.ShapeDtypeStruct((w, w), F32),
                   jax.ShapeDtypeStruct((w, w), F32), jax.ShapeDtypeStruct((SUBLANES, w), F32)],
        in_specs=[_rows_rev(tc, w, steps)] * 4 + [halo_spec, _whole((CONV_WIDTH, w)), _whole((w, w)), _whole((1, w)),
                                                  _whole((w, w)), _whole((1, w)), _whole((1, w))],
        out_specs=[_rows_rev(tc, w, steps), _acc((w, w)), _acc((w, w)), _acc((SUBLANES, w))],
        scratch_shapes=[pltpu.VMEM((tc, w), F32), pltpu.VMEM((tc, w), F32), pltpu.VMEM((tc + halo, w), F32),
                        pltpu.VMEM((halo + tc, w), F32), pltpu.VMEM((SUBLANES, w), F32)],
        semantics="arbitrary",
    )(dh, xc, hprev, u, u, conv_w, wr_blk, b_r, wi_blk, b_i, lru_lambda)


def _s5_bwd(dya, y, sr, si, u, w_glu, b_glu, cre_blk, cimn_blk, bbr_blk, bbi_blk, ar, ai, d_skip, comm=None):
    t, sa = dya.shape
    gn = sr.shape[1]
    tc = min(TIME_CHUNK, t)
    steps = t // tc
    halo = SUBLANES

    def body(dya_ref, y_ref, sr_ref, si_ref, u_ref, wg_ref, bg_ref, cre_ref, cim_ref, bbr_ref, bbi_ref,
             ar_ref, ai_ref, d_ref, du_ref, lr_ref, li_ref, dy_ref, dwg_out, vsa_ref, vgn_ref, gr_s, gi_s, cr_s, ci_s,
             dwg_ref):
        @pl.when(pl.program_id(0) == 0)
        def _():
            cr_s[...] = jnp.zeros_like(cr_s)
            ci_s[...] = jnp.zeros_like(ci_s)
            gr_s[tc:tc + halo, :] = jnp.zeros((halo, gn), F32)
            gi_s[tc:tc + halo, :] = jnp.zeros((halo, gn), F32)
            dwg_ref[...] = jnp.zeros_like(dwg_ref)
            vsa_ref[...] = jnp.zeros_like(vsa_ref)
            vgn_ref[...] = jnp.zeros_like(vgn_ref)

        yv = y_ref[...]
        uv = u_ref[...]
        zz = _gelu(yv)
        sg = _sigmoid(_dot(zz, wg_ref[...]) + bg_ref[...])
        dyav = dya_ref[...]
        dq = dyav * zz * sg * (1.0 - sg)
        dzz = dyav * sg + _dot_nt(dq, wg_ref[...])
        dwg_ref[...] += _dot_tn(zz, dq)
        dy = dzz * _gelu_grad(yv)
        dyb = dy.astype(BF16)
        dy_ref[...] = dyb.astype(dy_ref.dtype)
        vsa_ref[0:1, :] += _rowsum(dq)
        vsa_ref[1:2, :] += _rowsum(dy * uv)
        gr_s[0:tc, :] = _dot_nt(dyb, cre_ref[...])
        gi_s[0:tc, :] = _dot_nt(dyb, cim_ref[...])
        a_r = ar_ref[...]
        a_i = ai_ref[...]

        def step(i, carry):
            l_r, l_i = carry
            at = pl.ds(tc - 1 - i, 1)
            n_r = gr_s[at, :] + a_r * l_r + a_i * l_i
            n_i = gi_s[at, :] + a_r * l_i - a_i * l_r
            gr_s[at, :] = n_r
            gi_s[at, :] = n_i
            return n_r, n_i

        l_r, l_i = lax.fori_loop(0, tc, step, (cr_s[0:1, :], ci_s[0:1, :]), unroll=8)
        cr_s[0:1, :] = l_r
        ci_s[0:1, :] = l_i
        nxt_r = gr_s[1:tc + 1, :]
        nxt_i = gi_s[1:tc + 1, :]
        srv = sr_ref[...]
        siv = si_ref[...]
        vgn_ref[0:1, :] += _rowsum(nxt_r * srv + nxt_i * siv)
        vgn_ref[1:2, :] += _rowsum(nxt_i * srv - nxt_r * siv)
        lam_r = gr_s[0:tc, :]
        lam_i = gi_s[0:tc, :]
        gr_s[tc:tc + halo, :] = gr_s[0:halo, :]
        gi_s[tc:tc + halo, :] = gi_s[0:halo, :]
        lrb = lam_r.astype(BF16)
        lib = lam_i.astype(BF16)
        lr_ref[...] = lrb.astype(lr_ref.dtype)
        li_ref[...] = lib.astype(li_ref.dtype)
        du_ref[...] = _dot_nt(lrb, bbr_ref[...]) + _dot_nt(lib, bbi_ref[...]) + dy * d_ref[...]
        _store_on_last([(dwg_ref, dwg_out)])

    return _run(
        body, comm, name="s5_bwd", grid=(steps,),
        out_shape=[jax.ShapeDtypeStruct((t, sa), F32), jax.ShapeDtypeStruct((t, gn), BF16),
                   jax.ShapeDtypeStruct((t, gn), BF16), jax.ShapeDtypeStruct((t, sa), BF16),
                   jax.ShapeDtypeStruct((sa, sa), BF16), jax.ShapeDtypeStruct((SUBLANES, sa), F32),
                   jax.ShapeDtypeStruct((SUBLANES, gn), F32)],
        in_specs=[_rows_rev(tc, sa, steps), _rows_rev(tc, sa, steps), _rows_rev(tc, gn, steps), _rows_rev(tc, gn, steps),
                  _rows_rev(tc, sa, steps), _whole((sa, sa)), _whole((1, sa)), _whole((gn, sa)), _whole((gn, sa)),
                  _whole((sa, gn)), _whole((sa, gn)), _whole((1, gn)), _whole((1, gn)), _whole((1, sa))],
        out_specs=[_rows_rev(tc, sa, steps), _rows_rev(tc, gn, steps), _rows_rev(tc, gn, steps), _rows_rev(tc, sa, steps),
                   _acc((sa, sa)), _acc((SUBLANES, sa)), _acc((SUBLANES, gn))],
        scratch_shapes=[pltpu.VMEM((tc + halo, gn), F32), pltpu.VMEM((tc + halo, gn), F32),
                        pltpu.VMEM((SUBLANES, gn), F32), pltpu.VMEM((SUBLANES, gn), F32), pltpu.VMEM((sa, sa), F32)],
        semantics="arbitrary",
    )(dya, y, sr, si, u, w_glu, b_glu, cre_blk, cimn_blk, bbr_blk, bbi_blk, ar, ai, d_skip)


def _inproj_bwd(dparts, x, dx1, g_mix, w_in_t, comm=None):
    t, d = x.shape
    n = w_in_t.shape[0]
    widths = [p.shape[1] for p in dparts]
    offs = [sum(widths[:i]) for i in range(len(widths) + 1)]
    tm = min(TOKEN_TILE, t)
    np_ = len(dparts)

    def body(*refs):
        dz_refs = refs[:np_]
        x_ref, dx1_ref, g_ref, w_ref, gx_ref, h_ref, dz_ref, vd_ref, vn_ref = refs[np_:]
        _zero_on_first(vd_ref, vn_ref)
        dh = jnp.zeros((tm, d), F32)
        for k, r in enumerate(dz_refs):
            lo, hi = offs[k], offs[k + 1]
            dzk = r[...]
            dh = dh + _dot(dzk, w_ref[lo:hi, :])
            dz_ref[:, lo:hi] = dzk.astype(dz_ref.dtype)
            vn_ref[0:1, lo:hi] += _rowsum(dzk)
        xhat, r0 = _rms_stats(x_ref[...])
        h_ref[...] = (xhat * g_ref[...]).astype(h_ref.dtype)
        dxn, dg = _rms_bwd(dh, xhat, r0, g_ref[...])
        gx_ref[...] = dx1_ref[...] + dxn
        vd_ref[0:1, :] += _rowsum(dg)

    return _run(
        body, comm, name="inproj_bwd", grid=(t // tm,),
        out_shape=[jax.ShapeDtypeStruct((t, d), F32), jax.ShapeDtypeStruct((t, d), BF16),
                   jax.ShapeDtypeStruct((t, n), BF16), jax.ShapeDtypeStruct((SUBLANES, d), F32),
                   jax.ShapeDtypeStruct((SUBLANES, n), F32)],
        in_specs=[_rows(tm, w) for w in widths] + [_rows(tm, d), _rows(tm, d), _whole((1, d)), _whole((n, d))],
        out_specs=[_rows(tm, d), _rows(tm, d), _rows(tm, n), _acc((SUBLANES, d)), _acc((SUBLANES, n))],
        semantics="arbitrary",
    )(*dparts, x, dx1, g_mix, w_in_t)


def _adamw_math(w, g, m, v):
    m_new = ADAM_B1 * m + (1.0 - ADAM_B1) * g
    v_new = ADAM_B2 * v + (1.0 - ADAM_B2) * (g * g)
    m_hat = m_new / (1.0 - ADAM_B1 ** ADAM_STEP)
    v_hat = v_new / (1.0 - ADAM_B2 ** ADAM_STEP)
    delta = -ADAM_LR * (m_hat / (jnp.sqrt(v_hat) + ADAM_EPS) + ADAM_WD * w)
    return delta, m_new, v_new


def _row_tile(rows):
    for cand in (256, 128, 64, 32, 16, 8):
        if rows % cand == 0:
            return cand
    return rows


def _adamw(parts, w, m, v, name, transposed=False):
    rows, cols = w.shape
    npart = parts.shape[0]
    tr = _row_tile(rows)
    if transposed:
        parts_spec = pl.BlockSpec((npart, cols, tr), lambda i: (0, 0, i))
    else:
        parts_spec = pl.BlockSpec((npart, tr, cols), lambda i: (0, i, 0))

    def body(p_ref, w_ref, m_ref, v_ref, g_ref, d_ref, mo_ref, vo_ref):
        g = p_ref[0].astype(F32)
        for k in range(1, npart):
            g = g + p_ref[k].astype(F32)
        if transposed:
            g = g.T
        delta, m_new, v_new = _adamw_math(w_ref[...], g, m_ref[...], v_ref[...])
        g_ref[...] = g
        d_ref[...] = delta
        mo_ref[...] = m_new
        vo_ref[...] = v_new

    return pl.pallas_call(
        body, name=name, grid=(rows // tr,),
        out_shape=[jax.ShapeDtypeStruct((rows, cols), F32)] * 4,
        in_specs=[parts_spec] + [_rows(tr, cols)] * 3,
        out_specs=[_rows(tr, cols)] * 4,
        compiler_params=_params("parallel"),
    )(parts, w, m, v)


def _sum_parts(parts, name):
    npart, rows, cols = parts.shape
    tr = _row_tile(rows)

    def body(p_ref, o_ref):
        g = p_ref[0].astype(F32)
        for k in range(1, npart):
            g = g + p_ref[k].astype(F32)
        o_ref[...] = g

    return pl.pallas_call(
        body, name=name, grid=(rows // tr,),
        out_shape=jax.ShapeDtypeStruct((rows, cols), F32),
        in_specs=[pl.BlockSpec((npart, tr, cols), lambda i: (0, i, 0))],
        out_specs=_rows(tr, cols),
        compiler_params=_params("parallel"),
    )(parts)


def _mesh_position():
    return lax.axis_index("x"), lax.axis_index("y"), lax.axis_index("c")


def _flip(pos, rel):
    x, y, c = pos
    return (1 - x if rel & 4 else x, 1 - y if rel & 2 else y, 1 - c if rel & 1 else c)


def _index(pos):
    return 4 * pos[0] + 2 * pos[1] + pos[2]


_ANY = pl.BlockSpec(memory_space=pl.ANY)
FLAT_ROWS = 32


def _dma_sems(n):
    return [pltpu.SemaphoreType.DMA((n, N_DEV - 1)), pltpu.SemaphoreType.DMA((n, N_DEV - 1)), pltpu.SemaphoreType.DMA((n,))]


def _block_of(ref, idx, rows, flat):
    if flat:
        return ref.at[pl.ds(pl.multiple_of(idx * rows, FLAT_ROWS), rows), :]
    return ref.at[idx]


class _Gather:
    chips = (4, 2, 6)

    def __init__(self, shards):
        self.inputs = list(shards)
        self.flat = [s.shape[0] % FLAT_ROWS == 0 for s in shards]
        self.out_shape = [
            jax.ShapeDtypeStruct((N_DEV * s.shape[0], s.shape[1]) if f else (N_DEV,) + s.shape, s.dtype)
            for s, f in zip(shards, self.flat)]
        self.sems = _dma_sems(len(shards))

    def _copy(self, ins, outs, sems, i, k, block, to, own=False):
        dst = _block_of(outs[i], _index(block), self.inputs[i].shape[0], self.flat[i])
        return pltpu.make_async_remote_copy(
            src_ref=ins[i] if own else dst, dst_ref=dst, send_sem=sems[0].at[i, k], recv_sem=sems[1].at[i, k],
            device_id=to, device_id_type=MESH)

    def _local(self, ins, outs, sems, i, me):
        dst = _block_of(outs[i], _index(me), self.inputs[i].shape[0], self.flat[i])
        return pltpu.make_async_copy(ins[i], dst, sems[2].at[i])

    def _first(self, ins, outs, sems, i, me):
        cps = [self._copy(ins, outs, sems, i, 0, me, _flip(me, 1), own=True)]
        cps += [self._copy(ins, outs, sems, i, 1 + j, me, _flip(me, rel), own=True) for j, rel in enumerate(self.chips)]
        return cps

    def _passed(self, ins, outs, sems, i, j, me):
        return self._copy(ins, outs, sems, i, 4 + j, _flip(me, self.chips[j]), _flip(me, 1))

    def before(self, ins, outs, sems):
        n = len(self.inputs)
        me = _mesh_position()

        @pl.when(pl.program_id(0) == 0)
        def _():
            for i in range(n):
                self._local(ins, outs, sems, i, me).start()
                for cp in self._first(ins, outs, sems, i, me):
                    cp.start()

        @pl.when(pl.program_id(0) == pl.num_programs(0) - 1)
        def _():
            for j, rel in enumerate(self.chips):
                for i in range(n):
                    self._copy(ins, outs, sems, i, 1 + j, _flip(me, rel), me).wait_recv()
                    self._passed(ins, outs, sems, i, j, me).start()

    def after(self, ins, outs, sems):
        n = len(self.inputs)
        me = _mesh_position()
        sibling = _flip(me, 1)

        @pl.when(pl.program_id(0) == pl.num_programs(0) - 1)
        def _():
            for i in range(n):
                self._copy(ins, outs, sems, i, 0, sibling, me).wait_recv()
                for j, rel in enumerate(self.chips):
                    self._copy(ins, outs, sems, i, 4 + j, _flip(sibling, rel), me).wait_recv()
            for i in range(n):
                for cp in self._first(ins, outs, sems, i, me):
                    cp.wait_send()
                for j in range(len(self.chips)):
                    self._passed(ins, outs, sems, i, j, me).wait_send()
                self._local(ins, outs, sems, i, me).wait()


class _Exchange:
    def __init__(self, arrays):
        self.inputs = list(arrays)
        self.flat = [a.ndim == 2 for a in arrays]
        self.block = [(a.shape[0] // N_DEV, a.shape[1]) if a.ndim == 2 else a.shape[1:] for a in arrays]
        for f, b in zip(self.flat, self.block):
            assert not f or b[0] % FLAT_ROWS == 0, b
        self.out_shape = [jax.ShapeDtypeStruct((N_DEV,) + tuple(b), a.dtype) for a, b in zip(arrays, self.block)]
        self.sems = _dma_sems(len(arrays))

    def _send(self, ins, outs, sems, i, rel, me):
        peer = _flip(me, rel)
        return pltpu.make_async_remote_copy(
            src_ref=_block_of(ins[i], _index(peer), self.block[i][0], self.flat[i]), dst_ref=outs[i].at[_index(me)],
            send_sem=sems[0].at[i, rel - 1], recv_sem=sems[1].at[i, rel - 1], device_id=peer, device_id_type=MESH)

    def _arrival(self, ins, outs, sems, i, rel, me):
        peer = _flip(me, rel)
        return pltpu.make_async_remote_copy(
            src_ref=_block_of(ins[i], _index(me), self.block[i][0], self.flat[i]), dst_ref=outs[i].at[_index(peer)],
            send_sem=sems[0].at[i, rel - 1], recv_sem=sems[1].at[i, rel - 1], device_id=peer, device_id_type=MESH)

    def _local(self, ins, outs, sems, i, me):
        return pltpu.make_async_copy(_block_of(ins[i], _index(me), self.block[i][0], self.flat[i]),
                                     outs[i].at[_index(me)], sems[2].at[i])

    def before(self, ins, outs, sems):
        me = _mesh_position()

        @pl.when(pl.program_id(0) == 0)
        def _():
            for i in range(len(self.inputs)):
                self._local(ins, outs, sems, i, me).start()
                for rel in range(1, N_DEV):
                    self._send(ins, outs, sems, i, rel, me).start()

    def after(self, ins, outs, sems):
        me = _mesh_position()

        @pl.when(pl.program_id(0) == pl.num_programs(0) - 1)
        def _():
            for i in range(len(self.inputs)):
                for rel in range(1, N_DEV):
                    self._arrival(ins, outs, sems, i, rel, me).wait_recv()
            for i in range(len(self.inputs)):
                for rel in range(1, N_DEV):
                    self._send(ins, outs, sems, i, rel, me).wait_send()
                self._local(ins, outs, sems, i, me).wait()


def _run(body, comm, *, semantics, out_shape, in_specs, out_specs, scratch_shapes=(), **kw):
    if comm is None:
        return pl.pallas_call(body, out_shape=out_shape, in_specs=in_specs, out_specs=out_specs,
                              scratch_shapes=list(scratch_shapes), compiler_params=_params(semantics), **kw)
    single = not isinstance(out_shape, (list, tuple))
    outs = [out_shape] if single else list(out_shape)
    ospecs = [out_specs] if single else list(out_specs)
    counts = [len(in_specs), len(comm.inputs), len(outs), len(comm.out_shape), len(scratch_shapes), len(comm.sems)]

    def carrying(*refs):
        groups, pos = [], 0
        for c in counts:
            groups.append(refs[pos:pos + c])
            pos += c
        main_in, comm_in, main_out, comm_out, main_scratch, comm_sems = groups
        comm.before(comm_in, comm_out, comm_sems)
        body(*main_in, *main_out, *main_scratch)
        comm.after(comm_in, comm_out, comm_sems)

    call = pl.pallas_call(
        carrying, out_shape=outs + list(comm.out_shape), in_specs=list(in_specs) + [_ANY] * len(comm.inputs),
        out_specs=ospecs + [_ANY] * len(comm.out_shape), scratch_shapes=list(scratch_shapes) + list(comm.sems),
        compiler_params=_params("arbitrary"), **kw)

    def apply(*args):
        res = call(*args, *comm.inputs)
        main = res[:len(outs)]
        return (main[0] if single else list(main)), list(res[len(outs):])

    return apply


def _alone(comm, name):
    return _run(lambda: None, comm, semantics="arbitrary", name=name, grid=(1,), out_shape=[], in_specs=[], out_specs=[])()[1]


SHARDED = {"w_in": 1, "w_glu": 0, "conv_w": 1, "w_a_out": 1, "w_b_out": 0, "w_o": 0, "w_ffn_gate": 1, "w_ffn_up": 1,
           "w_ffn_down": 0, "w_ple_gate": 0, "w_ple": 1}
TRANSPOSED = ("w_in", "w_a_out", "w_ffn_gate", "w_ffn_up", "w_ple")
SMALL = ["g_mix", "b_in", "lam_re", "lam_im", "log_dt", "s5_b_re", "s5_b_im", "s5_c_re", "s5_c_im", "s5_d", "b_glu",
         "conv_b", "w_r", "b_r", "w_i", "b_i", "lru_lambda", "g_ffn", "g_ple_gate", "b_ple_gate", "g_ple", "g_final"]
WEIGHTS = ["g_mix", "w_in", "b_in", "lam_re", "lam_im", "log_dt", "s5_b_re", "s5_b_im", "s5_c_re", "s5_c_im", "s5_d",
           "w_glu", "b_glu", "conv_w", "conv_b", "w_r", "b_r", "w_i", "b_i", "lru_lambda", "w_a_out", "w_b_out", "w_o",
           "g_ffn", "w_ffn_gate", "w_ffn_up", "w_ffn_down", "g_ple_gate", "w_ple_gate", "b_ple_gate", "w_ple", "g_ple",
           "g_final"]


def _unblock(gathered, axis):
    nb, r, c = gathered.shape
    if axis == 0:
        return gathered.reshape(nb * r, c)
    return jnp.transpose(gathered, (1, 0, 2)).reshape(r, nb * c)


def _block(full, axis):
    r, c = full.shape
    if axis == 0:
        return full.reshape(N_DEV, r // N_DEV, c)
    return jnp.transpose(full.reshape(r, N_DEV, c // N_DEV), (1, 0, 2))


def _wire_grad(name, grad):
    return _block(grad, 1) if name == "conv_w" else grad


def _block_diag(w):
    h, i, j = w.shape
    eye = jnp.eye(h, dtype=w.dtype)
    return (w[:, :, None, :] * eye[:, None, :, None]).reshape(h * i, h * j)


def _diag_blocks(dense, h):
    hi, hj = dense.shape
    i, j = hi // h, hj // h
    return jnp.einsum("hihj->hij", dense.reshape(h, i, h, j))


def _s5_discretise(lam_re, lam_im, log_dt, b_re, b_im):
    dt = jnp.exp(log_dt)[:, None]
    mag = jnp.exp(lam_re * dt)
    ar = mag * jnp.cos(lam_im * dt)
    ai = mag * jnp.sin(lam_im * dt)
    den = lam_re * lam_re + lam_im * lam_im
    nr = ar - 1.0
    fr = (nr * lam_re + ai * lam_im) / den
    fi = (ai * lam_re - nr * lam_im) / den
    bbr = fr[..., None] * b_re - fi[..., None] * b_im
    bbi = fr[..., None] * b_im + fi[..., None] * b_re
    return ar, ai, bbr, bbi


def _pack(arrs, rows):
    flat = jnp.concatenate([a.reshape(-1).astype(F32) for a in arrs])
    return jnp.pad(flat, (0, rows * PACK_COLS - flat.shape[0])).reshape(rows, PACK_COLS)


def _unpack(packed, shapes):
    flat = packed.reshape(-1)
    out, off = [], 0
    for s in shapes:
        size = math.prod(s)
        out.append(flat[off:off + size].reshape(s))
        off += size
    return out


def _local_step(x, p, target, src, small, distributed=True):
    full = {} if distributed else dict(src)
    got = {}

    def gather(keys):
        return _Gather([src[k] for k in keys]) if distributed else None

    def exchange(keys, grads):
        return _Exchange([_wire_grad(k, grads[k]) for k in keys]) if distributed else None

    def carry(fn, *args, job=None, keys=(), sink=None):
        if job is None:
            return fn(*args)
        res, extra = fn(*args, comm=job)
        sink.update(zip(keys, extra))
        return res

    if distributed:
        first = ["w_in", "w_glu", "conv_w"]
        for k, a in zip(first, _alone(gather(first), "gather_first")):
            full[k] = _unblock(a, 1) if k == "conv_w" else a

    d = x.shape[1]
    g, n, pch = small["s5_b_re"].shape
    heads = small["w_r"].shape[0]
    sa, lw = g * pch, small["lru_lambda"].shape[-1]
    widths = [sa, lw, d, d]
    row = lambda v: v.reshape(1, -1)

    (ar, ai, bbr, bbi), disc_vjp = jax.vjp(_s5_discretise, small["lam_re"], small["lam_im"], small["log_dt"],
                                           small["s5_b_re"], small["s5_b_im"])
    bbr_blk = _block_diag(jnp.transpose(bbr, (0, 2, 1))).astype(BF16)
    bbi_blk = _block_diag(jnp.transpose(bbi, (0, 2, 1))).astype(BF16)
    cre_blk = _block_diag(jnp.transpose(small["s5_c_re"], (0, 2, 1))).astype(BF16)
    cimn_blk = _block_diag(jnp.transpose(-small["s5_c_im"], (0, 2, 1))).astype(BF16)
    wr_blk = _block_diag(small["w_r"]).astype(BF16)
    wi_blk = _block_diag(small["w_i"]).astype(BF16)
    ar_row, ai_row = row(ar), row(ai)
    d_row = row(small["s5_d"])
    conv_w = full["conv_w"].astype(F32)

    keys = ["w_a_out", "w_b_out", "w_o"]
    u_a, u_b, za, zb = carry(_inproj_fwd, x, row(small["g_mix"]), full["w_in"], row(small["b_in"]), widths,
                             job=gather(keys), keys=keys, sink=full)
    keys = ["w_ffn_gate"]
    sr, si, y, y_a = carry(_s5_fwd, u_a, bbr_blk, bbi_blk, ar_row, ai_row, cre_blk, cimn_blk, d_row, full["w_glu"],
                           row(small["b_glu"]), job=gather(keys), keys=keys, sink=full)
    keys = ["w_ffn_up"]
    xc, h, hprev = carry(_lru_fwd, u_b, conv_w, row(small["conv_b"]), wr_blk, row(small["b_r"]), wi_blk,
                         row(small["b_i"]), row(small["lru_lambda"]), job=gather(keys), keys=keys, sink=full)
    keys = ["w_ffn_down"]
    x1, merged, ma, mb = carry(_merge_fwd, y_a, h, za, zb, x, full["w_a_out"], full["w_b_out"], full["w_o"],
                               job=gather(keys), keys=keys, sink=full)
    keys = ["w_ple_gate", "w_ple"]
    fg, fu = carry(_ffn_up_fwd, x1, row(small["g_ffn"]), full["w_ffn_gate"], full["w_ffn_up"],
                   job=gather(keys), keys=keys, sink=full)
    x2 = _ffn_down_fwd(fg, fu, x1, full["w_ffn_down"])

    gw = {}
    dx2, loss_blk, gw["w_ple_gate"], gw["w_ple"], vec_tail = _tail_fwd_bwd(
        x2, p, target, row(small["g_ple_gate"]), full["w_ple_gate"], row(small["b_ple_gate"]), full["w_ple"],
        row(small["g_ple"]), row(small["g_final"]))
    keys = ["w_ple_gate", "w_ple"]
    dfg, dfu, act = carry(_ffn_bwd_a, dx2, fg, fu, full["w_ffn_down"], job=exchange(keys, gw), keys=keys, sink=got)
    tn_d = min(d, 512)
    gw["w_ffn_down"] = _matmul_tn(act, dx2, tn_d, "dw_ffn_down", BF16)
    dx1, h2, vec_ffn = _ffn_bwd_b(dfg, dfu, x1, dx2, row(small["g_ffn"]), full["w_ffn_gate"], full["w_ffn_up"])
    gw["w_ffn_gate"] = _matmul_tn(dfg, h2, tn_d, "dw_ffn_gate", BF16)
    gw["w_ffn_up"] = _matmul_tn(dfu, h2, tn_d, "dw_ffn_up", BF16)
    keys = ["w_ffn_down"]
    dza, dzb, dya, dyb, gw["w_o"], gw["w_a_out"], gw["w_b_out"] = carry(
        _merge_bwd, dx1, merged, ma, mb, za, zb, y_a, h, full["w_o"], full["w_a_out"], full["w_b_out"],
        job=exchange(keys, gw), keys=keys, sink=got)
    keys = ["w_ffn_gate"]
    du_b, dwr_dense, dwi_dense, vec_lru = carry(
        _lru_bwd, dyb, xc, hprev, u_b, conv_w, wr_blk, row(small["b_r"]), wi_blk, row(small["b_i"]),
        row(small["lru_lambda"]), job=exchange(keys, gw), keys=keys, sink=got)
    gw["conv_w"] = vec_lru[4:4 + CONV_WIDTH]
    keys = ["w_ffn_up"]
    du_a, lam_r, lam_i, dy16, gw["w_glu"], vec_sa, vec_gn = carry(
        _s5_bwd, dya, y, sr, si, u_a, full["w_glu"], row(small["b_glu"]), cre_blk, cimn_blk, bbr_blk, bbi_blk, ar_row,
        ai_row, d_row, job=exchange(keys, gw), keys=keys, sink=got)
    tn_s = min(sa, 512)
    dbbr_dense = _matmul_tn(lam_r, u_a, tn_s, "dbb_re")
    dbbi_dense = _matmul_tn(lam_i, u_a, tn_s, "dbb_im")
    tn_n = min(sr.shape[1], 512)
    dcre_dense = _matmul_tn(dy16, sr, tn_n, "dc_re")
    dcimn_dense = _matmul_tn(dy16, si, tn_n, "dc_im")
    keys = ["w_o", "w_a_out", "w_b_out"]
    grad_x, h0, dz16, vec_mix, vec_bin = carry(
        _inproj_bwd, [du_a, du_b, dza, dzb], x, dx1, row(small["g_mix"]), full["w_in"],
        job=exchange(keys, gw), keys=keys, sink=got)
    gw["w_in"] = _matmul_tn(dz16, h0, tn_d, "dw_in", BF16)

    d_bbr = _diag_blocks(dbbr_dense, g)
    d_bbi = _diag_blocks(dbbi_dense, g)
    d_cre = _diag_blocks(dcre_dense, g)
    d_cim = -_diag_blocks(dcimn_dense, g)
    pre = {"d_ar": vec_gn[0].reshape(g, n), "d_ai": vec_gn[1].reshape(g, n), "d_bbr": d_bbr, "d_bbi": d_bbi}
    gs = {
        "g_mix": vec_mix[0], "b_in": vec_bin[0], "s5_c_re": d_cre, "s5_c_im": d_cim, "s5_d": vec_sa[1].reshape(g, pch),
        "b_glu": vec_sa[0], "conv_b": vec_lru[0], "w_r": _diag_blocks(dwr_dense, heads), "b_r": vec_lru[1].reshape(heads, -1),
        "w_i": _diag_blocks(dwi_dense, heads), "b_i": vec_lru[2].reshape(heads, -1), "lru_lambda": vec_lru[3],
        "g_ffn": vec_ffn[0], "g_ple_gate": vec_tail[1], "b_ple_gate": vec_tail[0], "g_ple": vec_tail[2],
        "g_final": vec_tail[3],
    }
    if distributed:
        got.update({k: gw[k] for k in ("w_in", "w_glu", "conv_w")})
        gw = got
    return loss_blk[0, 0], grad_x, gw, gs, pre, disc_vjp


_PRE_KEYS = ["d_ar", "d_ai", "d_bbr", "d_bbi"]
_DIRECT_SMALL = [k for k in SMALL if k not in ("lam_re", "lam_im", "log_dt", "s5_b_re", "s5_b_im")]


def kernel(x, p, g_mix, w_in, b_in, lam_re, lam_im, log_dt, s5_b_re, s5_b_im, s5_c_re, s5_c_im, s5_d, w_glu, b_glu, conv_w, conv_b, w_r, b_r, w_i, b_i, lru_lambda, w_a_out, w_b_out, w_o, g_ffn, w_ffn_gate, w_ffn_up, w_ffn_down, g_ple_gate, w_ple_gate, b_ple_gate, w_ple, g_ple, g_final, loss_target, m_g_mix, m_w_in, m_b_in, m_lam_re, m_lam_im, m_log_dt, m_s5_b_re, m_s5_b_im, m_s5_c_re, m_s5_c_im, m_s5_d, m_w_glu, m_b_glu, m_conv_w, m_conv_b, m_w_r, m_b_r, m_w_i, m_b_i, m_lru_lambda, m_w_a_out, m_w_b_out, m_w_o, m_g_ffn, m_w_ffn_gate, m_w_ffn_up, m_w_ffn_down, m_g_ple_gate, m_w_ple_gate, m_b_ple_gate, m_w_ple, m_g_ple, m_g_final, v_g_mix, v_w_in, v_b_in, v_lam_re, v_lam_im, v_log_dt, v_s5_b_re, v_s5_b_im, v_s5_c_re, v_s5_c_im, v_s5_d, v_w_glu, v_b_glu, v_conv_w, v_conv_b, v_w_r, v_b_r, v_w_i, v_b_i, v_lru_lambda, v_w_a_out, v_w_b_out, v_w_o, v_g_ffn, v_w_ffn_gate, v_w_ffn_up, v_w_ffn_down, v_g_ple_gate, v_w_ple_gate, v_b_ple_gate, v_w_ple, v_g_ple, v_g_final):
    given = dict(locals())
    wts = {k: given[k] for k in WEIGHTS}
    moms = {k: given["m_" + k] for k in WEIGHTS}
    vels = {k: given["v_" + k] for k in WEIGHTS}

    def drop_depth(k, a):
        return a if k == "g_final" else a[0]

    small = {k: drop_depth(k, wts[k]) for k in SMALL}
    shard = {k: wts[k][0] for k in SHARDED}
    names = list(SHARDED)

    def wire(k):
        if k == "conv_w":
            return shard[k]
        return (shard[k].T if k in TRANSPOSED else shard[k]).astype(BF16)

    loss_part, grad_x, parts, gs, pre, disc_vjp = _local_step(x[0], p[0, 0], loss_target[0], {k: wire(k) for k in names},
                                                              small)

    pack_shapes = [pre[k].shape for k in _PRE_KEYS] + [small[k].shape for k in _DIRECT_SMALL] + [(1,)]
    pack_size = sum(math.prod(s) for s in pack_shapes)
    slice_rows = -(-pack_size // (N_DEV * PACK_COLS * SUBLANES)) * SUBLANES
    packed = _pack([pre[k] for k in _PRE_KEYS] + [gs[k] for k in _DIRECT_SMALL] + [loss_part], N_DEV * slice_rows)
    last = ["w_in", "w_glu", "conv_w"]
    received = _alone(_Exchange([_wire_grad(k, parts[k]) for k in last] + [packed.reshape(N_DEV, slice_rows, PACK_COLS)]),
                      "exchange_last")
    parts.update(zip(last, received[:-1]))

    my_slice = _sum_parts(received[-1], "reduce_small")
    (reduced,) = _alone(_Gather([my_slice]), "gather_small")
    red = _unpack(reduced.reshape(N_DEV * slice_rows, PACK_COLS), pack_shapes)
    red_pre = dict(zip(_PRE_KEYS, red[:len(_PRE_KEYS)]))
    g_small = dict(zip(_DIRECT_SMALL, red[len(_PRE_KEYS):-1]))
    loss = red[-1].reshape(())
    d_lr, d_li, d_ldt, d_bre, d_bim = disc_vjp((red_pre["d_ar"], red_pre["d_ai"], red_pre["d_bbr"], red_pre["d_bbi"]))
    g_small.update(lam_re=d_lr, lam_im=d_li, log_dt=d_ldt, s5_b_re=d_bre, s5_b_im=d_bim)
    small_shapes = [small[k].shape for k in SMALL]
    small_rows = -(-sum(math.prod(s) for s in small_shapes) // (PACK_COLS * SUBLANES)) * SUBLANES
    g_pk = _pack([g_small[k] for k in SMALL], small_rows)
    w_pk = _pack([small[k] for k in SMALL], small_rows)
    m_pk = _pack([drop_depth(k, moms[k]) for k in SMALL], small_rows)
    v_pk = _pack([drop_depth(k, vels[k]) for k in SMALL], small_rows)
    small_out = _adamw(g_pk[None], w_pk, m_pk, v_pk, "adamw_small")
    small_out = [dict(zip(SMALL, _unpack(o, small_shapes))) for o in small_out]

    big_out = {}
    for k in names:
        big_out[k] = _adamw(parts[k], shard[k], moms[k][0], vels[k][0], "adamw_" + k, transposed=k in TRANSPOSED)

    outs = [loss, grad_x[None]]
    for slot in range(4):
        for k in WEIGHTS:
            if k in SHARDED:
                outs.append(big_out[k][slot][None])
            else:
                a = small_out[slot][k]
                outs.append(a if k == "g_final" else a[None])
    return tuple(outs)
```

```python
import math

import jax
import jax.numpy as jnp
from jax import lax
from jax.experimental import pallas as pl
from jax.experimental.pallas import tpu as pltpu

F32 = jnp.float32
BF16 = jnp.bfloat16

EPS = 1e-6
LRU_C = 8.0
CONV_WIDTH = 4
ADAM_LR = 0.001
ADAM_B1 = 0.9
ADAM_B2 = 0.999
ADAM_EPS = 1e-08
ADAM_WD = 0.01
ADAM_STEP = 10

N_DEV = 8
MESH = pl.DeviceIdType.MESH
SUBLANES = 8
LANES = 128
VMEM_LIMIT = 56 * 1024 * 1024
TOKEN_TILE = 256
TIME_CHUNK = 256


def _dot(a, b):
    return jnp.dot(a.astype(BF16), b.astype(BF16), preferred_element_type=F32)


def _dot_nt(a, b):
    return lax.dot_general(a.astype(BF16), b.astype(BF16), (((1,), (1,)), ((), ())), preferred_element_type=F32)


def _dot_tn(a, b):
    return lax.dot_general(a.astype(BF16), b.astype(BF16), (((0,), (0,)), ((), ())), preferred_element_type=F32)


def _sigmoid(x):
    return jax.nn.sigmoid(x)


def _rms_stats(x):
    r = lax.rsqrt(jnp.mean(x * x, axis=-1, keepdims=True) + EPS)
    return x * r, r


def _rms_bwd(dy, xhat, r, g):
    dxn = dy * g
    dx = r * (dxn - xhat * jnp.mean(dxn * xhat, axis=-1, keepdims=True))
    return dx, dy * xhat


def _rowsum(v):
    return jnp.sum(v, axis=0, keepdims=True)


def _expm1(x):
    u = jnp.exp(x)
    um1 = u - 1.0
    safe = jnp.where(um1 == 0.0, 1.0, jnp.log(u))
    return jnp.where(um1 == 0.0, x, um1 * x / safe)


def _softplus(x):
    e = jnp.exp(-jnp.abs(x))
    u = 1.0 + e
    um1 = u - 1.0
    safe = jnp.where(um1 == 0.0, 1.0, um1)
    log1p_e = jnp.where(um1 == 0.0, e, jnp.log(u) * e / safe)
    return jnp.maximum(x, 0.0) + log1p_e


_GELU_K = math.sqrt(2.0 / math.pi)
_GELU_C = 0.044715


def _gelu(x):
    return 0.5 * x * (1.0 + jnp.tanh(_GELU_K * (x + _GELU_C * x * x * x)))


def _gelu_grad(x):
    th = jnp.tanh(_GELU_K * (x + _GELU_C * x * x * x))
    return 0.5 * (1.0 + th) + 0.5 * x * (1.0 - th * th) * _GELU_K * (1.0 + 3.0 * _GELU_C * x * x)


def _params(*sem):
    return pltpu.CompilerParams(dimension_semantics=sem, vmem_limit_bytes=VMEM_LIMIT)


def _rows(tm, n):
    return pl.BlockSpec((tm, n), lambda i: (i, 0))


def _rows_rev(tm, n, steps):
    return pl.BlockSpec((tm, n), lambda i: (steps - 1 - i, 0))


def _whole(shape):
    nd = len(shape)
    return pl.BlockSpec(shape, lambda i: (0,) * nd, pipeline_mode=pl.Buffered(1))


def _acc(shape):
    nd = len(shape)
    return pl.BlockSpec(shape, lambda i: (0,) * nd)


def _zero_on_first(*refs):
    @pl.when(pl.program_id(0) == 0)
    def _():
        for r in refs:
            r[...] = jnp.zeros_like(r)


def _inproj_fwd(x, g_mix, w_in_t, b_in, widths, comm=None):
    t, d = x.shape
    n = w_in_t.shape[0]
    tm = min(TOKEN_TILE, t)
    offs = [sum(widths[:i]) for i in range(len(widths) + 1)]

    def body(x_ref, g_ref, w_ref, b_ref, *outs):
        xhat, _ = _rms_stats(x_ref[...])
        h = (xhat * g_ref[...]).astype(BF16)
        for k, o_ref in enumerate(outs):
            lo, hi = offs[k], offs[k + 1]
            o_ref[...] = _dot_nt(h, w_ref[lo:hi, :]) + b_ref[:, lo:hi]

    return _run(
        body, comm, name="inproj_fwd", grid=(t // tm,),
        out_shape=[jax.ShapeDtypeStruct((t, w), F32) for w in widths],
        in_specs=[_rows(tm, d), _whole((1, d)), _whole((n, d)), _whole((1, n))],
        out_specs=[_rows(tm, w) for w in widths],
        semantics="parallel",
    )(x, g_mix, w_in_t, b_in)


def _s5_fwd(u, bbr_blk, bbi_blk, ar, ai, cre_blk, cimn_blk, d_skip, w_glu, b_glu, comm=None):
    t, sa = u.shape
    gn = bbr_blk.shape[0]
    tc = min(TIME_CHUNK, t)

    def body(u_ref, bbr_ref, bbi_ref, ar_ref, ai_ref, cre_ref, cim_ref, d_ref, wg_ref, bg_ref,
             sr_ref, si_ref, y_ref, ya_ref, cr_s, ci_s):
        _zero_on_first(cr_s, ci_s)
        uv = u_ref[...]
        ub = uv.astype(BF16)
        sr_ref[...] = _dot_nt(ub, bbr_ref[...])
        si_ref[...] = _dot_nt(ub, bbi_ref[...])
        a_r = ar_ref[...]
        a_i = ai_ref[...]

        def step(row, carry):
            c_r, c_i = carry
            at = pl.ds(row, 1)
            n_r = a_r * c_r - a_i * c_i + sr_ref[at, :]
            n_i = a_r * c_i + a_i * c_r + si_ref[at, :]
            sr_ref[at, :] = n_r
            si_ref[at, :] = n_i
            return n_r, n_i

        c_r, c_i = lax.fori_loop(0, tc, step, (cr_s[0:1, :], ci_s[0:1, :]), unroll=8)
        cr_s[0:1, :] = c_r
        ci_s[0:1, :] = c_i
        y = _dot_nt(sr_ref[...], cre_ref[...]) + _dot_nt(si_ref[...], cim_ref[...]) + d_ref[...] * uv
        y_ref[...] = y
        zz = _gelu(y)
        q = _dot(zz, wg_ref[...]) + bg_ref[...]
        ya_ref[...] = zz * _sigmoid(q)

    return _run(
        body, comm, name="s5_fwd", grid=(t // tc,),
        out_shape=[jax.ShapeDtypeStruct((t, gn), F32), jax.ShapeDtypeStruct((t, gn), F32),
                   jax.ShapeDtypeStruct((t, sa), F32), jax.ShapeDtypeStruct((t, sa), F32)],
        in_specs=[_rows(tc, sa), _whole((gn, sa)), _whole((gn, sa)), _whole((1, gn)), _whole((1, gn)),
                  _whole((sa, gn)), _whole((sa, gn)), _whole((1, sa)), _whole((sa, sa)), _whole((1, sa))],
        out_specs=[_rows(tc, gn), _rows(tc, gn), _rows(tc, sa), _rows(tc, sa)],
        scratch_shapes=[pltpu.VMEM((SUBLANES, gn), F32), pltpu.VMEM((SUBLANES, gn), F32)],
        semantics="arbitrary",
    )(u, bbr_blk, bbi_blk, ar, ai, cre_blk, cimn_blk, d_skip, w_glu, b_glu)


def _lru_gates(xc, wr_ref, br_ref, wi_ref, bi_ref, lam_ref):
    r = _sigmoid(_dot(xc, wr_ref[...]) + br_ref[...])
    ig = _sigmoid(_dot(xc, wi_ref[...]) + bi_ref[...])
    sp = _softplus(-lam_ref[...])
    log_a = (-LRU_C * r) * sp
    return r, ig, sp, log_a


def _lru_fwd(u, conv_w, conv_b, wr_blk, b_r, wi_blk, b_i, lru_lambda, comm=None):
    t, w = u.shape
    tc = min(TIME_CHUNK, t)
    halo = SUBLANES

    def body(u_ref, cw_ref, cb_ref, wr_ref, br_ref, wi_ref, bi_ref, lam_ref,
             xc_ref, h_ref, hp_ref, ext_s, a_s, carry_s):
        @pl.when(pl.program_id(0) == 0)
        def _():
            ext_s[0:halo, :] = jnp.zeros((halo, w), F32)
            carry_s[...] = jnp.zeros_like(carry_s)

        ext_s[halo:halo + tc, :] = u_ref[...]
        xc = cb_ref[...]
        for k in range(CONV_WIDTH):
            off = halo - (CONV_WIDTH - 1) + k
            xc = xc + cw_ref[k:k + 1, :] * ext_s[off:off + tc, :]
        ext_s[0:halo, :] = ext_s[tc:tc + halo, :]
        xc_ref[...] = xc
        r, ig, sp, log_a = _lru_gates(xc, wr_ref, br_ref, wi_ref, bi_ref, lam_ref)
        a_s[...] = jnp.exp(log_a)
        h_ref[...] = jnp.sqrt(-_expm1(2.0 * log_a)) * ig * xc

        def step(row, carry):
            at = pl.ds(row, 1)
            hp_ref[at, :] = carry
            nxt = a_s[at, :] * carry + h_ref[at, :]
            h_ref[at, :] = nxt
            return nxt

        carry_s[0:1, :] = lax.fori_loop(0, tc, step, carry_s[0:1, :], unroll=8)

    return _run(
        body, comm, name="lru_fwd", grid=(t // tc,),
        out_shape=[jax.ShapeDtypeStruct((t, w), F32)] * 3,
        in_specs=[_rows(tc, w), _whole((CONV_WIDTH, w)), _whole((1, w)), _whole((w, w)), _whole((1, w)),
                  _whole((w, w)), _whole((1, w)), _whole((1, w))],
        out_specs=[_rows(tc, w)] * 3,
        scratch_shapes=[pltpu.VMEM((halo + tc, w), F32), pltpu.VMEM((tc, w), F32), pltpu.VMEM((SUBLANES, w), F32)],
        semantics="arbitrary",
    )(u, conv_w, conv_b, wr_blk, b_r, wi_blk, b_i, lru_lambda)


def _merge_fwd(y_a, h, za, zb, x, w_a_out_t, w_b_out, w_o, comm=None):
    t, d = x.shape
    sa, lw = y_a.shape[1], h.shape[1]
    tm = min(TOKEN_TILE, t)

    def body(ya_ref, h_ref, za_ref, zb_ref, x_ref, wa_ref, wb_ref, wo_ref, x1_ref, mg_ref, ma_ref, mb_ref):
        ma = _dot_nt(ya_ref[...], wa_ref[...])
        mb = _dot(h_ref[...], wb_ref[...])
        merged = _sigmoid(za_ref[...]) * ma + _sigmoid(zb_ref[...]) * mb
        ma_ref[...] = ma
        mb_ref[...] = mb
        mg_ref[...] = merged.astype(mg_ref.dtype)
        x1_ref[...] = x_ref[...] + _dot(merged, wo_ref[...])

    return _run(
        body, comm, name="merge_fwd", grid=(t // tm,),
        out_shape=[jax.ShapeDtypeStruct((t, d), F32), jax.ShapeDtypeStruct((t, d), BF16),
                   jax.ShapeDtypeStruct((t, d), F32), jax.ShapeDtypeStruct((t, d), F32)],
        in_specs=[_rows(tm, sa), _rows(tm, lw), _rows(tm, d), _rows(tm, d), _rows(tm, d),
                  _whole((d, sa)), _whole((lw, d)), _whole((d, d))],
        out_specs=[_rows(tm, d)] * 4,
        semantics="parallel",
    )(y_a, h, za, zb, x, w_a_out_t, w_b_out, w_o)


def _ffn_up_fwd(x1, g_ffn, w_gate_t, w_up_t, comm=None):
    t, d = x1.shape
    f = w_gate_t.shape[0]
    tm = min(TOKEN_TILE, t)

    def body(x_ref, g_ref, wg_ref, wu_ref, fg_ref, fu_ref):
        xhat, _ = _rms_stats(x_ref[...])
        h2 = (xhat * g_ref[...]).astype(BF16)
        fg_ref[...] = _dot_nt(h2, wg_ref[...])
        fu_ref[...] = _dot_nt(h2, wu_ref[...])

    return _run(
        body, comm, name="ffn_up_fwd", grid=(t // tm,),
        out_shape=[jax.ShapeDtypeStruct((t, f), F32)] * 2,
        in_specs=[_rows(tm, d), _whole((1, d)), _whole((f, d)), _whole((f, d))],
        out_specs=[_rows(tm, f)] * 2,
        semantics="parallel",
    )(x1, g_ffn, w_gate_t, w_up_t)


def _ffn_down_fwd(fg, fu, x1, w_down):
    t, d = x1.shape
    f = fg.shape[1]
    tm = min(TOKEN_TILE, t)

    def body(fg_ref, fu_ref, x_ref, wd_ref, x2_ref):
        fgv = fg_ref[...]
        act = fgv * _sigmoid(fgv) * fu_ref[...]
        x2_ref[...] = x_ref[...] + _dot(act, wd_ref[...])

    return pl.pallas_call(
        body, name="ffn_down_fwd", grid=(t // tm,),
        out_shape=jax.ShapeDtypeStruct((t, d), F32),
        in_specs=[_rows(tm, f), _rows(tm, f), _rows(tm, d), _whole((f, d))],
        out_specs=_rows(tm, d),
        compiler_params=_params("parallel"),
    )(fg, fu, x1, w_down)


def _store_on_last(pairs):
    @pl.when(pl.program_id(0) == pl.num_programs(0) - 1)
    def _():
        for acc, out in pairs:
            out[...] = acc[...].astype(out.dtype)


def _tail_fwd_bwd(x2, p, target, g_pg, w_pg, b_pg, w_ple_t, g_ple, g_final):
    t, d = x2.shape
    pd = p.shape[1]
    tm = min(TOKEN_TILE, t)

    def body(x2_ref, p_ref, tg_ref, gpg_ref, wpg_ref, bpg_ref, wple_ref, gple_ref, gfin_ref,
             dx2_ref, loss_ref, dwpg_out, dwple_out, vec_ref, dwpg_ref, dwple_ref):
        _zero_on_first(loss_ref, dwpg_ref, dwple_ref, vec_ref)
        x2v = x2_ref[...]
        xh2, r2 = _rms_stats(x2v)
        h3 = xh2 * gpg_ref[...]
        gp = _sigmoid(_dot(h3, wpg_ref[...]) + bpg_ref[...])
        pe = _dot_nt(p_ref[...], wple_ref[...])
        peh, r3 = _rms_stats(pe)
        e = peh * gple_ref[...]
        x3 = x2v + gp * e
        xh3, r4 = _rms_stats(x3)
        diff = xh3 * gfin_ref[...] - tg_ref[...]
        loss_ref[...] += 0.5 * jnp.sum(jnp.mean(diff * diff, axis=-1, keepdims=True))
        dy = diff * (1.0 / d)
        dx3, dgfin = _rms_bwd(dy, xh3, r4, gfin_ref[...])
        d_gp = dx3 * e
        d_e = dx3 * gp
        dpe, dgple = _rms_bwd(d_e, peh, r3, gple_ref[...])
        dwple_ref[...] += _dot_tn(dpe, p_ref[...])
        dpre = d_gp * gp * (1.0 - gp)
        dwpg_ref[...] += _dot_tn(h3, dpre)
        dh3 = _dot_nt(dpre, wpg_ref[...])
        dx2n, dgpg = _rms_bwd(dh3, xh2, r2, gpg_ref[...])
        dx2_ref[...] = dx3 + dx2n
        vec_ref[0:1, :] += _rowsum(dpre)
        vec_ref[1:2, :] += _rowsum(dgpg)
        vec_ref[2:3, :] += _rowsum(dgple)
        vec_ref[3:4, :] += _rowsum(dgfin)
        _store_on_last([(dwpg_ref, dwpg_out), (dwple_ref, dwple_out)])

    return pl.pallas_call(
        body, name="tail_fwd_bwd", grid=(t // tm,),
        out_shape=[jax.ShapeDtypeStruct((t, d), F32), jax.ShapeDtypeStruct((SUBLANES, LANES), F32),
                   jax.ShapeDtypeStruct((d, d), BF16), jax.ShapeDtypeStruct((d, pd), BF16),
                   jax.ShapeDtypeStruct((SUBLANES, d), F32)],
        in_specs=[_rows(tm, d), _rows(tm, pd), _rows(tm, d), _whole((1, d)), _whole((d, d)), _whole((1, d)),
                  _whole((d, pd)), _whole((1, d)), _whole((1, d))],
        out_specs=[_rows(tm, d), _acc((SUBLANES, LANES)), _acc((d, d)), _acc((d, pd)), _acc((SUBLANES, d))],
        scratch_shapes=[pltpu.VMEM((d, d), F32), pltpu.VMEM((d, pd), F32)],
        compiler_params=_params("arbitrary"),
    )(x2, p, target, g_pg, w_pg, b_pg, w_ple_t, g_ple, g_final)


def _ffn_bwd_a(dx2, fg, fu, w_down, comm=None):
    t, d = dx2.shape
    f = fg.shape[1]
    tm = min(TOKEN_TILE, t)

    def body(dx_ref, fg_ref, fu_ref, wd_ref, dfg_ref, dfu_ref, act_ref):
        dact = _dot_nt(dx_ref[...], wd_ref[...])
        fgv = fg_ref[...]
        fuv = fu_ref[...]
        sg = _sigmoid(fgv)
        silu = fgv * sg
        dfu_ref[...] = (dact * silu).astype(dfu_ref.dtype)
        dfg_ref[...] = (dact * fuv * (sg * (1.0 + fgv * (1.0 - sg)))).astype(dfg_ref.dtype)
        act_ref[...] = (silu * fuv).astype(act_ref.dtype)

    return _run(
        body, comm, name="ffn_bwd_a", grid=(t // tm,),
        out_shape=[jax.ShapeDtypeStruct((t, f), BF16)] * 3,
        in_specs=[_rows(tm, d), _rows(tm, f), _rows(tm, f), _whole((f, d))],
        out_specs=[_rows(tm, f)] * 3,
        semantics="parallel",
    )(dx2, fg, fu, w_down)


def _ffn_bwd_b(dfg, dfu, x1, dx2, g_ffn, w_gate_t, w_up_t):
    t, d = x1.shape
    f = dfg.shape[1]
    tm = min(TOKEN_TILE, t)

    def body(dfg_ref, dfu_ref, x_ref, dx2_ref, g_ref, wg_ref, wu_ref, dx1_ref, h2_ref, vec_ref):
        _zero_on_first(vec_ref)
        dh2 = _dot(dfg_ref[...], wg_ref[...]) + _dot(dfu_ref[...], wu_ref[...])
        xhat, r = _rms_stats(x_ref[...])
        h2_ref[...] = (xhat * g_ref[...]).astype(h2_ref.dtype)
        dxn, dg = _rms_bwd(dh2, xhat, r, g_ref[...])
        dx1_ref[...] = dx2_ref[...] + dxn
        vec_ref[0:1, :] += _rowsum(dg)

    return pl.pallas_call(
        body, name="ffn_bwd_b", grid=(t // tm,),
        out_shape=[jax.ShapeDtypeStruct((t, d), F32), jax.ShapeDtypeStruct((t, d), BF16),
                   jax.ShapeDtypeStruct((SUBLANES, d), F32)],
        in_specs=[_rows(tm, f), _rows(tm, f), _rows(tm, d), _rows(tm, d), _whole((1, d)), _whole((f, d)), _whole((f, d))],
        out_specs=[_rows(tm, d), _rows(tm, d), _acc((SUBLANES, d))],
        compiler_params=_params("arbitrary"),
    )(dfg, dfu, x1, dx2, g_ffn, w_gate_t, w_up_t)


def _matmul_tn(a, b, tn, name, dtype=F32):
    t, k = a.shape
    n = b.shape[1]

    def body(a_ref, b_ref, o_ref):
        o_ref[...] = _dot_tn(a_ref[...], b_ref[...]).astype(o_ref.dtype)

    return pl.pallas_call(
        body, name=name, grid=(n // tn,),
        out_shape=jax.ShapeDtypeStruct((k, n), dtype),
        in_specs=[_whole((t, k)), pl.BlockSpec((t, tn), lambda j: (0, j))],
        out_specs=pl.BlockSpec((k, tn), lambda j: (0, j)),
        compiler_params=_params("parallel"),
    )(a, b)


def _merge_bwd(dx1, merged, ma, mb, za, zb, y_a, h, w_o, w_a_out_t, w_b_out, comm=None):
    t, d = dx1.shape
    sa, lw = y_a.shape[1], h.shape[1]
    tm = min(TOKEN_TILE, t)

    def body(dx1_ref, mg_ref, ma_ref, mb_ref, za_ref, zb_ref, ya_ref, h_ref, wo_ref, wa_ref, wb_ref,
             dza_ref, dzb_ref, dya_ref, dyb_ref, dwo_out, dwa_out, dwb_out, dwo_ref, dwa_ref, dwb_ref):
        _zero_on_first(dwo_ref, dwa_ref, dwb_ref)
        dx1v = dx1_ref[...].astype(BF16)
        dmg = _dot_nt(dx1v, wo_ref[...])
        ga = _sigmoid(za_ref[...])
        gb = _sigmoid(zb_ref[...])
        dza_ref[...] = dmg * ma_ref[...] * ga * (1.0 - ga)
        dzb_ref[...] = dmg * mb_ref[...] * gb * (1.0 - gb)
        dma = (dmg * ga).astype(BF16)
        dmb = (dmg * gb).astype(BF16)
        dya_ref[...] = _dot(dma, wa_ref[...])
        dyb_ref[...] = _dot_nt(dmb, wb_ref[...])
        dwo_ref[...] += _dot_tn(mg_ref[...], dx1v)
        dwa_ref[...] += _dot_tn(dma, ya_ref[...])
        dwb_ref[...] += _dot_tn(h_ref[...], dmb)
        _store_on_last([(dwo_ref, dwo_out), (dwa_ref, dwa_out), (dwb_ref, dwb_out)])

    return _run(
        body, comm, name="merge_bwd", grid=(t // tm,),
        out_shape=[jax.ShapeDtypeStruct((t, d), F32), jax.ShapeDtypeStruct((t, d), F32),
                   jax.ShapeDtypeStruct((t, sa), F32), jax.ShapeDtypeStruct((t, lw), F32),
                   jax.ShapeDtypeStruct((d, d), BF16), jax.ShapeDtypeStruct((d, sa), BF16),
                   jax.ShapeDtypeStruct((lw, d), BF16)],
        in_specs=[_rows(tm, d), _rows(tm, d), _rows(tm, d), _rows(tm, d), _rows(tm, d), _rows(tm, d),
                  _rows(tm, sa), _rows(tm, lw), _whole((d, d)), _whole((d, sa)), _whole((lw, d))],
        out_specs=[_rows(tm, d), _rows(tm, d), _rows(tm, sa), _rows(tm, lw), _acc((d, d)), _acc((d, sa)), _acc((lw, d))],
        scratch_shapes=[pltpu.VMEM((d, d), F32), pltpu.VMEM((d, sa), F32), pltpu.VMEM((lw, d), F32)],
        semantics="arbitrary",
    )(dx1, merged, ma, mb, za, zb, y_a, h, w_o, w_a_out_t, w_b_out)


def _fold_diag_blocks(dense, row_group, col_group, row0=0, col0=0):
    r, c = dense.shape
    rows = lax.broadcasted_iota(jnp.int32, (r, c), 0) + row0
    cols = lax.broadcasted_iota(jnp.int32, (r, c), 1) + col0
    kept = jnp.where(rows // row_group == cols // col_group, dense, 0.0)
    pick = (lax.broadcasted_iota(jnp.int32, (c, col_group), 0) % col_group
            == lax.broadcasted_iota(jnp.int32, (c, col_group), 1)).astype(F32)
    return jnp.dot(kept, pick, preferred_element_type=F32, precision=lax.Precision.HIGHEST)


def _lru_bwd(dh, xc, hprev, u, conv_w, wr_blk, b_r, wi_blk, b_i, lru_lambda, head_dim, comm=None):
    t, w = dh.shape
    tc = min(TIME_CHUNK, t)
    steps = t // tc
    halo = SUBLANES
    sub_per_chunk = tc // halo

    def body(dh_ref, xc_ref, hp_ref, u_ref, uh_ref, cw_ref, wr_ref, br_ref, wi_ref, bi_ref, lam_ref,
             du_ref, dwr_out, dwi_out, vec_ref, lam_s, a_s, dxc_s, uext_s, carry_s, dwr_ref, dwi_ref):
        chunk = steps - 1 - pl.program_id(0)

        @pl.when(pl.program_id(0) == 0)
        def _():
            carry_s[...] = jnp.zeros_like(carry_s)
            dxc_s[tc:tc + halo, :] = jnp.zeros((halo, w), F32)
            dwr_ref[...] = jnp.zeros_like(dwr_ref)
            dwi_ref[...] = jnp.zeros_like(dwi_ref)
            vec_ref[...] = jnp.zeros_like(vec_ref)

        xc = xc_ref[...]
        r, ig, sp, log_a = _lru_gates(xc, wr_ref, br_ref, wi_ref, bi_ref, lam_ref)
        a = jnp.exp(log_a)
        a_s[...] = a

        def step(i, q):
            at = pl.ds(tc - 1 - i, 1)
            lam_row = dh_ref[at, :] + q
            lam_s[at, :] = lam_row
            return a_s[at, :] * lam_row

        carry_s[0:1, :] = lax.fori_loop(0, tc, step, carry_s[0:1, :], unroll=8)
        lam = lam_s[...]
        mult = jnp.sqrt(-_expm1(2.0 * log_a))
        d_log_a = lam * hp_ref[...] * a - (lam * ig * xc) * (a * a) / mult
        d_ig = lam * mult * xc
        dpre_r = (d_log_a * (-LRU_C * sp)) * r * (1.0 - r)
        dpre_i = d_ig * ig * (1.0 - ig)
        dxc = lam * mult * ig + _dot_nt(dpre_r, wr_ref[...]) + _dot_nt(dpre_i, wi_ref[...])
        dwr_ref[...] += _dot_tn(xc, dpre_r)
        dwi_ref[...] += _dot_tn(xc, dpre_i)
        vec_ref[0:1, :] += _rowsum(dxc)
        vec_ref[1:2, :] += _rowsum(dpre_r)
        vec_ref[2:3, :] += _rowsum(dpre_i)
        vec_ref[3:4, :] += _rowsum(d_log_a * (-LRU_C * r)) * (-_sigmoid(-lam_ref[...]))
        dxc_s[0:tc, :] = dxc
        du = cw_ref[CONV_WIDTH - 1:CONV_WIDTH, :] * dxc
        for k in range(CONV_WIDTH - 1):
            off = CONV_WIDTH - 1 - k
            du = du + cw_ref[k:k + 1, :] * dxc_s[off:off + tc, :]
        du_ref[...] = du
        dxc_s[tc:tc + halo, :] = dxc_s[0:halo, :]
        uext_s[0:halo, :] = jnp.where(chunk > 0, uh_ref[...], 0.0)
        uext_s[halo:halo + tc, :] = u_ref[...]
        for k in range(CONV_WIDTH):
            off = halo - (CONV_WIDTH - 1) + k
            vec_ref[4 + k:5 + k, :] += _rowsum(dxc * uext_s[off:off + tc, :])

        @pl.when(pl.program_id(0) == steps - 1)
        def _():
            dwr_out[...] = _fold_diag_blocks(dwr_ref[...], head_dim, head_dim)
            dwi_out[...] = _fold_diag_blocks(dwi_ref[...], head_dim, head_dim)

    halo_spec = pl.BlockSpec((halo, w), lambda i: (jnp.maximum((steps - 1 - i) * sub_per_chunk - 1, 0), 0))
    return _run(
        body, comm, name="lru_bwd", grid=(steps,),
        out_shape=[jax.ShapeDtypeStruct((t, w), F32), jax.ShapeDtypeStruct((w, head_dim), F32),
                   jax.ShapeDtypeStruct((w, head_dim), F32), jax.ShapeDtypeStruct((SUBLANES, w), F32)],
        in_specs=[_rows_rev(tc, w, steps)] * 4 + [halo_spec, _whole((CONV_WIDTH, w)), _whole((w, w)), _whole((1, w)),
                                                  _whole((w, w)), _whole((1, w)), _whole((1, w))],
        out_specs=[_rows_rev(tc, w, steps), _acc((w, head_dim)), _acc((w, head_dim)), _acc((SUBLANES, w))],
        scratch_shapes=[pltpu.VMEM((tc, w), F32), pltpu.VMEM((tc, w), F32), pltpu.VMEM((tc + halo, w), F32),
                        pltpu.VMEM((halo + tc, w), F32), pltpu.VMEM((SUBLANES, w), F32), pltpu.VMEM((w, w), F32),
                        pltpu.VMEM((w, w), F32)],
        semantics="arbitrary",
    )(dh, xc, hprev, u, u, conv_w, wr_blk, b_r, wi_blk, b_i, lru_lambda)


def _s5_bwd(dya, y, sr, si, u, w_glu, b_glu, cre_blk, cimn_blk, bbr_blk, bbi_blk, ar, ai, d_skip, comm=None):
    t, sa = dya.shape
    gn = sr.shape[1]
    tc = min(TIME_CHUNK, t)
    steps = t // tc
    halo = SUBLANES

    def body(dya_ref, y_ref, sr_ref, si_ref, u_ref, wg_ref, bg_ref, cre_ref, cim_ref, bbr_ref, bbi_ref,
             ar_ref, ai_ref, d_ref, du_ref, lr_ref, li_ref, dy_ref, dwg_out, vsa_ref, vgn_ref, gr_s, gi_s, cr_s, ci_s,
             dwg_ref):
        @pl.when(pl.program_id(0) == 0)
        def _():
            cr_s[...] = jnp.zeros_like(cr_s)
            ci_s[...] = jnp.zeros_like(ci_s)
            gr_s[tc:tc + halo, :] = jnp.zeros((halo, gn), F32)
            gi_s[tc:tc + halo, :] = jnp.zeros((halo, gn), F32)
            dwg_ref[...] = jnp.zeros_like(dwg_ref)
            vsa_ref[...] = jnp.zeros_like(vsa_ref)
            vgn_ref[...] = jnp.zeros_like(vgn_ref)

        yv = y_ref[...]
        uv = u_ref[...]
        zz = _gelu(yv)
        sg = _sigmoid(_dot(zz, wg_ref[...]) + bg_ref[...])
        dyav = dya_ref[...]
        dq = dyav * zz * sg * (1.0 - sg)
        dzz = dyav * sg + _dot_nt(dq, wg_ref[...])
        dwg_ref[...] += _dot_tn(zz, dq)
        dy = dzz * _gelu_grad(yv)
        dyb = dy.astype(BF16)
        dy_ref[...] = dyb.astype(dy_ref.dtype)
        vsa_ref[0:1, :] += _rowsum(dq)
        vsa_ref[1:2, :] += _rowsum(dy * uv)
        gr_s[0:tc, :] = _dot(dyb, cre_ref[...])
        gi_s[0:tc, :] = _dot(dyb, cim_ref[...])
        a_r = ar_ref[...]
        a_i = ai_ref[...]

        def step(i, carry):
            l_r, l_i = carry
            at = pl.ds(tc - 1 - i, 1)
            n_r = gr_s[at, :] + a_r * l_r + a_i * l_i
            n_i = gi_s[at, :] + a_r * l_i - a_i * l_r
            gr_s[at, :] = n_r
            gi_s[at, :] = n_i
            return n_r, n_i

        l_r, l_i = lax.fori_loop(0, tc, step, (cr_s[0:1, :], ci_s[0:1, :]), unroll=8)
        cr_s[0:1, :] = l_r
        ci_s[0:1, :] = l_i
        nxt_r = gr_s[1:tc + 1, :]
        nxt_i = gi_s[1:tc + 1, :]
        srv = sr_ref[...]
        siv = si_ref[...]
        vgn_ref[0:1, :] += _rowsum(nxt_r * srv + nxt_i * siv)
        vgn_ref[1:2, :] += _rowsum(nxt_i * srv - nxt_r * siv)
        lam_r = gr_s[0:tc, :]
        lam_i = gi_s[0:tc, :]
        gr_s[tc:tc + halo, :] = gr_s[0:halo, :]
        gi_s[tc:tc + halo, :] = gi_s[0:halo, :]
        lrb = lam_r.astype(BF16)
        lib = lam_i.astype(BF16)
        lr_ref[...] = lrb.astype(lr_ref.dtype)
        li_ref[...] = lib.astype(li_ref.dtype)
        du_ref[...] = _dot(lrb, bbr_ref[...]) + _dot(lib, bbi_ref[...]) + dy * d_ref[...]
        _store_on_last([(dwg_ref, dwg_out)])

    return _run(
        body, comm, name="s5_bwd", grid=(steps,),
        out_shape=[jax.ShapeDtypeStruct((t, sa), F32), jax.ShapeDtypeStruct((t, gn), BF16),
                   jax.ShapeDtypeStruct((t, gn), BF16), jax.ShapeDtypeStruct((t, sa), BF16),
                   jax.ShapeDtypeStruct((sa, sa), BF16), jax.ShapeDtypeStruct((SUBLANES, sa), F32),
                   jax.ShapeDtypeStruct((SUBLANES, gn), F32)],
        in_specs=[_rows_rev(tc, sa, steps), _rows_rev(tc, sa, steps), _rows_rev(tc, gn, steps), _rows_rev(tc, gn, steps),
                  _rows_rev(tc, sa, steps), _whole((sa, sa)), _whole((1, sa)), _whole((sa, gn)), _whole((sa, gn)),
                  _whole((gn, sa)), _whole((gn, sa)), _whole((1, gn)), _whole((1, gn)), _whole((1, sa))],
        out_specs=[_rows_rev(tc, sa, steps), _rows_rev(tc, gn, steps), _rows_rev(tc, gn, steps), _rows_rev(tc, sa, steps),
                   _acc((sa, sa)), _acc((SUBLANES, sa)), _acc((SUBLANES, gn))],
        scratch_shapes=[pltpu.VMEM((tc + halo, gn), F32), pltpu.VMEM((tc + halo, gn), F32),
                        pltpu.VMEM((SUBLANES, gn), F32), pltpu.VMEM((SUBLANES, gn), F32), pltpu.VMEM((sa, sa), F32)],
        semantics="arbitrary",
    )(dya, y, sr, si, u, w_glu, b_glu, cre_blk, cimn_blk, bbr_blk, bbi_blk, ar, ai, d_skip)


def _inproj_bwd(dparts, x, dx1, g_mix, w_in_t, comm=None):
    t, d = x.shape
    n = w_in_t.shape[0]
    widths = [p.shape[1] for p in dparts]
    offs = [sum(widths[:i]) for i in range(len(widths) + 1)]
    tm = min(TOKEN_TILE, t)
    np_ = len(dparts)

    def body(*refs):
        dz_refs = refs[:np_]
        x_ref, dx1_ref, g_ref, w_ref, gx_ref, h_ref, dz_ref, vd_ref, vn_ref = refs[np_:]
        _zero_on_first(vd_ref, vn_ref)
        dh = jnp.zeros((tm, d), F32)
        for k, r in enumerate(dz_refs):
            lo, hi = offs[k], offs[k + 1]
            dzk = r[...]
            dh = dh + _dot(dzk, w_ref[lo:hi, :])
            dz_ref[:, lo:hi] = dzk.astype(dz_ref.dtype)
            vn_ref[0:1, lo:hi] += _rowsum(dzk)
        xhat, r0 = _rms_stats(x_ref[...])
        h_ref[...] = (xhat * g_ref[...]).astype(h_ref.dtype)
        dxn, dg = _rms_bwd(dh, xhat, r0, g_ref[...])
        gx_ref[...] = dx1_ref[...] + dxn
        vd_ref[0:1, :] += _rowsum(dg)

    return _run(
        body, comm, name="inproj_bwd", grid=(t // tm,),
        out_shape=[jax.ShapeDtypeStruct((t, d), F32), jax.ShapeDtypeStruct((t, d), BF16),
                   jax.ShapeDtypeStruct((t, n), BF16), jax.ShapeDtypeStruct((SUBLANES, d), F32),
                   jax.ShapeDtypeStruct((SUBLANES, n), F32)],
        in_specs=[_rows(tm, w) for w in widths] + [_rows(tm, d), _rows(tm, d), _whole((1, d)), _whole((n, d))],
        out_specs=[_rows(tm, d), _rows(tm, d), _rows(tm, n), _acc((SUBLANES, d)), _acc((SUBLANES, n))],
        semantics="arbitrary",
    )(*dparts, x, dx1, g_mix, w_in_t)


def _prep(lr_row, li_row, ldt_row, lr_col, li_col, ldt_col, b_re, b_im, c_re, c_im, w_r, w_i, groups, heads):
    gn, pch = b_re.shape
    sa, n = c_re.shape
    w, hd = w_r.shape

    def expand(vals, row_group, col_group, width):
        r, k = vals.shape
        tile = (lax.broadcasted_iota(jnp.int32, (k, width), 0) == lax.broadcasted_iota(jnp.int32, (k, width), 1) % k)
        rows = lax.broadcasted_iota(jnp.int32, (r, width), 0) // row_group
        cols = lax.broadcasted_iota(jnp.int32, (r, width), 1) // col_group
        return jnp.where(rows == cols, _dot(vals, tile.astype(BF16)), 0.0)

    def body(lrr, lir, ldr, lrc, lic, ldc, bre, bim, cre, cim, wr, wi,
             ar_o, ai_o, bbr_o, bbi_o, cre_o, cim_o, wr_o, wi_o):
        ar, ai, _, _ = _disc_scalars(lrr[...], lir[...], ldr[...])
        ar_o[...] = ar
        ai_o[...] = ai
        _, _, bbr, bbi = _disc_cols(lrc[...], lic[...], ldc[...], bre[...], bim[...])
        bbr_o[...] = expand(bbr, n, pch, sa).astype(bbr_o.dtype)
        bbi_o[...] = expand(bbi, n, pch, sa).astype(bbi_o.dtype)
        cre_o[...] = expand(cre[...], pch, n, gn).astype(cre_o.dtype)
        cim_o[...] = expand(-cim[...], pch, n, gn).astype(cim_o.dtype)
        wr_o[...] = expand(wr[...], hd, hd, w).astype(wr_o.dtype)
        wi_o[...] = expand(wi[...], hd, hd, w).astype(wi_o.dtype)

    return pl.pallas_call(
        body, name="prep",
        out_shape=[jax.ShapeDtypeStruct((1, gn), F32), jax.ShapeDtypeStruct((1, gn), F32),
                   jax.ShapeDtypeStruct((gn, sa), BF16), jax.ShapeDtypeStruct((gn, sa), BF16),
                   jax.ShapeDtypeStruct((sa, gn), BF16), jax.ShapeDtypeStruct((sa, gn), BF16),
                   jax.ShapeDtypeStruct((w, w), BF16), jax.ShapeDtypeStruct((w, w), BF16)],
        compiler_params=pltpu.CompilerParams(vmem_limit_bytes=VMEM_LIMIT),
    )(lr_row, li_row, ldt_row, lr_col, li_col, ldt_col, b_re, b_im, c_re, c_im, w_r, w_i)


def _s5_param_grads(lam_r, lam_i, sr, si, u, dy, pch, n, comm=None):
    t, gn = lam_r.shape
    sa = u.shape[1]
    tb = min(gn, 512)

    def body(lr_ref, li_ref, sr_ref, si_ref, u_ref, dy_ref, dbr_ref, dbi_ref, dcr_ref, dci_ref):
        _zero_on_first(dcr_ref, dci_ref)
        base = pl.program_id(0) * tb
        uv = u_ref[...]
        dyv = dy_ref[...]
        dbr_ref[...] = _fold_diag_blocks(_dot_tn(lr_ref[...], uv), n, pch, row0=base)
        dbi_ref[...] = _fold_diag_blocks(_dot_tn(li_ref[...], uv), n, pch, row0=base)
        dcr_ref[...] += _fold_diag_blocks(_dot_tn(dyv, sr_ref[...]), pch, n, col0=base)
        dci_ref[...] += _fold_diag_blocks(_dot_tn(dyv, si_ref[...]), pch, n, col0=base)

    cols = pl.BlockSpec((t, tb), lambda j: (0, j))
    return _run(
        body, comm, name="s5_param_grads", grid=(gn // tb,),
        out_shape=[jax.ShapeDtypeStruct((gn, pch), F32), jax.ShapeDtypeStruct((gn, pch), F32),
                   jax.ShapeDtypeStruct((sa, n), F32), jax.ShapeDtypeStruct((sa, n), F32)],
        in_specs=[cols, cols, cols, cols, _whole((t, sa)), _whole((t, sa))],
        out_specs=[pl.BlockSpec((tb, pch), lambda j: (j, 0)), pl.BlockSpec((tb, pch), lambda j: (j, 0)),
                   _acc((sa, n)), _acc((sa, n))],
        semantics="arbitrary",
    )(lam_r, lam_i, sr, si, u, dy)


SMALL_PARTS = ["vec_tail", "vec_ffn", "vec_lru", "vec_mix", "vec_bin", "vec_sa", "vec_gn", "dw_r", "dw_i", "dbb_re",
               "dbb_im", "dc_re", "dc_imn", "loss"]


def _small_reduce(parts, shapes, lr_col, li_col, ldt_col, b_re, b_im, groups):
    gn, pch = b_re.shape
    n = gn // groups
    nparts = parts[SMALL_PARTS[0]].size // math.prod(shapes[SMALL_PARTS[0]])
    np_, nout = len(SMALL_PARTS), 23

    def body(*refs):
        ins = refs[:np_]
        lr, li, ldt, bre, bim = refs[np_:np_ + 5]
        outs = refs[np_ + 5:np_ + 5 + nout]
        sums = dict(zip(SMALL_PARTS, refs[np_ + 5 + nout:]))

        @pl.when(pl.program_id(0) == 0)
        def _():
            for k, r in zip(SMALL_PARTS, ins):
                sums[k][...] = r[...]

        @pl.when(pl.program_id(0) > 0)
        def _():
            for k, r in zip(SMALL_PARTS, ins):
                sums[k][...] += r[...]

        @pl.when(pl.program_id(0) == nparts - 1)
        def _():
            finish({k: s[...] for k, s in sums.items()}, lr, li, ldt, bre, bim, *outs)

    def finish(tot, lr, li, ldt, bre, bim, o_loss, o_gmix, o_bin, o_bglu, o_s5d, o_convb, o_br, o_bi, o_lam, o_gffn,
               o_gpg, o_bpg, o_gple, o_gfin, o_wr, o_wi, o_cre, o_cim, o_bre, o_bim, o_lre, o_lim, o_ldt):
        o_loss[...] = tot["loss"]
        o_bpg[...] = tot["vec_tail"][0:1]
        o_gpg[...] = tot["vec_tail"][1:2]
        o_gple[...] = tot["vec_tail"][2:3]
        o_gfin[...] = tot["vec_tail"][3:4]
        o_gffn[...] = tot["vec_ffn"][0:1]
        o_convb[...] = tot["vec_lru"][0:1]
        o_br[...] = tot["vec_lru"][1:2]
        o_bi[...] = tot["vec_lru"][2:3]
        o_lam[...] = tot["vec_lru"][3:4]
        o_gmix[...] = tot["vec_mix"][0:1]
        o_bin[...] = tot["vec_bin"][0:1]
        o_bglu[...] = tot["vec_sa"][0:1]
        o_s5d[...] = tot["vec_sa"][1:2]
        o_wr[...] = tot["dw_r"]
        o_wi[...] = tot["dw_i"]
        o_cre[...] = tot["dc_re"]
        o_cim[...] = -tot["dc_imn"]
        d_a = tot["vec_gn"].T
        _, chain = jax.vjp(_disc_cols, lr[...], li[...], ldt[...], bre[...], bim[...])
        d_lr, d_li, d_ldt, d_bre, d_bim = chain((d_a[:, 0:1], d_a[:, 1:2], tot["dbb_re"], tot["dbb_im"]))
        o_lre[...] = d_lr
        o_lim[...] = d_li
        o_bre[...] = d_bre
        o_bim[...] = d_bim
        same = (lax.broadcasted_iota(jnp.int32, (groups, gn), 0)
                == lax.broadcasted_iota(jnp.int32, (groups, gn), 1) // n).astype(F32)
        o_ldt[...] = jnp.dot(same, d_ldt * jnp.ones((1, LANES), F32), preferred_element_type=F32,
                             precision=lax.Precision.HIGHEST)[:, 0:1]

    d = shapes["vec_mix"][1]
    nz = shapes["vec_bin"][1]
    sa = shapes["vec_sa"][1]
    w = shapes["vec_lru"][1]
    row = lambda c: jax.ShapeDtypeStruct((1, c), F32)
    out_shape = [jax.ShapeDtypeStruct(shapes["loss"], F32), row(d), row(nz), row(sa), row(sa), row(w), row(w), row(w),
                 row(w), row(d), row(d), row(d), row(d), row(d),
                 jax.ShapeDtypeStruct(shapes["dw_r"], F32), jax.ShapeDtypeStruct(shapes["dw_i"], F32),
                 jax.ShapeDtypeStruct(shapes["dc_re"], F32), jax.ShapeDtypeStruct(shapes["dc_imn"], F32),
                 jax.ShapeDtypeStruct((gn, pch), F32), jax.ShapeDtypeStruct((gn, pch), F32),
                 jax.ShapeDtypeStruct((gn, 1), F32), jax.ShapeDtypeStruct((gn, 1), F32),
                 jax.ShapeDtypeStruct((groups, 1), F32)]
    def part_spec(k):
        r, c = shapes[k]
        if parts[k].ndim == 3:
            return pl.BlockSpec((None, r, c), lambda i: (i, 0, 0))
        return pl.BlockSpec((r, c), lambda i: (i, 0))

    outs = pl.pallas_call(
        body, name="small_reduce", grid=(nparts,), out_shape=out_shape,
        in_specs=[part_spec(k) for k in SMALL_PARTS] + [_whole(a.shape) for a in (lr_col, li_col, ldt_col, b_re, b_im)],
        out_specs=[_acc(s.shape) for s in out_shape],
        scratch_shapes=[pltpu.VMEM(shapes[k], F32) for k in SMALL_PARTS],
        compiler_params=_params("arbitrary"),
    )(*[parts[k] for k in SMALL_PARTS], lr_col, li_col, ldt_col, b_re, b_im)
    names = ["loss", "g_mix", "b_in", "b_glu", "s5_d", "conv_b", "b_r", "b_i", "lru_lambda", "g_ffn", "g_ple_gate",
             "b_ple_gate", "g_ple", "g_final", "w_r", "w_i", "s5_c_re", "s5_c_im", "s5_b_re", "s5_b_im", "lam_re",
             "lam_im", "log_dt"]
    return dict(zip(names, outs))


def _adamw_small(ws, gs, ms, vs):
    n = len(ws)

    def body(*refs):
        w_r, g_r, m_r, v_r = (refs[i * n:(i + 1) * n] for i in range(4))
        d_o, m_o, v_o = (refs[(4 + i) * n:(5 + i) * n] for i in range(3))
        for i in range(n):
            delta, m_new, v_new = _adamw_math(w_r[i][...], g_r[i][...], m_r[i][...], v_r[i][...])
            d_o[i][...] = delta
            m_o[i][...] = m_new
            v_o[i][...] = v_new

    shapes = [jax.ShapeDtypeStruct(a.shape, F32) for a in ws]
    outs = pl.pallas_call(body, name="adamw_small", out_shape=shapes * 3)(*ws, *gs, *ms, *vs)
    return outs[:n], outs[n:2 * n], outs[2 * n:]


def _adamw_math(w, g, m, v):
    m_new = ADAM_B1 * m + (1.0 - ADAM_B1) * g
    v_new = ADAM_B2 * v + (1.0 - ADAM_B2) * (g * g)
    m_hat = m_new / (1.0 - ADAM_B1 ** ADAM_STEP)
    v_hat = v_new / (1.0 - ADAM_B2 ** ADAM_STEP)
    delta = -ADAM_LR * (m_hat / (jnp.sqrt(v_hat) + ADAM_EPS) + ADAM_WD * w)
    return delta, m_new, v_new


def _row_tile(rows):
    for cand in (256, 128, 64, 32, 16, 8):
        if rows % cand == 0:
            return cand
    return rows


def _adamw(parts, w, m, v, name, transposed=False):
    rows, cols = w.shape
    npart = parts.shape[0]
    tr = _row_tile(rows)
    if transposed:
        parts_spec = pl.BlockSpec((npart, cols, tr), lambda i: (0, 0, i))
    else:
        parts_spec = pl.BlockSpec((npart, tr, cols), lambda i: (0, i, 0))

    def body(p_ref, w_ref, m_ref, v_ref, g_ref, d_ref, mo_ref, vo_ref):
        g = p_ref[0].astype(F32)
        for k in range(1, npart):
            g = g + p_ref[k].astype(F32)
        if transposed:
            g = g.T
        delta, m_new, v_new = _adamw_math(w_ref[...], g, m_ref[...], v_ref[...])
        g_ref[...] = g
        d_ref[...] = delta
        mo_ref[...] = m_new
        vo_ref[...] = v_new

    return pl.pallas_call(
        body, name=name, grid=(rows // tr,),
        out_shape=[jax.ShapeDtypeStruct((rows, cols), F32)] * 4,
        in_specs=[parts_spec] + [_rows(tr, cols)] * 3,
        out_specs=[_rows(tr, cols)] * 4,
        compiler_params=_params("parallel"),
    )(parts, w, m, v)


def _mesh_position():
    return lax.axis_index("x"), lax.axis_index("y"), lax.axis_index("c")


def _flip(pos, rel):
    x, y, c = pos
    return (1 - x if rel & 4 else x, 1 - y if rel & 2 else y, 1 - c if rel & 1 else c)


def _index(pos):
    return 4 * pos[0] + 2 * pos[1] + pos[2]


_ANY = pl.BlockSpec(memory_space=pl.ANY)
FLAT_ROWS = 32


def _dma_sems(n):
    return [pltpu.SemaphoreType.DMA((n, N_DEV - 1)), pltpu.SemaphoreType.DMA((n, N_DEV - 1)), pltpu.SemaphoreType.DMA((n,))]


def _block_of(ref, idx, rows, flat):
    if flat:
        return ref.at[pl.ds(pl.multiple_of(idx * rows, FLAT_ROWS), rows), :]
    return ref.at[idx]


class _Gather:
    chips = (4, 2, 6)

    def __init__(self, shards):
        self.inputs = list(shards)
        self.flat = [s.shape[0] % FLAT_ROWS == 0 for s in shards]
        self.out_shape = [
            jax.ShapeDtypeStruct((N_DEV * s.shape[0], s.shape[1]) if f else (N_DEV,) + s.shape, s.dtype)
            for s, f in zip(shards, self.flat)]
        self.sems = _dma_sems(len(shards))

    def _copy(self, ins, outs, sems, i, k, block, to, own=False):
        dst = _block_of(outs[i], _index(block), self.inputs[i].shape[0], self.flat[i])
        return pltpu.make_async_remote_copy(
            src_ref=ins[i] if own else dst, dst_ref=dst, send_sem=sems[0].at[i, k], recv_sem=sems[1].at[i, k],
            device_id=to, device_id_type=MESH)

    def _local(self, ins, outs, sems, i, me):
        dst = _block_of(outs[i], _index(me), self.inputs[i].shape[0], self.flat[i])
        return pltpu.make_async_copy(ins[i], dst, sems[2].at[i])

    def _first(self, ins, outs, sems, i, me):
        cps = [self._copy(ins, outs, sems, i, 0, me, _flip(me, 1), own=True)]
        cps += [self._copy(ins, outs, sems, i, 1 + j, me, _flip(me, rel), own=True) for j, rel in enumerate(self.chips)]
        return cps

    def _passed(self, ins, outs, sems, i, j, me):
        return self._copy(ins, outs, sems, i, 4 + j, _flip(me, self.chips[j]), _flip(me, 1))

    def before(self, ins, outs, sems):
        n = len(self.inputs)
        me = _mesh_position()

        @pl.when(pl.program_id(0) == 0)
        def _():
            for i in range(n):
                self._local(ins, outs, sems, i, me).start()
                for cp in self._first(ins, outs, sems, i, me):
                    cp.start()

        @pl.when(pl.program_id(0) == pl.num_programs(0) - 1)
        def _():
            for j, rel in enumerate(self.chips):
                for i in range(n):
                    self._copy(ins, outs, sems, i, 1 + j, _flip(me, rel), me).wait_recv()
                    self._passed(ins, outs, sems, i, j, me).start()

    def after(self, ins, outs, sems):
        n = len(self.inputs)
        me = _mesh_position()
        sibling = _flip(me, 1)

        @pl.when(pl.program_id(0) == pl.num_programs(0) - 1)
        def _():
            for i in range(n):
                self._copy(ins, outs, sems, i, 0, sibling, me).wait_recv()
                for j, rel in enumerate(self.chips):
                    self._copy(ins, outs, sems, i, 4 + j, _flip(sibling, rel), me).wait_recv()
            for i in range(n):
                for cp in self._first(ins, outs, sems, i, me):
                    cp.wait_send()
                for j in range(len(self.chips)):
                    self._passed(ins, outs, sems, i, j, me).wait_send()
                self._local(ins, outs, sems, i, me).wait()


class _Exchange:
    def __init__(self, arrays):
        self.inputs = list(arrays)
        self.flat = [a.ndim == 2 for a in arrays]
        self.block = [(a.shape[0] // N_DEV, a.shape[1]) if a.ndim == 2 else a.shape[1:] for a in arrays]
        for f, b in zip(self.flat, self.block):
            assert not f or b[0] % FLAT_ROWS == 0, b
        self.out_shape = [jax.ShapeDtypeStruct((N_DEV,) + tuple(b), a.dtype) for a, b in zip(arrays, self.block)]
        self.sems = _dma_sems(len(arrays))

    def _send(self, ins, outs, sems, i, rel, me):
        peer = _flip(me, rel)
        return pltpu.make_async_remote_copy(
            src_ref=_block_of(ins[i], _index(peer), self.block[i][0], self.flat[i]), dst_ref=outs[i].at[_index(me)],
            send_sem=sems[0].at[i, rel - 1], recv_sem=sems[1].at[i, rel - 1], device_id=peer, device_id_type=MESH)

    def _arrival(self, ins, outs, sems, i, rel, me):
        peer = _flip(me, rel)
        return pltpu.make_async_remote_copy(
            src_ref=_block_of(ins[i], _index(me), self.block[i][0], self.flat[i]), dst_ref=outs[i].at[_index(peer)],
            send_sem=sems[0].at[i, rel - 1], recv_sem=sems[1].at[i, rel - 1], device_id=peer, device_id_type=MESH)

    def _local(self, ins, outs, sems, i, me):
        return pltpu.make_async_copy(_block_of(ins[i], _index(me), self.block[i][0], self.flat[i]),
                                     outs[i].at[_index(me)], sems[2].at[i])

    def before(self, ins, outs, sems):
        me = _mesh_position()

        @pl.when(pl.program_id(0) == 0)
        def _():
            for i in range(len(self.inputs)):
                self._local(ins, outs, sems, i, me).start()
                for rel in range(1, N_DEV):
                    self._send(ins, outs, sems, i, rel, me).start()

    def after(self, ins, outs, sems):
        me = _mesh_position()

        @pl.when(pl.program_id(0) == pl.num_programs(0) - 1)
        def _():
            for i in range(len(self.inputs)):
                for rel in range(1, N_DEV):
                    self._arrival(ins, outs, sems, i, rel, me).wait_recv()
            for i in range(len(self.inputs)):
                for rel in range(1, N_DEV):
                    self._send(ins, outs, sems, i, rel, me).wait_send()
                self._local(ins, outs, sems, i, me).wait()


class _Both:
    def __init__(self, first, second):
        self.jobs = (first, second)
        self.inputs = first.inputs + second.inputs
        self.out_shape = first.out_shape + second.out_shape
        self.sems = first.sems + second.sems

    def _each(self, ins, outs, sems):
        a = self.jobs[0]
        i, o, s = len(a.inputs), len(a.out_shape), len(a.sems)
        return ((a, ins[:i], outs[:o], sems[:s]), (self.jobs[1], ins[i:], outs[o:], sems[s:]))

    def before(self, ins, outs, sems):
        for job, i, o, s in self._each(ins, outs, sems):
            job.before(i, o, s)

    def after(self, ins, outs, sems):
        for job, i, o, s in self._each(ins, outs, sems):
            job.after(i, o, s)


def _run(body, comm, *, semantics, out_shape, in_specs, out_specs, scratch_shapes=(), **kw):
    if comm is None:
        return pl.pallas_call(body, out_shape=out_shape, in_specs=in_specs, out_specs=out_specs,
                              scratch_shapes=list(scratch_shapes), compiler_params=_params(semantics), **kw)
    single = not isinstance(out_shape, (list, tuple))
    outs = [out_shape] if single else list(out_shape)
    ospecs = [out_specs] if single else list(out_specs)
    counts = [len(in_specs), len(comm.inputs), len(outs), len(comm.out_shape), len(scratch_shapes), len(comm.sems)]

    def carrying(*refs):
        groups, pos = [], 0
        for c in counts:
            groups.append(refs[pos:pos + c])
            pos += c
        main_in, comm_in, main_out, comm_out, main_scratch, comm_sems = groups
        comm.before(comm_in, comm_out, comm_sems)
        body(*main_in, *main_out, *main_scratch)
        comm.after(comm_in, comm_out, comm_sems)

    call = pl.pallas_call(
        carrying, out_shape=outs + list(comm.out_shape), in_specs=list(in_specs) + [_ANY] * len(comm.inputs),
        out_specs=ospecs + [_ANY] * len(comm.out_shape), scratch_shapes=list(scratch_shapes) + list(comm.sems),
        compiler_params=_params("arbitrary"), **kw)

    def apply(*args):
        res = call(*args, *comm.inputs)
        main = res[:len(outs)]
        return (main[0] if single else list(main)), list(res[len(outs):])

    return apply


def _alone(comm, name):
    return _run(lambda: None, comm, semantics="arbitrary", name=name, grid=(1,), out_shape=[], in_specs=[], out_specs=[])()[1]


SHARDED = {"w_in": 1, "w_glu": 0, "conv_w": 1, "w_a_out": 1, "w_b_out": 0, "w_o": 0, "w_ffn_gate": 1, "w_ffn_up": 1,
           "w_ffn_down": 0, "w_ple_gate": 0, "w_ple": 1}
TRANSPOSED = ("w_in", "w_a_out", "w_ffn_gate", "w_ffn_up", "w_ple")
SMALL = ["g_mix", "b_in", "lam_re", "lam_im", "log_dt", "s5_b_re", "s5_b_im", "s5_c_re", "s5_c_im", "s5_d", "b_glu",
         "conv_b", "w_r", "b_r", "w_i", "b_i", "lru_lambda", "g_ffn", "g_ple_gate", "b_ple_gate", "g_ple", "g_final"]
WEIGHTS = ["g_mix", "w_in", "b_in", "lam_re", "lam_im", "log_dt", "s5_b_re", "s5_b_im", "s5_c_re", "s5_c_im", "s5_d",
           "w_glu", "b_glu", "conv_w", "conv_b", "w_r", "b_r", "w_i", "b_i", "lru_lambda", "w_a_out", "w_b_out", "w_o",
           "g_ffn", "w_ffn_gate", "w_ffn_up", "w_ffn_down", "g_ple_gate", "w_ple_gate", "b_ple_gate", "w_ple", "g_ple",
           "g_final"]


def _unblock(gathered, axis):
    nb, r, c = gathered.shape
    if axis == 0:
        return gathered.reshape(nb * r, c)
    return jnp.transpose(gathered, (1, 0, 2)).reshape(r, nb * c)


def _block(full, axis):
    r, c = full.shape
    if axis == 0:
        return full.reshape(N_DEV, r // N_DEV, c)
    return jnp.transpose(full.reshape(r, N_DEV, c // N_DEV), (1, 0, 2))


def _wire_grad(name, grad):
    return _block(grad, 1) if name == "conv_w" else grad


def _disc_scalars(lr, li, ldt):
    dt = jnp.exp(ldt)
    mag = jnp.exp(lr * dt)
    ar = mag * jnp.cos(li * dt)
    ai = mag * jnp.sin(li * dt)
    den = lr * lr + li * li
    nr = ar - 1.0
    fr = (nr * lr + ai * li) / den
    fi = (ai * lr - nr * li) / den
    return ar, ai, fr, fi


def _disc_cols(lr, li, ldt, b_re, b_im):
    ar, ai, fr, fi = _disc_scalars(lr, li, ldt)
    return ar, ai, fr * b_re - fi * b_im, fr * b_im + fi * b_re


def _local_step(x, p, target, src, small, disc, distributed=True):
    full = {} if distributed else dict(src)
    got = {}

    def gather(keys):
        return _Gather([src[k] for k in keys]) if distributed else None

    def exchange(keys, grads):
        return _Exchange([_wire_grad(k, grads[k]) for k in keys]) if distributed else None

    def carry(fn, *args, job=None, keys=(), sink=None):
        if job is None:
            return fn(*args)
        res, extra = fn(*args, comm=job)
        sink.update(zip(keys, extra))
        return res

    if distributed:
        first = ["w_in", "w_glu", "conv_w"]
        for k, a in zip(first, _alone(gather(first), "gather_first")):
            full[k] = _unblock(a, 1) if k == "conv_w" else a

    d = x.shape[1]
    g, n, pch = small["s5_b_re"].shape
    heads = small["w_r"].shape[0]
    sa, lw = g * pch, small["lru_lambda"].shape[-1]
    widths = [sa, lw, d, d]
    row = lambda v: v.reshape(1, -1)

    hd = small["w_r"].shape[-1]
    ar_row, ai_row, bbr_blk, bbi_blk, cre_blk, cimn_blk, wr_blk, wi_blk = _prep(
        *disc["rows"], *disc["cols"], disc["b_re"], disc["b_im"], small["s5_c_re"].reshape(sa, n),
        small["s5_c_im"].reshape(sa, n), small["w_r"].reshape(lw, hd), small["w_i"].reshape(lw, hd), g, heads)
    d_row = row(small["s5_d"])
    conv_w = full["conv_w"].astype(F32)

    keys = ["w_a_out", "w_b_out", "w_o"]
    u_a, u_b, za, zb = carry(_inproj_fwd, x, row(small["g_mix"]), full["w_in"], row(small["b_in"]), widths,
                             job=gather(keys), keys=keys, sink=full)
    keys = ["w_ffn_gate"]
    sr, si, y, y_a = carry(_s5_fwd, u_a, bbr_blk, bbi_blk, ar_row, ai_row, cre_blk, cimn_blk, d_row, full["w_glu"],
                           row(small["b_glu"]), job=gather(keys), keys=keys, sink=full)
    keys = ["w_ffn_up"]
    xc, h, hprev = carry(_lru_fwd, u_b, conv_w, row(small["conv_b"]), wr_blk, row(small["b_r"]), wi_blk,
                         row(small["b_i"]), row(small["lru_lambda"]), job=gather(keys), keys=keys, sink=full)
    keys = ["w_ffn_down"]
    x1, merged, ma, mb = carry(_merge_fwd, y_a, h, za, zb, x, full["w_a_out"], full["w_b_out"], full["w_o"],
                               job=gather(keys), keys=keys, sink=full)
    keys = ["w_ple_gate", "w_ple"]
    fg, fu = carry(_ffn_up_fwd, x1, row(small["g_ffn"]), full["w_ffn_gate"], full["w_ffn_up"],
                   job=gather(keys), keys=keys, sink=full)
    x2 = _ffn_down_fwd(fg, fu, x1, full["w_ffn_down"])

    gw = {}
    dx2, loss_blk, gw["w_ple_gate"], gw["w_ple"], vec_tail = _tail_fwd_bwd(
        x2, p, target, row(small["g_ple_gate"]), full["w_ple_gate"], row(small["b_ple_gate"]), full["w_ple"],
        row(small["g_ple"]), row(small["g_final"]))
    keys = ["w_ple_gate", "w_ple"]
    dfg, dfu, act = carry(_ffn_bwd_a, dx2, fg, fu, full["w_ffn_down"], job=exchange(keys, gw), keys=keys, sink=got)
    tn_d = min(d, 512)
    gw["w_ffn_down"] = _matmul_tn(act, dx2, tn_d, "dw_ffn_down", BF16)
    dx1, h2, vec_ffn = _ffn_bwd_b(dfg, dfu, x1, dx2, row(small["g_ffn"]), full["w_ffn_gate"], full["w_ffn_up"])
    gw["w_ffn_gate"] = _matmul_tn(dfg, h2, tn_d, "dw_ffn_gate", BF16)
    gw["w_ffn_up"] = _matmul_tn(dfu, h2, tn_d, "dw_ffn_up", BF16)
    keys = ["w_ffn_down"]
    dza, dzb, dya, dyb, gw["w_o"], gw["w_a_out"], gw["w_b_out"] = carry(
        _merge_bwd, dx1, merged, ma, mb, za, zb, y_a, h, full["w_o"], full["w_a_out"], full["w_b_out"],
        job=exchange(keys, gw), keys=keys, sink=got)
    keys = ["w_ffn_gate"]
    du_b, dw_r, dw_i, vec_lru = carry(
        _lru_bwd, dyb, xc, hprev, u_b, conv_w, wr_blk, row(small["b_r"]), wi_blk, row(small["b_i"]),
        row(small["lru_lambda"]), hd, job=exchange(keys, gw), keys=keys, sink=got)
    gw["conv_w"] = vec_lru[4:4 + CONV_WIDTH]
    keys = ["w_ffn_up"]
    du_a, lam_r, lam_i, dy16, gw["w_glu"], vec_sa, vec_gn = carry(
        _s5_bwd, dya, y, sr, si, u_a, full["w_glu"], row(small["b_glu"]), cre_blk, cimn_blk, bbr_blk, bbi_blk, ar_row,
        ai_row, d_row, job=exchange(keys, gw), keys=keys, sink=got)
    dbb_re, dbb_im, dc_re, dc_imn = _s5_param_grads(lam_r, lam_i, sr, si, u_a, dy16, pch, n)
    smalls = {"vec_tail": vec_tail, "vec_ffn": vec_ffn, "vec_lru": vec_lru, "vec_sa": vec_sa, "vec_gn": vec_gn,
              "dw_r": dw_r, "dw_i": dw_i, "dbb_re": dbb_re, "dbb_im": dbb_im, "dc_re": dc_re, "dc_imn": dc_imn,
              "loss": loss_blk}
    shapes = {k: a.shape for k, a in smalls.items()}
    keys = ["w_o", "w_a_out", "w_b_out"]
    early = list(smalls)
    job = _Both(exchange(keys, gw), _Gather([smalls[k] for k in early])) if distributed else None
    grad_x, h0, dz16, smalls["vec_mix"], smalls["vec_bin"] = carry(
        _inproj_bwd, [du_a, du_b, dza, dzb], x, dx1, row(small["g_mix"]), full["w_in"],
        job=job, keys=keys + early, sink=got)
    shapes.update(vec_mix=smalls["vec_mix"].shape, vec_bin=smalls["vec_bin"].shape)
    gw["w_in"] = _matmul_tn(dz16, h0, tn_d, "dw_in", BF16)
    if distributed:
        smalls.update({k: got.pop(k) for k in early})
        got.update({k: gw[k] for k in ("w_in", "w_glu", "conv_w")})
        gw = got
    return grad_x, gw, smalls, shapes


def _disc_inputs(small):
    g, n, pch = small["s5_b_re"].shape
    srcs = (small["lam_re"], small["lam_im"], jnp.repeat(small["log_dt"], n))
    return {"rows": [a.reshape(1, g * n) for a in srcs], "cols": [a.reshape(g * n, 1) for a in srcs],
            "b_re": small["s5_b_re"].reshape(g * n, pch), "b_im": small["s5_b_im"].reshape(g * n, pch)}


def kernel(x, p, g_mix, w_in, b_in, lam_re, lam_im, log_dt, s5_b_re, s5_b_im, s5_c_re, s5_c_im, s5_d, w_glu, b_glu, conv_w, conv_b, w_r, b_r, w_i, b_i, lru_lambda, w_a_out, w_b_out, w_o, g_ffn, w_ffn_gate, w_ffn_up, w_ffn_down, g_ple_gate, w_ple_gate, b_ple_gate, w_ple, g_ple, g_final, loss_target, m_g_mix, m_w_in, m_b_in, m_lam_re, m_lam_im, m_log_dt, m_s5_b_re, m_s5_b_im, m_s5_c_re, m_s5_c_im, m_s5_d, m_w_glu, m_b_glu, m_conv_w, m_conv_b, m_w_r, m_b_r, m_w_i, m_b_i, m_lru_lambda, m_w_a_out, m_w_b_out, m_w_o, m_g_ffn, m_w_ffn_gate, m_w_ffn_up, m_w_ffn_down, m_g_ple_gate, m_w_ple_gate, m_b_ple_gate, m_w_ple, m_g_ple, m_g_final, v_g_mix, v_w_in, v_b_in, v_lam_re, v_lam_im, v_log_dt, v_s5_b_re, v_s5_b_im, v_s5_c_re, v_s5_c_im, v_s5_d, v_w_glu, v_b_glu, v_conv_w, v_conv_b, v_w_r, v_b_r, v_w_i, v_b_i, v_lru_lambda, v_w_a_out, v_w_b_out, v_w_o, v_g_ffn, v_w_ffn_gate, v_w_ffn_up, v_w_ffn_down, v_g_ple_gate, v_w_ple_gate, v_b_ple_gate, v_w_ple, v_g_ple, v_g_final):
    given = dict(locals())
    wts = {k: given[k] for k in WEIGHTS}
    moms = {k: given["m_" + k] for k in WEIGHTS}
    vels = {k: given["v_" + k] for k in WEIGHTS}

    def drop_depth(k, a):
        return a if k == "g_final" else a[0]

    small = {k: drop_depth(k, wts[k]) for k in SMALL}
    shard = {k: wts[k][0] for k in SHARDED}
    names = list(SHARDED)

    def wire(k):
        if k == "conv_w":
            return shard[k]
        return (shard[k].T if k in TRANSPOSED else shard[k]).astype(BF16)

    disc = _disc_inputs(small)
    grad_x, parts, smalls, shapes = _local_step(x[0], p[0, 0], loss_target[0], {k: wire(k) for k in names}, small, disc)

    last = ["w_in", "w_glu", "conv_w"]
    late = ["vec_mix", "vec_bin"]
    received = _alone(_Both(_Exchange([_wire_grad(k, parts[k]) for k in last]), _Gather([smalls[k] for k in late])),
                      "exchange_last")
    parts.update(zip(last, received[:len(last)]))
    smalls.update(zip(late, received[len(last):]))

    g_small = _small_reduce(smalls, shapes, *disc["cols"], disc["b_re"], disc["b_im"], small["s5_b_re"].shape[0])
    loss = g_small.pop("loss")[0, 0]
    natural = lambda k, a: a.reshape((1, -1) if k == "g_final" else wts[k].shape)
    small_g = [natural(k, g_small[k]) for k in SMALL]
    deltas, new_m, new_v = _adamw_small([natural(k, wts[k]) for k in SMALL], small_g,
                                        [natural(k, moms[k]) for k in SMALL], [natural(k, vels[k]) for k in SMALL])
    small_out = [dict(zip(SMALL, [a.reshape(wts[k].shape) for k, a in zip(SMALL, slot)]))
                 for slot in (small_g, deltas, new_m, new_v)]

    big_out = {}
    for k in names:
        big_out[k] = _adamw(parts[k], shard[k], moms[k][0], vels[k][0], "adamw_" + k, transposed=k in TRANSPOSED)

    outs = [loss, grad_x[None]]
    for slot in range(4):
        for k in WEIGHTS:
            if k in SHARDED:
                outs.append(big_out[k][slot][None])
            else:
                outs.append(small_out[slot][k])
    return tuple(outs)
```

```python
import math

import jax
import jax.numpy as jnp
from jax import lax
from jax.experimental import pallas as pl
from jax.experimental.pallas import tpu as pltpu

F32 = jnp.float32
BF16 = jnp.bfloat16

EPS = 1e-6
LRU_C = 8.0
CONV_WIDTH = 4
ADAM_LR = 0.001
ADAM_B1 = 0.9
ADAM_B2 = 0.999
ADAM_EPS = 1e-08
ADAM_WD = 0.01
ADAM_STEP = 10

N_DEV = 8
MESH = pl.DeviceIdType.MESH
SUBLANES = 8
LANES = 128
VMEM_LIMIT = 56 * 1024 * 1024
TOKEN_TILE = 256
TIME_CHUNK = 256


def _dot(a, b):
    return jnp.dot(a.astype(BF16), b.astype(BF16), preferred_element_type=F32)


def _dot_nt(a, b):
    return lax.dot_general(a.astype(BF16), b.astype(BF16), (((1,), (1,)), ((), ())), preferred_element_type=F32)


def _dot_tn(a, b):
    return lax.dot_general(a.astype(BF16), b.astype(BF16), (((0,), (0,)), ((), ())), preferred_element_type=F32)


def _sigmoid(x):
    return jax.nn.sigmoid(x)


def _rms_stats(x):
    r = lax.rsqrt(jnp.mean(x * x, axis=-1, keepdims=True) + EPS)
    return x * r, r


def _rms_bwd(dy, xhat, r, g):
    dxn = dy * g
    dx = r * (dxn - xhat * jnp.mean(dxn * xhat, axis=-1, keepdims=True))
    return dx, dy * xhat


def _rowsum(v):
    return jnp.sum(v, axis=0, keepdims=True)


def _expm1(x):
    u = jnp.exp(x)
    um1 = u - 1.0
    safe = jnp.where(um1 == 0.0, 1.0, jnp.log(u))
    return jnp.where(um1 == 0.0, x, um1 * x / safe)


def _softplus(x):
    e = jnp.exp(-jnp.abs(x))
    u = 1.0 + e
    um1 = u - 1.0
    safe = jnp.where(um1 == 0.0, 1.0, um1)
    log1p_e = jnp.where(um1 == 0.0, e, jnp.log(u) * e / safe)
    return jnp.maximum(x, 0.0) + log1p_e


_GELU_K = math.sqrt(2.0 / math.pi)
_GELU_C = 0.044715


def _gelu(x):
    return 0.5 * x * (1.0 + jnp.tanh(_GELU_K * (x + _GELU_C * x * x * x)))


def _gelu_grad(x):
    th = jnp.tanh(_GELU_K * (x + _GELU_C * x * x * x))
    return 0.5 * (1.0 + th) + 0.5 * x * (1.0 - th * th) * _GELU_K * (1.0 + 3.0 * _GELU_C * x * x)


def _params(*sem):
    return pltpu.CompilerParams(dimension_semantics=sem, vmem_limit_bytes=VMEM_LIMIT)


def _rows(tm, n):
    return pl.BlockSpec((tm, n), lambda i: (i, 0))


def _rows_rev(tm, n, steps):
    return pl.BlockSpec((tm, n), lambda i: (steps - 1 - i, 0))


def _whole(shape):
    nd = len(shape)
    return pl.BlockSpec(shape, lambda i: (0,) * nd, pipeline_mode=pl.Buffered(1))


def _acc(shape):
    nd = len(shape)
    return pl.BlockSpec(shape, lambda i: (0,) * nd)


def _zero_on_first(*refs):
    @pl.when(pl.program_id(0) == 0)
    def _():
        for r in refs:
            r[...] = jnp.zeros_like(r)


def _inproj_fwd(x, g_mix, w_in_t, b_in, widths, comm=None):
    t, d = x.shape
    n = w_in_t.shape[0]
    tm = min(TOKEN_TILE, t)
    offs = [sum(widths[:i]) for i in range(len(widths) + 1)]

    def body(x_ref, g_ref, w_ref, b_ref, *outs):
        xhat, _ = _rms_stats(x_ref[...])
        h = (xhat * g_ref[...]).astype(BF16)
        for k, o_ref in enumerate(outs):
            lo, hi = offs[k], offs[k + 1]
            o_ref[...] = _dot_nt(h, w_ref[lo:hi, :]) + b_ref[:, lo:hi]

    return _run(
        body, comm, name="inproj_fwd", grid=(t // tm,),
        out_shape=[jax.ShapeDtypeStruct((t, w), F32) for w in widths],
        in_specs=[_rows(tm, d), _whole((1, d)), _whole((n, d)), _whole((1, n))],
        out_specs=[_rows(tm, w) for w in widths],
        semantics="parallel",
    )(x, g_mix, w_in_t, b_in)


def _s5_fwd(u, bbr_blk, bbi_blk, ar, ai, cre_blk, cimn_blk, d_skip, w_glu, b_glu, comm=None):
    t, sa = u.shape
    gn = bbr_blk.shape[0]
    tc = min(TIME_CHUNK, t)

    def body(u_ref, bbr_ref, bbi_ref, ar_ref, ai_ref, cre_ref, cim_ref, d_ref, wg_ref, bg_ref,
             sr_ref, si_ref, y_ref, ya_ref, cr_s, ci_s):
        _zero_on_first(cr_s, ci_s)
        uv = u_ref[...]
        ub = uv.astype(BF16)
        sr_ref[...] = _dot_nt(ub, bbr_ref[...])
        si_ref[...] = _dot_nt(ub, bbi_ref[...])
        a_r = ar_ref[...]
        a_i = ai_ref[...]

        def step(row, carry):
            c_r, c_i = carry
            at = pl.ds(row, 1)
            n_r = a_r * c_r - a_i * c_i + sr_ref[at, :]
            n_i = a_r * c_i + a_i * c_r + si_ref[at, :]
            sr_ref[at, :] = n_r
            si_ref[at, :] = n_i
            return n_r, n_i

        c_r, c_i = lax.fori_loop(0, tc, step, (cr_s[0:1, :], ci_s[0:1, :]), unroll=8)
        cr_s[0:1, :] = c_r
        ci_s[0:1, :] = c_i
        y = _dot_nt(sr_ref[...], cre_ref[...]) + _dot_nt(si_ref[...], cim_ref[...]) + d_ref[...] * uv
        y_ref[...] = y
        zz = _gelu(y)
        q = _dot(zz, wg_ref[...]) + bg_ref[...]
        ya_ref[...] = zz * _sigmoid(q)

    return _run(
        body, comm, name="s5_fwd", grid=(t // tc,),
        out_shape=[jax.ShapeDtypeStruct((t, gn), F32), jax.ShapeDtypeStruct((t, gn), F32),
                   jax.ShapeDtypeStruct((t, sa), F32), jax.ShapeDtypeStruct((t, sa), F32)],
        in_specs=[_rows(tc, sa), _whole((gn, sa)), _whole((gn, sa)), _whole((1, gn)), _whole((1, gn)),
                  _whole((sa, gn)), _whole((sa, gn)), _whole((1, sa)), _whole((sa, sa)), _whole((1, sa))],
        out_specs=[_rows(tc, gn), _rows(tc, gn), _rows(tc, sa), _rows(tc, sa)],
        scratch_shapes=[pltpu.VMEM((SUBLANES, gn), F32), pltpu.VMEM((SUBLANES, gn), F32)],
        semantics="arbitrary",
    )(u, bbr_blk, bbi_blk, ar, ai, cre_blk, cimn_blk, d_skip, w_glu, b_glu)


def _lru_gates(xc, wr_ref, br_ref, wi_ref, bi_ref, lam_ref):
    r = _sigmoid(_dot(xc, wr_ref[...]) + br_ref[...])
    ig = _sigmoid(_dot(xc, wi_ref[...]) + bi_ref[...])
    sp = _softplus(-lam_ref[...])
    log_a = (-LRU_C * r) * sp
    return r, ig, sp, log_a


def _lru_fwd(u, conv_w, conv_b, wr_blk, b_r, wi_blk, b_i, lru_lambda, comm=None):
    t, w = u.shape
    tc = min(TIME_CHUNK, t)
    halo = SUBLANES

    def body(u_ref, cw_ref, cb_ref, wr_ref, br_ref, wi_ref, bi_ref, lam_ref,
             xc_ref, h_ref, hp_ref, ext_s, a_s, carry_s):
        @pl.when(pl.program_id(0) == 0)
        def _():
            ext_s[0:halo, :] = jnp.zeros((halo, w), F32)
            carry_s[...] = jnp.zeros_like(carry_s)

        ext_s[halo:halo + tc, :] = u_ref[...]
        xc = cb_ref[...]
        for k in range(CONV_WIDTH):
            off = halo - (CONV_WIDTH - 1) + k
            xc = xc + cw_ref[k:k + 1, :] * ext_s[off:off + tc, :]
        ext_s[0:halo, :] = ext_s[tc:tc + halo, :]
        xc_ref[...] = xc
        r, ig, sp, log_a = _lru_gates(xc, wr_ref, br_ref, wi_ref, bi_ref, lam_ref)
        a_s[...] = jnp.exp(log_a)
        h_ref[...] = jnp.sqrt(-_expm1(2.0 * log_a)) * ig * xc

        def step(row, carry):
            at = pl.ds(row, 1)
            hp_ref[at, :] = carry
            nxt = a_s[at, :] * carry + h_ref[at, :]
            h_ref[at, :] = nxt
            return nxt

        carry_s[0:1, :] = lax.fori_loop(0, tc, step, carry_s[0:1, :], unroll=8)

    return _run(
        body, comm, name="lru_fwd", grid=(t // tc,),
        out_shape=[jax.ShapeDtypeStruct((t, w), F32)] * 3,
        in_specs=[_rows(tc, w), _whole((CONV_WIDTH, w)), _whole((1, w)), _whole((w, w)), _whole((1, w)),
                  _whole((w, w)), _whole((1, w)), _whole((1, w))],
        out_specs=[_rows(tc, w)] * 3,
        scratch_shapes=[pltpu.VMEM((halo + tc, w), F32), pltpu.VMEM((tc, w), F32), pltpu.VMEM((SUBLANES, w), F32)],
        semantics="arbitrary",
    )(u, conv_w, conv_b, wr_blk, b_r, wi_blk, b_i, lru_lambda)


def _merge_fwd(y_a, h, za, zb, x, w_a_out_t, w_b_out, w_o, comm=None):
    t, d = x.shape
    sa, lw = y_a.shape[1], h.shape[1]
    tm = min(TOKEN_TILE, t)

    def body(ya_ref, h_ref, za_ref, zb_ref, x_ref, wa_ref, wb_ref, wo_ref, x1_ref, mg_ref, ma_ref, mb_ref):
        ma = _dot_nt(ya_ref[...], wa_ref[...])
        mb = _dot(h_ref[...], wb_ref[...])
        merged = _sigmoid(za_ref[...]) * ma + _sigmoid(zb_ref[...]) * mb
        ma_ref[...] = ma
        mb_ref[...] = mb
        mg_ref[...] = merged.astype(mg_ref.dtype)
        x1_ref[...] = x_ref[...] + _dot(merged, wo_ref[...])

    return _run(
        body, comm, name="merge_fwd", grid=(t // tm,),
        out_shape=[jax.ShapeDtypeStruct((t, d), F32), jax.ShapeDtypeStruct((t, d), BF16),
                   jax.ShapeDtypeStruct((t, d), F32), jax.ShapeDtypeStruct((t, d), F32)],
        in_specs=[_rows(tm, sa), _rows(tm, lw), _rows(tm, d), _rows(tm, d), _rows(tm, d),
                  _whole((d, sa)), _whole((lw, d)), _whole((d, d))],
        out_specs=[_rows(tm, d)] * 4,
        semantics="parallel",
    )(y_a, h, za, zb, x, w_a_out_t, w_b_out, w_o)


def _ffn_up_fwd(x1, g_ffn, w_gate_t, w_up_t, comm=None):
    t, d = x1.shape
    f = w_gate_t.shape[0]
    tm = min(TOKEN_TILE, t)

    def body(x_ref, g_ref, wg_ref, wu_ref, fg_ref, fu_ref):
        xhat, _ = _rms_stats(x_ref[...])
        h2 = (xhat * g_ref[...]).astype(BF16)
        fg_ref[...] = _dot_nt(h2, wg_ref[...])
        fu_ref[...] = _dot_nt(h2, wu_ref[...])

    return _run(
        body, comm, name="ffn_up_fwd", grid=(t // tm,),
        out_shape=[jax.ShapeDtypeStruct((t, f), F32)] * 2,
        in_specs=[_rows(tm, d), _whole((1, d)), _whole((f, d)), _whole((f, d))],
        out_specs=[_rows(tm, f)] * 2,
        semantics="parallel",
    )(x1, g_ffn, w_gate_t, w_up_t)


def _ffn_down_fwd(fg, fu, x1, w_down):
    t, d = x1.shape
    f = fg.shape[1]
    tm = min(TOKEN_TILE, t)

    def body(fg_ref, fu_ref, x_ref, wd_ref, x2_ref):
        fgv = fg_ref[...]
        act = fgv * _sigmoid(fgv) * fu_ref[...]
        x2_ref[...] = x_ref[...] + _dot(act, wd_ref[...])

    return pl.pallas_call(
        body, name="ffn_down_fwd", grid=(t // tm,),
        out_shape=jax.ShapeDtypeStruct((t, d), F32),
        in_specs=[_rows(tm, f), _rows(tm, f), _rows(tm, d), _whole((f, d))],
        out_specs=_rows(tm, d),
        compiler_params=_params("parallel"),
    )(fg, fu, x1, w_down)


def _store_on_last(pairs):
    @pl.when(pl.program_id(0) == pl.num_programs(0) - 1)
    def _():
        for acc, out in pairs:
            out[...] = acc[...].astype(out.dtype)


def _tail_fwd_bwd(x2, p, target, g_pg, w_pg, b_pg, w_ple_t, g_ple, g_final):
    t, d = x2.shape
    pd = p.shape[1]
    tm = min(TOKEN_TILE, t)

    def body(x2_ref, p_ref, tg_ref, gpg_ref, wpg_ref, bpg_ref, wple_ref, gple_ref, gfin_ref,
             dx2_ref, loss_ref, dwpg_out, dwple_out, vec_ref, dwpg_ref, dwple_ref):
        _zero_on_first(loss_ref, dwpg_ref, dwple_ref, vec_ref)
        x2v = x2_ref[...]
        xh2, r2 = _rms_stats(x2v)
        h3 = xh2 * gpg_ref[...]
        gp = _sigmoid(_dot(h3, wpg_ref[...]) + bpg_ref[...])
        pe = _dot_nt(p_ref[...], wple_ref[...])
        peh, r3 = _rms_stats(pe)
        e = peh * gple_ref[...]
        x3 = x2v + gp * e
        xh3, r4 = _rms_stats(x3)
        diff = xh3 * gfin_ref[...] - tg_ref[...]
        loss_ref[...] += 0.5 * jnp.sum(jnp.mean(diff * diff, axis=-1, keepdims=True))
        dy = diff * (1.0 / d)
        dx3, dgfin = _rms_bwd(dy, xh3, r4, gfin_ref[...])
        d_gp = dx3 * e
        d_e = dx3 * gp
        dpe, dgple = _rms_bwd(d_e, peh, r3, gple_ref[...])
        dwple_ref[...] += _dot_tn(dpe, p_ref[...])
        dpre = d_gp * gp * (1.0 - gp)
        dwpg_ref[...] += _dot_tn(h3, dpre)
        dh3 = _dot_nt(dpre, wpg_ref[...])
        dx2n, dgpg = _rms_bwd(dh3, xh2, r2, gpg_ref[...])
        dx2_ref[...] = dx3 + dx2n
        vec_ref[0:1, :] += _rowsum(dpre)
        vec_ref[1:2, :] += _rowsum(dgpg)
        vec_ref[2:3, :] += _rowsum(dgple)
        vec_ref[3:4, :] += _rowsum(dgfin)
        _store_on_last([(dwpg_ref, dwpg_out), (dwple_ref, dwple_out)])

    return pl.pallas_call(
        body, name="tail_fwd_bwd", grid=(t // tm,),
        out_shape=[jax.ShapeDtypeStruct((t, d), F32), jax.ShapeDtypeStruct((SUBLANES, LANES), F32),
                   jax.ShapeDtypeStruct((d, d), BF16), jax.ShapeDtypeStruct((d, pd), BF16),
                   jax.ShapeDtypeStruct((SUBLANES, d), F32)],
        in_specs=[_rows(tm, d), _rows(tm, pd), _rows(tm, d), _whole((1, d)), _whole((d, d)), _whole((1, d)),
                  _whole((d, pd)), _whole((1, d)), _whole((1, d))],
        out_specs=[_rows(tm, d), _acc((SUBLANES, LANES)), _acc((d, d)), _acc((d, pd)), _acc((SUBLANES, d))],
        scratch_shapes=[pltpu.VMEM((d, d), F32), pltpu.VMEM((d, pd), F32)],
        compiler_params=_params("arbitrary"),
    )(x2, p, target, g_pg, w_pg, b_pg, w_ple_t, g_ple, g_final)


def _ffn_bwd_a(dx2, fg, fu, w_down, comm=None):
    t, d = dx2.shape
    f = fg.shape[1]
    tm = min(TOKEN_TILE, t)

    def body(dx_ref, fg_ref, fu_ref, wd_ref, dfg_ref, dfu_ref, act_ref):
        dact = _dot_nt(dx_ref[...], wd_ref[...])
        fgv = fg_ref[...]
        fuv = fu_ref[...]
        sg = _sigmoid(fgv)
        silu = fgv * sg
        dfu_ref[...] = (dact * silu).astype(dfu_ref.dtype)
        dfg_ref[...] = (dact * fuv * (sg * (1.0 + fgv * (1.0 - sg)))).astype(dfg_ref.dtype)
        act_ref[...] = (silu * fuv).astype(act_ref.dtype)

    return _run(
        body, comm, name="ffn_bwd_a", grid=(t // tm,),
        out_shape=[jax.ShapeDtypeStruct((t, f), BF16)] * 3,
        in_specs=[_rows(tm, d), _rows(tm, f), _rows(tm, f), _whole((f, d))],
        out_specs=[_rows(tm, f)] * 3,
        semantics="parallel",
    )(dx2, fg, fu, w_down)


def _ffn_bwd_b(dfg, dfu, x1, dx2, g_ffn, w_gate_t, w_up_t):
    t, d = x1.shape
    f = dfg.shape[1]
    tm = min(TOKEN_TILE, t)

    def body(dfg_ref, dfu_ref, x_ref, dx2_ref, g_ref, wg_ref, wu_ref, dx1_ref, h2_ref, vec_ref):
        _zero_on_first(vec_ref)
        dh2 = _dot(dfg_ref[...], wg_ref[...]) + _dot(dfu_ref[...], wu_ref[...])
        xhat, r = _rms_stats(x_ref[...])
        h2_ref[...] = (xhat * g_ref[...]).astype(h2_ref.dtype)
        dxn, dg = _rms_bwd(dh2, xhat, r, g_ref[...])
        dx1_ref[...] = dx2_ref[...] + dxn
        vec_ref[0:1, :] += _rowsum(dg)

    return pl.pallas_call(
        body, name="ffn_bwd_b", grid=(t // tm,),
        out_shape=[jax.ShapeDtypeStruct((t, d), F32), jax.ShapeDtypeStruct((t, d), BF16),
                   jax.ShapeDtypeStruct((SUBLANES, d), F32)],
        in_specs=[_rows(tm, f), _rows(tm, f), _rows(tm, d), _rows(tm, d), _whole((1, d)), _whole((f, d)), _whole((f, d))],
        out_specs=[_rows(tm, d), _rows(tm, d), _acc((SUBLANES, d))],
        compiler_params=_params("arbitrary"),
    )(dfg, dfu, x1, dx2, g_ffn, w_gate_t, w_up_t)


def _matmul_tn(a, b, tn, name, dtype=F32):
    t, k = a.shape
    n = b.shape[1]

    def body(a_ref, b_ref, o_ref):
        o_ref[...] = _dot_tn(a_ref[...], b_ref[...]).astype(o_ref.dtype)

    return pl.pallas_call(
        body, name=name, grid=(n // tn,),
        out_shape=jax.ShapeDtypeStruct((k, n), dtype),
        in_specs=[_whole((t, k)), pl.BlockSpec((t, tn), lambda j: (0, j))],
        out_specs=pl.BlockSpec((k, tn), lambda j: (0, j)),
        compiler_params=_params("parallel"),
    )(a, b)


def _merge_bwd(dx1, merged, ma, mb, za, zb, y_a, h, w_o, w_a_out_t, w_b_out, comm=None):
    t, d = dx1.shape
    sa, lw = y_a.shape[1], h.shape[1]
    tm = min(TOKEN_TILE, t)

    def body(dx1_ref, mg_ref, ma_ref, mb_ref, za_ref, zb_ref, ya_ref, h_ref, wo_ref, wa_ref, wb_ref,
             dza_ref, dzb_ref, dya_ref, dyb_ref, dwo_out, dwa_out, dwb_out, dwo_ref, dwa_ref, dwb_ref):
        _zero_on_first(dwo_ref, dwa_ref, dwb_ref)
        dx1v = dx1_ref[...].astype(BF16)
        dmg = _dot_nt(dx1v, wo_ref[...])
        ga = _sigmoid(za_ref[...])
        gb = _sigmoid(zb_ref[...])
        dza_ref[...] = dmg * ma_ref[...] * ga * (1.0 - ga)
        dzb_ref[...] = dmg * mb_ref[...] * gb * (1.0 - gb)
        dma = (dmg * ga).astype(BF16)
        dmb = (dmg * gb).astype(BF16)
        dya_ref[...] = _dot(dma, wa_ref[...])
        dyb_ref[...] = _dot_nt(dmb, wb_ref[...])
        dwo_ref[...] += _dot_tn(mg_ref[...], dx1v)
        dwa_ref[...] += _dot_tn(dma, ya_ref[...])
        dwb_ref[...] += _dot_tn(h_ref[...], dmb)
        _store_on_last([(dwo_ref, dwo_out), (dwa_ref, dwa_out), (dwb_ref, dwb_out)])

    return _run(
        body, comm, name="merge_bwd", grid=(t // tm,),
        out_shape=[jax.ShapeDtypeStruct((t, d), F32), jax.ShapeDtypeStruct((t, d), F32),
                   jax.ShapeDtypeStruct((t, sa), F32), jax.ShapeDtypeStruct((t, lw), F32),
                   jax.ShapeDtypeStruct((d, d), BF16), jax.ShapeDtypeStruct((d, sa), BF16),
                   jax.ShapeDtypeStruct((lw, d), BF16)],
        in_specs=[_rows(tm, d), _rows(tm, d), _rows(tm, d), _rows(tm, d), _rows(tm, d), _rows(tm, d),
                  _rows(tm, sa), _rows(tm, lw), _whole((d, d)), _whole((d, sa)), _whole((lw, d))],
        out_specs=[_rows(tm, d), _rows(tm, d), _rows(tm, sa), _rows(tm, lw), _acc((d, d)), _acc((d, sa)), _acc((lw, d))],
        scratch_shapes=[pltpu.VMEM((d, d), F32), pltpu.VMEM((d, sa), F32), pltpu.VMEM((lw, d), F32)],
        semantics="arbitrary",
    )(dx1, merged, ma, mb, za, zb, y_a, h, w_o, w_a_out_t, w_b_out)


def _fold_diag_blocks(dense, row_group, col_group, row0=0, col0=0):
    r, c = dense.shape
    rows = lax.broadcasted_iota(jnp.int32, (r, c), 0) + row0
    cols = lax.broadcasted_iota(jnp.int32, (r, c), 1) + col0
    kept = jnp.where(rows // row_group == cols // col_group, dense, 0.0)
    pick = (lax.broadcasted_iota(jnp.int32, (row_group, r), 0)
            == lax.broadcasted_iota(jnp.int32, (row_group, r), 1) % row_group).astype(F32)
    return jnp.dot(pick, kept, preferred_element_type=F32, precision=lax.Precision.HIGHEST)


def _lru_bwd(dh, xc, hprev, u, conv_w, wr_blk, b_r, wi_blk, b_i, lru_lambda, head_dim, comm=None):
    t, w = dh.shape
    tc = min(TIME_CHUNK, t)
    steps = t // tc
    halo = SUBLANES
    sub_per_chunk = tc // halo

    def body(dh_ref, xc_ref, hp_ref, u_ref, uh_ref, cw_ref, wr_ref, br_ref, wi_ref, bi_ref, lam_ref,
             du_ref, dwr_out, dwi_out, vec_ref, lam_s, a_s, dxc_s, uext_s, carry_s, dwr_ref, dwi_ref):
        chunk = steps - 1 - pl.program_id(0)

        @pl.when(pl.program_id(0) == 0)
        def _():
            carry_s[...] = jnp.zeros_like(carry_s)
            dxc_s[tc:tc + halo, :] = jnp.zeros((halo, w), F32)
            dwr_ref[...] = jnp.zeros_like(dwr_ref)
            dwi_ref[...] = jnp.zeros_like(dwi_ref)
            vec_ref[...] = jnp.zeros_like(vec_ref)

        xc = xc_ref[...]
        r, ig, sp, log_a = _lru_gates(xc, wr_ref, br_ref, wi_ref, bi_ref, lam_ref)
        a = jnp.exp(log_a)
        a_s[...] = a

        def step(i, q):
            at = pl.ds(tc - 1 - i, 1)
            lam_row = dh_ref[at, :] + q
            lam_s[at, :] = lam_row
            return a_s[at, :] * lam_row

        carry_s[0:1, :] = lax.fori_loop(0, tc, step, carry_s[0:1, :], unroll=8)
        lam = lam_s[...]
        mult = jnp.sqrt(-_expm1(2.0 * log_a))
        d_log_a = lam * hp_ref[...] * a - (lam * ig * xc) * (a * a) / mult
        d_ig = lam * mult * xc
        dpre_r = (d_log_a * (-LRU_C * sp)) * r * (1.0 - r)
        dpre_i = d_ig * ig * (1.0 - ig)
        dxc = lam * mult * ig + _dot_nt(dpre_r, wr_ref[...]) + _dot_nt(dpre_i, wi_ref[...])
        dwr_ref[...] += _dot_tn(dpre_r, xc)
        dwi_ref[...] += _dot_tn(dpre_i, xc)
        vec_ref[0:1, :] += _rowsum(dxc)
        vec_ref[1:2, :] += _rowsum(dpre_r)
        vec_ref[2:3, :] += _rowsum(dpre_i)
        vec_ref[3:4, :] += _rowsum(d_log_a * (-LRU_C * r)) * (-_sigmoid(-lam_ref[...]))
        dxc_s[0:tc, :] = dxc
        du = cw_ref[CONV_WIDTH - 1:CONV_WIDTH, :] * dxc
        for k in range(CONV_WIDTH - 1):
            off = CONV_WIDTH - 1 - k
            du = du + cw_ref[k:k + 1, :] * dxc_s[off:off + tc, :]
        du_ref[...] = du
        dxc_s[tc:tc + halo, :] = dxc_s[0:halo, :]
        uext_s[0:halo, :] = jnp.where(chunk > 0, uh_ref[...], 0.0)
        uext_s[halo:halo + tc, :] = u_ref[...]
        for k in range(CONV_WIDTH):
            off = halo - (CONV_WIDTH - 1) + k
            vec_ref[4 + k:5 + k, :] += _rowsum(dxc * uext_s[off:off + tc, :])

        @pl.when(pl.program_id(0) == steps - 1)
        def _():
            dwr_out[...] = _fold_diag_blocks(dwr_ref[...], head_dim, head_dim)
            dwi_out[...] = _fold_diag_blocks(dwi_ref[...], head_dim, head_dim)

    halo_spec = pl.BlockSpec((halo, w), lambda i: (jnp.maximum((steps - 1 - i) * sub_per_chunk - 1, 0), 0))
    return _run(
        body, comm, name="lru_bwd", grid=(steps,),
        out_shape=[jax.ShapeDtypeStruct((t, w), F32), jax.ShapeDtypeStruct((head_dim, w), F32),
                   jax.ShapeDtypeStruct((head_dim, w), F32), jax.ShapeDtypeStruct((SUBLANES, w), F32)],
        in_specs=[_rows_rev(tc, w, steps)] * 4 + [halo_spec, _whole((CONV_WIDTH, w)), _whole((w, w)), _whole((1, w)),
                                                  _whole((w, w)), _whole((1, w)), _whole((1, w))],
        out_specs=[_rows_rev(tc, w, steps), _acc((head_dim, w)), _acc((head_dim, w)), _acc((SUBLANES, w))],
        scratch_shapes=[pltpu.VMEM((tc, w), F32), pltpu.VMEM((tc, w), F32), pltpu.VMEM((tc + halo, w), F32),
                        pltpu.VMEM((halo + tc, w), F32), pltpu.VMEM((SUBLANES, w), F32), pltpu.VMEM((w, w), F32),
                        pltpu.VMEM((w, w), F32)],
        semantics="arbitrary",
    )(dh, xc, hprev, u, u, conv_w, wr_blk, b_r, wi_blk, b_i, lru_lambda)


def _s5_bwd(dya, y, sr, si, u, w_glu, b_glu, cre_blk, cimn_blk, bbr_blk, bbi_blk, ar, ai, d_skip, comm=None):
    t, sa = dya.shape
    gn = sr.shape[1]
    tc = min(TIME_CHUNK, t)
    steps = t // tc
    halo = SUBLANES

    def body(dya_ref, y_ref, sr_ref, si_ref, u_ref, wg_ref, bg_ref, cre_ref, cim_ref, bbr_ref, bbi_ref,
             ar_ref, ai_ref, d_ref, du_ref, lr_ref, li_ref, dy_ref, dwg_out, vsa_ref, vgn_ref, gr_s, gi_s, cr_s, ci_s,
             dwg_ref):
        @pl.when(pl.program_id(0) == 0)
        def _():
            cr_s[...] = jnp.zeros_like(cr_s)
            ci_s[...] = jnp.zeros_like(ci_s)
            gr_s[tc:tc + halo, :] = jnp.zeros((halo, gn), F32)
            gi_s[tc:tc + halo, :] = jnp.zeros((halo, gn), F32)
            dwg_ref[...] = jnp.zeros_like(dwg_ref)
            vsa_ref[...] = jnp.zeros_like(vsa_ref)
            vgn_ref[...] = jnp.zeros_like(vgn_ref)

        yv = y_ref[...]
        uv = u_ref[...]
        zz = _gelu(yv)
        sg = _sigmoid(_dot(zz, wg_ref[...]) + bg_ref[...])
        dyav = dya_ref[...]
        dq = dyav * zz * sg * (1.0 - sg)
        dzz = dyav * sg + _dot_nt(dq, wg_ref[...])
        dwg_ref[...] += _dot_tn(zz, dq)
        dy = dzz * _gelu_grad(yv)
        dyb = dy.astype(BF16)
        dy_ref[...] = dyb.astype(dy_ref.dtype)
        vsa_ref[0:1, :] += _rowsum(dq)
        vsa_ref[1:2, :] += _rowsum(dy * uv)
        gr_s[0:tc, :] = _dot(dyb, cre_ref[...])
        gi_s[0:tc, :] = _dot(dyb, cim_ref[...])
        a_r = ar_ref[...]
        a_i = ai_ref[...]

        def step(i, carry):
            l_r, l_i = carry
            at = pl.ds(tc - 1 - i, 1)
            n_r = gr_s[at, :] + a_r * l_r + a_i * l_i
            n_i = gi_s[at, :] + a_r * l_i - a_i * l_r
            gr_s[at, :] = n_r
            gi_s[at, :] = n_i
            return n_r, n_i

        l_r, l_i = lax.fori_loop(0, tc, step, (cr_s[0:1, :], ci_s[0:1, :]), unroll=8)
        cr_s[0:1, :] = l_r
        ci_s[0:1, :] = l_i
        nxt_r = gr_s[1:tc + 1, :]
        nxt_i = gi_s[1:tc + 1, :]
        srv = sr_ref[...]
        siv = si_ref[...]
        vgn_ref[0:1, :] += _rowsum(nxt_r * srv + nxt_i * siv)
        vgn_ref[1:2, :] += _rowsum(nxt_i * srv - nxt_r * siv)
        lam_r = gr_s[0:tc, :]
        lam_i = gi_s[0:tc, :]
        gr_s[tc:tc + halo, :] = gr_s[0:halo, :]
        gi_s[tc:tc + halo, :] = gi_s[0:halo, :]
        lrb = lam_r.astype(BF16)
        lib = lam_i.astype(BF16)
        lr_ref[...] = lrb.astype(lr_ref.dtype)
        li_ref[...] = lib.astype(li_ref.dtype)
        du_ref[...] = _dot(lrb, bbr_ref[...]) + _dot(lib, bbi_ref[...]) + dy * d_ref[...]
        _store_on_last([(dwg_ref, dwg_out)])

    return _run(
        body, comm, name="s5_bwd", grid=(steps,),
        out_shape=[jax.ShapeDtypeStruct((t, sa), F32), jax.ShapeDtypeStruct((t, gn), BF16),
                   jax.ShapeDtypeStruct((t, gn), BF16), jax.ShapeDtypeStruct((t, sa), BF16),
                   jax.ShapeDtypeStruct((sa, sa), BF16), jax.ShapeDtypeStruct((SUBLANES, sa), F32),
                   jax.ShapeDtypeStruct((SUBLANES, gn), F32)],
        in_specs=[_rows_rev(tc, sa, steps), _rows_rev(tc, sa, steps), _rows_rev(tc, gn, steps), _rows_rev(tc, gn, steps),
                  _rows_rev(tc, sa, steps), _whole((sa, sa)), _whole((1, sa)), _whole((sa, gn)), _whole((sa, gn)),
                  _whole((gn, sa)), _whole((gn, sa)), _whole((1, gn)), _whole((1, gn)), _whole((1, sa))],
        out_specs=[_rows_rev(tc, sa, steps), _rows_rev(tc, gn, steps), _rows_rev(tc, gn, steps), _rows_rev(tc, sa, steps),
                   _acc((sa, sa)), _acc((SUBLANES, sa)), _acc((SUBLANES, gn))],
        scratch_shapes=[pltpu.VMEM((tc + halo, gn), F32), pltpu.VMEM((tc + halo, gn), F32),
                        pltpu.VMEM((SUBLANES, gn), F32), pltpu.VMEM((SUBLANES, gn), F32), pltpu.VMEM((sa, sa), F32)],
        semantics="arbitrary",
    )(dya, y, sr, si, u, w_glu, b_glu, cre_blk, cimn_blk, bbr_blk, bbi_blk, ar, ai, d_skip)


def _inproj_bwd(dparts, x, dx1, g_mix, w_in_t, comm=None):
    t, d = x.shape
    n = w_in_t.shape[0]
    widths = [p.shape[1] for p in dparts]
    offs = [sum(widths[:i]) for i in range(len(widths) + 1)]
    tm = min(TOKEN_TILE, t)
    np_ = len(dparts)

    def body(*refs):
        dz_refs = refs[:np_]
        x_ref, dx1_ref, g_ref, w_ref, gx_ref, h_ref, dz_ref, vd_ref, vn_ref = refs[np_:]
        _zero_on_first(vd_ref, vn_ref)
        dh = jnp.zeros((tm, d), F32)
        for k, r in enumerate(dz_refs):
            lo, hi = offs[k], offs[k + 1]
            dzk = r[...]
            dh = dh + _dot(dzk, w_ref[lo:hi, :])
            dz_ref[:, lo:hi] = dzk.astype(dz_ref.dtype)
            vn_ref[0:1, lo:hi] += _rowsum(dzk)
        xhat, r0 = _rms_stats(x_ref[...])
        h_ref[...] = (xhat * g_ref[...]).astype(h_ref.dtype)
        dxn, dg = _rms_bwd(dh, xhat, r0, g_ref[...])
        gx_ref[...] = dx1_ref[...] + dxn
        vd_ref[0:1, :] += _rowsum(dg)

    return _run(
        body, comm, name="inproj_bwd", grid=(t // tm,),
        out_shape=[jax.ShapeDtypeStruct((t, d), F32), jax.ShapeDtypeStruct((t, d), BF16),
                   jax.ShapeDtypeStruct((t, n), BF16), jax.ShapeDtypeStruct((SUBLANES, d), F32),
                   jax.ShapeDtypeStruct((SUBLANES, n), F32)],
        in_specs=[_rows(tm, w) for w in widths] + [_rows(tm, d), _rows(tm, d), _whole((1, d)), _whole((n, d))],
        out_specs=[_rows(tm, d), _rows(tm, d), _rows(tm, n), _acc((SUBLANES, d)), _acc((SUBLANES, n))],
        semantics="arbitrary",
    )(*dparts, x, dx1, g_mix, w_in_t)


def _prep(lr_row, li_row, ldt_row, lr_col, li_col, ldt_col, b_re, b_im, c_re, c_im, w_r, w_i, groups, heads):
    gn, pch = b_re.shape
    sa, n = c_re.shape
    w, hd = w_r.shape

    def expand(vals, row_group, col_group, width):
        r, k = vals.shape
        tile = (lax.broadcasted_iota(jnp.int32, (k, width), 0) == lax.broadcasted_iota(jnp.int32, (k, width), 1) % k)
        rows = lax.broadcasted_iota(jnp.int32, (r, width), 0) // row_group
        cols = lax.broadcasted_iota(jnp.int32, (r, width), 1) // col_group
        return jnp.where(rows == cols, _dot(vals, tile.astype(BF16)), 0.0)

    def body(lrr, lir, ldr, lrc, lic, ldc, bre, bim, cre, cim, wr, wi,
             ar_o, ai_o, bbr_o, bbi_o, cre_o, cim_o, wr_o, wi_o):
        ar, ai, _, _ = _disc_scalars(lrr[...], lir[...], ldr[...])
        ar_o[...] = ar
        ai_o[...] = ai
        _, _, bbr, bbi = _disc_cols(lrc[...], lic[...], ldc[...], bre[...], bim[...])
        bbr_o[...] = expand(bbr, n, pch, sa).astype(bbr_o.dtype)
        bbi_o[...] = expand(bbi, n, pch, sa).astype(bbi_o.dtype)
        cre_o[...] = expand(cre[...], pch, n, gn).astype(cre_o.dtype)
        cim_o[...] = expand(-cim[...], pch, n, gn).astype(cim_o.dtype)
        wr_o[...] = expand(wr[...], hd, hd, w).astype(wr_o.dtype)
        wi_o[...] = expand(wi[...], hd, hd, w).astype(wi_o.dtype)

    return pl.pallas_call(
        body, name="prep",
        out_shape=[jax.ShapeDtypeStruct((1, gn), F32), jax.ShapeDtypeStruct((1, gn), F32),
                   jax.ShapeDtypeStruct((gn, sa), BF16), jax.ShapeDtypeStruct((gn, sa), BF16),
                   jax.ShapeDtypeStruct((sa, gn), BF16), jax.ShapeDtypeStruct((sa, gn), BF16),
                   jax.ShapeDtypeStruct((w, w), BF16), jax.ShapeDtypeStruct((w, w), BF16)],
        compiler_params=pltpu.CompilerParams(vmem_limit_bytes=VMEM_LIMIT),
    )(lr_row, li_row, ldt_row, lr_col, li_col, ldt_col, b_re, b_im, c_re, c_im, w_r, w_i)


def _s5_param_grads(lam_r, lam_i, sr, si, u, dy, pch, n, comm=None):
    t, gn = lam_r.shape
    sa = u.shape[1]
    tb = min(gn, 512)

    def body(lr_ref, li_ref, sr_ref, si_ref, u_ref, dy_ref, dbr_ref, dbi_ref, dcr_ref, dci_ref):
        _zero_on_first(dcr_ref, dci_ref)
        base = pl.program_id(0) * tb
        uv = u_ref[...]
        dyv = dy_ref[...]
        dbr_ref[...] = _fold_diag_blocks(_dot_tn(uv, lr_ref[...]), pch, n, col0=base)
        dbi_ref[...] = _fold_diag_blocks(_dot_tn(uv, li_ref[...]), pch, n, col0=base)
        dcr_ref[...] += _fold_diag_blocks(_dot_tn(sr_ref[...], dyv), n, pch, row0=base)
        dci_ref[...] += _fold_diag_blocks(_dot_tn(si_ref[...], dyv), n, pch, row0=base)

    cols = pl.BlockSpec((t, tb), lambda j: (0, j))
    return _run(
        body, comm, name="s5_param_grads", grid=(gn // tb,),
        out_shape=[jax.ShapeDtypeStruct((pch, gn), F32), jax.ShapeDtypeStruct((pch, gn), F32),
                   jax.ShapeDtypeStruct((n, sa), F32), jax.ShapeDtypeStruct((n, sa), F32)],
        in_specs=[cols, cols, cols, cols, _whole((t, sa)), _whole((t, sa))],
        out_specs=[pl.BlockSpec((pch, tb), lambda j: (0, j)), pl.BlockSpec((pch, tb), lambda j: (0, j)),
                   _acc((n, sa)), _acc((n, sa))],
        semantics="arbitrary",
    )(lam_r, lam_i, sr, si, u, dy)


SMALL_PARTS = ["vec_tail", "vec_ffn", "vec_lru", "vec_mix", "vec_bin", "vec_sa", "vec_gn", "dw_r", "dw_i", "dbb_re",
               "dbb_im", "dc_re", "dc_imn", "loss"]


def _small_reduce(parts, shapes, lr_col, li_col, ldt_col, b_re, b_im, groups):
    gn, pch = b_re.shape
    n = gn // groups
    nparts = parts[SMALL_PARTS[0]].size // math.prod(shapes[SMALL_PARTS[0]])
    np_, nout = len(SMALL_PARTS), 23

    def body(*refs):
        ins = refs[:np_]
        lr, li, ldt, bre, bim = refs[np_:np_ + 5]
        outs = refs[np_ + 5:np_ + 5 + nout]
        sums = dict(zip(SMALL_PARTS, refs[np_ + 5 + nout:]))

        @pl.when(pl.program_id(0) == 0)
        def _():
            for k, r in zip(SMALL_PARTS, ins):
                sums[k][...] = r[...]

        @pl.when(pl.program_id(0) > 0)
        def _():
            for k, r in zip(SMALL_PARTS, ins):
                sums[k][...] += r[...]

        @pl.when(pl.program_id(0) == nparts - 1)
        def _():
            finish({k: s[...] for k, s in sums.items()}, lr, li, ldt, bre, bim, *outs)

    def finish(tot, lr, li, ldt, bre, bim, o_loss, o_gmix, o_bin, o_bglu, o_s5d, o_convb, o_br, o_bi, o_lam, o_gffn,
               o_gpg, o_bpg, o_gple, o_gfin, o_wr, o_wi, o_cre, o_cim, o_bre, o_bim, o_lre, o_lim, o_ldt):
        o_loss[...] = tot["loss"]
        o_bpg[...] = tot["vec_tail"][0:1]
        o_gpg[...] = tot["vec_tail"][1:2]
        o_gple[...] = tot["vec_tail"][2:3]
        o_gfin[...] = tot["vec_tail"][3:4]
        o_gffn[...] = tot["vec_ffn"][0:1]
        o_convb[...] = tot["vec_lru"][0:1]
        o_br[...] = tot["vec_lru"][1:2]
        o_bi[...] = tot["vec_lru"][2:3]
        o_lam[...] = tot["vec_lru"][3:4]
        o_gmix[...] = tot["vec_mix"][0:1]
        o_bin[...] = tot["vec_bin"][0:1]
        o_bglu[...] = tot["vec_sa"][0:1]
        o_s5d[...] = tot["vec_sa"][1:2]
        o_wr[...] = tot["dw_r"].T
        o_wi[...] = tot["dw_i"].T
        o_cre[...] = tot["dc_re"].T
        o_cim[...] = -tot["dc_imn"].T
        d_a = tot["vec_gn"].T
        _, chain = jax.vjp(_disc_cols, lr[...], li[...], ldt[...], bre[...], bim[...])
        d_lr, d_li, d_ldt, d_bre, d_bim = chain((d_a[:, 0:1], d_a[:, 1:2], tot["dbb_re"].T, tot["dbb_im"].T))
        o_lre[...] = d_lr
        o_lim[...] = d_li
        o_bre[...] = d_bre
        o_bim[...] = d_bim
        same = (lax.broadcasted_iota(jnp.int32, (groups, gn), 0)
                == lax.broadcasted_iota(jnp.int32, (groups, gn), 1) // n).astype(F32)
        o_ldt[...] = jnp.dot(same, d_ldt * jnp.ones((1, LANES), F32), preferred_element_type=F32,
                             precision=lax.Precision.HIGHEST)[:, 0:1]

    d = shapes["vec_mix"][1]
    nz = shapes["vec_bin"][1]
    sa = shapes["vec_sa"][1]
    w = shapes["vec_lru"][1]
    row = lambda c: jax.ShapeDtypeStruct((1, c), F32)
    out_shape = [jax.ShapeDtypeStruct(shapes["loss"], F32), row(d), row(nz), row(sa), row(sa), row(w), row(w), row(w),
                 row(w), row(d), row(d), row(d), row(d), row(d),
                 jax.ShapeDtypeStruct(shapes["dw_r"][::-1], F32), jax.ShapeDtypeStruct(shapes["dw_i"][::-1], F32),
                 jax.ShapeDtypeStruct(shapes["dc_re"][::-1], F32), jax.ShapeDtypeStruct(shapes["dc_imn"][::-1], F32),
                 jax.ShapeDtypeStruct((gn, pch), F32), jax.ShapeDtypeStruct((gn, pch), F32),
                 jax.ShapeDtypeStruct((gn, 1), F32), jax.ShapeDtypeStruct((gn, 1), F32),
                 jax.ShapeDtypeStruct((groups, 1), F32)]
    def part_spec(k):
        r, c = shapes[k]
        if parts[k].ndim == 3:
            return pl.BlockSpec((None, r, c), lambda i: (i, 0, 0))
        return pl.BlockSpec((r, c), lambda i: (i, 0))

    outs = pl.pallas_call(
        body, name="small_reduce", grid=(nparts,), out_shape=out_shape,
        in_specs=[part_spec(k) for k in SMALL_PARTS] + [_whole(a.shape) for a in (lr_col, li_col, ldt_col, b_re, b_im)],
        out_specs=[_acc(s.shape) for s in out_shape],
        scratch_shapes=[pltpu.VMEM(shapes[k], F32) for k in SMALL_PARTS],
        compiler_params=_params("arbitrary"),
    )(*[parts[k] for k in SMALL_PARTS], lr_col, li_col, ldt_col, b_re, b_im)
    names = ["loss", "g_mix", "b_in", "b_glu", "s5_d", "conv_b", "b_r", "b_i", "lru_lambda", "g_ffn", "g_ple_gate",
             "b_ple_gate", "g_ple", "g_final", "w_r", "w_i", "s5_c_re", "s5_c_im", "s5_b_re", "s5_b_im", "lam_re",
             "lam_im", "log_dt"]
    return dict(zip(names, outs))


def _adamw_small(ws, gs, ms, vs):
    n = len(ws)

    def body(*refs):
        w_r, g_r, m_r, v_r = (refs[i * n:(i + 1) * n] for i in range(4))
        d_o, m_o, v_o = (refs[(4 + i) * n:(5 + i) * n] for i in range(3))
        for i in range(n):
            delta, m_new, v_new = _adamw_math(w_r[i][...], g_r[i][...], m_r[i][...], v_r[i][...])
            d_o[i][...] = delta
            m_o[i][...] = m_new
            v_o[i][...] = v_new

    shapes = [jax.ShapeDtypeStruct(a.shape, F32) for a in ws]
    outs = pl.pallas_call(body, name="adamw_small", out_shape=shapes * 3)(*ws, *gs, *ms, *vs)
    return outs[:n], outs[n:2 * n], outs[2 * n:]


def _adamw_math(w, g, m, v):
    m_new = ADAM_B1 * m + (1.0 - ADAM_B1) * g
    v_new = ADAM_B2 * v + (1.0 - ADAM_B2) * (g * g)
    m_hat = m_new / (1.0 - ADAM_B1 ** ADAM_STEP)
    v_hat = v_new / (1.0 - ADAM_B2 ** ADAM_STEP)
    delta = -ADAM_LR * (m_hat / (jnp.sqrt(v_hat) + ADAM_EPS) + ADAM_WD * w)
    return delta, m_new, v_new


def _row_tile(rows):
    for cand in (256, 128, 64, 32, 16, 8):
        if rows % cand == 0:
            return cand
    return rows


def _adamw(parts, w, m, v, name, transposed=False):
    rows, cols = w.shape
    npart = parts.shape[0]
    tr = _row_tile(rows)
    if transposed:
        parts_spec = pl.BlockSpec((npart, cols, tr), lambda i: (0, 0, i))
    else:
        parts_spec = pl.BlockSpec((npart, tr, cols), lambda i: (0, i, 0))

    def body(p_ref, w_ref, m_ref, v_ref, g_ref, d_ref, mo_ref, vo_ref):
        g = p_ref[0].astype(F32)
        for k in range(1, npart):
            g = g + p_ref[k].astype(F32)
        if transposed:
            g = g.T
        delta, m_new, v_new = _adamw_math(w_ref[...], g, m_ref[...], v_ref[...])
        g_ref[...] = g
        d_ref[...] = delta
        mo_ref[...] = m_new
        vo_ref[...] = v_new

    return pl.pallas_call(
        body, name=name, grid=(rows // tr,),
        out_shape=[jax.ShapeDtypeStruct((rows, cols), F32)] * 4,
        in_specs=[parts_spec] + [_rows(tr, cols)] * 3,
        out_specs=[_rows(tr, cols)] * 4,
        compiler_params=_params("parallel"),
    )(parts, w, m, v)


def _mesh_position():
    return lax.axis_index("x"), lax.axis_index("y"), lax.axis_index("c")


def _flip(pos, rel):
    x, y, c = pos
    return (1 - x if rel & 4 else x, 1 - y if rel & 2 else y, 1 - c if rel & 1 else c)


def _index(pos):
    return 4 * pos[0] + 2 * pos[1] + pos[2]


_ANY = pl.BlockSpec(memory_space=pl.ANY)
FLAT_ROWS = 32


def _dma_sems(n):
    return [pltpu.SemaphoreType.DMA((n, N_DEV - 1)), pltpu.SemaphoreType.DMA((n, N_DEV - 1)), pltpu.SemaphoreType.DMA((n,))]


def _block_of(ref, idx, rows, flat):
    if flat:
        return ref.at[pl.ds(pl.multiple_of(idx * rows, FLAT_ROWS), rows), :]
    return ref.at[idx]


class _Gather:
    chips = (4, 2, 6)

    def __init__(self, shards):
        self.inputs = list(shards)
        self.flat = [s.shape[0] % FLAT_ROWS == 0 for s in shards]
        self.out_shape = [
            jax.ShapeDtypeStruct((N_DEV * s.shape[0], s.shape[1]) if f else (N_DEV,) + s.shape, s.dtype)
            for s, f in zip(shards, self.flat)]
        self.sems = _dma_sems(len(shards))

    def _copy(self, ins, outs, sems, i, k, block, to, own=False):
        dst = _block_of(outs[i], _index(block), self.inputs[i].shape[0], self.flat[i])
        return pltpu.make_async_remote_copy(
            src_ref=ins[i] if own else dst, dst_ref=dst, send_sem=sems[0].at[i, k], recv_sem=sems[1].at[i, k],
            device_id=to, device_id_type=MESH)

    def _local(self, ins, outs, sems, i, me):
        dst = _block_of(outs[i], _index(me), self.inputs[i].shape[0], self.flat[i])
        return pltpu.make_async_copy(ins[i], dst, sems[2].at[i])

    def _first(self, ins, outs, sems, i, me):
        cps = [self._copy(ins, outs, sems, i, 0, me, _flip(me, 1), own=True)]
        cps += [self._copy(ins, outs, sems, i, 1 + j, me, _flip(me, rel), own=True) for j, rel in enumerate(self.chips)]
        return cps

    def _passed(self, ins, outs, sems, i, j, me):
        return self._copy(ins, outs, sems, i, 4 + j, _flip(me, self.chips[j]), _flip(me, 1))

    def before(self, ins, outs, sems):
        n = len(self.inputs)
        me = _mesh_position()

        @pl.when(pl.program_id(0) == 0)
        def _():
            for i in range(n):
                self._local(ins, outs, sems, i, me).start()
                for cp in self._first(ins, outs, sems, i, me):
                    cp.start()

        @pl.when(pl.program_id(0) == pl.num_programs(0) - 1)
        def _():
            for j, rel in enumerate(self.chips):
                for i in range(n):
                    self._copy(ins, outs, sems, i, 1 + j, _flip(me, rel), me).wait_recv()
                    self._passed(ins, outs, sems, i, j, me).start()

    def after(self, ins, outs, sems):
        n = len(self.inputs)
        me = _mesh_position()
        sibling = _flip(me, 1)

        @pl.when(pl.program_id(0) == pl.num_programs(0) - 1)
        def _():
            for i in range(n):
                self._copy(ins, outs, sems, i, 0, sibling, me).wait_recv()
                for j, rel in enumerate(self.chips):
                    self._copy(ins, outs, sems, i, 4 + j, _flip(sibling, rel), me).wait_recv()
            for i in range(n):
                for cp in self._first(ins, outs, sems, i, me):
                    cp.wait_send()
                for j in range(len(self.chips)):
                    self._passed(ins, outs, sems, i, j, me).wait_send()
                self._local(ins, outs, sems, i, me).wait()


class _Exchange:
    def __init__(self, arrays):
        self.inputs = list(arrays)
        self.flat = [a.ndim == 2 for a in arrays]
        self.block = [(a.shape[0] // N_DEV, a.shape[1]) if a.ndim == 2 else a.shape[1:] for a in arrays]
        for f, b in zip(self.flat, self.block):
            assert not f or b[0] % FLAT_ROWS == 0, b
        self.out_shape = [jax.ShapeDtypeStruct((N_DEV,) + tuple(b), a.dtype) for a, b in zip(arrays, self.block)]
        self.sems = _dma_sems(len(arrays))

    def _send(self, ins, outs, sems, i, rel, me):
        peer = _flip(me, rel)
        return pltpu.make_async_remote_copy(
            src_ref=_block_of(ins[i], _index(peer), self.block[i][0], self.flat[i]), dst_ref=outs[i].at[_index(me)],
            send_sem=sems[0].at[i, rel - 1], recv_sem=sems[1].at[i, rel - 1], device_id=peer, device_id_type=MESH)

    def _arrival(self, ins, outs, sems, i, rel, me):
        peer = _flip(me, rel)
        return pltpu.make_async_remote_copy(
            src_ref=_block_of(ins[i], _index(me), self.block[i][0], self.flat[i]), dst_ref=outs[i].at[_index(peer)],
            send_sem=sems[0].at[i, rel - 1], recv_sem=sems[1].at[i, rel - 1], device_id=peer, device_id_type=MESH)

    def _local(self, ins, outs, sems, i, me):
        return pltpu.make_async_copy(_block_of(ins[i], _index(me), self.block[i][0], self.flat[i]),
                                     outs[i].at[_index(me)], sems[2].at[i])

    def before(self, ins, outs, sems):
        me = _mesh_position()

        @pl.when(pl.program_id(0) == 0)
        def _():
            for i in range(len(self.inputs)):
                self._local(ins, outs, sems, i, me).start()
                for rel in range(1, N_DEV):
                    self._send(ins, outs, sems, i, rel, me).start()

    def after(self, ins, outs, sems):
        me = _mesh_position()

        @pl.when(pl.program_id(0) == pl.num_programs(0) - 1)
        def _():
            for i in range(len(self.inputs)):
                for rel in range(1, N_DEV):
                    self._arrival(ins, outs, sems, i, rel, me).wait_recv()
            for i in range(len(self.inputs)):
                for rel in range(1, N_DEV):
                    self._send(ins, outs, sems, i, rel, me).wait_send()
                self._local(ins, outs, sems, i, me).wait()


class _Both:
    def __init__(self, first, second):
        self.jobs = (first, second)
        self.inputs = first.inputs + second.inputs
        self.out_shape = first.out_shape + second.out_shape
        self.sems = first.sems + second.sems

    def _each(self, ins, outs, sems):
        a = self.jobs[0]
        i, o, s = len(a.inputs), len(a.out_shape), len(a.sems)
        return ((a, ins[:i], outs[:o], sems[:s]), (self.jobs[1], ins[i:], outs[o:], sems[s:]))

    def before(self, ins, outs, sems):
        for job, i, o, s in self._each(ins, outs, sems):
            job.before(i, o, s)

    def after(self, ins, outs, sems):
        for job, i, o, s in self._each(ins, outs, sems):
            job.after(i, o, s)


def _run(body, comm, *, semantics, out_shape, in_specs, out_specs, scratch_shapes=(), **kw):
    if comm is None:
        return pl.pallas_call(body, out_shape=out_shape, in_specs=in_specs, out_specs=out_specs,
                              scratch_shapes=list(scratch_shapes), compiler_params=_params(semantics), **kw)
    single = not isinstance(out_shape, (list, tuple))
    outs = [out_shape] if single else list(out_shape)
    ospecs = [out_specs] if single else list(out_specs)
    counts = [len(in_specs), len(comm.inputs), len(outs), len(comm.out_shape), len(scratch_shapes), len(comm.sems)]

    def carrying(*refs):
        groups, pos = [], 0
        for c in counts:
            groups.append(refs[pos:pos + c])
            pos += c
        main_in, comm_in, main_out, comm_out, main_scratch, comm_sems = groups
        comm.before(comm_in, comm_out, comm_sems)
        body(*main_in, *main_out, *main_scratch)
        comm.after(comm_in, comm_out, comm_sems)

    call = pl.pallas_call(
        carrying, out_shape=outs + list(comm.out_shape), in_specs=list(in_specs) + [_ANY] * len(comm.inputs),
        out_specs=ospecs + [_ANY] * len(comm.out_shape), scratch_shapes=list(scratch_shapes) + list(comm.sems),
        compiler_params=_params("arbitrary"), **kw)

    def apply(*args):
        res = call(*args, *comm.inputs)
        main = res[:len(outs)]
        return (main[0] if single else list(main)), list(res[len(outs):])

    return apply


def _alone(comm, name):
    return _run(lambda: None, comm, semantics="arbitrary", name=name, grid=(1,), out_shape=[], in_specs=[], out_specs=[])()[1]


SHARDED = {"w_in": 1, "w_glu": 0, "conv_w": 1, "w_a_out": 1, "w_b_out": 0, "w_o": 0, "w_ffn_gate": 1, "w_ffn_up": 1,
           "w_ffn_down": 0, "w_ple_gate": 0, "w_ple": 1}
TRANSPOSED = ("w_in", "w_a_out", "w_ffn_gate", "w_ffn_up", "w_ple")
SMALL = ["g_mix", "b_in", "lam_re", "lam_im", "log_dt", "s5_b_re", "s5_b_im", "s5_c_re", "s5_c_im", "s5_d", "b_glu",
         "conv_b", "w_r", "b_r", "w_i", "b_i", "lru_lambda", "g_ffn", "g_ple_gate", "b_ple_gate", "g_ple", "g_final"]
WEIGHTS = ["g_mix", "w_in", "b_in", "lam_re", "lam_im", "log_dt", "s5_b_re", "s5_b_im", "s5_c_re", "s5_c_im", "s5_d",
           "w_glu", "b_glu", "conv_w", "conv_b", "w_r", "b_r", "w_i", "b_i", "lru_lambda", "w_a_out", "w_b_out", "w_o",
           "g_ffn", "w_ffn_gate", "w_ffn_up", "w_ffn_down", "g_ple_gate", "w_ple_gate", "b_ple_gate", "w_ple", "g_ple",
           "g_final"]


def _unblock(gathered, axis):
    nb, r, c = gathered.shape
    if axis == 0:
        return gathered.reshape(nb * r, c)
    return jnp.transpose(gathered, (1, 0, 2)).reshape(r, nb * c)


def _block(full, axis):
    r, c = full.shape
    if axis == 0:
        return full.reshape(N_DEV, r // N_DEV, c)
    return jnp.transpose(full.reshape(r, N_DEV, c // N_DEV), (1, 0, 2))


def _wire_grad(name, grad):
    return _block(grad, 1) if name == "conv_w" else grad


def _disc_scalars(lr, li, ldt):
    dt = jnp.exp(ldt)
    mag = jnp.exp(lr * dt)
    ar = mag * jnp.cos(li * dt)
    ai = mag * jnp.sin(li * dt)
    den = lr * lr + li * li
    nr = ar - 1.0
    fr = (nr * lr + ai * li) / den
    fi = (ai * lr - nr * li) / den
    return ar, ai, fr, fi


def _disc_cols(lr, li, ldt, b_re, b_im):
    ar, ai, fr, fi = _disc_scalars(lr, li, ldt)
    return ar, ai, fr * b_re - fi * b_im, fr * b_im + fi * b_re


def _local_step(x, p, target, src, small, disc, distributed=True):
    full = {} if distributed else dict(src)
    got = {}

    def gather(keys):
        return _Gather([src[k] for k in keys]) if distributed else None

    def exchange(keys, grads):
        return _Exchange([_wire_grad(k, grads[k]) for k in keys]) if distributed else None

    def carry(fn, *args, job=None, keys=(), sink=None):
        if job is None:
            return fn(*args)
        res, extra = fn(*args, comm=job)
        sink.update(zip(keys, extra))
        return res

    if distributed:
        first = ["w_in", "w_glu", "conv_w"]
        for k, a in zip(first, _alone(gather(first), "gather_first")):
            full[k] = _unblock(a, 1) if k == "conv_w" else a

    d = x.shape[1]
    g, n, pch = small["s5_b_re"].shape
    heads = small["w_r"].shape[0]
    sa, lw = g * pch, small["lru_lambda"].shape[-1]
    widths = [sa, lw, d, d]
    row = lambda v: v.reshape(1, -1)

    hd = small["w_r"].shape[-1]
    ar_row, ai_row, bbr_blk, bbi_blk, cre_blk, cimn_blk, wr_blk, wi_blk = _prep(
        *disc["rows"], *disc["cols"], disc["b_re"], disc["b_im"], small["s5_c_re"].reshape(sa, n),
        small["s5_c_im"].reshape(sa, n), small["w_r"].reshape(lw, hd), small["w_i"].reshape(lw, hd), g, heads)
    d_row = row(small["s5_d"])
    conv_w = full["conv_w"].astype(F32)

    keys = ["w_a_out", "w_b_out", "w_o"]
    u_a, u_b, za, zb = carry(_inproj_fwd, x, row(small["g_mix"]), full["w_in"], row(small["b_in"]), widths,
                             job=gather(keys), keys=keys, sink=full)
    keys = ["w_ffn_gate"]
    sr, si, y, y_a = carry(_s5_fwd, u_a, bbr_blk, bbi_blk, ar_row, ai_row, cre_blk, cimn_blk, d_row, full["w_glu"],
                           row(small["b_glu"]), job=gather(keys), keys=keys, sink=full)
    keys = ["w_ffn_up"]
    xc, h, hprev = carry(_lru_fwd, u_b, conv_w, row(small["conv_b"]), wr_blk, row(small["b_r"]), wi_blk,
                         row(small["b_i"]), row(small["lru_lambda"]), job=gather(keys), keys=keys, sink=full)
    keys = ["w_ffn_down"]
    x1, merged, ma, mb = carry(_merge_fwd, y_a, h, za, zb, x, full["w_a_out"], full["w_b_out"], full["w_o"],
                               job=gather(keys), keys=keys, sink=full)
    keys = ["w_ple_gate", "w_ple"]
    fg, fu = carry(_ffn_up_fwd, x1, row(small["g_ffn"]), full["w_ffn_gate"], full["w_ffn_up"],
                   job=gather(keys), keys=keys, sink=full)
    x2 = _ffn_down_fwd(fg, fu, x1, full["w_ffn_down"])

    gw = {}
    dx2, loss_blk, gw["w_ple_gate"], gw["w_ple"], vec_tail = _tail_fwd_bwd(
        x2, p, target, row(small["g_ple_gate"]), full["w_ple_gate"], row(small["b_ple_gate"]), full["w_ple"],
        row(small["g_ple"]), row(small["g_final"]))
    keys = ["w_ple_gate", "w_ple"]
    dfg, dfu, act = carry(_ffn_bwd_a, dx2, fg, fu, full["w_ffn_down"], job=exchange(keys, gw), keys=keys, sink=got)
    tn_d = min(d, 512)
    gw["w_ffn_down"] = _matmul_tn(act, dx2, tn_d, "dw_ffn_down", BF16)
    dx1, h2, vec_ffn = _ffn_bwd_b(dfg, dfu, x1, dx2, row(small["g_ffn"]), full["w_ffn_gate"], full["w_ffn_up"])
    gw["w_ffn_gate"] = _matmul_tn(dfg, h2, tn_d, "dw_ffn_gate", BF16)
    gw["w_ffn_up"] = _matmul_tn(dfu, h2, tn_d, "dw_ffn_up", BF16)
    keys = ["w_ffn_down"]
    dza, dzb, dya, dyb, gw["w_o"], gw["w_a_out"], gw["w_b_out"] = carry(
        _merge_bwd, dx1, merged, ma, mb, za, zb, y_a, h, full["w_o"], full["w_a_out"], full["w_b_out"],
        job=exchange(keys, gw), keys=keys, sink=got)
    keys = ["w_ffn_gate"]
    du_b, dw_r, dw_i, vec_lru = carry(
        _lru_bwd, dyb, xc, hprev, u_b, conv_w, wr_blk, row(small["b_r"]), wi_blk, row(small["b_i"]),
        row(small["lru_lambda"]), hd, job=exchange(keys, gw), keys=keys, sink=got)
    gw["conv_w"] = vec_lru[4:4 + CONV_WIDTH]
    keys = ["w_ffn_up"]
    du_a, lam_r, lam_i, dy16, gw["w_glu"], vec_sa, vec_gn = carry(
        _s5_bwd, dya, y, sr, si, u_a, full["w_glu"], row(small["b_glu"]), cre_blk, cimn_blk, bbr_blk, bbi_blk, ar_row,
        ai_row, d_row, job=exchange(keys, gw), keys=keys, sink=got)
    dbb_re, dbb_im, dc_re, dc_imn = _s5_param_grads(lam_r, lam_i, sr, si, u_a, dy16, pch, n)
    smalls = {"vec_tail": vec_tail, "vec_ffn": vec_ffn, "vec_lru": vec_lru, "vec_sa": vec_sa, "vec_gn": vec_gn,
              "dw_r": dw_r, "dw_i": dw_i, "dbb_re": dbb_re, "dbb_im": dbb_im, "dc_re": dc_re, "dc_imn": dc_imn,
              "loss": loss_blk}
    shapes = {k: a.shape for k, a in smalls.items()}
    keys = ["w_o", "w_a_out", "w_b_out"]
    early = list(smalls)
    job = _Both(exchange(keys, gw), _Gather([smalls[k] for k in early])) if distributed else None
    grad_x, h0, dz16, smalls["vec_mix"], smalls["vec_bin"] = carry(
        _inproj_bwd, [du_a, du_b, dza, dzb], x, dx1, row(small["g_mix"]), full["w_in"],
        job=job, keys=keys + early, sink=got)
    shapes.update(vec_mix=smalls["vec_mix"].shape, vec_bin=smalls["vec_bin"].shape)
    gw["w_in"] = _matmul_tn(dz16, h0, tn_d, "dw_in", BF16)
    if distributed:
        smalls.update({k: got.pop(k) for k in early})
        got.update({k: gw[k] for k in ("w_in", "w_glu", "conv_w")})
        gw = got
    return grad_x, gw, smalls, shapes


def _disc_inputs(small):
    g, n, pch = small["s5_b_re"].shape
    srcs = (small["lam_re"], small["lam_im"], jnp.repeat(small["log_dt"], n))
    return {"rows": [a.reshape(1, g * n) for a in srcs], "cols": [a.reshape(g * n, 1) for a in srcs],
            "b_re": small["s5_b_re"].reshape(g * n, pch), "b_im": small["s5_b_im"].reshape(g * n, pch)}


def kernel(x, p, g_mix, w_in, b_in, lam_re, lam_im, log_dt, s5_b_re, s5_b_im, s5_c_re, s5_c_im, s5_d, w_glu, b_glu, conv_w, conv_b, w_r, b_r, w_i, b_i, lru_lambda, w_a_out, w_b_out, w_o, g_ffn, w_ffn_gate, w_ffn_up, w_ffn_down, g_ple_gate, w_ple_gate, b_ple_gate, w_ple, g_ple, g_final, loss_target, m_g_mix, m_w_in, m_b_in, m_lam_re, m_lam_im, m_log_dt, m_s5_b_re, m_s5_b_im, m_s5_c_re, m_s5_c_im, m_s5_d, m_w_glu, m_b_glu, m_conv_w, m_conv_b, m_w_r, m_b_r, m_w_i, m_b_i, m_lru_lambda, m_w_a_out, m_w_b_out, m_w_o, m_g_ffn, m_w_ffn_gate, m_w_ffn_up, m_w_ffn_down, m_g_ple_gate, m_w_ple_gate, m_b_ple_gate, m_w_ple, m_g_ple, m_g_final, v_g_mix, v_w_in, v_b_in, v_lam_re, v_lam_im, v_log_dt, v_s5_b_re, v_s5_b_im, v_s5_c_re, v_s5_c_im, v_s5_d, v_w_glu, v_b_glu, v_conv_w, v_conv_b, v_w_r, v_b_r, v_w_i, v_b_i, v_lru_lambda, v_w_a_out, v_w_b_out, v_w_o, v_g_ffn, v_w_ffn_gate, v_w_ffn_up, v_w_ffn_down, v_g_ple_gate, v_w_ple_gate, v_b_ple_gate, v_w_ple, v_g_ple, v_g_final):
    given = dict(locals())
    wts = {k: given[k] for k in WEIGHTS}
    moms = {k: given["m_" + k] for k in WEIGHTS}
    vels = {k: given["v_" + k] for k in WEIGHTS}

    def drop_depth(k, a):
        return a if k == "g_final" else a[0]

    small = {k: drop_depth(k, wts[k]) for k in SMALL}
    shard = {k: wts[k][0] for k in SHARDED}
    names = list(SHARDED)

    def wire(k):
        if k == "conv_w":
            return shard[k]
        return (shard[k].T if k in TRANSPOSED else shard[k]).astype(BF16)

    disc = _disc_inputs(small)
    grad_x, parts, smalls, shapes = _local_step(x[0], p[0, 0], loss_target[0], {k: wire(k) for k in names}, small, disc)

    last = ["w_in", "w_glu", "conv_w"]
    late = ["vec_mix", "vec_bin"]
    received = _alone(_Both(_Exchange([_wire_grad(k, parts[k]) for k in last]), _Gather([smalls[k] for k in late])),
                      "exchange_last")
    parts.update(zip(last, received[:len(last)]))
    smalls.update(zip(late, received[len(last):]))

    g_small = _small_reduce(smalls, shapes, *disc["cols"], disc["b_re"], disc["b_im"], small["s5_b_re"].shape[0])
    loss = g_small.pop("loss")[0, 0]
    natural = lambda k, a: a.reshape((1, -1) if k == "g_final" else wts[k].shape)
    small_g = [natural(k, g_small[k]) for k in SMALL]
    deltas, new_m, new_v = _adamw_small([natural(k, wts[k]) for k in SMALL], small_g,
                                        [natural(k, moms[k]) for k in SMALL], [natural(k, vels[k]) for k in SMALL])
    small_out = [dict(zip(SMALL, [a.reshape(wts[k].shape) for k, a in zip(SMALL, slot)]))
                 for slot in (small_g, deltas, new_m, new_v)]

    big_out = {}
    for k in names:
        big_out[k] = _adamw(parts[k], shard[k], moms[k][0], vels[k][0], "adamw_" + k, transposed=k in TRANSPOSED)

    outs = [loss, grad_x[None]]
    for slot in range(4):
        for k in WEIGHTS:
            if k in SHARDED:
                outs.append(big_out[k][slot][None])
            else:
                outs.append(small_out[slot][k])
    return tuple(outs)
```

```python
import math

import jax
import jax.numpy as jnp
from jax import lax
from jax.experimental import pallas as pl
from jax.experimental.pallas import tpu as pltpu

F32 = jnp.float32
BF16 = jnp.bfloat16

EPS = 1e-6
LRU_C = 8.0
CONV_WIDTH = 4
ADAM_LR = 0.001
ADAM_B1 = 0.9
ADAM_B2 = 0.999
ADAM_EPS = 1e-08
ADAM_WD = 0.01
ADAM_STEP = 10

N_DEV = 8
MESH = pl.DeviceIdType.MESH
SUBLANES = 8
LANES = 128
VMEM_LIMIT = 56 * 1024 * 1024
TOKEN_TILE = 256
TIME_CHUNK = 256


def _dot(a, b):
    return jnp.dot(a.astype(BF16), b.astype(BF16), preferred_element_type=F32)


def _dot_nt(a, b):
    return lax.dot_general(a.astype(BF16), b.astype(BF16), (((1,), (1,)), ((), ())), preferred_element_type=F32)


def _dot_tn(a, b):
    return lax.dot_general(a.astype(BF16), b.astype(BF16), (((0,), (0,)), ((), ())), preferred_element_type=F32)


def _sigmoid(x):
    return jax.nn.sigmoid(x)


def _rms_stats(x):
    r = lax.rsqrt(jnp.mean(x * x, axis=-1, keepdims=True) + EPS)
    return x * r, r


def _rms_bwd(dy, xhat, r, g):
    dxn = dy * g
    dx = r * (dxn - xhat * jnp.mean(dxn * xhat, axis=-1, keepdims=True))
    return dx, dy * xhat


def _rowsum(v):
    return jnp.sum(v, axis=0, keepdims=True)


def _expm1(x):
    u = jnp.exp(x)
    um1 = u - 1.0
    safe = jnp.where(um1 == 0.0, 1.0, jnp.log(u))
    return jnp.where(um1 == 0.0, x, um1 * x / safe)


def _softplus(x):
    e = jnp.exp(-jnp.abs(x))
    u = 1.0 + e
    um1 = u - 1.0
    safe = jnp.where(um1 == 0.0, 1.0, um1)
    log1p_e = jnp.where(um1 == 0.0, e, jnp.log(u) * e / safe)
    return jnp.maximum(x, 0.0) + log1p_e


_GELU_K = math.sqrt(2.0 / math.pi)
_GELU_C = 0.044715


def _gelu(x):
    return 0.5 * x * (1.0 + jnp.tanh(_GELU_K * (x + _GELU_C * x * x * x)))


def _gelu_grad(x):
    th = jnp.tanh(_GELU_K * (x + _GELU_C * x * x * x))
    return 0.5 * (1.0 + th) + 0.5 * x * (1.0 - th * th) * _GELU_K * (1.0 + 3.0 * _GELU_C * x * x)


def _params(*sem):
    return pltpu.CompilerParams(dimension_semantics=sem, vmem_limit_bytes=VMEM_LIMIT)


def _rows(tm, n):
    return pl.BlockSpec((tm, n), lambda i: (i, 0))


def _rows_rev(tm, n, steps):
    return pl.BlockSpec((tm, n), lambda i: (steps - 1 - i, 0))


def _whole(shape):
    nd = len(shape)
    return pl.BlockSpec(shape, lambda i: (0,) * nd, pipeline_mode=pl.Buffered(1))


def _acc(shape):
    nd = len(shape)
    return pl.BlockSpec(shape, lambda i: (0,) * nd)


def _zero_on_first(*refs):
    @pl.when(pl.program_id(0) == 0)
    def _():
        for r in refs:
            r[...] = jnp.zeros_like(r)


def _inproj_fwd(x, g_mix, w_in_t, b_in, widths, comm=None):
    t, d = x.shape
    n = w_in_t.shape[0]
    tm = min(TOKEN_TILE, t)
    offs = [sum(widths[:i]) for i in range(len(widths) + 1)]

    def body(x_ref, g_ref, w_ref, b_ref, *outs):
        xhat, _ = _rms_stats(x_ref[...])
        h = (xhat * g_ref[...]).astype(BF16)
        for k, o_ref in enumerate(outs):
            lo, hi = offs[k], offs[k + 1]
            o_ref[...] = _dot_nt(h, w_ref[lo:hi, :]) + b_ref[:, lo:hi]

    return _run(
        body, comm, name="inproj_fwd", grid=(t // tm,),
        out_shape=[jax.ShapeDtypeStruct((t, w), F32) for w in widths],
        in_specs=[_rows(tm, d), _whole((1, d)), _whole((n, d)), _whole((1, n))],
        out_specs=[_rows(tm, w) for w in widths],
        semantics="parallel",
    )(x, g_mix, w_in_t, b_in)


def _s5_fwd(u, bbr_blk, bbi_blk, ar, ai, cre_blk, cimn_blk, d_skip, w_glu, b_glu, comm=None):
    t, sa = u.shape
    gn = bbr_blk.shape[0]
    tc = min(TIME_CHUNK, t)

    def body(u_ref, bbr_ref, bbi_ref, ar_ref, ai_ref, cre_ref, cim_ref, d_ref, wg_ref, bg_ref,
             sr_ref, si_ref, y_ref, ya_ref, cr_s, ci_s):
        _zero_on_first(cr_s, ci_s)
        uv = u_ref[...]
        ub = uv.astype(BF16)
        sr_ref[...] = _dot_nt(ub, bbr_ref[...])
        si_ref[...] = _dot_nt(ub, bbi_ref[...])
        a_r = ar_ref[...]
        a_i = ai_ref[...]

        def step(row, carry):
            c_r, c_i = carry
            at = pl.ds(row, 1)
            n_r = a_r * c_r - a_i * c_i + sr_ref[at, :]
            n_i = a_r * c_i + a_i * c_r + si_ref[at, :]
            sr_ref[at, :] = n_r
            si_ref[at, :] = n_i
            return n_r, n_i

        c_r, c_i = lax.fori_loop(0, tc, step, (cr_s[0:1, :], ci_s[0:1, :]), unroll=8)
        cr_s[0:1, :] = c_r
        ci_s[0:1, :] = c_i
        y = _dot_nt(sr_ref[...], cre_ref[...]) + _dot_nt(si_ref[...], cim_ref[...]) + d_ref[...] * uv
        y_ref[...] = y
        zz = _gelu(y)
        q = _dot(zz, wg_ref[...]) + bg_ref[...]
        ya_ref[...] = zz * _sigmoid(q)

    return _run(
        body, comm, name="s5_fwd", grid=(t // tc,),
        out_shape=[jax.ShapeDtypeStruct((t, gn), F32), jax.ShapeDtypeStruct((t, gn), F32),
                   jax.ShapeDtypeStruct((t, sa), F32), jax.ShapeDtypeStruct((t, sa), F32)],
        in_specs=[_rows(tc, sa), _whole((gn, sa)), _whole((gn, sa)), _whole((1, gn)), _whole((1, gn)),
                  _whole((sa, gn)), _whole((sa, gn)), _whole((1, sa)), _whole((sa, sa)), _whole((1, sa))],
        out_specs=[_rows(tc, gn), _rows(tc, gn), _rows(tc, sa), _rows(tc, sa)],
        scratch_shapes=[pltpu.VMEM((SUBLANES, gn), F32), pltpu.VMEM((SUBLANES, gn), F32)],
        semantics="arbitrary",
    )(u, bbr_blk, bbi_blk, ar, ai, cre_blk, cimn_blk, d_skip, w_glu, b_glu)


def _lru_gates(xc, wr_ref, br_ref, wi_ref, bi_ref, lam_ref):
    r = _sigmoid(_dot(xc, wr_ref[...]) + br_ref[...])
    ig = _sigmoid(_dot(xc, wi_ref[...]) + bi_ref[...])
    sp = _softplus(-lam_ref[...])
    log_a = (-LRU_C * r) * sp
    return r, ig, sp, log_a


def _lru_fwd(u, conv_w, conv_b, wr_blk, b_r, wi_blk, b_i, lru_lambda, comm=None):
    t, w = u.shape
    tc = min(TIME_CHUNK, t)
    halo = SUBLANES

    def body(u_ref, cw_ref, cb_ref, wr_ref, br_ref, wi_ref, bi_ref, lam_ref,
             xc_ref, h_ref, hp_ref, ext_s, a_s, carry_s):
        @pl.when(pl.program_id(0) == 0)
        def _():
            ext_s[0:halo, :] = jnp.zeros((halo, w), F32)
            carry_s[...] = jnp.zeros_like(carry_s)

        ext_s[halo:halo + tc, :] = u_ref[...]
        xc = cb_ref[...]
        for k in range(CONV_WIDTH):
            off = halo - (CONV_WIDTH - 1) + k
            xc = xc + cw_ref[k:k + 1, :] * ext_s[off:off + tc, :]
        ext_s[0:halo, :] = ext_s[tc:tc + halo, :]
        xc_ref[...] = xc
        r, ig, sp, log_a = _lru_gates(xc, wr_ref, br_ref, wi_ref, bi_ref, lam_ref)
        a_s[...] = jnp.exp(log_a)
        h_ref[...] = jnp.sqrt(-_expm1(2.0 * log_a)) * ig * xc

        def step(row, carry):
            at = pl.ds(row, 1)
            hp_ref[at, :] = carry
            nxt = a_s[at, :] * carry + h_ref[at, :]
            h_ref[at, :] = nxt
            return nxt

        carry_s[0:1, :] = lax.fori_loop(0, tc, step, carry_s[0:1, :], unroll=8)

    return _run(
        body, comm, name="lru_fwd", grid=(t // tc,),
        out_shape=[jax.ShapeDtypeStruct((t, w), F32)] * 3,
        in_specs=[_rows(tc, w), _whole((CONV_WIDTH, w)), _whole((1, w)), _whole((w, w)), _whole((1, w)),
                  _whole((w, w)), _whole((1, w)), _whole((1, w))],
        out_specs=[_rows(tc, w)] * 3,
        scratch_shapes=[pltpu.VMEM((halo + tc, w), F32), pltpu.VMEM((tc, w), F32), pltpu.VMEM((SUBLANES, w), F32)],
        semantics="arbitrary",
    )(u, conv_w, conv_b, wr_blk, b_r, wi_blk, b_i, lru_lambda)


def _merge_fwd(y_a, h, za, zb, x, w_a_out_t, w_b_out, w_o, comm=None):
    t, d = x.shape
    sa, lw = y_a.shape[1], h.shape[1]
    tm = min(TOKEN_TILE, t)

    def body(ya_ref, h_ref, za_ref, zb_ref, x_ref, wa_ref, wb_ref, wo_ref, x1_ref, mg_ref, ma_ref, mb_ref):
        ma = _dot_nt(ya_ref[...], wa_ref[...])
        mb = _dot(h_ref[...], wb_ref[...])
        merged = _sigmoid(za_ref[...]) * ma + _sigmoid(zb_ref[...]) * mb
        ma_ref[...] = ma
        mb_ref[...] = mb
        mg_ref[...] = merged.astype(mg_ref.dtype)
        x1_ref[...] = x_ref[...] + _dot(merged, wo_ref[...])

    return _run(
        body, comm, name="merge_fwd", grid=(t // tm,),
        out_shape=[jax.ShapeDtypeStruct((t, d), F32), jax.ShapeDtypeStruct((t, d), BF16),
                   jax.ShapeDtypeStruct((t, d), F32), jax.ShapeDtypeStruct((t, d), F32)],
        in_specs=[_rows(tm, sa), _rows(tm, lw), _rows(tm, d), _rows(tm, d), _rows(tm, d),
                  _whole((d, sa)), _whole((lw, d)), _whole((d, d))],
        out_specs=[_rows(tm, d)] * 4,
        semantics="parallel",
    )(y_a, h, za, zb, x, w_a_out_t, w_b_out, w_o)


def _ffn_up_fwd(x1, g_ffn, w_gate_t, w_up_t, comm=None):
    t, d = x1.shape
    f = w_gate_t.shape[0]
    tm = min(TOKEN_TILE, t)

    def body(x_ref, g_ref, wg_ref, wu_ref, fg_ref, fu_ref):
        xhat, _ = _rms_stats(x_ref[...])
        h2 = (xhat * g_ref[...]).astype(BF16)
        fg_ref[...] = _dot_nt(h2, wg_ref[...])
        fu_ref[...] = _dot_nt(h2, wu_ref[...])

    return _run(
        body, comm, name="ffn_up_fwd", grid=(t // tm,),
        out_shape=[jax.ShapeDtypeStruct((t, f), F32)] * 2,
        in_specs=[_rows(tm, d), _whole((1, d)), _whole((f, d)), _whole((f, d))],
        out_specs=[_rows(tm, f)] * 2,
        semantics="parallel",
    )(x1, g_ffn, w_gate_t, w_up_t)


def _ffn_down_fwd(fg, fu, x1, w_down):
    t, d = x1.shape
    f = fg.shape[1]
    tm = min(TOKEN_TILE, t)

    def body(fg_ref, fu_ref, x_ref, wd_ref, x2_ref):
        fgv = fg_ref[...]
        act = fgv * _sigmoid(fgv) * fu_ref[...]
        x2_ref[...] = x_ref[...] + _dot(act, wd_ref[...])

    return pl.pallas_call(
        body, name="ffn_down_fwd", grid=(t // tm,),
        out_shape=jax.ShapeDtypeStruct((t, d), F32),
        in_specs=[_rows(tm, f), _rows(tm, f), _rows(tm, d), _whole((f, d))],
        out_specs=_rows(tm, d),
        compiler_params=_params("parallel"),
    )(fg, fu, x1, w_down)


def _store_on_last(pairs):
    @pl.when(pl.program_id(0) == pl.num_programs(0) - 1)
    def _():
        for acc, out in pairs:
            out[...] = acc[...].astype(out.dtype)


def _tail_fwd_bwd(x2, p, target, g_pg, w_pg, b_pg, w_ple_t, g_ple, g_final):
    t, d = x2.shape
    pd = p.shape[1]
    tm = min(TOKEN_TILE, t)

    def body(x2_ref, p_ref, tg_ref, gpg_ref, wpg_ref, bpg_ref, wple_ref, gple_ref, gfin_ref,
             dx2_ref, loss_ref, dwpg_out, dwple_out, vec_ref, dwpg_ref, dwple_ref):
        _zero_on_first(loss_ref, dwpg_ref, dwple_ref, vec_ref)
        x2v = x2_ref[...]
        xh2, r2 = _rms_stats(x2v)
        h3 = xh2 * gpg_ref[...]
        gp = _sigmoid(_dot(h3, wpg_ref[...]) + bpg_ref[...])
        pe = _dot_nt(p_ref[...], wple_ref[...])
        peh, r3 = _rms_stats(pe)
        e = peh * gple_ref[...]
        x3 = x2v + gp * e
        xh3, r4 = _rms_stats(x3)
        diff = xh3 * gfin_ref[...] - tg_ref[...]
        loss_ref[...] += 0.5 * jnp.sum(jnp.mean(diff * diff, axis=-1, keepdims=True))
        dy = diff * (1.0 / d)
        dx3, dgfin = _rms_bwd(dy, xh3, r4, gfin_ref[...])
        d_gp = dx3 * e
        d_e = dx3 * gp
        dpe, dgple = _rms_bwd(d_e, peh, r3, gple_ref[...])
        dwple_ref[...] += _dot_tn(dpe, p_ref[...])
        dpre = d_gp * gp * (1.0 - gp)
        dwpg_ref[...] += _dot_tn(h3, dpre)
        dh3 = _dot_nt(dpre, wpg_ref[...])
        dx2n, dgpg = _rms_bwd(dh3, xh2, r2, gpg_ref[...])
        dx2_ref[...] = dx3 + dx2n
        vec_ref[0:1, :] += _rowsum(dpre)
        vec_ref[1:2, :] += _rowsum(dgpg)
        vec_ref[2:3, :] += _rowsum(dgple)
        vec_ref[3:4, :] += _rowsum(dgfin)
        _store_on_last([(dwpg_ref, dwpg_out), (dwple_ref, dwple_out)])

    return pl.pallas_call(
        body, name="tail_fwd_bwd", grid=(t // tm,),
        out_shape=[jax.ShapeDtypeStruct((t, d), F32), jax.ShapeDtypeStruct((SUBLANES, LANES), F32),
                   jax.ShapeDtypeStruct((d, d), BF16), jax.ShapeDtypeStruct((d, pd), BF16),
                   jax.ShapeDtypeStruct((SUBLANES, d), F32)],
        in_specs=[_rows(tm, d), _rows(tm, pd), _rows(tm, d), _whole((1, d)), _whole((d, d)), _whole((1, d)),
                  _whole((d, pd)), _whole((1, d)), _whole((1, d))],
        out_specs=[_rows(tm, d), _acc((SUBLANES, LANES)), _acc((d, d)), _acc((d, pd)), _acc((SUBLANES, d))],
        scratch_shapes=[pltpu.VMEM((d, d), F32), pltpu.VMEM((d, pd), F32)],
        compiler_params=_params("arbitrary"),
    )(x2, p, target, g_pg, w_pg, b_pg, w_ple_t, g_ple, g_final)


def _ffn_bwd_a(dx2, fg, fu, w_down, comm=None):
    t, d = dx2.shape
    f = fg.shape[1]
    tm = min(TOKEN_TILE, t)

    def body(dx_ref, fg_ref, fu_ref, wd_ref, dfg_ref, dfu_ref, act_ref):
        dact = _dot_nt(dx_ref[...], wd_ref[...])
        fgv = fg_ref[...]
        fuv = fu_ref[...]
        sg = _sigmoid(fgv)
        silu = fgv * sg
        dfu_ref[...] = (dact * silu).astype(dfu_ref.dtype)
        dfg_ref[...] = (dact * fuv * (sg * (1.0 + fgv * (1.0 - sg)))).astype(dfg_ref.dtype)
        act_ref[...] = (silu * fuv).astype(act_ref.dtype)

    return _run(
        body, comm, name="ffn_bwd_a", grid=(t // tm,),
        out_shape=[jax.ShapeDtypeStruct((t, f), BF16)] * 3,
        in_specs=[_rows(tm, d), _rows(tm, f), _rows(tm, f), _whole((f, d))],
        out_specs=[_rows(tm, f)] * 3,
        semantics="parallel",
    )(dx2, fg, fu, w_down)


def _ffn_bwd_b(dfg, dfu, x1, dx2, g_ffn, w_gate_t, w_up_t, comm=None):
    t, d = x1.shape
    f = dfg.shape[1]
    tm = min(TOKEN_TILE, t)

    def body(dfg_ref, dfu_ref, x_ref, dx2_ref, g_ref, wg_ref, wu_ref, dx1_ref, h2_ref, vec_ref):
        _zero_on_first(vec_ref)
        dh2 = _dot(dfg_ref[...], wg_ref[...]) + _dot(dfu_ref[...], wu_ref[...])
        xhat, r = _rms_stats(x_ref[...])
        h2_ref[...] = (xhat * g_ref[...]).astype(h2_ref.dtype)
        dxn, dg = _rms_bwd(dh2, xhat, r, g_ref[...])
        dx1_ref[...] = dx2_ref[...] + dxn
        vec_ref[0:1, :] += _rowsum(dg)

    return _run(
        body, comm, name="ffn_bwd_b", grid=(t // tm,),
        out_shape=[jax.ShapeDtypeStruct((t, d), F32), jax.ShapeDtypeStruct((t, d), BF16),
                   jax.ShapeDtypeStruct((SUBLANES, d), F32)],
        in_specs=[_rows(tm, f), _rows(tm, f), _rows(tm, d), _rows(tm, d), _whole((1, d)), _whole((f, d)), _whole((f, d))],
        out_specs=[_rows(tm, d), _rows(tm, d), _acc((SUBLANES, d))],
        semantics="arbitrary",
    )(dfg, dfu, x1, dx2, g_ffn, w_gate_t, w_up_t)


def _matmul_tn(a, b, tn, name, dtype=F32):
    t, k = a.shape
    n = b.shape[1]

    def body(a_ref, b_ref, o_ref):
        o_ref[...] = _dot_tn(a_ref[...], b_ref[...]).astype(o_ref.dtype)

    return pl.pallas_call(
        body, name=name, grid=(n // tn,),
        out_shape=jax.ShapeDtypeStruct((k, n), dtype),
        in_specs=[_whole((t, k)), pl.BlockSpec((t, tn), lambda j: (0, j))],
        out_specs=pl.BlockSpec((k, tn), lambda j: (0, j)),
        compiler_params=_params("parallel"),
    )(a, b)


def _merge_bwd(dx1, merged, ma, mb, za, zb, y_a, h, w_o, w_a_out_t, w_b_out, comm=None):
    t, d = dx1.shape
    sa, lw = y_a.shape[1], h.shape[1]
    tm = min(TOKEN_TILE, t)

    def body(dx1_ref, mg_ref, ma_ref, mb_ref, za_ref, zb_ref, ya_ref, h_ref, wo_ref, wa_ref, wb_ref,
             dza_ref, dzb_ref, dya_ref, dyb_ref, dwo_out, dwa_out, dwb_out, dwo_ref, dwa_ref, dwb_ref):
        _zero_on_first(dwo_ref, dwa_ref, dwb_ref)
        dx1v = dx1_ref[...].astype(BF16)
        dmg = _dot_nt(dx1v, wo_ref[...])
        ga = _sigmoid(za_ref[...])
        gb = _sigmoid(zb_ref[...])
        dza_ref[...] = dmg * ma_ref[...] * ga * (1.0 - ga)
        dzb_ref[...] = dmg * mb_ref[...] * gb * (1.0 - gb)
        dma = (dmg * ga).astype(BF16)
        dmb = (dmg * gb).astype(BF16)
        dya_ref[...] = _dot(dma, wa_ref[...])
        dyb_ref[...] = _dot_nt(dmb, wb_ref[...])
        dwo_ref[...] += _dot_tn(mg_ref[...], dx1v)
        dwa_ref[...] += _dot_tn(dma, ya_ref[...])
        dwb_ref[...] += _dot_tn(h_ref[...], dmb)
        _store_on_last([(dwo_ref, dwo_out), (dwa_ref, dwa_out), (dwb_ref, dwb_out)])

    return _run(
        body, comm, name="merge_bwd", grid=(t // tm,),
        out_shape=[jax.ShapeDtypeStruct((t, d), F32), jax.ShapeDtypeStruct((t, d), F32),
                   jax.ShapeDtypeStruct((t, sa), F32), jax.ShapeDtypeStruct((t, lw), F32),
                   jax.ShapeDtypeStruct((d, d), BF16), jax.ShapeDtypeStruct((d, sa), BF16),
                   jax.ShapeDtypeStruct((lw, d), BF16)],
        in_specs=[_rows(tm, d), _rows(tm, d), _rows(tm, d), _rows(tm, d), _rows(tm, d), _rows(tm, d),
                  _rows(tm, sa), _rows(tm, lw), _whole((d, d)), _whole((d, sa)), _whole((lw, d))],
        out_specs=[_rows(tm, d), _rows(tm, d), _rows(tm, sa), _rows(tm, lw), _acc((d, d)), _acc((d, sa)), _acc((lw, d))],
        scratch_shapes=[pltpu.VMEM((d, d), F32), pltpu.VMEM((d, sa), F32), pltpu.VMEM((lw, d), F32)],
        semantics="arbitrary",
    )(dx1, merged, ma, mb, za, zb, y_a, h, w_o, w_a_out_t, w_b_out)


def _fold_diag_blocks(dense, row_group, col_group, row0=0, col0=0):
    r, c = dense.shape
    rows = lax.broadcasted_iota(jnp.int32, (r, c), 0) + row0
    cols = lax.broadcasted_iota(jnp.int32, (r, c), 1) + col0
    kept = jnp.where(rows // row_group == cols // col_group, dense, 0.0)
    pick = (lax.broadcasted_iota(jnp.int32, (row_group, r), 0)
            == lax.broadcasted_iota(jnp.int32, (row_group, r), 1) % row_group).astype(F32)
    return jnp.dot(pick, kept, preferred_element_type=F32, precision=lax.Precision.HIGHEST)


def _lru_bwd(dh, xc, hprev, u, conv_w, wr_blk, b_r, wi_blk, b_i, lru_lambda, head_dim, comm=None):
    t, w = dh.shape
    tc = min(TIME_CHUNK, t)
    steps = t // tc
    halo = SUBLANES
    sub_per_chunk = tc // halo

    def body(dh_ref, xc_ref, hp_ref, u_ref, uh_ref, cw_ref, wr_ref, br_ref, wi_ref, bi_ref, lam_ref,
             du_ref, dwr_out, dwi_out, vec_ref, lam_s, a_s, dxc_s, uext_s, carry_s, dwr_ref, dwi_ref):
        chunk = steps - 1 - pl.program_id(0)

        @pl.when(pl.program_id(0) == 0)
        def _():
            carry_s[...] = jnp.zeros_like(carry_s)
            dxc_s[tc:tc + halo, :] = jnp.zeros((halo, w), F32)
            dwr_ref[...] = jnp.zeros_like(dwr_ref)
            dwi_ref[...] = jnp.zeros_like(dwi_ref)
            vec_ref[...] = jnp.zeros_like(vec_ref)

        xc = xc_ref[...]
        r, ig, sp, log_a = _lru_gates(xc, wr_ref, br_ref, wi_ref, bi_ref, lam_ref)
        a = jnp.exp(log_a)
        a_s[...] = a

        def step(i, q):
            at = pl.ds(tc - 1 - i, 1)
            lam_row = dh_ref[at, :] + q
            lam_s[at, :] = lam_row
            return a_s[at, :] * lam_row

        carry_s[0:1, :] = lax.fori_loop(0, tc, step, carry_s[0:1, :], unroll=8)
        lam = lam_s[...]
        mult = jnp.sqrt(-_expm1(2.0 * log_a))
        d_log_a = lam * hp_ref[...] * a - (lam * ig * xc) * (a * a) / mult
        d_ig = lam * mult * xc
        dpre_r = (d_log_a * (-LRU_C * sp)) * r * (1.0 - r)
        dpre_i = d_ig * ig * (1.0 - ig)
        dxc = lam * mult * ig + _dot_nt(dpre_r, wr_ref[...]) + _dot_nt(dpre_i, wi_ref[...])
        dwr_ref[...] += _dot_tn(dpre_r, xc)
        dwi_ref[...] += _dot_tn(dpre_i, xc)
        vec_ref[0:1, :] += _rowsum(dxc)
        vec_ref[1:2, :] += _rowsum(dpre_r)
        vec_ref[2:3, :] += _rowsum(dpre_i)
        vec_ref[3:4, :] += _rowsum(d_log_a * (-LRU_C * r)) * (-_sigmoid(-lam_ref[...]))
        dxc_s[0:tc, :] = dxc
        du = cw_ref[CONV_WIDTH - 1:CONV_WIDTH, :] * dxc
        for k in range(CONV_WIDTH - 1):
            off = CONV_WIDTH - 1 - k
            du = du + cw_ref[k:k + 1, :] * dxc_s[off:off + tc, :]
        du_ref[...] = du
        dxc_s[tc:tc + halo, :] = dxc_s[0:halo, :]
        uext_s[0:halo, :] = jnp.where(chunk > 0, uh_ref[...], 0.0)
        uext_s[halo:halo + tc, :] = u_ref[...]
        for k in range(CONV_WIDTH):
            off = halo - (CONV_WIDTH - 1) + k
            vec_ref[4 + k:5 + k, :] += _rowsum(dxc * uext_s[off:off + tc, :])

        @pl.when(pl.program_id(0) == steps - 1)
        def _():
            dwr_out[...] = _fold_diag_blocks(dwr_ref[...], head_dim, head_dim)
            dwi_out[...] = _fold_diag_blocks(dwi_ref[...], head_dim, head_dim)

    halo_spec = pl.BlockSpec((halo, w), lambda i: (jnp.maximum((steps - 1 - i) * sub_per_chunk - 1, 0), 0))
    return _run(
        body, comm, name="lru_bwd", grid=(steps,),
        out_shape=[jax.ShapeDtypeStruct((t, w), F32), jax.ShapeDtypeStruct((head_dim, w), F32),
                   jax.ShapeDtypeStruct((head_dim, w), F32), jax.ShapeDtypeStruct((SUBLANES, w), F32)],
        in_specs=[_rows_rev(tc, w, steps)] * 4 + [halo_spec, _whole((CONV_WIDTH, w)), _whole((w, w)), _whole((1, w)),
                                                  _whole((w, w)), _whole((1, w)), _whole((1, w))],
        out_specs=[_rows_rev(tc, w, steps), _acc((head_dim, w)), _acc((head_dim, w)), _acc((SUBLANES, w))],
        scratch_shapes=[pltpu.VMEM((tc, w), F32), pltpu.VMEM((tc, w), F32), pltpu.VMEM((tc + halo, w), F32),
                        pltpu.VMEM((halo + tc, w), F32), pltpu.VMEM((SUBLANES, w), F32), pltpu.VMEM((w, w), F32),
                        pltpu.VMEM((w, w), F32)],
        semantics="arbitrary",
    )(dh, xc, hprev, u, u, conv_w, wr_blk, b_r, wi_blk, b_i, lru_lambda)


def _s5_bwd(dya, y, sr, si, u, w_glu, b_glu, cre_blk, cimn_blk, bbr_blk, bbi_blk, ar, ai, d_skip, comm=None):
    t, sa = dya.shape
    gn = sr.shape[1]
    tc = min(TIME_CHUNK, t)
    steps = t // tc
    halo = SUBLANES

    def body(dya_ref, y_ref, sr_ref, si_ref, u_ref, wg_ref, bg_ref, cre_ref, cim_ref, bbr_ref, bbi_ref,
             ar_ref, ai_ref, d_ref, du_ref, lr_ref, li_ref, dy_ref, dwg_out, vsa_ref, vgn_ref, gr_s, gi_s, cr_s, ci_s,
             dwg_ref):
        @pl.when(pl.program_id(0) == 0)
        def _():
            cr_s[...] = jnp.zeros_like(cr_s)
            ci_s[...] = jnp.zeros_like(ci_s)
            gr_s[tc:tc + halo, :] = jnp.zeros((halo, gn), F32)
            gi_s[tc:tc + halo, :] = jnp.zeros((halo, gn), F32)
            dwg_ref[...] = jnp.zeros_like(dwg_ref)
            vsa_ref[...] = jnp.zeros_like(vsa_ref)
            vgn_ref[...] = jnp.zeros_like(vgn_ref)

        yv = y_ref[...]
        uv = u_ref[...]
        zz = _gelu(yv)
        sg = _sigmoid(_dot(zz, wg_ref[...]) + bg_ref[...])
        dyav = dya_ref[...]
        dq = dyav * zz * sg * (1.0 - sg)
        dzz = dyav * sg + _dot_nt(dq, wg_ref[...])
        dwg_ref[...] += _dot_tn(zz, dq)
        dy = dzz * _gelu_grad(yv)
        dyb = dy.astype(BF16)
        dy_ref[...] = dyb.astype(dy_ref.dtype)
        vsa_ref[0:1, :] += _rowsum(dq)
        vsa_ref[1:2, :] += _rowsum(dy * uv)
        gr_s[0:tc, :] = _dot(dyb, cre_ref[...])
        gi_s[0:tc, :] = _dot(dyb, cim_ref[...])
        a_r = ar_ref[...]
        a_i = ai_ref[...]

        def step(i, carry):
            l_r, l_i = carry
            at = pl.ds(tc - 1 - i, 1)
            n_r = gr_s[at, :] + a_r * l_r + a_i * l_i
            n_i = gi_s[at, :] + a_r * l_i - a_i * l_r
            gr_s[at, :] = n_r
            gi_s[at, :] = n_i
            return n_r, n_i

        l_r, l_i = lax.fori_loop(0, tc, step, (cr_s[0:1, :], ci_s[0:1, :]), unroll=8)
        cr_s[0:1, :] = l_r
        ci_s[0:1, :] = l_i
        nxt_r = gr_s[1:tc + 1, :]
        nxt_i = gi_s[1:tc + 1, :]
        srv = sr_ref[...]
        siv = si_ref[...]
        vgn_ref[0:1, :] += _rowsum(nxt_r * srv + nxt_i * siv)
        vgn_ref[1:2, :] += _rowsum(nxt_i * srv - nxt_r * siv)
        lam_r = gr_s[0:tc, :]
        lam_i = gi_s[0:tc, :]
        gr_s[tc:tc + halo, :] = gr_s[0:halo, :]
        gi_s[tc:tc + halo, :] = gi_s[0:halo, :]
        lrb = lam_r.astype(BF16)
        lib = lam_i.astype(BF16)
        lr_ref[...] = lrb.astype(lr_ref.dtype)
        li_ref[...] = lib.astype(li_ref.dtype)
        du_ref[...] = _dot(lrb, bbr_ref[...]) + _dot(lib, bbi_ref[...]) + dy * d_ref[...]
        _store_on_last([(dwg_ref, dwg_out)])

    return _run(
        body, comm, name="s5_bwd", grid=(steps,),
        out_shape=[jax.ShapeDtypeStruct((t, sa), F32), jax.ShapeDtypeStruct((t, gn), BF16),
                   jax.ShapeDtypeStruct((t, gn), BF16), jax.ShapeDtypeStruct((t, sa), BF16),
                   jax.ShapeDtypeStruct((sa, sa), BF16), jax.ShapeDtypeStruct((SUBLANES, sa), F32),
                   jax.ShapeDtypeStruct((SUBLANES, gn), F32)],
        in_specs=[_rows_rev(tc, sa, steps), _rows_rev(tc, sa, steps), _rows_rev(tc, gn, steps), _rows_rev(tc, gn, steps),
                  _rows_rev(tc, sa, steps), _whole((sa, sa)), _whole((1, sa)), _whole((sa, gn)), _whole((sa, gn)),
                  _whole((gn, sa)), _whole((gn, sa)), _whole((1, gn)), _whole((1, gn)), _whole((1, sa))],
        out_specs=[_rows_rev(tc, sa, steps), _rows_rev(tc, gn, steps), _rows_rev(tc, gn, steps), _rows_rev(tc, sa, steps),
                   _acc((sa, sa)), _acc((SUBLANES, sa)), _acc((SUBLANES, gn))],
        scratch_shapes=[pltpu.VMEM((tc + halo, gn), F32), pltpu.VMEM((tc + halo, gn), F32),
                        pltpu.VMEM((SUBLANES, gn), F32), pltpu.VMEM((SUBLANES, gn), F32), pltpu.VMEM((sa, sa), F32)],
        semantics="arbitrary",
    )(dya, y, sr, si, u, w_glu, b_glu, cre_blk, cimn_blk, bbr_blk, bbi_blk, ar, ai, d_skip)


def _inproj_bwd(dparts, x, dx1, g_mix, w_in_t, comm=None):
    t, d = x.shape
    n = w_in_t.shape[0]
    widths = [p.shape[1] for p in dparts]
    offs = [sum(widths[:i]) for i in range(len(widths) + 1)]
    tm = min(TOKEN_TILE, t)
    np_ = len(dparts)

    def body(*refs):
        dz_refs = refs[:np_]
        x_ref, dx1_ref, g_ref, w_ref, gx_ref, h_ref, dz_ref, vd_ref, vn_ref = refs[np_:]
        _zero_on_first(vd_ref, vn_ref)
        dh = jnp.zeros((tm, d), F32)
        for k, r in enumerate(dz_refs):
            lo, hi = offs[k], offs[k + 1]
            dzk = r[...]
            dh = dh + _dot(dzk, w_ref[lo:hi, :])
            dz_ref[:, lo:hi] = dzk.astype(dz_ref.dtype)
            vn_ref[0:1, lo:hi] += _rowsum(dzk)
        xhat, r0 = _rms_stats(x_ref[...])
        h_ref[...] = (xhat * g_ref[...]).astype(h_ref.dtype)
        dxn, dg = _rms_bwd(dh, xhat, r0, g_ref[...])
        gx_ref[...] = dx1_ref[...] + dxn
        vd_ref[0:1, :] += _rowsum(dg)

    return _run(
        body, comm, name="inproj_bwd", grid=(t // tm,),
        out_shape=[jax.ShapeDtypeStruct((t, d), F32), jax.ShapeDtypeStruct((t, d), BF16),
                   jax.ShapeDtypeStruct((t, n), BF16), jax.ShapeDtypeStruct((SUBLANES, d), F32),
                   jax.ShapeDtypeStruct((SUBLANES, n), F32)],
        in_specs=[_rows(tm, w) for w in widths] + [_rows(tm, d), _rows(tm, d), _whole((1, d)), _whole((n, d))],
        out_specs=[_rows(tm, d), _rows(tm, d), _rows(tm, n), _acc((SUBLANES, d)), _acc((SUBLANES, n))],
        semantics="arbitrary",
    )(*dparts, x, dx1, g_mix, w_in_t)


def _prep(lr_row, li_row, ldt_row, lr_col, li_col, ldt_col, b_re, b_im, c_re, c_im, w_r, w_i, groups, heads):
    gn, pch = b_re.shape
    sa, n = c_re.shape
    w, hd = w_r.shape

    def expand(vals, row_group, col_group, width):
        r, k = vals.shape
        tile = (lax.broadcasted_iota(jnp.int32, (k, width), 0) == lax.broadcasted_iota(jnp.int32, (k, width), 1) % k)
        rows = lax.broadcasted_iota(jnp.int32, (r, width), 0) // row_group
        cols = lax.broadcasted_iota(jnp.int32, (r, width), 1) // col_group
        return jnp.where(rows == cols, _dot(vals, tile.astype(BF16)), 0.0)

    def body(lrr, lir, ldr, lrc, lic, ldc, bre, bim, cre, cim, wr, wi,
             ar_o, ai_o, bbr_o, bbi_o, cre_o, cim_o, wr_o, wi_o):
        ar, ai, _, _ = _disc_scalars(lrr[...], lir[...], ldr[...])
        ar_o[...] = ar
        ai_o[...] = ai
        _, _, bbr, bbi = _disc_cols(lrc[...], lic[...], ldc[...], bre[...], bim[...])
        bbr_o[...] = expand(bbr, n, pch, sa).astype(bbr_o.dtype)
        bbi_o[...] = expand(bbi, n, pch, sa).astype(bbi_o.dtype)
        cre_o[...] = expand(cre[...], pch, n, gn).astype(cre_o.dtype)
        cim_o[...] = expand(-cim[...], pch, n, gn).astype(cim_o.dtype)
        wr_o[...] = expand(wr[...], hd, hd, w).astype(wr_o.dtype)
        wi_o[...] = expand(wi[...], hd, hd, w).astype(wi_o.dtype)

    return pl.pallas_call(
        body, name="prep",
        out_shape=[jax.ShapeDtypeStruct((1, gn), F32), jax.ShapeDtypeStruct((1, gn), F32),
                   jax.ShapeDtypeStruct((gn, sa), BF16), jax.ShapeDtypeStruct((gn, sa), BF16),
                   jax.ShapeDtypeStruct((sa, gn), BF16), jax.ShapeDtypeStruct((sa, gn), BF16),
                   jax.ShapeDtypeStruct((w, w), BF16), jax.ShapeDtypeStruct((w, w), BF16)],
        compiler_params=pltpu.CompilerParams(vmem_limit_bytes=VMEM_LIMIT),
    )(lr_row, li_row, ldt_row, lr_col, li_col, ldt_col, b_re, b_im, c_re, c_im, w_r, w_i)


def _s5_param_grads(lam_r, lam_i, sr, si, u, dy, pch, n, comm=None):
    t, gn = lam_r.shape
    sa = u.shape[1]
    tb = min(gn, 512)

    def body(lr_ref, li_ref, sr_ref, si_ref, u_ref, dy_ref, dbr_ref, dbi_ref, dcr_ref, dci_ref):
        _zero_on_first(dcr_ref, dci_ref)
        base = pl.program_id(0) * tb
        uv = u_ref[...]
        dyv = dy_ref[...]
        dbr_ref[...] = _fold_diag_blocks(_dot_tn(uv, lr_ref[...]), pch, n, col0=base)
        dbi_ref[...] = _fold_diag_blocks(_dot_tn(uv, li_ref[...]), pch, n, col0=base)
        dcr_ref[...] += _fold_diag_blocks(_dot_tn(sr_ref[...], dyv), n, pch, row0=base)
        dci_ref[...] += _fold_diag_blocks(_dot_tn(si_ref[...], dyv), n, pch, row0=base)

    cols = pl.BlockSpec((t, tb), lambda j: (0, j))
    return _run(
        body, comm, name="s5_param_grads", grid=(gn // tb,),
        out_shape=[jax.ShapeDtypeStruct((pch, gn), F32), jax.ShapeDtypeStruct((pch, gn), F32),
                   jax.ShapeDtypeStruct((n, sa), F32), jax.ShapeDtypeStruct((n, sa), F32)],
        in_specs=[cols, cols, cols, cols, _whole((t, sa)), _whole((t, sa))],
        out_specs=[pl.BlockSpec((pch, tb), lambda j: (0, j)), pl.BlockSpec((pch, tb), lambda j: (0, j)),
                   _acc((n, sa)), _acc((n, sa))],
        semantics="arbitrary",
    )(lam_r, lam_i, sr, si, u, dy)


SMALL_PARTS = ["vec_tail", "vec_ffn", "vec_lru", "vec_mix", "vec_bin", "vec_sa", "vec_gn", "dw_r", "dw_i", "dbb_re",
               "dbb_im", "dc_re", "dc_imn", "loss"]


def _small_reduce(parts, shapes, lr_col, li_col, ldt_col, b_re, b_im, groups, comm=None):
    gn, pch = b_re.shape
    n = gn // groups
    nparts = parts[SMALL_PARTS[0]].size // math.prod(shapes[SMALL_PARTS[0]])
    np_, nout = len(SMALL_PARTS), 24

    def body(*refs):
        ins = refs[:np_]
        lr, li, ldt, bre, bim = refs[np_:np_ + 5]
        outs = refs[np_ + 5:np_ + 5 + nout]
        sums = dict(zip(SMALL_PARTS, refs[np_ + 5 + nout:]))

        @pl.when(pl.program_id(0) == 0)
        def _():
            for k, r in zip(SMALL_PARTS, ins):
                sums[k][...] = r[...]

        @pl.when(pl.program_id(0) > 0)
        def _():
            for k, r in zip(SMALL_PARTS, ins):
                sums[k][...] += r[...]

        @pl.when(pl.program_id(0) == nparts - 1)
        def _():
            finish({k: s[...] for k, s in sums.items()}, lr, li, ldt, bre, bim, *outs)

    def finish(tot, lr, li, ldt, bre, bim, o_loss, o_gmix, o_bin, o_bglu, o_s5d, o_convb, o_br, o_bi, o_lam, o_gffn,
               o_gpg, o_bpg, o_gple, o_gfin, o_wr, o_wi, o_cre, o_cim, o_bre, o_bim, o_lre, o_lim, o_ldt, o_convw):
        o_loss[...] = tot["loss"]
        o_convw[...] = tot["vec_lru"][SUBLANES - CONV_WIDTH:SUBLANES]
        o_bpg[...] = tot["vec_tail"][0:1]
        o_gpg[...] = tot["vec_tail"][1:2]
        o_gple[...] = tot["vec_tail"][2:3]
        o_gfin[...] = tot["vec_tail"][3:4]
        o_gffn[...] = tot["vec_ffn"][0:1]
        o_convb[...] = tot["vec_lru"][0:1]
        o_br[...] = tot["vec_lru"][1:2]
        o_bi[...] = tot["vec_lru"][2:3]
        o_lam[...] = tot["vec_lru"][3:4]
        o_gmix[...] = tot["vec_mix"][0:1]
        o_bin[...] = tot["vec_bin"][0:1]
        o_bglu[...] = tot["vec_sa"][0:1]
        o_s5d[...] = tot["vec_sa"][1:2]
        o_wr[...] = tot["dw_r"].T
        o_wi[...] = tot["dw_i"].T
        o_cre[...] = tot["dc_re"].T
        o_cim[...] = -tot["dc_imn"].T
        d_a = tot["vec_gn"].T
        _, chain = jax.vjp(_disc_cols, lr[...], li[...], ldt[...], bre[...], bim[...])
        d_lr, d_li, d_ldt, d_bre, d_bim = chain((d_a[:, 0:1], d_a[:, 1:2], tot["dbb_re"].T, tot["dbb_im"].T))
        o_lre[...] = d_lr
        o_lim[...] = d_li
        o_bre[...] = d_bre
        o_bim[...] = d_bim
        same = (lax.broadcasted_iota(jnp.int32, (groups, gn), 0)
                == lax.broadcasted_iota(jnp.int32, (groups, gn), 1) // n).astype(F32)
        o_ldt[...] = jnp.dot(same, d_ldt * jnp.ones((1, LANES), F32), preferred_element_type=F32,
                             precision=lax.Precision.HIGHEST)[:, 0:1]

    d = shapes["vec_mix"][1]
    nz = shapes["vec_bin"][1]
    sa = shapes["vec_sa"][1]
    w = shapes["vec_lru"][1]
    row = lambda c: jax.ShapeDtypeStruct((1, c), F32)
    out_shape = [jax.ShapeDtypeStruct(shapes["loss"], F32), row(d), row(nz), row(sa), row(sa), row(w), row(w), row(w),
                 row(w), row(d), row(d), row(d), row(d), row(d),
                 jax.ShapeDtypeStruct(shapes["dw_r"][::-1], F32), jax.ShapeDtypeStruct(shapes["dw_i"][::-1], F32),
                 jax.ShapeDtypeStruct(shapes["dc_re"][::-1], F32), jax.ShapeDtypeStruct(shapes["dc_imn"][::-1], F32),
                 jax.ShapeDtypeStruct((gn, pch), F32), jax.ShapeDtypeStruct((gn, pch), F32),
                 jax.ShapeDtypeStruct((gn, 1), F32), jax.ShapeDtypeStruct((gn, 1), F32),
                 jax.ShapeDtypeStruct((groups, 1), F32), jax.ShapeDtypeStruct((CONV_WIDTH, w), F32)]
    def part_spec(k):
        r, c = shapes[k]
        if parts[k].ndim == 3:
            return pl.BlockSpec((None, r, c), lambda i: (i, 0, 0))
        return pl.BlockSpec((r, c), lambda i: (i, 0))

    outs = _run(
        body, comm, name="small_reduce", grid=(nparts,), out_shape=out_shape,
        in_specs=[part_spec(k) for k in SMALL_PARTS] + [_whole(a.shape) for a in (lr_col, li_col, ldt_col, b_re, b_im)],
        out_specs=[_acc(s.shape) for s in out_shape],
        scratch_shapes=[pltpu.VMEM(shapes[k], F32) for k in SMALL_PARTS],
        semantics="arbitrary",
    )(*[parts[k] for k in SMALL_PARTS], lr_col, li_col, ldt_col, b_re, b_im)
    extra = None
    if comm is not None:
        outs, extra = outs
    names = ["loss", "g_mix", "b_in", "b_glu", "s5_d", "conv_b", "b_r", "b_i", "lru_lambda", "g_ffn", "g_ple_gate",
             "b_ple_gate", "g_ple", "g_final", "w_r", "w_i", "s5_c_re", "s5_c_im", "s5_b_re", "s5_b_im", "lam_re",
             "lam_im", "log_dt", "conv_w"]
    res = dict(zip(names, outs))
    return res if comm is None else (res, extra)


def _adamw_small(ws, gs, ms, vs):
    n = len(ws)

    def body(*refs):
        w_r, g_r, m_r, v_r = (refs[i * n:(i + 1) * n] for i in range(4))
        d_o, m_o, v_o = (refs[(4 + i) * n:(5 + i) * n] for i in range(3))
        for i in range(n):
            delta, m_new, v_new = _adamw_math(w_r[i][...], g_r[i][...], m_r[i][...], v_r[i][...])
            d_o[i][...] = delta
            m_o[i][...] = m_new
            v_o[i][...] = v_new

    shapes = [jax.ShapeDtypeStruct(a.shape, F32) for a in ws]
    outs = pl.pallas_call(body, name="adamw_small", out_shape=shapes * 3)(*ws, *gs, *ms, *vs)
    return outs[:n], outs[n:2 * n], outs[2 * n:]


def _adamw_math(w, g, m, v):
    m_new = ADAM_B1 * m + (1.0 - ADAM_B1) * g
    v_new = ADAM_B2 * v + (1.0 - ADAM_B2) * (g * g)
    m_hat = m_new / (1.0 - ADAM_B1 ** ADAM_STEP)
    v_hat = v_new / (1.0 - ADAM_B2 ** ADAM_STEP)
    delta = -ADAM_LR * (m_hat / (jnp.sqrt(v_hat) + ADAM_EPS) + ADAM_WD * w)
    return delta, m_new, v_new


def _row_tile(rows):
    for cand in (256, 128, 64, 32, 16, 8):
        if rows % cand == 0:
            return cand
    return rows


def _adamw(parts, w, m, v, name, transposed=False):
    rows, cols = w.shape
    npart = parts.shape[0]
    tr = _row_tile(rows)
    if transposed:
        parts_spec = pl.BlockSpec((npart, cols, tr), lambda i: (0, 0, i))
    else:
        parts_spec = pl.BlockSpec((npart, tr, cols), lambda i: (0, i, 0))

    def body(p_ref, w_ref, m_ref, v_ref, g_ref, d_ref, mo_ref, vo_ref):
        g = p_ref[0].astype(F32)
        for k in range(1, npart):
            g = g + p_ref[k].astype(F32)
        if transposed:
            g = g.T
        delta, m_new, v_new = _adamw_math(w_ref[...], g, m_ref[...], v_ref[...])
        g_ref[...] = g
        d_ref[...] = delta
        mo_ref[...] = m_new
        vo_ref[...] = v_new

    return pl.pallas_call(
        body, name=name, grid=(rows // tr,),
        out_shape=[jax.ShapeDtypeStruct((rows, cols), F32)] * 4,
        in_specs=[parts_spec] + [_rows(tr, cols)] * 3,
        out_specs=[_rows(tr, cols)] * 4,
        compiler_params=_params("parallel"),
    )(parts, w, m, v)


def _mesh_position():
    return lax.axis_index("x"), lax.axis_index("y"), lax.axis_index("c")


def _flip(pos, rel):
    x, y, c = pos
    return (1 - x if rel & 4 else x, 1 - y if rel & 2 else y, 1 - c if rel & 1 else c)


def _index(pos):
    return 4 * pos[0] + 2 * pos[1] + pos[2]


_ANY = pl.BlockSpec(memory_space=pl.ANY)
FLAT_ROWS = 32


def _dma_sems(n):
    return [pltpu.SemaphoreType.DMA((n, N_DEV - 1)), pltpu.SemaphoreType.DMA((n, N_DEV - 1)), pltpu.SemaphoreType.DMA((n,))]


def _block_of(ref, idx, rows, flat):
    if flat:
        return ref.at[pl.ds(pl.multiple_of(idx * rows, FLAT_ROWS), rows), :]
    return ref.at[idx]


class _Gather:
    chips = (4, 2, 6)

    def __init__(self, shards):
        self.inputs = list(shards)
        self.flat = [s.shape[0] % FLAT_ROWS == 0 for s in shards]
        self.out_shape = [
            jax.ShapeDtypeStruct((N_DEV * s.shape[0], s.shape[1]) if f else (N_DEV,) + s.shape, s.dtype)
            for s, f in zip(shards, self.flat)]
        self.sems = _dma_sems(len(shards))

    def _copy(self, ins, outs, sems, i, k, block, to, own=False):
        dst = _block_of(outs[i], _index(block), self.inputs[i].shape[0], self.flat[i])
        return pltpu.make_async_remote_copy(
            src_ref=ins[i] if own else dst, dst_ref=dst, send_sem=sems[0].at[i, k], recv_sem=sems[1].at[i, k],
            device_id=to, device_id_type=MESH)

    def _local(self, ins, outs, sems, i, me):
        dst = _block_of(outs[i], _index(me), self.inputs[i].shape[0], self.flat[i])
        return pltpu.make_async_copy(ins[i], dst, sems[2].at[i])

    def _first(self, ins, outs, sems, i, me):
        cps = [self._copy(ins, outs, sems, i, 0, me, _flip(me, 1), own=True)]
        cps += [self._copy(ins, outs, sems, i, 1 + j, me, _flip(me, rel), own=True) for j, rel in enumerate(self.chips)]
        return cps

    def _passed(self, ins, outs, sems, i, j, me):
        return self._copy(ins, outs, sems, i, 4 + j, _flip(me, self.chips[j]), _flip(me, 1))

    def before(self, ins, outs, sems):
        n = len(self.inputs)
        me = _mesh_position()

        @pl.when(pl.program_id(0) == 0)
        def _():
            for i in range(n):
                self._local(ins, outs, sems, i, me).start()
                for cp in self._first(ins, outs, sems, i, me):
                    cp.start()

        @pl.when(pl.program_id(0) == pl.num_programs(0) - 1)
        def _():
            for j, rel in enumerate(self.chips):
                for i in range(n):
                    self._copy(ins, outs, sems, i, 1 + j, _flip(me, rel), me).wait_recv()
                    self._passed(ins, outs, sems, i, j, me).start()

    def after(self, ins, outs, sems):
        n = len(self.inputs)
        me = _mesh_position()
        sibling = _flip(me, 1)

        @pl.when(pl.program_id(0) == pl.num_programs(0) - 1)
        def _():
            for i in range(n):
                self._copy(ins, outs, sems, i, 0, sibling, me).wait_recv()
                for j, rel in enumerate(self.chips):
                    self._copy(ins, outs, sems, i, 4 + j, _flip(sibling, rel), me).wait_recv()
            for i in range(n):
                for cp in self._first(ins, outs, sems, i, me):
                    cp.wait_send()
                for j in range(len(self.chips)):
                    self._passed(ins, outs, sems, i, j, me).wait_send()
                self._local(ins, outs, sems, i, me).wait()


N_CHIPS = N_DEV // 2


def _chip(pos):
    return 2 * pos[0] + pos[1]


class _PairSwap:
    def __init__(self, arrays):
        self.inputs = list(arrays)
        self.rows = [a.shape[0] // N_DEV for a in arrays]
        for r in self.rows:
            assert r % FLAT_ROWS == 0, r
        self.out_shape = [jax.ShapeDtypeStruct((N_CHIPS, r, a.shape[1]), a.dtype) for a, r in zip(arrays, self.rows)]
        n = len(arrays)
        self.sems = [pltpu.SemaphoreType.DMA((n, N_CHIPS)), pltpu.SemaphoreType.DMA((n, N_CHIPS))]

    def _copy(self, ins, outs, sems, i, j, me):
        sibling = _flip(me, 1)
        return pltpu.make_async_remote_copy(
            src_ref=_block_of(ins[i], 2 * j + sibling[2], self.rows[i], True), dst_ref=outs[i].at[j],
            send_sem=sems[0].at[i, j], recv_sem=sems[1].at[i, j], device_id=sibling, device_id_type=MESH)

    def before(self, ins, outs, sems):
        me = _mesh_position()

        @pl.when(pl.program_id(0) == 0)
        def _():
            for i in range(len(self.inputs)):
                for j in range(N_CHIPS):
                    self._copy(ins, outs, sems, i, j, me).start()

    def after(self, ins, outs, sems):
        me = _mesh_position()

        @pl.when(pl.program_id(0) == pl.num_programs(0) - 1)
        def _():
            for i in range(len(self.inputs)):
                for j in range(N_CHIPS):
                    self._copy(ins, outs, sems, i, j, me).wait()


class _ChipExchange:
    chips = (4, 2, 6)

    def __init__(self, arrays):
        self.inputs = list(arrays)
        self.out_shape = [jax.ShapeDtypeStruct(a.shape, a.dtype) for a in arrays]
        n = len(arrays)
        self.sems = [pltpu.SemaphoreType.DMA((n, 3)), pltpu.SemaphoreType.DMA((n, 3)), pltpu.SemaphoreType.DMA((n,))]

    def _send(self, ins, outs, sems, i, k, me):
        peer = _flip(me, self.chips[k])
        return pltpu.make_async_remote_copy(
            src_ref=ins[i].at[_chip(peer)], dst_ref=outs[i].at[_chip(me)], send_sem=sems[0].at[i, k],
            recv_sem=sems[1].at[i, k], device_id=peer, device_id_type=MESH)

    def _arrival(self, ins, outs, sems, i, k, me):
        peer = _flip(me, self.chips[k])
        return pltpu.make_async_remote_copy(
            src_ref=ins[i].at[_chip(me)], dst_ref=outs[i].at[_chip(peer)], send_sem=sems[0].at[i, k],
            recv_sem=sems[1].at[i, k], device_id=peer, device_id_type=MESH)

    def _local(self, ins, outs, sems, i, me):
        return pltpu.make_async_copy(ins[i].at[_chip(me)], outs[i].at[_chip(me)], sems[2].at[i])

    def before(self, ins, outs, sems):
        me = _mesh_position()

        @pl.when(pl.program_id(0) == 0)
        def _():
            for i in range(len(self.inputs)):
                self._local(ins, outs, sems, i, me).start()
                for k in range(len(self.chips)):
                    self._send(ins, outs, sems, i, k, me).start()

    def after(self, ins, outs, sems):
        me = _mesh_position()

        @pl.when(pl.program_id(0) == pl.num_programs(0) - 1)
        def _():
            for i in range(len(self.inputs)):
                for k in range(len(self.chips)):
                    self._arrival(ins, outs, sems, i, k, me).wait_recv()
            for i in range(len(self.inputs)):
                for k in range(len(self.chips)):
                    self._send(ins, outs, sems, i, k, me).wait_send()
                self._local(ins, outs, sems, i, me).wait()


def _pair_add(grads, halves, name):
    n = len(grads)

    def body(*refs):
        g_refs, h_refs, o_refs = refs[:n], refs[n:2 * n], refs[2 * n:]
        c = lax.axis_index("c")
        for i in range(n):
            r = h_refs[i].shape[1]
            for j in range(N_CHIPS):
                own = g_refs[i][pl.ds(pl.multiple_of((2 * j + c) * r, FLAT_ROWS), r), :]
                o_refs[i][j] = (own.astype(F32) + h_refs[i][j].astype(F32)).astype(o_refs[i].dtype)

    return pl.pallas_call(
        body, name=name, out_shape=[jax.ShapeDtypeStruct(h.shape, h.dtype) for h in halves],
        compiler_params=pltpu.CompilerParams(vmem_limit_bytes=VMEM_LIMIT),
    )(*grads, *halves)


class _Both:
    def __init__(self, first, second):
        self.jobs = (first, second)
        self.inputs = first.inputs + second.inputs
        self.out_shape = first.out_shape + second.out_shape
        self.sems = first.sems + second.sems

    def _each(self, ins, outs, sems):
        a = self.jobs[0]
        i, o, s = len(a.inputs), len(a.out_shape), len(a.sems)
        return ((a, ins[:i], outs[:o], sems[:s]), (self.jobs[1], ins[i:], outs[o:], sems[s:]))

    def before(self, ins, outs, sems):
        for job, i, o, s in self._each(ins, outs, sems):
            job.before(i, o, s)

    def after(self, ins, outs, sems):
        for job, i, o, s in self._each(ins, outs, sems):
            job.after(i, o, s)


def _run(body, comm, *, semantics, out_shape, in_specs, out_specs, scratch_shapes=(), **kw):
    if comm is None:
        return pl.pallas_call(body, out_shape=out_shape, in_specs=in_specs, out_specs=out_specs,
                              scratch_shapes=list(scratch_shapes), compiler_params=_params(semantics), **kw)
    single = not isinstance(out_shape, (list, tuple))
    outs = [out_shape] if single else list(out_shape)
    ospecs = [out_specs] if single else list(out_specs)
    counts = [len(in_specs), len(comm.inputs), len(outs), len(comm.out_shape), len(scratch_shapes), len(comm.sems)]

    def carrying(*refs):
        groups, pos = [], 0
        for c in counts:
            groups.append(refs[pos:pos + c])
            pos += c
        main_in, comm_in, main_out, comm_out, main_scratch, comm_sems = groups
        comm.before(comm_in, comm_out, comm_sems)
        body(*main_in, *main_out, *main_scratch)
        comm.after(comm_in, comm_out, comm_sems)

    call = pl.pallas_call(
        carrying, out_shape=outs + list(comm.out_shape), in_specs=list(in_specs) + [_ANY] * len(comm.inputs),
        out_specs=ospecs + [_ANY] * len(comm.out_shape), scratch_shapes=list(scratch_shapes) + list(comm.sems),
        compiler_params=_params("arbitrary"), **kw)

    def apply(*args):
        res = call(*args, *comm.inputs)
        main = res[:len(outs)]
        return (main[0] if single else list(main)), list(res[len(outs):])

    return apply


def _alone(comm, name):
    return _run(lambda: None, comm, semantics="arbitrary", name=name, grid=(1,), out_shape=[], in_specs=[], out_specs=[])()[1]


SHARDED = {"w_in": 1, "w_glu": 0, "conv_w": 1, "w_a_out": 1, "w_b_out": 0, "w_o": 0, "w_ffn_gate": 1, "w_ffn_up": 1,
           "w_ffn_down": 0, "w_ple_gate": 0, "w_ple": 1}
TRANSPOSED = ("w_in", "w_a_out", "w_ffn_gate", "w_ffn_up", "w_ple")
SMALL = ["g_mix", "b_in", "lam_re", "lam_im", "log_dt", "s5_b_re", "s5_b_im", "s5_c_re", "s5_c_im", "s5_d", "b_glu",
         "conv_b", "w_r", "b_r", "w_i", "b_i", "lru_lambda", "g_ffn", "g_ple_gate", "b_ple_gate", "g_ple", "g_final"]
WEIGHTS = ["g_mix", "w_in", "b_in", "lam_re", "lam_im", "log_dt", "s5_b_re", "s5_b_im", "s5_c_re", "s5_c_im", "s5_d",
           "w_glu", "b_glu", "conv_w", "conv_b", "w_r", "b_r", "w_i", "b_i", "lru_lambda", "w_a_out", "w_b_out", "w_o",
           "g_ffn", "w_ffn_gate", "w_ffn_up", "w_ffn_down", "g_ple_gate", "w_ple_gate", "b_ple_gate", "w_ple", "g_ple",
           "g_final"]


def _unblock(gathered, axis):
    nb, r, c = gathered.shape
    if axis == 0:
        return gathered.reshape(nb * r, c)
    return jnp.transpose(gathered, (1, 0, 2)).reshape(r, nb * c)


def _disc_scalars(lr, li, ldt):
    dt = jnp.exp(ldt)
    mag = jnp.exp(lr * dt)
    ar = mag * jnp.cos(li * dt)
    ai = mag * jnp.sin(li * dt)
    den = lr * lr + li * li
    nr = ar - 1.0
    fr = (nr * lr + ai * li) / den
    fi = (ai * lr - nr * li) / den
    return ar, ai, fr, fi


def _disc_cols(lr, li, ldt, b_re, b_im):
    ar, ai, fr, fi = _disc_scalars(lr, li, ldt)
    return ar, ai, fr * b_re - fi * b_im, fr * b_im + fi * b_re


def _local_step(x, p, target, src, small, disc, distributed=True):
    full = {} if distributed else dict(src)
    gw, halves, pairs, got = {}, {}, {}, {}

    def gather(keys):
        return (_Gather([src[k] for k in keys]), keys, full) if distributed else None

    def swap(keys):
        return (_PairSwap([gw[k] for k in keys]), keys, halves) if distributed else None

    def chips(keys):
        return (_ChipExchange([pairs[k] for k in keys]), keys, got) if distributed else None

    def add_pairs(keys):
        if distributed:
            pairs.update(zip(keys, _pair_add([gw[k] for k in keys], [halves[k] for k in keys], "pair_add_" + keys[0])))

    def carry(fn, *args, jobs=()):
        jobs = [j for j in jobs if j is not None]
        if not jobs:
            return fn(*args)
        comm = jobs[0][0]
        for j in jobs[1:]:
            comm = _Both(comm, j[0])
        res, extra = fn(*args, comm=comm)
        for job, keys, sink in jobs:
            sink.update(zip(keys, extra[:len(job.out_shape)]))
            extra = extra[len(job.out_shape):]
        return res

    if distributed:
        first = ["w_in", "w_glu", "conv_w"]
        for k, a in zip(first, _alone(gather(first)[0], "gather_first")):
            full[k] = _unblock(a, 1) if k == "conv_w" else a

    d = x.shape[1]
    g, n, pch = small["s5_b_re"].shape
    heads = small["w_r"].shape[0]
    sa, lw = g * pch, small["lru_lambda"].shape[-1]
    widths = [sa, lw, d, d]
    row = lambda v: v.reshape(1, -1)

    hd = small["w_r"].shape[-1]
    ar_row, ai_row, bbr_blk, bbi_blk, cre_blk, cimn_blk, wr_blk, wi_blk = _prep(
        *disc["rows"], *disc["cols"], disc["b_re"], disc["b_im"], small["s5_c_re"].reshape(sa, n),
        small["s5_c_im"].reshape(sa, n), small["w_r"].reshape(lw, hd), small["w_i"].reshape(lw, hd), g, heads)
    d_row = row(small["s5_d"])
    conv_w = full["conv_w"].astype(F32)

    u_a, u_b, za, zb = carry(_inproj_fwd, x, row(small["g_mix"]), full["w_in"], row(small["b_in"]), widths,
                             jobs=[gather(["w_a_out", "w_b_out", "w_o"])])
    sr, si, y, y_a = carry(_s5_fwd, u_a, bbr_blk, bbi_blk, ar_row, ai_row, cre_blk, cimn_blk, d_row, full["w_glu"],
                           row(small["b_glu"]), jobs=[gather(["w_ffn_gate"])])
    xc, h, hprev = carry(_lru_fwd, u_b, conv_w, row(small["conv_b"]), wr_blk, row(small["b_r"]), wi_blk,
                         row(small["b_i"]), row(small["lru_lambda"]), jobs=[gather(["w_ffn_up"])])
    x1, merged, ma, mb = carry(_merge_fwd, y_a, h, za, zb, x, full["w_a_out"], full["w_b_out"], full["w_o"],
                               jobs=[gather(["w_ffn_down"])])
    fg, fu = carry(_ffn_up_fwd, x1, row(small["g_ffn"]), full["w_ffn_gate"], full["w_ffn_up"],
                   jobs=[gather(["w_ple_gate", "w_ple"])])
    x2 = _ffn_down_fwd(fg, fu, x1, full["w_ffn_down"])

    dx2, loss_blk, gw["w_ple_gate"], gw["w_ple"], vec_tail = _tail_fwd_bwd(
        x2, p, target, row(small["g_ple_gate"]), full["w_ple_gate"], row(small["b_ple_gate"]), full["w_ple"],
        row(small["g_ple"]), row(small["g_final"]))
    dfg, dfu, act = carry(_ffn_bwd_a, dx2, fg, fu, full["w_ffn_down"], jobs=[swap(["w_ple_gate", "w_ple"])])
    tn_d = min(d, 512)
    gw["w_ffn_down"] = _matmul_tn(act, dx2, tn_d, "dw_ffn_down", BF16)
    add_pairs(["w_ple_gate", "w_ple"])
    dx1, h2, vec_ffn = carry(_ffn_bwd_b, dfg, dfu, x1, dx2, row(small["g_ffn"]), full["w_ffn_gate"], full["w_ffn_up"],
                             jobs=[chips(["w_ple_gate", "w_ple"]), swap(["w_ffn_down"])])
    gw["w_ffn_gate"] = _matmul_tn(dfg, h2, tn_d, "dw_ffn_gate", BF16)
    gw["w_ffn_up"] = _matmul_tn(dfu, h2, tn_d, "dw_ffn_up", BF16)
    add_pairs(["w_ffn_down"])
    dza, dzb, dya, dyb, gw["w_o"], gw["w_a_out"], gw["w_b_out"] = carry(
        _merge_bwd, dx1, merged, ma, mb, za, zb, y_a, h, full["w_o"], full["w_a_out"], full["w_b_out"],
        jobs=[chips(["w_ffn_down"]), swap(["w_ffn_gate", "w_ffn_up"])])
    add_pairs(["w_ffn_gate", "w_ffn_up"])
    du_b, dw_r, dw_i, vec_lru = carry(
        _lru_bwd, dyb, xc, hprev, u_b, conv_w, wr_blk, row(small["b_r"]), wi_blk, row(small["b_i"]),
        row(small["lru_lambda"]), hd, jobs=[chips(["w_ffn_gate", "w_ffn_up"]), swap(["w_o", "w_a_out", "w_b_out"])])
    add_pairs(["w_o", "w_a_out", "w_b_out"])
    du_a, lam_r, lam_i, dy16, gw["w_glu"], vec_sa, vec_gn = carry(
        _s5_bwd, dya, y, sr, si, u_a, full["w_glu"], row(small["b_glu"]), cre_blk, cimn_blk, bbr_blk, bbi_blk, ar_row,
        ai_row, d_row, jobs=[chips(["w_o", "w_a_out", "w_b_out"])])
    dbb_re, dbb_im, dc_re, dc_imn = carry(_s5_param_grads, lam_r, lam_i, sr, si, u_a, dy16, pch, n,
                                          jobs=[swap(["w_glu"])])
    add_pairs(["w_glu"])
    smalls = {"vec_tail": vec_tail, "vec_ffn": vec_ffn, "vec_lru": vec_lru, "vec_sa": vec_sa, "vec_gn": vec_gn,
              "dw_r": dw_r, "dw_i": dw_i, "dbb_re": dbb_re, "dbb_im": dbb_im, "dc_re": dc_re, "dc_imn": dc_imn,
              "loss": loss_blk}
    shapes = {k: a.shape for k, a in smalls.items()}
    early = list(smalls)
    everyones = {}
    grad_x, h0, dz16, smalls["vec_mix"], smalls["vec_bin"] = carry(
        _inproj_bwd, [du_a, du_b, dza, dzb], x, dx1, row(small["g_mix"]), full["w_in"],
        jobs=[chips(["w_glu"]), (_Gather([smalls[k] for k in early]), early, everyones) if distributed else None])
    shapes.update(vec_mix=smalls["vec_mix"].shape, vec_bin=smalls["vec_bin"].shape)
    smalls.update(everyones)
    gw["w_in"] = _matmul_tn(dz16, h0, tn_d, "dw_in", BF16)
    if distributed:
        got["w_in"] = gw["w_in"]
        gw = got
    return grad_x, gw, smalls, shapes


def _disc_inputs(small):
    g, n, pch = small["s5_b_re"].shape
    srcs = (small["lam_re"], small["lam_im"], jnp.repeat(small["log_dt"], n))
    return {"rows": [a.reshape(1, g * n) for a in srcs], "cols": [a.reshape(g * n, 1) for a in srcs],
            "b_re": small["s5_b_re"].reshape(g * n, pch), "b_im": small["s5_b_im"].reshape(g * n, pch)}


def kernel(x, p, g_mix, w_in, b_in, lam_re, lam_im, log_dt, s5_b_re, s5_b_im, s5_c_re, s5_c_im, s5_d, w_glu, b_glu, conv_w, conv_b, w_r, b_r, w_i, b_i, lru_lambda, w_a_out, w_b_out, w_o, g_ffn, w_ffn_gate, w_ffn_up, w_ffn_down, g_ple_gate, w_ple_gate, b_ple_gate, w_ple, g_ple, g_final, loss_target, m_g_mix, m_w_in, m_b_in, m_lam_re, m_lam_im, m_log_dt, m_s5_b_re, m_s5_b_im, m_s5_c_re, m_s5_c_im, m_s5_d, m_w_glu, m_b_glu, m_conv_w, m_conv_b, m_w_r, m_b_r, m_w_i, m_b_i, m_lru_lambda, m_w_a_out, m_w_b_out, m_w_o, m_g_ffn, m_w_ffn_gate, m_w_ffn_up, m_w_ffn_down, m_g_ple_gate, m_w_ple_gate, m_b_ple_gate, m_w_ple, m_g_ple, m_g_final, v_g_mix, v_w_in, v_b_in, v_lam_re, v_lam_im, v_log_dt, v_s5_b_re, v_s5_b_im, v_s5_c_re, v_s5_c_im, v_s5_d, v_w_glu, v_b_glu, v_conv_w, v_conv_b, v_w_r, v_b_r, v_w_i, v_b_i, v_lru_lambda, v_w_a_out, v_w_b_out, v_w_o, v_g_ffn, v_w_ffn_gate, v_w_ffn_up, v_w_ffn_down, v_g_ple_gate, v_w_ple_gate, v_b_ple_gate, v_w_ple, v_g_ple, v_g_final):
    given = dict(locals())
    wts = {k: given[k] for k in WEIGHTS}
    moms = {k: given["m_" + k] for k in WEIGHTS}
    vels = {k: given["v_" + k] for k in WEIGHTS}

    def drop_depth(k, a):
        return a if k == "g_final" else a[0]

    small = {k: drop_depth(k, wts[k]) for k in SMALL}
    shard = {k: wts[k][0] for k in SHARDED}
    names = list(SHARDED)

    def wire(k):
        if k == "conv_w":
            return shard[k]
        return (shard[k].T if k in TRANSPOSED else shard[k]).astype(BF16)

    disc = _disc_inputs(small)
    grad_x, parts, smalls, shapes = _local_step(x[0], p[0, 0], loss_target[0], {k: wire(k) for k in names}, small, disc)

    late = ["vec_mix", "vec_bin"]
    received = _alone(_Both(_PairSwap([parts["w_in"]]), _Gather([smalls[k] for k in late])), "swap_last")
    (pair_in,) = _pair_add([parts["w_in"]], received[:1], "pair_add_w_in")
    smalls.update(zip(late, received[1:]))

    g_small, (parts["w_in"],) = _small_reduce(smalls, shapes, *disc["cols"], disc["b_re"], disc["b_im"],
                                              small["s5_b_re"].shape[0], comm=_ChipExchange([pair_in]))
    loss = g_small.pop("loss")[0, 0]
    cols = shard["conv_w"].shape[1]
    mine = _index((lax.axis_index("x"), lax.axis_index("y"), lax.axis_index("c")))
    parts["conv_w"] = lax.dynamic_slice_in_dim(g_small.pop("conv_w"), mine * cols, cols, axis=1)[None]
    natural = lambda k, a: a.reshape((1, -1) if k == "g_final" else wts[k].shape)
    small_g = [natural(k, g_small[k]) for k in SMALL]
    deltas, new_m, new_v = _adamw_small([natural(k, wts[k]) for k in SMALL], small_g,
                                        [natural(k, moms[k]) for k in SMALL], [natural(k, vels[k]) for k in SMALL])
    small_out = [dict(zip(SMALL, [a.reshape(wts[k].shape) for k, a in zip(SMALL, slot)]))
                 for slot in (small_g, deltas, new_m, new_v)]

    big_out = {}
    for k in names:
        big_out[k] = _adamw(parts[k], shard[k], moms[k][0], vels[k][0], "adamw_" + k, transposed=k in TRANSPOSED)

    outs = [loss, grad_x[None]]
    for slot in range(4):
        for k in WEIGHTS:
            if k in SHARDED:
                outs.append(big_out[k][slot][None])
            else:
                outs.append(small_out[slot][k])
    return tuple(outs)
```

```python
import math

import jax
import jax.numpy as jnp
from jax import lax
from jax.experimental import pallas as pl
from jax.experimental.pallas import tpu as pltpu

F32 = jnp.float32
BF16 = jnp.bfloat16

EPS = 1e-6
LRU_C = 8.0
CONV_WIDTH = 4
ADAM_LR = 0.001
ADAM_B1 = 0.9
ADAM_B2 = 0.999
ADAM_EPS = 1e-08
ADAM_WD = 0.01
ADAM_STEP = 10

N_DEV = 8
MESH = pl.DeviceIdType.MESH
SUBLANES = 8
LANES = 128
VMEM_LIMIT = 56 * 1024 * 1024
TOKEN_TILE = 256
TIME_CHUNK = 256


def _dot(a, b):
    return jnp.dot(a.astype(BF16), b.astype(BF16), preferred_element_type=F32)


def _dot_nt(a, b):
    return lax.dot_general(a.astype(BF16), b.astype(BF16), (((1,), (1,)), ((), ())), preferred_element_type=F32)


def _dot_tn(a, b):
    return lax.dot_general(a.astype(BF16), b.astype(BF16), (((0,), (0,)), ((), ())), preferred_element_type=F32)


def _sigmoid(x):
    return jax.nn.sigmoid(x)


def _rms_stats(x):
    r = lax.rsqrt(jnp.mean(x * x, axis=-1, keepdims=True) + EPS)
    return x * r, r


def _rms_bwd(dy, xhat, r, g):
    dxn = dy * g
    dx = r * (dxn - xhat * jnp.mean(dxn * xhat, axis=-1, keepdims=True))
    return dx, dy * xhat


def _rowsum(v):
    return jnp.sum(v, axis=0, keepdims=True)


def _expm1(x):
    u = jnp.exp(x)
    um1 = u - 1.0
    safe = jnp.where(um1 == 0.0, 1.0, jnp.log(u))
    return jnp.where(um1 == 0.0, x, um1 * x / safe)


def _softplus(x):
    e = jnp.exp(-jnp.abs(x))
    u = 1.0 + e
    um1 = u - 1.0
    safe = jnp.where(um1 == 0.0, 1.0, um1)
    log1p_e = jnp.where(um1 == 0.0, e, jnp.log(u) * e / safe)
    return jnp.maximum(x, 0.0) + log1p_e


_GELU_K = math.sqrt(2.0 / math.pi)
_GELU_C = 0.044715


def _gelu(x):
    return 0.5 * x * (1.0 + jnp.tanh(_GELU_K * (x + _GELU_C * x * x * x)))


def _gelu_grad(x):
    th = jnp.tanh(_GELU_K * (x + _GELU_C * x * x * x))
    return 0.5 * (1.0 + th) + 0.5 * x * (1.0 - th * th) * _GELU_K * (1.0 + 3.0 * _GELU_C * x * x)


def _params(*sem):
    return pltpu.CompilerParams(dimension_semantics=sem, vmem_limit_bytes=VMEM_LIMIT)


def _rows(tm, n):
    return pl.BlockSpec((tm, n), lambda i: (i, 0))


def _rows_rev(tm, n, steps):
    return pl.BlockSpec((tm, n), lambda i: (steps - 1 - i, 0))


def _whole(shape):
    nd = len(shape)
    return pl.BlockSpec(shape, lambda i: (0,) * nd, pipeline_mode=pl.Buffered(1))


def _acc(shape):
    nd = len(shape)
    return pl.BlockSpec(shape, lambda i: (0,) * nd)


def _zero_on_first(*refs):
    @pl.when(pl.program_id(0) == 0)
    def _():
        for r in refs:
            r[...] = jnp.zeros_like(r)


def _inproj_fwd(x, g_mix, w_in_t, b_in, widths, comm=None):
    t, d = x.shape
    n = w_in_t.shape[0]
    tm = min(TOKEN_TILE, t)
    offs = [sum(widths[:i]) for i in range(len(widths) + 1)]

    def body(x_ref, g_ref, w_ref, b_ref, *outs):
        xhat, _ = _rms_stats(x_ref[...])
        h = (xhat * g_ref[...]).astype(BF16)
        for k, o_ref in enumerate(outs):
            lo, hi = offs[k], offs[k + 1]
            o_ref[...] = _dot_nt(h, w_ref[lo:hi, :]) + b_ref[:, lo:hi]

    return _run(
        body, comm, name="inproj_fwd", grid=(t // tm,),
        out_shape=[jax.ShapeDtypeStruct((t, w), F32) for w in widths],
        in_specs=[_rows(tm, d), _whole((1, d)), _whole((n, d)), _whole((1, n))],
        out_specs=[_rows(tm, w) for w in widths],
        semantics="parallel",
    )(x, g_mix, w_in_t, b_in)


def _s5_fwd(u, bbr_blk, bbi_blk, ar, ai, cre_blk, cimn_blk, d_skip, w_glu, b_glu, comm=None):
    t, sa = u.shape
    gn = bbr_blk.shape[0]
    tc = min(TIME_CHUNK, t)

    def body(u_ref, bbr_ref, bbi_ref, ar_ref, ai_ref, cre_ref, cim_ref, d_ref, wg_ref, bg_ref,
             sr_ref, si_ref, y_ref, ya_ref, cr_s, ci_s):
        _zero_on_first(cr_s, ci_s)
        uv = u_ref[...]
        ub = uv.astype(BF16)
        sr_ref[...] = _dot_nt(ub, bbr_ref[...])
        si_ref[...] = _dot_nt(ub, bbi_ref[...])
        a_r = ar_ref[...]
        a_i = ai_ref[...]

        def step(row, carry):
            c_r, c_i = carry
            at = pl.ds(row, 1)
            n_r = a_r * c_r - a_i * c_i + sr_ref[at, :]
            n_i = a_r * c_i + a_i * c_r + si_ref[at, :]
            sr_ref[at, :] = n_r
            si_ref[at, :] = n_i
            return n_r, n_i

        c_r, c_i = lax.fori_loop(0, tc, step, (cr_s[0:1, :], ci_s[0:1, :]), unroll=8)
        cr_s[0:1, :] = c_r
        ci_s[0:1, :] = c_i
        y = _dot_nt(sr_ref[...], cre_ref[...]) + _dot_nt(si_ref[...], cim_ref[...]) + d_ref[...] * uv
        y_ref[...] = y
        zz = _gelu(y)
        q = _dot(zz, wg_ref[...]) + bg_ref[...]
        ya_ref[...] = zz * _sigmoid(q)

    return _run(
        body, comm, name="s5_fwd", grid=(t // tc,),
        out_shape=[jax.ShapeDtypeStruct((t, gn), F32), jax.ShapeDtypeStruct((t, gn), F32),
                   jax.ShapeDtypeStruct((t, sa), F32), jax.ShapeDtypeStruct((t, sa), F32)],
        in_specs=[_rows(tc, sa), _whole((gn, sa)), _whole((gn, sa)), _whole((1, gn)), _whole((1, gn)),
                  _whole((sa, gn)), _whole((sa, gn)), _whole((1, sa)), _whole((sa, sa)), _whole((1, sa))],
        out_specs=[_rows(tc, gn), _rows(tc, gn), _rows(tc, sa), _rows(tc, sa)],
        scratch_shapes=[pltpu.VMEM((SUBLANES, gn), F32), pltpu.VMEM((SUBLANES, gn), F32)],
        semantics="arbitrary",
    )(u, bbr_blk, bbi_blk, ar, ai, cre_blk, cimn_blk, d_skip, w_glu, b_glu)


def _lru_gates(xc, wr_ref, br_ref, wi_ref, bi_ref, lam_ref):
    r = _sigmoid(_dot(xc, wr_ref[...]) + br_ref[...])
    ig = _sigmoid(_dot(xc, wi_ref[...]) + bi_ref[...])
    sp = _softplus(-lam_ref[...])
    log_a = (-LRU_C * r) * sp
    return r, ig, sp, log_a


def _lru_fwd(u, conv_w, conv_b, wr_blk, b_r, wi_blk, b_i, lru_lambda, comm=None):
    t, w = u.shape
    tc = min(TIME_CHUNK, t)
    halo = SUBLANES

    def body(u_ref, cw_ref, cb_ref, wr_ref, br_ref, wi_ref, bi_ref, lam_ref,
             xc_ref, h_ref, hp_ref, ext_s, a_s, carry_s):
        @pl.when(pl.program_id(0) == 0)
        def _():
            ext_s[0:halo, :] = jnp.zeros((halo, w), F32)
            carry_s[...] = jnp.zeros_like(carry_s)

        ext_s[halo:halo + tc, :] = u_ref[...]
        xc = cb_ref[...]
        for k in range(CONV_WIDTH):
            off = halo - (CONV_WIDTH - 1) + k
            xc = xc + cw_ref[k:k + 1, :] * ext_s[off:off + tc, :]
        ext_s[0:halo, :] = ext_s[tc:tc + halo, :]
        xc_ref[...] = xc
        r, ig, sp, log_a = _lru_gates(xc, wr_ref, br_ref, wi_ref, bi_ref, lam_ref)
        a_s[...] = jnp.exp(log_a)
        h_ref[...] = jnp.sqrt(-_expm1(2.0 * log_a)) * ig * xc

        def step(row, carry):
            at = pl.ds(row, 1)
            hp_ref[at, :] = carry
            nxt = a_s[at, :] * carry + h_ref[at, :]
            h_ref[at, :] = nxt
            return nxt

        carry_s[0:1, :] = lax.fori_loop(0, tc, step, carry_s[0:1, :], unroll=8)

    return _run(
        body, comm, name="lru_fwd", grid=(t // tc,),
        out_shape=[jax.ShapeDtypeStruct((t, w), F32)] * 3,
        in_specs=[_rows(tc, w), _whole((CONV_WIDTH, w)), _whole((1, w)), _whole((w, w)), _whole((1, w)),
                  _whole((w, w)), _whole((1, w)), _whole((1, w))],
        out_specs=[_rows(tc, w)] * 3,
        scratch_shapes=[pltpu.VMEM((halo + tc, w), F32), pltpu.VMEM((tc, w), F32), pltpu.VMEM((SUBLANES, w), F32)],
        semantics="arbitrary",
    )(u, conv_w, conv_b, wr_blk, b_r, wi_blk, b_i, lru_lambda)


def _merge_fwd(y_a, h, za, zb, x, w_a_out_t, w_b_out, w_o, comm=None):
    t, d = x.shape
    sa, lw = y_a.shape[1], h.shape[1]
    tm = min(TOKEN_TILE, t)

    def body(ya_ref, h_ref, za_ref, zb_ref, x_ref, wa_ref, wb_ref, wo_ref, x1_ref, mg_ref, ma_ref, mb_ref):
        ma = _dot_nt(ya_ref[...], wa_ref[...])
        mb = _dot(h_ref[...], wb_ref[...])
        merged = _sigmoid(za_ref[...]) * ma + _sigmoid(zb_ref[...]) * mb
        ma_ref[...] = ma
        mb_ref[...] = mb
        mg_ref[...] = merged.astype(mg_ref.dtype)
        x1_ref[...] = x_ref[...] + _dot(merged, wo_ref[...])

    return _run(
        body, comm, name="merge_fwd", grid=(t // tm,),
        out_shape=[jax.ShapeDtypeStruct((t, d), F32), jax.ShapeDtypeStruct((t, d), BF16),
                   jax.ShapeDtypeStruct((t, d), F32), jax.ShapeDtypeStruct((t, d), F32)],
        in_specs=[_rows(tm, sa), _rows(tm, lw), _rows(tm, d), _rows(tm, d), _rows(tm, d),
                  _whole((d, sa)), _whole((lw, d)), _whole((d, d))],
        out_specs=[_rows(tm, d)] * 4,
        semantics="parallel",
    )(y_a, h, za, zb, x, w_a_out_t, w_b_out, w_o)


def _ffn_up_fwd(x1, g_ffn, w_gate_t, w_up_t, comm=None):
    t, d = x1.shape
    f = w_gate_t.shape[0]
    tm = min(TOKEN_TILE, t)

    def body(x_ref, g_ref, wg_ref, wu_ref, fg_ref, fu_ref):
        xhat, _ = _rms_stats(x_ref[...])
        h2 = (xhat * g_ref[...]).astype(BF16)
        fg_ref[...] = _dot_nt(h2, wg_ref[...])
        fu_ref[...] = _dot_nt(h2, wu_ref[...])

    return _run(
        body, comm, name="ffn_up_fwd", grid=(t // tm,),
        out_shape=[jax.ShapeDtypeStruct((t, f), F32)] * 2,
        in_specs=[_rows(tm, d), _whole((1, d)), _whole((f, d)), _whole((f, d))],
        out_specs=[_rows(tm, f)] * 2,
        semantics="parallel",
    )(x1, g_ffn, w_gate_t, w_up_t)


def _ffn_down_fwd(fg, fu, x1, w_down, comm=None):
    t, d = x1.shape
    f = fg.shape[1]
    tm = min(TOKEN_TILE, t)

    def body(fg_ref, fu_ref, x_ref, wd_ref, x2_ref):
        fgv = fg_ref[...]
        act = fgv * _sigmoid(fgv) * fu_ref[...]
        x2_ref[...] = x_ref[...] + _dot(act, wd_ref[...])

    return _run(
        body, comm, name="ffn_down_fwd", grid=(t // tm,),
        out_shape=jax.ShapeDtypeStruct((t, d), F32),
        in_specs=[_rows(tm, f), _rows(tm, f), _rows(tm, d), _whole((f, d))],
        out_specs=_rows(tm, d),
        semantics="parallel",
    )(fg, fu, x1, w_down)


def _store_on_last(pairs):
    @pl.when(pl.program_id(0) == pl.num_programs(0) - 1)
    def _():
        for acc, out in pairs:
            out[...] = acc[...].astype(out.dtype)


def _tail_fwd_bwd(x2, p, target, g_pg, w_pg, b_pg, w_ple_t, g_ple, g_final):
    t, d = x2.shape
    pd = p.shape[1]
    tm = min(TOKEN_TILE, t)

    def body(x2_ref, p_ref, tg_ref, gpg_ref, wpg_ref, bpg_ref, wple_ref, gple_ref, gfin_ref,
             dx2_ref, loss_ref, dwpg_out, dwple_out, vec_ref, dwpg_ref, dwple_ref):
        _zero_on_first(loss_ref, dwpg_ref, dwple_ref, vec_ref)
        x2v = x2_ref[...]
        xh2, r2 = _rms_stats(x2v)
        h3 = xh2 * gpg_ref[...]
        gp = _sigmoid(_dot(h3, wpg_ref[...]) + bpg_ref[...])
        pe = _dot_nt(p_ref[...], wple_ref[...])
        peh, r3 = _rms_stats(pe)
        e = peh * gple_ref[...]
        x3 = x2v + gp * e
        xh3, r4 = _rms_stats(x3)
        diff = xh3 * gfin_ref[...] - tg_ref[...]
        loss_ref[...] += 0.5 * jnp.sum(jnp.mean(diff * diff, axis=-1, keepdims=True))
        dy = diff * (1.0 / d)
        dx3, dgfin = _rms_bwd(dy, xh3, r4, gfin_ref[...])
        d_gp = dx3 * e
        d_e = dx3 * gp
        dpe, dgple = _rms_bwd(d_e, peh, r3, gple_ref[...])
        dwple_ref[...] += _dot_tn(dpe, p_ref[...])
        dpre = d_gp * gp * (1.0 - gp)
        dwpg_ref[...] += _dot_tn(h3, dpre)
        dh3 = _dot_nt(dpre, wpg_ref[...])
        dx2n, dgpg = _rms_bwd(dh3, xh2, r2, gpg_ref[...])
        dx2_ref[...] = dx3 + dx2n
        vec_ref[0:1, :] += _rowsum(dpre)
        vec_ref[1:2, :] += _rowsum(dgpg)
        vec_ref[2:3, :] += _rowsum(dgple)
        vec_ref[3:4, :] += _rowsum(dgfin)
        _store_on_last([(dwpg_ref, dwpg_out), (dwple_ref, dwple_out)])

    return pl.pallas_call(
        body, name="tail_fwd_bwd", grid=(t // tm,),
        out_shape=[jax.ShapeDtypeStruct((t, d), F32), jax.ShapeDtypeStruct((SUBLANES, LANES), F32),
                   jax.ShapeDtypeStruct((d, d), BF16), jax.ShapeDtypeStruct((d, pd), BF16),
                   jax.ShapeDtypeStruct((SUBLANES, d), F32)],
        in_specs=[_rows(tm, d), _rows(tm, pd), _rows(tm, d), _whole((1, d)), _whole((d, d)), _whole((1, d)),
                  _whole((d, pd)), _whole((1, d)), _whole((1, d))],
        out_specs=[_rows(tm, d), _acc((SUBLANES, LANES)), _acc((d, d)), _acc((d, pd)), _acc((SUBLANES, d))],
        scratch_shapes=[pltpu.VMEM((d, d), F32), pltpu.VMEM((d, pd), F32)],
        compiler_params=_params("arbitrary"),
    )(x2, p, target, g_pg, w_pg, b_pg, w_ple_t, g_ple, g_final)


def _ffn_bwd_a(dx2, fg, fu, w_down, comm=None):
    t, d = dx2.shape
    f = fg.shape[1]
    tm = min(TOKEN_TILE, t)

    def body(dx_ref, fg_ref, fu_ref, wd_ref, dfg_ref, dfu_ref, act_ref):
        dact = _dot_nt(dx_ref[...], wd_ref[...])
        fgv = fg_ref[...]
        fuv = fu_ref[...]
        sg = _sigmoid(fgv)
        silu = fgv * sg
        dfu_ref[...] = (dact * silu).astype(dfu_ref.dtype)
        dfg_ref[...] = (dact * fuv * (sg * (1.0 + fgv * (1.0 - sg)))).astype(dfg_ref.dtype)
        act_ref[...] = (silu * fuv).astype(act_ref.dtype)

    return _run(
        body, comm, name="ffn_bwd_a", grid=(t // tm,),
        out_shape=[jax.ShapeDtypeStruct((t, f), BF16)] * 3,
        in_specs=[_rows(tm, d), _rows(tm, f), _rows(tm, f), _whole((f, d))],
        out_specs=[_rows(tm, f)] * 3,
        semantics="parallel",
    )(dx2, fg, fu, w_down)


def _ffn_bwd_b(dfg, dfu, x1, dx2, g_ffn, w_gate_t, w_up_t, comm=None):
    t, d = x1.shape
    f = dfg.shape[1]
    tm = min(TOKEN_TILE, t)

    def body(dfg_ref, dfu_ref, x_ref, dx2_ref, g_ref, wg_ref, wu_ref, dx1_ref, h2_ref, vec_ref):
        _zero_on_first(vec_ref)
        dh2 = _dot(dfg_ref[...], wg_ref[...]) + _dot(dfu_ref[...], wu_ref[...])
        xhat, r = _rms_stats(x_ref[...])
        h2_ref[...] = (xhat * g_ref[...]).astype(h2_ref.dtype)
        dxn, dg = _rms_bwd(dh2, xhat, r, g_ref[...])
        dx1_ref[...] = dx2_ref[...] + dxn
        vec_ref[0:1, :] += _rowsum(dg)

    return _run(
        body, comm, name="ffn_bwd_b", grid=(t // tm,),
        out_shape=[jax.ShapeDtypeStruct((t, d), F32), jax.ShapeDtypeStruct((t, d), BF16),
                   jax.ShapeDtypeStruct((SUBLANES, d), F32)],
        in_specs=[_rows(tm, f), _rows(tm, f), _rows(tm, d), _rows(tm, d), _whole((1, d)), _whole((f, d)), _whole((f, d))],
        out_specs=[_rows(tm, d), _rows(tm, d), _acc((SUBLANES, d))],
        semantics="arbitrary",
    )(dfg, dfu, x1, dx2, g_ffn, w_gate_t, w_up_t)


def _matmul_tn(a, b, tn, name, dtype=F32):
    t, k = a.shape
    n = b.shape[1]

    def body(a_ref, b_ref, o_ref):
        o_ref[...] = _dot_tn(a_ref[...], b_ref[...]).astype(o_ref.dtype)

    return pl.pallas_call(
        body, name=name, grid=(n // tn,),
        out_shape=jax.ShapeDtypeStruct((k, n), dtype),
        in_specs=[_whole((t, k)), pl.BlockSpec((t, tn), lambda j: (0, j))],
        out_specs=pl.BlockSpec((k, tn), lambda j: (0, j)),
        compiler_params=_params("parallel"),
    )(a, b)


def _merge_bwd(dx1, merged, ma, mb, za, zb, y_a, h, w_o, w_a_out_t, w_b_out, comm=None):
    t, d = dx1.shape
    sa, lw = y_a.shape[1], h.shape[1]
    tm = min(TOKEN_TILE, t)

    def body(dx1_ref, mg_ref, ma_ref, mb_ref, za_ref, zb_ref, ya_ref, h_ref, wo_ref, wa_ref, wb_ref,
             dza_ref, dzb_ref, dya_ref, dyb_ref, dwo_out, dwa_out, dwb_out, dwo_ref, dwa_ref, dwb_ref):
        _zero_on_first(dwo_ref, dwa_ref, dwb_ref)
        dx1v = dx1_ref[...].astype(BF16)
        dmg = _dot_nt(dx1v, wo_ref[...])
        ga = _sigmoid(za_ref[...])
        gb = _sigmoid(zb_ref[...])
        dza_ref[...] = dmg * ma_ref[...] * ga * (1.0 - ga)
        dzb_ref[...] = dmg * mb_ref[...] * gb * (1.0 - gb)
        dma = (dmg * ga).astype(BF16)
        dmb = (dmg * gb).astype(BF16)
        dya_ref[...] = _dot(dma, wa_ref[...])
        dyb_ref[...] = _dot_nt(dmb, wb_ref[...])
        dwo_ref[...] += _dot_tn(mg_ref[...], dx1v)
        dwa_ref[...] += _dot_tn(dma, ya_ref[...])
        dwb_ref[...] += _dot_tn(h_ref[...], dmb)
        _store_on_last([(dwo_ref, dwo_out), (dwa_ref, dwa_out), (dwb_ref, dwb_out)])

    return _run(
        body, comm, name="merge_bwd", grid=(t // tm,),
        out_shape=[jax.ShapeDtypeStruct((t, d), F32), jax.ShapeDtypeStruct((t, d), F32),
                   jax.ShapeDtypeStruct((t, sa), F32), jax.ShapeDtypeStruct((t, lw), F32),
                   jax.ShapeDtypeStruct((d, d), BF16), jax.ShapeDtypeStruct((d, sa), BF16),
                   jax.ShapeDtypeStruct((lw, d), BF16)],
        in_specs=[_rows(tm, d), _rows(tm, d), _rows(tm, d), _rows(tm, d), _rows(tm, d), _rows(tm, d),
                  _rows(tm, sa), _rows(tm, lw), _whole((d, d)), _whole((d, sa)), _whole((lw, d))],
        out_specs=[_rows(tm, d), _rows(tm, d), _rows(tm, sa), _rows(tm, lw), _acc((d, d)), _acc((d, sa)), _acc((lw, d))],
        scratch_shapes=[pltpu.VMEM((d, d), F32), pltpu.VMEM((d, sa), F32), pltpu.VMEM((lw, d), F32)],
        semantics="arbitrary",
    )(dx1, merged, ma, mb, za, zb, y_a, h, w_o, w_a_out_t, w_b_out)


def _fold_diag_blocks(dense, row_group, col_group, row0=0, col0=0):
    r, c = dense.shape
    rows = lax.broadcasted_iota(jnp.int32, (r, c), 0) + row0
    cols = lax.broadcasted_iota(jnp.int32, (r, c), 1) + col0
    kept = jnp.where(rows // row_group == cols // col_group, dense, 0.0)
    pick = (lax.broadcasted_iota(jnp.int32, (row_group, r), 0)
            == lax.broadcasted_iota(jnp.int32, (row_group, r), 1) % row_group).astype(F32)
    return jnp.dot(pick, kept, preferred_element_type=F32, precision=lax.Precision.HIGHEST)


def _lru_bwd(dh, xc, hprev, u, conv_w, wr_blk, b_r, wi_blk, b_i, lru_lambda, head_dim, comm=None):
    t, w = dh.shape
    tc = min(TIME_CHUNK, t)
    steps = t // tc
    halo = SUBLANES
    sub_per_chunk = tc // halo

    def body(dh_ref, xc_ref, hp_ref, u_ref, uh_ref, cw_ref, wr_ref, br_ref, wi_ref, bi_ref, lam_ref,
             du_ref, dwr_out, dwi_out, vec_ref, lam_s, a_s, dxc_s, uext_s, carry_s, dwr_ref, dwi_ref):
        chunk = steps - 1 - pl.program_id(0)

        @pl.when(pl.program_id(0) == 0)
        def _():
            carry_s[...] = jnp.zeros_like(carry_s)
            dxc_s[tc:tc + halo, :] = jnp.zeros((halo, w), F32)
            dwr_ref[...] = jnp.zeros_like(dwr_ref)
            dwi_ref[...] = jnp.zeros_like(dwi_ref)
            vec_ref[...] = jnp.zeros_like(vec_ref)

        xc = xc_ref[...]
        r, ig, sp, log_a = _lru_gates(xc, wr_ref, br_ref, wi_ref, bi_ref, lam_ref)
        a = jnp.exp(log_a)
        a_s[...] = a

        def step(i, q):
            at = pl.ds(tc - 1 - i, 1)
            lam_row = dh_ref[at, :] + q
            lam_s[at, :] = lam_row
            return a_s[at, :] * lam_row

        carry_s[0:1, :] = lax.fori_loop(0, tc, step, carry_s[0:1, :], unroll=8)
        lam = lam_s[...]
        mult = jnp.sqrt(-_expm1(2.0 * log_a))
        d_log_a = lam * hp_ref[...] * a - (lam * ig * xc) * (a * a) / mult
        d_ig = lam * mult * xc
        dpre_r = (d_log_a * (-LRU_C * sp)) * r * (1.0 - r)
        dpre_i = d_ig * ig * (1.0 - ig)
        dxc = lam * mult * ig + _dot_nt(dpre_r, wr_ref[...]) + _dot_nt(dpre_i, wi_ref[...])
        dwr_ref[...] += _dot_tn(dpre_r, xc)
        dwi_ref[...] += _dot_tn(dpre_i, xc)
        vec_ref[0:1, :] += _rowsum(dxc)
        vec_ref[1:2, :] += _rowsum(dpre_r)
        vec_ref[2:3, :] += _rowsum(dpre_i)
        vec_ref[3:4, :] += _rowsum(d_log_a * (-LRU_C * r)) * (-_sigmoid(-lam_ref[...]))
        dxc_s[0:tc, :] = dxc
        du = cw_ref[CONV_WIDTH - 1:CONV_WIDTH, :] * dxc
        for k in range(CONV_WIDTH - 1):
            off = CONV_WIDTH - 1 - k
            du = du + cw_ref[k:k + 1, :] * dxc_s[off:off + tc, :]
        du_ref[...] = du
        dxc_s[tc:tc + halo, :] = dxc_s[0:halo, :]
        uext_s[0:halo, :] = jnp.where(chunk > 0, uh_ref[...], 0.0)
        uext_s[halo:halo + tc, :] = u_ref[...]
        for k in range(CONV_WIDTH):
            off = halo - (CONV_WIDTH - 1) + k
            vec_ref[4 + k:5 + k, :] += _rowsum(dxc * uext_s[off:off + tc, :])

        @pl.when(pl.program_id(0) == steps - 1)
        def _():
            dwr_out[...] = _fold_diag_blocks(dwr_ref[...], head_dim, head_dim)
            dwi_out[...] = _fold_diag_blocks(dwi_ref[...], head_dim, head_dim)

    halo_spec = pl.BlockSpec((halo, w), lambda i: (jnp.maximum((steps - 1 - i) * sub_per_chunk - 1, 0), 0))
    return _run(
        body, comm, name="lru_bwd", grid=(steps,),
        out_shape=[jax.ShapeDtypeStruct((t, w), F32), jax.ShapeDtypeStruct((head_dim, w), F32),
                   jax.ShapeDtypeStruct((head_dim, w), F32), jax.ShapeDtypeStruct((SUBLANES, w), F32)],
        in_specs=[_rows_rev(tc, w, steps)] * 4 + [halo_spec, _whole((CONV_WIDTH, w)), _whole((w, w)), _whole((1, w)),
                                                  _whole((w, w)), _whole((1, w)), _whole((1, w))],
        out_specs=[_rows_rev(tc, w, steps), _acc((head_dim, w)), _acc((head_dim, w)), _acc((SUBLANES, w))],
        scratch_shapes=[pltpu.VMEM((tc, w), F32), pltpu.VMEM((tc, w), F32), pltpu.VMEM((tc + halo, w), F32),
                        pltpu.VMEM((halo + tc, w), F32), pltpu.VMEM((SUBLANES, w), F32), pltpu.VMEM((w, w), F32),
                        pltpu.VMEM((w, w), F32)],
        semantics="arbitrary",
    )(dh, xc, hprev, u, u, conv_w, wr_blk, b_r, wi_blk, b_i, lru_lambda)


def _s5_bwd(dya, y, sr, si, u, w_glu, b_glu, cre_blk, cimn_blk, bbr_blk, bbi_blk, ar, ai, d_skip, comm=None):
    t, sa = dya.shape
    gn = sr.shape[1]
    tc = min(TIME_CHUNK, t)
    steps = t // tc
    halo = SUBLANES

    def body(dya_ref, y_ref, sr_ref, si_ref, u_ref, wg_ref, bg_ref, cre_ref, cim_ref, bbr_ref, bbi_ref,
             ar_ref, ai_ref, d_ref, du_ref, lr_ref, li_ref, dy_ref, dwg_out, vsa_ref, vgn_ref, gr_s, gi_s, cr_s, ci_s,
             dwg_ref):
        @pl.when(pl.program_id(0) == 0)
        def _():
            cr_s[...] = jnp.zeros_like(cr_s)
            ci_s[...] = jnp.zeros_like(ci_s)
            gr_s[tc:tc + halo, :] = jnp.zeros((halo, gn), F32)
            gi_s[tc:tc + halo, :] = jnp.zeros((halo, gn), F32)
            dwg_ref[...] = jnp.zeros_like(dwg_ref)
            vsa_ref[...] = jnp.zeros_like(vsa_ref)
            vgn_ref[...] = jnp.zeros_like(vgn_ref)

        yv = y_ref[...]
        uv = u_ref[...]
        zz = _gelu(yv)
        sg = _sigmoid(_dot(zz, wg_ref[...]) + bg_ref[...])
        dyav = dya_ref[...]
        dq = dyav * zz * sg * (1.0 - sg)
        dzz = dyav * sg + _dot_nt(dq, wg_ref[...])
        dwg_ref[...] += _dot_tn(zz, dq)
        dy = dzz * _gelu_grad(yv)
        dyb = dy.astype(BF16)
        dy_ref[...] = dyb.astype(dy_ref.dtype)
        vsa_ref[0:1, :] += _rowsum(dq)
        vsa_ref[1:2, :] += _rowsum(dy * uv)
        gr_s[0:tc, :] = _dot(dyb, cre_ref[...])
        gi_s[0:tc, :] = _dot(dyb, cim_ref[...])
        a_r = ar_ref[...]
        a_i = ai_ref[...]

        def step(i, carry):
            l_r, l_i = carry
            at = pl.ds(tc - 1 - i, 1)
            n_r = gr_s[at, :] + a_r * l_r + a_i * l_i
            n_i = gi_s[at, :] + a_r * l_i - a_i * l_r
            gr_s[at, :] = n_r
            gi_s[at, :] = n_i
            return n_r, n_i

        l_r, l_i = lax.fori_loop(0, tc, step, (cr_s[0:1, :], ci_s[0:1, :]), unroll=8)
        cr_s[0:1, :] = l_r
        ci_s[0:1, :] = l_i
        nxt_r = gr_s[1:tc + 1, :]
        nxt_i = gi_s[1:tc + 1, :]
        srv = sr_ref[...]
        siv = si_ref[...]
        vgn_ref[0:1, :] += _rowsum(nxt_r * srv + nxt_i * siv)
        vgn_ref[1:2, :] += _rowsum(nxt_i * srv - nxt_r * siv)
        lam_r = gr_s[0:tc, :]
        lam_i = gi_s[0:tc, :]
        gr_s[tc:tc + halo, :] = gr_s[0:halo, :]
        gi_s[tc:tc + halo, :] = gi_s[0:halo, :]
        lrb = lam_r.astype(BF16)
        lib = lam_i.astype(BF16)
        lr_ref[...] = lrb.astype(lr_ref.dtype)
        li_ref[...] = lib.astype(li_ref.dtype)
        du_ref[...] = _dot(lrb, bbr_ref[...]) + _dot(lib, bbi_ref[...]) + dy * d_ref[...]
        _store_on_last([(dwg_ref, dwg_out)])

    return _run(
        body, comm, name="s5_bwd", grid=(steps,),
        out_shape=[jax.ShapeDtypeStruct((t, sa), F32), jax.ShapeDtypeStruct((t, gn), BF16),
                   jax.ShapeDtypeStruct((t, gn), BF16), jax.ShapeDtypeStruct((t, sa), BF16),
                   jax.ShapeDtypeStruct((sa, sa), BF16), jax.ShapeDtypeStruct((SUBLANES, sa), F32),
                   jax.ShapeDtypeStruct((SUBLANES, gn), F32)],
        in_specs=[_rows_rev(tc, sa, steps), _rows_rev(tc, sa, steps), _rows_rev(tc, gn, steps), _rows_rev(tc, gn, steps),
                  _rows_rev(tc, sa, steps), _whole((sa, sa)), _whole((1, sa)), _whole((sa, gn)), _whole((sa, gn)),
                  _whole((gn, sa)), _whole((gn, sa)), _whole((1, gn)), _whole((1, gn)), _whole((1, sa))],
        out_specs=[_rows_rev(tc, sa, steps), _rows_rev(tc, gn, steps), _rows_rev(tc, gn, steps), _rows_rev(tc, sa, steps),
                   _acc((sa, sa)), _acc((SUBLANES, sa)), _acc((SUBLANES, gn))],
        scratch_shapes=[pltpu.VMEM((tc + halo, gn), F32), pltpu.VMEM((tc + halo, gn), F32),
                        pltpu.VMEM((SUBLANES, gn), F32), pltpu.VMEM((SUBLANES, gn), F32), pltpu.VMEM((sa, sa), F32)],
        semantics="arbitrary",
    )(dya, y, sr, si, u, w_glu, b_glu, cre_blk, cimn_blk, bbr_blk, bbi_blk, ar, ai, d_skip)


def _inproj_bwd(dparts, x, dx1, g_mix, w_in_t, comm=None):
    t, d = x.shape
    n = w_in_t.shape[0]
    widths = [p.shape[1] for p in dparts]
    offs = [sum(widths[:i]) for i in range(len(widths) + 1)]
    tm = min(TOKEN_TILE, t)
    np_ = len(dparts)

    def body(*refs):
        dz_refs = refs[:np_]
        x_ref, dx1_ref, g_ref, w_ref, gx_ref, h_ref, dz_ref, vd_ref, vn_ref = refs[np_:]
        _zero_on_first(vd_ref, vn_ref)
        dh = jnp.zeros((tm, d), F32)
        for k, r in enumerate(dz_refs):
            lo, hi = offs[k], offs[k + 1]
            dzk = r[...]
            dh = dh + _dot(dzk, w_ref[lo:hi, :])
            dz_ref[:, lo:hi] = dzk.astype(dz_ref.dtype)
            vn_ref[0:1, lo:hi] += _rowsum(dzk)
        xhat, r0 = _rms_stats(x_ref[...])
        h_ref[...] = (xhat * g_ref[...]).astype(h_ref.dtype)
        dxn, dg = _rms_bwd(dh, xhat, r0, g_ref[...])
        gx_ref[...] = dx1_ref[...] + dxn
        vd_ref[0:1, :] += _rowsum(dg)

    return _run(
        body, comm, name="inproj_bwd", grid=(t // tm,),
        out_shape=[jax.ShapeDtypeStruct((t, d), F32), jax.ShapeDtypeStruct((t, d), BF16),
                   jax.ShapeDtypeStruct((t, n), BF16), jax.ShapeDtypeStruct((SUBLANES, d), F32),
                   jax.ShapeDtypeStruct((SUBLANES, n), F32)],
        in_specs=[_rows(tm, w) for w in widths] + [_rows(tm, d), _rows(tm, d), _whole((1, d)), _whole((n, d))],
        out_specs=[_rows(tm, d), _rows(tm, d), _rows(tm, n), _acc((SUBLANES, d)), _acc((SUBLANES, n))],
        semantics="arbitrary",
    )(*dparts, x, dx1, g_mix, w_in_t)


def _prep(lr_row, li_row, ldt_row, lr_col, li_col, ldt_col, b_re, b_im, c_re, c_im, w_r, w_i, comm=None):
    gn, pch = b_re.shape
    sa, n = c_re.shape
    w, hd = w_r.shape

    def expand(vals, row_group, col_group, width):
        r, k = vals.shape
        tile = (lax.broadcasted_iota(jnp.int32, (k, width), 0) == lax.broadcasted_iota(jnp.int32, (k, width), 1) % k)
        rows = lax.broadcasted_iota(jnp.int32, (r, width), 0) // row_group
        cols = lax.broadcasted_iota(jnp.int32, (r, width), 1) // col_group
        return jnp.where(rows == cols, _dot(vals, tile.astype(BF16)), 0.0)

    def body(lrr, lir, ldr, lrc, lic, ldc, bre, bim, cre, cim, wr, wi,
             ar_o, ai_o, bbr_o, bbi_o, cre_o, cim_o, wr_o, wi_o):
        ar, ai, _, _ = _disc_scalars(lrr[...], lir[...], ldr[...])
        ar_o[...] = ar
        ai_o[...] = ai
        _, _, bbr, bbi = _disc_cols(lrc[...], lic[...], ldc[...], bre[...], bim[...])
        bbr_o[...] = expand(bbr, n, pch, sa).astype(bbr_o.dtype)
        bbi_o[...] = expand(bbi, n, pch, sa).astype(bbi_o.dtype)
        cre_o[...] = expand(cre[...], pch, n, gn).astype(cre_o.dtype)
        cim_o[...] = expand(-cim[...], pch, n, gn).astype(cim_o.dtype)
        wr_o[...] = expand(wr[...], hd, hd, w).astype(wr_o.dtype)
        wi_o[...] = expand(wi[...], hd, hd, w).astype(wi_o.dtype)

    args = (lr_row, li_row, ldt_row, lr_col, li_col, ldt_col, b_re, b_im, c_re, c_im, w_r, w_i)
    out_shape = [jax.ShapeDtypeStruct((1, gn), F32), jax.ShapeDtypeStruct((1, gn), F32),
                 jax.ShapeDtypeStruct((gn, sa), BF16), jax.ShapeDtypeStruct((gn, sa), BF16),
                 jax.ShapeDtypeStruct((sa, gn), BF16), jax.ShapeDtypeStruct((sa, gn), BF16),
                 jax.ShapeDtypeStruct((w, w), BF16), jax.ShapeDtypeStruct((w, w), BF16)]
    return _run(
        body, comm, name="prep", grid=(1,), out_shape=out_shape, in_specs=[_whole(a.shape) for a in args],
        out_specs=[_acc(s.shape) for s in out_shape], semantics="arbitrary",
    )(*args)


def _s5_param_grads(lam_r, lam_i, sr, si, u, dy, pch, n, comm=None):
    t, gn = lam_r.shape
    sa = u.shape[1]
    tb = min(gn, 512)

    def body(lr_ref, li_ref, sr_ref, si_ref, u_ref, dy_ref, dbr_ref, dbi_ref, dcr_ref, dci_ref):
        _zero_on_first(dcr_ref, dci_ref)
        base = pl.program_id(0) * tb
        uv = u_ref[...]
        dyv = dy_ref[...]
        dbr_ref[...] = _fold_diag_blocks(_dot_tn(uv, lr_ref[...]), pch, n, col0=base)
        dbi_ref[...] = _fold_diag_blocks(_dot_tn(uv, li_ref[...]), pch, n, col0=base)
        dcr_ref[...] += _fold_diag_blocks(_dot_tn(sr_ref[...], dyv), n, pch, row0=base)
        dci_ref[...] += _fold_diag_blocks(_dot_tn(si_ref[...], dyv), n, pch, row0=base)

    cols = pl.BlockSpec((t, tb), lambda j: (0, j))
    return _run(
        body, comm, name="s5_param_grads", grid=(gn // tb,),
        out_shape=[jax.ShapeDtypeStruct((pch, gn), F32), jax.ShapeDtypeStruct((pch, gn), F32),
                   jax.ShapeDtypeStruct((n, sa), F32), jax.ShapeDtypeStruct((n, sa), F32)],
        in_specs=[cols, cols, cols, cols, _whole((t, sa)), _whole((t, sa))],
        out_specs=[pl.BlockSpec((pch, tb), lambda j: (0, j)), pl.BlockSpec((pch, tb), lambda j: (0, j)),
                   _acc((n, sa)), _acc((n, sa))],
        semantics="arbitrary",
    )(lam_r, lam_i, sr, si, u, dy)


SMALL_PARTS = ["vec_tail", "vec_ffn", "vec_lru", "vec_mix", "vec_bin", "vec_sa", "vec_gn", "dw_r", "dw_i", "dbb_re",
               "dbb_im", "dc_re", "dc_imn", "loss"]


def _small_reduce(parts, shapes, lr_col, li_col, ldt_col, b_re, b_im, groups, comm=None):
    gn, pch = b_re.shape
    n = gn // groups
    nparts = parts[SMALL_PARTS[0]].size // math.prod(shapes[SMALL_PARTS[0]])
    np_, nout = len(SMALL_PARTS), 24

    def body(*refs):
        ins = refs[:np_]
        lr, li, ldt, bre, bim = refs[np_:np_ + 5]
        outs = refs[np_ + 5:np_ + 5 + nout]
        sums = dict(zip(SMALL_PARTS, refs[np_ + 5 + nout:]))

        @pl.when(pl.program_id(0) == 0)
        def _():
            for k, r in zip(SMALL_PARTS, ins):
                sums[k][...] = r[...]

        @pl.when(pl.program_id(0) > 0)
        def _():
            for k, r in zip(SMALL_PARTS, ins):
                sums[k][...] += r[...]

        @pl.when(pl.program_id(0) == nparts - 1)
        def _():
            finish({k: s[...] for k, s in sums.items()}, lr, li, ldt, bre, bim, *outs)

    def finish(tot, lr, li, ldt, bre, bim, o_loss, o_gmix, o_bin, o_bglu, o_s5d, o_convb, o_br, o_bi, o_lam, o_gffn,
               o_gpg, o_bpg, o_gple, o_gfin, o_wr, o_wi, o_cre, o_cim, o_bre, o_bim, o_lre, o_lim, o_ldt, o_convw):
        o_loss[...] = tot["loss"]
        o_convw[...] = tot["vec_lru"][SUBLANES - CONV_WIDTH:SUBLANES]
        o_bpg[...] = tot["vec_tail"][0:1]
        o_gpg[...] = tot["vec_tail"][1:2]
        o_gple[...] = tot["vec_tail"][2:3]
        o_gfin[...] = tot["vec_tail"][3:4]
        o_gffn[...] = tot["vec_ffn"][0:1]
        o_convb[...] = tot["vec_lru"][0:1]
        o_br[...] = tot["vec_lru"][1:2]
        o_bi[...] = tot["vec_lru"][2:3]
        o_lam[...] = tot["vec_lru"][3:4]
        o_gmix[...] = tot["vec_mix"][0:1]
        o_bin[...] = tot["vec_bin"][0:1]
        o_bglu[...] = tot["vec_sa"][0:1]
        o_s5d[...] = tot["vec_sa"][1:2]
        o_wr[...] = tot["dw_r"].T
        o_wi[...] = tot["dw_i"].T
        o_cre[...] = tot["dc_re"].T
        o_cim[...] = -tot["dc_imn"].T
        d_a = tot["vec_gn"].T
        _, chain = jax.vjp(_disc_cols, lr[...], li[...], ldt[...], bre[...], bim[...])
        d_lr, d_li, d_ldt, d_bre, d_bim = chain((d_a[:, 0:1], d_a[:, 1:2], tot["dbb_re"].T, tot["dbb_im"].T))
        o_lre[...] = d_lr
        o_lim[...] = d_li
        o_bre[...] = d_bre
        o_bim[...] = d_bim
        same = (lax.broadcasted_iota(jnp.int32, (groups, gn), 0)
                == lax.broadcasted_iota(jnp.int32, (groups, gn), 1) // n).astype(F32)
        o_ldt[...] = jnp.dot(same, d_ldt * jnp.ones((1, LANES), F32), preferred_element_type=F32,
                             precision=lax.Precision.HIGHEST)[:, 0:1]

    d = shapes["vec_mix"][1]
    nz = shapes["vec_bin"][1]
    sa = shapes["vec_sa"][1]
    w = shapes["vec_lru"][1]
    row = lambda c: jax.ShapeDtypeStruct((1, c), F32)
    out_shape = [jax.ShapeDtypeStruct(shapes["loss"], F32), row(d), row(nz), row(sa), row(sa), row(w), row(w), row(w),
                 row(w), row(d), row(d), row(d), row(d), row(d),
                 jax.ShapeDtypeStruct(shapes["dw_r"][::-1], F32), jax.ShapeDtypeStruct(shapes["dw_i"][::-1], F32),
                 jax.ShapeDtypeStruct(shapes["dc_re"][::-1], F32), jax.ShapeDtypeStruct(shapes["dc_imn"][::-1], F32),
                 jax.ShapeDtypeStruct((gn, pch), F32), jax.ShapeDtypeStruct((gn, pch), F32),
                 jax.ShapeDtypeStruct((gn, 1), F32), jax.ShapeDtypeStruct((gn, 1), F32),
                 jax.ShapeDtypeStruct((groups, 1), F32), jax.ShapeDtypeStruct((CONV_WIDTH, w), F32)]
    def part_spec(k):
        r, c = shapes[k]
        if parts[k].ndim == 3:
            return pl.BlockSpec((None, r, c), lambda i: (i, 0, 0))
        return pl.BlockSpec((r, c), lambda i: (i, 0))

    outs = _run(
        body, comm, name="small_reduce", grid=(nparts,), out_shape=out_shape,
        in_specs=[part_spec(k) for k in SMALL_PARTS] + [_whole(a.shape) for a in (lr_col, li_col, ldt_col, b_re, b_im)],
        out_specs=[_acc(s.shape) for s in out_shape],
        scratch_shapes=[pltpu.VMEM(shapes[k], F32) for k in SMALL_PARTS],
        semantics="arbitrary",
    )(*[parts[k] for k in SMALL_PARTS], lr_col, li_col, ldt_col, b_re, b_im)
    extra = None
    if comm is not None:
        outs, extra = outs
    names = ["loss", "g_mix", "b_in", "b_glu", "s5_d", "conv_b", "b_r", "b_i", "lru_lambda", "g_ffn", "g_ple_gate",
             "b_ple_gate", "g_ple", "g_final", "w_r", "w_i", "s5_c_re", "s5_c_im", "s5_b_re", "s5_b_im", "lam_re",
             "lam_im", "log_dt", "conv_w"]
    res = dict(zip(names, outs))
    return res if comm is None else (res, extra)


def _adamw_small(ws, gs, ms, vs):
    n = len(ws)

    def body(*refs):
        w_r, g_r, m_r, v_r = (refs[i * n:(i + 1) * n] for i in range(4))
        d_o, m_o, v_o = (refs[(4 + i) * n:(5 + i) * n] for i in range(3))
        for i in range(n):
            delta, m_new, v_new = _adamw_math(w_r[i][...], g_r[i][...], m_r[i][...], v_r[i][...])
            d_o[i][...] = delta
            m_o[i][...] = m_new
            v_o[i][...] = v_new

    shapes = [jax.ShapeDtypeStruct(a.shape, F32) for a in ws]
    outs = pl.pallas_call(body, name="adamw_small", out_shape=shapes * 3)(*ws, *gs, *ms, *vs)
    return outs[:n], outs[n:2 * n], outs[2 * n:]


def _adamw_math(w, g, m, v):
    m_new = ADAM_B1 * m + (1.0 - ADAM_B1) * g
    v_new = ADAM_B2 * v + (1.0 - ADAM_B2) * (g * g)
    m_hat = m_new / (1.0 - ADAM_B1 ** ADAM_STEP)
    v_hat = v_new / (1.0 - ADAM_B2 ** ADAM_STEP)
    delta = -ADAM_LR * (m_hat / (jnp.sqrt(v_hat) + ADAM_EPS) + ADAM_WD * w)
    return delta, m_new, v_new


def _row_tile(rows):
    for cand in (256, 128, 64, 32, 16, 8):
        if rows % cand == 0:
            return cand
    return rows


def _adamw(parts, w, m, v, name, transposed=False):
    rows, cols = w.shape
    npart = parts.shape[0]
    tr = _row_tile(rows)
    if transposed:
        parts_spec = pl.BlockSpec((npart, cols, tr), lambda i: (0, 0, i))
    else:
        parts_spec = pl.BlockSpec((npart, tr, cols), lambda i: (0, i, 0))

    def body(p_ref, w_ref, m_ref, v_ref, g_ref, d_ref, mo_ref, vo_ref):
        g = p_ref[0].astype(F32)
        for k in range(1, npart):
            g = g + p_ref[k].astype(F32)
        if transposed:
            g = g.T
        delta, m_new, v_new = _adamw_math(w_ref[...], g, m_ref[...], v_ref[...])
        g_ref[...] = g
        d_ref[...] = delta
        mo_ref[...] = m_new
        vo_ref[...] = v_new

    return pl.pallas_call(
        body, name=name, grid=(rows // tr,),
        out_shape=[jax.ShapeDtypeStruct((rows, cols), F32)] * 4,
        in_specs=[parts_spec] + [_rows(tr, cols)] * 3,
        out_specs=[_rows(tr, cols)] * 4,
        compiler_params=_params("parallel"),
    )(parts, w, m, v)


def _mesh_position():
    return lax.axis_index("x"), lax.axis_index("y"), lax.axis_index("c")


def _flip(pos, rel):
    x, y, c = pos
    return (1 - x if rel & 4 else x, 1 - y if rel & 2 else y, 1 - c if rel & 1 else c)


def _index(pos):
    return 4 * pos[0] + 2 * pos[1] + pos[2]


_ANY = pl.BlockSpec(memory_space=pl.ANY)
FLAT_ROWS = 32


def _dma_sems(n):
    return [pltpu.SemaphoreType.DMA((n, N_DEV - 1)), pltpu.SemaphoreType.DMA((n, N_DEV - 1)), pltpu.SemaphoreType.DMA((n,))]


def _block_of(ref, idx, rows, flat):
    if flat:
        return ref.at[pl.ds(pl.multiple_of(idx * rows, FLAT_ROWS), rows), :]
    return ref.at[idx]


class _Gather:
    chips = (4, 2, 6)

    def __init__(self, shards):
        self.inputs = list(shards)
        self.flat = [s.shape[0] % FLAT_ROWS == 0 for s in shards]
        self.out_shape = [
            jax.ShapeDtypeStruct((N_DEV * s.shape[0], s.shape[1]) if f else (N_DEV,) + s.shape, s.dtype)
            for s, f in zip(shards, self.flat)]
        self.sems = _dma_sems(len(shards))

    def _copy(self, ins, outs, sems, i, k, block, to, own=False):
        dst = _block_of(outs[i], _index(block), self.inputs[i].shape[0], self.flat[i])
        return pltpu.make_async_remote_copy(
            src_ref=ins[i] if own else dst, dst_ref=dst, send_sem=sems[0].at[i, k], recv_sem=sems[1].at[i, k],
            device_id=to, device_id_type=MESH)

    def _local(self, ins, outs, sems, i, me):
        dst = _block_of(outs[i], _index(me), self.inputs[i].shape[0], self.flat[i])
        return pltpu.make_async_copy(ins[i], dst, sems[2].at[i])

    def _first(self, ins, outs, sems, i, me):
        cps = [self._copy(ins, outs, sems, i, 0, me, _flip(me, 1), own=True)]
        cps += [self._copy(ins, outs, sems, i, 1 + j, me, _flip(me, rel), own=True) for j, rel in enumerate(self.chips)]
        return cps

    def _passed(self, ins, outs, sems, i, j, me):
        return self._copy(ins, outs, sems, i, 4 + j, _flip(me, self.chips[j]), _flip(me, 1))

    def before(self, ins, outs, sems):
        n = len(self.inputs)
        me = _mesh_position()

        @pl.when(pl.program_id(0) == 0)
        def _():
            for i in range(n):
                self._local(ins, outs, sems, i, me).start()
                for cp in self._first(ins, outs, sems, i, me):
                    cp.start()

        @pl.when(pl.program_id(0) == pl.num_programs(0) - 1)
        def _():
            for j, rel in enumerate(self.chips):
                for i in range(n):
                    self._copy(ins, outs, sems, i, 1 + j, _flip(me, rel), me).wait_recv()
                    self._passed(ins, outs, sems, i, j, me).start()

    def after(self, ins, outs, sems):
        n = len(self.inputs)
        me = _mesh_position()
        sibling = _flip(me, 1)

        @pl.when(pl.program_id(0) == pl.num_programs(0) - 1)
        def _():
            for i in range(n):
                self._copy(ins, outs, sems, i, 0, sibling, me).wait_recv()
                for j, rel in enumerate(self.chips):
                    self._copy(ins, outs, sems, i, 4 + j, _flip(sibling, rel), me).wait_recv()
            for i in range(n):
                for cp in self._first(ins, outs, sems, i, me):
                    cp.wait_send()
                for j in range(len(self.chips)):
                    self._passed(ins, outs, sems, i, j, me).wait_send()
                self._local(ins, outs, sems, i, me).wait()


N_CHIPS = N_DEV // 2


def _chip(pos):
    return 2 * pos[0] + pos[1]


class _PairSwap:
    def __init__(self, arrays):
        self.inputs = list(arrays)
        self.rows = [a.shape[0] // N_DEV for a in arrays]
        for r in self.rows:
            assert r % FLAT_ROWS == 0, r
        self.out_shape = [jax.ShapeDtypeStruct((N_CHIPS, r, a.shape[1]), a.dtype) for a, r in zip(arrays, self.rows)]
        n = len(arrays)
        self.sems = [pltpu.SemaphoreType.DMA((n, N_CHIPS)), pltpu.SemaphoreType.DMA((n, N_CHIPS))]

    def _copy(self, ins, outs, sems, i, j, me):
        sibling = _flip(me, 1)
        return pltpu.make_async_remote_copy(
            src_ref=_block_of(ins[i], 2 * j + sibling[2], self.rows[i], True), dst_ref=outs[i].at[j],
            send_sem=sems[0].at[i, j], recv_sem=sems[1].at[i, j], device_id=sibling, device_id_type=MESH)

    def before(self, ins, outs, sems):
        me = _mesh_position()

        @pl.when(pl.program_id(0) == 0)
        def _():
            for i in range(len(self.inputs)):
                for j in range(N_CHIPS):
                    self._copy(ins, outs, sems, i, j, me).start()

    def after(self, ins, outs, sems):
        me = _mesh_position()

        @pl.when(pl.program_id(0) == pl.num_programs(0) - 1)
        def _():
            for i in range(len(self.inputs)):
                for j in range(N_CHIPS):
                    self._copy(ins, outs, sems, i, j, me).wait()


class _ChipExchange:
    chips = (4, 2, 6)

    def __init__(self, arrays):
        self.inputs = list(arrays)
        self.out_shape = [jax.ShapeDtypeStruct(a.shape, a.dtype) for a in arrays]
        n = len(arrays)
        self.sems = [pltpu.SemaphoreType.DMA((n, 3)), pltpu.SemaphoreType.DMA((n, 3)), pltpu.SemaphoreType.DMA((n,))]

    def _send(self, ins, outs, sems, i, k, me):
        peer = _flip(me, self.chips[k])
        return pltpu.make_async_remote_copy(
            src_ref=ins[i].at[_chip(peer)], dst_ref=outs[i].at[_chip(me)], send_sem=sems[0].at[i, k],
            recv_sem=sems[1].at[i, k], device_id=peer, device_id_type=MESH)

    def _arrival(self, ins, outs, sems, i, k, me):
        peer = _flip(me, self.chips[k])
        return pltpu.make_async_remote_copy(
            src_ref=ins[i].at[_chip(me)], dst_ref=outs[i].at[_chip(peer)], send_sem=sems[0].at[i, k],
            recv_sem=sems[1].at[i, k], device_id=peer, device_id_type=MESH)

    def _local(self, ins, outs, sems, i, me):
        return pltpu.make_async_copy(ins[i].at[_chip(me)], outs[i].at[_chip(me)], sems[2].at[i])

    def before(self, ins, outs, sems):
        me = _mesh_position()

        @pl.when(pl.program_id(0) == 0)
        def _():
            for i in range(len(self.inputs)):
                self._local(ins, outs, sems, i, me).start()
                for k in range(len(self.chips)):
                    self._send(ins, outs, sems, i, k, me).start()

    def after(self, ins, outs, sems):
        me = _mesh_position()

        @pl.when(pl.program_id(0) == pl.num_programs(0) - 1)
        def _():
            for i in range(len(self.inputs)):
                for k in range(len(self.chips)):
                    self._arrival(ins, outs, sems, i, k, me).wait_recv()
            for i in range(len(self.inputs)):
                for k in range(len(self.chips)):
                    self._send(ins, outs, sems, i, k, me).wait_send()
                self._local(ins, outs, sems, i, me).wait()


def _pair_add(grads, halves, name):
    n = len(grads)

    def body(*refs):
        g_refs, h_refs, o_refs = refs[:n], refs[n:2 * n], refs[2 * n:]
        c = lax.axis_index("c")
        for i in range(n):
            r = h_refs[i].shape[1]
            for j in range(N_CHIPS):
                own = g_refs[i][pl.ds(pl.multiple_of((2 * j + c) * r, FLAT_ROWS), r), :]
                o_refs[i][j] = (own.astype(F32) + h_refs[i][j].astype(F32)).astype(o_refs[i].dtype)

    return pl.pallas_call(
        body, name=name, out_shape=[jax.ShapeDtypeStruct(h.shape, h.dtype) for h in halves],
        compiler_params=pltpu.CompilerParams(vmem_limit_bytes=VMEM_LIMIT),
    )(*grads, *halves)


class _Both:
    def __init__(self, first, second):
        self.jobs = (first, second)
        self.inputs = first.inputs + second.inputs
        self.out_shape = first.out_shape + second.out_shape
        self.sems = first.sems + second.sems

    def _each(self, ins, outs, sems):
        a = self.jobs[0]
        i, o, s = len(a.inputs), len(a.out_shape), len(a.sems)
        return ((a, ins[:i], outs[:o], sems[:s]), (self.jobs[1], ins[i:], outs[o:], sems[s:]))

    def before(self, ins, outs, sems):
        for job, i, o, s in self._each(ins, outs, sems):
            job.before(i, o, s)

    def after(self, ins, outs, sems):
        for job, i, o, s in self._each(ins, outs, sems):
            job.after(i, o, s)


def _run(body, comm, *, semantics, out_shape, in_specs, out_specs, scratch_shapes=(), **kw):
    if comm is None:
        return pl.pallas_call(body, out_shape=out_shape, in_specs=in_specs, out_specs=out_specs,
                              scratch_shapes=list(scratch_shapes), compiler_params=_params(semantics), **kw)
    single = not isinstance(out_shape, (list, tuple))
    outs = [out_shape] if single else list(out_shape)
    ospecs = [out_specs] if single else list(out_specs)
    counts = [len(in_specs), len(comm.inputs), len(outs), len(comm.out_shape), len(scratch_shapes), len(comm.sems)]

    def carrying(*refs):
        groups, pos = [], 0
        for c in counts:
            groups.append(refs[pos:pos + c])
            pos += c
        main_in, comm_in, main_out, comm_out, main_scratch, comm_sems = groups
        comm.before(comm_in, comm_out, comm_sems)
        body(*main_in, *main_out, *main_scratch)
        comm.after(comm_in, comm_out, comm_sems)

    call = pl.pallas_call(
        carrying, out_shape=outs + list(comm.out_shape), in_specs=list(in_specs) + [_ANY] * len(comm.inputs),
        out_specs=ospecs + [_ANY] * len(comm.out_shape), scratch_shapes=list(scratch_shapes) + list(comm.sems),
        compiler_params=_params("arbitrary"), **kw)

    def apply(*args):
        res = call(*args, *comm.inputs)
        main = res[:len(outs)]
        return (main[0] if single else list(main)), list(res[len(outs):])

    return apply


def _alone(comm, name):
    return _run(lambda: None, comm, semantics="arbitrary", name=name, grid=(1,), out_shape=[], in_specs=[], out_specs=[])()[1]


SHARDED = {"w_in": 1, "w_glu": 0, "conv_w": 1, "w_a_out": 1, "w_b_out": 0, "w_o": 0, "w_ffn_gate": 1, "w_ffn_up": 1,
           "w_ffn_down": 0, "w_ple_gate": 0, "w_ple": 1}
TRANSPOSED = ("w_in", "w_a_out", "w_ffn_gate", "w_ffn_up", "w_ple")
SMALL = ["g_mix", "b_in", "lam_re", "lam_im", "log_dt", "s5_b_re", "s5_b_im", "s5_c_re", "s5_c_im", "s5_d", "b_glu",
         "conv_b", "w_r", "b_r", "w_i", "b_i", "lru_lambda", "g_ffn", "g_ple_gate", "b_ple_gate", "g_ple", "g_final"]
WEIGHTS = ["g_mix", "w_in", "b_in", "lam_re", "lam_im", "log_dt", "s5_b_re", "s5_b_im", "s5_c_re", "s5_c_im", "s5_d",
           "w_glu", "b_glu", "conv_w", "conv_b", "w_r", "b_r", "w_i", "b_i", "lru_lambda", "w_a_out", "w_b_out", "w_o",
           "g_ffn", "w_ffn_gate", "w_ffn_up", "w_ffn_down", "g_ple_gate", "w_ple_gate", "b_ple_gate", "w_ple", "g_ple",
           "g_final"]


def _unblock(gathered, axis):
    nb, r, c = gathered.shape
    if axis == 0:
        return gathered.reshape(nb * r, c)
    return jnp.transpose(gathered, (1, 0, 2)).reshape(r, nb * c)


def _disc_scalars(lr, li, ldt):
    dt = jnp.exp(ldt)
    mag = jnp.exp(lr * dt)
    ar = mag * jnp.cos(li * dt)
    ai = mag * jnp.sin(li * dt)
    den = lr * lr + li * li
    nr = ar - 1.0
    fr = (nr * lr + ai * li) / den
    fi = (ai * lr - nr * li) / den
    return ar, ai, fr, fi


def _disc_cols(lr, li, ldt, b_re, b_im):
    ar, ai, fr, fi = _disc_scalars(lr, li, ldt)
    return ar, ai, fr * b_re - fi * b_im, fr * b_im + fi * b_re


def _local_step(x, p, target, src, small, disc, distributed=True):
    full = {} if distributed else dict(src)
    gw, halves, pairs, got = {}, {}, {}, {}

    def gather(keys):
        return (_Gather([src[k] for k in keys]), keys, full) if distributed else None

    def swap(keys):
        return (_PairSwap([gw[k] for k in keys]), keys, halves) if distributed else None

    def chips(keys):
        return (_ChipExchange([pairs[k] for k in keys]), keys, got) if distributed else None

    def add_pairs(keys):
        if distributed:
            pairs.update(zip(keys, _pair_add([gw[k] for k in keys], [halves[k] for k in keys], "pair_add_" + keys[0])))

    def carry(fn, *args, jobs=()):
        jobs = [j for j in jobs if j is not None]
        if not jobs:
            return fn(*args)
        comm = jobs[0][0]
        for j in jobs[1:]:
            comm = _Both(comm, j[0])
        res, extra = fn(*args, comm=comm)
        for job, keys, sink in jobs:
            sink.update(zip(keys, extra[:len(job.out_shape)]))
            extra = extra[len(job.out_shape):]
        return res

    d = x.shape[1]
    g, n, pch = small["s5_b_re"].shape
    heads = small["w_r"].shape[0]
    sa, lw = g * pch, small["lru_lambda"].shape[-1]
    widths = [sa, lw, d, d]
    row = lambda v: v.reshape(1, -1)

    hd = small["w_r"].shape[-1]
    ar_row, ai_row, bbr_blk, bbi_blk, cre_blk, cimn_blk, wr_blk, wi_blk = carry(
        _prep, *disc["rows"], *disc["cols"], disc["b_re"], disc["b_im"], small["s5_c_re"].reshape(sa, n),
        small["s5_c_im"].reshape(sa, n), small["w_r"].reshape(lw, hd), small["w_i"].reshape(lw, hd),
        jobs=[gather(["w_in"])])
    d_row = row(small["s5_d"])
    u_a, u_b, za, zb = carry(_inproj_fwd, x, row(small["g_mix"]), full["w_in"], row(small["b_in"]), widths,
                             jobs=[gather(["w_glu", "conv_w", "w_a_out"])])
    conv_w = _unblock(full["conv_w"], 1) if distributed else full["conv_w"]
    sr, si, y, y_a = carry(_s5_fwd, u_a, bbr_blk, bbi_blk, ar_row, ai_row, cre_blk, cimn_blk, d_row, full["w_glu"],
                           row(small["b_glu"]), jobs=[gather(["w_ffn_gate", "w_b_out"])])
    xc, h, hprev = carry(_lru_fwd, u_b, conv_w, row(small["conv_b"]), wr_blk, row(small["b_r"]), wi_blk,
                         row(small["b_i"]), row(small["lru_lambda"]), jobs=[gather(["w_ffn_up", "w_o"])])
    x1, merged, ma, mb = _merge_fwd(y_a, h, za, zb, x, full["w_a_out"], full["w_b_out"], full["w_o"])
    fg, fu = carry(_ffn_up_fwd, x1, row(small["g_ffn"]), full["w_ffn_gate"], full["w_ffn_up"],
                   jobs=[gather(["w_ffn_down"])])
    x2 = carry(_ffn_down_fwd, fg, fu, x1, full["w_ffn_down"], jobs=[gather(["w_ple_gate", "w_ple"])])

    dx2, loss_blk, gw["w_ple_gate"], gw["w_ple"], vec_tail = _tail_fwd_bwd(
        x2, p, target, row(small["g_ple_gate"]), full["w_ple_gate"], row(small["b_ple_gate"]), full["w_ple"],
        row(small["g_ple"]), row(small["g_final"]))
    dfg, dfu, act = carry(_ffn_bwd_a, dx2, fg, fu, full["w_ffn_down"], jobs=[swap(["w_ple_gate", "w_ple"])])
    tn_d = min(d, 512)
    gw["w_ffn_down"] = _matmul_tn(act, dx2, tn_d, "dw_ffn_down", BF16)
    add_pairs(["w_ple_gate", "w_ple"])
    dx1, h2, vec_ffn = carry(_ffn_bwd_b, dfg, dfu, x1, dx2, row(small["g_ffn"]), full["w_ffn_gate"], full["w_ffn_up"],
                             jobs=[chips(["w_ple_gate", "w_ple"]), swap(["w_ffn_down"])])
    gw["w_ffn_gate"] = _matmul_tn(dfg, h2, tn_d, "dw_ffn_gate", BF16)
    gw["w_ffn_up"] = _matmul_tn(dfu, h2, tn_d, "dw_ffn_up", BF16)
    add_pairs(["w_ffn_down"])
    dza, dzb, dya, dyb, gw["w_o"], gw["w_a_out"], gw["w_b_out"] = carry(
        _merge_bwd, dx1, merged, ma, mb, za, zb, y_a, h, full["w_o"], full["w_a_out"], full["w_b_out"],
        jobs=[chips(["w_ffn_down"]), swap(["w_ffn_gate", "w_ffn_up"])])
    add_pairs(["w_ffn_gate", "w_ffn_up"])
    du_b, dw_r, dw_i, vec_lru = carry(
        _lru_bwd, dyb, xc, hprev, u_b, conv_w, wr_blk, row(small["b_r"]), wi_blk, row(small["b_i"]),
        row(small["lru_lambda"]), hd, jobs=[chips(["w_ffn_gate", "w_ffn_up"]), swap(["w_o", "w_a_out", "w_b_out"])])
    add_pairs(["w_o", "w_a_out", "w_b_out"])
    du_a, lam_r, lam_i, dy16, gw["w_glu"], vec_sa, vec_gn = carry(
        _s5_bwd, dya, y, sr, si, u_a, full["w_glu"], row(small["b_glu"]), cre_blk, cimn_blk, bbr_blk, bbi_blk, ar_row,
        ai_row, d_row, jobs=[chips(["w_o", "w_a_out", "w_b_out"])])
    smalls = {"vec_tail": vec_tail, "vec_ffn": vec_ffn, "vec_lru": vec_lru, "vec_sa": vec_sa, "vec_gn": vec_gn,
              "dw_r": dw_r, "dw_i": dw_i, "loss": loss_blk}
    everyones = {}

    def gather_smalls(keys):
        return (_Gather([smalls[k] for k in keys]), keys, everyones) if distributed else None

    smalls["dbb_re"], smalls["dbb_im"], smalls["dc_re"], smalls["dc_imn"] = carry(
        _s5_param_grads, lam_r, lam_i, sr, si, u_a, dy16, pch, n, jobs=[swap(["w_glu"]), gather_smalls(list(smalls))])
    add_pairs(["w_glu"])
    grad_x, h0, dz16, smalls["vec_mix"], smalls["vec_bin"] = carry(
        _inproj_bwd, [du_a, du_b, dza, dzb], x, dx1, row(small["g_mix"]), full["w_in"],
        jobs=[chips(["w_glu"]), gather_smalls(["dbb_re", "dbb_im", "dc_re", "dc_imn"])])
    shapes = {k: a.shape for k, a in smalls.items()}
    smalls.update(everyones)
    gw["w_in"] = _matmul_tn(dz16, h0, tn_d, "dw_in", BF16)
    if distributed:
        got["w_in"] = gw["w_in"]
        gw = got
    return grad_x, gw, smalls, shapes


def _disc_inputs(small):
    g, n, pch = small["s5_b_re"].shape
    srcs = (small["lam_re"], small["lam_im"], jnp.repeat(small["log_dt"], n))
    return {"rows": [a.reshape(1, g * n) for a in srcs], "cols": [a.reshape(g * n, 1) for a in srcs],
            "b_re": small["s5_b_re"].reshape(g * n, pch), "b_im": small["s5_b_im"].reshape(g * n, pch)}


def kernel(x, p, g_mix, w_in, b_in, lam_re, lam_im, log_dt, s5_b_re, s5_b_im, s5_c_re, s5_c_im, s5_d, w_glu, b_glu, conv_w, conv_b, w_r, b_r, w_i, b_i, lru_lambda, w_a_out, w_b_out, w_o, g_ffn, w_ffn_gate, w_ffn_up, w_ffn_down, g_ple_gate, w_ple_gate, b_ple_gate, w_ple, g_ple, g_final, loss_target, m_g_mix, m_w_in, m_b_in, m_lam_re, m_lam_im, m_log_dt, m_s5_b_re, m_s5_b_im, m_s5_c_re, m_s5_c_im, m_s5_d, m_w_glu, m_b_glu, m_conv_w, m_conv_b, m_w_r, m_b_r, m_w_i, m_b_i, m_lru_lambda, m_w_a_out, m_w_b_out, m_w_o, m_g_ffn, m_w_ffn_gate, m_w_ffn_up, m_w_ffn_down, m_g_ple_gate, m_w_ple_gate, m_b_ple_gate, m_w_ple, m_g_ple, m_g_final, v_g_mix, v_w_in, v_b_in, v_lam_re, v_lam_im, v_log_dt, v_s5_b_re, v_s5_b_im, v_s5_c_re, v_s5_c_im, v_s5_d, v_w_glu, v_b_glu, v_conv_w, v_conv_b, v_w_r, v_b_r, v_w_i, v_b_i, v_lru_lambda, v_w_a_out, v_w_b_out, v_w_o, v_g_ffn, v_w_ffn_gate, v_w_ffn_up, v_w_ffn_down, v_g_ple_gate, v_w_ple_gate, v_b_ple_gate, v_w_ple, v_g_ple, v_g_final):
    given = dict(locals())
    wts = {k: given[k] for k in WEIGHTS}
    moms = {k: given["m_" + k] for k in WEIGHTS}
    vels = {k: given["v_" + k] for k in WEIGHTS}

    def drop_depth(k, a):
        return a if k == "g_final" else a[0]

    small = {k: drop_depth(k, wts[k]) for k in SMALL}
    shard = {k: wts[k][0] for k in SHARDED}
    names = list(SHARDED)

    def wire(k):
        if k == "conv_w":
            return shard[k]
        return (shard[k].T if k in TRANSPOSED else shard[k]).astype(BF16)

    disc = _disc_inputs(small)
    grad_x, parts, smalls, shapes = _local_step(x[0], p[0, 0], loss_target[0], {k: wire(k) for k in names}, small, disc)

    late = ["vec_mix", "vec_bin"]
    received = _alone(_Both(_PairSwap([parts["w_in"]]), _Gather([smalls[k] for k in late])), "swap_last")
    (pair_in,) = _pair_add([parts["w_in"]], received[:1], "pair_add_w_in")
    smalls.update(zip(late, received[1:]))

    g_small, (parts["w_in"],) = _small_reduce(smalls, shapes, *disc["cols"], disc["b_re"], disc["b_im"],
                                              small["s5_b_re"].shape[0], comm=_ChipExchange([pair_in]))
    loss = g_small.pop("loss")[0, 0]
    cols = shard["conv_w"].shape[1]
    mine = _index((lax.axis_index("x"), lax.axis_index("y"), lax.axis_index("c")))
    parts["conv_w"] = lax.dynamic_slice_in_dim(g_small.pop("conv_w"), mine * cols, cols, axis=1)[None]
    natural = lambda k, a: a.reshape((1, -1) if k == "g_final" else wts[k].shape)
    small_g = [natural(k, g_small[k]) for k in SMALL]
    deltas, new_m, new_v = _adamw_small([natural(k, wts[k]) for k in SMALL], small_g,
                                        [natural(k, moms[k]) for k in SMALL], [natural(k, vels[k]) for k in SMALL])
    small_out = [dict(zip(SMALL, [a.reshape(wts[k].shape) for k, a in zip(SMALL, slot)]))
                 for slot in (small_g, deltas, new_m, new_v)]

    big_out = {}
    for k in names:
        big_out[k] = _adamw(parts[k], shard[k], moms[k][0], vels[k][0], "adamw_" + k, transposed=k in TRANSPOSED)

    outs = [loss, grad_x[None]]
    for slot in range(4):
        for k in WEIGHTS:
            if k in SHARDED:
                outs.append(big_out[k][slot][None])
            else:
                outs.append(small_out[slot][k])
    return tuple(outs)
```

```python
import math

import jax
import jax.numpy as jnp
from jax import lax
from jax.experimental import pallas as pl
from jax.experimental.pallas import tpu as pltpu

F32 = jnp.float32
BF16 = jnp.bfloat16

EPS = 1e-6
LRU_C = 8.0
CONV_WIDTH = 4
ADAM_LR = 0.001
ADAM_B1 = 0.9
ADAM_B2 = 0.999
ADAM_EPS = 1e-08
ADAM_WD = 0.01
ADAM_STEP = 10

N_DEV = 8
MESH = pl.DeviceIdType.MESH
SUBLANES = 8
LANES = 128
VMEM_LIMIT = 56 * 1024 * 1024
TOKEN_TILE = 256
TIME_CHUNK = 256


def _dot(a, b):
    return jnp.dot(a.astype(BF16), b.astype(BF16), preferred_element_type=F32)


def _dot_nt(a, b):
    return lax.dot_general(a.astype(BF16), b.astype(BF16), (((1,), (1,)), ((), ())), preferred_element_type=F32)


def _dot_tn(a, b):
    return lax.dot_general(a.astype(BF16), b.astype(BF16), (((0,), (0,)), ((), ())), preferred_element_type=F32)


def _sigmoid(x):
    return jax.nn.sigmoid(x)


def _rms_stats(x):
    r = lax.rsqrt(jnp.mean(x * x, axis=-1, keepdims=True) + EPS)
    return x * r, r


def _rms_bwd(dy, xhat, r, g):
    dxn = dy * g
    dx = r * (dxn - xhat * jnp.mean(dxn * xhat, axis=-1, keepdims=True))
    return dx, dy * xhat


def _rowsum(v):
    return jnp.sum(v, axis=0, keepdims=True)


def _expm1(x):
    u = jnp.exp(x)
    um1 = u - 1.0
    safe = jnp.where(um1 == 0.0, 1.0, jnp.log(u))
    return jnp.where(um1 == 0.0, x, um1 * x / safe)


def _softplus(x):
    e = jnp.exp(-jnp.abs(x))
    u = 1.0 + e
    um1 = u - 1.0
    safe = jnp.where(um1 == 0.0, 1.0, um1)
    log1p_e = jnp.where(um1 == 0.0, e, jnp.log(u) * e / safe)
    return jnp.maximum(x, 0.0) + log1p_e


_GELU_K = math.sqrt(2.0 / math.pi)
_GELU_C = 0.044715


def _gelu(x):
    return 0.5 * x * (1.0 + jnp.tanh(_GELU_K * (x + _GELU_C * x * x * x)))


def _gelu_grad(x):
    th = jnp.tanh(_GELU_K * (x + _GELU_C * x * x * x))
    return 0.5 * (1.0 + th) + 0.5 * x * (1.0 - th * th) * _GELU_K * (1.0 + 3.0 * _GELU_C * x * x)


def _params(*sem):
    return pltpu.CompilerParams(dimension_semantics=sem, vmem_limit_bytes=VMEM_LIMIT)


def _rows(tm, n):
    return pl.BlockSpec((tm, n), lambda i: (i, 0))


def _rows_rev(tm, n, steps):
    return pl.BlockSpec((tm, n), lambda i: (steps - 1 - i, 0))


def _whole(shape):
    nd = len(shape)
    return pl.BlockSpec(shape, lambda i: (0,) * nd, pipeline_mode=pl.Buffered(1))


def _acc(shape):
    nd = len(shape)
    return pl.BlockSpec(shape, lambda i: (0,) * nd)


def _zero_on_first(*refs):
    @pl.when(pl.program_id(0) == 0)
    def _():
        for r in refs:
            r[...] = jnp.zeros_like(r)


def _inproj_fwd(x, g_mix, w_in_t, b_in, widths, comm=None):
    t, d = x.shape
    n = w_in_t.shape[0]
    tm = min(TOKEN_TILE, t)
    offs = [sum(widths[:i]) for i in range(len(widths) + 1)]

    def body(x_ref, g_ref, w_ref, b_ref, *outs):
        xhat, _ = _rms_stats(x_ref[...])
        h = (xhat * g_ref[...]).astype(BF16)
        for k, o_ref in enumerate(outs):
            lo, hi = offs[k], offs[k + 1]
            o_ref[...] = _dot_nt(h, w_ref[lo:hi, :]) + b_ref[:, lo:hi]

    return _run(
        body, comm, name="inproj_fwd", grid=(t // tm,),
        out_shape=[jax.ShapeDtypeStruct((t, w), F32) for w in widths],
        in_specs=[_rows(tm, d), _whole((1, d)), _whole((n, d)), _whole((1, n))],
        out_specs=[_rows(tm, w) for w in widths],
        semantics="parallel",
    )(x, g_mix, w_in_t, b_in)


def _s5_fwd(u, bbr_blk, bbi_blk, ar, ai, cre_blk, cimn_blk, d_skip, w_glu, b_glu, comm=None):
    t, sa = u.shape
    gn = bbr_blk.shape[0]
    tc = min(TIME_CHUNK, t)

    def body(u_ref, bbr_ref, bbi_ref, ar_ref, ai_ref, cre_ref, cim_ref, d_ref, wg_ref, bg_ref,
             sr_ref, si_ref, y_ref, ya_ref, cr_s, ci_s):
        _zero_on_first(cr_s, ci_s)
        uv = u_ref[...]
        ub = uv.astype(BF16)
        sr_ref[...] = _dot_nt(ub, bbr_ref[...])
        si_ref[...] = _dot_nt(ub, bbi_ref[...])
        a_r = ar_ref[...]
        a_i = ai_ref[...]

        def step(row, carry):
            c_r, c_i = carry
            at = pl.ds(row, 1)
            n_r = a_r * c_r - a_i * c_i + sr_ref[at, :]
            n_i = a_r * c_i + a_i * c_r + si_ref[at, :]
            sr_ref[at, :] = n_r
            si_ref[at, :] = n_i
            return n_r, n_i

        c_r, c_i = lax.fori_loop(0, tc, step, (cr_s[0:1, :], ci_s[0:1, :]), unroll=8)
        cr_s[0:1, :] = c_r
        ci_s[0:1, :] = c_i
        y = _dot_nt(sr_ref[...], cre_ref[...]) + _dot_nt(si_ref[...], cim_ref[...]) + d_ref[...] * uv
        y_ref[...] = y
        zz = _gelu(y)
        q = _dot(zz, wg_ref[...]) + bg_ref[...]
        ya_ref[...] = zz * _sigmoid(q)

    return _run(
        body, comm, name="s5_fwd", grid=(t // tc,),
        out_shape=[jax.ShapeDtypeStruct((t, gn), F32), jax.ShapeDtypeStruct((t, gn), F32),
                   jax.ShapeDtypeStruct((t, sa), F32), jax.ShapeDtypeStruct((t, sa), F32)],
        in_specs=[_rows(tc, sa), _whole((gn, sa)), _whole((gn, sa)), _whole((1, gn)), _whole((1, gn)),
                  _whole((sa, gn)), _whole((sa, gn)), _whole((1, sa)), _whole((sa, sa)), _whole((1, sa))],
        out_specs=[_rows(tc, gn), _rows(tc, gn), _rows(tc, sa), _rows(tc, sa)],
        scratch_shapes=[pltpu.VMEM((SUBLANES, gn), F32), pltpu.VMEM((SUBLANES, gn), F32)],
        semantics="arbitrary",
    )(u, bbr_blk, bbi_blk, ar, ai, cre_blk, cimn_blk, d_skip, w_glu, b_glu)


def _lru_gates(xc, wr_ref, br_ref, wi_ref, bi_ref, lam_ref):
    r = _sigmoid(_dot(xc, wr_ref[...]) + br_ref[...])
    ig = _sigmoid(_dot(xc, wi_ref[...]) + bi_ref[...])
    sp = _softplus(-lam_ref[...])
    log_a = (-LRU_C * r) * sp
    return r, ig, sp, log_a


def _lru_fwd(u, conv_w, conv_b, wr_blk, b_r, wi_blk, b_i, lru_lambda, comm=None):
    t, w = u.shape
    tc = min(TIME_CHUNK, t)
    halo = SUBLANES

    def body(u_ref, cw_ref, cb_ref, wr_ref, br_ref, wi_ref, bi_ref, lam_ref,
             xc_ref, h_ref, hp_ref, ext_s, a_s, carry_s):
        @pl.when(pl.program_id(0) == 0)
        def _():
            ext_s[0:halo, :] = jnp.zeros((halo, w), F32)
            carry_s[...] = jnp.zeros_like(carry_s)

        ext_s[halo:halo + tc, :] = u_ref[...]
        xc = cb_ref[...]
        for k in range(CONV_WIDTH):
            off = halo - (CONV_WIDTH - 1) + k
            xc = xc + cw_ref[k:k + 1, :] * ext_s[off:off + tc, :]
        ext_s[0:halo, :] = ext_s[tc:tc + halo, :]
        xc_ref[...] = xc
        r, ig, sp, log_a = _lru_gates(xc, wr_ref, br_ref, wi_ref, bi_ref, lam_ref)
        a_s[...] = jnp.exp(log_a)
        h_ref[...] = jnp.sqrt(-_expm1(2.0 * log_a)) * ig * xc

        def step(row, carry):
            at = pl.ds(row, 1)
            hp_ref[at, :] = carry
            nxt = a_s[at, :] * carry + h_ref[at, :]
            h_ref[at, :] = nxt
            return nxt

        carry_s[0:1, :] = lax.fori_loop(0, tc, step, carry_s[0:1, :], unroll=8)

    return _run(
        body, comm, name="lru_fwd", grid=(t // tc,),
        out_shape=[jax.ShapeDtypeStruct((t, w), F32)] * 3,
        in_specs=[_rows(tc, w), _whole((CONV_WIDTH, w)), _whole((1, w)), _whole((w, w)), _whole((1, w)),
                  _whole((w, w)), _whole((1, w)), _whole((1, w))],
        out_specs=[_rows(tc, w)] * 3,
        scratch_shapes=[pltpu.VMEM((halo + tc, w), F32), pltpu.VMEM((tc, w), F32), pltpu.VMEM((SUBLANES, w), F32)],
        semantics="arbitrary",
    )(u, conv_w, conv_b, wr_blk, b_r, wi_blk, b_i, lru_lambda)


def _merge_fwd(y_a, h, za, zb, x, w_a_out_t, w_b_out, w_o, comm=None):
    t, d = x.shape
    sa, lw = y_a.shape[1], h.shape[1]
    tm = min(TOKEN_TILE, t)

    def body(ya_ref, h_ref, za_ref, zb_ref, x_ref, wa_ref, wb_ref, wo_ref, x1_ref, mg_ref, ma_ref, mb_ref):
        ma = _dot_nt(ya_ref[...], wa_ref[...])
        mb = _dot(h_ref[...], wb_ref[...])
        merged = _sigmoid(za_ref[...]) * ma + _sigmoid(zb_ref[...]) * mb
        ma_ref[...] = ma
        mb_ref[...] = mb
        mg_ref[...] = merged.astype(mg_ref.dtype)
        x1_ref[...] = x_ref[...] + _dot(merged, wo_ref[...])

    return _run(
        body, comm, name="merge_fwd", grid=(t // tm,),
        out_shape=[jax.ShapeDtypeStruct((t, d), F32), jax.ShapeDtypeStruct((t, d), BF16),
                   jax.ShapeDtypeStruct((t, d), F32), jax.ShapeDtypeStruct((t, d), F32)],
        in_specs=[_rows(tm, sa), _rows(tm, lw), _rows(tm, d), _rows(tm, d), _rows(tm, d),
                  _whole((d, sa)), _whole((lw, d)), _whole((d, d))],
        out_specs=[_rows(tm, d)] * 4,
        semantics="parallel",
    )(y_a, h, za, zb, x, w_a_out_t, w_b_out, w_o)


def _ffn_up_fwd(x1, g_ffn, w_gate_t, w_up_t, comm=None):
    t, d = x1.shape
    f = w_gate_t.shape[0]
    tm = min(TOKEN_TILE, t)

    def body(x_ref, g_ref, wg_ref, wu_ref, fg_ref, fu_ref):
        xhat, _ = _rms_stats(x_ref[...])
        h2 = (xhat * g_ref[...]).astype(BF16)
        fg_ref[...] = _dot_nt(h2, wg_ref[...])
        fu_ref[...] = _dot_nt(h2, wu_ref[...])

    return _run(
        body, comm, name="ffn_up_fwd", grid=(t // tm,),
        out_shape=[jax.ShapeDtypeStruct((t, f), F32)] * 2,
        in_specs=[_rows(tm, d), _whole((1, d)), _whole((f, d)), _whole((f, d))],
        out_specs=[_rows(tm, f)] * 2,
        semantics="parallel",
    )(x1, g_ffn, w_gate_t, w_up_t)


def _ffn_down_fwd(fg, fu, x1, w_down, comm=None):
    t, d = x1.shape
    f = fg.shape[1]
    tm = min(TOKEN_TILE, t)

    def body(fg_ref, fu_ref, x_ref, wd_ref, x2_ref):
        fgv = fg_ref[...]
        act = fgv * _sigmoid(fgv) * fu_ref[...]
        x2_ref[...] = x_ref[...] + _dot(act, wd_ref[...])

    return _run(
        body, comm, name="ffn_down_fwd", grid=(t // tm,),
        out_shape=jax.ShapeDtypeStruct((t, d), F32),
        in_specs=[_rows(tm, f), _rows(tm, f), _rows(tm, d), _whole((f, d))],
        out_specs=_rows(tm, d),
        semantics="parallel",
    )(fg, fu, x1, w_down)


def _store_on_last(pairs):
    @pl.when(pl.program_id(0) == pl.num_programs(0) - 1)
    def _():
        for acc, out in pairs:
            out[...] = acc[...].astype(out.dtype)


def _tail_fwd_bwd(x2, p, target, g_pg, w_pg, b_pg, w_ple_t, g_ple, g_final):
    t, d = x2.shape
    pd = p.shape[1]
    tm = min(TOKEN_TILE, t)

    def body(x2_ref, p_ref, tg_ref, gpg_ref, wpg_ref, bpg_ref, wple_ref, gple_ref, gfin_ref,
             dx2_ref, loss_ref, dwpg_out, dwple_out, vec_ref, dwpg_ref, dwple_ref):
        _zero_on_first(loss_ref, dwpg_ref, dwple_ref, vec_ref)
        x2v = x2_ref[...]
        xh2, r2 = _rms_stats(x2v)
        h3 = xh2 * gpg_ref[...]
        gp = _sigmoid(_dot(h3, wpg_ref[...]) + bpg_ref[...])
        pe = _dot_nt(p_ref[...], wple_ref[...])
        peh, r3 = _rms_stats(pe)
        e = peh * gple_ref[...]
        x3 = x2v + gp * e
        xh3, r4 = _rms_stats(x3)
        diff = xh3 * gfin_ref[...] - tg_ref[...]
        loss_ref[...] += 0.5 * jnp.sum(jnp.mean(diff * diff, axis=-1, keepdims=True))
        dy = diff * (1.0 / d)
        dx3, dgfin = _rms_bwd(dy, xh3, r4, gfin_ref[...])
        d_gp = dx3 * e
        d_e = dx3 * gp
        dpe, dgple = _rms_bwd(d_e, peh, r3, gple_ref[...])
        dwple_ref[...] += _dot_tn(dpe, p_ref[...])
        dpre = d_gp * gp * (1.0 - gp)
        dwpg_ref[...] += _dot_tn(h3, dpre)
        dh3 = _dot_nt(dpre, wpg_ref[...])
        dx2n, dgpg = _rms_bwd(dh3, xh2, r2, gpg_ref[...])
        dx2_ref[...] = dx3 + dx2n
        vec_ref[0:1, :] += _rowsum(dpre)
        vec_ref[1:2, :] += _rowsum(dgpg)
        vec_ref[2:3, :] += _rowsum(dgple)
        vec_ref[3:4, :] += _rowsum(dgfin)
        _store_on_last([(dwpg_ref, dwpg_out), (dwple_ref, dwple_out)])

    return pl.pallas_call(
        body, name="tail_fwd_bwd", grid=(t // tm,),
        out_shape=[jax.ShapeDtypeStruct((t, d), F32), jax.ShapeDtypeStruct((SUBLANES, LANES), F32),
                   jax.ShapeDtypeStruct((d, d), BF16), jax.ShapeDtypeStruct((d, pd), BF16),
                   jax.ShapeDtypeStruct((SUBLANES, d), F32)],
        in_specs=[_rows(tm, d), _rows(tm, pd), _rows(tm, d), _whole((1, d)), _whole((d, d)), _whole((1, d)),
                  _whole((d, pd)), _whole((1, d)), _whole((1, d))],
        out_specs=[_rows(tm, d), _acc((SUBLANES, LANES)), _acc((d, d)), _acc((d, pd)), _acc((SUBLANES, d))],
        scratch_shapes=[pltpu.VMEM((d, d), F32), pltpu.VMEM((d, pd), F32)],
        compiler_params=_params("arbitrary"),
    )(x2, p, target, g_pg, w_pg, b_pg, w_ple_t, g_ple, g_final)


def _ffn_bwd_a(dx2, fg, fu, w_down, comm=None):
    t, d = dx2.shape
    f = fg.shape[1]
    tm = min(TOKEN_TILE, t)

    def body(dx_ref, fg_ref, fu_ref, wd_ref, dfg_ref, dfu_ref, act_ref):
        dact = _dot_nt(dx_ref[...], wd_ref[...])
        fgv = fg_ref[...]
        fuv = fu_ref[...]
        sg = _sigmoid(fgv)
        silu = fgv * sg
        dfu_ref[...] = (dact * silu).astype(dfu_ref.dtype)
        dfg_ref[...] = (dact * fuv * (sg * (1.0 + fgv * (1.0 - sg)))).astype(dfg_ref.dtype)
        act_ref[...] = (silu * fuv).astype(act_ref.dtype)

    return _run(
        body, comm, name="ffn_bwd_a", grid=(t // tm,),
        out_shape=[jax.ShapeDtypeStruct((t, f), BF16)] * 3,
        in_specs=[_rows(tm, d), _rows(tm, f), _rows(tm, f), _whole((f, d))],
        out_specs=[_rows(tm, f)] * 3,
        semantics="parallel",
    )(dx2, fg, fu, w_down)


def _ffn_bwd_b(dfg, dfu, x1, dx2, g_ffn, w_gate_t, w_up_t, comm=None):
    t, d = x1.shape
    f = dfg.shape[1]
    tm = min(TOKEN_TILE, t)

    def body(dfg_ref, dfu_ref, x_ref, dx2_ref, g_ref, wg_ref, wu_ref, dx1_ref, h2_ref, vec_ref):
        _zero_on_first(vec_ref)
        dh2 = _dot(dfg_ref[...], wg_ref[...]) + _dot(dfu_ref[...], wu_ref[...])
        xhat, r = _rms_stats(x_ref[...])
        h2_ref[...] = (xhat * g_ref[...]).astype(h2_ref.dtype)
        dxn, dg = _rms_bwd(dh2, xhat, r, g_ref[...])
        dx1_ref[...] = dx2_ref[...] + dxn
        vec_ref[0:1, :] += _rowsum(dg)

    return _run(
        body, comm, name="ffn_bwd_b", grid=(t // tm,),
        out_shape=[jax.ShapeDtypeStruct((t, d), F32), jax.ShapeDtypeStruct((t, d), BF16),
                   jax.ShapeDtypeStruct((SUBLANES, d), F32)],
        in_specs=[_rows(tm, f), _rows(tm, f), _rows(tm, d), _rows(tm, d), _whole((1, d)), _whole((f, d)), _whole((f, d))],
        out_specs=[_rows(tm, d), _rows(tm, d), _acc((SUBLANES, d))],
        semantics="arbitrary",
    )(dfg, dfu, x1, dx2, g_ffn, w_gate_t, w_up_t)


def _matmul_tn(a, b, tn, name, dtype=F32, comm=None):
    t, k = a.shape
    n = b.shape[1]

    def body(a_ref, b_ref, o_ref):
        o_ref[...] = _dot_tn(a_ref[...], b_ref[...]).astype(o_ref.dtype)

    return _run(
        body, comm, name=name, grid=(n // tn,),
        out_shape=jax.ShapeDtypeStruct((k, n), dtype),
        in_specs=[_whole((t, k)), pl.BlockSpec((t, tn), lambda j: (0, j))],
        out_specs=pl.BlockSpec((k, tn), lambda j: (0, j)),
        semantics="parallel",
    )(a, b)


def _merge_bwd(dx1, merged, ma, mb, za, zb, y_a, h, w_o, w_a_out_t, w_b_out, comm=None):
    t, d = dx1.shape
    sa, lw = y_a.shape[1], h.shape[1]
    tm = min(TOKEN_TILE, t)

    def body(dx1_ref, mg_ref, ma_ref, mb_ref, za_ref, zb_ref, ya_ref, h_ref, wo_ref, wa_ref, wb_ref,
             dza_ref, dzb_ref, dya_ref, dyb_ref, dwo_out, dwa_out, dwb_out, dwo_ref, dwa_ref, dwb_ref):
        _zero_on_first(dwo_ref, dwa_ref, dwb_ref)
        dx1v = dx1_ref[...].astype(BF16)
        dmg = _dot_nt(dx1v, wo_ref[...])
        ga = _sigmoid(za_ref[...])
        gb = _sigmoid(zb_ref[...])
        dza_ref[...] = dmg * ma_ref[...] * ga * (1.0 - ga)
        dzb_ref[...] = dmg * mb_ref[...] * gb * (1.0 - gb)
        dma = (dmg * ga).astype(BF16)
        dmb = (dmg * gb).astype(BF16)
        dya_ref[...] = _dot(dma, wa_ref[...])
        dyb_ref[...] = _dot_nt(dmb, wb_ref[...])
        dwo_ref[...] += _dot_tn(mg_ref[...], dx1v)
        dwa_ref[...] += _dot_tn(dma, ya_ref[...])
        dwb_ref[...] += _dot_tn(h_ref[...], dmb)
        _store_on_last([(dwo_ref, dwo_out), (dwa_ref, dwa_out), (dwb_ref, dwb_out)])

    return _run(
        body, comm, name="merge_bwd", grid=(t // tm,),
        out_shape=[jax.ShapeDtypeStruct((t, d), F32), jax.ShapeDtypeStruct((t, d), F32),
                   jax.ShapeDtypeStruct((t, sa), F32), jax.ShapeDtypeStruct((t, lw), F32),
                   jax.ShapeDtypeStruct((d, d), BF16), jax.ShapeDtypeStruct((d, sa), BF16),
                   jax.ShapeDtypeStruct((lw, d), BF16)],
        in_specs=[_rows(tm, d), _rows(tm, d), _rows(tm, d), _rows(tm, d), _rows(tm, d), _rows(tm, d),
                  _rows(tm, sa), _rows(tm, lw), _whole((d, d)), _whole((d, sa)), _whole((lw, d))],
        out_specs=[_rows(tm, d), _rows(tm, d), _rows(tm, sa), _rows(tm, lw), _acc((d, d)), _acc((d, sa)), _acc((lw, d))],
        scratch_shapes=[pltpu.VMEM((d, d), F32), pltpu.VMEM((d, sa), F32), pltpu.VMEM((lw, d), F32)],
        semantics="arbitrary",
    )(dx1, merged, ma, mb, za, zb, y_a, h, w_o, w_a_out_t, w_b_out)


def _fold_diag_blocks(dense, row_group, col_group, row0=0, col0=0):
    r, c = dense.shape
    rows = lax.broadcasted_iota(jnp.int32, (r, c), 0) + row0
    cols = lax.broadcasted_iota(jnp.int32, (r, c), 1) + col0
    kept = jnp.where(rows // row_group == cols // col_group, dense, 0.0)
    pick = (lax.broadcasted_iota(jnp.int32, (row_group, r), 0)
            == lax.broadcasted_iota(jnp.int32, (row_group, r), 1) % row_group).astype(F32)
    return jnp.dot(pick, kept, preferred_element_type=F32, precision=lax.Precision.HIGHEST)


def _lru_bwd(dh, xc, hprev, u, conv_w, wr_blk, b_r, wi_blk, b_i, lru_lambda, head_dim, comm=None):
    t, w = dh.shape
    tc = min(TIME_CHUNK, t)
    steps = t // tc
    halo = SUBLANES
    sub_per_chunk = tc // halo

    def body(dh_ref, xc_ref, hp_ref, u_ref, uh_ref, cw_ref, wr_ref, br_ref, wi_ref, bi_ref, lam_ref,
             du_ref, dwr_out, dwi_out, vec_ref, lam_s, a_s, dxc_s, uext_s, carry_s, dwr_ref, dwi_ref):
        chunk = steps - 1 - pl.program_id(0)

        @pl.when(pl.program_id(0) == 0)
        def _():
            carry_s[...] = jnp.zeros_like(carry_s)
            dxc_s[tc:tc + halo, :] = jnp.zeros((halo, w), F32)
            dwr_ref[...] = jnp.zeros_like(dwr_ref)
            dwi_ref[...] = jnp.zeros_like(dwi_ref)
            vec_ref[...] = jnp.zeros_like(vec_ref)

        xc = xc_ref[...]
        r, ig, sp, log_a = _lru_gates(xc, wr_ref, br_ref, wi_ref, bi_ref, lam_ref)
        a = jnp.exp(log_a)
        a_s[...] = a

        def step(i, q):
            at = pl.ds(tc - 1 - i, 1)
            lam_row = dh_ref[at, :] + q
            lam_s[at, :] = lam_row
            return a_s[at, :] * lam_row

        carry_s[0:1, :] = lax.fori_loop(0, tc, step, carry_s[0:1, :], unroll=8)
        lam = lam_s[...]
        mult = jnp.sqrt(-_expm1(2.0 * log_a))
        d_log_a = lam * hp_ref[...] * a - (lam * ig * xc) * (a * a) / mult
        d_ig = lam * mult * xc
        dpre_r = (d_log_a * (-LRU_C * sp)) * r * (1.0 - r)
        dpre_i = d_ig * ig * (1.0 - ig)
        dxc = lam * mult * ig + _dot_nt(dpre_r, wr_ref[...]) + _dot_nt(dpre_i, wi_ref[...])
        dwr_ref[...] += _dot_tn(dpre_r, xc)
        dwi_ref[...] += _dot_tn(dpre_i, xc)
        vec_ref[0:1, :] += _rowsum(dxc)
        vec_ref[1:2, :] += _rowsum(dpre_r)
        vec_ref[2:3, :] += _rowsum(dpre_i)
        vec_ref[3:4, :] += _rowsum(d_log_a * (-LRU_C * r)) * (-_sigmoid(-lam_ref[...]))
        dxc_s[0:tc, :] = dxc
        du = cw_ref[CONV_WIDTH - 1:CONV_WIDTH, :] * dxc
        for k in range(CONV_WIDTH - 1):
            off = CONV_WIDTH - 1 - k
            du = du + cw_ref[k:k + 1, :] * dxc_s[off:off + tc, :]
        du_ref[...] = du
        dxc_s[tc:tc + halo, :] = dxc_s[0:halo, :]
        uext_s[0:halo, :] = jnp.where(chunk > 0, uh_ref[...], 0.0)
        uext_s[halo:halo + tc, :] = u_ref[...]
        for k in range(CONV_WIDTH):
            off = halo - (CONV_WIDTH - 1) + k
            vec_ref[4 + k:5 + k, :] += _rowsum(dxc * uext_s[off:off + tc, :])

        @pl.when(pl.program_id(0) == steps - 1)
        def _():
            dwr_out[...] = _fold_diag_blocks(dwr_ref[...], head_dim, head_dim)
            dwi_out[...] = _fold_diag_blocks(dwi_ref[...], head_dim, head_dim)

    halo_spec = pl.BlockSpec((halo, w), lambda i: (jnp.maximum((steps - 1 - i) * sub_per_chunk - 1, 0), 0))
    return _run(
        body, comm, name="lru_bwd", grid=(steps,),
        out_shape=[jax.ShapeDtypeStruct((t, w), F32), jax.ShapeDtypeStruct((head_dim, w), F32),
                   jax.ShapeDtypeStruct((head_dim, w), F32), jax.ShapeDtypeStruct((SUBLANES, w), F32)],
        in_specs=[_rows_rev(tc, w, steps)] * 4 + [halo_spec, _whole((CONV_WIDTH, w)), _whole((w, w)), _whole((1, w)),
                                                  _whole((w, w)), _whole((1, w)), _whole((1, w))],
        out_specs=[_rows_rev(tc, w, steps), _acc((head_dim, w)), _acc((head_dim, w)), _acc((SUBLANES, w))],
        scratch_shapes=[pltpu.VMEM((tc, w), F32), pltpu.VMEM((tc, w), F32), pltpu.VMEM((tc + halo, w), F32),
                        pltpu.VMEM((halo + tc, w), F32), pltpu.VMEM((SUBLANES, w), F32), pltpu.VMEM((w, w), F32),
                        pltpu.VMEM((w, w), F32)],
        semantics="arbitrary",
    )(dh, xc, hprev, u, u, conv_w, wr_blk, b_r, wi_blk, b_i, lru_lambda)


def _s5_bwd(dya, y, sr, si, u, w_glu, b_glu, cre_blk, cimn_blk, bbr_blk, bbi_blk, ar, ai, d_skip, comm=None):
    t, sa = dya.shape
    gn = sr.shape[1]
    tc = min(TIME_CHUNK, t)
    steps = t // tc
    halo = SUBLANES

    def body(dya_ref, y_ref, sr_ref, si_ref, u_ref, wg_ref, bg_ref, cre_ref, cim_ref, bbr_ref, bbi_ref,
             ar_ref, ai_ref, d_ref, du_ref, lr_ref, li_ref, dy_ref, dwg_out, vsa_ref, vgn_ref, gr_s, gi_s, cr_s, ci_s,
             dwg_ref):
        @pl.when(pl.program_id(0) == 0)
        def _():
            cr_s[...] = jnp.zeros_like(cr_s)
            ci_s[...] = jnp.zeros_like(ci_s)
            gr_s[tc:tc + halo, :] = jnp.zeros((halo, gn), F32)
            gi_s[tc:tc + halo, :] = jnp.zeros((halo, gn), F32)
            dwg_ref[...] = jnp.zeros_like(dwg_ref)
            vsa_ref[...] = jnp.zeros_like(vsa_ref)
            vgn_ref[...] = jnp.zeros_like(vgn_ref)

        yv = y_ref[...]
        uv = u_ref[...]
        zz = _gelu(yv)
        sg = _sigmoid(_dot(zz, wg_ref[...]) + bg_ref[...])
        dyav = dya_ref[...]
        dq = dyav * zz * sg * (1.0 - sg)
        dzz = dyav * sg + _dot_nt(dq, wg_ref[...])
        dwg_ref[...] += _dot_tn(zz, dq)
        dy = dzz * _gelu_grad(yv)
        dyb = dy.astype(BF16)
        dy_ref[...] = dyb.astype(dy_ref.dtype)
        vsa_ref[0:1, :] += _rowsum(dq)
        vsa_ref[1:2, :] += _rowsum(dy * uv)
        gr_s[0:tc, :] = _dot(dyb, cre_ref[...])
        gi_s[0:tc, :] = _dot(dyb, cim_ref[...])
        a_r = ar_ref[...]
        a_i = ai_ref[...]

        def step(i, carry):
            l_r, l_i = carry
            at = pl.ds(tc - 1 - i, 1)
            n_r = gr_s[at, :] + a_r * l_r + a_i * l_i
            n_i = gi_s[at, :] + a_r * l_i - a_i * l_r
            gr_s[at, :] = n_r
            gi_s[at, :] = n_i
            return n_r, n_i

        l_r, l_i = lax.fori_loop(0, tc, step, (cr_s[0:1, :], ci_s[0:1, :]), unroll=8)
        cr_s[0:1, :] = l_r
        ci_s[0:1, :] = l_i
        nxt_r = gr_s[1:tc + 1, :]
        nxt_i = gi_s[1:tc + 1, :]
        srv = sr_ref[...]
        siv = si_ref[...]
        vgn_ref[0:1, :] += _rowsum(nxt_r * srv + nxt_i * siv)
        vgn_ref[1:2, :] += _rowsum(nxt_i * srv - nxt_r * siv)
        lam_r = gr_s[0:tc, :]
        lam_i = gi_s[0:tc, :]
        gr_s[tc:tc + halo, :] = gr_s[0:halo, :]
        gi_s[tc:tc + halo, :] = gi_s[0:halo, :]
        lrb = lam_r.astype(BF16)
        lib = lam_i.astype(BF16)
        lr_ref[...] = lrb.astype(lr_ref.dtype)
        li_ref[...] = lib.astype(li_ref.dtype)
        du_ref[...] = _dot(lrb, bbr_ref[...]) + _dot(lib, bbi_ref[...]) + dy * d_ref[...]
        _store_on_last([(dwg_ref, dwg_out)])

    return _run(
        body, comm, name="s5_bwd", grid=(steps,),
        out_shape=[jax.ShapeDtypeStruct((t, sa), F32), jax.ShapeDtypeStruct((t, gn), BF16),
                   jax.ShapeDtypeStruct((t, gn), BF16), jax.ShapeDtypeStruct((t, sa), BF16),
                   jax.ShapeDtypeStruct((sa, sa), BF16), jax.ShapeDtypeStruct((SUBLANES, sa), F32),
                   jax.ShapeDtypeStruct((SUBLANES, gn), F32)],
        in_specs=[_rows_rev(tc, sa, steps), _rows_rev(tc, sa, steps), _rows_rev(tc, gn, steps), _rows_rev(tc, gn, steps),
                  _rows_rev(tc, sa, steps), _whole((sa, sa)), _whole((1, sa)), _whole((sa, gn)), _whole((sa, gn)),
                  _whole((gn, sa)), _whole((gn, sa)), _whole((1, gn)), _whole((1, gn)), _whole((1, sa))],
        out_specs=[_rows_rev(tc, sa, steps), _rows_rev(tc, gn, steps), _rows_rev(tc, gn, steps), _rows_rev(tc, sa, steps),
                   _acc((sa, sa)), _acc((SUBLANES, sa)), _acc((SUBLANES, gn))],
        scratch_shapes=[pltpu.VMEM((tc + halo, gn), F32), pltpu.VMEM((tc + halo, gn), F32),
                        pltpu.VMEM((SUBLANES, gn), F32), pltpu.VMEM((SUBLANES, gn), F32), pltpu.VMEM((sa, sa), F32)],
        semantics="arbitrary",
    )(dya, y, sr, si, u, w_glu, b_glu, cre_blk, cimn_blk, bbr_blk, bbi_blk, ar, ai, d_skip)


def _inproj_bwd(dparts, x, dx1, g_mix, w_in_t, comm=None):
    t, d = x.shape
    n = w_in_t.shape[0]
    widths = [p.shape[1] for p in dparts]
    offs = [sum(widths[:i]) for i in range(len(widths) + 1)]
    tm = min(TOKEN_TILE, t)
    np_ = len(dparts)

    def body(*refs):
        dz_refs = refs[:np_]
        x_ref, dx1_ref, g_ref, w_ref, gx_ref, h_ref, dz_ref, vd_ref, vn_ref = refs[np_:]
        _zero_on_first(vd_ref, vn_ref)
        dh = jnp.zeros((tm, d), F32)
        for k, r in enumerate(dz_refs):
            lo, hi = offs[k], offs[k + 1]
            dzk = r[...]
            dh = dh + _dot(dzk, w_ref[lo:hi, :])
            dz_ref[:, lo:hi] = dzk.astype(dz_ref.dtype)
            vn_ref[0:1, lo:hi] += _rowsum(dzk)
        xhat, r0 = _rms_stats(x_ref[...])
        h_ref[...] = (xhat * g_ref[...]).astype(h_ref.dtype)
        dxn, dg = _rms_bwd(dh, xhat, r0, g_ref[...])
        gx_ref[...] = dx1_ref[...] + dxn
        vd_ref[0:1, :] += _rowsum(dg)

    return _run(
        body, comm, name="inproj_bwd", grid=(t // tm,),
        out_shape=[jax.ShapeDtypeStruct((t, d), F32), jax.ShapeDtypeStruct((t, d), BF16),
                   jax.ShapeDtypeStruct((t, n), BF16), jax.ShapeDtypeStruct((SUBLANES, d), F32),
                   jax.ShapeDtypeStruct((SUBLANES, n), F32)],
        in_specs=[_rows(tm, w) for w in widths] + [_rows(tm, d), _rows(tm, d), _whole((1, d)), _whole((n, d))],
        out_specs=[_rows(tm, d), _rows(tm, d), _rows(tm, n), _acc((SUBLANES, d)), _acc((SUBLANES, n))],
        semantics="arbitrary",
    )(*dparts, x, dx1, g_mix, w_in_t)


def _prep(lr_row, li_row, ldt_row, lr_col, li_col, ldt_col, b_re, b_im, c_re, c_im, w_r, w_i, comm=None):
    gn, pch = b_re.shape
    sa, n = c_re.shape
    w, hd = w_r.shape

    def expand(vals, row_group, col_group, width):
        r, k = vals.shape
        tile = (lax.broadcasted_iota(jnp.int32, (k, width), 0) == lax.broadcasted_iota(jnp.int32, (k, width), 1) % k)
        rows = lax.broadcasted_iota(jnp.int32, (r, width), 0) // row_group
        cols = lax.broadcasted_iota(jnp.int32, (r, width), 1) // col_group
        return jnp.where(rows == cols, _dot(vals, tile.astype(BF16)), 0.0)

    def body(lrr, lir, ldr, lrc, lic, ldc, bre, bim, cre, cim, wr, wi,
             ar_o, ai_o, bbr_o, bbi_o, cre_o, cim_o, wr_o, wi_o):
        ar, ai, _, _ = _disc_scalars(lrr[...], lir[...], ldr[...])
        ar_o[...] = ar
        ai_o[...] = ai
        _, _, bbr, bbi = _disc_cols(lrc[...], lic[...], ldc[...], bre[...], bim[...])
        bbr_o[...] = expand(bbr, n, pch, sa).astype(bbr_o.dtype)
        bbi_o[...] = expand(bbi, n, pch, sa).astype(bbi_o.dtype)
        cre_o[...] = expand(cre[...], pch, n, gn).astype(cre_o.dtype)
        cim_o[...] = expand(-cim[...], pch, n, gn).astype(cim_o.dtype)
        wr_o[...] = expand(wr[...], hd, hd, w).astype(wr_o.dtype)
        wi_o[...] = expand(wi[...], hd, hd, w).astype(wi_o.dtype)

    args = (lr_row, li_row, ldt_row, lr_col, li_col, ldt_col, b_re, b_im, c_re, c_im, w_r, w_i)
    out_shape = [jax.ShapeDtypeStruct((1, gn), F32), jax.ShapeDtypeStruct((1, gn), F32),
                 jax.ShapeDtypeStruct((gn, sa), BF16), jax.ShapeDtypeStruct((gn, sa), BF16),
                 jax.ShapeDtypeStruct((sa, gn), BF16), jax.ShapeDtypeStruct((sa, gn), BF16),
                 jax.ShapeDtypeStruct((w, w), BF16), jax.ShapeDtypeStruct((w, w), BF16)]
    return _run(
        body, comm, name="prep", grid=(1,), out_shape=out_shape, in_specs=[_whole(a.shape) for a in args],
        out_specs=[_acc(s.shape) for s in out_shape], semantics="arbitrary",
    )(*args)


def _s5_param_grads(lam_r, lam_i, sr, si, u, dy, pch, n, comm=None):
    t, gn = lam_r.shape
    sa = u.shape[1]
    tb = min(gn, 512)

    def body(lr_ref, li_ref, sr_ref, si_ref, u_ref, dy_ref, dbr_ref, dbi_ref, dcr_ref, dci_ref):
        _zero_on_first(dcr_ref, dci_ref)
        base = pl.program_id(0) * tb
        uv = u_ref[...]
        dyv = dy_ref[...]
        dbr_ref[...] = _fold_diag_blocks(_dot_tn(uv, lr_ref[...]), pch, n, col0=base)
        dbi_ref[...] = _fold_diag_blocks(_dot_tn(uv, li_ref[...]), pch, n, col0=base)
        dcr_ref[...] += _fold_diag_blocks(_dot_tn(sr_ref[...], dyv), n, pch, row0=base)
        dci_ref[...] += _fold_diag_blocks(_dot_tn(si_ref[...], dyv), n, pch, row0=base)

    cols = pl.BlockSpec((t, tb), lambda j: (0, j))
    return _run(
        body, comm, name="s5_param_grads", grid=(gn // tb,),
        out_shape=[jax.ShapeDtypeStruct((pch, gn), F32), jax.ShapeDtypeStruct((pch, gn), F32),
                   jax.ShapeDtypeStruct((n, sa), F32), jax.ShapeDtypeStruct((n, sa), F32)],
        in_specs=[cols, cols, cols, cols, _whole((t, sa)), _whole((t, sa))],
        out_specs=[pl.BlockSpec((pch, tb), lambda j: (0, j)), pl.BlockSpec((pch, tb), lambda j: (0, j)),
                   _acc((n, sa)), _acc((n, sa))],
        semantics="arbitrary",
    )(lam_r, lam_i, sr, si, u, dy)


SMALL_PARTS = ["vec_tail", "vec_ffn", "vec_lru", "vec_mix", "vec_bin", "vec_sa", "vec_gn", "dw_r", "dw_i", "dbb_re",
               "dbb_im", "dc_re", "dc_imn", "loss"]


def _small_reduce(parts, shapes, lr_col, li_col, ldt_col, b_re, b_im, groups, comm=None):
    gn, pch = b_re.shape
    n = gn // groups
    nparts = parts[SMALL_PARTS[0]].size // math.prod(shapes[SMALL_PARTS[0]])
    np_, nout = len(SMALL_PARTS), 24

    def body(*refs):
        ins = refs[:np_]
        lr, li, ldt, bre, bim = refs[np_:np_ + 5]
        outs = refs[np_ + 5:np_ + 5 + nout]
        sums = dict(zip(SMALL_PARTS, refs[np_ + 5 + nout:]))

        @pl.when(pl.program_id(0) == 0)
        def _():
            for k, r in zip(SMALL_PARTS, ins):
                sums[k][...] = r[...]

        @pl.when(pl.program_id(0) > 0)
        def _():
            for k, r in zip(SMALL_PARTS, ins):
                sums[k][...] += r[...]

        @pl.when(pl.program_id(0) == nparts - 1)
        def _():
            finish({k: s[...] for k, s in sums.items()}, lr, li, ldt, bre, bim, *outs)

    def finish(tot, lr, li, ldt, bre, bim, o_loss, o_gmix, o_bin, o_bglu, o_s5d, o_convb, o_br, o_bi, o_lam, o_gffn,
               o_gpg, o_bpg, o_gple, o_gfin, o_wr, o_wi, o_cre, o_cim, o_bre, o_bim, o_lre, o_lim, o_ldt, o_convw):
        o_loss[...] = tot["loss"]
        o_convw[...] = tot["vec_lru"][SUBLANES - CONV_WIDTH:SUBLANES]
        o_bpg[...] = tot["vec_tail"][0:1]
        o_gpg[...] = tot["vec_tail"][1:2]
        o_gple[...] = tot["vec_tail"][2:3]
        o_gfin[...] = tot["vec_tail"][3:4]
        o_gffn[...] = tot["vec_ffn"][0:1]
        o_convb[...] = tot["vec_lru"][0:1]
        o_br[...] = tot["vec_lru"][1:2]
        o_bi[...] = tot["vec_lru"][2:3]
        o_lam[...] = tot["vec_lru"][3:4]
        o_gmix[...] = tot["vec_mix"][0:1]
        o_bin[...] = tot["vec_bin"][0:1]
        o_bglu[...] = tot["vec_sa"][0:1]
        o_s5d[...] = tot["vec_sa"][1:2]
        o_wr[...] = tot["dw_r"].T
        o_wi[...] = tot["dw_i"].T
        o_cre[...] = tot["dc_re"].T
        o_cim[...] = -tot["dc_imn"].T
        d_a = tot["vec_gn"].T
        _, chain = jax.vjp(_disc_cols, lr[...], li[...], ldt[...], bre[...], bim[...])
        d_lr, d_li, d_ldt, d_bre, d_bim = chain((d_a[:, 0:1], d_a[:, 1:2], tot["dbb_re"].T, tot["dbb_im"].T))
        o_lre[...] = d_lr
        o_lim[...] = d_li
        o_bre[...] = d_bre
        o_bim[...] = d_bim
        same = (lax.broadcasted_iota(jnp.int32, (groups, gn), 0)
                == lax.broadcasted_iota(jnp.int32, (groups, gn), 1) // n).astype(F32)
        o_ldt[...] = jnp.dot(same, d_ldt * jnp.ones((1, LANES), F32), preferred_element_type=F32,
                             precision=lax.Precision.HIGHEST)[:, 0:1]

    d = shapes["vec_mix"][1]
    nz = shapes["vec_bin"][1]
    sa = shapes["vec_sa"][1]
    w = shapes["vec_lru"][1]
    row = lambda c: jax.ShapeDtypeStruct((1, c), F32)
    out_shape = [jax.ShapeDtypeStruct(shapes["loss"], F32), row(d), row(nz), row(sa), row(sa), row(w), row(w), row(w),
                 row(w), row(d), row(d), row(d), row(d), row(d),
                 jax.ShapeDtypeStruct(shapes["dw_r"][::-1], F32), jax.ShapeDtypeStruct(shapes["dw_i"][::-1], F32),
                 jax.ShapeDtypeStruct(shapes["dc_re"][::-1], F32), jax.ShapeDtypeStruct(shapes["dc_imn"][::-1], F32),
                 jax.ShapeDtypeStruct((gn, pch), F32), jax.ShapeDtypeStruct((gn, pch), F32),
                 jax.ShapeDtypeStruct((gn, 1), F32), jax.ShapeDtypeStruct((gn, 1), F32),
                 jax.ShapeDtypeStruct((groups, 1), F32), jax.ShapeDtypeStruct((CONV_WIDTH, w), F32)]
    def part_spec(k):
        r, c = shapes[k]
        if parts[k].ndim == 3:
            return pl.BlockSpec((None, r, c), lambda i: (i, 0, 0))
        return pl.BlockSpec((r, c), lambda i: (i, 0))

    outs = _run(
        body, comm, name="small_reduce", grid=(nparts,), out_shape=out_shape,
        in_specs=[part_spec(k) for k in SMALL_PARTS] + [_whole(a.shape) for a in (lr_col, li_col, ldt_col, b_re, b_im)],
        out_specs=[_acc(s.shape) for s in out_shape],
        scratch_shapes=[pltpu.VMEM(shapes[k], F32) for k in SMALL_PARTS],
        semantics="arbitrary",
    )(*[parts[k] for k in SMALL_PARTS], lr_col, li_col, ldt_col, b_re, b_im)
    extra = None
    if comm is not None:
        outs, extra = outs
    names = ["loss", "g_mix", "b_in", "b_glu", "s5_d", "conv_b", "b_r", "b_i", "lru_lambda", "g_ffn", "g_ple_gate",
             "b_ple_gate", "g_ple", "g_final", "w_r", "w_i", "s5_c_re", "s5_c_im", "s5_b_re", "s5_b_im", "lam_re",
             "lam_im", "log_dt", "conv_w"]
    res = dict(zip(names, outs))
    return res if comm is None else (res, extra)


def _adamw_small(ws, gs, ms, vs):
    n = len(ws)

    def body(*refs):
        w_r, g_r, m_r, v_r = (refs[i * n:(i + 1) * n] for i in range(4))
        g_o, d_o, m_o, v_o = (refs[(4 + i) * n:(5 + i) * n] for i in range(4))
        for i in range(n):
            g = g_r[i][...]
            delta, m_new, v_new = _adamw_math(w_r[i][...], g, m_r[i][...], v_r[i][...])
            g_o[i][...] = g
            d_o[i][...] = delta
            m_o[i][...] = m_new
            v_o[i][...] = v_new

    shapes = [jax.ShapeDtypeStruct(a.shape, F32) for a in ws]
    outs = pl.pallas_call(body, name="adamw_small", out_shape=shapes * 4)(*ws, *gs, *ms, *vs)
    return outs[:n], outs[n:2 * n], outs[2 * n:3 * n], outs[3 * n:]


def _adamw_math(w, g, m, v):
    m_new = ADAM_B1 * m + (1.0 - ADAM_B1) * g
    v_new = ADAM_B2 * v + (1.0 - ADAM_B2) * (g * g)
    m_hat = m_new / (1.0 - ADAM_B1 ** ADAM_STEP)
    v_hat = v_new / (1.0 - ADAM_B2 ** ADAM_STEP)
    delta = -ADAM_LR * (m_hat / (jnp.sqrt(v_hat) + ADAM_EPS) + ADAM_WD * w)
    return delta, m_new, v_new


def _row_tile(rows):
    for cand in (256, 128, 64, 32, 16, 8):
        if rows % cand == 0:
            return cand
    return rows


def _adamw(parts, w, m, v, name, transposed=False):
    rows, cols = w.shape
    npart = parts.shape[0]
    tr = _row_tile(rows)
    if transposed:
        parts_spec = pl.BlockSpec((npart, cols, tr), lambda i: (0, 0, i))
    else:
        parts_spec = pl.BlockSpec((npart, tr, cols), lambda i: (0, i, 0))

    def body(p_ref, w_ref, m_ref, v_ref, g_ref, d_ref, mo_ref, vo_ref):
        g = p_ref[0].astype(F32)
        for k in range(1, npart):
            g = g + p_ref[k].astype(F32)
        if transposed:
            g = g.T
        delta, m_new, v_new = _adamw_math(w_ref[...], g, m_ref[...], v_ref[...])
        g_ref[...] = g
        d_ref[...] = delta
        mo_ref[...] = m_new
        vo_ref[...] = v_new

    return pl.pallas_call(
        body, name=name, grid=(rows // tr,),
        out_shape=[jax.ShapeDtypeStruct((rows, cols), F32)] * 4,
        in_specs=[parts_spec] + [_rows(tr, cols)] * 3,
        out_specs=[_rows(tr, cols)] * 4,
        compiler_params=_params("parallel"),
    )(parts, w, m, v)


def _mesh_position():
    return lax.axis_index("x"), lax.axis_index("y"), lax.axis_index("c")


def _flip(pos, rel):
    x, y, c = pos
    return (1 - x if rel & 4 else x, 1 - y if rel & 2 else y, 1 - c if rel & 1 else c)


def _index(pos):
    return 4 * pos[0] + 2 * pos[1] + pos[2]


_ANY = pl.BlockSpec(memory_space=pl.ANY)
FLAT_ROWS = 32


def _dma_sems(n):
    return [pltpu.SemaphoreType.DMA((n, N_DEV - 1)), pltpu.SemaphoreType.DMA((n, N_DEV - 1)), pltpu.SemaphoreType.DMA((n,))]


def _block_of(ref, idx, rows, flat):
    if flat:
        return ref.at[pl.ds(pl.multiple_of(idx * rows, FLAT_ROWS), rows), :]
    return ref.at[idx]


class _Gather:
    chips = (4, 2, 6)

    def __init__(self, shards):
        self.inputs = list(shards)
        self.flat = [s.shape[0] % FLAT_ROWS == 0 for s in shards]
        self.out_shape = [
            jax.ShapeDtypeStruct((N_DEV * s.shape[0], s.shape[1]) if f else (N_DEV,) + s.shape, s.dtype)
            for s, f in zip(shards, self.flat)]
        self.sems = _dma_sems(len(shards))

    def _copy(self, ins, outs, sems, i, k, block, to, own=False):
        dst = _block_of(outs[i], _index(block), self.inputs[i].shape[0], self.flat[i])
        return pltpu.make_async_remote_copy(
            src_ref=ins[i] if own else dst, dst_ref=dst, send_sem=sems[0].at[i, k], recv_sem=sems[1].at[i, k],
            device_id=to, device_id_type=MESH)

    def _local(self, ins, outs, sems, i, me):
        dst = _block_of(outs[i], _index(me), self.inputs[i].shape[0], self.flat[i])
        return pltpu.make_async_copy(ins[i], dst, sems[2].at[i])

    def _first(self, ins, outs, sems, i, me):
        cps = [self._copy(ins, outs, sems, i, 0, me, _flip(me, 1), own=True)]
        cps += [self._copy(ins, outs, sems, i, 1 + j, me, _flip(me, rel), own=True) for j, rel in enumerate(self.chips)]
        return cps

    def _passed(self, ins, outs, sems, i, j, me):
        return self._copy(ins, outs, sems, i, 4 + j, _flip(me, self.chips[j]), _flip(me, 1))

    def before(self, ins, outs, sems):
        n = len(self.inputs)
        me = _mesh_position()

        @pl.when(pl.program_id(0) == 0)
        def _():
            for i in range(n):
                self._local(ins, outs, sems, i, me).start()
                for cp in self._first(ins, outs, sems, i, me):
                    cp.start()

        for j, rel in enumerate(self.chips):
            @pl.when(pl.program_id(0) == jnp.maximum(pl.num_programs(0) - len(self.chips) + j, 0))
            def _(j=j, rel=rel):
                for i in range(n):
                    self._copy(ins, outs, sems, i, 1 + j, _flip(me, rel), me).wait_recv()
                    self._passed(ins, outs, sems, i, j, me).start()

    def after(self, ins, outs, sems):
        n = len(self.inputs)
        me = _mesh_position()
        sibling = _flip(me, 1)

        @pl.when(pl.program_id(0) == pl.num_programs(0) - 1)
        def _():
            for i in range(n):
                self._copy(ins, outs, sems, i, 0, sibling, me).wait_recv()
                for j, rel in enumerate(self.chips):
                    self._copy(ins, outs, sems, i, 4 + j, _flip(sibling, rel), me).wait_recv()
            for i in range(n):
                for cp in self._first(ins, outs, sems, i, me):
                    cp.wait_send()
                for j in range(len(self.chips)):
                    self._passed(ins, outs, sems, i, j, me).wait_send()
                self._local(ins, outs, sems, i, me).wait()


N_CHIPS = N_DEV // 2


def _chip(pos):
    return 2 * pos[0] + pos[1]


class _PairSwap:
    def __init__(self, arrays):
        self.inputs = list(arrays)
        self.rows = [a.shape[0] // N_DEV for a in arrays]
        for r in self.rows:
            assert r % FLAT_ROWS == 0, r
        self.out_shape = [jax.ShapeDtypeStruct((N_CHIPS, r, a.shape[1]), a.dtype) for a, r in zip(arrays, self.rows)]
        n = len(arrays)
        self.sems = [pltpu.SemaphoreType.DMA((n, N_CHIPS)), pltpu.SemaphoreType.DMA((n, N_CHIPS))]

    def _copy(self, ins, outs, sems, i, j, me):
        sibling = _flip(me, 1)
        return pltpu.make_async_remote_copy(
            src_ref=_block_of(ins[i], 2 * j + sibling[2], self.rows[i], True), dst_ref=outs[i].at[j],
            send_sem=sems[0].at[i, j], recv_sem=sems[1].at[i, j], device_id=sibling, device_id_type=MESH)

    def before(self, ins, outs, sems):
        me = _mesh_position()

        @pl.when(pl.program_id(0) == 0)
        def _():
            for i in range(len(self.inputs)):
                for j in range(N_CHIPS):
                    self._copy(ins, outs, sems, i, j, me).start()

    def after(self, ins, outs, sems):
        me = _mesh_position()

        @pl.when(pl.program_id(0) == pl.num_programs(0) - 1)
        def _():
            for i in range(len(self.inputs)):
                for j in range(N_CHIPS):
                    self._copy(ins, outs, sems, i, j, me).wait()


class _ChipExchange:
    chips = (4, 2, 6)

    def __init__(self, arrays):
        self.inputs = list(arrays)
        self.out_shape = [jax.ShapeDtypeStruct(a.shape, a.dtype) for a in arrays]
        n = len(arrays)
        self.sems = [pltpu.SemaphoreType.DMA((n, 3)), pltpu.SemaphoreType.DMA((n, 3)), pltpu.SemaphoreType.DMA((n,))]

    def _send(self, ins, outs, sems, i, k, me):
        peer = _flip(me, self.chips[k])
        return pltpu.make_async_remote_copy(
            src_ref=ins[i].at[_chip(peer)], dst_ref=outs[i].at[_chip(me)], send_sem=sems[0].at[i, k],
            recv_sem=sems[1].at[i, k], device_id=peer, device_id_type=MESH)

    def _arrival(self, ins, outs, sems, i, k, me):
        peer = _flip(me, self.chips[k])
        return pltpu.make_async_remote_copy(
            src_ref=ins[i].at[_chip(me)], dst_ref=outs[i].at[_chip(peer)], send_sem=sems[0].at[i, k],
            recv_sem=sems[1].at[i, k], device_id=peer, device_id_type=MESH)

    def _local(self, ins, outs, sems, i, me):
        return pltpu.make_async_copy(ins[i].at[_chip(me)], outs[i].at[_chip(me)], sems[2].at[i])

    def before(self, ins, outs, sems):
        me = _mesh_position()

        @pl.when(pl.program_id(0) == 0)
        def _():
            for i in range(len(self.inputs)):
                self._local(ins, outs, sems, i, me).start()
                for k in range(len(self.chips)):
                    self._send(ins, outs, sems, i, k, me).start()

    def after(self, ins, outs, sems):
        me = _mesh_position()

        @pl.when(pl.program_id(0) == pl.num_programs(0) - 1)
        def _():
            for i in range(len(self.inputs)):
                for k in range(len(self.chips)):
                    self._arrival(ins, outs, sems, i, k, me).wait_recv()
            for i in range(len(self.inputs)):
                for k in range(len(self.chips)):
                    self._send(ins, outs, sems, i, k, me).wait_send()
                self._local(ins, outs, sems, i, me).wait()


def _pair_add(grads, halves, name):
    n = len(grads)

    def body(*refs):
        g_refs, h_refs, o_refs = refs[:n], refs[n:2 * n], refs[2 * n:]
        c = lax.axis_index("c")
        for i in range(n):
            r = h_refs[i].shape[1]
            for j in range(N_CHIPS):
                own = g_refs[i][pl.ds(pl.multiple_of((2 * j + c) * r, FLAT_ROWS), r), :]
                o_refs[i][j] = (own.astype(F32) + h_refs[i][j].astype(F32)).astype(o_refs[i].dtype)

    return pl.pallas_call(
        body, name=name, out_shape=[jax.ShapeDtypeStruct(h.shape, h.dtype) for h in halves],
        compiler_params=pltpu.CompilerParams(vmem_limit_bytes=VMEM_LIMIT),
    )(*grads, *halves)


class _Both:
    def __init__(self, first, second):
        self.jobs = (first, second)
        self.inputs = first.inputs + second.inputs
        self.out_shape = first.out_shape + second.out_shape
        self.sems = first.sems + second.sems

    def _each(self, ins, outs, sems):
        a = self.jobs[0]
        i, o, s = len(a.inputs), len(a.out_shape), len(a.sems)
        return ((a, ins[:i], outs[:o], sems[:s]), (self.jobs[1], ins[i:], outs[o:], sems[s:]))

    def before(self, ins, outs, sems):
        for job, i, o, s in self._each(ins, outs, sems):
            job.before(i, o, s)

    def after(self, ins, outs, sems):
        for job, i, o, s in self._each(ins, outs, sems):
            job.after(i, o, s)


def _run(body, comm, *, semantics, out_shape, in_specs, out_specs, scratch_shapes=(), **kw):
    if comm is None:
        return pl.pallas_call(body, out_shape=out_shape, in_specs=in_specs, out_specs=out_specs,
                              scratch_shapes=list(scratch_shapes), compiler_params=_params(semantics), **kw)
    single = not isinstance(out_shape, (list, tuple))
    outs = [out_shape] if single else list(out_shape)
    ospecs = [out_specs] if single else list(out_specs)
    counts = [len(in_specs), len(comm.inputs), len(outs), len(comm.out_shape), len(scratch_shapes), len(comm.sems)]

    def carrying(*refs):
        groups, pos = [], 0
        for c in counts:
            groups.append(refs[pos:pos + c])
            pos += c
        main_in, comm_in, main_out, comm_out, main_scratch, comm_sems = groups
        comm.before(comm_in, comm_out, comm_sems)
        body(*main_in, *main_out, *main_scratch)
        comm.after(comm_in, comm_out, comm_sems)

    call = pl.pallas_call(
        carrying, out_shape=outs + list(comm.out_shape), in_specs=list(in_specs) + [_ANY] * len(comm.inputs),
        out_specs=ospecs + [_ANY] * len(comm.out_shape), scratch_shapes=list(scratch_shapes) + list(comm.sems),
        compiler_params=_params("arbitrary"), **kw)

    def apply(*args):
        res = call(*args, *comm.inputs)
        main = res[:len(outs)]
        return (main[0] if single else list(main)), list(res[len(outs):])

    return apply


def _alone(comm, name):
    return _run(lambda: None, comm, semantics="arbitrary", name=name, grid=(1,), out_shape=[], in_specs=[], out_specs=[])()[1]


SHARDED = {"w_in": 1, "w_glu": 0, "conv_w": 1, "w_a_out": 1, "w_b_out": 0, "w_o": 0, "w_ffn_gate": 1, "w_ffn_up": 1,
           "w_ffn_down": 0, "w_ple_gate": 0, "w_ple": 1}
TRANSPOSED = ("w_in", "w_a_out", "w_ffn_gate", "w_ffn_up", "w_ple")
SMALL = ["g_mix", "b_in", "lam_re", "lam_im", "log_dt", "s5_b_re", "s5_b_im", "s5_c_re", "s5_c_im", "s5_d", "b_glu",
         "conv_b", "w_r", "b_r", "w_i", "b_i", "lru_lambda", "g_ffn", "g_ple_gate", "b_ple_gate", "g_ple", "g_final"]
WEIGHTS = ["g_mix", "w_in", "b_in", "lam_re", "lam_im", "log_dt", "s5_b_re", "s5_b_im", "s5_c_re", "s5_c_im", "s5_d",
           "w_glu", "b_glu", "conv_w", "conv_b", "w_r", "b_r", "w_i", "b_i", "lru_lambda", "w_a_out", "w_b_out", "w_o",
           "g_ffn", "w_ffn_gate", "w_ffn_up", "w_ffn_down", "g_ple_gate", "w_ple_gate", "b_ple_gate", "w_ple", "g_ple",
           "g_final"]


def _unblock(gathered, axis):
    nb, r, c = gathered.shape
    if axis == 0:
        return gathered.reshape(nb * r, c)
    return jnp.transpose(gathered, (1, 0, 2)).reshape(r, nb * c)


def _disc_scalars(lr, li, ldt):
    dt = jnp.exp(ldt)
    mag = jnp.exp(lr * dt)
    ar = mag * jnp.cos(li * dt)
    ai = mag * jnp.sin(li * dt)
    den = lr * lr + li * li
    nr = ar - 1.0
    fr = (nr * lr + ai * li) / den
    fi = (ai * lr - nr * li) / den
    return ar, ai, fr, fi


def _disc_cols(lr, li, ldt, b_re, b_im):
    ar, ai, fr, fi = _disc_scalars(lr, li, ldt)
    return ar, ai, fr * b_re - fi * b_im, fr * b_im + fi * b_re


def _local_step(x, p, target, src, small, disc, distributed=True):
    full = {} if distributed else dict(src)
    gw, halves, pairs, got = {}, {}, {}, {}

    def gather(keys):
        return (_Gather([src[k] for k in keys]), keys, full) if distributed else None

    def swap(keys):
        return (_PairSwap([gw[k] for k in keys]), keys, halves) if distributed else None

    def chips(keys):
        return (_ChipExchange([pairs[k] for k in keys]), keys, got) if distributed else None

    def add_pairs(keys):
        if distributed:
            pairs.update(zip(keys, _pair_add([gw[k] for k in keys], [halves[k] for k in keys], "pair_add_" + keys[0])))

    def carry(fn, *args, jobs=()):
        jobs = [j for j in jobs if j is not None]
        if not jobs:
            return fn(*args)
        comm = jobs[0][0]
        for j in jobs[1:]:
            comm = _Both(comm, j[0])
        res, extra = fn(*args, comm=comm)
        for job, keys, sink in jobs:
            sink.update(zip(keys, extra[:len(job.out_shape)]))
            extra = extra[len(job.out_shape):]
        return res

    d = x.shape[1]
    g, n, pch = small["s5_b_re"].shape
    heads = small["w_r"].shape[0]
    sa, lw = g * pch, small["lru_lambda"].shape[-1]
    widths = [sa, lw, d, d]
    row = lambda v: v.reshape(1, -1)

    hd = small["w_r"].shape[-1]
    ar_row, ai_row, bbr_blk, bbi_blk, cre_blk, cimn_blk, wr_blk, wi_blk = carry(
        _prep, *disc["rows"], *disc["cols"], disc["b_re"], disc["b_im"], small["s5_c_re"].reshape(sa, n),
        small["s5_c_im"].reshape(sa, n), small["w_r"].reshape(lw, hd), small["w_i"].reshape(lw, hd),
        jobs=[gather(["w_in"])])
    d_row = row(small["s5_d"])
    u_a, u_b, za, zb = carry(_inproj_fwd, x, row(small["g_mix"]), full["w_in"], row(small["b_in"]), widths,
                             jobs=[gather(["w_glu", "conv_w", "w_a_out"])])
    conv_w = _unblock(full["conv_w"], 1) if distributed else full["conv_w"]
    sr, si, y, y_a = carry(_s5_fwd, u_a, bbr_blk, bbi_blk, ar_row, ai_row, cre_blk, cimn_blk, d_row, full["w_glu"],
                           row(small["b_glu"]), jobs=[gather(["w_ffn_gate", "w_b_out"])])
    xc, h, hprev = carry(_lru_fwd, u_b, conv_w, row(small["conv_b"]), wr_blk, row(small["b_r"]), wi_blk,
                         row(small["b_i"]), row(small["lru_lambda"]), jobs=[gather(["w_ffn_up", "w_o"])])
    x1, merged, ma, mb = _merge_fwd(y_a, h, za, zb, x, full["w_a_out"], full["w_b_out"], full["w_o"])
    fg, fu = carry(_ffn_up_fwd, x1, row(small["g_ffn"]), full["w_ffn_gate"], full["w_ffn_up"],
                   jobs=[gather(["w_ffn_down"])])
    x2 = carry(_ffn_down_fwd, fg, fu, x1, full["w_ffn_down"], jobs=[gather(["w_ple_gate", "w_ple"])])

    dx2, loss_blk, gw["w_ple_gate"], gw["w_ple"], vec_tail = _tail_fwd_bwd(
        x2, p, target, row(small["g_ple_gate"]), full["w_ple_gate"], row(small["b_ple_gate"]), full["w_ple"],
        row(small["g_ple"]), row(small["g_final"]))
    dfg, dfu, act = carry(_ffn_bwd_a, dx2, fg, fu, full["w_ffn_down"], jobs=[swap(["w_ple_gate", "w_ple"])])
    tn_d = min(d, 512)
    gw["w_ffn_down"] = _matmul_tn(act, dx2, tn_d, "dw_ffn_down", BF16)
    add_pairs(["w_ple_gate", "w_ple"])
    dx1, h2, vec_ffn = carry(_ffn_bwd_b, dfg, dfu, x1, dx2, row(small["g_ffn"]), full["w_ffn_gate"], full["w_ffn_up"],
                             jobs=[chips(["w_ple_gate", "w_ple"]), swap(["w_ffn_down"])])
    gw["w_ffn_gate"] = _matmul_tn(dfg, h2, tn_d, "dw_ffn_gate", BF16)
    gw["w_ffn_up"] = _matmul_tn(dfu, h2, tn_d, "dw_ffn_up", BF16)
    add_pairs(["w_ffn_down"])
    dza, dzb, dya, dyb, gw["w_o"], gw["w_a_out"], gw["w_b_out"] = carry(
        _merge_bwd, dx1, merged, ma, mb, za, zb, y_a, h, full["w_o"], full["w_a_out"], full["w_b_out"],
        jobs=[chips(["w_ffn_down"]), swap(["w_ffn_gate", "w_ffn_up"])])
    add_pairs(["w_ffn_gate", "w_ffn_up"])
    du_b, dw_r, dw_i, vec_lru = carry(
        _lru_bwd, dyb, xc, hprev, u_b, conv_w, wr_blk, row(small["b_r"]), wi_blk, row(small["b_i"]),
        row(small["lru_lambda"]), hd, jobs=[chips(["w_ffn_gate", "w_ffn_up"]), swap(["w_o", "w_a_out", "w_b_out"])])
    add_pairs(["w_o", "w_a_out", "w_b_out"])
    du_a, lam_r, lam_i, dy16, gw["w_glu"], vec_sa, vec_gn = carry(
        _s5_bwd, dya, y, sr, si, u_a, full["w_glu"], row(small["b_glu"]), cre_blk, cimn_blk, bbr_blk, bbi_blk, ar_row,
        ai_row, d_row, jobs=[chips(["w_o", "w_a_out", "w_b_out"])])
    smalls = {"vec_tail": vec_tail, "vec_ffn": vec_ffn, "vec_lru": vec_lru, "vec_sa": vec_sa, "vec_gn": vec_gn,
              "dw_r": dw_r, "dw_i": dw_i, "loss": loss_blk}
    everyones = {}

    def gather_smalls(keys):
        return (_Gather([smalls[k] for k in keys]), keys, everyones) if distributed else None

    smalls["dbb_re"], smalls["dbb_im"], smalls["dc_re"], smalls["dc_imn"] = carry(
        _s5_param_grads, lam_r, lam_i, sr, si, u_a, dy16, pch, n, jobs=[swap(["w_glu"]), gather_smalls(list(smalls))])
    add_pairs(["w_glu"])
    grad_x, h0, dz16, smalls["vec_mix"], smalls["vec_bin"] = _inproj_bwd(
        [du_a, du_b, dza, dzb], x, dx1, row(small["g_mix"]), full["w_in"])
    shapes = {k: a.shape for k, a in smalls.items()}
    gw["w_in"] = carry(_matmul_tn, dz16, h0, tn_d, "dw_in", BF16,
                       jobs=[chips(["w_glu"]), gather_smalls(["dbb_re", "dbb_im", "dc_re", "dc_imn"])])
    smalls.update(everyones)
    if distributed:
        got["w_in"] = gw["w_in"]
        gw = got
    return grad_x, gw, smalls, shapes


def _disc_inputs(small):
    g, n, pch = small["s5_b_re"].shape
    srcs = (small["lam_re"], small["lam_im"], jnp.repeat(small["log_dt"], n))
    return {"rows": [a.reshape(1, g * n) for a in srcs], "cols": [a.reshape(g * n, 1) for a in srcs],
            "b_re": small["s5_b_re"].reshape(g * n, pch), "b_im": small["s5_b_im"].reshape(g * n, pch)}


def kernel(x, p, g_mix, w_in, b_in, lam_re, lam_im, log_dt, s5_b_re, s5_b_im, s5_c_re, s5_c_im, s5_d, w_glu, b_glu, conv_w, conv_b, w_r, b_r, w_i, b_i, lru_lambda, w_a_out, w_b_out, w_o, g_ffn, w_ffn_gate, w_ffn_up, w_ffn_down, g_ple_gate, w_ple_gate, b_ple_gate, w_ple, g_ple, g_final, loss_target, m_g_mix, m_w_in, m_b_in, m_lam_re, m_lam_im, m_log_dt, m_s5_b_re, m_s5_b_im, m_s5_c_re, m_s5_c_im, m_s5_d, m_w_glu, m_b_glu, m_conv_w, m_conv_b, m_w_r, m_b_r, m_w_i, m_b_i, m_lru_lambda, m_w_a_out, m_w_b_out, m_w_o, m_g_ffn, m_w_ffn_gate, m_w_ffn_up, m_w_ffn_down, m_g_ple_gate, m_w_ple_gate, m_b_ple_gate, m_w_ple, m_g_ple, m_g_final, v_g_mix, v_w_in, v_b_in, v_lam_re, v_lam_im, v_log_dt, v_s5_b_re, v_s5_b_im, v_s5_c_re, v_s5_c_im, v_s5_d, v_w_glu, v_b_glu, v_conv_w, v_conv_b, v_w_r, v_b_r, v_w_i, v_b_i, v_lru_lambda, v_w_a_out, v_w_b_out, v_w_o, v_g_ffn, v_w_ffn_gate, v_w_ffn_up, v_w_ffn_down, v_g_ple_gate, v_w_ple_gate, v_b_ple_gate, v_w_ple, v_g_ple, v_g_final):
    given = dict(locals())
    wts = {k: given[k] for k in WEIGHTS}
    moms = {k: given["m_" + k] for k in WEIGHTS}
    vels = {k: given["v_" + k] for k in WEIGHTS}

    def drop_depth(k, a):
        return a if k == "g_final" else a[0]

    small = {k: drop_depth(k, wts[k]) for k in SMALL}
    shard = {k: wts[k][0] for k in SHARDED}
    names = list(SHARDED)

    def wire(k):
        if k == "conv_w":
            return shard[k]
        return (shard[k].T if k in TRANSPOSED else shard[k]).astype(BF16)

    disc = _disc_inputs(small)
    grad_x, parts, smalls, shapes = _local_step(x[0], p[0, 0], loss_target[0], {k: wire(k) for k in names}, small, disc)

    late = ["vec_mix", "vec_bin"]
    received = _alone(_Both(_PairSwap([parts["w_in"]]), _Gather([smalls[k] for k in late])), "swap_last")
    (pair_in,) = _pair_add([parts["w_in"]], received[:1], "pair_add_w_in")
    smalls.update(zip(late, received[1:]))

    g_small, (parts["w_in"],) = _small_reduce(smalls, shapes, *disc["cols"], disc["b_re"], disc["b_im"],
                                              small["s5_b_re"].shape[0], comm=_ChipExchange([pair_in]))
    loss = g_small.pop("loss")[0, 0]
    cols = shard["conv_w"].shape[1]
    mine = _index((lax.axis_index("x"), lax.axis_index("y"), lax.axis_index("c")))
    parts["conv_w"] = lax.dynamic_slice_in_dim(g_small.pop("conv_w"), mine * cols, cols, axis=1)[None]
    natural = lambda k, a: a.reshape((1, -1) if k == "g_final" else wts[k].shape)
    slots = _adamw_small([natural(k, wts[k]) for k in SMALL], [natural(k, g_small[k]) for k in SMALL],
                         [natural(k, moms[k]) for k in SMALL], [natural(k, vels[k]) for k in SMALL])
    small_out = [dict(zip(SMALL, [a.reshape(wts[k].shape) for k, a in zip(SMALL, slot)])) for slot in slots]

    big_out = {}
    for k in names:
        big_out[k] = _adamw(parts[k], shard[k], moms[k][0], vels[k][0], "adamw_" + k, transposed=k in TRANSPOSED)

    outs = [loss, grad_x[None]]
    for slot in range(4):
        for k in WEIGHTS:
            if k in SHARDED:
                outs.append(big_out[k][slot][None])
            else:
                outs.append(small_out[slot][k])
    return tuple(outs)
```

```python
import math

import jax
import jax.numpy as jnp
from jax import lax
from jax.experimental import pallas as pl
from jax.experimental.pallas import tpu as pltpu

F32 = jnp.float32
BF16 = jnp.bfloat16

EPS = 1e-6
LRU_C = 8.0
CONV_WIDTH = 4
ADAM_LR = 0.001
ADAM_B1 = 0.9
ADAM_B2 = 0.999
ADAM_EPS = 1e-08
ADAM_WD = 0.01
ADAM_STEP = 10

N_DEV = 8
MESH = pl.DeviceIdType.MESH
SUBLANES = 8
LANES = 128
VMEM_LIMIT = 56 * 1024 * 1024
TOKEN_TILE = 256
TIME_CHUNK = 256


def _dot(a, b):
    return jnp.dot(a.astype(BF16), b.astype(BF16), preferred_element_type=F32)


def _dot_nt(a, b):
    return lax.dot_general(a.astype(BF16), b.astype(BF16), (((1,), (1,)), ((), ())), preferred_element_type=F32)


def _dot_tn(a, b):
    return lax.dot_general(a.astype(BF16), b.astype(BF16), (((0,), (0,)), ((), ())), preferred_element_type=F32)


def _sigmoid(x):
    return jax.nn.sigmoid(x)


def _rms_stats(x):
    r = lax.rsqrt(jnp.mean(x * x, axis=-1, keepdims=True) + EPS)
    return x * r, r


def _rms_bwd(dy, xhat, r, g):
    dxn = dy * g
    dx = r * (dxn - xhat * jnp.mean(dxn * xhat, axis=-1, keepdims=True))
    return dx, dy * xhat


def _rowsum(v):
    return jnp.sum(v, axis=0, keepdims=True)


def _expm1(x):
    u = jnp.exp(x)
    um1 = u - 1.0
    safe = jnp.where(um1 == 0.0, 1.0, jnp.log(u))
    return jnp.where(um1 == 0.0, x, um1 * x / safe)


def _softplus(x):
    e = jnp.exp(-jnp.abs(x))
    u = 1.0 + e
    um1 = u - 1.0
    safe = jnp.where(um1 == 0.0, 1.0, um1)
    log1p_e = jnp.where(um1 == 0.0, e, jnp.log(u) * e / safe)
    return jnp.maximum(x, 0.0) + log1p_e


_GELU_K = math.sqrt(2.0 / math.pi)
_GELU_C = 0.044715


def _gelu(x):
    return 0.5 * x * (1.0 + jnp.tanh(_GELU_K * (x + _GELU_C * x * x * x)))


def _gelu_grad(x):
    th = jnp.tanh(_GELU_K * (x + _GELU_C * x * x * x))
    return 0.5 * (1.0 + th) + 0.5 * x * (1.0 - th * th) * _GELU_K * (1.0 + 3.0 * _GELU_C * x * x)


def _params(*sem):
    return pltpu.CompilerParams(dimension_semantics=sem, vmem_limit_bytes=VMEM_LIMIT)


def _rows(tm, n):
    return pl.BlockSpec((tm, n), lambda i: (i, 0))


def _rows_rev(tm, n, steps):
    return pl.BlockSpec((tm, n), lambda i: (steps - 1 - i, 0))


def _whole(shape):
    nd = len(shape)
    return pl.BlockSpec(shape, lambda i: (0,) * nd, pipeline_mode=pl.Buffered(1))


def _acc(shape):
    nd = len(shape)
    return pl.BlockSpec(shape, lambda i: (0,) * nd)


def _zero_on_first(*refs):
    @pl.when(pl.program_id(0) == 0)
    def _():
        for r in refs:
            r[...] = jnp.zeros_like(r)


def _inproj_fwd(x, g_mix, w_in_t, b_in, widths, comm=None):
    t, d = x.shape
    n = w_in_t.shape[0]
    tm = min(TOKEN_TILE, t)
    offs = [sum(widths[:i]) for i in range(len(widths) + 1)]

    def body(x_ref, g_ref, w_ref, b_ref, *outs):
        xhat, _ = _rms_stats(x_ref[...])
        h = (xhat * g_ref[...]).astype(BF16)
        for k, o_ref in enumerate(outs):
            lo, hi = offs[k], offs[k + 1]
            o_ref[...] = _dot_nt(h, w_ref[lo:hi, :]) + b_ref[:, lo:hi]

    return _run(
        body, comm, name="inproj_fwd", grid=(t // tm,),
        out_shape=[jax.ShapeDtypeStruct((t, w), F32) for w in widths],
        in_specs=[_rows(tm, d), _whole((1, d)), _whole((n, d)), _whole((1, n))],
        out_specs=[_rows(tm, w) for w in widths],
        semantics="parallel",
    )(x, g_mix, w_in_t, b_in)


def _s5_fwd(u, bbr_blk, bbi_blk, ar, ai, cre_blk, cimn_blk, d_skip, w_glu, b_glu, comm=None):
    t, sa = u.shape
    gn = bbr_blk.shape[0]
    tc = min(TIME_CHUNK, t)

    def body(u_ref, bbr_ref, bbi_ref, ar_ref, ai_ref, cre_ref, cim_ref, d_ref, wg_ref, bg_ref,
             sr_ref, si_ref, y_ref, ya_ref, cr_s, ci_s):
        _zero_on_first(cr_s, ci_s)
        uv = u_ref[...]
        ub = uv.astype(BF16)
        sr_ref[...] = _dot_nt(ub, bbr_ref[...])
        si_ref[...] = _dot_nt(ub, bbi_ref[...])
        a_r = ar_ref[...]
        a_i = ai_ref[...]

        def step(row, carry):
            c_r, c_i = carry
            at = pl.ds(row, 1)
            n_r = a_r * c_r - a_i * c_i + sr_ref[at, :]
            n_i = a_r * c_i + a_i * c_r + si_ref[at, :]
            sr_ref[at, :] = n_r
            si_ref[at, :] = n_i
            return n_r, n_i

        c_r, c_i = lax.fori_loop(0, tc, step, (cr_s[0:1, :], ci_s[0:1, :]), unroll=8)
        cr_s[0:1, :] = c_r
        ci_s[0:1, :] = c_i
        y = _dot_nt(sr_ref[...], cre_ref[...]) + _dot_nt(si_ref[...], cim_ref[...]) + d_ref[...] * uv
        y_ref[...] = y
        zz = _gelu(y)
        q = _dot(zz, wg_ref[...]) + bg_ref[...]
        ya_ref[...] = zz * _sigmoid(q)

    return _run(
        body, comm, name="s5_fwd", grid=(t // tc,),
        out_shape=[jax.ShapeDtypeStruct((t, gn), F32), jax.ShapeDtypeStruct((t, gn), F32),
                   jax.ShapeDtypeStruct((t, sa), F32), jax.ShapeDtypeStruct((t, sa), F32)],
        in_specs=[_rows(tc, sa), _whole((gn, sa)), _whole((gn, sa)), _whole((1, gn)), _whole((1, gn)),
                  _whole((sa, gn)), _whole((sa, gn)), _whole((1, sa)), _whole((sa, sa)), _whole((1, sa))],
        out_specs=[_rows(tc, gn), _rows(tc, gn), _rows(tc, sa), _rows(tc, sa)],
        scratch_shapes=[pltpu.VMEM((SUBLANES, gn), F32), pltpu.VMEM((SUBLANES, gn), F32)],
        semantics="arbitrary",
    )(u, bbr_blk, bbi_blk, ar, ai, cre_blk, cimn_blk, d_skip, w_glu, b_glu)


def _lru_gates(xc, wr_ref, br_ref, wi_ref, bi_ref, lam_ref):
    r = _sigmoid(_dot(xc, wr_ref[...]) + br_ref[...])
    ig = _sigmoid(_dot(xc, wi_ref[...]) + bi_ref[...])
    sp = _softplus(-lam_ref[...])
    log_a = (-LRU_C * r) * sp
    return r, ig, sp, log_a


def _lru_fwd(u, conv_w, conv_b, wr_blk, b_r, wi_blk, b_i, lru_lambda, comm=None):
    t, w = u.shape
    tc = min(TIME_CHUNK, t)
    halo = SUBLANES

    def body(u_ref, cw_ref, cb_ref, wr_ref, br_ref, wi_ref, bi_ref, lam_ref,
             xc_ref, h_ref, hp_ref, ext_s, a_s, carry_s):
        @pl.when(pl.program_id(0) == 0)
        def _():
            ext_s[0:halo, :] = jnp.zeros((halo, w), F32)
            carry_s[...] = jnp.zeros_like(carry_s)

        ext_s[halo:halo + tc, :] = u_ref[...]
        xc = cb_ref[...]
        for k in range(CONV_WIDTH):
            off = halo - (CONV_WIDTH - 1) + k
            xc = xc + cw_ref[k:k + 1, :] * ext_s[off:off + tc, :]
        ext_s[0:halo, :] = ext_s[tc:tc + halo, :]
        xc_ref[...] = xc
        r, ig, sp, log_a = _lru_gates(xc, wr_ref, br_ref, wi_ref, bi_ref, lam_ref)
        a_s[...] = jnp.exp(log_a)
        h_ref[...] = jnp.sqrt(-_expm1(2.0 * log_a)) * ig * xc

        def step(row, carry):
            at = pl.ds(row, 1)
            hp_ref[at, :] = carry
            nxt = a_s[at, :] * carry + h_ref[at, :]
            h_ref[at, :] = nxt
            return nxt

        carry_s[0:1, :] = lax.fori_loop(0, tc, step, carry_s[0:1, :], unroll=8)

    return _run(
        body, comm, name="lru_fwd", grid=(t // tc,),
        out_shape=[jax.ShapeDtypeStruct((t, w), F32)] * 3,
        in_specs=[_rows(tc, w), _whole((CONV_WIDTH, w)), _whole((1, w)), _whole((w, w)), _whole((1, w)),
                  _whole((w, w)), _whole((1, w)), _whole((1, w))],
        out_specs=[_rows(tc, w)] * 3,
        scratch_shapes=[pltpu.VMEM((halo + tc, w), F32), pltpu.VMEM((tc, w), F32), pltpu.VMEM((SUBLANES, w), F32)],
        semantics="arbitrary",
    )(u, conv_w, conv_b, wr_blk, b_r, wi_blk, b_i, lru_lambda)


def _merge_fwd(y_a, h, za, zb, x, w_a_out_t, w_b_out, w_o, comm=None):
    t, d = x.shape
    sa, lw = y_a.shape[1], h.shape[1]
    tm = min(TOKEN_TILE, t)

    def body(ya_ref, h_ref, za_ref, zb_ref, x_ref, wa_ref, wb_ref, wo_ref, x1_ref, mg_ref, ma_ref, mb_ref):
        ma = _dot_nt(ya_ref[...], wa_ref[...])
        mb = _dot(h_ref[...], wb_ref[...])
        merged = _sigmoid(za_ref[...]) * ma + _sigmoid(zb_ref[...]) * mb
        ma_ref[...] = ma
        mb_ref[...] = mb
        mg_ref[...] = merged.astype(mg_ref.dtype)
        x1_ref[...] = x_ref[...] + _dot(merged, wo_ref[...])

    return _run(
        body, comm, name="merge_fwd", grid=(t // tm,),
        out_shape=[jax.ShapeDtypeStruct((t, d), F32), jax.ShapeDtypeStruct((t, d), BF16),
                   jax.ShapeDtypeStruct((t, d), F32), jax.ShapeDtypeStruct((t, d), F32)],
        in_specs=[_rows(tm, sa), _rows(tm, lw), _rows(tm, d), _rows(tm, d), _rows(tm, d),
                  _whole((d, sa)), _whole((lw, d)), _whole((d, d))],
        out_specs=[_rows(tm, d)] * 4,
        semantics="parallel",
    )(y_a, h, za, zb, x, w_a_out_t, w_b_out, w_o)


def _ffn_up_fwd(x1, g_ffn, w_gate_t, w_up_t, comm=None):
    t, d = x1.shape
    f = w_gate_t.shape[0]
    tm = min(TOKEN_TILE, t)

    def body(x_ref, g_ref, wg_ref, wu_ref, fg_ref, fu_ref):
        xhat, _ = _rms_stats(x_ref[...])
        h2 = (xhat * g_ref[...]).astype(BF16)
        fg_ref[...] = _dot_nt(h2, wg_ref[...])
        fu_ref[...] = _dot_nt(h2, wu_ref[...])

    return _run(
        body, comm, name="ffn_up_fwd", grid=(t // tm,),
        out_shape=[jax.ShapeDtypeStruct((t, f), F32)] * 2,
        in_specs=[_rows(tm, d), _whole((1, d)), _whole((f, d)), _whole((f, d))],
        out_specs=[_rows(tm, f)] * 2,
        semantics="parallel",
    )(x1, g_ffn, w_gate_t, w_up_t)


def _ffn_down_fwd(fg, fu, x1, w_down, comm=None):
    t, d = x1.shape
    f = fg.shape[1]
    tm = min(TOKEN_TILE, t)

    def body(fg_ref, fu_ref, x_ref, wd_ref, x2_ref):
        fgv = fg_ref[...]
        act = fgv * _sigmoid(fgv) * fu_ref[...]
        x2_ref[...] = x_ref[...] + _dot(act, wd_ref[...])

    return _run(
        body, comm, name="ffn_down_fwd", grid=(t // tm,),
        out_shape=jax.ShapeDtypeStruct((t, d), F32),
        in_specs=[_rows(tm, f), _rows(tm, f), _rows(tm, d), _whole((f, d))],
        out_specs=_rows(tm, d),
        semantics="parallel",
    )(fg, fu, x1, w_down)


def _store_on_last(pairs):
    @pl.when(pl.program_id(0) == pl.num_programs(0) - 1)
    def _():
        for acc, out in pairs:
            out[...] = acc[...].astype(out.dtype)


def _tail_fwd_bwd(x2, p, target, g_pg, w_pg, b_pg, w_ple_t, g_ple, g_final):
    t, d = x2.shape
    pd = p.shape[1]
    tm = min(TOKEN_TILE, t)

    def body(x2_ref, p_ref, tg_ref, gpg_ref, wpg_ref, bpg_ref, wple_ref, gple_ref, gfin_ref,
             dx2_ref, loss_ref, dwpg_out, dwple_out, vec_ref, dwpg_ref, dwple_ref):
        _zero_on_first(loss_ref, dwpg_ref, dwple_ref, vec_ref)
        x2v = x2_ref[...]
        xh2, r2 = _rms_stats(x2v)
        h3 = xh2 * gpg_ref[...]
        gp = _sigmoid(_dot(h3, wpg_ref[...]) + bpg_ref[...])
        pe = _dot_nt(p_ref[...], wple_ref[...])
        peh, r3 = _rms_stats(pe)
        e = peh * gple_ref[...]
        x3 = x2v + gp * e
        xh3, r4 = _rms_stats(x3)
        diff = xh3 * gfin_ref[...] - tg_ref[...]
        loss_ref[...] += 0.5 * jnp.sum(jnp.mean(diff * diff, axis=-1, keepdims=True))
        dy = diff * (1.0 / d)
        dx3, dgfin = _rms_bwd(dy, xh3, r4, gfin_ref[...])
        d_gp = dx3 * e
        d_e = dx3 * gp
        dpe, dgple = _rms_bwd(d_e, peh, r3, gple_ref[...])
        dwple_ref[...] += _dot_tn(dpe, p_ref[...])
        dpre = d_gp * gp * (1.0 - gp)
        dwpg_ref[...] += _dot_tn(h3, dpre)
        dh3 = _dot_nt(dpre, wpg_ref[...])
        dx2n, dgpg = _rms_bwd(dh3, xh2, r2, gpg_ref[...])
        dx2_ref[...] = dx3 + dx2n
        vec_ref[0:1, :] += _rowsum(dpre)
        vec_ref[1:2, :] += _rowsum(dgpg)
        vec_ref[2:3, :] += _rowsum(dgple)
        vec_ref[3:4, :] += _rowsum(dgfin)
        _store_on_last([(dwpg_ref, dwpg_out), (dwple_ref, dwple_out)])

    return pl.pallas_call(
        body, name="tail_fwd_bwd", grid=(t // tm,),
        out_shape=[jax.ShapeDtypeStruct((t, d), F32), jax.ShapeDtypeStruct((SUBLANES, LANES), F32),
                   jax.ShapeDtypeStruct((d, d), BF16), jax.ShapeDtypeStruct((d, pd), BF16),
                   jax.ShapeDtypeStruct((SUBLANES, d), F32)],
        in_specs=[_rows(tm, d), _rows(tm, pd), _rows(tm, d), _whole((1, d)), _whole((d, d)), _whole((1, d)),
                  _whole((d, pd)), _whole((1, d)), _whole((1, d))],
        out_specs=[_rows(tm, d), _acc((SUBLANES, LANES)), _acc((d, d)), _acc((d, pd)), _acc((SUBLANES, d))],
        scratch_shapes=[pltpu.VMEM((d, d), F32), pltpu.VMEM((d, pd), F32)],
        compiler_params=_params("arbitrary"),
    )(x2, p, target, g_pg, w_pg, b_pg, w_ple_t, g_ple, g_final)


def _ffn_bwd_a(dx2, fg, fu, w_down, comm=None):
    t, d = dx2.shape
    f = fg.shape[1]
    tm = min(TOKEN_TILE, t)

    def body(dx_ref, fg_ref, fu_ref, wd_ref, dfg_ref, dfu_ref, act_ref):
        dact = _dot_nt(dx_ref[...], wd_ref[...])
        fgv = fg_ref[...]
        fuv = fu_ref[...]
        sg = _sigmoid(fgv)
        silu = fgv * sg
        dfu_ref[...] = (dact * silu).astype(dfu_ref.dtype)
        dfg_ref[...] = (dact * fuv * (sg * (1.0 + fgv * (1.0 - sg)))).astype(dfg_ref.dtype)
        act_ref[...] = (silu * fuv).astype(act_ref.dtype)

    return _run(
        body, comm, name="ffn_bwd_a", grid=(t // tm,),
        out_shape=[jax.ShapeDtypeStruct((t, f), BF16)] * 3,
        in_specs=[_rows(tm, d), _rows(tm, f), _rows(tm, f), _whole((f, d))],
        out_specs=[_rows(tm, f)] * 3,
        semantics="parallel",
    )(dx2, fg, fu, w_down)


def _ffn_bwd_b(dfg, dfu, x1, dx2, g_ffn, w_gate_t, w_up_t, comm=None):
    t, d = x1.shape
    f = dfg.shape[1]
    tm = min(TOKEN_TILE, t)

    def body(dfg_ref, dfu_ref, x_ref, dx2_ref, g_ref, wg_ref, wu_ref, dx1_ref, h2_ref, vec_ref):
        _zero_on_first(vec_ref)
        dh2 = _dot(dfg_ref[...], wg_ref[...]) + _dot(dfu_ref[...], wu_ref[...])
        xhat, r = _rms_stats(x_ref[...])
        h2_ref[...] = (xhat * g_ref[...]).astype(h2_ref.dtype)
        dxn, dg = _rms_bwd(dh2, xhat, r, g_ref[...])
        dx1_ref[...] = dx2_ref[...] + dxn
        vec_ref[0:1, :] += _rowsum(dg)

    return _run(
        body, comm, name="ffn_bwd_b", grid=(t // tm,),
        out_shape=[jax.ShapeDtypeStruct((t, d), F32), jax.ShapeDtypeStruct((t, d), BF16),
                   jax.ShapeDtypeStruct((SUBLANES, d), F32)],
        in_specs=[_rows(tm, f), _rows(tm, f), _rows(tm, d), _rows(tm, d), _whole((1, d)), _whole((f, d)), _whole((f, d))],
        out_specs=[_rows(tm, d), _rows(tm, d), _acc((SUBLANES, d))],
        semantics="arbitrary",
    )(dfg, dfu, x1, dx2, g_ffn, w_gate_t, w_up_t)


def _matmul_tn(a, b, tn, name, dtype=F32, comm=None):
    t, k = a.shape
    n = b.shape[1]

    def body(a_ref, b_ref, o_ref):
        o_ref[...] = _dot_tn(a_ref[...], b_ref[...]).astype(o_ref.dtype)

    return _run(
        body, comm, name=name, grid=(n // tn,),
        out_shape=jax.ShapeDtypeStruct((k, n), dtype),
        in_specs=[_whole((t, k)), pl.BlockSpec((t, tn), lambda j: (0, j))],
        out_specs=pl.BlockSpec((k, tn), lambda j: (0, j)),
        semantics="parallel",
    )(a, b)


def _merge_bwd(dx1, merged, ma, mb, za, zb, y_a, h, w_o, w_a_out_t, w_b_out, comm=None):
    t, d = dx1.shape
    sa, lw = y_a.shape[1], h.shape[1]
    tm = min(TOKEN_TILE, t)

    def body(dx1_ref, mg_ref, ma_ref, mb_ref, za_ref, zb_ref, ya_ref, h_ref, wo_ref, wa_ref, wb_ref,
             dza_ref, dzb_ref, dya_ref, dyb_ref, dwo_out, dwa_out, dwb_out, dwo_ref, dwa_ref, dwb_ref):
        _zero_on_first(dwo_ref, dwa_ref, dwb_ref)
        dx1v = dx1_ref[...].astype(BF16)
        dmg = _dot_nt(dx1v, wo_ref[...])
        ga = _sigmoid(za_ref[...])
        gb = _sigmoid(zb_ref[...])
        dza_ref[...] = dmg * ma_ref[...] * ga * (1.0 - ga)
        dzb_ref[...] = dmg * mb_ref[...] * gb * (1.0 - gb)
        dma = (dmg * ga).astype(BF16)
        dmb = (dmg * gb).astype(BF16)
        dya_ref[...] = _dot(dma, wa_ref[...])
        dyb_ref[...] = _dot_nt(dmb, wb_ref[...])
        dwo_ref[...] += _dot_tn(mg_ref[...], dx1v)
        dwa_ref[...] += _dot_tn(dma, ya_ref[...])
        dwb_ref[...] += _dot_tn(h_ref[...], dmb)
        _store_on_last([(dwo_ref, dwo_out), (dwa_ref, dwa_out), (dwb_ref, dwb_out)])

    return _run(
        body, comm, name="merge_bwd", grid=(t // tm,),
        out_shape=[jax.ShapeDtypeStruct((t, d), F32), jax.ShapeDtypeStruct((t, d), F32),
                   jax.ShapeDtypeStruct((t, sa), F32), jax.ShapeDtypeStruct((t, lw), F32),
                   jax.ShapeDtypeStruct((d, d), BF16), jax.ShapeDtypeStruct((d, sa), BF16),
                   jax.ShapeDtypeStruct((lw, d), BF16)],
        in_specs=[_rows(tm, d), _rows(tm, d), _rows(tm, d), _rows(tm, d), _rows(tm, d), _rows(tm, d),
                  _rows(tm, sa), _rows(tm, lw), _whole((d, d)), _whole((d, sa)), _whole((lw, d))],
        out_specs=[_rows(tm, d), _rows(tm, d), _rows(tm, sa), _rows(tm, lw), _acc((d, d)), _acc((d, sa)), _acc((lw, d))],
        scratch_shapes=[pltpu.VMEM((d, d), F32), pltpu.VMEM((d, sa), F32), pltpu.VMEM((lw, d), F32)],
        semantics="arbitrary",
    )(dx1, merged, ma, mb, za, zb, y_a, h, w_o, w_a_out_t, w_b_out)


def _fold_diag_blocks(dense, row_group, col_group, row0=0, col0=0):
    r, c = dense.shape
    rows = lax.broadcasted_iota(jnp.int32, (r, c), 0) + row0
    cols = lax.broadcasted_iota(jnp.int32, (r, c), 1) + col0
    kept = jnp.where(rows // row_group == cols // col_group, dense, 0.0)
    pick = (lax.broadcasted_iota(jnp.int32, (row_group, r), 0)
            == lax.broadcasted_iota(jnp.int32, (row_group, r), 1) % row_group).astype(F32)
    return jnp.dot(pick, kept, preferred_element_type=F32, precision=lax.Precision.HIGHEST)


def _lru_bwd(dh, xc, hprev, u, conv_w, wr_blk, b_r, wi_blk, b_i, lru_lambda, head_dim, comm=None):
    t, w = dh.shape
    tc = min(TIME_CHUNK, t)
    steps = t // tc
    halo = SUBLANES
    sub_per_chunk = tc // halo

    def body(dh_ref, xc_ref, hp_ref, u_ref, uh_ref, cw_ref, wr_ref, br_ref, wi_ref, bi_ref, lam_ref,
             du_ref, dwr_out, dwi_out, vec_ref, lam_s, a_s, dxc_s, uext_s, carry_s, dwr_ref, dwi_ref):
        chunk = steps - 1 - pl.program_id(0)

        @pl.when(pl.program_id(0) == 0)
        def _():
            carry_s[...] = jnp.zeros_like(carry_s)
            dxc_s[tc:tc + halo, :] = jnp.zeros((halo, w), F32)
            dwr_ref[...] = jnp.zeros_like(dwr_ref)
            dwi_ref[...] = jnp.zeros_like(dwi_ref)
            vec_ref[...] = jnp.zeros_like(vec_ref)

        xc = xc_ref[...]
        r, ig, sp, log_a = _lru_gates(xc, wr_ref, br_ref, wi_ref, bi_ref, lam_ref)
        a = jnp.exp(log_a)
        a_s[...] = a

        def step(i, q):
            at = pl.ds(tc - 1 - i, 1)
            lam_row = dh_ref[at, :] + q
            lam_s[at, :] = lam_row
            return a_s[at, :] * lam_row

        carry_s[0:1, :] = lax.fori_loop(0, tc, step, carry_s[0:1, :], unroll=8)
        lam = lam_s[...]
        mult = jnp.sqrt(-_expm1(2.0 * log_a))
        d_log_a = lam * hp_ref[...] * a - (lam * ig * xc) * (a * a) / mult
        d_ig = lam * mult * xc
        dpre_r = (d_log_a * (-LRU_C * sp)) * r * (1.0 - r)
        dpre_i = d_ig * ig * (1.0 - ig)
        dxc = lam * mult * ig + _dot_nt(dpre_r, wr_ref[...]) + _dot_nt(dpre_i, wi_ref[...])
        dwr_ref[...] += _dot_tn(dpre_r, xc)
        dwi_ref[...] += _dot_tn(dpre_i, xc)
        vec_ref[0:1, :] += _rowsum(dxc)
        vec_ref[1:2, :] += _rowsum(dpre_r)
        vec_ref[2:3, :] += _rowsum(dpre_i)
        vec_ref[3:4, :] += _rowsum(d_log_a * (-LRU_C * r)) * (-_sigmoid(-lam_ref[...]))
        dxc_s[0:tc, :] = dxc
        du = cw_ref[CONV_WIDTH - 1:CONV_WIDTH, :] * dxc
        for k in range(CONV_WIDTH - 1):
            off = CONV_WIDTH - 1 - k
            du = du + cw_ref[k:k + 1, :] * dxc_s[off:off + tc, :]
        du_ref[...] = du
        dxc_s[tc:tc + halo, :] = dxc_s[0:halo, :]
        uext_s[0:halo, :] = jnp.where(chunk > 0, uh_ref[...], 0.0)
        uext_s[halo:halo + tc, :] = u_ref[...]
        for k in range(CONV_WIDTH):
            off = halo - (CONV_WIDTH - 1) + k
            vec_ref[4 + k:5 + k, :] += _rowsum(dxc * uext_s[off:off + tc, :])

        @pl.when(pl.program_id(0) == steps - 1)
        def _():
            dwr_out[...] = _fold_diag_blocks(dwr_ref[...], head_dim, head_dim)
            dwi_out[...] = _fold_diag_blocks(dwi_ref[...], head_dim, head_dim)

    halo_spec = pl.BlockSpec((halo, w), lambda i: (jnp.maximum((steps - 1 - i) * sub_per_chunk - 1, 0), 0))
    return _run(
        body, comm, name="lru_bwd", grid=(steps,),
        out_shape=[jax.ShapeDtypeStruct((t, w), F32), jax.ShapeDtypeStruct((head_dim, w), F32),
                   jax.ShapeDtypeStruct((head_dim, w), F32), jax.ShapeDtypeStruct((SUBLANES, w), F32)],
        in_specs=[_rows_rev(tc, w, steps)] * 4 + [halo_spec, _whole((CONV_WIDTH, w)), _whole((w, w)), _whole((1, w)),
                                                  _whole((w, w)), _whole((1, w)), _whole((1, w))],
        out_specs=[_rows_rev(tc, w, steps), _acc((head_dim, w)), _acc((head_dim, w)), _acc((SUBLANES, w))],
        scratch_shapes=[pltpu.VMEM((tc, w), F32), pltpu.VMEM((tc, w), F32), pltpu.VMEM((tc + halo, w), F32),
                        pltpu.VMEM((halo + tc, w), F32), pltpu.VMEM((SUBLANES, w), F32), pltpu.VMEM((w, w), F32),
                        pltpu.VMEM((w, w), F32)],
        semantics="arbitrary",
    )(dh, xc, hprev, u, u, conv_w, wr_blk, b_r, wi_blk, b_i, lru_lambda)


def _s5_bwd(dya, y, sr, si, u, w_glu, b_glu, cre_blk, cimn_blk, bbr_blk, bbi_blk, ar, ai, d_skip, comm=None):
    t, sa = dya.shape
    gn = sr.shape[1]
    tc = min(TIME_CHUNK, t)
    steps = t // tc
    halo = SUBLANES

    def body(dya_ref, y_ref, sr_ref, si_ref, u_ref, wg_ref, bg_ref, cre_ref, cim_ref, bbr_ref, bbi_ref,
             ar_ref, ai_ref, d_ref, du_ref, lr_ref, li_ref, dy_ref, dwg_out, vsa_ref, vgn_ref, gr_s, gi_s, cr_s, ci_s,
             dwg_ref):
        @pl.when(pl.program_id(0) == 0)
        def _():
            cr_s[...] = jnp.zeros_like(cr_s)
            ci_s[...] = jnp.zeros_like(ci_s)
            gr_s[tc:tc + halo, :] = jnp.zeros((halo, gn), F32)
            gi_s[tc:tc + halo, :] = jnp.zeros((halo, gn), F32)
            dwg_ref[...] = jnp.zeros_like(dwg_ref)
            vsa_ref[...] = jnp.zeros_like(vsa_ref)
            vgn_ref[...] = jnp.zeros_like(vgn_ref)

        yv = y_ref[...]
        uv = u_ref[...]
        zz = _gelu(yv)
        sg = _sigmoid(_dot(zz, wg_ref[...]) + bg_ref[...])
        dyav = dya_ref[...]
        dq = dyav * zz * sg * (1.0 - sg)
        dzz = dyav * sg + _dot_nt(dq, wg_ref[...])
        dwg_ref[...] += _dot_tn(zz, dq)
        dy = dzz * _gelu_grad(yv)
        dyb = dy.astype(BF16)
        dy_ref[...] = dyb.astype(dy_ref.dtype)
        vsa_ref[0:1, :] += _rowsum(dq)
        vsa_ref[1:2, :] += _rowsum(dy * uv)
        gr_s[0:tc, :] = _dot(dyb, cre_ref[...])
        gi_s[0:tc, :] = _dot(dyb, cim_ref[...])
        a_r = ar_ref[...]
        a_i = ai_ref[...]

        def step(i, carry):
            l_r, l_i = carry
            at = pl.ds(tc - 1 - i, 1)
            n_r = gr_s[at, :] + a_r * l_r + a_i * l_i
            n_i = gi_s[at, :] + a_r * l_i - a_i * l_r
            gr_s[at, :] = n_r
            gi_s[at, :] = n_i
            return n_r, n_i

        l_r, l_i = lax.fori_loop(0, tc, step, (cr_s[0:1, :], ci_s[0:1, :]), unroll=8)
        cr_s[0:1, :] = l_r
        ci_s[0:1, :] = l_i
        nxt_r = gr_s[1:tc + 1, :]
        nxt_i = gi_s[1:tc + 1, :]
        srv = sr_ref[...]
        siv = si_ref[...]
        vgn_ref[0:1, :] += _rowsum(nxt_r * srv + nxt_i * siv)
        vgn_ref[1:2, :] += _rowsum(nxt_i * srv - nxt_r * siv)
        lam_r = gr_s[0:tc, :]
        lam_i = gi_s[0:tc, :]
        gr_s[tc:tc + halo, :] = gr_s[0:halo, :]
        gi_s[tc:tc + halo, :] = gi_s[0:halo, :]
        lrb = lam_r.astype(BF16)
        lib = lam_i.astype(BF16)
        lr_ref[...] = lrb.astype(lr_ref.dtype)
        li_ref[...] = lib.astype(li_ref.dtype)
        du_ref[...] = _dot(lrb, bbr_ref[...]) + _dot(lib, bbi_ref[...]) + dy * d_ref[...]
        _store_on_last([(dwg_ref, dwg_out)])

    return _run(
        body, comm, name="s5_bwd", grid=(steps,),
        out_shape=[jax.ShapeDtypeStruct((t, sa), F32), jax.ShapeDtypeStruct((t, gn), BF16),
                   jax.ShapeDtypeStruct((t, gn), BF16), jax.ShapeDtypeStruct((t, sa), BF16),
                   jax.ShapeDtypeStruct((sa, sa), BF16), jax.ShapeDtypeStruct((SUBLANES, sa), F32),
                   jax.ShapeDtypeStruct((SUBLANES, gn), F32)],
        in_specs=[_rows_rev(tc, sa, steps), _rows_rev(tc, sa, steps), _rows_rev(tc, gn, steps), _rows_rev(tc, gn, steps),
                  _rows_rev(tc, sa, steps), _whole((sa, sa)), _whole((1, sa)), _whole((sa, gn)), _whole((sa, gn)),
                  _whole((gn, sa)), _whole((gn, sa)), _whole((1, gn)), _whole((1, gn)), _whole((1, sa))],
        out_specs=[_rows_rev(tc, sa, steps), _rows_rev(tc, gn, steps), _rows_rev(tc, gn, steps), _rows_rev(tc, sa, steps),
                   _acc((sa, sa)), _acc((SUBLANES, sa)), _acc((SUBLANES, gn))],
        scratch_shapes=[pltpu.VMEM((tc + halo, gn), F32), pltpu.VMEM((tc + halo, gn), F32),
                        pltpu.VMEM((SUBLANES, gn), F32), pltpu.VMEM((SUBLANES, gn), F32), pltpu.VMEM((sa, sa), F32)],
        semantics="arbitrary",
    )(dya, y, sr, si, u, w_glu, b_glu, cre_blk, cimn_blk, bbr_blk, bbi_blk, ar, ai, d_skip)


def _inproj_bwd(dparts, x, dx1, g_mix, w_in_t, comm=None):
    t, d = x.shape
    n = w_in_t.shape[0]
    widths = [p.shape[1] for p in dparts]
    offs = [sum(widths[:i]) for i in range(len(widths) + 1)]
    tm = min(TOKEN_TILE, t)
    np_ = len(dparts)

    def body(*refs):
        dz_refs = refs[:np_]
        x_ref, dx1_ref, g_ref, w_ref, gx_ref, h_ref, dz_ref, vd_ref, vn_ref = refs[np_:]
        _zero_on_first(vd_ref, vn_ref)
        dh = jnp.zeros((tm, d), F32)
        for k, r in enumerate(dz_refs):
            lo, hi = offs[k], offs[k + 1]
            dzk = r[...]
            dh = dh + _dot(dzk, w_ref[lo:hi, :])
            dz_ref[:, lo:hi] = dzk.astype(dz_ref.dtype)
            vn_ref[0:1, lo:hi] += _rowsum(dzk)
        xhat, r0 = _rms_stats(x_ref[...])
        h_ref[...] = (xhat * g_ref[...]).astype(h_ref.dtype)
        dxn, dg = _rms_bwd(dh, xhat, r0, g_ref[...])
        gx_ref[...] = dx1_ref[...] + dxn
        vd_ref[0:1, :] += _rowsum(dg)

    return _run(
        body, comm, name="inproj_bwd", grid=(t // tm,),
        out_shape=[jax.ShapeDtypeStruct((t, d), F32), jax.ShapeDtypeStruct((t, d), BF16),
                   jax.ShapeDtypeStruct((t, n), BF16), jax.ShapeDtypeStruct((SUBLANES, d), F32),
                   jax.ShapeDtypeStruct((SUBLANES, n), F32)],
        in_specs=[_rows(tm, w) for w in widths] + [_rows(tm, d), _rows(tm, d), _whole((1, d)), _whole((n, d))],
        out_specs=[_rows(tm, d), _rows(tm, d), _rows(tm, n), _acc((SUBLANES, d)), _acc((SUBLANES, n))],
        semantics="arbitrary",
    )(*dparts, x, dx1, g_mix, w_in_t)


def _prep(lr_row, li_row, ldt_row, lr_col, li_col, ldt_col, b_re, b_im, c_re, c_im, w_r, w_i, comm=None):
    gn, pch = b_re.shape
    sa, n = c_re.shape
    w, hd = w_r.shape

    def expand(vals, row_group, col_group, width):
        r, k = vals.shape
        tile = (lax.broadcasted_iota(jnp.int32, (k, width), 0) == lax.broadcasted_iota(jnp.int32, (k, width), 1) % k)
        rows = lax.broadcasted_iota(jnp.int32, (r, width), 0) // row_group
        cols = lax.broadcasted_iota(jnp.int32, (r, width), 1) // col_group
        return jnp.where(rows == cols, _dot(vals, tile.astype(BF16)), 0.0)

    def body(lrr, lir, ldr, lrc, lic, ldc, bre, bim, cre, cim, wr, wi,
             ar_o, ai_o, bbr_o, bbi_o, cre_o, cim_o, wr_o, wi_o):
        ar, ai, _, _ = _disc_scalars(lrr[...], lir[...], ldr[...])
        ar_o[...] = ar
        ai_o[...] = ai
        _, _, bbr, bbi = _disc_cols(lrc[...], lic[...], ldc[...], bre[...], bim[...])
        bbr_o[...] = expand(bbr, n, pch, sa).astype(bbr_o.dtype)
        bbi_o[...] = expand(bbi, n, pch, sa).astype(bbi_o.dtype)
        cre_o[...] = expand(cre[...], pch, n, gn).astype(cre_o.dtype)
        cim_o[...] = expand(-cim[...], pch, n, gn).astype(cim_o.dtype)
        wr_o[...] = expand(wr[...], hd, hd, w).astype(wr_o.dtype)
        wi_o[...] = expand(wi[...], hd, hd, w).astype(wi_o.dtype)

    args = (lr_row, li_row, ldt_row, lr_col, li_col, ldt_col, b_re, b_im, c_re, c_im, w_r, w_i)
    out_shape = [jax.ShapeDtypeStruct((1, gn), F32), jax.ShapeDtypeStruct((1, gn), F32),
                 jax.ShapeDtypeStruct((gn, sa), BF16), jax.ShapeDtypeStruct((gn, sa), BF16),
                 jax.ShapeDtypeStruct((sa, gn), BF16), jax.ShapeDtypeStruct((sa, gn), BF16),
                 jax.ShapeDtypeStruct((w, w), BF16), jax.ShapeDtypeStruct((w, w), BF16)]
    return _run(
        body, comm, name="prep", grid=(1,), out_shape=out_shape, in_specs=[_whole(a.shape) for a in args],
        out_specs=[_acc(s.shape) for s in out_shape], semantics="arbitrary",
    )(*args)


def _s5_param_grads(lam_r, lam_i, sr, si, u, dy, pch, n, comm=None):
    t, gn = lam_r.shape
    sa = u.shape[1]
    tb = min(gn, 512)

    def body(lr_ref, li_ref, sr_ref, si_ref, u_ref, dy_ref, dbr_ref, dbi_ref, dcr_ref, dci_ref):
        _zero_on_first(dcr_ref, dci_ref)
        base = pl.program_id(0) * tb
        uv = u_ref[...]
        dyv = dy_ref[...]
        dbr_ref[...] = _fold_diag_blocks(_dot_tn(uv, lr_ref[...]), pch, n, col0=base)
        dbi_ref[...] = _fold_diag_blocks(_dot_tn(uv, li_ref[...]), pch, n, col0=base)
        dcr_ref[...] += _fold_diag_blocks(_dot_tn(sr_ref[...], dyv), n, pch, row0=base)
        dci_ref[...] += _fold_diag_blocks(_dot_tn(si_ref[...], dyv), n, pch, row0=base)

    cols = pl.BlockSpec((t, tb), lambda j: (0, j))
    return _run(
        body, comm, name="s5_param_grads", grid=(gn // tb,),
        out_shape=[jax.ShapeDtypeStruct((pch, gn), F32), jax.ShapeDtypeStruct((pch, gn), F32),
                   jax.ShapeDtypeStruct((n, sa), F32), jax.ShapeDtypeStruct((n, sa), F32)],
        in_specs=[cols, cols, cols, cols, _whole((t, sa)), _whole((t, sa))],
        out_specs=[pl.BlockSpec((pch, tb), lambda j: (0, j)), pl.BlockSpec((pch, tb), lambda j: (0, j)),
                   _acc((n, sa)), _acc((n, sa))],
        semantics="arbitrary",
    )(lam_r, lam_i, sr, si, u, dy)


SMALL_PARTS = ["vec_tail", "vec_ffn", "vec_lru", "vec_mix", "vec_bin", "vec_sa", "vec_gn", "dw_r", "dw_i", "dbb_re",
               "dbb_im", "dc_re", "dc_imn", "loss"]


def _small_reduce(parts, shapes, lr_col, li_col, ldt_col, b_re, b_im, groups, comm=None):
    gn, pch = b_re.shape
    n = gn // groups
    nparts = parts[SMALL_PARTS[0]].size // math.prod(shapes[SMALL_PARTS[0]])
    np_, nout = len(SMALL_PARTS), 24

    def body(*refs):
        ins = refs[:np_]
        lr, li, ldt, bre, bim = refs[np_:np_ + 5]
        outs = refs[np_ + 5:np_ + 5 + nout]
        sums = dict(zip(SMALL_PARTS, refs[np_ + 5 + nout:]))

        @pl.when(pl.program_id(0) == 0)
        def _():
            for k, r in zip(SMALL_PARTS, ins):
                sums[k][...] = r[...]

        @pl.when(pl.program_id(0) > 0)
        def _():
            for k, r in zip(SMALL_PARTS, ins):
                sums[k][...] += r[...]

        @pl.when(pl.program_id(0) == nparts - 1)
        def _():
            finish({k: s[...] for k, s in sums.items()}, lr, li, ldt, bre, bim, *outs)

    def finish(tot, lr, li, ldt, bre, bim, o_loss, o_gmix, o_bin, o_bglu, o_s5d, o_convb, o_br, o_bi, o_lam, o_gffn,
               o_gpg, o_bpg, o_gple, o_gfin, o_wr, o_wi, o_cre, o_cim, o_bre, o_bim, o_lre, o_lim, o_ldt, o_convw):
        o_loss[...] = tot["loss"]
        o_convw[...] = tot["vec_lru"][SUBLANES - CONV_WIDTH:SUBLANES]
        o_bpg[...] = tot["vec_tail"][0:1]
        o_gpg[...] = tot["vec_tail"][1:2]
        o_gple[...] = tot["vec_tail"][2:3]
        o_gfin[...] = tot["vec_tail"][3:4]
        o_gffn[...] = tot["vec_ffn"][0:1]
        o_convb[...] = tot["vec_lru"][0:1]
        o_br[...] = tot["vec_lru"][1:2]
        o_bi[...] = tot["vec_lru"][2:3]
        o_lam[...] = tot["vec_lru"][3:4]
        o_gmix[...] = tot["vec_mix"][0:1]
        o_bin[...] = tot["vec_bin"][0:1]
        o_bglu[...] = tot["vec_sa"][0:1]
        o_s5d[...] = tot["vec_sa"][1:2]
        o_wr[...] = tot["dw_r"].T
        o_wi[...] = tot["dw_i"].T
        o_cre[...] = tot["dc_re"].T
        o_cim[...] = -tot["dc_imn"].T
        d_a = tot["vec_gn"].T
        _, chain = jax.vjp(_disc_cols, lr[...], li[...], ldt[...], bre[...], bim[...])
        d_lr, d_li, d_ldt, d_bre, d_bim = chain((d_a[:, 0:1], d_a[:, 1:2], tot["dbb_re"].T, tot["dbb_im"].T))
        o_lre[...] = d_lr
        o_lim[...] = d_li
        o_bre[...] = d_bre
        o_bim[...] = d_bim
        same = (lax.broadcasted_iota(jnp.int32, (groups, gn), 0)
                == lax.broadcasted_iota(jnp.int32, (groups, gn), 1) // n).astype(F32)
        o_ldt[...] = jnp.dot(same, d_ldt * jnp.ones((1, LANES), F32), preferred_element_type=F32,
                             precision=lax.Precision.HIGHEST)[:, 0:1]

    d = shapes["vec_mix"][1]
    nz = shapes["vec_bin"][1]
    sa = shapes["vec_sa"][1]
    w = shapes["vec_lru"][1]
    row = lambda c: jax.ShapeDtypeStruct((1, c), F32)
    out_shape = [jax.ShapeDtypeStruct(shapes["loss"], F32), row(d), row(nz), row(sa), row(sa), row(w), row(w), row(w),
                 row(w), row(d), row(d), row(d), row(d), row(d),
                 jax.ShapeDtypeStruct(shapes["dw_r"][::-1], F32), jax.ShapeDtypeStruct(shapes["dw_i"][::-1], F32),
                 jax.ShapeDtypeStruct(shapes["dc_re"][::-1], F32), jax.ShapeDtypeStruct(shapes["dc_imn"][::-1], F32),
                 jax.ShapeDtypeStruct((gn, pch), F32), jax.ShapeDtypeStruct((gn, pch), F32),
                 jax.ShapeDtypeStruct((gn, 1), F32), jax.ShapeDtypeStruct((gn, 1), F32),
                 jax.ShapeDtypeStruct((groups, 1), F32), jax.ShapeDtypeStruct((CONV_WIDTH, w), F32)]
    def part_spec(k):
        r, c = shapes[k]
        if parts[k].ndim == 3:
            return pl.BlockSpec((None, r, c), lambda i: (i, 0, 0))
        return pl.BlockSpec((r, c), lambda i: (i, 0))

    outs = _run(
        body, comm, name="small_reduce", grid=(nparts,), out_shape=out_shape,
        in_specs=[part_spec(k) for k in SMALL_PARTS] + [_whole(a.shape) for a in (lr_col, li_col, ldt_col, b_re, b_im)],
        out_specs=[_acc(s.shape) for s in out_shape],
        scratch_shapes=[pltpu.VMEM(shapes[k], F32) for k in SMALL_PARTS],
        semantics="arbitrary",
    )(*[parts[k] for k in SMALL_PARTS], lr_col, li_col, ldt_col, b_re, b_im)
    extra = None
    if comm is not None:
        outs, extra = outs
    names = ["loss", "g_mix", "b_in", "b_glu", "s5_d", "conv_b", "b_r", "b_i", "lru_lambda", "g_ffn", "g_ple_gate",
             "b_ple_gate", "g_ple", "g_final", "w_r", "w_i", "s5_c_re", "s5_c_im", "s5_b_re", "s5_b_im", "lam_re",
             "lam_im", "log_dt", "conv_w"]
    res = dict(zip(names, outs))
    return res if comm is None else (res, extra)


def _adamw_small(ws, gs, ms, vs):
    n = len(ws)

    def body(*refs):
        w_r, g_r, m_r, v_r = (refs[i * n:(i + 1) * n] for i in range(4))
        g_o, d_o, m_o, v_o = (refs[(4 + i) * n:(5 + i) * n] for i in range(4))
        for i in range(n):
            g = g_r[i][...]
            delta, m_new, v_new = _adamw_math(w_r[i][...], g, m_r[i][...], v_r[i][...])
            g_o[i][...] = g
            d_o[i][...] = delta
            m_o[i][...] = m_new
            v_o[i][...] = v_new

    shapes = [jax.ShapeDtypeStruct(a.shape, F32) for a in ws]
    outs = pl.pallas_call(body, name="adamw_small", out_shape=shapes * 4)(*ws, *gs, *ms, *vs)
    return outs[:n], outs[n:2 * n], outs[2 * n:3 * n], outs[3 * n:]


def _adamw_math(w, g, m, v):
    m_new = ADAM_B1 * m + (1.0 - ADAM_B1) * g
    v_new = ADAM_B2 * v + (1.0 - ADAM_B2) * (g * g)
    m_hat = m_new / (1.0 - ADAM_B1 ** ADAM_STEP)
    v_hat = v_new / (1.0 - ADAM_B2 ** ADAM_STEP)
    delta = -ADAM_LR * (m_hat / (jnp.sqrt(v_hat) + ADAM_EPS) + ADAM_WD * w)
    return delta, m_new, v_new


def _row_tile(rows):
    for cand in (256, 128, 64, 32, 16, 8):
        if rows % cand == 0:
            return cand
    return rows


def _adamw(parts, w, m, v, name, transposed=False):
    rows, cols = w.shape
    npart = parts.shape[0]
    tr = _row_tile(rows)
    if transposed:
        parts_spec = pl.BlockSpec((npart, cols, tr), lambda i: (0, 0, i))
    else:
        parts_spec = pl.BlockSpec((npart, tr, cols), lambda i: (0, i, 0))

    def body(p_ref, w_ref, m_ref, v_ref, g_ref, d_ref, mo_ref, vo_ref):
        g = p_ref[0].astype(F32)
        for k in range(1, npart):
            g = g + p_ref[k].astype(F32)
        if transposed:
            g = g.T
        delta, m_new, v_new = _adamw_math(w_ref[...], g, m_ref[...], v_ref[...])
        g_ref[...] = g
        d_ref[...] = delta
        mo_ref[...] = m_new
        vo_ref[...] = v_new

    return pl.pallas_call(
        body, name=name, grid=(rows // tr,),
        out_shape=[jax.ShapeDtypeStruct((rows, cols), F32)] * 4,
        in_specs=[parts_spec] + [_rows(tr, cols)] * 3,
        out_specs=[_rows(tr, cols)] * 4,
        compiler_params=_params("parallel"),
    )(parts, w, m, v)


def _mesh_position():
    return lax.axis_index("x"), lax.axis_index("y"), lax.axis_index("c")


def _flip(pos, rel):
    x, y, c = pos
    return (1 - x if rel & 4 else x, 1 - y if rel & 2 else y, 1 - c if rel & 1 else c)


def _index(pos):
    return 4 * pos[0] + 2 * pos[1] + pos[2]


_ANY = pl.BlockSpec(memory_space=pl.ANY)
FLAT_ROWS = 32


def _dma_sems(n):
    return [pltpu.SemaphoreType.DMA((n, N_DEV - 1)), pltpu.SemaphoreType.DMA((n, N_DEV - 1)), pltpu.SemaphoreType.DMA((n,))]


def _block_of(ref, idx, rows, flat):
    if flat:
        return ref.at[pl.ds(pl.multiple_of(idx * rows, FLAT_ROWS), rows), :]
    return ref.at[idx]


class _Gather:
    chips = (4, 2, 6)

    def __init__(self, shards):
        self.inputs = list(shards)
        self.flat = [s.shape[0] % FLAT_ROWS == 0 for s in shards]
        self.out_shape = [
            jax.ShapeDtypeStruct((N_DEV * s.shape[0], s.shape[1]) if f else (N_DEV,) + s.shape, s.dtype)
            for s, f in zip(shards, self.flat)]
        self.sems = _dma_sems(len(shards))

    def _copy(self, ins, outs, sems, i, k, block, to, own=False):
        dst = _block_of(outs[i], _index(block), self.inputs[i].shape[0], self.flat[i])
        return pltpu.make_async_remote_copy(
            src_ref=ins[i] if own else dst, dst_ref=dst, send_sem=sems[0].at[i, k], recv_sem=sems[1].at[i, k],
            device_id=to, device_id_type=MESH)

    def _local(self, ins, outs, sems, i, me):
        dst = _block_of(outs[i], _index(me), self.inputs[i].shape[0], self.flat[i])
        return pltpu.make_async_copy(ins[i], dst, sems[2].at[i])

    def _first(self, ins, outs, sems, i, me):
        cps = [self._copy(ins, outs, sems, i, 0, me, _flip(me, 1), own=True)]
        cps += [self._copy(ins, outs, sems, i, 1 + j, me, _flip(me, rel), own=True) for j, rel in enumerate(self.chips)]
        return cps

    def _passed(self, ins, outs, sems, i, j, me):
        return self._copy(ins, outs, sems, i, 4 + j, _flip(me, self.chips[j]), _flip(me, 1))

    def before(self, ins, outs, sems):
        n = len(self.inputs)
        me = _mesh_position()

        @pl.when(pl.program_id(0) == 0)
        def _():
            for i in range(n):
                self._local(ins, outs, sems, i, me).start()
                for cp in self._first(ins, outs, sems, i, me):
                    cp.start()

        @pl.when(pl.program_id(0) == pl.num_programs(0) - 1)
        def _():
            for j, rel in enumerate(self.chips):
                for i in range(n):
                    self._copy(ins, outs, sems, i, 1 + j, _flip(me, rel), me).wait_recv()
                    self._passed(ins, outs, sems, i, j, me).start()

    def after(self, ins, outs, sems):
        n = len(self.inputs)
        me = _mesh_position()
        sibling = _flip(me, 1)

        @pl.when(pl.program_id(0) == pl.num_programs(0) - 1)
        def _():
            for i in range(n):
                self._copy(ins, outs, sems, i, 0, sibling, me).wait_recv()
                for j, rel in enumerate(self.chips):
                    self._copy(ins, outs, sems, i, 4 + j, _flip(sibling, rel), me).wait_recv()
            for i in range(n):
                for cp in self._first(ins, outs, sems, i, me):
                    cp.wait_send()
                for j in range(len(self.chips)):
                    self._passed(ins, outs, sems, i, j, me).wait_send()
                self._local(ins, outs, sems, i, me).wait()


N_CHIPS = N_DEV // 2


def _chip(pos):
    return 2 * pos[0] + pos[1]


class _PairSwap:
    def __init__(self, arrays):
        self.inputs = list(arrays)
        self.rows = [a.shape[0] // N_DEV for a in arrays]
        for r in self.rows:
            assert r % FLAT_ROWS == 0, r
        self.out_shape = [jax.ShapeDtypeStruct((N_CHIPS, r, a.shape[1]), a.dtype) for a, r in zip(arrays, self.rows)]
        n = len(arrays)
        self.sems = [pltpu.SemaphoreType.DMA((n, N_CHIPS)), pltpu.SemaphoreType.DMA((n, N_CHIPS))]

    def _copy(self, ins, outs, sems, i, j, me):
        sibling = _flip(me, 1)
        return pltpu.make_async_remote_copy(
            src_ref=_block_of(ins[i], 2 * j + sibling[2], self.rows[i], True), dst_ref=outs[i].at[j],
            send_sem=sems[0].at[i, j], recv_sem=sems[1].at[i, j], device_id=sibling, device_id_type=MESH)

    def before(self, ins, outs, sems):
        me = _mesh_position()

        @pl.when(pl.program_id(0) == 0)
        def _():
            for i in range(len(self.inputs)):
                for j in range(N_CHIPS):
                    self._copy(ins, outs, sems, i, j, me).start()

    def after(self, ins, outs, sems):
        me = _mesh_position()

        @pl.when(pl.program_id(0) == pl.num_programs(0) - 1)
        def _():
            for i in range(len(self.inputs)):
                for j in range(N_CHIPS):
                    self._copy(ins, outs, sems, i, j, me).wait()


class _ChipExchange:
    chips = (4, 2, 6)

    def __init__(self, arrays):
        self.inputs = list(arrays)
        self.out_shape = [jax.ShapeDtypeStruct(a.shape, a.dtype) for a in arrays]
        n = len(arrays)
        self.sems = [pltpu.SemaphoreType.DMA((n, 3)), pltpu.SemaphoreType.DMA((n, 3)), pltpu.SemaphoreType.DMA((n,))]

    def _send(self, ins, outs, sems, i, k, me):
        peer = _flip(me, self.chips[k])
        return pltpu.make_async_remote_copy(
            src_ref=ins[i].at[_chip(peer)], dst_ref=outs[i].at[_chip(me)], send_sem=sems[0].at[i, k],
            recv_sem=sems[1].at[i, k], device_id=peer, device_id_type=MESH)

    def _arrival(self, ins, outs, sems, i, k, me):
        peer = _flip(me, self.chips[k])
        return pltpu.make_async_remote_copy(
            src_ref=ins[i].at[_chip(me)], dst_ref=outs[i].at[_chip(peer)], send_sem=sems[0].at[i, k],
            recv_sem=sems[1].at[i, k], device_id=peer, device_id_type=MESH)

    def _local(self, ins, outs, sems, i, me):
        return pltpu.make_async_copy(ins[i].at[_chip(me)], outs[i].at[_chip(me)], sems[2].at[i])

    def before(self, ins, outs, sems):
        me = _mesh_position()

        @pl.when(pl.program_id(0) == 0)
        def _():
            for i in range(len(self.inputs)):
                self._local(ins, outs, sems, i, me).start()
                for k in range(len(self.chips)):
                    self._send(ins, outs, sems, i, k, me).start()

    def after(self, ins, outs, sems):
        me = _mesh_position()

        @pl.when(pl.program_id(0) == pl.num_programs(0) - 1)
        def _():
            for i in range(len(self.inputs)):
                for k in range(len(self.chips)):
                    self._arrival(ins, outs, sems, i, k, me).wait_recv()
            for i in range(len(self.inputs)):
                for k in range(len(self.chips)):
                    self._send(ins, outs, sems, i, k, me).wait_send()
                self._local(ins, outs, sems, i, me).wait()


def _pair_add(grads, halves, name):
    n = len(grads)

    def body(*refs):
        g_refs, h_refs, o_refs = refs[:n], refs[n:2 * n], refs[2 * n:]
        c = lax.axis_index("c")
        for i in range(n):
            r = h_refs[i].shape[1]
            for j in range(N_CHIPS):
                own = g_refs[i][pl.ds(pl.multiple_of((2 * j + c) * r, FLAT_ROWS), r), :]
                o_refs[i][j] = (own.astype(F32) + h_refs[i][j].astype(F32)).astype(o_refs[i].dtype)

    return pl.pallas_call(
        body, name=name, out_shape=[jax.ShapeDtypeStruct(h.shape, h.dtype) for h in halves],
        compiler_params=pltpu.CompilerParams(vmem_limit_bytes=VMEM_LIMIT),
    )(*grads, *halves)


class _Both:
    def __init__(self, first, second):
        self.jobs = (first, second)
        self.inputs = first.inputs + second.inputs
        self.out_shape = first.out_shape + second.out_shape
        self.sems = first.sems + second.sems

    def _each(self, ins, outs, sems):
        a = self.jobs[0]
        i, o, s = len(a.inputs), len(a.out_shape), len(a.sems)
        return ((a, ins[:i], outs[:o], sems[:s]), (self.jobs[1], ins[i:], outs[o:], sems[s:]))

    def before(self, ins, outs, sems):
        for job, i, o, s in self._each(ins, outs, sems):
            job.before(i, o, s)

    def after(self, ins, outs, sems):
        for job, i, o, s in self._each(ins, outs, sems):
            job.after(i, o, s)


def _run(body, comm, *, semantics, out_shape, in_specs, out_specs, scratch_shapes=(), **kw):
    if comm is None:
        return pl.pallas_call(body, out_shape=out_shape, in_specs=in_specs, out_specs=out_specs,
                              scratch_shapes=list(scratch_shapes), compiler_params=_params(semantics), **kw)
    single = not isinstance(out_shape, (list, tuple))
    outs = [out_shape] if single else list(out_shape)
    ospecs = [out_specs] if single else list(out_specs)
    counts = [len(in_specs), len(comm.inputs), len(outs), len(comm.out_shape), len(scratch_shapes), len(comm.sems)]

    def carrying(*refs):
        groups, pos = [], 0
        for c in counts:
            groups.append(refs[pos:pos + c])
            pos += c
        main_in, comm_in, main_out, comm_out, main_scratch, comm_sems = groups
        comm.before(comm_in, comm_out, comm_sems)
        body(*main_in, *main_out, *main_scratch)
        comm.after(comm_in, comm_out, comm_sems)

    call = pl.pallas_call(
        carrying, out_shape=outs + list(comm.out_shape), in_specs=list(in_specs) + [_ANY] * len(comm.inputs),
        out_specs=ospecs + [_ANY] * len(comm.out_shape), scratch_shapes=list(scratch_shapes) + list(comm.sems),
        compiler_params=_params("arbitrary"), **kw)

    def apply(*args):
        res = call(*args, *comm.inputs)
        main = res[:len(outs)]
        return (main[0] if single else list(main)), list(res[len(outs):])

    return apply


def _alone(comm, name):
    return _run(lambda: None, comm, semantics="arbitrary", name=name, grid=(1,), out_shape=[], in_specs=[], out_specs=[])()[1]


SHARDED = {"w_in": 1, "w_glu": 0, "conv_w": 1, "w_a_out": 1, "w_b_out": 0, "w_o": 0, "w_ffn_gate": 1, "w_ffn_up": 1,
           "w_ffn_down": 0, "w_ple_gate": 0, "w_ple": 1}
TRANSPOSED = ("w_in", "w_a_out", "w_ffn_gate", "w_ffn_up", "w_ple")
SMALL = ["g_mix", "b_in", "lam_re", "lam_im", "log_dt", "s5_b_re", "s5_b_im", "s5_c_re", "s5_c_im", "s5_d", "b_glu",
         "conv_b", "w_r", "b_r", "w_i", "b_i", "lru_lambda", "g_ffn", "g_ple_gate", "b_ple_gate", "g_ple", "g_final"]
WEIGHTS = ["g_mix", "w_in", "b_in", "lam_re", "lam_im", "log_dt", "s5_b_re", "s5_b_im", "s5_c_re", "s5_c_im", "s5_d",
           "w_glu", "b_glu", "conv_w", "conv_b", "w_r", "b_r", "w_i", "b_i", "lru_lambda", "w_a_out", "w_b_out", "w_o",
           "g_ffn", "w_ffn_gate", "w_ffn_up", "w_ffn_down", "g_ple_gate", "w_ple_gate", "b_ple_gate", "w_ple", "g_ple",
           "g_final"]


def _unblock(gathered, axis):
    nb, r, c = gathered.shape
    if axis == 0:
        return gathered.reshape(nb * r, c)
    return jnp.transpose(gathered, (1, 0, 2)).reshape(r, nb * c)


def _disc_scalars(lr, li, ldt):
    dt = jnp.exp(ldt)
    mag = jnp.exp(lr * dt)
    ar = mag * jnp.cos(li * dt)
    ai = mag * jnp.sin(li * dt)
    den = lr * lr + li * li
    nr = ar - 1.0
    fr = (nr * lr + ai * li) / den
    fi = (ai * lr - nr * li) / den
    return ar, ai, fr, fi


def _disc_cols(lr, li, ldt, b_re, b_im):
    ar, ai, fr, fi = _disc_scalars(lr, li, ldt)
    return ar, ai, fr * b_re - fi * b_im, fr * b_im + fi * b_re


def _local_step(x, p, target, src, small, disc, distributed=True):
    full = {} if distributed else dict(src)
    gw, halves, pairs, got = {}, {}, {}, {}

    def gather(keys):
        return (_Gather([src[k] for k in keys]), keys, full) if distributed else None

    def swap(keys):
        return (_PairSwap([gw[k] for k in keys]), keys, halves) if distributed else None

    def chips(keys):
        return (_ChipExchange([pairs[k] for k in keys]), keys, got) if distributed else None

    def add_pairs(keys):
        if distributed:
            pairs.update(zip(keys, _pair_add([gw[k] for k in keys], [halves[k] for k in keys], "pair_add_" + keys[0])))

    def carry(fn, *args, jobs=()):
        jobs = [j for j in jobs if j is not None]
        if not jobs:
            return fn(*args)
        comm = jobs[0][0]
        for j in jobs[1:]:
            comm = _Both(comm, j[0])
        res, extra = fn(*args, comm=comm)
        for job, keys, sink in jobs:
            sink.update(zip(keys, extra[:len(job.out_shape)]))
            extra = extra[len(job.out_shape):]
        return res

    d = x.shape[1]
    g, n, pch = small["s5_b_re"].shape
    heads = small["w_r"].shape[0]
    sa, lw = g * pch, small["lru_lambda"].shape[-1]
    widths = [sa, lw, d, d]
    row = lambda v: v.reshape(1, -1)

    hd = small["w_r"].shape[-1]
    ar_row, ai_row, bbr_blk, bbi_blk, cre_blk, cimn_blk, wr_blk, wi_blk = carry(
        _prep, *disc["rows"], *disc["cols"], disc["b_re"], disc["b_im"], small["s5_c_re"].reshape(sa, n),
        small["s5_c_im"].reshape(sa, n), small["w_r"].reshape(lw, hd), small["w_i"].reshape(lw, hd),
        jobs=[gather(["w_in"])])
    d_row = row(small["s5_d"])
    u_a, u_b, za, zb = carry(_inproj_fwd, x, row(small["g_mix"]), full["w_in"], row(small["b_in"]), widths,
                             jobs=[gather(["w_glu", "conv_w", "w_a_out"])])
    conv_w = _unblock(full["conv_w"], 1) if distributed else full["conv_w"]
    sr, si, y, y_a = carry(_s5_fwd, u_a, bbr_blk, bbi_blk, ar_row, ai_row, cre_blk, cimn_blk, d_row, full["w_glu"],
                           row(small["b_glu"]), jobs=[gather(["w_ffn_gate", "w_b_out"])])
    xc, h, hprev = carry(_lru_fwd, u_b, conv_w, row(small["conv_b"]), wr_blk, row(small["b_r"]), wi_blk,
                         row(small["b_i"]), row(small["lru_lambda"]), jobs=[gather(["w_ffn_up", "w_o"])])
    x1, merged, ma, mb = _merge_fwd(y_a, h, za, zb, x, full["w_a_out"], full["w_b_out"], full["w_o"])
    fg, fu = carry(_ffn_up_fwd, x1, row(small["g_ffn"]), full["w_ffn_gate"], full["w_ffn_up"],
                   jobs=[gather(["w_ffn_down"])])
    x2 = carry(_ffn_down_fwd, fg, fu, x1, full["w_ffn_down"], jobs=[gather(["w_ple_gate", "w_ple"])])

    dx2, loss_blk, gw["w_ple_gate"], gw["w_ple"], vec_tail = _tail_fwd_bwd(
        x2, p, target, row(small["g_ple_gate"]), full["w_ple_gate"], row(small["b_ple_gate"]), full["w_ple"],
        row(small["g_ple"]), row(small["g_final"]))
    dfg, dfu, act = carry(_ffn_bwd_a, dx2, fg, fu, full["w_ffn_down"], jobs=[swap(["w_ple_gate", "w_ple"])])
    tn_d = min(d, 512)
    gw["w_ffn_down"] = _matmul_tn(act, dx2, tn_d, "dw_ffn_down", BF16)
    add_pairs(["w_ple_gate", "w_ple"])
    dx1, h2, vec_ffn = carry(_ffn_bwd_b, dfg, dfu, x1, dx2, row(small["g_ffn"]), full["w_ffn_gate"], full["w_ffn_up"],
                             jobs=[chips(["w_ple_gate", "w_ple"]), swap(["w_ffn_down"])])
    gw["w_ffn_gate"] = _matmul_tn(dfg, h2, tn_d, "dw_ffn_gate", BF16)
    gw["w_ffn_up"] = _matmul_tn(dfu, h2, tn_d, "dw_ffn_up", BF16)
    add_pairs(["w_ffn_down"])
    dza, dzb, dya, dyb, gw["w_o"], gw["w_a_out"], gw["w_b_out"] = carry(
        _merge_bwd, dx1, merged, ma, mb, za, zb, y_a, h, full["w_o"], full["w_a_out"], full["w_b_out"],
        jobs=[chips(["w_ffn_down"]), swap(["w_ffn_gate", "w_ffn_up"])])
    add_pairs(["w_ffn_gate", "w_ffn_up"])
    du_b, dw_r, dw_i, vec_lru = carry(
        _lru_bwd, dyb, xc, hprev, u_b, conv_w, wr_blk, row(small["b_r"]), wi_blk, row(small["b_i"]),
        row(small["lru_lambda"]), hd, jobs=[chips(["w_ffn_gate", "w_ffn_up"]), swap(["w_o", "w_a_out", "w_b_out"])])
    add_pairs(["w_o", "w_a_out", "w_b_out"])
    du_a, lam_r, lam_i, dy16, gw["w_glu"], vec_sa, vec_gn = carry(
        _s5_bwd, dya, y, sr, si, u_a, full["w_glu"], row(small["b_glu"]), cre_blk, cimn_blk, bbr_blk, bbi_blk, ar_row,
        ai_row, d_row, jobs=[chips(["w_o", "w_a_out", "w_b_out"])])
    smalls = {"vec_tail": vec_tail, "vec_ffn": vec_ffn, "vec_lru": vec_lru, "vec_sa": vec_sa, "vec_gn": vec_gn,
              "dw_r": dw_r, "dw_i": dw_i, "loss": loss_blk}
    everyones = {}

    def gather_smalls(keys):
        return (_Gather([smalls[k] for k in keys]), keys, everyones) if distributed else None

    smalls["dbb_re"], smalls["dbb_im"], smalls["dc_re"], smalls["dc_imn"] = carry(
        _s5_param_grads, lam_r, lam_i, sr, si, u_a, dy16, pch, n, jobs=[swap(["w_glu"]), gather_smalls(list(smalls))])
    add_pairs(["w_glu"])
    grad_x, h0, dz16, smalls["vec_mix"], smalls["vec_bin"] = _inproj_bwd(
        [du_a, du_b, dza, dzb], x, dx1, row(small["g_mix"]), full["w_in"])
    shapes = {k: a.shape for k, a in smalls.items()}
    gw["w_in"] = carry(_matmul_tn, dz16, h0, tn_d, "dw_in", BF16,
                       jobs=[chips(["w_glu"]), gather_smalls(["dbb_re", "dbb_im", "dc_re", "dc_imn"])])
    smalls.update(everyones)
    if distributed:
        got["w_in"] = gw["w_in"]
        gw = got
    return grad_x, gw, smalls, shapes


def _disc_inputs(small):
    g, n, pch = small["s5_b_re"].shape
    srcs = (small["lam_re"], small["lam_im"], jnp.repeat(small["log_dt"], n))
    return {"rows": [a.reshape(1, g * n) for a in srcs], "cols": [a.reshape(g * n, 1) for a in srcs],
            "b_re": small["s5_b_re"].reshape(g * n, pch), "b_im": small["s5_b_im"].reshape(g * n, pch)}


def kernel(x, p, g_mix, w_in, b_in, lam_re, lam_im, log_dt, s5_b_re, s5_b_im, s5_c_re, s5_c_im, s5_d, w_glu, b_glu, conv_w, conv_b, w_r, b_r, w_i, b_i, lru_lambda, w_a_out, w_b_out, w_o, g_ffn, w_ffn_gate, w_ffn_up, w_ffn_down, g_ple_gate, w_ple_gate, b_ple_gate, w_ple, g_ple, g_final, loss_target, m_g_mix, m_w_in, m_b_in, m_lam_re, m_lam_im, m_log_dt, m_s5_b_re, m_s5_b_im, m_s5_c_re, m_s5_c_im, m_s5_d, m_w_glu, m_b_glu, m_conv_w, m_conv_b, m_w_r, m_b_r, m_w_i, m_b_i, m_lru_lambda, m_w_a_out, m_w_b_out, m_w_o, m_g_ffn, m_w_ffn_gate, m_w_ffn_up, m_w_ffn_down, m_g_ple_gate, m_w_ple_gate, m_b_ple_gate, m_w_ple, m_g_ple, m_g_final, v_g_mix, v_w_in, v_b_in, v_lam_re, v_lam_im, v_log_dt, v_s5_b_re, v_s5_b_im, v_s5_c_re, v_s5_c_im, v_s5_d, v_w_glu, v_b_glu, v_conv_w, v_conv_b, v_w_r, v_b_r, v_w_i, v_b_i, v_lru_lambda, v_w_a_out, v_w_b_out, v_w_o, v_g_ffn, v_w_ffn_gate, v_w_ffn_up, v_w_ffn_down, v_g_ple_gate, v_w_ple_gate, v_b_ple_gate, v_w_ple, v_g_ple, v_g_final):
    given = dict(locals())
    wts = {k: given[k] for k in WEIGHTS}
    moms = {k: given["m_" + k] for k in WEIGHTS}
    vels = {k: given["v_" + k] for k in WEIGHTS}

    def drop_depth(k, a):
        return a if k == "g_final" else a[0]

    small = {k: drop_depth(k, wts[k]) for k in SMALL}
    shard = {k: wts[k][0] for k in SHARDED}
    names = list(SHARDED)

    def wire(k):
        if k == "conv_w":
            return shard[k]
        return (shard[k].T if k in TRANSPOSED else shard[k]).astype(BF16)

    disc = _disc_inputs(small)
    grad_x, parts, smalls, shapes = _local_step(x[0], p[0, 0], loss_target[0], {k: wire(k) for k in names}, small, disc)

    late = ["vec_mix", "vec_bin"]
    received = _alone(_Both(_PairSwap([parts["w_in"]]), _Gather([smalls[k] for k in late])), "swap_last")
    (pair_in,) = _pair_add([parts["w_in"]], received[:1], "pair_add_w_in")
    smalls.update(zip(late, received[1:]))

    g_small, (parts["w_in"],) = _small_reduce(smalls, shapes, *disc["cols"], disc["b_re"], disc["b_im"],
                                              small["s5_b_re"].shape[0], comm=_ChipExchange([pair_in]))
    loss = g_small.pop("loss")[0, 0]
    cols = shard["conv_w"].shape[1]
    mine = _index((lax.axis_index("x"), lax.axis_index("y"), lax.axis_index("c")))
    parts["conv_w"] = lax.dynamic_slice_in_dim(g_small.pop("conv_w"), mine * cols, cols, axis=1)[None]
    natural = lambda k, a: a.reshape((1, -1) if k == "g_final" else wts[k].shape)
    slots = _adamw_small([natural(k, wts[k]) for k in SMALL], [natural(k, g_small[k]) for k in SMALL],
                         [natural(k, moms[k]) for k in SMALL], [natural(k, vels[k]) for k in SMALL])
    small_out = [dict(zip(SMALL, [a.reshape(wts[k].shape) for k, a in zip(SMALL, slot)])) for slot in slots]

    big_out = {}
    for k in names:
        big_out[k] = _adamw(parts[k], shard[k], moms[k][0], vels[k][0], "adamw_" + k, transposed=k in TRANSPOSED)

    outs = [loss, grad_x[None]]
    for slot in range(4):
        for k in WEIGHTS:
            if k in SHARDED:
                outs.append(big_out[k][slot][None])
            else:
                outs.append(small_out[slot][k])
    return tuple(outs)
```

```python
import math

import jax
import jax.numpy as jnp
from jax import lax
from jax.experimental import pallas as pl
from jax.experimental.pallas import tpu as pltpu

F32 = jnp.float32
BF16 = jnp.bfloat16

EPS = 1e-6
LRU_C = 8.0
CONV_WIDTH = 4
ADAM_LR = 0.001
ADAM_B1 = 0.9
ADAM_B2 = 0.999
ADAM_EPS = 1e-08
ADAM_WD = 0.01
ADAM_STEP = 10

N_DEV = 8
MESH = pl.DeviceIdType.MESH
SUBLANES = 8
LANES = 128
VMEM_LIMIT = 56 * 1024 * 1024
TOKEN_TILE = 256
TIME_CHUNK = 256
S5_SLAB = 128
LRU_SLAB = 256


def _dot(a, b):
    return jnp.dot(a.astype(BF16), b.astype(BF16), preferred_element_type=F32)


def _dot_nt(a, b):
    return lax.dot_general(a.astype(BF16), b.astype(BF16), (((1,), (1,)), ((), ())), preferred_element_type=F32)


def _dot_tn(a, b):
    return lax.dot_general(a.astype(BF16), b.astype(BF16), (((0,), (0,)), ((), ())), preferred_element_type=F32)


def _sigmoid(x):
    return jax.nn.sigmoid(x)


def _rms_stats(x):
    r = lax.rsqrt(jnp.mean(x * x, axis=-1, keepdims=True) + EPS)
    return x * r, r


def _rms_bwd(dy, xhat, r, g):
    dxn = dy * g
    dx = r * (dxn - xhat * jnp.mean(dxn * xhat, axis=-1, keepdims=True))
    return dx, dy * xhat


def _rowsum(v):
    return jnp.sum(v, axis=0, keepdims=True)


def _expm1(x):
    u = jnp.exp(x)
    um1 = u - 1.0
    safe = jnp.where(um1 == 0.0, 1.0, jnp.log(u))
    return jnp.where(um1 == 0.0, x, um1 * x / safe)


def _softplus(x):
    e = jnp.exp(-jnp.abs(x))
    u = 1.0 + e
    um1 = u - 1.0
    safe = jnp.where(um1 == 0.0, 1.0, um1)
    log1p_e = jnp.where(um1 == 0.0, e, jnp.log(u) * e / safe)
    return jnp.maximum(x, 0.0) + log1p_e


_GELU_K = math.sqrt(2.0 / math.pi)
_GELU_C = 0.044715


def _gelu(x):
    return 0.5 * x * (1.0 + jnp.tanh(_GELU_K * (x + _GELU_C * x * x * x)))


def _gelu_grad(x):
    th = jnp.tanh(_GELU_K * (x + _GELU_C * x * x * x))
    return 0.5 * (1.0 + th) + 0.5 * x * (1.0 - th * th) * _GELU_K * (1.0 + 3.0 * _GELU_C * x * x)


def _params(*sem):
    return pltpu.CompilerParams(dimension_semantics=sem, vmem_limit_bytes=VMEM_LIMIT)


def _rows(tm, n):
    return pl.BlockSpec((tm, n), lambda i: (i, 0))


def _rows_rev(tm, n, steps):
    return pl.BlockSpec((tm, n), lambda i: (steps - 1 - i, 0))


def _whole(shape):
    nd = len(shape)
    return pl.BlockSpec(shape, lambda i: (0,) * nd, pipeline_mode=pl.Buffered(1))


def _acc(shape):
    nd = len(shape)
    return pl.BlockSpec(shape, lambda i: (0,) * nd)


def _zero_on_first(*refs):
    @pl.when(pl.program_id(0) == 0)
    def _():
        for r in refs:
            r[...] = jnp.zeros_like(r)


def _inproj_fwd(x, g_mix, w_in_t, b_in, widths, comm=None):
    t, d = x.shape
    n = w_in_t.shape[0]
    tm = min(TOKEN_TILE, t)
    offs = [sum(widths[:i]) for i in range(len(widths) + 1)]

    def body(x_ref, g_ref, w_ref, b_ref, *outs):
        xhat, _ = _rms_stats(x_ref[...])
        h = (xhat * g_ref[...]).astype(BF16)
        for k, o_ref in enumerate(outs):
            lo, hi = offs[k], offs[k + 1]
            o_ref[...] = _dot_nt(h, w_ref[lo:hi, :]) + b_ref[:, lo:hi]

    return _run(
        body, comm, name="inproj_fwd", grid=(t // tm,),
        out_shape=[jax.ShapeDtypeStruct((t, w), F32) for w in widths],
        in_specs=[_rows(tm, d), _whole((1, d)), _whole((n, d)), _whole((1, n))],
        out_specs=[_rows(tm, w) for w in widths],
        semantics="parallel",
    )(x, g_mix, w_in_t, b_in)


def _s5_fwd(u, bbr_blk, bbi_blk, ar, ai, cre_blk, cimn_blk, d_skip, w_glu, b_glu, comm=None):
    t, sa = u.shape
    ns, _, sw = bbr_blk.shape
    gn = ns * sw
    tc = min(TIME_CHUNK, t)

    def body(u_ref, bbr_ref, bbi_ref, ar_ref, ai_ref, cre_ref, cim_ref, d_ref, wg_ref, bg_ref,
             sr_ref, si_ref, y_ref, ya_ref, cr_s, ci_s):
        _zero_on_first(cr_s, ci_s)
        uv = u_ref[...]
        ub = uv.astype(BF16)
        for m in range(ns):
            um = ub[:, m * S5_SLAB:(m + 1) * S5_SLAB]
            sr_ref[:, m * sw:(m + 1) * sw] = _dot(um, bbr_ref[m])
            si_ref[:, m * sw:(m + 1) * sw] = _dot(um, bbi_ref[m])
        a_r = ar_ref[...]
        a_i = ai_ref[...]

        def step(row, carry):
            c_r, c_i = carry
            at = pl.ds(row, 1)
            n_r = a_r * c_r - a_i * c_i + sr_ref[at, :]
            n_i = a_r * c_i + a_i * c_r + si_ref[at, :]
            sr_ref[at, :] = n_r
            si_ref[at, :] = n_i
            return n_r, n_i

        c_r, c_i = lax.fori_loop(0, tc, step, (cr_s[0:1, :], ci_s[0:1, :]), unroll=8)
        cr_s[0:1, :] = c_r
        ci_s[0:1, :] = c_i
        for m in range(ns):
            states, chans = slice(m * sw, (m + 1) * sw), slice(m * S5_SLAB, (m + 1) * S5_SLAB)
            y_ref[:, chans] = (_dot(sr_ref[:, states], cre_ref[m]) + _dot(si_ref[:, states], cim_ref[m])
                               + d_ref[:, chans] * uv[:, chans])
        y = y_ref[...]
        zz = _gelu(y)
        q = _dot(zz, wg_ref[...]) + bg_ref[...]
        ya_ref[...] = zz * _sigmoid(q)

    return _run(
        body, comm, name="s5_fwd", grid=(t // tc,),
        out_shape=[jax.ShapeDtypeStruct((t, gn), F32), jax.ShapeDtypeStruct((t, gn), F32),
                   jax.ShapeDtypeStruct((t, sa), F32), jax.ShapeDtypeStruct((t, sa), F32)],
        in_specs=[_rows(tc, sa), _whole(bbr_blk.shape), _whole(bbi_blk.shape), _whole((1, gn)), _whole((1, gn)),
                  _whole(cre_blk.shape), _whole(cimn_blk.shape), _whole((1, sa)), _whole((sa, sa)), _whole((1, sa))],
        out_specs=[_rows(tc, gn), _rows(tc, gn), _rows(tc, sa), _rows(tc, sa)],
        scratch_shapes=[pltpu.VMEM((SUBLANES, gn), F32), pltpu.VMEM((SUBLANES, gn), F32)],
        semantics="arbitrary",
    )(u, bbr_blk, bbi_blk, ar, ai, cre_blk, cimn_blk, d_skip, w_glu, b_glu)


def _slab_dot(x, w_ref, transposed=False):
    dot = _dot_nt if transposed else _dot
    xb = x.astype(BF16)
    return jnp.concatenate([dot(xb[:, j * LRU_SLAB:(j + 1) * LRU_SLAB], w_ref[j]) for j in range(w_ref.shape[0])], axis=1)


def _lru_gates(xc, wr_ref, br_ref, wi_ref, bi_ref, lam_ref):
    r = _sigmoid(_slab_dot(xc, wr_ref) + br_ref[...])
    ig = _sigmoid(_slab_dot(xc, wi_ref) + bi_ref[...])
    sp = _softplus(-lam_ref[...])
    log_a = (-LRU_C * r) * sp
    return r, ig, sp, log_a


def _lru_fwd(u, conv_w, conv_b, wr_blk, b_r, wi_blk, b_i, lru_lambda, comm=None):
    t, w = u.shape
    tc = min(TIME_CHUNK, t)
    halo = SUBLANES

    def body(u_ref, cw_ref, cb_ref, wr_ref, br_ref, wi_ref, bi_ref, lam_ref,
             xc_ref, h_ref, hp_ref, ext_s, a_s, carry_s):
        @pl.when(pl.program_id(0) == 0)
        def _():
            ext_s[0:halo, :] = jnp.zeros((halo, w), F32)
            carry_s[...] = jnp.zeros_like(carry_s)

        ext_s[halo:halo + tc, :] = u_ref[...]
        xc = cb_ref[...]
        for k in range(CONV_WIDTH):
            off = halo - (CONV_WIDTH - 1) + k
            xc = xc + cw_ref[k:k + 1, :] * ext_s[off:off + tc, :]
        ext_s[0:halo, :] = ext_s[tc:tc + halo, :]
        xc_ref[...] = xc
        r, ig, sp, log_a = _lru_gates(xc, wr_ref, br_ref, wi_ref, bi_ref, lam_ref)
        a_s[...] = jnp.exp(log_a)
        h_ref[...] = jnp.sqrt(-_expm1(2.0 * log_a)) * ig * xc

        def step(row, carry):
            at = pl.ds(row, 1)
            hp_ref[at, :] = carry
            nxt = a_s[at, :] * carry + h_ref[at, :]
            h_ref[at, :] = nxt
            return nxt

        carry_s[0:1, :] = lax.fori_loop(0, tc, step, carry_s[0:1, :], unroll=8)

    return _run(
        body, comm, name="lru_fwd", grid=(t // tc,),
        out_shape=[jax.ShapeDtypeStruct((t, w), F32)] * 3,
        in_specs=[_rows(tc, w), _whole((CONV_WIDTH, w)), _whole((1, w)), _whole(wr_blk.shape), _whole((1, w)),
                  _whole(wi_blk.shape), _whole((1, w)), _whole((1, w))],
        out_specs=[_rows(tc, w)] * 3,
        scratch_shapes=[pltpu.VMEM((halo + tc, w), F32), pltpu.VMEM((tc, w), F32), pltpu.VMEM((SUBLANES, w), F32)],
        semantics="arbitrary",
    )(u, conv_w, conv_b, wr_blk, b_r, wi_blk, b_i, lru_lambda)


def _merge_fwd(y_a, h, za, zb, x, w_a_out_t, w_b_out, w_o, comm=None):
    t, d = x.shape
    sa, lw = y_a.shape[1], h.shape[1]
    tm = min(TOKEN_TILE, t)

    def body(ya_ref, h_ref, za_ref, zb_ref, x_ref, wa_ref, wb_ref, wo_ref, x1_ref, mg_ref, ma_ref, mb_ref):
        ma = _dot_nt(ya_ref[...], wa_ref[...])
        mb = _dot(h_ref[...], wb_ref[...])
        merged = _sigmoid(za_ref[...]) * ma + _sigmoid(zb_ref[...]) * mb
        ma_ref[...] = ma
        mb_ref[...] = mb
        mg_ref[...] = merged.astype(mg_ref.dtype)
        x1_ref[...] = x_ref[...] + _dot(merged, wo_ref[...])

    return _run(
        body, comm, name="merge_fwd", grid=(t // tm,),
        out_shape=[jax.ShapeDtypeStruct((t, d), F32), jax.ShapeDtypeStruct((t, d), BF16),
                   jax.ShapeDtypeStruct((t, d), F32), jax.ShapeDtypeStruct((t, d), F32)],
        in_specs=[_rows(tm, sa), _rows(tm, lw), _rows(tm, d), _rows(tm, d), _rows(tm, d),
                  _whole((d, sa)), _whole((lw, d)), _whole((d, d))],
        out_specs=[_rows(tm, d)] * 4,
        semantics="parallel",
    )(y_a, h, za, zb, x, w_a_out_t, w_b_out, w_o)


def _ffn_up_fwd(x1, g_ffn, w_gate_t, w_up_t, comm=None):
    t, d = x1.shape
    f = w_gate_t.shape[0]
    tm = min(TOKEN_TILE, t)

    def body(x_ref, g_ref, wg_ref, wu_ref, fg_ref, fu_ref):
        xhat, _ = _rms_stats(x_ref[...])
        h2 = (xhat * g_ref[...]).astype(BF16)
        fg_ref[...] = _dot_nt(h2, wg_ref[...])
        fu_ref[...] = _dot_nt(h2, wu_ref[...])

    return _run(
        body, comm, name="ffn_up_fwd", grid=(t // tm,),
        out_shape=[jax.ShapeDtypeStruct((t, f), F32)] * 2,
        in_specs=[_rows(tm, d), _whole((1, d)), _whole((f, d)), _whole((f, d))],
        out_specs=[_rows(tm, f)] * 2,
        semantics="parallel",
    )(x1, g_ffn, w_gate_t, w_up_t)


def _ffn_down_fwd(fg, fu, x1, w_down, comm=None):
    t, d = x1.shape
    f = fg.shape[1]
    tm = min(TOKEN_TILE, t)

    def body(fg_ref, fu_ref, x_ref, wd_ref, x2_ref):
        fgv = fg_ref[...]
        act = fgv * _sigmoid(fgv) * fu_ref[...]
        x2_ref[...] = x_ref[...] + _dot(act, wd_ref[...])

    return _run(
        body, comm, name="ffn_down_fwd", grid=(t // tm,),
        out_shape=jax.ShapeDtypeStruct((t, d), F32),
        in_specs=[_rows(tm, f), _rows(tm, f), _rows(tm, d), _whole((f, d))],
        out_specs=_rows(tm, d),
        semantics="parallel",
    )(fg, fu, x1, w_down)


def _store_on_last(pairs):
    @pl.when(pl.program_id(0) == pl.num_programs(0) - 1)
    def _():
        for acc, out in pairs:
            out[...] = acc[...].astype(out.dtype)


def _tail_fwd_bwd(x2, p, target, g_pg, w_pg, b_pg, w_ple_t, g_ple, g_final):
    t, d = x2.shape
    pd = p.shape[1]
    tm = min(TOKEN_TILE, t)

    def body(x2_ref, p_ref, tg_ref, gpg_ref, wpg_ref, bpg_ref, wple_ref, gple_ref, gfin_ref,
             dx2_ref, loss_ref, dwpg_out, dwple_out, vec_ref, dwpg_ref, dwple_ref):
        _zero_on_first(loss_ref, dwpg_ref, dwple_ref, vec_ref)
        x2v = x2_ref[...]
        xh2, r2 = _rms_stats(x2v)
        h3 = xh2 * gpg_ref[...]
        gp = _sigmoid(_dot(h3, wpg_ref[...]) + bpg_ref[...])
        pe = _dot_nt(p_ref[...], wple_ref[...])
        peh, r3 = _rms_stats(pe)
        e = peh * gple_ref[...]
        x3 = x2v + gp * e
        xh3, r4 = _rms_stats(x3)
        diff = xh3 * gfin_ref[...] - tg_ref[...]
        loss_ref[...] += 0.5 * jnp.sum(jnp.mean(diff * diff, axis=-1, keepdims=True))
        dy = diff * (1.0 / d)
        dx3, dgfin = _rms_bwd(dy, xh3, r4, gfin_ref[...])
        d_gp = dx3 * e
        d_e = dx3 * gp
        dpe, dgple = _rms_bwd(d_e, peh, r3, gple_ref[...])
        dwple_ref[...] += _dot_tn(dpe, p_ref[...])
        dpre = d_gp * gp * (1.0 - gp)
        dwpg_ref[...] += _dot_tn(h3, dpre)
        dh3 = _dot_nt(dpre, wpg_ref[...])
        dx2n, dgpg = _rms_bwd(dh3, xh2, r2, gpg_ref[...])
        dx2_ref[...] = dx3 + dx2n
        vec_ref[0:1, :] += _rowsum(dpre)
        vec_ref[1:2, :] += _rowsum(dgpg)
        vec_ref[2:3, :] += _rowsum(dgple)
        vec_ref[3:4, :] += _rowsum(dgfin)
        _store_on_last([(dwpg_ref, dwpg_out), (dwple_ref, dwple_out)])

    return pl.pallas_call(
        body, name="tail_fwd_bwd", grid=(t // tm,),
        out_shape=[jax.ShapeDtypeStruct((t, d), F32), jax.ShapeDtypeStruct((SUBLANES, LANES), F32),
                   jax.ShapeDtypeStruct((d, d), BF16), jax.ShapeDtypeStruct((d, pd), BF16),
                   jax.ShapeDtypeStruct((SUBLANES, d), F32)],
        in_specs=[_rows(tm, d), _rows(tm, pd), _rows(tm, d), _whole((1, d)), _whole((d, d)), _whole((1, d)),
                  _whole((d, pd)), _whole((1, d)), _whole((1, d))],
        out_specs=[_rows(tm, d), _acc((SUBLANES, LANES)), _acc((d, d)), _acc((d, pd)), _acc((SUBLANES, d))],
        scratch_shapes=[pltpu.VMEM((d, d), F32), pltpu.VMEM((d, pd), F32)],
        compiler_params=_params("arbitrary"),
    )(x2, p, target, g_pg, w_pg, b_pg, w_ple_t, g_ple, g_final)


def _ffn_bwd_a(dx2, fg, fu, w_down, comm=None):
    t, d = dx2.shape
    f = fg.shape[1]
    tm = min(TOKEN_TILE, t)

    def body(dx_ref, fg_ref, fu_ref, wd_ref, dfg_ref, dfu_ref, act_ref):
        dact = _dot_nt(dx_ref[...], wd_ref[...])
        fgv = fg_ref[...]
        fuv = fu_ref[...]
        sg = _sigmoid(fgv)
        silu = fgv * sg
        dfu_ref[...] = (dact * silu).astype(dfu_ref.dtype)
        dfg_ref[...] = (dact * fuv * (sg * (1.0 + fgv * (1.0 - sg)))).astype(dfg_ref.dtype)
        act_ref[...] = (silu * fuv).astype(act_ref.dtype)

    return _run(
        body, comm, name="ffn_bwd_a", grid=(t // tm,),
        out_shape=[jax.ShapeDtypeStruct((t, f), BF16)] * 3,
        in_specs=[_rows(tm, d), _rows(tm, f), _rows(tm, f), _whole((f, d))],
        out_specs=[_rows(tm, f)] * 3,
        semantics="parallel",
    )(dx2, fg, fu, w_down)


def _ffn_bwd_b(dfg, dfu, x1, dx2, g_ffn, w_gate_t, w_up_t, comm=None):
    t, d = x1.shape
    f = dfg.shape[1]
    tm = min(TOKEN_TILE, t)

    def body(dfg_ref, dfu_ref, x_ref, dx2_ref, g_ref, wg_ref, wu_ref, dx1_ref, h2_ref, vec_ref):
        _zero_on_first(vec_ref)
        dh2 = _dot(dfg_ref[...], wg_ref[...]) + _dot(dfu_ref[...], wu_ref[...])
        xhat, r = _rms_stats(x_ref[...])
        h2_ref[...] = (xhat * g_ref[...]).astype(h2_ref.dtype)
        dxn, dg = _rms_bwd(dh2, xhat, r, g_ref[...])
        dx1_ref[...] = dx2_ref[...] + dxn
        vec_ref[0:1, :] += _rowsum(dg)

    return _run(
        body, comm, name="ffn_bwd_b", grid=(t // tm,),
        out_shape=[jax.ShapeDtypeStruct((t, d), F32), jax.ShapeDtypeStruct((t, d), BF16),
                   jax.ShapeDtypeStruct((SUBLANES, d), F32)],
        in_specs=[_rows(tm, f), _rows(tm, f), _rows(tm, d), _rows(tm, d), _whole((1, d)), _whole((f, d)), _whole((f, d))],
        out_specs=[_rows(tm, d), _rows(tm, d), _acc((SUBLANES, d))],
        semantics="arbitrary",
    )(dfg, dfu, x1, dx2, g_ffn, w_gate_t, w_up_t)


def _matmul_tn(a, b, tn, name, dtype=F32, comm=None):
    t, k = a.shape
    n = b.shape[1]

    def body(a_ref, b_ref, o_ref):
        o_ref[...] = _dot_tn(a_ref[...], b_ref[...]).astype(o_ref.dtype)

    return _run(
        body, comm, name=name, grid=(n // tn,),
        out_shape=jax.ShapeDtypeStruct((k, n), dtype),
        in_specs=[_whole((t, k)), pl.BlockSpec((t, tn), lambda j: (0, j))],
        out_specs=pl.BlockSpec((k, tn), lambda j: (0, j)),
        semantics="parallel",
    )(a, b)


def _merge_bwd(dx1, merged, ma, mb, za, zb, y_a, h, w_o, w_a_out_t, w_b_out, comm=None):
    t, d = dx1.shape
    sa, lw = y_a.shape[1], h.shape[1]
    tm = min(TOKEN_TILE, t)

    def body(dx1_ref, mg_ref, ma_ref, mb_ref, za_ref, zb_ref, ya_ref, h_ref, wo_ref, wa_ref, wb_ref,
             dza_ref, dzb_ref, dya_ref, dyb_ref, dwo_out, dwa_out, dwb_out, dwo_ref, dwa_ref, dwb_ref):
        _zero_on_first(dwo_ref, dwa_ref, dwb_ref)
        dx1v = dx1_ref[...].astype(BF16)
        dmg = _dot_nt(dx1v, wo_ref[...])
        ga = _sigmoid(za_ref[...])
        gb = _sigmoid(zb_ref[...])
        dza_ref[...] = dmg * ma_ref[...] * ga * (1.0 - ga)
        dzb_ref[...] = dmg * mb_ref[...] * gb * (1.0 - gb)
        dma = (dmg * ga).astype(BF16)
        dmb = (dmg * gb).astype(BF16)
        dya_ref[...] = _dot(dma, wa_ref[...])
        dyb_ref[...] = _dot_nt(dmb, wb_ref[...])
        dwo_ref[...] += _dot_tn(mg_ref[...], dx1v)
        dwa_ref[...] += _dot_tn(dma, ya_ref[...])
        dwb_ref[...] += _dot_tn(h_ref[...], dmb)
        _store_on_last([(dwo_ref, dwo_out), (dwa_ref, dwa_out), (dwb_ref, dwb_out)])

    return _run(
        body, comm, name="merge_bwd", grid=(t // tm,),
        out_shape=[jax.ShapeDtypeStruct((t, d), F32), jax.ShapeDtypeStruct((t, d), F32),
                   jax.ShapeDtypeStruct((t, sa), F32), jax.ShapeDtypeStruct((t, lw), F32),
                   jax.ShapeDtypeStruct((d, d), BF16), jax.ShapeDtypeStruct((d, sa), BF16),
                   jax.ShapeDtypeStruct((lw, d), BF16)],
        in_specs=[_rows(tm, d), _rows(tm, d), _rows(tm, d), _rows(tm, d), _rows(tm, d), _rows(tm, d),
                  _rows(tm, sa), _rows(tm, lw), _whole((d, d)), _whole((d, sa)), _whole((lw, d))],
        out_specs=[_rows(tm, d), _rows(tm, d), _rows(tm, sa), _rows(tm, lw), _acc((d, d)), _acc((d, sa)), _acc((lw, d))],
        scratch_shapes=[pltpu.VMEM((d, d), F32), pltpu.VMEM((d, sa), F32), pltpu.VMEM((lw, d), F32)],
        semantics="arbitrary",
    )(dx1, merged, ma, mb, za, zb, y_a, h, w_o, w_a_out_t, w_b_out)


def _fold_diag_blocks(dense, row_group, col_group, row0=0, col0=0):
    r, c = dense.shape
    rows = lax.broadcasted_iota(jnp.int32, (r, c), 0) + row0
    cols = lax.broadcasted_iota(jnp.int32, (r, c), 1) + col0
    kept = jnp.where(rows // row_group == cols // col_group, dense, 0.0)
    pick = (lax.broadcasted_iota(jnp.int32, (row_group, r), 0)
            == lax.broadcasted_iota(jnp.int32, (row_group, r), 1) % row_group).astype(F32)
    return jnp.dot(pick, kept, preferred_element_type=F32, precision=lax.Precision.HIGHEST)


def _lru_bwd(dh, xc, hprev, u, conv_w, wr_blk, b_r, wi_blk, b_i, lru_lambda, head_dim, comm=None):
    t, w = dh.shape
    tc = min(TIME_CHUNK, t)
    steps = t // tc
    halo = SUBLANES
    sub_per_chunk = tc // halo
    slabs = w // LRU_SLAB

    def body(dh_ref, xc_ref, hp_ref, u_ref, uh_ref, cw_ref, wr_ref, br_ref, wi_ref, bi_ref, lam_ref,
             du_ref, dwr_out, dwi_out, vec_ref, lam_s, a_s, dxc_s, uext_s, carry_s, dwr_ref, dwi_ref):
        chunk = steps - 1 - pl.program_id(0)

        @pl.when(pl.program_id(0) == 0)
        def _():
            carry_s[...] = jnp.zeros_like(carry_s)
            dxc_s[tc:tc + halo, :] = jnp.zeros((halo, w), F32)
            dwr_ref[...] = jnp.zeros_like(dwr_ref)
            dwi_ref[...] = jnp.zeros_like(dwi_ref)
            vec_ref[...] = jnp.zeros_like(vec_ref)

        xc = xc_ref[...]
        r, ig, sp, log_a = _lru_gates(xc, wr_ref, br_ref, wi_ref, bi_ref, lam_ref)
        a = jnp.exp(log_a)
        a_s[...] = a

        def step(i, q):
            at = pl.ds(tc - 1 - i, 1)
            lam_row = dh_ref[at, :] + q
            lam_s[at, :] = lam_row
            return a_s[at, :] * lam_row

        carry_s[0:1, :] = lax.fori_loop(0, tc, step, carry_s[0:1, :], unroll=8)
        lam = lam_s[...]
        mult = jnp.sqrt(-_expm1(2.0 * log_a))
        d_log_a = lam * hp_ref[...] * a - (lam * ig * xc) * (a * a) / mult
        d_ig = lam * mult * xc
        dpre_r = (d_log_a * (-LRU_C * sp)) * r * (1.0 - r)
        dpre_i = d_ig * ig * (1.0 - ig)
        dxc = lam * mult * ig + _slab_dot(dpre_r, wr_ref, transposed=True) + _slab_dot(dpre_i, wi_ref, transposed=True)
        xcb, drb, dib = xc.astype(BF16), dpre_r.astype(BF16), dpre_i.astype(BF16)
        for j in range(slabs):
            cols = slice(j * LRU_SLAB, (j + 1) * LRU_SLAB)
            dwr_ref[j] += _dot_tn(drb[:, cols], xcb[:, cols])
            dwi_ref[j] += _dot_tn(dib[:, cols], xcb[:, cols])
        vec_ref[0:1, :] += _rowsum(dxc)
        vec_ref[1:2, :] += _rowsum(dpre_r)
        vec_ref[2:3, :] += _rowsum(dpre_i)
        vec_ref[3:4, :] += _rowsum(d_log_a * (-LRU_C * r)) * (-_sigmoid(-lam_ref[...]))
        dxc_s[0:tc, :] = dxc
        du = cw_ref[CONV_WIDTH - 1:CONV_WIDTH, :] * dxc
        for k in range(CONV_WIDTH - 1):
            off = CONV_WIDTH - 1 - k
            du = du + cw_ref[k:k + 1, :] * dxc_s[off:off + tc, :]
        du_ref[...] = du
        dxc_s[tc:tc + halo, :] = dxc_s[0:halo, :]
        uext_s[0:halo, :] = jnp.where(chunk > 0, uh_ref[...], 0.0)
        uext_s[halo:halo + tc, :] = u_ref[...]
        for k in range(CONV_WIDTH):
            off = halo - (CONV_WIDTH - 1) + k
            vec_ref[4 + k:5 + k, :] += _rowsum(dxc * uext_s[off:off + tc, :])

        @pl.when(pl.program_id(0) == steps - 1)
        def _():
            for j in range(slabs):
                cols = slice(j * LRU_SLAB, (j + 1) * LRU_SLAB)
                dwr_out[:, cols] = _fold_diag_blocks(dwr_ref[j], head_dim, head_dim)
                dwi_out[:, cols] = _fold_diag_blocks(dwi_ref[j], head_dim, head_dim)

    halo_spec = pl.BlockSpec((halo, w), lambda i: (jnp.maximum((steps - 1 - i) * sub_per_chunk - 1, 0), 0))
    return _run(
        body, comm, name="lru_bwd", grid=(steps,),
        out_shape=[jax.ShapeDtypeStruct((t, w), F32), jax.ShapeDtypeStruct((head_dim, w), F32),
                   jax.ShapeDtypeStruct((head_dim, w), F32), jax.ShapeDtypeStruct((SUBLANES, w), F32)],
        in_specs=[_rows_rev(tc, w, steps)] * 4 + [halo_spec, _whole((CONV_WIDTH, w)), _whole(wr_blk.shape),
                                                  _whole((1, w)), _whole(wi_blk.shape), _whole((1, w)), _whole((1, w))],
        out_specs=[_rows_rev(tc, w, steps), _acc((head_dim, w)), _acc((head_dim, w)), _acc((SUBLANES, w))],
        scratch_shapes=[pltpu.VMEM((tc, w), F32), pltpu.VMEM((tc, w), F32), pltpu.VMEM((tc + halo, w), F32),
                        pltpu.VMEM((halo + tc, w), F32), pltpu.VMEM((SUBLANES, w), F32),
                        pltpu.VMEM((slabs, LRU_SLAB, LRU_SLAB), F32), pltpu.VMEM((slabs, LRU_SLAB, LRU_SLAB), F32)],
        semantics="arbitrary",
    )(dh, xc, hprev, u, u, conv_w, wr_blk, b_r, wi_blk, b_i, lru_lambda)


def _s5_bwd(dya, y, sr, si, u, w_glu, b_glu, cre_blk, cimn_blk, bbr_blk, bbi_blk, ar, ai, d_skip, comm=None):
    t, sa = dya.shape
    gn = sr.shape[1]
    ns, _, sw = bbr_blk.shape
    tc = min(TIME_CHUNK, t)
    steps = t // tc
    halo = SUBLANES

    def body(dya_ref, y_ref, sr_ref, si_ref, u_ref, wg_ref, bg_ref, cre_ref, cim_ref, bbr_ref, bbi_ref,
             ar_ref, ai_ref, d_ref, du_ref, lr_ref, li_ref, dy_ref, dwg_out, vsa_ref, vgn_ref, gr_s, gi_s, cr_s, ci_s,
             dwg_ref):
        @pl.when(pl.program_id(0) == 0)
        def _():
            cr_s[...] = jnp.zeros_like(cr_s)
            ci_s[...] = jnp.zeros_like(ci_s)
            gr_s[tc:tc + halo, :] = jnp.zeros((halo, gn), F32)
            gi_s[tc:tc + halo, :] = jnp.zeros((halo, gn), F32)
            dwg_ref[...] = jnp.zeros_like(dwg_ref)
            vsa_ref[...] = jnp.zeros_like(vsa_ref)
            vgn_ref[...] = jnp.zeros_like(vgn_ref)

        yv = y_ref[...]
        uv = u_ref[...]
        zz = _gelu(yv)
        sg = _sigmoid(_dot(zz, wg_ref[...]) + bg_ref[...])
        dyav = dya_ref[...]
        dq = dyav * zz * sg * (1.0 - sg)
        dzz = dyav * sg + _dot_nt(dq, wg_ref[...])
        dwg_ref[...] += _dot_tn(zz, dq)
        dy = dzz * _gelu_grad(yv)
        dyb = dy.astype(BF16)
        dy_ref[...] = dyb.astype(dy_ref.dtype)
        vsa_ref[0:1, :] += _rowsum(dq)
        vsa_ref[1:2, :] += _rowsum(dy * uv)
        for m in range(ns):
            dym = dyb[:, m * S5_SLAB:(m + 1) * S5_SLAB]
            gr_s[0:tc, m * sw:(m + 1) * sw] = _dot_nt(dym, cre_ref[m])
            gi_s[0:tc, m * sw:(m + 1) * sw] = _dot_nt(dym, cim_ref[m])
        a_r = ar_ref[...]
        a_i = ai_ref[...]

        def step(i, carry):
            l_r, l_i = carry
            at = pl.ds(tc - 1 - i, 1)
            n_r = gr_s[at, :] + a_r * l_r + a_i * l_i
            n_i = gi_s[at, :] + a_r * l_i - a_i * l_r
            gr_s[at, :] = n_r
            gi_s[at, :] = n_i
            return n_r, n_i

        l_r, l_i = lax.fori_loop(0, tc, step, (cr_s[0:1, :], ci_s[0:1, :]), unroll=8)
        cr_s[0:1, :] = l_r
        ci_s[0:1, :] = l_i
        nxt_r = gr_s[1:tc + 1, :]
        nxt_i = gi_s[1:tc + 1, :]
        srv = sr_ref[...]
        siv = si_ref[...]
        vgn_ref[0:1, :] += _rowsum(nxt_r * srv + nxt_i * siv)
        vgn_ref[1:2, :] += _rowsum(nxt_i * srv - nxt_r * siv)
        lam_r = gr_s[0:tc, :]
        lam_i = gi_s[0:tc, :]
        gr_s[tc:tc + halo, :] = gr_s[0:halo, :]
        gi_s[tc:tc + halo, :] = gi_s[0:halo, :]
        lrb = lam_r.astype(BF16)
        lib = lam_i.astype(BF16)
        lr_ref[...] = lrb.astype(lr_ref.dtype)
        li_ref[...] = lib.astype(li_ref.dtype)
        for m in range(ns):
            states, chans = slice(m * sw, (m + 1) * sw), slice(m * S5_SLAB, (m + 1) * S5_SLAB)
            du_ref[:, chans] = (_dot_nt(lrb[:, states], bbr_ref[m]) + _dot_nt(lib[:, states], bbi_ref[m])
                                + dy[:, chans] * d_ref[:, chans])
        _store_on_last([(dwg_ref, dwg_out)])

    return _run(
        body, comm, name="s5_bwd", grid=(steps,),
        out_shape=[jax.ShapeDtypeStruct((t, sa), F32), jax.ShapeDtypeStruct((t, gn), BF16),
                   jax.ShapeDtypeStruct((t, gn), BF16), jax.ShapeDtypeStruct((t, sa), BF16),
                   jax.ShapeDtypeStruct((sa, sa), BF16), jax.ShapeDtypeStruct((SUBLANES, sa), F32),
                   jax.ShapeDtypeStruct((SUBLANES, gn), F32)],
        in_specs=[_rows_rev(tc, sa, steps), _rows_rev(tc, sa, steps), _rows_rev(tc, gn, steps), _rows_rev(tc, gn, steps),
                  _rows_rev(tc, sa, steps), _whole((sa, sa)), _whole((1, sa)), _whole(cre_blk.shape),
                  _whole(cimn_blk.shape), _whole(bbr_blk.shape), _whole(bbi_blk.shape), _whole((1, gn)), _whole((1, gn)),
                  _whole((1, sa))],
        out_specs=[_rows_rev(tc, sa, steps), _rows_rev(tc, gn, steps), _rows_rev(tc, gn, steps), _rows_rev(tc, sa, steps),
                   _acc((sa, sa)), _acc((SUBLANES, sa)), _acc((SUBLANES, gn))],
        scratch_shapes=[pltpu.VMEM((tc + halo, gn), F32), pltpu.VMEM((tc + halo, gn), F32),
                        pltpu.VMEM((SUBLANES, gn), F32), pltpu.VMEM((SUBLANES, gn), F32), pltpu.VMEM((sa, sa), F32)],
        semantics="arbitrary",
    )(dya, y, sr, si, u, w_glu, b_glu, cre_blk, cimn_blk, bbr_blk, bbi_blk, ar, ai, d_skip)


def _inproj_bwd(dparts, x, dx1, g_mix, w_in_t, comm=None):
    t, d = x.shape
    n = w_in_t.shape[0]
    widths = [p.shape[1] for p in dparts]
    offs = [sum(widths[:i]) for i in range(len(widths) + 1)]
    tm = min(TOKEN_TILE, t)
    np_ = len(dparts)

    def body(*refs):
        dz_refs = refs[:np_]
        x_ref, dx1_ref, g_ref, w_ref, gx_ref, h_ref, dz_ref, vd_ref, vn_ref = refs[np_:]
        _zero_on_first(vd_ref, vn_ref)
        dh = jnp.zeros((tm, d), F32)
        for k, r in enumerate(dz_refs):
            lo, hi = offs[k], offs[k + 1]
            dzk = r[...]
            dh = dh + _dot(dzk, w_ref[lo:hi, :])
            dz_ref[:, lo:hi] = dzk.astype(dz_ref.dtype)
            vn_ref[0:1, lo:hi] += _rowsum(dzk)
        xhat, r0 = _rms_stats(x_ref[...])
        h_ref[...] = (xhat * g_ref[...]).astype(h_ref.dtype)
        dxn, dg = _rms_bwd(dh, xhat, r0, g_ref[...])
        gx_ref[...] = dx1_ref[...] + dxn
        vd_ref[0:1, :] += _rowsum(dg)

    return _run(
        body, comm, name="inproj_bwd", grid=(t // tm,),
        out_shape=[jax.ShapeDtypeStruct((t, d), F32), jax.ShapeDtypeStruct((t, d), BF16),
                   jax.ShapeDtypeStruct((t, n), BF16), jax.ShapeDtypeStruct((SUBLANES, d), F32),
                   jax.ShapeDtypeStruct((SUBLANES, n), F32)],
        in_specs=[_rows(tm, w) for w in widths] + [_rows(tm, d), _rows(tm, d), _whole((1, d)), _whole((n, d))],
        out_specs=[_rows(tm, d), _rows(tm, d), _rows(tm, n), _acc((SUBLANES, d)), _acc((SUBLANES, n))],
        semantics="arbitrary",
    )(*dparts, x, dx1, g_mix, w_in_t)


def _prep(lr_row, li_row, ldt_row, lr_col, li_col, ldt_col, b_re, b_im, c_re, c_im, w_r, w_i, comm=None):
    gn, pch = b_re.shape
    sa, n = c_re.shape
    w, hd = w_r.shape
    ns, sw, lsl = sa // S5_SLAB, S5_SLAB * n // pch, w // LRU_SLAB

    def spread_cols(vals, row_group, col_group, width):
        r, k = vals.shape
        tile = (lax.broadcasted_iota(jnp.int32, (k, width), 0) == lax.broadcasted_iota(jnp.int32, (k, width), 1) % k)
        rows = lax.broadcasted_iota(jnp.int32, (r, width), 0) // row_group
        cols = lax.broadcasted_iota(jnp.int32, (r, width), 1) // col_group
        return jnp.where(rows == cols, _dot(vals, tile.astype(BF16)), 0.0)

    def spread_rows(vals, row_group, col_group, height):
        k, c = vals.shape
        tile = (lax.broadcasted_iota(jnp.int32, (height, k), 0) % k == lax.broadcasted_iota(jnp.int32, (height, k), 1))
        rows = lax.broadcasted_iota(jnp.int32, (height, c), 0) // row_group
        cols = lax.broadcasted_iota(jnp.int32, (height, c), 1) // col_group
        return jnp.where(rows == cols, _dot(tile.astype(BF16), vals), 0.0)

    def body(lrr, lir, ldr, lrc, lic, ldc, bre, bim, cre, cim, wr, wi,
             ar_o, ai_o, bbr_o, bbi_o, cre_o, cim_o, wr_o, wi_o):
        ar, ai, _, _ = _disc_scalars(lrr[...], lir[...], ldr[...])
        ar_o[...] = ar
        ai_o[...] = ai
        _, _, bbr, bbi = _disc_cols(lrc[...], lic[...], ldc[...], bre[...], bim[...])
        bbr_t, bbi_t = bbr.T, bbi.T
        for m in range(ns):
            bbr_o[m] = spread_rows(bbr_t[:, m * sw:(m + 1) * sw], pch, n, S5_SLAB).astype(bbr_o.dtype)
            bbi_o[m] = spread_rows(bbi_t[:, m * sw:(m + 1) * sw], pch, n, S5_SLAB).astype(bbi_o.dtype)
            rows = slice(m * S5_SLAB, (m + 1) * S5_SLAB)
            cre_o[m] = spread_rows(cre[rows, :].T, n, pch, sw).astype(cre_o.dtype)
            cim_o[m] = spread_rows(-cim[rows, :].T, n, pch, sw).astype(cim_o.dtype)
        for j in range(lsl):
            rows = slice(j * LRU_SLAB, (j + 1) * LRU_SLAB)
            wr_o[j] = spread_cols(wr[rows, :], hd, hd, LRU_SLAB).astype(wr_o.dtype)
            wi_o[j] = spread_cols(wi[rows, :], hd, hd, LRU_SLAB).astype(wi_o.dtype)

    args = (lr_row, li_row, ldt_row, lr_col, li_col, ldt_col, b_re, b_im, c_re, c_im, w_r, w_i)
    out_shape = [jax.ShapeDtypeStruct((1, gn), F32), jax.ShapeDtypeStruct((1, gn), F32),
                 jax.ShapeDtypeStruct((ns, S5_SLAB, sw), BF16), jax.ShapeDtypeStruct((ns, S5_SLAB, sw), BF16),
                 jax.ShapeDtypeStruct((ns, sw, S5_SLAB), BF16), jax.ShapeDtypeStruct((ns, sw, S5_SLAB), BF16),
                 jax.ShapeDtypeStruct((lsl, LRU_SLAB, LRU_SLAB), BF16),
                 jax.ShapeDtypeStruct((lsl, LRU_SLAB, LRU_SLAB), BF16)]
    return _run(
        body, comm, name="prep", grid=(1,), out_shape=out_shape, in_specs=[_whole(a.shape) for a in args],
        out_specs=[_acc(s.shape) for s in out_shape], semantics="arbitrary",
    )(*args)


def _s5_param_grads(lam_r, lam_i, sr, si, u, dy, pch, n, comm=None):
    t, gn = lam_r.shape
    sa = u.shape[1]
    sw = S5_SLAB * n // pch

    def body(lr_ref, li_ref, sr_ref, si_ref, u_ref, dy_ref, dbr_ref, dbi_ref, dcr_ref, dci_ref):
        uv = u_ref[...]
        dyv = dy_ref[...]
        dbr_ref[...] = _fold_diag_blocks(_dot_tn(uv, lr_ref[...]), pch, n)
        dbi_ref[...] = _fold_diag_blocks(_dot_tn(uv, li_ref[...]), pch, n)
        dcr_ref[...] = _fold_diag_blocks(_dot_tn(sr_ref[...], dyv), n, pch)
        dci_ref[...] = _fold_diag_blocks(_dot_tn(si_ref[...], dyv), n, pch)

    states = pl.BlockSpec((t, sw), lambda m: (0, m))
    chans = pl.BlockSpec((t, S5_SLAB), lambda m: (0, m))
    return _run(
        body, comm, name="s5_param_grads", grid=(sa // S5_SLAB,),
        out_shape=[jax.ShapeDtypeStruct((pch, gn), F32), jax.ShapeDtypeStruct((pch, gn), F32),
                   jax.ShapeDtypeStruct((n, sa), F32), jax.ShapeDtypeStruct((n, sa), F32)],
        in_specs=[states, states, states, states, chans, chans],
        out_specs=[pl.BlockSpec((pch, sw), lambda m: (0, m)), pl.BlockSpec((pch, sw), lambda m: (0, m)),
                   pl.BlockSpec((n, S5_SLAB), lambda m: (0, m)), pl.BlockSpec((n, S5_SLAB), lambda m: (0, m))],
        semantics="parallel",
    )(lam_r, lam_i, sr, si, u, dy)


SMALL_PARTS = ["vec_tail", "vec_ffn", "vec_lru", "vec_mix", "vec_bin", "vec_sa", "vec_gn", "dw_r", "dw_i", "dbb_re",
               "dbb_im", "dc_re", "dc_imn", "loss"]


def _small_reduce(parts, shapes, lr_col, li_col, ldt_col, b_re, b_im, groups, comm=None):
    gn, pch = b_re.shape
    n = gn // groups
    nparts = parts[SMALL_PARTS[0]].size // math.prod(shapes[SMALL_PARTS[0]])
    np_, nout = len(SMALL_PARTS), 24

    def body(*refs):
        ins = refs[:np_]
        lr, li, ldt, bre, bim = refs[np_:np_ + 5]
        outs = refs[np_ + 5:np_ + 5 + nout]
        sums = dict(zip(SMALL_PARTS, refs[np_ + 5 + nout:]))

        @pl.when(pl.program_id(0) == 0)
        def _():
            for k, r in zip(SMALL_PARTS, ins):
                sums[k][...] = r[...]

        @pl.when(pl.program_id(0) > 0)
        def _():
            for k, r in zip(SMALL_PARTS, ins):
                sums[k][...] += r[...]

        @pl.when(pl.program_id(0) == nparts - 1)
        def _():
            finish({k: s[...] for k, s in sums.items()}, lr, li, ldt, bre, bim, *outs)

    def finish(tot, lr, li, ldt, bre, bim, o_loss, o_gmix, o_bin, o_bglu, o_s5d, o_convb, o_br, o_bi, o_lam, o_gffn,
               o_gpg, o_bpg, o_gple, o_gfin, o_wr, o_wi, o_cre, o_cim, o_bre, o_bim, o_lre, o_lim, o_ldt, o_convw):
        o_loss[...] = tot["loss"]
        o_convw[...] = tot["vec_lru"][SUBLANES - CONV_WIDTH:SUBLANES]
        o_bpg[...] = tot["vec_tail"][0:1]
        o_gpg[...] = tot["vec_tail"][1:2]
        o_gple[...] = tot["vec_tail"][2:3]
        o_gfin[...] = tot["vec_tail"][3:4]
        o_gffn[...] = tot["vec_ffn"][0:1]
        o_convb[...] = tot["vec_lru"][0:1]
        o_br[...] = tot["vec_lru"][1:2]
        o_bi[...] = tot["vec_lru"][2:3]
        o_lam[...] = tot["vec_lru"][3:4]
        o_gmix[...] = tot["vec_mix"][0:1]
        o_bin[...] = tot["vec_bin"][0:1]
        o_bglu[...] = tot["vec_sa"][0:1]
        o_s5d[...] = tot["vec_sa"][1:2]
        o_wr[...] = tot["dw_r"].T
        o_wi[...] = tot["dw_i"].T
        o_cre[...] = tot["dc_re"].T
        o_cim[...] = -tot["dc_imn"].T
        d_a = tot["vec_gn"].T
        _, chain = jax.vjp(_disc_cols, lr[...], li[...], ldt[...], bre[...], bim[...])
        d_lr, d_li, d_ldt, d_bre, d_bim = chain((d_a[:, 0:1], d_a[:, 1:2], tot["dbb_re"].T, tot["dbb_im"].T))
        o_lre[...] = d_lr
        o_lim[...] = d_li
        o_bre[...] = d_bre
        o_bim[...] = d_bim
        same = (lax.broadcasted_iota(jnp.int32, (groups, gn), 0)
                == lax.broadcasted_iota(jnp.int32, (groups, gn), 1) // n).astype(F32)
        o_ldt[...] = jnp.dot(same, d_ldt * jnp.ones((1, LANES), F32), preferred_element_type=F32,
                             precision=lax.Precision.HIGHEST)[:, 0:1]

    d = shapes["vec_mix"][1]
    nz = shapes["vec_bin"][1]
    sa = shapes["vec_sa"][1]
    w = shapes["vec_lru"][1]
    row = lambda c: jax.ShapeDtypeStruct((1, c), F32)
    out_shape = [jax.ShapeDtypeStruct(shapes["loss"], F32), row(d), row(nz), row(sa), row(sa), row(w), row(w), row(w),
                 row(w), row(d), row(d), row(d), row(d), row(d),
                 jax.ShapeDtypeStruct(shapes["dw_r"][::-1], F32), jax.ShapeDtypeStruct(shapes["dw_i"][::-1], F32),
                 jax.ShapeDtypeStruct(shapes["dc_re"][::-1], F32), jax.ShapeDtypeStruct(shapes["dc_imn"][::-1], F32),
                 jax.ShapeDtypeStruct((gn, pch), F32), jax.ShapeDtypeStruct((gn, pch), F32),
                 jax.ShapeDtypeStruct((gn, 1), F32), jax.ShapeDtypeStruct((gn, 1), F32),
                 jax.ShapeDtypeStruct((groups, 1), F32), jax.ShapeDtypeStruct((CONV_WIDTH, w), F32)]
    def part_spec(k):
        r, c = shapes[k]
        if parts[k].ndim == 3:
            return pl.BlockSpec((None, r, c), lambda i: (i, 0, 0))
        return pl.BlockSpec((r, c), lambda i: (i, 0))

    outs = _run(
        body, comm, name="small_reduce", grid=(nparts,), out_shape=out_shape,
        in_specs=[part_spec(k) for k in SMALL_PARTS] + [_whole(a.shape) for a in (lr_col, li_col, ldt_col, b_re, b_im)],
        out_specs=[_acc(s.shape) for s in out_shape],
        scratch_shapes=[pltpu.VMEM(shapes[k], F32) for k in SMALL_PARTS],
        semantics="arbitrary",
    )(*[parts[k] for k in SMALL_PARTS], lr_col, li_col, ldt_col, b_re, b_im)
    extra = None
    if comm is not None:
        outs, extra = outs
    names = ["loss", "g_mix", "b_in", "b_glu", "s5_d", "conv_b", "b_r", "b_i", "lru_lambda", "g_ffn", "g_ple_gate",
             "b_ple_gate", "g_ple", "g_final", "w_r", "w_i", "s5_c_re", "s5_c_im", "s5_b_re", "s5_b_im", "lam_re",
             "lam_im", "log_dt", "conv_w"]
    res = dict(zip(names, outs))
    return res if comm is None else (res, extra)


def _adamw_small(ws, gs, ms, vs):
    n = len(ws)

    def body(*refs):
        w_r, g_r, m_r, v_r = (refs[i * n:(i + 1) * n] for i in range(4))
        g_o, d_o, m_o, v_o = (refs[(4 + i) * n:(5 + i) * n] for i in range(4))
        for i in range(n):
            g = g_r[i][...]
            delta, m_new, v_new = _adamw_math(w_r[i][...], g, m_r[i][...], v_r[i][...])
            g_o[i][...] = g
            d_o[i][...] = delta
            m_o[i][...] = m_new
            v_o[i][...] = v_new

    shapes = [jax.ShapeDtypeStruct(a.shape, F32) for a in ws]
    outs = pl.pallas_call(body, name="adamw_small", out_shape=shapes * 4)(*ws, *gs, *ms, *vs)
    return outs[:n], outs[n:2 * n], outs[2 * n:3 * n], outs[3 * n:]


def _adamw_math(w, g, m, v):
    m_new = ADAM_B1 * m + (1.0 - ADAM_B1) * g
    v_new = ADAM_B2 * v + (1.0 - ADAM_B2) * (g * g)
    m_hat = m_new / (1.0 - ADAM_B1 ** ADAM_STEP)
    v_hat = v_new / (1.0 - ADAM_B2 ** ADAM_STEP)
    delta = -ADAM_LR * (m_hat / (jnp.sqrt(v_hat) + ADAM_EPS) + ADAM_WD * w)
    return delta, m_new, v_new


def _row_tile(rows):
    for cand in (256, 128, 64, 32, 16, 8):
        if rows % cand == 0:
            return cand
    return rows


def _adamw(parts, w, m, v, name, transposed=False):
    rows, cols = w.shape
    npart = parts.shape[0]
    tr = _row_tile(rows)
    if transposed:
        parts_spec = pl.BlockSpec((npart, cols, tr), lambda i: (0, 0, i))
    else:
        parts_spec = pl.BlockSpec((npart, tr, cols), lambda i: (0, i, 0))

    def body(p_ref, w_ref, m_ref, v_ref, g_ref, d_ref, mo_ref, vo_ref):
        g = p_ref[0].astype(F32)
        for k in range(1, npart):
            g = g + p_ref[k].astype(F32)
        if transposed:
            g = g.T
        delta, m_new, v_new = _adamw_math(w_ref[...], g, m_ref[...], v_ref[...])
        g_ref[...] = g
        d_ref[...] = delta
        mo_ref[...] = m_new
        vo_ref[...] = v_new

    return pl.pallas_call(
        body, name=name, grid=(rows // tr,),
        out_shape=[jax.ShapeDtypeStruct((rows, cols), F32)] * 4,
        in_specs=[parts_spec] + [_rows(tr, cols)] * 3,
        out_specs=[_rows(tr, cols)] * 4,
        compiler_params=_params("parallel"),
    )(parts, w, m, v)


def _mesh_position():
    return lax.axis_index("x"), lax.axis_index("y"), lax.axis_index("c")


def _flip(pos, rel):
    x, y, c = pos
    return (1 - x if rel & 4 else x, 1 - y if rel & 2 else y, 1 - c if rel & 1 else c)


def _index(pos):
    return 4 * pos[0] + 2 * pos[1] + pos[2]


_ANY = pl.BlockSpec(memory_space=pl.ANY)
FLAT_ROWS = 32


def _dma_sems(n):
    return [pltpu.SemaphoreType.DMA((n, N_DEV - 1)), pltpu.SemaphoreType.DMA((n, N_DEV - 1)), pltpu.SemaphoreType.DMA((n,))]


def _block_of(ref, idx, rows, flat):
    if flat:
        return ref.at[pl.ds(pl.multiple_of(idx * rows, FLAT_ROWS), rows), :]
    return ref.at[idx]


class _Gather:
    chips = (4, 2, 6)

    def __init__(self, shards):
        self.inputs = list(shards)
        self.flat = [s.shape[0] % FLAT_ROWS == 0 for s in shards]
        self.out_shape = [
            jax.ShapeDtypeStruct((N_DEV * s.shape[0], s.shape[1]) if f else (N_DEV,) + s.shape, s.dtype)
            for s, f in zip(shards, self.flat)]
        self.sems = _dma_sems(len(shards))

    def _copy(self, ins, outs, sems, i, k, block, to, own=False):
        dst = _block_of(outs[i], _index(block), self.inputs[i].shape[0], self.flat[i])
        return pltpu.make_async_remote_copy(
            src_ref=ins[i] if own else dst, dst_ref=dst, send_sem=sems[0].at[i, k], recv_sem=sems[1].at[i, k],
            device_id=to, device_id_type=MESH)

    def _local(self, ins, outs, sems, i, me):
        dst = _block_of(outs[i], _index(me), self.inputs[i].shape[0], self.flat[i])
        return pltpu.make_async_copy(ins[i], dst, sems[2].at[i])

    def _first(self, ins, outs, sems, i, me):
        cps = [self._copy(ins, outs, sems, i, 0, me, _flip(me, 1), own=True)]
        cps += [self._copy(ins, outs, sems, i, 1 + j, me, _flip(me, rel), own=True) for j, rel in enumerate(self.chips)]
        return cps

    def _passed(self, ins, outs, sems, i, j, me):
        return self._copy(ins, outs, sems, i, 4 + j, _flip(me, self.chips[j]), _flip(me, 1))

    def before(self, ins, outs, sems):
        n = len(self.inputs)
        me = _mesh_position()

        @pl.when(pl.program_id(0) == 0)
        def _():
            for i in range(n):
                self._local(ins, outs, sems, i, me).start()
                for cp in self._first(ins, outs, sems, i, me):
                    cp.start()

        @pl.when(pl.program_id(0) == pl.num_programs(0) - 1)
        def _():
            for j, rel in enumerate(self.chips):
                for i in range(n):
                    self._copy(ins, outs, sems, i, 1 + j, _flip(me, rel), me).wait_recv()
                    self._passed(ins, outs, sems, i, j, me).start()

    def after(self, ins, outs, sems):
        n = len(self.inputs)
        me = _mesh_position()
        sibling = _flip(me, 1)

        @pl.when(pl.program_id(0) == pl.num_programs(0) - 1)
        def _():
            for i in range(n):
                self._copy(ins, outs, sems, i, 0, sibling, me).wait_recv()
                for j, rel in enumerate(self.chips):
                    self._copy(ins, outs, sems, i, 4 + j, _flip(sibling, rel), me).wait_recv()
            for i in range(n):
                for cp in self._first(ins, outs, sems, i, me):
                    cp.wait_send()
                for j in range(len(self.chips)):
                    self._passed(ins, outs, sems, i, j, me).wait_send()
                self._local(ins, outs, sems, i, me).wait()


N_CHIPS = N_DEV // 2


def _chip(pos):
    return 2 * pos[0] + pos[1]


class _PairSwap:
    def __init__(self, arrays):
        self.inputs = list(arrays)
        self.rows = [a.shape[0] // N_DEV for a in arrays]
        for r in self.rows:
            assert r % FLAT_ROWS == 0, r
        self.out_shape = [jax.ShapeDtypeStruct((N_CHIPS, r, a.shape[1]), a.dtype) for a, r in zip(arrays, self.rows)]
        n = len(arrays)
        self.sems = [pltpu.SemaphoreType.DMA((n, N_CHIPS)), pltpu.SemaphoreType.DMA((n, N_CHIPS))]

    def _copy(self, ins, outs, sems, i, j, me):
        sibling = _flip(me, 1)
        return pltpu.make_async_remote_copy(
            src_ref=_block_of(ins[i], 2 * j + sibling[2], self.rows[i], True), dst_ref=outs[i].at[j],
            send_sem=sems[0].at[i, j], recv_sem=sems[1].at[i, j], device_id=sibling, device_id_type=MESH)

    def before(self, ins, outs, sems):
        me = _mesh_position()

        @pl.when(pl.program_id(0) == 0)
        def _():
            for i in range(len(self.inputs)):
                for j in range(N_CHIPS):
                    self._copy(ins, outs, sems, i, j, me).start()

    def after(self, ins, outs, sems):
        me = _mesh_position()

        @pl.when(pl.program_id(0) == pl.num_programs(0) - 1)
        def _():
            for i in range(len(self.inputs)):
                for j in range(N_CHIPS):
                    self._copy(ins, outs, sems, i, j, me).wait()


class _ChipExchange:
    chips = (4, 2, 6)

    def __init__(self, arrays):
        self.inputs = list(arrays)
        self.out_shape = [jax.ShapeDtypeStruct(a.shape, a.dtype) for a in arrays]
        n = len(arrays)
        self.sems = [pltpu.SemaphoreType.DMA((n, 3)), pltpu.SemaphoreType.DMA((n, 3)), pltpu.SemaphoreType.DMA((n,))]

    def _send(self, ins, outs, sems, i, k, me):
        peer = _flip(me, self.chips[k])
        return pltpu.make_async_remote_copy(
            src_ref=ins[i].at[_chip(peer)], dst_ref=outs[i].at[_chip(me)], send_sem=sems[0].at[i, k],
            recv_sem=sems[1].at[i, k], device_id=peer, device_id_type=MESH)

    def _arrival(self, ins, outs, sems, i, k, me):
        peer = _flip(me, self.chips[k])
        return pltpu.make_async_remote_copy(
            src_ref=ins[i].at[_chip(me)], dst_ref=outs[i].at[_chip(peer)], send_sem=sems[0].at[i, k],
            recv_sem=sems[1].at[i, k], device_id=peer, device_id_type=MESH)

    def _local(self, ins, outs, sems, i, me):
        return pltpu.make_async_copy(ins[i].at[_chip(me)], outs[i].at[_chip(me)], sems[2].at[i])

    def before(self, ins, outs, sems):
        me = _mesh_position()

        @pl.when(pl.program_id(0) == 0)
        def _():
            for i in range(len(self.inputs)):
                self._local(ins, outs, sems, i, me).start()
                for k in range(len(self.chips)):
                    self._send(ins, outs, sems, i, k, me).start()

    def after(self, ins, outs, sems):
        me = _mesh_position()

        @pl.when(pl.program_id(0) == pl.num_programs(0) - 1)
        def _():
            for i in range(len(self.inputs)):
                for k in range(len(self.chips)):
                    self._arrival(ins, outs, sems, i, k, me).wait_recv()
            for i in range(len(self.inputs)):
                for k in range(len(self.chips)):
                    self._send(ins, outs, sems, i, k, me).wait_send()
                self._local(ins, outs, sems, i, me).wait()


def _pair_add(grads, halves, name):
    n = len(grads)

    def body(*refs):
        g_refs, h_refs, o_refs = refs[:n], refs[n:2 * n], refs[2 * n:]
        c = lax.axis_index("c")
        for i in range(n):
            r = h_refs[i].shape[1]
            for j in range(N_CHIPS):
                own = g_refs[i][pl.ds(pl.multiple_of((2 * j + c) * r, FLAT_ROWS), r), :]
                o_refs[i][j] = (own.astype(F32) + h_refs[i][j].astype(F32)).astype(o_refs[i].dtype)

    return pl.pallas_call(
        body, name=name, out_shape=[jax.ShapeDtypeStruct(h.shape, h.dtype) for h in halves],
        compiler_params=pltpu.CompilerParams(vmem_limit_bytes=VMEM_LIMIT),
    )(*grads, *halves)


class _Both:
    def __init__(self, first, second):
        self.jobs = (first, second)
        self.inputs = first.inputs + second.inputs
        self.out_shape = first.out_shape + second.out_shape
        self.sems = first.sems + second.sems

    def _each(self, ins, outs, sems):
        a = self.jobs[0]
        i, o, s = len(a.inputs), len(a.out_shape), len(a.sems)
        return ((a, ins[:i], outs[:o], sems[:s]), (self.jobs[1], ins[i:], outs[o:], sems[s:]))

    def before(self, ins, outs, sems):
        for job, i, o, s in self._each(ins, outs, sems):
            job.before(i, o, s)

    def after(self, ins, outs, sems):
        for job, i, o, s in self._each(ins, outs, sems):
            job.after(i, o, s)


def _run(body, comm, *, semantics, out_shape, in_specs, out_specs, scratch_shapes=(), **kw):
    if comm is None:
        return pl.pallas_call(body, out_shape=out_shape, in_specs=in_specs, out_specs=out_specs,
                              scratch_shapes=list(scratch_shapes), compiler_params=_params(semantics), **kw)
    single = not isinstance(out_shape, (list, tuple))
    outs = [out_shape] if single else list(out_shape)
    ospecs = [out_specs] if single else list(out_specs)
    counts = [len(in_specs), len(comm.inputs), len(outs), len(comm.out_shape), len(scratch_shapes), len(comm.sems)]

    def carrying(*refs):
        groups, pos = [], 0
        for c in counts:
            groups.append(refs[pos:pos + c])
            pos += c
        main_in, comm_in, main_out, comm_out, main_scratch, comm_sems = groups
        comm.before(comm_in, comm_out, comm_sems)
        body(*main_in, *main_out, *main_scratch)
        comm.after(comm_in, comm_out, comm_sems)

    call = pl.pallas_call(
        carrying, out_shape=outs + list(comm.out_shape), in_specs=list(in_specs) + [_ANY] * len(comm.inputs),
        out_specs=ospecs + [_ANY] * len(comm.out_shape), scratch_shapes=list(scratch_shapes) + list(comm.sems),
        compiler_params=_params("arbitrary"), **kw)

    def apply(*args):
        res = call(*args, *comm.inputs)
        main = res[:len(outs)]
        return (main[0] if single else list(main)), list(res[len(outs):])

    return apply


def _alone(comm, name):
    return _run(lambda: None, comm, semantics="arbitrary", name=name, grid=(1,), out_shape=[], in_specs=[], out_specs=[])()[1]


SHARDED = {"w_in": 1, "w_glu": 0, "conv_w": 1, "w_a_out": 1, "w_b_out": 0, "w_o": 0, "w_ffn_gate": 1, "w_ffn_up": 1,
           "w_ffn_down": 0, "w_ple_gate": 0, "w_ple": 1}
TRANSPOSED = ("w_in", "w_a_out", "w_ffn_gate", "w_ffn_up", "w_ple")
SMALL = ["g_mix", "b_in", "lam_re", "lam_im", "log_dt", "s5_b_re", "s5_b_im", "s5_c_re", "s5_c_im", "s5_d", "b_glu",
         "conv_b", "w_r", "b_r", "w_i", "b_i", "lru_lambda", "g_ffn", "g_ple_gate", "b_ple_gate", "g_ple", "g_final"]
WEIGHTS = ["g_mix", "w_in", "b_in", "lam_re", "lam_im", "log_dt", "s5_b_re", "s5_b_im", "s5_c_re", "s5_c_im", "s5_d",
           "w_glu", "b_glu", "conv_w", "conv_b", "w_r", "b_r", "w_i", "b_i", "lru_lambda", "w_a_out", "w_b_out", "w_o",
           "g_ffn", "w_ffn_gate", "w_ffn_up", "w_ffn_down", "g_ple_gate", "w_ple_gate", "b_ple_gate", "w_ple", "g_ple",
           "g_final"]


def _unblock(gathered, axis):
    nb, r, c = gathered.shape
    if axis == 0:
        return gathered.reshape(nb * r, c)
    return jnp.transpose(gathered, (1, 0, 2)).reshape(r, nb * c)


def _disc_scalars(lr, li, ldt):
    dt = jnp.exp(ldt)
    mag = jnp.exp(lr * dt)
    ar = mag * jnp.cos(li * dt)
    ai = mag * jnp.sin(li * dt)
    den = lr * lr + li * li
    nr = ar - 1.0
    fr = (nr * lr + ai * li) / den
    fi = (ai * lr - nr * li) / den
    return ar, ai, fr, fi


def _disc_cols(lr, li, ldt, b_re, b_im):
    ar, ai, fr, fi = _disc_scalars(lr, li, ldt)
    return ar, ai, fr * b_re - fi * b_im, fr * b_im + fi * b_re


def _local_step(x, p, target, src, small, disc, distributed=True):
    full = {} if distributed else dict(src)
    gw, halves, pairs, got = {}, {}, {}, {}

    def gather(keys):
        return (_Gather([src[k] for k in keys]), keys, full) if distributed else None

    def swap(keys):
        return (_PairSwap([gw[k] for k in keys]), keys, halves) if distributed else None

    def chips(keys):
        return (_ChipExchange([pairs[k] for k in keys]), keys, got) if distributed else None

    def add_pairs(keys):
        if distributed:
            pairs.update(zip(keys, _pair_add([gw[k] for k in keys], [halves[k] for k in keys], "pair_add_" + keys[0])))

    def carry(fn, *args, jobs=()):
        jobs = [j for j in jobs if j is not None]
        if not jobs:
            return fn(*args)
        comm = jobs[0][0]
        for j in jobs[1:]:
            comm = _Both(comm, j[0])
        res, extra = fn(*args, comm=comm)
        for job, keys, sink in jobs:
            sink.update(zip(keys, extra[:len(job.out_shape)]))
            extra = extra[len(job.out_shape):]
        return res

    d = x.shape[1]
    g, n, pch = small["s5_b_re"].shape
    heads = small["w_r"].shape[0]
    sa, lw = g * pch, small["lru_lambda"].shape[-1]
    widths = [sa, lw, d, d]
    row = lambda v: v.reshape(1, -1)

    hd = small["w_r"].shape[-1]
    ar_row, ai_row, bbr_blk, bbi_blk, cre_blk, cimn_blk, wr_blk, wi_blk = carry(
        _prep, *disc["rows"], *disc["cols"], disc["b_re"], disc["b_im"], small["s5_c_re"].reshape(sa, n),
        small["s5_c_im"].reshape(sa, n), small["w_r"].reshape(lw, hd), small["w_i"].reshape(lw, hd),
        jobs=[gather(["w_in"])])
    d_row = row(small["s5_d"])
    u_a, u_b, za, zb = carry(_inproj_fwd, x, row(small["g_mix"]), full["w_in"], row(small["b_in"]), widths,
                             jobs=[gather(["w_glu", "conv_w", "w_a_out"])])
    conv_w = _unblock(full["conv_w"], 1) if distributed else full["conv_w"]
    sr, si, y, y_a = carry(_s5_fwd, u_a, bbr_blk, bbi_blk, ar_row, ai_row, cre_blk, cimn_blk, d_row, full["w_glu"],
                           row(small["b_glu"]), jobs=[gather(["w_ffn_gate", "w_b_out"])])
    xc, h, hprev = carry(_lru_fwd, u_b, conv_w, row(small["conv_b"]), wr_blk, row(small["b_r"]), wi_blk,
                         row(small["b_i"]), row(small["lru_lambda"]), jobs=[gather(["w_ffn_up", "w_o"])])
    x1, merged, ma, mb = _merge_fwd(y_a, h, za, zb, x, full["w_a_out"], full["w_b_out"], full["w_o"])
    fg, fu = carry(_ffn_up_fwd, x1, row(small["g_ffn"]), full["w_ffn_gate"], full["w_ffn_up"],
                   jobs=[gather(["w_ffn_down"])])
    x2 = carry(_ffn_down_fwd, fg, fu, x1, full["w_ffn_down"], jobs=[gather(["w_ple_gate", "w_ple"])])

    dx2, loss_blk, gw["w_ple_gate"], gw["w_ple"], vec_tail = _tail_fwd_bwd(
        x2, p, target, row(small["g_ple_gate"]), full["w_ple_gate"], row(small["b_ple_gate"]), full["w_ple"],
        row(small["g_ple"]), row(small["g_final"]))
    dfg, dfu, act = carry(_ffn_bwd_a, dx2, fg, fu, full["w_ffn_down"], jobs=[swap(["w_ple_gate", "w_ple"])])
    tn_d = min(d, 512)
    gw["w_ffn_down"] = _matmul_tn(act, dx2, tn_d, "dw_ffn_down", BF16)
    add_pairs(["w_ple_gate", "w_ple"])
    dx1, h2, vec_ffn = carry(_ffn_bwd_b, dfg, dfu, x1, dx2, row(small["g_ffn"]), full["w_ffn_gate"], full["w_ffn_up"],
                             jobs=[chips(["w_ple_gate", "w_ple"]), swap(["w_ffn_down"])])
    gw["w_ffn_gate"] = _matmul_tn(dfg, h2, tn_d, "dw_ffn_gate", BF16)
    gw["w_ffn_up"] = _matmul_tn(dfu, h2, tn_d, "dw_ffn_up", BF16)
    add_pairs(["w_ffn_down"])
    dza, dzb, dya, dyb, gw["w_o"], gw["w_a_out"], gw["w_b_out"] = carry(
        _merge_bwd, dx1, merged, ma, mb, za, zb, y_a, h, full["w_o"], full["w_a_out"], full["w_b_out"],
        jobs=[chips(["w_ffn_down"]), swap(["w_ffn_gate", "w_ffn_up"])])
    add_pairs(["w_ffn_gate", "w_ffn_up"])
    du_b, dw_r, dw_i, vec_lru = carry(
        _lru_bwd, dyb, xc, hprev, u_b, conv_w, wr_blk, row(small["b_r"]), wi_blk, row(small["b_i"]),
        row(small["lru_lambda"]), hd, jobs=[chips(["w_ffn_gate", "w_ffn_up"]), swap(["w_o", "w_a_out", "w_b_out"])])
    add_pairs(["w_o", "w_a_out", "w_b_out"])
    du_a, lam_r, lam_i, dy16, gw["w_glu"], vec_sa, vec_gn = carry(
        _s5_bwd, dya, y, sr, si, u_a, full["w_glu"], row(small["b_glu"]), cre_blk, cimn_blk, bbr_blk, bbi_blk, ar_row,
        ai_row, d_row, jobs=[chips(["w_o", "w_a_out", "w_b_out"])])
    smalls = {"vec_tail": vec_tail, "vec_ffn": vec_ffn, "vec_lru": vec_lru, "vec_sa": vec_sa, "vec_gn": vec_gn,
              "dw_r": dw_r, "dw_i": dw_i, "loss": loss_blk}
    everyones = {}

    def gather_smalls(keys):
        return (_Gather([smalls[k] for k in keys]), keys, everyones) if distributed else None

    smalls["dbb_re"], smalls["dbb_im"], smalls["dc_re"], smalls["dc_imn"] = carry(
        _s5_param_grads, lam_r, lam_i, sr, si, u_a, dy16, pch, n, jobs=[swap(["w_glu"]), gather_smalls(list(smalls))])
    add_pairs(["w_glu"])
    grad_x, h0, dz16, smalls["vec_mix"], smalls["vec_bin"] = _inproj_bwd(
        [du_a, du_b, dza, dzb], x, dx1, row(small["g_mix"]), full["w_in"])
    shapes = {k: a.shape for k, a in smalls.items()}
    gw["w_in"] = carry(_matmul_tn, dz16, h0, tn_d, "dw_in", BF16,
                       jobs=[chips(["w_glu"]), gather_smalls(["dbb_re", "dbb_im", "dc_re", "dc_imn"])])
    smalls.update(everyones)
    if distributed:
        got["w_in"] = gw["w_in"]
        gw = got
    return grad_x, gw, smalls, shapes


def _disc_inputs(small):
    g, n, pch = small["s5_b_re"].shape
    srcs = (small["lam_re"], small["lam_im"], jnp.repeat(small["log_dt"], n))
    return {"rows": [a.reshape(1, g * n) for a in srcs], "cols": [a.reshape(g * n, 1) for a in srcs],
            "b_re": small["s5_b_re"].reshape(g * n, pch), "b_im": small["s5_b_im"].reshape(g * n, pch)}


def kernel(x, p, g_mix, w_in, b_in, lam_re, lam_im, log_dt, s5_b_re, s5_b_im, s5_c_re, s5_c_im, s5_d, w_glu, b_glu, conv_w, conv_b, w_r, b_r, w_i, b_i, lru_lambda, w_a_out, w_b_out, w_o, g_ffn, w_ffn_gate, w_ffn_up, w_ffn_down, g_ple_gate, w_ple_gate, b_ple_gate, w_ple, g_ple, g_final, loss_target, m_g_mix, m_w_in, m_b_in, m_lam_re, m_lam_im, m_log_dt, m_s5_b_re, m_s5_b_im, m_s5_c_re, m_s5_c_im, m_s5_d, m_w_glu, m_b_glu, m_conv_w, m_conv_b, m_w_r, m_b_r, m_w_i, m_b_i, m_lru_lambda, m_w_a_out, m_w_b_out, m_w_o, m_g_ffn, m_w_ffn_gate, m_w_ffn_up, m_w_ffn_down, m_g_ple_gate, m_w_ple_gate, m_b_ple_gate, m_w_ple, m_g_ple, m_g_final, v_g_mix, v_w_in, v_b_in, v_lam_re, v_lam_im, v_log_dt, v_s5_b_re, v_s5_b_im, v_s5_c_re, v_s5_c_im, v_s5_d, v_w_glu, v_b_glu, v_conv_w, v_conv_b, v_w_r, v_b_r, v_w_i, v_b_i, v_lru_lambda, v_w_a_out, v_w_b_out, v_w_o, v_g_ffn, v_w_ffn_gate, v_w_ffn_up, v_w_ffn_down, v_g_ple_gate, v_w_ple_gate, v_b_ple_gate, v_w_ple, v_g_ple, v_g_final):
    given = dict(locals())
    wts = {k: given[k] for k in WEIGHTS}
    moms = {k: given["m_" + k] for k in WEIGHTS}
    vels = {k: given["v_" + k] for k in WEIGHTS}

    def drop_depth(k, a):
        return a if k == "g_final" else a[0]

    small = {k: drop_depth(k, wts[k]) for k in SMALL}
    shard = {k: wts[k][0] for k in SHARDED}
    names = list(SHARDED)

    def wire(k):
        if k == "conv_w":
            return shard[k]
        return (shard[k].T if k in TRANSPOSED else shard[k]).astype(BF16)

    disc = _disc_inputs(small)
    grad_x, parts, smalls, shapes = _local_step(x[0], p[0, 0], loss_target[0], {k: wire(k) for k in names}, small, disc)

    late = ["vec_mix", "vec_bin"]
    received = _alone(_Both(_PairSwap([parts["w_in"]]), _Gather([smalls[k] for k in late])), "swap_last")
    (pair_in,) = _pair_add([parts["w_in"]], received[:1], "pair_add_w_in")
    smalls.update(zip(late, received[1:]))

    g_small, (parts["w_in"],) = _small_reduce(smalls, shapes, *disc["cols"], disc["b_re"], disc["b_im"],
                                              small["s5_b_re"].shape[0], comm=_ChipExchange([pair_in]))
    loss = g_small.pop("loss")[0, 0]
    cols = shard["conv_w"].shape[1]
    mine = _index((lax.axis_index("x"), lax.axis_index("y"), lax.axis_index("c")))
    parts["conv_w"] = lax.dynamic_slice_in_dim(g_small.pop("conv_w"), mine * cols, cols, axis=1)[None]
    natural = lambda k, a: a.reshape((1, -1) if k == "g_final" else wts[k].shape)
    slots = _adamw_small([natural(k, wts[k]) for k in SMALL], [natural(k, g_small[k]) for k in SMALL],
                         [natural(k, moms[k]) for k in SMALL], [natural(k, vels[k]) for k in SMALL])
    small_out = [dict(zip(SMALL, [a.reshape(wts[k].shape) for k, a in zip(SMALL, slot)])) for slot in slots]

    big_out = {}
    for k in names:
        big_out[k] = _adamw(parts[k], shard[k], moms[k][0], vels[k][0], "adamw_" + k, transposed=k in TRANSPOSED)

    outs = [loss, grad_x[None]]
    for slot in range(4):
        for k in WEIGHTS:
            if k in SHARDED:
                outs.append(big_out[k][slot][None])
            else:
                outs.append(small_out[slot][k])
    return tuple(outs)
```

```python
import math

import jax
import jax.numpy as jnp
from jax import lax
from jax.experimental import pallas as pl
from jax.experimental.pallas import tpu as pltpu

F32 = jnp.float32
BF16 = jnp.bfloat16

EPS = 1e-6
LRU_C = 8.0
CONV_WIDTH = 4
ADAM_LR = 0.001
ADAM_B1 = 0.9
ADAM_B2 = 0.999
ADAM_EPS = 1e-08
ADAM_WD = 0.01
ADAM_STEP = 10

N_DEV = 8
MESH = pl.DeviceIdType.MESH
SUBLANES = 8
LANES = 128
VMEM_LIMIT = 56 * 1024 * 1024
TOKEN_TILE = 256
TIME_CHUNK = 256
S5_SLAB = 128
LRU_SLAB = 256


def _dot(a, b):
    return jnp.dot(a.astype(BF16), b.astype(BF16), preferred_element_type=F32)


def _dot_nt(a, b):
    return lax.dot_general(a.astype(BF16), b.astype(BF16), (((1,), (1,)), ((), ())), preferred_element_type=F32)


def _dot_tn(a, b):
    return lax.dot_general(a.astype(BF16), b.astype(BF16), (((0,), (0,)), ((), ())), preferred_element_type=F32)


def _sigmoid(x):
    return jax.nn.sigmoid(x)


def _rms_stats(x):
    r = lax.rsqrt(jnp.mean(x * x, axis=-1, keepdims=True) + EPS)
    return x * r, r


def _rms_bwd(dy, xhat, r, g):
    dxn = dy * g
    dx = r * (dxn - xhat * jnp.mean(dxn * xhat, axis=-1, keepdims=True))
    return dx, dy * xhat


def _rowsum(v):
    return jnp.sum(v, axis=0, keepdims=True)


def _expm1(x):
    u = jnp.exp(x)
    um1 = u - 1.0
    safe = jnp.where(um1 == 0.0, 1.0, jnp.log(u))
    return jnp.where(um1 == 0.0, x, um1 * x / safe)


def _softplus(x):
    e = jnp.exp(-jnp.abs(x))
    u = 1.0 + e
    um1 = u - 1.0
    safe = jnp.where(um1 == 0.0, 1.0, um1)
    log1p_e = jnp.where(um1 == 0.0, e, jnp.log(u) * e / safe)
    return jnp.maximum(x, 0.0) + log1p_e


_GELU_K = math.sqrt(2.0 / math.pi)
_GELU_C = 0.044715


def _gelu(x):
    return 0.5 * x * (1.0 + jnp.tanh(_GELU_K * (x + _GELU_C * x * x * x)))


def _gelu_grad(x):
    th = jnp.tanh(_GELU_K * (x + _GELU_C * x * x * x))
    return 0.5 * (1.0 + th) + 0.5 * x * (1.0 - th * th) * _GELU_K * (1.0 + 3.0 * _GELU_C * x * x)


def _params(*sem):
    return pltpu.CompilerParams(dimension_semantics=sem, vmem_limit_bytes=VMEM_LIMIT)


def _rows(tm, n):
    return pl.BlockSpec((tm, n), lambda i: (i, 0))


def _rows_rev(tm, n, steps):
    return pl.BlockSpec((tm, n), lambda i: (steps - 1 - i, 0))


def _whole(shape):
    nd = len(shape)
    return pl.BlockSpec(shape, lambda i: (0,) * nd, pipeline_mode=pl.Buffered(1))


def _acc(shape):
    nd = len(shape)
    return pl.BlockSpec(shape, lambda i: (0,) * nd)


def _zero_on_first(*refs):
    @pl.when(pl.program_id(0) == 0)
    def _():
        for r in refs:
            r[...] = jnp.zeros_like(r)


def _inproj_fwd(x, g_mix, w_in_t, b_in, widths, comm=None):
    t, d = x.shape
    n = w_in_t.shape[0]
    tm = min(TOKEN_TILE, t)
    offs = [sum(widths[:i]) for i in range(len(widths) + 1)]

    def body(x_ref, g_ref, w_ref, b_ref, *outs):
        xhat, _ = _rms_stats(x_ref[...])
        h = (xhat * g_ref[...]).astype(BF16)
        for k, o_ref in enumerate(outs):
            lo, hi = offs[k], offs[k + 1]
            o_ref[...] = _dot_nt(h, w_ref[lo:hi, :]) + b_ref[:, lo:hi]

    return _run(
        body, comm, name="inproj_fwd", grid=(t // tm,),
        out_shape=[jax.ShapeDtypeStruct((t, w), F32) for w in widths],
        in_specs=[_rows(tm, d), _whole((1, d)), _whole((n, d)), _whole((1, n))],
        out_specs=[_rows(tm, w) for w in widths],
        semantics="parallel",
    )(x, g_mix, w_in_t, b_in)


def _s5_fwd(u, bbr_blk, bbi_blk, ar, ai, cre_blk, cimn_blk, d_skip, w_glu, b_glu, comm=None):
    t, sa = u.shape
    ns, _, sw = bbr_blk.shape
    gn = ns * sw
    tc = min(TIME_CHUNK, t)

    def body(u_ref, bbr_ref, bbi_ref, ar_ref, ai_ref, cre_ref, cim_ref, d_ref, wg_ref, bg_ref,
             sr_ref, si_ref, y_ref, ya_ref, cr_s, ci_s):
        _zero_on_first(cr_s, ci_s)
        uv = u_ref[...]
        ub = uv.astype(BF16)
        for m in range(ns):
            um = ub[:, m * S5_SLAB:(m + 1) * S5_SLAB]
            sr_ref[:, m * sw:(m + 1) * sw] = _dot(um, bbr_ref[m])
            si_ref[:, m * sw:(m + 1) * sw] = _dot(um, bbi_ref[m])
        a_r = ar_ref[...]
        a_i = ai_ref[...]

        def step(row, carry):
            c_r, c_i = carry
            at = pl.ds(row, 1)
            n_r = a_r * c_r - a_i * c_i + sr_ref[at, :]
            n_i = a_r * c_i + a_i * c_r + si_ref[at, :]
            sr_ref[at, :] = n_r
            si_ref[at, :] = n_i
            return n_r, n_i

        c_r, c_i = lax.fori_loop(0, tc, step, (cr_s[0:1, :], ci_s[0:1, :]), unroll=8)
        cr_s[0:1, :] = c_r
        ci_s[0:1, :] = c_i
        for m in range(ns):
            states, chans = slice(m * sw, (m + 1) * sw), slice(m * S5_SLAB, (m + 1) * S5_SLAB)
            y_ref[:, chans] = (_dot(sr_ref[:, states], cre_ref[m]) + _dot(si_ref[:, states], cim_ref[m])
                               + d_ref[:, chans] * uv[:, chans])
        y = y_ref[...]
        zz = _gelu(y)
        q = _dot(zz, wg_ref[...]) + bg_ref[...]
        ya_ref[...] = zz * _sigmoid(q)

    return _run(
        body, comm, name="s5_fwd", grid=(t // tc,),
        out_shape=[jax.ShapeDtypeStruct((t, gn), F32), jax.ShapeDtypeStruct((t, gn), F32),
                   jax.ShapeDtypeStruct((t, sa), F32), jax.ShapeDtypeStruct((t, sa), F32)],
        in_specs=[_rows(tc, sa), _whole(bbr_blk.shape), _whole(bbi_blk.shape), _whole((1, gn)), _whole((1, gn)),
                  _whole(cre_blk.shape), _whole(cimn_blk.shape), _whole((1, sa)), _whole((sa, sa)), _whole((1, sa))],
        out_specs=[_rows(tc, gn), _rows(tc, gn), _rows(tc, sa), _rows(tc, sa)],
        scratch_shapes=[pltpu.VMEM((SUBLANES, gn), F32), pltpu.VMEM((SUBLANES, gn), F32)],
        semantics="arbitrary",
    )(u, bbr_blk, bbi_blk, ar, ai, cre_blk, cimn_blk, d_skip, w_glu, b_glu)


def _slab_dot(x, w_ref, transposed=False):
    dot = _dot_nt if transposed else _dot
    xb = x.astype(BF16)
    return jnp.concatenate([dot(xb[:, j * LRU_SLAB:(j + 1) * LRU_SLAB], w_ref[j]) for j in range(w_ref.shape[0])], axis=1)


def _lru_gates(xc, wr_ref, br_ref, wi_ref, bi_ref, lam_ref):
    r = _sigmoid(_slab_dot(xc, wr_ref) + br_ref[...])
    ig = _sigmoid(_slab_dot(xc, wi_ref) + bi_ref[...])
    sp = _softplus(-lam_ref[...])
    log_a = (-LRU_C * r) * sp
    return r, ig, sp, log_a


def _lru_fwd(u, conv_w, conv_b, wr_blk, b_r, wi_blk, b_i, lru_lambda, comm=None):
    t, w = u.shape
    tc = min(TIME_CHUNK, t)
    halo = SUBLANES

    def body(u_ref, cw_ref, cb_ref, wr_ref, br_ref, wi_ref, bi_ref, lam_ref,
             xc_ref, h_ref, hp_ref, ext_s, a_s, carry_s):
        @pl.when(pl.program_id(0) == 0)
        def _():
            ext_s[0:halo, :] = jnp.zeros((halo, w), F32)
            carry_s[...] = jnp.zeros_like(carry_s)

        ext_s[halo:halo + tc, :] = u_ref[...]
        xc = cb_ref[...]
        for k in range(CONV_WIDTH):
            off = halo - (CONV_WIDTH - 1) + k
            xc = xc + cw_ref[k:k + 1, :] * ext_s[off:off + tc, :]
        ext_s[0:halo, :] = ext_s[tc:tc + halo, :]
        xc_ref[...] = xc
        r, ig, sp, log_a = _lru_gates(xc, wr_ref, br_ref, wi_ref, bi_ref, lam_ref)
        a_s[...] = jnp.exp(log_a)
        h_ref[...] = jnp.sqrt(-_expm1(2.0 * log_a)) * ig * xc

        def step(row, carry):
            at = pl.ds(row, 1)
            hp_ref[at, :] = carry
            nxt = a_s[at, :] * carry + h_ref[at, :]
            h_ref[at, :] = nxt
            return nxt

        carry_s[0:1, :] = lax.fori_loop(0, tc, step, carry_s[0:1, :], unroll=8)

    return _run(
        body, comm, name="lru_fwd", grid=(t // tc,),
        out_shape=[jax.ShapeDtypeStruct((t, w), F32)] * 3,
        in_specs=[_rows(tc, w), _whole((CONV_WIDTH, w)), _whole((1, w)), _whole(wr_blk.shape), _whole((1, w)),
                  _whole(wi_blk.shape), _whole((1, w)), _whole((1, w))],
        out_specs=[_rows(tc, w)] * 3,
        scratch_shapes=[pltpu.VMEM((halo + tc, w), F32), pltpu.VMEM((tc, w), F32), pltpu.VMEM((SUBLANES, w), F32)],
        semantics="arbitrary",
    )(u, conv_w, conv_b, wr_blk, b_r, wi_blk, b_i, lru_lambda)


def _merge_fwd(y_a, h, za, zb, x, w_a_out_t, w_b_out, w_o, comm=None):
    t, d = x.shape
    sa, lw = y_a.shape[1], h.shape[1]
    tm = min(TOKEN_TILE, t)

    def body(ya_ref, h_ref, za_ref, zb_ref, x_ref, wa_ref, wb_ref, wo_ref, x1_ref, mg_ref, ma_ref, mb_ref):
        ma = _dot_nt(ya_ref[...], wa_ref[...])
        mb = _dot(h_ref[...], wb_ref[...])
        merged = _sigmoid(za_ref[...]) * ma + _sigmoid(zb_ref[...]) * mb
        ma_ref[...] = ma
        mb_ref[...] = mb
        mg_ref[...] = merged.astype(mg_ref.dtype)
        x1_ref[...] = x_ref[...] + _dot(merged, wo_ref[...])

    return _run(
        body, comm, name="merge_fwd", grid=(t // tm,),
        out_shape=[jax.ShapeDtypeStruct((t, d), F32), jax.ShapeDtypeStruct((t, d), BF16),
                   jax.ShapeDtypeStruct((t, d), F32), jax.ShapeDtypeStruct((t, d), F32)],
        in_specs=[_rows(tm, sa), _rows(tm, lw), _rows(tm, d), _rows(tm, d), _rows(tm, d),
                  _whole((d, sa)), _whole((lw, d)), _whole((d, d))],
        out_specs=[_rows(tm, d)] * 4,
        semantics="parallel",
    )(y_a, h, za, zb, x, w_a_out_t, w_b_out, w_o)


def _ffn_up_fwd(x1, g_ffn, w_gate_t, w_up_t, comm=None):
    t, d = x1.shape
    f = w_gate_t.shape[0]
    tm = min(TOKEN_TILE, t)

    def body(x_ref, g_ref, wg_ref, wu_ref, fg_ref, fu_ref):
        xhat, _ = _rms_stats(x_ref[...])
        h2 = (xhat * g_ref[...]).astype(BF16)
        fg_ref[...] = _dot_nt(h2, wg_ref[...])
        fu_ref[...] = _dot_nt(h2, wu_ref[...])

    return _run(
        body, comm, name="ffn_up_fwd", grid=(t // tm,),
        out_shape=[jax.ShapeDtypeStruct((t, f), F32)] * 2,
        in_specs=[_rows(tm, d), _whole((1, d)), _whole((f, d)), _whole((f, d))],
        out_specs=[_rows(tm, f)] * 2,
        semantics="parallel",
    )(x1, g_ffn, w_gate_t, w_up_t)


def _ffn_down_fwd(fg, fu, x1, w_down, comm=None):
    t, d = x1.shape
    f = fg.shape[1]
    tm = min(TOKEN_TILE, t)

    def body(fg_ref, fu_ref, x_ref, wd_ref, x2_ref):
        fgv = fg_ref[...]
        act = fgv * _sigmoid(fgv) * fu_ref[...]
        x2_ref[...] = x_ref[...] + _dot(act, wd_ref[...])

    return _run(
        body, comm, name="ffn_down_fwd", grid=(t // tm,),
        out_shape=jax.ShapeDtypeStruct((t, d), F32),
        in_specs=[_rows(tm, f), _rows(tm, f), _rows(tm, d), _whole((f, d))],
        out_specs=_rows(tm, d),
        semantics="parallel",
    )(fg, fu, x1, w_down)


def _store_on_last(pairs):
    @pl.when(pl.program_id(0) == pl.num_programs(0) - 1)
    def _():
        for acc, out in pairs:
            out[...] = acc[...].astype(out.dtype)


def _tail_fwd_bwd(x2, p, target, g_pg, w_pg, b_pg, w_ple_t, g_ple, g_final):
    t, d = x2.shape
    pd = p.shape[1]
    tm = min(TOKEN_TILE, t)

    def body(x2_ref, p_ref, tg_ref, gpg_ref, wpg_ref, bpg_ref, wple_ref, gple_ref, gfin_ref,
             dx2_ref, loss_ref, dwpg_out, dwple_out, vec_ref, dwpg_ref, dwple_ref):
        _zero_on_first(loss_ref, dwpg_ref, dwple_ref, vec_ref)
        x2v = x2_ref[...]
        xh2, r2 = _rms_stats(x2v)
        h3 = xh2 * gpg_ref[...]
        gp = _sigmoid(_dot(h3, wpg_ref[...]) + bpg_ref[...])
        pe = _dot_nt(p_ref[...], wple_ref[...])
        peh, r3 = _rms_stats(pe)
        e = peh * gple_ref[...]
        x3 = x2v + gp * e
        xh3, r4 = _rms_stats(x3)
        diff = xh3 * gfin_ref[...] - tg_ref[...]
        loss_ref[...] += 0.5 * jnp.sum(jnp.mean(diff * diff, axis=-1, keepdims=True))
        dy = diff * (1.0 / d)
        dx3, dgfin = _rms_bwd(dy, xh3, r4, gfin_ref[...])
        d_gp = dx3 * e
        d_e = dx3 * gp
        dpe, dgple = _rms_bwd(d_e, peh, r3, gple_ref[...])
        dwple_ref[...] += _dot_tn(dpe, p_ref[...])
        dpre = d_gp * gp * (1.0 - gp)
        dwpg_ref[...] += _dot_tn(h3, dpre)
        dh3 = _dot_nt(dpre, wpg_ref[...])
        dx2n, dgpg = _rms_bwd(dh3, xh2, r2, gpg_ref[...])
        dx2_ref[...] = dx3 + dx2n
        vec_ref[0:1, :] += _rowsum(dpre)
        vec_ref[1:2, :] += _rowsum(dgpg)
        vec_ref[2:3, :] += _rowsum(dgple)
        vec_ref[3:4, :] += _rowsum(dgfin)
        _store_on_last([(dwpg_ref, dwpg_out), (dwple_ref, dwple_out)])

    return pl.pallas_call(
        body, name="tail_fwd_bwd", grid=(t // tm,),
        out_shape=[jax.ShapeDtypeStruct((t, d), F32), jax.ShapeDtypeStruct((SUBLANES, LANES), F32),
                   jax.ShapeDtypeStruct((d, d), BF16), jax.ShapeDtypeStruct((d, pd), BF16),
                   jax.ShapeDtypeStruct((SUBLANES, d), F32)],
        in_specs=[_rows(tm, d), _rows(tm, pd), _rows(tm, d), _whole((1, d)), _whole((d, d)), _whole((1, d)),
                  _whole((d, pd)), _whole((1, d)), _whole((1, d))],
        out_specs=[_rows(tm, d), _acc((SUBLANES, LANES)), _acc((d, d)), _acc((d, pd)), _acc((SUBLANES, d))],
        scratch_shapes=[pltpu.VMEM((d, d), F32), pltpu.VMEM((d, pd), F32)],
        compiler_params=_params("arbitrary"),
    )(x2, p, target, g_pg, w_pg, b_pg, w_ple_t, g_ple, g_final)


def _ffn_bwd_a(dx2, fg, fu, w_down, comm=None):
    t, d = dx2.shape
    f = fg.shape[1]
    tm = min(TOKEN_TILE, t)

    def body(dx_ref, fg_ref, fu_ref, wd_ref, dfg_ref, dfu_ref, act_ref):
        dact = _dot_nt(dx_ref[...], wd_ref[...])
        fgv = fg_ref[...]
        fuv = fu_ref[...]
        sg = _sigmoid(fgv)
        silu = fgv * sg
        dfu_ref[...] = (dact * silu).astype(dfu_ref.dtype)
        dfg_ref[...] = (dact * fuv * (sg * (1.0 + fgv * (1.0 - sg)))).astype(dfg_ref.dtype)
        act_ref[...] = (silu * fuv).astype(act_ref.dtype)

    return _run(
        body, comm, name="ffn_bwd_a", grid=(t // tm,),
        out_shape=[jax.ShapeDtypeStruct((t, f), BF16)] * 3,
        in_specs=[_rows(tm, d), _rows(tm, f), _rows(tm, f), _whole((f, d))],
        out_specs=[_rows(tm, f)] * 3,
        semantics="parallel",
    )(dx2, fg, fu, w_down)


def _ffn_bwd_b(dfg, dfu, x1, dx2, g_ffn, w_gate_t, w_up_t, comm=None):
    t, d = x1.shape
    f = dfg.shape[1]
    tm = min(TOKEN_TILE, t)

    def body(dfg_ref, dfu_ref, x_ref, dx2_ref, g_ref, wg_ref, wu_ref, dx1_ref, h2_ref, vec_ref):
        _zero_on_first(vec_ref)
        dh2 = _dot(dfg_ref[...], wg_ref[...]) + _dot(dfu_ref[...], wu_ref[...])
        xhat, r = _rms_stats(x_ref[...])
        h2_ref[...] = (xhat * g_ref[...]).astype(h2_ref.dtype)
        dxn, dg = _rms_bwd(dh2, xhat, r, g_ref[...])
        dx1_ref[...] = dx2_ref[...] + dxn
        vec_ref[0:1, :] += _rowsum(dg)

    return _run(
        body, comm, name="ffn_bwd_b", grid=(t // tm,),
        out_shape=[jax.ShapeDtypeStruct((t, d), F32), jax.ShapeDtypeStruct((t, d), BF16),
                   jax.ShapeDtypeStruct((SUBLANES, d), F32)],
        in_specs=[_rows(tm, f), _rows(tm, f), _rows(tm, d), _rows(tm, d), _whole((1, d)), _whole((f, d)), _whole((f, d))],
        out_specs=[_rows(tm, d), _rows(tm, d), _acc((SUBLANES, d))],
        semantics="arbitrary",
    )(dfg, dfu, x1, dx2, g_ffn, w_gate_t, w_up_t)


def _matmul_tn(a, b, tn, name, dtype=F32, comm=None):
    t, k = a.shape
    n = b.shape[1]

    def body(a_ref, b_ref, o_ref):
        o_ref[...] = _dot_tn(a_ref[...], b_ref[...]).astype(o_ref.dtype)

    return _run(
        body, comm, name=name, grid=(n // tn,),
        out_shape=jax.ShapeDtypeStruct((k, n), dtype),
        in_specs=[_whole((t, k)), pl.BlockSpec((t, tn), lambda j: (0, j))],
        out_specs=pl.BlockSpec((k, tn), lambda j: (0, j)),
        semantics="parallel",
    )(a, b)


def _merge_bwd(dx1, merged, ma, mb, za, zb, y_a, h, w_o, w_a_out_t, w_b_out, comm=None):
    t, d = dx1.shape
    sa, lw = y_a.shape[1], h.shape[1]
    tm = min(TOKEN_TILE, t)

    def body(dx1_ref, mg_ref, ma_ref, mb_ref, za_ref, zb_ref, ya_ref, h_ref, wo_ref, wa_ref, wb_ref,
             dza_ref, dzb_ref, dya_ref, dyb_ref, dwo_out, dwa_out, dwb_out, dwo_ref, dwa_ref, dwb_ref):
        _zero_on_first(dwo_ref, dwa_ref, dwb_ref)
        dx1v = dx1_ref[...].astype(BF16)
        dmg = _dot_nt(dx1v, wo_ref[...])
        ga = _sigmoid(za_ref[...])
        gb = _sigmoid(zb_ref[...])
        dza_ref[...] = dmg * ma_ref[...] * ga * (1.0 - ga)
        dzb_ref[...] = dmg * mb_ref[...] * gb * (1.0 - gb)
        dma = (dmg * ga).astype(BF16)
        dmb = (dmg * gb).astype(BF16)
        dya_ref[...] = _dot(dma, wa_ref[...])
        dyb_ref[...] = _dot_nt(dmb, wb_ref[...])
        dwo_ref[...] += _dot_tn(mg_ref[...], dx1v)
        dwa_ref[...] += _dot_tn(dma, ya_ref[...])
        dwb_ref[...] += _dot_tn(h_ref[...], dmb)
        _store_on_last([(dwo_ref, dwo_out), (dwa_ref, dwa_out), (dwb_ref, dwb_out)])

    return _run(
        body, comm, name="merge_bwd", grid=(t // tm,),
        out_shape=[jax.ShapeDtypeStruct((t, d), F32), jax.ShapeDtypeStruct((t, d), F32),
                   jax.ShapeDtypeStruct((t, sa), F32), jax.ShapeDtypeStruct((t, lw), F32),
                   jax.ShapeDtypeStruct((d, d), BF16), jax.ShapeDtypeStruct((d, sa), BF16),
                   jax.ShapeDtypeStruct((lw, d), BF16)],
        in_specs=[_rows(tm, d), _rows(tm, d), _rows(tm, d), _rows(tm, d), _rows(tm, d), _rows(tm, d),
                  _rows(tm, sa), _rows(tm, lw), _whole((d, d)), _whole((d, sa)), _whole((lw, d))],
        out_specs=[_rows(tm, d), _rows(tm, d), _rows(tm, sa), _rows(tm, lw), _acc((d, d)), _acc((d, sa)), _acc((lw, d))],
        scratch_shapes=[pltpu.VMEM((d, d), F32), pltpu.VMEM((d, sa), F32), pltpu.VMEM((lw, d), F32)],
        semantics="arbitrary",
    )(dx1, merged, ma, mb, za, zb, y_a, h, w_o, w_a_out_t, w_b_out)


def _fold_diag_blocks(dense, row_group, col_group, row0=0, col0=0):
    r, c = dense.shape
    rows = lax.broadcasted_iota(jnp.int32, (r, c), 0) + row0
    cols = lax.broadcasted_iota(jnp.int32, (r, c), 1) + col0
    kept = jnp.where(rows // row_group == cols // col_group, dense, 0.0)
    pick = (lax.broadcasted_iota(jnp.int32, (row_group, r), 0)
            == lax.broadcasted_iota(jnp.int32, (row_group, r), 1) % row_group).astype(F32)
    return jnp.dot(pick, kept, preferred_element_type=F32, precision=lax.Precision.HIGHEST)


def _lru_bwd(dh, xc, hprev, u, conv_w, wr_blk, b_r, wi_blk, b_i, lru_lambda, head_dim, comm=None):
    t, w = dh.shape
    tc = min(TIME_CHUNK, t)
    steps = t // tc
    halo = SUBLANES
    sub_per_chunk = tc // halo
    slabs = w // LRU_SLAB

    def body(dh_ref, xc_ref, hp_ref, u_ref, uh_ref, cw_ref, wr_ref, br_ref, wi_ref, bi_ref, lam_ref,
             du_ref, dwr_out, dwi_out, vec_ref, lam_s, a_s, dxc_s, uext_s, carry_s, dwr_ref, dwi_ref):
        chunk = steps - 1 - pl.program_id(0)

        @pl.when(pl.program_id(0) == 0)
        def _():
            carry_s[...] = jnp.zeros_like(carry_s)
            dxc_s[tc:tc + halo, :] = jnp.zeros((halo, w), F32)
            dwr_ref[...] = jnp.zeros_like(dwr_ref)
            dwi_ref[...] = jnp.zeros_like(dwi_ref)
            vec_ref[...] = jnp.zeros_like(vec_ref)

        xc = xc_ref[...]
        r, ig, sp, log_a = _lru_gates(xc, wr_ref, br_ref, wi_ref, bi_ref, lam_ref)
        a = jnp.exp(log_a)
        a_s[...] = a

        def step(i, q):
            at = pl.ds(tc - 1 - i, 1)
            lam_row = dh_ref[at, :] + q
            lam_s[at, :] = lam_row
            return a_s[at, :] * lam_row

        carry_s[0:1, :] = lax.fori_loop(0, tc, step, carry_s[0:1, :], unroll=8)
        lam = lam_s[...]
        mult = jnp.sqrt(-_expm1(2.0 * log_a))
        d_log_a = lam * hp_ref[...] * a - (lam * ig * xc) * (a * a) / mult
        d_ig = lam * mult * xc
        dpre_r = (d_log_a * (-LRU_C * sp)) * r * (1.0 - r)
        dpre_i = d_ig * ig * (1.0 - ig)
        dxc = lam * mult * ig + _slab_dot(dpre_r, wr_ref, transposed=True) + _slab_dot(dpre_i, wi_ref, transposed=True)
        xcb, drb, dib = xc.astype(BF16), dpre_r.astype(BF16), dpre_i.astype(BF16)
        for j in range(slabs):
            cols = slice(j * LRU_SLAB, (j + 1) * LRU_SLAB)
            dwr_ref[j] += _dot_tn(drb[:, cols], xcb[:, cols])
            dwi_ref[j] += _dot_tn(dib[:, cols], xcb[:, cols])
        vec_ref[0:1, :] += _rowsum(dxc)
        vec_ref[1:2, :] += _rowsum(dpre_r)
        vec_ref[2:3, :] += _rowsum(dpre_i)
        vec_ref[3:4, :] += _rowsum(d_log_a * (-LRU_C * r)) * (-_sigmoid(-lam_ref[...]))
        dxc_s[0:tc, :] = dxc
        du = cw_ref[CONV_WIDTH - 1:CONV_WIDTH, :] * dxc
        for k in range(CONV_WIDTH - 1):
            off = CONV_WIDTH - 1 - k
            du = du + cw_ref[k:k + 1, :] * dxc_s[off:off + tc, :]
        du_ref[...] = du
        dxc_s[tc:tc + halo, :] = dxc_s[0:halo, :]
        uext_s[0:halo, :] = jnp.where(chunk > 0, uh_ref[...], 0.0)
        uext_s[halo:halo + tc, :] = u_ref[...]
        for k in range(CONV_WIDTH):
            off = halo - (CONV_WIDTH - 1) + k
            vec_ref[4 + k:5 + k, :] += _rowsum(dxc * uext_s[off:off + tc, :])

        @pl.when(pl.program_id(0) == steps - 1)
        def _():
            for j in range(slabs):
                cols = slice(j * LRU_SLAB, (j + 1) * LRU_SLAB)
                dwr_out[:, cols] = _fold_diag_blocks(dwr_ref[j], head_dim, head_dim)
                dwi_out[:, cols] = _fold_diag_blocks(dwi_ref[j], head_dim, head_dim)

    halo_spec = pl.BlockSpec((halo, w), lambda i: (jnp.maximum((steps - 1 - i) * sub_per_chunk - 1, 0), 0))
    return _run(
        body, comm, name="lru_bwd", grid=(steps,),
        out_shape=[jax.ShapeDtypeStruct((t, w), F32), jax.ShapeDtypeStruct((head_dim, w), F32),
                   jax.ShapeDtypeStruct((head_dim, w), F32), jax.ShapeDtypeStruct((SUBLANES, w), F32)],
        in_specs=[_rows_rev(tc, w, steps)] * 4 + [halo_spec, _whole((CONV_WIDTH, w)), _whole(wr_blk.shape),
                                                  _whole((1, w)), _whole(wi_blk.shape), _whole((1, w)), _whole((1, w))],
        out_specs=[_rows_rev(tc, w, steps), _acc((head_dim, w)), _acc((head_dim, w)), _acc((SUBLANES, w))],
        scratch_shapes=[pltpu.VMEM((tc, w), F32), pltpu.VMEM((tc, w), F32), pltpu.VMEM((tc + halo, w), F32),
                        pltpu.VMEM((halo + tc, w), F32), pltpu.VMEM((SUBLANES, w), F32),
                        pltpu.VMEM((slabs, LRU_SLAB, LRU_SLAB), F32), pltpu.VMEM((slabs, LRU_SLAB, LRU_SLAB), F32)],
        semantics="arbitrary",
    )(dh, xc, hprev, u, u, conv_w, wr_blk, b_r, wi_blk, b_i, lru_lambda)


def _s5_bwd(dya, y, sr, si, u, w_glu, b_glu, cre_blk, cimn_blk, bbr_blk, bbi_blk, ar, ai, d_skip, comm=None):
    t, sa = dya.shape
    gn = sr.shape[1]
    ns, _, sw = bbr_blk.shape
    tc = min(TIME_CHUNK, t)
    steps = t // tc
    halo = SUBLANES

    def body(dya_ref, y_ref, sr_ref, si_ref, u_ref, wg_ref, bg_ref, cre_ref, cim_ref, bbr_ref, bbi_ref,
             ar_ref, ai_ref, d_ref, du_ref, lr_ref, li_ref, dy_ref, dwg_out, vsa_ref, vgn_ref, gr_s, gi_s, cr_s, ci_s,
             dwg_ref):
        @pl.when(pl.program_id(0) == 0)
        def _():
            cr_s[...] = jnp.zeros_like(cr_s)
            ci_s[...] = jnp.zeros_like(ci_s)
            gr_s[tc:tc + halo, :] = jnp.zeros((halo, gn), F32)
            gi_s[tc:tc + halo, :] = jnp.zeros((halo, gn), F32)
            dwg_ref[...] = jnp.zeros_like(dwg_ref)
            vsa_ref[...] = jnp.zeros_like(vsa_ref)
            vgn_ref[...] = jnp.zeros_like(vgn_ref)

        yv = y_ref[...]
        uv = u_ref[...]
        zz = _gelu(yv)
        sg = _sigmoid(_dot(zz, wg_ref[...]) + bg_ref[...])
        dyav = dya_ref[...]
        dq = dyav * zz * sg * (1.0 - sg)
        dzz = dyav * sg + _dot_nt(dq, wg_ref[...])
        dwg_ref[...] += _dot_tn(zz, dq)
        dy = dzz * _gelu_grad(yv)
        dyb = dy.astype(BF16)
        dy_ref[...] = dyb.astype(dy_ref.dtype)
        vsa_ref[0:1, :] += _rowsum(dq)
        vsa_ref[1:2, :] += _rowsum(dy * uv)
        for m in range(ns):
            dym = dyb[:, m * S5_SLAB:(m + 1) * S5_SLAB]
            gr_s[0:tc, m * sw:(m + 1) * sw] = _dot_nt(dym, cre_ref[m])
            gi_s[0:tc, m * sw:(m + 1) * sw] = _dot_nt(dym, cim_ref[m])
        a_r = ar_ref[...]
        a_i = ai_ref[...]

        def step(i, carry):
            l_r, l_i = carry
            at = pl.ds(tc - 1 - i, 1)
            n_r = gr_s[at, :] + a_r * l_r + a_i * l_i
            n_i = gi_s[at, :] + a_r * l_i - a_i * l_r
            gr_s[at, :] = n_r
            gi_s[at, :] = n_i
            return n_r, n_i

        l_r, l_i = lax.fori_loop(0, tc, step, (cr_s[0:1, :], ci_s[0:1, :]), unroll=8)
        cr_s[0:1, :] = l_r
        ci_s[0:1, :] = l_i
        nxt_r = gr_s[1:tc + 1, :]
        nxt_i = gi_s[1:tc + 1, :]
        srv = sr_ref[...]
        siv = si_ref[...]
        vgn_ref[0:1, :] += _rowsum(nxt_r * srv + nxt_i * siv)
        vgn_ref[1:2, :] += _rowsum(nxt_i * srv - nxt_r * siv)
        lam_r = gr_s[0:tc, :]
        lam_i = gi_s[0:tc, :]
        gr_s[tc:tc + halo, :] = gr_s[0:halo, :]
        gi_s[tc:tc + halo, :] = gi_s[0:halo, :]
        lrb = lam_r.astype(BF16)
        lib = lam_i.astype(BF16)
        lr_ref[...] = lrb.astype(lr_ref.dtype)
        li_ref[...] = lib.astype(li_ref.dtype)
        for m in range(ns):
            states, chans = slice(m * sw, (m + 1) * sw), slice(m * S5_SLAB, (m + 1) * S5_SLAB)
            du_ref[:, chans] = (_dot_nt(lrb[:, states], bbr_ref[m]) + _dot_nt(lib[:, states], bbi_ref[m])
                                + dy[:, chans] * d_ref[:, chans])
        _store_on_last([(dwg_ref, dwg_out)])

    return _run(
        body, comm, name="s5_bwd", grid=(steps,),
        out_shape=[jax.ShapeDtypeStruct((t, sa), F32), jax.ShapeDtypeStruct((t, gn), BF16),
                   jax.ShapeDtypeStruct((t, gn), BF16), jax.ShapeDtypeStruct((t, sa), BF16),
                   jax.ShapeDtypeStruct((sa, sa), BF16), jax.ShapeDtypeStruct((SUBLANES, sa), F32),
                   jax.ShapeDtypeStruct((SUBLANES, gn), F32)],
        in_specs=[_rows_rev(tc, sa, steps), _rows_rev(tc, sa, steps), _rows_rev(tc, gn, steps), _rows_rev(tc, gn, steps),
                  _rows_rev(tc, sa, steps), _whole((sa, sa)), _whole((1, sa)), _whole(cre_blk.shape),
                  _whole(cimn_blk.shape), _whole(bbr_blk.shape), _whole(bbi_blk.shape), _whole((1, gn)), _whole((1, gn)),
                  _whole((1, sa))],
        out_specs=[_rows_rev(tc, sa, steps), _rows_rev(tc, gn, steps), _rows_rev(tc, gn, steps), _rows_rev(tc, sa, steps),
                   _acc((sa, sa)), _acc((SUBLANES, sa)), _acc((SUBLANES, gn))],
        scratch_shapes=[pltpu.VMEM((tc + halo, gn), F32), pltpu.VMEM((tc + halo, gn), F32),
                        pltpu.VMEM((SUBLANES, gn), F32), pltpu.VMEM((SUBLANES, gn), F32), pltpu.VMEM((sa, sa), F32)],
        semantics="arbitrary",
    )(dya, y, sr, si, u, w_glu, b_glu, cre_blk, cimn_blk, bbr_blk, bbi_blk, ar, ai, d_skip)


def _inproj_bwd(dparts, x, dx1, g_mix, w_in_t, comm=None):
    t, d = x.shape
    n = w_in_t.shape[0]
    widths = [p.shape[1] for p in dparts]
    offs = [sum(widths[:i]) for i in range(len(widths) + 1)]
    tm = min(TOKEN_TILE, t)
    np_ = len(dparts)

    def body(*refs):
        dz_refs = refs[:np_]
        x_ref, dx1_ref, g_ref, w_ref, gx_ref, h_ref, dz_ref, vd_ref, vn_ref = refs[np_:]
        _zero_on_first(vd_ref, vn_ref)
        dh = jnp.zeros((tm, d), F32)
        for k, r in enumerate(dz_refs):
            lo, hi = offs[k], offs[k + 1]
            dzk = r[...]
            dh = dh + _dot(dzk, w_ref[lo:hi, :])
            dz_ref[:, lo:hi] = dzk.astype(dz_ref.dtype)
            vn_ref[0:1, lo:hi] += _rowsum(dzk)
        xhat, r0 = _rms_stats(x_ref[...])
        h_ref[...] = (xhat * g_ref[...]).astype(h_ref.dtype)
        dxn, dg = _rms_bwd(dh, xhat, r0, g_ref[...])
        gx_ref[...] = dx1_ref[...] + dxn
        vd_ref[0:1, :] += _rowsum(dg)

    return _run(
        body, comm, name="inproj_bwd", grid=(t // tm,),
        out_shape=[jax.ShapeDtypeStruct((t, d), F32), jax.ShapeDtypeStruct((t, d), BF16),
                   jax.ShapeDtypeStruct((t, n), BF16), jax.ShapeDtypeStruct((SUBLANES, d), F32),
                   jax.ShapeDtypeStruct((SUBLANES, n), F32)],
        in_specs=[_rows(tm, w) for w in widths] + [_rows(tm, d), _rows(tm, d), _whole((1, d)), _whole((n, d))],
        out_specs=[_rows(tm, d), _rows(tm, d), _rows(tm, n), _acc((SUBLANES, d)), _acc((SUBLANES, n))],
        semantics="arbitrary",
    )(*dparts, x, dx1, g_mix, w_in_t)


def _prep(lr_row, li_row, ldt_row, lr_col, li_col, ldt_col, b_re, b_im, c_re, c_im, w_r, w_i, comm=None):
    gn, pch = b_re.shape
    sa, n = c_re.shape
    w, hd = w_r.shape
    ns, sw, lsl = sa // S5_SLAB, S5_SLAB * n // pch, w // LRU_SLAB

    def spread_cols(vals, row_group, col_group, width):
        r, k = vals.shape
        tile = (lax.broadcasted_iota(jnp.int32, (k, width), 0) == lax.broadcasted_iota(jnp.int32, (k, width), 1) % k)
        rows = lax.broadcasted_iota(jnp.int32, (r, width), 0) // row_group
        cols = lax.broadcasted_iota(jnp.int32, (r, width), 1) // col_group
        return jnp.where(rows == cols, _dot(vals, tile.astype(BF16)), 0.0)

    def spread_rows(vals, row_group, col_group, height):
        k, c = vals.shape
        tile = (lax.broadcasted_iota(jnp.int32, (height, k), 0) % k == lax.broadcasted_iota(jnp.int32, (height, k), 1))
        rows = lax.broadcasted_iota(jnp.int32, (height, c), 0) // row_group
        cols = lax.broadcasted_iota(jnp.int32, (height, c), 1) // col_group
        return jnp.where(rows == cols, _dot(tile.astype(BF16), vals), 0.0)

    def body(lrr, lir, ldr, lrc, lic, ldc, bre, bim, cre, cim, wr, wi,
             ar_o, ai_o, bbr_o, bbi_o, cre_o, cim_o, wr_o, wi_o):
        ar, ai, _, _ = _disc_scalars(lrr[...], lir[...], ldr[...])
        ar_o[...] = ar
        ai_o[...] = ai
        _, _, bbr, bbi = _disc_cols(lrc[...], lic[...], ldc[...], bre[...], bim[...])
        bbr_t, bbi_t = bbr.T, bbi.T
        for m in range(ns):
            bbr_o[m] = spread_rows(bbr_t[:, m * sw:(m + 1) * sw], pch, n, S5_SLAB).astype(bbr_o.dtype)
            bbi_o[m] = spread_rows(bbi_t[:, m * sw:(m + 1) * sw], pch, n, S5_SLAB).astype(bbi_o.dtype)
            rows = slice(m * S5_SLAB, (m + 1) * S5_SLAB)
            cre_o[m] = spread_rows(cre[rows, :].T, n, pch, sw).astype(cre_o.dtype)
            cim_o[m] = spread_rows(-cim[rows, :].T, n, pch, sw).astype(cim_o.dtype)
        for j in range(lsl):
            rows = slice(j * LRU_SLAB, (j + 1) * LRU_SLAB)
            wr_o[j] = spread_cols(wr[rows, :], hd, hd, LRU_SLAB).astype(wr_o.dtype)
            wi_o[j] = spread_cols(wi[rows, :], hd, hd, LRU_SLAB).astype(wi_o.dtype)

    args = (lr_row, li_row, ldt_row, lr_col, li_col, ldt_col, b_re, b_im, c_re, c_im, w_r, w_i)
    out_shape = [jax.ShapeDtypeStruct((1, gn), F32), jax.ShapeDtypeStruct((1, gn), F32),
                 jax.ShapeDtypeStruct((ns, S5_SLAB, sw), BF16), jax.ShapeDtypeStruct((ns, S5_SLAB, sw), BF16),
                 jax.ShapeDtypeStruct((ns, sw, S5_SLAB), BF16), jax.ShapeDtypeStruct((ns, sw, S5_SLAB), BF16),
                 jax.ShapeDtypeStruct((lsl, LRU_SLAB, LRU_SLAB), BF16),
                 jax.ShapeDtypeStruct((lsl, LRU_SLAB, LRU_SLAB), BF16)]
    return _run(
        body, comm, name="prep", grid=(1,), out_shape=out_shape, in_specs=[_whole(a.shape) for a in args],
        out_specs=[_acc(s.shape) for s in out_shape], semantics="arbitrary",
    )(*args)


def _s5_param_grads(lam_r, lam_i, sr, si, u, dy, pch, n, comm=None):
    t, gn = lam_r.shape
    sa = u.shape[1]
    sw = S5_SLAB * n // pch

    def body(lr_ref, li_ref, sr_ref, si_ref, u_ref, dy_ref, dbr_ref, dbi_ref, dcr_ref, dci_ref):
        uv = u_ref[...]
        dyv = dy_ref[...]
        dbr_ref[...] = _fold_diag_blocks(_dot_tn(uv, lr_ref[...]), pch, n)
        dbi_ref[...] = _fold_diag_blocks(_dot_tn(uv, li_ref[...]), pch, n)
        dcr_ref[...] = _fold_diag_blocks(_dot_tn(sr_ref[...], dyv), n, pch)
        dci_ref[...] = _fold_diag_blocks(_dot_tn(si_ref[...], dyv), n, pch)

    states = pl.BlockSpec((t, sw), lambda m: (0, m))
    chans = pl.BlockSpec((t, S5_SLAB), lambda m: (0, m))
    return _run(
        body, comm, name="s5_param_grads", grid=(sa // S5_SLAB,),
        out_shape=[jax.ShapeDtypeStruct((pch, gn), F32), jax.ShapeDtypeStruct((pch, gn), F32),
                   jax.ShapeDtypeStruct((n, sa), F32), jax.ShapeDtypeStruct((n, sa), F32)],
        in_specs=[states, states, states, states, chans, chans],
        out_specs=[pl.BlockSpec((pch, sw), lambda m: (0, m)), pl.BlockSpec((pch, sw), lambda m: (0, m)),
                   pl.BlockSpec((n, S5_SLAB), lambda m: (0, m)), pl.BlockSpec((n, S5_SLAB), lambda m: (0, m))],
        semantics="parallel",
    )(lam_r, lam_i, sr, si, u, dy)


SMALL_PARTS = ["vec_tail", "vec_ffn", "vec_lru", "vec_mix", "vec_bin", "vec_sa", "vec_gn", "dw_r", "dw_i", "dbb_re",
               "dbb_im", "dc_re", "dc_imn", "loss"]


def _small_reduce(parts, shapes, lr_col, li_col, ldt_col, b_re, b_im, groups, comm=None):
    gn, pch = b_re.shape
    n = gn // groups
    nparts = parts[SMALL_PARTS[0]].size // math.prod(shapes[SMALL_PARTS[0]])
    np_, nout = len(SMALL_PARTS), 24

    def body(*refs):
        ins = refs[:np_]
        lr, li, ldt, bre, bim = refs[np_:np_ + 5]
        outs = refs[np_ + 5:np_ + 5 + nout]
        sums = dict(zip(SMALL_PARTS, refs[np_ + 5 + nout:]))

        @pl.when(pl.program_id(0) == 0)
        def _():
            for k, r in zip(SMALL_PARTS, ins):
                sums[k][...] = r[...]

        @pl.when(pl.program_id(0) > 0)
        def _():
            for k, r in zip(SMALL_PARTS, ins):
                sums[k][...] += r[...]

        @pl.when(pl.program_id(0) == nparts - 1)
        def _():
            finish({k: s[...] for k, s in sums.items()}, lr, li, ldt, bre, bim, *outs)

    def finish(tot, lr, li, ldt, bre, bim, o_loss, o_gmix, o_bin, o_bglu, o_s5d, o_convb, o_br, o_bi, o_lam, o_gffn,
               o_gpg, o_bpg, o_gple, o_gfin, o_wr, o_wi, o_cre, o_cim, o_bre, o_bim, o_lre, o_lim, o_ldt, o_convw):
        o_loss[...] = tot["loss"]
        o_convw[...] = tot["vec_lru"][SUBLANES - CONV_WIDTH:SUBLANES]
        o_bpg[...] = tot["vec_tail"][0:1]
        o_gpg[...] = tot["vec_tail"][1:2]
        o_gple[...] = tot["vec_tail"][2:3]
        o_gfin[...] = tot["vec_tail"][3:4]
        o_gffn[...] = tot["vec_ffn"][0:1]
        o_convb[...] = tot["vec_lru"][0:1]
        o_br[...] = tot["vec_lru"][1:2]
        o_bi[...] = tot["vec_lru"][2:3]
        o_lam[...] = tot["vec_lru"][3:4]
        o_gmix[...] = tot["vec_mix"][0:1]
        o_bin[...] = tot["vec_bin"][0:1]
        o_bglu[...] = tot["vec_sa"][0:1]
        o_s5d[...] = tot["vec_sa"][1:2]
        o_wr[...] = tot["dw_r"].T
        o_wi[...] = tot["dw_i"].T
        o_cre[...] = tot["dc_re"].T
        o_cim[...] = -tot["dc_imn"].T
        d_a = tot["vec_gn"].T
        _, chain = jax.vjp(_disc_cols, lr[...], li[...], ldt[...], bre[...], bim[...])
        d_lr, d_li, d_ldt, d_bre, d_bim = chain((d_a[:, 0:1], d_a[:, 1:2], tot["dbb_re"].T, tot["dbb_im"].T))
        o_lre[...] = d_lr
        o_lim[...] = d_li
        o_bre[...] = d_bre
        o_bim[...] = d_bim
        same = (lax.broadcasted_iota(jnp.int32, (groups, gn), 0)
                == lax.broadcasted_iota(jnp.int32, (groups, gn), 1) // n).astype(F32)
        o_ldt[...] = jnp.dot(same, d_ldt * jnp.ones((1, LANES), F32), preferred_element_type=F32,
                             precision=lax.Precision.HIGHEST)[:, 0:1]

    d = shapes["vec_mix"][1]
    nz = shapes["vec_bin"][1]
    sa = shapes["vec_sa"][1]
    w = shapes["vec_lru"][1]
    row = lambda c: jax.ShapeDtypeStruct((1, c), F32)
    out_shape = [jax.ShapeDtypeStruct(shapes["loss"], F32), row(d), row(nz), row(sa), row(sa), row(w), row(w), row(w),
                 row(w), row(d), row(d), row(d), row(d), row(d),
                 jax.ShapeDtypeStruct(shapes["dw_r"][::-1], F32), jax.ShapeDtypeStruct(shapes["dw_i"][::-1], F32),
                 jax.ShapeDtypeStruct(shapes["dc_re"][::-1], F32), jax.ShapeDtypeStruct(shapes["dc_imn"][::-1], F32),
                 jax.ShapeDtypeStruct((gn, pch), F32), jax.ShapeDtypeStruct((gn, pch), F32),
                 jax.ShapeDtypeStruct((gn, 1), F32), jax.ShapeDtypeStruct((gn, 1), F32),
                 jax.ShapeDtypeStruct((groups, 1), F32), jax.ShapeDtypeStruct((CONV_WIDTH, w), F32)]
    def part_spec(k):
        r, c = shapes[k]
        if parts[k].ndim == 3:
            return pl.BlockSpec((None, r, c), lambda i: (i, 0, 0))
        return pl.BlockSpec((r, c), lambda i: (i, 0))

    outs = _run(
        body, comm, name="small_reduce", grid=(nparts,), out_shape=out_shape,
        in_specs=[part_spec(k) for k in SMALL_PARTS] + [_whole(a.shape) for a in (lr_col, li_col, ldt_col, b_re, b_im)],
        out_specs=[_acc(s.shape) for s in out_shape],
        scratch_shapes=[pltpu.VMEM(shapes[k], F32) for k in SMALL_PARTS],
        semantics="arbitrary",
    )(*[parts[k] for k in SMALL_PARTS], lr_col, li_col, ldt_col, b_re, b_im)
    extra = None
    if comm is not None:
        outs, extra = outs
    names = ["loss", "g_mix", "b_in", "b_glu", "s5_d", "conv_b", "b_r", "b_i", "lru_lambda", "g_ffn", "g_ple_gate",
             "b_ple_gate", "g_ple", "g_final", "w_r", "w_i", "s5_c_re", "s5_c_im", "s5_b_re", "s5_b_im", "lam_re",
             "lam_im", "log_dt", "conv_w"]
    res = dict(zip(names, outs))
    return res if comm is None else (res, extra)


def _adamw_small(ws, gs, ms, vs):
    n = len(ws)

    def body(*refs):
        w_r, g_r, m_r, v_r = (refs[i * n:(i + 1) * n] for i in range(4))
        g_o, d_o, m_o, v_o = (refs[(4 + i) * n:(5 + i) * n] for i in range(4))
        for i in range(n):
            g = g_r[i][...]
            delta, m_new, v_new = _adamw_math(w_r[i][...], g, m_r[i][...], v_r[i][...])
            g_o[i][...] = g
            d_o[i][...] = delta
            m_o[i][...] = m_new
            v_o[i][...] = v_new

    shapes = [jax.ShapeDtypeStruct(a.shape, F32) for a in ws]
    outs = pl.pallas_call(body, name="adamw_small", out_shape=shapes * 4)(*ws, *gs, *ms, *vs)
    return outs[:n], outs[n:2 * n], outs[2 * n:3 * n], outs[3 * n:]


def _adamw_math(w, g, m, v):
    m_new = ADAM_B1 * m + (1.0 - ADAM_B1) * g
    v_new = ADAM_B2 * v + (1.0 - ADAM_B2) * (g * g)
    m_hat = m_new / (1.0 - ADAM_B1 ** ADAM_STEP)
    v_hat = v_new / (1.0 - ADAM_B2 ** ADAM_STEP)
    delta = -ADAM_LR * (m_hat / (jnp.sqrt(v_hat) + ADAM_EPS) + ADAM_WD * w)
    return delta, m_new, v_new


def _row_tile(rows):
    for cand in (256, 128, 64, 32, 16, 8):
        if rows % cand == 0:
            return cand
    return rows


def _adamw(parts, w, m, v, name, transposed=False):
    rows, cols = w.shape
    npart = parts.shape[0]
    tr = _row_tile(rows)
    if transposed:
        parts_spec = pl.BlockSpec((npart, cols, tr), lambda i: (0, 0, i))
    else:
        parts_spec = pl.BlockSpec((npart, tr, cols), lambda i: (0, i, 0))

    def body(p_ref, w_ref, m_ref, v_ref, g_ref, d_ref, mo_ref, vo_ref):
        g = p_ref[0].astype(F32)
        for k in range(1, npart):
            g = g + p_ref[k].astype(F32)
        if transposed:
            g = g.T
        delta, m_new, v_new = _adamw_math(w_ref[...], g, m_ref[...], v_ref[...])
        g_ref[...] = g
        d_ref[...] = delta
        mo_ref[...] = m_new
        vo_ref[...] = v_new

    return pl.pallas_call(
        body, name=name, grid=(rows // tr,),
        out_shape=[jax.ShapeDtypeStruct((rows, cols), F32)] * 4,
        in_specs=[parts_spec] + [_rows(tr, cols)] * 3,
        out_specs=[_rows(tr, cols)] * 4,
        compiler_params=_params("parallel"),
    )(parts, w, m, v)


def _mesh_position():
    return lax.axis_index("x"), lax.axis_index("y"), lax.axis_index("c")


def _flip(pos, rel):
    x, y, c = pos
    return (1 - x if rel & 4 else x, 1 - y if rel & 2 else y, 1 - c if rel & 1 else c)


def _index(pos):
    return 4 * pos[0] + 2 * pos[1] + pos[2]


_ANY = pl.BlockSpec(memory_space=pl.ANY)
FLAT_ROWS = 32


def _dma_sems(n):
    return [pltpu.SemaphoreType.DMA((n, N_DEV - 1)), pltpu.SemaphoreType.DMA((n, N_DEV - 1)), pltpu.SemaphoreType.DMA((n,))]


def _block_of(ref, idx, rows, flat):
    if flat:
        return ref.at[pl.ds(pl.multiple_of(idx * rows, FLAT_ROWS), rows), :]
    return ref.at[idx]


class _Gather:
    chips = (4, 2, 6)

    def __init__(self, shards):
        self.inputs = list(shards)
        self.flat = [s.shape[0] % FLAT_ROWS == 0 for s in shards]
        self.out_shape = [
            jax.ShapeDtypeStruct((N_DEV * s.shape[0], s.shape[1]) if f else (N_DEV,) + s.shape, s.dtype)
            for s, f in zip(shards, self.flat)]
        self.sems = _dma_sems(len(shards))

    def _copy(self, ins, outs, sems, i, k, block, to, own=False):
        dst = _block_of(outs[i], _index(block), self.inputs[i].shape[0], self.flat[i])
        return pltpu.make_async_remote_copy(
            src_ref=ins[i] if own else dst, dst_ref=dst, send_sem=sems[0].at[i, k], recv_sem=sems[1].at[i, k],
            device_id=to, device_id_type=MESH)

    def _local(self, ins, outs, sems, i, me):
        dst = _block_of(outs[i], _index(me), self.inputs[i].shape[0], self.flat[i])
        return pltpu.make_async_copy(ins[i], dst, sems[2].at[i])

    def _first(self, ins, outs, sems, i, me):
        cps = [self._copy(ins, outs, sems, i, 0, me, _flip(me, 1), own=True)]
        cps += [self._copy(ins, outs, sems, i, 1 + j, me, _flip(me, rel), own=True) for j, rel in enumerate(self.chips)]
        return cps

    def _passed(self, ins, outs, sems, i, j, me):
        return self._copy(ins, outs, sems, i, 4 + j, _flip(me, self.chips[j]), _flip(me, 1))

    def before(self, ins, outs, sems):
        n = len(self.inputs)
        me = _mesh_position()

        @pl.when(pl.program_id(0) == 0)
        def _():
            for i in range(n):
                self._local(ins, outs, sems, i, me).start()
                for cp in self._first(ins, outs, sems, i, me):
                    cp.start()

        @pl.when(pl.program_id(0) == pl.num_programs(0) - 1)
        def _():
            for j, rel in enumerate(self.chips):
                for i in range(n):
                    self._copy(ins, outs, sems, i, 1 + j, _flip(me, rel), me).wait_recv()
                    self._passed(ins, outs, sems, i, j, me).start()

    def after(self, ins, outs, sems):
        n = len(self.inputs)
        me = _mesh_position()
        sibling = _flip(me, 1)

        @pl.when(pl.program_id(0) == pl.num_programs(0) - 1)
        def _():
            for i in range(n):
                self._copy(ins, outs, sems, i, 0, sibling, me).wait_recv()
                for j, rel in enumerate(self.chips):
                    self._copy(ins, outs, sems, i, 4 + j, _flip(sibling, rel), me).wait_recv()
            for i in range(n):
                for cp in self._first(ins, outs, sems, i, me):
                    cp.wait_send()
                for j in range(len(self.chips)):
                    self._passed(ins, outs, sems, i, j, me).wait_send()
                self._local(ins, outs, sems, i, me).wait()


N_CHIPS = N_DEV // 2


def _chip(pos):
    return 2 * pos[0] + pos[1]


class _PairSwap:
    def __init__(self, arrays):
        self.inputs = list(arrays)
        self.rows = [a.shape[0] // N_DEV for a in arrays]
        for r in self.rows:
            assert r % FLAT_ROWS == 0, r
        self.out_shape = [jax.ShapeDtypeStruct((N_CHIPS, r, a.shape[1]), a.dtype) for a, r in zip(arrays, self.rows)]
        n = len(arrays)
        self.sems = [pltpu.SemaphoreType.DMA((n, N_CHIPS)), pltpu.SemaphoreType.DMA((n, N_CHIPS))]

    def _copy(self, ins, outs, sems, i, j, me):
        sibling = _flip(me, 1)
        return pltpu.make_async_remote_copy(
            src_ref=_block_of(ins[i], 2 * j + sibling[2], self.rows[i], True), dst_ref=outs[i].at[j],
            send_sem=sems[0].at[i, j], recv_sem=sems[1].at[i, j], device_id=sibling, device_id_type=MESH)

    def before(self, ins, outs, sems):
        me = _mesh_position()

        @pl.when(pl.program_id(0) == 0)
        def _():
            for i in range(len(self.inputs)):
                for j in range(N_CHIPS):
                    self._copy(ins, outs, sems, i, j, me).start()

    def after(self, ins, outs, sems):
        me = _mesh_position()

        @pl.when(pl.program_id(0) == pl.num_programs(0) - 1)
        def _():
            for i in range(len(self.inputs)):
                for j in range(N_CHIPS):
                    self._copy(ins, outs, sems, i, j, me).wait()


class _ChipExchange:
    chips = (4, 2, 6)

    def __init__(self, arrays):
        self.inputs = list(arrays)
        self.out_shape = [jax.ShapeDtypeStruct(a.shape, a.dtype) for a in arrays]
        n = len(arrays)
        self.sems = [pltpu.SemaphoreType.DMA((n, 3)), pltpu.SemaphoreType.DMA((n, 3)), pltpu.SemaphoreType.DMA((n,))]

    def _send(self, ins, outs, sems, i, k, me):
        peer = _flip(me, self.chips[k])
        return pltpu.make_async_remote_copy(
            src_ref=ins[i].at[_chip(peer)], dst_ref=outs[i].at[_chip(me)], send_sem=sems[0].at[i, k],
            recv_sem=sems[1].at[i, k], device_id=peer, device_id_type=MESH)

    def _arrival(self, ins, outs, sems, i, k, me):
        peer = _flip(me, self.chips[k])
        return pltpu.make_async_remote_copy(
            src_ref=ins[i].at[_chip(me)], dst_ref=outs[i].at[_chip(peer)], send_sem=sems[0].at[i, k],
            recv_sem=sems[1].at[i, k], device_id=peer, device_id_type=MESH)

    def _local(self, ins, outs, sems, i, me):
        return pltpu.make_async_copy(ins[i].at[_chip(me)], outs[i].at[_chip(me)], sems[2].at[i])

    def before(self, ins, outs, sems):
        me = _mesh_position()

        @pl.when(pl.program_id(0) == 0)
        def _():
            for i in range(len(self.inputs)):
                self._local(ins, outs, sems, i, me).start()
                for k in range(len(self.chips)):
                    self._send(ins, outs, sems, i, k, me).start()

    def after(self, ins, outs, sems):
        me = _mesh_position()

        @pl.when(pl.program_id(0) == pl.num_programs(0) - 1)
        def _():
            for i in range(len(self.inputs)):
                for k in range(len(self.chips)):
                    self._arrival(ins, outs, sems, i, k, me).wait_recv()
            for i in range(len(self.inputs)):
                for k in range(len(self.chips)):
                    self._send(ins, outs, sems, i, k, me).wait_send()
                self._local(ins, outs, sems, i, me).wait()


def _pair_add(grads, halves, name):
    n = len(grads)

    def body(*refs):
        g_refs, h_refs, o_refs = refs[:n], refs[n:2 * n], refs[2 * n:]
        c = lax.axis_index("c")
        for i in range(n):
            r = h_refs[i].shape[1]
            for j in range(N_CHIPS):
                own = g_refs[i][pl.ds(pl.multiple_of((2 * j + c) * r, FLAT_ROWS), r), :]
                o_refs[i][j] = (own.astype(F32) + h_refs[i][j].astype(F32)).astype(o_refs[i].dtype)

    return pl.pallas_call(
        body, name=name, out_shape=[jax.ShapeDtypeStruct(h.shape, h.dtype) for h in halves],
        compiler_params=pltpu.CompilerParams(vmem_limit_bytes=VMEM_LIMIT),
    )(*grads, *halves)


class _Both:
    def __init__(self, first, second):
        self.jobs = (first, second)
        self.inputs = first.inputs + second.inputs
        self.out_shape = first.out_shape + second.out_shape
        self.sems = first.sems + second.sems

    def _each(self, ins, outs, sems):
        a = self.jobs[0]
        i, o, s = len(a.inputs), len(a.out_shape), len(a.sems)
        return ((a, ins[:i], outs[:o], sems[:s]), (self.jobs[1], ins[i:], outs[o:], sems[s:]))

    def before(self, ins, outs, sems):
        for job, i, o, s in self._each(ins, outs, sems):
            job.before(i, o, s)

    def after(self, ins, outs, sems):
        for job, i, o, s in self._each(ins, outs, sems):
            job.after(i, o, s)


def _run(body, comm, *, semantics, out_shape, in_specs, out_specs, scratch_shapes=(), **kw):
    if comm is None:
        return pl.pallas_call(body, out_shape=out_shape, in_specs=in_specs, out_specs=out_specs,
                              scratch_shapes=list(scratch_shapes), compiler_params=_params(semantics), **kw)
    single = not isinstance(out_shape, (list, tuple))
    outs = [out_shape] if single else list(out_shape)
    ospecs = [out_specs] if single else list(out_specs)
    counts = [len(in_specs), len(comm.inputs), len(outs), len(comm.out_shape), len(scratch_shapes), len(comm.sems)]

    def carrying(*refs):
        groups, pos = [], 0
        for c in counts:
            groups.append(refs[pos:pos + c])
            pos += c
        main_in, comm_in, main_out, comm_out, main_scratch, comm_sems = groups
        comm.before(comm_in, comm_out, comm_sems)
        body(*main_in, *main_out, *main_scratch)
        comm.after(comm_in, comm_out, comm_sems)

    call = pl.pallas_call(
        carrying, out_shape=outs + list(comm.out_shape), in_specs=list(in_specs) + [_ANY] * len(comm.inputs),
        out_specs=ospecs + [_ANY] * len(comm.out_shape), scratch_shapes=list(scratch_shapes) + list(comm.sems),
        compiler_params=_params("arbitrary"), **kw)

    def apply(*args):
        res = call(*args, *comm.inputs)
        main = res[:len(outs)]
        return (main[0] if single else list(main)), list(res[len(outs):])

    return apply


def _alone(comm, name):
    return _run(lambda: None, comm, semantics="arbitrary", name=name, grid=(1,), out_shape=[], in_specs=[], out_specs=[])()[1]


SHARDED = {"w_in": 1, "w_glu": 0, "conv_w": 1, "w_a_out": 1, "w_b_out": 0, "w_o": 0, "w_ffn_gate": 1, "w_ffn_up": 1,
           "w_ffn_down": 0, "w_ple_gate": 0, "w_ple": 1}
TRANSPOSED = ("w_in", "w_a_out", "w_ffn_gate", "w_ffn_up", "w_ple")
LONG_AXIS_MINOR = ("w_in", "w_ffn_gate", "w_ffn_up")
NARROW_LAST = ("s5_b_re", "s5_b_im", "s5_d")
SMALL = ["g_mix", "b_in", "lam_re", "lam_im", "log_dt", "s5_b_re", "s5_b_im", "s5_c_re", "s5_c_im", "s5_d", "b_glu",
         "conv_b", "w_r", "b_r", "w_i", "b_i", "lru_lambda", "g_ffn", "g_ple_gate", "b_ple_gate", "g_ple", "g_final"]
WEIGHTS = ["g_mix", "w_in", "b_in", "lam_re", "lam_im", "log_dt", "s5_b_re", "s5_b_im", "s5_c_re", "s5_c_im", "s5_d",
           "w_glu", "b_glu", "conv_w", "conv_b", "w_r", "b_r", "w_i", "b_i", "lru_lambda", "w_a_out", "w_b_out", "w_o",
           "g_ffn", "w_ffn_gate", "w_ffn_up", "w_ffn_down", "g_ple_gate", "w_ple_gate", "b_ple_gate", "w_ple", "g_ple",
           "g_final"]


def _unblock(gathered, axis):
    nb, r, c = gathered.shape
    if axis == 0:
        return gathered.reshape(nb * r, c)
    return jnp.transpose(gathered, (1, 0, 2)).reshape(r, nb * c)


def _disc_scalars(lr, li, ldt):
    dt = jnp.exp(ldt)
    mag = jnp.exp(lr * dt)
    ar = mag * jnp.cos(li * dt)
    ai = mag * jnp.sin(li * dt)
    den = lr * lr + li * li
    nr = ar - 1.0
    fr = (nr * lr + ai * li) / den
    fi = (ai * lr - nr * li) / den
    return ar, ai, fr, fi


def _disc_cols(lr, li, ldt, b_re, b_im):
    ar, ai, fr, fi = _disc_scalars(lr, li, ldt)
    return ar, ai, fr * b_re - fi * b_im, fr * b_im + fi * b_re


def _local_step(x, p, target, src, small, disc, distributed=True):
    full = {} if distributed else dict(src)
    gw, halves, pairs, got = {}, {}, {}, {}

    def gather(keys):
        return (_Gather([src[k] for k in keys]), keys, full) if distributed else None

    def swap(keys):
        return (_PairSwap([gw[k] for k in keys]), keys, halves) if distributed else None

    def chips(keys):
        return (_ChipExchange([pairs[k] for k in keys]), keys, got) if distributed else None

    def add_pairs(keys):
        if distributed:
            pairs.update(zip(keys, _pair_add([gw[k] for k in keys], [halves[k] for k in keys], "pair_add_" + keys[0])))

    def carry(fn, *args, jobs=()):
        jobs = [j for j in jobs if j is not None]
        if not jobs:
            return fn(*args)
        comm = jobs[0][0]
        for j in jobs[1:]:
            comm = _Both(comm, j[0])
        res, extra = fn(*args, comm=comm)
        for job, keys, sink in jobs:
            sink.update(zip(keys, extra[:len(job.out_shape)]))
            extra = extra[len(job.out_shape):]
        return res

    d = x.shape[1]
    g, n, pch = small["s5_b_re"].shape
    heads = small["w_r"].shape[0]
    sa, lw = g * pch, small["lru_lambda"].shape[-1]
    widths = [sa, lw, d, d]
    row = lambda v: v.reshape(1, -1)

    hd = small["w_r"].shape[-1]
    ar_row, ai_row, bbr_blk, bbi_blk, cre_blk, cimn_blk, wr_blk, wi_blk = carry(
        _prep, *disc["rows"], *disc["cols"], disc["b_re"], disc["b_im"], small["s5_c_re"].reshape(sa, n),
        small["s5_c_im"].reshape(sa, n), small["w_r"].reshape(lw, hd), small["w_i"].reshape(lw, hd),
        jobs=[gather(["w_in"])])
    d_row = row(small["s5_d"])
    u_a, u_b, za, zb = carry(_inproj_fwd, x, row(small["g_mix"]), full["w_in"], row(small["b_in"]), widths,
                             jobs=[gather(["w_glu", "conv_w", "w_a_out"])])
    conv_w = _unblock(full["conv_w"], 1) if distributed else full["conv_w"]
    sr, si, y, y_a = carry(_s5_fwd, u_a, bbr_blk, bbi_blk, ar_row, ai_row, cre_blk, cimn_blk, d_row, full["w_glu"],
                           row(small["b_glu"]), jobs=[gather(["w_ffn_gate", "w_b_out"])])
    xc, h, hprev = carry(_lru_fwd, u_b, conv_w, row(small["conv_b"]), wr_blk, row(small["b_r"]), wi_blk,
                         row(small["b_i"]), row(small["lru_lambda"]), jobs=[gather(["w_ffn_up", "w_o"])])
    x1, merged, ma, mb = _merge_fwd(y_a, h, za, zb, x, full["w_a_out"], full["w_b_out"], full["w_o"])
    fg, fu = carry(_ffn_up_fwd, x1, row(small["g_ffn"]), full["w_ffn_gate"], full["w_ffn_up"],
                   jobs=[gather(["w_ffn_down"])])
    x2 = carry(_ffn_down_fwd, fg, fu, x1, full["w_ffn_down"], jobs=[gather(["w_ple_gate", "w_ple"])])

    dx2, loss_blk, gw["w_ple_gate"], gw["w_ple"], vec_tail = _tail_fwd_bwd(
        x2, p, target, row(small["g_ple_gate"]), full["w_ple_gate"], row(small["b_ple_gate"]), full["w_ple"],
        row(small["g_ple"]), row(small["g_final"]))
    dfg, dfu, act = carry(_ffn_bwd_a, dx2, fg, fu, full["w_ffn_down"], jobs=[swap(["w_ple_gate", "w_ple"])])
    tn_d = min(d, 512)
    gw["w_ffn_down"] = _matmul_tn(act, dx2, tn_d, "dw_ffn_down", BF16)
    add_pairs(["w_ple_gate", "w_ple"])
    dx1, h2, vec_ffn = carry(_ffn_bwd_b, dfg, dfu, x1, dx2, row(small["g_ffn"]), full["w_ffn_gate"], full["w_ffn_up"],
                             jobs=[chips(["w_ple_gate", "w_ple"]), swap(["w_ffn_down"])])
    gw["w_ffn_gate"] = _matmul_tn(dfg, h2, tn_d, "dw_ffn_gate", BF16)
    gw["w_ffn_up"] = _matmul_tn(dfu, h2, tn_d, "dw_ffn_up", BF16)
    add_pairs(["w_ffn_down"])
    dza, dzb, dya, dyb, gw["w_o"], gw["w_a_out"], gw["w_b_out"] = carry(
        _merge_bwd, dx1, merged, ma, mb, za, zb, y_a, h, full["w_o"], full["w_a_out"], full["w_b_out"],
        jobs=[chips(["w_ffn_down"]), swap(["w_ffn_gate", "w_ffn_up"])])
    add_pairs(["w_ffn_gate", "w_ffn_up"])
    du_b, dw_r, dw_i, vec_lru = carry(
        _lru_bwd, dyb, xc, hprev, u_b, conv_w, wr_blk, row(small["b_r"]), wi_blk, row(small["b_i"]),
        row(small["lru_lambda"]), hd, jobs=[chips(["w_ffn_gate", "w_ffn_up"]), swap(["w_o", "w_a_out", "w_b_out"])])
    add_pairs(["w_o", "w_a_out", "w_b_out"])
    du_a, lam_r, lam_i, dy16, gw["w_glu"], vec_sa, vec_gn = carry(
        _s5_bwd, dya, y, sr, si, u_a, full["w_glu"], row(small["b_glu"]), cre_blk, cimn_blk, bbr_blk, bbi_blk, ar_row,
        ai_row, d_row, jobs=[chips(["w_o", "w_a_out", "w_b_out"])])
    smalls = {"vec_tail": vec_tail, "vec_ffn": vec_ffn, "vec_lru": vec_lru, "vec_sa": vec_sa, "vec_gn": vec_gn,
              "dw_r": dw_r, "dw_i": dw_i, "loss": loss_blk}
    everyones = {}

    def gather_smalls(keys):
        return (_Gather([smalls[k] for k in keys]), keys, everyones) if distributed else None

    smalls["dbb_re"], smalls["dbb_im"], smalls["dc_re"], smalls["dc_imn"] = carry(
        _s5_param_grads, lam_r, lam_i, sr, si, u_a, dy16, pch, n, jobs=[swap(["w_glu"]), gather_smalls(list(smalls))])
    add_pairs(["w_glu"])
    grad_x, h0, dz16, smalls["vec_mix"], smalls["vec_bin"] = _inproj_bwd(
        [du_a, du_b, dza, dzb], x, dx1, row(small["g_mix"]), full["w_in"])
    shapes = {k: a.shape for k, a in smalls.items()}
    gw["w_in"] = carry(_matmul_tn, dz16, h0, tn_d, "dw_in", BF16,
                       jobs=[chips(["w_glu"]), gather_smalls(["dbb_re", "dbb_im", "dc_re", "dc_imn"])])
    smalls.update(everyones)
    if distributed:
        got["w_in"] = gw["w_in"]
        gw = got
    return grad_x, gw, smalls, shapes


def _disc_inputs(small):
    g, n, pch = small["s5_b_re"].shape
    srcs = (small["lam_re"], small["lam_im"], jnp.repeat(small["log_dt"], n))
    return {"rows": [a.reshape(1, g * n) for a in srcs], "cols": [a.reshape(g * n, 1) for a in srcs],
            "b_re": small["s5_b_re"].reshape(g * n, pch), "b_im": small["s5_b_im"].reshape(g * n, pch)}


def kernel(x, p, g_mix, w_in, b_in, lam_re, lam_im, log_dt, s5_b_re, s5_b_im, s5_c_re, s5_c_im, s5_d, w_glu, b_glu, conv_w, conv_b, w_r, b_r, w_i, b_i, lru_lambda, w_a_out, w_b_out, w_o, g_ffn, w_ffn_gate, w_ffn_up, w_ffn_down, g_ple_gate, w_ple_gate, b_ple_gate, w_ple, g_ple, g_final, loss_target, m_g_mix, m_w_in, m_b_in, m_lam_re, m_lam_im, m_log_dt, m_s5_b_re, m_s5_b_im, m_s5_c_re, m_s5_c_im, m_s5_d, m_w_glu, m_b_glu, m_conv_w, m_conv_b, m_w_r, m_b_r, m_w_i, m_b_i, m_lru_lambda, m_w_a_out, m_w_b_out, m_w_o, m_g_ffn, m_w_ffn_gate, m_w_ffn_up, m_w_ffn_down, m_g_ple_gate, m_w_ple_gate, m_b_ple_gate, m_w_ple, m_g_ple, m_g_final, v_g_mix, v_w_in, v_b_in, v_lam_re, v_lam_im, v_log_dt, v_s5_b_re, v_s5_b_im, v_s5_c_re, v_s5_c_im, v_s5_d, v_w_glu, v_b_glu, v_conv_w, v_conv_b, v_w_r, v_b_r, v_w_i, v_b_i, v_lru_lambda, v_w_a_out, v_w_b_out, v_w_o, v_g_ffn, v_w_ffn_gate, v_w_ffn_up, v_w_ffn_down, v_g_ple_gate, v_w_ple_gate, v_b_ple_gate, v_w_ple, v_g_ple, v_g_final):
    given = dict(locals())
    wts = {k: given[k] for k in WEIGHTS}
    moms = {k: given["m_" + k] for k in WEIGHTS}
    vels = {k: given["v_" + k] for k in WEIGHTS}

    def drop_depth(k, a):
        return a if k == "g_final" else a[0]

    small = {k: drop_depth(k, wts[k]) for k in SMALL}
    shard = {k: wts[k][0] for k in SHARDED}
    names = list(SHARDED)

    def wire(k):
        if k == "conv_w":
            return shard[k]
        return (shard[k].T if k in TRANSPOSED else shard[k]).astype(BF16)

    disc = _disc_inputs(small)
    grad_x, parts, smalls, shapes = _local_step(x[0], p[0, 0], loss_target[0], {k: wire(k) for k in names}, small, disc)

    late = ["vec_mix", "vec_bin"]
    received = _alone(_Both(_PairSwap([parts["w_in"]]), _Gather([smalls[k] for k in late])), "swap_last")
    (pair_in,) = _pair_add([parts["w_in"]], received[:1], "pair_add_w_in")
    smalls.update(zip(late, received[1:]))

    g_small, (parts["w_in"],) = _small_reduce(smalls, shapes, *disc["cols"], disc["b_re"], disc["b_im"],
                                              small["s5_b_re"].shape[0], comm=_ChipExchange([pair_in]))
    loss = g_small.pop("loss")[0, 0]
    cols = shard["conv_w"].shape[1]
    mine = _index((lax.axis_index("x"), lax.axis_index("y"), lax.axis_index("c")))
    parts["conv_w"] = lax.dynamic_slice_in_dim(g_small.pop("conv_w"), mine * cols, cols, axis=1)[None]

    def view(k, a):
        a = a.reshape((1, -1) if k == "g_final" else wts[k].shape)
        return jnp.swapaxes(a, -1, -2) if k in NARROW_LAST else a

    def unview(k, a):
        return (jnp.swapaxes(a, -1, -2) if k in NARROW_LAST else a).reshape(wts[k].shape)

    slots = _adamw_small([view(k, wts[k]) for k in SMALL], [view(k, g_small[k]) for k in SMALL],
                         [view(k, moms[k]) for k in SMALL], [view(k, vels[k]) for k in SMALL])
    small_out = [dict(zip(SMALL, [unview(k, a) for k, a in zip(SMALL, slot)])) for slot in slots]

    big_out = {}
    for k in names:
        if k in LONG_AXIS_MINOR:
            res = _adamw(parts[k], shard[k].T, moms[k][0].T, vels[k][0].T, "adamw_" + k)
            big_out[k] = [a.T for a in res]
        else:
            big_out[k] = _adamw(parts[k], shard[k], moms[k][0], vels[k][0], "adamw_" + k, transposed=k in TRANSPOSED)

    outs = [loss, grad_x[None]]
    for slot in range(4):
        for k in WEIGHTS:
            if k in SHARDED:
                outs.append(big_out[k][slot][None])
            else:
                outs.append(small_out[slot][k])
    return tuple(outs)
```

```python
import math

import jax
import jax.numpy as jnp
from jax import lax
from jax.experimental import pallas as pl
from jax.experimental.pallas import tpu as pltpu

F32 = jnp.float32
BF16 = jnp.bfloat16

EPS = 1e-6
LRU_C = 8.0
CONV_WIDTH = 4
ADAM_LR = 0.001
ADAM_B1 = 0.9
ADAM_B2 = 0.999
ADAM_EPS = 1e-08
ADAM_WD = 0.01
ADAM_STEP = 10

N_DEV = 8
MESH = pl.DeviceIdType.MESH
SUBLANES = 8
LANES = 128
VMEM_LIMIT = 56 * 1024 * 1024
TOKEN_TILE = 256
TIME_CHUNK = 256
S5_SLAB = 128
LRU_SLAB = 256


def _dot(a, b):
    return jnp.dot(a.astype(BF16), b.astype(BF16), preferred_element_type=F32)


def _dot_nt(a, b):
    return lax.dot_general(a.astype(BF16), b.astype(BF16), (((1,), (1,)), ((), ())), preferred_element_type=F32)


def _dot_tn(a, b):
    return lax.dot_general(a.astype(BF16), b.astype(BF16), (((0,), (0,)), ((), ())), preferred_element_type=F32)


def _sigmoid(x):
    return jax.nn.sigmoid(x)


def _rms_stats(x):
    r = lax.rsqrt(jnp.mean(x * x, axis=-1, keepdims=True) + EPS)
    return x * r, r


def _rms_bwd(dy, xhat, r, g):
    dxn = dy * g
    dx = r * (dxn - xhat * jnp.mean(dxn * xhat, axis=-1, keepdims=True))
    return dx, dy * xhat


def _rowsum(v):
    return jnp.sum(v, axis=0, keepdims=True)


def _expm1(x):
    u = jnp.exp(x)
    um1 = u - 1.0
    safe = jnp.where(um1 == 0.0, 1.0, jnp.log(u))
    return jnp.where(um1 == 0.0, x, um1 * x / safe)


def _softplus(x):
    e = jnp.exp(-jnp.abs(x))
    u = 1.0 + e
    um1 = u - 1.0
    safe = jnp.where(um1 == 0.0, 1.0, um1)
    log1p_e = jnp.where(um1 == 0.0, e, jnp.log(u) * e / safe)
    return jnp.maximum(x, 0.0) + log1p_e


_GELU_K = math.sqrt(2.0 / math.pi)
_GELU_C = 0.044715


def _gelu(x):
    return 0.5 * x * (1.0 + jnp.tanh(_GELU_K * (x + _GELU_C * x * x * x)))


def _gelu_grad(x):
    th = jnp.tanh(_GELU_K * (x + _GELU_C * x * x * x))
    return 0.5 * (1.0 + th) + 0.5 * x * (1.0 - th * th) * _GELU_K * (1.0 + 3.0 * _GELU_C * x * x)


def _params(*sem):
    return pltpu.CompilerParams(dimension_semantics=sem, vmem_limit_bytes=VMEM_LIMIT)


def _rows(tm, n):
    return pl.BlockSpec((tm, n), lambda i: (i, 0))


def _rows_rev(tm, n, steps):
    return pl.BlockSpec((tm, n), lambda i: (steps - 1 - i, 0))


def _whole(shape):
    nd = len(shape)
    return pl.BlockSpec(shape, lambda i: (0,) * nd, pipeline_mode=pl.Buffered(1))


def _acc(shape):
    nd = len(shape)
    return pl.BlockSpec(shape, lambda i: (0,) * nd)


def _zero_on_first(*refs):
    @pl.when(pl.program_id(0) == 0)
    def _():
        for r in refs:
            r[...] = jnp.zeros_like(r)


def _inproj_fwd(x, g_mix, w_in_t, b_in, widths, comm=None):
    t, d = x.shape
    n = w_in_t.shape[0]
    tm = min(TOKEN_TILE, t)
    offs = [sum(widths[:i]) for i in range(len(widths) + 1)]

    def body(x_ref, g_ref, w_ref, b_ref, *outs):
        xhat, _ = _rms_stats(x_ref[...])
        h = (xhat * g_ref[...]).astype(BF16)
        for k, o_ref in enumerate(outs):
            lo, hi = offs[k], offs[k + 1]
            o_ref[...] = _dot_nt(h, w_ref[lo:hi, :]) + b_ref[:, lo:hi]

    return _run(
        body, comm, name="inproj_fwd", grid=(t // tm,),
        out_shape=[jax.ShapeDtypeStruct((t, w), F32) for w in widths],
        in_specs=[_rows(tm, d), _whole((1, d)), _whole((n, d)), _whole((1, n))],
        out_specs=[_rows(tm, w) for w in widths],
        semantics="parallel",
    )(x, g_mix, w_in_t, b_in)


def _s5_fwd(u, bbr_blk, bbi_blk, ar, ai, cre_blk, cimn_blk, d_skip, w_glu, b_glu, comm=None):
    t, sa = u.shape
    ns, _, sw = bbr_blk.shape
    gn = ns * sw
    tc = min(TIME_CHUNK, t)

    def body(u_ref, bbr_ref, bbi_ref, ar_ref, ai_ref, cre_ref, cim_ref, d_ref, wg_ref, bg_ref,
             sr_ref, si_ref, y_ref, ya_ref, cr_s, ci_s):
        _zero_on_first(cr_s, ci_s)
        uv = u_ref[...]
        ub = uv.astype(BF16)
        for m in range(ns):
            um = ub[:, m * S5_SLAB:(m + 1) * S5_SLAB]
            sr_ref[:, m * sw:(m + 1) * sw] = _dot(um, bbr_ref[m])
            si_ref[:, m * sw:(m + 1) * sw] = _dot(um, bbi_ref[m])
        a_r = ar_ref[...]
        a_i = ai_ref[...]

        def step(row, carry):
            c_r, c_i = carry
            at = pl.ds(row, 1)
            n_r = a_r * c_r - a_i * c_i + sr_ref[at, :]
            n_i = a_r * c_i + a_i * c_r + si_ref[at, :]
            sr_ref[at, :] = n_r
            si_ref[at, :] = n_i
            return n_r, n_i

        c_r, c_i = lax.fori_loop(0, tc, step, (cr_s[0:1, :], ci_s[0:1, :]), unroll=8)
        cr_s[0:1, :] = c_r
        ci_s[0:1, :] = c_i
        for m in range(ns):
            states, chans = slice(m * sw, (m + 1) * sw), slice(m * S5_SLAB, (m + 1) * S5_SLAB)
            y_ref[:, chans] = (_dot(sr_ref[:, states], cre_ref[m]) + _dot(si_ref[:, states], cim_ref[m])
                               + d_ref[:, chans] * uv[:, chans])
        y = y_ref[...]
        zz = _gelu(y)
        q = _dot(zz, wg_ref[...]) + bg_ref[...]
        ya_ref[...] = zz * _sigmoid(q)

    return _run(
        body, comm, name="s5_fwd", grid=(t // tc,),
        out_shape=[jax.ShapeDtypeStruct((t, gn), F32), jax.ShapeDtypeStruct((t, gn), F32),
                   jax.ShapeDtypeStruct((t, sa), F32), jax.ShapeDtypeStruct((t, sa), F32)],
        in_specs=[_rows(tc, sa), _whole(bbr_blk.shape), _whole(bbi_blk.shape), _whole((1, gn)), _whole((1, gn)),
                  _whole(cre_blk.shape), _whole(cimn_blk.shape), _whole((1, sa)), _whole((sa, sa)), _whole((1, sa))],
        out_specs=[_rows(tc, gn), _rows(tc, gn), _rows(tc, sa), _rows(tc, sa)],
        scratch_shapes=[pltpu.VMEM((SUBLANES, gn), F32), pltpu.VMEM((SUBLANES, gn), F32)],
        semantics="arbitrary",
    )(u, bbr_blk, bbi_blk, ar, ai, cre_blk, cimn_blk, d_skip, w_glu, b_glu)


def _slab_dot(x, w_ref, transposed=False):
    dot = _dot_nt if transposed else _dot
    xb = x.astype(BF16)
    return jnp.concatenate([dot(xb[:, j * LRU_SLAB:(j + 1) * LRU_SLAB], w_ref[j]) for j in range(w_ref.shape[0])], axis=1)


def _lru_gates(xc, wr_ref, br_ref, wi_ref, bi_ref, lam_ref):
    r = _sigmoid(_slab_dot(xc, wr_ref) + br_ref[...])
    ig = _sigmoid(_slab_dot(xc, wi_ref) + bi_ref[...])
    sp = _softplus(-lam_ref[...])
    log_a = (-LRU_C * r) * sp
    return r, ig, sp, log_a


def _lru_fwd(u, conv_w, conv_b, wr_blk, b_r, wi_blk, b_i, lru_lambda, comm=None):
    t, w = u.shape
    tc = min(TIME_CHUNK, t)
    halo = SUBLANES

    def body(u_ref, cw_ref, cb_ref, wr_ref, br_ref, wi_ref, bi_ref, lam_ref,
             xc_ref, h_ref, hp_ref, ext_s, a_s, carry_s):
        @pl.when(pl.program_id(0) == 0)
        def _():
            ext_s[0:halo, :] = jnp.zeros((halo, w), F32)
            carry_s[...] = jnp.zeros_like(carry_s)

        ext_s[halo:halo + tc, :] = u_ref[...]
        xc = cb_ref[...]
        for k in range(CONV_WIDTH):
            off = halo - (CONV_WIDTH - 1) + k
            xc = xc + cw_ref[k:k + 1, :] * ext_s[off:off + tc, :]
        ext_s[0:halo, :] = ext_s[tc:tc + halo, :]
        xc_ref[...] = xc
        r, ig, sp, log_a = _lru_gates(xc, wr_ref, br_ref, wi_ref, bi_ref, lam_ref)
        a_s[...] = jnp.exp(log_a)
        h_ref[...] = jnp.sqrt(-_expm1(2.0 * log_a)) * ig * xc

        def step(row, carry):
            at = pl.ds(row, 1)
            hp_ref[at, :] = carry
            nxt = a_s[at, :] * carry + h_ref[at, :]
            h_ref[at, :] = nxt
            return nxt

        carry_s[0:1, :] = lax.fori_loop(0, tc, step, carry_s[0:1, :], unroll=8)

    return _run(
        body, comm, name="lru_fwd", grid=(t // tc,),
        out_shape=[jax.ShapeDtypeStruct((t, w), F32)] * 3,
        in_specs=[_rows(tc, w), _whole((CONV_WIDTH, w)), _whole((1, w)), _whole(wr_blk.shape), _whole((1, w)),
                  _whole(wi_blk.shape), _whole((1, w)), _whole((1, w))],
        out_specs=[_rows(tc, w)] * 3,
        scratch_shapes=[pltpu.VMEM((halo + tc, w), F32), pltpu.VMEM((tc, w), F32), pltpu.VMEM((SUBLANES, w), F32)],
        semantics="arbitrary",
    )(u, conv_w, conv_b, wr_blk, b_r, wi_blk, b_i, lru_lambda)


def _merge_fwd(y_a, h, za, zb, x, w_a_out_t, w_b_out, w_o, comm=None):
    t, d = x.shape
    sa, lw = y_a.shape[1], h.shape[1]
    tm = min(TOKEN_TILE, t)

    def body(ya_ref, h_ref, za_ref, zb_ref, x_ref, wa_ref, wb_ref, wo_ref, x1_ref, mg_ref, ma_ref, mb_ref):
        ma = _dot_nt(ya_ref[...], wa_ref[...])
        mb = _dot(h_ref[...], wb_ref[...])
        merged = _sigmoid(za_ref[...]) * ma + _sigmoid(zb_ref[...]) * mb
        ma_ref[...] = ma
        mb_ref[...] = mb
        mg_ref[...] = merged.astype(mg_ref.dtype)
        x1_ref[...] = x_ref[...] + _dot(merged, wo_ref[...])

    return _run(
        body, comm, name="merge_fwd", grid=(t // tm,),
        out_shape=[jax.ShapeDtypeStruct((t, d), F32), jax.ShapeDtypeStruct((t, d), BF16),
                   jax.ShapeDtypeStruct((t, d), F32), jax.ShapeDtypeStruct((t, d), F32)],
        in_specs=[_rows(tm, sa), _rows(tm, lw), _rows(tm, d), _rows(tm, d), _rows(tm, d),
                  _whole((d, sa)), _whole((lw, d)), _whole((d, d))],
        out_specs=[_rows(tm, d)] * 4,
        semantics="parallel",
    )(y_a, h, za, zb, x, w_a_out_t, w_b_out, w_o)


def _ffn_up_fwd(x1, g_ffn, w_gate_t, w_up_t, comm=None):
    t, d = x1.shape
    f = w_gate_t.shape[0]
    tm = min(TOKEN_TILE, t)

    def body(x_ref, g_ref, wg_ref, wu_ref, fg_ref, fu_ref):
        xhat, _ = _rms_stats(x_ref[...])
        h2 = (xhat * g_ref[...]).astype(BF16)
        fg_ref[...] = _dot_nt(h2, wg_ref[...])
        fu_ref[...] = _dot_nt(h2, wu_ref[...])

    return _run(
        body, comm, name="ffn_up_fwd", grid=(t // tm,),
        out_shape=[jax.ShapeDtypeStruct((t, f), F32)] * 2,
        in_specs=[_rows(tm, d), _whole((1, d)), _whole((f, d)), _whole((f, d))],
        out_specs=[_rows(tm, f)] * 2,
        semantics="parallel",
    )(x1, g_ffn, w_gate_t, w_up_t)


def _ffn_down_fwd(fg, fu, x1, w_down, comm=None):
    t, d = x1.shape
    f = fg.shape[1]
    tm = min(TOKEN_TILE, t)

    def body(fg_ref, fu_ref, x_ref, wd_ref, x2_ref):
        fgv = fg_ref[...]
        act = fgv * _sigmoid(fgv) * fu_ref[...]
        x2_ref[...] = x_ref[...] + _dot(act, wd_ref[...])

    return _run(
        body, comm, name="ffn_down_fwd", grid=(t // tm,),
        out_shape=jax.ShapeDtypeStruct((t, d), F32),
        in_specs=[_rows(tm, f), _rows(tm, f), _rows(tm, d), _whole((f, d))],
        out_specs=_rows(tm, d),
        semantics="parallel",
    )(fg, fu, x1, w_down)


def _store_on_last(pairs):
    @pl.when(pl.program_id(0) == pl.num_programs(0) - 1)
    def _():
        for acc, out in pairs:
            out[...] = acc[...].astype(out.dtype)


def _tail_fwd_bwd(x2, p, target, g_pg, w_pg, b_pg, w_ple_t, g_ple, g_final):
    t, d = x2.shape
    pd = p.shape[1]
    tm = min(TOKEN_TILE, t)

    def body(x2_ref, p_ref, tg_ref, gpg_ref, wpg_ref, bpg_ref, wple_ref, gple_ref, gfin_ref,
             dx2_ref, loss_ref, dwpg_out, dwple_out, vec_ref, dwpg_ref, dwple_ref):
        _zero_on_first(loss_ref, dwpg_ref, dwple_ref, vec_ref)
        x2v = x2_ref[...]
        xh2, r2 = _rms_stats(x2v)
        h3 = xh2 * gpg_ref[...]
        gp = _sigmoid(_dot(h3, wpg_ref[...]) + bpg_ref[...])
        pe = _dot_nt(p_ref[...], wple_ref[...])
        peh, r3 = _rms_stats(pe)
        e = peh * gple_ref[...]
        x3 = x2v + gp * e
        xh3, r4 = _rms_stats(x3)
        diff = xh3 * gfin_ref[...] - tg_ref[...]
        loss_ref[...] += 0.5 * jnp.sum(jnp.mean(diff * diff, axis=-1, keepdims=True))
        dy = diff * (1.0 / d)
        dx3, dgfin = _rms_bwd(dy, xh3, r4, gfin_ref[...])
        d_gp = dx3 * e
        d_e = dx3 * gp
        dpe, dgple = _rms_bwd(d_e, peh, r3, gple_ref[...])
        dwple_ref[...] += _dot_tn(dpe, p_ref[...])
        dpre = d_gp * gp * (1.0 - gp)
        dwpg_ref[...] += _dot_tn(h3, dpre)
        dh3 = _dot_nt(dpre, wpg_ref[...])
        dx2n, dgpg = _rms_bwd(dh3, xh2, r2, gpg_ref[...])
        dx2_ref[...] = dx3 + dx2n
        vec_ref[0:1, :] += _rowsum(dpre)
        vec_ref[1:2, :] += _rowsum(dgpg)
        vec_ref[2:3, :] += _rowsum(dgple)
        vec_ref[3:4, :] += _rowsum(dgfin)
        _store_on_last([(dwpg_ref, dwpg_out), (dwple_ref, dwple_out)])

    return pl.pallas_call(
        body, name="tail_fwd_bwd", grid=(t // tm,),
        out_shape=[jax.ShapeDtypeStruct((t, d), F32), jax.ShapeDtypeStruct((SUBLANES, LANES), F32),
                   jax.ShapeDtypeStruct((d, d), BF16), jax.ShapeDtypeStruct((d, pd), BF16),
                   jax.ShapeDtypeStruct((SUBLANES, d), F32)],
        in_specs=[_rows(tm, d), _rows(tm, pd), _rows(tm, d), _whole((1, d)), _whole((d, d)), _whole((1, d)),
                  _whole((d, pd)), _whole((1, d)), _whole((1, d))],
        out_specs=[_rows(tm, d), _acc((SUBLANES, LANES)), _acc((d, d)), _acc((d, pd)), _acc((SUBLANES, d))],
        scratch_shapes=[pltpu.VMEM((d, d), F32), pltpu.VMEM((d, pd), F32)],
        compiler_params=_params("arbitrary"),
    )(x2, p, target, g_pg, w_pg, b_pg, w_ple_t, g_ple, g_final)


def _ffn_bwd_a(dx2, fg, fu, w_down, comm=None):
    t, d = dx2.shape
    f = fg.shape[1]
    tm = min(TOKEN_TILE, t)

    def body(dx_ref, fg_ref, fu_ref, wd_ref, dfg_ref, dfu_ref, act_ref):
        dact = _dot_nt(dx_ref[...], wd_ref[...])
        fgv = fg_ref[...]
        fuv = fu_ref[...]
        sg = _sigmoid(fgv)
        silu = fgv * sg
        dfu_ref[...] = (dact * silu).astype(dfu_ref.dtype)
        dfg_ref[...] = (dact * fuv * (sg * (1.0 + fgv * (1.0 - sg)))).astype(dfg_ref.dtype)
        act_ref[...] = (silu * fuv).astype(act_ref.dtype)

    return _run(
        body, comm, name="ffn_bwd_a", grid=(t // tm,),
        out_shape=[jax.ShapeDtypeStruct((t, f), BF16)] * 3,
        in_specs=[_rows(tm, d), _rows(tm, f), _rows(tm, f), _whole((f, d))],
        out_specs=[_rows(tm, f)] * 3,
        semantics="parallel",
    )(dx2, fg, fu, w_down)


def _ffn_bwd_b(dfg, dfu, x1, dx2, g_ffn, w_gate_t, w_up_t, comm=None):
    t, d = x1.shape
    f = dfg.shape[1]
    tm = min(TOKEN_TILE, t)

    def body(dfg_ref, dfu_ref, x_ref, dx2_ref, g_ref, wg_ref, wu_ref, dx1_ref, h2_ref, vec_ref):
        _zero_on_first(vec_ref)
        dh2 = _dot(dfg_ref[...], wg_ref[...]) + _dot(dfu_ref[...], wu_ref[...])
        xhat, r = _rms_stats(x_ref[...])
        h2_ref[...] = (xhat * g_ref[...]).astype(h2_ref.dtype)
        dxn, dg = _rms_bwd(dh2, xhat, r, g_ref[...])
        dx1_ref[...] = dx2_ref[...] + dxn
        vec_ref[0:1, :] += _rowsum(dg)

    return _run(
        body, comm, name="ffn_bwd_b", grid=(t // tm,),
        out_shape=[jax.ShapeDtypeStruct((t, d), F32), jax.ShapeDtypeStruct((t, d), BF16),
                   jax.ShapeDtypeStruct((SUBLANES, d), F32)],
        in_specs=[_rows(tm, f), _rows(tm, f), _rows(tm, d), _rows(tm, d), _whole((1, d)), _whole((f, d)), _whole((f, d))],
        out_specs=[_rows(tm, d), _rows(tm, d), _acc((SUBLANES, d))],
        semantics="arbitrary",
    )(dfg, dfu, x1, dx2, g_ffn, w_gate_t, w_up_t)


def _matmul_tn(a, b, tn, name, dtype=F32, comm=None):
    t, k = a.shape
    n = b.shape[1]

    def body(a_ref, b_ref, o_ref):
        o_ref[...] = _dot_tn(a_ref[...], b_ref[...]).astype(o_ref.dtype)

    return _run(
        body, comm, name=name, grid=(n // tn,),
        out_shape=jax.ShapeDtypeStruct((k, n), dtype),
        in_specs=[_whole((t, k)), pl.BlockSpec((t, tn), lambda j: (0, j))],
        out_specs=pl.BlockSpec((k, tn), lambda j: (0, j)),
        semantics="parallel",
    )(a, b)


def _merge_bwd(dx1, merged, ma, mb, za, zb, y_a, h, w_o, w_a_out_t, w_b_out, comm=None):
    t, d = dx1.shape
    sa, lw = y_a.shape[1], h.shape[1]
    tm = min(TOKEN_TILE, t)

    def body(dx1_ref, mg_ref, ma_ref, mb_ref, za_ref, zb_ref, ya_ref, h_ref, wo_ref, wa_ref, wb_ref,
             dza_ref, dzb_ref, dya_ref, dyb_ref, dwo_out, dwa_out, dwb_out, dwo_ref, dwa_ref, dwb_ref):
        _zero_on_first(dwo_ref, dwa_ref, dwb_ref)
        dx1v = dx1_ref[...].astype(BF16)
        dmg = _dot_nt(dx1v, wo_ref[...])
        ga = _sigmoid(za_ref[...])
        gb = _sigmoid(zb_ref[...])
        dza_ref[...] = dmg * ma_ref[...] * ga * (1.0 - ga)
        dzb_ref[...] = dmg * mb_ref[...] * gb * (1.0 - gb)
        dma = (dmg * ga).astype(BF16)
        dmb = (dmg * gb).astype(BF16)
        dya_ref[...] = _dot(dma, wa_ref[...])
        dyb_ref[...] = _dot_nt(dmb, wb_ref[...])
        dwo_ref[...] += _dot_tn(mg_ref[...], dx1v)
        dwa_ref[...] += _dot_tn(dma, ya_ref[...])
        dwb_ref[...] += _dot_tn(h_ref[...], dmb)
        _store_on_last([(dwo_ref, dwo_out), (dwa_ref, dwa_out), (dwb_ref, dwb_out)])

    return _run(
        body, comm, name="merge_bwd", grid=(t // tm,),
        out_shape=[jax.ShapeDtypeStruct((t, d), F32), jax.ShapeDtypeStruct((t, d), F32),
                   jax.ShapeDtypeStruct((t, sa), F32), jax.ShapeDtypeStruct((t, lw), F32),
                   jax.ShapeDtypeStruct((d, d), BF16), jax.ShapeDtypeStruct((d, sa), BF16),
                   jax.ShapeDtypeStruct((lw, d), BF16)],
        in_specs=[_rows(tm, d), _rows(tm, d), _rows(tm, d), _rows(tm, d), _rows(tm, d), _rows(tm, d),
                  _rows(tm, sa), _rows(tm, lw), _whole((d, d)), _whole((d, sa)), _whole((lw, d))],
        out_specs=[_rows(tm, d), _rows(tm, d), _rows(tm, sa), _rows(tm, lw), _acc((d, d)), _acc((d, sa)), _acc((lw, d))],
        scratch_shapes=[pltpu.VMEM((d, d), F32), pltpu.VMEM((d, sa), F32), pltpu.VMEM((lw, d), F32)],
        semantics="arbitrary",
    )(dx1, merged, ma, mb, za, zb, y_a, h, w_o, w_a_out_t, w_b_out)


def _fold_diag_blocks(dense, row_group, col_group, row0=0, col0=0):
    r, c = dense.shape
    rows = lax.broadcasted_iota(jnp.int32, (r, c), 0) + row0
    cols = lax.broadcasted_iota(jnp.int32, (r, c), 1) + col0
    kept = jnp.where(rows // row_group == cols // col_group, dense, 0.0)
    pick = (lax.broadcasted_iota(jnp.int32, (row_group, r), 0)
            == lax.broadcasted_iota(jnp.int32, (row_group, r), 1) % row_group).astype(F32)
    return jnp.dot(pick, kept, preferred_element_type=F32, precision=lax.Precision.HIGHEST)


def _lru_bwd(dh, xc, hprev, u, conv_w, wr_blk, b_r, wi_blk, b_i, lru_lambda, head_dim, comm=None):
    t, w = dh.shape
    tc = min(TIME_CHUNK, t)
    steps = t // tc
    halo = SUBLANES
    sub_per_chunk = tc // halo
    slabs = w // LRU_SLAB

    def body(dh_ref, xc_ref, hp_ref, u_ref, uh_ref, cw_ref, wr_ref, br_ref, wi_ref, bi_ref, lam_ref,
             du_ref, dwr_out, dwi_out, vec_ref, lam_s, a_s, dxc_s, uext_s, carry_s, dwr_ref, dwi_ref):
        chunk = steps - 1 - pl.program_id(0)

        @pl.when(pl.program_id(0) == 0)
        def _():
            carry_s[...] = jnp.zeros_like(carry_s)
            dxc_s[tc:tc + halo, :] = jnp.zeros((halo, w), F32)
            dwr_ref[...] = jnp.zeros_like(dwr_ref)
            dwi_ref[...] = jnp.zeros_like(dwi_ref)
            vec_ref[...] = jnp.zeros_like(vec_ref)

        xc = xc_ref[...]
        r, ig, sp, log_a = _lru_gates(xc, wr_ref, br_ref, wi_ref, bi_ref, lam_ref)
        a = jnp.exp(log_a)
        a_s[...] = a

        def step(i, q):
            at = pl.ds(tc - 1 - i, 1)
            lam_row = dh_ref[at, :] + q
            lam_s[at, :] = lam_row
            return a_s[at, :] * lam_row

        carry_s[0:1, :] = lax.fori_loop(0, tc, step, carry_s[0:1, :], unroll=8)
        lam = lam_s[...]
        mult = jnp.sqrt(-_expm1(2.0 * log_a))
        d_log_a = lam * hp_ref[...] * a - (lam * ig * xc) * (a * a) / mult
        d_ig = lam * mult * xc
        dpre_r = (d_log_a * (-LRU_C * sp)) * r * (1.0 - r)
        dpre_i = d_ig * ig * (1.0 - ig)
        dxc = lam * mult * ig + _slab_dot(dpre_r, wr_ref, transposed=True) + _slab_dot(dpre_i, wi_ref, transposed=True)
        xcb, drb, dib = xc.astype(BF16), dpre_r.astype(BF16), dpre_i.astype(BF16)
        for j in range(slabs):
            cols = slice(j * LRU_SLAB, (j + 1) * LRU_SLAB)
            dwr_ref[j] += _dot_tn(drb[:, cols], xcb[:, cols])
            dwi_ref[j] += _dot_tn(dib[:, cols], xcb[:, cols])
        vec_ref[0:1, :] += _rowsum(dxc)
        vec_ref[1:2, :] += _rowsum(dpre_r)
        vec_ref[2:3, :] += _rowsum(dpre_i)
        vec_ref[3:4, :] += _rowsum(d_log_a * (-LRU_C * r)) * (-_sigmoid(-lam_ref[...]))
        dxc_s[0:tc, :] = dxc
        du = cw_ref[CONV_WIDTH - 1:CONV_WIDTH, :] * dxc
        for k in range(CONV_WIDTH - 1):
            off = CONV_WIDTH - 1 - k
            du = du + cw_ref[k:k + 1, :] * dxc_s[off:off + tc, :]
        du_ref[...] = du
        dxc_s[tc:tc + halo, :] = dxc_s[0:halo, :]
        uext_s[0:halo, :] = jnp.where(chunk > 0, uh_ref[...], 0.0)
        uext_s[halo:halo + tc, :] = u_ref[...]
        for k in range(CONV_WIDTH):
            off = halo - (CONV_WIDTH - 1) + k
            vec_ref[4 + k:5 + k, :] += _rowsum(dxc * uext_s[off:off + tc, :])

        @pl.when(pl.program_id(0) == steps - 1)
        def _():
            for j in range(slabs):
                cols = slice(j * LRU_SLAB, (j + 1) * LRU_SLAB)
                dwr_out[:, cols] = _fold_diag_blocks(dwr_ref[j], head_dim, head_dim)
                dwi_out[:, cols] = _fold_diag_blocks(dwi_ref[j], head_dim, head_dim)

    halo_spec = pl.BlockSpec((halo, w), lambda i: (jnp.maximum((steps - 1 - i) * sub_per_chunk - 1, 0), 0))
    return _run(
        body, comm, name="lru_bwd", grid=(steps,),
        out_shape=[jax.ShapeDtypeStruct((t, w), F32), jax.ShapeDtypeStruct((head_dim, w), F32),
                   jax.ShapeDtypeStruct((head_dim, w), F32), jax.ShapeDtypeStruct((SUBLANES, w), F32)],
        in_specs=[_rows_rev(tc, w, steps)] * 4 + [halo_spec, _whole((CONV_WIDTH, w)), _whole(wr_blk.shape),
                                                  _whole((1, w)), _whole(wi_blk.shape), _whole((1, w)), _whole((1, w))],
        out_specs=[_rows_rev(tc, w, steps), _acc((head_dim, w)), _acc((head_dim, w)), _acc((SUBLANES, w))],
        scratch_shapes=[pltpu.VMEM((tc, w), F32), pltpu.VMEM((tc, w), F32), pltpu.VMEM((tc + halo, w), F32),
                        pltpu.VMEM((halo + tc, w), F32), pltpu.VMEM((SUBLANES, w), F32),
                        pltpu.VMEM((slabs, LRU_SLAB, LRU_SLAB), F32), pltpu.VMEM((slabs, LRU_SLAB, LRU_SLAB), F32)],
        semantics="arbitrary",
    )(dh, xc, hprev, u, u, conv_w, wr_blk, b_r, wi_blk, b_i, lru_lambda)


def _s5_bwd(dya, y, sr, si, u, w_glu, b_glu, cre_blk, cimn_blk, bbr_blk, bbi_blk, ar, ai, d_skip, comm=None):
    t, sa = dya.shape
    gn = sr.shape[1]
    ns, _, sw = bbr_blk.shape
    tc = min(TIME_CHUNK, t)
    steps = t // tc
    halo = SUBLANES

    def body(dya_ref, y_ref, sr_ref, si_ref, u_ref, wg_ref, bg_ref, cre_ref, cim_ref, bbr_ref, bbi_ref,
             ar_ref, ai_ref, d_ref, du_ref, lr_ref, li_ref, dy_ref, dwg_out, vsa_ref, vgn_ref, gr_s, gi_s, cr_s, ci_s,
             dwg_ref):
        @pl.when(pl.program_id(0) == 0)
        def _():
            cr_s[...] = jnp.zeros_like(cr_s)
            ci_s[...] = jnp.zeros_like(ci_s)
            gr_s[tc:tc + halo, :] = jnp.zeros((halo, gn), F32)
            gi_s[tc:tc + halo, :] = jnp.zeros((halo, gn), F32)
            dwg_ref[...] = jnp.zeros_like(dwg_ref)
            vsa_ref[...] = jnp.zeros_like(vsa_ref)
            vgn_ref[...] = jnp.zeros_like(vgn_ref)

        yv = y_ref[...]
        uv = u_ref[...]
        zz = _gelu(yv)
        sg = _sigmoid(_dot(zz, wg_ref[...]) + bg_ref[...])
        dyav = dya_ref[...]
        dq = dyav * zz * sg * (1.0 - sg)
        dzz = dyav * sg + _dot_nt(dq, wg_ref[...])
        dwg_ref[...] += _dot_tn(zz, dq)
        dy = dzz * _gelu_grad(yv)
        dyb = dy.astype(BF16)
        dy_ref[...] = dyb.astype(dy_ref.dtype)
        vsa_ref[0:1, :] += _rowsum(dq)
        vsa_ref[1:2, :] += _rowsum(dy * uv)
        for m in range(ns):
            dym = dyb[:, m * S5_SLAB:(m + 1) * S5_SLAB]
            gr_s[0:tc, m * sw:(m + 1) * sw] = _dot_nt(dym, cre_ref[m])
            gi_s[0:tc, m * sw:(m + 1) * sw] = _dot_nt(dym, cim_ref[m])
        a_r = ar_ref[...]
        a_i = ai_ref[...]

        def step(i, carry):
            l_r, l_i = carry
            at = pl.ds(tc - 1 - i, 1)
            n_r = gr_s[at, :] + a_r * l_r + a_i * l_i
            n_i = gi_s[at, :] + a_r * l_i - a_i * l_r
            gr_s[at, :] = n_r
            gi_s[at, :] = n_i
            return n_r, n_i

        l_r, l_i = lax.fori_loop(0, tc, step, (cr_s[0:1, :], ci_s[0:1, :]), unroll=8)
        cr_s[0:1, :] = l_r
        ci_s[0:1, :] = l_i
        nxt_r = gr_s[1:tc + 1, :]
        nxt_i = gi_s[1:tc + 1, :]
        srv = sr_ref[...]
        siv = si_ref[...]
        vgn_ref[0:1, :] += _rowsum(nxt_r * srv + nxt_i * siv)
        vgn_ref[1:2, :] += _rowsum(nxt_i * srv - nxt_r * siv)
        lam_r = gr_s[0:tc, :]
        lam_i = gi_s[0:tc, :]
        gr_s[tc:tc + halo, :] = gr_s[0:halo, :]
        gi_s[tc:tc + halo, :] = gi_s[0:halo, :]
        lrb = lam_r.astype(BF16)
        lib = lam_i.astype(BF16)
        lr_ref[...] = lrb.astype(lr_ref.dtype)
        li_ref[...] = lib.astype(li_ref.dtype)
        for m in range(ns):
            states, chans = slice(m * sw, (m + 1) * sw), slice(m * S5_SLAB, (m + 1) * S5_SLAB)
            du_ref[:, chans] = (_dot_nt(lrb[:, states], bbr_ref[m]) + _dot_nt(lib[:, states], bbi_ref[m])
                                + dy[:, chans] * d_ref[:, chans])
        _store_on_last([(dwg_ref, dwg_out)])

    return _run(
        body, comm, name="s5_bwd", grid=(steps,),
        out_shape=[jax.ShapeDtypeStruct((t, sa), F32), jax.ShapeDtypeStruct((t, gn), BF16),
                   jax.ShapeDtypeStruct((t, gn), BF16), jax.ShapeDtypeStruct((t, sa), BF16),
                   jax.ShapeDtypeStruct((sa, sa), BF16), jax.ShapeDtypeStruct((SUBLANES, sa), F32),
                   jax.ShapeDtypeStruct((SUBLANES, gn), F32)],
        in_specs=[_rows_rev(tc, sa, steps), _rows_rev(tc, sa, steps), _rows_rev(tc, gn, steps), _rows_rev(tc, gn, steps),
                  _rows_rev(tc, sa, steps), _whole((sa, sa)), _whole((1, sa)), _whole(cre_blk.shape),
                  _whole(cimn_blk.shape), _whole(bbr_blk.shape), _whole(bbi_blk.shape), _whole((1, gn)), _whole((1, gn)),
                  _whole((1, sa))],
        out_specs=[_rows_rev(tc, sa, steps), _rows_rev(tc, gn, steps), _rows_rev(tc, gn, steps), _rows_rev(tc, sa, steps),
                   _acc((sa, sa)), _acc((SUBLANES, sa)), _acc((SUBLANES, gn))],
        scratch_shapes=[pltpu.VMEM((tc + halo, gn), F32), pltpu.VMEM((tc + halo, gn), F32),
                        pltpu.VMEM((SUBLANES, gn), F32), pltpu.VMEM((SUBLANES, gn), F32), pltpu.VMEM((sa, sa), F32)],
        semantics="arbitrary",
    )(dya, y, sr, si, u, w_glu, b_glu, cre_blk, cimn_blk, bbr_blk, bbi_blk, ar, ai, d_skip)


def _inproj_bwd(dparts, x, dx1, g_mix, w_in_t, comm=None):
    t, d = x.shape
    n = w_in_t.shape[0]
    widths = [p.shape[1] for p in dparts]
    offs = [sum(widths[:i]) for i in range(len(widths) + 1)]
    tm = min(TOKEN_TILE, t)
    np_ = len(dparts)

    def body(*refs):
        dz_refs = refs[:np_]
        x_ref, dx1_ref, g_ref, w_ref, gx_ref, h_ref, dz_ref, vd_ref, vn_ref = refs[np_:]
        _zero_on_first(vd_ref, vn_ref)
        dh = jnp.zeros((tm, d), F32)
        for k, r in enumerate(dz_refs):
            lo, hi = offs[k], offs[k + 1]
            dzk = r[...]
            dh = dh + _dot(dzk, w_ref[lo:hi, :])
            dz_ref[:, lo:hi] = dzk.astype(dz_ref.dtype)
            vn_ref[0:1, lo:hi] += _rowsum(dzk)
        xhat, r0 = _rms_stats(x_ref[...])
        h_ref[...] = (xhat * g_ref[...]).astype(h_ref.dtype)
        dxn, dg = _rms_bwd(dh, xhat, r0, g_ref[...])
        gx_ref[...] = dx1_ref[...] + dxn
        vd_ref[0:1, :] += _rowsum(dg)

    return _run(
        body, comm, name="inproj_bwd", grid=(t // tm,),
        out_shape=[jax.ShapeDtypeStruct((t, d), F32), jax.ShapeDtypeStruct((t, d), BF16),
                   jax.ShapeDtypeStruct((t, n), BF16), jax.ShapeDtypeStruct((SUBLANES, d), F32),
                   jax.ShapeDtypeStruct((SUBLANES, n), F32)],
        in_specs=[_rows(tm, w) for w in widths] + [_rows(tm, d), _rows(tm, d), _whole((1, d)), _whole((n, d))],
        out_specs=[_rows(tm, d), _rows(tm, d), _rows(tm, n), _acc((SUBLANES, d)), _acc((SUBLANES, n))],
        semantics="arbitrary",
    )(*dparts, x, dx1, g_mix, w_in_t)


def _prep(lr_row, li_row, ldt_row, lr_col, li_col, ldt_col, b_re, b_im, c_re, c_im, w_r, w_i, comm=None):
    gn, pch = b_re.shape
    sa, n = c_re.shape
    w, hd = w_r.shape
    ns, sw, lsl = sa // S5_SLAB, S5_SLAB * n // pch, w // LRU_SLAB

    def spread_cols(vals, row_group, col_group, width):
        r, k = vals.shape
        tile = (lax.broadcasted_iota(jnp.int32, (k, width), 0) == lax.broadcasted_iota(jnp.int32, (k, width), 1) % k)
        rows = lax.broadcasted_iota(jnp.int32, (r, width), 0) // row_group
        cols = lax.broadcasted_iota(jnp.int32, (r, width), 1) // col_group
        return jnp.where(rows == cols, _dot(vals, tile.astype(BF16)), 0.0)

    def spread_rows(vals, row_group, col_group, height):
        k, c = vals.shape
        tile = (lax.broadcasted_iota(jnp.int32, (height, k), 0) % k == lax.broadcasted_iota(jnp.int32, (height, k), 1))
        rows = lax.broadcasted_iota(jnp.int32, (height, c), 0) // row_group
        cols = lax.broadcasted_iota(jnp.int32, (height, c), 1) // col_group
        return jnp.where(rows == cols, _dot(tile.astype(BF16), vals), 0.0)

    def body(lrr, lir, ldr, lrc, lic, ldc, bre, bim, cre, cim, wr, wi,
             ar_o, ai_o, bbr_o, bbi_o, cre_o, cim_o, wr_o, wi_o):
        ar, ai, _, _ = _disc_scalars(lrr[...], lir[...], ldr[...])
        ar_o[...] = ar
        ai_o[...] = ai
        _, _, bbr, bbi = _disc_cols(lrc[...], lic[...], ldc[...], bre[...], bim[...])
        bbr_t, bbi_t = bbr.T, bbi.T
        for m in range(ns):
            bbr_o[m] = spread_rows(bbr_t[:, m * sw:(m + 1) * sw], pch, n, S5_SLAB).astype(bbr_o.dtype)
            bbi_o[m] = spread_rows(bbi_t[:, m * sw:(m + 1) * sw], pch, n, S5_SLAB).astype(bbi_o.dtype)
            rows = slice(m * S5_SLAB, (m + 1) * S5_SLAB)
            cre_o[m] = spread_rows(cre[rows, :].T, n, pch, sw).astype(cre_o.dtype)
            cim_o[m] = spread_rows(-cim[rows, :].T, n, pch, sw).astype(cim_o.dtype)
        for j in range(lsl):
            rows = slice(j * LRU_SLAB, (j + 1) * LRU_SLAB)
            wr_o[j] = spread_cols(wr[rows, :], hd, hd, LRU_SLAB).astype(wr_o.dtype)
            wi_o[j] = spread_cols(wi[rows, :], hd, hd, LRU_SLAB).astype(wi_o.dtype)

    args = (lr_row, li_row, ldt_row, lr_col, li_col, ldt_col, b_re, b_im, c_re, c_im, w_r, w_i)
    out_shape = [jax.ShapeDtypeStruct((1, gn), F32), jax.ShapeDtypeStruct((1, gn), F32),
                 jax.ShapeDtypeStruct((ns, S5_SLAB, sw), BF16), jax.ShapeDtypeStruct((ns, S5_SLAB, sw), BF16),
                 jax.ShapeDtypeStruct((ns, sw, S5_SLAB), BF16), jax.ShapeDtypeStruct((ns, sw, S5_SLAB), BF16),
                 jax.ShapeDtypeStruct((lsl, LRU_SLAB, LRU_SLAB), BF16),
                 jax.ShapeDtypeStruct((lsl, LRU_SLAB, LRU_SLAB), BF16)]
    return _run(
        body, comm, name="prep", grid=(1,), out_shape=out_shape, in_specs=[_whole(a.shape) for a in args],
        out_specs=[_acc(s.shape) for s in out_shape], semantics="arbitrary",
    )(*args)


def _s5_param_grads(lam_r, lam_i, sr, si, u, dy, pch, n, comm=None):
    t, gn = lam_r.shape
    sa = u.shape[1]
    sw = S5_SLAB * n // pch

    def body(lr_ref, li_ref, sr_ref, si_ref, u_ref, dy_ref, dbr_ref, dbi_ref, dcr_ref, dci_ref):
        uv = u_ref[...]
        dyv = dy_ref[...]
        dbr_ref[...] = _fold_diag_blocks(_dot_tn(uv, lr_ref[...]), pch, n)
        dbi_ref[...] = _fold_diag_blocks(_dot_tn(uv, li_ref[...]), pch, n)
        dcr_ref[...] = _fold_diag_blocks(_dot_tn(sr_ref[...], dyv), n, pch)
        dci_ref[...] = _fold_diag_blocks(_dot_tn(si_ref[...], dyv), n, pch)

    states = pl.BlockSpec((t, sw), lambda m: (0, m))
    chans = pl.BlockSpec((t, S5_SLAB), lambda m: (0, m))
    return _run(
        body, comm, name="s5_param_grads", grid=(sa // S5_SLAB,),
        out_shape=[jax.ShapeDtypeStruct((pch, gn), F32), jax.ShapeDtypeStruct((pch, gn), F32),
                   jax.ShapeDtypeStruct((n, sa), F32), jax.ShapeDtypeStruct((n, sa), F32)],
        in_specs=[states, states, states, states, chans, chans],
        out_specs=[pl.BlockSpec((pch, sw), lambda m: (0, m)), pl.BlockSpec((pch, sw), lambda m: (0, m)),
                   pl.BlockSpec((n, S5_SLAB), lambda m: (0, m)), pl.BlockSpec((n, S5_SLAB), lambda m: (0, m))],
        semantics="parallel",
    )(lam_r, lam_i, sr, si, u, dy)


SMALL_PARTS = ["vec_tail", "vec_ffn", "vec_lru", "vec_mix", "vec_bin", "vec_sa", "vec_gn", "dw_r", "dw_i", "dbb_re",
               "dbb_im", "dc_re", "dc_imn", "loss"]


def _small_reduce(parts, shapes, lr_col, li_col, ldt_col, b_re, b_im, groups, comm=None):
    gn, pch = b_re.shape
    n = gn // groups
    nparts = parts[SMALL_PARTS[0]].size // math.prod(shapes[SMALL_PARTS[0]])
    np_, nout = len(SMALL_PARTS), 24

    def body(*refs):
        ins = refs[:np_]
        lr, li, ldt, bre, bim = refs[np_:np_ + 5]
        outs = refs[np_ + 5:np_ + 5 + nout]
        sums = dict(zip(SMALL_PARTS, refs[np_ + 5 + nout:]))

        @pl.when(pl.program_id(0) == 0)
        def _():
            for k, r in zip(SMALL_PARTS, ins):
                sums[k][...] = r[...]

        @pl.when(pl.program_id(0) > 0)
        def _():
            for k, r in zip(SMALL_PARTS, ins):
                sums[k][...] += r[...]

        @pl.when(pl.program_id(0) == nparts - 1)
        def _():
            finish({k: s[...] for k, s in sums.items()}, lr, li, ldt, bre, bim, *outs)

    def finish(tot, lr, li, ldt, bre, bim, o_loss, o_gmix, o_bin, o_bglu, o_s5d, o_convb, o_br, o_bi, o_lam, o_gffn,
               o_gpg, o_bpg, o_gple, o_gfin, o_wr, o_wi, o_cre, o_cim, o_bre, o_bim, o_lre, o_lim, o_ldt, o_convw):
        o_loss[...] = tot["loss"]
        o_convw[...] = tot["vec_lru"][SUBLANES - CONV_WIDTH:SUBLANES]
        o_bpg[...] = tot["vec_tail"][0:1]
        o_gpg[...] = tot["vec_tail"][1:2]
        o_gple[...] = tot["vec_tail"][2:3]
        o_gfin[...] = tot["vec_tail"][3:4]
        o_gffn[...] = tot["vec_ffn"][0:1]
        o_convb[...] = tot["vec_lru"][0:1]
        o_br[...] = tot["vec_lru"][1:2]
        o_bi[...] = tot["vec_lru"][2:3]
        o_lam[...] = tot["vec_lru"][3:4]
        o_gmix[...] = tot["vec_mix"][0:1]
        o_bin[...] = tot["vec_bin"][0:1]
        o_bglu[...] = tot["vec_sa"][0:1]
        o_s5d[...] = tot["vec_sa"][1:2]
        o_wr[...] = tot["dw_r"].T
        o_wi[...] = tot["dw_i"].T
        o_cre[...] = tot["dc_re"].T
        o_cim[...] = -tot["dc_imn"].T
        d_a = tot["vec_gn"].T
        _, chain = jax.vjp(_disc_cols, lr[...], li[...], ldt[...], bre[...], bim[...])
        d_lr, d_li, d_ldt, d_bre, d_bim = chain((d_a[:, 0:1], d_a[:, 1:2], tot["dbb_re"].T, tot["dbb_im"].T))
        o_lre[...] = d_lr
        o_lim[...] = d_li
        o_bre[...] = d_bre
        o_bim[...] = d_bim
        same = (lax.broadcasted_iota(jnp.int32, (groups, gn), 0)
                == lax.broadcasted_iota(jnp.int32, (groups, gn), 1) // n).astype(F32)
        o_ldt[...] = jnp.dot(same, d_ldt * jnp.ones((1, LANES), F32), preferred_element_type=F32,
                             precision=lax.Precision.HIGHEST)[:, 0:1]

    d = shapes["vec_mix"][1]
    nz = shapes["vec_bin"][1]
    sa = shapes["vec_sa"][1]
    w = shapes["vec_lru"][1]
    row = lambda c: jax.ShapeDtypeStruct((1, c), F32)
    out_shape = [jax.ShapeDtypeStruct(shapes["loss"], F32), row(d), row(nz), row(sa), row(sa), row(w), row(w), row(w),
                 row(w), row(d), row(d), row(d), row(d), row(d),
                 jax.ShapeDtypeStruct(shapes["dw_r"][::-1], F32), jax.ShapeDtypeStruct(shapes["dw_i"][::-1], F32),
                 jax.ShapeDtypeStruct(shapes["dc_re"][::-1], F32), jax.ShapeDtypeStruct(shapes["dc_imn"][::-1], F32),
                 jax.ShapeDtypeStruct((gn, pch), F32), jax.ShapeDtypeStruct((gn, pch), F32),
                 jax.ShapeDtypeStruct((gn, 1), F32), jax.ShapeDtypeStruct((gn, 1), F32),
                 jax.ShapeDtypeStruct((groups, 1), F32), jax.ShapeDtypeStruct((CONV_WIDTH, w), F32)]
    def part_spec(k):
        r, c = shapes[k]
        if parts[k].ndim == 3:
            return pl.BlockSpec((None, r, c), lambda i: (i, 0, 0))
        return pl.BlockSpec((r, c), lambda i: (i, 0))

    outs = _run(
        body, comm, name="small_reduce", grid=(nparts,), out_shape=out_shape,
        in_specs=[part_spec(k) for k in SMALL_PARTS] + [_whole(a.shape) for a in (lr_col, li_col, ldt_col, b_re, b_im)],
        out_specs=[_acc(s.shape) for s in out_shape],
        scratch_shapes=[pltpu.VMEM(shapes[k], F32) for k in SMALL_PARTS],
        semantics="arbitrary",
    )(*[parts[k] for k in SMALL_PARTS], lr_col, li_col, ldt_col, b_re, b_im)
    extra = None
    if comm is not None:
        outs, extra = outs
    names = ["loss", "g_mix", "b_in", "b_glu", "s5_d", "conv_b", "b_r", "b_i", "lru_lambda", "g_ffn", "g_ple_gate",
             "b_ple_gate", "g_ple", "g_final", "w_r", "w_i", "s5_c_re", "s5_c_im", "s5_b_re", "s5_b_im", "lam_re",
             "lam_im", "log_dt", "conv_w"]
    res = dict(zip(names, outs))
    return res if comm is None else (res, extra)


def _adamw_small(ws, gs, ms, vs):
    n = len(ws)

    def body(*refs):
        w_r, g_r, m_r, v_r = (refs[i * n:(i + 1) * n] for i in range(4))
        g_o, d_o, m_o, v_o = (refs[(4 + i) * n:(5 + i) * n] for i in range(4))
        for i in range(n):
            g = g_r[i][...]
            delta, m_new, v_new = _adamw_math(w_r[i][...], g, m_r[i][...], v_r[i][...])
            g_o[i][...] = g
            d_o[i][...] = delta
            m_o[i][...] = m_new
            v_o[i][...] = v_new

    shapes = [jax.ShapeDtypeStruct(a.shape, F32) for a in ws]
    outs = pl.pallas_call(body, name="adamw_small", out_shape=shapes * 4)(*ws, *gs, *ms, *vs)
    return outs[:n], outs[n:2 * n], outs[2 * n:3 * n], outs[3 * n:]


def _adamw_math(w, g, m, v):
    m_new = ADAM_B1 * m + (1.0 - ADAM_B1) * g
    v_new = ADAM_B2 * v + (1.0 - ADAM_B2) * (g * g)
    m_hat = m_new / (1.0 - ADAM_B1 ** ADAM_STEP)
    v_hat = v_new / (1.0 - ADAM_B2 ** ADAM_STEP)
    delta = -ADAM_LR * (m_hat / (jnp.sqrt(v_hat) + ADAM_EPS) + ADAM_WD * w)
    return delta, m_new, v_new


def _row_tile(rows):
    for cand in range(256, 0, -16):
        if rows % cand == 0:
            return cand
    return rows


def _adamw(parts, w, m, v, name, transposed=False):
    rows, cols = w.shape
    npart = parts.shape[0]
    tr = _row_tile(rows)
    if transposed:
        parts_spec = pl.BlockSpec((npart, cols, tr), lambda i: (0, 0, i))
    else:
        parts_spec = pl.BlockSpec((npart, tr, cols), lambda i: (0, i, 0))

    def body(p_ref, w_ref, m_ref, v_ref, g_ref, d_ref, mo_ref, vo_ref):
        g = p_ref[0].astype(F32)
        for k in range(1, npart):
            g = g + p_ref[k].astype(F32)
        if transposed:
            g = g.T
        delta, m_new, v_new = _adamw_math(w_ref[...], g, m_ref[...], v_ref[...])
        g_ref[...] = g
        d_ref[...] = delta
        mo_ref[...] = m_new
        vo_ref[...] = v_new

    return pl.pallas_call(
        body, name=name, grid=(rows // tr,),
        out_shape=[jax.ShapeDtypeStruct((rows, cols), F32)] * 4,
        in_specs=[parts_spec] + [_rows(tr, cols)] * 3,
        out_specs=[_rows(tr, cols)] * 4,
        compiler_params=_params("parallel"),
    )(parts, w, m, v)


def _mesh_position():
    return lax.axis_index("x"), lax.axis_index("y"), lax.axis_index("c")


def _flip(pos, rel):
    x, y, c = pos
    return (1 - x if rel & 4 else x, 1 - y if rel & 2 else y, 1 - c if rel & 1 else c)


def _index(pos):
    return 4 * pos[0] + 2 * pos[1] + pos[2]


_ANY = pl.BlockSpec(memory_space=pl.ANY)
FLAT_ROWS = 32


def _dma_sems(n):
    return [pltpu.SemaphoreType.DMA((n, N_DEV - 1)), pltpu.SemaphoreType.DMA((n, N_DEV - 1)), pltpu.SemaphoreType.DMA((n,))]


def _block_of(ref, idx, rows, flat):
    if flat:
        return ref.at[pl.ds(pl.multiple_of(idx * rows, FLAT_ROWS), rows), :]
    return ref.at[idx]


class _Gather:
    chips = (4, 2, 6)

    def __init__(self, shards):
        self.inputs = list(shards)
        self.flat = [s.shape[0] % FLAT_ROWS == 0 for s in shards]
        self.out_shape = [
            jax.ShapeDtypeStruct((N_DEV * s.shape[0], s.shape[1]) if f else (N_DEV,) + s.shape, s.dtype)
            for s, f in zip(shards, self.flat)]
        self.sems = _dma_sems(len(shards))

    def _copy(self, ins, outs, sems, i, k, block, to, own=False):
        dst = _block_of(outs[i], _index(block), self.inputs[i].shape[0], self.flat[i])
        return pltpu.make_async_remote_copy(
            src_ref=ins[i] if own else dst, dst_ref=dst, send_sem=sems[0].at[i, k], recv_sem=sems[1].at[i, k],
            device_id=to, device_id_type=MESH)

    def _local(self, ins, outs, sems, i, me):
        dst = _block_of(outs[i], _index(me), self.inputs[i].shape[0], self.flat[i])
        return pltpu.make_async_copy(ins[i], dst, sems[2].at[i])

    def _first(self, ins, outs, sems, i, me):
        cps = [self._copy(ins, outs, sems, i, 0, me, _flip(me, 1), own=True)]
        cps += [self._copy(ins, outs, sems, i, 1 + j, me, _flip(me, rel), own=True) for j, rel in enumerate(self.chips)]
        return cps

    def _passed(self, ins, outs, sems, i, j, me):
        return self._copy(ins, outs, sems, i, 4 + j, _flip(me, self.chips[j]), _flip(me, 1))

    def before(self, ins, outs, sems):
        n = len(self.inputs)
        me = _mesh_position()

        @pl.when(pl.program_id(0) == 0)
        def _():
            for i in range(n):
                self._local(ins, outs, sems, i, me).start()
                for cp in self._first(ins, outs, sems, i, me):
                    cp.start()

        @pl.when(pl.program_id(0) == pl.num_programs(0) - 1)
        def _():
            for j, rel in enumerate(self.chips):
                for i in range(n):
                    self._copy(ins, outs, sems, i, 1 + j, _flip(me, rel), me).wait_recv()
                    self._passed(ins, outs, sems, i, j, me).start()

    def after(self, ins, outs, sems):
        n = len(self.inputs)
        me = _mesh_position()
        sibling = _flip(me, 1)

        @pl.when(pl.program_id(0) == pl.num_programs(0) - 1)
        def _():
            for i in range(n):
                self._copy(ins, outs, sems, i, 0, sibling, me).wait_recv()
                for j, rel in enumerate(self.chips):
                    self._copy(ins, outs, sems, i, 4 + j, _flip(sibling, rel), me).wait_recv()
            for i in range(n):
                for cp in self._first(ins, outs, sems, i, me):
                    cp.wait_send()
                for j in range(len(self.chips)):
                    self._passed(ins, outs, sems, i, j, me).wait_send()
                self._local(ins, outs, sems, i, me).wait()


N_CHIPS = N_DEV // 2


def _chip(pos):
    return 2 * pos[0] + pos[1]


class _PairSwap:
    def __init__(self, arrays):
        self.inputs = list(arrays)
        self.rows = [a.shape[0] // N_DEV for a in arrays]
        for r in self.rows:
            assert r % FLAT_ROWS == 0, r
        self.out_shape = [jax.ShapeDtypeStruct((N_CHIPS, r, a.shape[1]), a.dtype) for a, r in zip(arrays, self.rows)]
        n = len(arrays)
        self.sems = [pltpu.SemaphoreType.DMA((n, N_CHIPS)), pltpu.SemaphoreType.DMA((n, N_CHIPS))]

    def _copy(self, ins, outs, sems, i, j, me):
        sibling = _flip(me, 1)
        return pltpu.make_async_remote_copy(
            src_ref=_block_of(ins[i], 2 * j + sibling[2], self.rows[i], True), dst_ref=outs[i].at[j],
            send_sem=sems[0].at[i, j], recv_sem=sems[1].at[i, j], device_id=sibling, device_id_type=MESH)

    def before(self, ins, outs, sems):
        me = _mesh_position()

        @pl.when(pl.program_id(0) == 0)
        def _():
            for i in range(len(self.inputs)):
                for j in range(N_CHIPS):
                    self._copy(ins, outs, sems, i, j, me).start()

    def after(self, ins, outs, sems):
        me = _mesh_position()

        @pl.when(pl.program_id(0) == pl.num_programs(0) - 1)
        def _():
            for i in range(len(self.inputs)):
                for j in range(N_CHIPS):
                    self._copy(ins, outs, sems, i, j, me).wait()


class _ChipExchange:
    chips = (4, 2, 6)

    def __init__(self, arrays):
        self.inputs = list(arrays)
        self.out_shape = [jax.ShapeDtypeStruct(a.shape, a.dtype) for a in arrays]
        n = len(arrays)
        self.sems = [pltpu.SemaphoreType.DMA((n, 3)), pltpu.SemaphoreType.DMA((n, 3)), pltpu.SemaphoreType.DMA((n,))]

    def _send(self, ins, outs, sems, i, k, me):
        peer = _flip(me, self.chips[k])
        return pltpu.make_async_remote_copy(
            src_ref=ins[i].at[_chip(peer)], dst_ref=outs[i].at[_chip(me)], send_sem=sems[0].at[i, k],
            recv_sem=sems[1].at[i, k], device_id=peer, device_id_type=MESH)

    def _arrival(self, ins, outs, sems, i, k, me):
        peer = _flip(me, self.chips[k])
        return pltpu.make_async_remote_copy(
            src_ref=ins[i].at[_chip(me)], dst_ref=outs[i].at[_chip(peer)], send_sem=sems[0].at[i, k],
            recv_sem=sems[1].at[i, k], device_id=peer, device_id_type=MESH)

    def _local(self, ins, outs, sems, i, me):
        return pltpu.make_async_copy(ins[i].at[_chip(me)], outs[i].at[_chip(me)], sems[2].at[i])

    def before(self, ins, outs, sems):
        me = _mesh_position()

        @pl.when(pl.program_id(0) == 0)
        def _():
            for i in range(len(self.inputs)):
                self._local(ins, outs, sems, i, me).start()
                for k in range(len(self.chips)):
                    self._send(ins, outs, sems, i, k, me).start()

    def after(self, ins, outs, sems):
        me = _mesh_position()

        @pl.when(pl.program_id(0) == pl.num_programs(0) - 1)
        def _():
            for i in range(len(self.inputs)):
                for k in range(len(self.chips)):
                    self._arrival(ins, outs, sems, i, k, me).wait_recv()
            for i in range(len(self.inputs)):
                for k in range(len(self.chips)):
                    self._send(ins, outs, sems, i, k, me).wait_send()
                self._local(ins, outs, sems, i, me).wait()


def _pair_add(grads, halves, name):
    n = len(grads)

    def body(*refs):
        g_refs, h_refs, o_refs = refs[:n], refs[n:2 * n], refs[2 * n:]
        c = lax.axis_index("c")
        for i in range(n):
            r = h_refs[i].shape[1]
            for j in range(N_CHIPS):
                own = g_refs[i][pl.ds(pl.multiple_of((2 * j + c) * r, FLAT_ROWS), r), :]
                o_refs[i][j] = (own.astype(F32) + h_refs[i][j].astype(F32)).astype(o_refs[i].dtype)

    return pl.pallas_call(
        body, name=name, out_shape=[jax.ShapeDtypeStruct(h.shape, h.dtype) for h in halves],
        compiler_params=pltpu.CompilerParams(vmem_limit_bytes=VMEM_LIMIT),
    )(*grads, *halves)


class _Both:
    def __init__(self, first, second):
        self.jobs = (first, second)
        self.inputs = first.inputs + second.inputs
        self.out_shape = first.out_shape + second.out_shape
        self.sems = first.sems + second.sems

    def _each(self, ins, outs, sems):
        a = self.jobs[0]
        i, o, s = len(a.inputs), len(a.out_shape), len(a.sems)
        return ((a, ins[:i], outs[:o], sems[:s]), (self.jobs[1], ins[i:], outs[o:], sems[s:]))

    def before(self, ins, outs, sems):
        for job, i, o, s in self._each(ins, outs, sems):
            job.before(i, o, s)

    def after(self, ins, outs, sems):
        for job, i, o, s in self._each(ins, outs, sems):
            job.after(i, o, s)


def _run(body, comm, *, semantics, out_shape, in_specs, out_specs, scratch_shapes=(), **kw):
    if comm is None:
        return pl.pallas_call(body, out_shape=out_shape, in_specs=in_specs, out_specs=out_specs,
                              scratch_shapes=list(scratch_shapes), compiler_params=_params(semantics), **kw)
    single = not isinstance(out_shape, (list, tuple))
    outs = [out_shape] if single else list(out_shape)
    ospecs = [out_specs] if single else list(out_specs)
    counts = [len(in_specs), len(comm.inputs), len(outs), len(comm.out_shape), len(scratch_shapes), len(comm.sems)]

    def carrying(*refs):
        groups, pos = [], 0
        for c in counts:
            groups.append(refs[pos:pos + c])
            pos += c
        main_in, comm_in, main_out, comm_out, main_scratch, comm_sems = groups
        comm.before(comm_in, comm_out, comm_sems)
        body(*main_in, *main_out, *main_scratch)
        comm.after(comm_in, comm_out, comm_sems)

    call = pl.pallas_call(
        carrying, out_shape=outs + list(comm.out_shape), in_specs=list(in_specs) + [_ANY] * len(comm.inputs),
        out_specs=ospecs + [_ANY] * len(comm.out_shape), scratch_shapes=list(scratch_shapes) + list(comm.sems),
        compiler_params=_params("arbitrary"), **kw)

    def apply(*args):
        res = call(*args, *comm.inputs)
        main = res[:len(outs)]
        return (main[0] if single else list(main)), list(res[len(outs):])

    return apply


def _alone(comm, name):
    return _run(lambda: None, comm, semantics="arbitrary", name=name, grid=(1,), out_shape=[], in_specs=[], out_specs=[])()[1]


SHARDED = {"w_in": 1, "w_glu": 0, "conv_w": 1, "w_a_out": 1, "w_b_out": 0, "w_o": 0, "w_ffn_gate": 1, "w_ffn_up": 1,
           "w_ffn_down": 0, "w_ple_gate": 0, "w_ple": 1}
TRANSPOSED = ("w_in", "w_a_out", "w_ffn_gate", "w_ffn_up", "w_ple")
LONG_AXIS_MINOR = ("w_in", "w_ffn_gate", "w_ffn_up")
NARROW_LAST = ("s5_b_re", "s5_b_im", "s5_d")
SMALL = ["g_mix", "b_in", "lam_re", "lam_im", "log_dt", "s5_b_re", "s5_b_im", "s5_c_re", "s5_c_im", "s5_d", "b_glu",
         "conv_b", "w_r", "b_r", "w_i", "b_i", "lru_lambda", "g_ffn", "g_ple_gate", "b_ple_gate", "g_ple", "g_final"]
WEIGHTS = ["g_mix", "w_in", "b_in", "lam_re", "lam_im", "log_dt", "s5_b_re", "s5_b_im", "s5_c_re", "s5_c_im", "s5_d",
           "w_glu", "b_glu", "conv_w", "conv_b", "w_r", "b_r", "w_i", "b_i", "lru_lambda", "w_a_out", "w_b_out", "w_o",
           "g_ffn", "w_ffn_gate", "w_ffn_up", "w_ffn_down", "g_ple_gate", "w_ple_gate", "b_ple_gate", "w_ple", "g_ple",
           "g_final"]


def _unblock(gathered, axis):
    nb, r, c = gathered.shape
    if axis == 0:
        return gathered.reshape(nb * r, c)
    return jnp.transpose(gathered, (1, 0, 2)).reshape(r, nb * c)


def _disc_scalars(lr, li, ldt):
    dt = jnp.exp(ldt)
    mag = jnp.exp(lr * dt)
    ar = mag * jnp.cos(li * dt)
    ai = mag * jnp.sin(li * dt)
    den = lr * lr + li * li
    nr = ar - 1.0
    fr = (nr * lr + ai * li) / den
    fi = (ai * lr - nr * li) / den
    return ar, ai, fr, fi


def _disc_cols(lr, li, ldt, b_re, b_im):
    ar, ai, fr, fi = _disc_scalars(lr, li, ldt)
    return ar, ai, fr * b_re - fi * b_im, fr * b_im + fi * b_re


def _local_step(x, p, target, src, small, disc, distributed=True):
    full = {} if distributed else dict(src)
    gw, halves, pairs, got = {}, {}, {}, {}

    def gather(keys):
        return (_Gather([src[k] for k in keys]), keys, full) if distributed else None

    def swap(keys):
        return (_PairSwap([gw[k] for k in keys]), keys, halves) if distributed else None

    def chips(keys):
        return (_ChipExchange([pairs[k] for k in keys]), keys, got) if distributed else None

    def add_pairs(keys):
        if distributed:
            pairs.update(zip(keys, _pair_add([gw[k] for k in keys], [halves[k] for k in keys], "pair_add_" + keys[0])))

    def carry(fn, *args, jobs=()):
        jobs = [j for j in jobs if j is not None]
        if not jobs:
            return fn(*args)
        comm = jobs[0][0]
        for j in jobs[1:]:
            comm = _Both(comm, j[0])
        res, extra = fn(*args, comm=comm)
        for job, keys, sink in jobs:
            sink.update(zip(keys, extra[:len(job.out_shape)]))
            extra = extra[len(job.out_shape):]
        return res

    d = x.shape[1]
    g, n, pch = small["s5_b_re"].shape
    heads = small["w_r"].shape[0]
    sa, lw = g * pch, small["lru_lambda"].shape[-1]
    widths = [sa, lw, d, d]
    row = lambda v: v.reshape(1, -1)

    hd = small["w_r"].shape[-1]
    ar_row, ai_row, bbr_blk, bbi_blk, cre_blk, cimn_blk, wr_blk, wi_blk = carry(
        _prep, *disc["rows"], *disc["cols"], disc["b_re"], disc["b_im"], small["s5_c_re"].reshape(sa, n),
        small["s5_c_im"].reshape(sa, n), small["w_r"].reshape(lw, hd), small["w_i"].reshape(lw, hd),
        jobs=[gather(["w_in"])])
    d_row = row(small["s5_d"])
    u_a, u_b, za, zb = carry(_inproj_fwd, x, row(small["g_mix"]), full["w_in"], row(small["b_in"]), widths,
                             jobs=[gather(["w_glu", "conv_w", "w_a_out", "w_b_out"])])
    conv_w = _unblock(full["conv_w"], 1) if distributed else full["conv_w"]
    sr, si, y, y_a = carry(_s5_fwd, u_a, bbr_blk, bbi_blk, ar_row, ai_row, cre_blk, cimn_blk, d_row, full["w_glu"],
                           row(small["b_glu"]), jobs=[gather(["w_ffn_gate", "w_o"])])
    xc, h, hprev = carry(_lru_fwd, u_b, conv_w, row(small["conv_b"]), wr_blk, row(small["b_r"]), wi_blk,
                         row(small["b_i"]), row(small["lru_lambda"]), jobs=[gather(["w_ffn_up"])])
    x1, merged, ma, mb = carry(_merge_fwd, y_a, h, za, zb, x, full["w_a_out"], full["w_b_out"], full["w_o"],
                               jobs=[gather(["w_ffn_down"])])
    fg, fu = carry(_ffn_up_fwd, x1, row(small["g_ffn"]), full["w_ffn_gate"], full["w_ffn_up"],
                   jobs=[gather(["w_ple_gate", "w_ple"])])
    x2 = _ffn_down_fwd(fg, fu, x1, full["w_ffn_down"])

    dx2, loss_blk, gw["w_ple_gate"], gw["w_ple"], vec_tail = _tail_fwd_bwd(
        x2, p, target, row(small["g_ple_gate"]), full["w_ple_gate"], row(small["b_ple_gate"]), full["w_ple"],
        row(small["g_ple"]), row(small["g_final"]))
    dfg, dfu, act = carry(_ffn_bwd_a, dx2, fg, fu, full["w_ffn_down"], jobs=[swap(["w_ple_gate", "w_ple"])])
    tn_d = min(d, 512)
    gw["w_ffn_down"] = _matmul_tn(act, dx2, tn_d, "dw_ffn_down", BF16)
    add_pairs(["w_ple_gate", "w_ple"])
    dx1, h2, vec_ffn = carry(_ffn_bwd_b, dfg, dfu, x1, dx2, row(small["g_ffn"]), full["w_ffn_gate"], full["w_ffn_up"],
                             jobs=[chips(["w_ple_gate", "w_ple"]), swap(["w_ffn_down"])])
    gw["w_ffn_gate"] = _matmul_tn(dfg, h2, tn_d, "dw_ffn_gate", BF16)
    gw["w_ffn_up"] = _matmul_tn(dfu, h2, tn_d, "dw_ffn_up", BF16)
    add_pairs(["w_ffn_down"])
    dza, dzb, dya, dyb, gw["w_o"], gw["w_a_out"], gw["w_b_out"] = carry(
        _merge_bwd, dx1, merged, ma, mb, za, zb, y_a, h, full["w_o"], full["w_a_out"], full["w_b_out"],
        jobs=[chips(["w_ffn_down"]), swap(["w_ffn_gate", "w_ffn_up"])])
    add_pairs(["w_ffn_gate", "w_ffn_up"])
    du_b, dw_r, dw_i, vec_lru = carry(
        _lru_bwd, dyb, xc, hprev, u_b, conv_w, wr_blk, row(small["b_r"]), wi_blk, row(small["b_i"]),
        row(small["lru_lambda"]), hd, jobs=[chips(["w_ffn_gate", "w_ffn_up"]), swap(["w_o", "w_a_out", "w_b_out"])])
    add_pairs(["w_o", "w_a_out", "w_b_out"])
    du_a, lam_r, lam_i, dy16, gw["w_glu"], vec_sa, vec_gn = carry(
        _s5_bwd, dya, y, sr, si, u_a, full["w_glu"], row(small["b_glu"]), cre_blk, cimn_blk, bbr_blk, bbi_blk, ar_row,
        ai_row, d_row, jobs=[chips(["w_o", "w_a_out", "w_b_out"])])
    smalls = {"vec_tail": vec_tail, "vec_ffn": vec_ffn, "vec_lru": vec_lru, "vec_sa": vec_sa, "vec_gn": vec_gn,
              "dw_r": dw_r, "dw_i": dw_i, "loss": loss_blk}
    everyones = {}

    def gather_smalls(keys):
        return (_Gather([smalls[k] for k in keys]), keys, everyones) if distributed else None

    smalls["dbb_re"], smalls["dbb_im"], smalls["dc_re"], smalls["dc_imn"] = carry(
        _s5_param_grads, lam_r, lam_i, sr, si, u_a, dy16, pch, n, jobs=[swap(["w_glu"]), gather_smalls(list(smalls))])
    add_pairs(["w_glu"])
    grad_x, h0, dz16, smalls["vec_mix"], smalls["vec_bin"] = _inproj_bwd(
        [du_a, du_b, dza, dzb], x, dx1, row(small["g_mix"]), full["w_in"])
    shapes = {k: a.shape for k, a in smalls.items()}
    gw["w_in"] = carry(_matmul_tn, dz16, h0, tn_d, "dw_in", BF16,
                       jobs=[chips(["w_glu"]), gather_smalls(["dbb_re", "dbb_im", "dc_re", "dc_imn"])])
    smalls.update(everyones)
    if distributed:
        got["w_in"] = gw["w_in"]
        gw = got
    return grad_x, gw, smalls, shapes


def _disc_inputs(small):
    g, n, pch = small["s5_b_re"].shape
    srcs = (small["lam_re"], small["lam_im"], jnp.repeat(small["log_dt"], n))
    return {"rows": [a.reshape(1, g * n) for a in srcs], "cols": [a.reshape(g * n, 1) for a in srcs],
            "b_re": small["s5_b_re"].reshape(g * n, pch), "b_im": small["s5_b_im"].reshape(g * n, pch)}


def kernel(x, p, g_mix, w_in, b_in, lam_re, lam_im, log_dt, s5_b_re, s5_b_im, s5_c_re, s5_c_im, s5_d, w_glu, b_glu, conv_w, conv_b, w_r, b_r, w_i, b_i, lru_lambda, w_a_out, w_b_out, w_o, g_ffn, w_ffn_gate, w_ffn_up, w_ffn_down, g_ple_gate, w_ple_gate, b_ple_gate, w_ple, g_ple, g_final, loss_target, m_g_mix, m_w_in, m_b_in, m_lam_re, m_lam_im, m_log_dt, m_s5_b_re, m_s5_b_im, m_s5_c_re, m_s5_c_im, m_s5_d, m_w_glu, m_b_glu, m_conv_w, m_conv_b, m_w_r, m_b_r, m_w_i, m_b_i, m_lru_lambda, m_w_a_out, m_w_b_out, m_w_o, m_g_ffn, m_w_ffn_gate, m_w_ffn_up, m_w_ffn_down, m_g_ple_gate, m_w_ple_gate, m_b_ple_gate, m_w_ple, m_g_ple, m_g_final, v_g_mix, v_w_in, v_b_in, v_lam_re, v_lam_im, v_log_dt, v_s5_b_re, v_s5_b_im, v_s5_c_re, v_s5_c_im, v_s5_d, v_w_glu, v_b_glu, v_conv_w, v_conv_b, v_w_r, v_b_r, v_w_i, v_b_i, v_lru_lambda, v_w_a_out, v_w_b_out, v_w_o, v_g_ffn, v_w_ffn_gate, v_w_ffn_up, v_w_ffn_down, v_g_ple_gate, v_w_ple_gate, v_b_ple_gate, v_w_ple, v_g_ple, v_g_final):
    given = dict(locals())
    wts = {k: given[k] for k in WEIGHTS}
    moms = {k: given["m_" + k] for k in WEIGHTS}
    vels = {k: given["v_" + k] for k in WEIGHTS}

    def drop_depth(k, a):
        return a if k == "g_final" else a[0]

    small = {k: drop_depth(k, wts[k]) for k in SMALL}
    shard = {k: wts[k][0] for k in SHARDED}
    names = list(SHARDED)

    def wire(k):
        if k == "conv_w":
            return shard[k]
        return (shard[k].T if k in TRANSPOSED else shard[k]).astype(BF16)

    disc = _disc_inputs(small)
    grad_x, parts, smalls, shapes = _local_step(x[0], p[0, 0], loss_target[0], {k: wire(k) for k in names}, small, disc)

    late = ["vec_mix", "vec_bin"]
    received = _alone(_Both(_PairSwap([parts["w_in"]]), _Gather([smalls[k] for k in late])), "swap_last")
    (pair_in,) = _pair_add([parts["w_in"]], received[:1], "pair_add_w_in")
    smalls.update(zip(late, received[1:]))

    g_small, (parts["w_in"],) = _small_reduce(smalls, shapes, *disc["cols"], disc["b_re"], disc["b_im"],
                                              small["s5_b_re"].shape[0], comm=_ChipExchange([pair_in]))
    loss = g_small.pop("loss")[0, 0]
    cols = shard["conv_w"].shape[1]
    mine = _index((lax.axis_index("x"), lax.axis_index("y"), lax.axis_index("c")))
    parts["conv_w"] = lax.dynamic_slice_in_dim(g_small.pop("conv_w"), mine * cols, cols, axis=1)[None]

    def view(k, a):
        a = a.reshape((1, -1) if k == "g_final" else wts[k].shape)
        return jnp.swapaxes(a, -1, -2) if k in NARROW_LAST else a

    def unview(k, a):
        return (jnp.swapaxes(a, -1, -2) if k in NARROW_LAST else a).reshape(wts[k].shape)

    slots = _adamw_small([view(k, wts[k]) for k in SMALL], [view(k, g_small[k]) for k in SMALL],
                         [view(k, moms[k]) for k in SMALL], [view(k, vels[k]) for k in SMALL])
    small_out = [dict(zip(SMALL, [unview(k, a) for k, a in zip(SMALL, slot)])) for slot in slots]

    big_out = {}
    for k in names:
        if k in LONG_AXIS_MINOR:
            res = _adamw(parts[k], shard[k].T, moms[k][0].T, vels[k][0].T, "adamw_" + k)
            big_out[k] = [a.T for a in res]
        else:
            big_out[k] = _adamw(parts[k], shard[k], moms[k][0], vels[k][0], "adamw_" + k, transposed=k in TRANSPOSED)

    outs = [loss, grad_x[None]]
    for slot in range(4):
        for k in WEIGHTS:
            if k in SHARDED:
                outs.append(big_out[k][slot][None])
            else:
                outs.append(small_out[slot][k])
    return tuple(outs)
```

```python
import math

import jax
import jax.numpy as jnp
from jax import lax
from jax.experimental import pallas as pl
from jax.experimental.pallas import tpu as pltpu

F32 = jnp.float32
BF16 = jnp.bfloat16

EPS = 1e-6
LRU_C = 8.0
CONV_WIDTH = 4
ADAM_LR = 0.001
ADAM_B1 = 0.9
ADAM_B2 = 0.999
ADAM_EPS = 1e-08
ADAM_WD = 0.01
ADAM_STEP = 10

N_DEV = 8
MESH = pl.DeviceIdType.MESH
SUBLANES = 8
LANES = 128
VMEM_LIMIT = 56 * 1024 * 1024
TOKEN_TILE = 256
TIME_CHUNK = 256
S5_SLAB = 128
LRU_SLAB = 256


def _dot(a, b):
    return jnp.dot(a.astype(BF16), b.astype(BF16), preferred_element_type=F32)


def _dot_nt(a, b):
    return lax.dot_general(a.astype(BF16), b.astype(BF16), (((1,), (1,)), ((), ())), preferred_element_type=F32)


def _dot_tn(a, b):
    return lax.dot_general(a.astype(BF16), b.astype(BF16), (((0,), (0,)), ((), ())), preferred_element_type=F32)


def _sigmoid(x):
    return jax.nn.sigmoid(x)


def _rms_stats(x):
    r = lax.rsqrt(jnp.mean(x * x, axis=-1, keepdims=True) + EPS)
    return x * r, r


def _rms_bwd(dy, xhat, r, g):
    dxn = dy * g
    dx = r * (dxn - xhat * jnp.mean(dxn * xhat, axis=-1, keepdims=True))
    return dx, dy * xhat


def _rowsum(v):
    return jnp.sum(v, axis=0, keepdims=True)


def _expm1(x):
    u = jnp.exp(x)
    um1 = u - 1.0
    safe = jnp.where(um1 == 0.0, 1.0, jnp.log(u))
    return jnp.where(um1 == 0.0, x, um1 * x / safe)


def _softplus(x):
    e = jnp.exp(-jnp.abs(x))
    u = 1.0 + e
    um1 = u - 1.0
    safe = jnp.where(um1 == 0.0, 1.0, um1)
    log1p_e = jnp.where(um1 == 0.0, e, jnp.log(u) * e / safe)
    return jnp.maximum(x, 0.0) + log1p_e


_GELU_K = math.sqrt(2.0 / math.pi)
_GELU_C = 0.044715


def _gelu(x):
    return 0.5 * x * (1.0 + jnp.tanh(_GELU_K * (x + _GELU_C * x * x * x)))


def _gelu_grad(x):
    th = jnp.tanh(_GELU_K * (x + _GELU_C * x * x * x))
    return 0.5 * (1.0 + th) + 0.5 * x * (1.0 - th * th) * _GELU_K * (1.0 + 3.0 * _GELU_C * x * x)


def _params(*sem):
    return pltpu.CompilerParams(dimension_semantics=sem, vmem_limit_bytes=VMEM_LIMIT)


def _rows(tm, n):
    return pl.BlockSpec((tm, n), lambda i: (i, 0))


def _rows_rev(tm, n, steps):
    return pl.BlockSpec((tm, n), lambda i: (steps - 1 - i, 0))


def _whole(shape):
    nd = len(shape)
    return pl.BlockSpec(shape, lambda i: (0,) * nd, pipeline_mode=pl.Buffered(1))


def _acc(shape):
    nd = len(shape)
    return pl.BlockSpec(shape, lambda i: (0,) * nd)


def _zero_on_first(*refs):
    @pl.when(pl.program_id(0) == 0)
    def _():
        for r in refs:
            r[...] = jnp.zeros_like(r)


def _inproj_fwd(x, g_mix, w_in_t, b_in, widths, comm=None):
    t, d = x.shape
    n = w_in_t.shape[0]
    tm = min(TOKEN_TILE, t)
    offs = [sum(widths[:i]) for i in range(len(widths) + 1)]

    def body(x_ref, g_ref, w_ref, b_ref, *outs):
        xhat, _ = _rms_stats(x_ref[...])
        h = (xhat * g_ref[...]).astype(BF16)
        for k, o_ref in enumerate(outs):
            lo, hi = offs[k], offs[k + 1]
            o_ref[...] = _dot_nt(h, w_ref[lo:hi, :]) + b_ref[:, lo:hi]

    return _run(
        body, comm, name="inproj_fwd", grid=(t // tm,),
        out_shape=[jax.ShapeDtypeStruct((t, w), F32) for w in widths],
        in_specs=[_rows(tm, d), _whole((1, d)), _whole((n, d)), _whole((1, n))],
        out_specs=[_rows(tm, w) for w in widths],
        semantics="parallel",
    )(x, g_mix, w_in_t, b_in)


def _s5_fwd(u, bbr_blk, bbi_blk, ar, ai, cre_blk, cimn_blk, d_skip, w_glu, b_glu, comm=None):
    t, sa = u.shape
    ns, _, sw = bbr_blk.shape
    gn = ns * sw
    tc = min(TIME_CHUNK, t)

    def body(u_ref, bbr_ref, bbi_ref, ar_ref, ai_ref, cre_ref, cim_ref, d_ref, wg_ref, bg_ref,
             sr_ref, si_ref, y_ref, ya_ref, cr_s, ci_s, sr_s, si_s):
        _zero_on_first(cr_s, ci_s)
        uv = u_ref[...]
        ub = uv.astype(BF16)
        for m in range(ns):
            um = ub[:, m * S5_SLAB:(m + 1) * S5_SLAB]
            sr_s[:, m * sw:(m + 1) * sw] = _dot(um, bbr_ref[m])
            si_s[:, m * sw:(m + 1) * sw] = _dot(um, bbi_ref[m])
        a_r = ar_ref[...]
        a_i = ai_ref[...]

        def step(row, carry):
            c_r, c_i = carry
            at = pl.ds(row, 1)
            n_r = a_r * c_r - a_i * c_i + sr_s[at, :]
            n_i = a_r * c_i + a_i * c_r + si_s[at, :]
            sr_s[at, :] = n_r
            si_s[at, :] = n_i
            return n_r, n_i

        c_r, c_i = lax.fori_loop(0, tc, step, (cr_s[0:1, :], ci_s[0:1, :]), unroll=8)
        cr_s[0:1, :] = c_r
        ci_s[0:1, :] = c_i
        for m in range(ns):
            states, chans = slice(m * sw, (m + 1) * sw), slice(m * S5_SLAB, (m + 1) * S5_SLAB)
            s_r, s_i = sr_s[:, states].astype(BF16), si_s[:, states].astype(BF16)
            sr_ref[:, states] = s_r.astype(sr_ref.dtype)
            si_ref[:, states] = s_i.astype(si_ref.dtype)
            y_ref[:, chans] = _dot(s_r, cre_ref[m]) + _dot(s_i, cim_ref[m]) + d_ref[:, chans] * uv[:, chans]
        y = y_ref[...]
        zz = _gelu(y)
        q = _dot(zz, wg_ref[...]) + bg_ref[...]
        ya_ref[...] = zz * _sigmoid(q)

    return _run(
        body, comm, name="s5_fwd", grid=(t // tc,),
        out_shape=[jax.ShapeDtypeStruct((t, gn), BF16), jax.ShapeDtypeStruct((t, gn), BF16),
                   jax.ShapeDtypeStruct((t, sa), F32), jax.ShapeDtypeStruct((t, sa), F32)],
        in_specs=[_rows(tc, sa), _whole(bbr_blk.shape), _whole(bbi_blk.shape), _whole((1, gn)), _whole((1, gn)),
                  _whole(cre_blk.shape), _whole(cimn_blk.shape), _whole((1, sa)), _whole((sa, sa)), _whole((1, sa))],
        out_specs=[_rows(tc, gn), _rows(tc, gn), _rows(tc, sa), _rows(tc, sa)],
        scratch_shapes=[pltpu.VMEM((SUBLANES, gn), F32), pltpu.VMEM((SUBLANES, gn), F32),
                        pltpu.VMEM((tc, gn), F32), pltpu.VMEM((tc, gn), F32)],
        semantics="arbitrary",
    )(u, bbr_blk, bbi_blk, ar, ai, cre_blk, cimn_blk, d_skip, w_glu, b_glu)


def _slab_dot(x, w_ref, transposed=False):
    dot = _dot_nt if transposed else _dot
    xb = x.astype(BF16)
    return jnp.concatenate([dot(xb[:, j * LRU_SLAB:(j + 1) * LRU_SLAB], w_ref[j]) for j in range(w_ref.shape[0])], axis=1)


def _lru_gates(xc, wr_ref, br_ref, wi_ref, bi_ref, lam_ref):
    r = _sigmoid(_slab_dot(xc, wr_ref) + br_ref[...])
    ig = _sigmoid(_slab_dot(xc, wi_ref) + bi_ref[...])
    sp = _softplus(-lam_ref[...])
    log_a = (-LRU_C * r) * sp
    return r, ig, sp, log_a


def _lru_fwd(u, conv_w, conv_b, wr_blk, b_r, wi_blk, b_i, lru_lambda, comm=None):
    t, w = u.shape
    tc = min(TIME_CHUNK, t)
    halo = SUBLANES

    def body(u_ref, cw_ref, cb_ref, wr_ref, br_ref, wi_ref, bi_ref, lam_ref,
             xc_ref, h_ref, hp_ref, ext_s, a_s, carry_s):
        @pl.when(pl.program_id(0) == 0)
        def _():
            ext_s[0:halo, :] = jnp.zeros((halo, w), F32)
            carry_s[...] = jnp.zeros_like(carry_s)

        ext_s[halo:halo + tc, :] = u_ref[...]
        xc = cb_ref[...]
        for k in range(CONV_WIDTH):
            off = halo - (CONV_WIDTH - 1) + k
            xc = xc + cw_ref[k:k + 1, :] * ext_s[off:off + tc, :]
        ext_s[0:halo, :] = ext_s[tc:tc + halo, :]
        xc_ref[...] = xc
        r, ig, sp, log_a = _lru_gates(xc, wr_ref, br_ref, wi_ref, bi_ref, lam_ref)
        a_s[...] = jnp.exp(log_a)
        h_ref[...] = jnp.sqrt(-_expm1(2.0 * log_a)) * ig * xc

        def step(row, carry):
            at = pl.ds(row, 1)
            hp_ref[at, :] = carry
            nxt = a_s[at, :] * carry + h_ref[at, :]
            h_ref[at, :] = nxt
            return nxt

        carry_s[0:1, :] = lax.fori_loop(0, tc, step, carry_s[0:1, :], unroll=8)

    return _run(
        body, comm, name="lru_fwd", grid=(t // tc,),
        out_shape=[jax.ShapeDtypeStruct((t, w), F32)] * 3,
        in_specs=[_rows(tc, w), _whole((CONV_WIDTH, w)), _whole((1, w)), _whole(wr_blk.shape), _whole((1, w)),
                  _whole(wi_blk.shape), _whole((1, w)), _whole((1, w))],
        out_specs=[_rows(tc, w)] * 3,
        scratch_shapes=[pltpu.VMEM((halo + tc, w), F32), pltpu.VMEM((tc, w), F32), pltpu.VMEM((SUBLANES, w), F32)],
        semantics="arbitrary",
    )(u, conv_w, conv_b, wr_blk, b_r, wi_blk, b_i, lru_lambda)


def _merge_fwd(y_a, h, za, zb, x, w_a_out_t, w_b_out, w_o, comm=None):
    t, d = x.shape
    sa, lw = y_a.shape[1], h.shape[1]
    tm = min(TOKEN_TILE, t)

    def body(ya_ref, h_ref, za_ref, zb_ref, x_ref, wa_ref, wb_ref, wo_ref, x1_ref, mg_ref, ma_ref, mb_ref):
        ma = _dot_nt(ya_ref[...], wa_ref[...])
        mb = _dot(h_ref[...], wb_ref[...])
        merged = _sigmoid(za_ref[...]) * ma + _sigmoid(zb_ref[...]) * mb
        ma_ref[...] = ma.astype(ma_ref.dtype)
        mb_ref[...] = mb.astype(mb_ref.dtype)
        mg_ref[...] = merged.astype(mg_ref.dtype)
        x1_ref[...] = x_ref[...] + _dot(merged, wo_ref[...])

    return _run(
        body, comm, name="merge_fwd", grid=(t // tm,),
        out_shape=[jax.ShapeDtypeStruct((t, d), F32), jax.ShapeDtypeStruct((t, d), BF16),
                   jax.ShapeDtypeStruct((t, d), BF16), jax.ShapeDtypeStruct((t, d), BF16)],
        in_specs=[_rows(tm, sa), _rows(tm, lw), _rows(tm, d), _rows(tm, d), _rows(tm, d),
                  _whole((d, sa)), _whole((lw, d)), _whole((d, d))],
        out_specs=[_rows(tm, d)] * 4,
        semantics="parallel",
    )(y_a, h, za, zb, x, w_a_out_t, w_b_out, w_o)


def _ffn_up_fwd(x1, g_ffn, w_gate_t, w_up_t, comm=None):
    t, d = x1.shape
    f = w_gate_t.shape[0]
    tm = min(TOKEN_TILE, t)

    def body(x_ref, g_ref, wg_ref, wu_ref, fg_ref, fu_ref):
        xhat, _ = _rms_stats(x_ref[...])
        h2 = (xhat * g_ref[...]).astype(BF16)
        fg_ref[...] = _dot_nt(h2, wg_ref[...]).astype(fg_ref.dtype)
        fu_ref[...] = _dot_nt(h2, wu_ref[...]).astype(fu_ref.dtype)

    return _run(
        body, comm, name="ffn_up_fwd", grid=(t // tm,),
        out_shape=[jax.ShapeDtypeStruct((t, f), BF16)] * 2,
        in_specs=[_rows(tm, d), _whole((1, d)), _whole((f, d)), _whole((f, d))],
        out_specs=[_rows(tm, f)] * 2,
        semantics="parallel",
    )(x1, g_ffn, w_gate_t, w_up_t)


def _ffn_down_fwd(fg, fu, x1, w_down, comm=None):
    t, d = x1.shape
    f = fg.shape[1]
    tm = min(TOKEN_TILE, t)

    def body(fg_ref, fu_ref, x_ref, wd_ref, x2_ref):
        fgv = fg_ref[...].astype(F32)
        act = fgv * _sigmoid(fgv) * fu_ref[...].astype(F32)
        x2_ref[...] = x_ref[...] + _dot(act, wd_ref[...])

    return _run(
        body, comm, name="ffn_down_fwd", grid=(t // tm,),
        out_shape=jax.ShapeDtypeStruct((t, d), F32),
        in_specs=[_rows(tm, f), _rows(tm, f), _rows(tm, d), _whole((f, d))],
        out_specs=_rows(tm, d),
        semantics="parallel",
    )(fg, fu, x1, w_down)


def _store_on_last(pairs):
    @pl.when(pl.program_id(0) == pl.num_programs(0) - 1)
    def _():
        for acc, out in pairs:
            out[...] = acc[...].astype(out.dtype)


def _tail_fwd_bwd(x2, p, target, g_pg, w_pg, b_pg, w_ple_t, g_ple, g_final):
    t, d = x2.shape
    pd = p.shape[1]
    tm = min(TOKEN_TILE, t)

    def body(x2_ref, p_ref, tg_ref, gpg_ref, wpg_ref, bpg_ref, wple_ref, gple_ref, gfin_ref,
             dx2_ref, loss_ref, dwpg_out, dwple_out, vec_ref, dwpg_ref, dwple_ref):
        _zero_on_first(loss_ref, dwpg_ref, dwple_ref, vec_ref)
        x2v = x2_ref[...]
        xh2, r2 = _rms_stats(x2v)
        h3 = xh2 * gpg_ref[...]
        gp = _sigmoid(_dot(h3, wpg_ref[...]) + bpg_ref[...])
        pe = _dot_nt(p_ref[...], wple_ref[...])
        peh, r3 = _rms_stats(pe)
        e = peh * gple_ref[...]
        x3 = x2v + gp * e
        xh3, r4 = _rms_stats(x3)
        diff = xh3 * gfin_ref[...] - tg_ref[...]
        loss_ref[...] += 0.5 * jnp.sum(jnp.mean(diff * diff, axis=-1, keepdims=True))
        dy = diff * (1.0 / d)
        dx3, dgfin = _rms_bwd(dy, xh3, r4, gfin_ref[...])
        d_gp = dx3 * e
        d_e = dx3 * gp
        dpe, dgple = _rms_bwd(d_e, peh, r3, gple_ref[...])
        dwple_ref[...] += _dot_tn(dpe, p_ref[...])
        dpre = d_gp * gp * (1.0 - gp)
        dwpg_ref[...] += _dot_tn(h3, dpre)
        dh3 = _dot_nt(dpre, wpg_ref[...])
        dx2n, dgpg = _rms_bwd(dh3, xh2, r2, gpg_ref[...])
        dx2_ref[...] = dx3 + dx2n
        vec_ref[0:1, :] += _rowsum(dpre)
        vec_ref[1:2, :] += _rowsum(dgpg)
        vec_ref[2:3, :] += _rowsum(dgple)
        vec_ref[3:4, :] += _rowsum(dgfin)
        _store_on_last([(dwpg_ref, dwpg_out), (dwple_ref, dwple_out)])

    return pl.pallas_call(
        body, name="tail_fwd_bwd", grid=(t // tm,),
        out_shape=[jax.ShapeDtypeStruct((t, d), F32), jax.ShapeDtypeStruct((SUBLANES, LANES), F32),
                   jax.ShapeDtypeStruct((d, d), BF16), jax.ShapeDtypeStruct((d, pd), BF16),
                   jax.ShapeDtypeStruct((SUBLANES, d), F32)],
        in_specs=[_rows(tm, d), _rows(tm, pd), _rows(tm, d), _whole((1, d)), _whole((d, d)), _whole((1, d)),
                  _whole((d, pd)), _whole((1, d)), _whole((1, d))],
        out_specs=[_rows(tm, d), _acc((SUBLANES, LANES)), _acc((d, d)), _acc((d, pd)), _acc((SUBLANES, d))],
        scratch_shapes=[pltpu.VMEM((d, d), F32), pltpu.VMEM((d, pd), F32)],
        compiler_params=_params("arbitrary"),
    )(x2, p, target, g_pg, w_pg, b_pg, w_ple_t, g_ple, g_final)


def _ffn_bwd_a(dx2, fg, fu, w_down, comm=None):
    t, d = dx2.shape
    f = fg.shape[1]
    tm = min(TOKEN_TILE, t)

    def body(dx_ref, fg_ref, fu_ref, wd_ref, dfg_ref, dfu_ref, act_ref):
        dact = _dot_nt(dx_ref[...], wd_ref[...])
        fgv = fg_ref[...].astype(F32)
        fuv = fu_ref[...].astype(F32)
        sg = _sigmoid(fgv)
        silu = fgv * sg
        dfu_ref[...] = (dact * silu).astype(dfu_ref.dtype)
        dfg_ref[...] = (dact * fuv * (sg * (1.0 + fgv * (1.0 - sg)))).astype(dfg_ref.dtype)
        act_ref[...] = (silu * fuv).astype(act_ref.dtype)

    return _run(
        body, comm, name="ffn_bwd_a", grid=(t // tm,),
        out_shape=[jax.ShapeDtypeStruct((t, f), BF16)] * 3,
        in_specs=[_rows(tm, d), _rows(tm, f), _rows(tm, f), _whole((f, d))],
        out_specs=[_rows(tm, f)] * 3,
        semantics="parallel",
    )(dx2, fg, fu, w_down)


def _ffn_bwd_b(dfg, dfu, x1, dx2, g_ffn, w_gate_t, w_up_t, comm=None):
    t, d = x1.shape
    f = dfg.shape[1]
    tm = min(TOKEN_TILE, t)

    def body(dfg_ref, dfu_ref, x_ref, dx2_ref, g_ref, wg_ref, wu_ref, dx1_ref, h2_ref, vec_ref):
        _zero_on_first(vec_ref)
        dh2 = _dot(dfg_ref[...], wg_ref[...]) + _dot(dfu_ref[...], wu_ref[...])
        xhat, r = _rms_stats(x_ref[...])
        h2_ref[...] = (xhat * g_ref[...]).astype(h2_ref.dtype)
        dxn, dg = _rms_bwd(dh2, xhat, r, g_ref[...])
        dx1_ref[...] = dx2_ref[...] + dxn
        vec_ref[0:1, :] += _rowsum(dg)

    return _run(
        body, comm, name="ffn_bwd_b", grid=(t // tm,),
        out_shape=[jax.ShapeDtypeStruct((t, d), F32), jax.ShapeDtypeStruct((t, d), BF16),
                   jax.ShapeDtypeStruct((SUBLANES, d), F32)],
        in_specs=[_rows(tm, f), _rows(tm, f), _rows(tm, d), _rows(tm, d), _whole((1, d)), _whole((f, d)), _whole((f, d))],
        out_specs=[_rows(tm, d), _rows(tm, d), _acc((SUBLANES, d))],
        semantics="arbitrary",
    )(dfg, dfu, x1, dx2, g_ffn, w_gate_t, w_up_t)


def _matmul_tn(a, b, tn, name, dtype=F32, comm=None):
    t, k = a.shape
    n = b.shape[1]

    def body(a_ref, b_ref, o_ref):
        o_ref[...] = _dot_tn(a_ref[...], b_ref[...]).astype(o_ref.dtype)

    return _run(
        body, comm, name=name, grid=(n // tn,),
        out_shape=jax.ShapeDtypeStruct((k, n), dtype),
        in_specs=[_whole((t, k)), pl.BlockSpec((t, tn), lambda j: (0, j))],
        out_specs=pl.BlockSpec((k, tn), lambda j: (0, j)),
        semantics="parallel",
    )(a, b)


def _merge_bwd(dx1, merged, ma, mb, za, zb, y_a, h, w_o, w_a_out_t, w_b_out, comm=None):
    t, d = dx1.shape
    sa, lw = y_a.shape[1], h.shape[1]
    tm = min(TOKEN_TILE, t)

    def body(dx1_ref, mg_ref, ma_ref, mb_ref, za_ref, zb_ref, ya_ref, h_ref, wo_ref, wa_ref, wb_ref,
             dza_ref, dzb_ref, dya_ref, dyb_ref, dwo_out, dwa_out, dwb_out, dwo_ref, dwa_ref, dwb_ref):
        _zero_on_first(dwo_ref, dwa_ref, dwb_ref)
        dx1v = dx1_ref[...].astype(BF16)
        dmg = _dot_nt(dx1v, wo_ref[...])
        ga = _sigmoid(za_ref[...])
        gb = _sigmoid(zb_ref[...])
        dza_ref[...] = dmg * ma_ref[...].astype(F32) * ga * (1.0 - ga)
        dzb_ref[...] = dmg * mb_ref[...].astype(F32) * gb * (1.0 - gb)
        dma = (dmg * ga).astype(BF16)
        dmb = (dmg * gb).astype(BF16)
        dya_ref[...] = _dot(dma, wa_ref[...])
        dyb_ref[...] = _dot_nt(dmb, wb_ref[...])
        dwo_ref[...] += _dot_tn(mg_ref[...], dx1v)
        dwa_ref[...] += _dot_tn(dma, ya_ref[...])
        dwb_ref[...] += _dot_tn(h_ref[...], dmb)
        _store_on_last([(dwo_ref, dwo_out), (dwa_ref, dwa_out), (dwb_ref, dwb_out)])

    return _run(
        body, comm, name="merge_bwd", grid=(t // tm,),
        out_shape=[jax.ShapeDtypeStruct((t, d), F32), jax.ShapeDtypeStruct((t, d), F32),
                   jax.ShapeDtypeStruct((t, sa), F32), jax.ShapeDtypeStruct((t, lw), F32),
                   jax.ShapeDtypeStruct((d, d), BF16), jax.ShapeDtypeStruct((d, sa), BF16),
                   jax.ShapeDtypeStruct((lw, d), BF16)],
        in_specs=[_rows(tm, d), _rows(tm, d), _rows(tm, d), _rows(tm, d), _rows(tm, d), _rows(tm, d),
                  _rows(tm, sa), _rows(tm, lw), _whole((d, d)), _whole((d, sa)), _whole((lw, d))],
        out_specs=[_rows(tm, d), _rows(tm, d), _rows(tm, sa), _rows(tm, lw), _acc((d, d)), _acc((d, sa)), _acc((lw, d))],
        scratch_shapes=[pltpu.VMEM((d, d), F32), pltpu.VMEM((d, sa), F32), pltpu.VMEM((lw, d), F32)],
        semantics="arbitrary",
    )(dx1, merged, ma, mb, za, zb, y_a, h, w_o, w_a_out_t, w_b_out)


def _fold_diag_blocks(dense, row_group, col_group, row0=0, col0=0):
    r, c = dense.shape
    rows = lax.broadcasted_iota(jnp.int32, (r, c), 0) + row0
    cols = lax.broadcasted_iota(jnp.int32, (r, c), 1) + col0
    kept = jnp.where(rows // row_group == cols // col_group, dense, 0.0)
    pick = (lax.broadcasted_iota(jnp.int32, (row_group, r), 0)
            == lax.broadcasted_iota(jnp.int32, (row_group, r), 1) % row_group).astype(F32)
    return jnp.dot(pick, kept, preferred_element_type=F32, precision=lax.Precision.HIGHEST)


def _lru_bwd(dh, xc, hprev, u, conv_w, wr_blk, b_r, wi_blk, b_i, lru_lambda, head_dim, comm=None):
    t, w = dh.shape
    tc = min(TIME_CHUNK, t)
    steps = t // tc
    halo = SUBLANES
    sub_per_chunk = tc // halo
    slabs = w // LRU_SLAB

    def body(dh_ref, xc_ref, hp_ref, u_ref, uh_ref, cw_ref, wr_ref, br_ref, wi_ref, bi_ref, lam_ref,
             du_ref, dwr_out, dwi_out, vec_ref, lam_s, a_s, dxc_s, uext_s, carry_s, dwr_ref, dwi_ref):
        chunk = steps - 1 - pl.program_id(0)

        @pl.when(pl.program_id(0) == 0)
        def _():
            carry_s[...] = jnp.zeros_like(carry_s)
            dxc_s[tc:tc + halo, :] = jnp.zeros((halo, w), F32)
            dwr_ref[...] = jnp.zeros_like(dwr_ref)
            dwi_ref[...] = jnp.zeros_like(dwi_ref)
            vec_ref[...] = jnp.zeros_like(vec_ref)

        xc = xc_ref[...]
        r, ig, sp, log_a = _lru_gates(xc, wr_ref, br_ref, wi_ref, bi_ref, lam_ref)
        a = jnp.exp(log_a)
        a_s[...] = a

        def step(i, q):
            at = pl.ds(tc - 1 - i, 1)
            lam_row = dh_ref[at, :] + q
            lam_s[at, :] = lam_row
            return a_s[at, :] * lam_row

        carry_s[0:1, :] = lax.fori_loop(0, tc, step, carry_s[0:1, :], unroll=8)
        lam = lam_s[...]
        mult = jnp.sqrt(-_expm1(2.0 * log_a))
        d_log_a = lam * hp_ref[...] * a - (lam * ig * xc) * (a * a) / mult
        d_ig = lam * mult * xc
        dpre_r = (d_log_a * (-LRU_C * sp)) * r * (1.0 - r)
        dpre_i = d_ig * ig * (1.0 - ig)
        dxc = lam * mult * ig + _slab_dot(dpre_r, wr_ref, transposed=True) + _slab_dot(dpre_i, wi_ref, transposed=True)
        xcb, drb, dib = xc.astype(BF16), dpre_r.astype(BF16), dpre_i.astype(BF16)
        for j in range(slabs):
            cols = slice(j * LRU_SLAB, (j + 1) * LRU_SLAB)
            dwr_ref[j] += _dot_tn(drb[:, cols], xcb[:, cols])
            dwi_ref[j] += _dot_tn(dib[:, cols], xcb[:, cols])
        vec_ref[0:1, :] += _rowsum(dxc)
        vec_ref[1:2, :] += _rowsum(dpre_r)
        vec_ref[2:3, :] += _rowsum(dpre_i)
        vec_ref[3:4, :] += _rowsum(d_log_a * (-LRU_C * r)) * (-_sigmoid(-lam_ref[...]))
        dxc_s[0:tc, :] = dxc
        du = cw_ref[CONV_WIDTH - 1:CONV_WIDTH, :] * dxc
        for k in range(CONV_WIDTH - 1):
            off = CONV_WIDTH - 1 - k
            du = du + cw_ref[k:k + 1, :] * dxc_s[off:off + tc, :]
        du_ref[...] = du
        dxc_s[tc:tc + halo, :] = dxc_s[0:halo, :]
        uext_s[0:halo, :] = jnp.where(chunk > 0, uh_ref[...], 0.0)
        uext_s[halo:halo + tc, :] = u_ref[...]
        for k in range(CONV_WIDTH):
            off = halo - (CONV_WIDTH - 1) + k
            vec_ref[4 + k:5 + k, :] += _rowsum(dxc * uext_s[off:off + tc, :])

        @pl.when(pl.program_id(0) == steps - 1)
        def _():
            for j in range(slabs):
                cols = slice(j * LRU_SLAB, (j + 1) * LRU_SLAB)
                dwr_out[:, cols] = _fold_diag_blocks(dwr_ref[j], head_dim, head_dim)
                dwi_out[:, cols] = _fold_diag_blocks(dwi_ref[j], head_dim, head_dim)

    halo_spec = pl.BlockSpec((halo, w), lambda i: (jnp.maximum((steps - 1 - i) * sub_per_chunk - 1, 0), 0))
    return _run(
        body, comm, name="lru_bwd", grid=(steps,),
        out_shape=[jax.ShapeDtypeStruct((t, w), F32), jax.ShapeDtypeStruct((head_dim, w), F32),
                   jax.ShapeDtypeStruct((head_dim, w), F32), jax.ShapeDtypeStruct((SUBLANES, w), F32)],
        in_specs=[_rows_rev(tc, w, steps)] * 4 + [halo_spec, _whole((CONV_WIDTH, w)), _whole(wr_blk.shape),
                                                  _whole((1, w)), _whole(wi_blk.shape), _whole((1, w)), _whole((1, w))],
        out_specs=[_rows_rev(tc, w, steps), _acc((head_dim, w)), _acc((head_dim, w)), _acc((SUBLANES, w))],
        scratch_shapes=[pltpu.VMEM((tc, w), F32), pltpu.VMEM((tc, w), F32), pltpu.VMEM((tc + halo, w), F32),
                        pltpu.VMEM((halo + tc, w), F32), pltpu.VMEM((SUBLANES, w), F32),
                        pltpu.VMEM((slabs, LRU_SLAB, LRU_SLAB), F32), pltpu.VMEM((slabs, LRU_SLAB, LRU_SLAB), F32)],
        semantics="arbitrary",
    )(dh, xc, hprev, u, u, conv_w, wr_blk, b_r, wi_blk, b_i, lru_lambda)


def _s5_bwd(dya, y, sr, si, u, w_glu, b_glu, cre_blk, cimn_blk, bbr_blk, bbi_blk, ar, ai, d_skip, comm=None):
    t, sa = dya.shape
    gn = sr.shape[1]
    ns, _, sw = bbr_blk.shape
    tc = min(TIME_CHUNK, t)
    steps = t // tc
    halo = SUBLANES

    def body(dya_ref, y_ref, sr_ref, si_ref, u_ref, wg_ref, bg_ref, cre_ref, cim_ref, bbr_ref, bbi_ref,
             ar_ref, ai_ref, d_ref, du_ref, lr_ref, li_ref, dy_ref, dwg_out, vsa_ref, vgn_ref, gr_s, gi_s, cr_s, ci_s,
             dwg_ref):
        @pl.when(pl.program_id(0) == 0)
        def _():
            cr_s[...] = jnp.zeros_like(cr_s)
            ci_s[...] = jnp.zeros_like(ci_s)
            gr_s[tc:tc + halo, :] = jnp.zeros((halo, gn), F32)
            gi_s[tc:tc + halo, :] = jnp.zeros((halo, gn), F32)
            dwg_ref[...] = jnp.zeros_like(dwg_ref)
            vsa_ref[...] = jnp.zeros_like(vsa_ref)
            vgn_ref[...] = jnp.zeros_like(vgn_ref)

        yv = y_ref[...]
        uv = u_ref[...]
        zz = _gelu(yv)
        sg = _sigmoid(_dot(zz, wg_ref[...]) + bg_ref[...])
        dyav = dya_ref[...]
        dq = dyav * zz * sg * (1.0 - sg)
        dzz = dyav * sg + _dot_nt(dq, wg_ref[...])
        dwg_ref[...] += _dot_tn(zz, dq)
        dy = dzz * _gelu_grad(yv)
        dyb = dy.astype(BF16)
        dy_ref[...] = dyb.astype(dy_ref.dtype)
        vsa_ref[0:1, :] += _rowsum(dq)
        vsa_ref[1:2, :] += _rowsum(dy * uv)
        for m in range(ns):
            dym = dyb[:, m * S5_SLAB:(m + 1) * S5_SLAB]
            gr_s[0:tc, m * sw:(m + 1) * sw] = _dot_nt(dym, cre_ref[m])
            gi_s[0:tc, m * sw:(m + 1) * sw] = _dot_nt(dym, cim_ref[m])
        a_r = ar_ref[...]
        a_i = ai_ref[...]

        def step(i, carry):
            l_r, l_i = carry
            at = pl.ds(tc - 1 - i, 1)
            n_r = gr_s[at, :] + a_r * l_r + a_i * l_i
            n_i = gi_s[at, :] + a_r * l_i - a_i * l_r
            gr_s[at, :] = n_r
            gi_s[at, :] = n_i
            return n_r, n_i

        l_r, l_i = lax.fori_loop(0, tc, step, (cr_s[0:1, :], ci_s[0:1, :]), unroll=8)
        cr_s[0:1, :] = l_r
        ci_s[0:1, :] = l_i
        nxt_r = gr_s[1:tc + 1, :]
        nxt_i = gi_s[1:tc + 1, :]
        srv = sr_ref[...].astype(F32)
        siv = si_ref[...].astype(F32)
        vgn_ref[0:1, :] += _rowsum(nxt_r * srv + nxt_i * siv)
        vgn_ref[1:2, :] += _rowsum(nxt_i * srv - nxt_r * siv)
        lam_r = gr_s[0:tc, :]
        lam_i = gi_s[0:tc, :]
        gr_s[tc:tc + halo, :] = gr_s[0:halo, :]
        gi_s[tc:tc + halo, :] = gi_s[0:halo, :]
        lrb = lam_r.astype(BF16)
        lib = lam_i.astype(BF16)
        lr_ref[...] = lrb.astype(lr_ref.dtype)
        li_ref[...] = lib.astype(li_ref.dtype)
        for m in range(ns):
            states, chans = slice(m * sw, (m + 1) * sw), slice(m * S5_SLAB, (m + 1) * S5_SLAB)
            du_ref[:, chans] = (_dot_nt(lrb[:, states], bbr_ref[m]) + _dot_nt(lib[:, states], bbi_ref[m])
                                + dy[:, chans] * d_ref[:, chans])
        _store_on_last([(dwg_ref, dwg_out)])

    return _run(
        body, comm, name="s5_bwd", grid=(steps,),
        out_shape=[jax.ShapeDtypeStruct((t, sa), F32), jax.ShapeDtypeStruct((t, gn), BF16),
                   jax.ShapeDtypeStruct((t, gn), BF16), jax.ShapeDtypeStruct((t, sa), BF16),
                   jax.ShapeDtypeStruct((sa, sa), BF16), jax.ShapeDtypeStruct((SUBLANES, sa), F32),
                   jax.ShapeDtypeStruct((SUBLANES, gn), F32)],
        in_specs=[_rows_rev(tc, sa, steps), _rows_rev(tc, sa, steps), _rows_rev(tc, gn, steps), _rows_rev(tc, gn, steps),
                  _rows_rev(tc, sa, steps), _whole((sa, sa)), _whole((1, sa)), _whole(cre_blk.shape),
                  _whole(cimn_blk.shape), _whole(bbr_blk.shape), _whole(bbi_blk.shape), _whole((1, gn)), _whole((1, gn)),
                  _whole((1, sa))],
        out_specs=[_rows_rev(tc, sa, steps), _rows_rev(tc, gn, steps), _rows_rev(tc, gn, steps), _rows_rev(tc, sa, steps),
                   _acc((sa, sa)), _acc((SUBLANES, sa)), _acc((SUBLANES, gn))],
        scratch_shapes=[pltpu.VMEM((tc + halo, gn), F32), pltpu.VMEM((tc + halo, gn), F32),
                        pltpu.VMEM((SUBLANES, gn), F32), pltpu.VMEM((SUBLANES, gn), F32), pltpu.VMEM((sa, sa), F32)],
        semantics="arbitrary",
    )(dya, y, sr, si, u, w_glu, b_glu, cre_blk, cimn_blk, bbr_blk, bbi_blk, ar, ai, d_skip)


def _inproj_bwd(dparts, x, dx1, g_mix, w_in_t, comm=None):
    t, d = x.shape
    n = w_in_t.shape[0]
    widths = [p.shape[1] for p in dparts]
    offs = [sum(widths[:i]) for i in range(len(widths) + 1)]
    tm = min(TOKEN_TILE, t)
    np_ = len(dparts)

    def body(*refs):
        dz_refs = refs[:np_]
        x_ref, dx1_ref, g_ref, w_ref, gx_ref, h_ref, dz_ref, vd_ref, vn_ref = refs[np_:]
        _zero_on_first(vd_ref, vn_ref)
        dh = jnp.zeros((tm, d), F32)
        for k, r in enumerate(dz_refs):
            lo, hi = offs[k], offs[k + 1]
            dzk = r[...]
            dh = dh + _dot(dzk, w_ref[lo:hi, :])
            dz_ref[:, lo:hi] = dzk.astype(dz_ref.dtype)
            vn_ref[0:1, lo:hi] += _rowsum(dzk)
        xhat, r0 = _rms_stats(x_ref[...])
        h_ref[...] = (xhat * g_ref[...]).astype(h_ref.dtype)
        dxn, dg = _rms_bwd(dh, xhat, r0, g_ref[...])
        gx_ref[...] = dx1_ref[...] + dxn
        vd_ref[0:1, :] += _rowsum(dg)

    return _run(
        body, comm, name="inproj_bwd", grid=(t // tm,),
        out_shape=[jax.ShapeDtypeStruct((t, d), F32), jax.ShapeDtypeStruct((t, d), BF16),
                   jax.ShapeDtypeStruct((t, n), BF16), jax.ShapeDtypeStruct((SUBLANES, d), F32),
                   jax.ShapeDtypeStruct((SUBLANES, n), F32)],
        in_specs=[_rows(tm, w) for w in widths] + [_rows(tm, d), _rows(tm, d), _whole((1, d)), _whole((n, d))],
        out_specs=[_rows(tm, d), _rows(tm, d), _rows(tm, n), _acc((SUBLANES, d)), _acc((SUBLANES, n))],
        semantics="arbitrary",
    )(*dparts, x, dx1, g_mix, w_in_t)


def _prep(lr_row, li_row, ldt_row, lr_col, li_col, ldt_col, b_re, b_im, c_re, c_im, w_r, w_i, comm=None):
    gn, pch = b_re.shape
    sa, n = c_re.shape
    w, hd = w_r.shape
    ns, sw, lsl = sa // S5_SLAB, S5_SLAB * n // pch, w // LRU_SLAB

    def spread_cols(vals, row_group, col_group, width):
        r, k = vals.shape
        tile = (lax.broadcasted_iota(jnp.int32, (k, width), 0) == lax.broadcasted_iota(jnp.int32, (k, width), 1) % k)
        rows = lax.broadcasted_iota(jnp.int32, (r, width), 0) // row_group
        cols = lax.broadcasted_iota(jnp.int32, (r, width), 1) // col_group
        return jnp.where(rows == cols, _dot(vals, tile.astype(BF16)), 0.0)

    def spread_rows(vals, row_group, col_group, height):
        k, c = vals.shape
        tile = (lax.broadcasted_iota(jnp.int32, (height, k), 0) % k == lax.broadcasted_iota(jnp.int32, (height, k), 1))
        rows = lax.broadcasted_iota(jnp.int32, (height, c), 0) // row_group
        cols = lax.broadcasted_iota(jnp.int32, (height, c), 1) // col_group
        return jnp.where(rows == cols, _dot(tile.astype(BF16), vals), 0.0)

    def body(lrr, lir, ldr, lrc, lic, ldc, bre, bim, cre, cim, wr, wi,
             ar_o, ai_o, bbr_o, bbi_o, cre_o, cim_o, wr_o, wi_o):
        ar, ai, _, _ = _disc_scalars(lrr[...], lir[...], ldr[...])
        ar_o[...] = ar
        ai_o[...] = ai
        _, _, bbr, bbi = _disc_cols(lrc[...], lic[...], ldc[...], bre[...], bim[...])
        bbr_t, bbi_t = bbr.T, bbi.T
        for m in range(ns):
            bbr_o[m] = spread_rows(bbr_t[:, m * sw:(m + 1) * sw], pch, n, S5_SLAB).astype(bbr_o.dtype)
            bbi_o[m] = spread_rows(bbi_t[:, m * sw:(m + 1) * sw], pch, n, S5_SLAB).astype(bbi_o.dtype)
            rows = slice(m * S5_SLAB, (m + 1) * S5_SLAB)
            cre_o[m] = spread_rows(cre[rows, :].T, n, pch, sw).astype(cre_o.dtype)
            cim_o[m] = spread_rows(-cim[rows, :].T, n, pch, sw).astype(cim_o.dtype)
        for j in range(lsl):
            rows = slice(j * LRU_SLAB, (j + 1) * LRU_SLAB)
            wr_o[j] = spread_cols(wr[rows, :], hd, hd, LRU_SLAB).astype(wr_o.dtype)
            wi_o[j] = spread_cols(wi[rows, :], hd, hd, LRU_SLAB).astype(wi_o.dtype)

    args = (lr_row, li_row, ldt_row, lr_col, li_col, ldt_col, b_re, b_im, c_re, c_im, w_r, w_i)
    out_shape = [jax.ShapeDtypeStruct((1, gn), F32), jax.ShapeDtypeStruct((1, gn), F32),
                 jax.ShapeDtypeStruct((ns, S5_SLAB, sw), BF16), jax.ShapeDtypeStruct((ns, S5_SLAB, sw), BF16),
                 jax.ShapeDtypeStruct((ns, sw, S5_SLAB), BF16), jax.ShapeDtypeStruct((ns, sw, S5_SLAB), BF16),
                 jax.ShapeDtypeStruct((lsl, LRU_SLAB, LRU_SLAB), BF16),
                 jax.ShapeDtypeStruct((lsl, LRU_SLAB, LRU_SLAB), BF16)]
    return _run(
        body, comm, name="prep", grid=(1,), out_shape=out_shape, in_specs=[_whole(a.shape) for a in args],
        out_specs=[_acc(s.shape) for s in out_shape], semantics="arbitrary",
    )(*args)


def _s5_param_grads(lam_r, lam_i, sr, si, u, dy, pch, n, comm=None):
    t, gn = lam_r.shape
    sa = u.shape[1]
    sw = S5_SLAB * n // pch

    def body(lr_ref, li_ref, sr_ref, si_ref, u_ref, dy_ref, dbr_ref, dbi_ref, dcr_ref, dci_ref):
        uv = u_ref[...]
        dyv = dy_ref[...]
        dbr_ref[...] = _fold_diag_blocks(_dot_tn(uv, lr_ref[...]), pch, n)
        dbi_ref[...] = _fold_diag_blocks(_dot_tn(uv, li_ref[...]), pch, n)
        dcr_ref[...] = _fold_diag_blocks(_dot_tn(sr_ref[...], dyv), n, pch)
        dci_ref[...] = _fold_diag_blocks(_dot_tn(si_ref[...], dyv), n, pch)

    states = pl.BlockSpec((t, sw), lambda m: (0, m))
    chans = pl.BlockSpec((t, S5_SLAB), lambda m: (0, m))
    return _run(
        body, comm, name="s5_param_grads", grid=(sa // S5_SLAB,),
        out_shape=[jax.ShapeDtypeStruct((pch, gn), F32), jax.ShapeDtypeStruct((pch, gn), F32),
                   jax.ShapeDtypeStruct((n, sa), F32), jax.ShapeDtypeStruct((n, sa), F32)],
        in_specs=[states, states, states, states, chans, chans],
        out_specs=[pl.BlockSpec((pch, sw), lambda m: (0, m)), pl.BlockSpec((pch, sw), lambda m: (0, m)),
                   pl.BlockSpec((n, S5_SLAB), lambda m: (0, m)), pl.BlockSpec((n, S5_SLAB), lambda m: (0, m))],
        semantics="parallel",
    )(lam_r, lam_i, sr, si, u, dy)


SMALL_PARTS = ["vec_tail", "vec_ffn", "vec_lru", "vec_mix", "vec_bin", "vec_sa", "vec_gn", "dw_r", "dw_i", "dbb_re",
               "dbb_im", "dc_re", "dc_imn", "loss"]


def _small_reduce(parts, shapes, lr_col, li_col, ldt_col, b_re, b_im, groups, comm=None):
    gn, pch = b_re.shape
    n = gn // groups
    nparts = parts[SMALL_PARTS[0]].size // math.prod(shapes[SMALL_PARTS[0]])
    np_, nout = len(SMALL_PARTS), 24

    def body(*refs):
        ins = refs[:np_]
        lr, li, ldt, bre, bim = refs[np_:np_ + 5]
        outs = refs[np_ + 5:np_ + 5 + nout]
        sums = dict(zip(SMALL_PARTS, refs[np_ + 5 + nout:]))

        @pl.when(pl.program_id(0) == 0)
        def _():
            for k, r in zip(SMALL_PARTS, ins):
                sums[k][...] = r[...]

        @pl.when(pl.program_id(0) > 0)
        def _():
            for k, r in zip(SMALL_PARTS, ins):
                sums[k][...] += r[...]

        @pl.when(pl.program_id(0) == nparts - 1)
        def _():
            finish({k: s[...] for k, s in sums.items()}, lr, li, ldt, bre, bim, *outs)

    def finish(tot, lr, li, ldt, bre, bim, o_loss, o_gmix, o_bin, o_bglu, o_s5d, o_convb, o_br, o_bi, o_lam, o_gffn,
               o_gpg, o_bpg, o_gple, o_gfin, o_wr, o_wi, o_cre, o_cim, o_bre, o_bim, o_lre, o_lim, o_ldt, o_convw):
        o_loss[...] = tot["loss"]
        o_convw[...] = tot["vec_lru"][SUBLANES - CONV_WIDTH:SUBLANES]
        o_bpg[...] = tot["vec_tail"][0:1]
        o_gpg[...] = tot["vec_tail"][1:2]
        o_gple[...] = tot["vec_tail"][2:3]
        o_gfin[...] = tot["vec_tail"][3:4]
        o_gffn[...] = tot["vec_ffn"][0:1]
        o_convb[...] = tot["vec_lru"][0:1]
        o_br[...] = tot["vec_lru"][1:2]
        o_bi[...] = tot["vec_lru"][2:3]
        o_lam[...] = tot["vec_lru"][3:4]
        o_gmix[...] = tot["vec_mix"][0:1]
        o_bin[...] = tot["vec_bin"][0:1]
        o_bglu[...] = tot["vec_sa"][0:1]
        o_s5d[...] = tot["vec_sa"][1:2]
        o_wr[...] = tot["dw_r"].T
        o_wi[...] = tot["dw_i"].T
        o_cre[...] = tot["dc_re"].T
        o_cim[...] = -tot["dc_imn"].T
        d_a = tot["vec_gn"].T
        _, chain = jax.vjp(_disc_cols, lr[...], li[...], ldt[...], bre[...], bim[...])
        d_lr, d_li, d_ldt, d_bre, d_bim = chain((d_a[:, 0:1], d_a[:, 1:2], tot["dbb_re"].T, tot["dbb_im"].T))
        o_lre[...] = d_lr
        o_lim[...] = d_li
        o_bre[...] = d_bre
        o_bim[...] = d_bim
        same = (lax.broadcasted_iota(jnp.int32, (groups, gn), 0)
                == lax.broadcasted_iota(jnp.int32, (groups, gn), 1) // n).astype(F32)
        o_ldt[...] = jnp.dot(same, d_ldt * jnp.ones((1, LANES), F32), preferred_element_type=F32,
                             precision=lax.Precision.HIGHEST)[:, 0:1]

    d = shapes["vec_mix"][1]
    nz = shapes["vec_bin"][1]
    sa = shapes["vec_sa"][1]
    w = shapes["vec_lru"][1]
    row = lambda c: jax.ShapeDtypeStruct((1, c), F32)
    out_shape = [jax.ShapeDtypeStruct(shapes["loss"], F32), row(d), row(nz), row(sa), row(sa), row(w), row(w), row(w),
                 row(w), row(d), row(d), row(d), row(d), row(d),
                 jax.ShapeDtypeStruct(shapes["dw_r"][::-1], F32), jax.ShapeDtypeStruct(shapes["dw_i"][::-1], F32),
                 jax.ShapeDtypeStruct(shapes["dc_re"][::-1], F32), jax.ShapeDtypeStruct(shapes["dc_imn"][::-1], F32),
                 jax.ShapeDtypeStruct((gn, pch), F32), jax.ShapeDtypeStruct((gn, pch), F32),
                 jax.ShapeDtypeStruct((gn, 1), F32), jax.ShapeDtypeStruct((gn, 1), F32),
                 jax.ShapeDtypeStruct((groups, 1), F32), jax.ShapeDtypeStruct((CONV_WIDTH, w), F32)]
    def part_spec(k):
        r, c = shapes[k]
        if parts[k].ndim == 3:
            return pl.BlockSpec((None, r, c), lambda i: (i, 0, 0))
        return pl.BlockSpec((r, c), lambda i: (i, 0))

    outs = _run(
        body, comm, name="small_reduce", grid=(nparts,), out_shape=out_shape,
        in_specs=[part_spec(k) for k in SMALL_PARTS] + [_whole(a.shape) for a in (lr_col, li_col, ldt_col, b_re, b_im)],
        out_specs=[_acc(s.shape) for s in out_shape],
        scratch_shapes=[pltpu.VMEM(shapes[k], F32) for k in SMALL_PARTS],
        semantics="arbitrary",
    )(*[parts[k] for k in SMALL_PARTS], lr_col, li_col, ldt_col, b_re, b_im)
    extra = None
    if comm is not None:
        outs, extra = outs
    names = ["loss", "g_mix", "b_in", "b_glu", "s5_d", "conv_b", "b_r", "b_i", "lru_lambda", "g_ffn", "g_ple_gate",
             "b_ple_gate", "g_ple", "g_final", "w_r", "w_i", "s5_c_re", "s5_c_im", "s5_b_re", "s5_b_im", "lam_re",
             "lam_im", "log_dt", "conv_w"]
    res = dict(zip(names, outs))
    return res if comm is None else (res, extra)


def _adamw_small(ws, gs, ms, vs):
    n = len(ws)

    def body(*refs):
        w_r, g_r, m_r, v_r = (refs[i * n:(i + 1) * n] for i in range(4))
        g_o, d_o, m_o, v_o = (refs[(4 + i) * n:(5 + i) * n] for i in range(4))
        for i in range(n):
            g = g_r[i][...]
            delta, m_new, v_new = _adamw_math(w_r[i][...], g, m_r[i][...], v_r[i][...])
            g_o[i][...] = g
            d_o[i][...] = delta
            m_o[i][...] = m_new
            v_o[i][...] = v_new

    shapes = [jax.ShapeDtypeStruct(a.shape, F32) for a in ws]
    outs = pl.pallas_call(body, name="adamw_small", out_shape=shapes * 4)(*ws, *gs, *ms, *vs)
    return outs[:n], outs[n:2 * n], outs[2 * n:3 * n], outs[3 * n:]


def _adamw_math(w, g, m, v):
    m_new = ADAM_B1 * m + (1.0 - ADAM_B1) * g
    v_new = ADAM_B2 * v + (1.0 - ADAM_B2) * (g * g)
    m_hat = m_new / (1.0 - ADAM_B1 ** ADAM_STEP)
    v_hat = v_new / (1.0 - ADAM_B2 ** ADAM_STEP)
    delta = -ADAM_LR * (m_hat / (jnp.sqrt(v_hat) + ADAM_EPS) + ADAM_WD * w)
    return delta, m_new, v_new


def _row_tile(rows):
    for cand in range(256, 0, -16):
        if rows % cand == 0:
            return cand
    return rows


def _adamw(parts, w, m, v, name, transposed=False):
    rows, cols = w.shape
    npart = parts.shape[0]
    tr = _row_tile(rows)
    if transposed:
        parts_spec = pl.BlockSpec((npart, cols, tr), lambda i: (0, 0, i))
    else:
        parts_spec = pl.BlockSpec((npart, tr, cols), lambda i: (0, i, 0))

    def body(p_ref, w_ref, m_ref, v_ref, g_ref, d_ref, mo_ref, vo_ref):
        g = p_ref[0].astype(F32)
        for k in range(1, npart):
            g = g + p_ref[k].astype(F32)
        if transposed:
            g = g.T
        delta, m_new, v_new = _adamw_math(w_ref[...], g, m_ref[...], v_ref[...])
        g_ref[...] = g
        d_ref[...] = delta
        mo_ref[...] = m_new
        vo_ref[...] = v_new

    return pl.pallas_call(
        body, name=name, grid=(rows // tr,),
        out_shape=[jax.ShapeDtypeStruct((rows, cols), F32)] * 4,
        in_specs=[parts_spec] + [_rows(tr, cols)] * 3,
        out_specs=[_rows(tr, cols)] * 4,
        compiler_params=_params("parallel"),
    )(parts, w, m, v)


def _mesh_position():
    return lax.axis_index("x"), lax.axis_index("y"), lax.axis_index("c")


def _flip(pos, rel):
    x, y, c = pos
    return (1 - x if rel & 4 else x, 1 - y if rel & 2 else y, 1 - c if rel & 1 else c)


def _index(pos):
    return 4 * pos[0] + 2 * pos[1] + pos[2]


_ANY = pl.BlockSpec(memory_space=pl.ANY)
FLAT_ROWS = 32


def _dma_sems(n):
    return [pltpu.SemaphoreType.DMA((n, N_DEV - 1)), pltpu.SemaphoreType.DMA((n, N_DEV - 1)), pltpu.SemaphoreType.DMA((n,))]


def _block_of(ref, idx, rows, flat):
    if flat:
        return ref.at[pl.ds(pl.multiple_of(idx * rows, FLAT_ROWS), rows), :]
    return ref.at[idx]


class _Gather:
    chips = (4, 2, 6)

    def __init__(self, shards):
        self.inputs = list(shards)
        self.flat = [s.shape[0] % FLAT_ROWS == 0 for s in shards]
        self.out_shape = [
            jax.ShapeDtypeStruct((N_DEV * s.shape[0], s.shape[1]) if f else (N_DEV,) + s.shape, s.dtype)
            for s, f in zip(shards, self.flat)]
        self.sems = _dma_sems(len(shards))

    def _copy(self, ins, outs, sems, i, k, block, to, own=False):
        dst = _block_of(outs[i], _index(block), self.inputs[i].shape[0], self.flat[i])
        return pltpu.make_async_remote_copy(
            src_ref=ins[i] if own else dst, dst_ref=dst, send_sem=sems[0].at[i, k], recv_sem=sems[1].at[i, k],
            device_id=to, device_id_type=MESH)

    def _local(self, ins, outs, sems, i, me):
        dst = _block_of(outs[i], _index(me), self.inputs[i].shape[0], self.flat[i])
        return pltpu.make_async_copy(ins[i], dst, sems[2].at[i])

    def _first(self, ins, outs, sems, i, me):
        cps = [self._copy(ins, outs, sems, i, 0, me, _flip(me, 1), own=True)]
        cps += [self._copy(ins, outs, sems, i, 1 + j, me, _flip(me, rel), own=True) for j, rel in enumerate(self.chips)]
        return cps

    def _passed(self, ins, outs, sems, i, j, me):
        return self._copy(ins, outs, sems, i, 4 + j, _flip(me, self.chips[j]), _flip(me, 1))

    def before(self, ins, outs, sems):
        n = len(self.inputs)
        me = _mesh_position()

        @pl.when(pl.program_id(0) == 0)
        def _():
            for i in range(n):
                self._local(ins, outs, sems, i, me).start()
                for cp in self._first(ins, outs, sems, i, me):
                    cp.start()

        @pl.when(pl.program_id(0) == pl.num_programs(0) - 1)
        def _():
            for j, rel in enumerate(self.chips):
                for i in range(n):
                    self._copy(ins, outs, sems, i, 1 + j, _flip(me, rel), me).wait_recv()
                    self._passed(ins, outs, sems, i, j, me).start()

    def after(self, ins, outs, sems):
        n = len(self.inputs)
        me = _mesh_position()
        sibling = _flip(me, 1)

        @pl.when(pl.program_id(0) == pl.num_programs(0) - 1)
        def _():
            for i in range(n):
                self._copy(ins, outs, sems, i, 0, sibling, me).wait_recv()
                for j, rel in enumerate(self.chips):
                    self._copy(ins, outs, sems, i, 4 + j, _flip(sibling, rel), me).wait_recv()
            for i in range(n):
                for cp in self._first(ins, outs, sems, i, me):
                    cp.wait_send()
                for j in range(len(self.chips)):
                    self._passed(ins, outs, sems, i, j, me).wait_send()
                self._local(ins, outs, sems, i, me).wait()


N_CHIPS = N_DEV // 2


def _chip(pos):
    return 2 * pos[0] + pos[1]


class _PairSwap:
    def __init__(self, arrays):
        self.inputs = list(arrays)
        self.rows = [a.shape[0] // N_DEV for a in arrays]
        for r in self.rows:
            assert r % FLAT_ROWS == 0, r
        self.out_shape = [jax.ShapeDtypeStruct((N_CHIPS, r, a.shape[1]), a.dtype) for a, r in zip(arrays, self.rows)]
        n = len(arrays)
        self.sems = [pltpu.SemaphoreType.DMA((n, N_CHIPS)), pltpu.SemaphoreType.DMA((n, N_CHIPS))]

    def _copy(self, ins, outs, sems, i, j, me):
        sibling = _flip(me, 1)
        return pltpu.make_async_remote_copy(
            src_ref=_block_of(ins[i], 2 * j + sibling[2], self.rows[i], True), dst_ref=outs[i].at[j],
            send_sem=sems[0].at[i, j], recv_sem=sems[1].at[i, j], device_id=sibling, device_id_type=MESH)

    def before(self, ins, outs, sems):
        me = _mesh_position()

        @pl.when(pl.program_id(0) == 0)
        def _():
            for i in range(len(self.inputs)):
                for j in range(N_CHIPS):
                    self._copy(ins, outs, sems, i, j, me).start()

    def after(self, ins, outs, sems):
        me = _mesh_position()

        @pl.when(pl.program_id(0) == pl.num_programs(0) - 1)
        def _():
            for i in range(len(self.inputs)):
                for j in range(N_CHIPS):
                    self._copy(ins, outs, sems, i, j, me).wait()


class _ChipExchange:
    chips = (4, 2, 6)

    def __init__(self, arrays):
        self.inputs = list(arrays)
        self.out_shape = [jax.ShapeDtypeStruct(a.shape, a.dtype) for a in arrays]
        n = len(arrays)
        self.sems = [pltpu.SemaphoreType.DMA((n, 3)), pltpu.SemaphoreType.DMA((n, 3)), pltpu.SemaphoreType.DMA((n,))]

    def _send(self, ins, outs, sems, i, k, me):
        peer = _flip(me, self.chips[k])
        return pltpu.make_async_remote_copy(
            src_ref=ins[i].at[_chip(peer)], dst_ref=outs[i].at[_chip(me)], send_sem=sems[0].at[i, k],
            recv_sem=sems[1].at[i, k], device_id=peer, device_id_type=MESH)

    def _arrival(self, ins, outs, sems, i, k, me):
        peer = _flip(me, self.chips[k])
        return pltpu.make_async_remote_copy(
            src_ref=ins[i].at[_chip(me)], dst_ref=outs[i].at[_chip(peer)], send_sem=sems[0].at[i, k],
            recv_sem=sems[1].at[i, k], device_id=peer, device_id_type=MESH)

    def _local(self, ins, outs, sems, i, me):
        return pltpu.make_async_copy(ins[i].at[_chip(me)], outs[i].at[_chip(me)], sems[2].at[i])

    def before(self, ins, outs, sems):
        me = _mesh_position()

        @pl.when(pl.program_id(0) == 0)
        def _():
            for i in range(len(self.inputs)):
                self._local(ins, outs, sems, i, me).start()
                for k in range(len(self.chips)):
                    self._send(ins, outs, sems, i, k, me).start()

    def after(self, ins, outs, sems):
        me = _mesh_position()

        @pl.when(pl.program_id(0) == pl.num_programs(0) - 1)
        def _():
            for i in range(len(self.inputs)):
                for k in range(len(self.chips)):
                    self._arrival(ins, outs, sems, i, k, me).wait_recv()
            for i in range(len(self.inputs)):
                for k in range(len(self.chips)):
                    self._send(ins, outs, sems, i, k, me).wait_send()
                self._local(ins, outs, sems, i, me).wait()


def _pair_add(grads, halves, name):
    n = len(grads)

    def body(*refs):
        g_refs, h_refs, o_refs = refs[:n], refs[n:2 * n], refs[2 * n:]
        c = lax.axis_index("c")
        for i in range(n):
            r = h_refs[i].shape[1]
            for j in range(N_CHIPS):
                own = g_refs[i][pl.ds(pl.multiple_of((2 * j + c) * r, FLAT_ROWS), r), :]
                o_refs[i][j] = (own.astype(F32) + h_refs[i][j].astype(F32)).astype(o_refs[i].dtype)

    return pl.pallas_call(
        body, name=name, out_shape=[jax.ShapeDtypeStruct(h.shape, h.dtype) for h in halves],
        compiler_params=pltpu.CompilerParams(vmem_limit_bytes=VMEM_LIMIT),
    )(*grads, *halves)


class _Both:
    def __init__(self, first, second):
        self.jobs = (first, second)
        self.inputs = first.inputs + second.inputs
        self.out_shape = first.out_shape + second.out_shape
        self.sems = first.sems + second.sems

    def _each(self, ins, outs, sems):
        a = self.jobs[0]
        i, o, s = len(a.inputs), len(a.out_shape), len(a.sems)
        return ((a, ins[:i], outs[:o], sems[:s]), (self.jobs[1], ins[i:], outs[o:], sems[s:]))

    def before(self, ins, outs, sems):
        for job, i, o, s in self._each(ins, outs, sems):
            job.before(i, o, s)

    def after(self, ins, outs, sems):
        for job, i, o, s in self._each(ins, outs, sems):
            job.after(i, o, s)


def _run(body, comm, *, semantics, out_shape, in_specs, out_specs, scratch_shapes=(), **kw):
    if comm is None:
        return pl.pallas_call(body, out_shape=out_shape, in_specs=in_specs, out_specs=out_specs,
                              scratch_shapes=list(scratch_shapes), compiler_params=_params(semantics), **kw)
    single = not isinstance(out_shape, (list, tuple))
    outs = [out_shape] if single else list(out_shape)
    ospecs = [out_specs] if single else list(out_specs)
    counts = [len(in_specs), len(comm.inputs), len(outs), len(comm.out_shape), len(scratch_shapes), len(comm.sems)]

    def carrying(*refs):
        groups, pos = [], 0
        for c in counts:
            groups.append(refs[pos:pos + c])
            pos += c
        main_in, comm_in, main_out, comm_out, main_scratch, comm_sems = groups
        comm.before(comm_in, comm_out, comm_sems)
        body(*main_in, *main_out, *main_scratch)
        comm.after(comm_in, comm_out, comm_sems)

    call = pl.pallas_call(
        carrying, out_shape=outs + list(comm.out_shape), in_specs=list(in_specs) + [_ANY] * len(comm.inputs),
        out_specs=ospecs + [_ANY] * len(comm.out_shape), scratch_shapes=list(scratch_shapes) + list(comm.sems),
        compiler_params=_params("arbitrary"), **kw)

    def apply(*args):
        res = call(*args, *comm.inputs)
        main = res[:len(outs)]
        return (main[0] if single else list(main)), list(res[len(outs):])

    return apply


def _alone(comm, name):
    return _run(lambda: None, comm, semantics="arbitrary", name=name, grid=(1,), out_shape=[], in_specs=[], out_specs=[])()[1]


SHARDED = {"w_in": 1, "w_glu": 0, "conv_w": 1, "w_a_out": 1, "w_b_out": 0, "w_o": 0, "w_ffn_gate": 1, "w_ffn_up": 1,
           "w_ffn_down": 0, "w_ple_gate": 0, "w_ple": 1}
TRANSPOSED = ("w_in", "w_a_out", "w_ffn_gate", "w_ffn_up", "w_ple")
LONG_AXIS_MINOR = ("w_in", "w_ffn_gate", "w_ffn_up")
NARROW_LAST = ("s5_b_re", "s5_b_im", "s5_d")
SMALL = ["g_mix", "b_in", "lam_re", "lam_im", "log_dt", "s5_b_re", "s5_b_im", "s5_c_re", "s5_c_im", "s5_d", "b_glu",
         "conv_b", "w_r", "b_r", "w_i", "b_i", "lru_lambda", "g_ffn", "g_ple_gate", "b_ple_gate", "g_ple", "g_final"]
WEIGHTS = ["g_mix", "w_in", "b_in", "lam_re", "lam_im", "log_dt", "s5_b_re", "s5_b_im", "s5_c_re", "s5_c_im", "s5_d",
           "w_glu", "b_glu", "conv_w", "conv_b", "w_r", "b_r", "w_i", "b_i", "lru_lambda", "w_a_out", "w_b_out", "w_o",
           "g_ffn", "w_ffn_gate", "w_ffn_up", "w_ffn_down", "g_ple_gate", "w_ple_gate", "b_ple_gate", "w_ple", "g_ple",
           "g_final"]


def _unblock(gathered, axis):
    nb, r, c = gathered.shape
    if axis == 0:
        return gathered.reshape(nb * r, c)
    return jnp.transpose(gathered, (1, 0, 2)).reshape(r, nb * c)


def _disc_scalars(lr, li, ldt):
    dt = jnp.exp(ldt)
    mag = jnp.exp(lr * dt)
    ar = mag * jnp.cos(li * dt)
    ai = mag * jnp.sin(li * dt)
    den = lr * lr + li * li
    nr = ar - 1.0
    fr = (nr * lr + ai * li) / den
    fi = (ai * lr - nr * li) / den
    return ar, ai, fr, fi


def _disc_cols(lr, li, ldt, b_re, b_im):
    ar, ai, fr, fi = _disc_scalars(lr, li, ldt)
    return ar, ai, fr * b_re - fi * b_im, fr * b_im + fi * b_re


def _local_step(x, p, target, src, small, disc, distributed=True):
    full = {} if distributed else dict(src)
    gw, halves, pairs, got = {}, {}, {}, {}

    def gather(keys):
        return (_Gather([src[k] for k in keys]), keys, full) if distributed else None

    def swap(keys):
        return (_PairSwap([gw[k] for k in keys]), keys, halves) if distributed else None

    def chips(keys):
        return (_ChipExchange([pairs[k] for k in keys]), keys, got) if distributed else None

    def add_pairs(keys):
        if distributed:
            pairs.update(zip(keys, _pair_add([gw[k] for k in keys], [halves[k] for k in keys], "pair_add_" + keys[0])))

    def carry(fn, *args, jobs=()):
        jobs = [j for j in jobs if j is not None]
        if not jobs:
            return fn(*args)
        comm = jobs[0][0]
        for j in jobs[1:]:
            comm = _Both(comm, j[0])
        res, extra = fn(*args, comm=comm)
        for job, keys, sink in jobs:
            sink.update(zip(keys, extra[:len(job.out_shape)]))
            extra = extra[len(job.out_shape):]
        return res

    d = x.shape[1]
    g, n, pch = small["s5_b_re"].shape
    heads = small["w_r"].shape[0]
    sa, lw = g * pch, small["lru_lambda"].shape[-1]
    widths = [sa, lw, d, d]
    row = lambda v: v.reshape(1, -1)

    hd = small["w_r"].shape[-1]
    ar_row, ai_row, bbr_blk, bbi_blk, cre_blk, cimn_blk, wr_blk, wi_blk = carry(
        _prep, *disc["rows"], *disc["cols"], disc["b_re"], disc["b_im"], small["s5_c_re"].reshape(sa, n),
        small["s5_c_im"].reshape(sa, n), small["w_r"].reshape(lw, hd), small["w_i"].reshape(lw, hd),
        jobs=[gather(["w_in"])])
    d_row = row(small["s5_d"])
    u_a, u_b, za, zb = carry(_inproj_fwd, x, row(small["g_mix"]), full["w_in"], row(small["b_in"]), widths,
                             jobs=[gather(["w_glu", "conv_w", "w_a_out", "w_b_out"])])
    conv_w = _unblock(full["conv_w"], 1) if distributed else full["conv_w"]
    sr, si, y, y_a = carry(_s5_fwd, u_a, bbr_blk, bbi_blk, ar_row, ai_row, cre_blk, cimn_blk, d_row, full["w_glu"],
                           row(small["b_glu"]), jobs=[gather(["w_ffn_gate", "w_o"])])
    xc, h, hprev = carry(_lru_fwd, u_b, conv_w, row(small["conv_b"]), wr_blk, row(small["b_r"]), wi_blk,
                         row(small["b_i"]), row(small["lru_lambda"]), jobs=[gather(["w_ffn_up"])])
    x1, merged, ma, mb = carry(_merge_fwd, y_a, h, za, zb, x, full["w_a_out"], full["w_b_out"], full["w_o"],
                               jobs=[gather(["w_ffn_down"])])
    fg, fu = carry(_ffn_up_fwd, x1, row(small["g_ffn"]), full["w_ffn_gate"], full["w_ffn_up"],
                   jobs=[gather(["w_ple_gate", "w_ple"])])
    x2 = _ffn_down_fwd(fg, fu, x1, full["w_ffn_down"])

    dx2, loss_blk, gw["w_ple_gate"], gw["w_ple"], vec_tail = _tail_fwd_bwd(
        x2, p, target, row(small["g_ple_gate"]), full["w_ple_gate"], row(small["b_ple_gate"]), full["w_ple"],
        row(small["g_ple"]), row(small["g_final"]))
    dfg, dfu, act = carry(_ffn_bwd_a, dx2, fg, fu, full["w_ffn_down"], jobs=[swap(["w_ple_gate", "w_ple"])])
    tn_d = min(d, 512)
    gw["w_ffn_down"] = _matmul_tn(act, dx2, tn_d, "dw_ffn_down", BF16)
    add_pairs(["w_ple_gate", "w_ple"])
    dx1, h2, vec_ffn = carry(_ffn_bwd_b, dfg, dfu, x1, dx2, row(small["g_ffn"]), full["w_ffn_gate"], full["w_ffn_up"],
                             jobs=[chips(["w_ple_gate", "w_ple"]), swap(["w_ffn_down"])])
    gw["w_ffn_gate"] = _matmul_tn(dfg, h2, tn_d, "dw_ffn_gate", BF16)
    gw["w_ffn_up"] = _matmul_tn(dfu, h2, tn_d, "dw_ffn_up", BF16)
    add_pairs(["w_ffn_down"])
    dza, dzb, dya, dyb, gw["w_o"], gw["w_a_out"], gw["w_b_out"] = carry(
        _merge_bwd, dx1, merged, ma, mb, za, zb, y_a, h, full["w_o"], full["w_a_out"], full["w_b_out"],
        jobs=[chips(["w_ffn_down"]), swap(["w_ffn_gate", "w_ffn_up"])])
    add_pairs(["w_ffn_gate", "w_ffn_up"])
    du_b, dw_r, dw_i, vec_lru = carry(
        _lru_bwd, dyb, xc, hprev, u_b, conv_w, wr_blk, row(small["b_r"]), wi_blk, row(small["b_i"]),
        row(small["lru_lambda"]), hd, jobs=[chips(["w_ffn_gate", "w_ffn_up"]), swap(["w_o", "w_a_out", "w_b_out"])])
    add_pairs(["w_o", "w_a_out", "w_b_out"])
    du_a, lam_r, lam_i, dy16, gw["w_glu"], vec_sa, vec_gn = carry(
        _s5_bwd, dya, y, sr, si, u_a, full["w_glu"], row(small["b_glu"]), cre_blk, cimn_blk, bbr_blk, bbi_blk, ar_row,
        ai_row, d_row, jobs=[chips(["w_o", "w_a_out", "w_b_out"])])
    smalls = {"vec_tail": vec_tail, "vec_ffn": vec_ffn, "vec_lru": vec_lru, "vec_sa": vec_sa, "vec_gn": vec_gn,
              "dw_r": dw_r, "dw_i": dw_i, "loss": loss_blk}
    everyones = {}

    def gather_smalls(keys):
        return (_Gather([smalls[k] for k in keys]), keys, everyones) if distributed else None

    smalls["dbb_re"], smalls["dbb_im"], smalls["dc_re"], smalls["dc_imn"] = carry(
        _s5_param_grads, lam_r, lam_i, sr, si, u_a, dy16, pch, n, jobs=[swap(["w_glu"]), gather_smalls(list(smalls))])
    add_pairs(["w_glu"])
    grad_x, h0, dz16, smalls["vec_mix"], smalls["vec_bin"] = _inproj_bwd(
        [du_a, du_b, dza, dzb], x, dx1, row(small["g_mix"]), full["w_in"])
    shapes = {k: a.shape for k, a in smalls.items()}
    gw["w_in"] = carry(_matmul_tn, dz16, h0, tn_d, "dw_in", BF16,
                       jobs=[chips(["w_glu"]), gather_smalls(["dbb_re", "dbb_im", "dc_re", "dc_imn"])])
    smalls.update(everyones)
    if distributed:
        got["w_in"] = gw["w_in"]
        gw = got
    return grad_x, gw, smalls, shapes


def _disc_inputs(small):
    g, n, pch = small["s5_b_re"].shape
    srcs = (small["lam_re"], small["lam_im"], jnp.repeat(small["log_dt"], n))
    return {"rows": [a.reshape(1, g * n) for a in srcs], "cols": [a.reshape(g * n, 1) for a in srcs],
            "b_re": small["s5_b_re"].reshape(g * n, pch), "b_im": small["s5_b_im"].reshape(g * n, pch)}


def kernel(x, p, g_mix, w_in, b_in, lam_re, lam_im, log_dt, s5_b_re, s5_b_im, s5_c_re, s5_c_im, s5_d, w_glu, b_glu, conv_w, conv_b, w_r, b_r, w_i, b_i, lru_lambda, w_a_out, w_b_out, w_o, g_ffn, w_ffn_gate, w_ffn_up, w_ffn_down, g_ple_gate, w_ple_gate, b_ple_gate, w_ple, g_ple, g_final, loss_target, m_g_mix, m_w_in, m_b_in, m_lam_re, m_lam_im, m_log_dt, m_s5_b_re, m_s5_b_im, m_s5_c_re, m_s5_c_im, m_s5_d, m_w_glu, m_b_glu, m_conv_w, m_conv_b, m_w_r, m_b_r, m_w_i, m_b_i, m_lru_lambda, m_w_a_out, m_w_b_out, m_w_o, m_g_ffn, m_w_ffn_gate, m_w_ffn_up, m_w_ffn_down, m_g_ple_gate, m_w_ple_gate, m_b_ple_gate, m_w_ple, m_g_ple, m_g_final, v_g_mix, v_w_in, v_b_in, v_lam_re, v_lam_im, v_log_dt, v_s5_b_re, v_s5_b_im, v_s5_c_re, v_s5_c_im, v_s5_d, v_w_glu, v_b_glu, v_conv_w, v_conv_b, v_w_r, v_b_r, v_w_i, v_b_i, v_lru_lambda, v_w_a_out, v_w_b_out, v_w_o, v_g_ffn, v_w_ffn_gate, v_w_ffn_up, v_w_ffn_down, v_g_ple_gate, v_w_ple_gate, v_b_ple_gate, v_w_ple, v_g_ple, v_g_final):
    given = dict(locals())
    wts = {k: given[k] for k in WEIGHTS}
    moms = {k: given["m_" + k] for k in WEIGHTS}
    vels = {k: given["v_" + k] for k in WEIGHTS}

    def drop_depth(k, a):
        return a if k == "g_final" else a[0]

    small = {k: drop_depth(k, wts[k]) for k in SMALL}
    shard = {k: wts[k][0] for k in SHARDED}
    names = list(SHARDED)

    def wire(k):
        if k == "conv_w":
            return shard[k]
        return (shard[k].T if k in TRANSPOSED else shard[k]).astype(BF16)

    disc = _disc_inputs(small)
    grad_x, parts, smalls, shapes = _local_step(x[0], p[0, 0], loss_target[0], {k: wire(k) for k in names}, small, disc)

    late = ["vec_mix", "vec_bin"]
    received = _alone(_Both(_PairSwap([parts["w_in"]]), _Gather([smalls[k] for k in late])), "swap_last")
    (pair_in,) = _pair_add([parts["w_in"]], received[:1], "pair_add_w_in")
    smalls.update(zip(late, received[1:]))

    g_small, (parts["w_in"],) = _small_reduce(smalls, shapes, *disc["cols"], disc["b_re"], disc["b_im"],
                                              small["s5_b_re"].shape[0], comm=_ChipExchange([pair_in]))
    loss = g_small.pop("loss")[0, 0]
    cols = shard["conv_w"].shape[1]
    mine = _index((lax.axis_index("x"), lax.axis_index("y"), lax.axis_index("c")))
    parts["conv_w"] = lax.dynamic_slice_in_dim(g_small.pop("conv_w"), mine * cols, cols, axis=1)[None]

    def view(k, a):
        a = a.reshape((1, -1) if k == "g_final" else wts[k].shape)
        return jnp.swapaxes(a, -1, -2) if k in NARROW_LAST else a

    def unview(k, a):
        return (jnp.swapaxes(a, -1, -2) if k in NARROW_LAST else a).reshape(wts[k].shape)

    slots = _adamw_small([view(k, wts[k]) for k in SMALL], [view(k, g_small[k]) for k in SMALL],
                         [view(k, moms[k]) for k in SMALL], [view(k, vels[k]) for k in SMALL])
    small_out = [dict(zip(SMALL, [unview(k, a) for k, a in zip(SMALL, slot)])) for slot in slots]

    big_out = {}
    for k in names:
        if k in LONG_AXIS_MINOR:
            res = _adamw(parts[k], shard[k].T, moms[k][0].T, vels[k][0].T, "adamw_" + k)
            big_out[k] = [a.T for a in res]
        else:
            big_out[k] = _adamw(parts[k], shard[k], moms[k][0], vels[k][0], "adamw_" + k, transposed=k in TRANSPOSED)

    outs = [loss, grad_x[None]]
    for slot in range(4):
        for k in WEIGHTS:
            if k in SHARDED:
                outs.append(big_out[k][slot][None])
            else:
                outs.append(small_out[slot][k])
    return tuple(outs)
```

```python
import math

import jax
import jax.numpy as jnp
from jax import lax
from jax.experimental import pallas as pl
from jax.experimental.pallas import tpu as pltpu

F32 = jnp.float32
BF16 = jnp.bfloat16

EPS = 1e-6
LRU_C = 8.0
CONV_WIDTH = 4
ADAM_LR = 0.001
ADAM_B1 = 0.9
ADAM_B2 = 0.999
ADAM_EPS = 1e-08
ADAM_WD = 0.01
ADAM_STEP = 10

N_DEV = 8
MESH = pl.DeviceIdType.MESH
SUBLANES = 8
LANES = 128
VMEM_LIMIT = 56 * 1024 * 1024
TOKEN_TILE = 256
TIME_CHUNK = 256
S5_SLAB = 128
LRU_SLAB = 256


def _dot(a, b):
    return jnp.dot(a.astype(BF16), b.astype(BF16), preferred_element_type=F32)


def _dot_nt(a, b):
    return lax.dot_general(a.astype(BF16), b.astype(BF16), (((1,), (1,)), ((), ())), preferred_element_type=F32)


def _dot_tn(a, b):
    return lax.dot_general(a.astype(BF16), b.astype(BF16), (((0,), (0,)), ((), ())), preferred_element_type=F32)


def _sigmoid(x):
    return jax.nn.sigmoid(x)


def _rms_stats(x):
    r = lax.rsqrt(jnp.mean(x * x, axis=-1, keepdims=True) + EPS)
    return x * r, r


def _rms_bwd(dy, xhat, r, g):
    dxn = dy * g
    dx = r * (dxn - xhat * jnp.mean(dxn * xhat, axis=-1, keepdims=True))
    return dx, dy * xhat


def _rowsum(v):
    return jnp.sum(v, axis=0, keepdims=True)


def _expm1(x):
    u = jnp.exp(x)
    um1 = u - 1.0
    safe = jnp.where(um1 == 0.0, 1.0, jnp.log(u))
    return jnp.where(um1 == 0.0, x, um1 * x / safe)


def _softplus(x):
    e = jnp.exp(-jnp.abs(x))
    u = 1.0 + e
    um1 = u - 1.0
    safe = jnp.where(um1 == 0.0, 1.0, um1)
    log1p_e = jnp.where(um1 == 0.0, e, jnp.log(u) * e / safe)
    return jnp.maximum(x, 0.0) + log1p_e


_GELU_K = math.sqrt(2.0 / math.pi)
_GELU_C = 0.044715


def _gelu(x):
    return 0.5 * x * (1.0 + jnp.tanh(_GELU_K * (x + _GELU_C * x * x * x)))


def _gelu_grad(x):
    th = jnp.tanh(_GELU_K * (x + _GELU_C * x * x * x))
    return 0.5 * (1.0 + th) + 0.5 * x * (1.0 - th * th) * _GELU_K * (1.0 + 3.0 * _GELU_C * x * x)


def _params(*sem):
    return pltpu.CompilerParams(dimension_semantics=sem, vmem_limit_bytes=VMEM_LIMIT)


def _rows(tm, n):
    return pl.BlockSpec((tm, n), lambda i: (i, 0))


def _rows_rev(tm, n, steps):
    return pl.BlockSpec((tm, n), lambda i: (steps - 1 - i, 0))


def _whole(shape):
    nd = len(shape)
    return pl.BlockSpec(shape, lambda i: (0,) * nd, pipeline_mode=pl.Buffered(1))


def _acc(shape):
    nd = len(shape)
    return pl.BlockSpec(shape, lambda i: (0,) * nd)


def _zero_on_first(*refs):
    @pl.when(pl.program_id(0) == 0)
    def _():
        for r in refs:
            r[...] = jnp.zeros_like(r)


def _inproj_fwd(x, g_mix, w_in_t, b_in, widths, comm=None):
    t, d = x.shape
    n = w_in_t.shape[0]
    tm = min(TOKEN_TILE, t)
    offs = [sum(widths[:i]) for i in range(len(widths) + 1)]

    def body(x_ref, g_ref, w_ref, b_ref, *outs):
        xhat, _ = _rms_stats(x_ref[...])
        h = (xhat * g_ref[...]).astype(BF16)
        for k, o_ref in enumerate(outs):
            lo, hi = offs[k], offs[k + 1]
            o_ref[...] = _dot_nt(h, w_ref[lo:hi, :]) + b_ref[:, lo:hi]

    return _run(
        body, comm, name="inproj_fwd", grid=(t // tm,),
        out_shape=[jax.ShapeDtypeStruct((t, w), F32) for w in widths],
        in_specs=[_rows(tm, d), _whole((1, d)), _whole((n, d)), _whole((1, n))],
        out_specs=[_rows(tm, w) for w in widths],
        semantics="parallel",
    )(x, g_mix, w_in_t, b_in)


def _s5_fwd(u, bbr_blk, bbi_blk, ar, ai, cre_blk, cimn_blk, d_skip, w_glu, b_glu, comm=None):
    t, sa = u.shape
    ns, _, sw = bbr_blk.shape
    gn = ns * sw
    tc = min(TIME_CHUNK, t)

    def body(u_ref, bbr_ref, bbi_ref, ar_ref, ai_ref, cre_ref, cim_ref, d_ref, wg_ref, bg_ref,
             sr_ref, si_ref, y_ref, ya_ref, cr_s, ci_s, sr_s, si_s):
        _zero_on_first(cr_s, ci_s)
        uv = u_ref[...]
        ub = uv.astype(BF16)
        for m in range(ns):
            um = ub[:, m * S5_SLAB:(m + 1) * S5_SLAB]
            sr_s[:, m * sw:(m + 1) * sw] = _dot(um, bbr_ref[m])
            si_s[:, m * sw:(m + 1) * sw] = _dot(um, bbi_ref[m])
        a_r = ar_ref[...]
        a_i = ai_ref[...]

        def step(row, carry):
            c_r, c_i = carry
            at = pl.ds(row, 1)
            n_r = a_r * c_r - a_i * c_i + sr_s[at, :]
            n_i = a_r * c_i + a_i * c_r + si_s[at, :]
            sr_s[at, :] = n_r
            si_s[at, :] = n_i
            return n_r, n_i

        c_r, c_i = lax.fori_loop(0, tc, step, (cr_s[0:1, :], ci_s[0:1, :]), unroll=8)
        cr_s[0:1, :] = c_r
        ci_s[0:1, :] = c_i
        for m in range(ns):
            states, chans = slice(m * sw, (m + 1) * sw), slice(m * S5_SLAB, (m + 1) * S5_SLAB)
            s_r, s_i = sr_s[:, states].astype(BF16), si_s[:, states].astype(BF16)
            sr_ref[:, states] = s_r.astype(sr_ref.dtype)
            si_ref[:, states] = s_i.astype(si_ref.dtype)
            y_ref[:, chans] = _dot(s_r, cre_ref[m]) + _dot(s_i, cim_ref[m]) + d_ref[:, chans] * uv[:, chans]
        y = y_ref[...]
        zz = _gelu(y)
        q = _dot(zz, wg_ref[...]) + bg_ref[...]
        ya_ref[...] = zz * _sigmoid(q)

    return _run(
        body, comm, name="s5_fwd", grid=(t // tc,),
        out_shape=[jax.ShapeDtypeStruct((t, gn), BF16), jax.ShapeDtypeStruct((t, gn), BF16),
                   jax.ShapeDtypeStruct((t, sa), F32), jax.ShapeDtypeStruct((t, sa), F32)],
        in_specs=[_rows(tc, sa), _whole(bbr_blk.shape), _whole(bbi_blk.shape), _whole((1, gn)), _whole((1, gn)),
                  _whole(cre_blk.shape), _whole(cimn_blk.shape), _whole((1, sa)), _whole((sa, sa)), _whole((1, sa))],
        out_specs=[_rows(tc, gn), _rows(tc, gn), _rows(tc, sa), _rows(tc, sa)],
        scratch_shapes=[pltpu.VMEM((SUBLANES, gn), F32), pltpu.VMEM((SUBLANES, gn), F32),
                        pltpu.VMEM((tc, gn), F32), pltpu.VMEM((tc, gn), F32)],
        semantics="arbitrary",
    )(u, bbr_blk, bbi_blk, ar, ai, cre_blk, cimn_blk, d_skip, w_glu, b_glu)


def _slab_dot(x, w_ref, transposed=False):
    dot = _dot_nt if transposed else _dot
    xb = x.astype(BF16)
    return jnp.concatenate([dot(xb[:, j * LRU_SLAB:(j + 1) * LRU_SLAB], w_ref[j]) for j in range(w_ref.shape[0])], axis=1)


def _lru_gates(xc, wr_ref, br_ref, wi_ref, bi_ref, lam_ref):
    r = _sigmoid(_slab_dot(xc, wr_ref) + br_ref[...])
    ig = _sigmoid(_slab_dot(xc, wi_ref) + bi_ref[...])
    sp = _softplus(-lam_ref[...])
    log_a = (-LRU_C * r) * sp
    return r, ig, sp, log_a


def _lru_fwd(u, conv_w, conv_b, wr_blk, b_r, wi_blk, b_i, lru_lambda, comm=None):
    t, w = u.shape
    tc = min(TIME_CHUNK, t)
    halo = SUBLANES

    def body(u_ref, cw_ref, cb_ref, wr_ref, br_ref, wi_ref, bi_ref, lam_ref,
             xc_ref, h_ref, hp_ref, ext_s, a_s, carry_s):
        @pl.when(pl.program_id(0) == 0)
        def _():
            ext_s[0:halo, :] = jnp.zeros((halo, w), F32)
            carry_s[...] = jnp.zeros_like(carry_s)

        ext_s[halo:halo + tc, :] = u_ref[...]
        xc = cb_ref[...]
        for k in range(CONV_WIDTH):
            off = halo - (CONV_WIDTH - 1) + k
            xc = xc + cw_ref[k:k + 1, :] * ext_s[off:off + tc, :]
        ext_s[0:halo, :] = ext_s[tc:tc + halo, :]
        xc_ref[...] = xc
        r, ig, sp, log_a = _lru_gates(xc, wr_ref, br_ref, wi_ref, bi_ref, lam_ref)
        a_s[...] = jnp.exp(log_a)
        h_ref[...] = jnp.sqrt(-_expm1(2.0 * log_a)) * ig * xc

        def step(row, carry):
            at = pl.ds(row, 1)
            hp_ref[at, :] = carry
            nxt = a_s[at, :] * carry + h_ref[at, :]
            h_ref[at, :] = nxt
            return nxt

        carry_s[0:1, :] = lax.fori_loop(0, tc, step, carry_s[0:1, :], unroll=8)

    return _run(
        body, comm, name="lru_fwd", grid=(t // tc,),
        out_shape=[jax.ShapeDtypeStruct((t, w), F32)] * 3,
        in_specs=[_rows(tc, w), _whole((CONV_WIDTH, w)), _whole((1, w)), _whole(wr_blk.shape), _whole((1, w)),
                  _whole(wi_blk.shape), _whole((1, w)), _whole((1, w))],
        out_specs=[_rows(tc, w)] * 3,
        scratch_shapes=[pltpu.VMEM((halo + tc, w), F32), pltpu.VMEM((tc, w), F32), pltpu.VMEM((SUBLANES, w), F32)],
        semantics="arbitrary",
    )(u, conv_w, conv_b, wr_blk, b_r, wi_blk, b_i, lru_lambda)


def _merge_fwd(y_a, h, za, zb, x, w_a_out_t, w_b_out, w_o, comm=None):
    t, d = x.shape
    sa, lw = y_a.shape[1], h.shape[1]
    tm = min(TOKEN_TILE, t)

    def body(ya_ref, h_ref, za_ref, zb_ref, x_ref, wa_ref, wb_ref, wo_ref, x1_ref, mg_ref, ma_ref, mb_ref):
        ma = _dot_nt(ya_ref[...], wa_ref[...])
        mb = _dot(h_ref[...], wb_ref[...])
        merged = _sigmoid(za_ref[...]) * ma + _sigmoid(zb_ref[...]) * mb
        ma_ref[...] = ma.astype(ma_ref.dtype)
        mb_ref[...] = mb.astype(mb_ref.dtype)
        mg_ref[...] = merged.astype(mg_ref.dtype)
        x1_ref[...] = x_ref[...] + _dot(merged, wo_ref[...])

    return _run(
        body, comm, name="merge_fwd", grid=(t // tm,),
        out_shape=[jax.ShapeDtypeStruct((t, d), F32), jax.ShapeDtypeStruct((t, d), BF16),
                   jax.ShapeDtypeStruct((t, d), BF16), jax.ShapeDtypeStruct((t, d), BF16)],
        in_specs=[_rows(tm, sa), _rows(tm, lw), _rows(tm, d), _rows(tm, d), _rows(tm, d),
                  _whole((d, sa)), _whole((lw, d)), _whole((d, d))],
        out_specs=[_rows(tm, d)] * 4,
        semantics="parallel",
    )(y_a, h, za, zb, x, w_a_out_t, w_b_out, w_o)


def _ffn_up_fwd(x1, g_ffn, w_gate_t, w_up_t, comm=None):
    t, d = x1.shape
    f = w_gate_t.shape[0]
    tm = min(TOKEN_TILE, t)

    def body(x_ref, g_ref, wg_ref, wu_ref, fg_ref, fu_ref):
        xhat, _ = _rms_stats(x_ref[...])
        h2 = (xhat * g_ref[...]).astype(BF16)
        fg_ref[...] = _dot_nt(h2, wg_ref[...]).astype(fg_ref.dtype)
        fu_ref[...] = _dot_nt(h2, wu_ref[...]).astype(fu_ref.dtype)

    return _run(
        body, comm, name="ffn_up_fwd", grid=(t // tm,),
        out_shape=[jax.ShapeDtypeStruct((t, f), BF16)] * 2,
        in_specs=[_rows(tm, d), _whole((1, d)), _whole((f, d)), _whole((f, d))],
        out_specs=[_rows(tm, f)] * 2,
        semantics="parallel",
    )(x1, g_ffn, w_gate_t, w_up_t)


def _ffn_down_fwd(fg, fu, x1, w_down, comm=None):
    t, d = x1.shape
    f = fg.shape[1]
    tm = min(TOKEN_TILE, t)

    def body(fg_ref, fu_ref, x_ref, wd_ref, x2_ref):
        fgv = fg_ref[...].astype(F32)
        act = fgv * _sigmoid(fgv) * fu_ref[...].astype(F32)
        x2_ref[...] = x_ref[...] + _dot(act, wd_ref[...])

    return _run(
        body, comm, name="ffn_down_fwd", grid=(t // tm,),
        out_shape=jax.ShapeDtypeStruct((t, d), F32),
        in_specs=[_rows(tm, f), _rows(tm, f), _rows(tm, d), _whole((f, d))],
        out_specs=_rows(tm, d),
        semantics="parallel",
    )(fg, fu, x1, w_down)


def _store_on_last(pairs):
    @pl.when(pl.program_id(0) == pl.num_programs(0) - 1)
    def _():
        for acc, out in pairs:
            out[...] = acc[...].astype(out.dtype)


def _tail_fwd_bwd(x2, p, target, g_pg, w_pg, b_pg, w_ple_t, g_ple, g_final):
    t, d = x2.shape
    pd = p.shape[1]
    tm = min(TOKEN_TILE, t)

    def body(x2_ref, p_ref, tg_ref, gpg_ref, wpg_ref, bpg_ref, wple_ref, gple_ref, gfin_ref,
             dx2_ref, loss_ref, dwpg_out, dwple_out, vec_ref, dwpg_ref, dwple_ref):
        _zero_on_first(loss_ref, dwpg_ref, dwple_ref, vec_ref)
        x2v = x2_ref[...]
        xh2, r2 = _rms_stats(x2v)
        h3 = xh2 * gpg_ref[...]
        gp = _sigmoid(_dot(h3, wpg_ref[...]) + bpg_ref[...])
        pe = _dot_nt(p_ref[...], wple_ref[...])
        peh, r3 = _rms_stats(pe)
        e = peh * gple_ref[...]
        x3 = x2v + gp * e
        xh3, r4 = _rms_stats(x3)
        diff = xh3 * gfin_ref[...] - tg_ref[...]
        loss_ref[...] += 0.5 * jnp.sum(jnp.mean(diff * diff, axis=-1, keepdims=True))
        dy = diff * (1.0 / d)
        dx3, dgfin = _rms_bwd(dy, xh3, r4, gfin_ref[...])
        d_gp = dx3 * e
        d_e = dx3 * gp
        dpe, dgple = _rms_bwd(d_e, peh, r3, gple_ref[...])
        dwple_ref[...] += _dot_tn(dpe, p_ref[...])
        dpre = d_gp * gp * (1.0 - gp)
        dwpg_ref[...] += _dot_tn(h3, dpre)
        dh3 = _dot_nt(dpre, wpg_ref[...])
        dx2n, dgpg = _rms_bwd(dh3, xh2, r2, gpg_ref[...])
        dx2_ref[...] = dx3 + dx2n
        vec_ref[0:1, :] += _rowsum(dpre)
        vec_ref[1:2, :] += _rowsum(dgpg)
        vec_ref[2:3, :] += _rowsum(dgple)
        vec_ref[3:4, :] += _rowsum(dgfin)
        _store_on_last([(dwpg_ref, dwpg_out), (dwple_ref, dwple_out)])

    return pl.pallas_call(
        body, name="tail_fwd_bwd", grid=(t // tm,),
        out_shape=[jax.ShapeDtypeStruct((t, d), F32), jax.ShapeDtypeStruct((SUBLANES, LANES), F32),
                   jax.ShapeDtypeStruct((d, d), BF16), jax.ShapeDtypeStruct((d, pd), BF16),
                   jax.ShapeDtypeStruct((SUBLANES, d), F32)],
        in_specs=[_rows(tm, d), _rows(tm, pd), _rows(tm, d), _whole((1, d)), _whole((d, d)), _whole((1, d)),
                  _whole((d, pd)), _whole((1, d)), _whole((1, d))],
        out_specs=[_rows(tm, d), _acc((SUBLANES, LANES)), _acc((d, d)), _acc((d, pd)), _acc((SUBLANES, d))],
        scratch_shapes=[pltpu.VMEM((d, d), F32), pltpu.VMEM((d, pd), F32)],
        compiler_params=_params("arbitrary"),
    )(x2, p, target, g_pg, w_pg, b_pg, w_ple_t, g_ple, g_final)


def _ffn_bwd_a(dx2, fg, fu, w_down, comm=None):
    t, d = dx2.shape
    f = fg.shape[1]
    tm = min(TOKEN_TILE, t)

    def body(dx_ref, fg_ref, fu_ref, wd_ref, dfg_ref, dfu_ref, act_ref):
        dact = _dot_nt(dx_ref[...], wd_ref[...])
        fgv = fg_ref[...].astype(F32)
        fuv = fu_ref[...].astype(F32)
        sg = _sigmoid(fgv)
        silu = fgv * sg
        dfu_ref[...] = (dact * silu).astype(dfu_ref.dtype)
        dfg_ref[...] = (dact * fuv * (sg * (1.0 + fgv * (1.0 - sg)))).astype(dfg_ref.dtype)
        act_ref[...] = (silu * fuv).astype(act_ref.dtype)

    return _run(
        body, comm, name="ffn_bwd_a", grid=(t // tm,),
        out_shape=[jax.ShapeDtypeStruct((t, f), BF16)] * 3,
        in_specs=[_rows(tm, d), _rows(tm, f), _rows(tm, f), _whole((f, d))],
        out_specs=[_rows(tm, f)] * 3,
        semantics="parallel",
    )(dx2, fg, fu, w_down)


def _ffn_bwd_b(dfg, dfu, x1, dx2, g_ffn, w_gate_t, w_up_t, comm=None):
    t, d = x1.shape
    f = dfg.shape[1]
    tm = min(TOKEN_TILE, t)

    def body(dfg_ref, dfu_ref, x_ref, dx2_ref, g_ref, wg_ref, wu_ref, dx1_ref, h2_ref, vec_ref):
        _zero_on_first(vec_ref)
        dh2 = _dot(dfg_ref[...], wg_ref[...]) + _dot(dfu_ref[...], wu_ref[...])
        xhat, r = _rms_stats(x_ref[...])
        h2_ref[...] = (xhat * g_ref[...]).astype(h2_ref.dtype)
        dxn, dg = _rms_bwd(dh2, xhat, r, g_ref[...])
        dx1_ref[...] = dx2_ref[...] + dxn
        vec_ref[0:1, :] += _rowsum(dg)

    return _run(
        body, comm, name="ffn_bwd_b", grid=(t // tm,),
        out_shape=[jax.ShapeDtypeStruct((t, d), F32), jax.ShapeDtypeStruct((t, d), BF16),
                   jax.ShapeDtypeStruct((SUBLANES, d), F32)],
        in_specs=[_rows(tm, f), _rows(tm, f), _rows(tm, d), _rows(tm, d), _whole((1, d)), _whole((f, d)), _whole((f, d))],
        out_specs=[_rows(tm, d), _rows(tm, d), _acc((SUBLANES, d))],
        semantics="arbitrary",
    )(dfg, dfu, x1, dx2, g_ffn, w_gate_t, w_up_t)


def _matmul_tn(a, b, tn, name, dtype=F32, comm=None):
    t, k = a.shape
    n = b.shape[1]

    def body(a_ref, b_ref, o_ref):
        o_ref[...] = _dot_tn(a_ref[...], b_ref[...]).astype(o_ref.dtype)

    return _run(
        body, comm, name=name, grid=(n // tn,),
        out_shape=jax.ShapeDtypeStruct((k, n), dtype),
        in_specs=[_whole((t, k)), pl.BlockSpec((t, tn), lambda j: (0, j))],
        out_specs=pl.BlockSpec((k, tn), lambda j: (0, j)),
        semantics="parallel",
    )(a, b)


def _merge_bwd(dx1, merged, ma, mb, za, zb, y_a, h, w_o, w_a_out_t, w_b_out, comm=None):
    t, d = dx1.shape
    sa, lw = y_a.shape[1], h.shape[1]
    tm = min(TOKEN_TILE, t)

    def body(dx1_ref, mg_ref, ma_ref, mb_ref, za_ref, zb_ref, ya_ref, h_ref, wo_ref, wa_ref, wb_ref,
             dza_ref, dzb_ref, dya_ref, dyb_ref, dwo_out, dwa_out, dwb_out, dwo_ref, dwa_ref, dwb_ref):
        _zero_on_first(dwo_ref, dwa_ref, dwb_ref)
        dx1v = dx1_ref[...].astype(BF16)
        dmg = _dot_nt(dx1v, wo_ref[...])
        ga = _sigmoid(za_ref[...])
        gb = _sigmoid(zb_ref[...])
        dza_ref[...] = dmg * ma_ref[...].astype(F32) * ga * (1.0 - ga)
        dzb_ref[...] = dmg * mb_ref[...].astype(F32) * gb * (1.0 - gb)
        dma = (dmg * ga).astype(BF16)
        dmb = (dmg * gb).astype(BF16)
        dya_ref[...] = _dot(dma, wa_ref[...])
        dyb_ref[...] = _dot_nt(dmb, wb_ref[...])
        dwo_ref[...] += _dot_tn(mg_ref[...], dx1v)
        dwa_ref[...] += _dot_tn(dma, ya_ref[...])
        dwb_ref[...] += _dot_tn(h_ref[...], dmb)
        _store_on_last([(dwo_ref, dwo_out), (dwa_ref, dwa_out), (dwb_ref, dwb_out)])

    return _run(
        body, comm, name="merge_bwd", grid=(t // tm,),
        out_shape=[jax.ShapeDtypeStruct((t, d), F32), jax.ShapeDtypeStruct((t, d), F32),
                   jax.ShapeDtypeStruct((t, sa), F32), jax.ShapeDtypeStruct((t, lw), F32),
                   jax.ShapeDtypeStruct((d, d), BF16), jax.ShapeDtypeStruct((d, sa), BF16),
                   jax.ShapeDtypeStruct((lw, d), BF16)],
        in_specs=[_rows(tm, d), _rows(tm, d), _rows(tm, d), _rows(tm, d), _rows(tm, d), _rows(tm, d),
                  _rows(tm, sa), _rows(tm, lw), _whole((d, d)), _whole((d, sa)), _whole((lw, d))],
        out_specs=[_rows(tm, d), _rows(tm, d), _rows(tm, sa), _rows(tm, lw), _acc((d, d)), _acc((d, sa)), _acc((lw, d))],
        scratch_shapes=[pltpu.VMEM((d, d), F32), pltpu.VMEM((d, sa), F32), pltpu.VMEM((lw, d), F32)],
        semantics="arbitrary",
    )(dx1, merged, ma, mb, za, zb, y_a, h, w_o, w_a_out_t, w_b_out)


def _fold_diag_blocks(dense, row_group, col_group, row0=0, col0=0):
    r, c = dense.shape
    rows = lax.broadcasted_iota(jnp.int32, (r, c), 0) + row0
    cols = lax.broadcasted_iota(jnp.int32, (r, c), 1) + col0
    kept = jnp.where(rows // row_group == cols // col_group, dense, 0.0)
    pick = (lax.broadcasted_iota(jnp.int32, (row_group, r), 0)
            == lax.broadcasted_iota(jnp.int32, (row_group, r), 1) % row_group).astype(F32)
    return jnp.dot(pick, kept, preferred_element_type=F32, precision=lax.Precision.HIGHEST)


def _lru_bwd(dh, xc, hprev, u, conv_w, wr_blk, b_r, wi_blk, b_i, lru_lambda, head_dim, comm=None):
    t, w = dh.shape
    tc = min(TIME_CHUNK, t)
    steps = t // tc
    halo = SUBLANES
    sub_per_chunk = tc // halo
    slabs = w // LRU_SLAB

    def body(dh_ref, xc_ref, hp_ref, u_ref, uh_ref, cw_ref, wr_ref, br_ref, wi_ref, bi_ref, lam_ref,
             du_ref, dwr_out, dwi_out, vec_ref, lam_s, a_s, dxc_s, uext_s, carry_s, dwr_ref, dwi_ref):
        chunk = steps - 1 - pl.program_id(0)

        @pl.when(pl.program_id(0) == 0)
        def _():
            carry_s[...] = jnp.zeros_like(carry_s)
            dxc_s[tc:tc + halo, :] = jnp.zeros((halo, w), F32)
            dwr_ref[...] = jnp.zeros_like(dwr_ref)
            dwi_ref[...] = jnp.zeros_like(dwi_ref)
            vec_ref[...] = jnp.zeros_like(vec_ref)

        xc = xc_ref[...]
        r, ig, sp, log_a = _lru_gates(xc, wr_ref, br_ref, wi_ref, bi_ref, lam_ref)
        a = jnp.exp(log_a)
        a_s[...] = a

        def step(i, q):
            at = pl.ds(tc - 1 - i, 1)
            lam_row = dh_ref[at, :] + q
            lam_s[at, :] = lam_row
            return a_s[at, :] * lam_row

        carry_s[0:1, :] = lax.fori_loop(0, tc, step, carry_s[0:1, :], unroll=8)
        lam = lam_s[...]
        mult = jnp.sqrt(-_expm1(2.0 * log_a))
        d_log_a = lam * hp_ref[...] * a - (lam * ig * xc) * (a * a) / mult
        d_ig = lam * mult * xc
        dpre_r = (d_log_a * (-LRU_C * sp)) * r * (1.0 - r)
        dpre_i = d_ig * ig * (1.0 - ig)
        dxc = lam * mult * ig + _slab_dot(dpre_r, wr_ref, transposed=True) + _slab_dot(dpre_i, wi_ref, transposed=True)
        xcb, drb, dib = xc.astype(BF16), dpre_r.astype(BF16), dpre_i.astype(BF16)
        for j in range(slabs):
            cols = slice(j * LRU_SLAB, (j + 1) * LRU_SLAB)
            dwr_ref[j] += _dot_tn(drb[:, cols], xcb[:, cols])
            dwi_ref[j] += _dot_tn(dib[:, cols], xcb[:, cols])
        vec_ref[0:1, :] += _rowsum(dxc)
        vec_ref[1:2, :] += _rowsum(dpre_r)
        vec_ref[2:3, :] += _rowsum(dpre_i)
        vec_ref[3:4, :] += _rowsum(d_log_a * (-LRU_C * r)) * (-_sigmoid(-lam_ref[...]))
        dxc_s[0:tc, :] = dxc
        du = cw_ref[CONV_WIDTH - 1:CONV_WIDTH, :] * dxc
        for k in range(CONV_WIDTH - 1):
            off = CONV_WIDTH - 1 - k
            du = du + cw_ref[k:k + 1, :] * dxc_s[off:off + tc, :]
        du_ref[...] = du
        dxc_s[tc:tc + halo, :] = dxc_s[0:halo, :]
        uext_s[0:halo, :] = jnp.where(chunk > 0, uh_ref[...], 0.0)
        uext_s[halo:halo + tc, :] = u_ref[...]
        for k in range(CONV_WIDTH):
            off = halo - (CONV_WIDTH - 1) + k
            vec_ref[4 + k:5 + k, :] += _rowsum(dxc * uext_s[off:off + tc, :])

        @pl.when(pl.program_id(0) == steps - 1)
        def _():
            for j in range(slabs):
                cols = slice(j * LRU_SLAB, (j + 1) * LRU_SLAB)
                dwr_out[:, cols] = _fold_diag_blocks(dwr_ref[j], head_dim, head_dim)
                dwi_out[:, cols] = _fold_diag_blocks(dwi_ref[j], head_dim, head_dim)

    halo_spec = pl.BlockSpec((halo, w), lambda i: (jnp.maximum((steps - 1 - i) * sub_per_chunk - 1, 0), 0))
    return _run(
        body, comm, name="lru_bwd", grid=(steps,),
        out_shape=[jax.ShapeDtypeStruct((t, w), F32), jax.ShapeDtypeStruct((head_dim, w), F32),
                   jax.ShapeDtypeStruct((head_dim, w), F32), jax.ShapeDtypeStruct((SUBLANES, w), F32)],
        in_specs=[_rows_rev(tc, w, steps)] * 4 + [halo_spec, _whole((CONV_WIDTH, w)), _whole(wr_blk.shape),
                                                  _whole((1, w)), _whole(wi_blk.shape), _whole((1, w)), _whole((1, w))],
        out_specs=[_rows_rev(tc, w, steps), _acc((head_dim, w)), _acc((head_dim, w)), _acc((SUBLANES, w))],
        scratch_shapes=[pltpu.VMEM((tc, w), F32), pltpu.VMEM((tc, w), F32), pltpu.VMEM((tc + halo, w), F32),
                        pltpu.VMEM((halo + tc, w), F32), pltpu.VMEM((SUBLANES, w), F32),
                        pltpu.VMEM((slabs, LRU_SLAB, LRU_SLAB), F32), pltpu.VMEM((slabs, LRU_SLAB, LRU_SLAB), F32)],
        semantics="arbitrary",
    )(dh, xc, hprev, u, u, conv_w, wr_blk, b_r, wi_blk, b_i, lru_lambda)


def _s5_bwd(dya, y, sr, si, u, w_glu, b_glu, cre_blk, cimn_blk, bbr_blk, bbi_blk, ar, ai, d_skip, comm=None):
    t, sa = dya.shape
    gn = sr.shape[1]
    ns, _, sw = bbr_blk.shape
    tc = min(TIME_CHUNK, t)
    steps = t // tc
    halo = SUBLANES

    def body(dya_ref, y_ref, sr_ref, si_ref, u_ref, wg_ref, bg_ref, cre_ref, cim_ref, bbr_ref, bbi_ref,
             ar_ref, ai_ref, d_ref, du_ref, lr_ref, li_ref, dy_ref, dwg_out, vsa_ref, vgn_ref, gr_s, gi_s, cr_s, ci_s,
             dwg_ref):
        @pl.when(pl.program_id(0) == 0)
        def _():
            cr_s[...] = jnp.zeros_like(cr_s)
            ci_s[...] = jnp.zeros_like(ci_s)
            gr_s[tc:tc + halo, :] = jnp.zeros((halo, gn), F32)
            gi_s[tc:tc + halo, :] = jnp.zeros((halo, gn), F32)
            dwg_ref[...] = jnp.zeros_like(dwg_ref)
            vsa_ref[...] = jnp.zeros_like(vsa_ref)
            vgn_ref[...] = jnp.zeros_like(vgn_ref)

        yv = y_ref[...]
        uv = u_ref[...]
        zz = _gelu(yv)
        sg = _sigmoid(_dot(zz, wg_ref[...]) + bg_ref[...])
        dyav = dya_ref[...]
        dq = dyav * zz * sg * (1.0 - sg)
        dzz = dyav * sg + _dot_nt(dq, wg_ref[...])
        dwg_ref[...] += _dot_tn(zz, dq)
        dy = dzz * _gelu_grad(yv)
        dyb = dy.astype(BF16)
        dy_ref[...] = dyb.astype(dy_ref.dtype)
        vsa_ref[0:1, :] += _rowsum(dq)
        vsa_ref[1:2, :] += _rowsum(dy * uv)
        for m in range(ns):
            dym = dyb[:, m * S5_SLAB:(m + 1) * S5_SLAB]
            gr_s[0:tc, m * sw:(m + 1) * sw] = _dot_nt(dym, cre_ref[m])
            gi_s[0:tc, m * sw:(m + 1) * sw] = _dot_nt(dym, cim_ref[m])
        a_r = ar_ref[...]
        a_i = ai_ref[...]

        def step(i, carry):
            l_r, l_i = carry
            at = pl.ds(tc - 1 - i, 1)
            n_r = gr_s[at, :] + a_r * l_r + a_i * l_i
            n_i = gi_s[at, :] + a_r * l_i - a_i * l_r
            gr_s[at, :] = n_r
            gi_s[at, :] = n_i
            return n_r, n_i

        l_r, l_i = lax.fori_loop(0, tc, step, (cr_s[0:1, :], ci_s[0:1, :]), unroll=8)
        cr_s[0:1, :] = l_r
        ci_s[0:1, :] = l_i
        nxt_r = gr_s[1:tc + 1, :]
        nxt_i = gi_s[1:tc + 1, :]
        srv = sr_ref[...].astype(F32)
        siv = si_ref[...].astype(F32)
        vgn_ref[0:1, :] += _rowsum(nxt_r * srv + nxt_i * siv)
        vgn_ref[1:2, :] += _rowsum(nxt_i * srv - nxt_r * siv)
        lam_r = gr_s[0:tc, :]
        lam_i = gi_s[0:tc, :]
        gr_s[tc:tc + halo, :] = gr_s[0:halo, :]
        gi_s[tc:tc + halo, :] = gi_s[0:halo, :]
        lrb = lam_r.astype(BF16)
        lib = lam_i.astype(BF16)
        lr_ref[...] = lrb.astype(lr_ref.dtype)
        li_ref[...] = lib.astype(li_ref.dtype)
        for m in range(ns):
            states, chans = slice(m * sw, (m + 1) * sw), slice(m * S5_SLAB, (m + 1) * S5_SLAB)
            du_ref[:, chans] = (_dot_nt(lrb[:, states], bbr_ref[m]) + _dot_nt(lib[:, states], bbi_ref[m])
                                + dy[:, chans] * d_ref[:, chans])
        _store_on_last([(dwg_ref, dwg_out)])

    return _run(
        body, comm, name="s5_bwd", grid=(steps,),
        out_shape=[jax.ShapeDtypeStruct((t, sa), F32), jax.ShapeDtypeStruct((t, gn), BF16),
                   jax.ShapeDtypeStruct((t, gn), BF16), jax.ShapeDtypeStruct((t, sa), BF16),
                   jax.ShapeDtypeStruct((sa, sa), BF16), jax.ShapeDtypeStruct((SUBLANES, sa), F32),
                   jax.ShapeDtypeStruct((SUBLANES, gn), F32)],
        in_specs=[_rows_rev(tc, sa, steps), _rows_rev(tc, sa, steps), _rows_rev(tc, gn, steps), _rows_rev(tc, gn, steps),
                  _rows_rev(tc, sa, steps), _whole((sa, sa)), _whole((1, sa)), _whole(cre_blk.shape),
                  _whole(cimn_blk.shape), _whole(bbr_blk.shape), _whole(bbi_blk.shape), _whole((1, gn)), _whole((1, gn)),
                  _whole((1, sa))],
        out_specs=[_rows_rev(tc, sa, steps), _rows_rev(tc, gn, steps), _rows_rev(tc, gn, steps), _rows_rev(tc, sa, steps),
                   _acc((sa, sa)), _acc((SUBLANES, sa)), _acc((SUBLANES, gn))],
        scratch_shapes=[pltpu.VMEM((tc + halo, gn), F32), pltpu.VMEM((tc + halo, gn), F32),
                        pltpu.VMEM((SUBLANES, gn), F32), pltpu.VMEM((SUBLANES, gn), F32), pltpu.VMEM((sa, sa), F32)],
        semantics="arbitrary",
    )(dya, y, sr, si, u, w_glu, b_glu, cre_blk, cimn_blk, bbr_blk, bbi_blk, ar, ai, d_skip)


def _inproj_bwd(dparts, x, dx1, g_mix, w_in_t, comm=None):
    t, d = x.shape
    n = w_in_t.shape[0]
    widths = [p.shape[1] for p in dparts]
    offs = [sum(widths[:i]) for i in range(len(widths) + 1)]
    tm = min(TOKEN_TILE, t)
    np_ = len(dparts)

    def body(*refs):
        dz_refs = refs[:np_]
        x_ref, dx1_ref, g_ref, w_ref, gx_ref, h_ref, dz_ref, vd_ref, vn_ref = refs[np_:]
        _zero_on_first(vd_ref, vn_ref)
        dh = jnp.zeros((tm, d), F32)
        for k, r in enumerate(dz_refs):
            lo, hi = offs[k], offs[k + 1]
            dzk = r[...]
            dh = dh + _dot(dzk, w_ref[lo:hi, :])
            dz_ref[:, lo:hi] = dzk.astype(dz_ref.dtype)
            vn_ref[0:1, lo:hi] += _rowsum(dzk)
        xhat, r0 = _rms_stats(x_ref[...])
        h_ref[...] = (xhat * g_ref[...]).astype(h_ref.dtype)
        dxn, dg = _rms_bwd(dh, xhat, r0, g_ref[...])
        gx_ref[...] = dx1_ref[...] + dxn
        vd_ref[0:1, :] += _rowsum(dg)

    return _run(
        body, comm, name="inproj_bwd", grid=(t // tm,),
        out_shape=[jax.ShapeDtypeStruct((t, d), F32), jax.ShapeDtypeStruct((t, d), BF16),
                   jax.ShapeDtypeStruct((t, n), BF16), jax.ShapeDtypeStruct((SUBLANES, d), F32),
                   jax.ShapeDtypeStruct((SUBLANES, n), F32)],
        in_specs=[_rows(tm, w) for w in widths] + [_rows(tm, d), _rows(tm, d), _whole((1, d)), _whole((n, d))],
        out_specs=[_rows(tm, d), _rows(tm, d), _rows(tm, n), _acc((SUBLANES, d)), _acc((SUBLANES, n))],
        semantics="arbitrary",
    )(*dparts, x, dx1, g_mix, w_in_t)


def _prep(lr_row, li_row, ldt_row, lr_col, li_col, ldt_col, b_re, b_im, c_re, c_im, w_r, w_i, comm=None):
    gn, pch = b_re.shape
    sa, n = c_re.shape
    w, hd = w_r.shape
    ns, sw, lsl = sa // S5_SLAB, S5_SLAB * n // pch, w // LRU_SLAB

    def spread_cols(vals, row_group, col_group, width):
        r, k = vals.shape
        tile = (lax.broadcasted_iota(jnp.int32, (k, width), 0) == lax.broadcasted_iota(jnp.int32, (k, width), 1) % k)
        rows = lax.broadcasted_iota(jnp.int32, (r, width), 0) // row_group
        cols = lax.broadcasted_iota(jnp.int32, (r, width), 1) // col_group
        return jnp.where(rows == cols, _dot(vals, tile.astype(BF16)), 0.0)

    def spread_rows(vals, row_group, col_group, height):
        k, c = vals.shape
        tile = (lax.broadcasted_iota(jnp.int32, (height, k), 0) % k == lax.broadcasted_iota(jnp.int32, (height, k), 1))
        rows = lax.broadcasted_iota(jnp.int32, (height, c), 0) // row_group
        cols = lax.broadcasted_iota(jnp.int32, (height, c), 1) // col_group
        return jnp.where(rows == cols, _dot(tile.astype(BF16), vals), 0.0)

    def body(lrr, lir, ldr, lrc, lic, ldc, bre, bim, cre, cim, wr, wi,
             ar_o, ai_o, bbr_o, bbi_o, cre_o, cim_o, wr_o, wi_o):
        ar, ai, _, _ = _disc_scalars(lrr[...], lir[...], ldr[...])
        ar_o[...] = ar
        ai_o[...] = ai
        _, _, bbr, bbi = _disc_cols(lrc[...], lic[...], ldc[...], bre[...], bim[...])
        bbr_t, bbi_t = bbr.T, bbi.T
        for m in range(ns):
            bbr_o[m] = spread_rows(bbr_t[:, m * sw:(m + 1) * sw], pch, n, S5_SLAB).astype(bbr_o.dtype)
            bbi_o[m] = spread_rows(bbi_t[:, m * sw:(m + 1) * sw], pch, n, S5_SLAB).astype(bbi_o.dtype)
            rows = slice(m * S5_SLAB, (m + 1) * S5_SLAB)
            cre_o[m] = spread_rows(cre[rows, :].T, n, pch, sw).astype(cre_o.dtype)
            cim_o[m] = spread_rows(-cim[rows, :].T, n, pch, sw).astype(cim_o.dtype)
        for j in range(lsl):
            rows = slice(j * LRU_SLAB, (j + 1) * LRU_SLAB)
            wr_o[j] = spread_cols(wr[rows, :], hd, hd, LRU_SLAB).astype(wr_o.dtype)
            wi_o[j] = spread_cols(wi[rows, :], hd, hd, LRU_SLAB).astype(wi_o.dtype)

    args = (lr_row, li_row, ldt_row, lr_col, li_col, ldt_col, b_re, b_im, c_re, c_im, w_r, w_i)
    out_shape = [jax.ShapeDtypeStruct((1, gn), F32), jax.ShapeDtypeStruct((1, gn), F32),
                 jax.ShapeDtypeStruct((ns, S5_SLAB, sw), BF16), jax.ShapeDtypeStruct((ns, S5_SLAB, sw), BF16),
                 jax.ShapeDtypeStruct((ns, sw, S5_SLAB), BF16), jax.ShapeDtypeStruct((ns, sw, S5_SLAB), BF16),
                 jax.ShapeDtypeStruct((lsl, LRU_SLAB, LRU_SLAB), BF16),
                 jax.ShapeDtypeStruct((lsl, LRU_SLAB, LRU_SLAB), BF16)]
    return _run(
        body, comm, name="prep", grid=(1,), out_shape=out_shape, in_specs=[_whole(a.shape) for a in args],
        out_specs=[_acc(s.shape) for s in out_shape], semantics="arbitrary",
    )(*args)


def _s5_param_grads(lam_r, lam_i, sr, si, u, dy, pch, n, comm=None):
    t, gn = lam_r.shape
    sa = u.shape[1]
    sw = S5_SLAB * n // pch

    def body(lr_ref, li_ref, sr_ref, si_ref, u_ref, dy_ref, dbr_ref, dbi_ref, dcr_ref, dci_ref):
        uv = u_ref[...]
        dyv = dy_ref[...]
        dbr_ref[...] = _fold_diag_blocks(_dot_tn(uv, lr_ref[...]), pch, n)
        dbi_ref[...] = _fold_diag_blocks(_dot_tn(uv, li_ref[...]), pch, n)
        dcr_ref[...] = _fold_diag_blocks(_dot_tn(sr_ref[...], dyv), n, pch)
        dci_ref[...] = _fold_diag_blocks(_dot_tn(si_ref[...], dyv), n, pch)

    states = pl.BlockSpec((t, sw), lambda m: (0, m))
    chans = pl.BlockSpec((t, S5_SLAB), lambda m: (0, m))
    return _run(
        body, comm, name="s5_param_grads", grid=(sa // S5_SLAB,),
        out_shape=[jax.ShapeDtypeStruct((pch, gn), F32), jax.ShapeDtypeStruct((pch, gn), F32),
                   jax.ShapeDtypeStruct((n, sa), F32), jax.ShapeDtypeStruct((n, sa), F32)],
        in_specs=[states, states, states, states, chans, chans],
        out_specs=[pl.BlockSpec((pch, sw), lambda m: (0, m)), pl.BlockSpec((pch, sw), lambda m: (0, m)),
                   pl.BlockSpec((n, S5_SLAB), lambda m: (0, m)), pl.BlockSpec((n, S5_SLAB), lambda m: (0, m))],
        semantics="parallel",
    )(lam_r, lam_i, sr, si, u, dy)


SMALL_PARTS = ["vec_tail", "vec_ffn", "vec_lru", "vec_mix", "vec_bin", "vec_sa", "vec_gn", "dw_r", "dw_i", "dbb_re",
               "dbb_im", "dc_re", "dc_imn", "loss"]


def _small_reduce(parts, shapes, lr_col, li_col, ldt_col, b_re, b_im, groups, comm=None):
    gn, pch = b_re.shape
    n = gn // groups
    nparts = parts[SMALL_PARTS[0]].size // math.prod(shapes[SMALL_PARTS[0]])
    np_, nout = len(SMALL_PARTS), 24

    def body(*refs):
        ins = refs[:np_]
        lr, li, ldt, bre, bim = refs[np_:np_ + 5]
        outs = refs[np_ + 5:np_ + 5 + nout]
        sums = dict(zip(SMALL_PARTS, refs[np_ + 5 + nout:]))

        @pl.when(pl.program_id(0) == 0)
        def _():
            for k, r in zip(SMALL_PARTS, ins):
                sums[k][...] = r[...]

        @pl.when(pl.program_id(0) > 0)
        def _():
            for k, r in zip(SMALL_PARTS, ins):
                sums[k][...] += r[...]

        @pl.when(pl.program_id(0) == nparts - 1)
        def _():
            finish({k: s[...] for k, s in sums.items()}, lr, li, ldt, bre, bim, *outs)

    def finish(tot, lr, li, ldt, bre, bim, o_loss, o_gmix, o_bin, o_bglu, o_s5d, o_convb, o_br, o_bi, o_lam, o_gffn,
               o_gpg, o_bpg, o_gple, o_gfin, o_wr, o_wi, o_cre, o_cim, o_bre, o_bim, o_lre, o_lim, o_ldt, o_convw):
        o_loss[...] = tot["loss"]
        o_convw[...] = tot["vec_lru"][SUBLANES - CONV_WIDTH:SUBLANES]
        o_bpg[...] = tot["vec_tail"][0:1]
        o_gpg[...] = tot["vec_tail"][1:2]
        o_gple[...] = tot["vec_tail"][2:3]
        o_gfin[...] = tot["vec_tail"][3:4]
        o_gffn[...] = tot["vec_ffn"][0:1]
        o_convb[...] = tot["vec_lru"][0:1]
        o_br[...] = tot["vec_lru"][1:2]
        o_bi[...] = tot["vec_lru"][2:3]
        o_lam[...] = tot["vec_lru"][3:4]
        o_gmix[...] = tot["vec_mix"][0:1]
        o_bin[...] = tot["vec_bin"][0:1]
        o_bglu[...] = tot["vec_sa"][0:1]
        o_s5d[...] = tot["vec_sa"][1:2]
        o_wr[...] = tot["dw_r"].T
        o_wi[...] = tot["dw_i"].T
        o_cre[...] = tot["dc_re"].T
        o_cim[...] = -tot["dc_imn"].T
        d_a = tot["vec_gn"].T
        _, chain = jax.vjp(_disc_cols, lr[...], li[...], ldt[...], bre[...], bim[...])
        d_lr, d_li, d_ldt, d_bre, d_bim = chain((d_a[:, 0:1], d_a[:, 1:2], tot["dbb_re"].T, tot["dbb_im"].T))
        o_lre[...] = d_lr
        o_lim[...] = d_li
        o_bre[...] = d_bre
        o_bim[...] = d_bim
        same = (lax.broadcasted_iota(jnp.int32, (groups, gn), 0)
                == lax.broadcasted_iota(jnp.int32, (groups, gn), 1) // n).astype(F32)
        o_ldt[...] = jnp.dot(same, d_ldt * jnp.ones((1, LANES), F32), preferred_element_type=F32,
                             precision=lax.Precision.HIGHEST)[:, 0:1]

    d = shapes["vec_mix"][1]
    nz = shapes["vec_bin"][1]
    sa = shapes["vec_sa"][1]
    w = shapes["vec_lru"][1]
    row = lambda c: jax.ShapeDtypeStruct((1, c), F32)
    out_shape = [jax.ShapeDtypeStruct(shapes["loss"], F32), row(d), row(nz), row(sa), row(sa), row(w), row(w), row(w),
                 row(w), row(d), row(d), row(d), row(d), row(d),
                 jax.ShapeDtypeStruct(shapes["dw_r"][::-1], F32), jax.ShapeDtypeStruct(shapes["dw_i"][::-1], F32),
                 jax.ShapeDtypeStruct(shapes["dc_re"][::-1], F32), jax.ShapeDtypeStruct(shapes["dc_imn"][::-1], F32),
                 jax.ShapeDtypeStruct((gn, pch), F32), jax.ShapeDtypeStruct((gn, pch), F32),
                 jax.ShapeDtypeStruct((gn, 1), F32), jax.ShapeDtypeStruct((gn, 1), F32),
                 jax.ShapeDtypeStruct((groups, 1), F32), jax.ShapeDtypeStruct((CONV_WIDTH, w), F32)]
    def part_spec(k):
        r, c = shapes[k]
        if parts[k].ndim == 3:
            return pl.BlockSpec((None, r, c), lambda i: (i, 0, 0))
        return pl.BlockSpec((r, c), lambda i: (i, 0))

    outs = _run(
        body, comm, name="small_reduce", grid=(nparts,), out_shape=out_shape,
        in_specs=[part_spec(k) for k in SMALL_PARTS] + [_whole(a.shape) for a in (lr_col, li_col, ldt_col, b_re, b_im)],
        out_specs=[_acc(s.shape) for s in out_shape],
        scratch_shapes=[pltpu.VMEM(shapes[k], F32) for k in SMALL_PARTS],
        semantics="arbitrary",
    )(*[parts[k] for k in SMALL_PARTS], lr_col, li_col, ldt_col, b_re, b_im)
    extra = None
    if comm is not None:
        outs, extra = outs
    names = ["loss", "g_mix", "b_in", "b_glu", "s5_d", "conv_b", "b_r", "b_i", "lru_lambda", "g_ffn", "g_ple_gate",
             "b_ple_gate", "g_ple", "g_final", "w_r", "w_i", "s5_c_re", "s5_c_im", "s5_b_re", "s5_b_im", "lam_re",
             "lam_im", "log_dt", "conv_w"]
    res = dict(zip(names, outs))
    return res if comm is None else (res, extra)


def _adamw_small(ws, gs, ms, vs):
    n = len(ws)

    def body(*refs):
        w_r, g_r, m_r, v_r = (refs[i * n:(i + 1) * n] for i in range(4))
        g_o, d_o, m_o, v_o = (refs[(4 + i) * n:(5 + i) * n] for i in range(4))
        for i in range(n):
            g = g_r[i][...]
            delta, m_new, v_new = _adamw_math(w_r[i][...], g, m_r[i][...], v_r[i][...])
            g_o[i][...] = g
            d_o[i][...] = delta
            m_o[i][...] = m_new
            v_o[i][...] = v_new

    shapes = [jax.ShapeDtypeStruct(a.shape, F32) for a in ws]
    outs = pl.pallas_call(body, name="adamw_small", out_shape=shapes * 4)(*ws, *gs, *ms, *vs)
    return outs[:n], outs[n:2 * n], outs[2 * n:3 * n], outs[3 * n:]


def _adamw_math(w, g, m, v):
    m_new = ADAM_B1 * m + (1.0 - ADAM_B1) * g
    v_new = ADAM_B2 * v + (1.0 - ADAM_B2) * (g * g)
    m_hat = m_new / (1.0 - ADAM_B1 ** ADAM_STEP)
    v_hat = v_new / (1.0 - ADAM_B2 ** ADAM_STEP)
    delta = -ADAM_LR * (m_hat / (jnp.sqrt(v_hat) + ADAM_EPS) + ADAM_WD * w)
    return delta, m_new, v_new


def _row_tile(rows):
    for cand in range(256, 0, -16):
        if rows % cand == 0:
            return cand
    return rows


def _adamw(parts, w, m, v, name, transposed=False):
    rows, cols = w.shape
    npart = parts.shape[0]
    tr = _row_tile(rows)
    if transposed:
        parts_spec = pl.BlockSpec((npart, cols, tr), lambda i: (0, 0, i))
    else:
        parts_spec = pl.BlockSpec((npart, tr, cols), lambda i: (0, i, 0))

    def body(p_ref, w_ref, m_ref, v_ref, g_ref, d_ref, mo_ref, vo_ref):
        g = p_ref[0].astype(F32)
        for k in range(1, npart):
            g = g + p_ref[k].astype(F32)
        if transposed:
            g = g.T
        delta, m_new, v_new = _adamw_math(w_ref[...], g, m_ref[...], v_ref[...])
        g_ref[...] = g
        d_ref[...] = delta
        mo_ref[...] = m_new
        vo_ref[...] = v_new

    return pl.pallas_call(
        body, name=name, grid=(rows // tr,),
        out_shape=[jax.ShapeDtypeStruct((rows, cols), F32)] * 4,
        in_specs=[parts_spec] + [_rows(tr, cols)] * 3,
        out_specs=[_rows(tr, cols)] * 4,
        compiler_params=_params("parallel"),
    )(parts, w, m, v)


def _mesh_position():
    return lax.axis_index("x"), lax.axis_index("y"), lax.axis_index("c")


def _flip(pos, rel):
    x, y, c = pos
    return (1 - x if rel & 4 else x, 1 - y if rel & 2 else y, 1 - c if rel & 1 else c)


def _index(pos):
    return 4 * pos[0] + 2 * pos[1] + pos[2]


_ANY = pl.BlockSpec(memory_space=pl.ANY)
FLAT_ROWS = 32


def _dma_sems(n):
    return [pltpu.SemaphoreType.DMA((n, N_DEV - 1)), pltpu.SemaphoreType.DMA((n, N_DEV - 1)), pltpu.SemaphoreType.DMA((n,))]


def _block_of(ref, idx, rows, flat):
    if flat:
        return ref.at[pl.ds(pl.multiple_of(idx * rows, FLAT_ROWS), rows), :]
    return ref.at[idx]


class _Gather:
    chips = (4, 2, 6)
    rels = frozenset((1, 4, 2, 6))

    def __init__(self, shards):
        self.inputs = list(shards)
        self.flat = [s.shape[0] % FLAT_ROWS == 0 for s in shards]
        self.out_shape = [
            jax.ShapeDtypeStruct((N_DEV * s.shape[0], s.shape[1]) if f else (N_DEV,) + s.shape, s.dtype)
            for s, f in zip(shards, self.flat)]
        self.sems = _dma_sems(len(shards))

    def _copy(self, ins, outs, sems, i, k, block, to, own=False):
        dst = _block_of(outs[i], _index(block), self.inputs[i].shape[0], self.flat[i])
        return pltpu.make_async_remote_copy(
            src_ref=ins[i] if own else dst, dst_ref=dst, send_sem=sems[0].at[i, k], recv_sem=sems[1].at[i, k],
            device_id=to, device_id_type=MESH)

    def _local(self, ins, outs, sems, i, me):
        dst = _block_of(outs[i], _index(me), self.inputs[i].shape[0], self.flat[i])
        return pltpu.make_async_copy(ins[i], dst, sems[2].at[i])

    def _first(self, ins, outs, sems, i, me):
        cps = [self._copy(ins, outs, sems, i, 0, me, _flip(me, 1), own=True)]
        cps += [self._copy(ins, outs, sems, i, 1 + j, me, _flip(me, rel), own=True) for j, rel in enumerate(self.chips)]
        return cps

    def _passed(self, ins, outs, sems, i, j, me):
        return self._copy(ins, outs, sems, i, 4 + j, _flip(me, self.chips[j]), _flip(me, 1))

    def before(self, ins, outs, sems):
        n = len(self.inputs)
        me = _mesh_position()

        @pl.when(pl.program_id(0) == 0)
        def _():
            for i in range(n):
                self._local(ins, outs, sems, i, me).start()
                for cp in self._first(ins, outs, sems, i, me):
                    cp.start()

        @pl.when(pl.program_id(0) == pl.num_programs(0) - 1)
        def _():
            for j, rel in enumerate(self.chips):
                for i in range(n):
                    self._copy(ins, outs, sems, i, 1 + j, _flip(me, rel), me).wait_recv()
                    self._passed(ins, outs, sems, i, j, me).start()

    def after(self, ins, outs, sems):
        n = len(self.inputs)
        me = _mesh_position()
        sibling = _flip(me, 1)

        @pl.when(pl.program_id(0) == pl.num_programs(0) - 1)
        def _():
            for i in range(n):
                self._copy(ins, outs, sems, i, 0, sibling, me).wait_recv()
                for j, rel in enumerate(self.chips):
                    self._copy(ins, outs, sems, i, 4 + j, _flip(sibling, rel), me).wait_recv()
            for i in range(n):
                for cp in self._first(ins, outs, sems, i, me):
                    cp.wait_send()
                for j in range(len(self.chips)):
                    self._passed(ins, outs, sems, i, j, me).wait_send()
                self._local(ins, outs, sems, i, me).wait()


N_CHIPS = N_DEV // 2


def _chip(pos):
    return 2 * pos[0] + pos[1]


class _PairSwap:
    rels = frozenset((1,))

    def __init__(self, arrays):
        self.inputs = list(arrays)
        self.rows = [a.shape[0] // N_DEV for a in arrays]
        for r in self.rows:
            assert r % FLAT_ROWS == 0, r
        self.out_shape = [jax.ShapeDtypeStruct((N_CHIPS, r, a.shape[1]), a.dtype) for a, r in zip(arrays, self.rows)]
        n = len(arrays)
        self.sems = [pltpu.SemaphoreType.DMA((n, N_CHIPS)), pltpu.SemaphoreType.DMA((n, N_CHIPS))]

    def _copy(self, ins, outs, sems, i, j, me):
        sibling = _flip(me, 1)
        return pltpu.make_async_remote_copy(
            src_ref=_block_of(ins[i], 2 * j + sibling[2], self.rows[i], True), dst_ref=outs[i].at[j],
            send_sem=sems[0].at[i, j], recv_sem=sems[1].at[i, j], device_id=sibling, device_id_type=MESH)

    def before(self, ins, outs, sems):
        me = _mesh_position()

        @pl.when(pl.program_id(0) == 0)
        def _():
            for i in range(len(self.inputs)):
                for j in range(N_CHIPS):
                    self._copy(ins, outs, sems, i, j, me).start()

    def after(self, ins, outs, sems):
        me = _mesh_position()

        @pl.when(pl.program_id(0) == pl.num_programs(0) - 1)
        def _():
            for i in range(len(self.inputs)):
                for j in range(N_CHIPS):
                    self._copy(ins, outs, sems, i, j, me).wait()


class _ChipExchange:
    chips = (4, 2, 6)
    rels = frozenset(chips)

    def __init__(self, arrays):
        self.inputs = list(arrays)
        self.out_shape = [jax.ShapeDtypeStruct(a.shape, a.dtype) for a in arrays]
        n = len(arrays)
        self.sems = [pltpu.SemaphoreType.DMA((n, 3)), pltpu.SemaphoreType.DMA((n, 3)), pltpu.SemaphoreType.DMA((n,))]

    def _send(self, ins, outs, sems, i, k, me):
        peer = _flip(me, self.chips[k])
        return pltpu.make_async_remote_copy(
            src_ref=ins[i].at[_chip(peer)], dst_ref=outs[i].at[_chip(me)], send_sem=sems[0].at[i, k],
            recv_sem=sems[1].at[i, k], device_id=peer, device_id_type=MESH)

    def _arrival(self, ins, outs, sems, i, k, me):
        peer = _flip(me, self.chips[k])
        return pltpu.make_async_remote_copy(
            src_ref=ins[i].at[_chip(me)], dst_ref=outs[i].at[_chip(peer)], send_sem=sems[0].at[i, k],
            recv_sem=sems[1].at[i, k], device_id=peer, device_id_type=MESH)

    def _local(self, ins, outs, sems, i, me):
        return pltpu.make_async_copy(ins[i].at[_chip(me)], outs[i].at[_chip(me)], sems[2].at[i])

    def before(self, ins, outs, sems):
        me = _mesh_position()

        @pl.when(pl.program_id(0) == 0)
        def _():
            for i in range(len(self.inputs)):
                self._local(ins, outs, sems, i, me).start()
                for k in range(len(self.chips)):
                    self._send(ins, outs, sems, i, k, me).start()

    def after(self, ins, outs, sems):
        me = _mesh_position()

        @pl.when(pl.program_id(0) == pl.num_programs(0) - 1)
        def _():
            for i in range(len(self.inputs)):
                for k in range(len(self.chips)):
                    self._arrival(ins, outs, sems, i, k, me).wait_recv()
            for i in range(len(self.inputs)):
                for k in range(len(self.chips)):
                    self._send(ins, outs, sems, i, k, me).wait_send()
                self._local(ins, outs, sems, i, me).wait()


def _pair_add(grads, halves, name):
    n = len(grads)

    def body(*refs):
        g_refs, h_refs, o_refs = refs[:n], refs[n:2 * n], refs[2 * n:]
        c = lax.axis_index("c")
        for i in range(n):
            r = h_refs[i].shape[1]
            for j in range(N_CHIPS):
                own = g_refs[i][pl.ds(pl.multiple_of((2 * j + c) * r, FLAT_ROWS), r), :]
                o_refs[i][j] = (own.astype(F32) + h_refs[i][j].astype(F32)).astype(o_refs[i].dtype)

    return pl.pallas_call(
        body, name=name, out_shape=[jax.ShapeDtypeStruct(h.shape, h.dtype) for h in halves],
        compiler_params=pltpu.CompilerParams(vmem_limit_bytes=VMEM_LIMIT),
    )(*grads, *halves)


class _Both:
    def __init__(self, first, second):
        self.jobs = (first, second)
        self.rels = first.rels | second.rels
        self.inputs = first.inputs + second.inputs
        self.out_shape = first.out_shape + second.out_shape
        self.sems = first.sems + second.sems

    def _each(self, ins, outs, sems):
        a = self.jobs[0]
        i, o, s = len(a.inputs), len(a.out_shape), len(a.sems)
        return ((a, ins[:i], outs[:o], sems[:s]), (self.jobs[1], ins[i:], outs[o:], sems[s:]))

    def before(self, ins, outs, sems):
        for job, i, o, s in self._each(ins, outs, sems):
            job.before(i, o, s)

    def after(self, ins, outs, sems):
        for job, i, o, s in self._each(ins, outs, sems):
            job.after(i, o, s)


_COLLECTIVE_IDS = {(1,): 0, (2, 4, 6): 1, (1, 2, 4, 6): 2}


def _entry_barrier(rels):
    @pl.when(pl.program_id(0) == 0)
    def _():
        me = _mesh_position()
        sem = pltpu.get_barrier_semaphore()
        for rel in rels:
            pl.semaphore_signal(sem, inc=1, device_id=_flip(me, rel), device_id_type=MESH)
        pl.semaphore_wait(sem, len(rels))


def _run(body, comm, *, semantics, out_shape, in_specs, out_specs, scratch_shapes=(), **kw):
    if comm is None:
        return pl.pallas_call(body, out_shape=out_shape, in_specs=in_specs, out_specs=out_specs,
                              scratch_shapes=list(scratch_shapes), compiler_params=_params(semantics), **kw)
    single = not isinstance(out_shape, (list, tuple))
    outs = [out_shape] if single else list(out_shape)
    ospecs = [out_specs] if single else list(out_specs)
    counts = [len(in_specs), len(comm.inputs), len(outs), len(comm.out_shape), len(scratch_shapes), len(comm.sems)]
    rels = tuple(sorted(comm.rels))

    def carrying(*refs):
        groups, pos = [], 0
        for c in counts:
            groups.append(refs[pos:pos + c])
            pos += c
        main_in, comm_in, main_out, comm_out, main_scratch, comm_sems = groups
        _entry_barrier(rels)
        comm.before(comm_in, comm_out, comm_sems)
        body(*main_in, *main_out, *main_scratch)
        comm.after(comm_in, comm_out, comm_sems)

    call = pl.pallas_call(
        carrying, out_shape=outs + list(comm.out_shape), in_specs=list(in_specs) + [_ANY] * len(comm.inputs),
        out_specs=ospecs + [_ANY] * len(comm.out_shape), scratch_shapes=list(scratch_shapes) + list(comm.sems),
        compiler_params=pltpu.CompilerParams(dimension_semantics=("arbitrary",), vmem_limit_bytes=VMEM_LIMIT,
                                             collective_id=_COLLECTIVE_IDS[rels]), **kw)

    def apply(*args):
        res = call(*args, *comm.inputs)
        main = res[:len(outs)]
        return (main[0] if single else list(main)), list(res[len(outs):])

    return apply


def _alone(comm, name):
    return _run(lambda: None, comm, semantics="arbitrary", name=name, grid=(1,), out_shape=[], in_specs=[], out_specs=[])()[1]


SHARDED = {"w_in": 1, "w_glu": 0, "conv_w": 1, "w_a_out": 1, "w_b_out": 0, "w_o": 0, "w_ffn_gate": 1, "w_ffn_up": 1,
           "w_ffn_down": 0, "w_ple_gate": 0, "w_ple": 1}
TRANSPOSED = ("w_in", "w_a_out", "w_ffn_gate", "w_ffn_up", "w_ple")
LONG_AXIS_MINOR = ("w_in", "w_ffn_gate", "w_ffn_up")
NARROW_LAST = ("s5_b_re", "s5_b_im", "s5_d")
SMALL = ["g_mix", "b_in", "lam_re", "lam_im", "log_dt", "s5_b_re", "s5_b_im", "s5_c_re", "s5_c_im", "s5_d", "b_glu",
         "conv_b", "w_r", "b_r", "w_i", "b_i", "lru_lambda", "g_ffn", "g_ple_gate", "b_ple_gate", "g_ple", "g_final"]
WEIGHTS = ["g_mix", "w_in", "b_in", "lam_re", "lam_im", "log_dt", "s5_b_re", "s5_b_im", "s5_c_re", "s5_c_im", "s5_d",
           "w_glu", "b_glu", "conv_w", "conv_b", "w_r", "b_r", "w_i", "b_i", "lru_lambda", "w_a_out", "w_b_out", "w_o",
           "g_ffn", "w_ffn_gate", "w_ffn_up", "w_ffn_down", "g_ple_gate", "w_ple_gate", "b_ple_gate", "w_ple", "g_ple",
           "g_final"]


def _unblock(gathered, axis):
    nb, r, c = gathered.shape
    if axis == 0:
        return gathered.reshape(nb * r, c)
    return jnp.transpose(gathered, (1, 0, 2)).reshape(r, nb * c)


def _disc_scalars(lr, li, ldt):
    dt = jnp.exp(ldt)
    mag = jnp.exp(lr * dt)
    ar = mag * jnp.cos(li * dt)
    ai = mag * jnp.sin(li * dt)
    den = lr * lr + li * li
    nr = ar - 1.0
    fr = (nr * lr + ai * li) / den
    fi = (ai * lr - nr * li) / den
    return ar, ai, fr, fi


def _disc_cols(lr, li, ldt, b_re, b_im):
    ar, ai, fr, fi = _disc_scalars(lr, li, ldt)
    return ar, ai, fr * b_re - fi * b_im, fr * b_im + fi * b_re


def _local_step(x, p, target, src, small, disc, distributed=True):
    full = {} if distributed else dict(src)
    gw, halves, pairs, got = {}, {}, {}, {}

    def gather(keys):
        return (_Gather([src[k] for k in keys]), keys, full) if distributed else None

    def swap(keys):
        return (_PairSwap([gw[k] for k in keys]), keys, halves) if distributed else None

    def chips(keys):
        return (_ChipExchange([pairs[k] for k in keys]), keys, got) if distributed else None

    def add_pairs(keys):
        if distributed:
            pairs.update(zip(keys, _pair_add([gw[k] for k in keys], [halves[k] for k in keys], "pair_add_" + keys[0])))

    def carry(fn, *args, jobs=()):
        jobs = [j for j in jobs if j is not None]
        if not jobs:
            return fn(*args)
        comm = jobs[0][0]
        for j in jobs[1:]:
            comm = _Both(comm, j[0])
        res, extra = fn(*args, comm=comm)
        for job, keys, sink in jobs:
            sink.update(zip(keys, extra[:len(job.out_shape)]))
            extra = extra[len(job.out_shape):]
        return res

    d = x.shape[1]
    g, n, pch = small["s5_b_re"].shape
    heads = small["w_r"].shape[0]
    sa, lw = g * pch, small["lru_lambda"].shape[-1]
    widths = [sa, lw, d, d]
    row = lambda v: v.reshape(1, -1)

    hd = small["w_r"].shape[-1]
    ar_row, ai_row, bbr_blk, bbi_blk, cre_blk, cimn_blk, wr_blk, wi_blk = carry(
        _prep, *disc["rows"], *disc["cols"], disc["b_re"], disc["b_im"], small["s5_c_re"].reshape(sa, n),
        small["s5_c_im"].reshape(sa, n), small["w_r"].reshape(lw, hd), small["w_i"].reshape(lw, hd),
        jobs=[gather(["w_in"])])
    d_row = row(small["s5_d"])
    u_a, u_b, za, zb = carry(_inproj_fwd, x, row(small["g_mix"]), full["w_in"], row(small["b_in"]), widths,
                             jobs=[gather(["w_glu", "conv_w", "w_a_out", "w_b_out"])])
    conv_w = _unblock(full["conv_w"], 1) if distributed else full["conv_w"]
    sr, si, y, y_a = carry(_s5_fwd, u_a, bbr_blk, bbi_blk, ar_row, ai_row, cre_blk, cimn_blk, d_row, full["w_glu"],
                           row(small["b_glu"]), jobs=[gather(["w_ffn_gate", "w_o"])])
    xc, h, hprev = carry(_lru_fwd, u_b, conv_w, row(small["conv_b"]), wr_blk, row(small["b_r"]), wi_blk,
                         row(small["b_i"]), row(small["lru_lambda"]), jobs=[gather(["w_ffn_up"])])
    x1, merged, ma, mb = carry(_merge_fwd, y_a, h, za, zb, x, full["w_a_out"], full["w_b_out"], full["w_o"],
                               jobs=[gather(["w_ffn_down"])])
    fg, fu = carry(_ffn_up_fwd, x1, row(small["g_ffn"]), full["w_ffn_gate"], full["w_ffn_up"],
                   jobs=[gather(["w_ple_gate", "w_ple"])])
    x2 = _ffn_down_fwd(fg, fu, x1, full["w_ffn_down"])

    dx2, loss_blk, gw["w_ple_gate"], gw["w_ple"], vec_tail = _tail_fwd_bwd(
        x2, p, target, row(small["g_ple_gate"]), full["w_ple_gate"], row(small["b_ple_gate"]), full["w_ple"],
        row(small["g_ple"]), row(small["g_final"]))
    dfg, dfu, act = carry(_ffn_bwd_a, dx2, fg, fu, full["w_ffn_down"], jobs=[swap(["w_ple_gate", "w_ple"])])
    tn_d = min(d, 512)
    gw["w_ffn_down"] = _matmul_tn(act, dx2, tn_d, "dw_ffn_down", BF16)
    add_pairs(["w_ple_gate", "w_ple"])
    dx1, h2, vec_ffn = carry(_ffn_bwd_b, dfg, dfu, x1, dx2, row(small["g_ffn"]), full["w_ffn_gate"], full["w_ffn_up"],
                             jobs=[chips(["w_ple_gate", "w_ple"]), swap(["w_ffn_down"])])
    gw["w_ffn_gate"] = _matmul_tn(dfg, h2, tn_d, "dw_ffn_gate", BF16)
    gw["w_ffn_up"] = _matmul_tn(dfu, h2, tn_d, "dw_ffn_up", BF16)
    add_pairs(["w_ffn_down"])
    dza, dzb, dya, dyb, gw["w_o"], gw["w_a_out"], gw["w_b_out"] = carry(
        _merge_bwd, dx1, merged, ma, mb, za, zb, y_a, h, full["w_o"], full["w_a_out"], full["w_b_out"],
        jobs=[chips(["w_ffn_down"]), swap(["w_ffn_gate", "w_ffn_up"])])
    add_pairs(["w_ffn_gate", "w_ffn_up"])
    du_b, dw_r, dw_i, vec_lru = carry(
        _lru_bwd, dyb, xc, hprev, u_b, conv_w, wr_blk, row(small["b_r"]), wi_blk, row(small["b_i"]),
        row(small["lru_lambda"]), hd, jobs=[chips(["w_ffn_gate", "w_ffn_up"]), swap(["w_o", "w_a_out", "w_b_out"])])
    add_pairs(["w_o", "w_a_out", "w_b_out"])
    du_a, lam_r, lam_i, dy16, gw["w_glu"], vec_sa, vec_gn = carry(
        _s5_bwd, dya, y, sr, si, u_a, full["w_glu"], row(small["b_glu"]), cre_blk, cimn_blk, bbr_blk, bbi_blk, ar_row,
        ai_row, d_row, jobs=[chips(["w_o", "w_a_out", "w_b_out"])])
    smalls = {"vec_tail": vec_tail, "vec_ffn": vec_ffn, "vec_lru": vec_lru, "vec_sa": vec_sa, "vec_gn": vec_gn,
              "dw_r": dw_r, "dw_i": dw_i, "loss": loss_blk}
    everyones = {}

    def gather_smalls(keys):
        return (_Gather([smalls[k] for k in keys]), keys, everyones) if distributed else None

    smalls["dbb_re"], smalls["dbb_im"], smalls["dc_re"], smalls["dc_imn"] = carry(
        _s5_param_grads, lam_r, lam_i, sr, si, u_a, dy16, pch, n, jobs=[swap(["w_glu"]), gather_smalls(list(smalls))])
    add_pairs(["w_glu"])
    grad_x, h0, dz16, smalls["vec_mix"], smalls["vec_bin"] = _inproj_bwd(
        [du_a, du_b, dza, dzb], x, dx1, row(small["g_mix"]), full["w_in"])
    shapes = {k: a.shape for k, a in smalls.items()}
    gw["w_in"] = carry(_matmul_tn, dz16, h0, tn_d, "dw_in", BF16,
                       jobs=[chips(["w_glu"]), gather_smalls(["dbb_re", "dbb_im", "dc_re", "dc_imn"])])
    smalls.update(everyones)
    if distributed:
        got["w_in"] = gw["w_in"]
        gw = got
    return grad_x, gw, smalls, shapes


def _disc_inputs(small):
    g, n, pch = small["s5_b_re"].shape
    srcs = (small["lam_re"], small["lam_im"], jnp.repeat(small["log_dt"], n))
    return {"rows": [a.reshape(1, g * n) for a in srcs], "cols": [a.reshape(g * n, 1) for a in srcs],
            "b_re": small["s5_b_re"].reshape(g * n, pch), "b_im": small["s5_b_im"].reshape(g * n, pch)}


def kernel(x, p, g_mix, w_in, b_in, lam_re, lam_im, log_dt, s5_b_re, s5_b_im, s5_c_re, s5_c_im, s5_d, w_glu, b_glu, conv_w, conv_b, w_r, b_r, w_i, b_i, lru_lambda, w_a_out, w_b_out, w_o, g_ffn, w_ffn_gate, w_ffn_up, w_ffn_down, g_ple_gate, w_ple_gate, b_ple_gate, w_ple, g_ple, g_final, loss_target, m_g_mix, m_w_in, m_b_in, m_lam_re, m_lam_im, m_log_dt, m_s5_b_re, m_s5_b_im, m_s5_c_re, m_s5_c_im, m_s5_d, m_w_glu, m_b_glu, m_conv_w, m_conv_b, m_w_r, m_b_r, m_w_i, m_b_i, m_lru_lambda, m_w_a_out, m_w_b_out, m_w_o, m_g_ffn, m_w_ffn_gate, m_w_ffn_up, m_w_ffn_down, m_g_ple_gate, m_w_ple_gate, m_b_ple_gate, m_w_ple, m_g_ple, m_g_final, v_g_mix, v_w_in, v_b_in, v_lam_re, v_lam_im, v_log_dt, v_s5_b_re, v_s5_b_im, v_s5_c_re, v_s5_c_im, v_s5_d, v_w_glu, v_b_glu, v_conv_w, v_conv_b, v_w_r, v_b_r, v_w_i, v_b_i, v_lru_lambda, v_w_a_out, v_w_b_out, v_w_o, v_g_ffn, v_w_ffn_gate, v_w_ffn_up, v_w_ffn_down, v_g_ple_gate, v_w_ple_gate, v_b_ple_gate, v_w_ple, v_g_ple, v_g_final):
    given = dict(locals())
    wts = {k: given[k] for k in WEIGHTS}
    moms = {k: given["m_" + k] for k in WEIGHTS}
    vels = {k: given["v_" + k] for k in WEIGHTS}

    def drop_depth(k, a):
        return a if k == "g_final" else a[0]

    small = {k: drop_depth(k, wts[k]) for k in SMALL}
    shard = {k: wts[k][0] for k in SHARDED}
    names = list(SHARDED)

    def wire(k):
        if k == "conv_w":
            return shard[k]
        return (shard[k].T if k in TRANSPOSED else shard[k]).astype(BF16)

    disc = _disc_inputs(small)
    grad_x, parts, smalls, shapes = _local_step(x[0], p[0, 0], loss_target[0], {k: wire(k) for k in names}, small, disc)

    late = ["vec_mix", "vec_bin"]
    received = _alone(_Both(_PairSwap([parts["w_in"]]), _Gather([smalls[k] for k in late])), "swap_last")
    (pair_in,) = _pair_add([parts["w_in"]], received[:1], "pair_add_w_in")
    smalls.update(zip(late, received[1:]))

    g_small, (parts["w_in"],) = _small_reduce(smalls, shapes, *disc["cols"], disc["b_re"], disc["b_im"],
                                              small["s5_b_re"].shape[0], comm=_ChipExchange([pair_in]))
    loss = g_small.pop("loss")[0, 0]
    cols = shard["conv_w"].shape[1]
    mine = _index((lax.axis_index("x"), lax.axis_index("y"), lax.axis_index("c")))
    parts["conv_w"] = lax.dynamic_slice_in_dim(g_small.pop("conv_w"), mine * cols, cols, axis=1)[None]

    def view(k, a):
        a = a.reshape((1, -1) if k == "g_final" else wts[k].shape)
        return jnp.swapaxes(a, -1, -2) if k in NARROW_LAST else a

    def unview(k, a):
        return (jnp.swapaxes(a, -1, -2) if k in NARROW_LAST else a).reshape(wts[k].shape)

    slots = _adamw_small([view(k, wts[k]) for k in SMALL], [view(k, g_small[k]) for k in SMALL],
                         [view(k, moms[k]) for k in SMALL], [view(k, vels[k]) for k in SMALL])
    small_out = [dict(zip(SMALL, [unview(k, a) for k, a in zip(SMALL, slot)])) for slot in slots]

    big_out = {}
    for k in names:
        if k in LONG_AXIS_MINOR:
            res = _adamw(parts[k], shard[k].T, moms[k][0].T, vels[k][0].T, "adamw_" + k)
            big_out[k] = [a.T for a in res]
        else:
            big_out[k] = _adamw(parts[k], shard[k], moms[k][0], vels[k][0], "adamw_" + k, transposed=k in TRANSPOSED)

    outs = [loss, grad_x[None]]
    for slot in range(4):
        for k in WEIGHTS:
            if k in SHARDED:
                outs.append(big_out[k][slot][None])
            else:
                outs.append(small_out[slot][k])
    return tuple(outs)
```

```python
import math

import jax
import jax.numpy as jnp
from jax import lax
from jax.experimental import pallas as pl
from jax.experimental.pallas import tpu as pltpu

F32 = jnp.float32
BF16 = jnp.bfloat16

EPS = 1e-6
LRU_C = 8.0
CONV_WIDTH = 4
ADAM_LR = 0.001
ADAM_B1 = 0.9
ADAM_B2 = 0.999
ADAM_EPS = 1e-08
ADAM_WD = 0.01
ADAM_STEP = 10

N_DEV = 8
MESH = pl.DeviceIdType.MESH
SUBLANES = 8
LANES = 128
VMEM_LIMIT = 56 * 1024 * 1024
TOKEN_TILE = 256
TIME_CHUNK = 256
S5_SLAB = 128
LRU_SLAB = 256


def _dot(a, b):
    return jnp.dot(a.astype(BF16), b.astype(BF16), preferred_element_type=F32)


def _dot_nt(a, b):
    return lax.dot_general(a.astype(BF16), b.astype(BF16), (((1,), (1,)), ((), ())), preferred_element_type=F32)


def _dot_tn(a, b):
    return lax.dot_general(a.astype(BF16), b.astype(BF16), (((0,), (0,)), ((), ())), preferred_element_type=F32)


def _sigmoid(x):
    return jax.nn.sigmoid(x)


def _rms_stats(x):
    r = lax.rsqrt(jnp.mean(x * x, axis=-1, keepdims=True) + EPS)
    return x * r, r


def _rms_bwd(dy, xhat, r, g):
    dxn = dy * g
    dx = r * (dxn - xhat * jnp.mean(dxn * xhat, axis=-1, keepdims=True))
    return dx, dy * xhat


def _rowsum(v):
    return jnp.sum(v, axis=0, keepdims=True)


def _expm1(x):
    u = jnp.exp(x)
    um1 = u - 1.0
    safe = jnp.where(um1 == 0.0, 1.0, jnp.log(u))
    return jnp.where(um1 == 0.0, x, um1 * x / safe)


def _softplus(x):
    e = jnp.exp(-jnp.abs(x))
    u = 1.0 + e
    um1 = u - 1.0
    safe = jnp.where(um1 == 0.0, 1.0, um1)
    log1p_e = jnp.where(um1 == 0.0, e, jnp.log(u) * e / safe)
    return jnp.maximum(x, 0.0) + log1p_e


_GELU_K = math.sqrt(2.0 / math.pi)
_GELU_C = 0.044715


def _gelu(x):
    return 0.5 * x * (1.0 + jnp.tanh(_GELU_K * (x + _GELU_C * x * x * x)))


def _gelu_grad(x):
    th = jnp.tanh(_GELU_K * (x + _GELU_C * x * x * x))
    return 0.5 * (1.0 + th) + 0.5 * x * (1.0 - th * th) * _GELU_K * (1.0 + 3.0 * _GELU_C * x * x)


def _params(*sem):
    return pltpu.CompilerParams(dimension_semantics=sem, vmem_limit_bytes=VMEM_LIMIT)


def _rows(tm, n):
    return pl.BlockSpec((tm, n), lambda i: (i, 0))


def _rows_rev(tm, n, steps):
    return pl.BlockSpec((tm, n), lambda i: (steps - 1 - i, 0))


def _whole(shape):
    nd = len(shape)
    return pl.BlockSpec(shape, lambda i: (0,) * nd, pipeline_mode=pl.Buffered(1))


def _acc(shape):
    nd = len(shape)
    return pl.BlockSpec(shape, lambda i: (0,) * nd)


def _zero_on_first(*refs):
    @pl.when(pl.program_id(0) == 0)
    def _():
        for r in refs:
            r[...] = jnp.zeros_like(r)


def _inproj_fwd(x, g_mix, w_in_t, b_in, widths, comm=None):
    t, d = x.shape
    n = w_in_t.shape[0]
    tm = min(TOKEN_TILE, t)
    offs = [sum(widths[:i]) for i in range(len(widths) + 1)]

    def body(x_ref, g_ref, w_ref, b_ref, *outs):
        xhat, _ = _rms_stats(x_ref[...])
        h = (xhat * g_ref[...]).astype(BF16)
        for k, o_ref in enumerate(outs):
            lo, hi = offs[k], offs[k + 1]
            o_ref[...] = _dot_nt(h, w_ref[lo:hi, :]) + b_ref[:, lo:hi]

    return _run(
        body, comm, name="inproj_fwd", grid=(t // tm,),
        out_shape=[jax.ShapeDtypeStruct((t, w), F32) for w in widths],
        in_specs=[_rows(tm, d), _whole((1, d)), _whole((n, d)), _whole((1, n))],
        out_specs=[_rows(tm, w) for w in widths],
        semantics="parallel",
    )(x, g_mix, w_in_t, b_in)


def _s5_fwd(u, bbr_blk, bbi_blk, ar, ai, cre_blk, cimn_blk, d_skip, w_glu, b_glu, comm=None):
    t, sa = u.shape
    ns, _, sw = bbr_blk.shape
    gn = ns * sw
    tc = min(TIME_CHUNK, t)

    def body(u_ref, bbr_ref, bbi_ref, ar_ref, ai_ref, cre_ref, cim_ref, d_ref, wg_ref, bg_ref,
             sr_ref, si_ref, y_ref, ya_ref, cr_s, ci_s, sr_s, si_s):
        _zero_on_first(cr_s, ci_s)
        uv = u_ref[...]
        ub = uv.astype(BF16)
        for m in range(ns):
            um = ub[:, m * S5_SLAB:(m + 1) * S5_SLAB]
            sr_s[:, m * sw:(m + 1) * sw] = _dot(um, bbr_ref[m])
            si_s[:, m * sw:(m + 1) * sw] = _dot(um, bbi_ref[m])
        a_r = ar_ref[...]
        a_i = ai_ref[...]

        def step(row, carry):
            c_r, c_i = carry
            at = pl.ds(row, 1)
            n_r = a_r * c_r - a_i * c_i + sr_s[at, :]
            n_i = a_r * c_i + a_i * c_r + si_s[at, :]
            sr_s[at, :] = n_r
            si_s[at, :] = n_i
            return n_r, n_i

        c_r, c_i = lax.fori_loop(0, tc, step, (cr_s[0:1, :], ci_s[0:1, :]), unroll=8)
        cr_s[0:1, :] = c_r
        ci_s[0:1, :] = c_i
        for m in range(ns):
            states, chans = slice(m * sw, (m + 1) * sw), slice(m * S5_SLAB, (m + 1) * S5_SLAB)
            s_r, s_i = sr_s[:, states].astype(BF16), si_s[:, states].astype(BF16)
            sr_ref[:, states] = s_r.astype(sr_ref.dtype)
            si_ref[:, states] = s_i.astype(si_ref.dtype)
            y_ref[:, chans] = _dot(s_r, cre_ref[m]) + _dot(s_i, cim_ref[m]) + d_ref[:, chans] * uv[:, chans]
        y = y_ref[...]
        zz = _gelu(y)
        q = _dot(zz, wg_ref[...]) + bg_ref[...]
        ya_ref[...] = zz * _sigmoid(q)

    return _run(
        body, comm, name="s5_fwd", grid=(t // tc,),
        out_shape=[jax.ShapeDtypeStruct((t, gn), BF16), jax.ShapeDtypeStruct((t, gn), BF16),
                   jax.ShapeDtypeStruct((t, sa), F32), jax.ShapeDtypeStruct((t, sa), F32)],
        in_specs=[_rows(tc, sa), _whole(bbr_blk.shape), _whole(bbi_blk.shape), _whole((1, gn)), _whole((1, gn)),
                  _whole(cre_blk.shape), _whole(cimn_blk.shape), _whole((1, sa)), _whole((sa, sa)), _whole((1, sa))],
        out_specs=[_rows(tc, gn), _rows(tc, gn), _rows(tc, sa), _rows(tc, sa)],
        scratch_shapes=[pltpu.VMEM((SUBLANES, gn), F32), pltpu.VMEM((SUBLANES, gn), F32),
                        pltpu.VMEM((tc, gn), F32), pltpu.VMEM((tc, gn), F32)],
        semantics="arbitrary",
    )(u, bbr_blk, bbi_blk, ar, ai, cre_blk, cimn_blk, d_skip, w_glu, b_glu)


def _slab_dot(x, w_ref, transposed=False):
    dot = _dot_nt if transposed else _dot
    xb = x.astype(BF16)
    return jnp.concatenate([dot(xb[:, j * LRU_SLAB:(j + 1) * LRU_SLAB], w_ref[j]) for j in range(w_ref.shape[0])], axis=1)


def _lru_gates(xc, wr_ref, br_ref, wi_ref, bi_ref, lam_ref):
    r = _sigmoid(_slab_dot(xc, wr_ref) + br_ref[...])
    ig = _sigmoid(_slab_dot(xc, wi_ref) + bi_ref[...])
    sp = _softplus(-lam_ref[...])
    log_a = (-LRU_C * r) * sp
    return r, ig, sp, log_a


def _lru_fwd(u, conv_w, conv_b, wr_blk, b_r, wi_blk, b_i, lru_lambda, comm=None):
    t, w = u.shape
    tc = min(TIME_CHUNK, t)
    halo = SUBLANES

    def body(u_ref, cw_ref, cb_ref, wr_ref, br_ref, wi_ref, bi_ref, lam_ref,
             xc_ref, h_ref, hp_ref, ext_s, a_s, carry_s):
        @pl.when(pl.program_id(0) == 0)
        def _():
            ext_s[0:halo, :] = jnp.zeros((halo, w), F32)
            carry_s[...] = jnp.zeros_like(carry_s)

        ext_s[halo:halo + tc, :] = u_ref[...]
        xc = cb_ref[...]
        for k in range(CONV_WIDTH):
            off = halo - (CONV_WIDTH - 1) + k
            xc = xc + cw_ref[k:k + 1, :] * ext_s[off:off + tc, :]
        ext_s[0:halo, :] = ext_s[tc:tc + halo, :]
        xc_ref[...] = xc
        r, ig, sp, log_a = _lru_gates(xc, wr_ref, br_ref, wi_ref, bi_ref, lam_ref)
        a_s[...] = jnp.exp(log_a)
        h_ref[...] = jnp.sqrt(-_expm1(2.0 * log_a)) * ig * xc

        def step(row, carry):
            at = pl.ds(row, 1)
            hp_ref[at, :] = carry
            nxt = a_s[at, :] * carry + h_ref[at, :]
            h_ref[at, :] = nxt
            return nxt

        carry_s[0:1, :] = lax.fori_loop(0, tc, step, carry_s[0:1, :], unroll=8)

    return _run(
        body, comm, name="lru_fwd", grid=(t // tc,),
        out_shape=[jax.ShapeDtypeStruct((t, w), F32)] * 3,
        in_specs=[_rows(tc, w), _whole((CONV_WIDTH, w)), _whole((1, w)), _whole(wr_blk.shape), _whole((1, w)),
                  _whole(wi_blk.shape), _whole((1, w)), _whole((1, w))],
        out_specs=[_rows(tc, w)] * 3,
        scratch_shapes=[pltpu.VMEM((halo + tc, w), F32), pltpu.VMEM((tc, w), F32), pltpu.VMEM((SUBLANES, w), F32)],
        semantics="arbitrary",
    )(u, conv_w, conv_b, wr_blk, b_r, wi_blk, b_i, lru_lambda)


def _merge_fwd(y_a, h, za, zb, x, w_a_out_t, w_b_out, w_o, comm=None):
    t, d = x.shape
    sa, lw = y_a.shape[1], h.shape[1]
    tm = min(TOKEN_TILE, t)

    def body(ya_ref, h_ref, za_ref, zb_ref, x_ref, wa_ref, wb_ref, wo_ref, x1_ref, mg_ref, ma_ref, mb_ref):
        ma = _dot_nt(ya_ref[...], wa_ref[...])
        mb = _dot(h_ref[...], wb_ref[...])
        merged = _sigmoid(za_ref[...]) * ma + _sigmoid(zb_ref[...]) * mb
        ma_ref[...] = ma.astype(ma_ref.dtype)
        mb_ref[...] = mb.astype(mb_ref.dtype)
        mg_ref[...] = merged.astype(mg_ref.dtype)
        x1_ref[...] = x_ref[...] + _dot(merged, wo_ref[...])

    return _run(
        body, comm, name="merge_fwd", grid=(t // tm,),
        out_shape=[jax.ShapeDtypeStruct((t, d), F32), jax.ShapeDtypeStruct((t, d), BF16),
                   jax.ShapeDtypeStruct((t, d), BF16), jax.ShapeDtypeStruct((t, d), BF16)],
        in_specs=[_rows(tm, sa), _rows(tm, lw), _rows(tm, d), _rows(tm, d), _rows(tm, d),
                  _whole((d, sa)), _whole((lw, d)), _whole((d, d))],
        out_specs=[_rows(tm, d)] * 4,
        semantics="parallel",
    )(y_a, h, za, zb, x, w_a_out_t, w_b_out, w_o)


def _ffn_up_fwd(x1, g_ffn, w_gate_t, w_up_t, comm=None):
    t, d = x1.shape
    f = w_gate_t.shape[0]
    tm = min(TOKEN_TILE, t)

    def body(x_ref, g_ref, wg_ref, wu_ref, fg_ref, fu_ref):
        xhat, _ = _rms_stats(x_ref[...])
        h2 = (xhat * g_ref[...]).astype(BF16)
        fg_ref[...] = _dot_nt(h2, wg_ref[...]).astype(fg_ref.dtype)
        fu_ref[...] = _dot_nt(h2, wu_ref[...]).astype(fu_ref.dtype)

    return _run(
        body, comm, name="ffn_up_fwd", grid=(t // tm,),
        out_shape=[jax.ShapeDtypeStruct((t, f), BF16)] * 2,
        in_specs=[_rows(tm, d), _whole((1, d)), _whole((f, d)), _whole((f, d))],
        out_specs=[_rows(tm, f)] * 2,
        semantics="parallel",
    )(x1, g_ffn, w_gate_t, w_up_t)


def _ffn_down_fwd(fg, fu, x1, w_down, comm=None):
    t, d = x1.shape
    f = fg.shape[1]
    tm = min(TOKEN_TILE, t)

    def body(fg_ref, fu_ref, x_ref, wd_ref, x2_ref):
        fgv = fg_ref[...].astype(F32)
        act = fgv * _sigmoid(fgv) * fu_ref[...].astype(F32)
        x2_ref[...] = x_ref[...] + _dot(act, wd_ref[...])

    return _run(
        body, comm, name="ffn_down_fwd", grid=(t // tm,),
        out_shape=jax.ShapeDtypeStruct((t, d), F32),
        in_specs=[_rows(tm, f), _rows(tm, f), _rows(tm, d), _whole((f, d))],
        out_specs=_rows(tm, d),
        semantics="parallel",
    )(fg, fu, x1, w_down)


def _store_on_last(pairs):
    @pl.when(pl.program_id(0) == pl.num_programs(0) - 1)
    def _():
        for acc, out in pairs:
            out[...] = acc[...].astype(out.dtype)


def _tail_fwd_bwd(x2, p, target, g_pg, w_pg, b_pg, w_ple_t, g_ple, g_final):
    t, d = x2.shape
    pd = p.shape[1]
    tm = min(TOKEN_TILE, t)

    def body(x2_ref, p_ref, tg_ref, gpg_ref, wpg_ref, bpg_ref, wple_ref, gple_ref, gfin_ref,
             dx2_ref, loss_ref, dwpg_out, dwple_out, vec_ref, dwpg_ref, dwple_ref):
        _zero_on_first(loss_ref, dwpg_ref, dwple_ref, vec_ref)
        x2v = x2_ref[...]
        xh2, r2 = _rms_stats(x2v)
        h3 = xh2 * gpg_ref[...]
        gp = _sigmoid(_dot(h3, wpg_ref[...]) + bpg_ref[...])
        pe = _dot_nt(p_ref[...], wple_ref[...])
        peh, r3 = _rms_stats(pe)
        e = peh * gple_ref[...]
        x3 = x2v + gp * e
        xh3, r4 = _rms_stats(x3)
        diff = xh3 * gfin_ref[...] - tg_ref[...]
        loss_ref[...] += 0.5 * jnp.sum(jnp.mean(diff * diff, axis=-1, keepdims=True))
        dy = diff * (1.0 / d)
        dx3, dgfin = _rms_bwd(dy, xh3, r4, gfin_ref[...])
        d_gp = dx3 * e
        d_e = dx3 * gp
        dpe, dgple = _rms_bwd(d_e, peh, r3, gple_ref[...])
        dwple_ref[...] += _dot_tn(dpe, p_ref[...])
        dpre = d_gp * gp * (1.0 - gp)
        dwpg_ref[...] += _dot_tn(h3, dpre)
        dh3 = _dot_nt(dpre, wpg_ref[...])
        dx2n, dgpg = _rms_bwd(dh3, xh2, r2, gpg_ref[...])
        dx2_ref[...] = dx3 + dx2n
        vec_ref[0:1, :] += _rowsum(dpre)
        vec_ref[1:2, :] += _rowsum(dgpg)
        vec_ref[2:3, :] += _rowsum(dgple)
        vec_ref[3:4, :] += _rowsum(dgfin)
        _store_on_last([(dwpg_ref, dwpg_out), (dwple_ref, dwple_out)])

    return pl.pallas_call(
        body, name="tail_fwd_bwd", grid=(t // tm,),
        out_shape=[jax.ShapeDtypeStruct((t, d), F32), jax.ShapeDtypeStruct((SUBLANES, LANES), F32),
                   jax.ShapeDtypeStruct((d, d), BF16), jax.ShapeDtypeStruct((d, pd), BF16),
                   jax.ShapeDtypeStruct((SUBLANES, d), F32)],
        in_specs=[_rows(tm, d), _rows(tm, pd), _rows(tm, d), _whole((1, d)), _whole((d, d)), _whole((1, d)),
                  _whole((d, pd)), _whole((1, d)), _whole((1, d))],
        out_specs=[_rows(tm, d), _acc((SUBLANES, LANES)), _acc((d, d)), _acc((d, pd)), _acc((SUBLANES, d))],
        scratch_shapes=[pltpu.VMEM((d, d), F32), pltpu.VMEM((d, pd), F32)],
        compiler_params=_params("arbitrary"),
    )(x2, p, target, g_pg, w_pg, b_pg, w_ple_t, g_ple, g_final)


def _ffn_bwd_a(dx2, fg, fu, w_down, comm=None):
    t, d = dx2.shape
    f = fg.shape[1]
    tm = min(TOKEN_TILE, t)

    def body(dx_ref, fg_ref, fu_ref, wd_ref, dfg_ref, dfu_ref, act_ref):
        dact = _dot_nt(dx_ref[...], wd_ref[...])
        fgv = fg_ref[...].astype(F32)
        fuv = fu_ref[...].astype(F32)
        sg = _sigmoid(fgv)
        silu = fgv * sg
        dfu_ref[...] = (dact * silu).astype(dfu_ref.dtype)
        dfg_ref[...] = (dact * fuv * (sg * (1.0 + fgv * (1.0 - sg)))).astype(dfg_ref.dtype)
        act_ref[...] = (silu * fuv).astype(act_ref.dtype)

    return _run(
        body, comm, name="ffn_bwd_a", grid=(t // tm,),
        out_shape=[jax.ShapeDtypeStruct((t, f), BF16)] * 3,
        in_specs=[_rows(tm, d), _rows(tm, f), _rows(tm, f), _whole((f, d))],
        out_specs=[_rows(tm, f)] * 3,
        semantics="parallel",
    )(dx2, fg, fu, w_down)


def _ffn_bwd_b(dfg, dfu, x1, dx2, g_ffn, w_gate_t, w_up_t, comm=None):
    t, d = x1.shape
    f = dfg.shape[1]
    tm = min(TOKEN_TILE, t)

    def body(dfg_ref, dfu_ref, x_ref, dx2_ref, g_ref, wg_ref, wu_ref, dx1_ref, h2_ref, vec_ref):
        _zero_on_first(vec_ref)
        dh2 = _dot(dfg_ref[...], wg_ref[...]) + _dot(dfu_ref[...], wu_ref[...])
        xhat, r = _rms_stats(x_ref[...])
        h2_ref[...] = (xhat * g_ref[...]).astype(h2_ref.dtype)
        dxn, dg = _rms_bwd(dh2, xhat, r, g_ref[...])
        dx1_ref[...] = dx2_ref[...] + dxn
        vec_ref[0:1, :] += _rowsum(dg)

    return _run(
        body, comm, name="ffn_bwd_b", grid=(t // tm,),
        out_shape=[jax.ShapeDtypeStruct((t, d), F32), jax.ShapeDtypeStruct((t, d), BF16),
                   jax.ShapeDtypeStruct((SUBLANES, d), F32)],
        in_specs=[_rows(tm, f), _rows(tm, f), _rows(tm, d), _rows(tm, d), _whole((1, d)), _whole((f, d)), _whole((f, d))],
        out_specs=[_rows(tm, d), _rows(tm, d), _acc((SUBLANES, d))],
        semantics="arbitrary",
    )(dfg, dfu, x1, dx2, g_ffn, w_gate_t, w_up_t)


def _matmul_tn(a, b, tn, name, dtype=F32, comm=None):
    t, k = a.shape
    n = b.shape[1]

    def body(a_ref, b_ref, o_ref):
        o_ref[...] = _dot_tn(a_ref[...], b_ref[...]).astype(o_ref.dtype)

    return _run(
        body, comm, name=name, grid=(n // tn,),
        out_shape=jax.ShapeDtypeStruct((k, n), dtype),
        in_specs=[_whole((t, k)), pl.BlockSpec((t, tn), lambda j: (0, j))],
        out_specs=pl.BlockSpec((k, tn), lambda j: (0, j)),
        semantics="parallel",
    )(a, b)


def _merge_bwd(dx1, merged, ma, mb, za, zb, y_a, h, w_o, w_a_out_t, w_b_out, comm=None):
    t, d = dx1.shape
    sa, lw = y_a.shape[1], h.shape[1]
    tm = min(TOKEN_TILE, t)

    def body(dx1_ref, mg_ref, ma_ref, mb_ref, za_ref, zb_ref, ya_ref, h_ref, wo_ref, wa_ref, wb_ref,
             dza_ref, dzb_ref, dya_ref, dyb_ref, dwo_out, dwa_out, dwb_out, dwo_ref, dwa_ref, dwb_ref):
        _zero_on_first(dwo_ref, dwa_ref, dwb_ref)
        dx1v = dx1_ref[...].astype(BF16)
        dmg = _dot_nt(dx1v, wo_ref[...])
        ga = _sigmoid(za_ref[...])
        gb = _sigmoid(zb_ref[...])
        dza_ref[...] = (dmg * ma_ref[...].astype(F32) * ga * (1.0 - ga)).astype(dza_ref.dtype)
        dzb_ref[...] = (dmg * mb_ref[...].astype(F32) * gb * (1.0 - gb)).astype(dzb_ref.dtype)
        dma = (dmg * ga).astype(BF16)
        dmb = (dmg * gb).astype(BF16)
        dya_ref[...] = _dot(dma, wa_ref[...])
        dyb_ref[...] = _dot_nt(dmb, wb_ref[...])
        dwo_ref[...] += _dot_tn(mg_ref[...], dx1v)
        dwa_ref[...] += _dot_tn(dma, ya_ref[...])
        dwb_ref[...] += _dot_tn(h_ref[...], dmb)
        _store_on_last([(dwo_ref, dwo_out), (dwa_ref, dwa_out), (dwb_ref, dwb_out)])

    return _run(
        body, comm, name="merge_bwd", grid=(t // tm,),
        out_shape=[jax.ShapeDtypeStruct((t, d), BF16), jax.ShapeDtypeStruct((t, d), BF16),
                   jax.ShapeDtypeStruct((t, sa), F32), jax.ShapeDtypeStruct((t, lw), F32),
                   jax.ShapeDtypeStruct((d, d), BF16), jax.ShapeDtypeStruct((d, sa), BF16),
                   jax.ShapeDtypeStruct((lw, d), BF16)],
        in_specs=[_rows(tm, d), _rows(tm, d), _rows(tm, d), _rows(tm, d), _rows(tm, d), _rows(tm, d),
                  _rows(tm, sa), _rows(tm, lw), _whole((d, d)), _whole((d, sa)), _whole((lw, d))],
        out_specs=[_rows(tm, d), _rows(tm, d), _rows(tm, sa), _rows(tm, lw), _acc((d, d)), _acc((d, sa)), _acc((lw, d))],
        scratch_shapes=[pltpu.VMEM((d, d), F32), pltpu.VMEM((d, sa), F32), pltpu.VMEM((lw, d), F32)],
        semantics="arbitrary",
    )(dx1, merged, ma, mb, za, zb, y_a, h, w_o, w_a_out_t, w_b_out)


def _fold_diag_blocks(dense, row_group, col_group, row0=0, col0=0):
    r, c = dense.shape
    rows = lax.broadcasted_iota(jnp.int32, (r, c), 0) + row0
    cols = lax.broadcasted_iota(jnp.int32, (r, c), 1) + col0
    kept = jnp.where(rows // row_group == cols // col_group, dense, 0.0)
    pick = (lax.broadcasted_iota(jnp.int32, (row_group, r), 0)
            == lax.broadcasted_iota(jnp.int32, (row_group, r), 1) % row_group).astype(F32)
    return jnp.dot(pick, kept, preferred_element_type=F32, precision=lax.Precision.HIGHEST)


def _lru_bwd(dh, xc, hprev, u, conv_w, wr_blk, b_r, wi_blk, b_i, lru_lambda, head_dim, comm=None):
    t, w = dh.shape
    tc = min(TIME_CHUNK, t)
    steps = t // tc
    halo = SUBLANES
    sub_per_chunk = tc // halo
    slabs = w // LRU_SLAB

    def body(dh_ref, xc_ref, hp_ref, u_ref, uh_ref, cw_ref, wr_ref, br_ref, wi_ref, bi_ref, lam_ref,
             du_ref, dwr_out, dwi_out, vec_ref, lam_s, a_s, dxc_s, uext_s, carry_s, dwr_ref, dwi_ref):
        chunk = steps - 1 - pl.program_id(0)

        @pl.when(pl.program_id(0) == 0)
        def _():
            carry_s[...] = jnp.zeros_like(carry_s)
            dxc_s[tc:tc + halo, :] = jnp.zeros((halo, w), F32)
            dwr_ref[...] = jnp.zeros_like(dwr_ref)
            dwi_ref[...] = jnp.zeros_like(dwi_ref)
            vec_ref[...] = jnp.zeros_like(vec_ref)

        xc = xc_ref[...]
        r, ig, sp, log_a = _lru_gates(xc, wr_ref, br_ref, wi_ref, bi_ref, lam_ref)
        a = jnp.exp(log_a)
        a_s[...] = a

        def step(i, q):
            at = pl.ds(tc - 1 - i, 1)
            lam_row = dh_ref[at, :] + q
            lam_s[at, :] = lam_row
            return a_s[at, :] * lam_row

        carry_s[0:1, :] = lax.fori_loop(0, tc, step, carry_s[0:1, :], unroll=8)
        lam = lam_s[...]
        mult = jnp.sqrt(-_expm1(2.0 * log_a))
        d_log_a = lam * hp_ref[...] * a - (lam * ig * xc) * (a * a) / mult
        d_ig = lam * mult * xc
        dpre_r = (d_log_a * (-LRU_C * sp)) * r * (1.0 - r)
        dpre_i = d_ig * ig * (1.0 - ig)
        dxc = lam * mult * ig + _slab_dot(dpre_r, wr_ref, transposed=True) + _slab_dot(dpre_i, wi_ref, transposed=True)
        xcb, drb, dib = xc.astype(BF16), dpre_r.astype(BF16), dpre_i.astype(BF16)
        for j in range(slabs):
            cols = slice(j * LRU_SLAB, (j + 1) * LRU_SLAB)
            dwr_ref[j] += _dot_tn(drb[:, cols], xcb[:, cols])
            dwi_ref[j] += _dot_tn(dib[:, cols], xcb[:, cols])
        vec_ref[0:1, :] += _rowsum(dxc)
        vec_ref[1:2, :] += _rowsum(dpre_r)
        vec_ref[2:3, :] += _rowsum(dpre_i)
        vec_ref[3:4, :] += _rowsum(d_log_a * (-LRU_C * r)) * (-_sigmoid(-lam_ref[...]))
        dxc_s[0:tc, :] = dxc
        du = cw_ref[CONV_WIDTH - 1:CONV_WIDTH, :] * dxc
        for k in range(CONV_WIDTH - 1):
            off = CONV_WIDTH - 1 - k
            du = du + cw_ref[k:k + 1, :] * dxc_s[off:off + tc, :]
        du_ref[...] = du.astype(du_ref.dtype)
        dxc_s[tc:tc + halo, :] = dxc_s[0:halo, :]
        uext_s[0:halo, :] = jnp.where(chunk > 0, uh_ref[...], 0.0)
        uext_s[halo:halo + tc, :] = u_ref[...]
        for k in range(CONV_WIDTH):
            off = halo - (CONV_WIDTH - 1) + k
            vec_ref[4 + k:5 + k, :] += _rowsum(dxc * uext_s[off:off + tc, :])

        @pl.when(pl.program_id(0) == steps - 1)
        def _():
            for j in range(slabs):
                cols = slice(j * LRU_SLAB, (j + 1) * LRU_SLAB)
                dwr_out[:, cols] = _fold_diag_blocks(dwr_ref[j], head_dim, head_dim)
                dwi_out[:, cols] = _fold_diag_blocks(dwi_ref[j], head_dim, head_dim)

    halo_spec = pl.BlockSpec((halo, w), lambda i: (jnp.maximum((steps - 1 - i) * sub_per_chunk - 1, 0), 0))
    return _run(
        body, comm, name="lru_bwd", grid=(steps,),
        out_shape=[jax.ShapeDtypeStruct((t, w), BF16), jax.ShapeDtypeStruct((head_dim, w), F32),
                   jax.ShapeDtypeStruct((head_dim, w), F32), jax.ShapeDtypeStruct((SUBLANES, w), F32)],
        in_specs=[_rows_rev(tc, w, steps)] * 4 + [halo_spec, _whole((CONV_WIDTH, w)), _whole(wr_blk.shape),
                                                  _whole((1, w)), _whole(wi_blk.shape), _whole((1, w)), _whole((1, w))],
        out_specs=[_rows_rev(tc, w, steps), _acc((head_dim, w)), _acc((head_dim, w)), _acc((SUBLANES, w))],
        scratch_shapes=[pltpu.VMEM((tc, w), F32), pltpu.VMEM((tc, w), F32), pltpu.VMEM((tc + halo, w), F32),
                        pltpu.VMEM((halo + tc, w), F32), pltpu.VMEM((SUBLANES, w), F32),
                        pltpu.VMEM((slabs, LRU_SLAB, LRU_SLAB), F32), pltpu.VMEM((slabs, LRU_SLAB, LRU_SLAB), F32)],
        semantics="arbitrary",
    )(dh, xc, hprev, u, u, conv_w, wr_blk, b_r, wi_blk, b_i, lru_lambda)


def _s5_bwd(dya, y, sr, si, u, w_glu, b_glu, cre_blk, cimn_blk, bbr_blk, bbi_blk, ar, ai, d_skip, comm=None):
    t, sa = dya.shape
    gn = sr.shape[1]
    ns, _, sw = bbr_blk.shape
    tc = min(TIME_CHUNK, t)
    steps = t // tc
    halo = SUBLANES

    def body(dya_ref, y_ref, sr_ref, si_ref, u_ref, wg_ref, bg_ref, cre_ref, cim_ref, bbr_ref, bbi_ref,
             ar_ref, ai_ref, d_ref, du_ref, lr_ref, li_ref, dy_ref, dwg_out, vsa_ref, vgn_ref, gr_s, gi_s, cr_s, ci_s,
             dwg_ref):
        @pl.when(pl.program_id(0) == 0)
        def _():
            cr_s[...] = jnp.zeros_like(cr_s)
            ci_s[...] = jnp.zeros_like(ci_s)
            gr_s[tc:tc + halo, :] = jnp.zeros((halo, gn), F32)
            gi_s[tc:tc + halo, :] = jnp.zeros((halo, gn), F32)
            dwg_ref[...] = jnp.zeros_like(dwg_ref)
            vsa_ref[...] = jnp.zeros_like(vsa_ref)
            vgn_ref[...] = jnp.zeros_like(vgn_ref)

        yv = y_ref[...]
        uv = u_ref[...]
        zz = _gelu(yv)
        sg = _sigmoid(_dot(zz, wg_ref[...]) + bg_ref[...])
        dyav = dya_ref[...]
        dq = dyav * zz * sg * (1.0 - sg)
        dzz = dyav * sg + _dot_nt(dq, wg_ref[...])
        dwg_ref[...] += _dot_tn(zz, dq)
        dy = dzz * _gelu_grad(yv)
        dyb = dy.astype(BF16)
        dy_ref[...] = dyb.astype(dy_ref.dtype)
        vsa_ref[0:1, :] += _rowsum(dq)
        vsa_ref[1:2, :] += _rowsum(dy * uv)
        for m in range(ns):
            dym = dyb[:, m * S5_SLAB:(m + 1) * S5_SLAB]
            gr_s[0:tc, m * sw:(m + 1) * sw] = _dot_nt(dym, cre_ref[m])
            gi_s[0:tc, m * sw:(m + 1) * sw] = _dot_nt(dym, cim_ref[m])
        a_r = ar_ref[...]
        a_i = ai_ref[...]

        def step(i, carry):
            l_r, l_i = carry
            at = pl.ds(tc - 1 - i, 1)
            n_r = gr_s[at, :] + a_r * l_r + a_i * l_i
            n_i = gi_s[at, :] + a_r * l_i - a_i * l_r
            gr_s[at, :] = n_r
            gi_s[at, :] = n_i
            return n_r, n_i

        l_r, l_i = lax.fori_loop(0, tc, step, (cr_s[0:1, :], ci_s[0:1, :]), unroll=8)
        cr_s[0:1, :] = l_r
        ci_s[0:1, :] = l_i
        nxt_r = gr_s[1:tc + 1, :]
        nxt_i = gi_s[1:tc + 1, :]
        srv = sr_ref[...].astype(F32)
        siv = si_ref[...].astype(F32)
        vgn_ref[0:1, :] += _rowsum(nxt_r * srv + nxt_i * siv)
        vgn_ref[1:2, :] += _rowsum(nxt_i * srv - nxt_r * siv)
        lam_r = gr_s[0:tc, :]
        lam_i = gi_s[0:tc, :]
        gr_s[tc:tc + halo, :] = gr_s[0:halo, :]
        gi_s[tc:tc + halo, :] = gi_s[0:halo, :]
        lrb = lam_r.astype(BF16)
        lib = lam_i.astype(BF16)
        lr_ref[...] = lrb.astype(lr_ref.dtype)
        li_ref[...] = lib.astype(li_ref.dtype)
        for m in range(ns):
            states, chans = slice(m * sw, (m + 1) * sw), slice(m * S5_SLAB, (m + 1) * S5_SLAB)
            du_ref[:, chans] = (_dot_nt(lrb[:, states], bbr_ref[m]) + _dot_nt(lib[:, states], bbi_ref[m])
                                + dy[:, chans] * d_ref[:, chans]).astype(du_ref.dtype)
        _store_on_last([(dwg_ref, dwg_out)])

    return _run(
        body, comm, name="s5_bwd", grid=(steps,),
        out_shape=[jax.ShapeDtypeStruct((t, sa), BF16), jax.ShapeDtypeStruct((t, gn), BF16),
                   jax.ShapeDtypeStruct((t, gn), BF16), jax.ShapeDtypeStruct((t, sa), BF16),
                   jax.ShapeDtypeStruct((sa, sa), BF16), jax.ShapeDtypeStruct((SUBLANES, sa), F32),
                   jax.ShapeDtypeStruct((SUBLANES, gn), F32)],
        in_specs=[_rows_rev(tc, sa, steps), _rows_rev(tc, sa, steps), _rows_rev(tc, gn, steps), _rows_rev(tc, gn, steps),
                  _rows_rev(tc, sa, steps), _whole((sa, sa)), _whole((1, sa)), _whole(cre_blk.shape),
                  _whole(cimn_blk.shape), _whole(bbr_blk.shape), _whole(bbi_blk.shape), _whole((1, gn)), _whole((1, gn)),
                  _whole((1, sa))],
        out_specs=[_rows_rev(tc, sa, steps), _rows_rev(tc, gn, steps), _rows_rev(tc, gn, steps), _rows_rev(tc, sa, steps),
                   _acc((sa, sa)), _acc((SUBLANES, sa)), _acc((SUBLANES, gn))],
        scratch_shapes=[pltpu.VMEM((tc + halo, gn), F32), pltpu.VMEM((tc + halo, gn), F32),
                        pltpu.VMEM((SUBLANES, gn), F32), pltpu.VMEM((SUBLANES, gn), F32), pltpu.VMEM((sa, sa), F32)],
        semantics="arbitrary",
    )(dya, y, sr, si, u, w_glu, b_glu, cre_blk, cimn_blk, bbr_blk, bbi_blk, ar, ai, d_skip)


def _inproj_bwd(dparts, x, dx1, g_mix, w_in_t, comm=None):
    t, d = x.shape
    n = w_in_t.shape[0]
    widths = [p.shape[1] for p in dparts]
    offs = [sum(widths[:i]) for i in range(len(widths) + 1)]
    tm = min(TOKEN_TILE, t)
    np_ = len(dparts)

    def body(*refs):
        dz_refs = refs[:np_]
        x_ref, dx1_ref, g_ref, w_ref, gx_ref, h_ref, vd_ref, vn_ref = refs[np_:]
        _zero_on_first(vd_ref, vn_ref)
        dh = jnp.zeros((tm, d), F32)
        for k, r in enumerate(dz_refs):
            lo, hi = offs[k], offs[k + 1]
            dzk = r[...]
            dh = dh + _dot(dzk, w_ref[lo:hi, :])
            vn_ref[0:1, lo:hi] += _rowsum(dzk.astype(F32))
        xhat, r0 = _rms_stats(x_ref[...])
        h_ref[...] = (xhat * g_ref[...]).astype(h_ref.dtype)
        dxn, dg = _rms_bwd(dh, xhat, r0, g_ref[...])
        gx_ref[...] = dx1_ref[...] + dxn
        vd_ref[0:1, :] += _rowsum(dg)

    return _run(
        body, comm, name="inproj_bwd", grid=(t // tm,),
        out_shape=[jax.ShapeDtypeStruct((t, d), F32), jax.ShapeDtypeStruct((t, d), BF16),
                   jax.ShapeDtypeStruct((SUBLANES, d), F32), jax.ShapeDtypeStruct((SUBLANES, n), F32)],
        in_specs=[_rows(tm, w) for w in widths] + [_rows(tm, d), _rows(tm, d), _whole((1, d)), _whole((n, d))],
        out_specs=[_rows(tm, d), _rows(tm, d), _acc((SUBLANES, d)), _acc((SUBLANES, n))],
        semantics="arbitrary",
    )(*dparts, x, dx1, g_mix, w_in_t)


def _dw_from_parts(dparts, h, tn, name, comm=None):
    t, d = h.shape
    widths = [p.shape[1] for p in dparts]
    offs = [sum(widths[:i]) for i in range(len(widths) + 1)]
    np_ = len(dparts)

    def body(*refs):
        h_ref, o_ref = refs[np_], refs[np_ + 1]
        hv = h_ref[...]
        for k, r in enumerate(refs[:np_]):
            o_ref[offs[k]:offs[k + 1], :] = _dot_tn(r[...], hv).astype(o_ref.dtype)

    return _run(
        body, comm, name=name, grid=(d // tn,),
        out_shape=jax.ShapeDtypeStruct((offs[-1], d), BF16),
        in_specs=[_whole(p.shape) for p in dparts] + [pl.BlockSpec((t, tn), lambda j: (0, j))],
        out_specs=pl.BlockSpec((offs[-1], tn), lambda j: (0, j)),
        semantics="parallel",
    )(*dparts, h)


def _prep(lr_row, li_row, ldt_row, lr_col, li_col, ldt_col, b_re, b_im, c_re, c_im, w_r, w_i, comm=None):
    gn, pch = b_re.shape
    sa, n = c_re.shape
    w, hd = w_r.shape
    ns, sw, lsl = sa // S5_SLAB, S5_SLAB * n // pch, w // LRU_SLAB

    def spread_cols(vals, row_group, col_group, width):
        r, k = vals.shape
        tile = (lax.broadcasted_iota(jnp.int32, (k, width), 0) == lax.broadcasted_iota(jnp.int32, (k, width), 1) % k)
        rows = lax.broadcasted_iota(jnp.int32, (r, width), 0) // row_group
        cols = lax.broadcasted_iota(jnp.int32, (r, width), 1) // col_group
        return jnp.where(rows == cols, _dot(vals, tile.astype(BF16)), 0.0)

    def spread_rows(vals, row_group, col_group, height):
        k, c = vals.shape
        tile = (lax.broadcasted_iota(jnp.int32, (height, k), 0) % k == lax.broadcasted_iota(jnp.int32, (height, k), 1))
        rows = lax.broadcasted_iota(jnp.int32, (height, c), 0) // row_group
        cols = lax.broadcasted_iota(jnp.int32, (height, c), 1) // col_group
        return jnp.where(rows == cols, _dot(tile.astype(BF16), vals), 0.0)

    def body(lrr, lir, ldr, lrc, lic, ldc, bre, bim, cre, cim, wr, wi,
             ar_o, ai_o, bbr_o, bbi_o, cre_o, cim_o, wr_o, wi_o):
        ar, ai, _, _ = _disc_scalars(lrr[...], lir[...], ldr[...])
        ar_o[...] = ar
        ai_o[...] = ai
        _, _, bbr, bbi = _disc_cols(lrc[...], lic[...], ldc[...], bre[...], bim[...])
        bbr_t, bbi_t = bbr.T, bbi.T
        for m in range(ns):
            bbr_o[m] = spread_rows(bbr_t[:, m * sw:(m + 1) * sw], pch, n, S5_SLAB).astype(bbr_o.dtype)
            bbi_o[m] = spread_rows(bbi_t[:, m * sw:(m + 1) * sw], pch, n, S5_SLAB).astype(bbi_o.dtype)
            rows = slice(m * S5_SLAB, (m + 1) * S5_SLAB)
            cre_o[m] = spread_rows(cre[rows, :].T, n, pch, sw).astype(cre_o.dtype)
            cim_o[m] = spread_rows(-cim[rows, :].T, n, pch, sw).astype(cim_o.dtype)
        for j in range(lsl):
            rows = slice(j * LRU_SLAB, (j + 1) * LRU_SLAB)
            wr_o[j] = spread_cols(wr[rows, :], hd, hd, LRU_SLAB).astype(wr_o.dtype)
            wi_o[j] = spread_cols(wi[rows, :], hd, hd, LRU_SLAB).astype(wi_o.dtype)

    args = (lr_row, li_row, ldt_row, lr_col, li_col, ldt_col, b_re, b_im, c_re, c_im, w_r, w_i)
    out_shape = [jax.ShapeDtypeStruct((1, gn), F32), jax.ShapeDtypeStruct((1, gn), F32),
                 jax.ShapeDtypeStruct((ns, S5_SLAB, sw), BF16), jax.ShapeDtypeStruct((ns, S5_SLAB, sw), BF16),
                 jax.ShapeDtypeStruct((ns, sw, S5_SLAB), BF16), jax.ShapeDtypeStruct((ns, sw, S5_SLAB), BF16),
                 jax.ShapeDtypeStruct((lsl, LRU_SLAB, LRU_SLAB), BF16),
                 jax.ShapeDtypeStruct((lsl, LRU_SLAB, LRU_SLAB), BF16)]
    return _run(
        body, comm, name="prep", grid=(1,), out_shape=out_shape, in_specs=[_whole(a.shape) for a in args],
        out_specs=[_acc(s.shape) for s in out_shape], semantics="arbitrary",
    )(*args)


def _s5_param_grads(lam_r, lam_i, sr, si, u, dy, pch, n, comm=None):
    t, gn = lam_r.shape
    sa = u.shape[1]
    sw = S5_SLAB * n // pch

    def body(lr_ref, li_ref, sr_ref, si_ref, u_ref, dy_ref, dbr_ref, dbi_ref, dcr_ref, dci_ref):
        uv = u_ref[...]
        dyv = dy_ref[...]
        dbr_ref[...] = _fold_diag_blocks(_dot_tn(uv, lr_ref[...]), pch, n)
        dbi_ref[...] = _fold_diag_blocks(_dot_tn(uv, li_ref[...]), pch, n)
        dcr_ref[...] = _fold_diag_blocks(_dot_tn(sr_ref[...], dyv), n, pch)
        dci_ref[...] = _fold_diag_blocks(_dot_tn(si_ref[...], dyv), n, pch)

    states = pl.BlockSpec((t, sw), lambda m: (0, m))
    chans = pl.BlockSpec((t, S5_SLAB), lambda m: (0, m))
    return _run(
        body, comm, name="s5_param_grads", grid=(sa // S5_SLAB,),
        out_shape=[jax.ShapeDtypeStruct((pch, gn), F32), jax.ShapeDtypeStruct((pch, gn), F32),
                   jax.ShapeDtypeStruct((n, sa), F32), jax.ShapeDtypeStruct((n, sa), F32)],
        in_specs=[states, states, states, states, chans, chans],
        out_specs=[pl.BlockSpec((pch, sw), lambda m: (0, m)), pl.BlockSpec((pch, sw), lambda m: (0, m)),
                   pl.BlockSpec((n, S5_SLAB), lambda m: (0, m)), pl.BlockSpec((n, S5_SLAB), lambda m: (0, m))],
        semantics="parallel",
    )(lam_r, lam_i, sr, si, u, dy)


SMALL_PARTS = ["vec_tail", "vec_ffn", "vec_lru", "vec_mix", "vec_bin", "vec_sa", "vec_gn", "dw_r", "dw_i", "dbb_re",
               "dbb_im", "dc_re", "dc_imn", "loss"]


def _small_reduce(parts, shapes, lr_col, li_col, ldt_col, b_re, b_im, groups, comm=None):
    gn, pch = b_re.shape
    n = gn // groups
    nparts = parts[SMALL_PARTS[0]].size // math.prod(shapes[SMALL_PARTS[0]])
    np_, nout = len(SMALL_PARTS), 24

    def body(*refs):
        ins = refs[:np_]
        lr, li, ldt, bre, bim = refs[np_:np_ + 5]
        outs = refs[np_ + 5:np_ + 5 + nout]
        sums = dict(zip(SMALL_PARTS, refs[np_ + 5 + nout:]))

        @pl.when(pl.program_id(0) == 0)
        def _():
            for k, r in zip(SMALL_PARTS, ins):
                sums[k][...] = r[...]

        @pl.when(pl.program_id(0) > 0)
        def _():
            for k, r in zip(SMALL_PARTS, ins):
                sums[k][...] += r[...]

        @pl.when(pl.program_id(0) == nparts - 1)
        def _():
            finish({k: s[...] for k, s in sums.items()}, lr, li, ldt, bre, bim, *outs)

    def finish(tot, lr, li, ldt, bre, bim, o_loss, o_gmix, o_bin, o_bglu, o_s5d, o_convb, o_br, o_bi, o_lam, o_gffn,
               o_gpg, o_bpg, o_gple, o_gfin, o_wr, o_wi, o_cre, o_cim, o_bre, o_bim, o_lre, o_lim, o_ldt, o_convw):
        o_loss[...] = tot["loss"]
        o_convw[...] = tot["vec_lru"][SUBLANES - CONV_WIDTH:SUBLANES]
        o_bpg[...] = tot["vec_tail"][0:1]
        o_gpg[...] = tot["vec_tail"][1:2]
        o_gple[...] = tot["vec_tail"][2:3]
        o_gfin[...] = tot["vec_tail"][3:4]
        o_gffn[...] = tot["vec_ffn"][0:1]
        o_convb[...] = tot["vec_lru"][0:1]
        o_br[...] = tot["vec_lru"][1:2]
        o_bi[...] = tot["vec_lru"][2:3]
        o_lam[...] = tot["vec_lru"][3:4]
        o_gmix[...] = tot["vec_mix"][0:1]
        o_bin[...] = tot["vec_bin"][0:1]
        o_bglu[...] = tot["vec_sa"][0:1]
        o_s5d[...] = tot["vec_sa"][1:2]
        o_wr[...] = tot["dw_r"].T
        o_wi[...] = tot["dw_i"].T
        o_cre[...] = tot["dc_re"].T
        o_cim[...] = -tot["dc_imn"].T
        d_a = tot["vec_gn"].T
        _, chain = jax.vjp(_disc_cols, lr[...], li[...], ldt[...], bre[...], bim[...])
        d_lr, d_li, d_ldt, d_bre, d_bim = chain((d_a[:, 0:1], d_a[:, 1:2], tot["dbb_re"].T, tot["dbb_im"].T))
        o_lre[...] = d_lr
        o_lim[...] = d_li
        o_bre[...] = d_bre
        o_bim[...] = d_bim
        same = (lax.broadcasted_iota(jnp.int32, (groups, gn), 0)
                == lax.broadcasted_iota(jnp.int32, (groups, gn), 1) // n).astype(F32)
        o_ldt[...] = jnp.dot(same, d_ldt * jnp.ones((1, LANES), F32), preferred_element_type=F32,
                             precision=lax.Precision.HIGHEST)[:, 0:1]

    d = shapes["vec_mix"][1]
    nz = shapes["vec_bin"][1]
    sa = shapes["vec_sa"][1]
    w = shapes["vec_lru"][1]
    row = lambda c: jax.ShapeDtypeStruct((1, c), F32)
    out_shape = [jax.ShapeDtypeStruct(shapes["loss"], F32), row(d), row(nz), row(sa), row(sa), row(w), row(w), row(w),
                 row(w), row(d), row(d), row(d), row(d), row(d),
                 jax.ShapeDtypeStruct(shapes["dw_r"][::-1], F32), jax.ShapeDtypeStruct(shapes["dw_i"][::-1], F32),
                 jax.ShapeDtypeStruct(shapes["dc_re"][::-1], F32), jax.ShapeDtypeStruct(shapes["dc_imn"][::-1], F32),
                 jax.ShapeDtypeStruct((gn, pch), F32), jax.ShapeDtypeStruct((gn, pch), F32),
                 jax.ShapeDtypeStruct((gn, 1), F32), jax.ShapeDtypeStruct((gn, 1), F32),
                 jax.ShapeDtypeStruct((groups, 1), F32), jax.ShapeDtypeStruct((CONV_WIDTH, w), F32)]
    def part_spec(k):
        r, c = shapes[k]
        if parts[k].ndim == 3:
            return pl.BlockSpec((None, r, c), lambda i: (i, 0, 0))
        return pl.BlockSpec((r, c), lambda i: (i, 0))

    outs = _run(
        body, comm, name="small_reduce", grid=(nparts,), out_shape=out_shape,
        in_specs=[part_spec(k) for k in SMALL_PARTS] + [_whole(a.shape) for a in (lr_col, li_col, ldt_col, b_re, b_im)],
        out_specs=[_acc(s.shape) for s in out_shape],
        scratch_shapes=[pltpu.VMEM(shapes[k], F32) for k in SMALL_PARTS],
        semantics="arbitrary",
    )(*[parts[k] for k in SMALL_PARTS], lr_col, li_col, ldt_col, b_re, b_im)
    extra = None
    if comm is not None:
        outs, extra = outs
    names = ["loss", "g_mix", "b_in", "b_glu", "s5_d", "conv_b", "b_r", "b_i", "lru_lambda", "g_ffn", "g_ple_gate",
             "b_ple_gate", "g_ple", "g_final", "w_r", "w_i", "s5_c_re", "s5_c_im", "s5_b_re", "s5_b_im", "lam_re",
             "lam_im", "log_dt", "conv_w"]
    res = dict(zip(names, outs))
    return res if comm is None else (res, extra)


def _adamw_small(ws, gs, ms, vs):
    n = len(ws)

    def body(*refs):
        w_r, g_r, m_r, v_r = (refs[i * n:(i + 1) * n] for i in range(4))
        g_o, d_o, m_o, v_o = (refs[(4 + i) * n:(5 + i) * n] for i in range(4))
        for i in range(n):
            g = g_r[i][...]
            delta, m_new, v_new = _adamw_math(w_r[i][...], g, m_r[i][...], v_r[i][...])
            g_o[i][...] = g
            d_o[i][...] = delta
            m_o[i][...] = m_new
            v_o[i][...] = v_new

    shapes = [jax.ShapeDtypeStruct(a.shape, F32) for a in ws]
    outs = pl.pallas_call(body, name="adamw_small", out_shape=shapes * 4)(*ws, *gs, *ms, *vs)
    return outs[:n], outs[n:2 * n], outs[2 * n:3 * n], outs[3 * n:]


def _adamw_math(w, g, m, v):
    m_new = ADAM_B1 * m + (1.0 - ADAM_B1) * g
    v_new = ADAM_B2 * v + (1.0 - ADAM_B2) * (g * g)
    m_hat = m_new / (1.0 - ADAM_B1 ** ADAM_STEP)
    v_hat = v_new / (1.0 - ADAM_B2 ** ADAM_STEP)
    delta = -ADAM_LR * (m_hat / (jnp.sqrt(v_hat) + ADAM_EPS) + ADAM_WD * w)
    return delta, m_new, v_new


def _row_tile(rows):
    for cand in range(256, 0, -16):
        if rows % cand == 0:
            return cand
    return rows


def _adamw(parts, w, m, v, name, transposed=False):
    rows, cols = w.shape
    npart = parts.shape[0]
    tr = _row_tile(rows)
    if transposed:
        parts_spec = pl.BlockSpec((npart, cols, tr), lambda i: (0, 0, i))
    else:
        parts_spec = pl.BlockSpec((npart, tr, cols), lambda i: (0, i, 0))

    def body(p_ref, w_ref, m_ref, v_ref, g_ref, d_ref, mo_ref, vo_ref):
        g = p_ref[0].astype(F32)
        for k in range(1, npart):
            g = g + p_ref[k].astype(F32)
        if transposed:
            g = g.T
        delta, m_new, v_new = _adamw_math(w_ref[...], g, m_ref[...], v_ref[...])
        g_ref[...] = g
        d_ref[...] = delta
        mo_ref[...] = m_new
        vo_ref[...] = v_new

    return pl.pallas_call(
        body, name=name, grid=(rows // tr,),
        out_shape=[jax.ShapeDtypeStruct((rows, cols), F32)] * 4,
        in_specs=[parts_spec] + [_rows(tr, cols)] * 3,
        out_specs=[_rows(tr, cols)] * 4,
        compiler_params=_params("parallel"),
    )(parts, w, m, v)


def _mesh_position():
    return lax.axis_index("x"), lax.axis_index("y"), lax.axis_index("c")


def _flip(pos, rel):
    x, y, c = pos
    return (1 - x if rel & 4 else x, 1 - y if rel & 2 else y, 1 - c if rel & 1 else c)


def _index(pos):
    return 4 * pos[0] + 2 * pos[1] + pos[2]


_ANY = pl.BlockSpec(memory_space=pl.ANY)
FLAT_ROWS = 32


def _dma_sems(n):
    return [pltpu.SemaphoreType.DMA((n, N_DEV - 1)), pltpu.SemaphoreType.DMA((n, N_DEV - 1)), pltpu.SemaphoreType.DMA((n,))]


def _block_of(ref, idx, rows, flat):
    if flat:
        return ref.at[pl.ds(pl.multiple_of(idx * rows, FLAT_ROWS), rows), :]
    return ref.at[idx]


class _Gather:
    chips = (4, 2, 6)
    rels = frozenset((1, 4, 2, 6))

    def __init__(self, shards):
        self.inputs = list(shards)
        self.flat = [s.shape[0] % FLAT_ROWS == 0 for s in shards]
        self.out_shape = [
            jax.ShapeDtypeStruct((N_DEV * s.shape[0], s.shape[1]) if f else (N_DEV,) + s.shape, s.dtype)
            for s, f in zip(shards, self.flat)]
        self.sems = _dma_sems(len(shards))

    def _copy(self, ins, outs, sems, i, k, block, to, own=False):
        dst = _block_of(outs[i], _index(block), self.inputs[i].shape[0], self.flat[i])
        return pltpu.make_async_remote_copy(
            src_ref=ins[i] if own else dst, dst_ref=dst, send_sem=sems[0].at[i, k], recv_sem=sems[1].at[i, k],
            device_id=to, device_id_type=MESH)

    def _local(self, ins, outs, sems, i, me):
        dst = _block_of(outs[i], _index(me), self.inputs[i].shape[0], self.flat[i])
        return pltpu.make_async_copy(ins[i], dst, sems[2].at[i])

    def _first(self, ins, outs, sems, i, me):
        cps = [self._copy(ins, outs, sems, i, 0, me, _flip(me, 1), own=True)]
        cps += [self._copy(ins, outs, sems, i, 1 + j, me, _flip(me, rel), own=True) for j, rel in enumerate(self.chips)]
        return cps

    def _passed(self, ins, outs, sems, i, j, me):
        return self._copy(ins, outs, sems, i, 4 + j, _flip(me, self.chips[j]), _flip(me, 1))

    def before(self, ins, outs, sems):
        n = len(self.inputs)
        me = _mesh_position()

        @pl.when(pl.program_id(0) == 0)
        def _():
            for i in range(n):
                self._local(ins, outs, sems, i, me).start()
                for cp in self._first(ins, outs, sems, i, me):
                    cp.start()

        @pl.when(pl.program_id(0) == pl.num_programs(0) - 1)
        def _():
            for j, rel in enumerate(self.chips):
                for i in range(n):
                    self._copy(ins, outs, sems, i, 1 + j, _flip(me, rel), me).wait_recv()
                    self._passed(ins, outs, sems, i, j, me).start()

    def after(self, ins, outs, sems):
        n = len(self.inputs)
        me = _mesh_position()
        sibling = _flip(me, 1)

        @pl.when(pl.program_id(0) == pl.num_programs(0) - 1)
        def _():
            for i in range(n):
                self._copy(ins, outs, sems, i, 0, sibling, me).wait_recv()
                for j, rel in enumerate(self.chips):
                    self._copy(ins, outs, sems, i, 4 + j, _flip(sibling, rel), me).wait_recv()
            for i in range(n):
                for cp in self._first(ins, outs, sems, i, me):
                    cp.wait_send()
                for j in range(len(self.chips)):
                    self._passed(ins, outs, sems, i, j, me).wait_send()
                self._local(ins, outs, sems, i, me).wait()


N_CHIPS = N_DEV // 2


def _chip(pos):
    return 2 * pos[0] + pos[1]


class _PairSwap:
    rels = frozenset((1,))

    def __init__(self, arrays):
        self.inputs = list(arrays)
        self.rows = [a.shape[0] // N_DEV for a in arrays]
        for r in self.rows:
            assert r % FLAT_ROWS == 0, r
        self.out_shape = [jax.ShapeDtypeStruct((N_CHIPS, r, a.shape[1]), a.dtype) for a, r in zip(arrays, self.rows)]
        n = len(arrays)
        self.sems = [pltpu.SemaphoreType.DMA((n, N_CHIPS)), pltpu.SemaphoreType.DMA((n, N_CHIPS))]

    def _copy(self, ins, outs, sems, i, j, me):
        sibling = _flip(me, 1)
        return pltpu.make_async_remote_copy(
            src_ref=_block_of(ins[i], 2 * j + sibling[2], self.rows[i], True), dst_ref=outs[i].at[j],
            send_sem=sems[0].at[i, j], recv_sem=sems[1].at[i, j], device_id=sibling, device_id_type=MESH)

    def before(self, ins, outs, sems):
        me = _mesh_position()

        @pl.when(pl.program_id(0) == 0)
        def _():
            for i in range(len(self.inputs)):
                for j in range(N_CHIPS):
                    self._copy(ins, outs, sems, i, j, me).start()

    def after(self, ins, outs, sems):
        me = _mesh_position()

        @pl.when(pl.program_id(0) == pl.num_programs(0) - 1)
        def _():
            for i in range(len(self.inputs)):
                for j in range(N_CHIPS):
                    self._copy(ins, outs, sems, i, j, me).wait()


class _ChipExchange:
    chips = (4, 2, 6)
    rels = frozenset(chips)

    def __init__(self, arrays):
        self.inputs = list(arrays)
        self.out_shape = [jax.ShapeDtypeStruct(a.shape, a.dtype) for a in arrays]
        n = len(arrays)
        self.sems = [pltpu.SemaphoreType.DMA((n, 3)), pltpu.SemaphoreType.DMA((n, 3)), pltpu.SemaphoreType.DMA((n,))]

    def _send(self, ins, outs, sems, i, k, me):
        peer = _flip(me, self.chips[k])
        return pltpu.make_async_remote_copy(
            src_ref=ins[i].at[_chip(peer)], dst_ref=outs[i].at[_chip(me)], send_sem=sems[0].at[i, k],
            recv_sem=sems[1].at[i, k], device_id=peer, device_id_type=MESH)

    def _arrival(self, ins, outs, sems, i, k, me):
        peer = _flip(me, self.chips[k])
        return pltpu.make_async_remote_copy(
            src_ref=ins[i].at[_chip(me)], dst_ref=outs[i].at[_chip(peer)], send_sem=sems[0].at[i, k],
            recv_sem=sems[1].at[i, k], device_id=peer, device_id_type=MESH)

    def _local(self, ins, outs, sems, i, me):
        return pltpu.make_async_copy(ins[i].at[_chip(me)], outs[i].at[_chip(me)], sems[2].at[i])

    def before(self, ins, outs, sems):
        me = _mesh_position()

        @pl.when(pl.program_id(0) == 0)
        def _():
            for i in range(len(self.inputs)):
                self._local(ins, outs, sems, i, me).start()
                for k in range(len(self.chips)):
                    self._send(ins, outs, sems, i, k, me).start()

    def after(self, ins, outs, sems):
        me = _mesh_position()

        @pl.when(pl.program_id(0) == pl.num_programs(0) - 1)
        def _():
            for i in range(len(self.inputs)):
                for k in range(len(self.chips)):
                    self._arrival(ins, outs, sems, i, k, me).wait_recv()
            for i in range(len(self.inputs)):
                for k in range(len(self.chips)):
                    self._send(ins, outs, sems, i, k, me).wait_send()
                self._local(ins, outs, sems, i, me).wait()


def _pair_add(grads, halves, name):
    n = len(grads)

    def body(*refs):
        g_refs, h_refs, o_refs = refs[:n], refs[n:2 * n], refs[2 * n:]
        c = lax.axis_index("c")
        for i in range(n):
            r = h_refs[i].shape[1]
            for j in range(N_CHIPS):
                own = g_refs[i][pl.ds(pl.multiple_of((2 * j + c) * r, FLAT_ROWS), r), :]
                o_refs[i][j] = (own.astype(F32) + h_refs[i][j].astype(F32)).astype(o_refs[i].dtype)

    return pl.pallas_call(
        body, name=name, out_shape=[jax.ShapeDtypeStruct(h.shape, h.dtype) for h in halves],
        compiler_params=pltpu.CompilerParams(vmem_limit_bytes=VMEM_LIMIT),
    )(*grads, *halves)


class _Both:
    def __init__(self, first, second):
        self.jobs = (first, second)
        self.rels = first.rels | second.rels
        self.inputs = first.inputs + second.inputs
        self.out_shape = first.out_shape + second.out_shape
        self.sems = first.sems + second.sems

    def _each(self, ins, outs, sems):
        a = self.jobs[0]
        i, o, s = len(a.inputs), len(a.out_shape), len(a.sems)
        return ((a, ins[:i], outs[:o], sems[:s]), (self.jobs[1], ins[i:], outs[o:], sems[s:]))

    def before(self, ins, outs, sems):
        for job, i, o, s in self._each(ins, outs, sems):
            job.before(i, o, s)

    def after(self, ins, outs, sems):
        for job, i, o, s in self._each(ins, outs, sems):
            job.after(i, o, s)


_COLLECTIVE_IDS = {(1,): 0, (2, 4, 6): 1, (1, 2, 4, 6): 2}


def _entry_barrier(rels):
    @pl.when(pl.program_id(0) == 0)
    def _():
        me = _mesh_position()
        sem = pltpu.get_barrier_semaphore()
        for rel in rels:
            pl.semaphore_signal(sem, inc=1, device_id=_flip(me, rel), device_id_type=MESH)
        pl.semaphore_wait(sem, len(rels))


def _run(body, comm, *, semantics, out_shape, in_specs, out_specs, scratch_shapes=(), **kw):
    if comm is None:
        return pl.pallas_call(body, out_shape=out_shape, in_specs=in_specs, out_specs=out_specs,
                              scratch_shapes=list(scratch_shapes), compiler_params=_params(semantics), **kw)
    single = not isinstance(out_shape, (list, tuple))
    outs = [out_shape] if single else list(out_shape)
    ospecs = [out_specs] if single else list(out_specs)
    counts = [len(in_specs), len(comm.inputs), len(outs), len(comm.out_shape), len(scratch_shapes), len(comm.sems)]
    rels = tuple(sorted(comm.rels))

    def carrying(*refs):
        groups, pos = [], 0
        for c in counts:
            groups.append(refs[pos:pos + c])
            pos += c
        main_in, comm_in, main_out, comm_out, main_scratch, comm_sems = groups
        _entry_barrier(rels)
        comm.before(comm_in, comm_out, comm_sems)
        body(*main_in, *main_out, *main_scratch)
        comm.after(comm_in, comm_out, comm_sems)

    call = pl.pallas_call(
        carrying, out_shape=outs + list(comm.out_shape), in_specs=list(in_specs) + [_ANY] * len(comm.inputs),
        out_specs=ospecs + [_ANY] * len(comm.out_shape), scratch_shapes=list(scratch_shapes) + list(comm.sems),
        compiler_params=pltpu.CompilerParams(dimension_semantics=("arbitrary",), vmem_limit_bytes=VMEM_LIMIT,
                                             collective_id=_COLLECTIVE_IDS[rels]), **kw)

    def apply(*args):
        res = call(*args, *comm.inputs)
        main = res[:len(outs)]
        return (main[0] if single else list(main)), list(res[len(outs):])

    return apply


def _alone(comm, name):
    return _run(lambda: None, comm, semantics="arbitrary", name=name, grid=(1,), out_shape=[], in_specs=[], out_specs=[])()[1]


SHARDED = {"w_in": 1, "w_glu": 0, "conv_w": 1, "w_a_out": 1, "w_b_out": 0, "w_o": 0, "w_ffn_gate": 1, "w_ffn_up": 1,
           "w_ffn_down": 0, "w_ple_gate": 0, "w_ple": 1}
TRANSPOSED = ("w_in", "w_a_out", "w_ffn_gate", "w_ffn_up", "w_ple")
LONG_AXIS_MINOR = ("w_in", "w_ffn_gate", "w_ffn_up")
NARROW_LAST = ("s5_b_re", "s5_b_im", "s5_d")
SMALL = ["g_mix", "b_in", "lam_re", "lam_im", "log_dt", "s5_b_re", "s5_b_im", "s5_c_re", "s5_c_im", "s5_d", "b_glu",
         "conv_b", "w_r", "b_r", "w_i", "b_i", "lru_lambda", "g_ffn", "g_ple_gate", "b_ple_gate", "g_ple", "g_final"]
WEIGHTS = ["g_mix", "w_in", "b_in", "lam_re", "lam_im", "log_dt", "s5_b_re", "s5_b_im", "s5_c_re", "s5_c_im", "s5_d",
           "w_glu", "b_glu", "conv_w", "conv_b", "w_r", "b_r", "w_i", "b_i", "lru_lambda", "w_a_out", "w_b_out", "w_o",
           "g_ffn", "w_ffn_gate", "w_ffn_up", "w_ffn_down", "g_ple_gate", "w_ple_gate", "b_ple_gate", "w_ple", "g_ple",
           "g_final"]


def _unblock(gathered, axis):
    nb, r, c = gathered.shape
    if axis == 0:
        return gathered.reshape(nb * r, c)
    return jnp.transpose(gathered, (1, 0, 2)).reshape(r, nb * c)


def _disc_scalars(lr, li, ldt):
    dt = jnp.exp(ldt)
    mag = jnp.exp(lr * dt)
    ar = mag * jnp.cos(li * dt)
    ai = mag * jnp.sin(li * dt)
    den = lr * lr + li * li
    nr = ar - 1.0
    fr = (nr * lr + ai * li) / den
    fi = (ai * lr - nr * li) / den
    return ar, ai, fr, fi


def _disc_cols(lr, li, ldt, b_re, b_im):
    ar, ai, fr, fi = _disc_scalars(lr, li, ldt)
    return ar, ai, fr * b_re - fi * b_im, fr * b_im + fi * b_re


def _local_step(x, p, target, src, small, disc, distributed=True):
    full = {} if distributed else dict(src)
    gw, halves, pairs, got = {}, {}, {}, {}

    def gather(keys):
        return (_Gather([src[k] for k in keys]), keys, full) if distributed else None

    def swap(keys):
        return (_PairSwap([gw[k] for k in keys]), keys, halves) if distributed else None

    def chips(keys):
        return (_ChipExchange([pairs[k] for k in keys]), keys, got) if distributed else None

    def add_pairs(keys):
        if distributed:
            pairs.update(zip(keys, _pair_add([gw[k] for k in keys], [halves[k] for k in keys], "pair_add_" + keys[0])))

    def carry(fn, *args, jobs=()):
        jobs = [j for j in jobs if j is not None]
        if not jobs:
            return fn(*args)
        comm = jobs[0][0]
        for j in jobs[1:]:
            comm = _Both(comm, j[0])
        res, extra = fn(*args, comm=comm)
        for job, keys, sink in jobs:
            sink.update(zip(keys, extra[:len(job.out_shape)]))
            extra = extra[len(job.out_shape):]
        return res

    d = x.shape[1]
    g, n, pch = small["s5_b_re"].shape
    heads = small["w_r"].shape[0]
    sa, lw = g * pch, small["lru_lambda"].shape[-1]
    widths = [sa, lw, d, d]
    row = lambda v: v.reshape(1, -1)

    hd = small["w_r"].shape[-1]
    ar_row, ai_row, bbr_blk, bbi_blk, cre_blk, cimn_blk, wr_blk, wi_blk = carry(
        _prep, *disc["rows"], *disc["cols"], disc["b_re"], disc["b_im"], small["s5_c_re"].reshape(sa, n),
        small["s5_c_im"].reshape(sa, n), small["w_r"].reshape(lw, hd), small["w_i"].reshape(lw, hd),
        jobs=[gather(["w_in"])])
    d_row = row(small["s5_d"])
    u_a, u_b, za, zb = carry(_inproj_fwd, x, row(small["g_mix"]), full["w_in"], row(small["b_in"]), widths,
                             jobs=[gather(["w_glu", "conv_w", "w_a_out", "w_b_out"])])
    conv_w = _unblock(full["conv_w"], 1) if distributed else full["conv_w"]
    sr, si, y, y_a = carry(_s5_fwd, u_a, bbr_blk, bbi_blk, ar_row, ai_row, cre_blk, cimn_blk, d_row, full["w_glu"],
                           row(small["b_glu"]), jobs=[gather(["w_ffn_gate", "w_o"])])
    xc, h, hprev = carry(_lru_fwd, u_b, conv_w, row(small["conv_b"]), wr_blk, row(small["b_r"]), wi_blk,
                         row(small["b_i"]), row(small["lru_lambda"]), jobs=[gather(["w_ffn_up"])])
    x1, merged, ma, mb = carry(_merge_fwd, y_a, h, za, zb, x, full["w_a_out"], full["w_b_out"], full["w_o"],
                               jobs=[gather(["w_ffn_down"])])
    fg, fu = carry(_ffn_up_fwd, x1, row(small["g_ffn"]), full["w_ffn_gate"], full["w_ffn_up"],
                   jobs=[gather(["w_ple_gate", "w_ple"])])
    x2 = _ffn_down_fwd(fg, fu, x1, full["w_ffn_down"])

    dx2, loss_blk, gw["w_ple_gate"], gw["w_ple"], vec_tail = _tail_fwd_bwd(
        x2, p, target, row(small["g_ple_gate"]), full["w_ple_gate"], row(small["b_ple_gate"]), full["w_ple"],
        row(small["g_ple"]), row(small["g_final"]))
    dfg, dfu, act = carry(_ffn_bwd_a, dx2, fg, fu, full["w_ffn_down"], jobs=[swap(["w_ple_gate", "w_ple"])])
    tn_d = min(d, 512)
    gw["w_ffn_down"] = _matmul_tn(act, dx2, tn_d, "dw_ffn_down", BF16)
    add_pairs(["w_ple_gate", "w_ple"])
    dx1, h2, vec_ffn = carry(_ffn_bwd_b, dfg, dfu, x1, dx2, row(small["g_ffn"]), full["w_ffn_gate"], full["w_ffn_up"],
                             jobs=[chips(["w_ple_gate", "w_ple"]), swap(["w_ffn_down"])])
    gw["w_ffn_gate"] = _matmul_tn(dfg, h2, tn_d, "dw_ffn_gate", BF16)
    gw["w_ffn_up"] = _matmul_tn(dfu, h2, tn_d, "dw_ffn_up", BF16)
    add_pairs(["w_ffn_down"])
    dza, dzb, dya, dyb, gw["w_o"], gw["w_a_out"], gw["w_b_out"] = carry(
        _merge_bwd, dx1, merged, ma, mb, za, zb, y_a, h, full["w_o"], full["w_a_out"], full["w_b_out"],
        jobs=[chips(["w_ffn_down"]), swap(["w_ffn_gate", "w_ffn_up"])])
    add_pairs(["w_ffn_gate", "w_ffn_up"])
    du_b, dw_r, dw_i, vec_lru = carry(
        _lru_bwd, dyb, xc, hprev, u_b, conv_w, wr_blk, row(small["b_r"]), wi_blk, row(small["b_i"]),
        row(small["lru_lambda"]), hd, jobs=[chips(["w_ffn_gate", "w_ffn_up"]), swap(["w_o", "w_a_out", "w_b_out"])])
    add_pairs(["w_o", "w_a_out", "w_b_out"])
    du_a, lam_r, lam_i, dy16, gw["w_glu"], vec_sa, vec_gn = carry(
        _s5_bwd, dya, y, sr, si, u_a, full["w_glu"], row(small["b_glu"]), cre_blk, cimn_blk, bbr_blk, bbi_blk, ar_row,
        ai_row, d_row, jobs=[chips(["w_o", "w_a_out", "w_b_out"])])
    smalls = {"vec_tail": vec_tail, "vec_ffn": vec_ffn, "vec_lru": vec_lru, "vec_sa": vec_sa, "vec_gn": vec_gn,
              "dw_r": dw_r, "dw_i": dw_i, "loss": loss_blk}
    everyones = {}

    def gather_smalls(keys):
        return (_Gather([smalls[k] for k in keys]), keys, everyones) if distributed else None

    smalls["dbb_re"], smalls["dbb_im"], smalls["dc_re"], smalls["dc_imn"] = carry(
        _s5_param_grads, lam_r, lam_i, sr, si, u_a, dy16, pch, n, jobs=[swap(["w_glu"]), gather_smalls(list(smalls))])
    add_pairs(["w_glu"])
    dz = [du_a, du_b, dza, dzb]
    grad_x, h0, smalls["vec_mix"], smalls["vec_bin"] = _inproj_bwd(
        dz, x, dx1, row(small["g_mix"]), full["w_in"])
    shapes = {k: a.shape for k, a in smalls.items()}
    gw["w_in"] = carry(_dw_from_parts, dz, h0, tn_d, "dw_in",
                       jobs=[chips(["w_glu"]), gather_smalls(["dbb_re", "dbb_im", "dc_re", "dc_imn"])])
    smalls.update(everyones)
    if distributed:
        got["w_in"] = gw["w_in"]
        gw = got
    return grad_x, gw, smalls, shapes


def _disc_inputs(small):
    g, n, pch = small["s5_b_re"].shape
    srcs = (small["lam_re"], small["lam_im"], jnp.repeat(small["log_dt"], n))
    return {"rows": [a.reshape(1, g * n) for a in srcs], "cols": [a.reshape(g * n, 1) for a in srcs],
            "b_re": small["s5_b_re"].reshape(g * n, pch), "b_im": small["s5_b_im"].reshape(g * n, pch)}


def kernel(x, p, g_mix, w_in, b_in, lam_re, lam_im, log_dt, s5_b_re, s5_b_im, s5_c_re, s5_c_im, s5_d, w_glu, b_glu, conv_w, conv_b, w_r, b_r, w_i, b_i, lru_lambda, w_a_out, w_b_out, w_o, g_ffn, w_ffn_gate, w_ffn_up, w_ffn_down, g_ple_gate, w_ple_gate, b_ple_gate, w_ple, g_ple, g_final, loss_target, m_g_mix, m_w_in, m_b_in, m_lam_re, m_lam_im, m_log_dt, m_s5_b_re, m_s5_b_im, m_s5_c_re, m_s5_c_im, m_s5_d, m_w_glu, m_b_glu, m_conv_w, m_conv_b, m_w_r, m_b_r, m_w_i, m_b_i, m_lru_lambda, m_w_a_out, m_w_b_out, m_w_o, m_g_ffn, m_w_ffn_gate, m_w_ffn_up, m_w_ffn_down, m_g_ple_gate, m_w_ple_gate, m_b_ple_gate, m_w_ple, m_g_ple, m_g_final, v_g_mix, v_w_in, v_b_in, v_lam_re, v_lam_im, v_log_dt, v_s5_b_re, v_s5_b_im, v_s5_c_re, v_s5_c_im, v_s5_d, v_w_glu, v_b_glu, v_conv_w, v_conv_b, v_w_r, v_b_r, v_w_i, v_b_i, v_lru_lambda, v_w_a_out, v_w_b_out, v_w_o, v_g_ffn, v_w_ffn_gate, v_w_ffn_up, v_w_ffn_down, v_g_ple_gate, v_w_ple_gate, v_b_ple_gate, v_w_ple, v_g_ple, v_g_final):
    given = dict(locals())
    wts = {k: given[k] for k in WEIGHTS}
    moms = {k: given["m_" + k] for k in WEIGHTS}
    vels = {k: given["v_" + k] for k in WEIGHTS}

    def drop_depth(k, a):
        return a if k == "g_final" else a[0]

    small = {k: drop_depth(k, wts[k]) for k in SMALL}
    shard = {k: wts[k][0] for k in SHARDED}
    names = list(SHARDED)

    def wire(k):
        if k == "conv_w":
            return shard[k]
        return (shard[k].T if k in TRANSPOSED else shard[k]).astype(BF16)

    disc = _disc_inputs(small)
    grad_x, parts, smalls, shapes = _local_step(x[0], p[0, 0], loss_target[0], {k: wire(k) for k in names}, small, disc)

    late = ["vec_mix", "vec_bin"]
    received = _alone(_Both(_PairSwap([parts["w_in"]]), _Gather([smalls[k] for k in late])), "swap_last")
    (pair_in,) = _pair_add([parts["w_in"]], received[:1], "pair_add_w_in")
    smalls.update(zip(late, received[1:]))

    g_small, (parts["w_in"],) = _small_reduce(smalls, shapes, *disc["cols"], disc["b_re"], disc["b_im"],
                                              small["s5_b_re"].shape[0], comm=_ChipExchange([pair_in]))
    loss = g_small.pop("loss")[0, 0]
    cols = shard["conv_w"].shape[1]
    mine = _index((lax.axis_index("x"), lax.axis_index("y"), lax.axis_index("c")))
    parts["conv_w"] = lax.dynamic_slice_in_dim(g_small.pop("conv_w"), mine * cols, cols, axis=1)[None]

    def view(k, a):
        a = a.reshape((1, -1) if k == "g_final" else wts[k].shape)
        return jnp.swapaxes(a, -1, -2) if k in NARROW_LAST else a

    def unview(k, a):
        return (jnp.swapaxes(a, -1, -2) if k in NARROW_LAST else a).reshape(wts[k].shape)

    slots = _adamw_small([view(k, wts[k]) for k in SMALL], [view(k, g_small[k]) for k in SMALL],
                         [view(k, moms[k]) for k in SMALL], [view(k, vels[k]) for k in SMALL])
    small_out = [dict(zip(SMALL, [unview(k, a) for k, a in zip(SMALL, slot)])) for slot in slots]

    big_out = {}
    for k in names:
        if k in LONG_AXIS_MINOR:
            res = _adamw(parts[k], shard[k].T, moms[k][0].T, vels[k][0].T, "adamw_" + k)
            big_out[k] = [a.T for a in res]
        else:
            big_out[k] = _adamw(parts[k], shard[k], moms[k][0], vels[k][0], "adamw_" + k, transposed=k in TRANSPOSED)

    outs = [loss, grad_x[None]]
    for slot in range(4):
        for k in WEIGHTS:
            if k in SHARDED:
                outs.append(big_out[k][slot][None])
            else:
                outs.append(small_out[slot][k])
    return tuple(outs)
```

```python
import math

import jax
import jax.numpy as jnp
from jax import lax
from jax.experimental import pallas as pl
from jax.experimental.pallas import tpu as pltpu

F32 = jnp.float32
BF16 = jnp.bfloat16

EPS = 1e-6
LRU_C = 8.0
CONV_WIDTH = 4
ADAM_LR = 0.001
ADAM_B1 = 0.9
ADAM_B2 = 0.999
ADAM_EPS = 1e-08
ADAM_WD = 0.01
ADAM_STEP = 10

N_DEV = 8
MESH = pl.DeviceIdType.MESH
SUBLANES = 8
LANES = 128
VMEM_LIMIT = 56 * 1024 * 1024
TOKEN_TILE = 256
MATMUL_TILE = 512
TIME_CHUNK = 256
S5_SLAB = 128
LRU_SLAB = 256


def _dot(a, b):
    return jnp.dot(a.astype(BF16), b.astype(BF16), preferred_element_type=F32)


def _dot_nt(a, b):
    return lax.dot_general(a.astype(BF16), b.astype(BF16), (((1,), (1,)), ((), ())), preferred_element_type=F32)


def _dot_tn(a, b):
    return lax.dot_general(a.astype(BF16), b.astype(BF16), (((0,), (0,)), ((), ())), preferred_element_type=F32)


def _sigmoid(x):
    return jax.nn.sigmoid(x)


def _rms_stats(x):
    r = lax.rsqrt(jnp.mean(x * x, axis=-1, keepdims=True) + EPS)
    return x * r, r


def _rms_bwd(dy, xhat, r, g):
    dxn = dy * g
    dx = r * (dxn - xhat * jnp.mean(dxn * xhat, axis=-1, keepdims=True))
    return dx, dy * xhat


def _rowsum(v):
    return jnp.sum(v, axis=0, keepdims=True)


def _expm1(x):
    u = jnp.exp(x)
    um1 = u - 1.0
    safe = jnp.where(um1 == 0.0, 1.0, jnp.log(u))
    return jnp.where(um1 == 0.0, x, um1 * x / safe)


def _softplus(x):
    e = jnp.exp(-jnp.abs(x))
    u = 1.0 + e
    um1 = u - 1.0
    safe = jnp.where(um1 == 0.0, 1.0, um1)
    log1p_e = jnp.where(um1 == 0.0, e, jnp.log(u) * e / safe)
    return jnp.maximum(x, 0.0) + log1p_e


_GELU_K = math.sqrt(2.0 / math.pi)
_GELU_C = 0.044715


def _gelu(x):
    return 0.5 * x * (1.0 + jnp.tanh(_GELU_K * (x + _GELU_C * x * x * x)))


def _gelu_grad(x):
    th = jnp.tanh(_GELU_K * (x + _GELU_C * x * x * x))
    return 0.5 * (1.0 + th) + 0.5 * x * (1.0 - th * th) * _GELU_K * (1.0 + 3.0 * _GELU_C * x * x)


def _params(*sem):
    return pltpu.CompilerParams(dimension_semantics=sem, vmem_limit_bytes=VMEM_LIMIT)


def _rows(tm, n):
    return pl.BlockSpec((tm, n), lambda i: (i, 0))


def _rows_rev(tm, n, steps):
    return pl.BlockSpec((tm, n), lambda i: (steps - 1 - i, 0))


def _whole(shape):
    nd = len(shape)
    return pl.BlockSpec(shape, lambda i: (0,) * nd, pipeline_mode=pl.Buffered(1))


def _acc(shape):
    nd = len(shape)
    return pl.BlockSpec(shape, lambda i: (0,) * nd)


def _zero_on_first(*refs):
    @pl.when(pl.program_id(0) == 0)
    def _():
        for r in refs:
            r[...] = jnp.zeros_like(r)


def _inproj_fwd(x, g_mix, w_in_t, b_in, widths, comm=None):
    t, d = x.shape
    n = w_in_t.shape[0]
    tm = min(MATMUL_TILE, t)
    offs = [sum(widths[:i]) for i in range(len(widths) + 1)]

    def body(x_ref, g_ref, w_ref, b_ref, *outs):
        xhat, _ = _rms_stats(x_ref[...])
        h = (xhat * g_ref[...]).astype(BF16)
        for k, o_ref in enumerate(outs):
            lo, hi = offs[k], offs[k + 1]
            o_ref[...] = _dot_nt(h, w_ref[lo:hi, :]) + b_ref[:, lo:hi]

    return _run(
        body, comm, name="inproj_fwd", grid=(t // tm,),
        out_shape=[jax.ShapeDtypeStruct((t, w), F32) for w in widths],
        in_specs=[_rows(tm, d), _whole((1, d)), _whole((n, d)), _whole((1, n))],
        out_specs=[_rows(tm, w) for w in widths],
        semantics="parallel",
    )(x, g_mix, w_in_t, b_in)


def _s5_fwd(u, bbr_blk, bbi_blk, ar, ai, cre_blk, cimn_blk, d_skip, w_glu, b_glu, comm=None):
    t, sa = u.shape
    ns, _, sw = bbr_blk.shape
    gn = ns * sw
    tc = min(TIME_CHUNK, t)

    def body(u_ref, bbr_ref, bbi_ref, ar_ref, ai_ref, cre_ref, cim_ref, d_ref, wg_ref, bg_ref,
             sr_ref, si_ref, y_ref, ya_ref, cr_s, ci_s, sr_s, si_s):
        _zero_on_first(cr_s, ci_s)
        uv = u_ref[...]
        ub = uv.astype(BF16)
        for m in range(ns):
            um = ub[:, m * S5_SLAB:(m + 1) * S5_SLAB]
            sr_s[:, m * sw:(m + 1) * sw] = _dot(um, bbr_ref[m])
            si_s[:, m * sw:(m + 1) * sw] = _dot(um, bbi_ref[m])
        a_r = ar_ref[...]
        a_i = ai_ref[...]

        def step(row, carry):
            c_r, c_i = carry
            at = pl.ds(row, 1)
            n_r = a_r * c_r - a_i * c_i + sr_s[at, :]
            n_i = a_r * c_i + a_i * c_r + si_s[at, :]
            sr_s[at, :] = n_r
            si_s[at, :] = n_i
            return n_r, n_i

        c_r, c_i = lax.fori_loop(0, tc, step, (cr_s[0:1, :], ci_s[0:1, :]), unroll=8)
        cr_s[0:1, :] = c_r
        ci_s[0:1, :] = c_i
        for m in range(ns):
            states, chans = slice(m * sw, (m + 1) * sw), slice(m * S5_SLAB, (m + 1) * S5_SLAB)
            s_r, s_i = sr_s[:, states].astype(BF16), si_s[:, states].astype(BF16)
            sr_ref[:, states] = s_r.astype(sr_ref.dtype)
            si_ref[:, states] = s_i.astype(si_ref.dtype)
            y_ref[:, chans] = _dot(s_r, cre_ref[m]) + _dot(s_i, cim_ref[m]) + d_ref[:, chans] * uv[:, chans]
        y = y_ref[...]
        zz = _gelu(y)
        q = _dot(zz, wg_ref[...]) + bg_ref[...]
        ya_ref[...] = zz * _sigmoid(q)

    return _run(
        body, comm, name="s5_fwd", grid=(t // tc,),
        out_shape=[jax.ShapeDtypeStruct((t, gn), BF16), jax.ShapeDtypeStruct((t, gn), BF16),
                   jax.ShapeDtypeStruct((t, sa), F32), jax.ShapeDtypeStruct((t, sa), F32)],
        in_specs=[_rows(tc, sa), _whole(bbr_blk.shape), _whole(bbi_blk.shape), _whole((1, gn)), _whole((1, gn)),
                  _whole(cre_blk.shape), _whole(cimn_blk.shape), _whole((1, sa)), _whole((sa, sa)), _whole((1, sa))],
        out_specs=[_rows(tc, gn), _rows(tc, gn), _rows(tc, sa), _rows(tc, sa)],
        scratch_shapes=[pltpu.VMEM((SUBLANES, gn), F32), pltpu.VMEM((SUBLANES, gn), F32),
                        pltpu.VMEM((tc, gn), F32), pltpu.VMEM((tc, gn), F32)],
        semantics="arbitrary",
    )(u, bbr_blk, bbi_blk, ar, ai, cre_blk, cimn_blk, d_skip, w_glu, b_glu)


def _slab_dot(x, w_ref, transposed=False):
    dot = _dot_nt if transposed else _dot
    xb = x.astype(BF16)
    return jnp.concatenate([dot(xb[:, j * LRU_SLAB:(j + 1) * LRU_SLAB], w_ref[j]) for j in range(w_ref.shape[0])], axis=1)


def _lru_gates(xc, wr_ref, br_ref, wi_ref, bi_ref, lam_ref):
    r = _sigmoid(_slab_dot(xc, wr_ref) + br_ref[...])
    ig = _sigmoid(_slab_dot(xc, wi_ref) + bi_ref[...])
    sp = _softplus(-lam_ref[...])
    log_a = (-LRU_C * r) * sp
    return r, ig, sp, log_a


def _lru_fwd(u, conv_w, conv_b, wr_blk, b_r, wi_blk, b_i, lru_lambda, comm=None):
    t, w = u.shape
    tc = min(TIME_CHUNK, t)
    halo = SUBLANES

    def body(u_ref, cw_ref, cb_ref, wr_ref, br_ref, wi_ref, bi_ref, lam_ref,
             xc_ref, h_ref, hp_ref, ext_s, a_s, carry_s):
        @pl.when(pl.program_id(0) == 0)
        def _():
            ext_s[0:halo, :] = jnp.zeros((halo, w), F32)
            carry_s[...] = jnp.zeros_like(carry_s)

        ext_s[halo:halo + tc, :] = u_ref[...]
        xc = cb_ref[...]
        for k in range(CONV_WIDTH):
            off = halo - (CONV_WIDTH - 1) + k
            xc = xc + cw_ref[k:k + 1, :] * ext_s[off:off + tc, :]
        ext_s[0:halo, :] = ext_s[tc:tc + halo, :]
        xc_ref[...] = xc
        r, ig, sp, log_a = _lru_gates(xc, wr_ref, br_ref, wi_ref, bi_ref, lam_ref)
        a_s[...] = jnp.exp(log_a)
        h_ref[...] = jnp.sqrt(-_expm1(2.0 * log_a)) * ig * xc

        def step(row, carry):
            at = pl.ds(row, 1)
            hp_ref[at, :] = carry
            nxt = a_s[at, :] * carry + h_ref[at, :]
            h_ref[at, :] = nxt
            return nxt

        carry_s[0:1, :] = lax.fori_loop(0, tc, step, carry_s[0:1, :], unroll=8)

    return _run(
        body, comm, name="lru_fwd", grid=(t // tc,),
        out_shape=[jax.ShapeDtypeStruct((t, w), F32)] * 3,
        in_specs=[_rows(tc, w), _whole((CONV_WIDTH, w)), _whole((1, w)), _whole(wr_blk.shape), _whole((1, w)),
                  _whole(wi_blk.shape), _whole((1, w)), _whole((1, w))],
        out_specs=[_rows(tc, w)] * 3,
        scratch_shapes=[pltpu.VMEM((halo + tc, w), F32), pltpu.VMEM((tc, w), F32), pltpu.VMEM((SUBLANES, w), F32)],
        semantics="arbitrary",
    )(u, conv_w, conv_b, wr_blk, b_r, wi_blk, b_i, lru_lambda)


def _merge_fwd(y_a, h, za, zb, x, w_a_out_t, w_b_out, w_o, comm=None):
    t, d = x.shape
    sa, lw = y_a.shape[1], h.shape[1]
    tm = min(TOKEN_TILE, t)

    def body(ya_ref, h_ref, za_ref, zb_ref, x_ref, wa_ref, wb_ref, wo_ref, x1_ref, mg_ref, ma_ref, mb_ref):
        ma = _dot_nt(ya_ref[...], wa_ref[...])
        mb = _dot(h_ref[...], wb_ref[...])
        merged = _sigmoid(za_ref[...]) * ma + _sigmoid(zb_ref[...]) * mb
        ma_ref[...] = ma.astype(ma_ref.dtype)
        mb_ref[...] = mb.astype(mb_ref.dtype)
        mg_ref[...] = merged.astype(mg_ref.dtype)
        x1_ref[...] = x_ref[...] + _dot(merged, wo_ref[...])

    return _run(
        body, comm, name="merge_fwd", grid=(t // tm,),
        out_shape=[jax.ShapeDtypeStruct((t, d), F32), jax.ShapeDtypeStruct((t, d), BF16),
                   jax.ShapeDtypeStruct((t, d), BF16), jax.ShapeDtypeStruct((t, d), BF16)],
        in_specs=[_rows(tm, sa), _rows(tm, lw), _rows(tm, d), _rows(tm, d), _rows(tm, d),
                  _whole((d, sa)), _whole((lw, d)), _whole((d, d))],
        out_specs=[_rows(tm, d)] * 4,
        semantics="parallel",
    )(y_a, h, za, zb, x, w_a_out_t, w_b_out, w_o)


def _ffn_up_fwd(x1, g_ffn, w_gate_t, w_up_t, comm=None):
    t, d = x1.shape
    f = w_gate_t.shape[0]
    tm = min(MATMUL_TILE, t)

    def body(x_ref, g_ref, wg_ref, wu_ref, fg_ref, fu_ref):
        xhat, _ = _rms_stats(x_ref[...])
        h2 = (xhat * g_ref[...]).astype(BF16)
        fg_ref[...] = _dot_nt(h2, wg_ref[...]).astype(fg_ref.dtype)
        fu_ref[...] = _dot_nt(h2, wu_ref[...]).astype(fu_ref.dtype)

    return _run(
        body, comm, name="ffn_up_fwd", grid=(t // tm,),
        out_shape=[jax.ShapeDtypeStruct((t, f), BF16)] * 2,
        in_specs=[_rows(tm, d), _whole((1, d)), _whole((f, d)), _whole((f, d))],
        out_specs=[_rows(tm, f)] * 2,
        semantics="parallel",
    )(x1, g_ffn, w_gate_t, w_up_t)


def _ffn_down_fwd(fg, fu, x1, w_down, comm=None):
    t, d = x1.shape
    f = fg.shape[1]
    tm = min(MATMUL_TILE, t)

    def body(fg_ref, fu_ref, x_ref, wd_ref, x2_ref):
        fgv = fg_ref[...].astype(F32)
        act = fgv * _sigmoid(fgv) * fu_ref[...].astype(F32)
        x2_ref[...] = x_ref[...] + _dot(act, wd_ref[...])

    return _run(
        body, comm, name="ffn_down_fwd", grid=(t // tm,),
        out_shape=jax.ShapeDtypeStruct((t, d), F32),
        in_specs=[_rows(tm, f), _rows(tm, f), _rows(tm, d), _whole((f, d))],
        out_specs=_rows(tm, d),
        semantics="parallel",
    )(fg, fu, x1, w_down)


def _store_on_last(pairs):
    @pl.when(pl.program_id(0) == pl.num_programs(0) - 1)
    def _():
        for acc, out in pairs:
            out[...] = acc[...].astype(out.dtype)


def _tail_fwd_bwd(x2, p, target, g_pg, w_pg, b_pg, w_ple_t, g_ple, g_final):
    t, d = x2.shape
    pd = p.shape[1]
    tm = min(TOKEN_TILE, t)

    def body(x2_ref, p_ref, tg_ref, gpg_ref, wpg_ref, bpg_ref, wple_ref, gple_ref, gfin_ref,
             dx2_ref, loss_ref, dwpg_out, dwple_out, vec_ref, dwpg_ref, dwple_ref):
        _zero_on_first(loss_ref, dwpg_ref, dwple_ref, vec_ref)
        x2v = x2_ref[...]
        xh2, r2 = _rms_stats(x2v)
        h3 = xh2 * gpg_ref[...]
        gp = _sigmoid(_dot(h3, wpg_ref[...]) + bpg_ref[...])
        pe = _dot_nt(p_ref[...], wple_ref[...])
        peh, r3 = _rms_stats(pe)
        e = peh * gple_ref[...]
        x3 = x2v + gp * e
        xh3, r4 = _rms_stats(x3)
        diff = xh3 * gfin_ref[...] - tg_ref[...]
        loss_ref[...] += 0.5 * jnp.sum(jnp.mean(diff * diff, axis=-1, keepdims=True))
        dy = diff * (1.0 / d)
        dx3, dgfin = _rms_bwd(dy, xh3, r4, gfin_ref[...])
        d_gp = dx3 * e
        d_e = dx3 * gp
        dpe, dgple = _rms_bwd(d_e, peh, r3, gple_ref[...])
        dwple_ref[...] += _dot_tn(dpe, p_ref[...])
        dpre = d_gp * gp * (1.0 - gp)
        dwpg_ref[...] += _dot_tn(h3, dpre)
        dh3 = _dot_nt(dpre, wpg_ref[...])
        dx2n, dgpg = _rms_bwd(dh3, xh2, r2, gpg_ref[...])
        dx2_ref[...] = dx3 + dx2n
        vec_ref[0:1, :] += _rowsum(dpre)
        vec_ref[1:2, :] += _rowsum(dgpg)
        vec_ref[2:3, :] += _rowsum(dgple)
        vec_ref[3:4, :] += _rowsum(dgfin)
        _store_on_last([(dwpg_ref, dwpg_out), (dwple_ref, dwple_out)])

    return pl.pallas_call(
        body, name="tail_fwd_bwd", grid=(t // tm,),
        out_shape=[jax.ShapeDtypeStruct((t, d), F32), jax.ShapeDtypeStruct((SUBLANES, LANES), F32),
                   jax.ShapeDtypeStruct((d, d), BF16), jax.ShapeDtypeStruct((d, pd), BF16),
                   jax.ShapeDtypeStruct((SUBLANES, d), F32)],
        in_specs=[_rows(tm, d), _rows(tm, pd), _rows(tm, d), _whole((1, d)), _whole((d, d)), _whole((1, d)),
                  _whole((d, pd)), _whole((1, d)), _whole((1, d))],
        out_specs=[_rows(tm, d), _acc((SUBLANES, LANES)), _acc((d, d)), _acc((d, pd)), _acc((SUBLANES, d))],
        scratch_shapes=[pltpu.VMEM((d, d), F32), pltpu.VMEM((d, pd), F32)],
        compiler_params=_params("arbitrary"),
    )(x2, p, target, g_pg, w_pg, b_pg, w_ple_t, g_ple, g_final)


def _ffn_bwd_a(dx2, fg, fu, w_down, comm=None):
    t, d = dx2.shape
    f = fg.shape[1]
    tm = min(MATMUL_TILE, t)

    def body(dx_ref, fg_ref, fu_ref, wd_ref, dfg_ref, dfu_ref, act_ref):
        dact = _dot_nt(dx_ref[...], wd_ref[...])
        fgv = fg_ref[...].astype(F32)
        fuv = fu_ref[...].astype(F32)
        sg = _sigmoid(fgv)
        silu = fgv * sg
        dfu_ref[...] = (dact * silu).astype(dfu_ref.dtype)
        dfg_ref[...] = (dact * fuv * (sg * (1.0 + fgv * (1.0 - sg)))).astype(dfg_ref.dtype)
        act_ref[...] = (silu * fuv).astype(act_ref.dtype)

    return _run(
        body, comm, name="ffn_bwd_a", grid=(t // tm,),
        out_shape=[jax.ShapeDtypeStruct((t, f), BF16)] * 3,
        in_specs=[_rows(tm, d), _rows(tm, f), _rows(tm, f), _whole((f, d))],
        out_specs=[_rows(tm, f)] * 3,
        semantics="parallel",
    )(dx2, fg, fu, w_down)


def _ffn_bwd_b(dfg, dfu, x1, dx2, g_ffn, w_gate_t, w_up_t, comm=None):
    t, d = x1.shape
    f = dfg.shape[1]
    tm = min(MATMUL_TILE, t)

    def body(dfg_ref, dfu_ref, x_ref, dx2_ref, g_ref, wg_ref, wu_ref, dx1_ref, h2_ref, vec_ref):
        _zero_on_first(vec_ref)
        dh2 = _dot(dfg_ref[...], wg_ref[...]) + _dot(dfu_ref[...], wu_ref[...])
        xhat, r = _rms_stats(x_ref[...])
        h2_ref[...] = (xhat * g_ref[...]).astype(h2_ref.dtype)
        dxn, dg = _rms_bwd(dh2, xhat, r, g_ref[...])
        dx1_ref[...] = dx2_ref[...] + dxn
        vec_ref[0:1, :] += _rowsum(dg)

    return _run(
        body, comm, name="ffn_bwd_b", grid=(t // tm,),
        out_shape=[jax.ShapeDtypeStruct((t, d), F32), jax.ShapeDtypeStruct((t, d), BF16),
                   jax.ShapeDtypeStruct((SUBLANES, d), F32)],
        in_specs=[_rows(tm, f), _rows(tm, f), _rows(tm, d), _rows(tm, d), _whole((1, d)), _whole((f, d)), _whole((f, d))],
        out_specs=[_rows(tm, d), _rows(tm, d), _acc((SUBLANES, d))],
        semantics="arbitrary",
    )(dfg, dfu, x1, dx2, g_ffn, w_gate_t, w_up_t)


def _matmul_tn(a, b, tn, name, dtype=F32, comm=None):
    t, k = a.shape
    n = b.shape[1]

    def body(a_ref, b_ref, o_ref):
        o_ref[...] = _dot_tn(a_ref[...], b_ref[...]).astype(o_ref.dtype)

    return _run(
        body, comm, name=name, grid=(n // tn,),
        out_shape=jax.ShapeDtypeStruct((k, n), dtype),
        in_specs=[_whole((t, k)), pl.BlockSpec((t, tn), lambda j: (0, j))],
        out_specs=pl.BlockSpec((k, tn), lambda j: (0, j)),
        semantics="parallel",
    )(a, b)


def _merge_bwd(dx1, merged, ma, mb, za, zb, y_a, h, w_o, w_a_out_t, w_b_out, comm=None):
    t, d = dx1.shape
    sa, lw = y_a.shape[1], h.shape[1]
    tm = min(TOKEN_TILE, t)

    def body(dx1_ref, mg_ref, ma_ref, mb_ref, za_ref, zb_ref, ya_ref, h_ref, wo_ref, wa_ref, wb_ref,
             dza_ref, dzb_ref, dya_ref, dyb_ref, dwo_out, dwa_out, dwb_out, dwo_ref, dwa_ref, dwb_ref):
        _zero_on_first(dwo_ref, dwa_ref, dwb_ref)
        dx1v = dx1_ref[...].astype(BF16)
        dmg = _dot_nt(dx1v, wo_ref[...])
        ga = _sigmoid(za_ref[...])
        gb = _sigmoid(zb_ref[...])
        dza_ref[...] = (dmg * ma_ref[...].astype(F32) * ga * (1.0 - ga)).astype(dza_ref.dtype)
        dzb_ref[...] = (dmg * mb_ref[...].astype(F32) * gb * (1.0 - gb)).astype(dzb_ref.dtype)
        dma = (dmg * ga).astype(BF16)
        dmb = (dmg * gb).astype(BF16)
        dya_ref[...] = _dot(dma, wa_ref[...])
        dyb_ref[...] = _dot_nt(dmb, wb_ref[...])
        dwo_ref[...] += _dot_tn(mg_ref[...], dx1v)
        dwa_ref[...] += _dot_tn(dma, ya_ref[...])
        dwb_ref[...] += _dot_tn(h_ref[...], dmb)
        _store_on_last([(dwo_ref, dwo_out), (dwa_ref, dwa_out), (dwb_ref, dwb_out)])

    return _run(
        body, comm, name="merge_bwd", grid=(t // tm,),
        out_shape=[jax.ShapeDtypeStruct((t, d), BF16), jax.ShapeDtypeStruct((t, d), BF16),
                   jax.ShapeDtypeStruct((t, sa), F32), jax.ShapeDtypeStruct((t, lw), F32),
                   jax.ShapeDtypeStruct((d, d), BF16), jax.ShapeDtypeStruct((d, sa), BF16),
                   jax.ShapeDtypeStruct((lw, d), BF16)],
        in_specs=[_rows(tm, d), _rows(tm, d), _rows(tm, d), _rows(tm, d), _rows(tm, d), _rows(tm, d),
                  _rows(tm, sa), _rows(tm, lw), _whole((d, d)), _whole((d, sa)), _whole((lw, d))],
        out_specs=[_rows(tm, d), _rows(tm, d), _rows(tm, sa), _rows(tm, lw), _acc((d, d)), _acc((d, sa)), _acc((lw, d))],
        scratch_shapes=[pltpu.VMEM((d, d), F32), pltpu.VMEM((d, sa), F32), pltpu.VMEM((lw, d), F32)],
        semantics="arbitrary",
    )(dx1, merged, ma, mb, za, zb, y_a, h, w_o, w_a_out_t, w_b_out)


def _fold_diag_blocks(dense, row_group, col_group, row0=0, col0=0):
    r, c = dense.shape
    rows = lax.broadcasted_iota(jnp.int32, (r, c), 0) + row0
    cols = lax.broadcasted_iota(jnp.int32, (r, c), 1) + col0
    kept = jnp.where(rows // row_group == cols // col_group, dense, 0.0)
    pick = (lax.broadcasted_iota(jnp.int32, (row_group, r), 0)
            == lax.broadcasted_iota(jnp.int32, (row_group, r), 1) % row_group).astype(F32)
    return jnp.dot(pick, kept, preferred_element_type=F32, precision=lax.Precision.HIGHEST)


def _lru_bwd(dh, xc, hprev, u, conv_w, wr_blk, b_r, wi_blk, b_i, lru_lambda, head_dim, comm=None):
    t, w = dh.shape
    tc = min(TIME_CHUNK, t)
    steps = t // tc
    halo = SUBLANES
    sub_per_chunk = tc // halo
    slabs = w // LRU_SLAB

    def body(dh_ref, xc_ref, hp_ref, u_ref, uh_ref, cw_ref, wr_ref, br_ref, wi_ref, bi_ref, lam_ref,
             du_ref, dwr_out, dwi_out, vec_ref, lam_s, a_s, dxc_s, uext_s, carry_s, dwr_ref, dwi_ref):
        chunk = steps - 1 - pl.program_id(0)

        @pl.when(pl.program_id(0) == 0)
        def _():
            carry_s[...] = jnp.zeros_like(carry_s)
            dxc_s[tc:tc + halo, :] = jnp.zeros((halo, w), F32)
            dwr_ref[...] = jnp.zeros_like(dwr_ref)
            dwi_ref[...] = jnp.zeros_like(dwi_ref)
            vec_ref[...] = jnp.zeros_like(vec_ref)

        xc = xc_ref[...]
        r, ig, sp, log_a = _lru_gates(xc, wr_ref, br_ref, wi_ref, bi_ref, lam_ref)
        a = jnp.exp(log_a)
        a_s[...] = a

        def step(i, q):
            at = pl.ds(tc - 1 - i, 1)
            lam_row = dh_ref[at, :] + q
            lam_s[at, :] = lam_row
            return a_s[at, :] * lam_row

        carry_s[0:1, :] = lax.fori_loop(0, tc, step, carry_s[0:1, :], unroll=8)
        lam = lam_s[...]
        mult = jnp.sqrt(-_expm1(2.0 * log_a))
        d_log_a = lam * hp_ref[...] * a - (lam * ig * xc) * (a * a) / mult
        d_ig = lam * mult * xc
        dpre_r = (d_log_a * (-LRU_C * sp)) * r * (1.0 - r)
        dpre_i = d_ig * ig * (1.0 - ig)
        dxc = lam * mult * ig + _slab_dot(dpre_r, wr_ref, transposed=True) + _slab_dot(dpre_i, wi_ref, transposed=True)
        xcb, drb, dib = xc.astype(BF16), dpre_r.astype(BF16), dpre_i.astype(BF16)
        for j in range(slabs):
            cols = slice(j * LRU_SLAB, (j + 1) * LRU_SLAB)
            dwr_ref[j] += _dot_tn(drb[:, cols], xcb[:, cols])
            dwi_ref[j] += _dot_tn(dib[:, cols], xcb[:, cols])
        vec_ref[0:1, :] += _rowsum(dxc)
        vec_ref[1:2, :] += _rowsum(dpre_r)
        vec_ref[2:3, :] += _rowsum(dpre_i)
        vec_ref[3:4, :] += _rowsum(d_log_a * (-LRU_C * r)) * (-_sigmoid(-lam_ref[...]))
        dxc_s[0:tc, :] = dxc
        du = cw_ref[CONV_WIDTH - 1:CONV_WIDTH, :] * dxc
        for k in range(CONV_WIDTH - 1):
            off = CONV_WIDTH - 1 - k
            du = du + cw_ref[k:k + 1, :] * dxc_s[off:off + tc, :]
        du_ref[...] = du.astype(du_ref.dtype)
        dxc_s[tc:tc + halo, :] = dxc_s[0:halo, :]
        uext_s[0:halo, :] = jnp.where(chunk > 0, uh_ref[...], 0.0)
        uext_s[halo:halo + tc, :] = u_ref[...]
        for k in range(CONV_WIDTH):
            off = halo - (CONV_WIDTH - 1) + k
            vec_ref[4 + k:5 + k, :] += _rowsum(dxc * uext_s[off:off + tc, :])

        @pl.when(pl.program_id(0) == steps - 1)
        def _():
            for j in range(slabs):
                cols = slice(j * LRU_SLAB, (j + 1) * LRU_SLAB)
                dwr_out[:, cols] = _fold_diag_blocks(dwr_ref[j], head_dim, head_dim).astype(dwr_out.dtype)
                dwi_out[:, cols] = _fold_diag_blocks(dwi_ref[j], head_dim, head_dim).astype(dwi_out.dtype)

    halo_spec = pl.BlockSpec((halo, w), lambda i: (jnp.maximum((steps - 1 - i) * sub_per_chunk - 1, 0), 0))
    return _run(
        body, comm, name="lru_bwd", grid=(steps,),
        out_shape=[jax.ShapeDtypeStruct((t, w), BF16), jax.ShapeDtypeStruct((head_dim, w), BF16),
                   jax.ShapeDtypeStruct((head_dim, w), BF16), jax.ShapeDtypeStruct((SUBLANES, w), F32)],
        in_specs=[_rows_rev(tc, w, steps)] * 4 + [halo_spec, _whole((CONV_WIDTH, w)), _whole(wr_blk.shape),
                                                  _whole((1, w)), _whole(wi_blk.shape), _whole((1, w)), _whole((1, w))],
        out_specs=[_rows_rev(tc, w, steps), _acc((head_dim, w)), _acc((head_dim, w)), _acc((SUBLANES, w))],
        scratch_shapes=[pltpu.VMEM((tc, w), F32), pltpu.VMEM((tc, w), F32), pltpu.VMEM((tc + halo, w), F32),
                        pltpu.VMEM((halo + tc, w), F32), pltpu.VMEM((SUBLANES, w), F32),
                        pltpu.VMEM((slabs, LRU_SLAB, LRU_SLAB), F32), pltpu.VMEM((slabs, LRU_SLAB, LRU_SLAB), F32)],
        semantics="arbitrary",
    )(dh, xc, hprev, u, u, conv_w, wr_blk, b_r, wi_blk, b_i, lru_lambda)


def _s5_bwd(dya, y, sr, si, u, w_glu, b_glu, cre_blk, cimn_blk, bbr_blk, bbi_blk, ar, ai, d_skip, comm=None):
    t, sa = dya.shape
    gn = sr.shape[1]
    ns, _, sw = bbr_blk.shape
    tc = min(TIME_CHUNK, t)
    steps = t // tc
    halo = SUBLANES

    def body(dya_ref, y_ref, sr_ref, si_ref, u_ref, wg_ref, bg_ref, cre_ref, cim_ref, bbr_ref, bbi_ref,
             ar_ref, ai_ref, d_ref, du_ref, lr_ref, li_ref, dy_ref, dwg_out, vsa_ref, vgn_ref, gr_s, gi_s, cr_s, ci_s,
             dwg_ref):
        @pl.when(pl.program_id(0) == 0)
        def _():
            cr_s[...] = jnp.zeros_like(cr_s)
            ci_s[...] = jnp.zeros_like(ci_s)
            gr_s[tc:tc + halo, :] = jnp.zeros((halo, gn), F32)
            gi_s[tc:tc + halo, :] = jnp.zeros((halo, gn), F32)
            dwg_ref[...] = jnp.zeros_like(dwg_ref)
            vsa_ref[...] = jnp.zeros_like(vsa_ref)
            vgn_ref[...] = jnp.zeros_like(vgn_ref)

        yv = y_ref[...]
        uv = u_ref[...]
        zz = _gelu(yv)
        sg = _sigmoid(_dot(zz, wg_ref[...]) + bg_ref[...])
        dyav = dya_ref[...]
        dq = dyav * zz * sg * (1.0 - sg)
        dzz = dyav * sg + _dot_nt(dq, wg_ref[...])
        dwg_ref[...] += _dot_tn(zz, dq)
        dy = dzz * _gelu_grad(yv)
        dyb = dy.astype(BF16)
        dy_ref[...] = dyb.astype(dy_ref.dtype)
        vsa_ref[0:1, :] += _rowsum(dq)
        vsa_ref[1:2, :] += _rowsum(dy * uv)
        for m in range(ns):
            dym = dyb[:, m * S5_SLAB:(m + 1) * S5_SLAB]
            gr_s[0:tc, m * sw:(m + 1) * sw] = _dot_nt(dym, cre_ref[m])
            gi_s[0:tc, m * sw:(m + 1) * sw] = _dot_nt(dym, cim_ref[m])
        a_r = ar_ref[...]
        a_i = ai_ref[...]

        def step(i, carry):
            l_r, l_i = carry
            at = pl.ds(tc - 1 - i, 1)
            n_r = gr_s[at, :] + a_r * l_r + a_i * l_i
            n_i = gi_s[at, :] + a_r * l_i - a_i * l_r
            gr_s[at, :] = n_r
            gi_s[at, :] = n_i
            return n_r, n_i

        l_r, l_i = lax.fori_loop(0, tc, step, (cr_s[0:1, :], ci_s[0:1, :]), unroll=8)
        cr_s[0:1, :] = l_r
        ci_s[0:1, :] = l_i
        nxt_r = gr_s[1:tc + 1, :]
        nxt_i = gi_s[1:tc + 1, :]
        srv = sr_ref[...].astype(F32)
        siv = si_ref[...].astype(F32)
        vgn_ref[0:1, :] += _rowsum(nxt_r * srv + nxt_i * siv)
        vgn_ref[1:2, :] += _rowsum(nxt_i * srv - nxt_r * siv)
        lam_r = gr_s[0:tc, :]
        lam_i = gi_s[0:tc, :]
        gr_s[tc:tc + halo, :] = gr_s[0:halo, :]
        gi_s[tc:tc + halo, :] = gi_s[0:halo, :]
        lrb = lam_r.astype(BF16)
        lib = lam_i.astype(BF16)
        lr_ref[...] = lrb.astype(lr_ref.dtype)
        li_ref[...] = lib.astype(li_ref.dtype)
        for m in range(ns):
            states, chans = slice(m * sw, (m + 1) * sw), slice(m * S5_SLAB, (m + 1) * S5_SLAB)
            du_ref[:, chans] = (_dot_nt(lrb[:, states], bbr_ref[m]) + _dot_nt(lib[:, states], bbi_ref[m])
                                + dy[:, chans] * d_ref[:, chans]).astype(du_ref.dtype)
        _store_on_last([(dwg_ref, dwg_out)])

    return _run(
        body, comm, name="s5_bwd", grid=(steps,),
        out_shape=[jax.ShapeDtypeStruct((t, sa), BF16), jax.ShapeDtypeStruct((t, gn), BF16),
                   jax.ShapeDtypeStruct((t, gn), BF16), jax.ShapeDtypeStruct((t, sa), BF16),
                   jax.ShapeDtypeStruct((sa, sa), BF16), jax.ShapeDtypeStruct((SUBLANES, sa), F32),
                   jax.ShapeDtypeStruct((SUBLANES, gn), F32)],
        in_specs=[_rows_rev(tc, sa, steps), _rows_rev(tc, sa, steps), _rows_rev(tc, gn, steps), _rows_rev(tc, gn, steps),
                  _rows_rev(tc, sa, steps), _whole((sa, sa)), _whole((1, sa)), _whole(cre_blk.shape),
                  _whole(cimn_blk.shape), _whole(bbr_blk.shape), _whole(bbi_blk.shape), _whole((1, gn)), _whole((1, gn)),
                  _whole((1, sa))],
        out_specs=[_rows_rev(tc, sa, steps), _rows_rev(tc, gn, steps), _rows_rev(tc, gn, steps), _rows_rev(tc, sa, steps),
                   _acc((sa, sa)), _acc((SUBLANES, sa)), _acc((SUBLANES, gn))],
        scratch_shapes=[pltpu.VMEM((tc + halo, gn), F32), pltpu.VMEM((tc + halo, gn), F32),
                        pltpu.VMEM((SUBLANES, gn), F32), pltpu.VMEM((SUBLANES, gn), F32), pltpu.VMEM((sa, sa), F32)],
        semantics="arbitrary",
    )(dya, y, sr, si, u, w_glu, b_glu, cre_blk, cimn_blk, bbr_blk, bbi_blk, ar, ai, d_skip)


def _inproj_bwd(dparts, x, dx1, g_mix, w_in_t, comm=None):
    t, d = x.shape
    n = w_in_t.shape[0]
    widths = [p.shape[1] for p in dparts]
    offs = [sum(widths[:i]) for i in range(len(widths) + 1)]
    tm = min(MATMUL_TILE, t)
    np_ = len(dparts)

    def body(*refs):
        dz_refs = refs[:np_]
        x_ref, dx1_ref, g_ref, w_ref, gx_ref, h_ref, vd_ref, vn_ref = refs[np_:]
        _zero_on_first(vd_ref, vn_ref)
        dh = jnp.zeros((tm, d), F32)
        for k, r in enumerate(dz_refs):
            lo, hi = offs[k], offs[k + 1]
            dzk = r[...]
            dh = dh + _dot(dzk, w_ref[lo:hi, :])
            vn_ref[0:1, lo:hi] += _rowsum(dzk.astype(F32))
        xhat, r0 = _rms_stats(x_ref[...])
        h_ref[...] = (xhat * g_ref[...]).astype(h_ref.dtype)
        dxn, dg = _rms_bwd(dh, xhat, r0, g_ref[...])
        gx_ref[...] = dx1_ref[...] + dxn
        vd_ref[0:1, :] += _rowsum(dg)

    return _run(
        body, comm, name="inproj_bwd", grid=(t // tm,),
        out_shape=[jax.ShapeDtypeStruct((t, d), F32), jax.ShapeDtypeStruct((t, d), BF16),
                   jax.ShapeDtypeStruct((SUBLANES, d), F32), jax.ShapeDtypeStruct((SUBLANES, n), F32)],
        in_specs=[_rows(tm, w) for w in widths] + [_rows(tm, d), _rows(tm, d), _whole((1, d)), _whole((n, d))],
        out_specs=[_rows(tm, d), _rows(tm, d), _acc((SUBLANES, d)), _acc((SUBLANES, n))],
        semantics="arbitrary",
    )(*dparts, x, dx1, g_mix, w_in_t)


def _dw_from_parts(dparts, h, tn, name, comm=None):
    t, d = h.shape
    widths = [p.shape[1] for p in dparts]
    offs = [sum(widths[:i]) for i in range(len(widths) + 1)]
    np_ = len(dparts)

    def body(*refs):
        h_ref, o_ref = refs[np_], refs[np_ + 1]
        hv = h_ref[...]
        for k, r in enumerate(refs[:np_]):
            o_ref[offs[k]:offs[k + 1], :] = _dot_tn(r[...], hv).astype(o_ref.dtype)

    return _run(
        body, comm, name=name, grid=(d // tn,),
        out_shape=jax.ShapeDtypeStruct((offs[-1], d), BF16),
        in_specs=[_whole(p.shape) for p in dparts] + [pl.BlockSpec((t, tn), lambda j: (0, j))],
        out_specs=pl.BlockSpec((offs[-1], tn), lambda j: (0, j)),
        semantics="parallel",
    )(*dparts, h)


def _prep(lr_row, li_row, ldt_row, lr_col, li_col, ldt_col, b_re, b_im, c_re, c_im, w_r, w_i, comm=None):
    gn, pch = b_re.shape
    sa, n = c_re.shape
    w, hd = w_r.shape
    ns, sw, lsl = sa // S5_SLAB, S5_SLAB * n // pch, w // LRU_SLAB

    def spread_cols(vals, row_group, col_group, width):
        r, k = vals.shape
        tile = (lax.broadcasted_iota(jnp.int32, (k, width), 0) == lax.broadcasted_iota(jnp.int32, (k, width), 1) % k)
        rows = lax.broadcasted_iota(jnp.int32, (r, width), 0) // row_group
        cols = lax.broadcasted_iota(jnp.int32, (r, width), 1) // col_group
        return jnp.where(rows == cols, _dot(vals, tile.astype(BF16)), 0.0)

    def spread_rows(vals, row_group, col_group, height):
        k, c = vals.shape
        tile = (lax.broadcasted_iota(jnp.int32, (height, k), 0) % k == lax.broadcasted_iota(jnp.int32, (height, k), 1))
        rows = lax.broadcasted_iota(jnp.int32, (height, c), 0) // row_group
        cols = lax.broadcasted_iota(jnp.int32, (height, c), 1) // col_group
        return jnp.where(rows == cols, _dot(tile.astype(BF16), vals), 0.0)

    def body(lrr, lir, ldr, lrc, lic, ldc, bre, bim, cre, cim, wr, wi,
             ar_o, ai_o, bbr_o, bbi_o, cre_o, cim_o, wr_o, wi_o):
        ar, ai, _, _ = _disc_scalars(lrr[...], lir[...], ldr[...])
        ar_o[...] = ar
        ai_o[...] = ai
        _, _, bbr, bbi = _disc_cols(lrc[...], lic[...], ldc[...], bre[...], bim[...])
        bbr_t, bbi_t = bbr.T, bbi.T
        for m in range(ns):
            bbr_o[m] = spread_rows(bbr_t[:, m * sw:(m + 1) * sw], pch, n, S5_SLAB).astype(bbr_o.dtype)
            bbi_o[m] = spread_rows(bbi_t[:, m * sw:(m + 1) * sw], pch, n, S5_SLAB).astype(bbi_o.dtype)
            rows = slice(m * S5_SLAB, (m + 1) * S5_SLAB)
            cre_o[m] = spread_rows(cre[rows, :].T, n, pch, sw).astype(cre_o.dtype)
            cim_o[m] = spread_rows(-cim[rows, :].T, n, pch, sw).astype(cim_o.dtype)
        for j in range(lsl):
            rows = slice(j * LRU_SLAB, (j + 1) * LRU_SLAB)
            wr_o[j] = spread_cols(wr[rows, :], hd, hd, LRU_SLAB).astype(wr_o.dtype)
            wi_o[j] = spread_cols(wi[rows, :], hd, hd, LRU_SLAB).astype(wi_o.dtype)

    args = (lr_row, li_row, ldt_row, lr_col, li_col, ldt_col, b_re, b_im, c_re, c_im, w_r, w_i)
    out_shape = [jax.ShapeDtypeStruct((1, gn), F32), jax.ShapeDtypeStruct((1, gn), F32),
                 jax.ShapeDtypeStruct((ns, S5_SLAB, sw), BF16), jax.ShapeDtypeStruct((ns, S5_SLAB, sw), BF16),
                 jax.ShapeDtypeStruct((ns, sw, S5_SLAB), BF16), jax.ShapeDtypeStruct((ns, sw, S5_SLAB), BF16),
                 jax.ShapeDtypeStruct((lsl, LRU_SLAB, LRU_SLAB), BF16),
                 jax.ShapeDtypeStruct((lsl, LRU_SLAB, LRU_SLAB), BF16)]
    return _run(
        body, comm, name="prep", grid=(1,), out_shape=out_shape, in_specs=[_whole(a.shape) for a in args],
        out_specs=[_acc(s.shape) for s in out_shape], semantics="arbitrary",
    )(*args)


def _s5_param_grads(lam_r, lam_i, sr, si, u, dy, pch, n, comm=None):
    t, gn = lam_r.shape
    sa = u.shape[1]
    sw = S5_SLAB * n // pch

    def body(lr_ref, li_ref, sr_ref, si_ref, u_ref, dy_ref, dbr_ref, dbi_ref, dcr_ref, dci_ref):
        uv = u_ref[...]
        dyv = dy_ref[...]
        dbr_ref[...] = _fold_diag_blocks(_dot_tn(uv, lr_ref[...]), pch, n).astype(dbr_ref.dtype)
        dbi_ref[...] = _fold_diag_blocks(_dot_tn(uv, li_ref[...]), pch, n).astype(dbi_ref.dtype)
        dcr_ref[...] = _fold_diag_blocks(_dot_tn(sr_ref[...], dyv), n, pch).astype(dcr_ref.dtype)
        dci_ref[...] = _fold_diag_blocks(_dot_tn(si_ref[...], dyv), n, pch).astype(dci_ref.dtype)

    states = pl.BlockSpec((t, sw), lambda m: (0, m))
    chans = pl.BlockSpec((t, S5_SLAB), lambda m: (0, m))
    return _run(
        body, comm, name="s5_param_grads", grid=(sa // S5_SLAB,),
        out_shape=[jax.ShapeDtypeStruct((pch, gn), BF16), jax.ShapeDtypeStruct((pch, gn), BF16),
                   jax.ShapeDtypeStruct((n, sa), BF16), jax.ShapeDtypeStruct((n, sa), BF16)],
        in_specs=[states, states, states, states, chans, chans],
        out_specs=[pl.BlockSpec((pch, sw), lambda m: (0, m)), pl.BlockSpec((pch, sw), lambda m: (0, m)),
                   pl.BlockSpec((n, S5_SLAB), lambda m: (0, m)), pl.BlockSpec((n, S5_SLAB), lambda m: (0, m))],
        semantics="parallel",
    )(lam_r, lam_i, sr, si, u, dy)


SMALL_PARTS = ["vec_tail", "vec_ffn", "vec_lru", "vec_mix", "vec_bin", "vec_sa", "vec_gn", "dw_r", "dw_i", "dbb_re",
               "dbb_im", "dc_re", "dc_imn", "loss"]


def _small_reduce(parts, shapes, lr_col, li_col, ldt_col, b_re, b_im, groups, comm=None):
    gn, pch = b_re.shape
    n = gn // groups
    nparts = parts[SMALL_PARTS[0]].size // math.prod(shapes[SMALL_PARTS[0]])
    np_, nout = len(SMALL_PARTS), 24

    def body(*refs):
        ins = refs[:np_]
        lr, li, ldt, bre, bim = refs[np_:np_ + 5]
        outs = refs[np_ + 5:np_ + 5 + nout]
        sums = dict(zip(SMALL_PARTS, refs[np_ + 5 + nout:]))

        @pl.when(pl.program_id(0) == 0)
        def _():
            for k, r in zip(SMALL_PARTS, ins):
                sums[k][...] = r[...].astype(F32)

        @pl.when(pl.program_id(0) > 0)
        def _():
            for k, r in zip(SMALL_PARTS, ins):
                sums[k][...] += r[...].astype(F32)

        @pl.when(pl.program_id(0) == nparts - 1)
        def _():
            finish({k: s[...] for k, s in sums.items()}, lr, li, ldt, bre, bim, *outs)

    def finish(tot, lr, li, ldt, bre, bim, o_loss, o_gmix, o_bin, o_bglu, o_s5d, o_convb, o_br, o_bi, o_lam, o_gffn,
               o_gpg, o_bpg, o_gple, o_gfin, o_wr, o_wi, o_cre, o_cim, o_bre, o_bim, o_lre, o_lim, o_ldt, o_convw):
        o_loss[...] = tot["loss"]
        o_convw[...] = tot["vec_lru"][SUBLANES - CONV_WIDTH:SUBLANES]
        o_bpg[...] = tot["vec_tail"][0:1]
        o_gpg[...] = tot["vec_tail"][1:2]
        o_gple[...] = tot["vec_tail"][2:3]
        o_gfin[...] = tot["vec_tail"][3:4]
        o_gffn[...] = tot["vec_ffn"][0:1]
        o_convb[...] = tot["vec_lru"][0:1]
        o_br[...] = tot["vec_lru"][1:2]
        o_bi[...] = tot["vec_lru"][2:3]
        o_lam[...] = tot["vec_lru"][3:4]
        o_gmix[...] = tot["vec_mix"][0:1]
        o_bin[...] = tot["vec_bin"][0:1]
        o_bglu[...] = tot["vec_sa"][0:1]
        o_s5d[...] = tot["vec_sa"][1:2]
        o_wr[...] = tot["dw_r"].T
        o_wi[...] = tot["dw_i"].T
        o_cre[...] = tot["dc_re"].T
        o_cim[...] = -tot["dc_imn"].T
        d_a = tot["vec_gn"].T
        _, chain = jax.vjp(_disc_cols, lr[...], li[...], ldt[...], bre[...], bim[...])
        d_lr, d_li, d_ldt, d_bre, d_bim = chain((d_a[:, 0:1], d_a[:, 1:2], tot["dbb_re"].T, tot["dbb_im"].T))
        o_lre[...] = d_lr
        o_lim[...] = d_li
        o_bre[...] = d_bre
        o_bim[...] = d_bim
        same = (lax.broadcasted_iota(jnp.int32, (groups, gn), 0)
                == lax.broadcasted_iota(jnp.int32, (groups, gn), 1) // n).astype(F32)
        o_ldt[...] = jnp.dot(same, d_ldt * jnp.ones((1, LANES), F32), preferred_element_type=F32,
                             precision=lax.Precision.HIGHEST)[:, 0:1]

    d = shapes["vec_mix"][1]
    nz = shapes["vec_bin"][1]
    sa = shapes["vec_sa"][1]
    w = shapes["vec_lru"][1]
    row = lambda c: jax.ShapeDtypeStruct((1, c), F32)
    out_shape = [jax.ShapeDtypeStruct(shapes["loss"], F32), row(d), row(nz), row(sa), row(sa), row(w), row(w), row(w),
                 row(w), row(d), row(d), row(d), row(d), row(d),
                 jax.ShapeDtypeStruct(shapes["dw_r"][::-1], F32), jax.ShapeDtypeStruct(shapes["dw_i"][::-1], F32),
                 jax.ShapeDtypeStruct(shapes["dc_re"][::-1], F32), jax.ShapeDtypeStruct(shapes["dc_imn"][::-1], F32),
                 jax.ShapeDtypeStruct((gn, pch), F32), jax.ShapeDtypeStruct((gn, pch), F32),
                 jax.ShapeDtypeStruct((gn, 1), F32), jax.ShapeDtypeStruct((gn, 1), F32),
                 jax.ShapeDtypeStruct((groups, 1), F32), jax.ShapeDtypeStruct((CONV_WIDTH, w), F32)]
    def part_spec(k):
        r, c = shapes[k]
        if parts[k].ndim == 3:
            return pl.BlockSpec((None, r, c), lambda i: (i, 0, 0))
        return pl.BlockSpec((r, c), lambda i: (i, 0))

    outs = _run(
        body, comm, name="small_reduce", grid=(nparts,), out_shape=out_shape,
        in_specs=[part_spec(k) for k in SMALL_PARTS] + [_whole(a.shape) for a in (lr_col, li_col, ldt_col, b_re, b_im)],
        out_specs=[_acc(s.shape) for s in out_shape],
        scratch_shapes=[pltpu.VMEM(shapes[k], F32) for k in SMALL_PARTS],
        semantics="arbitrary",
    )(*[parts[k] for k in SMALL_PARTS], lr_col, li_col, ldt_col, b_re, b_im)
    extra = None
    if comm is not None:
        outs, extra = outs
    names = ["loss", "g_mix", "b_in", "b_glu", "s5_d", "conv_b", "b_r", "b_i", "lru_lambda", "g_ffn", "g_ple_gate",
             "b_ple_gate", "g_ple", "g_final", "w_r", "w_i", "s5_c_re", "s5_c_im", "s5_b_re", "s5_b_im", "lam_re",
             "lam_im", "log_dt", "conv_w"]
    res = dict(zip(names, outs))
    return res if comm is None else (res, extra)


def _adamw_small(ws, gs, ms, vs):
    n = len(ws)

    def body(*refs):
        w_r, g_r, m_r, v_r = (refs[i * n:(i + 1) * n] for i in range(4))
        g_o, d_o, m_o, v_o = (refs[(4 + i) * n:(5 + i) * n] for i in range(4))
        for i in range(n):
            g = g_r[i][...]
            delta, m_new, v_new = _adamw_math(w_r[i][...], g, m_r[i][...], v_r[i][...])
            g_o[i][...] = g
            d_o[i][...] = delta
            m_o[i][...] = m_new
            v_o[i][...] = v_new

    shapes = [jax.ShapeDtypeStruct(a.shape, F32) for a in ws]
    outs = pl.pallas_call(body, name="adamw_small", out_shape=shapes * 4)(*ws, *gs, *ms, *vs)
    return outs[:n], outs[n:2 * n], outs[2 * n:3 * n], outs[3 * n:]


def _adamw_math(w, g, m, v):
    m_new = ADAM_B1 * m + (1.0 - ADAM_B1) * g
    v_new = ADAM_B2 * v + (1.0 - ADAM_B2) * (g * g)
    m_hat = m_new / (1.0 - ADAM_B1 ** ADAM_STEP)
    v_hat = v_new / (1.0 - ADAM_B2 ** ADAM_STEP)
    delta = -ADAM_LR * (m_hat / (jnp.sqrt(v_hat) + ADAM_EPS) + ADAM_WD * w)
    return delta, m_new, v_new


def _row_tile(rows):
    for cand in range(256, 0, -16):
        if rows % cand == 0:
            return cand
    return rows


def _adamw(parts, w, m, v, name, transposed=False):
    rows, cols = w.shape
    npart = parts.shape[0]
    tr = _row_tile(rows)
    if transposed:
        parts_spec = pl.BlockSpec((npart, cols, tr), lambda i: (0, 0, i))
    else:
        parts_spec = pl.BlockSpec((npart, tr, cols), lambda i: (0, i, 0))

    def body(p_ref, w_ref, m_ref, v_ref, g_ref, d_ref, mo_ref, vo_ref):
        g = p_ref[0].astype(F32)
        for k in range(1, npart):
            g = g + p_ref[k].astype(F32)
        if transposed:
            g = g.T
        delta, m_new, v_new = _adamw_math(w_ref[...], g, m_ref[...], v_ref[...])
        g_ref[...] = g
        d_ref[...] = delta
        mo_ref[...] = m_new
        vo_ref[...] = v_new

    return pl.pallas_call(
        body, name=name, grid=(rows // tr,),
        out_shape=[jax.ShapeDtypeStruct((rows, cols), F32)] * 4,
        in_specs=[parts_spec] + [_rows(tr, cols)] * 3,
        out_specs=[_rows(tr, cols)] * 4,
        compiler_params=_params("parallel"),
    )(parts, w, m, v)


def _mesh_position():
    return lax.axis_index("x"), lax.axis_index("y"), lax.axis_index("c")


def _flip(pos, rel):
    x, y, c = pos
    return (1 - x if rel & 4 else x, 1 - y if rel & 2 else y, 1 - c if rel & 1 else c)


def _index(pos):
    return 4 * pos[0] + 2 * pos[1] + pos[2]


_ANY = pl.BlockSpec(memory_space=pl.ANY)
FLAT_ROWS = 32


def _dma_sems(n):
    return [pltpu.SemaphoreType.DMA((n, N_DEV - 1)), pltpu.SemaphoreType.DMA((n, N_DEV - 1)), pltpu.SemaphoreType.DMA((n,))]


def _block_of(ref, idx, rows, flat):
    if flat:
        return ref.at[pl.ds(pl.multiple_of(idx * rows, FLAT_ROWS), rows), :]
    return ref.at[idx]


class _Gather:
    chips = (4, 2, 6)
    rels = frozenset((1, 4, 2, 6))

    def __init__(self, shards):
        self.inputs = list(shards)
        self.flat = [s.shape[0] % FLAT_ROWS == 0 for s in shards]
        self.out_shape = [
            jax.ShapeDtypeStruct((N_DEV * s.shape[0], s.shape[1]) if f else (N_DEV,) + s.shape, s.dtype)
            for s, f in zip(shards, self.flat)]
        self.sems = _dma_sems(len(shards))

    def _copy(self, ins, outs, sems, i, k, block, to, own=False):
        dst = _block_of(outs[i], _index(block), self.inputs[i].shape[0], self.flat[i])
        return pltpu.make_async_remote_copy(
            src_ref=ins[i] if own else dst, dst_ref=dst, send_sem=sems[0].at[i, k], recv_sem=sems[1].at[i, k],
            device_id=to, device_id_type=MESH)

    def _local(self, ins, outs, sems, i, me):
        dst = _block_of(outs[i], _index(me), self.inputs[i].shape[0], self.flat[i])
        return pltpu.make_async_copy(ins[i], dst, sems[2].at[i])

    def _first(self, ins, outs, sems, i, me):
        cps = [self._copy(ins, outs, sems, i, 0, me, _flip(me, 1), own=True)]
        cps += [self._copy(ins, outs, sems, i, 1 + j, me, _flip(me, rel), own=True) for j, rel in enumerate(self.chips)]
        return cps

    def _passed(self, ins, outs, sems, i, j, me):
        return self._copy(ins, outs, sems, i, 4 + j, _flip(me, self.chips[j]), _flip(me, 1))

    def before(self, ins, outs, sems):
        n = len(self.inputs)
        me = _mesh_position()

        @pl.when(pl.program_id(0) == 0)
        def _():
            for i in range(n):
                self._local(ins, outs, sems, i, me).start()
                for cp in self._first(ins, outs, sems, i, me):
                    cp.start()

        @pl.when(pl.program_id(0) == pl.num_programs(0) - 1)
        def _():
            for j, rel in enumerate(self.chips):
                for i in range(n):
                    self._copy(ins, outs, sems, i, 1 + j, _flip(me, rel), me).wait_recv()
                    self._passed(ins, outs, sems, i, j, me).start()

    def after(self, ins, outs, sems):
        n = len(self.inputs)
        me = _mesh_position()
        sibling = _flip(me, 1)

        @pl.when(pl.program_id(0) == pl.num_programs(0) - 1)
        def _():
            for i in range(n):
                self._copy(ins, outs, sems, i, 0, sibling, me).wait_recv()
                for j, rel in enumerate(self.chips):
                    self._copy(ins, outs, sems, i, 4 + j, _flip(sibling, rel), me).wait_recv()
            for i in range(n):
                for cp in self._first(ins, outs, sems, i, me):
                    cp.wait_send()
                for j in range(len(self.chips)):
                    self._passed(ins, outs, sems, i, j, me).wait_send()
                self._local(ins, outs, sems, i, me).wait()


N_CHIPS = N_DEV // 2


def _chip(pos):
    return 2 * pos[0] + pos[1]


class _PairSwap:
    rels = frozenset((1,))

    def __init__(self, arrays):
        self.inputs = list(arrays)
        self.rows = [a.shape[0] // N_DEV for a in arrays]
        for r in self.rows:
            assert r % FLAT_ROWS == 0, r
        self.out_shape = [jax.ShapeDtypeStruct((N_CHIPS, r, a.shape[1]), a.dtype) for a, r in zip(arrays, self.rows)]
        n = len(arrays)
        self.sems = [pltpu.SemaphoreType.DMA((n, N_CHIPS)), pltpu.SemaphoreType.DMA((n, N_CHIPS))]

    def _copy(self, ins, outs, sems, i, j, me):
        sibling = _flip(me, 1)
        return pltpu.make_async_remote_copy(
            src_ref=_block_of(ins[i], 2 * j + sibling[2], self.rows[i], True), dst_ref=outs[i].at[j],
            send_sem=sems[0].at[i, j], recv_sem=sems[1].at[i, j], device_id=sibling, device_id_type=MESH)

    def before(self, ins, outs, sems):
        me = _mesh_position()

        @pl.when(pl.program_id(0) == 0)
        def _():
            for i in range(len(self.inputs)):
                for j in range(N_CHIPS):
                    self._copy(ins, outs, sems, i, j, me).start()

    def after(self, ins, outs, sems):
        me = _mesh_position()

        @pl.when(pl.program_id(0) == pl.num_programs(0) - 1)
        def _():
            for i in range(len(self.inputs)):
                for j in range(N_CHIPS):
                    self._copy(ins, outs, sems, i, j, me).wait()


class _ChipExchange:
    chips = (4, 2, 6)
    rels = frozenset(chips)

    def __init__(self, arrays):
        self.inputs = list(arrays)
        self.out_shape = [jax.ShapeDtypeStruct(a.shape, a.dtype) for a in arrays]
        n = len(arrays)
        self.sems = [pltpu.SemaphoreType.DMA((n, 3)), pltpu.SemaphoreType.DMA((n, 3)), pltpu.SemaphoreType.DMA((n,))]

    def _send(self, ins, outs, sems, i, k, me):
        peer = _flip(me, self.chips[k])
        return pltpu.make_async_remote_copy(
            src_ref=ins[i].at[_chip(peer)], dst_ref=outs[i].at[_chip(me)], send_sem=sems[0].at[i, k],
            recv_sem=sems[1].at[i, k], device_id=peer, device_id_type=MESH)

    def _arrival(self, ins, outs, sems, i, k, me):
        peer = _flip(me, self.chips[k])
        return pltpu.make_async_remote_copy(
            src_ref=ins[i].at[_chip(me)], dst_ref=outs[i].at[_chip(peer)], send_sem=sems[0].at[i, k],
            recv_sem=sems[1].at[i, k], device_id=peer, device_id_type=MESH)

    def _local(self, ins, outs, sems, i, me):
        return pltpu.make_async_copy(ins[i].at[_chip(me)], outs[i].at[_chip(me)], sems[2].at[i])

    def before(self, ins, outs, sems):
        me = _mesh_position()

        @pl.when(pl.program_id(0) == 0)
        def _():
            for i in range(len(self.inputs)):
                self._local(ins, outs, sems, i, me).start()
                for k in range(len(self.chips)):
                    self._send(ins, outs, sems, i, k, me).start()

    def after(self, ins, outs, sems):
        me = _mesh_position()

        @pl.when(pl.program_id(0) == pl.num_programs(0) - 1)
        def _():
            for i in range(len(self.inputs)):
                for k in range(len(self.chips)):
                    self._arrival(ins, outs, sems, i, k, me).wait_recv()
            for i in range(len(self.inputs)):
                for k in range(len(self.chips)):
                    self._send(ins, outs, sems, i, k, me).wait_send()
                self._local(ins, outs, sems, i, me).wait()


def _pair_add(grads, halves, name):
    n = len(grads)

    def body(*refs):
        g_refs, h_refs, o_refs = refs[:n], refs[n:2 * n], refs[2 * n:]
        c = lax.axis_index("c")
        for i in range(n):
            r = h_refs[i].shape[1]
            for j in range(N_CHIPS):
                own = g_refs[i][pl.ds(pl.multiple_of((2 * j + c) * r, FLAT_ROWS), r), :]
                o_refs[i][j] = (own.astype(F32) + h_refs[i][j].astype(F32)).astype(o_refs[i].dtype)

    return pl.pallas_call(
        body, name=name, out_shape=[jax.ShapeDtypeStruct(h.shape, h.dtype) for h in halves],
        compiler_params=pltpu.CompilerParams(vmem_limit_bytes=VMEM_LIMIT),
    )(*grads, *halves)


class _Both:
    def __init__(self, first, second):
        self.jobs = (first, second)
        self.rels = first.rels | second.rels
        self.inputs = first.inputs + second.inputs
        self.out_shape = first.out_shape + second.out_shape
        self.sems = first.sems + second.sems

    def _each(self, ins, outs, sems):
        a = self.jobs[0]
        i, o, s = len(a.inputs), len(a.out_shape), len(a.sems)
        return ((a, ins[:i], outs[:o], sems[:s]), (self.jobs[1], ins[i:], outs[o:], sems[s:]))

    def before(self, ins, outs, sems):
        for job, i, o, s in self._each(ins, outs, sems):
            job.before(i, o, s)

    def after(self, ins, outs, sems):
        for job, i, o, s in self._each(ins, outs, sems):
            job.after(i, o, s)


_COLLECTIVE_IDS = {(1,): 0, (2, 4, 6): 1, (1, 2, 4, 6): 2}


def _entry_barrier(rels):
    @pl.when(pl.program_id(0) == 0)
    def _():
        me = _mesh_position()
        sem = pltpu.get_barrier_semaphore()
        for rel in rels:
            pl.semaphore_signal(sem, inc=1, device_id=_flip(me, rel), device_id_type=MESH)
        pl.semaphore_wait(sem, len(rels))


def _run(body, comm, *, semantics, out_shape, in_specs, out_specs, scratch_shapes=(), **kw):
    if comm is None:
        return pl.pallas_call(body, out_shape=out_shape, in_specs=in_specs, out_specs=out_specs,
                              scratch_shapes=list(scratch_shapes), compiler_params=_params(semantics), **kw)
    single = not isinstance(out_shape, (list, tuple))
    outs = [out_shape] if single else list(out_shape)
    ospecs = [out_specs] if single else list(out_specs)
    counts = [len(in_specs), len(comm.inputs), len(outs), len(comm.out_shape), len(scratch_shapes), len(comm.sems)]
    rels = tuple(sorted(comm.rels))

    def carrying(*refs):
        groups, pos = [], 0
        for c in counts:
            groups.append(refs[pos:pos + c])
            pos += c
        main_in, comm_in, main_out, comm_out, main_scratch, comm_sems = groups
        _entry_barrier(rels)
        comm.before(comm_in, comm_out, comm_sems)
        body(*main_in, *main_out, *main_scratch)
        comm.after(comm_in, comm_out, comm_sems)

    call = pl.pallas_call(
        carrying, out_shape=outs + list(comm.out_shape), in_specs=list(in_specs) + [_ANY] * len(comm.inputs),
        out_specs=ospecs + [_ANY] * len(comm.out_shape), scratch_shapes=list(scratch_shapes) + list(comm.sems),
        compiler_params=pltpu.CompilerParams(dimension_semantics=("arbitrary",), vmem_limit_bytes=VMEM_LIMIT,
                                             collective_id=_COLLECTIVE_IDS[rels]), **kw)

    def apply(*args):
        res = call(*args, *comm.inputs)
        main = res[:len(outs)]
        return (main[0] if single else list(main)), list(res[len(outs):])

    return apply


def _alone(comm, name):
    return _run(lambda: None, comm, semantics="arbitrary", name=name, grid=(1,), out_shape=[], in_specs=[], out_specs=[])()[1]


SHARDED = {"w_in": 1, "w_glu": 0, "conv_w": 1, "w_a_out": 1, "w_b_out": 0, "w_o": 0, "w_ffn_gate": 1, "w_ffn_up": 1,
           "w_ffn_down": 0, "w_ple_gate": 0, "w_ple": 1}
TRANSPOSED = ("w_in", "w_a_out", "w_ffn_gate", "w_ffn_up", "w_ple")
LONG_AXIS_MINOR = ("w_in", "w_ffn_gate", "w_ffn_up")
NARROW_LAST = ("s5_b_re", "s5_b_im", "s5_d")
SMALL = ["g_mix", "b_in", "lam_re", "lam_im", "log_dt", "s5_b_re", "s5_b_im", "s5_c_re", "s5_c_im", "s5_d", "b_glu",
         "conv_b", "w_r", "b_r", "w_i", "b_i", "lru_lambda", "g_ffn", "g_ple_gate", "b_ple_gate", "g_ple", "g_final"]
WEIGHTS = ["g_mix", "w_in", "b_in", "lam_re", "lam_im", "log_dt", "s5_b_re", "s5_b_im", "s5_c_re", "s5_c_im", "s5_d",
           "w_glu", "b_glu", "conv_w", "conv_b", "w_r", "b_r", "w_i", "b_i", "lru_lambda", "w_a_out", "w_b_out", "w_o",
           "g_ffn", "w_ffn_gate", "w_ffn_up", "w_ffn_down", "g_ple_gate", "w_ple_gate", "b_ple_gate", "w_ple", "g_ple",
           "g_final"]


def _unblock(gathered, axis):
    nb, r, c = gathered.shape
    if axis == 0:
        return gathered.reshape(nb * r, c)
    return jnp.transpose(gathered, (1, 0, 2)).reshape(r, nb * c)


def _disc_scalars(lr, li, ldt):
    dt = jnp.exp(ldt)
    mag = jnp.exp(lr * dt)
    ar = mag * jnp.cos(li * dt)
    ai = mag * jnp.sin(li * dt)
    den = lr * lr + li * li
    nr = ar - 1.0
    fr = (nr * lr + ai * li) / den
    fi = (ai * lr - nr * li) / den
    return ar, ai, fr, fi


def _disc_cols(lr, li, ldt, b_re, b_im):
    ar, ai, fr, fi = _disc_scalars(lr, li, ldt)
    return ar, ai, fr * b_re - fi * b_im, fr * b_im + fi * b_re


def _local_step(x, p, target, src, small, disc, distributed=True):
    full = {} if distributed else dict(src)
    gw, halves, pairs, got = {}, {}, {}, {}

    def gather(keys):
        return (_Gather([src[k] for k in keys]), keys, full) if distributed else None

    def swap(keys):
        return (_PairSwap([gw[k] for k in keys]), keys, halves) if distributed else None

    def chips(keys):
        return (_ChipExchange([pairs[k] for k in keys]), keys, got) if distributed else None

    def add_pairs(keys):
        if distributed:
            pairs.update(zip(keys, _pair_add([gw[k] for k in keys], [halves[k] for k in keys], "pair_add_" + keys[0])))

    def carry(fn, *args, jobs=()):
        jobs = [j for j in jobs if j is not None]
        if not jobs:
            return fn(*args)
        comm = jobs[0][0]
        for j in jobs[1:]:
            comm = _Both(comm, j[0])
        res, extra = fn(*args, comm=comm)
        for job, keys, sink in jobs:
            sink.update(zip(keys, extra[:len(job.out_shape)]))
            extra = extra[len(job.out_shape):]
        return res

    d = x.shape[1]
    g, n, pch = small["s5_b_re"].shape
    heads = small["w_r"].shape[0]
    sa, lw = g * pch, small["lru_lambda"].shape[-1]
    widths = [sa, lw, d, d]
    row = lambda v: v.reshape(1, -1)

    hd = small["w_r"].shape[-1]
    ar_row, ai_row, bbr_blk, bbi_blk, cre_blk, cimn_blk, wr_blk, wi_blk = carry(
        _prep, *disc["rows"], *disc["cols"], disc["b_re"], disc["b_im"], small["s5_c_re"].reshape(sa, n),
        small["s5_c_im"].reshape(sa, n), small["w_r"].reshape(lw, hd), small["w_i"].reshape(lw, hd),
        jobs=[gather(["w_in"])])
    d_row = row(small["s5_d"])
    u_a, u_b, za, zb = carry(_inproj_fwd, x, row(small["g_mix"]), full["w_in"], row(small["b_in"]), widths,
                             jobs=[gather(["w_glu", "conv_w", "w_a_out", "w_b_out"])])
    conv_w = _unblock(full["conv_w"], 1) if distributed else full["conv_w"]
    sr, si, y, y_a = carry(_s5_fwd, u_a, bbr_blk, bbi_blk, ar_row, ai_row, cre_blk, cimn_blk, d_row, full["w_glu"],
                           row(small["b_glu"]), jobs=[gather(["w_ffn_gate", "w_o"])])
    xc, h, hprev = carry(_lru_fwd, u_b, conv_w, row(small["conv_b"]), wr_blk, row(small["b_r"]), wi_blk,
                         row(small["b_i"]), row(small["lru_lambda"]), jobs=[gather(["w_ffn_up"])])
    x1, merged, ma, mb = carry(_merge_fwd, y_a, h, za, zb, x, full["w_a_out"], full["w_b_out"], full["w_o"],
                               jobs=[gather(["w_ffn_down"])])
    fg, fu = carry(_ffn_up_fwd, x1, row(small["g_ffn"]), full["w_ffn_gate"], full["w_ffn_up"],
                   jobs=[gather(["w_ple_gate", "w_ple"])])
    x2 = _ffn_down_fwd(fg, fu, x1, full["w_ffn_down"])

    dx2, loss_blk, gw["w_ple_gate"], gw["w_ple"], vec_tail = _tail_fwd_bwd(
        x2, p, target, row(small["g_ple_gate"]), full["w_ple_gate"], row(small["b_ple_gate"]), full["w_ple"],
        row(small["g_ple"]), row(small["g_final"]))
    dfg, dfu, act = carry(_ffn_bwd_a, dx2, fg, fu, full["w_ffn_down"], jobs=[swap(["w_ple_gate", "w_ple"])])
    tn_d = min(d, 512)
    gw["w_ffn_down"] = _matmul_tn(act, dx2, tn_d, "dw_ffn_down", BF16)
    add_pairs(["w_ple_gate", "w_ple"])
    dx1, h2, vec_ffn = carry(_ffn_bwd_b, dfg, dfu, x1, dx2, row(small["g_ffn"]), full["w_ffn_gate"], full["w_ffn_up"],
                             jobs=[chips(["w_ple_gate", "w_ple"]), swap(["w_ffn_down"])])
    gw["w_ffn_gate"] = _matmul_tn(dfg, h2, tn_d, "dw_ffn_gate", BF16)
    gw["w_ffn_up"] = _matmul_tn(dfu, h2, tn_d, "dw_ffn_up", BF16)
    add_pairs(["w_ffn_down"])
    dza, dzb, dya, dyb, gw["w_o"], gw["w_a_out"], gw["w_b_out"] = carry(
        _merge_bwd, dx1, merged, ma, mb, za, zb, y_a, h, full["w_o"], full["w_a_out"], full["w_b_out"],
        jobs=[chips(["w_ffn_down"]), swap(["w_ffn_gate", "w_ffn_up"])])
    add_pairs(["w_ffn_gate", "w_ffn_up"])
    du_b, dw_r, dw_i, vec_lru = carry(
        _lru_bwd, dyb, xc, hprev, u_b, conv_w, wr_blk, row(small["b_r"]), wi_blk, row(small["b_i"]),
        row(small["lru_lambda"]), hd, jobs=[chips(["w_ffn_gate", "w_ffn_up"]), swap(["w_o", "w_a_out", "w_b_out"])])
    add_pairs(["w_o", "w_a_out", "w_b_out"])
    du_a, lam_r, lam_i, dy16, gw["w_glu"], vec_sa, vec_gn = carry(
        _s5_bwd, dya, y, sr, si, u_a, full["w_glu"], row(small["b_glu"]), cre_blk, cimn_blk, bbr_blk, bbi_blk, ar_row,
        ai_row, d_row, jobs=[chips(["w_o", "w_a_out", "w_b_out"])])
    smalls = {"vec_tail": vec_tail, "vec_ffn": vec_ffn, "vec_lru": vec_lru, "vec_sa": vec_sa, "vec_gn": vec_gn,
              "dw_r": dw_r, "dw_i": dw_i, "loss": loss_blk}
    everyones = {}

    def gather_smalls(keys):
        return (_Gather([smalls[k] for k in keys]), keys, everyones) if distributed else None

    smalls["dbb_re"], smalls["dbb_im"], smalls["dc_re"], smalls["dc_imn"] = carry(
        _s5_param_grads, lam_r, lam_i, sr, si, u_a, dy16, pch, n, jobs=[swap(["w_glu"]), gather_smalls(list(smalls))])
    add_pairs(["w_glu"])
    dz = [du_a, du_b, dza, dzb]
    grad_x, h0, smalls["vec_mix"], smalls["vec_bin"] = _inproj_bwd(
        dz, x, dx1, row(small["g_mix"]), full["w_in"])
    shapes = {k: a.shape for k, a in smalls.items()}
    gw["w_in"] = carry(_dw_from_parts, dz, h0, tn_d, "dw_in",
                       jobs=[chips(["w_glu"]), gather_smalls(["dbb_re", "dbb_im", "dc_re", "dc_imn"])])
    smalls.update(everyones)
    if distributed:
        got["w_in"] = gw["w_in"]
        gw = got
    return grad_x, gw, smalls, shapes


def _disc_inputs(small):
    g, n, pch = small["s5_b_re"].shape
    srcs = (small["lam_re"], small["lam_im"], jnp.repeat(small["log_dt"], n))
    return {"rows": [a.reshape(1, g * n) for a in srcs], "cols": [a.reshape(g * n, 1) for a in srcs],
            "b_re": small["s5_b_re"].reshape(g * n, pch), "b_im": small["s5_b_im"].reshape(g * n, pch)}


def kernel(x, p, g_mix, w_in, b_in, lam_re, lam_im, log_dt, s5_b_re, s5_b_im, s5_c_re, s5_c_im, s5_d, w_glu, b_glu, conv_w, conv_b, w_r, b_r, w_i, b_i, lru_lambda, w_a_out, w_b_out, w_o, g_ffn, w_ffn_gate, w_ffn_up, w_ffn_down, g_ple_gate, w_ple_gate, b_ple_gate, w_ple, g_ple, g_final, loss_target, m_g_mix, m_w_in, m_b_in, m_lam_re, m_lam_im, m_log_dt, m_s5_b_re, m_s5_b_im, m_s5_c_re, m_s5_c_im, m_s5_d, m_w_glu, m_b_glu, m_conv_w, m_conv_b, m_w_r, m_b_r, m_w_i, m_b_i, m_lru_lambda, m_w_a_out, m_w_b_out, m_w_o, m_g_ffn, m_w_ffn_gate, m_w_ffn_up, m_w_ffn_down, m_g_ple_gate, m_w_ple_gate, m_b_ple_gate, m_w_ple, m_g_ple, m_g_final, v_g_mix, v_w_in, v_b_in, v_lam_re, v_lam_im, v_log_dt, v_s5_b_re, v_s5_b_im, v_s5_c_re, v_s5_c_im, v_s5_d, v_w_glu, v_b_glu, v_conv_w, v_conv_b, v_w_r, v_b_r, v_w_i, v_b_i, v_lru_lambda, v_w_a_out, v_w_b_out, v_w_o, v_g_ffn, v_w_ffn_gate, v_w_ffn_up, v_w_ffn_down, v_g_ple_gate, v_w_ple_gate, v_b_ple_gate, v_w_ple, v_g_ple, v_g_final):
    given = dict(locals())
    wts = {k: given[k] for k in WEIGHTS}
    moms = {k: given["m_" + k] for k in WEIGHTS}
    vels = {k: given["v_" + k] for k in WEIGHTS}

    def drop_depth(k, a):
        return a if k == "g_final" else a[0]

    small = {k: drop_depth(k, wts[k]) for k in SMALL}
    shard = {k: wts[k][0] for k in SHARDED}
    names = list(SHARDED)

    def wire(k):
        if k == "conv_w":
            return shard[k]
        return (shard[k].T if k in TRANSPOSED else shard[k]).astype(BF16)

    disc = _disc_inputs(small)
    grad_x, parts, smalls, shapes = _local_step(x[0], p[0, 0], loss_target[0], {k: wire(k) for k in names}, small, disc)

    late = ["vec_mix", "vec_bin"]
    received = _alone(_Both(_PairSwap([parts["w_in"]]), _Gather([smalls[k] for k in late])), "swap_last")
    (pair_in,) = _pair_add([parts["w_in"]], received[:1], "pair_add_w_in")
    smalls.update(zip(late, received[1:]))

    g_small, (parts["w_in"],) = _small_reduce(smalls, shapes, *disc["cols"], disc["b_re"], disc["b_im"],
                                              small["s5_b_re"].shape[0], comm=_ChipExchange([pair_in]))
    loss = g_small.pop("loss")[0, 0]
    cols = shard["conv_w"].shape[1]
    mine = _index((lax.axis_index("x"), lax.axis_index("y"), lax.axis_index("c")))
    parts["conv_w"] = lax.dynamic_slice_in_dim(g_small.pop("conv_w"), mine * cols, cols, axis=1)[None]

    def view(k, a):
        a = a.reshape((1, -1) if k == "g_final" else wts[k].shape)
        return jnp.swapaxes(a, -1, -2) if k in NARROW_LAST else a

    def unview(k, a):
        return (jnp.swapaxes(a, -1, -2) if k in NARROW_LAST else a).reshape(wts[k].shape)

    slots = _adamw_small([view(k, wts[k]) for k in SMALL], [view(k, g_small[k]) for k in SMALL],
                         [view(k, moms[k]) for k in SMALL], [view(k, vels[k]) for k in SMALL])
    small_out = [dict(zip(SMALL, [unview(k, a) for k, a in zip(SMALL, slot)])) for slot in slots]

    big_out = {}
    for k in names:
        if k in LONG_AXIS_MINOR:
            res = _adamw(parts[k], shard[k].T, moms[k][0].T, vels[k][0].T, "adamw_" + k)
            big_out[k] = [a.T for a in res]
        else:
            big_out[k] = _adamw(parts[k], shard[k], moms[k][0], vels[k][0], "adamw_" + k, transposed=k in TRANSPOSED)

    outs = [loss, grad_x[None]]
    for slot in range(4):
        for k in WEIGHTS:
            if k in SHARDED:
                outs.append(big_out[k][slot][None])
            else:
                outs.append(small_out[slot][k])
    return tuple(outs)
```

```python
import math

import jax
import jax.numpy as jnp
from jax import lax
from jax.experimental import pallas as pl
from jax.experimental.pallas import tpu as pltpu

F32 = jnp.float32
BF16 = jnp.bfloat16

EPS = 1e-6
LRU_C = 8.0
CONV_WIDTH = 4
ADAM_LR = 0.001
ADAM_B1 = 0.9
ADAM_B2 = 0.999
ADAM_EPS = 1e-08
ADAM_WD = 0.01
ADAM_STEP = 10

N_DEV = 8
MESH = pl.DeviceIdType.MESH
SUBLANES = 8
LANES = 128
VMEM_LIMIT = 56 * 1024 * 1024
TOKEN_TILE = 256
MATMUL_TILE = 256
TIME_CHUNK = 256
S5_SLAB = 128
LRU_SLAB = 256


def _dot(a, b):
    return jnp.dot(a.astype(BF16), b.astype(BF16), preferred_element_type=F32)


def _dot_nt(a, b):
    return lax.dot_general(a.astype(BF16), b.astype(BF16), (((1,), (1,)), ((), ())), preferred_element_type=F32)


def _dot_tn(a, b):
    return lax.dot_general(a.astype(BF16), b.astype(BF16), (((0,), (0,)), ((), ())), preferred_element_type=F32)


def _sigmoid(x):
    return jax.nn.sigmoid(x)


def _rms_stats(x):
    r = lax.rsqrt(jnp.mean(x * x, axis=-1, keepdims=True) + EPS)
    return x * r, r


def _rms_bwd(dy, xhat, r, g):
    dxn = dy * g
    dx = r * (dxn - xhat * jnp.mean(dxn * xhat, axis=-1, keepdims=True))
    return dx, dy * xhat


def _rowsum(v):
    return jnp.sum(v, axis=0, keepdims=True)


def _expm1(x):
    u = jnp.exp(x)
    um1 = u - 1.0
    safe = jnp.where(um1 == 0.0, 1.0, jnp.log(u))
    return jnp.where(um1 == 0.0, x, um1 * x / safe)


def _softplus(x):
    e = jnp.exp(-jnp.abs(x))
    u = 1.0 + e
    um1 = u - 1.0
    safe = jnp.where(um1 == 0.0, 1.0, um1)
    log1p_e = jnp.where(um1 == 0.0, e, jnp.log(u) * e / safe)
    return jnp.maximum(x, 0.0) + log1p_e


_GELU_K = math.sqrt(2.0 / math.pi)
_GELU_C = 0.044715


def _gelu(x):
    return 0.5 * x * (1.0 + jnp.tanh(_GELU_K * (x + _GELU_C * x * x * x)))


def _gelu_grad(x):
    th = jnp.tanh(_GELU_K * (x + _GELU_C * x * x * x))
    return 0.5 * (1.0 + th) + 0.5 * x * (1.0 - th * th) * _GELU_K * (1.0 + 3.0 * _GELU_C * x * x)


def _params(*sem):
    return pltpu.CompilerParams(dimension_semantics=sem, vmem_limit_bytes=VMEM_LIMIT)


def _rows(tm, n):
    return pl.BlockSpec((tm, n), lambda i: (i, 0))


def _rows_rev(tm, n, steps):
    return pl.BlockSpec((tm, n), lambda i: (steps - 1 - i, 0))


def _whole(shape):
    nd = len(shape)
    return pl.BlockSpec(shape, lambda i: (0,) * nd, pipeline_mode=pl.Buffered(1))


def _acc(shape):
    nd = len(shape)
    return pl.BlockSpec(shape, lambda i: (0,) * nd)


def _zero_on_first(*refs):
    @pl.when(pl.program_id(0) == 0)
    def _():
        for r in refs:
            r[...] = jnp.zeros_like(r)


def _inproj_fwd(x, g_mix, w_in_t, b_in, widths, comm=None):
    t, d = x.shape
    n = w_in_t.shape[0]
    tm = min(MATMUL_TILE, t)
    offs = [sum(widths[:i]) for i in range(len(widths) + 1)]

    def body(x_ref, g_ref, w_ref, b_ref, *outs):
        xhat, _ = _rms_stats(x_ref[...])
        h = (xhat * g_ref[...]).astype(BF16)
        for k, o_ref in enumerate(outs):
            lo, hi = offs[k], offs[k + 1]
            o_ref[...] = _dot_nt(h, w_ref[lo:hi, :]) + b_ref[:, lo:hi]

    return _run(
        body, comm, name="inproj_fwd", grid=(t // tm,),
        out_shape=[jax.ShapeDtypeStruct((t, w), F32) for w in widths],
        in_specs=[_rows(tm, d), _whole((1, d)), _whole((n, d)), _whole((1, n))],
        out_specs=[_rows(tm, w) for w in widths],
        semantics="parallel",
    )(x, g_mix, w_in_t, b_in)


def _s5_fwd(u, bbr_blk, bbi_blk, ar, ai, cre_blk, cimn_blk, d_skip, w_glu, b_glu, comm=None):
    t, sa = u.shape
    ns, _, sw = bbr_blk.shape
    gn = ns * sw
    tc = min(TIME_CHUNK, t)

    def body(u_ref, bbr_ref, bbi_ref, ar_ref, ai_ref, cre_ref, cim_ref, d_ref, wg_ref, bg_ref,
             sr_ref, si_ref, y_ref, ya_ref, cr_s, ci_s, sr_s, si_s):
        _zero_on_first(cr_s, ci_s)
        uv = u_ref[...]
        ub = uv.astype(BF16)
        for m in range(ns):
            um = ub[:, m * S5_SLAB:(m + 1) * S5_SLAB]
            sr_s[:, m * sw:(m + 1) * sw] = _dot(um, bbr_ref[m])
            si_s[:, m * sw:(m + 1) * sw] = _dot(um, bbi_ref[m])
        a_r = ar_ref[...]
        a_i = ai_ref[...]

        def step(row, carry):
            c_r, c_i = carry
            at = pl.ds(row, 1)
            n_r = a_r * c_r - a_i * c_i + sr_s[at, :]
            n_i = a_r * c_i + a_i * c_r + si_s[at, :]
            sr_s[at, :] = n_r
            si_s[at, :] = n_i
            return n_r, n_i

        c_r, c_i = lax.fori_loop(0, tc, step, (cr_s[0:1, :], ci_s[0:1, :]), unroll=8)
        cr_s[0:1, :] = c_r
        ci_s[0:1, :] = c_i
        for m in range(ns):
            states, chans = slice(m * sw, (m + 1) * sw), slice(m * S5_SLAB, (m + 1) * S5_SLAB)
            s_r, s_i = sr_s[:, states].astype(BF16), si_s[:, states].astype(BF16)
            sr_ref[:, states] = s_r.astype(sr_ref.dtype)
            si_ref[:, states] = s_i.astype(si_ref.dtype)
            y_ref[:, chans] = _dot(s_r, cre_ref[m]) + _dot(s_i, cim_ref[m]) + d_ref[:, chans] * uv[:, chans]
        y = y_ref[...]
        zz = _gelu(y)
        q = _dot(zz, wg_ref[...]) + bg_ref[...]
        ya_ref[...] = zz * _sigmoid(q)

    return _run(
        body, comm, name="s5_fwd", grid=(t // tc,),
        out_shape=[jax.ShapeDtypeStruct((t, gn), BF16), jax.ShapeDtypeStruct((t, gn), BF16),
                   jax.ShapeDtypeStruct((t, sa), F32), jax.ShapeDtypeStruct((t, sa), F32)],
        in_specs=[_rows(tc, sa), _whole(bbr_blk.shape), _whole(bbi_blk.shape), _whole((1, gn)), _whole((1, gn)),
                  _whole(cre_blk.shape), _whole(cimn_blk.shape), _whole((1, sa)), _whole((sa, sa)), _whole((1, sa))],
        out_specs=[_rows(tc, gn), _rows(tc, gn), _rows(tc, sa), _rows(tc, sa)],
        scratch_shapes=[pltpu.VMEM((SUBLANES, gn), F32), pltpu.VMEM((SUBLANES, gn), F32),
                        pltpu.VMEM((tc, gn), F32), pltpu.VMEM((tc, gn), F32)],
        semantics="arbitrary",
    )(u, bbr_blk, bbi_blk, ar, ai, cre_blk, cimn_blk, d_skip, w_glu, b_glu)


def _slab_dot(x, w_ref, transposed=False):
    dot = _dot_nt if transposed else _dot
    xb = x.astype(BF16)
    return jnp.concatenate([dot(xb[:, j * LRU_SLAB:(j + 1) * LRU_SLAB], w_ref[j]) for j in range(w_ref.shape[0])], axis=1)


def _lru_gates(xc, wr_ref, br_ref, wi_ref, bi_ref, lam_ref):
    r = _sigmoid(_slab_dot(xc, wr_ref) + br_ref[...])
    ig = _sigmoid(_slab_dot(xc, wi_ref) + bi_ref[...])
    sp = _softplus(-lam_ref[...])
    log_a = (-LRU_C * r) * sp
    return r, ig, sp, log_a


def _lru_fwd(u, conv_w, conv_b, wr_blk, b_r, wi_blk, b_i, lru_lambda, comm=None):
    t, w = u.shape
    tc = min(TIME_CHUNK, t)
    halo = SUBLANES

    def body(u_ref, cw_ref, cb_ref, wr_ref, br_ref, wi_ref, bi_ref, lam_ref,
             xc_ref, h_ref, hp_ref, ext_s, a_s, carry_s):
        @pl.when(pl.program_id(0) == 0)
        def _():
            ext_s[0:halo, :] = jnp.zeros((halo, w), F32)
            carry_s[...] = jnp.zeros_like(carry_s)

        ext_s[halo:halo + tc, :] = u_ref[...]
        xc = cb_ref[...]
        for k in range(CONV_WIDTH):
            off = halo - (CONV_WIDTH - 1) + k
            xc = xc + cw_ref[k:k + 1, :] * ext_s[off:off + tc, :]
        ext_s[0:halo, :] = ext_s[tc:tc + halo, :]
        xc_ref[...] = xc
        r, ig, sp, log_a = _lru_gates(xc, wr_ref, br_ref, wi_ref, bi_ref, lam_ref)
        a_s[...] = jnp.exp(log_a)
        h_ref[...] = jnp.sqrt(-_expm1(2.0 * log_a)) * ig * xc

        def step(row, carry):
            at = pl.ds(row, 1)
            hp_ref[at, :] = carry
            nxt = a_s[at, :] * carry + h_ref[at, :]
            h_ref[at, :] = nxt
            return nxt

        carry_s[0:1, :] = lax.fori_loop(0, tc, step, carry_s[0:1, :], unroll=8)

    return _run(
        body, comm, name="lru_fwd", grid=(t // tc,),
        out_shape=[jax.ShapeDtypeStruct((t, w), F32)] * 3,
        in_specs=[_rows(tc, w), _whole((CONV_WIDTH, w)), _whole((1, w)), _whole(wr_blk.shape), _whole((1, w)),
                  _whole(wi_blk.shape), _whole((1, w)), _whole((1, w))],
        out_specs=[_rows(tc, w)] * 3,
        scratch_shapes=[pltpu.VMEM((halo + tc, w), F32), pltpu.VMEM((tc, w), F32), pltpu.VMEM((SUBLANES, w), F32)],
        semantics="arbitrary",
    )(u, conv_w, conv_b, wr_blk, b_r, wi_blk, b_i, lru_lambda)


def _merge_fwd(y_a, h, za, zb, x, w_a_out_t, w_b_out, w_o, comm=None):
    t, d = x.shape
    sa, lw = y_a.shape[1], h.shape[1]
    tm = min(TOKEN_TILE, t)

    def body(ya_ref, h_ref, za_ref, zb_ref, x_ref, wa_ref, wb_ref, wo_ref, x1_ref, mg_ref, ma_ref, mb_ref):
        ma = _dot_nt(ya_ref[...], wa_ref[...])
        mb = _dot(h_ref[...], wb_ref[...])
        merged = _sigmoid(za_ref[...]) * ma + _sigmoid(zb_ref[...]) * mb
        ma_ref[...] = ma.astype(ma_ref.dtype)
        mb_ref[...] = mb.astype(mb_ref.dtype)
        mg_ref[...] = merged.astype(mg_ref.dtype)
        x1_ref[...] = x_ref[...] + _dot(merged, wo_ref[...])

    return _run(
        body, comm, name="merge_fwd", grid=(t // tm,),
        out_shape=[jax.ShapeDtypeStruct((t, d), F32), jax.ShapeDtypeStruct((t, d), BF16),
                   jax.ShapeDtypeStruct((t, d), BF16), jax.ShapeDtypeStruct((t, d), BF16)],
        in_specs=[_rows(tm, sa), _rows(tm, lw), _rows(tm, d), _rows(tm, d), _rows(tm, d),
                  _whole((d, sa)), _whole((lw, d)), _whole((d, d))],
        out_specs=[_rows(tm, d)] * 4,
        semantics="parallel",
    )(y_a, h, za, zb, x, w_a_out_t, w_b_out, w_o)


def _ffn_up_fwd(x1, g_ffn, w_gate_t, w_up_t, comm=None):
    t, d = x1.shape
    f = w_gate_t.shape[0]
    tm = min(MATMUL_TILE, t)

    def body(x_ref, g_ref, wg_ref, wu_ref, fg_ref, fu_ref):
        xhat, _ = _rms_stats(x_ref[...])
        h2 = (xhat * g_ref[...]).astype(BF16)
        fg_ref[...] = _dot_nt(h2, wg_ref[...]).astype(fg_ref.dtype)
        fu_ref[...] = _dot_nt(h2, wu_ref[...]).astype(fu_ref.dtype)

    return _run(
        body, comm, name="ffn_up_fwd", grid=(t // tm,),
        out_shape=[jax.ShapeDtypeStruct((t, f), BF16)] * 2,
        in_specs=[_rows(tm, d), _whole((1, d)), _whole((f, d)), _whole((f, d))],
        out_specs=[_rows(tm, f)] * 2,
        semantics="parallel",
    )(x1, g_ffn, w_gate_t, w_up_t)


def _ffn_down_fwd(fg, fu, x1, w_down, comm=None):
    t, d = x1.shape
    f = fg.shape[1]
    tm = min(MATMUL_TILE, t)

    def body(fg_ref, fu_ref, x_ref, wd_ref, x2_ref):
        fgv = fg_ref[...].astype(F32)
        act = fgv * _sigmoid(fgv) * fu_ref[...].astype(F32)
        x2_ref[...] = x_ref[...] + _dot(act, wd_ref[...])

    return _run(
        body, comm, name="ffn_down_fwd", grid=(t // tm,),
        out_shape=jax.ShapeDtypeStruct((t, d), F32),
        in_specs=[_rows(tm, f), _rows(tm, f), _rows(tm, d), _whole((f, d))],
        out_specs=_rows(tm, d),
        semantics="parallel",
    )(fg, fu, x1, w_down)


def _store_on_last(pairs):
    @pl.when(pl.program_id(0) == pl.num_programs(0) - 1)
    def _():
        for acc, out in pairs:
            out[...] = acc[...].astype(out.dtype)


def _tail_fwd_bwd(x2, p, target, g_pg, w_pg, b_pg, w_ple_t, g_ple, g_final):
    t, d = x2.shape
    pd = p.shape[1]
    tm = min(TOKEN_TILE, t)

    def body(x2_ref, p_ref, tg_ref, gpg_ref, wpg_ref, bpg_ref, wple_ref, gple_ref, gfin_ref,
             dx2_ref, loss_ref, dwpg_out, dwple_out, vec_ref, dwpg_ref, dwple_ref):
        _zero_on_first(loss_ref, dwpg_ref, dwple_ref, vec_ref)
        x2v = x2_ref[...]
        xh2, r2 = _rms_stats(x2v)
        h3 = xh2 * gpg_ref[...]
        gp = _sigmoid(_dot(h3, wpg_ref[...]) + bpg_ref[...])
        pe = _dot_nt(p_ref[...], wple_ref[...])
        peh, r3 = _rms_stats(pe)
        e = peh * gple_ref[...]
        x3 = x2v + gp * e
        xh3, r4 = _rms_stats(x3)
        diff = xh3 * gfin_ref[...] - tg_ref[...]
        loss_ref[...] += 0.5 * jnp.sum(jnp.mean(diff * diff, axis=-1, keepdims=True))
        dy = diff * (1.0 / d)
        dx3, dgfin = _rms_bwd(dy, xh3, r4, gfin_ref[...])
        d_gp = dx3 * e
        d_e = dx3 * gp
        dpe, dgple = _rms_bwd(d_e, peh, r3, gple_ref[...])
        dwple_ref[...] += _dot_tn(dpe, p_ref[...])
        dpre = d_gp * gp * (1.0 - gp)
        dwpg_ref[...] += _dot_tn(h3, dpre)
        dh3 = _dot_nt(dpre, wpg_ref[...])
        dx2n, dgpg = _rms_bwd(dh3, xh2, r2, gpg_ref[...])
        dx2_ref[...] = dx3 + dx2n
        vec_ref[0:1, :] += _rowsum(dpre)
        vec_ref[1:2, :] += _rowsum(dgpg)
        vec_ref[2:3, :] += _rowsum(dgple)
        vec_ref[3:4, :] += _rowsum(dgfin)
        _store_on_last([(dwpg_ref, dwpg_out), (dwple_ref, dwple_out)])

    return pl.pallas_call(
        body, name="tail_fwd_bwd", grid=(t // tm,),
        out_shape=[jax.ShapeDtypeStruct((t, d), F32), jax.ShapeDtypeStruct((SUBLANES, LANES), F32),
                   jax.ShapeDtypeStruct((d, d), BF16), jax.ShapeDtypeStruct((d, pd), BF16),
                   jax.ShapeDtypeStruct((SUBLANES, d), F32)],
        in_specs=[_rows(tm, d), _rows(tm, pd), _rows(tm, d), _whole((1, d)), _whole((d, d)), _whole((1, d)),
                  _whole((d, pd)), _whole((1, d)), _whole((1, d))],
        out_specs=[_rows(tm, d), _acc((SUBLANES, LANES)), _acc((d, d)), _acc((d, pd)), _acc((SUBLANES, d))],
        scratch_shapes=[pltpu.VMEM((d, d), F32), pltpu.VMEM((d, pd), F32)],
        compiler_params=_params("arbitrary"),
    )(x2, p, target, g_pg, w_pg, b_pg, w_ple_t, g_ple, g_final)


def _ffn_bwd_a(dx2, fg, fu, w_down, comm=None):
    t, d = dx2.shape
    f = fg.shape[1]
    tm = min(MATMUL_TILE, t)

    def body(dx_ref, fg_ref, fu_ref, wd_ref, dfg_ref, dfu_ref, act_ref):
        dact = _dot_nt(dx_ref[...], wd_ref[...])
        fgv = fg_ref[...].astype(F32)
        fuv = fu_ref[...].astype(F32)
        sg = _sigmoid(fgv)
        silu = fgv * sg
        dfu_ref[...] = (dact * silu).astype(dfu_ref.dtype)
        dfg_ref[...] = (dact * fuv * (sg * (1.0 + fgv * (1.0 - sg)))).astype(dfg_ref.dtype)
        act_ref[...] = (silu * fuv).astype(act_ref.dtype)

    return _run(
        body, comm, name="ffn_bwd_a", grid=(t // tm,),
        out_shape=[jax.ShapeDtypeStruct((t, f), BF16)] * 3,
        in_specs=[_rows(tm, d), _rows(tm, f), _rows(tm, f), _whole((f, d))],
        out_specs=[_rows(tm, f)] * 3,
        semantics="parallel",
    )(dx2, fg, fu, w_down)


def _ffn_bwd_b(dfg, dfu, x1, dx2, g_ffn, w_gate_t, w_up_t, comm=None):
    t, d = x1.shape
    f = dfg.shape[1]
    tm = min(MATMUL_TILE, t)

    def body(dfg_ref, dfu_ref, x_ref, dx2_ref, g_ref, wg_ref, wu_ref, dx1_ref, h2_ref, vec_ref):
        _zero_on_first(vec_ref)
        dh2 = _dot(dfg_ref[...], wg_ref[...]) + _dot(dfu_ref[...], wu_ref[...])
        xhat, r = _rms_stats(x_ref[...])
        h2_ref[...] = (xhat * g_ref[...]).astype(h2_ref.dtype)
        dxn, dg = _rms_bwd(dh2, xhat, r, g_ref[...])
        dx1_ref[...] = dx2_ref[...] + dxn
        vec_ref[0:1, :] += _rowsum(dg)

    return _run(
        body, comm, name="ffn_bwd_b", grid=(t // tm,),
        out_shape=[jax.ShapeDtypeStruct((t, d), F32), jax.ShapeDtypeStruct((t, d), BF16),
                   jax.ShapeDtypeStruct((SUBLANES, d), F32)],
        in_specs=[_rows(tm, f), _rows(tm, f), _rows(tm, d), _rows(tm, d), _whole((1, d)), _whole((f, d)), _whole((f, d))],
        out_specs=[_rows(tm, d), _rows(tm, d), _acc((SUBLANES, d))],
        semantics="arbitrary",
    )(dfg, dfu, x1, dx2, g_ffn, w_gate_t, w_up_t)


def _matmul_tn(a, b, tn, name, dtype=F32, comm=None):
    t, k = a.shape
    n = b.shape[1]

    def body(a_ref, b_ref, o_ref):
        o_ref[...] = _dot_tn(a_ref[...], b_ref[...]).astype(o_ref.dtype)

    return _run(
        body, comm, name=name, grid=(n // tn,),
        out_shape=jax.ShapeDtypeStruct((k, n), dtype),
        in_specs=[_whole((t, k)), pl.BlockSpec((t, tn), lambda j: (0, j))],
        out_specs=pl.BlockSpec((k, tn), lambda j: (0, j)),
        semantics="parallel",
    )(a, b)


def _merge_bwd(dx1, merged, ma, mb, za, zb, y_a, h, w_o, w_a_out_t, w_b_out, comm=None):
    t, d = dx1.shape
    sa, lw = y_a.shape[1], h.shape[1]
    tm = min(TOKEN_TILE, t)

    def body(dx1_ref, mg_ref, ma_ref, mb_ref, za_ref, zb_ref, ya_ref, h_ref, wo_ref, wa_ref, wb_ref,
             dza_ref, dzb_ref, dya_ref, dyb_ref, dwo_out, dwa_out, dwb_out, dwo_ref, dwa_ref, dwb_ref):
        _zero_on_first(dwo_ref, dwa_ref, dwb_ref)
        dx1v = dx1_ref[...].astype(BF16)
        dmg = _dot_nt(dx1v, wo_ref[...])
        ga = _sigmoid(za_ref[...])
        gb = _sigmoid(zb_ref[...])
        dza_ref[...] = (dmg * ma_ref[...].astype(F32) * ga * (1.0 - ga)).astype(dza_ref.dtype)
        dzb_ref[...] = (dmg * mb_ref[...].astype(F32) * gb * (1.0 - gb)).astype(dzb_ref.dtype)
        dma = (dmg * ga).astype(BF16)
        dmb = (dmg * gb).astype(BF16)
        dya_ref[...] = _dot(dma, wa_ref[...])
        dyb_ref[...] = _dot_nt(dmb, wb_ref[...])
        dwo_ref[...] += _dot_tn(mg_ref[...], dx1v)
        dwa_ref[...] += _dot_tn(dma, ya_ref[...])
        dwb_ref[...] += _dot_tn(h_ref[...], dmb)
        _store_on_last([(dwo_ref, dwo_out), (dwa_ref, dwa_out), (dwb_ref, dwb_out)])

    return _run(
        body, comm, name="merge_bwd", grid=(t // tm,),
        out_shape=[jax.ShapeDtypeStruct((t, d), BF16), jax.ShapeDtypeStruct((t, d), BF16),
                   jax.ShapeDtypeStruct((t, sa), F32), jax.ShapeDtypeStruct((t, lw), F32),
                   jax.ShapeDtypeStruct((d, d), BF16), jax.ShapeDtypeStruct((d, sa), BF16),
                   jax.ShapeDtypeStruct((lw, d), BF16)],
        in_specs=[_rows(tm, d), _rows(tm, d), _rows(tm, d), _rows(tm, d), _rows(tm, d), _rows(tm, d),
                  _rows(tm, sa), _rows(tm, lw), _whole((d, d)), _whole((d, sa)), _whole((lw, d))],
        out_specs=[_rows(tm, d), _rows(tm, d), _rows(tm, sa), _rows(tm, lw), _acc((d, d)), _acc((d, sa)), _acc((lw, d))],
        scratch_shapes=[pltpu.VMEM((d, d), F32), pltpu.VMEM((d, sa), F32), pltpu.VMEM((lw, d), F32)],
        semantics="arbitrary",
    )(dx1, merged, ma, mb, za, zb, y_a, h, w_o, w_a_out_t, w_b_out)


def _fold_diag_blocks(dense, row_group, col_group, row0=0, col0=0):
    r, c = dense.shape
    rows = lax.broadcasted_iota(jnp.int32, (r, c), 0) + row0
    cols = lax.broadcasted_iota(jnp.int32, (r, c), 1) + col0
    kept = jnp.where(rows // row_group == cols // col_group, dense, 0.0)
    pick = (lax.broadcasted_iota(jnp.int32, (row_group, r), 0)
            == lax.broadcasted_iota(jnp.int32, (row_group, r), 1) % row_group).astype(F32)
    return jnp.dot(pick, kept, preferred_element_type=F32, precision=lax.Precision.HIGHEST)


def _lru_bwd(dh, xc, hprev, u, conv_w, wr_blk, b_r, wi_blk, b_i, lru_lambda, head_dim, comm=None):
    t, w = dh.shape
    tc = min(TIME_CHUNK, t)
    steps = t // tc
    halo = SUBLANES
    sub_per_chunk = tc // halo
    slabs = w // LRU_SLAB

    def body(dh_ref, xc_ref, hp_ref, u_ref, uh_ref, cw_ref, wr_ref, br_ref, wi_ref, bi_ref, lam_ref,
             du_ref, dwr_out, dwi_out, vec_ref, lam_s, a_s, dxc_s, uext_s, carry_s, dwr_ref, dwi_ref):
        chunk = steps - 1 - pl.program_id(0)

        @pl.when(pl.program_id(0) == 0)
        def _():
            carry_s[...] = jnp.zeros_like(carry_s)
            dxc_s[tc:tc + halo, :] = jnp.zeros((halo, w), F32)
            dwr_ref[...] = jnp.zeros_like(dwr_ref)
            dwi_ref[...] = jnp.zeros_like(dwi_ref)
            vec_ref[...] = jnp.zeros_like(vec_ref)

        xc = xc_ref[...]
        r, ig, sp, log_a = _lru_gates(xc, wr_ref, br_ref, wi_ref, bi_ref, lam_ref)
        a = jnp.exp(log_a)
        a_s[...] = a

        def step(i, q):
            at = pl.ds(tc - 1 - i, 1)
            lam_row = dh_ref[at, :] + q
            lam_s[at, :] = lam_row
            return a_s[at, :] * lam_row

        carry_s[0:1, :] = lax.fori_loop(0, tc, step, carry_s[0:1, :], unroll=8)
        lam = lam_s[...]
        mult = jnp.sqrt(-_expm1(2.0 * log_a))
        d_log_a = lam * hp_ref[...] * a - (lam * ig * xc) * (a * a) / mult
        d_ig = lam * mult * xc
        dpre_r = (d_log_a * (-LRU_C * sp)) * r * (1.0 - r)
        dpre_i = d_ig * ig * (1.0 - ig)
        dxc = lam * mult * ig + _slab_dot(dpre_r, wr_ref, transposed=True) + _slab_dot(dpre_i, wi_ref, transposed=True)
        xcb, drb, dib = xc.astype(BF16), dpre_r.astype(BF16), dpre_i.astype(BF16)
        for j in range(slabs):
            cols = slice(j * LRU_SLAB, (j + 1) * LRU_SLAB)
            dwr_ref[j] += _dot_tn(drb[:, cols], xcb[:, cols])
            dwi_ref[j] += _dot_tn(dib[:, cols], xcb[:, cols])
        vec_ref[0:1, :] += _rowsum(dxc)
        vec_ref[1:2, :] += _rowsum(dpre_r)
        vec_ref[2:3, :] += _rowsum(dpre_i)
        vec_ref[3:4, :] += _rowsum(d_log_a * (-LRU_C * r)) * (-_sigmoid(-lam_ref[...]))
        dxc_s[0:tc, :] = dxc
        du = cw_ref[CONV_WIDTH - 1:CONV_WIDTH, :] * dxc
        for k in range(CONV_WIDTH - 1):
            off = CONV_WIDTH - 1 - k
            du = du + cw_ref[k:k + 1, :] * dxc_s[off:off + tc, :]
        du_ref[...] = du.astype(du_ref.dtype)
        dxc_s[tc:tc + halo, :] = dxc_s[0:halo, :]
        uext_s[0:halo, :] = jnp.where(chunk > 0, uh_ref[...], 0.0)
        uext_s[halo:halo + tc, :] = u_ref[...]
        for k in range(CONV_WIDTH):
            off = halo - (CONV_WIDTH - 1) + k
            vec_ref[4 + k:5 + k, :] += _rowsum(dxc * uext_s[off:off + tc, :])

        @pl.when(pl.program_id(0) == steps - 1)
        def _():
            for j in range(slabs):
                cols = slice(j * LRU_SLAB, (j + 1) * LRU_SLAB)
                dwr_out[:, cols] = _fold_diag_blocks(dwr_ref[j], head_dim, head_dim).astype(dwr_out.dtype)
                dwi_out[:, cols] = _fold_diag_blocks(dwi_ref[j], head_dim, head_dim).astype(dwi_out.dtype)

    halo_spec = pl.BlockSpec((halo, w), lambda i: (jnp.maximum((steps - 1 - i) * sub_per_chunk - 1, 0), 0))
    return _run(
        body, comm, name="lru_bwd", grid=(steps,),
        out_shape=[jax.ShapeDtypeStruct((t, w), BF16), jax.ShapeDtypeStruct((head_dim, w), BF16),
                   jax.ShapeDtypeStruct((head_dim, w), BF16), jax.ShapeDtypeStruct((SUBLANES, w), F32)],
        in_specs=[_rows_rev(tc, w, steps)] * 4 + [halo_spec, _whole((CONV_WIDTH, w)), _whole(wr_blk.shape),
                                                  _whole((1, w)), _whole(wi_blk.shape), _whole((1, w)), _whole((1, w))],
        out_specs=[_rows_rev(tc, w, steps), _acc((head_dim, w)), _acc((head_dim, w)), _acc((SUBLANES, w))],
        scratch_shapes=[pltpu.VMEM((tc, w), F32), pltpu.VMEM((tc, w), F32), pltpu.VMEM((tc + halo, w), F32),
                        pltpu.VMEM((halo + tc, w), F32), pltpu.VMEM((SUBLANES, w), F32),
                        pltpu.VMEM((slabs, LRU_SLAB, LRU_SLAB), F32), pltpu.VMEM((slabs, LRU_SLAB, LRU_SLAB), F32)],
        semantics="arbitrary",
    )(dh, xc, hprev, u, u, conv_w, wr_blk, b_r, wi_blk, b_i, lru_lambda)


def _s5_bwd(dya, y, sr, si, u, w_glu, b_glu, cre_blk, cimn_blk, bbr_blk, bbi_blk, ar, ai, d_skip, comm=None):
    t, sa = dya.shape
    gn = sr.shape[1]
    ns, _, sw = bbr_blk.shape
    tc = min(TIME_CHUNK, t)
    steps = t // tc
    halo = SUBLANES

    def body(dya_ref, y_ref, sr_ref, si_ref, u_ref, wg_ref, bg_ref, cre_ref, cim_ref, bbr_ref, bbi_ref,
             ar_ref, ai_ref, d_ref, du_ref, lr_ref, li_ref, dy_ref, dwg_out, vsa_ref, vgn_ref, gr_s, gi_s, cr_s, ci_s,
             dwg_ref):
        @pl.when(pl.program_id(0) == 0)
        def _():
            cr_s[...] = jnp.zeros_like(cr_s)
            ci_s[...] = jnp.zeros_like(ci_s)
            gr_s[tc:tc + halo, :] = jnp.zeros((halo, gn), F32)
            gi_s[tc:tc + halo, :] = jnp.zeros((halo, gn), F32)
            dwg_ref[...] = jnp.zeros_like(dwg_ref)
            vsa_ref[...] = jnp.zeros_like(vsa_ref)
            vgn_ref[...] = jnp.zeros_like(vgn_ref)

        yv = y_ref[...]
        uv = u_ref[...]
        zz = _gelu(yv)
        sg = _sigmoid(_dot(zz, wg_ref[...]) + bg_ref[...])
        dyav = dya_ref[...]
        dq = dyav * zz * sg * (1.0 - sg)
        dzz = dyav * sg + _dot_nt(dq, wg_ref[...])
        dwg_ref[...] += _dot_tn(zz, dq)
        dy = dzz * _gelu_grad(yv)
        dyb = dy.astype(BF16)
        dy_ref[...] = dyb.astype(dy_ref.dtype)
        vsa_ref[0:1, :] += _rowsum(dq)
        vsa_ref[1:2, :] += _rowsum(dy * uv)
        for m in range(ns):
            dym = dyb[:, m * S5_SLAB:(m + 1) * S5_SLAB]
            gr_s[0:tc, m * sw:(m + 1) * sw] = _dot_nt(dym, cre_ref[m])
            gi_s[0:tc, m * sw:(m + 1) * sw] = _dot_nt(dym, cim_ref[m])
        a_r = ar_ref[...]
        a_i = ai_ref[...]

        def step(i, carry):
            l_r, l_i = carry
            at = pl.ds(tc - 1 - i, 1)
            n_r = gr_s[at, :] + a_r * l_r + a_i * l_i
            n_i = gi_s[at, :] + a_r * l_i - a_i * l_r
            gr_s[at, :] = n_r
            gi_s[at, :] = n_i
            return n_r, n_i

        l_r, l_i = lax.fori_loop(0, tc, step, (cr_s[0:1, :], ci_s[0:1, :]), unroll=8)
        cr_s[0:1, :] = l_r
        ci_s[0:1, :] = l_i
        nxt_r = gr_s[1:tc + 1, :]
        nxt_i = gi_s[1:tc + 1, :]
        srv = sr_ref[...].astype(F32)
        siv = si_ref[...].astype(F32)
        vgn_ref[0:1, :] += _rowsum(nxt_r * srv + nxt_i * siv)
        vgn_ref[1:2, :] += _rowsum(nxt_i * srv - nxt_r * siv)
        lam_r = gr_s[0:tc, :]
        lam_i = gi_s[0:tc, :]
        gr_s[tc:tc + halo, :] = gr_s[0:halo, :]
        gi_s[tc:tc + halo, :] = gi_s[0:halo, :]
        lrb = lam_r.astype(BF16)
        lib = lam_i.astype(BF16)
        lr_ref[...] = lrb.astype(lr_ref.dtype)
        li_ref[...] = lib.astype(li_ref.dtype)
        for m in range(ns):
            states, chans = slice(m * sw, (m + 1) * sw), slice(m * S5_SLAB, (m + 1) * S5_SLAB)
            du_ref[:, chans] = (_dot_nt(lrb[:, states], bbr_ref[m]) + _dot_nt(lib[:, states], bbi_ref[m])
                                + dy[:, chans] * d_ref[:, chans]).astype(du_ref.dtype)
        _store_on_last([(dwg_ref, dwg_out)])

    return _run(
        body, comm, name="s5_bwd", grid=(steps,),
        out_shape=[jax.ShapeDtypeStruct((t, sa), BF16), jax.ShapeDtypeStruct((t, gn), BF16),
                   jax.ShapeDtypeStruct((t, gn), BF16), jax.ShapeDtypeStruct((t, sa), BF16),
                   jax.ShapeDtypeStruct((sa, sa), BF16), jax.ShapeDtypeStruct((SUBLANES, sa), F32),
                   jax.ShapeDtypeStruct((SUBLANES, gn), F32)],
        in_specs=[_rows_rev(tc, sa, steps), _rows_rev(tc, sa, steps), _rows_rev(tc, gn, steps), _rows_rev(tc, gn, steps),
                  _rows_rev(tc, sa, steps), _whole((sa, sa)), _whole((1, sa)), _whole(cre_blk.shape),
                  _whole(cimn_blk.shape), _whole(bbr_blk.shape), _whole(bbi_blk.shape), _whole((1, gn)), _whole((1, gn)),
                  _whole((1, sa))],
        out_specs=[_rows_rev(tc, sa, steps), _rows_rev(tc, gn, steps), _rows_rev(tc, gn, steps), _rows_rev(tc, sa, steps),
                   _acc((sa, sa)), _acc((SUBLANES, sa)), _acc((SUBLANES, gn))],
        scratch_shapes=[pltpu.VMEM((tc + halo, gn), F32), pltpu.VMEM((tc + halo, gn), F32),
                        pltpu.VMEM((SUBLANES, gn), F32), pltpu.VMEM((SUBLANES, gn), F32), pltpu.VMEM((sa, sa), F32)],
        semantics="arbitrary",
    )(dya, y, sr, si, u, w_glu, b_glu, cre_blk, cimn_blk, bbr_blk, bbi_blk, ar, ai, d_skip)


def _inproj_bwd(dparts, x, dx1, g_mix, w_in_t, comm=None):
    t, d = x.shape
    n = w_in_t.shape[0]
    widths = [p.shape[1] for p in dparts]
    offs = [sum(widths[:i]) for i in range(len(widths) + 1)]
    tm = min(MATMUL_TILE, t)
    np_ = len(dparts)

    def body(*refs):
        dz_refs = refs[:np_]
        x_ref, dx1_ref, g_ref, w_ref, gx_ref, h_ref, vd_ref, vn_ref = refs[np_:]
        _zero_on_first(vd_ref, vn_ref)
        dh = jnp.zeros((tm, d), F32)
        for k, r in enumerate(dz_refs):
            lo, hi = offs[k], offs[k + 1]
            dzk = r[...]
            dh = dh + _dot(dzk, w_ref[lo:hi, :])
            vn_ref[0:1, lo:hi] += _rowsum(dzk.astype(F32))
        xhat, r0 = _rms_stats(x_ref[...])
        h_ref[...] = (xhat * g_ref[...]).astype(h_ref.dtype)
        dxn, dg = _rms_bwd(dh, xhat, r0, g_ref[...])
        gx_ref[...] = dx1_ref[...] + dxn
        vd_ref[0:1, :] += _rowsum(dg)

    return _run(
        body, comm, name="inproj_bwd", grid=(t // tm,),
        out_shape=[jax.ShapeDtypeStruct((t, d), F32), jax.ShapeDtypeStruct((t, d), BF16),
                   jax.ShapeDtypeStruct((SUBLANES, d), F32), jax.ShapeDtypeStruct((SUBLANES, n), F32)],
        in_specs=[_rows(tm, w) for w in widths] + [_rows(tm, d), _rows(tm, d), _whole((1, d)), _whole((n, d))],
        out_specs=[_rows(tm, d), _rows(tm, d), _acc((SUBLANES, d)), _acc((SUBLANES, n))],
        semantics="arbitrary",
    )(*dparts, x, dx1, g_mix, w_in_t)


def _dw_from_parts(dparts, h, tn, name, comm=None):
    t, d = h.shape
    widths = [p.shape[1] for p in dparts]
    offs = [sum(widths[:i]) for i in range(len(widths) + 1)]
    np_ = len(dparts)

    def body(*refs):
        h_ref, o_ref = refs[np_], refs[np_ + 1]
        hv = h_ref[...]
        for k, r in enumerate(refs[:np_]):
            o_ref[offs[k]:offs[k + 1], :] = _dot_tn(r[...], hv).astype(o_ref.dtype)

    return _run(
        body, comm, name=name, grid=(d // tn,),
        out_shape=jax.ShapeDtypeStruct((offs[-1], d), BF16),
        in_specs=[_whole(p.shape) for p in dparts] + [pl.BlockSpec((t, tn), lambda j: (0, j))],
        out_specs=pl.BlockSpec((offs[-1], tn), lambda j: (0, j)),
        semantics="parallel",
    )(*dparts, h)


def _prep(lr_row, li_row, ldt_row, lr_col, li_col, ldt_col, b_re, b_im, c_re, c_im, w_r, w_i, comm=None):
    gn, pch = b_re.shape
    sa, n = c_re.shape
    w, hd = w_r.shape
    ns, sw, lsl = sa // S5_SLAB, S5_SLAB * n // pch, w // LRU_SLAB

    def spread_cols(vals, row_group, col_group, width):
        r, k = vals.shape
        tile = (lax.broadcasted_iota(jnp.int32, (k, width), 0) == lax.broadcasted_iota(jnp.int32, (k, width), 1) % k)
        rows = lax.broadcasted_iota(jnp.int32, (r, width), 0) // row_group
        cols = lax.broadcasted_iota(jnp.int32, (r, width), 1) // col_group
        return jnp.where(rows == cols, _dot(vals, tile.astype(BF16)), 0.0)

    def spread_rows(vals, row_group, col_group, height):
        k, c = vals.shape
        tile = (lax.broadcasted_iota(jnp.int32, (height, k), 0) % k == lax.broadcasted_iota(jnp.int32, (height, k), 1))
        rows = lax.broadcasted_iota(jnp.int32, (height, c), 0) // row_group
        cols = lax.broadcasted_iota(jnp.int32, (height, c), 1) // col_group
        return jnp.where(rows == cols, _dot(tile.astype(BF16), vals), 0.0)

    def body(lrr, lir, ldr, lrc, lic, ldc, bre, bim, cre, cim, wr, wi,
             ar_o, ai_o, bbr_o, bbi_o, cre_o, cim_o, wr_o, wi_o):
        ar, ai, _, _ = _disc_scalars(lrr[...], lir[...], ldr[...])
        ar_o[...] = ar
        ai_o[...] = ai
        _, _, bbr, bbi = _disc_cols(lrc[...], lic[...], ldc[...], bre[...], bim[...])
        bbr_t, bbi_t = bbr.T, bbi.T
        for m in range(ns):
            bbr_o[m] = spread_rows(bbr_t[:, m * sw:(m + 1) * sw], pch, n, S5_SLAB).astype(bbr_o.dtype)
            bbi_o[m] = spread_rows(bbi_t[:, m * sw:(m + 1) * sw], pch, n, S5_SLAB).astype(bbi_o.dtype)
            rows = slice(m * S5_SLAB, (m + 1) * S5_SLAB)
            cre_o[m] = spread_rows(cre[rows, :].T, n, pch, sw).astype(cre_o.dtype)
            cim_o[m] = spread_rows(-cim[rows, :].T, n, pch, sw).astype(cim_o.dtype)
        for j in range(lsl):
            rows = slice(j * LRU_SLAB, (j + 1) * LRU_SLAB)
            wr_o[j] = spread_cols(wr[rows, :], hd, hd, LRU_SLAB).astype(wr_o.dtype)
            wi_o[j] = spread_cols(wi[rows, :], hd, hd, LRU_SLAB).astype(wi_o.dtype)

    args = (lr_row, li_row, ldt_row, lr_col, li_col, ldt_col, b_re, b_im, c_re, c_im, w_r, w_i)
    out_shape = [jax.ShapeDtypeStruct((1, gn), F32), jax.ShapeDtypeStruct((1, gn), F32),
                 jax.ShapeDtypeStruct((ns, S5_SLAB, sw), BF16), jax.ShapeDtypeStruct((ns, S5_SLAB, sw), BF16),
                 jax.ShapeDtypeStruct((ns, sw, S5_SLAB), BF16), jax.ShapeDtypeStruct((ns, sw, S5_SLAB), BF16),
                 jax.ShapeDtypeStruct((lsl, LRU_SLAB, LRU_SLAB), BF16),
                 jax.ShapeDtypeStruct((lsl, LRU_SLAB, LRU_SLAB), BF16)]
    return _run(
        body, comm, name="prep", grid=(1,), out_shape=out_shape, in_specs=[_whole(a.shape) for a in args],
        out_specs=[_acc(s.shape) for s in out_shape], semantics="arbitrary",
    )(*args)


def _s5_param_grads(lam_r, lam_i, sr, si, u, dy, pch, n, comm=None):
    t, gn = lam_r.shape
    sa = u.shape[1]
    sw = S5_SLAB * n // pch

    def body(lr_ref, li_ref, sr_ref, si_ref, u_ref, dy_ref, dbr_ref, dbi_ref, dcr_ref, dci_ref):
        uv = u_ref[...]
        dyv = dy_ref[...]
        dbr_ref[...] = _fold_diag_blocks(_dot_tn(uv, lr_ref[...]), pch, n).astype(dbr_ref.dtype)
        dbi_ref[...] = _fold_diag_blocks(_dot_tn(uv, li_ref[...]), pch, n).astype(dbi_ref.dtype)
        dcr_ref[...] = _fold_diag_blocks(_dot_tn(sr_ref[...], dyv), n, pch).astype(dcr_ref.dtype)
        dci_ref[...] = _fold_diag_blocks(_dot_tn(si_ref[...], dyv), n, pch).astype(dci_ref.dtype)

    states = pl.BlockSpec((t, sw), lambda m: (0, m))
    chans = pl.BlockSpec((t, S5_SLAB), lambda m: (0, m))
    return _run(
        body, comm, name="s5_param_grads", grid=(sa // S5_SLAB,),
        out_shape=[jax.ShapeDtypeStruct((pch, gn), BF16), jax.ShapeDtypeStruct((pch, gn), BF16),
                   jax.ShapeDtypeStruct((n, sa), BF16), jax.ShapeDtypeStruct((n, sa), BF16)],
        in_specs=[states, states, states, states, chans, chans],
        out_specs=[pl.BlockSpec((pch, sw), lambda m: (0, m)), pl.BlockSpec((pch, sw), lambda m: (0, m)),
                   pl.BlockSpec((n, S5_SLAB), lambda m: (0, m)), pl.BlockSpec((n, S5_SLAB), lambda m: (0, m))],
        semantics="parallel",
    )(lam_r, lam_i, sr, si, u, dy)


SMALL_PARTS = ["vec_tail", "vec_ffn", "vec_lru", "vec_mix", "vec_bin", "vec_sa", "vec_gn", "dw_r", "dw_i", "dbb_re",
               "dbb_im", "dc_re", "dc_imn", "loss"]


def _small_reduce(parts, shapes, lr_col, li_col, ldt_col, b_re, b_im, groups, comm=None):
    gn, pch = b_re.shape
    n = gn // groups
    nparts = parts[SMALL_PARTS[0]].size // math.prod(shapes[SMALL_PARTS[0]])
    np_, nout = len(SMALL_PARTS), 24

    def body(*refs):
        ins = refs[:np_]
        lr, li, ldt, bre, bim = refs[np_:np_ + 5]
        outs = refs[np_ + 5:np_ + 5 + nout]
        sums = dict(zip(SMALL_PARTS, refs[np_ + 5 + nout:]))

        @pl.when(pl.program_id(0) == 0)
        def _():
            for k, r in zip(SMALL_PARTS, ins):
                sums[k][...] = r[...].astype(F32)

        @pl.when(pl.program_id(0) > 0)
        def _():
            for k, r in zip(SMALL_PARTS, ins):
                sums[k][...] += r[...].astype(F32)

        @pl.when(pl.program_id(0) == nparts - 1)
        def _():
            finish({k: s[...] for k, s in sums.items()}, lr, li, ldt, bre, bim, *outs)

    def finish(tot, lr, li, ldt, bre, bim, o_loss, o_gmix, o_bin, o_bglu, o_s5d, o_convb, o_br, o_bi, o_lam, o_gffn,
               o_gpg, o_bpg, o_gple, o_gfin, o_wr, o_wi, o_cre, o_cim, o_bre, o_bim, o_lre, o_lim, o_ldt, o_convw):
        o_loss[...] = tot["loss"]
        o_convw[...] = tot["vec_lru"][SUBLANES - CONV_WIDTH:SUBLANES]
        o_bpg[...] = tot["vec_tail"][0:1]
        o_gpg[...] = tot["vec_tail"][1:2]
        o_gple[...] = tot["vec_tail"][2:3]
        o_gfin[...] = tot["vec_tail"][3:4]
        o_gffn[...] = tot["vec_ffn"][0:1]
        o_convb[...] = tot["vec_lru"][0:1]
        o_br[...] = tot["vec_lru"][1:2]
        o_bi[...] = tot["vec_lru"][2:3]
        o_lam[...] = tot["vec_lru"][3:4]
        o_gmix[...] = tot["vec_mix"][0:1]
        o_bin[...] = tot["vec_bin"][0:1]
        o_bglu[...] = tot["vec_sa"][0:1]
        o_s5d[...] = tot["vec_sa"][1:2]
        o_wr[...] = tot["dw_r"].T
        o_wi[...] = tot["dw_i"].T
        o_cre[...] = tot["dc_re"].T
        o_cim[...] = -tot["dc_imn"].T
        d_a = tot["vec_gn"].T
        _, chain = jax.vjp(_disc_cols, lr[...], li[...], ldt[...], bre[...], bim[...])
        d_lr, d_li, d_ldt, d_bre, d_bim = chain((d_a[:, 0:1], d_a[:, 1:2], tot["dbb_re"].T, tot["dbb_im"].T))
        o_lre[...] = d_lr
        o_lim[...] = d_li
        o_bre[...] = d_bre
        o_bim[...] = d_bim
        same = (lax.broadcasted_iota(jnp.int32, (groups, gn), 0)
                == lax.broadcasted_iota(jnp.int32, (groups, gn), 1) // n).astype(F32)
        o_ldt[...] = jnp.dot(same, d_ldt * jnp.ones((1, LANES), F32), preferred_element_type=F32,
                             precision=lax.Precision.HIGHEST)[:, 0:1]

    d = shapes["vec_mix"][1]
    nz = shapes["vec_bin"][1]
    sa = shapes["vec_sa"][1]
    w = shapes["vec_lru"][1]
    row = lambda c: jax.ShapeDtypeStruct((1, c), F32)
    out_shape = [jax.ShapeDtypeStruct(shapes["loss"], F32), row(d), row(nz), row(sa), row(sa), row(w), row(w), row(w),
                 row(w), row(d), row(d), row(d), row(d), row(d),
                 jax.ShapeDtypeStruct(shapes["dw_r"][::-1], F32), jax.ShapeDtypeStruct(shapes["dw_i"][::-1], F32),
                 jax.ShapeDtypeStruct(shapes["dc_re"][::-1], F32), jax.ShapeDtypeStruct(shapes["dc_imn"][::-1], F32),
                 jax.ShapeDtypeStruct((gn, pch), F32), jax.ShapeDtypeStruct((gn, pch), F32),
                 jax.ShapeDtypeStruct((gn, 1), F32), jax.ShapeDtypeStruct((gn, 1), F32),
                 jax.ShapeDtypeStruct((groups, 1), F32), jax.ShapeDtypeStruct((CONV_WIDTH, w), F32)]
    def part_spec(k):
        r, c = shapes[k]
        if parts[k].ndim == 3:
            return pl.BlockSpec((None, r, c), lambda i: (i, 0, 0))
        return pl.BlockSpec((r, c), lambda i: (i, 0))

    outs = _run(
        body, comm, name="small_reduce", grid=(nparts,), out_shape=out_shape,
        in_specs=[part_spec(k) for k in SMALL_PARTS] + [_whole(a.shape) for a in (lr_col, li_col, ldt_col, b_re, b_im)],
        out_specs=[_acc(s.shape) for s in out_shape],
        scratch_shapes=[pltpu.VMEM(shapes[k], F32) for k in SMALL_PARTS],
        semantics="arbitrary",
    )(*[parts[k] for k in SMALL_PARTS], lr_col, li_col, ldt_col, b_re, b_im)
    extra = None
    if comm is not None:
        outs, extra = outs
    names = ["loss", "g_mix", "b_in", "b_glu", "s5_d", "conv_b", "b_r", "b_i", "lru_lambda", "g_ffn", "g_ple_gate",
             "b_ple_gate", "g_ple", "g_final", "w_r", "w_i", "s5_c_re", "s5_c_im", "s5_b_re", "s5_b_im", "lam_re",
             "lam_im", "log_dt", "conv_w"]
    res = dict(zip(names, outs))
    return res if comm is None else (res, extra)


def _adamw_small(ws, gs, ms, vs):
    n = len(ws)

    def body(*refs):
        w_r, g_r, m_r, v_r = (refs[i * n:(i + 1) * n] for i in range(4))
        g_o, d_o, m_o, v_o = (refs[(4 + i) * n:(5 + i) * n] for i in range(4))
        for i in range(n):
            g = g_r[i][...]
            delta, m_new, v_new = _adamw_math(w_r[i][...], g, m_r[i][...], v_r[i][...])
            g_o[i][...] = g
            d_o[i][...] = delta
            m_o[i][...] = m_new
            v_o[i][...] = v_new

    shapes = [jax.ShapeDtypeStruct(a.shape, F32) for a in ws]
    outs = pl.pallas_call(body, name="adamw_small", out_shape=shapes * 4)(*ws, *gs, *ms, *vs)
    return outs[:n], outs[n:2 * n], outs[2 * n:3 * n], outs[3 * n:]


def _adamw_math(w, g, m, v):
    m_new = ADAM_B1 * m + (1.0 - ADAM_B1) * g
    v_new = ADAM_B2 * v + (1.0 - ADAM_B2) * (g * g)
    m_hat = m_new / (1.0 - ADAM_B1 ** ADAM_STEP)
    v_hat = v_new / (1.0 - ADAM_B2 ** ADAM_STEP)
    delta = -ADAM_LR * (m_hat / (jnp.sqrt(v_hat) + ADAM_EPS) + ADAM_WD * w)
    return delta, m_new, v_new


def _row_tile(rows):
    for cand in range(256, 0, -16):
        if rows % cand == 0:
            return cand
    return rows


def _adamw(parts, w, m, v, name, transposed=False):
    nw = len(w)
    rows, cols = w[0].shape
    npart = parts[0].shape[0]
    tr = _row_tile(rows)
    if transposed:
        parts_spec = pl.BlockSpec((npart, cols, tr), lambda i: (0, 0, i))
    else:
        parts_spec = pl.BlockSpec((npart, tr, cols), lambda i: (0, i, 0))

    def body(*refs):
        for i in range(nw):
            p_ref, w_ref, m_ref, v_ref = refs[4 * i:4 * i + 4]
            g_ref, d_ref, mo_ref, vo_ref = refs[4 * (nw + i):4 * (nw + i) + 4]
            g = p_ref[0].astype(F32)
            for k in range(1, npart):
                g = g + p_ref[k].astype(F32)
            if transposed:
                g = g.T
            delta, m_new, v_new = _adamw_math(w_ref[...], g, m_ref[...], v_ref[...])
            g_ref[...] = g
            d_ref[...] = delta
            mo_ref[...] = m_new
            vo_ref[...] = v_new

    res = pl.pallas_call(
        body, name=name, grid=(rows // tr,),
        out_shape=[jax.ShapeDtypeStruct((rows, cols), F32)] * (4 * nw),
        in_specs=([parts_spec] + [_rows(tr, cols)] * 3) * nw,
        out_specs=[_rows(tr, cols)] * (4 * nw),
        compiler_params=_params("parallel"),
    )(*[a for group in zip(parts, w, m, v) for a in group])
    return [res[4 * i:4 * i + 4] for i in range(nw)]


def _mesh_position():
    return lax.axis_index("x"), lax.axis_index("y"), lax.axis_index("c")


def _flip(pos, rel):
    x, y, c = pos
    return (1 - x if rel & 4 else x, 1 - y if rel & 2 else y, 1 - c if rel & 1 else c)


def _index(pos):
    return 4 * pos[0] + 2 * pos[1] + pos[2]


_ANY = pl.BlockSpec(memory_space=pl.ANY)
FLAT_ROWS = 32


def _dma_sems(n):
    return [pltpu.SemaphoreType.DMA((n, N_DEV - 1)), pltpu.SemaphoreType.DMA((n, N_DEV - 1)), pltpu.SemaphoreType.DMA((n,))]


def _block_of(ref, idx, rows, flat):
    if flat:
        return ref.at[pl.ds(pl.multiple_of(idx * rows, FLAT_ROWS), rows), :]
    return ref.at[idx]


class _Gather:
    chips = (4, 2, 6)
    rels = frozenset((1, 4, 2, 6))

    def __init__(self, shards):
        self.inputs = list(shards)
        self.flat = [s.shape[0] % FLAT_ROWS == 0 for s in shards]
        self.out_shape = [
            jax.ShapeDtypeStruct((N_DEV * s.shape[0], s.shape[1]) if f else (N_DEV,) + s.shape, s.dtype)
            for s, f in zip(shards, self.flat)]
        self.sems = _dma_sems(len(shards))

    def _copy(self, ins, outs, sems, i, k, block, to, own=False):
        dst = _block_of(outs[i], _index(block), self.inputs[i].shape[0], self.flat[i])
        return pltpu.make_async_remote_copy(
            src_ref=ins[i] if own else dst, dst_ref=dst, send_sem=sems[0].at[i, k], recv_sem=sems[1].at[i, k],
            device_id=to, device_id_type=MESH)

    def _local(self, ins, outs, sems, i, me):
        dst = _block_of(outs[i], _index(me), self.inputs[i].shape[0], self.flat[i])
        return pltpu.make_async_copy(ins[i], dst, sems[2].at[i])

    def _first(self, ins, outs, sems, i, me):
        cps = [self._copy(ins, outs, sems, i, 0, me, _flip(me, 1), own=True)]
        cps += [self._copy(ins, outs, sems, i, 1 + j, me, _flip(me, rel), own=True) for j, rel in enumerate(self.chips)]
        return cps

    def _passed(self, ins, outs, sems, i, j, me):
        return self._copy(ins, outs, sems, i, 4 + j, _flip(me, self.chips[j]), _flip(me, 1))

    def before(self, ins, outs, sems):
        n = len(self.inputs)
        me = _mesh_position()

        @pl.when(pl.program_id(0) == 0)
        def _():
            for i in range(n):
                self._local(ins, outs, sems, i, me).start()
                for cp in self._first(ins, outs, sems, i, me):
                    cp.start()

        @pl.when(pl.program_id(0) == pl.num_programs(0) - 1)
        def _():
            for j, rel in enumerate(self.chips):
                for i in range(n):
                    self._copy(ins, outs, sems, i, 1 + j, _flip(me, rel), me).wait_recv()
                    self._passed(ins, outs, sems, i, j, me).start()

    def after(self, ins, outs, sems):
        n = len(self.inputs)
        me = _mesh_position()
        sibling = _flip(me, 1)

        @pl.when(pl.program_id(0) == pl.num_programs(0) - 1)
        def _():
            for i in range(n):
                self._copy(ins, outs, sems, i, 0, sibling, me).wait_recv()
                for j, rel in enumerate(self.chips):
                    self._copy(ins, outs, sems, i, 4 + j, _flip(sibling, rel), me).wait_recv()
            for i in range(n):
                for cp in self._first(ins, outs, sems, i, me):
                    cp.wait_send()
                for j in range(len(self.chips)):
                    self._passed(ins, outs, sems, i, j, me).wait_send()
                self._local(ins, outs, sems, i, me).wait()


N_CHIPS = N_DEV // 2


def _chip(pos):
    return 2 * pos[0] + pos[1]


class _PairSwap:
    rels = frozenset((1,))

    def __init__(self, arrays):
        self.inputs = list(arrays)
        self.rows = [a.shape[0] // N_DEV for a in arrays]
        for r in self.rows:
            assert r % FLAT_ROWS == 0, r
        self.out_shape = [jax.ShapeDtypeStruct((N_CHIPS, r, a.shape[1]), a.dtype) for a, r in zip(arrays, self.rows)]
        n = len(arrays)
        self.sems = [pltpu.SemaphoreType.DMA((n, N_CHIPS)), pltpu.SemaphoreType.DMA((n, N_CHIPS))]

    def _copy(self, ins, outs, sems, i, j, me):
        sibling = _flip(me, 1)
        return pltpu.make_async_remote_copy(
            src_ref=_block_of(ins[i], 2 * j + sibling[2], self.rows[i], True), dst_ref=outs[i].at[j],
            send_sem=sems[0].at[i, j], recv_sem=sems[1].at[i, j], device_id=sibling, device_id_type=MESH)

    def before(self, ins, outs, sems):
        me = _mesh_position()

        @pl.when(pl.program_id(0) == 0)
        def _():
            for i in range(len(self.inputs)):
                for j in range(N_CHIPS):
                    self._copy(ins, outs, sems, i, j, me).start()

    def after(self, ins, outs, sems):
        me = _mesh_position()

        @pl.when(pl.program_id(0) == pl.num_programs(0) - 1)
        def _():
            for i in range(len(self.inputs)):
                for j in range(N_CHIPS):
                    self._copy(ins, outs, sems, i, j, me).wait()


class _ChipExchange:
    chips = (4, 2, 6)
    rels = frozenset(chips)

    def __init__(self, arrays):
        self.inputs = list(arrays)
        self.out_shape = [jax.ShapeDtypeStruct(a.shape, a.dtype) for a in arrays]
        n = len(arrays)
        self.sems = [pltpu.SemaphoreType.DMA((n, 3)), pltpu.SemaphoreType.DMA((n, 3)), pltpu.SemaphoreType.DMA((n,))]

    def _send(self, ins, outs, sems, i, k, me):
        peer = _flip(me, self.chips[k])
        return pltpu.make_async_remote_copy(
            src_ref=ins[i].at[_chip(peer)], dst_ref=outs[i].at[_chip(me)], send_sem=sems[0].at[i, k],
            recv_sem=sems[1].at[i, k], device_id=peer, device_id_type=MESH)

    def _arrival(self, ins, outs, sems, i, k, me):
        peer = _flip(me, self.chips[k])
        return pltpu.make_async_remote_copy(
            src_ref=ins[i].at[_chip(me)], dst_ref=outs[i].at[_chip(peer)], send_sem=sems[0].at[i, k],
            recv_sem=sems[1].at[i, k], device_id=peer, device_id_type=MESH)

    def _local(self, ins, outs, sems, i, me):
        return pltpu.make_async_copy(ins[i].at[_chip(me)], outs[i].at[_chip(me)], sems[2].at[i])

    def before(self, ins, outs, sems):
        me = _mesh_position()

        @pl.when(pl.program_id(0) == 0)
        def _():
            for i in range(len(self.inputs)):
                self._local(ins, outs, sems, i, me).start()
                for k in range(len(self.chips)):
                    self._send(ins, outs, sems, i, k, me).start()

    def after(self, ins, outs, sems):
        me = _mesh_position()

        @pl.when(pl.program_id(0) == pl.num_programs(0) - 1)
        def _():
            for i in range(len(self.inputs)):
                for k in range(len(self.chips)):
                    self._arrival(ins, outs, sems, i, k, me).wait_recv()
            for i in range(len(self.inputs)):
                for k in range(len(self.chips)):
                    self._send(ins, outs, sems, i, k, me).wait_send()
                self._local(ins, outs, sems, i, me).wait()


def _pair_add(grads, halves, name):
    n = len(grads)

    def body(*refs):
        g_refs, h_refs, o_refs = refs[:n], refs[n:2 * n], refs[2 * n:]
        c = lax.axis_index("c")
        for i in range(n):
            r = h_refs[i].shape[1]
            for j in range(N_CHIPS):
                own = g_refs[i][pl.ds(pl.multiple_of((2 * j + c) * r, FLAT_ROWS), r), :]
                o_refs[i][j] = (own.astype(F32) + h_refs[i][j].astype(F32)).astype(o_refs[i].dtype)

    return pl.pallas_call(
        body, name=name, out_shape=[jax.ShapeDtypeStruct(h.shape, h.dtype) for h in halves],
        compiler_params=pltpu.CompilerParams(vmem_limit_bytes=VMEM_LIMIT),
    )(*grads, *halves)


class _Both:
    def __init__(self, first, second):
        self.jobs = (first, second)
        self.rels = first.rels | second.rels
        self.inputs = first.inputs + second.inputs
        self.out_shape = first.out_shape + second.out_shape
        self.sems = first.sems + second.sems

    def _each(self, ins, outs, sems):
        a = self.jobs[0]
        i, o, s = len(a.inputs), len(a.out_shape), len(a.sems)
        return ((a, ins[:i], outs[:o], sems[:s]), (self.jobs[1], ins[i:], outs[o:], sems[s:]))

    def before(self, ins, outs, sems):
        for job, i, o, s in self._each(ins, outs, sems):
            job.before(i, o, s)

    def after(self, ins, outs, sems):
        for job, i, o, s in self._each(ins, outs, sems):
            job.after(i, o, s)


_COLLECTIVE_IDS = {(1,): 0, (2, 4, 6): 1, (1, 2, 4, 6): 2}


def _entry_barrier(rels):
    @pl.when(pl.program_id(0) == 0)
    def _():
        me = _mesh_position()
        sem = pltpu.get_barrier_semaphore()
        for rel in rels:
            pl.semaphore_signal(sem, inc=1, device_id=_flip(me, rel), device_id_type=MESH)
        pl.semaphore_wait(sem, len(rels))


def _run(body, comm, *, semantics, out_shape, in_specs, out_specs, scratch_shapes=(), **kw):
    if comm is None:
        return pl.pallas_call(body, out_shape=out_shape, in_specs=in_specs, out_specs=out_specs,
                              scratch_shapes=list(scratch_shapes), compiler_params=_params(semantics), **kw)
    single = not isinstance(out_shape, (list, tuple))
    outs = [out_shape] if single else list(out_shape)
    ospecs = [out_specs] if single else list(out_specs)
    counts = [len(in_specs), len(comm.inputs), len(outs), len(comm.out_shape), len(scratch_shapes), len(comm.sems)]
    rels = tuple(sorted(comm.rels))

    def carrying(*refs):
        groups, pos = [], 0
        for c in counts:
            groups.append(refs[pos:pos + c])
            pos += c
        main_in, comm_in, main_out, comm_out, main_scratch, comm_sems = groups
        _entry_barrier(rels)
        comm.before(comm_in, comm_out, comm_sems)
        body(*main_in, *main_out, *main_scratch)
        comm.after(comm_in, comm_out, comm_sems)

    call = pl.pallas_call(
        carrying, out_shape=outs + list(comm.out_shape), in_specs=list(in_specs) + [_ANY] * len(comm.inputs),
        out_specs=ospecs + [_ANY] * len(comm.out_shape), scratch_shapes=list(scratch_shapes) + list(comm.sems),
        compiler_params=pltpu.CompilerParams(dimension_semantics=("arbitrary",), vmem_limit_bytes=VMEM_LIMIT,
                                             collective_id=_COLLECTIVE_IDS[rels]), **kw)

    def apply(*args):
        res = call(*args, *comm.inputs)
        main = res[:len(outs)]
        return (main[0] if single else list(main)), list(res[len(outs):])

    return apply


def _alone(comm, name):
    return _run(lambda: None, comm, semantics="arbitrary", name=name, grid=(1,), out_shape=[], in_specs=[], out_specs=[])()[1]


SHARDED = {"w_in": 1, "w_glu": 0, "conv_w": 1, "w_a_out": 1, "w_b_out": 0, "w_o": 0, "w_ffn_gate": 1, "w_ffn_up": 1,
           "w_ffn_down": 0, "w_ple_gate": 0, "w_ple": 1}
TRANSPOSED = ("w_in", "w_a_out", "w_ffn_gate", "w_ffn_up", "w_ple")
LONG_AXIS_MINOR = ("w_in", "w_ffn_gate", "w_ffn_up")
NARROW_LAST = ("s5_b_re", "s5_b_im", "s5_d")
ADAMW_GROUPS = (("w_in",), ("w_glu",), ("conv_w",), ("w_a_out",), ("w_b_out", "w_o", "w_ple_gate"),
                ("w_ffn_gate", "w_ffn_up"), ("w_ffn_down",), ("w_ple",))
SMALL = ["g_mix", "b_in", "lam_re", "lam_im", "log_dt", "s5_b_re", "s5_b_im", "s5_c_re", "s5_c_im", "s5_d", "b_glu",
         "conv_b", "w_r", "b_r", "w_i", "b_i", "lru_lambda", "g_ffn", "g_ple_gate", "b_ple_gate", "g_ple", "g_final"]
WEIGHTS = ["g_mix", "w_in", "b_in", "lam_re", "lam_im", "log_dt", "s5_b_re", "s5_b_im", "s5_c_re", "s5_c_im", "s5_d",
           "w_glu", "b_glu", "conv_w", "conv_b", "w_r", "b_r", "w_i", "b_i", "lru_lambda", "w_a_out", "w_b_out", "w_o",
           "g_ffn", "w_ffn_gate", "w_ffn_up", "w_ffn_down", "g_ple_gate", "w_ple_gate", "b_ple_gate", "w_ple", "g_ple",
           "g_final"]


def _unblock(gathered, axis):
    nb, r, c = gathered.shape
    if axis == 0:
        return gathered.reshape(nb * r, c)
    return jnp.transpose(gathered, (1, 0, 2)).reshape(r, nb * c)


def _disc_scalars(lr, li, ldt):
    dt = jnp.exp(ldt)
    mag = jnp.exp(lr * dt)
    ar = mag * jnp.cos(li * dt)
    ai = mag * jnp.sin(li * dt)
    den = lr * lr + li * li
    nr = ar - 1.0
    fr = (nr * lr + ai * li) / den
    fi = (ai * lr - nr * li) / den
    return ar, ai, fr, fi


def _disc_cols(lr, li, ldt, b_re, b_im):
    ar, ai, fr, fi = _disc_scalars(lr, li, ldt)
    return ar, ai, fr * b_re - fi * b_im, fr * b_im + fi * b_re


def _local_step(x, p, target, src, small, disc, distributed=True):
    full = {} if distributed else dict(src)
    gw, halves, pairs, got = {}, {}, {}, {}

    def gather(keys):
        return (_Gather([src[k] for k in keys]), keys, full) if distributed else None

    def swap(keys):
        return (_PairSwap([gw[k] for k in keys]), keys, halves) if distributed else None

    def chips(keys):
        return (_ChipExchange([pairs[k] for k in keys]), keys, got) if distributed else None

    def add_pairs(keys):
        if distributed:
            pairs.update(zip(keys, _pair_add([gw[k] for k in keys], [halves[k] for k in keys], "pair_add_" + keys[0])))

    def carry(fn, *args, jobs=()):
        jobs = [j for j in jobs if j is not None]
        if not jobs:
            return fn(*args)
        comm = jobs[0][0]
        for j in jobs[1:]:
            comm = _Both(comm, j[0])
        res, extra = fn(*args, comm=comm)
        for job, keys, sink in jobs:
            sink.update(zip(keys, extra[:len(job.out_shape)]))
            extra = extra[len(job.out_shape):]
        return res

    d = x.shape[1]
    g, n, pch = small["s5_b_re"].shape
    heads = small["w_r"].shape[0]
    sa, lw = g * pch, small["lru_lambda"].shape[-1]
    widths = [sa, lw, d, d]
    row = lambda v: v.reshape(1, -1)

    hd = small["w_r"].shape[-1]
    ar_row, ai_row, bbr_blk, bbi_blk, cre_blk, cimn_blk, wr_blk, wi_blk = carry(
        _prep, *disc["rows"], *disc["cols"], disc["b_re"], disc["b_im"], small["s5_c_re"].reshape(sa, n),
        small["s5_c_im"].reshape(sa, n), small["w_r"].reshape(lw, hd), small["w_i"].reshape(lw, hd),
        jobs=[gather(["w_in"])])
    d_row = row(small["s5_d"])
    u_a, u_b, za, zb = carry(_inproj_fwd, x, row(small["g_mix"]), full["w_in"], row(small["b_in"]), widths,
                             jobs=[gather(["w_glu", "conv_w", "w_a_out", "w_b_out"])])
    conv_w = _unblock(full["conv_w"], 1) if distributed else full["conv_w"]
    sr, si, y, y_a = carry(_s5_fwd, u_a, bbr_blk, bbi_blk, ar_row, ai_row, cre_blk, cimn_blk, d_row, full["w_glu"],
                           row(small["b_glu"]), jobs=[gather(["w_ffn_gate", "w_o"])])
    xc, h, hprev = carry(_lru_fwd, u_b, conv_w, row(small["conv_b"]), wr_blk, row(small["b_r"]), wi_blk,
                         row(small["b_i"]), row(small["lru_lambda"]), jobs=[gather(["w_ffn_up"])])
    x1, merged, ma, mb = carry(_merge_fwd, y_a, h, za, zb, x, full["w_a_out"], full["w_b_out"], full["w_o"],
                               jobs=[gather(["w_ffn_down"])])
    fg, fu = carry(_ffn_up_fwd, x1, row(small["g_ffn"]), full["w_ffn_gate"], full["w_ffn_up"],
                   jobs=[gather(["w_ple_gate", "w_ple"])])
    x2 = _ffn_down_fwd(fg, fu, x1, full["w_ffn_down"])

    dx2, loss_blk, gw["w_ple_gate"], gw["w_ple"], vec_tail = _tail_fwd_bwd(
        x2, p, target, row(small["g_ple_gate"]), full["w_ple_gate"], row(small["b_ple_gate"]), full["w_ple"],
        row(small["g_ple"]), row(small["g_final"]))
    dfg, dfu, act = carry(_ffn_bwd_a, dx2, fg, fu, full["w_ffn_down"], jobs=[swap(["w_ple_gate", "w_ple"])])
    tn_d = min(d, 512)
    gw["w_ffn_down"] = _matmul_tn(act, dx2, tn_d, "dw_ffn_down", BF16)
    add_pairs(["w_ple_gate", "w_ple"])
    dx1, h2, vec_ffn = carry(_ffn_bwd_b, dfg, dfu, x1, dx2, row(small["g_ffn"]), full["w_ffn_gate"], full["w_ffn_up"],
                             jobs=[chips(["w_ple_gate", "w_ple"]), swap(["w_ffn_down"])])
    gw["w_ffn_gate"] = _matmul_tn(dfg, h2, tn_d, "dw_ffn_gate", BF16)
    gw["w_ffn_up"] = _matmul_tn(dfu, h2, tn_d, "dw_ffn_up", BF16)
    add_pairs(["w_ffn_down"])
    dza, dzb, dya, dyb, gw["w_o"], gw["w_a_out"], gw["w_b_out"] = carry(
        _merge_bwd, dx1, merged, ma, mb, za, zb, y_a, h, full["w_o"], full["w_a_out"], full["w_b_out"],
        jobs=[chips(["w_ffn_down"]), swap(["w_ffn_gate", "w_ffn_up"])])
    add_pairs(["w_ffn_gate", "w_ffn_up"])
    du_b, dw_r, dw_i, vec_lru = carry(
        _lru_bwd, dyb, xc, hprev, u_b, conv_w, wr_blk, row(small["b_r"]), wi_blk, row(small["b_i"]),
        row(small["lru_lambda"]), hd, jobs=[chips(["w_ffn_gate", "w_ffn_up"]), swap(["w_o", "w_a_out", "w_b_out"])])
    add_pairs(["w_o", "w_a_out", "w_b_out"])
    du_a, lam_r, lam_i, dy16, gw["w_glu"], vec_sa, vec_gn = carry(
        _s5_bwd, dya, y, sr, si, u_a, full["w_glu"], row(small["b_glu"]), cre_blk, cimn_blk, bbr_blk, bbi_blk, ar_row,
        ai_row, d_row, jobs=[chips(["w_o", "w_a_out", "w_b_out"])])
    smalls = {"vec_tail": vec_tail, "vec_ffn": vec_ffn, "vec_lru": vec_lru, "vec_sa": vec_sa, "vec_gn": vec_gn,
              "dw_r": dw_r, "dw_i": dw_i, "loss": loss_blk}
    everyones = {}

    def gather_smalls(keys):
        return (_Gather([smalls[k] for k in keys]), keys, everyones) if distributed else None

    smalls["dbb_re"], smalls["dbb_im"], smalls["dc_re"], smalls["dc_imn"] = carry(
        _s5_param_grads, lam_r, lam_i, sr, si, u_a, dy16, pch, n, jobs=[swap(["w_glu"]), gather_smalls(list(smalls))])
    add_pairs(["w_glu"])
    dz = [du_a, du_b, dza, dzb]
    grad_x, h0, smalls["vec_mix"], smalls["vec_bin"] = _inproj_bwd(
        dz, x, dx1, row(small["g_mix"]), full["w_in"])
    shapes = {k: a.shape for k, a in smalls.items()}
    gw["w_in"] = carry(_dw_from_parts, dz, h0, tn_d, "dw_in",
                       jobs=[chips(["w_glu"]), gather_smalls(["dbb_re", "dbb_im", "dc_re", "dc_imn"])])
    smalls.update(everyones)
    if distributed:
        got["w_in"] = gw["w_in"]
        gw = got
    return grad_x, gw, smalls, shapes


def _disc_inputs(small):
    g, n, pch = small["s5_b_re"].shape
    srcs = (small["lam_re"], small["lam_im"], jnp.repeat(small["log_dt"], n))
    return {"rows": [a.reshape(1, g * n) for a in srcs], "cols": [a.reshape(g * n, 1) for a in srcs],
            "b_re": small["s5_b_re"].reshape(g * n, pch), "b_im": small["s5_b_im"].reshape(g * n, pch)}


def kernel(x, p, g_mix, w_in, b_in, lam_re, lam_im, log_dt, s5_b_re, s5_b_im, s5_c_re, s5_c_im, s5_d, w_glu, b_glu, conv_w, conv_b, w_r, b_r, w_i, b_i, lru_lambda, w_a_out, w_b_out, w_o, g_ffn, w_ffn_gate, w_ffn_up, w_ffn_down, g_ple_gate, w_ple_gate, b_ple_gate, w_ple, g_ple, g_final, loss_target, m_g_mix, m_w_in, m_b_in, m_lam_re, m_lam_im, m_log_dt, m_s5_b_re, m_s5_b_im, m_s5_c_re, m_s5_c_im, m_s5_d, m_w_glu, m_b_glu, m_conv_w, m_conv_b, m_w_r, m_b_r, m_w_i, m_b_i, m_lru_lambda, m_w_a_out, m_w_b_out, m_w_o, m_g_ffn, m_w_ffn_gate, m_w_ffn_up, m_w_ffn_down, m_g_ple_gate, m_w_ple_gate, m_b_ple_gate, m_w_ple, m_g_ple, m_g_final, v_g_mix, v_w_in, v_b_in, v_lam_re, v_lam_im, v_log_dt, v_s5_b_re, v_s5_b_im, v_s5_c_re, v_s5_c_im, v_s5_d, v_w_glu, v_b_glu, v_conv_w, v_conv_b, v_w_r, v_b_r, v_w_i, v_b_i, v_lru_lambda, v_w_a_out, v_w_b_out, v_w_o, v_g_ffn, v_w_ffn_gate, v_w_ffn_up, v_w_ffn_down, v_g_ple_gate, v_w_ple_gate, v_b_ple_gate, v_w_ple, v_g_ple, v_g_final):
    given = dict(locals())
    wts = {k: given[k] for k in WEIGHTS}
    moms = {k: given["m_" + k] for k in WEIGHTS}
    vels = {k: given["v_" + k] for k in WEIGHTS}

    def drop_depth(k, a):
        return a if k == "g_final" else a[0]

    small = {k: drop_depth(k, wts[k]) for k in SMALL}
    shard = {k: wts[k][0] for k in SHARDED}
    names = list(SHARDED)

    def wire(k):
        if k == "conv_w":
            return shard[k]
        return (shard[k].T if k in TRANSPOSED else shard[k]).astype(BF16)

    disc = _disc_inputs(small)
    grad_x, parts, smalls, shapes = _local_step(x[0], p[0, 0], loss_target[0], {k: wire(k) for k in names}, small, disc)

    late = ["vec_mix", "vec_bin"]
    received = _alone(_Both(_PairSwap([parts["w_in"]]), _Gather([smalls[k] for k in late])), "swap_last")
    (pair_in,) = _pair_add([parts["w_in"]], received[:1], "pair_add_w_in")
    smalls.update(zip(late, received[1:]))

    g_small, (parts["w_in"],) = _small_reduce(smalls, shapes, *disc["cols"], disc["b_re"], disc["b_im"],
                                              small["s5_b_re"].shape[0], comm=_ChipExchange([pair_in]))
    loss = g_small.pop("loss")[0, 0]
    cols = shard["conv_w"].shape[1]
    mine = _index((lax.axis_index("x"), lax.axis_index("y"), lax.axis_index("c")))
    parts["conv_w"] = lax.dynamic_slice_in_dim(g_small.pop("conv_w"), mine * cols, cols, axis=1)[None]

    def view(k, a):
        a = a.reshape((1, -1) if k == "g_final" else wts[k].shape)
        return jnp.swapaxes(a, -1, -2) if k in NARROW_LAST else a

    def unview(k, a):
        return (jnp.swapaxes(a, -1, -2) if k in NARROW_LAST else a).reshape(wts[k].shape)

    slots = _adamw_small([view(k, wts[k]) for k in SMALL], [view(k, g_small[k]) for k in SMALL],
                         [view(k, moms[k]) for k in SMALL], [view(k, vels[k]) for k in SMALL])
    small_out = [dict(zip(SMALL, [unview(k, a) for k, a in zip(SMALL, slot)])) for slot in slots]

    big_out = {}
    for group in ADAMW_GROUPS:
        flip = group[0] in LONG_AXIS_MINOR
        look = (lambda a: a.T) if flip else (lambda a: a)
        res = _adamw([parts[k] for k in group], [look(shard[k]) for k in group], [look(moms[k][0]) for k in group],
                     [look(vels[k][0]) for k in group], "adamw_" + group[0],
                     transposed=group[0] in TRANSPOSED and not flip)
        for k, outs4 in zip(group, res):
            big_out[k] = [look(a) for a in outs4]

    outs = [loss, grad_x[None]]
    for slot in range(4):
        for k in WEIGHTS:
            if k in SHARDED:
                outs.append(big_out[k][slot][None])
            else:
                outs.append(small_out[slot][k])
    return tuple(outs)
```

```python
import math

import jax
import jax.numpy as jnp
from jax import lax
from jax.experimental import pallas as pl
from jax.experimental.pallas import tpu as pltpu

F32 = jnp.float32
BF16 = jnp.bfloat16

EPS = 1e-6
LRU_C = 8.0
CONV_WIDTH = 4
ADAM_LR = 0.001
ADAM_B1 = 0.9
ADAM_B2 = 0.999
ADAM_EPS = 1e-08
ADAM_WD = 0.01
ADAM_STEP = 10

N_DEV = 8
MESH = pl.DeviceIdType.MESH
SUBLANES = 8
LANES = 128
VMEM_LIMIT = 56 * 1024 * 1024
TOKEN_TILE = 256
MATMUL_TILE = 256
TIME_CHUNK = 256
S5_SLAB = 128
LRU_SLAB = 256


def _dot(a, b):
    return jnp.dot(a.astype(BF16), b.astype(BF16), preferred_element_type=F32)


def _dot_nt(a, b):
    return lax.dot_general(a.astype(BF16), b.astype(BF16), (((1,), (1,)), ((), ())), preferred_element_type=F32)


def _dot_tn(a, b):
    return lax.dot_general(a.astype(BF16), b.astype(BF16), (((0,), (0,)), ((), ())), preferred_element_type=F32)


def _sigmoid(x):
    return jax.nn.sigmoid(x)


def _rms_stats(x):
    r = lax.rsqrt(jnp.mean(x * x, axis=-1, keepdims=True) + EPS)
    return x * r, r


def _rms_bwd(dy, xhat, r, g):
    dxn = dy * g
    dx = r * (dxn - xhat * jnp.mean(dxn * xhat, axis=-1, keepdims=True))
    return dx, dy * xhat


def _rowsum(v):
    return jnp.sum(v, axis=0, keepdims=True)


def _expm1(x):
    u = jnp.exp(x)
    um1 = u - 1.0
    safe = jnp.where(um1 == 0.0, 1.0, jnp.log(u))
    return jnp.where(um1 == 0.0, x, um1 * x / safe)


def _softplus(x):
    e = jnp.exp(-jnp.abs(x))
    u = 1.0 + e
    um1 = u - 1.0
    safe = jnp.where(um1 == 0.0, 1.0, um1)
    log1p_e = jnp.where(um1 == 0.0, e, jnp.log(u) * e / safe)
    return jnp.maximum(x, 0.0) + log1p_e


_GELU_K = math.sqrt(2.0 / math.pi)
_GELU_C = 0.044715


def _gelu(x):
    return 0.5 * x * (1.0 + jnp.tanh(_GELU_K * (x + _GELU_C * x * x * x)))


def _gelu_grad(x):
    th = jnp.tanh(_GELU_K * (x + _GELU_C * x * x * x))
    return 0.5 * (1.0 + th) + 0.5 * x * (1.0 - th * th) * _GELU_K * (1.0 + 3.0 * _GELU_C * x * x)


def _params(*sem):
    return pltpu.CompilerParams(dimension_semantics=sem, vmem_limit_bytes=VMEM_LIMIT)


def _rows(tm, n):
    return pl.BlockSpec((tm, n), lambda i: (i, 0))


def _rows_rev(tm, n, steps):
    return pl.BlockSpec((tm, n), lambda i: (steps - 1 - i, 0))


def _whole(shape):
    nd = len(shape)
    return pl.BlockSpec(shape, lambda i: (0,) * nd, pipeline_mode=pl.Buffered(1))


def _acc(shape):
    nd = len(shape)
    return pl.BlockSpec(shape, lambda i: (0,) * nd)


def _zero_on_first(*refs):
    @pl.when(pl.program_id(0) == 0)
    def _():
        for r in refs:
            r[...] = jnp.zeros_like(r)


def _inproj_fwd(x, g_mix, w_in_t, b_in, widths, comm=None):
    t, d = x.shape
    n = w_in_t.shape[0]
    tm = min(MATMUL_TILE, t)
    offs = [sum(widths[:i]) for i in range(len(widths) + 1)]

    def body(x_ref, g_ref, w_ref, b_ref, *outs):
        xhat, _ = _rms_stats(x_ref[...])
        h = (xhat * g_ref[...]).astype(BF16)
        for k, o_ref in enumerate(outs):
            lo, hi = offs[k], offs[k + 1]
            o_ref[...] = _dot_nt(h, w_ref[lo:hi, :]) + b_ref[:, lo:hi]

    return _run(
        body, comm, name="inproj_fwd", grid=(t // tm,),
        out_shape=[jax.ShapeDtypeStruct((t, w), F32) for w in widths],
        in_specs=[_rows(tm, d), _whole((1, d)), _whole((n, d)), _whole((1, n))],
        out_specs=[_rows(tm, w) for w in widths],
        semantics="parallel",
    )(x, g_mix, w_in_t, b_in)


def _s5_fwd(u, bbr_blk, bbi_blk, ar, ai, cre_blk, cimn_blk, d_skip, w_glu, b_glu, comm=None):
    t, sa = u.shape
    ns, _, sw = bbr_blk.shape
    gn = ns * sw
    tc = min(TIME_CHUNK, t)

    def body(u_ref, bbr_ref, bbi_ref, ar_ref, ai_ref, cre_ref, cim_ref, d_ref, wg_ref, bg_ref,
             sr_ref, si_ref, y_ref, ya_ref, cr_s, ci_s, sr_s, si_s):
        _zero_on_first(cr_s, ci_s)
        uv = u_ref[...]
        ub = uv.astype(BF16)
        for m in range(ns):
            um = ub[:, m * S5_SLAB:(m + 1) * S5_SLAB]
            sr_s[:, m * sw:(m + 1) * sw] = _dot(um, bbr_ref[m])
            si_s[:, m * sw:(m + 1) * sw] = _dot(um, bbi_ref[m])
        a_r = ar_ref[...]
        a_i = ai_ref[...]

        def step(row, carry):
            c_r, c_i = carry
            at = pl.ds(row, 1)
            n_r = a_r * c_r - a_i * c_i + sr_s[at, :]
            n_i = a_r * c_i + a_i * c_r + si_s[at, :]
            sr_s[at, :] = n_r
            si_s[at, :] = n_i
            return n_r, n_i

        c_r, c_i = lax.fori_loop(0, tc, step, (cr_s[0:1, :], ci_s[0:1, :]), unroll=8)
        cr_s[0:1, :] = c_r
        ci_s[0:1, :] = c_i
        for m in range(ns):
            states, chans = slice(m * sw, (m + 1) * sw), slice(m * S5_SLAB, (m + 1) * S5_SLAB)
            s_r, s_i = sr_s[:, states].astype(BF16), si_s[:, states].astype(BF16)
            sr_ref[:, states] = s_r.astype(sr_ref.dtype)
            si_ref[:, states] = s_i.astype(si_ref.dtype)
            y_ref[:, chans] = _dot(s_r, cre_ref[m]) + _dot(s_i, cim_ref[m]) + d_ref[:, chans] * uv[:, chans]
        y = y_ref[...]
        zz = _gelu(y)
        q = _dot(zz, wg_ref[...]) + bg_ref[...]
        ya_ref[...] = zz * _sigmoid(q)

    return _run(
        body, comm, name="s5_fwd", grid=(t // tc,),
        out_shape=[jax.ShapeDtypeStruct((t, gn), BF16), jax.ShapeDtypeStruct((t, gn), BF16),
                   jax.ShapeDtypeStruct((t, sa), F32), jax.ShapeDtypeStruct((t, sa), F32)],
        in_specs=[_rows(tc, sa), _whole(bbr_blk.shape), _whole(bbi_blk.shape), _whole((1, gn)), _whole((1, gn)),
                  _whole(cre_blk.shape), _whole(cimn_blk.shape), _whole((1, sa)), _whole((sa, sa)), _whole((1, sa))],
        out_specs=[_rows(tc, gn), _rows(tc, gn), _rows(tc, sa), _rows(tc, sa)],
        scratch_shapes=[pltpu.VMEM((SUBLANES, gn), F32), pltpu.VMEM((SUBLANES, gn), F32),
                        pltpu.VMEM((tc, gn), F32), pltpu.VMEM((tc, gn), F32)],
        semantics="arbitrary",
    )(u, bbr_blk, bbi_blk, ar, ai, cre_blk, cimn_blk, d_skip, w_glu, b_glu)


def _slab_dot(x, w_ref, transposed=False):
    dot = _dot_nt if transposed else _dot
    xb = x.astype(BF16)
    return jnp.concatenate([dot(xb[:, j * LRU_SLAB:(j + 1) * LRU_SLAB], w_ref[j]) for j in range(w_ref.shape[0])], axis=1)


def _lru_gates(xc, wr_ref, br_ref, wi_ref, bi_ref, lam_ref):
    r = _sigmoid(_slab_dot(xc, wr_ref) + br_ref[...])
    ig = _sigmoid(_slab_dot(xc, wi_ref) + bi_ref[...])
    sp = _softplus(-lam_ref[...])
    log_a = (-LRU_C * r) * sp
    return r, ig, sp, log_a


def _lru_fwd(u, conv_w, conv_b, wr_blk, b_r, wi_blk, b_i, lru_lambda, comm=None):
    t, w = u.shape
    tc = min(TIME_CHUNK, t)
    halo = SUBLANES

    def body(u_ref, cw_ref, cb_ref, wr_ref, br_ref, wi_ref, bi_ref, lam_ref,
             xc_ref, h_ref, hp_ref, ext_s, a_s, carry_s):
        @pl.when(pl.program_id(0) == 0)
        def _():
            ext_s[0:halo, :] = jnp.zeros((halo, w), F32)
            carry_s[...] = jnp.zeros_like(carry_s)

        ext_s[halo:halo + tc, :] = u_ref[...]
        xc = cb_ref[...]
        for k in range(CONV_WIDTH):
            off = halo - (CONV_WIDTH - 1) + k
            xc = xc + cw_ref[k:k + 1, :] * ext_s[off:off + tc, :]
        ext_s[0:halo, :] = ext_s[tc:tc + halo, :]
        xc_ref[...] = xc
        r, ig, sp, log_a = _lru_gates(xc, wr_ref, br_ref, wi_ref, bi_ref, lam_ref)
        a_s[...] = jnp.exp(log_a)
        h_ref[...] = jnp.sqrt(-_expm1(2.0 * log_a)) * ig * xc

        def step(row, carry):
            at = pl.ds(row, 1)
            hp_ref[at, :] = carry
            nxt = a_s[at, :] * carry + h_ref[at, :]
            h_ref[at, :] = nxt
            return nxt

        carry_s[0:1, :] = lax.fori_loop(0, tc, step, carry_s[0:1, :], unroll=8)

    return _run(
        body, comm, name="lru_fwd", grid=(t // tc,),
        out_shape=[jax.ShapeDtypeStruct((t, w), F32)] * 3,
        in_specs=[_rows(tc, w), _whole((CONV_WIDTH, w)), _whole((1, w)), _whole(wr_blk.shape), _whole((1, w)),
                  _whole(wi_blk.shape), _whole((1, w)), _whole((1, w))],
        out_specs=[_rows(tc, w)] * 3,
        scratch_shapes=[pltpu.VMEM((halo + tc, w), F32), pltpu.VMEM((tc, w), F32), pltpu.VMEM((SUBLANES, w), F32)],
        semantics="arbitrary",
    )(u, conv_w, conv_b, wr_blk, b_r, wi_blk, b_i, lru_lambda)


def _merge_ffn_up_fwd(y_a, h, za, zb, x, w_a_out_t, w_b_out, w_o, g_ffn, w_gate_t, w_up_t, comm=None):
    t, d = x.shape
    sa, lw = y_a.shape[1], h.shape[1]
    f = w_gate_t.shape[0]
    tm = min(TOKEN_TILE, t)

    def body(ya_ref, h_ref, za_ref, zb_ref, x_ref, wa_ref, wb_ref, wo_ref, g_ref, wg_ref, wu_ref,
             x1_ref, mg_ref, ma_ref, mb_ref, fg_ref, fu_ref):
        ma = _dot_nt(ya_ref[...], wa_ref[...])
        mb = _dot(h_ref[...], wb_ref[...])
        merged = _sigmoid(za_ref[...]) * ma + _sigmoid(zb_ref[...]) * mb
        ma_ref[...] = ma.astype(ma_ref.dtype)
        mb_ref[...] = mb.astype(mb_ref.dtype)
        mg_ref[...] = merged.astype(mg_ref.dtype)
        x1 = x_ref[...] + _dot(merged, wo_ref[...])
        x1_ref[...] = x1
        xhat, _ = _rms_stats(x1)
        h2 = (xhat * g_ref[...]).astype(BF16)
        fg_ref[...] = _dot_nt(h2, wg_ref[...]).astype(fg_ref.dtype)
        fu_ref[...] = _dot_nt(h2, wu_ref[...]).astype(fu_ref.dtype)

    return _run(
        body, comm, name="merge_ffn_up_fwd", grid=(t // tm,),
        out_shape=[jax.ShapeDtypeStruct((t, d), F32), jax.ShapeDtypeStruct((t, d), BF16),
                   jax.ShapeDtypeStruct((t, d), BF16), jax.ShapeDtypeStruct((t, d), BF16),
                   jax.ShapeDtypeStruct((t, f), BF16), jax.ShapeDtypeStruct((t, f), BF16)],
        in_specs=[_rows(tm, sa), _rows(tm, lw), _rows(tm, d), _rows(tm, d), _rows(tm, d),
                  _whole((d, sa)), _whole((lw, d)), _whole((d, d)), _whole((1, d)), _whole((f, d)), _whole((f, d))],
        out_specs=[_rows(tm, d)] * 4 + [_rows(tm, f)] * 2,
        semantics="parallel",
    )(y_a, h, za, zb, x, w_a_out_t, w_b_out, w_o, g_ffn, w_gate_t, w_up_t)


def _ffn_down_fwd(fg, fu, x1, w_down, comm=None):
    t, d = x1.shape
    f = fg.shape[1]
    tm = min(MATMUL_TILE, t)

    def body(fg_ref, fu_ref, x_ref, wd_ref, x2_ref):
        fgv = fg_ref[...].astype(F32)
        act = fgv * _sigmoid(fgv) * fu_ref[...].astype(F32)
        x2_ref[...] = x_ref[...] + _dot(act, wd_ref[...])

    return _run(
        body, comm, name="ffn_down_fwd", grid=(t // tm,),
        out_shape=jax.ShapeDtypeStruct((t, d), F32),
        in_specs=[_rows(tm, f), _rows(tm, f), _rows(tm, d), _whole((f, d))],
        out_specs=_rows(tm, d),
        semantics="parallel",
    )(fg, fu, x1, w_down)


def _store_on_last(pairs):
    @pl.when(pl.program_id(0) == pl.num_programs(0) - 1)
    def _():
        for acc, out in pairs:
            out[...] = acc[...].astype(out.dtype)


def _tail_fwd_bwd(x2, p, target, g_pg, w_pg, b_pg, w_ple_t, g_ple, g_final):
    t, d = x2.shape
    pd = p.shape[1]
    tm = min(TOKEN_TILE, t)

    def body(x2_ref, p_ref, tg_ref, gpg_ref, wpg_ref, bpg_ref, wple_ref, gple_ref, gfin_ref,
             dx2_ref, loss_ref, dwpg_out, dwple_out, vec_ref, dwpg_ref, dwple_ref):
        _zero_on_first(loss_ref, dwpg_ref, dwple_ref, vec_ref)
        x2v = x2_ref[...]
        xh2, r2 = _rms_stats(x2v)
        h3 = xh2 * gpg_ref[...]
        gp = _sigmoid(_dot(h3, wpg_ref[...]) + bpg_ref[...])
        pe = _dot_nt(p_ref[...], wple_ref[...])
        peh, r3 = _rms_stats(pe)
        e = peh * gple_ref[...]
        x3 = x2v + gp * e
        xh3, r4 = _rms_stats(x3)
        diff = xh3 * gfin_ref[...] - tg_ref[...]
        loss_ref[...] += 0.5 * jnp.sum(jnp.mean(diff * diff, axis=-1, keepdims=True))
        dy = diff * (1.0 / d)
        dx3, dgfin = _rms_bwd(dy, xh3, r4, gfin_ref[...])
        d_gp = dx3 * e
        d_e = dx3 * gp
        dpe, dgple = _rms_bwd(d_e, peh, r3, gple_ref[...])
        dwple_ref[...] += _dot_tn(dpe, p_ref[...])
        dpre = d_gp * gp * (1.0 - gp)
        dwpg_ref[...] += _dot_tn(h3, dpre)
        dh3 = _dot_nt(dpre, wpg_ref[...])
        dx2n, dgpg = _rms_bwd(dh3, xh2, r2, gpg_ref[...])
        dx2_ref[...] = dx3 + dx2n
        vec_ref[0:1, :] += _rowsum(dpre)
        vec_ref[1:2, :] += _rowsum(dgpg)
        vec_ref[2:3, :] += _rowsum(dgple)
        vec_ref[3:4, :] += _rowsum(dgfin)
        _store_on_last([(dwpg_ref, dwpg_out), (dwple_ref, dwple_out)])

    return pl.pallas_call(
        body, name="tail_fwd_bwd", grid=(t // tm,),
        out_shape=[jax.ShapeDtypeStruct((t, d), F32), jax.ShapeDtypeStruct((SUBLANES, LANES), F32),
                   jax.ShapeDtypeStruct((d, d), BF16), jax.ShapeDtypeStruct((d, pd), BF16),
                   jax.ShapeDtypeStruct((SUBLANES, d), F32)],
        in_specs=[_rows(tm, d), _rows(tm, pd), _rows(tm, d), _whole((1, d)), _whole((d, d)), _whole((1, d)),
                  _whole((d, pd)), _whole((1, d)), _whole((1, d))],
        out_specs=[_rows(tm, d), _acc((SUBLANES, LANES)), _acc((d, d)), _acc((d, pd)), _acc((SUBLANES, d))],
        scratch_shapes=[pltpu.VMEM((d, d), F32), pltpu.VMEM((d, pd), F32)],
        compiler_params=_params("arbitrary"),
    )(x2, p, target, g_pg, w_pg, b_pg, w_ple_t, g_ple, g_final)


def _ffn_bwd_a(dx2, fg, fu, w_down, comm=None):
    t, d = dx2.shape
    f = fg.shape[1]
    tm = min(MATMUL_TILE, t)

    def body(dx_ref, fg_ref, fu_ref, wd_ref, dfg_ref, dfu_ref, act_ref):
        dact = _dot_nt(dx_ref[...], wd_ref[...])
        fgv = fg_ref[...].astype(F32)
        fuv = fu_ref[...].astype(F32)
        sg = _sigmoid(fgv)
        silu = fgv * sg
        dfu_ref[...] = (dact * silu).astype(dfu_ref.dtype)
        dfg_ref[...] = (dact * fuv * (sg * (1.0 + fgv * (1.0 - sg)))).astype(dfg_ref.dtype)
        act_ref[...] = (silu * fuv).astype(act_ref.dtype)

    return _run(
        body, comm, name="ffn_bwd_a", grid=(t // tm,),
        out_shape=[jax.ShapeDtypeStruct((t, f), BF16)] * 3,
        in_specs=[_rows(tm, d), _rows(tm, f), _rows(tm, f), _whole((f, d))],
        out_specs=[_rows(tm, f)] * 3,
        semantics="parallel",
    )(dx2, fg, fu, w_down)


def _ffn_bwd_b(dfg, dfu, x1, dx2, g_ffn, w_gate_t, w_up_t, comm=None):
    t, d = x1.shape
    f = dfg.shape[1]
    tm = min(MATMUL_TILE, t)

    def body(dfg_ref, dfu_ref, x_ref, dx2_ref, g_ref, wg_ref, wu_ref, dx1_ref, h2_ref, vec_ref):
        _zero_on_first(vec_ref)
        dh2 = _dot(dfg_ref[...], wg_ref[...]) + _dot(dfu_ref[...], wu_ref[...])
        xhat, r = _rms_stats(x_ref[...])
        h2_ref[...] = (xhat * g_ref[...]).astype(h2_ref.dtype)
        dxn, dg = _rms_bwd(dh2, xhat, r, g_ref[...])
        dx1_ref[...] = dx2_ref[...] + dxn
        vec_ref[0:1, :] += _rowsum(dg)

    return _run(
        body, comm, name="ffn_bwd_b", grid=(t // tm,),
        out_shape=[jax.ShapeDtypeStruct((t, d), F32), jax.ShapeDtypeStruct((t, d), BF16),
                   jax.ShapeDtypeStruct((SUBLANES, d), F32)],
        in_specs=[_rows(tm, f), _rows(tm, f), _rows(tm, d), _rows(tm, d), _whole((1, d)), _whole((f, d)), _whole((f, d))],
        out_specs=[_rows(tm, d), _rows(tm, d), _acc((SUBLANES, d))],
        semantics="arbitrary",
    )(dfg, dfu, x1, dx2, g_ffn, w_gate_t, w_up_t)


def _matmul_tn(a, b, tn, name, dtype=F32, comm=None):
    t, k = a.shape
    n = b.shape[1]

    def body(a_ref, b_ref, o_ref):
        o_ref[...] = _dot_tn(a_ref[...], b_ref[...]).astype(o_ref.dtype)

    return _run(
        body, comm, name=name, grid=(n // tn,),
        out_shape=jax.ShapeDtypeStruct((k, n), dtype),
        in_specs=[_whole((t, k)), pl.BlockSpec((t, tn), lambda j: (0, j))],
        out_specs=pl.BlockSpec((k, tn), lambda j: (0, j)),
        semantics="parallel",
    )(a, b)


def _merge_bwd(dx1, merged, ma, mb, za, zb, y_a, h, w_o, w_a_out_t, w_b_out, comm=None):
    t, d = dx1.shape
    sa, lw = y_a.shape[1], h.shape[1]
    tm = min(TOKEN_TILE, t)

    def body(dx1_ref, mg_ref, ma_ref, mb_ref, za_ref, zb_ref, ya_ref, h_ref, wo_ref, wa_ref, wb_ref,
             dza_ref, dzb_ref, dya_ref, dyb_ref, dwo_out, dwa_out, dwb_out, dwo_ref, dwa_ref, dwb_ref):
        _zero_on_first(dwo_ref, dwa_ref, dwb_ref)
        dx1v = dx1_ref[...].astype(BF16)
        dmg = _dot_nt(dx1v, wo_ref[...])
        ga = _sigmoid(za_ref[...])
        gb = _sigmoid(zb_ref[...])
        dza_ref[...] = (dmg * ma_ref[...].astype(F32) * ga * (1.0 - ga)).astype(dza_ref.dtype)
        dzb_ref[...] = (dmg * mb_ref[...].astype(F32) * gb * (1.0 - gb)).astype(dzb_ref.dtype)
        dma = (dmg * ga).astype(BF16)
        dmb = (dmg * gb).astype(BF16)
        dya_ref[...] = _dot(dma, wa_ref[...])
        dyb_ref[...] = _dot_nt(dmb, wb_ref[...])
        dwo_ref[...] += _dot_tn(mg_ref[...], dx1v)
        dwa_ref[...] += _dot_tn(dma, ya_ref[...])
        dwb_ref[...] += _dot_tn(h_ref[...], dmb)
        _store_on_last([(dwo_ref, dwo_out), (dwa_ref, dwa_out), (dwb_ref, dwb_out)])

    return _run(
        body, comm, name="merge_bwd", grid=(t // tm,),
        out_shape=[jax.ShapeDtypeStruct((t, d), BF16), jax.ShapeDtypeStruct((t, d), BF16),
                   jax.ShapeDtypeStruct((t, sa), F32), jax.ShapeDtypeStruct((t, lw), F32),
                   jax.ShapeDtypeStruct((d, d), BF16), jax.ShapeDtypeStruct((d, sa), BF16),
                   jax.ShapeDtypeStruct((lw, d), BF16)],
        in_specs=[_rows(tm, d), _rows(tm, d), _rows(tm, d), _rows(tm, d), _rows(tm, d), _rows(tm, d),
                  _rows(tm, sa), _rows(tm, lw), _whole((d, d)), _whole((d, sa)), _whole((lw, d))],
        out_specs=[_rows(tm, d), _rows(tm, d), _rows(tm, sa), _rows(tm, lw), _acc((d, d)), _acc((d, sa)), _acc((lw, d))],
        scratch_shapes=[pltpu.VMEM((d, d), F32), pltpu.VMEM((d, sa), F32), pltpu.VMEM((lw, d), F32)],
        semantics="arbitrary",
    )(dx1, merged, ma, mb, za, zb, y_a, h, w_o, w_a_out_t, w_b_out)


def _fold_diag_blocks(dense, row_group, col_group, row0=0, col0=0):
    r, c = dense.shape
    rows = lax.broadcasted_iota(jnp.int32, (r, c), 0) + row0
    cols = lax.broadcasted_iota(jnp.int32, (r, c), 1) + col0
    kept = jnp.where(rows // row_group == cols // col_group, dense, 0.0)
    pick = (lax.broadcasted_iota(jnp.int32, (row_group, r), 0)
            == lax.broadcasted_iota(jnp.int32, (row_group, r), 1) % row_group).astype(F32)
    return jnp.dot(pick, kept, preferred_element_type=F32, precision=lax.Precision.HIGHEST)


def _lru_bwd(dh, xc, hprev, u, conv_w, wr_blk, b_r, wi_blk, b_i, lru_lambda, head_dim, comm=None):
    t, w = dh.shape
    tc = min(TIME_CHUNK, t)
    steps = t // tc
    halo = SUBLANES
    sub_per_chunk = tc // halo
    slabs = w // LRU_SLAB

    def body(dh_ref, xc_ref, hp_ref, u_ref, uh_ref, cw_ref, wr_ref, br_ref, wi_ref, bi_ref, lam_ref,
             du_ref, dwr_out, dwi_out, vec_ref, lam_s, a_s, dxc_s, uext_s, carry_s, dwr_ref, dwi_ref):
        chunk = steps - 1 - pl.program_id(0)

        @pl.when(pl.program_id(0) == 0)
        def _():
            carry_s[...] = jnp.zeros_like(carry_s)
            dxc_s[tc:tc + halo, :] = jnp.zeros((halo, w), F32)
            dwr_ref[...] = jnp.zeros_like(dwr_ref)
            dwi_ref[...] = jnp.zeros_like(dwi_ref)
            vec_ref[...] = jnp.zeros_like(vec_ref)

        xc = xc_ref[...]
        r, ig, sp, log_a = _lru_gates(xc, wr_ref, br_ref, wi_ref, bi_ref, lam_ref)
        a = jnp.exp(log_a)
        a_s[...] = a

        def step(i, q):
            at = pl.ds(tc - 1 - i, 1)
            lam_row = dh_ref[at, :] + q
            lam_s[at, :] = lam_row
            return a_s[at, :] * lam_row

        carry_s[0:1, :] = lax.fori_loop(0, tc, step, carry_s[0:1, :], unroll=8)
        lam = lam_s[...]
        mult = jnp.sqrt(-_expm1(2.0 * log_a))
        d_log_a = lam * hp_ref[...] * a - (lam * ig * xc) * (a * a) / mult
        d_ig = lam * mult * xc
        dpre_r = (d_log_a * (-LRU_C * sp)) * r * (1.0 - r)
        dpre_i = d_ig * ig * (1.0 - ig)
        dxc = lam * mult * ig + _slab_dot(dpre_r, wr_ref, transposed=True) + _slab_dot(dpre_i, wi_ref, transposed=True)
        xcb, drb, dib = xc.astype(BF16), dpre_r.astype(BF16), dpre_i.astype(BF16)
        for j in range(slabs):
            cols = slice(j * LRU_SLAB, (j + 1) * LRU_SLAB)
            dwr_ref[j] += _dot_tn(drb[:, cols], xcb[:, cols])
            dwi_ref[j] += _dot_tn(dib[:, cols], xcb[:, cols])
        vec_ref[0:1, :] += _rowsum(dxc)
        vec_ref[1:2, :] += _rowsum(dpre_r)
        vec_ref[2:3, :] += _rowsum(dpre_i)
        vec_ref[3:4, :] += _rowsum(d_log_a * (-LRU_C * r)) * (-_sigmoid(-lam_ref[...]))
        dxc_s[0:tc, :] = dxc
        du = cw_ref[CONV_WIDTH - 1:CONV_WIDTH, :] * dxc
        for k in range(CONV_WIDTH - 1):
            off = CONV_WIDTH - 1 - k
            du = du + cw_ref[k:k + 1, :] * dxc_s[off:off + tc, :]
        du_ref[...] = du.astype(du_ref.dtype)
        dxc_s[tc:tc + halo, :] = dxc_s[0:halo, :]
        uext_s[0:halo, :] = jnp.where(chunk > 0, uh_ref[...], 0.0)
        uext_s[halo:halo + tc, :] = u_ref[...]
        for k in range(CONV_WIDTH):
            off = halo - (CONV_WIDTH - 1) + k
            vec_ref[4 + k:5 + k, :] += _rowsum(dxc * uext_s[off:off + tc, :])

        @pl.when(pl.program_id(0) == steps - 1)
        def _():
            for j in range(slabs):
                cols = slice(j * LRU_SLAB, (j + 1) * LRU_SLAB)
                dwr_out[:, cols] = _fold_diag_blocks(dwr_ref[j], head_dim, head_dim).astype(dwr_out.dtype)
                dwi_out[:, cols] = _fold_diag_blocks(dwi_ref[j], head_dim, head_dim).astype(dwi_out.dtype)

    halo_spec = pl.BlockSpec((halo, w), lambda i: (jnp.maximum((steps - 1 - i) * sub_per_chunk - 1, 0), 0))
    return _run(
        body, comm, name="lru_bwd", grid=(steps,),
        out_shape=[jax.ShapeDtypeStruct((t, w), BF16), jax.ShapeDtypeStruct((head_dim, w), BF16),
                   jax.ShapeDtypeStruct((head_dim, w), BF16), jax.ShapeDtypeStruct((SUBLANES, w), F32)],
        in_specs=[_rows_rev(tc, w, steps)] * 4 + [halo_spec, _whole((CONV_WIDTH, w)), _whole(wr_blk.shape),
                                                  _whole((1, w)), _whole(wi_blk.shape), _whole((1, w)), _whole((1, w))],
        out_specs=[_rows_rev(tc, w, steps), _acc((head_dim, w)), _acc((head_dim, w)), _acc((SUBLANES, w))],
        scratch_shapes=[pltpu.VMEM((tc, w), F32), pltpu.VMEM((tc, w), F32), pltpu.VMEM((tc + halo, w), F32),
                        pltpu.VMEM((halo + tc, w), F32), pltpu.VMEM((SUBLANES, w), F32),
                        pltpu.VMEM((slabs, LRU_SLAB, LRU_SLAB), F32), pltpu.VMEM((slabs, LRU_SLAB, LRU_SLAB), F32)],
        semantics="arbitrary",
    )(dh, xc, hprev, u, u, conv_w, wr_blk, b_r, wi_blk, b_i, lru_lambda)


def _s5_bwd(dya, y, sr, si, u, w_glu, b_glu, cre_blk, cimn_blk, bbr_blk, bbi_blk, ar, ai, d_skip, comm=None):
    t, sa = dya.shape
    gn = sr.shape[1]
    ns, _, sw = bbr_blk.shape
    tc = min(TIME_CHUNK, t)
    steps = t // tc
    halo = SUBLANES

    def body(dya_ref, y_ref, sr_ref, si_ref, u_ref, wg_ref, bg_ref, cre_ref, cim_ref, bbr_ref, bbi_ref,
             ar_ref, ai_ref, d_ref, du_ref, lr_ref, li_ref, dy_ref, dwg_out, vsa_ref, vgn_ref, gr_s, gi_s, cr_s, ci_s,
             dwg_ref):
        @pl.when(pl.program_id(0) == 0)
        def _():
            cr_s[...] = jnp.zeros_like(cr_s)
            ci_s[...] = jnp.zeros_like(ci_s)
            gr_s[tc:tc + halo, :] = jnp.zeros((halo, gn), F32)
            gi_s[tc:tc + halo, :] = jnp.zeros((halo, gn), F32)
            dwg_ref[...] = jnp.zeros_like(dwg_ref)
            vsa_ref[...] = jnp.zeros_like(vsa_ref)
            vgn_ref[...] = jnp.zeros_like(vgn_ref)

        yv = y_ref[...]
        uv = u_ref[...]
        zz = _gelu(yv)
        sg = _sigmoid(_dot(zz, wg_ref[...]) + bg_ref[...])
        dyav = dya_ref[...]
        dq = dyav * zz * sg * (1.0 - sg)
        dzz = dyav * sg + _dot_nt(dq, wg_ref[...])
        dwg_ref[...] += _dot_tn(zz, dq)
        dy = dzz * _gelu_grad(yv)
        dyb = dy.astype(BF16)
        dy_ref[...] = dyb.astype(dy_ref.dtype)
        vsa_ref[0:1, :] += _rowsum(dq)
        vsa_ref[1:2, :] += _rowsum(dy * uv)
        for m in range(ns):
            dym = dyb[:, m * S5_SLAB:(m + 1) * S5_SLAB]
            gr_s[0:tc, m * sw:(m + 1) * sw] = _dot_nt(dym, cre_ref[m])
            gi_s[0:tc, m * sw:(m + 1) * sw] = _dot_nt(dym, cim_ref[m])
        a_r = ar_ref[...]
        a_i = ai_ref[...]

        def step(i, carry):
            l_r, l_i = carry
            at = pl.ds(tc - 1 - i, 1)
            n_r = gr_s[at, :] + a_r * l_r + a_i * l_i
            n_i = gi_s[at, :] + a_r * l_i - a_i * l_r
            gr_s[at, :] = n_r
            gi_s[at, :] = n_i
            return n_r, n_i

        l_r, l_i = lax.fori_loop(0, tc, step, (cr_s[0:1, :], ci_s[0:1, :]), unroll=8)
        cr_s[0:1, :] = l_r
        ci_s[0:1, :] = l_i
        nxt_r = gr_s[1:tc + 1, :]
        nxt_i = gi_s[1:tc + 1, :]
        srv = sr_ref[...].astype(F32)
        siv = si_ref[...].astype(F32)
        vgn_ref[0:1, :] += _rowsum(nxt_r * srv + nxt_i * siv)
        vgn_ref[1:2, :] += _rowsum(nxt_i * srv - nxt_r * siv)
        lam_r = gr_s[0:tc, :]
        lam_i = gi_s[0:tc, :]
        gr_s[tc:tc + halo, :] = gr_s[0:halo, :]
        gi_s[tc:tc + halo, :] = gi_s[0:halo, :]
        lrb = lam_r.astype(BF16)
        lib = lam_i.astype(BF16)
        lr_ref[...] = lrb.astype(lr_ref.dtype)
        li_ref[...] = lib.astype(li_ref.dtype)
        for m in range(ns):
            states, chans = slice(m * sw, (m + 1) * sw), slice(m * S5_SLAB, (m + 1) * S5_SLAB)
            du_ref[:, chans] = (_dot_nt(lrb[:, states], bbr_ref[m]) + _dot_nt(lib[:, states], bbi_ref[m])
                                + dy[:, chans] * d_ref[:, chans]).astype(du_ref.dtype)
        _store_on_last([(dwg_ref, dwg_out)])

    return _run(
        body, comm, name="s5_bwd", grid=(steps,),
        out_shape=[jax.ShapeDtypeStruct((t, sa), BF16), jax.ShapeDtypeStruct((t, gn), BF16),
                   jax.ShapeDtypeStruct((t, gn), BF16), jax.ShapeDtypeStruct((t, sa), BF16),
                   jax.ShapeDtypeStruct((sa, sa), BF16), jax.ShapeDtypeStruct((SUBLANES, sa), F32),
                   jax.ShapeDtypeStruct((SUBLANES, gn), F32)],
        in_specs=[_rows_rev(tc, sa, steps), _rows_rev(tc, sa, steps), _rows_rev(tc, gn, steps), _rows_rev(tc, gn, steps),
                  _rows_rev(tc, sa, steps), _whole((sa, sa)), _whole((1, sa)), _whole(cre_blk.shape),
                  _whole(cimn_blk.shape), _whole(bbr_blk.shape), _whole(bbi_blk.shape), _whole((1, gn)), _whole((1, gn)),
                  _whole((1, sa))],
        out_specs=[_rows_rev(tc, sa, steps), _rows_rev(tc, gn, steps), _rows_rev(tc, gn, steps), _rows_rev(tc, sa, steps),
                   _acc((sa, sa)), _acc((SUBLANES, sa)), _acc((SUBLANES, gn))],
        scratch_shapes=[pltpu.VMEM((tc + halo, gn), F32), pltpu.VMEM((tc + halo, gn), F32),
                        pltpu.VMEM((SUBLANES, gn), F32), pltpu.VMEM((SUBLANES, gn), F32), pltpu.VMEM((sa, sa), F32)],
        semantics="arbitrary",
    )(dya, y, sr, si, u, w_glu, b_glu, cre_blk, cimn_blk, bbr_blk, bbi_blk, ar, ai, d_skip)


def _inproj_bwd(dparts, x, dx1, g_mix, w_in_t, comm=None):
    t, d = x.shape
    n = w_in_t.shape[0]
    widths = [p.shape[1] for p in dparts]
    offs = [sum(widths[:i]) for i in range(len(widths) + 1)]
    tm = min(MATMUL_TILE, t)
    np_ = len(dparts)

    def body(*refs):
        dz_refs = refs[:np_]
        x_ref, dx1_ref, g_ref, w_ref, gx_ref, h_ref, vd_ref, vn_ref = refs[np_:]
        _zero_on_first(vd_ref, vn_ref)
        dh = jnp.zeros((tm, d), F32)
        for k, r in enumerate(dz_refs):
            lo, hi = offs[k], offs[k + 1]
            dzk = r[...]
            dh = dh + _dot(dzk, w_ref[lo:hi, :])
            vn_ref[0:1, lo:hi] += _rowsum(dzk.astype(F32))
        xhat, r0 = _rms_stats(x_ref[...])
        h_ref[...] = (xhat * g_ref[...]).astype(h_ref.dtype)
        dxn, dg = _rms_bwd(dh, xhat, r0, g_ref[...])
        gx_ref[...] = dx1_ref[...] + dxn
        vd_ref[0:1, :] += _rowsum(dg)

    return _run(
        body, comm, name="inproj_bwd", grid=(t // tm,),
        out_shape=[jax.ShapeDtypeStruct((t, d), F32), jax.ShapeDtypeStruct((t, d), BF16),
                   jax.ShapeDtypeStruct((SUBLANES, d), F32), jax.ShapeDtypeStruct((SUBLANES, n), F32)],
        in_specs=[_rows(tm, w) for w in widths] + [_rows(tm, d), _rows(tm, d), _whole((1, d)), _whole((n, d))],
        out_specs=[_rows(tm, d), _rows(tm, d), _acc((SUBLANES, d)), _acc((SUBLANES, n))],
        semantics="arbitrary",
    )(*dparts, x, dx1, g_mix, w_in_t)


def _dw_from_parts(dparts, h, tn, name, comm=None):
    t, d = h.shape
    widths = [p.shape[1] for p in dparts]
    offs = [sum(widths[:i]) for i in range(len(widths) + 1)]
    np_ = len(dparts)

    def body(*refs):
        h_ref, o_ref = refs[np_], refs[np_ + 1]
        hv = h_ref[...]
        for k, r in enumerate(refs[:np_]):
            o_ref[offs[k]:offs[k + 1], :] = _dot_tn(r[...], hv).astype(o_ref.dtype)

    return _run(
        body, comm, name=name, grid=(d // tn,),
        out_shape=jax.ShapeDtypeStruct((offs[-1], d), BF16),
        in_specs=[_whole(p.shape) for p in dparts] + [pl.BlockSpec((t, tn), lambda j: (0, j))],
        out_specs=pl.BlockSpec((offs[-1], tn), lambda j: (0, j)),
        semantics="parallel",
    )(*dparts, h)


def _prep(lr_row, li_row, ldt_row, lr_col, li_col, ldt_col, b_re, b_im, c_re, c_im, w_r, w_i, comm=None):
    gn, pch = b_re.shape
    sa, n = c_re.shape
    w, hd = w_r.shape
    ns, sw, lsl = sa // S5_SLAB, S5_SLAB * n // pch, w // LRU_SLAB

    def spread_cols(vals, row_group, col_group, width):
        r, k = vals.shape
        tile = (lax.broadcasted_iota(jnp.int32, (k, width), 0) == lax.broadcasted_iota(jnp.int32, (k, width), 1) % k)
        rows = lax.broadcasted_iota(jnp.int32, (r, width), 0) // row_group
        cols = lax.broadcasted_iota(jnp.int32, (r, width), 1) // col_group
        return jnp.where(rows == cols, _dot(vals, tile.astype(BF16)), 0.0)

    def spread_rows(vals, row_group, col_group, height):
        k, c = vals.shape
        tile = (lax.broadcasted_iota(jnp.int32, (height, k), 0) % k == lax.broadcasted_iota(jnp.int32, (height, k), 1))
        rows = lax.broadcasted_iota(jnp.int32, (height, c), 0) // row_group
        cols = lax.broadcasted_iota(jnp.int32, (height, c), 1) // col_group
        return jnp.where(rows == cols, _dot(tile.astype(BF16), vals), 0.0)

    def body(lrr, lir, ldr, lrc, lic, ldc, bre, bim, cre, cim, wr, wi,
             ar_o, ai_o, bbr_o, bbi_o, cre_o, cim_o, wr_o, wi_o):
        ar, ai, _, _ = _disc_scalars(lrr[...], lir[...], ldr[...])
        ar_o[...] = ar
        ai_o[...] = ai
        _, _, bbr, bbi = _disc_cols(lrc[...], lic[...], ldc[...], bre[...], bim[...])
        bbr_t, bbi_t = bbr.T, bbi.T
        for m in range(ns):
            bbr_o[m] = spread_rows(bbr_t[:, m * sw:(m + 1) * sw], pch, n, S5_SLAB).astype(bbr_o.dtype)
            bbi_o[m] = spread_rows(bbi_t[:, m * sw:(m + 1) * sw], pch, n, S5_SLAB).astype(bbi_o.dtype)
            rows = slice(m * S5_SLAB, (m + 1) * S5_SLAB)
            cre_o[m] = spread_rows(cre[rows, :].T, n, pch, sw).astype(cre_o.dtype)
            cim_o[m] = spread_rows(-cim[rows, :].T, n, pch, sw).astype(cim_o.dtype)
        for j in range(lsl):
            rows = slice(j * LRU_SLAB, (j + 1) * LRU_SLAB)
            wr_o[j] = spread_cols(wr[rows, :], hd, hd, LRU_SLAB).astype(wr_o.dtype)
            wi_o[j] = spread_cols(wi[rows, :], hd, hd, LRU_SLAB).astype(wi_o.dtype)

    args = (lr_row, li_row, ldt_row, lr_col, li_col, ldt_col, b_re, b_im, c_re, c_im, w_r, w_i)
    out_shape = [jax.ShapeDtypeStruct((1, gn), F32), jax.ShapeDtypeStruct((1, gn), F32),
                 jax.ShapeDtypeStruct((ns, S5_SLAB, sw), BF16), jax.ShapeDtypeStruct((ns, S5_SLAB, sw), BF16),
                 jax.ShapeDtypeStruct((ns, sw, S5_SLAB), BF16), jax.ShapeDtypeStruct((ns, sw, S5_SLAB), BF16),
                 jax.ShapeDtypeStruct((lsl, LRU_SLAB, LRU_SLAB), BF16),
                 jax.ShapeDtypeStruct((lsl, LRU_SLAB, LRU_SLAB), BF16)]
    return _run(
        body, comm, name="prep", grid=(1,), out_shape=out_shape, in_specs=[_whole(a.shape) for a in args],
        out_specs=[_acc(s.shape) for s in out_shape], semantics="arbitrary",
    )(*args)


def _s5_param_grads(lam_r, lam_i, sr, si, u, dy, pch, n, comm=None):
    t, gn = lam_r.shape
    sa = u.shape[1]
    sw = S5_SLAB * n // pch

    def body(lr_ref, li_ref, sr_ref, si_ref, u_ref, dy_ref, dbr_ref, dbi_ref, dcr_ref, dci_ref):
        uv = u_ref[...]
        dyv = dy_ref[...]
        dbr_ref[...] = _fold_diag_blocks(_dot_tn(uv, lr_ref[...]), pch, n).astype(dbr_ref.dtype)
        dbi_ref[...] = _fold_diag_blocks(_dot_tn(uv, li_ref[...]), pch, n).astype(dbi_ref.dtype)
        dcr_ref[...] = _fold_diag_blocks(_dot_tn(sr_ref[...], dyv), n, pch).astype(dcr_ref.dtype)
        dci_ref[...] = _fold_diag_blocks(_dot_tn(si_ref[...], dyv), n, pch).astype(dci_ref.dtype)

    states = pl.BlockSpec((t, sw), lambda m: (0, m))
    chans = pl.BlockSpec((t, S5_SLAB), lambda m: (0, m))
    return _run(
        body, comm, name="s5_param_grads", grid=(sa // S5_SLAB,),
        out_shape=[jax.ShapeDtypeStruct((pch, gn), BF16), jax.ShapeDtypeStruct((pch, gn), BF16),
                   jax.ShapeDtypeStruct((n, sa), BF16), jax.ShapeDtypeStruct((n, sa), BF16)],
        in_specs=[states, states, states, states, chans, chans],
        out_specs=[pl.BlockSpec((pch, sw), lambda m: (0, m)), pl.BlockSpec((pch, sw), lambda m: (0, m)),
                   pl.BlockSpec((n, S5_SLAB), lambda m: (0, m)), pl.BlockSpec((n, S5_SLAB), lambda m: (0, m))],
        semantics="parallel",
    )(lam_r, lam_i, sr, si, u, dy)


SMALL_PARTS = ["vec_tail", "vec_ffn", "vec_lru", "vec_mix", "vec_bin", "vec_sa", "vec_gn", "dw_r", "dw_i", "dbb_re",
               "dbb_im", "dc_re", "dc_imn", "loss"]


def _small_reduce(parts, shapes, lr_col, li_col, ldt_col, b_re, b_im, groups, comm=None):
    gn, pch = b_re.shape
    n = gn // groups
    nparts = parts[SMALL_PARTS[0]].size // math.prod(shapes[SMALL_PARTS[0]])
    np_, nout = len(SMALL_PARTS), 24

    def body(*refs):
        ins = refs[:np_]
        lr, li, ldt, bre, bim = refs[np_:np_ + 5]
        outs = refs[np_ + 5:np_ + 5 + nout]
        sums = dict(zip(SMALL_PARTS, refs[np_ + 5 + nout:]))

        @pl.when(pl.program_id(0) == 0)
        def _():
            for k, r in zip(SMALL_PARTS, ins):
                sums[k][...] = r[...].astype(F32)

        @pl.when(pl.program_id(0) > 0)
        def _():
            for k, r in zip(SMALL_PARTS, ins):
                sums[k][...] += r[...].astype(F32)

        @pl.when(pl.program_id(0) == nparts - 1)
        def _():
            finish({k: s[...] for k, s in sums.items()}, lr, li, ldt, bre, bim, *outs)

    def finish(tot, lr, li, ldt, bre, bim, o_loss, o_gmix, o_bin, o_bglu, o_s5d, o_convb, o_br, o_bi, o_lam, o_gffn,
               o_gpg, o_bpg, o_gple, o_gfin, o_wr, o_wi, o_cre, o_cim, o_bre, o_bim, o_lre, o_lim, o_ldt, o_convw):
        o_loss[...] = tot["loss"]
        o_convw[...] = tot["vec_lru"][SUBLANES - CONV_WIDTH:SUBLANES]
        o_bpg[...] = tot["vec_tail"][0:1]
        o_gpg[...] = tot["vec_tail"][1:2]
        o_gple[...] = tot["vec_tail"][2:3]
        o_gfin[...] = tot["vec_tail"][3:4]
        o_gffn[...] = tot["vec_ffn"][0:1]
        o_convb[...] = tot["vec_lru"][0:1]
        o_br[...] = tot["vec_lru"][1:2]
        o_bi[...] = tot["vec_lru"][2:3]
        o_lam[...] = tot["vec_lru"][3:4]
        o_gmix[...] = tot["vec_mix"][0:1]
        o_bin[...] = tot["vec_bin"][0:1]
        o_bglu[...] = tot["vec_sa"][0:1]
        o_s5d[...] = tot["vec_sa"][1:2]
        o_wr[...] = tot["dw_r"].T
        o_wi[...] = tot["dw_i"].T
        o_cre[...] = tot["dc_re"].T
        o_cim[...] = -tot["dc_imn"].T
        d_a = tot["vec_gn"].T
        _, chain = jax.vjp(_disc_cols, lr[...], li[...], ldt[...], bre[...], bim[...])
        d_lr, d_li, d_ldt, d_bre, d_bim = chain((d_a[:, 0:1], d_a[:, 1:2], tot["dbb_re"].T, tot["dbb_im"].T))
        o_lre[...] = d_lr
        o_lim[...] = d_li
        o_bre[...] = d_bre
        o_bim[...] = d_bim
        same = (lax.broadcasted_iota(jnp.int32, (groups, gn), 0)
                == lax.broadcasted_iota(jnp.int32, (groups, gn), 1) // n).astype(F32)
        o_ldt[...] = jnp.dot(same, d_ldt * jnp.ones((1, LANES), F32), preferred_element_type=F32,
                             precision=lax.Precision.HIGHEST)[:, 0:1]

    d = shapes["vec_mix"][1]
    nz = shapes["vec_bin"][1]
    sa = shapes["vec_sa"][1]
    w = shapes["vec_lru"][1]
    row = lambda c: jax.ShapeDtypeStruct((1, c), F32)
    out_shape = [jax.ShapeDtypeStruct(shapes["loss"], F32), row(d), row(nz), row(sa), row(sa), row(w), row(w), row(w),
                 row(w), row(d), row(d), row(d), row(d), row(d),
                 jax.ShapeDtypeStruct(shapes["dw_r"][::-1], F32), jax.ShapeDtypeStruct(shapes["dw_i"][::-1], F32),
                 jax.ShapeDtypeStruct(shapes["dc_re"][::-1], F32), jax.ShapeDtypeStruct(shapes["dc_imn"][::-1], F32),
                 jax.ShapeDtypeStruct((gn, pch), F32), jax.ShapeDtypeStruct((gn, pch), F32),
                 jax.ShapeDtypeStruct((gn, 1), F32), jax.ShapeDtypeStruct((gn, 1), F32),
                 jax.ShapeDtypeStruct((groups, 1), F32), jax.ShapeDtypeStruct((CONV_WIDTH, w), F32)]
    def part_spec(k):
        r, c = shapes[k]
        if parts[k].ndim == 3:
            return pl.BlockSpec((None, r, c), lambda i: (i, 0, 0))
        return pl.BlockSpec((r, c), lambda i: (i, 0))

    outs = _run(
        body, comm, name="small_reduce", grid=(nparts,), out_shape=out_shape,
        in_specs=[part_spec(k) for k in SMALL_PARTS] + [_whole(a.shape) for a in (lr_col, li_col, ldt_col, b_re, b_im)],
        out_specs=[_acc(s.shape) for s in out_shape],
        scratch_shapes=[pltpu.VMEM(shapes[k], F32) for k in SMALL_PARTS],
        semantics="arbitrary",
    )(*[parts[k] for k in SMALL_PARTS], lr_col, li_col, ldt_col, b_re, b_im)
    extra = None
    if comm is not None:
        outs, extra = outs
    names = ["loss", "g_mix", "b_in", "b_glu", "s5_d", "conv_b", "b_r", "b_i", "lru_lambda", "g_ffn", "g_ple_gate",
             "b_ple_gate", "g_ple", "g_final", "w_r", "w_i", "s5_c_re", "s5_c_im", "s5_b_re", "s5_b_im", "lam_re",
             "lam_im", "log_dt", "conv_w"]
    res = dict(zip(names, outs))
    return res if comm is None else (res, extra)


def _adamw_small(ws, gs, ms, vs):
    n = len(ws)

    def body(*refs):
        w_r, g_r, m_r, v_r = (refs[i * n:(i + 1) * n] for i in range(4))
        g_o, d_o, m_o, v_o = (refs[(4 + i) * n:(5 + i) * n] for i in range(4))
        for i in range(n):
            g = g_r[i][...]
            delta, m_new, v_new = _adamw_math(w_r[i][...], g, m_r[i][...], v_r[i][...])
            g_o[i][...] = g
            d_o[i][...] = delta
            m_o[i][...] = m_new
            v_o[i][...] = v_new

    shapes = [jax.ShapeDtypeStruct(a.shape, F32) for a in ws]
    outs = pl.pallas_call(body, name="adamw_small", out_shape=shapes * 4)(*ws, *gs, *ms, *vs)
    return outs[:n], outs[n:2 * n], outs[2 * n:3 * n], outs[3 * n:]


def _adamw_math(w, g, m, v):
    m_new = ADAM_B1 * m + (1.0 - ADAM_B1) * g
    v_new = ADAM_B2 * v + (1.0 - ADAM_B2) * (g * g)
    m_hat = m_new / (1.0 - ADAM_B1 ** ADAM_STEP)
    v_hat = v_new / (1.0 - ADAM_B2 ** ADAM_STEP)
    delta = -ADAM_LR * (m_hat / (jnp.sqrt(v_hat) + ADAM_EPS) + ADAM_WD * w)
    return delta, m_new, v_new


def _row_tile(rows):
    for cand in range(256, 0, -16):
        if rows % cand == 0:
            return cand
    return rows


def _adamw(parts, w, m, v, name, transposed=False):
    nw = len(w)
    rows, cols = w[0].shape
    npart = parts[0].shape[0]
    tr = _row_tile(rows)
    if transposed:
        parts_spec = pl.BlockSpec((npart, cols, tr), lambda i: (0, 0, i))
    else:
        parts_spec = pl.BlockSpec((npart, tr, cols), lambda i: (0, i, 0))

    def body(*refs):
        for i in range(nw):
            p_ref, w_ref, m_ref, v_ref = refs[4 * i:4 * i + 4]
            g_ref, d_ref, mo_ref, vo_ref = refs[4 * (nw + i):4 * (nw + i) + 4]
            g = p_ref[0].astype(F32)
            for k in range(1, npart):
                g = g + p_ref[k].astype(F32)
            if transposed:
                g = g.T
            delta, m_new, v_new = _adamw_math(w_ref[...], g, m_ref[...], v_ref[...])
            g_ref[...] = g
            d_ref[...] = delta
            mo_ref[...] = m_new
            vo_ref[...] = v_new

    res = pl.pallas_call(
        body, name=name, grid=(rows // tr,),
        out_shape=[jax.ShapeDtypeStruct((rows, cols), F32)] * (4 * nw),
        in_specs=([parts_spec] + [_rows(tr, cols)] * 3) * nw,
        out_specs=[_rows(tr, cols)] * (4 * nw),
        compiler_params=_params("parallel"),
    )(*[a for group in zip(parts, w, m, v) for a in group])
    return [res[4 * i:4 * i + 4] for i in range(nw)]


def _mesh_position():
    return lax.axis_index("x"), lax.axis_index("y"), lax.axis_index("c")


def _flip(pos, rel):
    x, y, c = pos
    return (1 - x if rel & 4 else x, 1 - y if rel & 2 else y, 1 - c if rel & 1 else c)


def _index(pos):
    return 4 * pos[0] + 2 * pos[1] + pos[2]


_ANY = pl.BlockSpec(memory_space=pl.ANY)
FLAT_ROWS = 32


def _dma_sems(n):
    return [pltpu.SemaphoreType.DMA((n, N_DEV - 1)), pltpu.SemaphoreType.DMA((n, N_DEV - 1)), pltpu.SemaphoreType.DMA((n,))]


def _block_of(ref, idx, rows, flat):
    if flat:
        return ref.at[pl.ds(pl.multiple_of(idx * rows, FLAT_ROWS), rows), :]
    return ref.at[idx]


class _Gather:
    chips = (4, 2, 6)
    rels = frozenset((1, 4, 2, 6))

    def __init__(self, shards):
        self.inputs = list(shards)
        self.flat = [s.shape[0] % FLAT_ROWS == 0 for s in shards]
        self.out_shape = [
            jax.ShapeDtypeStruct((N_DEV * s.shape[0], s.shape[1]) if f else (N_DEV,) + s.shape, s.dtype)
            for s, f in zip(shards, self.flat)]
        self.sems = _dma_sems(len(shards))

    def _copy(self, ins, outs, sems, i, k, block, to, own=False):
        dst = _block_of(outs[i], _index(block), self.inputs[i].shape[0], self.flat[i])
        return pltpu.make_async_remote_copy(
            src_ref=ins[i] if own else dst, dst_ref=dst, send_sem=sems[0].at[i, k], recv_sem=sems[1].at[i, k],
            device_id=to, device_id_type=MESH)

    def _local(self, ins, outs, sems, i, me):
        dst = _block_of(outs[i], _index(me), self.inputs[i].shape[0], self.flat[i])
        return pltpu.make_async_copy(ins[i], dst, sems[2].at[i])

    def _first(self, ins, outs, sems, i, me):
        cps = [self._copy(ins, outs, sems, i, 0, me, _flip(me, 1), own=True)]
        cps += [self._copy(ins, outs, sems, i, 1 + j, me, _flip(me, rel), own=True) for j, rel in enumerate(self.chips)]
        return cps

    def _passed(self, ins, outs, sems, i, j, me):
        return self._copy(ins, outs, sems, i, 4 + j, _flip(me, self.chips[j]), _flip(me, 1))

    def before(self, ins, outs, sems):
        n = len(self.inputs)
        me = _mesh_position()

        @pl.when(pl.program_id(0) == 0)
        def _():
            for i in range(n):
                self._local(ins, outs, sems, i, me).start()
                for cp in self._first(ins, outs, sems, i, me):
                    cp.start()

        @pl.when(pl.program_id(0) == pl.num_programs(0) - 1)
        def _():
            for j, rel in enumerate(self.chips):
                for i in range(n):
                    self._copy(ins, outs, sems, i, 1 + j, _flip(me, rel), me).wait_recv()
                    self._passed(ins, outs, sems, i, j, me).start()

    def after(self, ins, outs, sems):
        n = len(self.inputs)
        me = _mesh_position()
        sibling = _flip(me, 1)

        @pl.when(pl.program_id(0) == pl.num_programs(0) - 1)
        def _():
            for i in range(n):
                self._copy(ins, outs, sems, i, 0, sibling, me).wait_recv()
                for j, rel in enumerate(self.chips):
                    self._copy(ins, outs, sems, i, 4 + j, _flip(sibling, rel), me).wait_recv()
            for i in range(n):
                for cp in self._first(ins, outs, sems, i, me):
                    cp.wait_send()
                for j in range(len(self.chips)):
                    self._passed(ins, outs, sems, i, j, me).wait_send()
                self._local(ins, outs, sems, i, me).wait()


N_CHIPS = N_DEV // 2


def _chip(pos):
    return 2 * pos[0] + pos[1]


class _PairSwap:
    rels = frozenset((1,))

    def __init__(self, arrays):
        self.inputs = list(arrays)
        self.rows = [a.shape[0] // N_DEV for a in arrays]
        for r in self.rows:
            assert r % FLAT_ROWS == 0, r
        self.out_shape = [jax.ShapeDtypeStruct((N_CHIPS, r, a.shape[1]), a.dtype) for a, r in zip(arrays, self.rows)]
        n = len(arrays)
        self.sems = [pltpu.SemaphoreType.DMA((n, N_CHIPS)), pltpu.SemaphoreType.DMA((n, N_CHIPS))]

    def _copy(self, ins, outs, sems, i, j, me):
        sibling = _flip(me, 1)
        return pltpu.make_async_remote_copy(
            src_ref=_block_of(ins[i], 2 * j + sibling[2], self.rows[i], True), dst_ref=outs[i].at[j],
            send_sem=sems[0].at[i, j], recv_sem=sems[1].at[i, j], device_id=sibling, device_id_type=MESH)

    def before(self, ins, outs, sems):
        me = _mesh_position()

        @pl.when(pl.program_id(0) == 0)
        def _():
            for i in range(len(self.inputs)):
                for j in range(N_CHIPS):
                    self._copy(ins, outs, sems, i, j, me).start()

    def after(self, ins, outs, sems):
        me = _mesh_position()

        @pl.when(pl.program_id(0) == pl.num_programs(0) - 1)
        def _():
            for i in range(len(self.inputs)):
                for j in range(N_CHIPS):
                    self._copy(ins, outs, sems, i, j, me).wait()


class _ChipExchange:
    chips = (4, 2, 6)
    rels = frozenset(chips)

    def __init__(self, arrays):
        self.inputs = list(arrays)
        self.out_shape = [jax.ShapeDtypeStruct(a.shape, a.dtype) for a in arrays]
        n = len(arrays)
        self.sems = [pltpu.SemaphoreType.DMA((n, 3)), pltpu.SemaphoreType.DMA((n, 3)), pltpu.SemaphoreType.DMA((n,))]

    def _send(self, ins, outs, sems, i, k, me):
        peer = _flip(me, self.chips[k])
        return pltpu.make_async_remote_copy(
            src_ref=ins[i].at[_chip(peer)], dst_ref=outs[i].at[_chip(me)], send_sem=sems[0].at[i, k],
            recv_sem=sems[1].at[i, k], device_id=peer, device_id_type=MESH)

    def _arrival(self, ins, outs, sems, i, k, me):
        peer = _flip(me, self.chips[k])
        return pltpu.make_async_remote_copy(
            src_ref=ins[i].at[_chip(me)], dst_ref=outs[i].at[_chip(peer)], send_sem=sems[0].at[i, k],
            recv_sem=sems[1].at[i, k], device_id=peer, device_id_type=MESH)

    def _local(self, ins, outs, sems, i, me):
        return pltpu.make_async_copy(ins[i].at[_chip(me)], outs[i].at[_chip(me)], sems[2].at[i])

    def before(self, ins, outs, sems):
        me = _mesh_position()

        @pl.when(pl.program_id(0) == 0)
        def _():
            for i in range(len(self.inputs)):
                self._local(ins, outs, sems, i, me).start()
                for k in range(len(self.chips)):
                    self._send(ins, outs, sems, i, k, me).start()

    def after(self, ins, outs, sems):
        me = _mesh_position()

        @pl.when(pl.program_id(0) == pl.num_programs(0) - 1)
        def _():
            for i in range(len(self.inputs)):
                for k in range(len(self.chips)):
                    self._arrival(ins, outs, sems, i, k, me).wait_recv()
            for i in range(len(self.inputs)):
                for k in range(len(self.chips)):
                    self._send(ins, outs, sems, i, k, me).wait_send()
                self._local(ins, outs, sems, i, me).wait()


def _pair_add(grads, halves, name):
    n = len(grads)

    def body(*refs):
        g_refs, h_refs, o_refs = refs[:n], refs[n:2 * n], refs[2 * n:]
        c = lax.axis_index("c")
        for i in range(n):
            r = h_refs[i].shape[1]
            for j in range(N_CHIPS):
                own = g_refs[i][pl.ds(pl.multiple_of((2 * j + c) * r, FLAT_ROWS), r), :]
                o_refs[i][j] = (own.astype(F32) + h_refs[i][j].astype(F32)).astype(o_refs[i].dtype)

    return pl.pallas_call(
        body, name=name, out_shape=[jax.ShapeDtypeStruct(h.shape, h.dtype) for h in halves],
        compiler_params=pltpu.CompilerParams(vmem_limit_bytes=VMEM_LIMIT),
    )(*grads, *halves)


class _Both:
    def __init__(self, first, second):
        self.jobs = (first, second)
        self.rels = first.rels | second.rels
        self.inputs = first.inputs + second.inputs
        self.out_shape = first.out_shape + second.out_shape
        self.sems = first.sems + second.sems

    def _each(self, ins, outs, sems):
        a = self.jobs[0]
        i, o, s = len(a.inputs), len(a.out_shape), len(a.sems)
        return ((a, ins[:i], outs[:o], sems[:s]), (self.jobs[1], ins[i:], outs[o:], sems[s:]))

    def before(self, ins, outs, sems):
        for job, i, o, s in self._each(ins, outs, sems):
            job.before(i, o, s)

    def after(self, ins, outs, sems):
        for job, i, o, s in self._each(ins, outs, sems):
            job.after(i, o, s)


_COLLECTIVE_IDS = {(1,): 0, (2, 4, 6): 1, (1, 2, 4, 6): 2}


def _entry_barrier(rels):
    @pl.when(pl.program_id(0) == 0)
    def _():
        me = _mesh_position()
        sem = pltpu.get_barrier_semaphore()
        for rel in rels:
            pl.semaphore_signal(sem, inc=1, device_id=_flip(me, rel), device_id_type=MESH)
        pl.semaphore_wait(sem, len(rels))


def _run(body, comm, *, semantics, out_shape, in_specs, out_specs, scratch_shapes=(), **kw):
    if comm is None:
        return pl.pallas_call(body, out_shape=out_shape, in_specs=in_specs, out_specs=out_specs,
                              scratch_shapes=list(scratch_shapes), compiler_params=_params(semantics), **kw)
    single = not isinstance(out_shape, (list, tuple))
    outs = [out_shape] if single else list(out_shape)
    ospecs = [out_specs] if single else list(out_specs)
    counts = [len(in_specs), len(comm.inputs), len(outs), len(comm.out_shape), len(scratch_shapes), len(comm.sems)]
    rels = tuple(sorted(comm.rels))

    def carrying(*refs):
        groups, pos = [], 0
        for c in counts:
            groups.append(refs[pos:pos + c])
            pos += c
        main_in, comm_in, main_out, comm_out, main_scratch, comm_sems = groups
        _entry_barrier(rels)
        comm.before(comm_in, comm_out, comm_sems)
        body(*main_in, *main_out, *main_scratch)
        comm.after(comm_in, comm_out, comm_sems)

    call = pl.pallas_call(
        carrying, out_shape=outs + list(comm.out_shape), in_specs=list(in_specs) + [_ANY] * len(comm.inputs),
        out_specs=ospecs + [_ANY] * len(comm.out_shape), scratch_shapes=list(scratch_shapes) + list(comm.sems),
        compiler_params=pltpu.CompilerParams(dimension_semantics=("arbitrary",), vmem_limit_bytes=VMEM_LIMIT,
                                             collective_id=_COLLECTIVE_IDS[rels]), **kw)

    def apply(*args):
        res = call(*args, *comm.inputs)
        main = res[:len(outs)]
        return (main[0] if single else list(main)), list(res[len(outs):])

    return apply


def _alone(comm, name):
    return _run(lambda: None, comm, semantics="arbitrary", name=name, grid=(1,), out_shape=[], in_specs=[], out_specs=[])()[1]


SHARDED = {"w_in": 1, "w_glu": 0, "conv_w": 1, "w_a_out": 1, "w_b_out": 0, "w_o": 0, "w_ffn_gate": 1, "w_ffn_up": 1,
           "w_ffn_down": 0, "w_ple_gate": 0, "w_ple": 1}
TRANSPOSED = ("w_in", "w_a_out", "w_ffn_gate", "w_ffn_up", "w_ple")
LONG_AXIS_MINOR = ("w_in", "w_ffn_gate", "w_ffn_up")
NARROW_LAST = ("s5_b_re", "s5_b_im", "s5_d")
ADAMW_GROUPS = (("w_in",), ("w_glu",), ("conv_w",), ("w_a_out",), ("w_b_out", "w_o", "w_ple_gate"),
                ("w_ffn_gate", "w_ffn_up"), ("w_ffn_down",), ("w_ple",))
SMALL = ["g_mix", "b_in", "lam_re", "lam_im", "log_dt", "s5_b_re", "s5_b_im", "s5_c_re", "s5_c_im", "s5_d", "b_glu",
         "conv_b", "w_r", "b_r", "w_i", "b_i", "lru_lambda", "g_ffn", "g_ple_gate", "b_ple_gate", "g_ple", "g_final"]
WEIGHTS = ["g_mix", "w_in", "b_in", "lam_re", "lam_im", "log_dt", "s5_b_re", "s5_b_im", "s5_c_re", "s5_c_im", "s5_d",
           "w_glu", "b_glu", "conv_w", "conv_b", "w_r", "b_r", "w_i", "b_i", "lru_lambda", "w_a_out", "w_b_out", "w_o",
           "g_ffn", "w_ffn_gate", "w_ffn_up", "w_ffn_down", "g_ple_gate", "w_ple_gate", "b_ple_gate", "w_ple", "g_ple",
           "g_final"]


def _unblock(gathered, axis):
    nb, r, c = gathered.shape
    if axis == 0:
        return gathered.reshape(nb * r, c)
    return jnp.transpose(gathered, (1, 0, 2)).reshape(r, nb * c)


def _disc_scalars(lr, li, ldt):
    dt = jnp.exp(ldt)
    mag = jnp.exp(lr * dt)
    ar = mag * jnp.cos(li * dt)
    ai = mag * jnp.sin(li * dt)
    den = lr * lr + li * li
    nr = ar - 1.0
    fr = (nr * lr + ai * li) / den
    fi = (ai * lr - nr * li) / den
    return ar, ai, fr, fi


def _disc_cols(lr, li, ldt, b_re, b_im):
    ar, ai, fr, fi = _disc_scalars(lr, li, ldt)
    return ar, ai, fr * b_re - fi * b_im, fr * b_im + fi * b_re


def _local_step(x, p, target, src, small, disc, distributed=True):
    full = {} if distributed else dict(src)
    gw, halves, pairs, got = {}, {}, {}, {}

    def gather(keys):
        return (_Gather([src[k] for k in keys]), keys, full) if distributed else None

    def swap(keys):
        return (_PairSwap([gw[k] for k in keys]), keys, halves) if distributed else None

    def chips(keys):
        return (_ChipExchange([pairs[k] for k in keys]), keys, got) if distributed else None

    def add_pairs(keys):
        if distributed:
            pairs.update(zip(keys, _pair_add([gw[k] for k in keys], [halves[k] for k in keys], "pair_add_" + keys[0])))

    def carry(fn, *args, jobs=()):
        jobs = [j for j in jobs if j is not None]
        if not jobs:
            return fn(*args)
        comm = jobs[0][0]
        for j in jobs[1:]:
            comm = _Both(comm, j[0])
        res, extra = fn(*args, comm=comm)
        for job, keys, sink in jobs:
            sink.update(zip(keys, extra[:len(job.out_shape)]))
            extra = extra[len(job.out_shape):]
        return res

    d = x.shape[1]
    g, n, pch = small["s5_b_re"].shape
    heads = small["w_r"].shape[0]
    sa, lw = g * pch, small["lru_lambda"].shape[-1]
    widths = [sa, lw, d, d]
    row = lambda v: v.reshape(1, -1)

    hd = small["w_r"].shape[-1]
    ar_row, ai_row, bbr_blk, bbi_blk, cre_blk, cimn_blk, wr_blk, wi_blk = carry(
        _prep, *disc["rows"], *disc["cols"], disc["b_re"], disc["b_im"], small["s5_c_re"].reshape(sa, n),
        small["s5_c_im"].reshape(sa, n), small["w_r"].reshape(lw, hd), small["w_i"].reshape(lw, hd),
        jobs=[gather(["w_in"])])
    d_row = row(small["s5_d"])
    u_a, u_b, za, zb = carry(_inproj_fwd, x, row(small["g_mix"]), full["w_in"], row(small["b_in"]), widths,
                             jobs=[gather(["w_glu", "conv_w", "w_a_out", "w_b_out"])])
    conv_w = _unblock(full["conv_w"], 1) if distributed else full["conv_w"]
    sr, si, y, y_a = carry(_s5_fwd, u_a, bbr_blk, bbi_blk, ar_row, ai_row, cre_blk, cimn_blk, d_row, full["w_glu"],
                           row(small["b_glu"]), jobs=[gather(["w_ffn_gate", "w_o"])])
    xc, h, hprev = carry(_lru_fwd, u_b, conv_w, row(small["conv_b"]), wr_blk, row(small["b_r"]), wi_blk,
                         row(small["b_i"]), row(small["lru_lambda"]), jobs=[gather(["w_ffn_up"])])
    x1, merged, ma, mb, fg, fu = carry(
        _merge_ffn_up_fwd, y_a, h, za, zb, x, full["w_a_out"], full["w_b_out"], full["w_o"], row(small["g_ffn"]),
        full["w_ffn_gate"], full["w_ffn_up"], jobs=[gather(["w_ffn_down", "w_ple_gate", "w_ple"])])
    x2 = _ffn_down_fwd(fg, fu, x1, full["w_ffn_down"])

    dx2, loss_blk, gw["w_ple_gate"], gw["w_ple"], vec_tail = _tail_fwd_bwd(
        x2, p, target, row(small["g_ple_gate"]), full["w_ple_gate"], row(small["b_ple_gate"]), full["w_ple"],
        row(small["g_ple"]), row(small["g_final"]))
    dfg, dfu, act = carry(_ffn_bwd_a, dx2, fg, fu, full["w_ffn_down"], jobs=[swap(["w_ple_gate", "w_ple"])])
    tn_d = min(d, 512)
    gw["w_ffn_down"] = _matmul_tn(act, dx2, tn_d, "dw_ffn_down", BF16)
    add_pairs(["w_ple_gate", "w_ple"])
    dx1, h2, vec_ffn = carry(_ffn_bwd_b, dfg, dfu, x1, dx2, row(small["g_ffn"]), full["w_ffn_gate"], full["w_ffn_up"],
                             jobs=[chips(["w_ple_gate", "w_ple"]), swap(["w_ffn_down"])])
    gw["w_ffn_gate"] = _matmul_tn(dfg, h2, tn_d, "dw_ffn_gate", BF16)
    gw["w_ffn_up"] = _matmul_tn(dfu, h2, tn_d, "dw_ffn_up", BF16)
    add_pairs(["w_ffn_down"])
    dza, dzb, dya, dyb, gw["w_o"], gw["w_a_out"], gw["w_b_out"] = carry(
        _merge_bwd, dx1, merged, ma, mb, za, zb, y_a, h, full["w_o"], full["w_a_out"], full["w_b_out"],
        jobs=[chips(["w_ffn_down"]), swap(["w_ffn_gate", "w_ffn_up"])])
    add_pairs(["w_ffn_gate", "w_ffn_up"])
    du_b, dw_r, dw_i, vec_lru = carry(
        _lru_bwd, dyb, xc, hprev, u_b, conv_w, wr_blk, row(small["b_r"]), wi_blk, row(small["b_i"]),
        row(small["lru_lambda"]), hd, jobs=[chips(["w_ffn_gate", "w_ffn_up"]), swap(["w_o", "w_a_out", "w_b_out"])])
    add_pairs(["w_o", "w_a_out", "w_b_out"])
    du_a, lam_r, lam_i, dy16, gw["w_glu"], vec_sa, vec_gn = carry(
        _s5_bwd, dya, y, sr, si, u_a, full["w_glu"], row(small["b_glu"]), cre_blk, cimn_blk, bbr_blk, bbi_blk, ar_row,
        ai_row, d_row, jobs=[chips(["w_o", "w_a_out", "w_b_out"])])
    smalls = {"vec_tail": vec_tail, "vec_ffn": vec_ffn, "vec_lru": vec_lru, "vec_sa": vec_sa, "vec_gn": vec_gn,
              "dw_r": dw_r, "dw_i": dw_i, "loss": loss_blk}
    everyones = {}

    def gather_smalls(keys):
        return (_Gather([smalls[k] for k in keys]), keys, everyones) if distributed else None

    smalls["dbb_re"], smalls["dbb_im"], smalls["dc_re"], smalls["dc_imn"] = carry(
        _s5_param_grads, lam_r, lam_i, sr, si, u_a, dy16, pch, n, jobs=[swap(["w_glu"]), gather_smalls(list(smalls))])
    add_pairs(["w_glu"])
    dz = [du_a, du_b, dza, dzb]
    grad_x, h0, smalls["vec_mix"], smalls["vec_bin"] = _inproj_bwd(
        dz, x, dx1, row(small["g_mix"]), full["w_in"])
    shapes = {k: a.shape for k, a in smalls.items()}
    gw["w_in"] = carry(_dw_from_parts, dz, h0, tn_d, "dw_in",
                       jobs=[chips(["w_glu"]), gather_smalls(["dbb_re", "dbb_im", "dc_re", "dc_imn"])])
    smalls.update(everyones)
    if distributed:
        got["w_in"] = gw["w_in"]
        gw = got
    return grad_x, gw, smalls, shapes


def _disc_inputs(small):
    g, n, pch = small["s5_b_re"].shape
    srcs = (small["lam_re"], small["lam_im"], jnp.repeat(small["log_dt"], n))
    return {"rows": [a.reshape(1, g * n) for a in srcs], "cols": [a.reshape(g * n, 1) for a in srcs],
            "b_re": small["s5_b_re"].reshape(g * n, pch), "b_im": small["s5_b_im"].reshape(g * n, pch)}


def kernel(x, p, g_mix, w_in, b_in, lam_re, lam_im, log_dt, s5_b_re, s5_b_im, s5_c_re, s5_c_im, s5_d, w_glu, b_glu, conv_w, conv_b, w_r, b_r, w_i, b_i, lru_lambda, w_a_out, w_b_out, w_o, g_ffn, w_ffn_gate, w_ffn_up, w_ffn_down, g_ple_gate, w_ple_gate, b_ple_gate, w_ple, g_ple, g_final, loss_target, m_g_mix, m_w_in, m_b_in, m_lam_re, m_lam_im, m_log_dt, m_s5_b_re, m_s5_b_im, m_s5_c_re, m_s5_c_im, m_s5_d, m_w_glu, m_b_glu, m_conv_w, m_conv_b, m_w_r, m_b_r, m_w_i, m_b_i, m_lru_lambda, m_w_a_out, m_w_b_out, m_w_o, m_g_ffn, m_w_ffn_gate, m_w_ffn_up, m_w_ffn_down, m_g_ple_gate, m_w_ple_gate, m_b_ple_gate, m_w_ple, m_g_ple, m_g_final, v_g_mix, v_w_in, v_b_in, v_lam_re, v_lam_im, v_log_dt, v_s5_b_re, v_s5_b_im, v_s5_c_re, v_s5_c_im, v_s5_d, v_w_glu, v_b_glu, v_conv_w, v_conv_b, v_w_r, v_b_r, v_w_i, v_b_i, v_lru_lambda, v_w_a_out, v_w_b_out, v_w_o, v_g_ffn, v_w_ffn_gate, v_w_ffn_up, v_w_ffn_down, v_g_ple_gate, v_w_ple_gate, v_b_ple_gate, v_w_ple, v_g_ple, v_g_final):
    given = dict(locals())
    wts = {k: given[k] for k in WEIGHTS}
    moms = {k: given["m_" + k] for k in WEIGHTS}
    vels = {k: given["v_" + k] for k in WEIGHTS}

    def drop_depth(k, a):
        return a if k == "g_final" else a[0]

    small = {k: drop_depth(k, wts[k]) for k in SMALL}
    shard = {k: wts[k][0] for k in SHARDED}
    names = list(SHARDED)

    def wire(k):
        if k == "conv_w":
            return shard[k]
        return (shard[k].T if k in TRANSPOSED else shard[k]).astype(BF16)

    disc = _disc_inputs(small)
    grad_x, parts, smalls, shapes = _local_step(x[0], p[0, 0], loss_target[0], {k: wire(k) for k in names}, small, disc)

    late = ["vec_mix", "vec_bin"]
    received = _alone(_Both(_PairSwap([parts["w_in"]]), _Gather([smalls[k] for k in late])), "swap_last")
    (pair_in,) = _pair_add([parts["w_in"]], received[:1], "pair_add_w_in")
    smalls.update(zip(late, received[1:]))

    g_small, (parts["w_in"],) = _small_reduce(smalls, shapes, *disc["cols"], disc["b_re"], disc["b_im"],
                                              small["s5_b_re"].shape[0], comm=_ChipExchange([pair_in]))
    loss = g_small.pop("loss")[0, 0]
    cols = shard["conv_w"].shape[1]
    mine = _index((lax.axis_index("x"), lax.axis_index("y"), lax.axis_index("c")))
    parts["conv_w"] = lax.dynamic_slice_in_dim(g_small.pop("conv_w"), mine * cols, cols, axis=1)[None]

    def view(k, a):
        a = a.reshape((1, -1) if k == "g_final" else wts[k].shape)
        return jnp.swapaxes(a, -1, -2) if k in NARROW_LAST else a

    def unview(k, a):
        return (jnp.swapaxes(a, -1, -2) if k in NARROW_LAST else a).reshape(wts[k].shape)

    slots = _adamw_small([view(k, wts[k]) for k in SMALL], [view(k, g_small[k]) for k in SMALL],
                         [view(k, moms[k]) for k in SMALL], [view(k, vels[k]) for k in SMALL])
    small_out = [dict(zip(SMALL, [unview(k, a) for k, a in zip(SMALL, slot)])) for slot in slots]

    big_out = {}
    for group in ADAMW_GROUPS:
        flip = group[0] in LONG_AXIS_MINOR
        look = (lambda a: a.T) if flip else (lambda a: a)
        res = _adamw([parts[k] for k in group], [look(shard[k]) for k in group], [look(moms[k][0]) for k in group],
                     [look(vels[k][0]) for k in group], "adamw_" + group[0],
                     transposed=group[0] in TRANSPOSED and not flip)
        for k, outs4 in zip(group, res):
            big_out[k] = [look(a) for a in outs4]

    outs = [loss, grad_x[None]]
    for slot in range(4):
        for k in WEIGHTS:
            if k in SHARDED:
                outs.append(big_out[k][slot][None])
            else:
                outs.append(small_out[slot][k])
    return tuple(outs)
```

```python
import math

import jax
import jax.numpy as jnp
from jax import lax
from jax.experimental import pallas as pl
from jax.experimental.pallas import tpu as pltpu

F32 = jnp.float32
BF16 = jnp.bfloat16

EPS = 1e-6
LRU_C = 8.0
CONV_WIDTH = 4
ADAM_LR = 0.001
ADAM_B1 = 0.9
ADAM_B2 = 0.999
ADAM_EPS = 1e-08
ADAM_WD = 0.01
ADAM_STEP = 10

N_DEV = 8
MESH = pl.DeviceIdType.MESH
SUBLANES = 8
LANES = 128
VMEM_LIMIT = 56 * 1024 * 1024
TOKEN_TILE = 256
MATMUL_TILE = 256
TIME_CHUNK = 256
S5_SLAB = 128
LRU_SLAB = 256


def _dot(a, b):
    return jnp.dot(a.astype(BF16), b.astype(BF16), preferred_element_type=F32)


def _dot_nt(a, b):
    return lax.dot_general(a.astype(BF16), b.astype(BF16), (((1,), (1,)), ((), ())), preferred_element_type=F32)


def _dot_tn(a, b):
    return lax.dot_general(a.astype(BF16), b.astype(BF16), (((0,), (0,)), ((), ())), preferred_element_type=F32)


def _sigmoid(x):
    return jax.nn.sigmoid(x)


def _rms_stats(x):
    r = lax.rsqrt(jnp.mean(x * x, axis=-1, keepdims=True) + EPS)
    return x * r, r


def _rms_bwd(dy, xhat, r, g):
    dxn = dy * g
    dx = r * (dxn - xhat * jnp.mean(dxn * xhat, axis=-1, keepdims=True))
    return dx, dy * xhat


def _rowsum(v):
    return jnp.sum(v, axis=0, keepdims=True)


def _expm1(x):
    u = jnp.exp(x)
    um1 = u - 1.0
    safe = jnp.where(um1 == 0.0, 1.0, jnp.log(u))
    return jnp.where(um1 == 0.0, x, um1 * x / safe)


def _softplus(x):
    e = jnp.exp(-jnp.abs(x))
    u = 1.0 + e
    um1 = u - 1.0
    safe = jnp.where(um1 == 0.0, 1.0, um1)
    log1p_e = jnp.where(um1 == 0.0, e, jnp.log(u) * e / safe)
    return jnp.maximum(x, 0.0) + log1p_e


_GELU_K = math.sqrt(2.0 / math.pi)
_GELU_C = 0.044715


def _gelu(x):
    return 0.5 * x * (1.0 + jnp.tanh(_GELU_K * (x + _GELU_C * x * x * x)))


def _gelu_grad(x):
    th = jnp.tanh(_GELU_K * (x + _GELU_C * x * x * x))
    return 0.5 * (1.0 + th) + 0.5 * x * (1.0 - th * th) * _GELU_K * (1.0 + 3.0 * _GELU_C * x * x)


def _params(*sem):
    return pltpu.CompilerParams(dimension_semantics=sem, vmem_limit_bytes=VMEM_LIMIT)


def _rows(tm, n):
    return pl.BlockSpec((tm, n), lambda i: (i, 0))


def _rows_rev(tm, n, steps):
    return pl.BlockSpec((tm, n), lambda i: (steps - 1 - i, 0))


def _whole(shape):
    nd = len(shape)
    return pl.BlockSpec(shape, lambda i: (0,) * nd, pipeline_mode=pl.Buffered(1))


def _acc(shape):
    nd = len(shape)
    return pl.BlockSpec(shape, lambda i: (0,) * nd)


def _zero_on_first(*refs):
    @pl.when(pl.program_id(0) == 0)
    def _():
        for r in refs:
            r[...] = jnp.zeros_like(r)


def _inproj_fwd(x, g_mix, w_in_t, b_in, widths, comm=None):
    t, d = x.shape
    n = w_in_t.shape[0]
    tm = min(MATMUL_TILE, t)
    offs = [sum(widths[:i]) for i in range(len(widths) + 1)]

    def body(x_ref, g_ref, w_ref, b_ref, *outs):
        xhat, _ = _rms_stats(x_ref[...])
        h = (xhat * g_ref[...]).astype(BF16)
        for k, o_ref in enumerate(outs):
            lo, hi = offs[k], offs[k + 1]
            o_ref[...] = _dot_nt(h, w_ref[lo:hi, :]) + b_ref[:, lo:hi]

    return _run(
        body, comm, name="inproj_fwd", grid=(t // tm,),
        out_shape=[jax.ShapeDtypeStruct((t, w), F32) for w in widths],
        in_specs=[_rows(tm, d), _whole((1, d)), _whole((n, d)), _whole((1, n))],
        out_specs=[_rows(tm, w) for w in widths],
        semantics="parallel",
    )(x, g_mix, w_in_t, b_in)


def _s5_fwd(u, bbr_blk, bbi_blk, ar, ai, cre_blk, cimn_blk, d_skip, w_glu, b_glu):
    t, sa = u.shape
    ns, _, sw = bbr_blk.shape
    gn = ns * sw
    tc = min(TIME_CHUNK, t)

    def body(u_ref, bbr_ref, bbi_ref, ar_ref, ai_ref, cre_ref, cim_ref, d_ref, wg_ref, bg_ref,
             sr_ref, si_ref, y_ref, ya_ref, cr_s, ci_s, sr_s, si_s):
        _zero_on_first(cr_s, ci_s)
        uv = u_ref[...]
        ub = uv.astype(BF16)
        for m in range(ns):
            um = ub[:, m * S5_SLAB:(m + 1) * S5_SLAB]
            sr_s[:, m * sw:(m + 1) * sw] = _dot(um, bbr_ref[m])
            si_s[:, m * sw:(m + 1) * sw] = _dot(um, bbi_ref[m])
        a_r = ar_ref[...]
        a_i = ai_ref[...]

        def step(row, carry):
            c_r, c_i = carry
            at = pl.ds(row, 1)
            n_r = a_r * c_r - a_i * c_i + sr_s[at, :]
            n_i = a_r * c_i + a_i * c_r + si_s[at, :]
            sr_s[at, :] = n_r
            si_s[at, :] = n_i
            return n_r, n_i

        c_r, c_i = lax.fori_loop(0, tc, step, (cr_s[0:1, :], ci_s[0:1, :]), unroll=8)
        cr_s[0:1, :] = c_r
        ci_s[0:1, :] = c_i
        for m in range(ns):
            states, chans = slice(m * sw, (m + 1) * sw), slice(m * S5_SLAB, (m + 1) * S5_SLAB)
            s_r, s_i = sr_s[:, states].astype(BF16), si_s[:, states].astype(BF16)
            sr_ref[:, states] = s_r.astype(sr_ref.dtype)
            si_ref[:, states] = s_i.astype(si_ref.dtype)
            y_ref[:, chans] = _dot(s_r, cre_ref[m]) + _dot(s_i, cim_ref[m]) + d_ref[:, chans] * uv[:, chans]
        y = y_ref[...]
        zz = _gelu(y)
        q = _dot(zz, wg_ref[...]) + bg_ref[...]
        ya_ref[...] = zz * _sigmoid(q)

    return dict(
        body=body, steps=t // tc, args=(u, bbr_blk, bbi_blk, ar, ai, cre_blk, cimn_blk, d_skip, w_glu, b_glu),
        out_shape=[jax.ShapeDtypeStruct((t, gn), BF16), jax.ShapeDtypeStruct((t, gn), BF16),
                   jax.ShapeDtypeStruct((t, sa), F32), jax.ShapeDtypeStruct((t, sa), F32)],
        in_specs=[_rows(tc, sa), _whole(bbr_blk.shape), _whole(bbi_blk.shape), _whole((1, gn)), _whole((1, gn)),
                  _whole(cre_blk.shape), _whole(cimn_blk.shape), _whole((1, sa)), _whole((sa, sa)), _whole((1, sa))],
        out_specs=[_rows(tc, gn), _rows(tc, gn), _rows(tc, sa), _rows(tc, sa)],
        scratch=[pltpu.VMEM((SUBLANES, gn), F32), pltpu.VMEM((SUBLANES, gn), F32),
                 pltpu.VMEM((tc, gn), F32), pltpu.VMEM((tc, gn), F32)])


def _stages(stages, name, comm=None):
    counts = [[len(s[k]) for s in stages] for k in ("args", "out_shape", "scratch")]

    def body(*refs):
        groups, pos = [], 0
        for kind in counts:
            per_stage = []
            for c in kind:
                per_stage.append(refs[pos:pos + c])
                pos += c
            groups.append(per_stage)
        for s, ins, outs, scratch in zip(stages, *groups):
            s["body"](*ins, *outs, *scratch)

    res = _run(
        body, comm, name=name, grid=(stages[0]["steps"],),
        out_shape=[o for s in stages for o in s["out_shape"]], in_specs=[i for s in stages for i in s["in_specs"]],
        out_specs=[o for s in stages for o in s["out_specs"]], scratch_shapes=[c for s in stages for c in s["scratch"]],
        semantics="arbitrary",
    )(*[a for s in stages for a in s["args"]])
    extra = None
    if comm is not None:
        res, extra = res
    per_stage, pos = [], 0
    for c in counts[1]:
        per_stage.append(list(res[pos:pos + c]))
        pos += c
    return per_stage if comm is None else (per_stage, extra)


def _slab_dot(x, w_ref, transposed=False):
    dot = _dot_nt if transposed else _dot
    xb = x.astype(BF16)
    return jnp.concatenate([dot(xb[:, j * LRU_SLAB:(j + 1) * LRU_SLAB], w_ref[j]) for j in range(w_ref.shape[0])], axis=1)


def _lru_gates(xc, wr_ref, br_ref, wi_ref, bi_ref, lam_ref):
    r = _sigmoid(_slab_dot(xc, wr_ref) + br_ref[...])
    ig = _sigmoid(_slab_dot(xc, wi_ref) + bi_ref[...])
    sp = _softplus(-lam_ref[...])
    log_a = (-LRU_C * r) * sp
    return r, ig, sp, log_a


def _lru_fwd(u, conv_w, conv_b, wr_blk, b_r, wi_blk, b_i, lru_lambda):
    t, w = u.shape
    tc = min(TIME_CHUNK, t)
    halo = SUBLANES

    def body(u_ref, cw_ref, cb_ref, wr_ref, br_ref, wi_ref, bi_ref, lam_ref,
             xc_ref, h_ref, hp_ref, ext_s, a_s, carry_s):
        @pl.when(pl.program_id(0) == 0)
        def _():
            ext_s[0:halo, :] = jnp.zeros((halo, w), F32)
            carry_s[...] = jnp.zeros_like(carry_s)

        ext_s[halo:halo + tc, :] = u_ref[...]
        xc = cb_ref[...]
        for k in range(CONV_WIDTH):
            off = halo - (CONV_WIDTH - 1) + k
            xc = xc + cw_ref[k:k + 1, :] * ext_s[off:off + tc, :]
        ext_s[0:halo, :] = ext_s[tc:tc + halo, :]
        xc_ref[...] = xc
        r, ig, sp, log_a = _lru_gates(xc, wr_ref, br_ref, wi_ref, bi_ref, lam_ref)
        a_s[...] = jnp.exp(log_a)
        h_ref[...] = jnp.sqrt(-_expm1(2.0 * log_a)) * ig * xc

        def step(row, carry):
            at = pl.ds(row, 1)
            hp_ref[at, :] = carry
            nxt = a_s[at, :] * carry + h_ref[at, :]
            h_ref[at, :] = nxt
            return nxt

        carry_s[0:1, :] = lax.fori_loop(0, tc, step, carry_s[0:1, :], unroll=8)

    return dict(
        body=body, steps=t // tc, args=(u, conv_w, conv_b, wr_blk, b_r, wi_blk, b_i, lru_lambda),
        out_shape=[jax.ShapeDtypeStruct((t, w), F32)] * 3,
        in_specs=[_rows(tc, w), _whole((CONV_WIDTH, w)), _whole((1, w)), _whole(wr_blk.shape), _whole((1, w)),
                  _whole(wi_blk.shape), _whole((1, w)), _whole((1, w))],
        out_specs=[_rows(tc, w)] * 3,
        scratch=[pltpu.VMEM((halo + tc, w), F32), pltpu.VMEM((tc, w), F32), pltpu.VMEM((SUBLANES, w), F32)])


def _merge_ffn_up_fwd(y_a, h, za, zb, x, w_a_out_t, w_b_out, w_o, g_ffn, w_gate_t, w_up_t, comm=None):
    t, d = x.shape
    sa, lw = y_a.shape[1], h.shape[1]
    f = w_gate_t.shape[0]
    tm = min(TOKEN_TILE, t)

    def body(ya_ref, h_ref, za_ref, zb_ref, x_ref, wa_ref, wb_ref, wo_ref, g_ref, wg_ref, wu_ref,
             x1_ref, mg_ref, ma_ref, mb_ref, fg_ref, fu_ref):
        ma = _dot_nt(ya_ref[...], wa_ref[...])
        mb = _dot(h_ref[...], wb_ref[...])
        merged = _sigmoid(za_ref[...]) * ma + _sigmoid(zb_ref[...]) * mb
        ma_ref[...] = ma.astype(ma_ref.dtype)
        mb_ref[...] = mb.astype(mb_ref.dtype)
        mg_ref[...] = merged.astype(mg_ref.dtype)
        x1 = x_ref[...] + _dot(merged, wo_ref[...])
        x1_ref[...] = x1
        xhat, _ = _rms_stats(x1)
        h2 = (xhat * g_ref[...]).astype(BF16)
        fg_ref[...] = _dot_nt(h2, wg_ref[...]).astype(fg_ref.dtype)
        fu_ref[...] = _dot_nt(h2, wu_ref[...]).astype(fu_ref.dtype)

    return _run(
        body, comm, name="merge_ffn_up_fwd", grid=(t // tm,),
        out_shape=[jax.ShapeDtypeStruct((t, d), F32), jax.ShapeDtypeStruct((t, d), BF16),
                   jax.ShapeDtypeStruct((t, d), BF16), jax.ShapeDtypeStruct((t, d), BF16),
                   jax.ShapeDtypeStruct((t, f), BF16), jax.ShapeDtypeStruct((t, f), BF16)],
        in_specs=[_rows(tm, sa), _rows(tm, lw), _rows(tm, d), _rows(tm, d), _rows(tm, d),
                  _whole((d, sa)), _whole((lw, d)), _whole((d, d)), _whole((1, d)), _whole((f, d)), _whole((f, d))],
        out_specs=[_rows(tm, d)] * 4 + [_rows(tm, f)] * 2,
        semantics="parallel",
    )(y_a, h, za, zb, x, w_a_out_t, w_b_out, w_o, g_ffn, w_gate_t, w_up_t)


def _ffn_down_fwd(fg, fu, x1, w_down, comm=None):
    t, d = x1.shape
    f = fg.shape[1]
    tm = min(MATMUL_TILE, t)

    def body(fg_ref, fu_ref, x_ref, wd_ref, x2_ref):
        fgv = fg_ref[...].astype(F32)
        act = fgv * _sigmoid(fgv) * fu_ref[...].astype(F32)
        x2_ref[...] = x_ref[...] + _dot(act, wd_ref[...])

    return _run(
        body, comm, name="ffn_down_fwd", grid=(t // tm,),
        out_shape=jax.ShapeDtypeStruct((t, d), F32),
        in_specs=[_rows(tm, f), _rows(tm, f), _rows(tm, d), _whole((f, d))],
        out_specs=_rows(tm, d),
        semantics="parallel",
    )(fg, fu, x1, w_down)


def _store_on_last(pairs):
    @pl.when(pl.program_id(0) == pl.num_programs(0) - 1)
    def _():
        for acc, out in pairs:
            out[...] = acc[...].astype(out.dtype)


def _tail_fwd_bwd(x2, p, target, g_pg, w_pg, b_pg, w_ple_t, g_ple, g_final):
    t, d = x2.shape
    pd = p.shape[1]
    tm = min(TOKEN_TILE, t)

    def body(x2_ref, p_ref, tg_ref, gpg_ref, wpg_ref, bpg_ref, wple_ref, gple_ref, gfin_ref,
             dx2_ref, loss_ref, dwpg_out, dwple_out, vec_ref, dwpg_ref, dwple_ref):
        _zero_on_first(loss_ref, dwpg_ref, dwple_ref, vec_ref)
        x2v = x2_ref[...]
        xh2, r2 = _rms_stats(x2v)
        h3 = xh2 * gpg_ref[...]
        gp = _sigmoid(_dot(h3, wpg_ref[...]) + bpg_ref[...])
        pe = _dot_nt(p_ref[...], wple_ref[...])
        peh, r3 = _rms_stats(pe)
        e = peh * gple_ref[...]
        x3 = x2v + gp * e
        xh3, r4 = _rms_stats(x3)
        diff = xh3 * gfin_ref[...] - tg_ref[...]
        loss_ref[...] += 0.5 * jnp.sum(jnp.mean(diff * diff, axis=-1, keepdims=True))
        dy = diff * (1.0 / d)
        dx3, dgfin = _rms_bwd(dy, xh3, r4, gfin_ref[...])
        d_gp = dx3 * e
        d_e = dx3 * gp
        dpe, dgple = _rms_bwd(d_e, peh, r3, gple_ref[...])
        dwple_ref[...] += _dot_tn(dpe, p_ref[...])
        dpre = d_gp * gp * (1.0 - gp)
        dwpg_ref[...] += _dot_tn(h3, dpre)
        dh3 = _dot_nt(dpre, wpg_ref[...])
        dx2n, dgpg = _rms_bwd(dh3, xh2, r2, gpg_ref[...])
        dx2_ref[...] = dx3 + dx2n
        vec_ref[0:1, :] += _rowsum(dpre)
        vec_ref[1:2, :] += _rowsum(dgpg)
        vec_ref[2:3, :] += _rowsum(dgple)
        vec_ref[3:4, :] += _rowsum(dgfin)
        _store_on_last([(dwpg_ref, dwpg_out), (dwple_ref, dwple_out)])

    return pl.pallas_call(
        body, name="tail_fwd_bwd", grid=(t // tm,),
        out_shape=[jax.ShapeDtypeStruct((t, d), F32), jax.ShapeDtypeStruct((SUBLANES, LANES), F32),
                   jax.ShapeDtypeStruct((d, d), BF16), jax.ShapeDtypeStruct((d, pd), BF16),
                   jax.ShapeDtypeStruct((SUBLANES, d), F32)],
        in_specs=[_rows(tm, d), _rows(tm, pd), _rows(tm, d), _whole((1, d)), _whole((d, d)), _whole((1, d)),
                  _whole((d, pd)), _whole((1, d)), _whole((1, d))],
        out_specs=[_rows(tm, d), _acc((SUBLANES, LANES)), _acc((d, d)), _acc((d, pd)), _acc((SUBLANES, d))],
        scratch_shapes=[pltpu.VMEM((d, d), F32), pltpu.VMEM((d, pd), F32)],
        compiler_params=_params("arbitrary"),
    )(x2, p, target, g_pg, w_pg, b_pg, w_ple_t, g_ple, g_final)


def _ffn_bwd_a(dx2, fg, fu, w_down, comm=None):
    t, d = dx2.shape
    f = fg.shape[1]
    tm = min(MATMUL_TILE, t)

    def body(dx_ref, fg_ref, fu_ref, wd_ref, dfg_ref, dfu_ref, act_ref):
        dact = _dot_nt(dx_ref[...], wd_ref[...])
        fgv = fg_ref[...].astype(F32)
        fuv = fu_ref[...].astype(F32)
        sg = _sigmoid(fgv)
        silu = fgv * sg
        dfu_ref[...] = (dact * silu).astype(dfu_ref.dtype)
        dfg_ref[...] = (dact * fuv * (sg * (1.0 + fgv * (1.0 - sg)))).astype(dfg_ref.dtype)
        act_ref[...] = (silu * fuv).astype(act_ref.dtype)

    return _run(
        body, comm, name="ffn_bwd_a", grid=(t // tm,),
        out_shape=[jax.ShapeDtypeStruct((t, f), BF16)] * 3,
        in_specs=[_rows(tm, d), _rows(tm, f), _rows(tm, f), _whole((f, d))],
        out_specs=[_rows(tm, f)] * 3,
        semantics="parallel",
    )(dx2, fg, fu, w_down)


def _ffn_bwd_b(dfg, dfu, x1, dx2, g_ffn, w_gate_t, w_up_t, comm=None):
    t, d = x1.shape
    f = dfg.shape[1]
    tm = min(MATMUL_TILE, t)

    def body(dfg_ref, dfu_ref, x_ref, dx2_ref, g_ref, wg_ref, wu_ref, dx1_ref, h2_ref, vec_ref):
        _zero_on_first(vec_ref)
        dh2 = _dot(dfg_ref[...], wg_ref[...]) + _dot(dfu_ref[...], wu_ref[...])
        xhat, r = _rms_stats(x_ref[...])
        h2_ref[...] = (xhat * g_ref[...]).astype(h2_ref.dtype)
        dxn, dg = _rms_bwd(dh2, xhat, r, g_ref[...])
        dx1_ref[...] = dx2_ref[...] + dxn
        vec_ref[0:1, :] += _rowsum(dg)

    return _run(
        body, comm, name="ffn_bwd_b", grid=(t // tm,),
        out_shape=[jax.ShapeDtypeStruct((t, d), F32), jax.ShapeDtypeStruct((t, d), BF16),
                   jax.ShapeDtypeStruct((SUBLANES, d), F32)],
        in_specs=[_rows(tm, f), _rows(tm, f), _rows(tm, d), _rows(tm, d), _whole((1, d)), _whole((f, d)), _whole((f, d))],
        out_specs=[_rows(tm, d), _rows(tm, d), _acc((SUBLANES, d))],
        semantics="arbitrary",
    )(dfg, dfu, x1, dx2, g_ffn, w_gate_t, w_up_t)


def _matmul_tn(a, b, tn, name, dtype=F32, comm=None):
    t, k = a.shape
    n = b.shape[1]

    def body(a_ref, b_ref, o_ref):
        o_ref[...] = _dot_tn(a_ref[...], b_ref[...]).astype(o_ref.dtype)

    return _run(
        body, comm, name=name, grid=(n // tn,),
        out_shape=jax.ShapeDtypeStruct((k, n), dtype),
        in_specs=[_whole((t, k)), pl.BlockSpec((t, tn), lambda j: (0, j))],
        out_specs=pl.BlockSpec((k, tn), lambda j: (0, j)),
        semantics="parallel",
    )(a, b)


def _merge_bwd(dx1, merged, ma, mb, za, zb, y_a, h, w_o, w_a_out_t, w_b_out, comm=None):
    t, d = dx1.shape
    sa, lw = y_a.shape[1], h.shape[1]
    tm = min(TOKEN_TILE, t)

    def body(dx1_ref, mg_ref, ma_ref, mb_ref, za_ref, zb_ref, ya_ref, h_ref, wo_ref, wa_ref, wb_ref,
             dza_ref, dzb_ref, dya_ref, dyb_ref, dwo_out, dwa_out, dwb_out, dwo_ref, dwa_ref, dwb_ref):
        _zero_on_first(dwo_ref, dwa_ref, dwb_ref)
        dx1v = dx1_ref[...].astype(BF16)
        dmg = _dot_nt(dx1v, wo_ref[...])
        ga = _sigmoid(za_ref[...])
        gb = _sigmoid(zb_ref[...])
        dza_ref[...] = (dmg * ma_ref[...].astype(F32) * ga * (1.0 - ga)).astype(dza_ref.dtype)
        dzb_ref[...] = (dmg * mb_ref[...].astype(F32) * gb * (1.0 - gb)).astype(dzb_ref.dtype)
        dma = (dmg * ga).astype(BF16)
        dmb = (dmg * gb).astype(BF16)
        dya_ref[...] = _dot(dma, wa_ref[...])
        dyb_ref[...] = _dot_nt(dmb, wb_ref[...])
        dwo_ref[...] += _dot_tn(mg_ref[...], dx1v)
        dwa_ref[...] += _dot_tn(dma, ya_ref[...])
        dwb_ref[...] += _dot_tn(h_ref[...], dmb)
        _store_on_last([(dwo_ref, dwo_out), (dwa_ref, dwa_out), (dwb_ref, dwb_out)])

    return _run(
        body, comm, name="merge_bwd", grid=(t // tm,),
        out_shape=[jax.ShapeDtypeStruct((t, d), BF16), jax.ShapeDtypeStruct((t, d), BF16),
                   jax.ShapeDtypeStruct((t, sa), F32), jax.ShapeDtypeStruct((t, lw), F32),
                   jax.ShapeDtypeStruct((d, d), BF16), jax.ShapeDtypeStruct((d, sa), BF16),
                   jax.ShapeDtypeStruct((lw, d), BF16)],
        in_specs=[_rows(tm, d), _rows(tm, d), _rows(tm, d), _rows(tm, d), _rows(tm, d), _rows(tm, d),
                  _rows(tm, sa), _rows(tm, lw), _whole((d, d)), _whole((d, sa)), _whole((lw, d))],
        out_specs=[_rows(tm, d), _rows(tm, d), _rows(tm, sa), _rows(tm, lw), _acc((d, d)), _acc((d, sa)), _acc((lw, d))],
        scratch_shapes=[pltpu.VMEM((d, d), F32), pltpu.VMEM((d, sa), F32), pltpu.VMEM((lw, d), F32)],
        semantics="arbitrary",
    )(dx1, merged, ma, mb, za, zb, y_a, h, w_o, w_a_out_t, w_b_out)


def _fold_diag_blocks(dense, row_group, col_group, row0=0, col0=0):
    r, c = dense.shape
    rows = lax.broadcasted_iota(jnp.int32, (r, c), 0) + row0
    cols = lax.broadcasted_iota(jnp.int32, (r, c), 1) + col0
    kept = jnp.where(rows // row_group == cols // col_group, dense, 0.0)
    pick = (lax.broadcasted_iota(jnp.int32, (row_group, r), 0)
            == lax.broadcasted_iota(jnp.int32, (row_group, r), 1) % row_group).astype(F32)
    return jnp.dot(pick, kept, preferred_element_type=F32, precision=lax.Precision.HIGHEST)


def _lru_bwd(dh, xc, hprev, u, conv_w, wr_blk, b_r, wi_blk, b_i, lru_lambda, head_dim, comm=None):
    t, w = dh.shape
    tc = min(TIME_CHUNK, t)
    steps = t // tc
    halo = SUBLANES
    sub_per_chunk = tc // halo
    slabs = w // LRU_SLAB

    def body(dh_ref, xc_ref, hp_ref, u_ref, uh_ref, cw_ref, wr_ref, br_ref, wi_ref, bi_ref, lam_ref,
             du_ref, dwr_out, dwi_out, vec_ref, lam_s, a_s, dxc_s, uext_s, carry_s, dwr_ref, dwi_ref):
        chunk = steps - 1 - pl.program_id(0)

        @pl.when(pl.program_id(0) == 0)
        def _():
            carry_s[...] = jnp.zeros_like(carry_s)
            dxc_s[tc:tc + halo, :] = jnp.zeros((halo, w), F32)
            dwr_ref[...] = jnp.zeros_like(dwr_ref)
            dwi_ref[...] = jnp.zeros_like(dwi_ref)
            vec_ref[...] = jnp.zeros_like(vec_ref)

        xc = xc_ref[...]
        r, ig, sp, log_a = _lru_gates(xc, wr_ref, br_ref, wi_ref, bi_ref, lam_ref)
        a = jnp.exp(log_a)
        a_s[...] = a

        def step(i, q):
            at = pl.ds(tc - 1 - i, 1)
            lam_row = dh_ref[at, :] + q
            lam_s[at, :] = lam_row
            return a_s[at, :] * lam_row

        carry_s[0:1, :] = lax.fori_loop(0, tc, step, carry_s[0:1, :], unroll=8)
        lam = lam_s[...]
        mult = jnp.sqrt(-_expm1(2.0 * log_a))
        d_log_a = lam * hp_ref[...] * a - (lam * ig * xc) * (a * a) / mult
        d_ig = lam * mult * xc
        dpre_r = (d_log_a * (-LRU_C * sp)) * r * (1.0 - r)
        dpre_i = d_ig * ig * (1.0 - ig)
        dxc = lam * mult * ig + _slab_dot(dpre_r, wr_ref, transposed=True) + _slab_dot(dpre_i, wi_ref, transposed=True)
        xcb, drb, dib = xc.astype(BF16), dpre_r.astype(BF16), dpre_i.astype(BF16)
        for j in range(slabs):
            cols = slice(j * LRU_SLAB, (j + 1) * LRU_SLAB)
            dwr_ref[j] += _dot_tn(drb[:, cols], xcb[:, cols])
            dwi_ref[j] += _dot_tn(dib[:, cols], xcb[:, cols])
        vec_ref[0:1, :] += _rowsum(dxc)
        vec_ref[1:2, :] += _rowsum(dpre_r)
        vec_ref[2:3, :] += _rowsum(dpre_i)
        vec_ref[3:4, :] += _rowsum(d_log_a * (-LRU_C * r)) * (-_sigmoid(-lam_ref[...]))
        dxc_s[0:tc, :] = dxc
        du = cw_ref[CONV_WIDTH - 1:CONV_WIDTH, :] * dxc
        for k in range(CONV_WIDTH - 1):
            off = CONV_WIDTH - 1 - k
            du = du + cw_ref[k:k + 1, :] * dxc_s[off:off + tc, :]
        du_ref[...] = du.astype(du_ref.dtype)
        dxc_s[tc:tc + halo, :] = dxc_s[0:halo, :]
        uext_s[0:halo, :] = jnp.where(chunk > 0, uh_ref[...], 0.0)
        uext_s[halo:halo + tc, :] = u_ref[...]
        for k in range(CONV_WIDTH):
            off = halo - (CONV_WIDTH - 1) + k
            vec_ref[4 + k:5 + k, :] += _rowsum(dxc * uext_s[off:off + tc, :])

        @pl.when(pl.program_id(0) == steps - 1)
        def _():
            for j in range(slabs):
                cols = slice(j * LRU_SLAB, (j + 1) * LRU_SLAB)
                dwr_out[:, cols] = _fold_diag_blocks(dwr_ref[j], head_dim, head_dim).astype(dwr_out.dtype)
                dwi_out[:, cols] = _fold_diag_blocks(dwi_ref[j], head_dim, head_dim).astype(dwi_out.dtype)

    halo_spec = pl.BlockSpec((halo, w), lambda i: (jnp.maximum((steps - 1 - i) * sub_per_chunk - 1, 0), 0))
    return _run(
        body, comm, name="lru_bwd", grid=(steps,),
        out_shape=[jax.ShapeDtypeStruct((t, w), BF16), jax.ShapeDtypeStruct((head_dim, w), BF16),
                   jax.ShapeDtypeStruct((head_dim, w), BF16), jax.ShapeDtypeStruct((SUBLANES, w), F32)],
        in_specs=[_rows_rev(tc, w, steps)] * 4 + [halo_spec, _whole((CONV_WIDTH, w)), _whole(wr_blk.shape),
                                                  _whole((1, w)), _whole(wi_blk.shape), _whole((1, w)), _whole((1, w))],
        out_specs=[_rows_rev(tc, w, steps), _acc((head_dim, w)), _acc((head_dim, w)), _acc((SUBLANES, w))],
        scratch_shapes=[pltpu.VMEM((tc, w), F32), pltpu.VMEM((tc, w), F32), pltpu.VMEM((tc + halo, w), F32),
                        pltpu.VMEM((halo + tc, w), F32), pltpu.VMEM((SUBLANES, w), F32),
                        pltpu.VMEM((slabs, LRU_SLAB, LRU_SLAB), F32), pltpu.VMEM((slabs, LRU_SLAB, LRU_SLAB), F32)],
        semantics="arbitrary",
    )(dh, xc, hprev, u, u, conv_w, wr_blk, b_r, wi_blk, b_i, lru_lambda)


def _s5_bwd(dya, y, sr, si, u, w_glu, b_glu, cre_blk, cimn_blk, bbr_blk, bbi_blk, ar, ai, d_skip, comm=None):
    t, sa = dya.shape
    gn = sr.shape[1]
    ns, _, sw = bbr_blk.shape
    tc = min(TIME_CHUNK, t)
    steps = t // tc
    halo = SUBLANES

    def body(dya_ref, y_ref, sr_ref, si_ref, u_ref, wg_ref, bg_ref, cre_ref, cim_ref, bbr_ref, bbi_ref,
             ar_ref, ai_ref, d_ref, du_ref, lr_ref, li_ref, dy_ref, dwg_out, vsa_ref, vgn_ref, gr_s, gi_s, cr_s, ci_s,
             dwg_ref):
        @pl.when(pl.program_id(0) == 0)
        def _():
            cr_s[...] = jnp.zeros_like(cr_s)
            ci_s[...] = jnp.zeros_like(ci_s)
            gr_s[tc:tc + halo, :] = jnp.zeros((halo, gn), F32)
            gi_s[tc:tc + halo, :] = jnp.zeros((halo, gn), F32)
            dwg_ref[...] = jnp.zeros_like(dwg_ref)
            vsa_ref[...] = jnp.zeros_like(vsa_ref)
            vgn_ref[...] = jnp.zeros_like(vgn_ref)

        yv = y_ref[...]
        uv = u_ref[...]
        zz = _gelu(yv)
        sg = _sigmoid(_dot(zz, wg_ref[...]) + bg_ref[...])
        dyav = dya_ref[...]
        dq = dyav * zz * sg * (1.0 - sg)
        dzz = dyav * sg + _dot_nt(dq, wg_ref[...])
        dwg_ref[...] += _dot_tn(zz, dq)
        dy = dzz * _gelu_grad(yv)
        dyb = dy.astype(BF16)
        dy_ref[...] = dyb.astype(dy_ref.dtype)
        vsa_ref[0:1, :] += _rowsum(dq)
        vsa_ref[1:2, :] += _rowsum(dy * uv)
        for m in range(ns):
            dym = dyb[:, m * S5_SLAB:(m + 1) * S5_SLAB]
            gr_s[0:tc, m * sw:(m + 1) * sw] = _dot_nt(dym, cre_ref[m])
            gi_s[0:tc, m * sw:(m + 1) * sw] = _dot_nt(dym, cim_ref[m])
        a_r = ar_ref[...]
        a_i = ai_ref[...]

        def step(i, carry):
            l_r, l_i = carry
            at = pl.ds(tc - 1 - i, 1)
            n_r = gr_s[at, :] + a_r * l_r + a_i * l_i
            n_i = gi_s[at, :] + a_r * l_i - a_i * l_r
            gr_s[at, :] = n_r
            gi_s[at, :] = n_i
            return n_r, n_i

        l_r, l_i = lax.fori_loop(0, tc, step, (cr_s[0:1, :], ci_s[0:1, :]), unroll=8)
        cr_s[0:1, :] = l_r
        ci_s[0:1, :] = l_i
        nxt_r = gr_s[1:tc + 1, :]
        nxt_i = gi_s[1:tc + 1, :]
        srv = sr_ref[...].astype(F32)
        siv = si_ref[...].astype(F32)
        vgn_ref[0:1, :] += _rowsum(nxt_r * srv + nxt_i * siv)
        vgn_ref[1:2, :] += _rowsum(nxt_i * srv - nxt_r * siv)
        lam_r = gr_s[0:tc, :]
        lam_i = gi_s[0:tc, :]
        gr_s[tc:tc + halo, :] = gr_s[0:halo, :]
        gi_s[tc:tc + halo, :] = gi_s[0:halo, :]
        lrb = lam_r.astype(BF16)
        lib = lam_i.astype(BF16)
        lr_ref[...] = lrb.astype(lr_ref.dtype)
        li_ref[...] = lib.astype(li_ref.dtype)
        for m in range(ns):
            states, chans = slice(m * sw, (m + 1) * sw), slice(m * S5_SLAB, (m + 1) * S5_SLAB)
            du_ref[:, chans] = (_dot_nt(lrb[:, states], bbr_ref[m]) + _dot_nt(lib[:, states], bbi_ref[m])
                                + dy[:, chans] * d_ref[:, chans]).astype(du_ref.dtype)
        _store_on_last([(dwg_ref, dwg_out)])

    return _run(
        body, comm, name="s5_bwd", grid=(steps,),
        out_shape=[jax.ShapeDtypeStruct((t, sa), BF16), jax.ShapeDtypeStruct((t, gn), BF16),
                   jax.ShapeDtypeStruct((t, gn), BF16), jax.ShapeDtypeStruct((t, sa), BF16),
                   jax.ShapeDtypeStruct((sa, sa), BF16), jax.ShapeDtypeStruct((SUBLANES, sa), F32),
                   jax.ShapeDtypeStruct((SUBLANES, gn), F32)],
        in_specs=[_rows_rev(tc, sa, steps), _rows_rev(tc, sa, steps), _rows_rev(tc, gn, steps), _rows_rev(tc, gn, steps),
                  _rows_rev(tc, sa, steps), _whole((sa, sa)), _whole((1, sa)), _whole(cre_blk.shape),
                  _whole(cimn_blk.shape), _whole(bbr_blk.shape), _whole(bbi_blk.shape), _whole((1, gn)), _whole((1, gn)),
                  _whole((1, sa))],
        out_specs=[_rows_rev(tc, sa, steps), _rows_rev(tc, gn, steps), _rows_rev(tc, gn, steps), _rows_rev(tc, sa, steps),
                   _acc((sa, sa)), _acc((SUBLANES, sa)), _acc((SUBLANES, gn))],
        scratch_shapes=[pltpu.VMEM((tc + halo, gn), F32), pltpu.VMEM((tc + halo, gn), F32),
                        pltpu.VMEM((SUBLANES, gn), F32), pltpu.VMEM((SUBLANES, gn), F32), pltpu.VMEM((sa, sa), F32)],
        semantics="arbitrary",
    )(dya, y, sr, si, u, w_glu, b_glu, cre_blk, cimn_blk, bbr_blk, bbi_blk, ar, ai, d_skip)


def _inproj_bwd(dparts, x, dx1, g_mix, w_in_t, comm=None):
    t, d = x.shape
    n = w_in_t.shape[0]
    widths = [p.shape[1] for p in dparts]
    offs = [sum(widths[:i]) for i in range(len(widths) + 1)]
    tm = min(MATMUL_TILE, t)
    np_ = len(dparts)

    def body(*refs):
        dz_refs = refs[:np_]
        x_ref, dx1_ref, g_ref, w_ref, gx_ref, h_ref, vd_ref, vn_ref = refs[np_:]
        _zero_on_first(vd_ref, vn_ref)
        dh = jnp.zeros((tm, d), F32)
        for k, r in enumerate(dz_refs):
            lo, hi = offs[k], offs[k + 1]
            dzk = r[...]
            dh = dh + _dot(dzk, w_ref[lo:hi, :])
            vn_ref[0:1, lo:hi] += _rowsum(dzk.astype(F32))
        xhat, r0 = _rms_stats(x_ref[...])
        h_ref[...] = (xhat * g_ref[...]).astype(h_ref.dtype)
        dxn, dg = _rms_bwd(dh, xhat, r0, g_ref[...])
        gx_ref[...] = dx1_ref[...] + dxn
        vd_ref[0:1, :] += _rowsum(dg)

    return _run(
        body, comm, name="inproj_bwd", grid=(t // tm,),
        out_shape=[jax.ShapeDtypeStruct((t, d), F32), jax.ShapeDtypeStruct((t, d), BF16),
                   jax.ShapeDtypeStruct((SUBLANES, d), F32), jax.ShapeDtypeStruct((SUBLANES, n), F32)],
        in_specs=[_rows(tm, w) for w in widths] + [_rows(tm, d), _rows(tm, d), _whole((1, d)), _whole((n, d))],
        out_specs=[_rows(tm, d), _rows(tm, d), _acc((SUBLANES, d)), _acc((SUBLANES, n))],
        semantics="arbitrary",
    )(*dparts, x, dx1, g_mix, w_in_t)


def _dw_from_parts(dparts, h, tn, name, comm=None):
    t, d = h.shape
    widths = [p.shape[1] for p in dparts]
    offs = [sum(widths[:i]) for i in range(len(widths) + 1)]
    np_ = len(dparts)

    def body(*refs):
        h_ref, o_ref = refs[np_], refs[np_ + 1]
        hv = h_ref[...]
        for k, r in enumerate(refs[:np_]):
            o_ref[offs[k]:offs[k + 1], :] = _dot_tn(r[...], hv).astype(o_ref.dtype)

    return _run(
        body, comm, name=name, grid=(d // tn,),
        out_shape=jax.ShapeDtypeStruct((offs[-1], d), BF16),
        in_specs=[_whole(p.shape) for p in dparts] + [pl.BlockSpec((t, tn), lambda j: (0, j))],
        out_specs=pl.BlockSpec((offs[-1], tn), lambda j: (0, j)),
        semantics="parallel",
    )(*dparts, h)


def _prep(lr_row, li_row, ldt_row, lr_col, li_col, ldt_col, b_re, b_im, c_re, c_im, w_r, w_i, comm=None):
    gn, pch = b_re.shape
    sa, n = c_re.shape
    w, hd = w_r.shape
    ns, sw, lsl = sa // S5_SLAB, S5_SLAB * n // pch, w // LRU_SLAB

    def spread_cols(vals, row_group, col_group, width):
        r, k = vals.shape
        tile = (lax.broadcasted_iota(jnp.int32, (k, width), 0) == lax.broadcasted_iota(jnp.int32, (k, width), 1) % k)
        rows = lax.broadcasted_iota(jnp.int32, (r, width), 0) // row_group
        cols = lax.broadcasted_iota(jnp.int32, (r, width), 1) // col_group
        return jnp.where(rows == cols, _dot(vals, tile.astype(BF16)), 0.0)

    def spread_rows(vals, row_group, col_group, height):
        k, c = vals.shape
        tile = (lax.broadcasted_iota(jnp.int32, (height, k), 0) % k == lax.broadcasted_iota(jnp.int32, (height, k), 1))
        rows = lax.broadcasted_iota(jnp.int32, (height, c), 0) // row_group
        cols = lax.broadcasted_iota(jnp.int32, (height, c), 1) // col_group
        return jnp.where(rows == cols, _dot(tile.astype(BF16), vals), 0.0)

    def body(lrr, lir, ldr, lrc, lic, ldc, bre, bim, cre, cim, wr, wi,
             ar_o, ai_o, bbr_o, bbi_o, cre_o, cim_o, wr_o, wi_o):
        ar, ai, _, _ = _disc_scalars(lrr[...], lir[...], ldr[...])
        ar_o[...] = ar
        ai_o[...] = ai
        _, _, bbr, bbi = _disc_cols(lrc[...], lic[...], ldc[...], bre[...], bim[...])
        bbr_t, bbi_t = bbr.T, bbi.T
        for m in range(ns):
            bbr_o[m] = spread_rows(bbr_t[:, m * sw:(m + 1) * sw], pch, n, S5_SLAB).astype(bbr_o.dtype)
            bbi_o[m] = spread_rows(bbi_t[:, m * sw:(m + 1) * sw], pch, n, S5_SLAB).astype(bbi_o.dtype)
            rows = slice(m * S5_SLAB, (m + 1) * S5_SLAB)
            cre_o[m] = spread_rows(cre[rows, :].T, n, pch, sw).astype(cre_o.dtype)
            cim_o[m] = spread_rows(-cim[rows, :].T, n, pch, sw).astype(cim_o.dtype)
        for j in range(lsl):
            rows = slice(j * LRU_SLAB, (j + 1) * LRU_SLAB)
            wr_o[j] = spread_cols(wr[rows, :], hd, hd, LRU_SLAB).astype(wr_o.dtype)
            wi_o[j] = spread_cols(wi[rows, :], hd, hd, LRU_SLAB).astype(wi_o.dtype)

    args = (lr_row, li_row, ldt_row, lr_col, li_col, ldt_col, b_re, b_im, c_re, c_im, w_r, w_i)
    out_shape = [jax.ShapeDtypeStruct((1, gn), F32), jax.ShapeDtypeStruct((1, gn), F32),
                 jax.ShapeDtypeStruct((ns, S5_SLAB, sw), BF16), jax.ShapeDtypeStruct((ns, S5_SLAB, sw), BF16),
                 jax.ShapeDtypeStruct((ns, sw, S5_SLAB), BF16), jax.ShapeDtypeStruct((ns, sw, S5_SLAB), BF16),
                 jax.ShapeDtypeStruct((lsl, LRU_SLAB, LRU_SLAB), BF16),
                 jax.ShapeDtypeStruct((lsl, LRU_SLAB, LRU_SLAB), BF16)]
    return _run(
        body, comm, name="prep", grid=(1,), out_shape=out_shape, in_specs=[_whole(a.shape) for a in args],
        out_specs=[_acc(s.shape) for s in out_shape], semantics="arbitrary",
    )(*args)


def _s5_param_grads(lam_r, lam_i, sr, si, u, dy, pch, n, comm=None):
    t, gn = lam_r.shape
    sa = u.shape[1]
    sw = S5_SLAB * n // pch

    def body(lr_ref, li_ref, sr_ref, si_ref, u_ref, dy_ref, dbr_ref, dbi_ref, dcr_ref, dci_ref):
        uv = u_ref[...]
        dyv = dy_ref[...]
        dbr_ref[...] = _fold_diag_blocks(_dot_tn(uv, lr_ref[...]), pch, n).astype(dbr_ref.dtype)
        dbi_ref[...] = _fold_diag_blocks(_dot_tn(uv, li_ref[...]), pch, n).astype(dbi_ref.dtype)
        dcr_ref[...] = _fold_diag_blocks(_dot_tn(sr_ref[...], dyv), n, pch).astype(dcr_ref.dtype)
        dci_ref[...] = _fold_diag_blocks(_dot_tn(si_ref[...], dyv), n, pch).astype(dci_ref.dtype)

    states = pl.BlockSpec((t, sw), lambda m: (0, m))
    chans = pl.BlockSpec((t, S5_SLAB), lambda m: (0, m))
    return _run(
        body, comm, name="s5_param_grads", grid=(sa // S5_SLAB,),
        out_shape=[jax.ShapeDtypeStruct((pch, gn), BF16), jax.ShapeDtypeStruct((pch, gn), BF16),
                   jax.ShapeDtypeStruct((n, sa), BF16), jax.ShapeDtypeStruct((n, sa), BF16)],
        in_specs=[states, states, states, states, chans, chans],
        out_specs=[pl.BlockSpec((pch, sw), lambda m: (0, m)), pl.BlockSpec((pch, sw), lambda m: (0, m)),
                   pl.BlockSpec((n, S5_SLAB), lambda m: (0, m)), pl.BlockSpec((n, S5_SLAB), lambda m: (0, m))],
        semantics="parallel",
    )(lam_r, lam_i, sr, si, u, dy)


SMALL_PARTS = ["vec_tail", "vec_ffn", "vec_lru", "vec_mix", "vec_bin", "vec_sa", "vec_gn", "dw_r", "dw_i", "dbb_re",
               "dbb_im", "dc_re", "dc_imn", "loss"]


def _small_reduce(parts, shapes, lr_col, li_col, ldt_col, b_re, b_im, groups, comm=None):
    gn, pch = b_re.shape
    n = gn // groups
    nparts = parts[SMALL_PARTS[0]].size // math.prod(shapes[SMALL_PARTS[0]])
    np_, nout = len(SMALL_PARTS), 24

    def body(*refs):
        ins = refs[:np_]
        lr, li, ldt, bre, bim = refs[np_:np_ + 5]
        outs = refs[np_ + 5:np_ + 5 + nout]
        sums = dict(zip(SMALL_PARTS, refs[np_ + 5 + nout:]))

        @pl.when(pl.program_id(0) == 0)
        def _():
            for k, r in zip(SMALL_PARTS, ins):
                sums[k][...] = r[...].astype(F32)

        @pl.when(pl.program_id(0) > 0)
        def _():
            for k, r in zip(SMALL_PARTS, ins):
                sums[k][...] += r[...].astype(F32)

        @pl.when(pl.program_id(0) == nparts - 1)
        def _():
            finish({k: s[...] for k, s in sums.items()}, lr, li, ldt, bre, bim, *outs)

    def finish(tot, lr, li, ldt, bre, bim, o_loss, o_gmix, o_bin, o_bglu, o_s5d, o_convb, o_br, o_bi, o_lam, o_gffn,
               o_gpg, o_bpg, o_gple, o_gfin, o_wr, o_wi, o_cre, o_cim, o_bre, o_bim, o_lre, o_lim, o_ldt, o_convw):
        o_loss[...] = tot["loss"]
        o_convw[...] = tot["vec_lru"][SUBLANES - CONV_WIDTH:SUBLANES]
        o_bpg[...] = tot["vec_tail"][0:1]
        o_gpg[...] = tot["vec_tail"][1:2]
        o_gple[...] = tot["vec_tail"][2:3]
        o_gfin[...] = tot["vec_tail"][3:4]
        o_gffn[...] = tot["vec_ffn"][0:1]
        o_convb[...] = tot["vec_lru"][0:1]
        o_br[...] = tot["vec_lru"][1:2]
        o_bi[...] = tot["vec_lru"][2:3]
        o_lam[...] = tot["vec_lru"][3:4]
        o_gmix[...] = tot["vec_mix"][0:1]
        o_bin[...] = tot["vec_bin"][0:1]
        o_bglu[...] = tot["vec_sa"][0:1]
        o_s5d[...] = tot["vec_sa"][1:2]
        o_wr[...] = tot["dw_r"].T
        o_wi[...] = tot["dw_i"].T
        o_cre[...] = tot["dc_re"].T
        o_cim[...] = -tot["dc_imn"].T
        d_a = tot["vec_gn"].T
        _, chain = jax.vjp(_disc_cols, lr[...], li[...], ldt[...], bre[...], bim[...])
        d_lr, d_li, d_ldt, d_bre, d_bim = chain((d_a[:, 0:1], d_a[:, 1:2], tot["dbb_re"].T, tot["dbb_im"].T))
        o_lre[...] = d_lr
        o_lim[...] = d_li
        o_bre[...] = d_bre
        o_bim[...] = d_bim
        same = (lax.broadcasted_iota(jnp.int32, (groups, gn), 0)
                == lax.broadcasted_iota(jnp.int32, (groups, gn), 1) // n).astype(F32)
        o_ldt[...] = jnp.dot(same, d_ldt * jnp.ones((1, LANES), F32), preferred_element_type=F32,
                             precision=lax.Precision.HIGHEST)[:, 0:1]

    d = shapes["vec_mix"][1]
    nz = shapes["vec_bin"][1]
    sa = shapes["vec_sa"][1]
    w = shapes["vec_lru"][1]
    row = lambda c: jax.ShapeDtypeStruct((1, c), F32)
    out_shape = [jax.ShapeDtypeStruct(shapes["loss"], F32), row(d), row(nz), row(sa), row(sa), row(w), row(w), row(w),
                 row(w), row(d), row(d), row(d), row(d), row(d),
                 jax.ShapeDtypeStruct(shapes["dw_r"][::-1], F32), jax.ShapeDtypeStruct(shapes["dw_i"][::-1], F32),
                 jax.ShapeDtypeStruct(shapes["dc_re"][::-1], F32), jax.ShapeDtypeStruct(shapes["dc_imn"][::-1], F32),
                 jax.ShapeDtypeStruct((gn, pch), F32), jax.ShapeDtypeStruct((gn, pch), F32),
                 jax.ShapeDtypeStruct((gn, 1), F32), jax.ShapeDtypeStruct((gn, 1), F32),
                 jax.ShapeDtypeStruct((groups, 1), F32), jax.ShapeDtypeStruct((CONV_WIDTH, w), F32)]
    def part_spec(k):
        r, c = shapes[k]
        if parts[k].ndim == 3:
            return pl.BlockSpec((None, r, c), lambda i: (i, 0, 0))
        return pl.BlockSpec((r, c), lambda i: (i, 0))

    outs = _run(
        body, comm, name="small_reduce", grid=(nparts,), out_shape=out_shape,
        in_specs=[part_spec(k) for k in SMALL_PARTS] + [_whole(a.shape) for a in (lr_col, li_col, ldt_col, b_re, b_im)],
        out_specs=[_acc(s.shape) for s in out_shape],
        scratch_shapes=[pltpu.VMEM(shapes[k], F32) for k in SMALL_PARTS],
        semantics="arbitrary",
    )(*[parts[k] for k in SMALL_PARTS], lr_col, li_col, ldt_col, b_re, b_im)
    extra = None
    if comm is not None:
        outs, extra = outs
    names = ["loss", "g_mix", "b_in", "b_glu", "s5_d", "conv_b", "b_r", "b_i", "lru_lambda", "g_ffn", "g_ple_gate",
             "b_ple_gate", "g_ple", "g_final", "w_r", "w_i", "s5_c_re", "s5_c_im", "s5_b_re", "s5_b_im", "lam_re",
             "lam_im", "log_dt", "conv_w"]
    res = dict(zip(names, outs))
    return res if comm is None else (res, extra)


def _adamw_small(ws, gs, ms, vs):
    n = len(ws)

    def body(*refs):
        w_r, g_r, m_r, v_r = (refs[i * n:(i + 1) * n] for i in range(4))
        g_o, d_o, m_o, v_o = (refs[(4 + i) * n:(5 + i) * n] for i in range(4))
        for i in range(n):
            g = g_r[i][...]
            delta, m_new, v_new = _adamw_math(w_r[i][...], g, m_r[i][...], v_r[i][...])
            g_o[i][...] = g
            d_o[i][...] = delta
            m_o[i][...] = m_new
            v_o[i][...] = v_new

    shapes = [jax.ShapeDtypeStruct(a.shape, F32) for a in ws]
    outs = pl.pallas_call(body, name="adamw_small", out_shape=shapes * 4)(*ws, *gs, *ms, *vs)
    return outs[:n], outs[n:2 * n], outs[2 * n:3 * n], outs[3 * n:]


def _adamw_math(w, g, m, v):
    m_new = ADAM_B1 * m + (1.0 - ADAM_B1) * g
    v_new = ADAM_B2 * v + (1.0 - ADAM_B2) * (g * g)
    m_hat = m_new / (1.0 - ADAM_B1 ** ADAM_STEP)
    v_hat = v_new / (1.0 - ADAM_B2 ** ADAM_STEP)
    delta = -ADAM_LR * (m_hat / (jnp.sqrt(v_hat) + ADAM_EPS) + ADAM_WD * w)
    return delta, m_new, v_new


def _row_tile(rows):
    for cand in range(256, 0, -16):
        if rows % cand == 0:
            return cand
    return rows


def _adamw(parts, w, m, v, name, transposed=False):
    nw = len(w)
    rows, cols = w[0].shape
    npart = parts[0].shape[0]
    tr = _row_tile(rows)
    if transposed:
        parts_spec = pl.BlockSpec((npart, cols, tr), lambda i: (0, 0, i))
    else:
        parts_spec = pl.BlockSpec((npart, tr, cols), lambda i: (0, i, 0))

    def body(*refs):
        for i in range(nw):
            p_ref, w_ref, m_ref, v_ref = refs[4 * i:4 * i + 4]
            g_ref, d_ref, mo_ref, vo_ref = refs[4 * (nw + i):4 * (nw + i) + 4]
            g = p_ref[0].astype(F32)
            for k in range(1, npart):
                g = g + p_ref[k].astype(F32)
            if transposed:
                g = g.T
            delta, m_new, v_new = _adamw_math(w_ref[...], g, m_ref[...], v_ref[...])
            g_ref[...] = g
            d_ref[...] = delta
            mo_ref[...] = m_new
            vo_ref[...] = v_new

    res = pl.pallas_call(
        body, name=name, grid=(rows // tr,),
        out_shape=[jax.ShapeDtypeStruct((rows, cols), F32)] * (4 * nw),
        in_specs=([parts_spec] + [_rows(tr, cols)] * 3) * nw,
        out_specs=[_rows(tr, cols)] * (4 * nw),
        compiler_params=_params("parallel"),
    )(*[a for group in zip(parts, w, m, v) for a in group])
    return [res[4 * i:4 * i + 4] for i in range(nw)]


def _mesh_position():
    return lax.axis_index("x"), lax.axis_index("y"), lax.axis_index("c")


def _flip(pos, rel):
    x, y, c = pos
    return (1 - x if rel & 4 else x, 1 - y if rel & 2 else y, 1 - c if rel & 1 else c)


def _index(pos):
    return 4 * pos[0] + 2 * pos[1] + pos[2]


_ANY = pl.BlockSpec(memory_space=pl.ANY)
FLAT_ROWS = 32


def _dma_sems(n):
    return [pltpu.SemaphoreType.DMA((n, N_DEV - 1)), pltpu.SemaphoreType.DMA((n, N_DEV - 1)), pltpu.SemaphoreType.DMA((n,))]


def _block_of(ref, idx, rows, flat):
    if flat:
        return ref.at[pl.ds(pl.multiple_of(idx * rows, FLAT_ROWS), rows), :]
    return ref.at[idx]


class _Gather:
    chips = (4, 2, 6)
    rels = frozenset((1, 4, 2, 6))

    def __init__(self, shards):
        self.inputs = list(shards)
        self.flat = [s.shape[0] % FLAT_ROWS == 0 for s in shards]
        self.out_shape = [
            jax.ShapeDtypeStruct((N_DEV * s.shape[0], s.shape[1]) if f else (N_DEV,) + s.shape, s.dtype)
            for s, f in zip(shards, self.flat)]
        self.sems = _dma_sems(len(shards))

    def _copy(self, ins, outs, sems, i, k, block, to, own=False):
        dst = _block_of(outs[i], _index(block), self.inputs[i].shape[0], self.flat[i])
        return pltpu.make_async_remote_copy(
            src_ref=ins[i] if own else dst, dst_ref=dst, send_sem=sems[0].at[i, k], recv_sem=sems[1].at[i, k],
            device_id=to, device_id_type=MESH)

    def _local(self, ins, outs, sems, i, me):
        dst = _block_of(outs[i], _index(me), self.inputs[i].shape[0], self.flat[i])
        return pltpu.make_async_copy(ins[i], dst, sems[2].at[i])

    def _first(self, ins, outs, sems, i, me):
        cps = [self._copy(ins, outs, sems, i, 0, me, _flip(me, 1), own=True)]
        cps += [self._copy(ins, outs, sems, i, 1 + j, me, _flip(me, rel), own=True) for j, rel in enumerate(self.chips)]
        return cps

    def _passed(self, ins, outs, sems, i, j, me):
        return self._copy(ins, outs, sems, i, 4 + j, _flip(me, self.chips[j]), _flip(me, 1))

    def before(self, ins, outs, sems):
        n = len(self.inputs)
        me = _mesh_position()

        @pl.when(pl.program_id(0) == 0)
        def _():
            for i in range(n):
                self._local(ins, outs, sems, i, me).start()
                for cp in self._first(ins, outs, sems, i, me):
                    cp.start()

        @pl.when(pl.program_id(0) == pl.num_programs(0) - 1)
        def _():
            for j, rel in enumerate(self.chips):
                for i in range(n):
                    self._copy(ins, outs, sems, i, 1 + j, _flip(me, rel), me).wait_recv()
                    self._passed(ins, outs, sems, i, j, me).start()

    def after(self, ins, outs, sems):
        n = len(self.inputs)
        me = _mesh_position()
        sibling = _flip(me, 1)

        @pl.when(pl.program_id(0) == pl.num_programs(0) - 1)
        def _():
            for i in range(n):
                self._copy(ins, outs, sems, i, 0, sibling, me).wait_recv()
                for j, rel in enumerate(self.chips):
                    self._copy(ins, outs, sems, i, 4 + j, _flip(sibling, rel), me).wait_recv()
            for i in range(n):
                for cp in self._first(ins, outs, sems, i, me):
                    cp.wait_send()
                for j in range(len(self.chips)):
                    self._passed(ins, outs, sems, i, j, me).wait_send()
                self._local(ins, outs, sems, i, me).wait()


N_CHIPS = N_DEV // 2


def _chip(pos):
    return 2 * pos[0] + pos[1]


class _PairSwap:
    rels = frozenset((1,))

    def __init__(self, arrays):
        self.inputs = list(arrays)
        self.rows = [a.shape[0] // N_DEV for a in arrays]
        for r in self.rows:
            assert r % FLAT_ROWS == 0, r
        self.out_shape = [jax.ShapeDtypeStruct((N_CHIPS, r, a.shape[1]), a.dtype) for a, r in zip(arrays, self.rows)]
        n = len(arrays)
        self.sems = [pltpu.SemaphoreType.DMA((n, N_CHIPS)), pltpu.SemaphoreType.DMA((n, N_CHIPS))]

    def _copy(self, ins, outs, sems, i, j, me):
        sibling = _flip(me, 1)
        return pltpu.make_async_remote_copy(
            src_ref=_block_of(ins[i], 2 * j + sibling[2], self.rows[i], True), dst_ref=outs[i].at[j],
            send_sem=sems[0].at[i, j], recv_sem=sems[1].at[i, j], device_id=sibling, device_id_type=MESH)

    def before(self, ins, outs, sems):
        me = _mesh_position()

        @pl.when(pl.program_id(0) == 0)
        def _():
            for i in range(len(self.inputs)):
                for j in range(N_CHIPS):
                    self._copy(ins, outs, sems, i, j, me).start()

    def after(self, ins, outs, sems):
        me = _mesh_position()

        @pl.when(pl.program_id(0) == pl.num_programs(0) - 1)
        def _():
            for i in range(len(self.inputs)):
                for j in range(N_CHIPS):
                    self._copy(ins, outs, sems, i, j, me).wait()


class _ChipExchange:
    chips = (4, 2, 6)
    rels = frozenset(chips)

    def __init__(self, arrays):
        self.inputs = list(arrays)
        self.out_shape = [jax.ShapeDtypeStruct(a.shape, a.dtype) for a in arrays]
        n = len(arrays)
        self.sems = [pltpu.SemaphoreType.DMA((n, 3)), pltpu.SemaphoreType.DMA((n, 3)), pltpu.SemaphoreType.DMA((n,))]

    def _send(self, ins, outs, sems, i, k, me):
        peer = _flip(me, self.chips[k])
        return pltpu.make_async_remote_copy(
            src_ref=ins[i].at[_chip(peer)], dst_ref=outs[i].at[_chip(me)], send_sem=sems[0].at[i, k],
            recv_sem=sems[1].at[i, k], device_id=peer, device_id_type=MESH)

    def _arrival(self, ins, outs, sems, i, k, me):
        peer = _flip(me, self.chips[k])
        return pltpu.make_async_remote_copy(
            src_ref=ins[i].at[_chip(me)], dst_ref=outs[i].at[_chip(peer)], send_sem=sems[0].at[i, k],
            recv_sem=sems[1].at[i, k], device_id=peer, device_id_type=MESH)

    def _local(self, ins, outs, sems, i, me):
        return pltpu.make_async_copy(ins[i].at[_chip(me)], outs[i].at[_chip(me)], sems[2].at[i])

    def before(self, ins, outs, sems):
        me = _mesh_position()

        @pl.when(pl.program_id(0) == 0)
        def _():
            for i in range(len(self.inputs)):
                self._local(ins, outs, sems, i, me).start()
                for k in range(len(self.chips)):
                    self._send(ins, outs, sems, i, k, me).start()

    def after(self, ins, outs, sems):
        me = _mesh_position()

        @pl.when(pl.program_id(0) == pl.num_programs(0) - 1)
        def _():
            for i in range(len(self.inputs)):
                for k in range(len(self.chips)):
                    self._arrival(ins, outs, sems, i, k, me).wait_recv()
            for i in range(len(self.inputs)):
                for k in range(len(self.chips)):
                    self._send(ins, outs, sems, i, k, me).wait_send()
                self._local(ins, outs, sems, i, me).wait()


def _pair_add(grads, halves, name):
    n = len(grads)

    def body(*refs):
        g_refs, h_refs, o_refs = refs[:n], refs[n:2 * n], refs[2 * n:]
        c = lax.axis_index("c")
        for i in range(n):
            r = h_refs[i].shape[1]
            for j in range(N_CHIPS):
                own = g_refs[i][pl.ds(pl.multiple_of((2 * j + c) * r, FLAT_ROWS), r), :]
                o_refs[i][j] = (own.astype(F32) + h_refs[i][j].astype(F32)).astype(o_refs[i].dtype)

    return pl.pallas_call(
        body, name=name, out_shape=[jax.ShapeDtypeStruct(h.shape, h.dtype) for h in halves],
        compiler_params=pltpu.CompilerParams(vmem_limit_bytes=VMEM_LIMIT),
    )(*grads, *halves)


class _Both:
    def __init__(self, first, second):
        self.jobs = (first, second)
        self.rels = first.rels | second.rels
        self.inputs = first.inputs + second.inputs
        self.out_shape = first.out_shape + second.out_shape
        self.sems = first.sems + second.sems

    def _each(self, ins, outs, sems):
        a = self.jobs[0]
        i, o, s = len(a.inputs), len(a.out_shape), len(a.sems)
        return ((a, ins[:i], outs[:o], sems[:s]), (self.jobs[1], ins[i:], outs[o:], sems[s:]))

    def before(self, ins, outs, sems):
        for job, i, o, s in self._each(ins, outs, sems):
            job.before(i, o, s)

    def after(self, ins, outs, sems):
        for job, i, o, s in self._each(ins, outs, sems):
            job.after(i, o, s)


_COLLECTIVE_IDS = {(1,): 0, (2, 4, 6): 1, (1, 2, 4, 6): 2}


def _entry_barrier(rels):
    @pl.when(pl.program_id(0) == 0)
    def _():
        me = _mesh_position()
        sem = pltpu.get_barrier_semaphore()
        for rel in rels:
            pl.semaphore_signal(sem, inc=1, device_id=_flip(me, rel), device_id_type=MESH)
        pl.semaphore_wait(sem, len(rels))


def _run(body, comm, *, semantics, out_shape, in_specs, out_specs, scratch_shapes=(), **kw):
    if comm is None:
        return pl.pallas_call(body, out_shape=out_shape, in_specs=in_specs, out_specs=out_specs,
                              scratch_shapes=list(scratch_shapes), compiler_params=_params(semantics), **kw)
    single = not isinstance(out_shape, (list, tuple))
    outs = [out_shape] if single else list(out_shape)
    ospecs = [out_specs] if single else list(out_specs)
    counts = [len(in_specs), len(comm.inputs), len(outs), len(comm.out_shape), len(scratch_shapes), len(comm.sems)]
    rels = tuple(sorted(comm.rels))

    def carrying(*refs):
        groups, pos = [], 0
        for c in counts:
            groups.append(refs[pos:pos + c])
            pos += c
        main_in, comm_in, main_out, comm_out, main_scratch, comm_sems = groups
        _entry_barrier(rels)
        comm.before(comm_in, comm_out, comm_sems)
        body(*main_in, *main_out, *main_scratch)
        comm.after(comm_in, comm_out, comm_sems)

    call = pl.pallas_call(
        carrying, out_shape=outs + list(comm.out_shape), in_specs=list(in_specs) + [_ANY] * len(comm.inputs),
        out_specs=ospecs + [_ANY] * len(comm.out_shape), scratch_shapes=list(scratch_shapes) + list(comm.sems),
        compiler_params=pltpu.CompilerParams(dimension_semantics=("arbitrary",), vmem_limit_bytes=VMEM_LIMIT,
                                             collective_id=_COLLECTIVE_IDS[rels]), **kw)

    def apply(*args):
        res = call(*args, *comm.inputs)
        main = res[:len(outs)]
        return (main[0] if single else list(main)), list(res[len(outs):])

    return apply


def _alone(comm, name):
    return _run(lambda: None, comm, semantics="arbitrary", name=name, grid=(1,), out_shape=[], in_specs=[], out_specs=[])()[1]


SHARDED = {"w_in": 1, "w_glu": 0, "conv_w": 1, "w_a_out": 1, "w_b_out": 0, "w_o": 0, "w_ffn_gate": 1, "w_ffn_up": 1,
           "w_ffn_down": 0, "w_ple_gate": 0, "w_ple": 1}
TRANSPOSED = ("w_in", "w_a_out", "w_ffn_gate", "w_ffn_up", "w_ple")
LONG_AXIS_MINOR = ("w_in", "w_ffn_gate", "w_ffn_up")
NARROW_LAST = ("s5_b_re", "s5_b_im", "s5_d")
ADAMW_GROUPS = (("w_in",), ("w_glu",), ("conv_w",), ("w_a_out",), ("w_b_out", "w_o", "w_ple_gate"),
                ("w_ffn_gate", "w_ffn_up"), ("w_ffn_down",), ("w_ple",))
SMALL = ["g_mix", "b_in", "lam_re", "lam_im", "log_dt", "s5_b_re", "s5_b_im", "s5_c_re", "s5_c_im", "s5_d", "b_glu",
         "conv_b", "w_r", "b_r", "w_i", "b_i", "lru_lambda", "g_ffn", "g_ple_gate", "b_ple_gate", "g_ple", "g_final"]
WEIGHTS = ["g_mix", "w_in", "b_in", "lam_re", "lam_im", "log_dt", "s5_b_re", "s5_b_im", "s5_c_re", "s5_c_im", "s5_d",
           "w_glu", "b_glu", "conv_w", "conv_b", "w_r", "b_r", "w_i", "b_i", "lru_lambda", "w_a_out", "w_b_out", "w_o",
           "g_ffn", "w_ffn_gate", "w_ffn_up", "w_ffn_down", "g_ple_gate", "w_ple_gate", "b_ple_gate", "w_ple", "g_ple",
           "g_final"]


def _unblock(gathered, axis):
    nb, r, c = gathered.shape
    if axis == 0:
        return gathered.reshape(nb * r, c)
    return jnp.transpose(gathered, (1, 0, 2)).reshape(r, nb * c)


def _disc_scalars(lr, li, ldt):
    dt = jnp.exp(ldt)
    mag = jnp.exp(lr * dt)
    ar = mag * jnp.cos(li * dt)
    ai = mag * jnp.sin(li * dt)
    den = lr * lr + li * li
    nr = ar - 1.0
    fr = (nr * lr + ai * li) / den
    fi = (ai * lr - nr * li) / den
    return ar, ai, fr, fi


def _disc_cols(lr, li, ldt, b_re, b_im):
    ar, ai, fr, fi = _disc_scalars(lr, li, ldt)
    return ar, ai, fr * b_re - fi * b_im, fr * b_im + fi * b_re


def _local_step(x, p, target, src, small, disc, distributed=True):
    full = {} if distributed else dict(src)
    gw, halves, pairs, got = {}, {}, {}, {}

    def gather(keys):
        return (_Gather([src[k] for k in keys]), keys, full) if distributed else None

    def swap(keys):
        return (_PairSwap([gw[k] for k in keys]), keys, halves) if distributed else None

    def chips(keys):
        return (_ChipExchange([pairs[k] for k in keys]), keys, got) if distributed else None

    def add_pairs(keys):
        if distributed:
            pairs.update(zip(keys, _pair_add([gw[k] for k in keys], [halves[k] for k in keys], "pair_add_" + keys[0])))

    def carry(fn, *args, jobs=()):
        jobs = [j for j in jobs if j is not None]
        if not jobs:
            return fn(*args)
        comm = jobs[0][0]
        for j in jobs[1:]:
            comm = _Both(comm, j[0])
        res, extra = fn(*args, comm=comm)
        for job, keys, sink in jobs:
            sink.update(zip(keys, extra[:len(job.out_shape)]))
            extra = extra[len(job.out_shape):]
        return res

    d = x.shape[1]
    g, n, pch = small["s5_b_re"].shape
    heads = small["w_r"].shape[0]
    sa, lw = g * pch, small["lru_lambda"].shape[-1]
    widths = [sa, lw, d, d]
    row = lambda v: v.reshape(1, -1)

    hd = small["w_r"].shape[-1]
    ar_row, ai_row, bbr_blk, bbi_blk, cre_blk, cimn_blk, wr_blk, wi_blk = carry(
        _prep, *disc["rows"], *disc["cols"], disc["b_re"], disc["b_im"], small["s5_c_re"].reshape(sa, n),
        small["s5_c_im"].reshape(sa, n), small["w_r"].reshape(lw, hd), small["w_i"].reshape(lw, hd),
        jobs=[gather(["w_in"])])
    d_row = row(small["s5_d"])
    u_a, u_b, za, zb = carry(_inproj_fwd, x, row(small["g_mix"]), full["w_in"], row(small["b_in"]), widths,
                             jobs=[gather(["w_glu", "conv_w", "w_a_out", "w_b_out"])])
    conv_w = _unblock(full["conv_w"], 1) if distributed else full["conv_w"]
    branches = [_s5_fwd(u_a, bbr_blk, bbi_blk, ar_row, ai_row, cre_blk, cimn_blk, d_row, full["w_glu"],
                        row(small["b_glu"])),
                _lru_fwd(u_b, conv_w, row(small["conv_b"]), wr_blk, row(small["b_r"]), wi_blk, row(small["b_i"]),
                         row(small["lru_lambda"]))]
    (sr, si, y, y_a), (xc, h, hprev) = carry(_stages, branches, "branches_fwd",
                                             jobs=[gather(["w_ffn_gate", "w_ffn_up", "w_o"])])
    x1, merged, ma, mb, fg, fu = carry(
        _merge_ffn_up_fwd, y_a, h, za, zb, x, full["w_a_out"], full["w_b_out"], full["w_o"], row(small["g_ffn"]),
        full["w_ffn_gate"], full["w_ffn_up"], jobs=[gather(["w_ffn_down", "w_ple_gate", "w_ple"])])
    x2 = _ffn_down_fwd(fg, fu, x1, full["w_ffn_down"])

    dx2, loss_blk, gw["w_ple_gate"], gw["w_ple"], vec_tail = _tail_fwd_bwd(
        x2, p, target, row(small["g_ple_gate"]), full["w_ple_gate"], row(small["b_ple_gate"]), full["w_ple"],
        row(small["g_ple"]), row(small["g_final"]))
    dfg, dfu, act = carry(_ffn_bwd_a, dx2, fg, fu, full["w_ffn_down"], jobs=[swap(["w_ple_gate", "w_ple"])])
    tn_d = min(d, 512)
    gw["w_ffn_down"] = _matmul_tn(act, dx2, tn_d, "dw_ffn_down", BF16)
    add_pairs(["w_ple_gate", "w_ple"])
    dx1, h2, vec_ffn = carry(_ffn_bwd_b, dfg, dfu, x1, dx2, row(small["g_ffn"]), full["w_ffn_gate"], full["w_ffn_up"],
                             jobs=[chips(["w_ple_gate", "w_ple"]), swap(["w_ffn_down"])])
    gw["w_ffn_gate"] = _matmul_tn(dfg, h2, tn_d, "dw_ffn_gate", BF16)
    gw["w_ffn_up"] = _matmul_tn(dfu, h2, tn_d, "dw_ffn_up", BF16)
    add_pairs(["w_ffn_down"])
    dza, dzb, dya, dyb, gw["w_o"], gw["w_a_out"], gw["w_b_out"] = carry(
        _merge_bwd, dx1, merged, ma, mb, za, zb, y_a, h, full["w_o"], full["w_a_out"], full["w_b_out"],
        jobs=[chips(["w_ffn_down"]), swap(["w_ffn_gate", "w_ffn_up"])])
    add_pairs(["w_ffn_gate", "w_ffn_up"])
    du_b, dw_r, dw_i, vec_lru = carry(
        _lru_bwd, dyb, xc, hprev, u_b, conv_w, wr_blk, row(small["b_r"]), wi_blk, row(small["b_i"]),
        row(small["lru_lambda"]), hd, jobs=[chips(["w_ffn_gate", "w_ffn_up"]), swap(["w_o", "w_a_out", "w_b_out"])])
    add_pairs(["w_o", "w_a_out", "w_b_out"])
    du_a, lam_r, lam_i, dy16, gw["w_glu"], vec_sa, vec_gn = carry(
        _s5_bwd, dya, y, sr, si, u_a, full["w_glu"], row(small["b_glu"]), cre_blk, cimn_blk, bbr_blk, bbi_blk, ar_row,
        ai_row, d_row, jobs=[chips(["w_o", "w_a_out", "w_b_out"])])
    smalls = {"vec_tail": vec_tail, "vec_ffn": vec_ffn, "vec_lru": vec_lru, "vec_sa": vec_sa, "vec_gn": vec_gn,
              "dw_r": dw_r, "dw_i": dw_i, "loss": loss_blk}
    everyones = {}

    def gather_smalls(keys):
        return (_Gather([smalls[k] for k in keys]), keys, everyones) if distributed else None

    smalls["dbb_re"], smalls["dbb_im"], smalls["dc_re"], smalls["dc_imn"] = carry(
        _s5_param_grads, lam_r, lam_i, sr, si, u_a, dy16, pch, n, jobs=[swap(["w_glu"]), gather_smalls(list(smalls))])
    add_pairs(["w_glu"])
    dz = [du_a, du_b, dza, dzb]
    grad_x, h0, smalls["vec_mix"], smalls["vec_bin"] = _inproj_bwd(
        dz, x, dx1, row(small["g_mix"]), full["w_in"])
    shapes = {k: a.shape for k, a in smalls.items()}
    gw["w_in"] = carry(_dw_from_parts, dz, h0, tn_d, "dw_in",
                       jobs=[chips(["w_glu"]), gather_smalls(["dbb_re", "dbb_im", "dc_re", "dc_imn"])])
    smalls.update(everyones)
    if distributed:
        got["w_in"] = gw["w_in"]
        gw = got
    return grad_x, gw, smalls, shapes


def _disc_inputs(small):
    g, n, pch = small["s5_b_re"].shape
    srcs = (small["lam_re"], small["lam_im"], jnp.repeat(small["log_dt"], n))
    return {"rows": [a.reshape(1, g * n) for a in srcs], "cols": [a.reshape(g * n, 1) for a in srcs],
            "b_re": small["s5_b_re"].reshape(g * n, pch), "b_im": small["s5_b_im"].reshape(g * n, pch)}


def kernel(x, p, g_mix, w_in, b_in, lam_re, lam_im, log_dt, s5_b_re, s5_b_im, s5_c_re, s5_c_im, s5_d, w_glu, b_glu, conv_w, conv_b, w_r, b_r, w_i, b_i, lru_lambda, w_a_out, w_b_out, w_o, g_ffn, w_ffn_gate, w_ffn_up, w_ffn_down, g_ple_gate, w_ple_gate, b_ple_gate, w_ple, g_ple, g_final, loss_target, m_g_mix, m_w_in, m_b_in, m_lam_re, m_lam_im, m_log_dt, m_s5_b_re, m_s5_b_im, m_s5_c_re, m_s5_c_im, m_s5_d, m_w_glu, m_b_glu, m_conv_w, m_conv_b, m_w_r, m_b_r, m_w_i, m_b_i, m_lru_lambda, m_w_a_out, m_w_b_out, m_w_o, m_g_ffn, m_w_ffn_gate, m_w_ffn_up, m_w_ffn_down, m_g_ple_gate, m_w_ple_gate, m_b_ple_gate, m_w_ple, m_g_ple, m_g_final, v_g_mix, v_w_in, v_b_in, v_lam_re, v_lam_im, v_log_dt, v_s5_b_re, v_s5_b_im, v_s5_c_re, v_s5_c_im, v_s5_d, v_w_glu, v_b_glu, v_conv_w, v_conv_b, v_w_r, v_b_r, v_w_i, v_b_i, v_lru_lambda, v_w_a_out, v_w_b_out, v_w_o, v_g_ffn, v_w_ffn_gate, v_w_ffn_up, v_w_ffn_down, v_g_ple_gate, v_w_ple_gate, v_b_ple_gate, v_w_ple, v_g_ple, v_g_final):
    given = dict(locals())
    wts = {k: given[k] for k in WEIGHTS}
    moms = {k: given["m_" + k] for k in WEIGHTS}
    vels = {k: given["v_" + k] for k in WEIGHTS}

    def drop_depth(k, a):
        return a if k == "g_final" else a[0]

    small = {k: drop_depth(k, wts[k]) for k in SMALL}
    shard = {k: wts[k][0] for k in SHARDED}
    names = list(SHARDED)

    def wire(k):
        if k == "conv_w":
            return shard[k]
        return (shard[k].T if k in TRANSPOSED else shard[k]).astype(BF16)

    disc = _disc_inputs(small)
    grad_x, parts, smalls, shapes = _local_step(x[0], p[0, 0], loss_target[0], {k: wire(k) for k in names}, small, disc)

    late = ["vec_mix", "vec_bin"]
    received = _alone(_Both(_PairSwap([parts["w_in"]]), _Gather([smalls[k] for k in late])), "swap_last")
    (pair_in,) = _pair_add([parts["w_in"]], received[:1], "pair_add_w_in")
    smalls.update(zip(late, received[1:]))

    g_small, (parts["w_in"],) = _small_reduce(smalls, shapes, *disc["cols"], disc["b_re"], disc["b_im"],
                                              small["s5_b_re"].shape[0], comm=_ChipExchange([pair_in]))
    loss = g_small.pop("loss")[0, 0]
    cols = shard["conv_w"].shape[1]
    mine = _index((lax.axis_index("x"), lax.axis_index("y"), lax.axis_index("c")))
    parts["conv_w"] = lax.dynamic_slice_in_dim(g_small.pop("conv_w"), mine * cols, cols, axis=1)[None]

    def view(k, a):
        a = a.reshape((1, -1) if k == "g_final" else wts[k].shape)
        return jnp.swapaxes(a, -1, -2) if k in NARROW_LAST else a

    def unview(k, a):
        return (jnp.swapaxes(a, -1, -2) if k in NARROW_LAST else a).reshape(wts[k].shape)

    slots = _adamw_small([view(k, wts[k]) for k in SMALL], [view(k, g_small[k]) for k in SMALL],
                         [view(k, moms[k]) for k in SMALL], [view(k, vels[k]) for k in SMALL])
    small_out = [dict(zip(SMALL, [unview(k, a) for k, a in zip(SMALL, slot)])) for slot in slots]

    big_out = {}
    for group in ADAMW_GROUPS:
        flip = group[0] in LONG_AXIS_MINOR
        look = (lambda a: a.T) if flip else (lambda a: a)
        res = _adamw([parts[k] for k in group], [look(shard[k]) for k in group], [look(moms[k][0]) for k in group],
                     [look(vels[k][0]) for k in group], "adamw_" + group[0],
                     transposed=group[0] in TRANSPOSED and not flip)
        for k, outs4 in zip(group, res):
            big_out[k] = [look(a) for a in outs4]

    outs = [loss, grad_x[None]]
    for slot in range(4):
        for k in WEIGHTS:
            if k in SHARDED:
                outs.append(big_out[k][slot][None])
            else:
                outs.append(small_out[slot][k])
    return tuple(outs)
```

```python
import math

import jax
import jax.numpy as jnp
from jax import lax
from jax.experimental import pallas as pl
from jax.experimental.pallas import tpu as pltpu

F32 = jnp.float32
BF16 = jnp.bfloat16

EPS = 1e-6
LRU_C = 8.0
CONV_WIDTH = 4
ADAM_LR = 0.001
ADAM_B1 = 0.9
ADAM_B2 = 0.999
ADAM_EPS = 1e-08
ADAM_WD = 0.01
ADAM_STEP = 10

N_DEV = 8
MESH = pl.DeviceIdType.MESH
SUBLANES = 8
LANES = 128
VMEM_LIMIT = 56 * 1024 * 1024
TOKEN_TILE = 256
TIME_CHUNK = 256
S5_SLAB = 128
LRU_SLAB = 256


def _dot(a, b):
    return jnp.dot(a.astype(BF16), b.astype(BF16), preferred_element_type=F32)


def _dot_nt(a, b):
    return lax.dot_general(a.astype(BF16), b.astype(BF16), (((1,), (1,)), ((), ())), preferred_element_type=F32)


def _dot_tn(a, b):
    return lax.dot_general(a.astype(BF16), b.astype(BF16), (((0,), (0,)), ((), ())), preferred_element_type=F32)


def _sigmoid(x):
    return jax.nn.sigmoid(x)


def _rms_stats(x):
    r = lax.rsqrt(jnp.mean(x * x, axis=-1, keepdims=True) + EPS)
    return x * r, r


def _rms_bwd(dy, xhat, r, g):
    dxn = dy * g
    dx = r * (dxn - xhat * jnp.mean(dxn * xhat, axis=-1, keepdims=True))
    return dx, dy * xhat


def _rowsum(v):
    return jnp.sum(v, axis=0, keepdims=True)


def _expm1(x):
    u = jnp.exp(x)
    um1 = u - 1.0
    safe = jnp.where(um1 == 0.0, 1.0, jnp.log(u))
    return jnp.where(um1 == 0.0, x, um1 * x / safe)


def _softplus(x):
    e = jnp.exp(-jnp.abs(x))
    u = 1.0 + e
    um1 = u - 1.0
    safe = jnp.where(um1 == 0.0, 1.0, um1)
    log1p_e = jnp.where(um1 == 0.0, e, jnp.log(u) * e / safe)
    return jnp.maximum(x, 0.0) + log1p_e


_GELU_K = math.sqrt(2.0 / math.pi)
_GELU_C = 0.044715


def _gelu(x):
    return 0.5 * x * (1.0 + jnp.tanh(_GELU_K * (x + _GELU_C * x * x * x)))


def _gelu_grad(x):
    th = jnp.tanh(_GELU_K * (x + _GELU_C * x * x * x))
    return 0.5 * (1.0 + th) + 0.5 * x * (1.0 - th * th) * _GELU_K * (1.0 + 3.0 * _GELU_C * x * x)


def _params(*sem):
    return pltpu.CompilerParams(dimension_semantics=sem, vmem_limit_bytes=VMEM_LIMIT)


def _rows(tm, n):
    return pl.BlockSpec((tm, n), lambda i: (i, 0))


def _rows_rev(tm, n, steps):
    return pl.BlockSpec((tm, n), lambda i: (steps - 1 - i, 0))


def _whole(shape):
    nd = len(shape)
    return pl.BlockSpec(shape, lambda i: (0,) * nd, pipeline_mode=pl.Buffered(1))


def _acc(shape):
    nd = len(shape)
    return pl.BlockSpec(shape, lambda i: (0,) * nd)


def _zero_on_first(*refs):
    @pl.when(pl.program_id(0) == 0)
    def _():
        for r in refs:
            r[...] = jnp.zeros_like(r)


def _inproj_fwd(x, g_mix, w_in_t, b_in, widths, comm=None):
    t, d = x.shape
    n = w_in_t.shape[0]
    tm = min(TOKEN_TILE, t)
    offs = [sum(widths[:i]) for i in range(len(widths) + 1)]

    def body(x_ref, g_ref, w_ref, b_ref, *outs):
        xhat, _ = _rms_stats(x_ref[...])
        h = (xhat * g_ref[...]).astype(BF16)
        for k, o_ref in enumerate(outs):
            lo, hi = offs[k], offs[k + 1]
            o_ref[...] = _dot_nt(h, w_ref[lo:hi, :]) + b_ref[:, lo:hi]

    return _run(
        body, comm, name="inproj_fwd", grid=(t // tm,),
        out_shape=[jax.ShapeDtypeStruct((t, w), F32) for w in widths],
        in_specs=[_rows(tm, d), _whole((1, d)), _whole((n, d)), _whole((1, n))],
        out_specs=[_rows(tm, w) for w in widths],
        semantics="parallel",
    )(x, g_mix, w_in_t, b_in)


def _s5_fwd(u, bbr_blk, bbi_blk, ar, ai, cre_blk, cimn_blk, d_skip, w_glu, b_glu):
    t, sa = u.shape
    ns, _, sw = bbr_blk.shape
    gn = ns * sw
    tc = min(TIME_CHUNK, t)

    def body(u_ref, bbr_ref, bbi_ref, ar_ref, ai_ref, cre_ref, cim_ref, d_ref, wg_ref, bg_ref,
             sr_ref, si_ref, y_ref, ya_ref, cr_s, ci_s, sr_s, si_s):
        _zero_on_first(cr_s, ci_s)
        uv = u_ref[...]
        ub = uv.astype(BF16)
        for m in range(ns):
            um = ub[:, m * S5_SLAB:(m + 1) * S5_SLAB]
            sr_s[:, m * sw:(m + 1) * sw] = _dot(um, bbr_ref[m])
            si_s[:, m * sw:(m + 1) * sw] = _dot(um, bbi_ref[m])
        a_r = ar_ref[...]
        a_i = ai_ref[...]

        def step(row, carry):
            c_r, c_i = carry
            at = pl.ds(row, 1)
            n_r = a_r * c_r - a_i * c_i + sr_s[at, :]
            n_i = a_r * c_i + a_i * c_r + si_s[at, :]
            sr_s[at, :] = n_r
            si_s[at, :] = n_i
            return n_r, n_i

        c_r, c_i = lax.fori_loop(0, tc, step, (cr_s[0:1, :], ci_s[0:1, :]), unroll=8)
        cr_s[0:1, :] = c_r
        ci_s[0:1, :] = c_i
        for m in range(ns):
            states, chans = slice(m * sw, (m + 1) * sw), slice(m * S5_SLAB, (m + 1) * S5_SLAB)
            s_r, s_i = sr_s[:, states].astype(BF16), si_s[:, states].astype(BF16)
            sr_ref[:, states] = s_r.astype(sr_ref.dtype)
            si_ref[:, states] = s_i.astype(si_ref.dtype)
            y_ref[:, chans] = _dot(s_r, cre_ref[m]) + _dot(s_i, cim_ref[m]) + d_ref[:, chans] * uv[:, chans]
        y = y_ref[...]
        zz = _gelu(y)
        q = _dot(zz, wg_ref[...]) + bg_ref[...]
        ya_ref[...] = zz * _sigmoid(q)

    return dict(
        body=body, steps=t // tc, args=(u, bbr_blk, bbi_blk, ar, ai, cre_blk, cimn_blk, d_skip, w_glu, b_glu),
        out_shape=[jax.ShapeDtypeStruct((t, gn), BF16), jax.ShapeDtypeStruct((t, gn), BF16),
                   jax.ShapeDtypeStruct((t, sa), F32), jax.ShapeDtypeStruct((t, sa), F32)],
        in_specs=[_rows(tc, sa), _whole(bbr_blk.shape), _whole(bbi_blk.shape), _whole((1, gn)), _whole((1, gn)),
                  _whole(cre_blk.shape), _whole(cimn_blk.shape), _whole((1, sa)), _whole((sa, sa)), _whole((1, sa))],
        out_specs=[_rows(tc, gn), _rows(tc, gn), _rows(tc, sa), _rows(tc, sa)],
        scratch=[pltpu.VMEM((SUBLANES, gn), F32), pltpu.VMEM((SUBLANES, gn), F32),
                 pltpu.VMEM((tc, gn), F32), pltpu.VMEM((tc, gn), F32)])


def _stages(stages, name, comm=None):
    counts = [[len(s[k]) for s in stages] for k in ("args", "out_shape", "scratch")]

    def body(*refs):
        groups, pos = [], 0
        for kind in counts:
            per_stage = []
            for c in kind:
                per_stage.append(refs[pos:pos + c])
                pos += c
            groups.append(per_stage)
        for s, ins, outs, scratch in zip(stages, *groups):
            s["body"](*ins, *outs, *scratch)

    res = _run(
        body, comm, name=name, grid=(stages[0]["steps"],),
        out_shape=[o for s in stages for o in s["out_shape"]], in_specs=[i for s in stages for i in s["in_specs"]],
        out_specs=[o for s in stages for o in s["out_specs"]], scratch_shapes=[c for s in stages for c in s["scratch"]],
        semantics="arbitrary",
    )(*[a for s in stages for a in s["args"]])
    extra = None
    if comm is not None:
        res, extra = res
    per_stage, pos = [], 0
    for c in counts[1]:
        per_stage.append(list(res[pos:pos + c]))
        pos += c
    return per_stage if comm is None else (per_stage, extra)


def _slab_dot(x, w_ref, transposed=False):
    dot = _dot_nt if transposed else _dot
    xb = x.astype(BF16)
    return jnp.concatenate([dot(xb[:, j * LRU_SLAB:(j + 1) * LRU_SLAB], w_ref[j]) for j in range(w_ref.shape[0])], axis=1)


def _lru_gates(xc, wr_ref, br_ref, wi_ref, bi_ref, lam_ref):
    r = _sigmoid(_slab_dot(xc, wr_ref) + br_ref[...])
    ig = _sigmoid(_slab_dot(xc, wi_ref) + bi_ref[...])
    sp = _softplus(-lam_ref[...])
    log_a = (-LRU_C * r) * sp
    return r, ig, sp, log_a


def _lru_fwd(u, conv_w, conv_b, wr_blk, b_r, wi_blk, b_i, lru_lambda):
    t, w = u.shape
    tc = min(TIME_CHUNK, t)
    halo = SUBLANES

    def body(u_ref, cw_ref, cb_ref, wr_ref, br_ref, wi_ref, bi_ref, lam_ref,
             xc_ref, h_ref, hp_ref, ext_s, a_s, carry_s):
        @pl.when(pl.program_id(0) == 0)
        def _():
            ext_s[0:halo, :] = jnp.zeros((halo, w), F32)
            carry_s[...] = jnp.zeros_like(carry_s)

        ext_s[halo:halo + tc, :] = u_ref[...]
        xc = cb_ref[...]
        for k in range(CONV_WIDTH):
            off = halo - (CONV_WIDTH - 1) + k
            xc = xc + cw_ref[k:k + 1, :] * ext_s[off:off + tc, :]
        ext_s[0:halo, :] = ext_s[tc:tc + halo, :]
        xc_ref[...] = xc
        r, ig, sp, log_a = _lru_gates(xc, wr_ref, br_ref, wi_ref, bi_ref, lam_ref)
        a_s[...] = jnp.exp(log_a)
        h_ref[...] = jnp.sqrt(-_expm1(2.0 * log_a)) * ig * xc

        def step(row, carry):
            at = pl.ds(row, 1)
            hp_ref[at, :] = carry
            nxt = a_s[at, :] * carry + h_ref[at, :]
            h_ref[at, :] = nxt
            return nxt

        carry_s[0:1, :] = lax.fori_loop(0, tc, step, carry_s[0:1, :], unroll=8)

    return dict(
        body=body, steps=t // tc, args=(u, conv_w, conv_b, wr_blk, b_r, wi_blk, b_i, lru_lambda),
        out_shape=[jax.ShapeDtypeStruct((t, w), F32)] * 3,
        in_specs=[_rows(tc, w), _whole((CONV_WIDTH, w)), _whole((1, w)), _whole(wr_blk.shape), _whole((1, w)),
                  _whole(wi_blk.shape), _whole((1, w)), _whole((1, w))],
        out_specs=[_rows(tc, w)] * 3,
        scratch=[pltpu.VMEM((halo + tc, w), F32), pltpu.VMEM((tc, w), F32), pltpu.VMEM((SUBLANES, w), F32)])


def _merge_ffn_up_fwd(y_a, h, za, zb, x, w_a_out_t, w_b_out, w_o, g_ffn, w_gate_t, w_up_t, comm=None):
    t, d = x.shape
    sa, lw = y_a.shape[1], h.shape[1]
    f = w_gate_t.shape[0]
    tm = min(TOKEN_TILE, t)

    def body(ya_ref, h_ref, za_ref, zb_ref, x_ref, wa_ref, wb_ref, wo_ref, g_ref, wg_ref, wu_ref,
             x1_ref, mg_ref, ma_ref, mb_ref, fg_ref, fu_ref):
        ma = _dot_nt(ya_ref[...], wa_ref[...])
        mb = _dot(h_ref[...], wb_ref[...])
        merged = _sigmoid(za_ref[...]) * ma + _sigmoid(zb_ref[...]) * mb
        ma_ref[...] = ma.astype(ma_ref.dtype)
        mb_ref[...] = mb.astype(mb_ref.dtype)
        mg_ref[...] = merged.astype(mg_ref.dtype)
        x1 = x_ref[...] + _dot(merged, wo_ref[...])
        x1_ref[...] = x1
        xhat, _ = _rms_stats(x1)
        h2 = (xhat * g_ref[...]).astype(BF16)
        fg_ref[...] = _dot_nt(h2, wg_ref[...]).astype(fg_ref.dtype)
        fu_ref[...] = _dot_nt(h2, wu_ref[...]).astype(fu_ref.dtype)

    return _run(
        body, comm, name="merge_ffn_up_fwd", grid=(t // tm,),
        out_shape=[jax.ShapeDtypeStruct((t, d), F32), jax.ShapeDtypeStruct((t, d), BF16),
                   jax.ShapeDtypeStruct((t, d), BF16), jax.ShapeDtypeStruct((t, d), BF16),
                   jax.ShapeDtypeStruct((t, f), BF16), jax.ShapeDtypeStruct((t, f), BF16)],
        in_specs=[_rows(tm, sa), _rows(tm, lw), _rows(tm, d), _rows(tm, d), _rows(tm, d),
                  _whole((d, sa)), _whole((lw, d)), _whole((d, d)), _whole((1, d)), _whole((f, d)), _whole((f, d))],
        out_specs=[_rows(tm, d)] * 4 + [_rows(tm, f)] * 2,
        semantics="parallel",
    )(y_a, h, za, zb, x, w_a_out_t, w_b_out, w_o, g_ffn, w_gate_t, w_up_t)


def _ffn_down_fwd(fg, fu, x1, w_down, comm=None):
    t, d = x1.shape
    f = fg.shape[1]
    tm = min(TOKEN_TILE, t)

    def body(fg_ref, fu_ref, x_ref, wd_ref, x2_ref):
        fgv = fg_ref[...].astype(F32)
        act = fgv * _sigmoid(fgv) * fu_ref[...].astype(F32)
        x2_ref[...] = x_ref[...] + _dot(act, wd_ref[...])

    return _run(
        body, comm, name="ffn_down_fwd", grid=(t // tm,),
        out_shape=jax.ShapeDtypeStruct((t, d), F32),
        in_specs=[_rows(tm, f), _rows(tm, f), _rows(tm, d), _whole((f, d))],
        out_specs=_rows(tm, d),
        semantics="parallel",
    )(fg, fu, x1, w_down)


def _store_on_last(pairs):
    @pl.when(pl.program_id(0) == pl.num_programs(0) - 1)
    def _():
        for acc, out in pairs:
            out[...] = acc[...].astype(out.dtype)


def _tail_fwd_bwd(x2, p, target, g_pg, w_pg, b_pg, w_ple_t, g_ple, g_final):
    t, d = x2.shape
    pd = p.shape[1]
    tm = min(TOKEN_TILE, t)

    def body(x2_ref, p_ref, tg_ref, gpg_ref, wpg_ref, bpg_ref, wple_ref, gple_ref, gfin_ref,
             dx2_ref, loss_ref, dwpg_out, dwple_out, vec_ref, dwpg_ref, dwple_ref):
        _zero_on_first(loss_ref, dwpg_ref, dwple_ref, vec_ref)
        x2v = x2_ref[...]
        xh2, r2 = _rms_stats(x2v)
        h3 = xh2 * gpg_ref[...]
        gp = _sigmoid(_dot(h3, wpg_ref[...]) + bpg_ref[...])
        pe = _dot_nt(p_ref[...], wple_ref[...])
        peh, r3 = _rms_stats(pe)
        e = peh * gple_ref[...]
        x3 = x2v + gp * e
        xh3, r4 = _rms_stats(x3)
        diff = xh3 * gfin_ref[...] - tg_ref[...]
        loss_ref[...] += 0.5 * jnp.sum(jnp.mean(diff * diff, axis=-1, keepdims=True))
        dy = diff * (1.0 / d)
        dx3, dgfin = _rms_bwd(dy, xh3, r4, gfin_ref[...])
        d_gp = dx3 * e
        d_e = dx3 * gp
        dpe, dgple = _rms_bwd(d_e, peh, r3, gple_ref[...])
        dwple_ref[...] += _dot_tn(dpe, p_ref[...])
        dpre = d_gp * gp * (1.0 - gp)
        dwpg_ref[...] += _dot_tn(h3, dpre)
        dh3 = _dot_nt(dpre, wpg_ref[...])
        dx2n, dgpg = _rms_bwd(dh3, xh2, r2, gpg_ref[...])
        dx2_ref[...] = dx3 + dx2n
        vec_ref[0:1, :] += _rowsum(dpre)
        vec_ref[1:2, :] += _rowsum(dgpg)
        vec_ref[2:3, :] += _rowsum(dgple)
        vec_ref[3:4, :] += _rowsum(dgfin)
        _store_on_last([(dwpg_ref, dwpg_out), (dwple_ref, dwple_out)])

    return pl.pallas_call(
        body, name="tail_fwd_bwd", grid=(t // tm,),
        out_shape=[jax.ShapeDtypeStruct((t, d), F32), jax.ShapeDtypeStruct((SUBLANES, LANES), F32),
                   jax.ShapeDtypeStruct((d, d), BF16), jax.ShapeDtypeStruct((d, pd), BF16),
                   jax.ShapeDtypeStruct((SUBLANES, d), F32)],
        in_specs=[_rows(tm, d), _rows(tm, pd), _rows(tm, d), _whole((1, d)), _whole((d, d)), _whole((1, d)),
                  _whole((d, pd)), _whole((1, d)), _whole((1, d))],
        out_specs=[_rows(tm, d), _acc((SUBLANES, LANES)), _acc((d, d)), _acc((d, pd)), _acc((SUBLANES, d))],
        scratch_shapes=[pltpu.VMEM((d, d), F32), pltpu.VMEM((d, pd), F32)],
        compiler_params=_params("arbitrary"),
    )(x2, p, target, g_pg, w_pg, b_pg, w_ple_t, g_ple, g_final)


def _ffn_bwd_a(dx2, fg, fu, w_down, comm=None):
    t, d = dx2.shape
    f = fg.shape[1]
    tm = min(TOKEN_TILE, t)

    def body(dx_ref, fg_ref, fu_ref, wd_ref, dfg_ref, dfu_ref, act_ref):
        dact = _dot_nt(dx_ref[...], wd_ref[...])
        fgv = fg_ref[...].astype(F32)
        fuv = fu_ref[...].astype(F32)
        sg = _sigmoid(fgv)
        silu = fgv * sg
        dfu_ref[...] = (dact * silu).astype(dfu_ref.dtype)
        dfg_ref[...] = (dact * fuv * (sg * (1.0 + fgv * (1.0 - sg)))).astype(dfg_ref.dtype)
        act_ref[...] = (silu * fuv).astype(act_ref.dtype)

    return _run(
        body, comm, name="ffn_bwd_a", grid=(t // tm,),
        out_shape=[jax.ShapeDtypeStruct((t, f), BF16)] * 3,
        in_specs=[_rows(tm, d), _rows(tm, f), _rows(tm, f), _whole((f, d))],
        out_specs=[_rows(tm, f)] * 3,
        semantics="parallel",
    )(dx2, fg, fu, w_down)


def _ffn_bwd_b(dfg, dfu, x1, dx2, g_ffn, w_gate_t, w_up_t, comm=None):
    t, d = x1.shape
    f = dfg.shape[1]
    tm = min(TOKEN_TILE, t)

    def body(dfg_ref, dfu_ref, x_ref, dx2_ref, g_ref, wg_ref, wu_ref, dx1_ref, h2_ref, vec_ref):
        _zero_on_first(vec_ref)
        dh2 = _dot(dfg_ref[...], wg_ref[...]) + _dot(dfu_ref[...], wu_ref[...])
        xhat, r = _rms_stats(x_ref[...])
        h2_ref[...] = (xhat * g_ref[...]).astype(h2_ref.dtype)
        dxn, dg = _rms_bwd(dh2, xhat, r, g_ref[...])
        dx1_ref[...] = dx2_ref[...] + dxn
        vec_ref[0:1, :] += _rowsum(dg)

    return _run(
        body, comm, name="ffn_bwd_b", grid=(t // tm,),
        out_shape=[jax.ShapeDtypeStruct((t, d), F32), jax.ShapeDtypeStruct((t, d), BF16),
                   jax.ShapeDtypeStruct((SUBLANES, d), F32)],
        in_specs=[_rows(tm, f), _rows(tm, f), _rows(tm, d), _rows(tm, d), _whole((1, d)), _whole((f, d)), _whole((f, d))],
        out_specs=[_rows(tm, d), _rows(tm, d), _acc((SUBLANES, d))],
        semantics="arbitrary",
    )(dfg, dfu, x1, dx2, g_ffn, w_gate_t, w_up_t)


def _matmul_tn(a, b, tn, name, dtype=F32, comm=None):
    t, k = a.shape
    n = b.shape[1]

    def body(a_ref, b_ref, o_ref):
        o_ref[...] = _dot_tn(a_ref[...], b_ref[...]).astype(o_ref.dtype)

    return _run(
        body, comm, name=name, grid=(n // tn,),
        out_shape=jax.ShapeDtypeStruct((k, n), dtype),
        in_specs=[_whole((t, k)), pl.BlockSpec((t, tn), lambda j: (0, j))],
        out_specs=pl.BlockSpec((k, tn), lambda j: (0, j)),
        semantics="parallel",
    )(a, b)


def _merge_bwd(dx1, merged, ma, mb, za, zb, y_a, h, w_o, w_a_out_t, w_b_out, comm=None):
    t, d = dx1.shape
    sa, lw = y_a.shape[1], h.shape[1]
    tm = min(TOKEN_TILE, t)

    def body(dx1_ref, mg_ref, ma_ref, mb_ref, za_ref, zb_ref, ya_ref, h_ref, wo_ref, wa_ref, wb_ref,
             dza_ref, dzb_ref, dya_ref, dyb_ref, dwo_out, dwa_out, dwb_out, dwo_ref, dwa_ref, dwb_ref):
        _zero_on_first(dwo_ref, dwa_ref, dwb_ref)
        dx1v = dx1_ref[...].astype(BF16)
        dmg = _dot_nt(dx1v, wo_ref[...])
        ga = _sigmoid(za_ref[...])
        gb = _sigmoid(zb_ref[...])
        dza_ref[...] = (dmg * ma_ref[...].astype(F32) * ga * (1.0 - ga)).astype(dza_ref.dtype)
        dzb_ref[...] = (dmg * mb_ref[...].astype(F32) * gb * (1.0 - gb)).astype(dzb_ref.dtype)
        dma = (dmg * ga).astype(BF16)
        dmb = (dmg * gb).astype(BF16)
        dya_ref[...] = _dot(dma, wa_ref[...])
        dyb_ref[...] = _dot_nt(dmb, wb_ref[...])
        dwo_ref[...] += _dot_tn(mg_ref[...], dx1v)
        dwa_ref[...] += _dot_tn(dma, ya_ref[...])
        dwb_ref[...] += _dot_tn(h_ref[...], dmb)
        _store_on_last([(dwo_ref, dwo_out), (dwa_ref, dwa_out), (dwb_ref, dwb_out)])

    return _run(
        body, comm, name="merge_bwd", grid=(t // tm,),
        out_shape=[jax.ShapeDtypeStruct((t, d), BF16), jax.ShapeDtypeStruct((t, d), BF16),
                   jax.ShapeDtypeStruct((t, sa), F32), jax.ShapeDtypeStruct((t, lw), F32),
                   jax.ShapeDtypeStruct((d, d), BF16), jax.ShapeDtypeStruct((d, sa), BF16),
                   jax.ShapeDtypeStruct((lw, d), BF16)],
        in_specs=[_rows(tm, d), _rows(tm, d), _rows(tm, d), _rows(tm, d), _rows(tm, d), _rows(tm, d),
                  _rows(tm, sa), _rows(tm, lw), _whole((d, d)), _whole((d, sa)), _whole((lw, d))],
        out_specs=[_rows(tm, d), _rows(tm, d), _rows(tm, sa), _rows(tm, lw), _acc((d, d)), _acc((d, sa)), _acc((lw, d))],
        scratch_shapes=[pltpu.VMEM((d, d), F32), pltpu.VMEM((d, sa), F32), pltpu.VMEM((lw, d), F32)],
        semantics="arbitrary",
    )(dx1, merged, ma, mb, za, zb, y_a, h, w_o, w_a_out_t, w_b_out)


def _fold_diag_blocks(dense, row_group, col_group, row0=0, col0=0):
    r, c = dense.shape
    rows = lax.broadcasted_iota(jnp.int32, (r, c), 0) + row0
    cols = lax.broadcasted_iota(jnp.int32, (r, c), 1) + col0
    kept = jnp.where(rows // row_group == cols // col_group, dense, 0.0)
    pick = (lax.broadcasted_iota(jnp.int32, (row_group, r), 0)
            == lax.broadcasted_iota(jnp.int32, (row_group, r), 1) % row_group).astype(F32)
    return jnp.dot(pick, kept, preferred_element_type=F32, precision=lax.Precision.HIGHEST)


def _lru_bwd(dh, xc, hprev, u, conv_w, wr_blk, b_r, wi_blk, b_i, lru_lambda, head_dim, comm=None):
    t, w = dh.shape
    tc = min(TIME_CHUNK, t)
    steps = t // tc
    halo = SUBLANES
    sub_per_chunk = tc // halo
    slabs = w // LRU_SLAB

    def body(dh_ref, xc_ref, hp_ref, u_ref, uh_ref, cw_ref, wr_ref, br_ref, wi_ref, bi_ref, lam_ref,
             du_ref, dwr_out, dwi_out, vec_ref, lam_s, a_s, dxc_s, uext_s, carry_s, dwr_ref, dwi_ref):
        chunk = steps - 1 - pl.program_id(0)

        @pl.when(pl.program_id(0) == 0)
        def _():
            carry_s[...] = jnp.zeros_like(carry_s)
            dxc_s[tc:tc + halo, :] = jnp.zeros((halo, w), F32)
            dwr_ref[...] = jnp.zeros_like(dwr_ref)
            dwi_ref[...] = jnp.zeros_like(dwi_ref)
            vec_ref[...] = jnp.zeros_like(vec_ref)

        xc = xc_ref[...]
        r, ig, sp, log_a = _lru_gates(xc, wr_ref, br_ref, wi_ref, bi_ref, lam_ref)
        a = jnp.exp(log_a)
        a_s[...] = a

        def step(i, q):
            at = pl.ds(tc - 1 - i, 1)
            lam_row = dh_ref[at, :] + q
            lam_s[at, :] = lam_row
            return a_s[at, :] * lam_row

        carry_s[0:1, :] = lax.fori_loop(0, tc, step, carry_s[0:1, :], unroll=8)
        lam = lam_s[...]
        mult = jnp.sqrt(-_expm1(2.0 * log_a))
        d_log_a = lam * hp_ref[...] * a - (lam * ig * xc) * (a * a) / mult
        d_ig = lam * mult * xc
        dpre_r = (d_log_a * (-LRU_C * sp)) * r * (1.0 - r)
        dpre_i = d_ig * ig * (1.0 - ig)
        dxc = lam * mult * ig + _slab_dot(dpre_r, wr_ref, transposed=True) + _slab_dot(dpre_i, wi_ref, transposed=True)
        xcb, drb, dib = xc.astype(BF16), dpre_r.astype(BF16), dpre_i.astype(BF16)
        for j in range(slabs):
            cols = slice(j * LRU_SLAB, (j + 1) * LRU_SLAB)
            dwr_ref[j] += _dot_tn(drb[:, cols], xcb[:, cols])
            dwi_ref[j] += _dot_tn(dib[:, cols], xcb[:, cols])
        vec_ref[0:1, :] += _rowsum(dxc)
        vec_ref[1:2, :] += _rowsum(dpre_r)
        vec_ref[2:3, :] += _rowsum(dpre_i)
        vec_ref[3:4, :] += _rowsum(d_log_a * (-LRU_C * r)) * (-_sigmoid(-lam_ref[...]))
        dxc_s[0:tc, :] = dxc
        du = cw_ref[CONV_WIDTH - 1:CONV_WIDTH, :] * dxc
        for k in range(CONV_WIDTH - 1):
            off = CONV_WIDTH - 1 - k
            du = du + cw_ref[k:k + 1, :] * dxc_s[off:off + tc, :]
        du_ref[...] = du.astype(du_ref.dtype)
        dxc_s[tc:tc + halo, :] = dxc_s[0:halo, :]
        uext_s[0:halo, :] = jnp.where(chunk > 0, uh_ref[...], 0.0)
        uext_s[halo:halo + tc, :] = u_ref[...]
        for k in range(CONV_WIDTH):
            off = halo - (CONV_WIDTH - 1) + k
            vec_ref[4 + k:5 + k, :] += _rowsum(dxc * uext_s[off:off + tc, :])

        @pl.when(pl.program_id(0) == steps - 1)
        def _():
            for j in range(slabs):
                cols = slice(j * LRU_SLAB, (j + 1) * LRU_SLAB)
                dwr_out[:, cols] = _fold_diag_blocks(dwr_ref[j], head_dim, head_dim).astype(dwr_out.dtype)
                dwi_out[:, cols] = _fold_diag_blocks(dwi_ref[j], head_dim, head_dim).astype(dwi_out.dtype)

    halo_spec = pl.BlockSpec((halo, w), lambda i: (jnp.maximum((steps - 1 - i) * sub_per_chunk - 1, 0), 0))
    return _run(
        body, comm, name="lru_bwd", grid=(steps,),
        out_shape=[jax.ShapeDtypeStruct((t, w), BF16), jax.ShapeDtypeStruct((head_dim, w), BF16),
                   jax.ShapeDtypeStruct((head_dim, w), BF16), jax.ShapeDtypeStruct((SUBLANES, w), F32)],
        in_specs=[_rows_rev(tc, w, steps)] * 4 + [halo_spec, _whole((CONV_WIDTH, w)), _whole(wr_blk.shape),
                                                  _whole((1, w)), _whole(wi_blk.shape), _whole((1, w)), _whole((1, w))],
        out_specs=[_rows_rev(tc, w, steps), _acc((head_dim, w)), _acc((head_dim, w)), _acc((SUBLANES, w))],
        scratch_shapes=[pltpu.VMEM((tc, w), F32), pltpu.VMEM((tc, w), F32), pltpu.VMEM((tc + halo, w), F32),
                        pltpu.VMEM((halo + tc, w), F32), pltpu.VMEM((SUBLANES, w), F32),
                        pltpu.VMEM((slabs, LRU_SLAB, LRU_SLAB), F32), pltpu.VMEM((slabs, LRU_SLAB, LRU_SLAB), F32)],
        semantics="arbitrary",
    )(dh, xc, hprev, u, u, conv_w, wr_blk, b_r, wi_blk, b_i, lru_lambda)


def _s5_bwd(dya, y, sr, si, u, w_glu, b_glu, cre_blk, cimn_blk, bbr_blk, bbi_blk, ar, ai, d_skip, comm=None):
    t, sa = dya.shape
    gn = sr.shape[1]
    ns, _, sw = bbr_blk.shape
    tc = min(TIME_CHUNK, t)
    steps = t // tc
    halo = SUBLANES

    def body(dya_ref, y_ref, sr_ref, si_ref, u_ref, wg_ref, bg_ref, cre_ref, cim_ref, bbr_ref, bbi_ref,
             ar_ref, ai_ref, d_ref, du_ref, lr_ref, li_ref, dy_ref, dwg_out, vsa_ref, vgn_ref, gr_s, gi_s, cr_s, ci_s,
             dwg_ref):
        @pl.when(pl.program_id(0) == 0)
        def _():
            cr_s[...] = jnp.zeros_like(cr_s)
            ci_s[...] = jnp.zeros_like(ci_s)
            gr_s[tc:tc + halo, :] = jnp.zeros((halo, gn), F32)
            gi_s[tc:tc + halo, :] = jnp.zeros((halo, gn), F32)
            dwg_ref[...] = jnp.zeros_like(dwg_ref)
            vsa_ref[...] = jnp.zeros_like(vsa_ref)
            vgn_ref[...] = jnp.zeros_like(vgn_ref)

        yv = y_ref[...]
        uv = u_ref[...]
        zz = _gelu(yv)
        sg = _sigmoid(_dot(zz, wg_ref[...]) + bg_ref[...])
        dyav = dya_ref[...]
        dq = dyav * zz * sg * (1.0 - sg)
        dzz = dyav * sg + _dot_nt(dq, wg_ref[...])
        dwg_ref[...] += _dot_tn(zz, dq)
        dy = dzz * _gelu_grad(yv)
        dyb = dy.astype(BF16)
        dy_ref[...] = dyb.astype(dy_ref.dtype)
        vsa_ref[0:1, :] += _rowsum(dq)
        vsa_ref[1:2, :] += _rowsum(dy * uv)
        for m in range(ns):
            dym = dyb[:, m * S5_SLAB:(m + 1) * S5_SLAB]
            gr_s[0:tc, m * sw:(m + 1) * sw] = _dot_nt(dym, cre_ref[m])
            gi_s[0:tc, m * sw:(m + 1) * sw] = _dot_nt(dym, cim_ref[m])
        a_r = ar_ref[...]
        a_i = ai_ref[...]

        def step(i, carry):
            l_r, l_i = carry
            at = pl.ds(tc - 1 - i, 1)
            n_r = gr_s[at, :] + a_r * l_r + a_i * l_i
            n_i = gi_s[at, :] + a_r * l_i - a_i * l_r
            gr_s[at, :] = n_r
            gi_s[at, :] = n_i
            return n_r, n_i

        l_r, l_i = lax.fori_loop(0, tc, step, (cr_s[0:1, :], ci_s[0:1, :]), unroll=8)
        cr_s[0:1, :] = l_r
        ci_s[0:1, :] = l_i
        nxt_r = gr_s[1:tc + 1, :]
        nxt_i = gi_s[1:tc + 1, :]
        srv = sr_ref[...].astype(F32)
        siv = si_ref[...].astype(F32)
        vgn_ref[0:1, :] += _rowsum(nxt_r * srv + nxt_i * siv)
        vgn_ref[1:2, :] += _rowsum(nxt_i * srv - nxt_r * siv)
        lam_r = gr_s[0:tc, :]
        lam_i = gi_s[0:tc, :]
        gr_s[tc:tc + halo, :] = gr_s[0:halo, :]
        gi_s[tc:tc + halo, :] = gi_s[0:halo, :]
        lrb = lam_r.astype(BF16)
        lib = lam_i.astype(BF16)
        lr_ref[...] = lrb.astype(lr_ref.dtype)
        li_ref[...] = lib.astype(li_ref.dtype)
        for m in range(ns):
            states, chans = slice(m * sw, (m + 1) * sw), slice(m * S5_SLAB, (m + 1) * S5_SLAB)
            du_ref[:, chans] = (_dot_nt(lrb[:, states], bbr_ref[m]) + _dot_nt(lib[:, states], bbi_ref[m])
                                + dy[:, chans] * d_ref[:, chans]).astype(du_ref.dtype)
        _store_on_last([(dwg_ref, dwg_out)])

    return _run(
        body, comm, name="s5_bwd", grid=(steps,),
        out_shape=[jax.ShapeDtypeStruct((t, sa), BF16), jax.ShapeDtypeStruct((t, gn), BF16),
                   jax.ShapeDtypeStruct((t, gn), BF16), jax.ShapeDtypeStruct((t, sa), BF16),
                   jax.ShapeDtypeStruct((sa, sa), BF16), jax.ShapeDtypeStruct((SUBLANES, sa), F32),
                   jax.ShapeDtypeStruct((SUBLANES, gn), F32)],
        in_specs=[_rows_rev(tc, sa, steps), _rows_rev(tc, sa, steps), _rows_rev(tc, gn, steps), _rows_rev(tc, gn, steps),
                  _rows_rev(tc, sa, steps), _whole((sa, sa)), _whole((1, sa)), _whole(cre_blk.shape),
                  _whole(cimn_blk.shape), _whole(bbr_blk.shape), _whole(bbi_blk.shape), _whole((1, gn)), _whole((1, gn)),
                  _whole((1, sa))],
        out_specs=[_rows_rev(tc, sa, steps), _rows_rev(tc, gn, steps), _rows_rev(tc, gn, steps), _rows_rev(tc, sa, steps),
                   _acc((sa, sa)), _acc((SUBLANES, sa)), _acc((SUBLANES, gn))],
        scratch_shapes=[pltpu.VMEM((tc + halo, gn), F32), pltpu.VMEM((tc + halo, gn), F32),
                        pltpu.VMEM((SUBLANES, gn), F32), pltpu.VMEM((SUBLANES, gn), F32), pltpu.VMEM((sa, sa), F32)],
        semantics="arbitrary",
    )(dya, y, sr, si, u, w_glu, b_glu, cre_blk, cimn_blk, bbr_blk, bbi_blk, ar, ai, d_skip)


def _inproj_bwd(dparts, x, dx1, g_mix, w_in_t, comm=None):
    t, d = x.shape
    n = w_in_t.shape[0]
    widths = [p.shape[1] for p in dparts]
    offs = [sum(widths[:i]) for i in range(len(widths) + 1)]
    tm = min(TOKEN_TILE, t)
    np_ = len(dparts)

    def body(*refs):
        dz_refs = refs[:np_]
        x_ref, dx1_ref, g_ref, w_ref, gx_ref, h_ref, vd_ref, vn_ref = refs[np_:]
        _zero_on_first(vd_ref, vn_ref)
        dh = jnp.zeros((tm, d), F32)
        for k, r in enumerate(dz_refs):
            lo, hi = offs[k], offs[k + 1]
            dzk = r[...]
            dh = dh + _dot(dzk, w_ref[lo:hi, :])
            vn_ref[0:1, lo:hi] += _rowsum(dzk.astype(F32))
        xhat, r0 = _rms_stats(x_ref[...])
        h_ref[...] = (xhat * g_ref[...]).astype(h_ref.dtype)
        dxn, dg = _rms_bwd(dh, xhat, r0, g_ref[...])
        gx_ref[...] = dx1_ref[...] + dxn
        vd_ref[0:1, :] += _rowsum(dg)

    return _run(
        body, comm, name="inproj_bwd", grid=(t // tm,),
        out_shape=[jax.ShapeDtypeStruct((t, d), F32), jax.ShapeDtypeStruct((t, d), BF16),
                   jax.ShapeDtypeStruct((SUBLANES, d), F32), jax.ShapeDtypeStruct((SUBLANES, n), F32)],
        in_specs=[_rows(tm, w) for w in widths] + [_rows(tm, d), _rows(tm, d), _whole((1, d)), _whole((n, d))],
        out_specs=[_rows(tm, d), _rows(tm, d), _acc((SUBLANES, d)), _acc((SUBLANES, n))],
        semantics="arbitrary",
    )(*dparts, x, dx1, g_mix, w_in_t)


def _dw_from_parts(dparts, h, tn, name, comm=None):
    t, d = h.shape
    widths = [p.shape[1] for p in dparts]
    offs = [sum(widths[:i]) for i in range(len(widths) + 1)]
    np_ = len(dparts)

    def body(*refs):
        h_ref, o_ref = refs[np_], refs[np_ + 1]
        hv = h_ref[...]
        for k, r in enumerate(refs[:np_]):
            o_ref[offs[k]:offs[k + 1], :] = _dot_tn(r[...], hv).astype(o_ref.dtype)

    return _run(
        body, comm, name=name, grid=(d // tn,),
        out_shape=jax.ShapeDtypeStruct((offs[-1], d), BF16),
        in_specs=[_whole(p.shape) for p in dparts] + [pl.BlockSpec((t, tn), lambda j: (0, j))],
        out_specs=pl.BlockSpec((offs[-1], tn), lambda j: (0, j)),
        semantics="parallel",
    )(*dparts, h)


def _prep(lr_row, li_row, ldt_row, lr_col, li_col, ldt_col, b_re, b_im, c_re, c_im, w_r, w_i, comm=None):
    gn, pch = b_re.shape
    sa, n = c_re.shape
    w, hd = w_r.shape
    ns, sw, lsl = sa // S5_SLAB, S5_SLAB * n // pch, w // LRU_SLAB

    def spread_cols(vals, row_group, col_group, width):
        r, k = vals.shape
        tile = (lax.broadcasted_iota(jnp.int32, (k, width), 0) == lax.broadcasted_iota(jnp.int32, (k, width), 1) % k)
        rows = lax.broadcasted_iota(jnp.int32, (r, width), 0) // row_group
        cols = lax.broadcasted_iota(jnp.int32, (r, width), 1) // col_group
        return jnp.where(rows == cols, _dot(vals, tile.astype(BF16)), 0.0)

    def spread_rows(vals, row_group, col_group, height):
        k, c = vals.shape
        tile = (lax.broadcasted_iota(jnp.int32, (height, k), 0) % k == lax.broadcasted_iota(jnp.int32, (height, k), 1))
        rows = lax.broadcasted_iota(jnp.int32, (height, c), 0) // row_group
        cols = lax.broadcasted_iota(jnp.int32, (height, c), 1) // col_group
        return jnp.where(rows == cols, _dot(tile.astype(BF16), vals), 0.0)

    def body(lrr, lir, ldr, lrc, lic, ldc, bre, bim, cre, cim, wr, wi,
             ar_o, ai_o, bbr_o, bbi_o, cre_o, cim_o, wr_o, wi_o):
        ar, ai, _, _ = _disc_scalars(lrr[...], lir[...], ldr[...])
        ar_o[...] = ar
        ai_o[...] = ai
        _, _, bbr, bbi = _disc_cols(lrc[...], lic[...], ldc[...], bre[...], bim[...])
        bbr_t, bbi_t = bbr.T, bbi.T
        for m in range(ns):
            bbr_o[m] = spread_rows(bbr_t[:, m * sw:(m + 1) * sw], pch, n, S5_SLAB).astype(bbr_o.dtype)
            bbi_o[m] = spread_rows(bbi_t[:, m * sw:(m + 1) * sw], pch, n, S5_SLAB).astype(bbi_o.dtype)
            rows = slice(m * S5_SLAB, (m + 1) * S5_SLAB)
            cre_o[m] = spread_rows(cre[rows, :].T, n, pch, sw).astype(cre_o.dtype)
            cim_o[m] = spread_rows(-cim[rows, :].T, n, pch, sw).astype(cim_o.dtype)
        for j in range(lsl):
            rows = slice(j * LRU_SLAB, (j + 1) * LRU_SLAB)
            wr_o[j] = spread_cols(wr[rows, :], hd, hd, LRU_SLAB).astype(wr_o.dtype)
            wi_o[j] = spread_cols(wi[rows, :], hd, hd, LRU_SLAB).astype(wi_o.dtype)

    args = (lr_row, li_row, ldt_row, lr_col, li_col, ldt_col, b_re, b_im, c_re, c_im, w_r, w_i)
    out_shape = [jax.ShapeDtypeStruct((1, gn), F32), jax.ShapeDtypeStruct((1, gn), F32),
                 jax.ShapeDtypeStruct((ns, S5_SLAB, sw), BF16), jax.ShapeDtypeStruct((ns, S5_SLAB, sw), BF16),
                 jax.ShapeDtypeStruct((ns, sw, S5_SLAB), BF16), jax.ShapeDtypeStruct((ns, sw, S5_SLAB), BF16),
                 jax.ShapeDtypeStruct((lsl, LRU_SLAB, LRU_SLAB), BF16),
                 jax.ShapeDtypeStruct((lsl, LRU_SLAB, LRU_SLAB), BF16)]
    return _run(
        body, comm, name="prep", grid=(1,), out_shape=out_shape, in_specs=[_whole(a.shape) for a in args],
        out_specs=[_acc(s.shape) for s in out_shape], semantics="arbitrary",
    )(*args)


def _s5_param_grads(lam_r, lam_i, sr, si, u, dy, pch, n, comm=None):
    t, gn = lam_r.shape
    sa = u.shape[1]
    sw = S5_SLAB * n // pch

    def body(lr_ref, li_ref, sr_ref, si_ref, u_ref, dy_ref, dbr_ref, dbi_ref, dcr_ref, dci_ref):
        uv = u_ref[...]
        dyv = dy_ref[...]
        dbr_ref[...] = _fold_diag_blocks(_dot_tn(uv, lr_ref[...]), pch, n).astype(dbr_ref.dtype)
        dbi_ref[...] = _fold_diag_blocks(_dot_tn(uv, li_ref[...]), pch, n).astype(dbi_ref.dtype)
        dcr_ref[...] = _fold_diag_blocks(_dot_tn(sr_ref[...], dyv), n, pch).astype(dcr_ref.dtype)
        dci_ref[...] = _fold_diag_blocks(_dot_tn(si_ref[...], dyv), n, pch).astype(dci_ref.dtype)

    states = pl.BlockSpec((t, sw), lambda m: (0, m))
    chans = pl.BlockSpec((t, S5_SLAB), lambda m: (0, m))
    return _run(
        body, comm, name="s5_param_grads", grid=(sa // S5_SLAB,),
        out_shape=[jax.ShapeDtypeStruct((pch, gn), BF16), jax.ShapeDtypeStruct((pch, gn), BF16),
                   jax.ShapeDtypeStruct((n, sa), BF16), jax.ShapeDtypeStruct((n, sa), BF16)],
        in_specs=[states, states, states, states, chans, chans],
        out_specs=[pl.BlockSpec((pch, sw), lambda m: (0, m)), pl.BlockSpec((pch, sw), lambda m: (0, m)),
                   pl.BlockSpec((n, S5_SLAB), lambda m: (0, m)), pl.BlockSpec((n, S5_SLAB), lambda m: (0, m))],
        semantics="parallel",
    )(lam_r, lam_i, sr, si, u, dy)


SMALL_PARTS = ["vec_tail", "vec_ffn", "vec_lru", "vec_mix", "vec_bin", "vec_sa", "vec_gn", "dw_r", "dw_i", "dbb_re",
               "dbb_im", "dc_re", "dc_imn", "loss"]


def _small_reduce(parts, shapes, lr_col, li_col, ldt_col, b_re, b_im, groups, comm=None):
    gn, pch = b_re.shape
    n = gn // groups
    nparts = parts[SMALL_PARTS[0]].size // math.prod(shapes[SMALL_PARTS[0]])
    np_, nout = len(SMALL_PARTS), 24

    def body(*refs):
        ins = refs[:np_]
        lr, li, ldt, bre, bim = refs[np_:np_ + 5]
        outs = refs[np_ + 5:np_ + 5 + nout]
        sums = dict(zip(SMALL_PARTS, refs[np_ + 5 + nout:]))

        @pl.when(pl.program_id(0) == 0)
        def _():
            for k, r in zip(SMALL_PARTS, ins):
                sums[k][...] = r[...].astype(F32)

        @pl.when(pl.program_id(0) > 0)
        def _():
            for k, r in zip(SMALL_PARTS, ins):
                sums[k][...] += r[...].astype(F32)

        @pl.when(pl.program_id(0) == nparts - 1)
        def _():
            finish({k: s[...] for k, s in sums.items()}, lr, li, ldt, bre, bim, *outs)

    def finish(tot, lr, li, ldt, bre, bim, o_loss, o_gmix, o_bin, o_bglu, o_s5d, o_convb, o_br, o_bi, o_lam, o_gffn,
               o_gpg, o_bpg, o_gple, o_gfin, o_wr, o_wi, o_cre, o_cim, o_bre, o_bim, o_lre, o_lim, o_ldt, o_convw):
        o_loss[...] = tot["loss"]
        o_convw[...] = tot["vec_lru"][SUBLANES - CONV_WIDTH:SUBLANES]
        o_bpg[...] = tot["vec_tail"][0:1]
        o_gpg[...] = tot["vec_tail"][1:2]
        o_gple[...] = tot["vec_tail"][2:3]
        o_gfin[...] = tot["vec_tail"][3:4]
        o_gffn[...] = tot["vec_ffn"][0:1]
        o_convb[...] = tot["vec_lru"][0:1]
        o_br[...] = tot["vec_lru"][1:2]
        o_bi[...] = tot["vec_lru"][2:3]
        o_lam[...] = tot["vec_lru"][3:4]
        o_gmix[...] = tot["vec_mix"][0:1]
        o_bin[...] = tot["vec_bin"][0:1]
        o_bglu[...] = tot["vec_sa"][0:1]
        o_s5d[...] = tot["vec_sa"][1:2]
        o_wr[...] = tot["dw_r"].T
        o_wi[...] = tot["dw_i"].T
        o_cre[...] = tot["dc_re"].T
        o_cim[...] = -tot["dc_imn"].T
        d_a = tot["vec_gn"].T
        _, chain = jax.vjp(_disc_cols, lr[...], li[...], ldt[...], bre[...], bim[...])
        d_lr, d_li, d_ldt, d_bre, d_bim = chain((d_a[:, 0:1], d_a[:, 1:2], tot["dbb_re"].T, tot["dbb_im"].T))
        o_lre[...] = d_lr
        o_lim[...] = d_li
        o_bre[...] = d_bre
        o_bim[...] = d_bim
        same = (lax.broadcasted_iota(jnp.int32, (groups, gn), 0)
                == lax.broadcasted_iota(jnp.int32, (groups, gn), 1) // n).astype(F32)
        o_ldt[...] = jnp.dot(same, d_ldt * jnp.ones((1, LANES), F32), preferred_element_type=F32,
                             precision=lax.Precision.HIGHEST)[:, 0:1]

    d = shapes["vec_mix"][1]
    nz = shapes["vec_bin"][1]
    sa = shapes["vec_sa"][1]
    w = shapes["vec_lru"][1]
    row = lambda c: jax.ShapeDtypeStruct((1, c), F32)
    out_shape = [jax.ShapeDtypeStruct(shapes["loss"], F32), row(d), row(nz), row(sa), row(sa), row(w), row(w), row(w),
                 row(w), row(d), row(d), row(d), row(d), row(d),
                 jax.ShapeDtypeStruct(shapes["dw_r"][::-1], F32), jax.ShapeDtypeStruct(shapes["dw_i"][::-1], F32),
                 jax.ShapeDtypeStruct(shapes["dc_re"][::-1], F32), jax.ShapeDtypeStruct(shapes["dc_imn"][::-1], F32),
                 jax.ShapeDtypeStruct((gn, pch), F32), jax.ShapeDtypeStruct((gn, pch), F32),
                 jax.ShapeDtypeStruct((gn, 1), F32), jax.ShapeDtypeStruct((gn, 1), F32),
                 jax.ShapeDtypeStruct((groups, 1), F32), jax.ShapeDtypeStruct((CONV_WIDTH, w), F32)]
    def part_spec(k):
        r, c = shapes[k]
        if parts[k].ndim == 3:
            return pl.BlockSpec((None, r, c), lambda i: (i, 0, 0))
        return pl.BlockSpec((r, c), lambda i: (i, 0))

    outs = _run(
        body, comm, name="small_reduce", grid=(nparts,), out_shape=out_shape,
        in_specs=[part_spec(k) for k in SMALL_PARTS] + [_whole(a.shape) for a in (lr_col, li_col, ldt_col, b_re, b_im)],
        out_specs=[_acc(s.shape) for s in out_shape],
        scratch_shapes=[pltpu.VMEM(shapes[k], F32) for k in SMALL_PARTS],
        semantics="arbitrary",
    )(*[parts[k] for k in SMALL_PARTS], lr_col, li_col, ldt_col, b_re, b_im)
    extra = None
    if comm is not None:
        outs, extra = outs
    names = ["loss", "g_mix", "b_in", "b_glu", "s5_d", "conv_b", "b_r", "b_i", "lru_lambda", "g_ffn", "g_ple_gate",
             "b_ple_gate", "g_ple", "g_final", "w_r", "w_i", "s5_c_re", "s5_c_im", "s5_b_re", "s5_b_im", "lam_re",
             "lam_im", "log_dt", "conv_w"]
    res = dict(zip(names, outs))
    return res if comm is None else (res, extra)


def _adamw_small(ws, gs, ms, vs):
    n = len(ws)

    def body(*refs):
        w_r, g_r, m_r, v_r = (refs[i * n:(i + 1) * n] for i in range(4))
        g_o, d_o, m_o, v_o = (refs[(4 + i) * n:(5 + i) * n] for i in range(4))
        for i in range(n):
            g = g_r[i][...]
            delta, m_new, v_new = _adamw_math(w_r[i][...], g, m_r[i][...], v_r[i][...])
            g_o[i][...] = g
            d_o[i][...] = delta
            m_o[i][...] = m_new
            v_o[i][...] = v_new

    shapes = [jax.ShapeDtypeStruct(a.shape, F32) for a in ws]
    outs = pl.pallas_call(body, name="adamw_small", out_shape=shapes * 4)(*ws, *gs, *ms, *vs)
    return outs[:n], outs[n:2 * n], outs[2 * n:3 * n], outs[3 * n:]


def _adamw_math(w, g, m, v):
    m_new = ADAM_B1 * m + (1.0 - ADAM_B1) * g
    v_new = ADAM_B2 * v + (1.0 - ADAM_B2) * (g * g)
    m_hat = m_new / (1.0 - ADAM_B1 ** ADAM_STEP)
    v_hat = v_new / (1.0 - ADAM_B2 ** ADAM_STEP)
    delta = -ADAM_LR * (m_hat / (jnp.sqrt(v_hat) + ADAM_EPS) + ADAM_WD * w)
    return delta, m_new, v_new


def _row_tile(rows):
    for cand in range(256, 0, -16):
        if rows % cand == 0:
            return cand
    return rows


def _adamw(parts, w, m, v, name, transposed=False):
    nw = len(w)
    rows, cols = w[0].shape
    npart = parts[0].shape[0]
    tr = _row_tile(rows)
    if transposed:
        parts_spec = pl.BlockSpec((npart, cols, tr), lambda i: (0, 0, i))
    else:
        parts_spec = pl.BlockSpec((npart, tr, cols), lambda i: (0, i, 0))

    def body(*refs):
        for i in range(nw):
            p_ref, w_ref, m_ref, v_ref = refs[4 * i:4 * i + 4]
            g_ref, d_ref, mo_ref, vo_ref = refs[4 * (nw + i):4 * (nw + i) + 4]
            g = p_ref[0].astype(F32)
            for k in range(1, npart):
                g = g + p_ref[k].astype(F32)
            if transposed:
                g = g.T
            delta, m_new, v_new = _adamw_math(w_ref[...], g, m_ref[...], v_ref[...])
            g_ref[...] = g
            d_ref[...] = delta
            mo_ref[...] = m_new
            vo_ref[...] = v_new

    res = pl.pallas_call(
        body, name=name, grid=(rows // tr,),
        out_shape=[jax.ShapeDtypeStruct((rows, cols), F32)] * (4 * nw),
        in_specs=([parts_spec] + [_rows(tr, cols)] * 3) * nw,
        out_specs=[_rows(tr, cols)] * (4 * nw),
        compiler_params=_params("parallel"),
    )(*[a for group in zip(parts, w, m, v) for a in group])
    return [res[4 * i:4 * i + 4] for i in range(nw)]


def _mesh_position():
    return lax.axis_index("x"), lax.axis_index("y"), lax.axis_index("c")


def _flip(pos, rel):
    x, y, c = pos
    return (1 - x if rel & 4 else x, 1 - y if rel & 2 else y, 1 - c if rel & 1 else c)


def _index(pos):
    return 4 * pos[0] + 2 * pos[1] + pos[2]


_ANY = pl.BlockSpec(memory_space=pl.ANY)
FLAT_ROWS = 32


def _dma_sems(n):
    return [pltpu.SemaphoreType.DMA((n, N_DEV - 1)), pltpu.SemaphoreType.DMA((n, N_DEV - 1)), pltpu.SemaphoreType.DMA((n,))]


def _block_of(ref, idx, rows, flat):
    if flat:
        return ref.at[pl.ds(pl.multiple_of(idx * rows, FLAT_ROWS), rows), :]
    return ref.at[idx]


class _Gather:
    chips = (4, 2, 6)
    rels = frozenset((1, 4, 2, 6))

    def __init__(self, shards):
        self.inputs = list(shards)
        self.flat = [s.shape[0] % FLAT_ROWS == 0 for s in shards]
        self.out_shape = [
            jax.ShapeDtypeStruct((N_DEV * s.shape[0], s.shape[1]) if f else (N_DEV,) + s.shape, s.dtype)
            for s, f in zip(shards, self.flat)]
        self.sems = _dma_sems(len(shards))

    def _copy(self, ins, outs, sems, i, k, block, to, own=False):
        dst = _block_of(outs[i], _index(block), self.inputs[i].shape[0], self.flat[i])
        return pltpu.make_async_remote_copy(
            src_ref=ins[i] if own else dst, dst_ref=dst, send_sem=sems[0].at[i, k], recv_sem=sems[1].at[i, k],
            device_id=to, device_id_type=MESH)

    def _local(self, ins, outs, sems, i, me):
        dst = _block_of(outs[i], _index(me), self.inputs[i].shape[0], self.flat[i])
        return pltpu.make_async_copy(ins[i], dst, sems[2].at[i])

    def _first(self, ins, outs, sems, i, me):
        cps = [self._copy(ins, outs, sems, i, 0, me, _flip(me, 1), own=True)]
        cps += [self._copy(ins, outs, sems, i, 1 + j, me, _flip(me, rel), own=True) for j, rel in enumerate(self.chips)]
        return cps

    def _passed(self, ins, outs, sems, i, j, me):
        return self._copy(ins, outs, sems, i, 4 + j, _flip(me, self.chips[j]), _flip(me, 1))

    def before(self, ins, outs, sems):
        n = len(self.inputs)
        me = _mesh_position()

        @pl.when(pl.program_id(0) == 0)
        def _():
            for i in range(n):
                self._local(ins, outs, sems, i, me).start()
                for cp in self._first(ins, outs, sems, i, me):
                    cp.start()

        @pl.when(pl.program_id(0) == pl.num_programs(0) - 1)
        def _():
            for j, rel in enumerate(self.chips):
                for i in range(n):
                    self._copy(ins, outs, sems, i, 1 + j, _flip(me, rel), me).wait_recv()
                    self._passed(ins, outs, sems, i, j, me).start()

    def after(self, ins, outs, sems):
        n = len(self.inputs)
        me = _mesh_position()
        sibling = _flip(me, 1)

        @pl.when(pl.program_id(0) == pl.num_programs(0) - 1)
        def _():
            for i in range(n):
                self._copy(ins, outs, sems, i, 0, sibling, me).wait_recv()
                for j, rel in enumerate(self.chips):
                    self._copy(ins, outs, sems, i, 4 + j, _flip(sibling, rel), me).wait_recv()
            for i in range(n):
                for cp in self._first(ins, outs, sems, i, me):
                    cp.wait_send()
                for j in range(len(self.chips)):
                    self._passed(ins, outs, sems, i, j, me).wait_send()
                self._local(ins, outs, sems, i, me).wait()


N_CHIPS = N_DEV // 2


def _chip(pos):
    return 2 * pos[0] + pos[1]


class _PairSwap:
    rels = frozenset((1,))

    def __init__(self, arrays):
        self.inputs = list(arrays)
        self.rows = [a.shape[0] // N_DEV for a in arrays]
        for r in self.rows:
            assert r % FLAT_ROWS == 0, r
        self.out_shape = [jax.ShapeDtypeStruct((N_CHIPS, r, a.shape[1]), a.dtype) for a, r in zip(arrays, self.rows)]
        n = len(arrays)
        self.sems = [pltpu.SemaphoreType.DMA((n, N_CHIPS)), pltpu.SemaphoreType.DMA((n, N_CHIPS))]

    def _copy(self, ins, outs, sems, i, j, me):
        sibling = _flip(me, 1)
        return pltpu.make_async_remote_copy(
            src_ref=_block_of(ins[i], 2 * j + sibling[2], self.rows[i], True), dst_ref=outs[i].at[j],
            send_sem=sems[0].at[i, j], recv_sem=sems[1].at[i, j], device_id=sibling, device_id_type=MESH)

    def before(self, ins, outs, sems):
        me = _mesh_position()

        @pl.when(pl.program_id(0) == 0)
        def _():
            for i in range(len(self.inputs)):
                for j in range(N_CHIPS):
                    self._copy(ins, outs, sems, i, j, me).start()

    def after(self, ins, outs, sems):
        me = _mesh_position()

        @pl.when(pl.program_id(0) == pl.num_programs(0) - 1)
        def _():
            for i in range(len(self.inputs)):
                for j in range(N_CHIPS):
                    self._copy(ins, outs, sems, i, j, me).wait()


class _ChipExchange:
    chips = (4, 2, 6)
    rels = frozenset(chips)

    def __init__(self, arrays):
        self.inputs = list(arrays)
        self.out_shape = [jax.ShapeDtypeStruct(a.shape, a.dtype) for a in arrays]
        n = len(arrays)
        self.sems = [pltpu.SemaphoreType.DMA((n, 3)), pltpu.SemaphoreType.DMA((n, 3)), pltpu.SemaphoreType.DMA((n,))]

    def _send(self, ins, outs, sems, i, k, me):
        peer = _flip(me, self.chips[k])
        return pltpu.make_async_remote_copy(
            src_ref=ins[i].at[_chip(peer)], dst_ref=outs[i].at[_chip(me)], send_sem=sems[0].at[i, k],
            recv_sem=sems[1].at[i, k], device_id=peer, device_id_type=MESH)

    def _arrival(self, ins, outs, sems, i, k, me):
        peer = _flip(me, self.chips[k])
        return pltpu.make_async_remote_copy(
            src_ref=ins[i].at[_chip(me)], dst_ref=outs[i].at[_chip(peer)], send_sem=sems[0].at[i, k],
            recv_sem=sems[1].at[i, k], device_id=peer, device_id_type=MESH)

    def _local(self, ins, outs, sems, i, me):
        return pltpu.make_async_copy(ins[i].at[_chip(me)], outs[i].at[_chip(me)], sems[2].at[i])

    def before(self, ins, outs, sems):
        me = _mesh_position()

        @pl.when(pl.program_id(0) == 0)
        def _():
            for i in range(len(self.inputs)):
                self._local(ins, outs, sems, i, me).start()
                for k in range(len(self.chips)):
                    self._send(ins, outs, sems, i, k, me).start()

    def after(self, ins, outs, sems):
        me = _mesh_position()

        @pl.when(pl.program_id(0) == pl.num_programs(0) - 1)
        def _():
            for i in range(len(self.inputs)):
                for k in range(len(self.chips)):
                    self._arrival(ins, outs, sems, i, k, me).wait_recv()
            for i in range(len(self.inputs)):
                for k in range(len(self.chips)):
                    self._send(ins, outs, sems, i, k, me).wait_send()
                self._local(ins, outs, sems, i, me).wait()


def _pair_add(grads, halves, name):
    n = len(grads)

    def body(*refs):
        g_refs, h_refs, o_refs = refs[:n], refs[n:2 * n], refs[2 * n:]
        c = lax.axis_index("c")
        for i in range(n):
            r = h_refs[i].shape[1]
            for j in range(N_CHIPS):
                own = g_refs[i][pl.ds(pl.multiple_of((2 * j + c) * r, FLAT_ROWS), r), :]
                o_refs[i][j] = (own.astype(F32) + h_refs[i][j].astype(F32)).astype(o_refs[i].dtype)

    return pl.pallas_call(
        body, name=name, out_shape=[jax.ShapeDtypeStruct(h.shape, h.dtype) for h in halves],
        compiler_params=pltpu.CompilerParams(vmem_limit_bytes=VMEM_LIMIT),
    )(*grads, *halves)


class _Both:
    def __init__(self, first, second):
        self.jobs = (first, second)
        self.rels = first.rels | second.rels
        self.inputs = first.inputs + second.inputs
        self.out_shape = first.out_shape + second.out_shape
        self.sems = first.sems + second.sems

    def _each(self, ins, outs, sems):
        a = self.jobs[0]
        i, o, s = len(a.inputs), len(a.out_shape), len(a.sems)
        return ((a, ins[:i], outs[:o], sems[:s]), (self.jobs[1], ins[i:], outs[o:], sems[s:]))

    def before(self, ins, outs, sems):
        for job, i, o, s in self._each(ins, outs, sems):
            job.before(i, o, s)

    def after(self, ins, outs, sems):
        for job, i, o, s in self._each(ins, outs, sems):
            job.after(i, o, s)


_COLLECTIVE_IDS = {(1,): 0, (2, 4, 6): 1, (1, 2, 4, 6): 2}


def _entry_barrier(rels):
    @pl.when(pl.program_id(0) == 0)
    def _():
        me = _mesh_position()
        sem = pltpu.get_barrier_semaphore()
        for rel in rels:
            pl.semaphore_signal(sem, inc=1, device_id=_flip(me, rel), device_id_type=MESH)
        pl.semaphore_wait(sem, len(rels))


def _run(body, comm, *, semantics, out_shape, in_specs, out_specs, scratch_shapes=(), **kw):
    if comm is None:
        return pl.pallas_call(body, out_shape=out_shape, in_specs=in_specs, out_specs=out_specs,
                              scratch_shapes=list(scratch_shapes), compiler_params=_params(semantics), **kw)
    single = not isinstance(out_shape, (list, tuple))
    outs = [out_shape] if single else list(out_shape)
    ospecs = [out_specs] if single else list(out_specs)
    counts = [len(in_specs), len(comm.inputs), len(outs), len(comm.out_shape), len(scratch_shapes), len(comm.sems)]
    rels = tuple(sorted(comm.rels))

    def carrying(*refs):
        groups, pos = [], 0
        for c in counts:
            groups.append(refs[pos:pos + c])
            pos += c
        main_in, comm_in, main_out, comm_out, main_scratch, comm_sems = groups
        _entry_barrier(rels)
        comm.before(comm_in, comm_out, comm_sems)
        body(*main_in, *main_out, *main_scratch)
        comm.after(comm_in, comm_out, comm_sems)

    call = pl.pallas_call(
        carrying, out_shape=outs + list(comm.out_shape), in_specs=list(in_specs) + [_ANY] * len(comm.inputs),
        out_specs=ospecs + [_ANY] * len(comm.out_shape), scratch_shapes=list(scratch_shapes) + list(comm.sems),
        compiler_params=pltpu.CompilerParams(dimension_semantics=("arbitrary",), vmem_limit_bytes=VMEM_LIMIT,
                                             collective_id=_COLLECTIVE_IDS[rels]), **kw)

    def apply(*args):
        res = call(*args, *comm.inputs)
        main = res[:len(outs)]
        return (main[0] if single else list(main)), list(res[len(outs):])

    return apply


def _alone(comm, name):
    return _run(lambda: None, comm, semantics="arbitrary", name=name, grid=(1,), out_shape=[], in_specs=[], out_specs=[])()[1]


SHARDED = {"w_in": 1, "w_glu": 0, "conv_w": 1, "w_a_out": 1, "w_b_out": 0, "w_o": 0, "w_ffn_gate": 1, "w_ffn_up": 1,
           "w_ffn_down": 0, "w_ple_gate": 0, "w_ple": 1}
TRANSPOSED = ("w_in", "w_a_out", "w_ffn_gate", "w_ffn_up", "w_ple")
LONG_AXIS_MINOR = ("w_in", "w_ffn_gate", "w_ffn_up")
NARROW_LAST = ("s5_b_re", "s5_b_im", "s5_d")
ADAMW_GROUPS = (("w_in",), ("w_glu",), ("conv_w",), ("w_a_out",), ("w_b_out", "w_o", "w_ple_gate"),
                ("w_ffn_gate", "w_ffn_up"), ("w_ffn_down",), ("w_ple",))
SMALL = ["g_mix", "b_in", "lam_re", "lam_im", "log_dt", "s5_b_re", "s5_b_im", "s5_c_re", "s5_c_im", "s5_d", "b_glu",
         "conv_b", "w_r", "b_r", "w_i", "b_i", "lru_lambda", "g_ffn", "g_ple_gate", "b_ple_gate", "g_ple", "g_final"]
WEIGHTS = ["g_mix", "w_in", "b_in", "lam_re", "lam_im", "log_dt", "s5_b_re", "s5_b_im", "s5_c_re", "s5_c_im", "s5_d",
           "w_glu", "b_glu", "conv_w", "conv_b", "w_r", "b_r", "w_i", "b_i", "lru_lambda", "w_a_out", "w_b_out", "w_o",
           "g_ffn", "w_ffn_gate", "w_ffn_up", "w_ffn_down", "g_ple_gate", "w_ple_gate", "b_ple_gate", "w_ple", "g_ple",
           "g_final"]


def _unblock(gathered, axis):
    nb, r, c = gathered.shape
    if axis == 0:
        return gathered.reshape(nb * r, c)
    return jnp.transpose(gathered, (1, 0, 2)).reshape(r, nb * c)


def _disc_scalars(lr, li, ldt):
    dt = jnp.exp(ldt)
    mag = jnp.exp(lr * dt)
    ar = mag * jnp.cos(li * dt)
    ai = mag * jnp.sin(li * dt)
    den = lr * lr + li * li
    nr = ar - 1.0
    fr = (nr * lr + ai * li) / den
    fi = (ai * lr - nr * li) / den
    return ar, ai, fr, fi


def _disc_cols(lr, li, ldt, b_re, b_im):
    ar, ai, fr, fi = _disc_scalars(lr, li, ldt)
    return ar, ai, fr * b_re - fi * b_im, fr * b_im + fi * b_re


def _local_step(x, p, target, src, small, disc, distributed=True):
    full = {} if distributed else dict(src)
    gw, halves, pairs, got = {}, {}, {}, {}

    def gather(keys):
        return (_Gather([src[k] for k in keys]), keys, full) if distributed else None

    def swap(keys):
        return (_PairSwap([gw[k] for k in keys]), keys, halves) if distributed else None

    def chips(keys):
        return (_ChipExchange([pairs[k] for k in keys]), keys, got) if distributed else None

    def add_pairs(keys):
        if distributed:
            pairs.update(zip(keys, _pair_add([gw[k] for k in keys], [halves[k] for k in keys], "pair_add_" + keys[0])))

    def carry(fn, *args, jobs=()):
        jobs = [j for j in jobs if j is not None]
        if not jobs:
            return fn(*args)
        comm = jobs[0][0]
        for j in jobs[1:]:
            comm = _Both(comm, j[0])
        res, extra = fn(*args, comm=comm)
        for job, keys, sink in jobs:
            sink.update(zip(keys, extra[:len(job.out_shape)]))
            extra = extra[len(job.out_shape):]
        return res

    d = x.shape[1]
    g, n, pch = small["s5_b_re"].shape
    sa, lw = g * pch, small["lru_lambda"].shape[-1]
    widths = [sa, lw, d, d]
    row = lambda v: v.reshape(1, -1)

    hd = small["w_r"].shape[-1]
    ar_row, ai_row, bbr_blk, bbi_blk, cre_blk, cimn_blk, wr_blk, wi_blk = carry(
        _prep, *disc["rows"], *disc["cols"], disc["b_re"], disc["b_im"], small["s5_c_re"].reshape(sa, n),
        small["s5_c_im"].reshape(sa, n), small["w_r"].reshape(lw, hd), small["w_i"].reshape(lw, hd),
        jobs=[gather(["w_in"])])
    d_row = row(small["s5_d"])
    u_a, u_b, za, zb = carry(_inproj_fwd, x, row(small["g_mix"]), full["w_in"], row(small["b_in"]), widths,
                             jobs=[gather(["w_glu", "conv_w", "w_a_out", "w_b_out"])])
    conv_w = _unblock(full["conv_w"], 1) if distributed else full["conv_w"]
    branches = [_s5_fwd(u_a, bbr_blk, bbi_blk, ar_row, ai_row, cre_blk, cimn_blk, d_row, full["w_glu"],
                        row(small["b_glu"])),
                _lru_fwd(u_b, conv_w, row(small["conv_b"]), wr_blk, row(small["b_r"]), wi_blk, row(small["b_i"]),
                         row(small["lru_lambda"]))]
    (sr, si, y, y_a), (xc, h, hprev) = carry(_stages, branches, "branches_fwd",
                                             jobs=[gather(["w_ffn_gate", "w_ffn_up", "w_o"])])
    x1, merged, ma, mb, fg, fu = carry(
        _merge_ffn_up_fwd, y_a, h, za, zb, x, full["w_a_out"], full["w_b_out"], full["w_o"], row(small["g_ffn"]),
        full["w_ffn_gate"], full["w_ffn_up"], jobs=[gather(["w_ffn_down", "w_ple_gate", "w_ple"])])
    x2 = _ffn_down_fwd(fg, fu, x1, full["w_ffn_down"])

    dx2, loss_blk, gw["w_ple_gate"], gw["w_ple"], vec_tail = _tail_fwd_bwd(
        x2, p, target, row(small["g_ple_gate"]), full["w_ple_gate"], row(small["b_ple_gate"]), full["w_ple"],
        row(small["g_ple"]), row(small["g_final"]))
    dfg, dfu, act = carry(_ffn_bwd_a, dx2, fg, fu, full["w_ffn_down"], jobs=[swap(["w_ple_gate", "w_ple"])])
    tn_d = min(d, 512)
    gw["w_ffn_down"] = _matmul_tn(act, dx2, tn_d, "dw_ffn_down", BF16)
    add_pairs(["w_ple_gate", "w_ple"])
    dx1, h2, vec_ffn = carry(_ffn_bwd_b, dfg, dfu, x1, dx2, row(small["g_ffn"]), full["w_ffn_gate"], full["w_ffn_up"],
                             jobs=[chips(["w_ple_gate", "w_ple"]), swap(["w_ffn_down"])])
    gw["w_ffn_gate"] = _matmul_tn(dfg, h2, tn_d, "dw_ffn_gate", BF16)
    gw["w_ffn_up"] = _matmul_tn(dfu, h2, tn_d, "dw_ffn_up", BF16)
    add_pairs(["w_ffn_down"])
    dza, dzb, dya, dyb, gw["w_o"], gw["w_a_out"], gw["w_b_out"] = carry(
        _merge_bwd, dx1, merged, ma, mb, za, zb, y_a, h, full["w_o"], full["w_a_out"], full["w_b_out"],
        jobs=[chips(["w_ffn_down"]), swap(["w_ffn_gate", "w_ffn_up"])])
    add_pairs(["w_ffn_gate", "w_ffn_up"])
    du_b, dw_r, dw_i, vec_lru = carry(
        _lru_bwd, dyb, xc, hprev, u_b, conv_w, wr_blk, row(small["b_r"]), wi_blk, row(small["b_i"]),
        row(small["lru_lambda"]), hd, jobs=[chips(["w_ffn_gate"]), swap(["w_o", "w_a_out", "w_b_out"])])
    add_pairs(["w_o", "w_a_out", "w_b_out"])
    du_a, lam_r, lam_i, dy16, gw["w_glu"], vec_sa, vec_gn = carry(
        _s5_bwd, dya, y, sr, si, u_a, full["w_glu"], row(small["b_glu"]), cre_blk, cimn_blk, bbr_blk, bbi_blk, ar_row,
        ai_row, d_row, jobs=[chips(["w_ffn_up"]), chips(["w_o", "w_a_out", "w_b_out"])])
    smalls = {"vec_tail": vec_tail, "vec_ffn": vec_ffn, "vec_lru": vec_lru, "vec_sa": vec_sa, "vec_gn": vec_gn,
              "dw_r": dw_r, "dw_i": dw_i, "loss": loss_blk}
    everyones = {}

    def gather_smalls(keys):
        return (_Gather([smalls[k] for k in keys]), keys, everyones) if distributed else None

    smalls["dbb_re"], smalls["dbb_im"], smalls["dc_re"], smalls["dc_imn"] = carry(
        _s5_param_grads, lam_r, lam_i, sr, si, u_a, dy16, pch, n, jobs=[swap(["w_glu"]), gather_smalls(list(smalls))])
    add_pairs(["w_glu"])
    dz = [du_a, du_b, dza, dzb]
    grad_x, h0, smalls["vec_mix"], smalls["vec_bin"] = _inproj_bwd(
        dz, x, dx1, row(small["g_mix"]), full["w_in"])
    shapes = {k: a.shape for k, a in smalls.items()}
    gw["w_in"] = carry(_dw_from_parts, dz, h0, tn_d, "dw_in",
                       jobs=[chips(["w_glu"]), gather_smalls(["dbb_re", "dbb_im", "dc_re", "dc_imn"])])
    smalls.update(everyones)
    if distributed:
        got["w_in"] = gw["w_in"]
        gw = got
    return grad_x, gw, smalls, shapes


def _disc_inputs(small):
    g, n, pch = small["s5_b_re"].shape
    srcs = (small["lam_re"], small["lam_im"], jnp.repeat(small["log_dt"], n))
    return {"rows": [a.reshape(1, g * n) for a in srcs], "cols": [a.reshape(g * n, 1) for a in srcs],
            "b_re": small["s5_b_re"].reshape(g * n, pch), "b_im": small["s5_b_im"].reshape(g * n, pch)}


def kernel(x, p, g_mix, w_in, b_in, lam_re, lam_im, log_dt, s5_b_re, s5_b_im, s5_c_re, s5_c_im, s5_d, w_glu, b_glu, conv_w, conv_b, w_r, b_r, w_i, b_i, lru_lambda, w_a_out, w_b_out, w_o, g_ffn, w_ffn_gate, w_ffn_up, w_ffn_down, g_ple_gate, w_ple_gate, b_ple_gate, w_ple, g_ple, g_final, loss_target, m_g_mix, m_w_in, m_b_in, m_lam_re, m_lam_im, m_log_dt, m_s5_b_re, m_s5_b_im, m_s5_c_re, m_s5_c_im, m_s5_d, m_w_glu, m_b_glu, m_conv_w, m_conv_b, m_w_r, m_b_r, m_w_i, m_b_i, m_lru_lambda, m_w_a_out, m_w_b_out, m_w_o, m_g_ffn, m_w_ffn_gate, m_w_ffn_up, m_w_ffn_down, m_g_ple_gate, m_w_ple_gate, m_b_ple_gate, m_w_ple, m_g_ple, m_g_final, v_g_mix, v_w_in, v_b_in, v_lam_re, v_lam_im, v_log_dt, v_s5_b_re, v_s5_b_im, v_s5_c_re, v_s5_c_im, v_s5_d, v_w_glu, v_b_glu, v_conv_w, v_conv_b, v_w_r, v_b_r, v_w_i, v_b_i, v_lru_lambda, v_w_a_out, v_w_b_out, v_w_o, v_g_ffn, v_w_ffn_gate, v_w_ffn_up, v_w_ffn_down, v_g_ple_gate, v_w_ple_gate, v_b_ple_gate, v_w_ple, v_g_ple, v_g_final):
    given = dict(locals())
    wts = {k: given[k] for k in WEIGHTS}
    moms = {k: given["m_" + k] for k in WEIGHTS}
    vels = {k: given["v_" + k] for k in WEIGHTS}

    def drop_depth(k, a):
        return a if k == "g_final" else a[0]

    small = {k: drop_depth(k, wts[k]) for k in SMALL}
    shard = {k: wts[k][0] for k in SHARDED}
    names = list(SHARDED)

    def wire(k):
        if k == "conv_w":
            return shard[k]
        return (shard[k].T if k in TRANSPOSED else shard[k]).astype(BF16)

    disc = _disc_inputs(small)
    grad_x, parts, smalls, shapes = _local_step(x[0], p[0, 0], loss_target[0], {k: wire(k) for k in names}, small, disc)

    late = ["vec_mix", "vec_bin"]
    received = _alone(_Both(_PairSwap([parts["w_in"]]), _Gather([smalls[k] for k in late])), "swap_last")
    (pair_in,) = _pair_add([parts["w_in"]], received[:1], "pair_add_w_in")
    smalls.update(zip(late, received[1:]))

    g_small, (parts["w_in"],) = _small_reduce(smalls, shapes, *disc["cols"], disc["b_re"], disc["b_im"],
                                              small["s5_b_re"].shape[0], comm=_ChipExchange([pair_in]))
    loss = g_small.pop("loss")[0, 0]
    cols = shard["conv_w"].shape[1]
    mine = _index((lax.axis_index("x"), lax.axis_index("y"), lax.axis_index("c")))
    parts["conv_w"] = lax.dynamic_slice_in_dim(g_small.pop("conv_w"), mine * cols, cols, axis=1)[None]

    def view(k, a):
        a = a.reshape((1, -1) if k == "g_final" else wts[k].shape)
        return jnp.swapaxes(a, -1, -2) if k in NARROW_LAST else a

    def unview(k, a):
        return (jnp.swapaxes(a, -1, -2) if k in NARROW_LAST else a).reshape(wts[k].shape)

    slots = _adamw_small([view(k, wts[k]) for k in SMALL], [view(k, g_small[k]) for k in SMALL],
                         [view(k, moms[k]) for k in SMALL], [view(k, vels[k]) for k in SMALL])
    small_out = [dict(zip(SMALL, [unview(k, a) for k, a in zip(SMALL, slot)])) for slot in slots]

    big_out = {}
    for group in ADAMW_GROUPS:
        flip = group[0] in LONG_AXIS_MINOR
        look = (lambda a: a.T) if flip else (lambda a: a)
        res = _adamw([parts[k] for k in group], [look(shard[k]) for k in group], [look(moms[k][0]) for k in group],
                     [look(vels[k][0]) for k in group], "adamw_" + group[0],
                     transposed=group[0] in TRANSPOSED and not flip)
        for k, outs4 in zip(group, res):
            big_out[k] = [look(a) for a in outs4]

    outs = [loss, grad_x[None]]
    for slot in range(4):
        for k in WEIGHTS:
            if k in SHARDED:
                outs.append(big_out[k][slot][None])
            else:
                outs.append(small_out[slot][k])
    return tuple(outs)
```

```python
import math

import jax
import jax.numpy as jnp
from jax import lax
from jax.experimental import pallas as pl
from jax.experimental.pallas import tpu as pltpu

F32 = jnp.float32
BF16 = jnp.bfloat16

EPS = 1e-6
LRU_C = 8.0
CONV_WIDTH = 4
ADAM_LR = 0.001
ADAM_B1 = 0.9
ADAM_B2 = 0.999
ADAM_EPS = 1e-08
ADAM_WD = 0.01
ADAM_STEP = 10

N_DEV = 8
MESH = pl.DeviceIdType.MESH
SUBLANES = 8
LANES = 128
VMEM_LIMIT = 56 * 1024 * 1024
TOKEN_TILE = 256
TIME_CHUNK = 256
S5_SLAB = 128
LRU_SLAB = 256


def _dot(a, b):
    return jnp.dot(a.astype(BF16), b.astype(BF16), preferred_element_type=F32)


def _dot_nt(a, b):
    return lax.dot_general(a.astype(BF16), b.astype(BF16), (((1,), (1,)), ((), ())), preferred_element_type=F32)


def _dot_tn(a, b):
    return lax.dot_general(a.astype(BF16), b.astype(BF16), (((0,), (0,)), ((), ())), preferred_element_type=F32)


def _sigmoid(x):
    return jax.nn.sigmoid(x)


def _rms_stats(x):
    r = lax.rsqrt(jnp.mean(x * x, axis=-1, keepdims=True) + EPS)
    return x * r, r


def _rms_bwd(dy, xhat, r, g):
    dxn = dy * g
    dx = r * (dxn - xhat * jnp.mean(dxn * xhat, axis=-1, keepdims=True))
    return dx, dy * xhat


def _rowsum(v):
    return jnp.sum(v, axis=0, keepdims=True)


def _expm1(x):
    u = jnp.exp(x)
    um1 = u - 1.0
    safe = jnp.where(um1 == 0.0, 1.0, jnp.log(u))
    return jnp.where(um1 == 0.0, x, um1 * x / safe)


def _softplus(x):
    e = jnp.exp(-jnp.abs(x))
    u = 1.0 + e
    um1 = u - 1.0
    safe = jnp.where(um1 == 0.0, 1.0, um1)
    log1p_e = jnp.where(um1 == 0.0, e, jnp.log(u) * e / safe)
    return jnp.maximum(x, 0.0) + log1p_e


_GELU_K = math.sqrt(2.0 / math.pi)
_GELU_C = 0.044715


def _gelu(x):
    return 0.5 * x * (1.0 + jnp.tanh(_GELU_K * (x + _GELU_C * x * x * x)))


def _gelu_grad(x):
    th = jnp.tanh(_GELU_K * (x + _GELU_C * x * x * x))
    return 0.5 * (1.0 + th) + 0.5 * x * (1.0 - th * th) * _GELU_K * (1.0 + 3.0 * _GELU_C * x * x)


def _params(*sem):
    return pltpu.CompilerParams(dimension_semantics=sem, vmem_limit_bytes=VMEM_LIMIT)


def _rows(tm, n):
    return pl.BlockSpec((tm, n), lambda i: (i, 0))


def _rows_rev(tm, n, steps):
    return pl.BlockSpec((tm, n), lambda i: (steps - 1 - i, 0))


def _whole(shape):
    nd = len(shape)
    return pl.BlockSpec(shape, lambda i: (0,) * nd, pipeline_mode=pl.Buffered(1))


def _acc(shape):
    nd = len(shape)
    return pl.BlockSpec(shape, lambda i: (0,) * nd)


def _zero_on_first(*refs):
    @pl.when(pl.program_id(0) == 0)
    def _():
        for r in refs:
            r[...] = jnp.zeros_like(r)


def _inproj_fwd(x, g_mix, w_in_t, b_in, widths, comm=None):
    t, d = x.shape
    n = w_in_t.shape[0]
    tm = min(TOKEN_TILE, t)
    offs = [sum(widths[:i]) for i in range(len(widths) + 1)]

    def body(x_ref, g_ref, w_ref, b_ref, *outs):
        xhat, _ = _rms_stats(x_ref[...])
        h = (xhat * g_ref[...]).astype(BF16)
        for k, o_ref in enumerate(outs):
            lo, hi = offs[k], offs[k + 1]
            o_ref[...] = _dot_nt(h, w_ref[lo:hi, :]) + b_ref[:, lo:hi]

    return _run(
        body, comm, name="inproj_fwd", grid=(t // tm,),
        out_shape=[jax.ShapeDtypeStruct((t, w), F32) for w in widths],
        in_specs=[_rows(tm, d), _whole((1, d)), _whole((n, d)), _whole((1, n))],
        out_specs=[_rows(tm, w) for w in widths],
        semantics="parallel",
    )(x, g_mix, w_in_t, b_in)


def _s5_fwd(u, bbr_blk, bbi_blk, ar, ai, cre_blk, cimn_blk, d_skip, w_glu, b_glu):
    t, sa = u.shape
    ns, _, sw = bbr_blk.shape
    gn = ns * sw
    tc = min(TIME_CHUNK, t)

    def body(u_ref, bbr_ref, bbi_ref, ar_ref, ai_ref, cre_ref, cim_ref, d_ref, wg_ref, bg_ref,
             sr_ref, si_ref, y_ref, ya_ref, cr_s, ci_s, sr_s, si_s):
        _zero_on_first(cr_s, ci_s)
        uv = u_ref[...]
        ub = uv.astype(BF16)
        for m in range(ns):
            um = ub[:, m * S5_SLAB:(m + 1) * S5_SLAB]
            sr_s[:, m * sw:(m + 1) * sw] = _dot(um, bbr_ref[m])
            si_s[:, m * sw:(m + 1) * sw] = _dot(um, bbi_ref[m])
        a_r = ar_ref[...]
        a_i = ai_ref[...]

        def step(row, carry):
            c_r, c_i = carry
            at = pl.ds(row, 1)
            n_r = a_r * c_r - a_i * c_i + sr_s[at, :]
            n_i = a_r * c_i + a_i * c_r + si_s[at, :]
            sr_s[at, :] = n_r
            si_s[at, :] = n_i
            return n_r, n_i

        c_r, c_i = lax.fori_loop(0, tc, step, (cr_s[0:1, :], ci_s[0:1, :]), unroll=8)
        cr_s[0:1, :] = c_r
        ci_s[0:1, :] = c_i
        for m in range(ns):
            states, chans = slice(m * sw, (m + 1) * sw), slice(m * S5_SLAB, (m + 1) * S5_SLAB)
            s_r, s_i = sr_s[:, states].astype(BF16), si_s[:, states].astype(BF16)
            sr_ref[:, states] = s_r.astype(sr_ref.dtype)
            si_ref[:, states] = s_i.astype(si_ref.dtype)
            y_ref[:, chans] = _dot(s_r, cre_ref[m]) + _dot(s_i, cim_ref[m]) + d_ref[:, chans] * uv[:, chans]
        y = y_ref[...]
        zz = _gelu(y)
        q = _dot(zz, wg_ref[...]) + bg_ref[...]
        ya_ref[...] = zz * _sigmoid(q)

    return dict(
        body=body, steps=t // tc, args=(u, bbr_blk, bbi_blk, ar, ai, cre_blk, cimn_blk, d_skip, w_glu, b_glu),
        out_shape=[jax.ShapeDtypeStruct((t, gn), BF16), jax.ShapeDtypeStruct((t, gn), BF16),
                   jax.ShapeDtypeStruct((t, sa), F32), jax.ShapeDtypeStruct((t, sa), F32)],
        in_specs=[_rows(tc, sa), _whole(bbr_blk.shape), _whole(bbi_blk.shape), _whole((1, gn)), _whole((1, gn)),
                  _whole(cre_blk.shape), _whole(cimn_blk.shape), _whole((1, sa)), _whole((sa, sa)), _whole((1, sa))],
        out_specs=[_rows(tc, gn), _rows(tc, gn), _rows(tc, sa), _rows(tc, sa)],
        scratch=[pltpu.VMEM((SUBLANES, gn), F32), pltpu.VMEM((SUBLANES, gn), F32),
                 pltpu.VMEM((tc, gn), F32), pltpu.VMEM((tc, gn), F32)])


def _stages(stages, name, comm=None):
    counts = [[len(s[k]) for s in stages] for k in ("args", "out_shape", "scratch")]

    def body(*refs):
        groups, pos = [], 0
        for kind in counts:
            per_stage = []
            for c in kind:
                per_stage.append(refs[pos:pos + c])
                pos += c
            groups.append(per_stage)
        for s, ins, outs, scratch in zip(stages, *groups):
            s["body"](*ins, *outs, *scratch)

    res = _run(
        body, comm, name=name, grid=(stages[0]["steps"],),
        out_shape=[o for s in stages for o in s["out_shape"]], in_specs=[i for s in stages for i in s["in_specs"]],
        out_specs=[o for s in stages for o in s["out_specs"]], scratch_shapes=[c for s in stages for c in s["scratch"]],
        semantics="arbitrary",
    )(*[a for s in stages for a in s["args"]])
    extra = None
    if comm is not None:
        res, extra = res
    per_stage, pos = [], 0
    for c in counts[1]:
        per_stage.append(list(res[pos:pos + c]))
        pos += c
    return per_stage if comm is None else (per_stage, extra)


def _slab_dot(x, w_ref, transposed=False):
    dot = _dot_nt if transposed else _dot
    xb = x.astype(BF16)
    return jnp.concatenate([dot(xb[:, j * LRU_SLAB:(j + 1) * LRU_SLAB], w_ref[j]) for j in range(w_ref.shape[0])], axis=1)


def _lru_gates(xc, wr_ref, br_ref, wi_ref, bi_ref, lam_ref):
    r = _sigmoid(_slab_dot(xc, wr_ref) + br_ref[...])
    ig = _sigmoid(_slab_dot(xc, wi_ref) + bi_ref[...])
    sp = _softplus(-lam_ref[...])
    log_a = (-LRU_C * r) * sp
    return r, ig, sp, log_a


def _lru_fwd(u, conv_w, conv_b, wr_blk, b_r, wi_blk, b_i, lru_lambda):
    t, w = u.shape
    tc = min(TIME_CHUNK, t)
    halo = SUBLANES

    def body(u_ref, cw_ref, cb_ref, wr_ref, br_ref, wi_ref, bi_ref, lam_ref,
             xc_ref, h_ref, hp_ref, ext_s, a_s, carry_s):
        @pl.when(pl.program_id(0) == 0)
        def _():
            ext_s[0:halo, :] = jnp.zeros((halo, w), F32)
            carry_s[...] = jnp.zeros_like(carry_s)

        ext_s[halo:halo + tc, :] = u_ref[...]
        xc = cb_ref[...]
        for k in range(CONV_WIDTH):
            off = halo - (CONV_WIDTH - 1) + k
            xc = xc + cw_ref[k:k + 1, :] * ext_s[off:off + tc, :]
        ext_s[0:halo, :] = ext_s[tc:tc + halo, :]
        xc_ref[...] = xc
        r, ig, sp, log_a = _lru_gates(xc, wr_ref, br_ref, wi_ref, bi_ref, lam_ref)
        a_s[...] = jnp.exp(log_a)
        h_ref[...] = jnp.sqrt(-_expm1(2.0 * log_a)) * ig * xc

        def step(row, carry):
            at = pl.ds(row, 1)
            hp_ref[at, :] = carry
            nxt = a_s[at, :] * carry + h_ref[at, :]
            h_ref[at, :] = nxt
            return nxt

        carry_s[0:1, :] = lax.fori_loop(0, tc, step, carry_s[0:1, :], unroll=8)

    return dict(
        body=body, steps=t // tc, args=(u, conv_w, conv_b, wr_blk, b_r, wi_blk, b_i, lru_lambda),
        out_shape=[jax.ShapeDtypeStruct((t, w), F32)] * 3,
        in_specs=[_rows(tc, w), _whole((CONV_WIDTH, w)), _whole((1, w)), _whole(wr_blk.shape), _whole((1, w)),
                  _whole(wi_blk.shape), _whole((1, w)), _whole((1, w))],
        out_specs=[_rows(tc, w)] * 3,
        scratch=[pltpu.VMEM((halo + tc, w), F32), pltpu.VMEM((tc, w), F32), pltpu.VMEM((SUBLANES, w), F32)])


def _merge_ffn_up_fwd(y_a, h, za, zb, x, w_a_out_t, w_b_out, w_o, g_ffn, w_gate_t, w_up_t, comm=None):
    t, d = x.shape
    sa, lw = y_a.shape[1], h.shape[1]
    f = w_gate_t.shape[0]
    tm = min(TOKEN_TILE, t)

    def body(ya_ref, h_ref, za_ref, zb_ref, x_ref, wa_ref, wb_ref, wo_ref, g_ref, wg_ref, wu_ref,
             x1_ref, mg_ref, ma_ref, mb_ref, fg_ref, fu_ref):
        ma = _dot_nt(ya_ref[...], wa_ref[...])
        mb = _dot(h_ref[...], wb_ref[...])
        merged = _sigmoid(za_ref[...]) * ma + _sigmoid(zb_ref[...]) * mb
        ma_ref[...] = ma.astype(ma_ref.dtype)
        mb_ref[...] = mb.astype(mb_ref.dtype)
        mg_ref[...] = merged.astype(mg_ref.dtype)
        x1 = x_ref[...] + _dot(merged, wo_ref[...])
        x1_ref[...] = x1
        xhat, _ = _rms_stats(x1)
        h2 = (xhat * g_ref[...]).astype(BF16)
        fg_ref[...] = _dot_nt(h2, wg_ref[...]).astype(fg_ref.dtype)
        fu_ref[...] = _dot_nt(h2, wu_ref[...]).astype(fu_ref.dtype)

    return _run(
        body, comm, name="merge_ffn_up_fwd", grid=(t // tm,),
        out_shape=[jax.ShapeDtypeStruct((t, d), F32), jax.ShapeDtypeStruct((t, d), BF16),
                   jax.ShapeDtypeStruct((t, d), BF16), jax.ShapeDtypeStruct((t, d), BF16),
                   jax.ShapeDtypeStruct((t, f), BF16), jax.ShapeDtypeStruct((t, f), BF16)],
        in_specs=[_rows(tm, sa), _rows(tm, lw), _rows(tm, d), _rows(tm, d), _rows(tm, d),
                  _whole((d, sa)), _whole((lw, d)), _whole((d, d)), _whole((1, d)), _whole((f, d)), _whole((f, d))],
        out_specs=[_rows(tm, d)] * 4 + [_rows(tm, f)] * 2,
        semantics="parallel",
    )(y_a, h, za, zb, x, w_a_out_t, w_b_out, w_o, g_ffn, w_gate_t, w_up_t)


def _ffn_down_fwd(fg, fu, x1, w_down, comm=None):
    t, d = x1.shape
    f = fg.shape[1]
    tm = min(TOKEN_TILE, t)

    def body(fg_ref, fu_ref, x_ref, wd_ref, x2_ref):
        fgv = fg_ref[...].astype(F32)
        act = fgv * _sigmoid(fgv) * fu_ref[...].astype(F32)
        x2_ref[...] = x_ref[...] + _dot(act, wd_ref[...])

    return _run(
        body, comm, name="ffn_down_fwd", grid=(t // tm,),
        out_shape=jax.ShapeDtypeStruct((t, d), F32),
        in_specs=[_rows(tm, f), _rows(tm, f), _rows(tm, d), _whole((f, d))],
        out_specs=_rows(tm, d),
        semantics="parallel",
    )(fg, fu, x1, w_down)


def _store_on_last(pairs):
    @pl.when(pl.program_id(0) == pl.num_programs(0) - 1)
    def _():
        for acc, out in pairs:
            out[...] = acc[...].astype(out.dtype)


def _tail_fwd_bwd(x2, p, target, g_pg, w_pg, b_pg, w_ple_t, g_ple, g_final):
    t, d = x2.shape
    pd = p.shape[1]
    tm = min(TOKEN_TILE, t)

    def body(x2_ref, p_ref, tg_ref, gpg_ref, wpg_ref, bpg_ref, wple_ref, gple_ref, gfin_ref,
             dx2_ref, loss_ref, dwpg_out, dwple_out, vec_ref, dwpg_ref, dwple_ref):
        _zero_on_first(loss_ref, dwpg_ref, dwple_ref, vec_ref)
        x2v = x2_ref[...]
        xh2, r2 = _rms_stats(x2v)
        h3 = xh2 * gpg_ref[...]
        gp = _sigmoid(_dot(h3, wpg_ref[...]) + bpg_ref[...])
        pe = _dot_nt(p_ref[...], wple_ref[...])
        peh, r3 = _rms_stats(pe)
        e = peh * gple_ref[...]
        x3 = x2v + gp * e
        xh3, r4 = _rms_stats(x3)
        diff = xh3 * gfin_ref[...] - tg_ref[...]
        loss_ref[...] += 0.5 * jnp.sum(jnp.mean(diff * diff, axis=-1, keepdims=True))
        dy = diff * (1.0 / d)
        dx3, dgfin = _rms_bwd(dy, xh3, r4, gfin_ref[...])
        d_gp = dx3 * e
        d_e = dx3 * gp
        dpe, dgple = _rms_bwd(d_e, peh, r3, gple_ref[...])
        dwple_ref[...] += _dot_tn(dpe, p_ref[...])
        dpre = d_gp * gp * (1.0 - gp)
        dwpg_ref[...] += _dot_tn(h3, dpre)
        dh3 = _dot_nt(dpre, wpg_ref[...])
        dx2n, dgpg = _rms_bwd(dh3, xh2, r2, gpg_ref[...])
        dx2_ref[...] = dx3 + dx2n
        vec_ref[0:1, :] += _rowsum(dpre)
        vec_ref[1:2, :] += _rowsum(dgpg)
        vec_ref[2:3, :] += _rowsum(dgple)
        vec_ref[3:4, :] += _rowsum(dgfin)
        _store_on_last([(dwpg_ref, dwpg_out), (dwple_ref, dwple_out)])

    return pl.pallas_call(
        body, name="tail_fwd_bwd", grid=(t // tm,),
        out_shape=[jax.ShapeDtypeStruct((t, d), F32), jax.ShapeDtypeStruct((SUBLANES, LANES), F32),
                   jax.ShapeDtypeStruct((d, d), BF16), jax.ShapeDtypeStruct((d, pd), BF16),
                   jax.ShapeDtypeStruct((SUBLANES, d), F32)],
        in_specs=[_rows(tm, d), _rows(tm, pd), _rows(tm, d), _whole((1, d)), _whole((d, d)), _whole((1, d)),
                  _whole((d, pd)), _whole((1, d)), _whole((1, d))],
        out_specs=[_rows(tm, d), _acc((SUBLANES, LANES)), _acc((d, d)), _acc((d, pd)), _acc((SUBLANES, d))],
        scratch_shapes=[pltpu.VMEM((d, d), F32), pltpu.VMEM((d, pd), F32)],
        compiler_params=_params("arbitrary"),
    )(x2, p, target, g_pg, w_pg, b_pg, w_ple_t, g_ple, g_final)


def _ffn_bwd_a(dx2, fg, fu, w_down, comm=None):
    t, d = dx2.shape
    f = fg.shape[1]
    tm = min(TOKEN_TILE, t)

    def body(dx_ref, fg_ref, fu_ref, wd_ref, dfg_ref, dfu_ref, act_ref):
        dact = _dot_nt(dx_ref[...], wd_ref[...])
        fgv = fg_ref[...].astype(F32)
        fuv = fu_ref[...].astype(F32)
        sg = _sigmoid(fgv)
        silu = fgv * sg
        dfu_ref[...] = (dact * silu).astype(dfu_ref.dtype)
        dfg_ref[...] = (dact * fuv * (sg * (1.0 + fgv * (1.0 - sg)))).astype(dfg_ref.dtype)
        act_ref[...] = (silu * fuv).astype(act_ref.dtype)

    return _run(
        body, comm, name="ffn_bwd_a", grid=(t // tm,),
        out_shape=[jax.ShapeDtypeStruct((t, f), BF16)] * 3,
        in_specs=[_rows(tm, d), _rows(tm, f), _rows(tm, f), _whole((f, d))],
        out_specs=[_rows(tm, f)] * 3,
        semantics="parallel",
    )(dx2, fg, fu, w_down)


def _ffn_bwd_b(dfg, dfu, x1, dx2, g_ffn, w_gate_t, w_up_t, comm=None):
    t, d = x1.shape
    f = dfg.shape[1]
    tm = min(TOKEN_TILE, t)

    def body(dfg_ref, dfu_ref, x_ref, dx2_ref, g_ref, wg_ref, wu_ref, dx1_ref, h2_ref, vec_ref):
        _zero_on_first(vec_ref)
        dh2 = _dot(dfg_ref[...], wg_ref[...]) + _dot(dfu_ref[...], wu_ref[...])
        xhat, r = _rms_stats(x_ref[...])
        h2_ref[...] = (xhat * g_ref[...]).astype(h2_ref.dtype)
        dxn, dg = _rms_bwd(dh2, xhat, r, g_ref[...])
        dx1_ref[...] = dx2_ref[...] + dxn
        vec_ref[0:1, :] += _rowsum(dg)

    return _run(
        body, comm, name="ffn_bwd_b", grid=(t // tm,),
        out_shape=[jax.ShapeDtypeStruct((t, d), F32), jax.ShapeDtypeStruct((t, d), BF16),
                   jax.ShapeDtypeStruct((SUBLANES, d), F32)],
        in_specs=[_rows(tm, f), _rows(tm, f), _rows(tm, d), _rows(tm, d), _whole((1, d)), _whole((f, d)), _whole((f, d))],
        out_specs=[_rows(tm, d), _rows(tm, d), _acc((SUBLANES, d))],
        semantics="arbitrary",
    )(dfg, dfu, x1, dx2, g_ffn, w_gate_t, w_up_t)


def _matmul_tn(a, b, tn, name, dtype=F32, comm=None):
    t, k = a.shape
    n = b.shape[1]

    def body(a_ref, b_ref, o_ref):
        o_ref[...] = _dot_tn(a_ref[...], b_ref[...]).astype(o_ref.dtype)

    return _run(
        body, comm, name=name, grid=(n // tn,),
        out_shape=jax.ShapeDtypeStruct((k, n), dtype),
        in_specs=[_whole((t, k)), pl.BlockSpec((t, tn), lambda j: (0, j))],
        out_specs=pl.BlockSpec((k, tn), lambda j: (0, j)),
        semantics="parallel",
    )(a, b)


def _merge_bwd(dx1, merged, ma, mb, za, zb, y_a, h, w_o, w_a_out_t, w_b_out, comm=None):
    t, d = dx1.shape
    sa, lw = y_a.shape[1], h.shape[1]
    tm = min(TOKEN_TILE, t)

    def body(dx1_ref, mg_ref, ma_ref, mb_ref, za_ref, zb_ref, ya_ref, h_ref, wo_ref, wa_ref, wb_ref,
             dza_ref, dzb_ref, dya_ref, dyb_ref, dwo_out, dwa_out, dwb_out, dwo_ref, dwa_ref, dwb_ref):
        _zero_on_first(dwo_ref, dwa_ref, dwb_ref)
        dx1v = dx1_ref[...].astype(BF16)
        dmg = _dot_nt(dx1v, wo_ref[...])
        ga = _sigmoid(za_ref[...])
        gb = _sigmoid(zb_ref[...])
        dza_ref[...] = (dmg * ma_ref[...].astype(F32) * ga * (1.0 - ga)).astype(dza_ref.dtype)
        dzb_ref[...] = (dmg * mb_ref[...].astype(F32) * gb * (1.0 - gb)).astype(dzb_ref.dtype)
        dma = (dmg * ga).astype(BF16)
        dmb = (dmg * gb).astype(BF16)
        dya_ref[...] = _dot(dma, wa_ref[...])
        dyb_ref[...] = _dot_nt(dmb, wb_ref[...])
        dwo_ref[...] += _dot_tn(mg_ref[...], dx1v)
        dwa_ref[...] += _dot_tn(dma, ya_ref[...])
        dwb_ref[...] += _dot_tn(h_ref[...], dmb)
        _store_on_last([(dwo_ref, dwo_out), (dwa_ref, dwa_out), (dwb_ref, dwb_out)])

    return _run(
        body, comm, name="merge_bwd", grid=(t // tm,),
        out_shape=[jax.ShapeDtypeStruct((t, d), BF16), jax.ShapeDtypeStruct((t, d), BF16),
                   jax.ShapeDtypeStruct((t, sa), F32), jax.ShapeDtypeStruct((t, lw), F32),
                   jax.ShapeDtypeStruct((d, d), BF16), jax.ShapeDtypeStruct((d, sa), BF16),
                   jax.ShapeDtypeStruct((lw, d), BF16)],
        in_specs=[_rows(tm, d), _rows(tm, d), _rows(tm, d), _rows(tm, d), _rows(tm, d), _rows(tm, d),
                  _rows(tm, sa), _rows(tm, lw), _whole((d, d)), _whole((d, sa)), _whole((lw, d))],
        out_specs=[_rows(tm, d), _rows(tm, d), _rows(tm, sa), _rows(tm, lw), _acc((d, d)), _acc((d, sa)), _acc((lw, d))],
        scratch_shapes=[pltpu.VMEM((d, d), F32), pltpu.VMEM((d, sa), F32), pltpu.VMEM((lw, d), F32)],
        semantics="arbitrary",
    )(dx1, merged, ma, mb, za, zb, y_a, h, w_o, w_a_out_t, w_b_out)


def _fold_diag_blocks(dense, row_group, col_group, row0=0, col0=0):
    r, c = dense.shape
    rows = lax.broadcasted_iota(jnp.int32, (r, c), 0) + row0
    cols = lax.broadcasted_iota(jnp.int32, (r, c), 1) + col0
    kept = jnp.where(rows // row_group == cols // col_group, dense, 0.0)
    pick = (lax.broadcasted_iota(jnp.int32, (row_group, r), 0)
            == lax.broadcasted_iota(jnp.int32, (row_group, r), 1) % row_group).astype(F32)
    return jnp.dot(pick, kept, preferred_element_type=F32, precision=lax.Precision.HIGHEST)


def _lru_bwd(dh, xc, hprev, u, conv_w, wr_blk, b_r, wi_blk, b_i, lru_lambda, head_dim, comm=None):
    t, w = dh.shape
    tc = min(TIME_CHUNK, t)
    steps = t // tc
    halo = SUBLANES
    sub_per_chunk = tc // halo
    slabs = w // LRU_SLAB

    def body(dh_ref, xc_ref, hp_ref, u_ref, uh_ref, cw_ref, wr_ref, br_ref, wi_ref, bi_ref, lam_ref,
             du_ref, dwr_out, dwi_out, vec_ref, lam_s, a_s, dxc_s, uext_s, carry_s, dwr_ref, dwi_ref):
        chunk = steps - 1 - pl.program_id(0)

        @pl.when(pl.program_id(0) == 0)
        def _():
            carry_s[...] = jnp.zeros_like(carry_s)
            dxc_s[tc:tc + halo, :] = jnp.zeros((halo, w), F32)
            dwr_ref[...] = jnp.zeros_like(dwr_ref)
            dwi_ref[...] = jnp.zeros_like(dwi_ref)
            vec_ref[...] = jnp.zeros_like(vec_ref)

        xc = xc_ref[...]
        r, ig, sp, log_a = _lru_gates(xc, wr_ref, br_ref, wi_ref, bi_ref, lam_ref)
        a = jnp.exp(log_a)
        a_s[...] = a

        def step(i, q):
            at = pl.ds(tc - 1 - i, 1)
            lam_row = dh_ref[at, :] + q
            lam_s[at, :] = lam_row
            return a_s[at, :] * lam_row

        carry_s[0:1, :] = lax.fori_loop(0, tc, step, carry_s[0:1, :], unroll=8)
        lam = lam_s[...]
        mult = jnp.sqrt(-_expm1(2.0 * log_a))
        d_log_a = lam * hp_ref[...] * a - (lam * ig * xc) * (a * a) / mult
        d_ig = lam * mult * xc
        dpre_r = (d_log_a * (-LRU_C * sp)) * r * (1.0 - r)
        dpre_i = d_ig * ig * (1.0 - ig)
        dxc = lam * mult * ig + _slab_dot(dpre_r, wr_ref, transposed=True) + _slab_dot(dpre_i, wi_ref, transposed=True)
        xcb, drb, dib = xc.astype(BF16), dpre_r.astype(BF16), dpre_i.astype(BF16)
        for j in range(slabs):
            cols = slice(j * LRU_SLAB, (j + 1) * LRU_SLAB)
            dwr_ref[j] += _dot_tn(drb[:, cols], xcb[:, cols])
            dwi_ref[j] += _dot_tn(dib[:, cols], xcb[:, cols])
        vec_ref[0:1, :] += _rowsum(dxc)
        vec_ref[1:2, :] += _rowsum(dpre_r)
        vec_ref[2:3, :] += _rowsum(dpre_i)
        vec_ref[3:4, :] += _rowsum(d_log_a * (-LRU_C * r)) * (-_sigmoid(-lam_ref[...]))
        dxc_s[0:tc, :] = dxc
        du = cw_ref[CONV_WIDTH - 1:CONV_WIDTH, :] * dxc
        for k in range(CONV_WIDTH - 1):
            off = CONV_WIDTH - 1 - k
            du = du + cw_ref[k:k + 1, :] * dxc_s[off:off + tc, :]
        du_ref[...] = du.astype(du_ref.dtype)
        dxc_s[tc:tc + halo, :] = dxc_s[0:halo, :]
        uext_s[0:halo, :] = jnp.where(chunk > 0, uh_ref[...], 0.0)
        uext_s[halo:halo + tc, :] = u_ref[...]
        for k in range(CONV_WIDTH):
            off = halo - (CONV_WIDTH - 1) + k
            vec_ref[4 + k:5 + k, :] += _rowsum(dxc * uext_s[off:off + tc, :])

        @pl.when(pl.program_id(0) == steps - 1)
        def _():
            for j in range(slabs):
                cols = slice(j * LRU_SLAB, (j + 1) * LRU_SLAB)
                dwr_out[:, cols] = _fold_diag_blocks(dwr_ref[j], head_dim, head_dim).astype(dwr_out.dtype)
                dwi_out[:, cols] = _fold_diag_blocks(dwi_ref[j], head_dim, head_dim).astype(dwi_out.dtype)

    halo_spec = pl.BlockSpec((halo, w), lambda i: (jnp.maximum((steps - 1 - i) * sub_per_chunk - 1, 0), 0))
    return _run(
        body, comm, name="lru_bwd", grid=(steps,),
        out_shape=[jax.ShapeDtypeStruct((t, w), BF16), jax.ShapeDtypeStruct((head_dim, w), BF16),
                   jax.ShapeDtypeStruct((head_dim, w), BF16), jax.ShapeDtypeStruct((SUBLANES, w), F32)],
        in_specs=[_rows_rev(tc, w, steps)] * 4 + [halo_spec, _whole((CONV_WIDTH, w)), _whole(wr_blk.shape),
                                                  _whole((1, w)), _whole(wi_blk.shape), _whole((1, w)), _whole((1, w))],
        out_specs=[_rows_rev(tc, w, steps), _acc((head_dim, w)), _acc((head_dim, w)), _acc((SUBLANES, w))],
        scratch_shapes=[pltpu.VMEM((tc, w), F32), pltpu.VMEM((tc, w), F32), pltpu.VMEM((tc + halo, w), F32),
                        pltpu.VMEM((halo + tc, w), F32), pltpu.VMEM((SUBLANES, w), F32),
                        pltpu.VMEM((slabs, LRU_SLAB, LRU_SLAB), F32), pltpu.VMEM((slabs, LRU_SLAB, LRU_SLAB), F32)],
        semantics="arbitrary",
    )(dh, xc, hprev, u, u, conv_w, wr_blk, b_r, wi_blk, b_i, lru_lambda)


def _s5_bwd(dya, y, sr, si, u, w_glu, b_glu, cre_blk, cimn_blk, bbr_blk, bbi_blk, ar, ai, d_skip, comm=None):
    t, sa = dya.shape
    gn = sr.shape[1]
    ns, _, sw = bbr_blk.shape
    tc = min(TIME_CHUNK, t)
    steps = t // tc
    halo = SUBLANES

    def body(dya_ref, y_ref, sr_ref, si_ref, u_ref, wg_ref, bg_ref, cre_ref, cim_ref, bbr_ref, bbi_ref,
             ar_ref, ai_ref, d_ref, du_ref, lr_ref, li_ref, dy_ref, dwg_out, vsa_ref, vgn_ref, gr_s, gi_s, cr_s, ci_s,
             dwg_ref):
        @pl.when(pl.program_id(0) == 0)
        def _():
            cr_s[...] = jnp.zeros_like(cr_s)
            ci_s[...] = jnp.zeros_like(ci_s)
            gr_s[tc:tc + halo, :] = jnp.zeros((halo, gn), F32)
            gi_s[tc:tc + halo, :] = jnp.zeros((halo, gn), F32)
            dwg_ref[...] = jnp.zeros_like(dwg_ref)
            vsa_ref[...] = jnp.zeros_like(vsa_ref)
            vgn_ref[...] = jnp.zeros_like(vgn_ref)

        yv = y_ref[...]
        uv = u_ref[...]
        zz = _gelu(yv)
        sg = _sigmoid(_dot(zz, wg_ref[...]) + bg_ref[...])
        dyav = dya_ref[...]
        dq = dyav * zz * sg * (1.0 - sg)
        dzz = dyav * sg + _dot_nt(dq, wg_ref[...])
        dwg_ref[...] += _dot_tn(zz, dq)
        dy = dzz * _gelu_grad(yv)
        dyb = dy.astype(BF16)
        dy_ref[...] = dyb.astype(dy_ref.dtype)
        vsa_ref[0:1, :] += _rowsum(dq)
        vsa_ref[1:2, :] += _rowsum(dy * uv)
        for m in range(ns):
            dym = dyb[:, m * S5_SLAB:(m + 1) * S5_SLAB]
            gr_s[0:tc, m * sw:(m + 1) * sw] = _dot_nt(dym, cre_ref[m])
            gi_s[0:tc, m * sw:(m + 1) * sw] = _dot_nt(dym, cim_ref[m])
        a_r = ar_ref[...]
        a_i = ai_ref[...]

        def step(i, carry):
            l_r, l_i = carry
            at = pl.ds(tc - 1 - i, 1)
            n_r = gr_s[at, :] + a_r * l_r + a_i * l_i
            n_i = gi_s[at, :] + a_r * l_i - a_i * l_r
            gr_s[at, :] = n_r
            gi_s[at, :] = n_i
            return n_r, n_i

        l_r, l_i = lax.fori_loop(0, tc, step, (cr_s[0:1, :], ci_s[0:1, :]), unroll=8)
        cr_s[0:1, :] = l_r
        ci_s[0:1, :] = l_i
        nxt_r = gr_s[1:tc + 1, :]
        nxt_i = gi_s[1:tc + 1, :]
        srv = sr_ref[...].astype(F32)
        siv = si_ref[...].astype(F32)
        vgn_ref[0:1, :] += _rowsum(nxt_r * srv + nxt_i * siv)
        vgn_ref[1:2, :] += _rowsum(nxt_i * srv - nxt_r * siv)
        lam_r = gr_s[0:tc, :]
        lam_i = gi_s[0:tc, :]
        gr_s[tc:tc + halo, :] = gr_s[0:halo, :]
        gi_s[tc:tc + halo, :] = gi_s[0:halo, :]
        lrb = lam_r.astype(BF16)
        lib = lam_i.astype(BF16)
        lr_ref[...] = lrb.astype(lr_ref.dtype)
        li_ref[...] = lib.astype(li_ref.dtype)
        for m in range(ns):
            states, chans = slice(m * sw, (m + 1) * sw), slice(m * S5_SLAB, (m + 1) * S5_SLAB)
            du_ref[:, chans] = (_dot_nt(lrb[:, states], bbr_ref[m]) + _dot_nt(lib[:, states], bbi_ref[m])
                                + dy[:, chans] * d_ref[:, chans]).astype(du_ref.dtype)
        _store_on_last([(dwg_ref, dwg_out)])

    return _run(
        body, comm, name="s5_bwd", grid=(steps,),
        out_shape=[jax.ShapeDtypeStruct((t, sa), BF16), jax.ShapeDtypeStruct((t, gn), BF16),
                   jax.ShapeDtypeStruct((t, gn), BF16), jax.ShapeDtypeStruct((t, sa), BF16),
                   jax.ShapeDtypeStruct((sa, sa), BF16), jax.ShapeDtypeStruct((SUBLANES, sa), F32),
                   jax.ShapeDtypeStruct((SUBLANES, gn), F32)],
        in_specs=[_rows_rev(tc, sa, steps), _rows_rev(tc, sa, steps), _rows_rev(tc, gn, steps), _rows_rev(tc, gn, steps),
                  _rows_rev(tc, sa, steps), _whole((sa, sa)), _whole((1, sa)), _whole(cre_blk.shape),
                  _whole(cimn_blk.shape), _whole(bbr_blk.shape), _whole(bbi_blk.shape), _whole((1, gn)), _whole((1, gn)),
                  _whole((1, sa))],
        out_specs=[_rows_rev(tc, sa, steps), _rows_rev(tc, gn, steps), _rows_rev(tc, gn, steps), _rows_rev(tc, sa, steps),
                   _acc((sa, sa)), _acc((SUBLANES, sa)), _acc((SUBLANES, gn))],
        scratch_shapes=[pltpu.VMEM((tc + halo, gn), F32), pltpu.VMEM((tc + halo, gn), F32),
                        pltpu.VMEM((SUBLANES, gn), F32), pltpu.VMEM((SUBLANES, gn), F32), pltpu.VMEM((sa, sa), F32)],
        semantics="arbitrary",
    )(dya, y, sr, si, u, w_glu, b_glu, cre_blk, cimn_blk, bbr_blk, bbi_blk, ar, ai, d_skip)


def _inproj_bwd(dparts, x, dx1, g_mix, w_in_t, comm=None):
    t, d = x.shape
    n = w_in_t.shape[0]
    widths = [p.shape[1] for p in dparts]
    offs = [sum(widths[:i]) for i in range(len(widths) + 1)]
    tm = min(TOKEN_TILE, t)
    np_ = len(dparts)

    def body(*refs):
        dz_refs = refs[:np_]
        x_ref, dx1_ref, g_ref, w_ref, gx_ref, h_ref, vd_ref, vn_ref = refs[np_:]
        _zero_on_first(vd_ref, vn_ref)
        dh = jnp.zeros((tm, d), F32)
        for k, r in enumerate(dz_refs):
            lo, hi = offs[k], offs[k + 1]
            dzk = r[...]
            dh = dh + _dot(dzk, w_ref[lo:hi, :])
            vn_ref[0:1, lo:hi] += _rowsum(dzk.astype(F32))
        xhat, r0 = _rms_stats(x_ref[...])
        h_ref[...] = (xhat * g_ref[...]).astype(h_ref.dtype)
        dxn, dg = _rms_bwd(dh, xhat, r0, g_ref[...])
        gx_ref[...] = dx1_ref[...] + dxn
        vd_ref[0:1, :] += _rowsum(dg)

    return _run(
        body, comm, name="inproj_bwd", grid=(t // tm,),
        out_shape=[jax.ShapeDtypeStruct((t, d), F32), jax.ShapeDtypeStruct((t, d), BF16),
                   jax.ShapeDtypeStruct((SUBLANES, d), F32), jax.ShapeDtypeStruct((SUBLANES, n), F32)],
        in_specs=[_rows(tm, w) for w in widths] + [_rows(tm, d), _rows(tm, d), _whole((1, d)), _whole((n, d))],
        out_specs=[_rows(tm, d), _rows(tm, d), _acc((SUBLANES, d)), _acc((SUBLANES, n))],
        semantics="arbitrary",
    )(*dparts, x, dx1, g_mix, w_in_t)


def _dw_from_parts(dparts, h, tn, name, comm=None):
    t, d = h.shape
    widths = [p.shape[1] for p in dparts]
    offs = [sum(widths[:i]) for i in range(len(widths) + 1)]
    np_ = len(dparts)

    def body(*refs):
        h_ref, o_ref = refs[np_], refs[np_ + 1]
        hv = h_ref[...]
        for k, r in enumerate(refs[:np_]):
            o_ref[offs[k]:offs[k + 1], :] = _dot_tn(r[...], hv).astype(o_ref.dtype)

    return _run(
        body, comm, name=name, grid=(d // tn,),
        out_shape=jax.ShapeDtypeStruct((offs[-1], d), BF16),
        in_specs=[_whole(p.shape) for p in dparts] + [pl.BlockSpec((t, tn), lambda j: (0, j))],
        out_specs=pl.BlockSpec((offs[-1], tn), lambda j: (0, j)),
        semantics="parallel",
    )(*dparts, h)


def _prep(lr_row, li_row, ldt_row, lr_col, li_col, ldt_col, b_re, b_im, c_re, c_im, w_r, w_i, comm=None):
    gn, pch = b_re.shape
    sa, n = c_re.shape
    w, hd = w_r.shape
    ns, sw, lsl = sa // S5_SLAB, S5_SLAB * n // pch, w // LRU_SLAB

    def spread_cols(vals, row_group, col_group, width):
        r, k = vals.shape
        tile = (lax.broadcasted_iota(jnp.int32, (k, width), 0) == lax.broadcasted_iota(jnp.int32, (k, width), 1) % k)
        rows = lax.broadcasted_iota(jnp.int32, (r, width), 0) // row_group
        cols = lax.broadcasted_iota(jnp.int32, (r, width), 1) // col_group
        return jnp.where(rows == cols, _dot(vals, tile.astype(BF16)), 0.0)

    def spread_rows(vals, row_group, col_group, height):
        k, c = vals.shape
        tile = (lax.broadcasted_iota(jnp.int32, (height, k), 0) % k == lax.broadcasted_iota(jnp.int32, (height, k), 1))
        rows = lax.broadcasted_iota(jnp.int32, (height, c), 0) // row_group
        cols = lax.broadcasted_iota(jnp.int32, (height, c), 1) // col_group
        return jnp.where(rows == cols, _dot(tile.astype(BF16), vals), 0.0)

    def body(lrr, lir, ldr, lrc, lic, ldc, bre, bim, cre, cim, wr, wi,
             ar_o, ai_o, bbr_o, bbi_o, cre_o, cim_o, wr_o, wi_o):
        ar, ai, _, _ = _disc_scalars(lrr[...], lir[...], ldr[...])
        ar_o[...] = ar
        ai_o[...] = ai
        _, _, bbr, bbi = _disc_cols(lrc[...], lic[...], ldc[...], bre[...], bim[...])
        bbr_t, bbi_t = bbr.T, bbi.T
        for m in range(ns):
            bbr_o[m] = spread_rows(bbr_t[:, m * sw:(m + 1) * sw], pch, n, S5_SLAB).astype(bbr_o.dtype)
            bbi_o[m] = spread_rows(bbi_t[:, m * sw:(m + 1) * sw], pch, n, S5_SLAB).astype(bbi_o.dtype)
            rows = slice(m * S5_SLAB, (m + 1) * S5_SLAB)
            cre_o[m] = spread_rows(cre[rows, :].T, n, pch, sw).astype(cre_o.dtype)
            cim_o[m] = spread_rows(-cim[rows, :].T, n, pch, sw).astype(cim_o.dtype)
        for j in range(lsl):
            rows = slice(j * LRU_SLAB, (j + 1) * LRU_SLAB)
            wr_o[j] = spread_cols(wr[rows, :], hd, hd, LRU_SLAB).astype(wr_o.dtype)
            wi_o[j] = spread_cols(wi[rows, :], hd, hd, LRU_SLAB).astype(wi_o.dtype)

    args = (lr_row, li_row, ldt_row, lr_col, li_col, ldt_col, b_re, b_im, c_re, c_im, w_r, w_i)
    out_shape = [jax.ShapeDtypeStruct((1, gn), F32), jax.ShapeDtypeStruct((1, gn), F32),
                 jax.ShapeDtypeStruct((ns, S5_SLAB, sw), BF16), jax.ShapeDtypeStruct((ns, S5_SLAB, sw), BF16),
                 jax.ShapeDtypeStruct((ns, sw, S5_SLAB), BF16), jax.ShapeDtypeStruct((ns, sw, S5_SLAB), BF16),
                 jax.ShapeDtypeStruct((lsl, LRU_SLAB, LRU_SLAB), BF16),
                 jax.ShapeDtypeStruct((lsl, LRU_SLAB, LRU_SLAB), BF16)]
    return _run(
        body, comm, name="prep", grid=(1,), out_shape=out_shape, in_specs=[_whole(a.shape) for a in args],
        out_specs=[_acc(s.shape) for s in out_shape], semantics="arbitrary",
    )(*args)


def _s5_param_grads(lam_r, lam_i, sr, si, u, dy, pch, n, comm=None):
    t, gn = lam_r.shape
    sa = u.shape[1]
    sw = S5_SLAB * n // pch

    def body(lr_ref, li_ref, sr_ref, si_ref, u_ref, dy_ref, dbr_ref, dbi_ref, dcr_ref, dci_ref):
        uv = u_ref[...]
        dyv = dy_ref[...]
        dbr_ref[...] = _fold_diag_blocks(_dot_tn(uv, lr_ref[...]), pch, n).astype(dbr_ref.dtype)
        dbi_ref[...] = _fold_diag_blocks(_dot_tn(uv, li_ref[...]), pch, n).astype(dbi_ref.dtype)
        dcr_ref[...] = _fold_diag_blocks(_dot_tn(sr_ref[...], dyv), n, pch).astype(dcr_ref.dtype)
        dci_ref[...] = _fold_diag_blocks(_dot_tn(si_ref[...], dyv), n, pch).astype(dci_ref.dtype)

    states = pl.BlockSpec((t, sw), lambda m: (0, m))
    chans = pl.BlockSpec((t, S5_SLAB), lambda m: (0, m))
    return _run(
        body, comm, name="s5_param_grads", grid=(sa // S5_SLAB,),
        out_shape=[jax.ShapeDtypeStruct((pch, gn), BF16), jax.ShapeDtypeStruct((pch, gn), BF16),
                   jax.ShapeDtypeStruct((n, sa), BF16), jax.ShapeDtypeStruct((n, sa), BF16)],
        in_specs=[states, states, states, states, chans, chans],
        out_specs=[pl.BlockSpec((pch, sw), lambda m: (0, m)), pl.BlockSpec((pch, sw), lambda m: (0, m)),
                   pl.BlockSpec((n, S5_SLAB), lambda m: (0, m)), pl.BlockSpec((n, S5_SLAB), lambda m: (0, m))],
        semantics="parallel",
    )(lam_r, lam_i, sr, si, u, dy)


SMALL_PARTS = ["vec_tail", "vec_ffn", "vec_lru", "vec_mix", "vec_bin", "vec_sa", "vec_gn", "dw_r", "dw_i", "dbb_re",
               "dbb_im", "dc_re", "dc_imn", "loss"]


def _small_reduce(parts, shapes, lr_col, li_col, ldt_col, b_re, b_im, groups, comm=None):
    gn, pch = b_re.shape
    n = gn // groups
    nparts = parts[SMALL_PARTS[0]].size // math.prod(shapes[SMALL_PARTS[0]])
    np_, nout = len(SMALL_PARTS), 24

    def body(*refs):
        ins = refs[:np_]
        lr, li, ldt, bre, bim = refs[np_:np_ + 5]
        outs = refs[np_ + 5:np_ + 5 + nout]
        sums = dict(zip(SMALL_PARTS, refs[np_ + 5 + nout:]))

        @pl.when(pl.program_id(0) == 0)
        def _():
            for k, r in zip(SMALL_PARTS, ins):
                sums[k][...] = r[...].astype(F32)

        @pl.when(pl.program_id(0) > 0)
        def _():
            for k, r in zip(SMALL_PARTS, ins):
                sums[k][...] += r[...].astype(F32)

        @pl.when(pl.program_id(0) == nparts - 1)
        def _():
            finish({k: s[...] for k, s in sums.items()}, lr, li, ldt, bre, bim, *outs)

    def finish(tot, lr, li, ldt, bre, bim, o_loss, o_gmix, o_bin, o_bglu, o_s5d, o_convb, o_br, o_bi, o_lam, o_gffn,
               o_gpg, o_bpg, o_gple, o_gfin, o_wr, o_wi, o_cre, o_cim, o_bre, o_bim, o_lre, o_lim, o_ldt, o_convw):
        o_loss[...] = tot["loss"]
        o_convw[...] = tot["vec_lru"][SUBLANES - CONV_WIDTH:SUBLANES]
        o_bpg[...] = tot["vec_tail"][0:1]
        o_gpg[...] = tot["vec_tail"][1:2]
        o_gple[...] = tot["vec_tail"][2:3]
        o_gfin[...] = tot["vec_tail"][3:4]
        o_gffn[...] = tot["vec_ffn"][0:1]
        o_convb[...] = tot["vec_lru"][0:1]
        o_br[...] = tot["vec_lru"][1:2]
        o_bi[...] = tot["vec_lru"][2:3]
        o_lam[...] = tot["vec_lru"][3:4]
        o_gmix[...] = tot["vec_mix"][0:1]
        o_bin[...] = tot["vec_bin"][0:1]
        o_bglu[...] = tot["vec_sa"][0:1]
        o_s5d[...] = tot["vec_sa"][1:2]
        o_wr[...] = tot["dw_r"].T
        o_wi[...] = tot["dw_i"].T
        o_cre[...] = tot["dc_re"].T
        o_cim[...] = -tot["dc_imn"].T
        d_a = tot["vec_gn"].T
        _, chain = jax.vjp(_disc_cols, lr[...], li[...], ldt[...], bre[...], bim[...])
        d_lr, d_li, d_ldt, d_bre, d_bim = chain((d_a[:, 0:1], d_a[:, 1:2], tot["dbb_re"].T, tot["dbb_im"].T))
        o_lre[...] = d_lr
        o_lim[...] = d_li
        o_bre[...] = d_bre
        o_bim[...] = d_bim
        same = (lax.broadcasted_iota(jnp.int32, (groups, gn), 0)
                == lax.broadcasted_iota(jnp.int32, (groups, gn), 1) // n).astype(F32)
        o_ldt[...] = jnp.dot(same, d_ldt * jnp.ones((1, LANES), F32), preferred_element_type=F32,
                             precision=lax.Precision.HIGHEST)[:, 0:1]

    d = shapes["vec_mix"][1]
    nz = shapes["vec_bin"][1]
    sa = shapes["vec_sa"][1]
    w = shapes["vec_lru"][1]
    row = lambda c: jax.ShapeDtypeStruct((1, c), F32)
    out_shape = [jax.ShapeDtypeStruct(shapes["loss"], F32), row(d), row(nz), row(sa), row(sa), row(w), row(w), row(w),
                 row(w), row(d), row(d), row(d), row(d), row(d),
                 jax.ShapeDtypeStruct(shapes["dw_r"][::-1], F32), jax.ShapeDtypeStruct(shapes["dw_i"][::-1], F32),
                 jax.ShapeDtypeStruct(shapes["dc_re"][::-1], F32), jax.ShapeDtypeStruct(shapes["dc_imn"][::-1], F32),
                 jax.ShapeDtypeStruct((gn, pch), F32), jax.ShapeDtypeStruct((gn, pch), F32),
                 jax.ShapeDtypeStruct((gn, 1), F32), jax.ShapeDtypeStruct((gn, 1), F32),
                 jax.ShapeDtypeStruct((groups, 1), F32), jax.ShapeDtypeStruct((CONV_WIDTH, w), F32)]
    def part_spec(k):
        r, c = shapes[k]
        if parts[k].ndim == 3:
            return pl.BlockSpec((None, r, c), lambda i: (i, 0, 0))
        return pl.BlockSpec((r, c), lambda i: (i, 0))

    outs = _run(
        body, comm, name="small_reduce", grid=(nparts,), out_shape=out_shape,
        in_specs=[part_spec(k) for k in SMALL_PARTS] + [_whole(a.shape) for a in (lr_col, li_col, ldt_col, b_re, b_im)],
        out_specs=[_acc(s.shape) for s in out_shape],
        scratch_shapes=[pltpu.VMEM(shapes[k], F32) for k in SMALL_PARTS],
        semantics="arbitrary",
    )(*[parts[k] for k in SMALL_PARTS], lr_col, li_col, ldt_col, b_re, b_im)
    extra = None
    if comm is not None:
        outs, extra = outs
    names = ["loss", "g_mix", "b_in", "b_glu", "s5_d", "conv_b", "b_r", "b_i", "lru_lambda", "g_ffn", "g_ple_gate",
             "b_ple_gate", "g_ple", "g_final", "w_r", "w_i", "s5_c_re", "s5_c_im", "s5_b_re", "s5_b_im", "lam_re",
             "lam_im", "log_dt", "conv_w"]
    res = dict(zip(names, outs))
    return res if comm is None else (res, extra)


def _adamw_small(ws, gs, ms, vs):
    n = len(ws)

    def body(*refs):
        w_r, g_r, m_r, v_r = (refs[i * n:(i + 1) * n] for i in range(4))
        g_o, d_o, m_o, v_o = (refs[(4 + i) * n:(5 + i) * n] for i in range(4))
        for i in range(n):
            g = g_r[i][...]
            delta, m_new, v_new = _adamw_math(w_r[i][...], g, m_r[i][...], v_r[i][...])
            g_o[i][...] = g
            d_o[i][...] = delta
            m_o[i][...] = m_new
            v_o[i][...] = v_new

    shapes = [jax.ShapeDtypeStruct(a.shape, F32) for a in ws]
    outs = pl.pallas_call(body, name="adamw_small", out_shape=shapes * 4)(*ws, *gs, *ms, *vs)
    return outs[:n], outs[n:2 * n], outs[2 * n:3 * n], outs[3 * n:]


def _adamw_math(w, g, m, v):
    m_new = ADAM_B1 * m + (1.0 - ADAM_B1) * g
    v_new = ADAM_B2 * v + (1.0 - ADAM_B2) * (g * g)
    m_hat = m_new / (1.0 - ADAM_B1 ** ADAM_STEP)
    v_hat = v_new / (1.0 - ADAM_B2 ** ADAM_STEP)
    delta = -ADAM_LR * (m_hat / (jnp.sqrt(v_hat) + ADAM_EPS) + ADAM_WD * w)
    return delta, m_new, v_new


def _row_tile(rows):
    for cand in range(256, 0, -16):
        if rows % cand == 0:
            return cand
    return rows


def _adamw(parts, w, m, v, name, transposed=False):
    nw = len(w)
    rows, cols = w[0].shape
    npart = parts[0].shape[0]
    tr = _row_tile(rows)
    if transposed:
        parts_spec = pl.BlockSpec((npart, cols, tr), lambda i: (0, 0, i))
    else:
        parts_spec = pl.BlockSpec((npart, tr, cols), lambda i: (0, i, 0))

    def body(*refs):
        for i in range(nw):
            p_ref, w_ref, m_ref, v_ref = refs[4 * i:4 * i + 4]
            g_ref, d_ref, mo_ref, vo_ref = refs[4 * (nw + i):4 * (nw + i) + 4]
            g = p_ref[0].astype(F32)
            for k in range(1, npart):
                g = g + p_ref[k].astype(F32)
            if transposed:
                g = g.T
            delta, m_new, v_new = _adamw_math(w_ref[...], g, m_ref[...], v_ref[...])
            g_ref[...] = g
            d_ref[...] = delta
            mo_ref[...] = m_new
            vo_ref[...] = v_new

    res = pl.pallas_call(
        body, name=name, grid=(rows // tr,),
        out_shape=[jax.ShapeDtypeStruct((rows, cols), F32)] * (4 * nw),
        in_specs=([parts_spec] + [_rows(tr, cols)] * 3) * nw,
        out_specs=[_rows(tr, cols)] * (4 * nw),
        compiler_params=_params("parallel"),
    )(*[a for group in zip(parts, w, m, v) for a in group])
    return [res[4 * i:4 * i + 4] for i in range(nw)]


def _mesh_position():
    return lax.axis_index("x"), lax.axis_index("y"), lax.axis_index("c")


def _flip(pos, rel):
    x, y, c = pos
    return (1 - x if rel & 4 else x, 1 - y if rel & 2 else y, 1 - c if rel & 1 else c)


def _index(pos):
    return 4 * pos[0] + 2 * pos[1] + pos[2]


_ANY = pl.BlockSpec(memory_space=pl.ANY)
FLAT_ROWS = 32


def _dma_sems(n):
    return [pltpu.SemaphoreType.DMA((n, N_DEV - 1)), pltpu.SemaphoreType.DMA((n, N_DEV - 1)), pltpu.SemaphoreType.DMA((n,))]


def _block_of(ref, idx, rows, flat):
    if flat:
        return ref.at[pl.ds(pl.multiple_of(idx * rows, FLAT_ROWS), rows), :]
    return ref.at[idx]


class _Gather:
    chips = (4, 2, 6)
    rels = frozenset((1, 4, 2, 6))

    def __init__(self, shards):
        self.inputs = list(shards)
        self.flat = [s.shape[0] % FLAT_ROWS == 0 for s in shards]
        self.out_shape = [
            jax.ShapeDtypeStruct((N_DEV * s.shape[0], s.shape[1]) if f else (N_DEV,) + s.shape, s.dtype)
            for s, f in zip(shards, self.flat)]
        self.sems = _dma_sems(len(shards))

    def _copy(self, ins, outs, sems, i, k, block, to, own=False):
        dst = _block_of(outs[i], _index(block), self.inputs[i].shape[0], self.flat[i])
        return pltpu.make_async_remote_copy(
            src_ref=ins[i] if own else dst, dst_ref=dst, send_sem=sems[0].at[i, k], recv_sem=sems[1].at[i, k],
            device_id=to, device_id_type=MESH)

    def _local(self, ins, outs, sems, i, me):
        dst = _block_of(outs[i], _index(me), self.inputs[i].shape[0], self.flat[i])
        return pltpu.make_async_copy(ins[i], dst, sems[2].at[i])

    def _first(self, ins, outs, sems, i, me):
        cps = [self._copy(ins, outs, sems, i, 0, me, _flip(me, 1), own=True)]
        cps += [self._copy(ins, outs, sems, i, 1 + j, me, _flip(me, rel), own=True) for j, rel in enumerate(self.chips)]
        return cps

    def _passed(self, ins, outs, sems, i, j, me):
        return self._copy(ins, outs, sems, i, 4 + j, _flip(me, self.chips[j]), _flip(me, 1))

    def before(self, ins, outs, sems):
        n = len(self.inputs)
        me = _mesh_position()

        @pl.when(pl.program_id(0) == 0)
        def _():
            for i in range(n):
                self._local(ins, outs, sems, i, me).start()
                for cp in self._first(ins, outs, sems, i, me):
                    cp.start()

        @pl.when(pl.program_id(0) == pl.num_programs(0) - 1)
        def _():
            for j, rel in enumerate(self.chips):
                for i in range(n):
                    self._copy(ins, outs, sems, i, 1 + j, _flip(me, rel), me).wait_recv()
                    self._passed(ins, outs, sems, i, j, me).start()

    def after(self, ins, outs, sems):
        n = len(self.inputs)
        me = _mesh_position()
        sibling = _flip(me, 1)

        @pl.when(pl.program_id(0) == pl.num_programs(0) - 1)
        def _():
            for i in range(n):
                self._copy(ins, outs, sems, i, 0, sibling, me).wait_recv()
                for j, rel in enumerate(self.chips):
                    self._copy(ins, outs, sems, i, 4 + j, _flip(sibling, rel), me).wait_recv()
            for i in range(n):
                for cp in self._first(ins, outs, sems, i, me):
                    cp.wait_send()
                for j in range(len(self.chips)):
                    self._passed(ins, outs, sems, i, j, me).wait_send()
                self._local(ins, outs, sems, i, me).wait()


N_CHIPS = N_DEV // 2


def _chip(pos):
    return 2 * pos[0] + pos[1]


class _PairSwap:
    rels = frozenset((1,))

    def __init__(self, arrays):
        self.inputs = list(arrays)
        self.rows = [a.shape[0] // N_DEV for a in arrays]
        for r in self.rows:
            assert r % FLAT_ROWS == 0, r
        self.out_shape = [jax.ShapeDtypeStruct((N_CHIPS, r, a.shape[1]), a.dtype) for a, r in zip(arrays, self.rows)]
        n = len(arrays)
        self.sems = [pltpu.SemaphoreType.DMA((n, N_CHIPS)), pltpu.SemaphoreType.DMA((n, N_CHIPS))]

    def _copy(self, ins, outs, sems, i, j, me):
        sibling = _flip(me, 1)
        return pltpu.make_async_remote_copy(
            src_ref=_block_of(ins[i], 2 * j + sibling[2], self.rows[i], True), dst_ref=outs[i].at[j],
            send_sem=sems[0].at[i, j], recv_sem=sems[1].at[i, j], device_id=sibling, device_id_type=MESH)

    def before(self, ins, outs, sems):
        me = _mesh_position()

        @pl.when(pl.program_id(0) == 0)
        def _():
            for i in range(len(self.inputs)):
                for j in range(N_CHIPS):
                    self._copy(ins, outs, sems, i, j, me).start()

    def after(self, ins, outs, sems):
        me = _mesh_position()

        @pl.when(pl.program_id(0) == pl.num_programs(0) - 1)
        def _():
            for i in range(len(self.inputs)):
                for j in range(N_CHIPS):
                    self._copy(ins, outs, sems, i, j, me).wait()


class _ChipExchange:
    chips = (4, 2, 6)
    rels = frozenset(chips)

    def __init__(self, arrays):
        self.inputs = list(arrays)
        self.out_shape = [jax.ShapeDtypeStruct(a.shape, a.dtype) for a in arrays]
        n = len(arrays)
        self.sems = [pltpu.SemaphoreType.DMA((n, 3)), pltpu.SemaphoreType.DMA((n, 3)), pltpu.SemaphoreType.DMA((n,))]

    def _send(self, ins, outs, sems, i, k, me):
        peer = _flip(me, self.chips[k])
        return pltpu.make_async_remote_copy(
            src_ref=ins[i].at[_chip(peer)], dst_ref=outs[i].at[_chip(me)], send_sem=sems[0].at[i, k],
            recv_sem=sems[1].at[i, k], device_id=peer, device_id_type=MESH)

    def _arrival(self, ins, outs, sems, i, k, me):
        peer = _flip(me, self.chips[k])
        return pltpu.make_async_remote_copy(
            src_ref=ins[i].at[_chip(me)], dst_ref=outs[i].at[_chip(peer)], send_sem=sems[0].at[i, k],
            recv_sem=sems[1].at[i, k], device_id=peer, device_id_type=MESH)

    def _local(self, ins, outs, sems, i, me):
        return pltpu.make_async_copy(ins[i].at[_chip(me)], outs[i].at[_chip(me)], sems[2].at[i])

    def before(self, ins, outs, sems):
        me = _mesh_position()

        @pl.when(pl.program_id(0) == 0)
        def _():
            for i in range(len(self.inputs)):
                self._local(ins, outs, sems, i, me).start()
                for k in range(len(self.chips)):
                    self._send(ins, outs, sems, i, k, me).start()

    def after(self, ins, outs, sems):
        me = _mesh_position()

        @pl.when(pl.program_id(0) == pl.num_programs(0) - 1)
        def _():
            for i in range(len(self.inputs)):
                for k in range(len(self.chips)):
                    self._arrival(ins, outs, sems, i, k, me).wait_recv()
            for i in range(len(self.inputs)):
                for k in range(len(self.chips)):
                    self._send(ins, outs, sems, i, k, me).wait_send()
                self._local(ins, outs, sems, i, me).wait()


def _pair_add(grads, halves, name):
    n = len(grads)

    def body(*refs):
        g_refs, h_refs, o_refs = refs[:n], refs[n:2 * n], refs[2 * n:]
        c = lax.axis_index("c")
        for i in range(n):
            r = h_refs[i].shape[1]
            for j in range(N_CHIPS):
                own = g_refs[i][pl.ds(pl.multiple_of((2 * j + c) * r, FLAT_ROWS), r), :]
                o_refs[i][j] = (own.astype(F32) + h_refs[i][j].astype(F32)).astype(o_refs[i].dtype)

    return pl.pallas_call(
        body, name=name, out_shape=[jax.ShapeDtypeStruct(h.shape, h.dtype) for h in halves],
        compiler_params=pltpu.CompilerParams(vmem_limit_bytes=VMEM_LIMIT),
    )(*grads, *halves)


class _Both:
    def __init__(self, first, second):
        self.jobs = (first, second)
        self.rels = first.rels | second.rels
        self.inputs = first.inputs + second.inputs
        self.out_shape = first.out_shape + second.out_shape
        self.sems = first.sems + second.sems

    def _each(self, ins, outs, sems):
        a = self.jobs[0]
        i, o, s = len(a.inputs), len(a.out_shape), len(a.sems)
        return ((a, ins[:i], outs[:o], sems[:s]), (self.jobs[1], ins[i:], outs[o:], sems[s:]))

    def before(self, ins, outs, sems):
        for job, i, o, s in self._each(ins, outs, sems):
            job.before(i, o, s)

    def after(self, ins, outs, sems):
        for job, i, o, s in self._each(ins, outs, sems):
            job.after(i, o, s)


_COLLECTIVE_IDS = {(1,): 0, (2, 4, 6): 1, (1, 2, 4, 6): 2}


def _entry_barrier(rels):
    @pl.when(pl.program_id(0) == 0)
    def _():
        me = _mesh_position()
        sem = pltpu.get_barrier_semaphore()
        for rel in rels:
            pl.semaphore_signal(sem, inc=1, device_id=_flip(me, rel), device_id_type=MESH)
        pl.semaphore_wait(sem, len(rels))


def _run(body, comm, *, semantics, out_shape, in_specs, out_specs, scratch_shapes=(), **kw):
    if comm is None:
        return pl.pallas_call(body, out_shape=out_shape, in_specs=in_specs, out_specs=out_specs,
                              scratch_shapes=list(scratch_shapes), compiler_params=_params(semantics), **kw)
    single = not isinstance(out_shape, (list, tuple))
    outs = [out_shape] if single else list(out_shape)
    ospecs = [out_specs] if single else list(out_specs)
    counts = [len(in_specs), len(comm.inputs), len(outs), len(comm.out_shape), len(scratch_shapes), len(comm.sems)]
    rels = tuple(sorted(comm.rels))

    def carrying(*refs):
        groups, pos = [], 0
        for c in counts:
            groups.append(refs[pos:pos + c])
            pos += c
        main_in, comm_in, main_out, comm_out, main_scratch, comm_sems = groups
        _entry_barrier(rels)
        comm.before(comm_in, comm_out, comm_sems)
        body(*main_in, *main_out, *main_scratch)
        comm.after(comm_in, comm_out, comm_sems)

    call = pl.pallas_call(
        carrying, out_shape=outs + list(comm.out_shape), in_specs=list(in_specs) + [_ANY] * len(comm.inputs),
        out_specs=ospecs + [_ANY] * len(comm.out_shape), scratch_shapes=list(scratch_shapes) + list(comm.sems),
        compiler_params=pltpu.CompilerParams(dimension_semantics=("arbitrary",), vmem_limit_bytes=VMEM_LIMIT,
                                             collective_id=_COLLECTIVE_IDS[rels]), **kw)

    def apply(*args):
        res = call(*args, *comm.inputs)
        main = res[:len(outs)]
        return (main[0] if single else list(main)), list(res[len(outs):])

    return apply


def _alone(comm, name):
    return _run(lambda: None, comm, semantics="arbitrary", name=name, grid=(1,), out_shape=[], in_specs=[], out_specs=[])()[1]


SHARDED = {"w_in": 1, "w_glu": 0, "conv_w": 1, "w_a_out": 1, "w_b_out": 0, "w_o": 0, "w_ffn_gate": 1, "w_ffn_up": 1,
           "w_ffn_down": 0, "w_ple_gate": 0, "w_ple": 1}
TRANSPOSED = ("w_in", "w_a_out", "w_ffn_gate", "w_ffn_up", "w_ple")
LONG_AXIS_MINOR = ("w_in", "w_ffn_gate", "w_ffn_up")
NARROW_LAST = ("s5_b_re", "s5_b_im", "s5_d")
ADAMW_GROUPS = (("w_in",), ("w_glu",), ("conv_w",), ("w_a_out",), ("w_b_out", "w_o", "w_ple_gate"),
                ("w_ffn_gate", "w_ffn_up"), ("w_ffn_down",), ("w_ple",))
SMALL = ["g_mix", "b_in", "lam_re", "lam_im", "log_dt", "s5_b_re", "s5_b_im", "s5_c_re", "s5_c_im", "s5_d", "b_glu",
         "conv_b", "w_r", "b_r", "w_i", "b_i", "lru_lambda", "g_ffn", "g_ple_gate", "b_ple_gate", "g_ple", "g_final"]
WEIGHTS = ["g_mix", "w_in", "b_in", "lam_re", "lam_im", "log_dt", "s5_b_re", "s5_b_im", "s5_c_re", "s5_c_im", "s5_d",
           "w_glu", "b_glu", "conv_w", "conv_b", "w_r", "b_r", "w_i", "b_i", "lru_lambda", "w_a_out", "w_b_out", "w_o",
           "g_ffn", "w_ffn_gate", "w_ffn_up", "w_ffn_down", "g_ple_gate", "w_ple_gate", "b_ple_gate", "w_ple", "g_ple",
           "g_final"]


def _unblock(gathered, axis):
    nb, r, c = gathered.shape
    if axis == 0:
        return gathered.reshape(nb * r, c)
    return jnp.transpose(gathered, (1, 0, 2)).reshape(r, nb * c)


def _disc_scalars(lr, li, ldt):
    dt = jnp.exp(ldt)
    mag = jnp.exp(lr * dt)
    ar = mag * jnp.cos(li * dt)
    ai = mag * jnp.sin(li * dt)
    den = lr * lr + li * li
    nr = ar - 1.0
    fr = (nr * lr + ai * li) / den
    fi = (ai * lr - nr * li) / den
    return ar, ai, fr, fi


def _disc_cols(lr, li, ldt, b_re, b_im):
    ar, ai, fr, fi = _disc_scalars(lr, li, ldt)
    return ar, ai, fr * b_re - fi * b_im, fr * b_im + fi * b_re


def _local_step(x, p, target, src, small, disc, distributed=True):
    full = {} if distributed else dict(src)
    gw, halves, pairs, got = {}, {}, {}, {}

    def gather(keys):
        return (_Gather([src[k] for k in keys]), keys, full) if distributed else None

    def swap(keys):
        return (_PairSwap([gw[k] for k in keys]), keys, halves) if distributed else None

    def chips(keys):
        return (_ChipExchange([pairs[k] for k in keys]), keys, got) if distributed else None

    def add_pairs(keys):
        if distributed:
            pairs.update(zip(keys, _pair_add([gw[k] for k in keys], [halves[k] for k in keys], "pair_add_" + keys[0])))

    def carry(fn, *args, jobs=()):
        jobs = [j for j in jobs if j is not None]
        if not jobs:
            return fn(*args)
        comm = jobs[0][0]
        for j in jobs[1:]:
            comm = _Both(comm, j[0])
        res, extra = fn(*args, comm=comm)
        for job, keys, sink in jobs:
            sink.update(zip(keys, extra[:len(job.out_shape)]))
            extra = extra[len(job.out_shape):]
        return res

    d = x.shape[1]
    g, n, pch = small["s5_b_re"].shape
    sa, lw = g * pch, small["lru_lambda"].shape[-1]
    widths = [sa, lw, d, d]
    row = lambda v: v.reshape(1, -1)

    hd = small["w_r"].shape[-1]
    ar_row, ai_row, bbr_blk, bbi_blk, cre_blk, cimn_blk, wr_blk, wi_blk = carry(
        _prep, *disc["rows"], *disc["cols"], disc["b_re"], disc["b_im"], small["s5_c_re"].reshape(sa, n),
        small["s5_c_im"].reshape(sa, n), small["w_r"].reshape(lw, hd), small["w_i"].reshape(lw, hd),
        jobs=[gather(["w_in"])])
    d_row = row(small["s5_d"])
    u_a, u_b, za, zb = carry(_inproj_fwd, x, row(small["g_mix"]), full["w_in"], row(small["b_in"]), widths,
                             jobs=[gather(["w_glu", "conv_w", "w_a_out", "w_b_out"])])
    conv_w = _unblock(full["conv_w"], 1) if distributed else full["conv_w"]
    branches = [_s5_fwd(u_a, bbr_blk, bbi_blk, ar_row, ai_row, cre_blk, cimn_blk, d_row, full["w_glu"],
                        row(small["b_glu"])),
                _lru_fwd(u_b, conv_w, row(small["conv_b"]), wr_blk, row(small["b_r"]), wi_blk, row(small["b_i"]),
                         row(small["lru_lambda"]))]
    (sr, si, y, y_a), (xc, h, hprev) = carry(_stages, branches, "branches_fwd",
                                             jobs=[gather(["w_ffn_gate", "w_ffn_up", "w_o"])])
    x1, merged, ma, mb, fg, fu = carry(
        _merge_ffn_up_fwd, y_a, h, za, zb, x, full["w_a_out"], full["w_b_out"], full["w_o"], row(small["g_ffn"]),
        full["w_ffn_gate"], full["w_ffn_up"], jobs=[gather(["w_ffn_down"])])
    x2 = carry(_ffn_down_fwd, fg, fu, x1, full["w_ffn_down"], jobs=[gather(["w_ple_gate", "w_ple"])])

    dx2, loss_blk, gw["w_ple_gate"], gw["w_ple"], vec_tail = _tail_fwd_bwd(
        x2, p, target, row(small["g_ple_gate"]), full["w_ple_gate"], row(small["b_ple_gate"]), full["w_ple"],
        row(small["g_ple"]), row(small["g_final"]))
    dfg, dfu, act = carry(_ffn_bwd_a, dx2, fg, fu, full["w_ffn_down"], jobs=[swap(["w_ple_gate", "w_ple"])])
    tn_d = min(d, 512)
    gw["w_ffn_down"] = _matmul_tn(act, dx2, tn_d, "dw_ffn_down", BF16)
    add_pairs(["w_ple_gate", "w_ple"])
    dx1, h2, vec_ffn = carry(_ffn_bwd_b, dfg, dfu, x1, dx2, row(small["g_ffn"]), full["w_ffn_gate"], full["w_ffn_up"],
                             jobs=[chips(["w_ple_gate", "w_ple"]), swap(["w_ffn_down"])])
    gw["w_ffn_gate"] = _matmul_tn(dfg, h2, tn_d, "dw_ffn_gate", BF16)
    gw["w_ffn_up"] = _matmul_tn(dfu, h2, tn_d, "dw_ffn_up", BF16)
    add_pairs(["w_ffn_down"])
    dza, dzb, dya, dyb, gw["w_o"], gw["w_a_out"], gw["w_b_out"] = carry(
        _merge_bwd, dx1, merged, ma, mb, za, zb, y_a, h, full["w_o"], full["w_a_out"], full["w_b_out"],
        jobs=[chips(["w_ffn_down"]), swap(["w_ffn_gate", "w_ffn_up"])])
    add_pairs(["w_ffn_gate", "w_ffn_up"])
    du_b, dw_r, dw_i, vec_lru = carry(
        _lru_bwd, dyb, xc, hprev, u_b, conv_w, wr_blk, row(small["b_r"]), wi_blk, row(small["b_i"]),
        row(small["lru_lambda"]), hd, jobs=[chips(["w_ffn_gate"]), swap(["w_o", "w_a_out", "w_b_out"])])
    add_pairs(["w_o", "w_a_out", "w_b_out"])
    du_a, lam_r, lam_i, dy16, gw["w_glu"], vec_sa, vec_gn = carry(
        _s5_bwd, dya, y, sr, si, u_a, full["w_glu"], row(small["b_glu"]), cre_blk, cimn_blk, bbr_blk, bbi_blk, ar_row,
        ai_row, d_row, jobs=[chips(["w_ffn_up"]), chips(["w_o", "w_a_out", "w_b_out"])])
    smalls = {"vec_tail": vec_tail, "vec_ffn": vec_ffn, "vec_lru": vec_lru, "vec_sa": vec_sa, "vec_gn": vec_gn,
              "dw_r": dw_r, "dw_i": dw_i, "loss": loss_blk}
    everyones = {}

    def gather_smalls(keys):
        return (_Gather([smalls[k] for k in keys]), keys, everyones) if distributed else None

    smalls["dbb_re"], smalls["dbb_im"], smalls["dc_re"], smalls["dc_imn"] = carry(
        _s5_param_grads, lam_r, lam_i, sr, si, u_a, dy16, pch, n, jobs=[swap(["w_glu"]), gather_smalls(list(smalls))])
    add_pairs(["w_glu"])
    dz = [du_a, du_b, dza, dzb]
    grad_x, h0, smalls["vec_mix"], smalls["vec_bin"] = _inproj_bwd(
        dz, x, dx1, row(small["g_mix"]), full["w_in"])
    shapes = {k: a.shape for k, a in smalls.items()}
    gw["w_in"] = carry(_dw_from_parts, dz, h0, tn_d, "dw_in",
                       jobs=[chips(["w_glu"]), gather_smalls(["dbb_re", "dbb_im", "dc_re", "dc_imn"])])
    smalls.update(everyones)
    if distributed:
        got["w_in"] = gw["w_in"]
        gw = got
    return grad_x, gw, smalls, shapes


def _disc_inputs(small):
    g, n, pch = small["s5_b_re"].shape
    srcs = (small["lam_re"], small["lam_im"], jnp.repeat(small["log_dt"], n))
    return {"rows": [a.reshape(1, g * n) for a in srcs], "cols": [a.reshape(g * n, 1) for a in srcs],
            "b_re": small["s5_b_re"].reshape(g * n, pch), "b_im": small["s5_b_im"].reshape(g * n, pch)}


def kernel(x, p, g_mix, w_in, b_in, lam_re, lam_im, log_dt, s5_b_re, s5_b_im, s5_c_re, s5_c_im, s5_d, w_glu, b_glu, conv_w, conv_b, w_r, b_r, w_i, b_i, lru_lambda, w_a_out, w_b_out, w_o, g_ffn, w_ffn_gate, w_ffn_up, w_ffn_down, g_ple_gate, w_ple_gate, b_ple_gate, w_ple, g_ple, g_final, loss_target, m_g_mix, m_w_in, m_b_in, m_lam_re, m_lam_im, m_log_dt, m_s5_b_re, m_s5_b_im, m_s5_c_re, m_s5_c_im, m_s5_d, m_w_glu, m_b_glu, m_conv_w, m_conv_b, m_w_r, m_b_r, m_w_i, m_b_i, m_lru_lambda, m_w_a_out, m_w_b_out, m_w_o, m_g_ffn, m_w_ffn_gate, m_w_ffn_up, m_w_ffn_down, m_g_ple_gate, m_w_ple_gate, m_b_ple_gate, m_w_ple, m_g_ple, m_g_final, v_g_mix, v_w_in, v_b_in, v_lam_re, v_lam_im, v_log_dt, v_s5_b_re, v_s5_b_im, v_s5_c_re, v_s5_c_im, v_s5_d, v_w_glu, v_b_glu, v_conv_w, v_conv_b, v_w_r, v_b_r, v_w_i, v_b_i, v_lru_lambda, v_w_a_out, v_w_b_out, v_w_o, v_g_ffn, v_w_ffn_gate, v_w_ffn_up, v_w_ffn_down, v_g_ple_gate, v_w_ple_gate, v_b_ple_gate, v_w_ple, v_g_ple, v_g_final):
    given = dict(locals())
    wts = {k: given[k] for k in WEIGHTS}
    moms = {k: given["m_" + k] for k in WEIGHTS}
    vels = {k: given["v_" + k] for k in WEIGHTS}

    def drop_depth(k, a):
        return a if k == "g_final" else a[0]

    small = {k: drop_depth(k, wts[k]) for k in SMALL}
    shard = {k: wts[k][0] for k in SHARDED}
    names = list(SHARDED)

    def wire(k):
        if k == "conv_w":
            return shard[k]
        return (shard[k].T if k in TRANSPOSED else shard[k]).astype(BF16)

    disc = _disc_inputs(small)
    grad_x, parts, smalls, shapes = _local_step(x[0], p[0, 0], loss_target[0], {k: wire(k) for k in names}, small, disc)

    late = ["vec_mix", "vec_bin"]
    received = _alone(_Both(_PairSwap([parts["w_in"]]), _Gather([smalls[k] for k in late])), "swap_last")
    (pair_in,) = _pair_add([parts["w_in"]], received[:1], "pair_add_w_in")
    smalls.update(zip(late, received[1:]))

    g_small, (parts["w_in"],) = _small_reduce(smalls, shapes, *disc["cols"], disc["b_re"], disc["b_im"],
                                              small["s5_b_re"].shape[0], comm=_ChipExchange([pair_in]))
    loss = g_small.pop("loss")[0, 0]
    cols = shard["conv_w"].shape[1]
    mine = _index((lax.axis_index("x"), lax.axis_index("y"), lax.axis_index("c")))
    parts["conv_w"] = lax.dynamic_slice_in_dim(g_small.pop("conv_w"), mine * cols, cols, axis=1)[None]

    def view(k, a):
        a = a.reshape((1, -1) if k == "g_final" else wts[k].shape)
        return jnp.swapaxes(a, -1, -2) if k in NARROW_LAST else a

    def unview(k, a):
        return (jnp.swapaxes(a, -1, -2) if k in NARROW_LAST else a).reshape(wts[k].shape)

    slots = _adamw_small([view(k, wts[k]) for k in SMALL], [view(k, g_small[k]) for k in SMALL],
                         [view(k, moms[k]) for k in SMALL], [view(k, vels[k]) for k in SMALL])
    small_out = [dict(zip(SMALL, [unview(k, a) for k, a in zip(SMALL, slot)])) for slot in slots]

    big_out = {}
    for group in ADAMW_GROUPS:
        flip = group[0] in LONG_AXIS_MINOR
        look = (lambda a: a.T) if flip else (lambda a: a)
        res = _adamw([parts[k] for k in group], [look(shard[k]) for k in group], [look(moms[k][0]) for k in group],
                     [look(vels[k][0]) for k in group], "adamw_" + group[0],
                     transposed=group[0] in TRANSPOSED and not flip)
        for k, outs4 in zip(group, res):
            big_out[k] = [look(a) for a in outs4]

    outs = [loss, grad_x[None]]
    for slot in range(4):
        for k in WEIGHTS:
            if k in SHARDED:
                outs.append(big_out[k][slot][None])
            else:
                outs.append(small_out[slot][k])
    return tuple(outs)
```

```python
import math

import jax
import jax.numpy as jnp
from jax import lax
from jax.experimental import pallas as pl
from jax.experimental.pallas import tpu as pltpu

F32 = jnp.float32
BF16 = jnp.bfloat16

EPS = 1e-6
LRU_C = 8.0
CONV_WIDTH = 4
ADAM_LR = 0.001
ADAM_B1 = 0.9
ADAM_B2 = 0.999
ADAM_EPS = 1e-08
ADAM_WD = 0.01
ADAM_STEP = 10

N_DEV = 8
MESH = pl.DeviceIdType.MESH
SUBLANES = 8
LANES = 128
VMEM_LIMIT = 56 * 1024 * 1024
TOKEN_TILE = 256
TIME_CHUNK = 256
S5_SLAB = 128
LRU_SLAB = 256


def _dot(a, b):
    return jnp.dot(a.astype(BF16), b.astype(BF16), preferred_element_type=F32)


def _dot_nt(a, b):
    return lax.dot_general(a.astype(BF16), b.astype(BF16), (((1,), (1,)), ((), ())), preferred_element_type=F32)


def _dot_tn(a, b):
    return lax.dot_general(a.astype(BF16), b.astype(BF16), (((0,), (0,)), ((), ())), preferred_element_type=F32)


def _sigmoid(x):
    return jax.nn.sigmoid(x)


def _rms_stats(x):
    r = lax.rsqrt(jnp.mean(x * x, axis=-1, keepdims=True) + EPS)
    return x * r, r


def _rms_bwd(dy, xhat, r, g):
    dxn = dy * g
    dx = r * (dxn - xhat * jnp.mean(dxn * xhat, axis=-1, keepdims=True))
    return dx, dy * xhat


def _rowsum(v):
    return jnp.sum(v, axis=0, keepdims=True)


def _expm1(x):
    u = jnp.exp(x)
    um1 = u - 1.0
    safe = jnp.where(um1 == 0.0, 1.0, jnp.log(u))
    return jnp.where(um1 == 0.0, x, um1 * x / safe)


def _softplus(x):
    e = jnp.exp(-jnp.abs(x))
    u = 1.0 + e
    um1 = u - 1.0
    safe = jnp.where(um1 == 0.0, 1.0, um1)
    log1p_e = jnp.where(um1 == 0.0, e, jnp.log(u) * e / safe)
    return jnp.maximum(x, 0.0) + log1p_e


_GELU_K = math.sqrt(2.0 / math.pi)
_GELU_C = 0.044715


def _gelu(x):
    return 0.5 * x * (1.0 + jnp.tanh(_GELU_K * (x + _GELU_C * x * x * x)))


def _gelu_grad(x):
    th = jnp.tanh(_GELU_K * (x + _GELU_C * x * x * x))
    return 0.5 * (1.0 + th) + 0.5 * x * (1.0 - th * th) * _GELU_K * (1.0 + 3.0 * _GELU_C * x * x)


def _params(*sem):
    return pltpu.CompilerParams(dimension_semantics=sem, vmem_limit_bytes=VMEM_LIMIT)


def _rows(tm, n):
    return pl.BlockSpec((tm, n), lambda i: (i, 0))


def _rows_rev(tm, n, steps):
    return pl.BlockSpec((tm, n), lambda i: (steps - 1 - i, 0))


def _whole(shape):
    nd = len(shape)
    return pl.BlockSpec(shape, lambda i: (0,) * nd, pipeline_mode=pl.Buffered(1))


def _acc(shape):
    nd = len(shape)
    return pl.BlockSpec(shape, lambda i: (0,) * nd)


def _zero_on_first(*refs):
    @pl.when(pl.program_id(0) == 0)
    def _():
        for r in refs:
            r[...] = jnp.zeros_like(r)


def _inproj_fwd(x, g_mix, w_in_t, b_in, widths, comm=None):
    t, d = x.shape
    n = w_in_t.shape[0]
    tm = min(TOKEN_TILE, t)
    offs = [sum(widths[:i]) for i in range(len(widths) + 1)]

    def body(x_ref, g_ref, w_ref, b_ref, *outs):
        xhat, _ = _rms_stats(x_ref[...])
        h = (xhat * g_ref[...]).astype(BF16)
        for k, o_ref in enumerate(outs):
            lo, hi = offs[k], offs[k + 1]
            o_ref[...] = _dot_nt(h, w_ref[lo:hi, :]) + b_ref[:, lo:hi]

    return _run(
        body, comm, name="inproj_fwd", grid=(t // tm,),
        out_shape=[jax.ShapeDtypeStruct((t, w), F32) for w in widths],
        in_specs=[_rows(tm, d), _whole((1, d)), _whole((n, d)), _whole((1, n))],
        out_specs=[_rows(tm, w) for w in widths],
        semantics="parallel",
    )(x, g_mix, w_in_t, b_in)


def _s5_fwd(u, bbr_blk, bbi_blk, ar, ai, cre_blk, cimn_blk, d_skip, w_glu, b_glu):
    t, sa = u.shape
    ns, _, sw = bbr_blk.shape
    gn = ns * sw
    tc = min(TIME_CHUNK, t)

    def body(u_ref, bbr_ref, bbi_ref, ar_ref, ai_ref, cre_ref, cim_ref, d_ref, wg_ref, bg_ref,
             sr_ref, si_ref, y_ref, ya_ref, cr_s, ci_s, sr_s, si_s):
        _zero_on_first(cr_s, ci_s)
        uv = u_ref[...]
        ub = uv.astype(BF16)
        for m in range(ns):
            um = ub[:, m * S5_SLAB:(m + 1) * S5_SLAB]
            sr_s[:, m * sw:(m + 1) * sw] = _dot(um, bbr_ref[m])
            si_s[:, m * sw:(m + 1) * sw] = _dot(um, bbi_ref[m])
        a_r = ar_ref[...]
        a_i = ai_ref[...]

        def step(row, carry):
            c_r, c_i = carry
            at = pl.ds(row, 1)
            n_r = a_r * c_r - a_i * c_i + sr_s[at, :]
            n_i = a_r * c_i + a_i * c_r + si_s[at, :]
            sr_s[at, :] = n_r
            si_s[at, :] = n_i
            return n_r, n_i

        c_r, c_i = lax.fori_loop(0, tc, step, (cr_s[0:1, :], ci_s[0:1, :]), unroll=8)
        cr_s[0:1, :] = c_r
        ci_s[0:1, :] = c_i
        for m in range(ns):
            states, chans = slice(m * sw, (m + 1) * sw), slice(m * S5_SLAB, (m + 1) * S5_SLAB)
            s_r, s_i = sr_s[:, states].astype(BF16), si_s[:, states].astype(BF16)
            sr_ref[:, states] = s_r.astype(sr_ref.dtype)
            si_ref[:, states] = s_i.astype(si_ref.dtype)
            y_ref[:, chans] = _dot(s_r, cre_ref[m]) + _dot(s_i, cim_ref[m]) + d_ref[:, chans] * uv[:, chans]
        y = y_ref[...]
        zz = _gelu(y)
        q = _dot(zz, wg_ref[...]) + bg_ref[...]
        ya_ref[...] = zz * _sigmoid(q)

    return dict(
        body=body, steps=t // tc, args=(u, bbr_blk, bbi_blk, ar, ai, cre_blk, cimn_blk, d_skip, w_glu, b_glu),
        out_shape=[jax.ShapeDtypeStruct((t, gn), BF16), jax.ShapeDtypeStruct((t, gn), BF16),
                   jax.ShapeDtypeStruct((t, sa), F32), jax.ShapeDtypeStruct((t, sa), F32)],
        in_specs=[_rows(tc, sa), _whole(bbr_blk.shape), _whole(bbi_blk.shape), _whole((1, gn)), _whole((1, gn)),
                  _whole(cre_blk.shape), _whole(cimn_blk.shape), _whole((1, sa)), _whole((sa, sa)), _whole((1, sa))],
        out_specs=[_rows(tc, gn), _rows(tc, gn), _rows(tc, sa), _rows(tc, sa)],
        scratch=[pltpu.VMEM((SUBLANES, gn), F32), pltpu.VMEM((SUBLANES, gn), F32),
                 pltpu.VMEM((tc, gn), F32), pltpu.VMEM((tc, gn), F32)])


def _stages(stages, name, comm=None):
    counts = [[len(s[k]) for s in stages] for k in ("args", "out_shape", "scratch")]

    def body(*refs):
        groups, pos = [], 0
        for kind in counts:
            per_stage = []
            for c in kind:
                per_stage.append(refs[pos:pos + c])
                pos += c
            groups.append(per_stage)
        for s, ins, outs, scratch in zip(stages, *groups):
            s["body"](*ins, *outs, *scratch)

    res = _run(
        body, comm, name=name, grid=(stages[0]["steps"],),
        out_shape=[o for s in stages for o in s["out_shape"]], in_specs=[i for s in stages for i in s["in_specs"]],
        out_specs=[o for s in stages for o in s["out_specs"]], scratch_shapes=[c for s in stages for c in s["scratch"]],
        semantics="arbitrary",
    )(*[a for s in stages for a in s["args"]])
    extra = None
    if comm is not None:
        res, extra = res
    per_stage, pos = [], 0
    for c in counts[1]:
        per_stage.append(list(res[pos:pos + c]))
        pos += c
    return per_stage if comm is None else (per_stage, extra)


def _slab_dot(x, w_ref, transposed=False):
    dot = _dot_nt if transposed else _dot
    xb = x.astype(BF16)
    return jnp.concatenate([dot(xb[:, j * LRU_SLAB:(j + 1) * LRU_SLAB], w_ref[j]) for j in range(w_ref.shape[0])], axis=1)


def _lru_gates(xc, wr_ref, br_ref, wi_ref, bi_ref, lam_ref):
    r = _sigmoid(_slab_dot(xc, wr_ref) + br_ref[...])
    ig = _sigmoid(_slab_dot(xc, wi_ref) + bi_ref[...])
    sp = _softplus(-lam_ref[...])
    log_a = (-LRU_C * r) * sp
    return r, ig, sp, log_a


def _lru_fwd(u, conv_w, conv_b, wr_blk, b_r, wi_blk, b_i, lru_lambda):
    t, w = u.shape
    tc = min(TIME_CHUNK, t)
    halo = SUBLANES

    def body(u_ref, cw_ref, cb_ref, wr_ref, br_ref, wi_ref, bi_ref, lam_ref,
             xc_ref, h_ref, hp_ref, ext_s, a_s, carry_s):
        @pl.when(pl.program_id(0) == 0)
        def _():
            ext_s[0:halo, :] = jnp.zeros((halo, w), F32)
            carry_s[...] = jnp.zeros_like(carry_s)

        ext_s[halo:halo + tc, :] = u_ref[...]
        xc = cb_ref[...]
        for k in range(CONV_WIDTH):
            off = halo - (CONV_WIDTH - 1) + k
            xc = xc + cw_ref[k:k + 1, :] * ext_s[off:off + tc, :]
        ext_s[0:halo, :] = ext_s[tc:tc + halo, :]
        xc_ref[...] = xc
        r, ig, sp, log_a = _lru_gates(xc, wr_ref, br_ref, wi_ref, bi_ref, lam_ref)
        a_s[...] = jnp.exp(log_a)
        h_ref[...] = jnp.sqrt(-_expm1(2.0 * log_a)) * ig * xc

        def step(row, carry):
            at = pl.ds(row, 1)
            hp_ref[at, :] = carry
            nxt = a_s[at, :] * carry + h_ref[at, :]
            h_ref[at, :] = nxt
            return nxt

        carry_s[0:1, :] = lax.fori_loop(0, tc, step, carry_s[0:1, :], unroll=8)

    return dict(
        body=body, steps=t // tc, args=(u, conv_w, conv_b, wr_blk, b_r, wi_blk, b_i, lru_lambda),
        out_shape=[jax.ShapeDtypeStruct((t, w), F32)] * 3,
        in_specs=[_rows(tc, w), _whole((CONV_WIDTH, w)), _whole((1, w)), _whole(wr_blk.shape), _whole((1, w)),
                  _whole(wi_blk.shape), _whole((1, w)), _whole((1, w))],
        out_specs=[_rows(tc, w)] * 3,
        scratch=[pltpu.VMEM((halo + tc, w), F32), pltpu.VMEM((tc, w), F32), pltpu.VMEM((SUBLANES, w), F32)])


def _merge_ffn_up_fwd(y_a, h, za, zb, x, w_a_out_t, w_b_out, w_o, g_ffn, w_gate_t, w_up_t, comm=None):
    t, d = x.shape
    sa, lw = y_a.shape[1], h.shape[1]
    f = w_gate_t.shape[0]
    tm = min(TOKEN_TILE, t)

    def body(ya_ref, h_ref, za_ref, zb_ref, x_ref, wa_ref, wb_ref, wo_ref, g_ref, wg_ref, wu_ref,
             x1_ref, mg_ref, ma_ref, mb_ref, fg_ref, fu_ref):
        ma = _dot_nt(ya_ref[...], wa_ref[...])
        mb = _dot(h_ref[...], wb_ref[...])
        merged = _sigmoid(za_ref[...]) * ma + _sigmoid(zb_ref[...]) * mb
        ma_ref[...] = ma.astype(ma_ref.dtype)
        mb_ref[...] = mb.astype(mb_ref.dtype)
        mg_ref[...] = merged.astype(mg_ref.dtype)
        x1 = x_ref[...] + _dot(merged, wo_ref[...])
        x1_ref[...] = x1
        xhat, _ = _rms_stats(x1)
        h2 = (xhat * g_ref[...]).astype(BF16)
        fg_ref[...] = _dot_nt(h2, wg_ref[...]).astype(fg_ref.dtype)
        fu_ref[...] = _dot_nt(h2, wu_ref[...]).astype(fu_ref.dtype)

    return _run(
        body, comm, name="merge_ffn_up_fwd", grid=(t // tm,),
        out_shape=[jax.ShapeDtypeStruct((t, d), F32), jax.ShapeDtypeStruct((t, d), BF16),
                   jax.ShapeDtypeStruct((t, d), BF16), jax.ShapeDtypeStruct((t, d), BF16),
                   jax.ShapeDtypeStruct((t, f), BF16), jax.ShapeDtypeStruct((t, f), BF16)],
        in_specs=[_rows(tm, sa), _rows(tm, lw), _rows(tm, d), _rows(tm, d), _rows(tm, d),
                  _whole((d, sa)), _whole((lw, d)), _whole((d, d)), _whole((1, d)), _whole((f, d)), _whole((f, d))],
        out_specs=[_rows(tm, d)] * 4 + [_rows(tm, f)] * 2,
        semantics="parallel",
    )(y_a, h, za, zb, x, w_a_out_t, w_b_out, w_o, g_ffn, w_gate_t, w_up_t)


def _ffn_down_fwd(fg, fu, x1, w_down, comm=None):
    t, d = x1.shape
    f = fg.shape[1]
    tm = min(TOKEN_TILE, t)

    def body(fg_ref, fu_ref, x_ref, wd_ref, x2_ref):
        fgv = fg_ref[...].astype(F32)
        act = fgv * _sigmoid(fgv) * fu_ref[...].astype(F32)
        x2_ref[...] = x_ref[...] + _dot(act, wd_ref[...])

    return _run(
        body, comm, name="ffn_down_fwd", grid=(t // tm,),
        out_shape=jax.ShapeDtypeStruct((t, d), F32),
        in_specs=[_rows(tm, f), _rows(tm, f), _rows(tm, d), _whole((f, d))],
        out_specs=_rows(tm, d),
        semantics="parallel",
    )(fg, fu, x1, w_down)


def _store_on_last(pairs):
    @pl.when(pl.program_id(0) == pl.num_programs(0) - 1)
    def _():
        for acc, out in pairs:
            out[...] = acc[...].astype(out.dtype)


def _tail_fwd_bwd(x2, p, target, g_pg, w_pg, b_pg, w_ple_t, g_ple, g_final):
    t, d = x2.shape
    pd = p.shape[1]
    tm = min(TOKEN_TILE, t)

    def body(x2_ref, p_ref, tg_ref, gpg_ref, wpg_ref, bpg_ref, wple_ref, gple_ref, gfin_ref,
             dx2_ref, loss_ref, dwpg_out, dwple_out, vec_ref, dwpg_ref, dwple_ref):
        _zero_on_first(loss_ref, dwpg_ref, dwple_ref, vec_ref)
        x2v = x2_ref[...]
        xh2, r2 = _rms_stats(x2v)
        h3 = xh2 * gpg_ref[...]
        gp = _sigmoid(_dot(h3, wpg_ref[...]) + bpg_ref[...])
        pe = _dot_nt(p_ref[...], wple_ref[...])
        peh, r3 = _rms_stats(pe)
        e = peh * gple_ref[...]
        x3 = x2v + gp * e
        xh3, r4 = _rms_stats(x3)
        diff = xh3 * gfin_ref[...] - tg_ref[...]
        loss_ref[...] += 0.5 * jnp.sum(jnp.mean(diff * diff, axis=-1, keepdims=True))
        dy = diff * (1.0 / d)
        dx3, dgfin = _rms_bwd(dy, xh3, r4, gfin_ref[...])
        d_gp = dx3 * e
        d_e = dx3 * gp
        dpe, dgple = _rms_bwd(d_e, peh, r3, gple_ref[...])
        dwple_ref[...] += _dot_tn(dpe, p_ref[...])
        dpre = d_gp * gp * (1.0 - gp)
        dwpg_ref[...] += _dot_tn(h3, dpre)
        dh3 = _dot_nt(dpre, wpg_ref[...])
        dx2n, dgpg = _rms_bwd(dh3, xh2, r2, gpg_ref[...])
        dx2_ref[...] = dx3 + dx2n
        vec_ref[0:1, :] += _rowsum(dpre)
        vec_ref[1:2, :] += _rowsum(dgpg)
        vec_ref[2:3, :] += _rowsum(dgple)
        vec_ref[3:4, :] += _rowsum(dgfin)
        _store_on_last([(dwpg_ref, dwpg_out), (dwple_ref, dwple_out)])

    return pl.pallas_call(
        body, name="tail_fwd_bwd", grid=(t // tm,),
        out_shape=[jax.ShapeDtypeStruct((t, d), F32), jax.ShapeDtypeStruct((SUBLANES, LANES), F32),
                   jax.ShapeDtypeStruct((d, d), BF16), jax.ShapeDtypeStruct((d, pd), BF16),
                   jax.ShapeDtypeStruct((SUBLANES, d), F32)],
        in_specs=[_rows(tm, d), _rows(tm, pd), _rows(tm, d), _whole((1, d)), _whole((d, d)), _whole((1, d)),
                  _whole((d, pd)), _whole((1, d)), _whole((1, d))],
        out_specs=[_rows(tm, d), _acc((SUBLANES, LANES)), _acc((d, d)), _acc((d, pd)), _acc((SUBLANES, d))],
        scratch_shapes=[pltpu.VMEM((d, d), F32), pltpu.VMEM((d, pd), F32)],
        compiler_params=_params("arbitrary"),
    )(x2, p, target, g_pg, w_pg, b_pg, w_ple_t, g_ple, g_final)


def _ffn_bwd_a(dx2, fg, fu, w_down, comm=None):
    t, d = dx2.shape
    f = fg.shape[1]
    tm = min(TOKEN_TILE, t)

    def body(dx_ref, fg_ref, fu_ref, wd_ref, dfg_ref, dfu_ref, act_ref):
        dact = _dot_nt(dx_ref[...], wd_ref[...])
        fgv = fg_ref[...].astype(F32)
        fuv = fu_ref[...].astype(F32)
        sg = _sigmoid(fgv)
        silu = fgv * sg
        dfu_ref[...] = (dact * silu).astype(dfu_ref.dtype)
        dfg_ref[...] = (dact * fuv * (sg * (1.0 + fgv * (1.0 - sg)))).astype(dfg_ref.dtype)
        act_ref[...] = (silu * fuv).astype(act_ref.dtype)

    return _run(
        body, comm, name="ffn_bwd_a", grid=(t // tm,),
        out_shape=[jax.ShapeDtypeStruct((t, f), BF16)] * 3,
        in_specs=[_rows(tm, d), _rows(tm, f), _rows(tm, f), _whole((f, d))],
        out_specs=[_rows(tm, f)] * 3,
        semantics="parallel",
    )(dx2, fg, fu, w_down)


def _ffn_bwd_b(dfg, dfu, x1, dx2, g_ffn, w_gate_t, w_up_t, comm=None):
    t, d = x1.shape
    f = dfg.shape[1]
    tm = min(TOKEN_TILE, t)

    def body(dfg_ref, dfu_ref, x_ref, dx2_ref, g_ref, wg_ref, wu_ref, dx1_ref, h2_ref, vec_ref):
        _zero_on_first(vec_ref)
        dh2 = _dot(dfg_ref[...], wg_ref[...]) + _dot(dfu_ref[...], wu_ref[...])
        xhat, r = _rms_stats(x_ref[...])
        h2_ref[...] = (xhat * g_ref[...]).astype(h2_ref.dtype)
        dxn, dg = _rms_bwd(dh2, xhat, r, g_ref[...])
        dx1_ref[...] = dx2_ref[...] + dxn
        vec_ref[0:1, :] += _rowsum(dg)

    return _run(
        body, comm, name="ffn_bwd_b", grid=(t // tm,),
        out_shape=[jax.ShapeDtypeStruct((t, d), F32), jax.ShapeDtypeStruct((t, d), BF16),
                   jax.ShapeDtypeStruct((SUBLANES, d), F32)],
        in_specs=[_rows(tm, f), _rows(tm, f), _rows(tm, d), _rows(tm, d), _whole((1, d)), _whole((f, d)), _whole((f, d))],
        out_specs=[_rows(tm, d), _rows(tm, d), _acc((SUBLANES, d))],
        semantics="arbitrary",
    )(dfg, dfu, x1, dx2, g_ffn, w_gate_t, w_up_t)


def _matmul_tn(a, b, tn, name, dtype=F32):
    t, k = a.shape
    n = b.shape[1]

    def body(a_ref, b_ref, o_ref):
        o_ref[...] = _dot_tn(a_ref[...], b_ref[...]).astype(o_ref.dtype)

    return _run(
        body, None, name=name, grid=(n // tn,),
        out_shape=jax.ShapeDtypeStruct((k, n), dtype),
        in_specs=[_whole((t, k)), pl.BlockSpec((t, tn), lambda j: (0, j))],
        out_specs=pl.BlockSpec((k, tn), lambda j: (0, j)),
        semantics="parallel",
    )(a, b)


def _merge_bwd(dx1, merged, ma, mb, za, zb, y_a, h, w_o, w_a_out_t, w_b_out, comm=None):
    t, d = dx1.shape
    sa, lw = y_a.shape[1], h.shape[1]
    tm = min(TOKEN_TILE, t)

    def body(dx1_ref, mg_ref, ma_ref, mb_ref, za_ref, zb_ref, ya_ref, h_ref, wo_ref, wa_ref, wb_ref,
             dza_ref, dzb_ref, dya_ref, dyb_ref, dwo_out, dwa_out, dwb_out, dwo_ref, dwa_ref, dwb_ref):
        _zero_on_first(dwo_ref, dwa_ref, dwb_ref)
        dx1v = dx1_ref[...].astype(BF16)
        dmg = _dot_nt(dx1v, wo_ref[...])
        ga = _sigmoid(za_ref[...])
        gb = _sigmoid(zb_ref[...])
        dza_ref[...] = (dmg * ma_ref[...].astype(F32) * ga * (1.0 - ga)).astype(dza_ref.dtype)
        dzb_ref[...] = (dmg * mb_ref[...].astype(F32) * gb * (1.0 - gb)).astype(dzb_ref.dtype)
        dma = (dmg * ga).astype(BF16)
        dmb = (dmg * gb).astype(BF16)
        dya_ref[...] = _dot(dma, wa_ref[...])
        dyb_ref[...] = _dot_nt(dmb, wb_ref[...])
        dwo_ref[...] += _dot_tn(mg_ref[...], dx1v)
        dwa_ref[...] += _dot_tn(dma, ya_ref[...])
        dwb_ref[...] += _dot_tn(h_ref[...], dmb)
        _store_on_last([(dwo_ref, dwo_out), (dwa_ref, dwa_out), (dwb_ref, dwb_out)])

    return _run(
        body, comm, name="merge_bwd", grid=(t // tm,),
        out_shape=[jax.ShapeDtypeStruct((t, d), BF16), jax.ShapeDtypeStruct((t, d), BF16),
                   jax.ShapeDtypeStruct((t, sa), F32), jax.ShapeDtypeStruct((t, lw), F32),
                   jax.ShapeDtypeStruct((d, d), BF16), jax.ShapeDtypeStruct((d, sa), BF16),
                   jax.ShapeDtypeStruct((lw, d), BF16)],
        in_specs=[_rows(tm, d), _rows(tm, d), _rows(tm, d), _rows(tm, d), _rows(tm, d), _rows(tm, d),
                  _rows(tm, sa), _rows(tm, lw), _whole((d, d)), _whole((d, sa)), _whole((lw, d))],
        out_specs=[_rows(tm, d), _rows(tm, d), _rows(tm, sa), _rows(tm, lw), _acc((d, d)), _acc((d, sa)), _acc((lw, d))],
        scratch_shapes=[pltpu.VMEM((d, d), F32), pltpu.VMEM((d, sa), F32), pltpu.VMEM((lw, d), F32)],
        semantics="arbitrary",
    )(dx1, merged, ma, mb, za, zb, y_a, h, w_o, w_a_out_t, w_b_out)


def _fold_diag_blocks(dense, row_group, col_group):
    r, c = dense.shape
    rows = lax.broadcasted_iota(jnp.int32, (r, c), 0)
    cols = lax.broadcasted_iota(jnp.int32, (r, c), 1)
    kept = jnp.where(rows // row_group == cols // col_group, dense, 0.0)
    pick = (lax.broadcasted_iota(jnp.int32, (row_group, r), 0)
            == lax.broadcasted_iota(jnp.int32, (row_group, r), 1) % row_group).astype(F32)
    return jnp.dot(pick, kept, preferred_element_type=F32, precision=lax.Precision.HIGHEST)


def _lru_bwd(dh, xc, hprev, u, conv_w, wr_blk, b_r, wi_blk, b_i, lru_lambda, head_dim, comm=None):
    t, w = dh.shape
    tc = min(TIME_CHUNK, t)
    steps = t // tc
    halo = SUBLANES
    sub_per_chunk = tc // halo
    slabs = w // LRU_SLAB

    def body(dh_ref, xc_ref, hp_ref, u_ref, uh_ref, cw_ref, wr_ref, br_ref, wi_ref, bi_ref, lam_ref,
             du_ref, dwr_out, dwi_out, vec_ref, lam_s, a_s, dxc_s, uext_s, carry_s, dwr_ref, dwi_ref):
        chunk = steps - 1 - pl.program_id(0)

        @pl.when(pl.program_id(0) == 0)
        def _():
            carry_s[...] = jnp.zeros_like(carry_s)
            dxc_s[tc:tc + halo, :] = jnp.zeros((halo, w), F32)
            dwr_ref[...] = jnp.zeros_like(dwr_ref)
            dwi_ref[...] = jnp.zeros_like(dwi_ref)
            vec_ref[...] = jnp.zeros_like(vec_ref)

        xc = xc_ref[...]
        r, ig, sp, log_a = _lru_gates(xc, wr_ref, br_ref, wi_ref, bi_ref, lam_ref)
        a = jnp.exp(log_a)
        a_s[...] = a

        def step(i, q):
            at = pl.ds(tc - 1 - i, 1)
            lam_row = dh_ref[at, :] + q
            lam_s[at, :] = lam_row
            return a_s[at, :] * lam_row

        carry_s[0:1, :] = lax.fori_loop(0, tc, step, carry_s[0:1, :], unroll=8)
        lam = lam_s[...]
        mult = jnp.sqrt(-_expm1(2.0 * log_a))
        d_log_a = lam * hp_ref[...] * a - (lam * ig * xc) * (a * a) / mult
        d_ig = lam * mult * xc
        dpre_r = (d_log_a * (-LRU_C * sp)) * r * (1.0 - r)
        dpre_i = d_ig * ig * (1.0 - ig)
        dxc = lam * mult * ig + _slab_dot(dpre_r, wr_ref, transposed=True) + _slab_dot(dpre_i, wi_ref, transposed=True)
        xcb, drb, dib = xc.astype(BF16), dpre_r.astype(BF16), dpre_i.astype(BF16)
        for j in range(slabs):
            cols = slice(j * LRU_SLAB, (j + 1) * LRU_SLAB)
            dwr_ref[j] += _dot_tn(drb[:, cols], xcb[:, cols])
            dwi_ref[j] += _dot_tn(dib[:, cols], xcb[:, cols])
        vec_ref[0:1, :] += _rowsum(dxc)
        vec_ref[1:2, :] += _rowsum(dpre_r)
        vec_ref[2:3, :] += _rowsum(dpre_i)
        vec_ref[3:4, :] += _rowsum(d_log_a * (-LRU_C * r)) * (-_sigmoid(-lam_ref[...]))
        dxc_s[0:tc, :] = dxc
        du = cw_ref[CONV_WIDTH - 1:CONV_WIDTH, :] * dxc
        for k in range(CONV_WIDTH - 1):
            off = CONV_WIDTH - 1 - k
            du = du + cw_ref[k:k + 1, :] * dxc_s[off:off + tc, :]
        du_ref[...] = du.astype(du_ref.dtype)
        dxc_s[tc:tc + halo, :] = dxc_s[0:halo, :]
        uext_s[0:halo, :] = jnp.where(chunk > 0, uh_ref[...], 0.0)
        uext_s[halo:halo + tc, :] = u_ref[...]
        for k in range(CONV_WIDTH):
            off = halo - (CONV_WIDTH - 1) + k
            vec_ref[4 + k:5 + k, :] += _rowsum(dxc * uext_s[off:off + tc, :])

        @pl.when(pl.program_id(0) == steps - 1)
        def _():
            for j in range(slabs):
                cols = slice(j * LRU_SLAB, (j + 1) * LRU_SLAB)
                dwr_out[:, cols] = _fold_diag_blocks(dwr_ref[j], head_dim, head_dim).astype(dwr_out.dtype)
                dwi_out[:, cols] = _fold_diag_blocks(dwi_ref[j], head_dim, head_dim).astype(dwi_out.dtype)

    halo_spec = pl.BlockSpec((halo, w), lambda i: (jnp.maximum((steps - 1 - i) * sub_per_chunk - 1, 0), 0))
    return _run(
        body, comm, name="lru_bwd", grid=(steps,),
        out_shape=[jax.ShapeDtypeStruct((t, w), BF16), jax.ShapeDtypeStruct((head_dim, w), BF16),
                   jax.ShapeDtypeStruct((head_dim, w), BF16), jax.ShapeDtypeStruct((SUBLANES, w), F32)],
        in_specs=[_rows_rev(tc, w, steps)] * 4 + [halo_spec, _whole((CONV_WIDTH, w)), _whole(wr_blk.shape),
                                                  _whole((1, w)), _whole(wi_blk.shape), _whole((1, w)), _whole((1, w))],
        out_specs=[_rows_rev(tc, w, steps), _acc((head_dim, w)), _acc((head_dim, w)), _acc((SUBLANES, w))],
        scratch_shapes=[pltpu.VMEM((tc, w), F32), pltpu.VMEM((tc, w), F32), pltpu.VMEM((tc + halo, w), F32),
                        pltpu.VMEM((halo + tc, w), F32), pltpu.VMEM((SUBLANES, w), F32),
                        pltpu.VMEM((slabs, LRU_SLAB, LRU_SLAB), F32), pltpu.VMEM((slabs, LRU_SLAB, LRU_SLAB), F32)],
        semantics="arbitrary",
    )(dh, xc, hprev, u, u, conv_w, wr_blk, b_r, wi_blk, b_i, lru_lambda)


def _s5_bwd(dya, y, sr, si, u, w_glu, b_glu, cre_blk, cimn_blk, bbr_blk, bbi_blk, ar, ai, d_skip, comm=None):
    t, sa = dya.shape
    gn = sr.shape[1]
    ns, _, sw = bbr_blk.shape
    tc = min(TIME_CHUNK, t)
    steps = t // tc
    halo = SUBLANES

    def body(dya_ref, y_ref, sr_ref, si_ref, u_ref, wg_ref, bg_ref, cre_ref, cim_ref, bbr_ref, bbi_ref,
             ar_ref, ai_ref, d_ref, du_ref, lr_ref, li_ref, dy_ref, dwg_out, vsa_ref, vgn_ref, gr_s, gi_s, cr_s, ci_s,
             dwg_ref):
        @pl.when(pl.program_id(0) == 0)
        def _():
            cr_s[...] = jnp.zeros_like(cr_s)
            ci_s[...] = jnp.zeros_like(ci_s)
            gr_s[tc:tc + halo, :] = jnp.zeros((halo, gn), F32)
            gi_s[tc:tc + halo, :] = jnp.zeros((halo, gn), F32)
            dwg_ref[...] = jnp.zeros_like(dwg_ref)
            vsa_ref[...] = jnp.zeros_like(vsa_ref)
            vgn_ref[...] = jnp.zeros_like(vgn_ref)

        yv = y_ref[...]
        uv = u_ref[...]
        zz = _gelu(yv)
        sg = _sigmoid(_dot(zz, wg_ref[...]) + bg_ref[...])
        dyav = dya_ref[...]
        dq = dyav * zz * sg * (1.0 - sg)
        dzz = dyav * sg + _dot_nt(dq, wg_ref[...])
        dwg_ref[...] += _dot_tn(zz, dq)
        dy = dzz * _gelu_grad(yv)
        dyb = dy.astype(BF16)
        dy_ref[...] = dyb.astype(dy_ref.dtype)
        vsa_ref[0:1, :] += _rowsum(dq)
        vsa_ref[1:2, :] += _rowsum(dy * uv)
        for m in range(ns):
            dym = dyb[:, m * S5_SLAB:(m + 1) * S5_SLAB]
            gr_s[0:tc, m * sw:(m + 1) * sw] = _dot_nt(dym, cre_ref[m])
            gi_s[0:tc, m * sw:(m + 1) * sw] = _dot_nt(dym, cim_ref[m])
        a_r = ar_ref[...]
        a_i = ai_ref[...]

        def step(i, carry):
            l_r, l_i = carry
            at = pl.ds(tc - 1 - i, 1)
            n_r = gr_s[at, :] + a_r * l_r + a_i * l_i
            n_i = gi_s[at, :] + a_r * l_i - a_i * l_r
            gr_s[at, :] = n_r
            gi_s[at, :] = n_i
            return n_r, n_i

        l_r, l_i = lax.fori_loop(0, tc, step, (cr_s[0:1, :], ci_s[0:1, :]), unroll=8)
        cr_s[0:1, :] = l_r
        ci_s[0:1, :] = l_i
        nxt_r = gr_s[1:tc + 1, :]
        nxt_i = gi_s[1:tc + 1, :]
        srv = sr_ref[...].astype(F32)
        siv = si_ref[...].astype(F32)
        vgn_ref[0:1, :] += _rowsum(nxt_r * srv + nxt_i * siv)
        vgn_ref[1:2, :] += _rowsum(nxt_i * srv - nxt_r * siv)
        lam_r = gr_s[0:tc, :]
        lam_i = gi_s[0:tc, :]
        gr_s[tc:tc + halo, :] = gr_s[0:halo, :]
        gi_s[tc:tc + halo, :] = gi_s[0:halo, :]
        lrb = lam_r.astype(BF16)
        lib = lam_i.astype(BF16)
        lr_ref[...] = lrb.astype(lr_ref.dtype)
        li_ref[...] = lib.astype(li_ref.dtype)
        for m in range(ns):
            states, chans = slice(m * sw, (m + 1) * sw), slice(m * S5_SLAB, (m + 1) * S5_SLAB)
            du_ref[:, chans] = (_dot_nt(lrb[:, states], bbr_ref[m]) + _dot_nt(lib[:, states], bbi_ref[m])
                                + dy[:, chans] * d_ref[:, chans]).astype(du_ref.dtype)
        _store_on_last([(dwg_ref, dwg_out)])

    return _run(
        body, comm, name="s5_bwd", grid=(steps,),
        out_shape=[jax.ShapeDtypeStruct((t, sa), BF16), jax.ShapeDtypeStruct((t, gn), BF16),
                   jax.ShapeDtypeStruct((t, gn), BF16), jax.ShapeDtypeStruct((t, sa), BF16),
                   jax.ShapeDtypeStruct((sa, sa), BF16), jax.ShapeDtypeStruct((SUBLANES, sa), F32),
                   jax.ShapeDtypeStruct((SUBLANES, gn), F32)],
        in_specs=[_rows_rev(tc, sa, steps), _rows_rev(tc, sa, steps), _rows_rev(tc, gn, steps), _rows_rev(tc, gn, steps),
                  _rows_rev(tc, sa, steps), _whole((sa, sa)), _whole((1, sa)), _whole(cre_blk.shape),
                  _whole(cimn_blk.shape), _whole(bbr_blk.shape), _whole(bbi_blk.shape), _whole((1, gn)), _whole((1, gn)),
                  _whole((1, sa))],
        out_specs=[_rows_rev(tc, sa, steps), _rows_rev(tc, gn, steps), _rows_rev(tc, gn, steps), _rows_rev(tc, sa, steps),
                   _acc((sa, sa)), _acc((SUBLANES, sa)), _acc((SUBLANES, gn))],
        scratch_shapes=[pltpu.VMEM((tc + halo, gn), F32), pltpu.VMEM((tc + halo, gn), F32),
                        pltpu.VMEM((SUBLANES, gn), F32), pltpu.VMEM((SUBLANES, gn), F32), pltpu.VMEM((sa, sa), F32)],
        semantics="arbitrary",
    )(dya, y, sr, si, u, w_glu, b_glu, cre_blk, cimn_blk, bbr_blk, bbi_blk, ar, ai, d_skip)


def _inproj_bwd(dparts, x, dx1, g_mix, w_in_t, comm=None):
    t, d = x.shape
    n = w_in_t.shape[0]
    widths = [p.shape[1] for p in dparts]
    offs = [sum(widths[:i]) for i in range(len(widths) + 1)]
    tm = min(TOKEN_TILE, t)
    np_ = len(dparts)

    def body(*refs):
        dz_refs = refs[:np_]
        x_ref, dx1_ref, g_ref, w_ref, gx_ref, h_ref, vd_ref, vn_ref = refs[np_:]
        _zero_on_first(vd_ref, vn_ref)
        dh = jnp.zeros((tm, d), F32)
        for k, r in enumerate(dz_refs):
            lo, hi = offs[k], offs[k + 1]
            dzk = r[...]
            dh = dh + _dot(dzk, w_ref[lo:hi, :])
            vn_ref[0:1, lo:hi] += _rowsum(dzk.astype(F32))
        xhat, r0 = _rms_stats(x_ref[...])
        h_ref[...] = (xhat * g_ref[...]).astype(h_ref.dtype)
        dxn, dg = _rms_bwd(dh, xhat, r0, g_ref[...])
        gx_ref[...] = dx1_ref[...] + dxn
        vd_ref[0:1, :] += _rowsum(dg)

    return _run(
        body, comm, name="inproj_bwd", grid=(t // tm,),
        out_shape=[jax.ShapeDtypeStruct((t, d), F32), jax.ShapeDtypeStruct((t, d), BF16),
                   jax.ShapeDtypeStruct((SUBLANES, d), F32), jax.ShapeDtypeStruct((SUBLANES, n), F32)],
        in_specs=[_rows(tm, w) for w in widths] + [_rows(tm, d), _rows(tm, d), _whole((1, d)), _whole((n, d))],
        out_specs=[_rows(tm, d), _rows(tm, d), _acc((SUBLANES, d)), _acc((SUBLANES, n))],
        semantics="arbitrary",
    )(*dparts, x, dx1, g_mix, w_in_t)


def _dw_from_parts(dparts, h, tn, name, comm=None):
    t, d = h.shape
    widths = [p.shape[1] for p in dparts]
    offs = [sum(widths[:i]) for i in range(len(widths) + 1)]
    np_ = len(dparts)

    def body(*refs):
        h_ref, o_ref = refs[np_], refs[np_ + 1]
        hv = h_ref[...]
        for k, r in enumerate(refs[:np_]):
            o_ref[offs[k]:offs[k + 1], :] = _dot_tn(r[...], hv).astype(o_ref.dtype)

    return _run(
        body, comm, name=name, grid=(d // tn,),
        out_shape=jax.ShapeDtypeStruct((offs[-1], d), BF16),
        in_specs=[_whole(p.shape) for p in dparts] + [pl.BlockSpec((t, tn), lambda j: (0, j))],
        out_specs=pl.BlockSpec((offs[-1], tn), lambda j: (0, j)),
        semantics="parallel",
    )(*dparts, h)


def _prep(lr_row, li_row, ldt_row, lr_col, li_col, ldt_col, b_re, b_im, c_re, c_im, w_r, w_i, comm=None):
    gn, pch = b_re.shape
    sa, n = c_re.shape
    w, hd = w_r.shape
    ns, sw, lsl = sa // S5_SLAB, S5_SLAB * n // pch, w // LRU_SLAB

    def spread_cols(vals, row_group, col_group, width):
        r, k = vals.shape
        tile = (lax.broadcasted_iota(jnp.int32, (k, width), 0) == lax.broadcasted_iota(jnp.int32, (k, width), 1) % k)
        rows = lax.broadcasted_iota(jnp.int32, (r, width), 0) // row_group
        cols = lax.broadcasted_iota(jnp.int32, (r, width), 1) // col_group
        return jnp.where(rows == cols, _dot(vals, tile.astype(BF16)), 0.0)

    def spread_rows(vals, row_group, col_group, height):
        k, c = vals.shape
        tile = (lax.broadcasted_iota(jnp.int32, (height, k), 0) % k == lax.broadcasted_iota(jnp.int32, (height, k), 1))
        rows = lax.broadcasted_iota(jnp.int32, (height, c), 0) // row_group
        cols = lax.broadcasted_iota(jnp.int32, (height, c), 1) // col_group
        return jnp.where(rows == cols, _dot(tile.astype(BF16), vals), 0.0)

    def body(lrr, lir, ldr, lrc, lic, ldc, bre, bim, cre, cim, wr, wi,
             ar_o, ai_o, bbr_o, bbi_o, cre_o, cim_o, wr_o, wi_o):
        ar, ai, _, _ = _disc_scalars(lrr[...], lir[...], ldr[...])
        ar_o[...] = ar
        ai_o[...] = ai
        _, _, bbr, bbi = _disc_cols(lrc[...], lic[...], ldc[...], bre[...], bim[...])
        bbr_t, bbi_t = bbr.T, bbi.T
        for m in range(ns):
            bbr_o[m] = spread_rows(bbr_t[:, m * sw:(m + 1) * sw], pch, n, S5_SLAB).astype(bbr_o.dtype)
            bbi_o[m] = spread_rows(bbi_t[:, m * sw:(m + 1) * sw], pch, n, S5_SLAB).astype(bbi_o.dtype)
            rows = slice(m * S5_SLAB, (m + 1) * S5_SLAB)
            cre_o[m] = spread_rows(cre[rows, :].T, n, pch, sw).astype(cre_o.dtype)
            cim_o[m] = spread_rows(-cim[rows, :].T, n, pch, sw).astype(cim_o.dtype)
        for j in range(lsl):
            rows = slice(j * LRU_SLAB, (j + 1) * LRU_SLAB)
            wr_o[j] = spread_cols(wr[rows, :], hd, hd, LRU_SLAB).astype(wr_o.dtype)
            wi_o[j] = spread_cols(wi[rows, :], hd, hd, LRU_SLAB).astype(wi_o.dtype)

    args = (lr_row, li_row, ldt_row, lr_col, li_col, ldt_col, b_re, b_im, c_re, c_im, w_r, w_i)
    out_shape = [jax.ShapeDtypeStruct((1, gn), F32), jax.ShapeDtypeStruct((1, gn), F32),
                 jax.ShapeDtypeStruct((ns, S5_SLAB, sw), BF16), jax.ShapeDtypeStruct((ns, S5_SLAB, sw), BF16),
                 jax.ShapeDtypeStruct((ns, sw, S5_SLAB), BF16), jax.ShapeDtypeStruct((ns, sw, S5_SLAB), BF16),
                 jax.ShapeDtypeStruct((lsl, LRU_SLAB, LRU_SLAB), BF16),
                 jax.ShapeDtypeStruct((lsl, LRU_SLAB, LRU_SLAB), BF16)]
    return _run(
        body, comm, name="prep", grid=(1,), out_shape=out_shape, in_specs=[_whole(a.shape) for a in args],
        out_specs=[_acc(s.shape) for s in out_shape], semantics="arbitrary",
    )(*args)


def _s5_param_grads(lam_r, lam_i, sr, si, u, dy, pch, n, comm=None):
    t, gn = lam_r.shape
    sa = u.shape[1]
    sw = S5_SLAB * n // pch

    def body(lr_ref, li_ref, sr_ref, si_ref, u_ref, dy_ref, dbr_ref, dbi_ref, dcr_ref, dci_ref):
        uv = u_ref[...]
        dyv = dy_ref[...]
        dbr_ref[...] = _fold_diag_blocks(_dot_tn(uv, lr_ref[...]), pch, n).astype(dbr_ref.dtype)
        dbi_ref[...] = _fold_diag_blocks(_dot_tn(uv, li_ref[...]), pch, n).astype(dbi_ref.dtype)
        dcr_ref[...] = _fold_diag_blocks(_dot_tn(sr_ref[...], dyv), n, pch).astype(dcr_ref.dtype)
        dci_ref[...] = _fold_diag_blocks(_dot_tn(si_ref[...], dyv), n, pch).astype(dci_ref.dtype)

    states = pl.BlockSpec((t, sw), lambda m: (0, m))
    chans = pl.BlockSpec((t, S5_SLAB), lambda m: (0, m))
    return _run(
        body, comm, name="s5_param_grads", grid=(sa // S5_SLAB,),
        out_shape=[jax.ShapeDtypeStruct((pch, gn), BF16), jax.ShapeDtypeStruct((pch, gn), BF16),
                   jax.ShapeDtypeStruct((n, sa), BF16), jax.ShapeDtypeStruct((n, sa), BF16)],
        in_specs=[states, states, states, states, chans, chans],
        out_specs=[pl.BlockSpec((pch, sw), lambda m: (0, m)), pl.BlockSpec((pch, sw), lambda m: (0, m)),
                   pl.BlockSpec((n, S5_SLAB), lambda m: (0, m)), pl.BlockSpec((n, S5_SLAB), lambda m: (0, m))],
        semantics="parallel",
    )(lam_r, lam_i, sr, si, u, dy)


SMALL_PARTS = ["vec_tail", "vec_ffn", "vec_lru", "vec_mix", "vec_bin", "vec_sa", "vec_gn", "dw_r", "dw_i", "dbb_re",
               "dbb_im", "dc_re", "dc_imn", "loss"]


def _small_reduce(parts, shapes, lr_col, li_col, ldt_col, b_re, b_im, groups, comm=None):
    gn, pch = b_re.shape
    n = gn // groups
    nparts = parts[SMALL_PARTS[0]].size // math.prod(shapes[SMALL_PARTS[0]])
    np_, nout = len(SMALL_PARTS), 24

    def body(*refs):
        ins = refs[:np_]
        lr, li, ldt, bre, bim = refs[np_:np_ + 5]
        outs = refs[np_ + 5:np_ + 5 + nout]
        sums = dict(zip(SMALL_PARTS, refs[np_ + 5 + nout:]))

        @pl.when(pl.program_id(0) == 0)
        def _():
            for k, r in zip(SMALL_PARTS, ins):
                sums[k][...] = r[...].astype(F32)

        @pl.when(pl.program_id(0) > 0)
        def _():
            for k, r in zip(SMALL_PARTS, ins):
                sums[k][...] += r[...].astype(F32)

        @pl.when(pl.program_id(0) == nparts - 1)
        def _():
            finish({k: s[...] for k, s in sums.items()}, lr, li, ldt, bre, bim, *outs)

    def finish(tot, lr, li, ldt, bre, bim, o_loss, o_gmix, o_bin, o_bglu, o_s5d, o_convb, o_br, o_bi, o_lam, o_gffn,
               o_gpg, o_bpg, o_gple, o_gfin, o_wr, o_wi, o_cre, o_cim, o_bre, o_bim, o_lre, o_lim, o_ldt, o_convw):
        o_loss[...] = tot["loss"]
        o_convw[...] = tot["vec_lru"][SUBLANES - CONV_WIDTH:SUBLANES]
        o_bpg[...] = tot["vec_tail"][0:1]
        o_gpg[...] = tot["vec_tail"][1:2]
        o_gple[...] = tot["vec_tail"][2:3]
        o_gfin[...] = tot["vec_tail"][3:4]
        o_gffn[...] = tot["vec_ffn"][0:1]
        o_convb[...] = tot["vec_lru"][0:1]
        o_br[...] = tot["vec_lru"][1:2]
        o_bi[...] = tot["vec_lru"][2:3]
        o_lam[...] = tot["vec_lru"][3:4]
        o_gmix[...] = tot["vec_mix"][0:1]
        o_bin[...] = tot["vec_bin"][0:1]
        o_bglu[...] = tot["vec_sa"][0:1]
        o_s5d[...] = tot["vec_sa"][1:2]
        o_wr[...] = tot["dw_r"].T
        o_wi[...] = tot["dw_i"].T
        o_cre[...] = tot["dc_re"].T
        o_cim[...] = -tot["dc_imn"].T
        d_a = tot["vec_gn"].T
        _, chain = jax.vjp(_disc_cols, lr[...], li[...], ldt[...], bre[...], bim[...])
        d_lr, d_li, d_ldt, d_bre, d_bim = chain((d_a[:, 0:1], d_a[:, 1:2], tot["dbb_re"].T, tot["dbb_im"].T))
        o_lre[...] = d_lr
        o_lim[...] = d_li
        o_bre[...] = d_bre
        o_bim[...] = d_bim
        same = (lax.broadcasted_iota(jnp.int32, (groups, gn), 0)
                == lax.broadcasted_iota(jnp.int32, (groups, gn), 1) // n).astype(F32)
        o_ldt[...] = jnp.dot(same, d_ldt * jnp.ones((1, LANES), F32), preferred_element_type=F32,
                             precision=lax.Precision.HIGHEST)[:, 0:1]

    d = shapes["vec_mix"][1]
    nz = shapes["vec_bin"][1]
    sa = shapes["vec_sa"][1]
    w = shapes["vec_lru"][1]
    row = lambda c: jax.ShapeDtypeStruct((1, c), F32)
    out_shape = [jax.ShapeDtypeStruct(shapes["loss"], F32), row(d), row(nz), row(sa), row(sa), row(w), row(w), row(w),
                 row(w), row(d), row(d), row(d), row(d), row(d),
                 jax.ShapeDtypeStruct(shapes["dw_r"][::-1], F32), jax.ShapeDtypeStruct(shapes["dw_i"][::-1], F32),
                 jax.ShapeDtypeStruct(shapes["dc_re"][::-1], F32), jax.ShapeDtypeStruct(shapes["dc_imn"][::-1], F32),
                 jax.ShapeDtypeStruct((gn, pch), F32), jax.ShapeDtypeStruct((gn, pch), F32),
                 jax.ShapeDtypeStruct((gn, 1), F32), jax.ShapeDtypeStruct((gn, 1), F32),
                 jax.ShapeDtypeStruct((groups, 1), F32), jax.ShapeDtypeStruct((CONV_WIDTH, w), F32)]
    def part_spec(k):
        r, c = shapes[k]
        if parts[k].ndim == 3:
            return pl.BlockSpec((None, r, c), lambda i: (i, 0, 0))
        return pl.BlockSpec((r, c), lambda i: (i, 0))

    outs = _run(
        body, comm, name="small_reduce", grid=(nparts,), out_shape=out_shape,
        in_specs=[part_spec(k) for k in SMALL_PARTS] + [_whole(a.shape) for a in (lr_col, li_col, ldt_col, b_re, b_im)],
        out_specs=[_acc(s.shape) for s in out_shape],
        scratch_shapes=[pltpu.VMEM(shapes[k], F32) for k in SMALL_PARTS],
        semantics="arbitrary",
    )(*[parts[k] for k in SMALL_PARTS], lr_col, li_col, ldt_col, b_re, b_im)
    extra = None
    if comm is not None:
        outs, extra = outs
    names = ["loss", "g_mix", "b_in", "b_glu", "s5_d", "conv_b", "b_r", "b_i", "lru_lambda", "g_ffn", "g_ple_gate",
             "b_ple_gate", "g_ple", "g_final", "w_r", "w_i", "s5_c_re", "s5_c_im", "s5_b_re", "s5_b_im", "lam_re",
             "lam_im", "log_dt", "conv_w"]
    res = dict(zip(names, outs))
    return res if comm is None else (res, extra)


def _adamw_small(ws, gs, ms, vs):
    n = len(ws)

    def body(*refs):
        w_r, g_r, m_r, v_r = (refs[i * n:(i + 1) * n] for i in range(4))
        g_o, d_o, m_o, v_o = (refs[(4 + i) * n:(5 + i) * n] for i in range(4))
        for i in range(n):
            g = g_r[i][...]
            delta, m_new, v_new = _adamw_math(w_r[i][...], g, m_r[i][...], v_r[i][...])
            g_o[i][...] = g
            d_o[i][...] = delta
            m_o[i][...] = m_new
            v_o[i][...] = v_new

    shapes = [jax.ShapeDtypeStruct(a.shape, F32) for a in ws]
    outs = pl.pallas_call(body, name="adamw_small", out_shape=shapes * 4)(*ws, *gs, *ms, *vs)
    return outs[:n], outs[n:2 * n], outs[2 * n:3 * n], outs[3 * n:]


def _adamw_math(w, g, m, v):
    m_new = ADAM_B1 * m + (1.0 - ADAM_B1) * g
    v_new = ADAM_B2 * v + (1.0 - ADAM_B2) * (g * g)
    m_hat = m_new / (1.0 - ADAM_B1 ** ADAM_STEP)
    v_hat = v_new / (1.0 - ADAM_B2 ** ADAM_STEP)
    delta = -ADAM_LR * (m_hat / (jnp.sqrt(v_hat) + ADAM_EPS) + ADAM_WD * w)
    return delta, m_new, v_new


def _row_tile(rows):
    for cand in range(256, 0, -16):
        if rows % cand == 0:
            return cand
    return rows


def _adamw(parts, w, m, v, name, transposed=False):
    nw = len(w)
    rows, cols = w[0].shape
    npart = parts[0].shape[0]
    tr = _row_tile(rows)
    if transposed:
        parts_spec = pl.BlockSpec((npart, cols, tr), lambda i: (0, 0, i))
    else:
        parts_spec = pl.BlockSpec((npart, tr, cols), lambda i: (0, i, 0))

    def body(*refs):
        for i in range(nw):
            p_ref, w_ref, m_ref, v_ref = refs[4 * i:4 * i + 4]
            g_ref, d_ref, mo_ref, vo_ref = refs[4 * (nw + i):4 * (nw + i) + 4]
            g = p_ref[0].astype(F32)
            for k in range(1, npart):
                g = g + p_ref[k].astype(F32)
            if transposed:
                g = g.T
            delta, m_new, v_new = _adamw_math(w_ref[...], g, m_ref[...], v_ref[...])
            g_ref[...] = g
            d_ref[...] = delta
            mo_ref[...] = m_new
            vo_ref[...] = v_new

    res = pl.pallas_call(
        body, name=name, grid=(rows // tr,),
        out_shape=[jax.ShapeDtypeStruct((rows, cols), F32)] * (4 * nw),
        in_specs=([parts_spec] + [_rows(tr, cols)] * 3) * nw,
        out_specs=[_rows(tr, cols)] * (4 * nw),
        compiler_params=_params("parallel"),
    )(*[a for group in zip(parts, w, m, v) for a in group])
    return [res[4 * i:4 * i + 4] for i in range(nw)]


def _mesh_position():
    return lax.axis_index("x"), lax.axis_index("y"), lax.axis_index("c")


def _flip(pos, rel):
    x, y, c = pos
    return (1 - x if rel & 4 else x, 1 - y if rel & 2 else y, 1 - c if rel & 1 else c)


def _index(pos):
    return 4 * pos[0] + 2 * pos[1] + pos[2]


_ANY = pl.BlockSpec(memory_space=pl.ANY)
FLAT_ROWS = 32


def _dma_sems(n):
    return [pltpu.SemaphoreType.DMA((n, N_DEV - 1)), pltpu.SemaphoreType.DMA((n, N_DEV - 1)), pltpu.SemaphoreType.DMA((n,))]


def _block_of(ref, idx, rows, flat):
    if flat:
        return ref.at[pl.ds(pl.multiple_of(idx * rows, FLAT_ROWS), rows), :]
    return ref.at[idx]


class _Gather:
    chips = (4, 2, 6)
    rels = frozenset((1, 4, 2, 6))

    def __init__(self, shards):
        self.inputs = list(shards)
        self.flat = [s.shape[0] % FLAT_ROWS == 0 for s in shards]
        self.out_shape = [
            jax.ShapeDtypeStruct((N_DEV * s.shape[0], s.shape[1]) if f else (N_DEV,) + s.shape, s.dtype)
            for s, f in zip(shards, self.flat)]
        self.sems = _dma_sems(len(shards))

    def _copy(self, ins, outs, sems, i, k, block, to, own=False):
        dst = _block_of(outs[i], _index(block), self.inputs[i].shape[0], self.flat[i])
        return pltpu.make_async_remote_copy(
            src_ref=ins[i] if own else dst, dst_ref=dst, send_sem=sems[0].at[i, k], recv_sem=sems[1].at[i, k],
            device_id=to, device_id_type=MESH)

    def _local(self, ins, outs, sems, i, me):
        dst = _block_of(outs[i], _index(me), self.inputs[i].shape[0], self.flat[i])
        return pltpu.make_async_copy(ins[i], dst, sems[2].at[i])

    def _first(self, ins, outs, sems, i, me):
        cps = [self._copy(ins, outs, sems, i, 0, me, _flip(me, 1), own=True)]
        cps += [self._copy(ins, outs, sems, i, 1 + j, me, _flip(me, rel), own=True) for j, rel in enumerate(self.chips)]
        return cps

    def _passed(self, ins, outs, sems, i, j, me):
        return self._copy(ins, outs, sems, i, 4 + j, _flip(me, self.chips[j]), _flip(me, 1))

    def before(self, ins, outs, sems):
        n = len(self.inputs)
        me = _mesh_position()

        @pl.when(pl.program_id(0) == 0)
        def _():
            for i in range(n):
                self._local(ins, outs, sems, i, me).start()
                for cp in self._first(ins, outs, sems, i, me):
                    cp.start()

        @pl.when(pl.program_id(0) == pl.num_programs(0) - 1)
        def _():
            for j, rel in enumerate(self.chips):
                for i in range(n):
                    self._copy(ins, outs, sems, i, 1 + j, _flip(me, rel), me).wait_recv()
                    self._passed(ins, outs, sems, i, j, me).start()

    def after(self, ins, outs, sems):
        n = len(self.inputs)
        me = _mesh_position()
        sibling = _flip(me, 1)

        @pl.when(pl.program_id(0) == pl.num_programs(0) - 1)
        def _():
            for i in range(n):
                self._copy(ins, outs, sems, i, 0, sibling, me).wait_recv()
                for j, rel in enumerate(self.chips):
                    self._copy(ins, outs, sems, i, 4 + j, _flip(sibling, rel), me).wait_recv()
            for i in range(n):
                for cp in self._first(ins, outs, sems, i, me):
                    cp.wait_send()
                for j in range(len(self.chips)):
                    self._passed(ins, outs, sems, i, j, me).wait_send()
                self._local(ins, outs, sems, i, me).wait()


N_CHIPS = N_DEV // 2


def _chip(pos):
    return 2 * pos[0] + pos[1]


class _PairSwap:
    rels = frozenset((1,))

    def __init__(self, arrays):
        self.inputs = list(arrays)
        self.rows = [a.shape[0] // N_DEV for a in arrays]
        for r in self.rows:
            assert r % FLAT_ROWS == 0, r
        self.out_shape = [jax.ShapeDtypeStruct((N_CHIPS, r, a.shape[1]), a.dtype) for a, r in zip(arrays, self.rows)]
        n = len(arrays)
        self.sems = [pltpu.SemaphoreType.DMA((n, N_CHIPS)), pltpu.SemaphoreType.DMA((n, N_CHIPS))]

    def _copy(self, ins, outs, sems, i, j, me):
        sibling = _flip(me, 1)
        return pltpu.make_async_remote_copy(
            src_ref=_block_of(ins[i], 2 * j + sibling[2], self.rows[i], True), dst_ref=outs[i].at[j],
            send_sem=sems[0].at[i, j], recv_sem=sems[1].at[i, j], device_id=sibling, device_id_type=MESH)

    def before(self, ins, outs, sems):
        me = _mesh_position()

        @pl.when(pl.program_id(0) == 0)
        def _():
            for i in range(len(self.inputs)):
                for j in range(N_CHIPS):
                    self._copy(ins, outs, sems, i, j, me).start()

    def after(self, ins, outs, sems):
        me = _mesh_position()

        @pl.when(pl.program_id(0) == pl.num_programs(0) - 1)
        def _():
            for i in range(len(self.inputs)):
                for j in range(N_CHIPS):
                    self._copy(ins, outs, sems, i, j, me).wait()


class _ChipExchange:
    chips = (4, 2, 6)
    rels = frozenset(chips)

    def __init__(self, arrays):
        self.inputs = list(arrays)
        self.out_shape = [jax.ShapeDtypeStruct(a.shape, a.dtype) for a in arrays]
        n = len(arrays)
        self.sems = [pltpu.SemaphoreType.DMA((n, 3)), pltpu.SemaphoreType.DMA((n, 3)), pltpu.SemaphoreType.DMA((n,))]

    def _send(self, ins, outs, sems, i, k, me):
        peer = _flip(me, self.chips[k])
        return pltpu.make_async_remote_copy(
            src_ref=ins[i].at[_chip(peer)], dst_ref=outs[i].at[_chip(me)], send_sem=sems[0].at[i, k],
            recv_sem=sems[1].at[i, k], device_id=peer, device_id_type=MESH)

    def _arrival(self, ins, outs, sems, i, k, me):
        peer = _flip(me, self.chips[k])
        return pltpu.make_async_remote_copy(
            src_ref=ins[i].at[_chip(me)], dst_ref=outs[i].at[_chip(peer)], send_sem=sems[0].at[i, k],
            recv_sem=sems[1].at[i, k], device_id=peer, device_id_type=MESH)

    def _local(self, ins, outs, sems, i, me):
        return pltpu.make_async_copy(ins[i].at[_chip(me)], outs[i].at[_chip(me)], sems[2].at[i])

    def before(self, ins, outs, sems):
        me = _mesh_position()

        @pl.when(pl.program_id(0) == 0)
        def _():
            for i in range(len(self.inputs)):
                self._local(ins, outs, sems, i, me).start()
                for k in range(len(self.chips)):
                    self._send(ins, outs, sems, i, k, me).start()

    def after(self, ins, outs, sems):
        me = _mesh_position()

        @pl.when(pl.program_id(0) == pl.num_programs(0) - 1)
        def _():
            for i in range(len(self.inputs)):
                for k in range(len(self.chips)):
                    self._arrival(ins, outs, sems, i, k, me).wait_recv()
            for i in range(len(self.inputs)):
                for k in range(len(self.chips)):
                    self._send(ins, outs, sems, i, k, me).wait_send()
                self._local(ins, outs, sems, i, me).wait()


def _pair_add(grads, halves, name):
    n = len(grads)

    def body(*refs):
        g_refs, h_refs, o_refs = refs[:n], refs[n:2 * n], refs[2 * n:]
        c = lax.axis_index("c")
        for i in range(n):
            r = h_refs[i].shape[1]
            for j in range(N_CHIPS):
                own = g_refs[i][pl.ds(pl.multiple_of((2 * j + c) * r, FLAT_ROWS), r), :]
                o_refs[i][j] = (own.astype(F32) + h_refs[i][j].astype(F32)).astype(o_refs[i].dtype)

    return pl.pallas_call(
        body, name=name, out_shape=[jax.ShapeDtypeStruct(h.shape, h.dtype) for h in halves],
        compiler_params=pltpu.CompilerParams(vmem_limit_bytes=VMEM_LIMIT),
    )(*grads, *halves)


class _Both:
    def __init__(self, first, second):
        self.jobs = (first, second)
        self.rels = first.rels | second.rels
        self.inputs = first.inputs + second.inputs
        self.out_shape = first.out_shape + second.out_shape
        self.sems = first.sems + second.sems

    def _each(self, ins, outs, sems):
        a = self.jobs[0]
        i, o, s = len(a.inputs), len(a.out_shape), len(a.sems)
        return ((a, ins[:i], outs[:o], sems[:s]), (self.jobs[1], ins[i:], outs[o:], sems[s:]))

    def before(self, ins, outs, sems):
        for job, i, o, s in self._each(ins, outs, sems):
            job.before(i, o, s)

    def after(self, ins, outs, sems):
        for job, i, o, s in self._each(ins, outs, sems):
            job.after(i, o, s)


_COLLECTIVE_IDS = {(1,): 0, (2, 4, 6): 1, (1, 2, 4, 6): 2}


def _entry_barrier(rels):
    @pl.when(pl.program_id(0) == 0)
    def _():
        me = _mesh_position()
        sem = pltpu.get_barrier_semaphore()
        for rel in rels:
            pl.semaphore_signal(sem, inc=1, device_id=_flip(me, rel), device_id_type=MESH)
        pl.semaphore_wait(sem, len(rels))


def _run(body, comm, *, semantics, out_shape, in_specs, out_specs, scratch_shapes=(), **kw):
    if comm is None:
        return pl.pallas_call(body, out_shape=out_shape, in_specs=in_specs, out_specs=out_specs,
                              scratch_shapes=list(scratch_shapes), compiler_params=_params(semantics), **kw)
    single = not isinstance(out_shape, (list, tuple))
    outs = [out_shape] if single else list(out_shape)
    ospecs = [out_specs] if single else list(out_specs)
    counts = [len(in_specs), len(comm.inputs), len(outs), len(comm.out_shape), len(scratch_shapes), len(comm.sems)]
    rels = tuple(sorted(comm.rels))

    def carrying(*refs):
        groups, pos = [], 0
        for c in counts:
            groups.append(refs[pos:pos + c])
            pos += c
        main_in, comm_in, main_out, comm_out, main_scratch, comm_sems = groups
        _entry_barrier(rels)
        comm.before(comm_in, comm_out, comm_sems)
        body(*main_in, *main_out, *main_scratch)
        comm.after(comm_in, comm_out, comm_sems)

    call = pl.pallas_call(
        carrying, out_shape=outs + list(comm.out_shape), in_specs=list(in_specs) + [_ANY] * len(comm.inputs),
        out_specs=ospecs + [_ANY] * len(comm.out_shape), scratch_shapes=list(scratch_shapes) + list(comm.sems),
        compiler_params=pltpu.CompilerParams(dimension_semantics=("arbitrary",), vmem_limit_bytes=VMEM_LIMIT,
                                             collective_id=_COLLECTIVE_IDS[rels]), **kw)

    def apply(*args):
        res = call(*args, *comm.inputs)
        main = res[:len(outs)]
        return (main[0] if single else list(main)), list(res[len(outs):])

    return apply


def _alone(comm, name):
    return _run(lambda: None, comm, semantics="arbitrary", name=name, grid=(1,), out_shape=[], in_specs=[], out_specs=[])()[1]


SHARDED = {"w_in": 1, "w_glu": 0, "conv_w": 1, "w_a_out": 1, "w_b_out": 0, "w_o": 0, "w_ffn_gate": 1, "w_ffn_up": 1,
           "w_ffn_down": 0, "w_ple_gate": 0, "w_ple": 1}
TRANSPOSED = ("w_in", "w_a_out", "w_ffn_gate", "w_ffn_up", "w_ple")
LONG_AXIS_MINOR = ("w_in", "w_ffn_gate", "w_ffn_up")
NARROW_LAST = ("s5_b_re", "s5_b_im", "s5_d")
ADAMW_GROUPS = (("w_in",), ("w_glu",), ("conv_w",), ("w_a_out",), ("w_b_out", "w_o", "w_ple_gate"),
                ("w_ffn_gate", "w_ffn_up"), ("w_ffn_down",), ("w_ple",))
SMALL = ["g_mix", "b_in", "lam_re", "lam_im", "log_dt", "s5_b_re", "s5_b_im", "s5_c_re", "s5_c_im", "s5_d", "b_glu",
         "conv_b", "w_r", "b_r", "w_i", "b_i", "lru_lambda", "g_ffn", "g_ple_gate", "b_ple_gate", "g_ple", "g_final"]
WEIGHTS = ["g_mix", "w_in", "b_in", "lam_re", "lam_im", "log_dt", "s5_b_re", "s5_b_im", "s5_c_re", "s5_c_im", "s5_d",
           "w_glu", "b_glu", "conv_w", "conv_b", "w_r", "b_r", "w_i", "b_i", "lru_lambda", "w_a_out", "w_b_out", "w_o",
           "g_ffn", "w_ffn_gate", "w_ffn_up", "w_ffn_down", "g_ple_gate", "w_ple_gate", "b_ple_gate", "w_ple", "g_ple",
           "g_final"]


def _join_columns(gathered):
    nb, r, c = gathered.shape
    return jnp.transpose(gathered, (1, 0, 2)).reshape(r, nb * c)


def _disc_scalars(lr, li, ldt):
    dt = jnp.exp(ldt)
    mag = jnp.exp(lr * dt)
    ar = mag * jnp.cos(li * dt)
    ai = mag * jnp.sin(li * dt)
    den = lr * lr + li * li
    nr = ar - 1.0
    fr = (nr * lr + ai * li) / den
    fi = (ai * lr - nr * li) / den
    return ar, ai, fr, fi


def _disc_cols(lr, li, ldt, b_re, b_im):
    ar, ai, fr, fi = _disc_scalars(lr, li, ldt)
    return ar, ai, fr * b_re - fi * b_im, fr * b_im + fi * b_re


def _local_step(x, p, target, src, small, disc, distributed=True):
    full = {} if distributed else dict(src)
    gw, halves, pairs, got = {}, {}, {}, {}

    def gather(keys):
        return (_Gather([src[k] for k in keys]), keys, full) if distributed else None

    def swap(keys):
        return (_PairSwap([gw[k] for k in keys]), keys, halves) if distributed else None

    def chips(keys):
        return (_ChipExchange([pairs[k] for k in keys]), keys, got) if distributed else None

    def add_pairs(keys):
        if distributed:
            pairs.update(zip(keys, _pair_add([gw[k] for k in keys], [halves[k] for k in keys], "pair_add_" + keys[0])))

    def carry(fn, *args, jobs=()):
        jobs = [j for j in jobs if j is not None]
        if not jobs:
            return fn(*args)
        comm = jobs[0][0]
        for j in jobs[1:]:
            comm = _Both(comm, j[0])
        res, extra = fn(*args, comm=comm)
        for job, keys, sink in jobs:
            sink.update(zip(keys, extra[:len(job.out_shape)]))
            extra = extra[len(job.out_shape):]
        return res

    d = x.shape[1]
    g, n, pch = small["s5_b_re"].shape
    sa, lw = g * pch, small["lru_lambda"].shape[-1]
    widths = [sa, lw, d, d]
    row = lambda v: v.reshape(1, -1)

    hd = small["w_r"].shape[-1]
    ar_row, ai_row, bbr_blk, bbi_blk, cre_blk, cimn_blk, wr_blk, wi_blk = carry(
        _prep, *disc["rows"], *disc["cols"], disc["b_re"], disc["b_im"], small["s5_c_re"].reshape(sa, n),
        small["s5_c_im"].reshape(sa, n), small["w_r"].reshape(lw, hd), small["w_i"].reshape(lw, hd),
        jobs=[gather(["w_in"])])
    d_row = row(small["s5_d"])
    u_a, u_b, za, zb = carry(_inproj_fwd, x, row(small["g_mix"]), full["w_in"], row(small["b_in"]), widths,
                             jobs=[gather(["w_glu", "conv_w", "w_a_out", "w_b_out"])])
    conv_w = _join_columns(full["conv_w"]) if distributed else full["conv_w"]
    branches = [_s5_fwd(u_a, bbr_blk, bbi_blk, ar_row, ai_row, cre_blk, cimn_blk, d_row, full["w_glu"],
                        row(small["b_glu"])),
                _lru_fwd(u_b, conv_w, row(small["conv_b"]), wr_blk, row(small["b_r"]), wi_blk, row(small["b_i"]),
                         row(small["lru_lambda"]))]
    (sr, si, y, y_a), (xc, h, hprev) = carry(_stages, branches, "branches_fwd",
                                             jobs=[gather(["w_ffn_gate", "w_ffn_up", "w_o"])])
    x1, merged, ma, mb, fg, fu = carry(
        _merge_ffn_up_fwd, y_a, h, za, zb, x, full["w_a_out"], full["w_b_out"], full["w_o"], row(small["g_ffn"]),
        full["w_ffn_gate"], full["w_ffn_up"], jobs=[gather(["w_ffn_down"])])
    x2 = carry(_ffn_down_fwd, fg, fu, x1, full["w_ffn_down"], jobs=[gather(["w_ple_gate", "w_ple"])])

    dx2, loss_blk, gw["w_ple_gate"], gw["w_ple"], vec_tail = _tail_fwd_bwd(
        x2, p, target, row(small["g_ple_gate"]), full["w_ple_gate"], row(small["b_ple_gate"]), full["w_ple"],
        row(small["g_ple"]), row(small["g_final"]))
    dfg, dfu, act = carry(_ffn_bwd_a, dx2, fg, fu, full["w_ffn_down"], jobs=[swap(["w_ple_gate", "w_ple"])])
    tn_d = min(d, 512)
    gw["w_ffn_down"] = _matmul_tn(act, dx2, tn_d, "dw_ffn_down", BF16)
    add_pairs(["w_ple_gate", "w_ple"])
    dx1, h2, vec_ffn = carry(_ffn_bwd_b, dfg, dfu, x1, dx2, row(small["g_ffn"]), full["w_ffn_gate"], full["w_ffn_up"],
                             jobs=[chips(["w_ple_gate", "w_ple"]), swap(["w_ffn_down"])])
    gw["w_ffn_gate"] = _matmul_tn(dfg, h2, tn_d, "dw_ffn_gate", BF16)
    gw["w_ffn_up"] = _matmul_tn(dfu, h2, tn_d, "dw_ffn_up", BF16)
    add_pairs(["w_ffn_down"])
    dza, dzb, dya, dyb, gw["w_o"], gw["w_a_out"], gw["w_b_out"] = carry(
        _merge_bwd, dx1, merged, ma, mb, za, zb, y_a, h, full["w_o"], full["w_a_out"], full["w_b_out"],
        jobs=[chips(["w_ffn_down"]), swap(["w_ffn_gate", "w_ffn_up"])])
    add_pairs(["w_ffn_gate", "w_ffn_up"])
    du_b, dw_r, dw_i, vec_lru = carry(
        _lru_bwd, dyb, xc, hprev, u_b, conv_w, wr_blk, row(small["b_r"]), wi_blk, row(small["b_i"]),
        row(small["lru_lambda"]), hd, jobs=[chips(["w_ffn_gate"]), swap(["w_o", "w_a_out", "w_b_out"])])
    add_pairs(["w_o", "w_a_out", "w_b_out"])
    du_a, lam_r, lam_i, dy16, gw["w_glu"], vec_sa, vec_gn = carry(
        _s5_bwd, dya, y, sr, si, u_a, full["w_glu"], row(small["b_glu"]), cre_blk, cimn_blk, bbr_blk, bbi_blk, ar_row,
        ai_row, d_row, jobs=[chips(["w_ffn_up"]), chips(["w_o", "w_a_out", "w_b_out"])])
    smalls = {"vec_tail": vec_tail, "vec_ffn": vec_ffn, "vec_lru": vec_lru, "vec_sa": vec_sa, "vec_gn": vec_gn,
              "dw_r": dw_r, "dw_i": dw_i, "loss": loss_blk}
    everyones = {}

    def gather_smalls(keys):
        return (_Gather([smalls[k] for k in keys]), keys, everyones) if distributed else None

    smalls["dbb_re"], smalls["dbb_im"], smalls["dc_re"], smalls["dc_imn"] = carry(
        _s5_param_grads, lam_r, lam_i, sr, si, u_a, dy16, pch, n, jobs=[swap(["w_glu"]), gather_smalls(list(smalls))])
    add_pairs(["w_glu"])
    dz = [du_a, du_b, dza, dzb]
    grad_x, h0, smalls["vec_mix"], smalls["vec_bin"] = _inproj_bwd(
        dz, x, dx1, row(small["g_mix"]), full["w_in"])
    shapes = {k: a.shape for k, a in smalls.items()}
    gw["w_in"] = carry(_dw_from_parts, dz, h0, tn_d, "dw_in",
                       jobs=[chips(["w_glu"]),
                             gather_smalls(["dbb_re", "dbb_im", "dc_re", "dc_imn", "vec_mix", "vec_bin"])])
    smalls.update(everyones)
    if distributed:
        got["w_in"] = gw["w_in"]
        gw = got
    return grad_x, gw, smalls, shapes


def _disc_inputs(small):
    g, n, pch = small["s5_b_re"].shape
    srcs = (small["lam_re"], small["lam_im"], jnp.repeat(small["log_dt"], n))
    return {"rows": [a.reshape(1, g * n) for a in srcs], "cols": [a.reshape(g * n, 1) for a in srcs],
            "b_re": small["s5_b_re"].reshape(g * n, pch), "b_im": small["s5_b_im"].reshape(g * n, pch)}


def kernel(x, p, g_mix, w_in, b_in, lam_re, lam_im, log_dt, s5_b_re, s5_b_im, s5_c_re, s5_c_im, s5_d, w_glu, b_glu, conv_w, conv_b, w_r, b_r, w_i, b_i, lru_lambda, w_a_out, w_b_out, w_o, g_ffn, w_ffn_gate, w_ffn_up, w_ffn_down, g_ple_gate, w_ple_gate, b_ple_gate, w_ple, g_ple, g_final, loss_target, m_g_mix, m_w_in, m_b_in, m_lam_re, m_lam_im, m_log_dt, m_s5_b_re, m_s5_b_im, m_s5_c_re, m_s5_c_im, m_s5_d, m_w_glu, m_b_glu, m_conv_w, m_conv_b, m_w_r, m_b_r, m_w_i, m_b_i, m_lru_lambda, m_w_a_out, m_w_b_out, m_w_o, m_g_ffn, m_w_ffn_gate, m_w_ffn_up, m_w_ffn_down, m_g_ple_gate, m_w_ple_gate, m_b_ple_gate, m_w_ple, m_g_ple, m_g_final, v_g_mix, v_w_in, v_b_in, v_lam_re, v_lam_im, v_log_dt, v_s5_b_re, v_s5_b_im, v_s5_c_re, v_s5_c_im, v_s5_d, v_w_glu, v_b_glu, v_conv_w, v_conv_b, v_w_r, v_b_r, v_w_i, v_b_i, v_lru_lambda, v_w_a_out, v_w_b_out, v_w_o, v_g_ffn, v_w_ffn_gate, v_w_ffn_up, v_w_ffn_down, v_g_ple_gate, v_w_ple_gate, v_b_ple_gate, v_w_ple, v_g_ple, v_g_final):
    given = dict(locals())
    wts = {k: given[k] for k in WEIGHTS}
    moms = {k: given["m_" + k] for k in WEIGHTS}
    vels = {k: given["v_" + k] for k in WEIGHTS}

    def drop_depth(k, a):
        return a if k == "g_final" else a[0]

    small = {k: drop_depth(k, wts[k]) for k in SMALL}
    shard = {k: wts[k][0] for k in SHARDED}
    names = list(SHARDED)

    def wire(k):
        if k == "conv_w":
            return shard[k]
        return (shard[k].T if k in TRANSPOSED else shard[k]).astype(BF16)

    disc = _disc_inputs(small)
    grad_x, parts, smalls, shapes = _local_step(x[0], p[0, 0], loss_target[0], {k: wire(k) for k in names}, small, disc)

    (pair_in,) = _pair_add([parts["w_in"]], _alone(_PairSwap([parts["w_in"]]), "swap_last"), "pair_add_w_in")

    g_small, (parts["w_in"],) = _small_reduce(smalls, shapes, *disc["cols"], disc["b_re"], disc["b_im"],
                                              small["s5_b_re"].shape[0], comm=_ChipExchange([pair_in]))
    loss = g_small.pop("loss")[0, 0]
    cols = shard["conv_w"].shape[1]
    mine = _index((lax.axis_index("x"), lax.axis_index("y"), lax.axis_index("c")))
    parts["conv_w"] = lax.dynamic_slice_in_dim(g_small.pop("conv_w"), mine * cols, cols, axis=1)[None]

    def view(k, a):
        a = a.reshape((1, -1) if k == "g_final" else wts[k].shape)
        return jnp.swapaxes(a, -1, -2) if k in NARROW_LAST else a

    def unview(k, a):
        return (jnp.swapaxes(a, -1, -2) if k in NARROW_LAST else a).reshape(wts[k].shape)

    slots = _adamw_small([view(k, wts[k]) for k in SMALL], [view(k, g_small[k]) for k in SMALL],
                         [view(k, moms[k]) for k in SMALL], [view(k, vels[k]) for k in SMALL])
    small_out = [dict(zip(SMALL, [unview(k, a) for k, a in zip(SMALL, slot)])) for slot in slots]

    big_out = {}
    for group in ADAMW_GROUPS:
        flip = group[0] in LONG_AXIS_MINOR
        look = (lambda a: a.T) if flip else (lambda a: a)
        res = _adamw([parts[k] for k in group], [look(shard[k]) for k in group], [look(moms[k][0]) for k in group],
                     [look(vels[k][0]) for k in group], "adamw_" + group[0],
                     transposed=group[0] in TRANSPOSED and not flip)
        for k, outs4 in zip(group, res):
            big_out[k] = [look(a) for a in outs4]

    outs = [loss, grad_x[None]]
    for slot in range(4):
        for k in WEIGHTS:
            if k in SHARDED:
                outs.append(big_out[k][slot][None])
            else:
                outs.append(small_out[slot][k])
    return tuple(outs)
```

```python
import math

import jax
import jax.numpy as jnp
from jax import lax
from jax.experimental import pallas as pl
from jax.experimental.pallas import tpu as pltpu

F32 = jnp.float32
BF16 = jnp.bfloat16

EPS = 1e-6
LRU_C = 8.0
CONV_WIDTH = 4
ADAM_LR = 0.001
ADAM_B1 = 0.9
ADAM_B2 = 0.999
ADAM_EPS = 1e-08
ADAM_WD = 0.01
ADAM_STEP = 10

N_DEV = 8
MESH = pl.DeviceIdType.MESH
SUBLANES = 8
LANES = 128
VMEM_LIMIT = 56 * 1024 * 1024
TOKEN_TILE = 256
TIME_CHUNK = 256
S5_SLAB = 128
LRU_SLAB = 256


def _dot(a, b):
    return jnp.dot(a.astype(BF16), b.astype(BF16), preferred_element_type=F32)


def _dot_nt(a, b):
    return lax.dot_general(a.astype(BF16), b.astype(BF16), (((1,), (1,)), ((), ())), preferred_element_type=F32)


def _dot_tn(a, b):
    return lax.dot_general(a.astype(BF16), b.astype(BF16), (((0,), (0,)), ((), ())), preferred_element_type=F32)


def _sigmoid(x):
    return jax.nn.sigmoid(x)


def _rms_stats(x):
    r = lax.rsqrt(jnp.mean(x * x, axis=-1, keepdims=True) + EPS)
    return x * r, r


def _rms_bwd(dy, xhat, r, g):
    dxn = dy * g
    dx = r * (dxn - xhat * jnp.mean(dxn * xhat, axis=-1, keepdims=True))
    return dx, dy * xhat


def _rowsum(v):
    return jnp.sum(v, axis=0, keepdims=True)


def _expm1(x):
    u = jnp.exp(x)
    um1 = u - 1.0
    safe = jnp.where(um1 == 0.0, 1.0, jnp.log(u))
    return jnp.where(um1 == 0.0, x, um1 * x / safe)


def _softplus(x):
    e = jnp.exp(-jnp.abs(x))
    u = 1.0 + e
    um1 = u - 1.0
    safe = jnp.where(um1 == 0.0, 1.0, um1)
    log1p_e = jnp.where(um1 == 0.0, e, jnp.log(u) * e / safe)
    return jnp.maximum(x, 0.0) + log1p_e


_GELU_K = math.sqrt(2.0 / math.pi)
_GELU_C = 0.044715


def _gelu(x):
    return 0.5 * x * (1.0 + jnp.tanh(_GELU_K * (x + _GELU_C * x * x * x)))


def _gelu_grad(x):
    th = jnp.tanh(_GELU_K * (x + _GELU_C * x * x * x))
    return 0.5 * (1.0 + th) + 0.5 * x * (1.0 - th * th) * _GELU_K * (1.0 + 3.0 * _GELU_C * x * x)


def _params(*sem):
    return pltpu.CompilerParams(dimension_semantics=sem, vmem_limit_bytes=VMEM_LIMIT)


def _rows(tm, n):
    return pl.BlockSpec((tm, n), lambda i: (i, 0))


def _rows_rev(tm, n, steps):
    return pl.BlockSpec((tm, n), lambda i: (steps - 1 - i, 0))


def _whole(shape):
    nd = len(shape)
    return pl.BlockSpec(shape, lambda i: (0,) * nd, pipeline_mode=pl.Buffered(1))


def _acc(shape):
    nd = len(shape)
    return pl.BlockSpec(shape, lambda i: (0,) * nd)


def _zero_on_first(*refs):
    @pl.when(pl.program_id(0) == 0)
    def _():
        for r in refs:
            r[...] = jnp.zeros_like(r)


def _inproj_fwd(x, g_mix, w_in_t, b_in, widths, comm=None):
    t, d = x.shape
    n = w_in_t.shape[0]
    tm = min(TOKEN_TILE, t)
    offs = [sum(widths[:i]) for i in range(len(widths) + 1)]

    def body(x_ref, g_ref, w_ref, b_ref, *outs):
        xhat, _ = _rms_stats(x_ref[...])
        h = (xhat * g_ref[...]).astype(BF16)
        for k, o_ref in enumerate(outs):
            lo, hi = offs[k], offs[k + 1]
            o_ref[...] = _dot_nt(h, w_ref[lo:hi, :]) + b_ref[:, lo:hi]

    return _run(
        body, comm, name="inproj_fwd", grid=(t // tm,),
        out_shape=[jax.ShapeDtypeStruct((t, w), F32) for w in widths],
        in_specs=[_rows(tm, d), _whole((1, d)), _whole((n, d)), _whole((1, n))],
        out_specs=[_rows(tm, w) for w in widths],
        semantics="parallel",
    )(x, g_mix, w_in_t, b_in)


def _s5_fwd(u, bbr_blk, bbi_blk, ar, ai, cre_blk, cimn_blk, d_skip, w_glu, b_glu):
    t, sa = u.shape
    ns, _, sw = bbr_blk.shape
    gn = ns * sw
    tc = min(TIME_CHUNK, t)

    def body(u_ref, bbr_ref, bbi_ref, ar_ref, ai_ref, cre_ref, cim_ref, d_ref, wg_ref, bg_ref,
             sr_ref, si_ref, y_ref, ya_ref, cr_s, ci_s, sr_s, si_s):
        _zero_on_first(cr_s, ci_s)
        uv = u_ref[...]
        ub = uv.astype(BF16)
        for m in range(ns):
            um = ub[:, m * S5_SLAB:(m + 1) * S5_SLAB]
            sr_s[:, m * sw:(m + 1) * sw] = _dot(um, bbr_ref[m])
            si_s[:, m * sw:(m + 1) * sw] = _dot(um, bbi_ref[m])
        a_r = ar_ref[...]
        a_i = ai_ref[...]

        def step(row, carry):
            c_r, c_i = carry
            at = pl.ds(row, 1)
            n_r = a_r * c_r - a_i * c_i + sr_s[at, :]
            n_i = a_r * c_i + a_i * c_r + si_s[at, :]
            sr_s[at, :] = n_r
            si_s[at, :] = n_i
            return n_r, n_i

        c_r, c_i = lax.fori_loop(0, tc, step, (cr_s[0:1, :], ci_s[0:1, :]), unroll=8)
        cr_s[0:1, :] = c_r
        ci_s[0:1, :] = c_i
        for m in range(ns):
            states, chans = slice(m * sw, (m + 1) * sw), slice(m * S5_SLAB, (m + 1) * S5_SLAB)
            s_r, s_i = sr_s[:, states].astype(BF16), si_s[:, states].astype(BF16)
            sr_ref[:, states] = s_r.astype(sr_ref.dtype)
            si_ref[:, states] = s_i.astype(si_ref.dtype)
            y_ref[:, chans] = _dot(s_r, cre_ref[m]) + _dot(s_i, cim_ref[m]) + d_ref[:, chans] * uv[:, chans]
        y = y_ref[...]
        zz = _gelu(y)
        q = _dot(zz, wg_ref[...]) + bg_ref[...]
        ya_ref[...] = zz * _sigmoid(q)

    return dict(
        body=body, steps=t // tc, args=(u, bbr_blk, bbi_blk, ar, ai, cre_blk, cimn_blk, d_skip, w_glu, b_glu),
        out_shape=[jax.ShapeDtypeStruct((t, gn), BF16), jax.ShapeDtypeStruct((t, gn), BF16),
                   jax.ShapeDtypeStruct((t, sa), F32), jax.ShapeDtypeStruct((t, sa), F32)],
        in_specs=[_rows(tc, sa), _whole(bbr_blk.shape), _whole(bbi_blk.shape), _whole((1, gn)), _whole((1, gn)),
                  _whole(cre_blk.shape), _whole(cimn_blk.shape), _whole((1, sa)), _whole((sa, sa)), _whole((1, sa))],
        out_specs=[_rows(tc, gn), _rows(tc, gn), _rows(tc, sa), _rows(tc, sa)],
        scratch=[pltpu.VMEM((SUBLANES, gn), F32), pltpu.VMEM((SUBLANES, gn), F32),
                 pltpu.VMEM((tc, gn), F32), pltpu.VMEM((tc, gn), F32)])


def _stages(stages, name, comm=None):
    counts = [[len(s[k]) for s in stages] for k in ("args", "out_shape", "scratch")]

    def body(*refs):
        groups, pos = [], 0
        for kind in counts:
            per_stage = []
            for c in kind:
                per_stage.append(refs[pos:pos + c])
                pos += c
            groups.append(per_stage)
        for s, ins, outs, scratch in zip(stages, *groups):
            s["body"](*ins, *outs, *scratch)

    res = _run(
        body, comm, name=name, grid=(stages[0]["steps"],),
        out_shape=[o for s in stages for o in s["out_shape"]], in_specs=[i for s in stages for i in s["in_specs"]],
        out_specs=[o for s in stages for o in s["out_specs"]], scratch_shapes=[c for s in stages for c in s["scratch"]],
        semantics="arbitrary",
    )(*[a for s in stages for a in s["args"]])
    extra = None
    if comm is not None:
        res, extra = res
    per_stage, pos = [], 0
    for c in counts[1]:
        per_stage.append(list(res[pos:pos + c]))
        pos += c
    return per_stage if comm is None else (per_stage, extra)


def _slab_dot(x, w_ref, transposed=False):
    dot = _dot_nt if transposed else _dot
    xb = x.astype(BF16)
    return jnp.concatenate([dot(xb[:, j * LRU_SLAB:(j + 1) * LRU_SLAB], w_ref[j]) for j in range(w_ref.shape[0])], axis=1)


def _lru_gates(xc, wr_ref, br_ref, wi_ref, bi_ref, lam_ref):
    r = _sigmoid(_slab_dot(xc, wr_ref) + br_ref[...])
    ig = _sigmoid(_slab_dot(xc, wi_ref) + bi_ref[...])
    sp = _softplus(-lam_ref[...])
    log_a = (-LRU_C * r) * sp
    return r, ig, sp, log_a


def _lru_fwd(u, conv_w, conv_b, wr_blk, b_r, wi_blk, b_i, lru_lambda):
    t, w = u.shape
    tc = min(TIME_CHUNK, t)
    halo = SUBLANES

    def body(u_ref, cw_ref, cb_ref, wr_ref, br_ref, wi_ref, bi_ref, lam_ref,
             xc_ref, h_ref, hp_ref, ext_s, a_s, carry_s):
        @pl.when(pl.program_id(0) == 0)
        def _():
            ext_s[0:halo, :] = jnp.zeros((halo, w), F32)
            carry_s[...] = jnp.zeros_like(carry_s)

        ext_s[halo:halo + tc, :] = u_ref[...]
        xc = cb_ref[...]
        for k in range(CONV_WIDTH):
            off = halo - (CONV_WIDTH - 1) + k
            xc = xc + cw_ref[k:k + 1, :] * ext_s[off:off + tc, :]
        ext_s[0:halo, :] = ext_s[tc:tc + halo, :]
        xc_ref[...] = xc
        r, ig, sp, log_a = _lru_gates(xc, wr_ref, br_ref, wi_ref, bi_ref, lam_ref)
        a_s[...] = jnp.exp(log_a)
        h_ref[...] = jnp.sqrt(-_expm1(2.0 * log_a)) * ig * xc

        def step(row, carry):
            at = pl.ds(row, 1)
            hp_ref[at, :] = carry
            nxt = a_s[at, :] * carry + h_ref[at, :]
            h_ref[at, :] = nxt
            return nxt

        carry_s[0:1, :] = lax.fori_loop(0, tc, step, carry_s[0:1, :], unroll=8)

    return dict(
        body=body, steps=t // tc, args=(u, conv_w, conv_b, wr_blk, b_r, wi_blk, b_i, lru_lambda),
        out_shape=[jax.ShapeDtypeStruct((t, w), F32)] * 3,
        in_specs=[_rows(tc, w), _whole((CONV_WIDTH, w)), _whole((1, w)), _whole(wr_blk.shape), _whole((1, w)),
                  _whole(wi_blk.shape), _whole((1, w)), _whole((1, w))],
        out_specs=[_rows(tc, w)] * 3,
        scratch=[pltpu.VMEM((halo + tc, w), F32), pltpu.VMEM((tc, w), F32), pltpu.VMEM((SUBLANES, w), F32)])


def _merge_ffn_up_fwd(y_a, h, za, zb, x, w_a_out_t, w_b_out, w_o, g_ffn, w_gate_t, w_up_t, comm=None):
    t, d = x.shape
    sa, lw = y_a.shape[1], h.shape[1]
    f = w_gate_t.shape[0]
    tm = min(TOKEN_TILE, t)

    def body(ya_ref, h_ref, za_ref, zb_ref, x_ref, wa_ref, wb_ref, wo_ref, g_ref, wg_ref, wu_ref,
             x1_ref, mg_ref, ma_ref, mb_ref, fg_ref, fu_ref):
        ma = _dot_nt(ya_ref[...], wa_ref[...])
        mb = _dot(h_ref[...], wb_ref[...])
        merged = _sigmoid(za_ref[...]) * ma + _sigmoid(zb_ref[...]) * mb
        ma_ref[...] = ma.astype(ma_ref.dtype)
        mb_ref[...] = mb.astype(mb_ref.dtype)
        mg_ref[...] = merged.astype(mg_ref.dtype)
        x1 = x_ref[...] + _dot(merged, wo_ref[...])
        x1_ref[...] = x1
        xhat, _ = _rms_stats(x1)
        h2 = (xhat * g_ref[...]).astype(BF16)
        fg_ref[...] = _dot_nt(h2, wg_ref[...]).astype(fg_ref.dtype)
        fu_ref[...] = _dot_nt(h2, wu_ref[...]).astype(fu_ref.dtype)

    return _run(
        body, comm, name="merge_ffn_up_fwd", grid=(t // tm,),
        out_shape=[jax.ShapeDtypeStruct((t, d), F32), jax.ShapeDtypeStruct((t, d), BF16),
                   jax.ShapeDtypeStruct((t, d), BF16), jax.ShapeDtypeStruct((t, d), BF16),
                   jax.ShapeDtypeStruct((t, f), BF16), jax.ShapeDtypeStruct((t, f), BF16)],
        in_specs=[_rows(tm, sa), _rows(tm, lw), _rows(tm, d), _rows(tm, d), _rows(tm, d),
                  _whole((d, sa)), _whole((lw, d)), _whole((d, d)), _whole((1, d)), _whole((f, d)), _whole((f, d))],
        out_specs=[_rows(tm, d)] * 4 + [_rows(tm, f)] * 2,
        semantics="parallel",
    )(y_a, h, za, zb, x, w_a_out_t, w_b_out, w_o, g_ffn, w_gate_t, w_up_t)


def _ffn_down_fwd(fg, fu, x1, w_down, comm=None):
    t, d = x1.shape
    f = fg.shape[1]
    tm = min(TOKEN_TILE, t)

    def body(fg_ref, fu_ref, x_ref, wd_ref, x2_ref):
        fgv = fg_ref[...].astype(F32)
        act = fgv * _sigmoid(fgv) * fu_ref[...].astype(F32)
        x2_ref[...] = x_ref[...] + _dot(act, wd_ref[...])

    return _run(
        body, comm, name="ffn_down_fwd", grid=(t // tm,),
        out_shape=jax.ShapeDtypeStruct((t, d), F32),
        in_specs=[_rows(tm, f), _rows(tm, f), _rows(tm, d), _whole((f, d))],
        out_specs=_rows(tm, d),
        semantics="parallel",
    )(fg, fu, x1, w_down)


def _store_on_last(pairs):
    @pl.when(pl.program_id(0) == pl.num_programs(0) - 1)
    def _():
        for acc, out in pairs:
            out[...] = acc[...].astype(out.dtype)


def _tail_fwd_bwd(x2, p, target, g_pg, w_pg, b_pg, w_ple_t, g_ple, g_final):
    t, d = x2.shape
    pd = p.shape[1]
    tm = min(TOKEN_TILE, t)

    def body(x2_ref, p_ref, tg_ref, gpg_ref, wpg_ref, bpg_ref, wple_ref, gple_ref, gfin_ref,
             dx2_ref, loss_ref, dwpg_out, dwple_out, vec_ref, dwpg_ref, dwple_ref):
        _zero_on_first(loss_ref, dwpg_ref, dwple_ref, vec_ref)
        x2v = x2_ref[...]
        xh2, r2 = _rms_stats(x2v)
        h3 = xh2 * gpg_ref[...]
        gp = _sigmoid(_dot(h3, wpg_ref[...]) + bpg_ref[...])
        pe = _dot_nt(p_ref[...], wple_ref[...])
        peh, r3 = _rms_stats(pe)
        e = peh * gple_ref[...]
        x3 = x2v + gp * e
        xh3, r4 = _rms_stats(x3)
        diff = xh3 * gfin_ref[...] - tg_ref[...]
        loss_ref[...] += 0.5 * jnp.sum(jnp.mean(diff * diff, axis=-1, keepdims=True))
        dy = diff * (1.0 / d)
        dx3, dgfin = _rms_bwd(dy, xh3, r4, gfin_ref[...])
        d_gp = dx3 * e
        d_e = dx3 * gp
        dpe, dgple = _rms_bwd(d_e, peh, r3, gple_ref[...])
        dwple_ref[...] += _dot_tn(dpe, p_ref[...])
        dpre = d_gp * gp * (1.0 - gp)
        dwpg_ref[...] += _dot_tn(h3, dpre)
        dh3 = _dot_nt(dpre, wpg_ref[...])
        dx2n, dgpg = _rms_bwd(dh3, xh2, r2, gpg_ref[...])
        dx2_ref[...] = dx3 + dx2n
        vec_ref[0:1, :] += _rowsum(dpre)
        vec_ref[1:2, :] += _rowsum(dgpg)
        vec_ref[2:3, :] += _rowsum(dgple)
        vec_ref[3:4, :] += _rowsum(dgfin)
        _store_on_last([(dwpg_ref, dwpg_out), (dwple_ref, dwple_out)])

    return pl.pallas_call(
        body, name="tail_fwd_bwd", grid=(t // tm,),
        out_shape=[jax.ShapeDtypeStruct((t, d), F32), jax.ShapeDtypeStruct((SUBLANES, LANES), F32),
                   jax.ShapeDtypeStruct((d, d), BF16), jax.ShapeDtypeStruct((d, pd), BF16),
                   jax.ShapeDtypeStruct((SUBLANES, d), F32)],
        in_specs=[_rows(tm, d), _rows(tm, pd), _rows(tm, d), _whole((1, d)), _whole((d, d)), _whole((1, d)),
                  _whole((d, pd)), _whole((1, d)), _whole((1, d))],
        out_specs=[_rows(tm, d), _acc((SUBLANES, LANES)), _acc((d, d)), _acc((d, pd)), _acc((SUBLANES, d))],
        scratch_shapes=[pltpu.VMEM((d, d), F32), pltpu.VMEM((d, pd), F32)],
        compiler_params=_params("arbitrary"),
    )(x2, p, target, g_pg, w_pg, b_pg, w_ple_t, g_ple, g_final)


def _ffn_bwd_a(dx2, fg, fu, w_down, comm=None):
    t, d = dx2.shape
    f = fg.shape[1]
    tm = min(TOKEN_TILE, t)

    def body(dx_ref, fg_ref, fu_ref, wd_ref, dfg_ref, dfu_ref, act_ref):
        dact = _dot_nt(dx_ref[...], wd_ref[...])
        fgv = fg_ref[...].astype(F32)
        fuv = fu_ref[...].astype(F32)
        sg = _sigmoid(fgv)
        silu = fgv * sg
        dfu_ref[...] = (dact * silu).astype(dfu_ref.dtype)
        dfg_ref[...] = (dact * fuv * (sg * (1.0 + fgv * (1.0 - sg)))).astype(dfg_ref.dtype)
        act_ref[...] = (silu * fuv).astype(act_ref.dtype)

    return _run(
        body, comm, name="ffn_bwd_a", grid=(t // tm,),
        out_shape=[jax.ShapeDtypeStruct((t, f), BF16)] * 3,
        in_specs=[_rows(tm, d), _rows(tm, f), _rows(tm, f), _whole((f, d))],
        out_specs=[_rows(tm, f)] * 3,
        semantics="parallel",
    )(dx2, fg, fu, w_down)


def _ffn_bwd_b(dfg, dfu, x1, dx2, g_ffn, w_gate_t, w_up_t, comm=None):
    t, d = x1.shape
    f = dfg.shape[1]
    tm = min(TOKEN_TILE, t)

    def body(dfg_ref, dfu_ref, x_ref, dx2_ref, g_ref, wg_ref, wu_ref, dx1_ref, h2_ref, vec_ref):
        _zero_on_first(vec_ref)
        dh2 = _dot(dfg_ref[...], wg_ref[...]) + _dot(dfu_ref[...], wu_ref[...])
        xhat, r = _rms_stats(x_ref[...])
        h2_ref[...] = (xhat * g_ref[...]).astype(h2_ref.dtype)
        dxn, dg = _rms_bwd(dh2, xhat, r, g_ref[...])
        dx1_ref[...] = dx2_ref[...] + dxn
        vec_ref[0:1, :] += _rowsum(dg)

    return _run(
        body, comm, name="ffn_bwd_b", grid=(t // tm,),
        out_shape=[jax.ShapeDtypeStruct((t, d), F32), jax.ShapeDtypeStruct((t, d), BF16),
                   jax.ShapeDtypeStruct((SUBLANES, d), F32)],
        in_specs=[_rows(tm, f), _rows(tm, f), _rows(tm, d), _rows(tm, d), _whole((1, d)), _whole((f, d)), _whole((f, d))],
        out_specs=[_rows(tm, d), _rows(tm, d), _acc((SUBLANES, d))],
        semantics="arbitrary",
    )(dfg, dfu, x1, dx2, g_ffn, w_gate_t, w_up_t)


def _matmul_tn(a, b, tn, name, dtype=F32):
    t, k = a.shape
    n = b.shape[1]

    def body(a_ref, b_ref, o_ref):
        o_ref[...] = _dot_tn(a_ref[...], b_ref[...]).astype(o_ref.dtype)

    return _run(
        body, None, name=name, grid=(n // tn,),
        out_shape=jax.ShapeDtypeStruct((k, n), dtype),
        in_specs=[_whole((t, k)), pl.BlockSpec((t, tn), lambda j: (0, j))],
        out_specs=pl.BlockSpec((k, tn), lambda j: (0, j)),
        semantics="parallel",
    )(a, b)


def _merge_bwd(dx1, merged, ma, mb, za, zb, y_a, h, w_o, w_a_out_t, w_b_out, comm=None):
    t, d = dx1.shape
    sa, lw = y_a.shape[1], h.shape[1]
    tm = min(TOKEN_TILE, t)

    def body(dx1_ref, mg_ref, ma_ref, mb_ref, za_ref, zb_ref, ya_ref, h_ref, wo_ref, wa_ref, wb_ref,
             dza_ref, dzb_ref, dya_ref, dyb_ref, dwo_out, dwa_out, dwb_out, dwo_ref, dwa_ref, dwb_ref):
        _zero_on_first(dwo_ref, dwa_ref, dwb_ref)
        dx1v = dx1_ref[...].astype(BF16)
        dmg = _dot_nt(dx1v, wo_ref[...])
        ga = _sigmoid(za_ref[...])
        gb = _sigmoid(zb_ref[...])
        dza_ref[...] = (dmg * ma_ref[...].astype(F32) * ga * (1.0 - ga)).astype(dza_ref.dtype)
        dzb_ref[...] = (dmg * mb_ref[...].astype(F32) * gb * (1.0 - gb)).astype(dzb_ref.dtype)
        dma = (dmg * ga).astype(BF16)
        dmb = (dmg * gb).astype(BF16)
        dya_ref[...] = _dot(dma, wa_ref[...])
        dyb_ref[...] = _dot_nt(dmb, wb_ref[...])
        dwo_ref[...] += _dot_tn(mg_ref[...], dx1v)
        dwa_ref[...] += _dot_tn(dma, ya_ref[...])
        dwb_ref[...] += _dot_tn(h_ref[...], dmb)
        _store_on_last([(dwo_ref, dwo_out), (dwa_ref, dwa_out), (dwb_ref, dwb_out)])

    return _run(
        body, comm, name="merge_bwd", grid=(t // tm,),
        out_shape=[jax.ShapeDtypeStruct((t, d), BF16), jax.ShapeDtypeStruct((t, d), BF16),
                   jax.ShapeDtypeStruct((t, sa), F32), jax.ShapeDtypeStruct((t, lw), F32),
                   jax.ShapeDtypeStruct((d, d), BF16), jax.ShapeDtypeStruct((d, sa), BF16),
                   jax.ShapeDtypeStruct((lw, d), BF16)],
        in_specs=[_rows(tm, d), _rows(tm, d), _rows(tm, d), _rows(tm, d), _rows(tm, d), _rows(tm, d),
                  _rows(tm, sa), _rows(tm, lw), _whole((d, d)), _whole((d, sa)), _whole((lw, d))],
        out_specs=[_rows(tm, d), _rows(tm, d), _rows(tm, sa), _rows(tm, lw), _acc((d, d)), _acc((d, sa)), _acc((lw, d))],
        scratch_shapes=[pltpu.VMEM((d, d), F32), pltpu.VMEM((d, sa), F32), pltpu.VMEM((lw, d), F32)],
        semantics="arbitrary",
    )(dx1, merged, ma, mb, za, zb, y_a, h, w_o, w_a_out_t, w_b_out)


def _fold_diag_blocks(dense, row_group, col_group):
    r, c = dense.shape
    rows = lax.broadcasted_iota(jnp.int32, (r, c), 0)
    cols = lax.broadcasted_iota(jnp.int32, (r, c), 1)
    kept = jnp.where(rows // row_group == cols // col_group, dense, 0.0)
    pick = (lax.broadcasted_iota(jnp.int32, (row_group, r), 0)
            == lax.broadcasted_iota(jnp.int32, (row_group, r), 1) % row_group).astype(F32)
    return jnp.dot(pick, kept, preferred_element_type=F32, precision=lax.Precision.HIGHEST)


def _lru_bwd(dh, xc, hprev, u, conv_w, wr_blk, b_r, wi_blk, b_i, lru_lambda, head_dim, comm=None):
    t, w = dh.shape
    tc = min(TIME_CHUNK, t)
    steps = t // tc
    halo = SUBLANES
    sub_per_chunk = tc // halo
    slabs = w // LRU_SLAB

    def body(dh_ref, xc_ref, hp_ref, u_ref, uh_ref, cw_ref, wr_ref, br_ref, wi_ref, bi_ref, lam_ref,
             du_ref, dwr_out, dwi_out, vec_ref, lam_s, a_s, dxc_s, uext_s, carry_s, dwr_ref, dwi_ref):
        chunk = steps - 1 - pl.program_id(0)

        @pl.when(pl.program_id(0) == 0)
        def _():
            carry_s[...] = jnp.zeros_like(carry_s)
            dxc_s[tc:tc + halo, :] = jnp.zeros((halo, w), F32)
            dwr_ref[...] = jnp.zeros_like(dwr_ref)
            dwi_ref[...] = jnp.zeros_like(dwi_ref)
            vec_ref[...] = jnp.zeros_like(vec_ref)

        xc = xc_ref[...]
        r, ig, sp, log_a = _lru_gates(xc, wr_ref, br_ref, wi_ref, bi_ref, lam_ref)
        a = jnp.exp(log_a)
        a_s[...] = a

        def step(i, q):
            at = pl.ds(tc - 1 - i, 1)
            lam_row = dh_ref[at, :] + q
            lam_s[at, :] = lam_row
            return a_s[at, :] * lam_row

        carry_s[0:1, :] = lax.fori_loop(0, tc, step, carry_s[0:1, :], unroll=8)
        lam = lam_s[...]
        mult = jnp.sqrt(-_expm1(2.0 * log_a))
        d_log_a = lam * hp_ref[...] * a - (lam * ig * xc) * (a * a) / mult
        d_ig = lam * mult * xc
        dpre_r = (d_log_a * (-LRU_C * sp)) * r * (1.0 - r)
        dpre_i = d_ig * ig * (1.0 - ig)
        dxc = lam * mult * ig + _slab_dot(dpre_r, wr_ref, transposed=True) + _slab_dot(dpre_i, wi_ref, transposed=True)
        xcb, drb, dib = xc.astype(BF16), dpre_r.astype(BF16), dpre_i.astype(BF16)
        for j in range(slabs):
            cols = slice(j * LRU_SLAB, (j + 1) * LRU_SLAB)
            dwr_ref[j] += _dot_tn(drb[:, cols], xcb[:, cols])
            dwi_ref[j] += _dot_tn(dib[:, cols], xcb[:, cols])
        vec_ref[0:1, :] += _rowsum(dxc)
        vec_ref[1:2, :] += _rowsum(dpre_r)
        vec_ref[2:3, :] += _rowsum(dpre_i)
        vec_ref[3:4, :] += _rowsum(d_log_a * (-LRU_C * r)) * (-_sigmoid(-lam_ref[...]))
        dxc_s[0:tc, :] = dxc
        du = cw_ref[CONV_WIDTH - 1:CONV_WIDTH, :] * dxc
        for k in range(CONV_WIDTH - 1):
            off = CONV_WIDTH - 1 - k
            du = du + cw_ref[k:k + 1, :] * dxc_s[off:off + tc, :]
        du_ref[...] = du.astype(du_ref.dtype)
        dxc_s[tc:tc + halo, :] = dxc_s[0:halo, :]
        uext_s[0:halo, :] = jnp.where(chunk > 0, uh_ref[...], 0.0)
        uext_s[halo:halo + tc, :] = u_ref[...]
        for k in range(CONV_WIDTH):
            off = halo - (CONV_WIDTH - 1) + k
            vec_ref[4 + k:5 + k, :] += _rowsum(dxc * uext_s[off:off + tc, :])

        @pl.when(pl.program_id(0) == steps - 1)
        def _():
            for j in range(slabs):
                cols = slice(j * LRU_SLAB, (j + 1) * LRU_SLAB)
                dwr_out[:, cols] = _fold_diag_blocks(dwr_ref[j], head_dim, head_dim).astype(dwr_out.dtype)
                dwi_out[:, cols] = _fold_diag_blocks(dwi_ref[j], head_dim, head_dim).astype(dwi_out.dtype)

    halo_spec = pl.BlockSpec((halo, w), lambda i: (jnp.maximum((steps - 1 - i) * sub_per_chunk - 1, 0), 0))
    return _run(
        body, comm, name="lru_bwd", grid=(steps,),
        out_shape=[jax.ShapeDtypeStruct((t, w), BF16), jax.ShapeDtypeStruct((head_dim, w), BF16),
                   jax.ShapeDtypeStruct((head_dim, w), BF16), jax.ShapeDtypeStruct((SUBLANES, w), F32)],
        in_specs=[_rows_rev(tc, w, steps)] * 4 + [halo_spec, _whole((CONV_WIDTH, w)), _whole(wr_blk.shape),
                                                  _whole((1, w)), _whole(wi_blk.shape), _whole((1, w)), _whole((1, w))],
        out_specs=[_rows_rev(tc, w, steps), _acc((head_dim, w)), _acc((head_dim, w)), _acc((SUBLANES, w))],
        scratch_shapes=[pltpu.VMEM((tc, w), F32), pltpu.VMEM((tc, w), F32), pltpu.VMEM((tc + halo, w), F32),
                        pltpu.VMEM((halo + tc, w), F32), pltpu.VMEM((SUBLANES, w), F32),
                        pltpu.VMEM((slabs, LRU_SLAB, LRU_SLAB), F32), pltpu.VMEM((slabs, LRU_SLAB, LRU_SLAB), F32)],
        semantics="arbitrary",
    )(dh, xc, hprev, u, u, conv_w, wr_blk, b_r, wi_blk, b_i, lru_lambda)


def _s5_bwd(dya, y, sr, si, u, w_glu, b_glu, cre_blk, cimn_blk, bbr_blk, bbi_blk, ar, ai, d_skip, comm=None):
    t, sa = dya.shape
    gn = sr.shape[1]
    ns, _, sw = bbr_blk.shape
    tc = min(TIME_CHUNK, t)
    steps = t // tc
    halo = SUBLANES

    def body(dya_ref, y_ref, sr_ref, si_ref, u_ref, wg_ref, bg_ref, cre_ref, cim_ref, bbr_ref, bbi_ref,
             ar_ref, ai_ref, d_ref, du_ref, lr_ref, li_ref, dy_ref, dwg_out, vsa_ref, vgn_ref, gr_s, gi_s, cr_s, ci_s,
             dwg_ref):
        @pl.when(pl.program_id(0) == 0)
        def _():
            cr_s[...] = jnp.zeros_like(cr_s)
            ci_s[...] = jnp.zeros_like(ci_s)
            gr_s[tc:tc + halo, :] = jnp.zeros((halo, gn), F32)
            gi_s[tc:tc + halo, :] = jnp.zeros((halo, gn), F32)
            dwg_ref[...] = jnp.zeros_like(dwg_ref)
            vsa_ref[...] = jnp.zeros_like(vsa_ref)
            vgn_ref[...] = jnp.zeros_like(vgn_ref)

        yv = y_ref[...]
        uv = u_ref[...]
        zz = _gelu(yv)
        sg = _sigmoid(_dot(zz, wg_ref[...]) + bg_ref[...])
        dyav = dya_ref[...]
        dq = dyav * zz * sg * (1.0 - sg)
        dzz = dyav * sg + _dot_nt(dq, wg_ref[...])
        dwg_ref[...] += _dot_tn(zz, dq)
        dy = dzz * _gelu_grad(yv)
        dyb = dy.astype(BF16)
        dy_ref[...] = dyb.astype(dy_ref.dtype)
        vsa_ref[0:1, :] += _rowsum(dq)
        vsa_ref[1:2, :] += _rowsum(dy * uv)
        for m in range(ns):
            dym = dyb[:, m * S5_SLAB:(m + 1) * S5_SLAB]
            gr_s[0:tc, m * sw:(m + 1) * sw] = _dot_nt(dym, cre_ref[m])
            gi_s[0:tc, m * sw:(m + 1) * sw] = _dot_nt(dym, cim_ref[m])
        a_r = ar_ref[...]
        a_i = ai_ref[...]

        def step(i, carry):
            l_r, l_i = carry
            at = pl.ds(tc - 1 - i, 1)
            n_r = gr_s[at, :] + a_r * l_r + a_i * l_i
            n_i = gi_s[at, :] + a_r * l_i - a_i * l_r
            gr_s[at, :] = n_r
            gi_s[at, :] = n_i
            return n_r, n_i

        l_r, l_i = lax.fori_loop(0, tc, step, (cr_s[0:1, :], ci_s[0:1, :]), unroll=8)
        cr_s[0:1, :] = l_r
        ci_s[0:1, :] = l_i
        nxt_r = gr_s[1:tc + 1, :]
        nxt_i = gi_s[1:tc + 1, :]
        srv = sr_ref[...].astype(F32)
        siv = si_ref[...].astype(F32)
        vgn_ref[0:1, :] += _rowsum(nxt_r * srv + nxt_i * siv)
        vgn_ref[1:2, :] += _rowsum(nxt_i * srv - nxt_r * siv)
        lam_r = gr_s[0:tc, :]
        lam_i = gi_s[0:tc, :]
        gr_s[tc:tc + halo, :] = gr_s[0:halo, :]
        gi_s[tc:tc + halo, :] = gi_s[0:halo, :]
        lrb = lam_r.astype(BF16)
        lib = lam_i.astype(BF16)
        lr_ref[...] = lrb.astype(lr_ref.dtype)
        li_ref[...] = lib.astype(li_ref.dtype)
        for m in range(ns):
            states, chans = slice(m * sw, (m + 1) * sw), slice(m * S5_SLAB, (m + 1) * S5_SLAB)
            du_ref[:, chans] = (_dot_nt(lrb[:, states], bbr_ref[m]) + _dot_nt(lib[:, states], bbi_ref[m])
                                + dy[:, chans] * d_ref[:, chans]).astype(du_ref.dtype)
        _store_on_last([(dwg_ref, dwg_out)])

    return _run(
        body, comm, name="s5_bwd", grid=(steps,),
        out_shape=[jax.ShapeDtypeStruct((t, sa), BF16), jax.ShapeDtypeStruct((t, gn), BF16),
                   jax.ShapeDtypeStruct((t, gn), BF16), jax.ShapeDtypeStruct((t, sa), BF16),
                   jax.ShapeDtypeStruct((sa, sa), BF16), jax.ShapeDtypeStruct((SUBLANES, sa), F32),
                   jax.ShapeDtypeStruct((SUBLANES, gn), F32)],
        in_specs=[_rows_rev(tc, sa, steps), _rows_rev(tc, sa, steps), _rows_rev(tc, gn, steps), _rows_rev(tc, gn, steps),
                  _rows_rev(tc, sa, steps), _whole((sa, sa)), _whole((1, sa)), _whole(cre_blk.shape),
                  _whole(cimn_blk.shape), _whole(bbr_blk.shape), _whole(bbi_blk.shape), _whole((1, gn)), _whole((1, gn)),
                  _whole((1, sa))],
        out_specs=[_rows_rev(tc, sa, steps), _rows_rev(tc, gn, steps), _rows_rev(tc, gn, steps), _rows_rev(tc, sa, steps),
                   _acc((sa, sa)), _acc((SUBLANES, sa)), _acc((SUBLANES, gn))],
        scratch_shapes=[pltpu.VMEM((tc + halo, gn), F32), pltpu.VMEM((tc + halo, gn), F32),
                        pltpu.VMEM((SUBLANES, gn), F32), pltpu.VMEM((SUBLANES, gn), F32), pltpu.VMEM((sa, sa), F32)],
        semantics="arbitrary",
    )(dya, y, sr, si, u, w_glu, b_glu, cre_blk, cimn_blk, bbr_blk, bbi_blk, ar, ai, d_skip)


def _inproj_bwd(dparts, x, dx1, g_mix, w_in_t, comm=None):
    t, d = x.shape
    n = w_in_t.shape[0]
    widths = [p.shape[1] for p in dparts]
    offs = [sum(widths[:i]) for i in range(len(widths) + 1)]
    tm = min(TOKEN_TILE, t)
    np_ = len(dparts)

    def body(*refs):
        dz_refs = refs[:np_]
        x_ref, dx1_ref, g_ref, w_ref, gx_ref, h_ref, vd_ref, vn_ref = refs[np_:]
        _zero_on_first(vd_ref, vn_ref)
        dh = jnp.zeros((tm, d), F32)
        for k, r in enumerate(dz_refs):
            lo, hi = offs[k], offs[k + 1]
            dzk = r[...]
            dh = dh + _dot(dzk, w_ref[lo:hi, :])
            vn_ref[0:1, lo:hi] += _rowsum(dzk.astype(F32))
        xhat, r0 = _rms_stats(x_ref[...])
        h_ref[...] = (xhat * g_ref[...]).astype(h_ref.dtype)
        dxn, dg = _rms_bwd(dh, xhat, r0, g_ref[...])
        gx_ref[...] = dx1_ref[...] + dxn
        vd_ref[0:1, :] += _rowsum(dg)

    return _run(
        body, comm, name="inproj_bwd", grid=(t // tm,),
        out_shape=[jax.ShapeDtypeStruct((t, d), F32), jax.ShapeDtypeStruct((t, d), BF16),
                   jax.ShapeDtypeStruct((SUBLANES, d), F32), jax.ShapeDtypeStruct((SUBLANES, n), F32)],
        in_specs=[_rows(tm, w) for w in widths] + [_rows(tm, d), _rows(tm, d), _whole((1, d)), _whole((n, d))],
        out_specs=[_rows(tm, d), _rows(tm, d), _acc((SUBLANES, d)), _acc((SUBLANES, n))],
        semantics="arbitrary",
    )(*dparts, x, dx1, g_mix, w_in_t)


def _dw_from_parts(dparts, h, tn, name, comm=None):
    t, d = h.shape
    widths = [p.shape[1] for p in dparts]
    offs = [sum(widths[:i]) for i in range(len(widths) + 1)]
    np_ = len(dparts)

    def body(*refs):
        h_ref, o_ref = refs[np_], refs[np_ + 1]
        hv = h_ref[...]
        for k, r in enumerate(refs[:np_]):
            o_ref[offs[k]:offs[k + 1], :] = _dot_tn(r[...], hv).astype(o_ref.dtype)

    return _run(
        body, comm, name=name, grid=(d // tn,),
        out_shape=jax.ShapeDtypeStruct((offs[-1], d), BF16),
        in_specs=[_whole(p.shape) for p in dparts] + [pl.BlockSpec((t, tn), lambda j: (0, j))],
        out_specs=pl.BlockSpec((offs[-1], tn), lambda j: (0, j)),
        semantics="parallel",
    )(*dparts, h)


def _prep(lr_row, li_row, ldt_row, lr_col, li_col, ldt_col, b_re, b_im, c_re, c_im, w_r, w_i, comm=None):
    gn, pch = b_re.shape
    sa, n = c_re.shape
    w, hd = w_r.shape
    ns, sw, lsl = sa // S5_SLAB, S5_SLAB * n // pch, w // LRU_SLAB

    def spread_cols(vals, row_group, col_group, width):
        r, k = vals.shape
        tile = (lax.broadcasted_iota(jnp.int32, (k, width), 0) == lax.broadcasted_iota(jnp.int32, (k, width), 1) % k)
        rows = lax.broadcasted_iota(jnp.int32, (r, width), 0) // row_group
        cols = lax.broadcasted_iota(jnp.int32, (r, width), 1) // col_group
        return jnp.where(rows == cols, _dot(vals, tile.astype(BF16)), 0.0)

    def spread_rows(vals, row_group, col_group, height):
        k, c = vals.shape
        tile = (lax.broadcasted_iota(jnp.int32, (height, k), 0) % k == lax.broadcasted_iota(jnp.int32, (height, k), 1))
        rows = lax.broadcasted_iota(jnp.int32, (height, c), 0) // row_group
        cols = lax.broadcasted_iota(jnp.int32, (height, c), 1) // col_group
        return jnp.where(rows == cols, _dot(tile.astype(BF16), vals), 0.0)

    def body(lrr, lir, ldr, lrc, lic, ldc, bre, bim, cre, cim, wr, wi,
             ar_o, ai_o, bbr_o, bbi_o, cre_o, cim_o, wr_o, wi_o):
        ar, ai, _, _ = _disc_scalars(lrr[...], lir[...], ldr[...])
        ar_o[...] = ar
        ai_o[...] = ai
        _, _, bbr, bbi = _disc_cols(lrc[...], lic[...], ldc[...], bre[...], bim[...])
        bbr_t, bbi_t = bbr.T, bbi.T
        for m in range(ns):
            bbr_o[m] = spread_rows(bbr_t[:, m * sw:(m + 1) * sw], pch, n, S5_SLAB).astype(bbr_o.dtype)
            bbi_o[m] = spread_rows(bbi_t[:, m * sw:(m + 1) * sw], pch, n, S5_SLAB).astype(bbi_o.dtype)
            rows = slice(m * S5_SLAB, (m + 1) * S5_SLAB)
            cre_o[m] = spread_rows(cre[rows, :].T, n, pch, sw).astype(cre_o.dtype)
            cim_o[m] = spread_rows(-cim[rows, :].T, n, pch, sw).astype(cim_o.dtype)
        for j in range(lsl):
            rows = slice(j * LRU_SLAB, (j + 1) * LRU_SLAB)
            wr_o[j] = spread_cols(wr[rows, :], hd, hd, LRU_SLAB).astype(wr_o.dtype)
            wi_o[j] = spread_cols(wi[rows, :], hd, hd, LRU_SLAB).astype(wi_o.dtype)

    args = (lr_row, li_row, ldt_row, lr_col, li_col, ldt_col, b_re, b_im, c_re, c_im, w_r, w_i)
    out_shape = [jax.ShapeDtypeStruct((1, gn), F32), jax.ShapeDtypeStruct((1, gn), F32),
                 jax.ShapeDtypeStruct((ns, S5_SLAB, sw), BF16), jax.ShapeDtypeStruct((ns, S5_SLAB, sw), BF16),
                 jax.ShapeDtypeStruct((ns, sw, S5_SLAB), BF16), jax.ShapeDtypeStruct((ns, sw, S5_SLAB), BF16),
                 jax.ShapeDtypeStruct((lsl, LRU_SLAB, LRU_SLAB), BF16),
                 jax.ShapeDtypeStruct((lsl, LRU_SLAB, LRU_SLAB), BF16)]
    return _run(
        body, comm, name="prep", grid=(1,), out_shape=out_shape, in_specs=[_whole(a.shape) for a in args],
        out_specs=[_acc(s.shape) for s in out_shape], semantics="arbitrary",
    )(*args)


def _s5_param_grads(lam_r, lam_i, sr, si, u, dy, pch, n, comm=None):
    t, gn = lam_r.shape
    sa = u.shape[1]
    sw = S5_SLAB * n // pch

    def body(lr_ref, li_ref, sr_ref, si_ref, u_ref, dy_ref, dbr_ref, dbi_ref, dcr_ref, dci_ref):
        uv = u_ref[...]
        dyv = dy_ref[...]
        dbr_ref[...] = _fold_diag_blocks(_dot_tn(uv, lr_ref[...]), pch, n).astype(dbr_ref.dtype)
        dbi_ref[...] = _fold_diag_blocks(_dot_tn(uv, li_ref[...]), pch, n).astype(dbi_ref.dtype)
        dcr_ref[...] = _fold_diag_blocks(_dot_tn(sr_ref[...], dyv), n, pch).astype(dcr_ref.dtype)
        dci_ref[...] = _fold_diag_blocks(_dot_tn(si_ref[...], dyv), n, pch).astype(dci_ref.dtype)

    states = pl.BlockSpec((t, sw), lambda m: (0, m))
    chans = pl.BlockSpec((t, S5_SLAB), lambda m: (0, m))
    return _run(
        body, comm, name="s5_param_grads", grid=(sa // S5_SLAB,),
        out_shape=[jax.ShapeDtypeStruct((pch, gn), BF16), jax.ShapeDtypeStruct((pch, gn), BF16),
                   jax.ShapeDtypeStruct((n, sa), BF16), jax.ShapeDtypeStruct((n, sa), BF16)],
        in_specs=[states, states, states, states, chans, chans],
        out_specs=[pl.BlockSpec((pch, sw), lambda m: (0, m)), pl.BlockSpec((pch, sw), lambda m: (0, m)),
                   pl.BlockSpec((n, S5_SLAB), lambda m: (0, m)), pl.BlockSpec((n, S5_SLAB), lambda m: (0, m))],
        semantics="parallel",
    )(lam_r, lam_i, sr, si, u, dy)


SMALL_PARTS = ["vec_tail", "vec_ffn", "vec_lru", "vec_mix", "vec_bin", "vec_sa", "vec_gn", "dw_r", "dw_i", "dbb_re",
               "dbb_im", "dc_re", "dc_imn", "loss"]


def _small_reduce(parts, shapes, lr_col, li_col, ldt_col, b_re, b_im, groups, after=()):
    gn, pch = b_re.shape
    n = gn // groups
    nparts = parts[SMALL_PARTS[0]].size // math.prod(shapes[SMALL_PARTS[0]])
    np_, nout, first_out = len(SMALL_PARTS), 24, len(SMALL_PARTS) + 5 + len(after)

    def body(*refs):
        ins = refs[:np_]
        lr, li, ldt, bre, bim = refs[np_:np_ + 5]
        outs = refs[first_out:first_out + nout]
        sums = dict(zip(SMALL_PARTS, refs[first_out + nout:]))

        @pl.when(pl.program_id(0) == 0)
        def _():
            for k, r in zip(SMALL_PARTS, ins):
                sums[k][...] = r[...].astype(F32)

        @pl.when(pl.program_id(0) > 0)
        def _():
            for k, r in zip(SMALL_PARTS, ins):
                sums[k][...] += r[...].astype(F32)

        @pl.when(pl.program_id(0) == nparts - 1)
        def _():
            finish({k: s[...] for k, s in sums.items()}, lr, li, ldt, bre, bim, *outs)

    def finish(tot, lr, li, ldt, bre, bim, o_loss, o_gmix, o_bin, o_bglu, o_s5d, o_convb, o_br, o_bi, o_lam, o_gffn,
               o_gpg, o_bpg, o_gple, o_gfin, o_wr, o_wi, o_cre, o_cim, o_bre, o_bim, o_lre, o_lim, o_ldt, o_convw):
        o_loss[...] = tot["loss"]
        o_convw[...] = tot["vec_lru"][SUBLANES - CONV_WIDTH:SUBLANES]
        o_bpg[...] = tot["vec_tail"][0:1]
        o_gpg[...] = tot["vec_tail"][1:2]
        o_gple[...] = tot["vec_tail"][2:3]
        o_gfin[...] = tot["vec_tail"][3:4]
        o_gffn[...] = tot["vec_ffn"][0:1]
        o_convb[...] = tot["vec_lru"][0:1]
        o_br[...] = tot["vec_lru"][1:2]
        o_bi[...] = tot["vec_lru"][2:3]
        o_lam[...] = tot["vec_lru"][3:4]
        o_gmix[...] = tot["vec_mix"][0:1]
        o_bin[...] = tot["vec_bin"][0:1]
        o_bglu[...] = tot["vec_sa"][0:1]
        o_s5d[...] = tot["vec_sa"][1:2]
        o_wr[...] = tot["dw_r"].T
        o_wi[...] = tot["dw_i"].T
        o_cre[...] = tot["dc_re"].T
        o_cim[...] = -tot["dc_imn"].T
        d_a = tot["vec_gn"].T
        _, chain = jax.vjp(_disc_cols, lr[...], li[...], ldt[...], bre[...], bim[...])
        d_lr, d_li, d_ldt, d_bre, d_bim = chain((d_a[:, 0:1], d_a[:, 1:2], tot["dbb_re"].T, tot["dbb_im"].T))
        o_lre[...] = d_lr
        o_lim[...] = d_li
        o_bre[...] = d_bre
        o_bim[...] = d_bim
        same = (lax.broadcasted_iota(jnp.int32, (groups, gn), 0)
                == lax.broadcasted_iota(jnp.int32, (groups, gn), 1) // n).astype(F32)
        o_ldt[...] = jnp.dot(same, d_ldt * jnp.ones((1, LANES), F32), preferred_element_type=F32,
                             precision=lax.Precision.HIGHEST)[:, 0:1]

    d = shapes["vec_mix"][1]
    nz = shapes["vec_bin"][1]
    sa = shapes["vec_sa"][1]
    w = shapes["vec_lru"][1]
    row = lambda c: jax.ShapeDtypeStruct((1, c), F32)
    out_shape = [jax.ShapeDtypeStruct(shapes["loss"], F32), row(d), row(nz), row(sa), row(sa), row(w), row(w), row(w),
                 row(w), row(d), row(d), row(d), row(d), row(d),
                 jax.ShapeDtypeStruct(shapes["dw_r"][::-1], F32), jax.ShapeDtypeStruct(shapes["dw_i"][::-1], F32),
                 jax.ShapeDtypeStruct(shapes["dc_re"][::-1], F32), jax.ShapeDtypeStruct(shapes["dc_imn"][::-1], F32),
                 jax.ShapeDtypeStruct((gn, pch), F32), jax.ShapeDtypeStruct((gn, pch), F32),
                 jax.ShapeDtypeStruct((gn, 1), F32), jax.ShapeDtypeStruct((gn, 1), F32),
                 jax.ShapeDtypeStruct((groups, 1), F32), jax.ShapeDtypeStruct((CONV_WIDTH, w), F32)]
    def part_spec(k):
        r, c = shapes[k]
        if parts[k].ndim == 3:
            return pl.BlockSpec((None, r, c), lambda i: (i, 0, 0))
        return pl.BlockSpec((r, c), lambda i: (i, 0))

    outs = _run(
        body, None, name="small_reduce", grid=(nparts,), out_shape=out_shape,
        in_specs=([part_spec(k) for k in SMALL_PARTS] + [_whole(a.shape) for a in (lr_col, li_col, ldt_col, b_re, b_im)]
                  + [_ANY] * len(after)),
        out_specs=[_acc(s.shape) for s in out_shape],
        scratch_shapes=[pltpu.VMEM(shapes[k], F32) for k in SMALL_PARTS],
        semantics="arbitrary",
    )(*[parts[k] for k in SMALL_PARTS], lr_col, li_col, ldt_col, b_re, b_im, *after)
    names = ["loss", "g_mix", "b_in", "b_glu", "s5_d", "conv_b", "b_r", "b_i", "lru_lambda", "g_ffn", "g_ple_gate",
             "b_ple_gate", "g_ple", "g_final", "w_r", "w_i", "s5_c_re", "s5_c_im", "s5_b_re", "s5_b_im", "lam_re",
             "lam_im", "log_dt", "conv_w"]
    return dict(zip(names, outs))


def _adamw_small(ws, gs, ms, vs):
    n = len(ws)

    def body(*refs):
        w_r, g_r, m_r, v_r = (refs[i * n:(i + 1) * n] for i in range(4))
        g_o, d_o, m_o, v_o = (refs[(4 + i) * n:(5 + i) * n] for i in range(4))
        for i in range(n):
            g = g_r[i][...]
            delta, m_new, v_new = _adamw_math(w_r[i][...], g, m_r[i][...], v_r[i][...])
            g_o[i][...] = g
            d_o[i][...] = delta
            m_o[i][...] = m_new
            v_o[i][...] = v_new

    shapes = [jax.ShapeDtypeStruct(a.shape, F32) for a in ws]
    outs = pl.pallas_call(body, name="adamw_small", out_shape=shapes * 4)(*ws, *gs, *ms, *vs)
    return outs[:n], outs[n:2 * n], outs[2 * n:3 * n], outs[3 * n:]


def _adamw_math(w, g, m, v):
    m_new = ADAM_B1 * m + (1.0 - ADAM_B1) * g
    v_new = ADAM_B2 * v + (1.0 - ADAM_B2) * (g * g)
    m_hat = m_new / (1.0 - ADAM_B1 ** ADAM_STEP)
    v_hat = v_new / (1.0 - ADAM_B2 ** ADAM_STEP)
    delta = -ADAM_LR * (m_hat / (jnp.sqrt(v_hat) + ADAM_EPS) + ADAM_WD * w)
    return delta, m_new, v_new


def _row_tile(rows):
    for cand in range(256, 0, -16):
        if rows % cand == 0:
            return cand
    return rows


def _adamw(parts, w, m, v, name, transposed=False, own=None, after=()):
    nw = len(w)
    rows, cols = w[0].shape
    npart = parts[0].shape[0]
    tr = _row_tile(rows)
    if transposed:
        parts_spec = pl.BlockSpec((npart, cols, tr), lambda i: (0, 0, i))
    else:
        parts_spec = pl.BlockSpec((npart, tr, cols), lambda i: (0, i, 0))
    per = 4 if own is None else 5
    first_out = per * nw + len(after)

    def body(*refs):
        mine = None if own is None else _chip(_mesh_position())
        for i in range(nw):
            group = refs[per * i:per * i + per]
            p_ref, (w_ref, m_ref, v_ref) = group[0], group[-3:]
            g_ref, d_ref, mo_ref, vo_ref = refs[first_out + 4 * i:first_out + 4 * i + 4]

            def part(k):
                a = p_ref[k].astype(F32)
                return a if own is None else jnp.where(mine == k, group[1][k].astype(F32), a)

            g = part(0)
            for k in range(1, npart):
                g = g + part(k)
            if transposed:
                g = g.T
            delta, m_new, v_new = _adamw_math(w_ref[...], g, m_ref[...], v_ref[...])
            g_ref[...] = g
            d_ref[...] = delta
            mo_ref[...] = m_new
            vo_ref[...] = v_new

    groups = zip(parts, w, m, v) if own is None else zip(parts, own, w, m, v)
    res = pl.pallas_call(
        body, name=name, grid=(rows // tr,),
        out_shape=[jax.ShapeDtypeStruct((rows, cols), F32)] * (4 * nw),
        in_specs=([parts_spec] * (per - 3) + [_rows(tr, cols)] * 3) * nw + [_ANY] * len(after),
        out_specs=[_rows(tr, cols)] * (4 * nw),
        compiler_params=_params("parallel"),
    )(*[a for group in groups for a in group], *after)
    return [res[4 * i:4 * i + 4] for i in range(nw)]


def _mesh_position():
    return lax.axis_index("x"), lax.axis_index("y"), lax.axis_index("c")


def _flip(pos, rel):
    x, y, c = pos
    return (1 - x if rel & 4 else x, 1 - y if rel & 2 else y, 1 - c if rel & 1 else c)


def _index(pos):
    return 4 * pos[0] + 2 * pos[1] + pos[2]


_ANY = pl.BlockSpec(memory_space=pl.ANY)
FLAT_ROWS = 32


def _dma_sems(n):
    return [pltpu.SemaphoreType.DMA((n, N_DEV - 1)), pltpu.SemaphoreType.DMA((n, N_DEV - 1)), pltpu.SemaphoreType.DMA((n,))]


def _block_of(ref, idx, rows, flat):
    if flat:
        return ref.at[pl.ds(pl.multiple_of(idx * rows, FLAT_ROWS), rows), :]
    return ref.at[idx]


class _Gather:
    chips = (4, 2, 6)
    rels = frozenset((1, 4, 2, 6))

    def __init__(self, shards):
        self.inputs = list(shards)
        self.flat = [s.shape[0] % FLAT_ROWS == 0 for s in shards]
        self.out_shape = [
            jax.ShapeDtypeStruct((N_DEV * s.shape[0], s.shape[1]) if f else (N_DEV,) + s.shape, s.dtype)
            for s, f in zip(shards, self.flat)]
        self.sems = _dma_sems(len(shards))

    def _copy(self, ins, outs, sems, i, k, block, to, own=False):
        dst = _block_of(outs[i], _index(block), self.inputs[i].shape[0], self.flat[i])
        return pltpu.make_async_remote_copy(
            src_ref=ins[i] if own else dst, dst_ref=dst, send_sem=sems[0].at[i, k], recv_sem=sems[1].at[i, k],
            device_id=to, device_id_type=MESH)

    def _local(self, ins, outs, sems, i, me):
        dst = _block_of(outs[i], _index(me), self.inputs[i].shape[0], self.flat[i])
        return pltpu.make_async_copy(ins[i], dst, sems[2].at[i])

    def _first(self, ins, outs, sems, i, me):
        cps = [self._copy(ins, outs, sems, i, 0, me, _flip(me, 1), own=True)]
        cps += [self._copy(ins, outs, sems, i, 1 + j, me, _flip(me, rel), own=True) for j, rel in enumerate(self.chips)]
        return cps

    def _passed(self, ins, outs, sems, i, j, me):
        return self._copy(ins, outs, sems, i, 4 + j, _flip(me, self.chips[j]), _flip(me, 1))

    def before(self, ins, outs, sems):
        n = len(self.inputs)
        me = _mesh_position()

        @pl.when(pl.program_id(0) == 0)
        def _():
            for i in range(n):
                self._local(ins, outs, sems, i, me).start()
                for cp in self._first(ins, outs, sems, i, me):
                    cp.start()

        @pl.when(pl.program_id(0) == pl.num_programs(0) - 1)
        def _():
            for j, rel in enumerate(self.chips):
                for i in range(n):
                    self._copy(ins, outs, sems, i, 1 + j, _flip(me, rel), me).wait_recv()
                    self._passed(ins, outs, sems, i, j, me).start()

    def after(self, ins, outs, sems):
        n = len(self.inputs)
        me = _mesh_position()
        sibling = _flip(me, 1)

        @pl.when(pl.program_id(0) == pl.num_programs(0) - 1)
        def _():
            for i in range(n):
                self._copy(ins, outs, sems, i, 0, sibling, me).wait_recv()
                for j, rel in enumerate(self.chips):
                    self._copy(ins, outs, sems, i, 4 + j, _flip(sibling, rel), me).wait_recv()
            for i in range(n):
                for cp in self._first(ins, outs, sems, i, me):
                    cp.wait_send()
                for j in range(len(self.chips)):
                    self._passed(ins, outs, sems, i, j, me).wait_send()
                self._local(ins, outs, sems, i, me).wait()


N_CHIPS = N_DEV // 2


def _chip(pos):
    return 2 * pos[0] + pos[1]


class _PairSwap:
    rels = frozenset((1,))

    def __init__(self, arrays):
        self.inputs = list(arrays)
        self.rows = [a.shape[0] // N_DEV for a in arrays]
        for r in self.rows:
            assert r % FLAT_ROWS == 0, r
        self.out_shape = [jax.ShapeDtypeStruct((N_CHIPS, r, a.shape[1]), a.dtype) for a, r in zip(arrays, self.rows)]
        n = len(arrays)
        self.sems = [pltpu.SemaphoreType.DMA((n, N_CHIPS)), pltpu.SemaphoreType.DMA((n, N_CHIPS))]

    def _copy(self, ins, outs, sems, i, j, me):
        sibling = _flip(me, 1)
        return pltpu.make_async_remote_copy(
            src_ref=_block_of(ins[i], 2 * j + sibling[2], self.rows[i], True), dst_ref=outs[i].at[j],
            send_sem=sems[0].at[i, j], recv_sem=sems[1].at[i, j], device_id=sibling, device_id_type=MESH)

    def before(self, ins, outs, sems):
        me = _mesh_position()

        @pl.when(pl.program_id(0) == 0)
        def _():
            for i in range(len(self.inputs)):
                for j in range(N_CHIPS):
                    self._copy(ins, outs, sems, i, j, me).start()

    def after(self, ins, outs, sems):
        me = _mesh_position()

        @pl.when(pl.program_id(0) == pl.num_programs(0) - 1)
        def _():
            for i in range(len(self.inputs)):
                for j in range(N_CHIPS):
                    self._copy(ins, outs, sems, i, j, me).wait()


class _ChipExchange:
    chips = (4, 2, 6)
    rels = frozenset(chips)

    def __init__(self, arrays):
        self.inputs = list(arrays)
        self.out_shape = [jax.ShapeDtypeStruct(a.shape, a.dtype) for a in arrays]
        n = len(arrays)
        self.sems = [pltpu.SemaphoreType.DMA((n, 3)), pltpu.SemaphoreType.DMA((n, 3)), pltpu.SemaphoreType.DMA((n,))]

    def _send(self, ins, outs, sems, i, k, me):
        peer = _flip(me, self.chips[k])
        return pltpu.make_async_remote_copy(
            src_ref=ins[i].at[_chip(peer)], dst_ref=outs[i].at[_chip(me)], send_sem=sems[0].at[i, k],
            recv_sem=sems[1].at[i, k], device_id=peer, device_id_type=MESH)

    def _arrival(self, ins, outs, sems, i, k, me):
        peer = _flip(me, self.chips[k])
        return pltpu.make_async_remote_copy(
            src_ref=ins[i].at[_chip(me)], dst_ref=outs[i].at[_chip(peer)], send_sem=sems[0].at[i, k],
            recv_sem=sems[1].at[i, k], device_id=peer, device_id_type=MESH)

    def _local(self, ins, outs, sems, i, me):
        return pltpu.make_async_copy(ins[i].at[_chip(me)], outs[i].at[_chip(me)], sems[2].at[i])

    def before(self, ins, outs, sems):
        me = _mesh_position()

        @pl.when(pl.program_id(0) == 0)
        def _():
            for i in range(len(self.inputs)):
                self._local(ins, outs, sems, i, me).start()
                for k in range(len(self.chips)):
                    self._send(ins, outs, sems, i, k, me).start()

    def after(self, ins, outs, sems):
        me = _mesh_position()

        @pl.when(pl.program_id(0) == pl.num_programs(0) - 1)
        def _():
            for i in range(len(self.inputs)):
                for k in range(len(self.chips)):
                    self._arrival(ins, outs, sems, i, k, me).wait_recv()
            for i in range(len(self.inputs)):
                for k in range(len(self.chips)):
                    self._send(ins, outs, sems, i, k, me).wait_send()
                self._local(ins, outs, sems, i, me).wait()


_HBM = pl.BlockSpec(memory_space=pltpu.HBM)
_SEM = pl.BlockSpec(memory_space=pltpu.SEMAPHORE)
_DATAFLOW = pltpu.SideEffectType.DATAFLOW_SIDE_EFFECTING
_CHIP_RELS = (4, 2, 6)
_SPLIT_COLLECTIVE_ID = 3


def _chips_copy(src_ref, dst_ref, send_sems, recv_sems, k, me):
    peer = _flip(me, _CHIP_RELS[k])
    return pltpu.make_async_remote_copy(
        src_ref=src_ref.at[_chip(peer)], dst_ref=dst_ref.at[_chip(me)], send_sem=send_sems.at[k],
        recv_sem=recv_sems.at[k], device_id=peer, device_id_type=MESH)


def _chips_start(pairs):
    def body(p_ref, land_ref, send_sems, recv_sems, p_thru, land_thru, token):
        me = _mesh_position()
        sem = pltpu.get_barrier_semaphore()
        for rel in _CHIP_RELS:
            pl.semaphore_signal(sem, inc=1, device_id=_flip(me, rel), device_id_type=MESH)
        pl.semaphore_wait(sem, len(_CHIP_RELS))
        for k in range(len(_CHIP_RELS)):
            _chips_copy(p_ref, land_ref, send_sems, recv_sems, k, me).start()
        token[...] = jnp.zeros_like(token)

    n = len(_CHIP_RELS)
    return pl.pallas_call(
        body, name="chips_start",
        out_shape=(pltpu.SemaphoreType.DMA((n,)), pltpu.SemaphoreType.DMA((n,)), pltpu.HBM(pairs.shape, pairs.dtype),
                   pltpu.HBM(pairs.shape, pairs.dtype), jax.ShapeDtypeStruct((SUBLANES, LANES), F32)),
        in_specs=(_HBM, _HBM), out_specs=(_SEM, _SEM, _HBM, _HBM, pl.BlockSpec(memory_space=pltpu.VMEM)),
        input_output_aliases={0: 2, 1: 3},
        compiler_params=pltpu.CompilerParams(has_side_effects=_DATAFLOW, collective_id=_SPLIT_COLLECTIVE_ID),
    )(pltpu.with_memory_space_constraint(pairs, pltpu.HBM),
      pltpu.with_memory_space_constraint(lax.empty(pairs.shape, pairs.dtype), pltpu.HBM))


def _chips_wait(send_sems, recv_sems, p_thru, land_thru, after):
    def body(p_ref, land_ref, send_sems, recv_sems, *rest):
        me = _mesh_position()
        for k in range(len(_CHIP_RELS)):
            copy = _chips_copy(p_ref, land_ref, send_sems, recv_sems, k, me)
            copy.wait_send()
            copy.wait_recv()

    return pl.pallas_call(
        body, name="chips_wait",
        out_shape=(pltpu.HBM(p_thru.shape, p_thru.dtype), pltpu.HBM(p_thru.shape, p_thru.dtype)),
        in_specs=(_HBM, _HBM, _SEM, _SEM) + (_ANY,) * len(after), out_specs=(_HBM, _HBM),
        input_output_aliases={0: 0, 1: 1}, compiler_params=pltpu.CompilerParams(has_side_effects=_DATAFLOW),
    )(p_thru, land_thru, send_sems, recv_sems, *after)


def _pair_add(grads, halves, name):
    n = len(grads)

    def body(*refs):
        g_refs, h_refs, o_refs = refs[:n], refs[n:2 * n], refs[2 * n:]
        c = lax.axis_index("c")
        for i in range(n):
            r = h_refs[i].shape[1]
            for j in range(N_CHIPS):
                own = g_refs[i][pl.ds(pl.multiple_of((2 * j + c) * r, FLAT_ROWS), r), :]
                o_refs[i][j] = (own.astype(F32) + h_refs[i][j].astype(F32)).astype(o_refs[i].dtype)

    return pl.pallas_call(
        body, name=name, out_shape=[jax.ShapeDtypeStruct(h.shape, h.dtype) for h in halves],
        compiler_params=pltpu.CompilerParams(vmem_limit_bytes=VMEM_LIMIT),
    )(*grads, *halves)


class _Both:
    def __init__(self, first, second):
        self.jobs = (first, second)
        self.rels = first.rels | second.rels
        self.inputs = first.inputs + second.inputs
        self.out_shape = first.out_shape + second.out_shape
        self.sems = first.sems + second.sems

    def _each(self, ins, outs, sems):
        a = self.jobs[0]
        i, o, s = len(a.inputs), len(a.out_shape), len(a.sems)
        return ((a, ins[:i], outs[:o], sems[:s]), (self.jobs[1], ins[i:], outs[o:], sems[s:]))

    def before(self, ins, outs, sems):
        for job, i, o, s in self._each(ins, outs, sems):
            job.before(i, o, s)

    def after(self, ins, outs, sems):
        for job, i, o, s in self._each(ins, outs, sems):
            job.after(i, o, s)


_COLLECTIVE_IDS = {(1,): 0, (2, 4, 6): 1, (1, 2, 4, 6): 2}


def _entry_barrier(rels):
    @pl.when(pl.program_id(0) == 0)
    def _():
        me = _mesh_position()
        sem = pltpu.get_barrier_semaphore()
        for rel in rels:
            pl.semaphore_signal(sem, inc=1, device_id=_flip(me, rel), device_id_type=MESH)
        pl.semaphore_wait(sem, len(rels))


def _run(body, comm, *, semantics, out_shape, in_specs, out_specs, scratch_shapes=(), **kw):
    if comm is None:
        return pl.pallas_call(body, out_shape=out_shape, in_specs=in_specs, out_specs=out_specs,
                              scratch_shapes=list(scratch_shapes), compiler_params=_params(semantics), **kw)
    single = not isinstance(out_shape, (list, tuple))
    outs = [out_shape] if single else list(out_shape)
    ospecs = [out_specs] if single else list(out_specs)
    counts = [len(in_specs), len(comm.inputs), len(outs), len(comm.out_shape), len(scratch_shapes), len(comm.sems)]
    rels = tuple(sorted(comm.rels))

    def carrying(*refs):
        groups, pos = [], 0
        for c in counts:
            groups.append(refs[pos:pos + c])
            pos += c
        main_in, comm_in, main_out, comm_out, main_scratch, comm_sems = groups
        _entry_barrier(rels)
        comm.before(comm_in, comm_out, comm_sems)
        body(*main_in, *main_out, *main_scratch)
        comm.after(comm_in, comm_out, comm_sems)

    call = pl.pallas_call(
        carrying, out_shape=outs + list(comm.out_shape), in_specs=list(in_specs) + [_ANY] * len(comm.inputs),
        out_specs=ospecs + [_ANY] * len(comm.out_shape), scratch_shapes=list(scratch_shapes) + list(comm.sems),
        compiler_params=pltpu.CompilerParams(dimension_semantics=("arbitrary",), vmem_limit_bytes=VMEM_LIMIT,
                                             collective_id=_COLLECTIVE_IDS[rels]), **kw)

    def apply(*args):
        res = call(*args, *comm.inputs)
        main = res[:len(outs)]
        return (main[0] if single else list(main)), list(res[len(outs):])

    return apply


def _alone(comm, name):
    return _run(lambda: None, comm, semantics="arbitrary", name=name, grid=(1,), out_shape=[], in_specs=[], out_specs=[])()[1]


SHARDED = {"w_in": 1, "w_glu": 0, "conv_w": 1, "w_a_out": 1, "w_b_out": 0, "w_o": 0, "w_ffn_gate": 1, "w_ffn_up": 1,
           "w_ffn_down": 0, "w_ple_gate": 0, "w_ple": 1}
TRANSPOSED = ("w_in", "w_a_out", "w_ffn_gate", "w_ffn_up", "w_ple")
LONG_AXIS_MINOR = ("w_in", "w_ffn_gate", "w_ffn_up")
NARROW_LAST = ("s5_b_re", "s5_b_im", "s5_d")
ADAMW_GROUPS = (("w_in",), ("w_glu",), ("conv_w",), ("w_a_out",), ("w_b_out", "w_o", "w_ple_gate"),
                ("w_ffn_gate", "w_ffn_up"), ("w_ffn_down",), ("w_ple",))
SMALL = ["g_mix", "b_in", "lam_re", "lam_im", "log_dt", "s5_b_re", "s5_b_im", "s5_c_re", "s5_c_im", "s5_d", "b_glu",
         "conv_b", "w_r", "b_r", "w_i", "b_i", "lru_lambda", "g_ffn", "g_ple_gate", "b_ple_gate", "g_ple", "g_final"]
WEIGHTS = ["g_mix", "w_in", "b_in", "lam_re", "lam_im", "log_dt", "s5_b_re", "s5_b_im", "s5_c_re", "s5_c_im", "s5_d",
           "w_glu", "b_glu", "conv_w", "conv_b", "w_r", "b_r", "w_i", "b_i", "lru_lambda", "w_a_out", "w_b_out", "w_o",
           "g_ffn", "w_ffn_gate", "w_ffn_up", "w_ffn_down", "g_ple_gate", "w_ple_gate", "b_ple_gate", "w_ple", "g_ple",
           "g_final"]


def _join_columns(gathered):
    nb, r, c = gathered.shape
    return jnp.transpose(gathered, (1, 0, 2)).reshape(r, nb * c)


def _disc_scalars(lr, li, ldt):
    dt = jnp.exp(ldt)
    mag = jnp.exp(lr * dt)
    ar = mag * jnp.cos(li * dt)
    ai = mag * jnp.sin(li * dt)
    den = lr * lr + li * li
    nr = ar - 1.0
    fr = (nr * lr + ai * li) / den
    fi = (ai * lr - nr * li) / den
    return ar, ai, fr, fi


def _disc_cols(lr, li, ldt, b_re, b_im):
    ar, ai, fr, fi = _disc_scalars(lr, li, ldt)
    return ar, ai, fr * b_re - fi * b_im, fr * b_im + fi * b_re


def _local_step(x, p, target, src, small, disc, distributed=True):
    full = {} if distributed else dict(src)
    gw, halves, pairs, got = {}, {}, {}, {}

    def gather(keys):
        return (_Gather([src[k] for k in keys]), keys, full) if distributed else None

    def swap(keys):
        return (_PairSwap([gw[k] for k in keys]), keys, halves) if distributed else None

    def chips(keys):
        return (_ChipExchange([pairs[k] for k in keys]), keys, got) if distributed else None

    def add_pairs(keys):
        if distributed:
            pairs.update(zip(keys, _pair_add([gw[k] for k in keys], [halves[k] for k in keys], "pair_add_" + keys[0])))

    def carry(fn, *args, jobs=()):
        jobs = [j for j in jobs if j is not None]
        if not jobs:
            return fn(*args)
        comm = jobs[0][0]
        for j in jobs[1:]:
            comm = _Both(comm, j[0])
        res, extra = fn(*args, comm=comm)
        for job, keys, sink in jobs:
            sink.update(zip(keys, extra[:len(job.out_shape)]))
            extra = extra[len(job.out_shape):]
        return res

    d = x.shape[1]
    g, n, pch = small["s5_b_re"].shape
    sa, lw = g * pch, small["lru_lambda"].shape[-1]
    widths = [sa, lw, d, d]
    row = lambda v: v.reshape(1, -1)

    hd = small["w_r"].shape[-1]
    ar_row, ai_row, bbr_blk, bbi_blk, cre_blk, cimn_blk, wr_blk, wi_blk = carry(
        _prep, *disc["rows"], *disc["cols"], disc["b_re"], disc["b_im"], small["s5_c_re"].reshape(sa, n),
        small["s5_c_im"].reshape(sa, n), small["w_r"].reshape(lw, hd), small["w_i"].reshape(lw, hd),
        jobs=[gather(["w_in"])])
    d_row = row(small["s5_d"])
    u_a, u_b, za, zb = carry(_inproj_fwd, x, row(small["g_mix"]), full["w_in"], row(small["b_in"]), widths,
                             jobs=[gather(["w_glu", "conv_w", "w_a_out", "w_b_out"])])
    conv_w = _join_columns(full["conv_w"]) if distributed else full["conv_w"]
    branches = [_s5_fwd(u_a, bbr_blk, bbi_blk, ar_row, ai_row, cre_blk, cimn_blk, d_row, full["w_glu"],
                        row(small["b_glu"])),
                _lru_fwd(u_b, conv_w, row(small["conv_b"]), wr_blk, row(small["b_r"]), wi_blk, row(small["b_i"]),
                         row(small["lru_lambda"]))]
    (sr, si, y, y_a), (xc, h, hprev) = carry(_stages, branches, "branches_fwd",
                                             jobs=[gather(["w_ffn_gate", "w_ffn_up", "w_o"])])
    x1, merged, ma, mb, fg, fu = carry(
        _merge_ffn_up_fwd, y_a, h, za, zb, x, full["w_a_out"], full["w_b_out"], full["w_o"], row(small["g_ffn"]),
        full["w_ffn_gate"], full["w_ffn_up"], jobs=[gather(["w_ffn_down"])])
    x2 = carry(_ffn_down_fwd, fg, fu, x1, full["w_ffn_down"], jobs=[gather(["w_ple_gate", "w_ple"])])

    dx2, loss_blk, gw["w_ple_gate"], gw["w_ple"], vec_tail = _tail_fwd_bwd(
        x2, p, target, row(small["g_ple_gate"]), full["w_ple_gate"], row(small["b_ple_gate"]), full["w_ple"],
        row(small["g_ple"]), row(small["g_final"]))
    dfg, dfu, act = carry(_ffn_bwd_a, dx2, fg, fu, full["w_ffn_down"], jobs=[swap(["w_ple_gate", "w_ple"])])
    tn_d = min(d, 512)
    gw["w_ffn_down"] = _matmul_tn(act, dx2, tn_d, "dw_ffn_down", BF16)
    add_pairs(["w_ple_gate", "w_ple"])
    dx1, h2, vec_ffn = carry(_ffn_bwd_b, dfg, dfu, x1, dx2, row(small["g_ffn"]), full["w_ffn_gate"], full["w_ffn_up"],
                             jobs=[chips(["w_ple_gate", "w_ple"]), swap(["w_ffn_down"])])
    gw["w_ffn_gate"] = _matmul_tn(dfg, h2, tn_d, "dw_ffn_gate", BF16)
    gw["w_ffn_up"] = _matmul_tn(dfu, h2, tn_d, "dw_ffn_up", BF16)
    add_pairs(["w_ffn_down"])
    dza, dzb, dya, dyb, gw["w_o"], gw["w_a_out"], gw["w_b_out"] = carry(
        _merge_bwd, dx1, merged, ma, mb, za, zb, y_a, h, full["w_o"], full["w_a_out"], full["w_b_out"],
        jobs=[chips(["w_ffn_down"]), swap(["w_ffn_gate", "w_ffn_up"])])
    add_pairs(["w_ffn_gate", "w_ffn_up"])
    du_b, dw_r, dw_i, vec_lru = carry(
        _lru_bwd, dyb, xc, hprev, u_b, conv_w, wr_blk, row(small["b_r"]), wi_blk, row(small["b_i"]),
        row(small["lru_lambda"]), hd, jobs=[chips(["w_ffn_gate"]), swap(["w_o", "w_a_out", "w_b_out"])])
    add_pairs(["w_o", "w_a_out", "w_b_out"])
    du_a, lam_r, lam_i, dy16, gw["w_glu"], vec_sa, vec_gn = carry(
        _s5_bwd, dya, y, sr, si, u_a, full["w_glu"], row(small["b_glu"]), cre_blk, cimn_blk, bbr_blk, bbi_blk, ar_row,
        ai_row, d_row, jobs=[chips(["w_ffn_up"]), chips(["w_o", "w_a_out", "w_b_out"])])
    smalls = {"vec_tail": vec_tail, "vec_ffn": vec_ffn, "vec_lru": vec_lru, "vec_sa": vec_sa, "vec_gn": vec_gn,
              "dw_r": dw_r, "dw_i": dw_i, "loss": loss_blk}
    everyones = {}

    def gather_smalls(keys):
        return (_Gather([smalls[k] for k in keys]), keys, everyones) if distributed else None

    smalls["dbb_re"], smalls["dbb_im"], smalls["dc_re"], smalls["dc_imn"] = carry(
        _s5_param_grads, lam_r, lam_i, sr, si, u_a, dy16, pch, n, jobs=[swap(["w_glu"]), gather_smalls(list(smalls))])
    add_pairs(["w_glu"])
    dz = [du_a, du_b, dza, dzb]
    grad_x, h0, smalls["vec_mix"], smalls["vec_bin"] = _inproj_bwd(
        dz, x, dx1, row(small["g_mix"]), full["w_in"])
    shapes = {k: a.shape for k, a in smalls.items()}
    gw["w_in"] = carry(_dw_from_parts, dz, h0, tn_d, "dw_in",
                       jobs=[chips(["w_glu"]),
                             gather_smalls(["dbb_re", "dbb_im", "dc_re", "dc_imn", "vec_mix", "vec_bin"])])
    smalls.update(everyones)
    if distributed:
        got["w_in"] = gw["w_in"]
        gw = got
    return grad_x, gw, smalls, shapes


def _disc_inputs(small):
    g, n, pch = small["s5_b_re"].shape
    srcs = (small["lam_re"], small["lam_im"], jnp.repeat(small["log_dt"], n))
    return {"rows": [a.reshape(1, g * n) for a in srcs], "cols": [a.reshape(g * n, 1) for a in srcs],
            "b_re": small["s5_b_re"].reshape(g * n, pch), "b_im": small["s5_b_im"].reshape(g * n, pch)}


def kernel(x, p, g_mix, w_in, b_in, lam_re, lam_im, log_dt, s5_b_re, s5_b_im, s5_c_re, s5_c_im, s5_d, w_glu, b_glu, conv_w, conv_b, w_r, b_r, w_i, b_i, lru_lambda, w_a_out, w_b_out, w_o, g_ffn, w_ffn_gate, w_ffn_up, w_ffn_down, g_ple_gate, w_ple_gate, b_ple_gate, w_ple, g_ple, g_final, loss_target, m_g_mix, m_w_in, m_b_in, m_lam_re, m_lam_im, m_log_dt, m_s5_b_re, m_s5_b_im, m_s5_c_re, m_s5_c_im, m_s5_d, m_w_glu, m_b_glu, m_conv_w, m_conv_b, m_w_r, m_b_r, m_w_i, m_b_i, m_lru_lambda, m_w_a_out, m_w_b_out, m_w_o, m_g_ffn, m_w_ffn_gate, m_w_ffn_up, m_w_ffn_down, m_g_ple_gate, m_w_ple_gate, m_b_ple_gate, m_w_ple, m_g_ple, m_g_final, v_g_mix, v_w_in, v_b_in, v_lam_re, v_lam_im, v_log_dt, v_s5_b_re, v_s5_b_im, v_s5_c_re, v_s5_c_im, v_s5_d, v_w_glu, v_b_glu, v_conv_w, v_conv_b, v_w_r, v_b_r, v_w_i, v_b_i, v_lru_lambda, v_w_a_out, v_w_b_out, v_w_o, v_g_ffn, v_w_ffn_gate, v_w_ffn_up, v_w_ffn_down, v_g_ple_gate, v_w_ple_gate, v_b_ple_gate, v_w_ple, v_g_ple, v_g_final):
    given = dict(locals())
    wts = {k: given[k] for k in WEIGHTS}
    moms = {k: given["m_" + k] for k in WEIGHTS}
    vels = {k: given["v_" + k] for k in WEIGHTS}

    def drop_depth(k, a):
        return a if k == "g_final" else a[0]

    small = {k: drop_depth(k, wts[k]) for k in SMALL}
    shard = {k: wts[k][0] for k in SHARDED}
    names = list(SHARDED)

    def wire(k):
        if k == "conv_w":
            return shard[k]
        return (shard[k].T if k in TRANSPOSED else shard[k]).astype(BF16)

    disc = _disc_inputs(small)
    grad_x, parts, smalls, shapes = _local_step(x[0], p[0, 0], loss_target[0], {k: wire(k) for k in names}, small, disc)

    (pair_in,) = _pair_add([parts["w_in"]], _alone(_PairSwap([parts["w_in"]]), "swap_last"), "pair_add_w_in")
    send_sems, recv_sems, pair_in, landing, token = _chips_start(pair_in)
    ordered = (token,)

    g_small = _small_reduce(smalls, shapes, *disc["cols"], disc["b_re"], disc["b_im"], small["s5_b_re"].shape[0],
                            after=ordered)
    loss = g_small.pop("loss")[0, 0]
    cols = shard["conv_w"].shape[1]
    mine = _index((lax.axis_index("x"), lax.axis_index("y"), lax.axis_index("c")))
    parts["conv_w"] = lax.dynamic_slice_in_dim(g_small.pop("conv_w"), mine * cols, cols, axis=1)[None]

    def view(k, a):
        a = a.reshape((1, -1) if k == "g_final" else wts[k].shape)
        return jnp.swapaxes(a, -1, -2) if k in NARROW_LAST else a

    def unview(k, a):
        return (jnp.swapaxes(a, -1, -2) if k in NARROW_LAST else a).reshape(wts[k].shape)

    slots = _adamw_small([view(k, wts[k]) for k in SMALL], [view(k, g_small[k]) for k in SMALL],
                         [view(k, moms[k]) for k in SMALL], [view(k, vels[k]) for k in SMALL])
    small_out = [dict(zip(SMALL, [unview(k, a) for k, a in zip(SMALL, slot)])) for slot in slots]

    big_out = {}
    between = [slots[0][0]]
    for group in sorted(ADAMW_GROUPS, key=lambda g: g[0] == "w_in"):
        flip = group[0] in LONG_AXIS_MINOR
        look = (lambda a: a.T) if flip else (lambda a: a)
        last = group[0] == "w_in"
        if last:
            pair_in, parts["w_in"] = _chips_wait(send_sems, recv_sems, pair_in, landing, between)
        res = _adamw([parts[k] for k in group], [look(shard[k]) for k in group], [look(moms[k][0]) for k in group],
                     [look(vels[k][0]) for k in group], "adamw_" + group[0],
                     transposed=group[0] in TRANSPOSED and not flip, own=[pair_in] if last else None,
                     after=() if last else ordered)
        between.append(res[0][0])
        for k, outs4 in zip(group, res):
            big_out[k] = [look(a) for a in outs4]

    outs = [loss, grad_x[None]]
    for slot in range(4):
        for k in WEIGHTS:
            if k in SHARDED:
                outs.append(big_out[k][slot][None])
            else:
                outs.append(small_out[slot][k])
    return tuple(outs)
```

```python
import math

import jax
import jax.numpy as jnp
from jax import lax
from jax.experimental import pallas as pl
from jax.experimental.pallas import tpu as pltpu

F32 = jnp.float32
BF16 = jnp.bfloat16

EPS = 1e-6
LRU_C = 8.0
CONV_WIDTH = 4
ADAM_LR = 0.001
ADAM_B1 = 0.9
ADAM_B2 = 0.999
ADAM_EPS = 1e-08
ADAM_WD = 0.01
ADAM_STEP = 10

N_DEV = 8
MESH = pl.DeviceIdType.MESH
SUBLANES = 8
LANES = 128
VMEM_LIMIT = 56 * 1024 * 1024
TOKEN_TILE = 256
TIME_CHUNK = 256
S5_SLAB = 128
LRU_SLAB = 256


def _dot(a, b):
    return jnp.dot(a.astype(BF16), b.astype(BF16), preferred_element_type=F32)


def _dot_nt(a, b):
    return lax.dot_general(a.astype(BF16), b.astype(BF16), (((1,), (1,)), ((), ())), preferred_element_type=F32)


def _dot_tn(a, b):
    return lax.dot_general(a.astype(BF16), b.astype(BF16), (((0,), (0,)), ((), ())), preferred_element_type=F32)


def _sigmoid(x):
    return jax.nn.sigmoid(x)


def _rms_stats(x):
    r = lax.rsqrt(jnp.mean(x * x, axis=-1, keepdims=True) + EPS)
    return x * r, r


def _rms_bwd(dy, xhat, r, g):
    dxn = dy * g
    dx = r * (dxn - xhat * jnp.mean(dxn * xhat, axis=-1, keepdims=True))
    return dx, dy * xhat


def _rowsum(v):
    return jnp.sum(v, axis=0, keepdims=True)


def _expm1(x):
    u = jnp.exp(x)
    um1 = u - 1.0
    safe = jnp.where(um1 == 0.0, 1.0, jnp.log(u))
    return jnp.where(um1 == 0.0, x, um1 * x / safe)


def _softplus(x):
    e = jnp.exp(-jnp.abs(x))
    u = 1.0 + e
    um1 = u - 1.0
    safe = jnp.where(um1 == 0.0, 1.0, um1)
    log1p_e = jnp.where(um1 == 0.0, e, jnp.log(u) * e / safe)
    return jnp.maximum(x, 0.0) + log1p_e


_GELU_K = math.sqrt(2.0 / math.pi)
_GELU_C = 0.044715


def _gelu(x):
    return 0.5 * x * (1.0 + jnp.tanh(_GELU_K * (x + _GELU_C * x * x * x)))


def _gelu_grad(x):
    th = jnp.tanh(_GELU_K * (x + _GELU_C * x * x * x))
    return 0.5 * (1.0 + th) + 0.5 * x * (1.0 - th * th) * _GELU_K * (1.0 + 3.0 * _GELU_C * x * x)


def _params(*sem):
    return pltpu.CompilerParams(dimension_semantics=sem, vmem_limit_bytes=VMEM_LIMIT)


def _rows(tm, n):
    return pl.BlockSpec((tm, n), lambda i: (i, 0))


def _rows_rev(tm, n, steps):
    return pl.BlockSpec((tm, n), lambda i: (steps - 1 - i, 0))


def _whole(shape):
    nd = len(shape)
    return pl.BlockSpec(shape, lambda i: (0,) * nd, pipeline_mode=pl.Buffered(1))


def _acc(shape):
    nd = len(shape)
    return pl.BlockSpec(shape, lambda i: (0,) * nd)


def _zero_on_first(*refs):
    @pl.when(pl.program_id(0) == 0)
    def _():
        for r in refs:
            r[...] = jnp.zeros_like(r)


def _inproj_fwd(x, g_mix, w_in_t, b_in, widths, comm=None):
    t, d = x.shape
    n = w_in_t.shape[0]
    tm = min(TOKEN_TILE, t)
    offs = [sum(widths[:i]) for i in range(len(widths) + 1)]

    def body(x_ref, g_ref, w_ref, b_ref, *outs):
        xhat, _ = _rms_stats(x_ref[...])
        h = (xhat * g_ref[...]).astype(BF16)
        for k, o_ref in enumerate(outs):
            lo, hi = offs[k], offs[k + 1]
            o_ref[...] = _dot_nt(h, w_ref[lo:hi, :]) + b_ref[:, lo:hi]

    return _run(
        body, comm, name="inproj_fwd", grid=(t // tm,),
        out_shape=[jax.ShapeDtypeStruct((t, w), F32) for w in widths],
        in_specs=[_rows(tm, d), _whole((1, d)), _whole((n, d)), _whole((1, n))],
        out_specs=[_rows(tm, w) for w in widths],
        semantics="parallel",
    )(x, g_mix, w_in_t, b_in)


def _s5_fwd(u, bbr_blk, bbi_blk, ar, ai, cre_blk, cimn_blk, d_skip, w_glu, b_glu):
    t, sa = u.shape
    ns, _, sw = bbr_blk.shape
    gn = ns * sw
    tc = min(TIME_CHUNK, t)

    def body(u_ref, bbr_ref, bbi_ref, ar_ref, ai_ref, cre_ref, cim_ref, d_ref, wg_ref, bg_ref,
             sr_ref, si_ref, y_ref, ya_ref, cr_s, ci_s, sr_s, si_s):
        _zero_on_first(cr_s, ci_s)
        uv = u_ref[...]
        ub = uv.astype(BF16)
        for m in range(ns):
            um = ub[:, m * S5_SLAB:(m + 1) * S5_SLAB]
            sr_s[:, m * sw:(m + 1) * sw] = _dot(um, bbr_ref[m])
            si_s[:, m * sw:(m + 1) * sw] = _dot(um, bbi_ref[m])
        a_r = ar_ref[...]
        a_i = ai_ref[...]

        def step(row, carry):
            c_r, c_i = carry
            at = pl.ds(row, 1)
            n_r = a_r * c_r - a_i * c_i + sr_s[at, :]
            n_i = a_r * c_i + a_i * c_r + si_s[at, :]
            sr_s[at, :] = n_r
            si_s[at, :] = n_i
            return n_r, n_i

        c_r, c_i = lax.fori_loop(0, tc, step, (cr_s[0:1, :], ci_s[0:1, :]), unroll=8)
        cr_s[0:1, :] = c_r
        ci_s[0:1, :] = c_i
        for m in range(ns):
            states, chans = slice(m * sw, (m + 1) * sw), slice(m * S5_SLAB, (m + 1) * S5_SLAB)
            s_r, s_i = sr_s[:, states].astype(BF16), si_s[:, states].astype(BF16)
            sr_ref[:, states] = s_r.astype(sr_ref.dtype)
            si_ref[:, states] = s_i.astype(si_ref.dtype)
            y_ref[:, chans] = _dot(s_r, cre_ref[m]) + _dot(s_i, cim_ref[m]) + d_ref[:, chans] * uv[:, chans]
        y = y_ref[...]
        zz = _gelu(y)
        q = _dot(zz, wg_ref[...]) + bg_ref[...]
        ya_ref[...] = zz * _sigmoid(q)

    return dict(
        body=body, steps=t // tc, args=(u, bbr_blk, bbi_blk, ar, ai, cre_blk, cimn_blk, d_skip, w_glu, b_glu),
        out_shape=[jax.ShapeDtypeStruct((t, gn), BF16), jax.ShapeDtypeStruct((t, gn), BF16),
                   jax.ShapeDtypeStruct((t, sa), F32), jax.ShapeDtypeStruct((t, sa), F32)],
        in_specs=[_rows(tc, sa), _whole(bbr_blk.shape), _whole(bbi_blk.shape), _whole((1, gn)), _whole((1, gn)),
                  _whole(cre_blk.shape), _whole(cimn_blk.shape), _whole((1, sa)), _whole((sa, sa)), _whole((1, sa))],
        out_specs=[_rows(tc, gn), _rows(tc, gn), _rows(tc, sa), _rows(tc, sa)],
        scratch=[pltpu.VMEM((SUBLANES, gn), F32), pltpu.VMEM((SUBLANES, gn), F32),
                 pltpu.VMEM((tc, gn), F32), pltpu.VMEM((tc, gn), F32)])


def _stages(stages, name, comm=None):
    counts = [[len(s[k]) for s in stages] for k in ("args", "out_shape", "scratch")]

    def body(*refs):
        groups, pos = [], 0
        for kind in counts:
            per_stage = []
            for c in kind:
                per_stage.append(refs[pos:pos + c])
                pos += c
            groups.append(per_stage)
        for s, ins, outs, scratch in zip(stages, *groups):
            s["body"](*ins, *outs, *scratch)

    res = _run(
        body, comm, name=name, grid=(stages[0]["steps"],),
        out_shape=[o for s in stages for o in s["out_shape"]], in_specs=[i for s in stages for i in s["in_specs"]],
        out_specs=[o for s in stages for o in s["out_specs"]], scratch_shapes=[c for s in stages for c in s["scratch"]],
        semantics="arbitrary",
    )(*[a for s in stages for a in s["args"]])
    extra = None
    if comm is not None:
        res, extra = res
    per_stage, pos = [], 0
    for c in counts[1]:
        per_stage.append(list(res[pos:pos + c]))
        pos += c
    return per_stage if comm is None else (per_stage, extra)


def _slab_dot(x, w_ref, transposed=False):
    dot = _dot_nt if transposed else _dot
    xb = x.astype(BF16)
    return jnp.concatenate([dot(xb[:, j * LRU_SLAB:(j + 1) * LRU_SLAB], w_ref[j]) for j in range(w_ref.shape[0])], axis=1)


def _lru_gates(xc, wr_ref, br_ref, wi_ref, bi_ref, lam_ref):
    r = _sigmoid(_slab_dot(xc, wr_ref) + br_ref[...])
    ig = _sigmoid(_slab_dot(xc, wi_ref) + bi_ref[...])
    sp = _softplus(-lam_ref[...])
    log_a = (-LRU_C * r) * sp
    return r, ig, sp, log_a


def _lru_fwd(u, conv_w, conv_b, wr_blk, b_r, wi_blk, b_i, lru_lambda):
    t, w = u.shape
    tc = min(TIME_CHUNK, t)
    halo = SUBLANES

    def body(u_ref, cw_ref, cb_ref, wr_ref, br_ref, wi_ref, bi_ref, lam_ref,
             xc_ref, h_ref, hp_ref, ext_s, a_s, carry_s):
        @pl.when(pl.program_id(0) == 0)
        def _():
            ext_s[0:halo, :] = jnp.zeros((halo, w), F32)
            carry_s[...] = jnp.zeros_like(carry_s)

        ext_s[halo:halo + tc, :] = u_ref[...]
        xc = cb_ref[...]
        for k in range(CONV_WIDTH):
            off = halo - (CONV_WIDTH - 1) + k
            xc = xc + cw_ref[k:k + 1, :] * ext_s[off:off + tc, :]
        ext_s[0:halo, :] = ext_s[tc:tc + halo, :]
        xc_ref[...] = xc
        r, ig, sp, log_a = _lru_gates(xc, wr_ref, br_ref, wi_ref, bi_ref, lam_ref)
        a_s[...] = jnp.exp(log_a)
        h_ref[...] = jnp.sqrt(-_expm1(2.0 * log_a)) * ig * xc

        def step(row, carry):
            at = pl.ds(row, 1)
            hp_ref[at, :] = carry
            nxt = a_s[at, :] * carry + h_ref[at, :]
            h_ref[at, :] = nxt
            return nxt

        carry_s[0:1, :] = lax.fori_loop(0, tc, step, carry_s[0:1, :], unroll=8)

    return dict(
        body=body, steps=t // tc, args=(u, conv_w, conv_b, wr_blk, b_r, wi_blk, b_i, lru_lambda),
        out_shape=[jax.ShapeDtypeStruct((t, w), F32)] * 3,
        in_specs=[_rows(tc, w), _whole((CONV_WIDTH, w)), _whole((1, w)), _whole(wr_blk.shape), _whole((1, w)),
                  _whole(wi_blk.shape), _whole((1, w)), _whole((1, w))],
        out_specs=[_rows(tc, w)] * 3,
        scratch=[pltpu.VMEM((halo + tc, w), F32), pltpu.VMEM((tc, w), F32), pltpu.VMEM((SUBLANES, w), F32)])


def _merge_ffn_up_fwd(y_a, h, za, zb, x, w_a_out_t, w_b_out, w_o, g_ffn, w_gate_t, w_up_t, comm=None):
    t, d = x.shape
    sa, lw = y_a.shape[1], h.shape[1]
    f = w_gate_t.shape[0]
    tm = min(TOKEN_TILE, t)

    def body(ya_ref, h_ref, za_ref, zb_ref, x_ref, wa_ref, wb_ref, wo_ref, g_ref, wg_ref, wu_ref,
             x1_ref, mg_ref, ma_ref, mb_ref, fg_ref, fu_ref):
        ma = _dot_nt(ya_ref[...], wa_ref[...])
        mb = _dot(h_ref[...], wb_ref[...])
        merged = _sigmoid(za_ref[...]) * ma + _sigmoid(zb_ref[...]) * mb
        ma_ref[...] = ma.astype(ma_ref.dtype)
        mb_ref[...] = mb.astype(mb_ref.dtype)
        mg_ref[...] = merged.astype(mg_ref.dtype)
        x1 = x_ref[...] + _dot(merged, wo_ref[...])
        x1_ref[...] = x1
        xhat, _ = _rms_stats(x1)
        h2 = (xhat * g_ref[...]).astype(BF16)
        fg_ref[...] = _dot_nt(h2, wg_ref[...]).astype(fg_ref.dtype)
        fu_ref[...] = _dot_nt(h2, wu_ref[...]).astype(fu_ref.dtype)

    return _run(
        body, comm, name="merge_ffn_up_fwd", grid=(t // tm,),
        out_shape=[jax.ShapeDtypeStruct((t, d), F32), jax.ShapeDtypeStruct((t, d), BF16),
                   jax.ShapeDtypeStruct((t, d), BF16), jax.ShapeDtypeStruct((t, d), BF16),
                   jax.ShapeDtypeStruct((t, f), BF16), jax.ShapeDtypeStruct((t, f), BF16)],
        in_specs=[_rows(tm, sa), _rows(tm, lw), _rows(tm, d), _rows(tm, d), _rows(tm, d),
                  _whole((d, sa)), _whole((lw, d)), _whole((d, d)), _whole((1, d)), _whole((f, d)), _whole((f, d))],
        out_specs=[_rows(tm, d)] * 4 + [_rows(tm, f)] * 2,
        semantics="parallel",
    )(y_a, h, za, zb, x, w_a_out_t, w_b_out, w_o, g_ffn, w_gate_t, w_up_t)


def _ffn_down_fwd(fg, fu, x1, w_down, comm=None):
    t, d = x1.shape
    f = fg.shape[1]
    tm = min(TOKEN_TILE, t)

    def body(fg_ref, fu_ref, x_ref, wd_ref, x2_ref):
        fgv = fg_ref[...].astype(F32)
        act = fgv * _sigmoid(fgv) * fu_ref[...].astype(F32)
        x2_ref[...] = x_ref[...] + _dot(act, wd_ref[...])

    return _run(
        body, comm, name="ffn_down_fwd", grid=(t // tm,),
        out_shape=jax.ShapeDtypeStruct((t, d), F32),
        in_specs=[_rows(tm, f), _rows(tm, f), _rows(tm, d), _whole((f, d))],
        out_specs=_rows(tm, d),
        semantics="parallel",
    )(fg, fu, x1, w_down)


def _store_on_last(pairs):
    @pl.when(pl.program_id(0) == pl.num_programs(0) - 1)
    def _():
        for acc, out in pairs:
            out[...] = acc[...].astype(out.dtype)


def _tail_fwd_bwd(x2, p, target, g_pg, w_pg, b_pg, w_ple_t, g_ple, g_final):
    t, d = x2.shape
    pd = p.shape[1]
    tm = min(TOKEN_TILE, t)

    def body(x2_ref, p_ref, tg_ref, gpg_ref, wpg_ref, bpg_ref, wple_ref, gple_ref, gfin_ref,
             dx2_ref, loss_ref, dwpg_out, dwple_out, vec_ref, dwpg_ref, dwple_ref):
        _zero_on_first(loss_ref, dwpg_ref, dwple_ref, vec_ref)
        x2v = x2_ref[...]
        xh2, r2 = _rms_stats(x2v)
        h3 = xh2 * gpg_ref[...]
        gp = _sigmoid(_dot(h3, wpg_ref[...]) + bpg_ref[...])
        pe = _dot_nt(p_ref[...], wple_ref[...])
        peh, r3 = _rms_stats(pe)
        e = peh * gple_ref[...]
        x3 = x2v + gp * e
        xh3, r4 = _rms_stats(x3)
        diff = xh3 * gfin_ref[...] - tg_ref[...]
        loss_ref[...] += 0.5 * jnp.sum(jnp.mean(diff * diff, axis=-1, keepdims=True))
        dy = diff * (1.0 / d)
        dx3, dgfin = _rms_bwd(dy, xh3, r4, gfin_ref[...])
        d_gp = dx3 * e
        d_e = dx3 * gp
        dpe, dgple = _rms_bwd(d_e, peh, r3, gple_ref[...])
        dwple_ref[...] += _dot_tn(dpe, p_ref[...])
        dpre = d_gp * gp * (1.0 - gp)
        dwpg_ref[...] += _dot_tn(h3, dpre)
        dh3 = _dot_nt(dpre, wpg_ref[...])
        dx2n, dgpg = _rms_bwd(dh3, xh2, r2, gpg_ref[...])
        dx2_ref[...] = dx3 + dx2n
        vec_ref[0:1, :] += _rowsum(dpre)
        vec_ref[1:2, :] += _rowsum(dgpg)
        vec_ref[2:3, :] += _rowsum(dgple)
        vec_ref[3:4, :] += _rowsum(dgfin)
        _store_on_last([(dwpg_ref, dwpg_out), (dwple_ref, dwple_out)])

    return pl.pallas_call(
        body, name="tail_fwd_bwd", grid=(t // tm,),
        out_shape=[jax.ShapeDtypeStruct((t, d), F32), jax.ShapeDtypeStruct((SUBLANES, LANES), F32),
                   jax.ShapeDtypeStruct((d, d), BF16), jax.ShapeDtypeStruct((d, pd), BF16),
                   jax.ShapeDtypeStruct((SUBLANES, d), F32)],
        in_specs=[_rows(tm, d), _rows(tm, pd), _rows(tm, d), _whole((1, d)), _whole((d, d)), _whole((1, d)),
                  _whole((d, pd)), _whole((1, d)), _whole((1, d))],
        out_specs=[_rows(tm, d), _acc((SUBLANES, LANES)), _acc((d, d)), _acc((d, pd)), _acc((SUBLANES, d))],
        scratch_shapes=[pltpu.VMEM((d, d), F32), pltpu.VMEM((d, pd), F32)],
        compiler_params=_params("arbitrary"),
    )(x2, p, target, g_pg, w_pg, b_pg, w_ple_t, g_ple, g_final)


def _ffn_bwd_a(dx2, fg, fu, w_down, comm=None):
    t, d = dx2.shape
    f = fg.shape[1]
    tm = min(TOKEN_TILE, t)

    def body(dx_ref, fg_ref, fu_ref, wd_ref, dfg_ref, dfu_ref, act_ref):
        dact = _dot_nt(dx_ref[...], wd_ref[...])
        fgv = fg_ref[...].astype(F32)
        fuv = fu_ref[...].astype(F32)
        sg = _sigmoid(fgv)
        silu = fgv * sg
        dfu_ref[...] = (dact * silu).astype(dfu_ref.dtype)
        dfg_ref[...] = (dact * fuv * (sg * (1.0 + fgv * (1.0 - sg)))).astype(dfg_ref.dtype)
        act_ref[...] = (silu * fuv).astype(act_ref.dtype)

    return _run(
        body, comm, name="ffn_bwd_a", grid=(t // tm,),
        out_shape=[jax.ShapeDtypeStruct((t, f), BF16)] * 3,
        in_specs=[_rows(tm, d), _rows(tm, f), _rows(tm, f), _whole((f, d))],
        out_specs=[_rows(tm, f)] * 3,
        semantics="parallel",
    )(dx2, fg, fu, w_down)


def _ffn_bwd_b(dfg, dfu, x1, dx2, g_ffn, w_gate_t, w_up_t, comm=None):
    t, d = x1.shape
    f = dfg.shape[1]
    tm = min(TOKEN_TILE, t)

    def body(dfg_ref, dfu_ref, x_ref, dx2_ref, g_ref, wg_ref, wu_ref, dx1_ref, h2_ref, vec_ref):
        _zero_on_first(vec_ref)
        dh2 = _dot(dfg_ref[...], wg_ref[...]) + _dot(dfu_ref[...], wu_ref[...])
        xhat, r = _rms_stats(x_ref[...])
        h2_ref[...] = (xhat * g_ref[...]).astype(h2_ref.dtype)
        dxn, dg = _rms_bwd(dh2, xhat, r, g_ref[...])
        dx1_ref[...] = dx2_ref[...] + dxn
        vec_ref[0:1, :] += _rowsum(dg)

    return _run(
        body, comm, name="ffn_bwd_b", grid=(t // tm,),
        out_shape=[jax.ShapeDtypeStruct((t, d), F32), jax.ShapeDtypeStruct((t, d), BF16),
                   jax.ShapeDtypeStruct((SUBLANES, d), F32)],
        in_specs=[_rows(tm, f), _rows(tm, f), _rows(tm, d), _rows(tm, d), _whole((1, d)), _whole((f, d)), _whole((f, d))],
        out_specs=[_rows(tm, d), _rows(tm, d), _acc((SUBLANES, d))],
        semantics="arbitrary",
    )(dfg, dfu, x1, dx2, g_ffn, w_gate_t, w_up_t)


def _matmul_tn(a, b, tn, name, dtype=F32):
    t, k = a.shape
    n = b.shape[1]

    def body(a_ref, b_ref, o_ref):
        o_ref[...] = _dot_tn(a_ref[...], b_ref[...]).astype(o_ref.dtype)

    return _run(
        body, None, name=name, grid=(n // tn,),
        out_shape=jax.ShapeDtypeStruct((k, n), dtype),
        in_specs=[_whole((t, k)), pl.BlockSpec((t, tn), lambda j: (0, j))],
        out_specs=pl.BlockSpec((k, tn), lambda j: (0, j)),
        semantics="parallel",
    )(a, b)


def _merge_bwd(dx1, merged, ma, mb, za, zb, y_a, h, w_o, w_a_out_t, w_b_out, comm=None):
    t, d = dx1.shape
    sa, lw = y_a.shape[1], h.shape[1]
    tm = min(TOKEN_TILE, t)

    def body(dx1_ref, mg_ref, ma_ref, mb_ref, za_ref, zb_ref, ya_ref, h_ref, wo_ref, wa_ref, wb_ref,
             dza_ref, dzb_ref, dya_ref, dyb_ref, dwo_out, dwa_out, dwb_out, dwo_ref, dwa_ref, dwb_ref):
        _zero_on_first(dwo_ref, dwa_ref, dwb_ref)
        dx1v = dx1_ref[...].astype(BF16)
        dmg = _dot_nt(dx1v, wo_ref[...])
        ga = _sigmoid(za_ref[...])
        gb = _sigmoid(zb_ref[...])
        dza_ref[...] = (dmg * ma_ref[...].astype(F32) * ga * (1.0 - ga)).astype(dza_ref.dtype)
        dzb_ref[...] = (dmg * mb_ref[...].astype(F32) * gb * (1.0 - gb)).astype(dzb_ref.dtype)
        dma = (dmg * ga).astype(BF16)
        dmb = (dmg * gb).astype(BF16)
        dya_ref[...] = _dot(dma, wa_ref[...])
        dyb_ref[...] = _dot_nt(dmb, wb_ref[...])
        dwo_ref[...] += _dot_tn(mg_ref[...], dx1v)
        dwa_ref[...] += _dot_tn(dma, ya_ref[...])
        dwb_ref[...] += _dot_tn(h_ref[...], dmb)
        _store_on_last([(dwo_ref, dwo_out), (dwa_ref, dwa_out), (dwb_ref, dwb_out)])

    return _run(
        body, comm, name="merge_bwd", grid=(t // tm,),
        out_shape=[jax.ShapeDtypeStruct((t, d), BF16), jax.ShapeDtypeStruct((t, d), BF16),
                   jax.ShapeDtypeStruct((t, sa), F32), jax.ShapeDtypeStruct((t, lw), F32),
                   jax.ShapeDtypeStruct((d, d), BF16), jax.ShapeDtypeStruct((d, sa), BF16),
                   jax.ShapeDtypeStruct((lw, d), BF16)],
        in_specs=[_rows(tm, d), _rows(tm, d), _rows(tm, d), _rows(tm, d), _rows(tm, d), _rows(tm, d),
                  _rows(tm, sa), _rows(tm, lw), _whole((d, d)), _whole((d, sa)), _whole((lw, d))],
        out_specs=[_rows(tm, d), _rows(tm, d), _rows(tm, sa), _rows(tm, lw), _acc((d, d)), _acc((d, sa)), _acc((lw, d))],
        scratch_shapes=[pltpu.VMEM((d, d), F32), pltpu.VMEM((d, sa), F32), pltpu.VMEM((lw, d), F32)],
        semantics="arbitrary",
    )(dx1, merged, ma, mb, za, zb, y_a, h, w_o, w_a_out_t, w_b_out)


def _fold_diag_blocks(dense, row_group, col_group):
    r, c = dense.shape
    rows = lax.broadcasted_iota(jnp.int32, (r, c), 0)
    cols = lax.broadcasted_iota(jnp.int32, (r, c), 1)
    kept = jnp.where(rows // row_group == cols // col_group, dense, 0.0)
    pick = (lax.broadcasted_iota(jnp.int32, (row_group, r), 0)
            == lax.broadcasted_iota(jnp.int32, (row_group, r), 1) % row_group).astype(F32)
    return jnp.dot(pick, kept, preferred_element_type=F32, precision=lax.Precision.HIGHEST)


def _lru_bwd(dh, xc, hprev, u, conv_w, wr_blk, b_r, wi_blk, b_i, lru_lambda, head_dim, comm=None):
    t, w = dh.shape
    tc = min(TIME_CHUNK, t)
    steps = t // tc
    halo = SUBLANES
    sub_per_chunk = tc // halo
    slabs = w // LRU_SLAB

    def body(dh_ref, xc_ref, hp_ref, u_ref, uh_ref, cw_ref, wr_ref, br_ref, wi_ref, bi_ref, lam_ref,
             du_ref, dwr_out, dwi_out, vec_ref, lam_s, a_s, dxc_s, uext_s, carry_s, dwr_ref, dwi_ref):
        chunk = steps - 1 - pl.program_id(0)

        @pl.when(pl.program_id(0) == 0)
        def _():
            carry_s[...] = jnp.zeros_like(carry_s)
            dxc_s[tc:tc + halo, :] = jnp.zeros((halo, w), F32)
            dwr_ref[...] = jnp.zeros_like(dwr_ref)
            dwi_ref[...] = jnp.zeros_like(dwi_ref)
            vec_ref[...] = jnp.zeros_like(vec_ref)

        xc = xc_ref[...]
        r, ig, sp, log_a = _lru_gates(xc, wr_ref, br_ref, wi_ref, bi_ref, lam_ref)
        a = jnp.exp(log_a)
        a_s[...] = a

        def step(i, q):
            at = pl.ds(tc - 1 - i, 1)
            lam_row = dh_ref[at, :] + q
            lam_s[at, :] = lam_row
            return a_s[at, :] * lam_row

        carry_s[0:1, :] = lax.fori_loop(0, tc, step, carry_s[0:1, :], unroll=8)
        lam = lam_s[...]
        mult = jnp.sqrt(-_expm1(2.0 * log_a))
        d_log_a = lam * hp_ref[...] * a - (lam * ig * xc) * (a * a) / mult
        d_ig = lam * mult * xc
        dpre_r = (d_log_a * (-LRU_C * sp)) * r * (1.0 - r)
        dpre_i = d_ig * ig * (1.0 - ig)
        dxc = lam * mult * ig + _slab_dot(dpre_r, wr_ref, transposed=True) + _slab_dot(dpre_i, wi_ref, transposed=True)
        xcb, drb, dib = xc.astype(BF16), dpre_r.astype(BF16), dpre_i.astype(BF16)
        for j in range(slabs):
            cols = slice(j * LRU_SLAB, (j + 1) * LRU_SLAB)
            dwr_ref[j] += _dot_tn(drb[:, cols], xcb[:, cols])
            dwi_ref[j] += _dot_tn(dib[:, cols], xcb[:, cols])
        vec_ref[0:1, :] += _rowsum(dxc)
        vec_ref[1:2, :] += _rowsum(dpre_r)
        vec_ref[2:3, :] += _rowsum(dpre_i)
        vec_ref[3:4, :] += _rowsum(d_log_a * (-LRU_C * r)) * (-_sigmoid(-lam_ref[...]))
        dxc_s[0:tc, :] = dxc
        du = cw_ref[CONV_WIDTH - 1:CONV_WIDTH, :] * dxc
        for k in range(CONV_WIDTH - 1):
            off = CONV_WIDTH - 1 - k
            du = du + cw_ref[k:k + 1, :] * dxc_s[off:off + tc, :]
        du_ref[...] = du.astype(du_ref.dtype)
        dxc_s[tc:tc + halo, :] = dxc_s[0:halo, :]
        uext_s[0:halo, :] = jnp.where(chunk > 0, uh_ref[...], 0.0)
        uext_s[halo:halo + tc, :] = u_ref[...]
        for k in range(CONV_WIDTH):
            off = halo - (CONV_WIDTH - 1) + k
            vec_ref[4 + k:5 + k, :] += _rowsum(dxc * uext_s[off:off + tc, :])

        @pl.when(pl.program_id(0) == steps - 1)
        def _():
            for j in range(slabs):
                cols = slice(j * LRU_SLAB, (j + 1) * LRU_SLAB)
                dwr_out[:, cols] = _fold_diag_blocks(dwr_ref[j], head_dim, head_dim).astype(dwr_out.dtype)
                dwi_out[:, cols] = _fold_diag_blocks(dwi_ref[j], head_dim, head_dim).astype(dwi_out.dtype)

    halo_spec = pl.BlockSpec((halo, w), lambda i: (jnp.maximum((steps - 1 - i) * sub_per_chunk - 1, 0), 0))
    return _run(
        body, comm, name="lru_bwd", grid=(steps,),
        out_shape=[jax.ShapeDtypeStruct((t, w), BF16), jax.ShapeDtypeStruct((head_dim, w), BF16),
                   jax.ShapeDtypeStruct((head_dim, w), BF16), jax.ShapeDtypeStruct((SUBLANES, w), F32)],
        in_specs=[_rows_rev(tc, w, steps)] * 4 + [halo_spec, _whole((CONV_WIDTH, w)), _whole(wr_blk.shape),
                                                  _whole((1, w)), _whole(wi_blk.shape), _whole((1, w)), _whole((1, w))],
        out_specs=[_rows_rev(tc, w, steps), _acc((head_dim, w)), _acc((head_dim, w)), _acc((SUBLANES, w))],
        scratch_shapes=[pltpu.VMEM((tc, w), F32), pltpu.VMEM((tc, w), F32), pltpu.VMEM((tc + halo, w), F32),
                        pltpu.VMEM((halo + tc, w), F32), pltpu.VMEM((SUBLANES, w), F32),
                        pltpu.VMEM((slabs, LRU_SLAB, LRU_SLAB), F32), pltpu.VMEM((slabs, LRU_SLAB, LRU_SLAB), F32)],
        semantics="arbitrary",
    )(dh, xc, hprev, u, u, conv_w, wr_blk, b_r, wi_blk, b_i, lru_lambda)


def _s5_bwd(dya, y, sr, si, u, w_glu, b_glu, cre_blk, cimn_blk, bbr_blk, bbi_blk, ar, ai, d_skip, comm=None):
    t, sa = dya.shape
    gn = sr.shape[1]
    ns, _, sw = bbr_blk.shape
    tc = min(TIME_CHUNK, t)
    steps = t // tc
    halo = SUBLANES

    def body(dya_ref, y_ref, sr_ref, si_ref, u_ref, wg_ref, bg_ref, cre_ref, cim_ref, bbr_ref, bbi_ref,
             ar_ref, ai_ref, d_ref, du_ref, lr_ref, li_ref, dy_ref, dwg_out, vsa_ref, vgn_ref, gr_s, gi_s, cr_s, ci_s,
             dwg_ref):
        @pl.when(pl.program_id(0) == 0)
        def _():
            cr_s[...] = jnp.zeros_like(cr_s)
            ci_s[...] = jnp.zeros_like(ci_s)
            gr_s[tc:tc + halo, :] = jnp.zeros((halo, gn), F32)
            gi_s[tc:tc + halo, :] = jnp.zeros((halo, gn), F32)
            dwg_ref[...] = jnp.zeros_like(dwg_ref)
            vsa_ref[...] = jnp.zeros_like(vsa_ref)
            vgn_ref[...] = jnp.zeros_like(vgn_ref)

        yv = y_ref[...]
        uv = u_ref[...]
        zz = _gelu(yv)
        sg = _sigmoid(_dot(zz, wg_ref[...]) + bg_ref[...])
        dyav = dya_ref[...]
        dq = dyav * zz * sg * (1.0 - sg)
        dzz = dyav * sg + _dot_nt(dq, wg_ref[...])
        dwg_ref[...] += _dot_tn(zz, dq)
        dy = dzz * _gelu_grad(yv)
        dyb = dy.astype(BF16)
        dy_ref[...] = dyb.astype(dy_ref.dtype)
        vsa_ref[0:1, :] += _rowsum(dq)
        vsa_ref[1:2, :] += _rowsum(dy * uv)
        for m in range(ns):
            dym = dyb[:, m * S5_SLAB:(m + 1) * S5_SLAB]
            gr_s[0:tc, m * sw:(m + 1) * sw] = _dot_nt(dym, cre_ref[m])
            gi_s[0:tc, m * sw:(m + 1) * sw] = _dot_nt(dym, cim_ref[m])
        a_r = ar_ref[...]
        a_i = ai_ref[...]

        def step(i, carry):
            l_r, l_i = carry
            at = pl.ds(tc - 1 - i, 1)
            n_r = gr_s[at, :] + a_r * l_r + a_i * l_i
            n_i = gi_s[at, :] + a_r * l_i - a_i * l_r
            gr_s[at, :] = n_r
            gi_s[at, :] = n_i
            return n_r, n_i

        l_r, l_i = lax.fori_loop(0, tc, step, (cr_s[0:1, :], ci_s[0:1, :]), unroll=8)
        cr_s[0:1, :] = l_r
        ci_s[0:1, :] = l_i
        nxt_r = gr_s[1:tc + 1, :]
        nxt_i = gi_s[1:tc + 1, :]
        srv = sr_ref[...].astype(F32)
        siv = si_ref[...].astype(F32)
        vgn_ref[0:1, :] += _rowsum(nxt_r * srv + nxt_i * siv)
        vgn_ref[1:2, :] += _rowsum(nxt_i * srv - nxt_r * siv)
        lam_r = gr_s[0:tc, :]
        lam_i = gi_s[0:tc, :]
        gr_s[tc:tc + halo, :] = gr_s[0:halo, :]
        gi_s[tc:tc + halo, :] = gi_s[0:halo, :]
        lrb = lam_r.astype(BF16)
        lib = lam_i.astype(BF16)
        lr_ref[...] = lrb.astype(lr_ref.dtype)
        li_ref[...] = lib.astype(li_ref.dtype)
        for m in range(ns):
            states, chans = slice(m * sw, (m + 1) * sw), slice(m * S5_SLAB, (m + 1) * S5_SLAB)
            du_ref[:, chans] = (_dot_nt(lrb[:, states], bbr_ref[m]) + _dot_nt(lib[:, states], bbi_ref[m])
                                + dy[:, chans] * d_ref[:, chans]).astype(du_ref.dtype)
        _store_on_last([(dwg_ref, dwg_out)])

    return _run(
        body, comm, name="s5_bwd", grid=(steps,),
        out_shape=[jax.ShapeDtypeStruct((t, sa), BF16), jax.ShapeDtypeStruct((t, gn), BF16),
                   jax.ShapeDtypeStruct((t, gn), BF16), jax.ShapeDtypeStruct((t, sa), BF16),
                   jax.ShapeDtypeStruct((sa, sa), BF16), jax.ShapeDtypeStruct((SUBLANES, sa), F32),
                   jax.ShapeDtypeStruct((SUBLANES, gn), F32)],
        in_specs=[_rows_rev(tc, sa, steps), _rows_rev(tc, sa, steps), _rows_rev(tc, gn, steps), _rows_rev(tc, gn, steps),
                  _rows_rev(tc, sa, steps), _whole((sa, sa)), _whole((1, sa)), _whole(cre_blk.shape),
                  _whole(cimn_blk.shape), _whole(bbr_blk.shape), _whole(bbi_blk.shape), _whole((1, gn)), _whole((1, gn)),
                  _whole((1, sa))],
        out_specs=[_rows_rev(tc, sa, steps), _rows_rev(tc, gn, steps), _rows_rev(tc, gn, steps), _rows_rev(tc, sa, steps),
                   _acc((sa, sa)), _acc((SUBLANES, sa)), _acc((SUBLANES, gn))],
        scratch_shapes=[pltpu.VMEM((tc + halo, gn), F32), pltpu.VMEM((tc + halo, gn), F32),
                        pltpu.VMEM((SUBLANES, gn), F32), pltpu.VMEM((SUBLANES, gn), F32), pltpu.VMEM((sa, sa), F32)],
        semantics="arbitrary",
    )(dya, y, sr, si, u, w_glu, b_glu, cre_blk, cimn_blk, bbr_blk, bbi_blk, ar, ai, d_skip)


def _inproj_bwd(dparts, x, dx1, g_mix, w_in_t, comm=None):
    t, d = x.shape
    n = w_in_t.shape[0]
    widths = [p.shape[1] for p in dparts]
    offs = [sum(widths[:i]) for i in range(len(widths) + 1)]
    tm = min(TOKEN_TILE, t)
    np_ = len(dparts)

    def body(*refs):
        dz_refs = refs[:np_]
        x_ref, dx1_ref, g_ref, w_ref, gx_ref, h_ref, vd_ref, vn_ref = refs[np_:]
        _zero_on_first(vd_ref, vn_ref)
        dh = jnp.zeros((tm, d), F32)
        for k, r in enumerate(dz_refs):
            lo, hi = offs[k], offs[k + 1]
            dzk = r[...]
            dh = dh + _dot(dzk, w_ref[lo:hi, :])
            vn_ref[0:1, lo:hi] += _rowsum(dzk.astype(F32))
        xhat, r0 = _rms_stats(x_ref[...])
        h_ref[...] = (xhat * g_ref[...]).astype(h_ref.dtype)
        dxn, dg = _rms_bwd(dh, xhat, r0, g_ref[...])
        gx_ref[...] = dx1_ref[...] + dxn
        vd_ref[0:1, :] += _rowsum(dg)

    return _run(
        body, comm, name="inproj_bwd", grid=(t // tm,),
        out_shape=[jax.ShapeDtypeStruct((t, d), F32), jax.ShapeDtypeStruct((t, d), BF16),
                   jax.ShapeDtypeStruct((SUBLANES, d), F32), jax.ShapeDtypeStruct((SUBLANES, n), F32)],
        in_specs=[_rows(tm, w) for w in widths] + [_rows(tm, d), _rows(tm, d), _whole((1, d)), _whole((n, d))],
        out_specs=[_rows(tm, d), _rows(tm, d), _acc((SUBLANES, d)), _acc((SUBLANES, n))],
        semantics="arbitrary",
    )(*dparts, x, dx1, g_mix, w_in_t)


def _dw_from_parts(dparts, h, tn, name, comm=None):
    t, d = h.shape
    widths = [p.shape[1] for p in dparts]
    offs = [sum(widths[:i]) for i in range(len(widths) + 1)]
    np_ = len(dparts)

    def body(*refs):
        h_ref, o_ref = refs[np_], refs[np_ + 1]
        hv = h_ref[...]
        for k, r in enumerate(refs[:np_]):
            o_ref[offs[k]:offs[k + 1], :] = _dot_tn(r[...], hv).astype(o_ref.dtype)

    return _run(
        body, comm, name=name, grid=(d // tn,),
        out_shape=jax.ShapeDtypeStruct((offs[-1], d), BF16),
        in_specs=[_whole(p.shape) for p in dparts] + [pl.BlockSpec((t, tn), lambda j: (0, j))],
        out_specs=pl.BlockSpec((offs[-1], tn), lambda j: (0, j)),
        semantics="parallel",
    )(*dparts, h)


def _prep(lr_row, li_row, ldt_row, lr_col, li_col, ldt_col, b_re, b_im, c_re, c_im, w_r, w_i, comm=None):
    gn, pch = b_re.shape
    sa, n = c_re.shape
    w, hd = w_r.shape
    ns, sw, lsl = sa // S5_SLAB, S5_SLAB * n // pch, w // LRU_SLAB

    def spread_cols(vals, row_group, col_group, width):
        r, k = vals.shape
        tile = (lax.broadcasted_iota(jnp.int32, (k, width), 0) == lax.broadcasted_iota(jnp.int32, (k, width), 1) % k)
        rows = lax.broadcasted_iota(jnp.int32, (r, width), 0) // row_group
        cols = lax.broadcasted_iota(jnp.int32, (r, width), 1) // col_group
        return jnp.where(rows == cols, _dot(vals, tile.astype(BF16)), 0.0)

    def spread_rows(vals, row_group, col_group, height):
        k, c = vals.shape
        tile = (lax.broadcasted_iota(jnp.int32, (height, k), 0) % k == lax.broadcasted_iota(jnp.int32, (height, k), 1))
        rows = lax.broadcasted_iota(jnp.int32, (height, c), 0) // row_group
        cols = lax.broadcasted_iota(jnp.int32, (height, c), 1) // col_group
        return jnp.where(rows == cols, _dot(tile.astype(BF16), vals), 0.0)

    def body(lrr, lir, ldr, lrc, lic, ldc, bre, bim, cre, cim, wr, wi,
             ar_o, ai_o, bbr_o, bbi_o, cre_o, cim_o, wr_o, wi_o):
        ar, ai, _, _ = _disc_scalars(lrr[...], lir[...], ldr[...])
        ar_o[...] = ar
        ai_o[...] = ai
        _, _, bbr, bbi = _disc_cols(lrc[...], lic[...], ldc[...], bre[...], bim[...])
        bbr_t, bbi_t = bbr.T, bbi.T
        for m in range(ns):
            bbr_o[m] = spread_rows(bbr_t[:, m * sw:(m + 1) * sw], pch, n, S5_SLAB).astype(bbr_o.dtype)
            bbi_o[m] = spread_rows(bbi_t[:, m * sw:(m + 1) * sw], pch, n, S5_SLAB).astype(bbi_o.dtype)
            rows = slice(m * S5_SLAB, (m + 1) * S5_SLAB)
            cre_o[m] = spread_rows(cre[rows, :].T, n, pch, sw).astype(cre_o.dtype)
            cim_o[m] = spread_rows(-cim[rows, :].T, n, pch, sw).astype(cim_o.dtype)
        for j in range(lsl):
            rows = slice(j * LRU_SLAB, (j + 1) * LRU_SLAB)
            wr_o[j] = spread_cols(wr[rows, :], hd, hd, LRU_SLAB).astype(wr_o.dtype)
            wi_o[j] = spread_cols(wi[rows, :], hd, hd, LRU_SLAB).astype(wi_o.dtype)

    args = (lr_row, li_row, ldt_row, lr_col, li_col, ldt_col, b_re, b_im, c_re, c_im, w_r, w_i)
    out_shape = [jax.ShapeDtypeStruct((1, gn), F32), jax.ShapeDtypeStruct((1, gn), F32),
                 jax.ShapeDtypeStruct((ns, S5_SLAB, sw), BF16), jax.ShapeDtypeStruct((ns, S5_SLAB, sw), BF16),
                 jax.ShapeDtypeStruct((ns, sw, S5_SLAB), BF16), jax.ShapeDtypeStruct((ns, sw, S5_SLAB), BF16),
                 jax.ShapeDtypeStruct((lsl, LRU_SLAB, LRU_SLAB), BF16),
                 jax.ShapeDtypeStruct((lsl, LRU_SLAB, LRU_SLAB), BF16)]
    return _run(
        body, comm, name="prep", grid=(1,), out_shape=out_shape, in_specs=[_whole(a.shape) for a in args],
        out_specs=[_acc(s.shape) for s in out_shape], semantics="arbitrary",
    )(*args)


def _s5_param_grads(lam_r, lam_i, sr, si, u, dy, pch, n, comm=None):
    t, gn = lam_r.shape
    sa = u.shape[1]
    sw = S5_SLAB * n // pch

    def body(lr_ref, li_ref, sr_ref, si_ref, u_ref, dy_ref, dbr_ref, dbi_ref, dcr_ref, dci_ref):
        uv = u_ref[...]
        dyv = dy_ref[...]
        dbr_ref[...] = _fold_diag_blocks(_dot_tn(uv, lr_ref[...]), pch, n).astype(dbr_ref.dtype)
        dbi_ref[...] = _fold_diag_blocks(_dot_tn(uv, li_ref[...]), pch, n).astype(dbi_ref.dtype)
        dcr_ref[...] = _fold_diag_blocks(_dot_tn(sr_ref[...], dyv), n, pch).astype(dcr_ref.dtype)
        dci_ref[...] = _fold_diag_blocks(_dot_tn(si_ref[...], dyv), n, pch).astype(dci_ref.dtype)

    states = pl.BlockSpec((t, sw), lambda m: (0, m))
    chans = pl.BlockSpec((t, S5_SLAB), lambda m: (0, m))
    return _run(
        body, comm, name="s5_param_grads", grid=(sa // S5_SLAB,),
        out_shape=[jax.ShapeDtypeStruct((pch, gn), BF16), jax.ShapeDtypeStruct((pch, gn), BF16),
                   jax.ShapeDtypeStruct((n, sa), BF16), jax.ShapeDtypeStruct((n, sa), BF16)],
        in_specs=[states, states, states, states, chans, chans],
        out_specs=[pl.BlockSpec((pch, sw), lambda m: (0, m)), pl.BlockSpec((pch, sw), lambda m: (0, m)),
                   pl.BlockSpec((n, S5_SLAB), lambda m: (0, m)), pl.BlockSpec((n, S5_SLAB), lambda m: (0, m))],
        semantics="parallel",
    )(lam_r, lam_i, sr, si, u, dy)


SMALL_PARTS = ["vec_tail", "vec_ffn", "vec_lru", "vec_mix", "vec_bin", "vec_sa", "vec_gn", "dw_r", "dw_i", "dbb_re",
               "dbb_im", "dc_re", "dc_imn", "loss"]


def _small_reduce(parts, shapes, lr_col, li_col, ldt_col, b_re, b_im, groups, after=()):
    gn, pch = b_re.shape
    n = gn // groups
    nparts = parts[SMALL_PARTS[0]].size // math.prod(shapes[SMALL_PARTS[0]])
    np_, nout, first_out = len(SMALL_PARTS), 24, len(SMALL_PARTS) + 5 + len(after)

    def body(*refs):
        ins = refs[:np_]
        lr, li, ldt, bre, bim = refs[np_:np_ + 5]
        outs = refs[first_out:first_out + nout]
        sums = dict(zip(SMALL_PARTS, refs[first_out + nout:]))

        @pl.when(pl.program_id(0) == 0)
        def _():
            for k, r in zip(SMALL_PARTS, ins):
                sums[k][...] = r[...].astype(F32)

        @pl.when(pl.program_id(0) > 0)
        def _():
            for k, r in zip(SMALL_PARTS, ins):
                sums[k][...] += r[...].astype(F32)

        @pl.when(pl.program_id(0) == nparts - 1)
        def _():
            finish({k: s[...] for k, s in sums.items()}, lr, li, ldt, bre, bim, *outs)

    def finish(tot, lr, li, ldt, bre, bim, o_loss, o_gmix, o_bin, o_bglu, o_s5d, o_convb, o_br, o_bi, o_lam, o_gffn,
               o_gpg, o_bpg, o_gple, o_gfin, o_wr, o_wi, o_cre, o_cim, o_bre, o_bim, o_lre, o_lim, o_ldt, o_convw):
        o_loss[...] = tot["loss"]
        o_convw[...] = tot["vec_lru"][SUBLANES - CONV_WIDTH:SUBLANES]
        o_bpg[...] = tot["vec_tail"][0:1]
        o_gpg[...] = tot["vec_tail"][1:2]
        o_gple[...] = tot["vec_tail"][2:3]
        o_gfin[...] = tot["vec_tail"][3:4]
        o_gffn[...] = tot["vec_ffn"][0:1]
        o_convb[...] = tot["vec_lru"][0:1]
        o_br[...] = tot["vec_lru"][1:2]
        o_bi[...] = tot["vec_lru"][2:3]
        o_lam[...] = tot["vec_lru"][3:4]
        o_gmix[...] = tot["vec_mix"][0:1]
        o_bin[...] = tot["vec_bin"][0:1]
        o_bglu[...] = tot["vec_sa"][0:1]
        o_s5d[...] = tot["vec_sa"][1:2]
        o_wr[...] = tot["dw_r"].T
        o_wi[...] = tot["dw_i"].T
        o_cre[...] = tot["dc_re"].T
        o_cim[...] = -tot["dc_imn"].T
        d_a = tot["vec_gn"].T
        _, chain = jax.vjp(_disc_cols, lr[...], li[...], ldt[...], bre[...], bim[...])
        d_lr, d_li, d_ldt, d_bre, d_bim = chain((d_a[:, 0:1], d_a[:, 1:2], tot["dbb_re"].T, tot["dbb_im"].T))
        o_lre[...] = d_lr
        o_lim[...] = d_li
        o_bre[...] = d_bre
        o_bim[...] = d_bim
        same = (lax.broadcasted_iota(jnp.int32, (groups, gn), 0)
                == lax.broadcasted_iota(jnp.int32, (groups, gn), 1) // n).astype(F32)
        o_ldt[...] = jnp.dot(same, d_ldt * jnp.ones((1, LANES), F32), preferred_element_type=F32,
                             precision=lax.Precision.HIGHEST)[:, 0:1]

    d = shapes["vec_mix"][1]
    nz = shapes["vec_bin"][1]
    sa = shapes["vec_sa"][1]
    w = shapes["vec_lru"][1]
    row = lambda c: jax.ShapeDtypeStruct((1, c), F32)
    out_shape = [jax.ShapeDtypeStruct(shapes["loss"], F32), row(d), row(nz), row(sa), row(sa), row(w), row(w), row(w),
                 row(w), row(d), row(d), row(d), row(d), row(d),
                 jax.ShapeDtypeStruct(shapes["dw_r"][::-1], F32), jax.ShapeDtypeStruct(shapes["dw_i"][::-1], F32),
                 jax.ShapeDtypeStruct(shapes["dc_re"][::-1], F32), jax.ShapeDtypeStruct(shapes["dc_imn"][::-1], F32),
                 jax.ShapeDtypeStruct((gn, pch), F32), jax.ShapeDtypeStruct((gn, pch), F32),
                 jax.ShapeDtypeStruct((gn, 1), F32), jax.ShapeDtypeStruct((gn, 1), F32),
                 jax.ShapeDtypeStruct((groups, 1), F32), jax.ShapeDtypeStruct((CONV_WIDTH, w), F32)]
    def part_spec(k):
        r, c = shapes[k]
        if parts[k].ndim == 3:
            return pl.BlockSpec((None, r, c), lambda i: (i, 0, 0))
        return pl.BlockSpec((r, c), lambda i: (i, 0))

    outs = _run(
        body, None, name="small_reduce", grid=(nparts,), out_shape=out_shape,
        in_specs=([part_spec(k) for k in SMALL_PARTS] + [_whole(a.shape) for a in (lr_col, li_col, ldt_col, b_re, b_im)]
                  + [_ANY] * len(after)),
        out_specs=[_acc(s.shape) for s in out_shape],
        scratch_shapes=[pltpu.VMEM(shapes[k], F32) for k in SMALL_PARTS],
        semantics="arbitrary",
    )(*[parts[k] for k in SMALL_PARTS], lr_col, li_col, ldt_col, b_re, b_im, *after)
    names = ["loss", "g_mix", "b_in", "b_glu", "s5_d", "conv_b", "b_r", "b_i", "lru_lambda", "g_ffn", "g_ple_gate",
             "b_ple_gate", "g_ple", "g_final", "w_r", "w_i", "s5_c_re", "s5_c_im", "s5_b_re", "s5_b_im", "lam_re",
             "lam_im", "log_dt", "conv_w"]
    return dict(zip(names, outs))


def _adamw_small(ws, gs, ms, vs):
    n = len(ws)

    def body(*refs):
        w_r, g_r, m_r, v_r = (refs[i * n:(i + 1) * n] for i in range(4))
        g_o, d_o, m_o, v_o = (refs[(4 + i) * n:(5 + i) * n] for i in range(4))
        for i in range(n):
            g = g_r[i][...]
            delta, m_new, v_new = _adamw_math(w_r[i][...], g, m_r[i][...], v_r[i][...])
            g_o[i][...] = g
            d_o[i][...] = delta
            m_o[i][...] = m_new
            v_o[i][...] = v_new

    shapes = [jax.ShapeDtypeStruct(a.shape, F32) for a in ws]
    outs = pl.pallas_call(body, name="adamw_small", out_shape=shapes * 4)(*ws, *gs, *ms, *vs)
    return outs[:n], outs[n:2 * n], outs[2 * n:3 * n], outs[3 * n:]


def _adamw_math(w, g, m, v):
    m_new = ADAM_B1 * m + (1.0 - ADAM_B1) * g
    v_new = ADAM_B2 * v + (1.0 - ADAM_B2) * (g * g)
    m_hat = m_new / (1.0 - ADAM_B1 ** ADAM_STEP)
    v_hat = v_new / (1.0 - ADAM_B2 ** ADAM_STEP)
    delta = -ADAM_LR * (m_hat / (jnp.sqrt(v_hat) + ADAM_EPS) + ADAM_WD * w)
    return delta, m_new, v_new


def _row_tile(rows):
    for cand in range(256, 0, -16):
        if rows % cand == 0:
            return cand
    return rows


def _adamw(parts, w, m, v, name, transposed=False, own=None, after=()):
    nw = len(w)
    rows, cols = w[0].shape
    npart = parts[0].shape[0]
    tr = _row_tile(rows)
    if transposed:
        parts_spec = pl.BlockSpec((npart, cols, tr), lambda i: (0, 0, i))
    else:
        parts_spec = pl.BlockSpec((npart, tr, cols), lambda i: (0, i, 0))
    per = 4 if own is None else 5
    first_out = per * nw + len(after)

    def body(*refs):
        mine = None if own is None else _chip(_mesh_position())
        for i in range(nw):
            group = refs[per * i:per * i + per]
            p_ref, (w_ref, m_ref, v_ref) = group[0], group[-3:]
            g_ref, d_ref, mo_ref, vo_ref = refs[first_out + 4 * i:first_out + 4 * i + 4]

            def part(k):
                a = p_ref[k].astype(F32)
                return a if own is None else jnp.where(mine == k, group[1][k].astype(F32), a)

            g = part(0)
            for k in range(1, npart):
                g = g + part(k)
            if transposed:
                g = g.T
            delta, m_new, v_new = _adamw_math(w_ref[...], g, m_ref[...], v_ref[...])
            g_ref[...] = g
            d_ref[...] = delta
            mo_ref[...] = m_new
            vo_ref[...] = v_new

    groups = zip(parts, w, m, v) if own is None else zip(parts, own, w, m, v)
    res = pl.pallas_call(
        body, name=name, grid=(rows // tr,),
        out_shape=[jax.ShapeDtypeStruct((rows, cols), F32)] * (4 * nw),
        in_specs=([parts_spec] * (per - 3) + [_rows(tr, cols)] * 3) * nw + [_ANY] * len(after),
        out_specs=[_rows(tr, cols)] * (4 * nw),
        compiler_params=_params("parallel"),
    )(*[a for group in groups for a in group], *after)
    return [res[4 * i:4 * i + 4] for i in range(nw)]


def _mesh_position():
    return lax.axis_index("x"), lax.axis_index("y"), lax.axis_index("c")


def _flip(pos, rel):
    x, y, c = pos
    return (1 - x if rel & 4 else x, 1 - y if rel & 2 else y, 1 - c if rel & 1 else c)


def _index(pos):
    return 4 * pos[0] + 2 * pos[1] + pos[2]


_ANY = pl.BlockSpec(memory_space=pl.ANY)
FLAT_ROWS = 32


def _dma_sems(n):
    return [pltpu.SemaphoreType.DMA((n, N_DEV - 1)), pltpu.SemaphoreType.DMA((n, N_DEV - 1)), pltpu.SemaphoreType.DMA((n,))]


def _block_of(ref, idx, rows, flat):
    if flat:
        return ref.at[pl.ds(pl.multiple_of(idx * rows, FLAT_ROWS), rows), :]
    return ref.at[idx]


class _Gather:
    chips = (4, 2, 6)
    rels = frozenset((1, 4, 2, 6))

    def __init__(self, shards):
        self.inputs = list(shards)
        self.flat = [s.shape[0] % FLAT_ROWS == 0 for s in shards]
        self.out_shape = [
            jax.ShapeDtypeStruct((N_DEV * s.shape[0], s.shape[1]) if f else (N_DEV,) + s.shape, s.dtype)
            for s, f in zip(shards, self.flat)]
        self.sems = _dma_sems(len(shards))

    def _copy(self, ins, outs, sems, i, k, block, to, own=False):
        dst = _block_of(outs[i], _index(block), self.inputs[i].shape[0], self.flat[i])
        return pltpu.make_async_remote_copy(
            src_ref=ins[i] if own else dst, dst_ref=dst, send_sem=sems[0].at[i, k], recv_sem=sems[1].at[i, k],
            device_id=to, device_id_type=MESH)

    def _local(self, ins, outs, sems, i, me):
        dst = _block_of(outs[i], _index(me), self.inputs[i].shape[0], self.flat[i])
        return pltpu.make_async_copy(ins[i], dst, sems[2].at[i])

    def _first(self, ins, outs, sems, i, me):
        cps = [self._copy(ins, outs, sems, i, 0, me, _flip(me, 1), own=True)]
        cps += [self._copy(ins, outs, sems, i, 1 + j, me, _flip(me, rel), own=True) for j, rel in enumerate(self.chips)]
        return cps

    def _passed(self, ins, outs, sems, i, j, me):
        return self._copy(ins, outs, sems, i, 4 + j, _flip(me, self.chips[j]), _flip(me, 1))

    def before(self, ins, outs, sems):
        n = len(self.inputs)
        me = _mesh_position()

        @pl.when(pl.program_id(0) == 0)
        def _():
            for i in range(n):
                self._local(ins, outs, sems, i, me).start()
                for cp in self._first(ins, outs, sems, i, me):
                    cp.start()

        @pl.when(pl.program_id(0) == pl.num_programs(0) - 1)
        def _():
            for j, rel in enumerate(self.chips):
                for i in range(n):
                    self._copy(ins, outs, sems, i, 1 + j, _flip(me, rel), me).wait_recv()
                    self._passed(ins, outs, sems, i, j, me).start()

    def after(self, ins, outs, sems):
        n = len(self.inputs)
        me = _mesh_position()
        sibling = _flip(me, 1)

        @pl.when(pl.program_id(0) == pl.num_programs(0) - 1)
        def _():
            for i in range(n):
                self._copy(ins, outs, sems, i, 0, sibling, me).wait_recv()
                for j, rel in enumerate(self.chips):
                    self._copy(ins, outs, sems, i, 4 + j, _flip(sibling, rel), me).wait_recv()
            for i in range(n):
                for cp in self._first(ins, outs, sems, i, me):
                    cp.wait_send()
                for j in range(len(self.chips)):
                    self._passed(ins, outs, sems, i, j, me).wait_send()
                self._local(ins, outs, sems, i, me).wait()


N_CHIPS = N_DEV // 2


def _chip(pos):
    return 2 * pos[0] + pos[1]


class _PairSwap:
    rels = frozenset((1,))

    def __init__(self, arrays):
        self.inputs = list(arrays)
        self.rows = [a.shape[0] // N_DEV for a in arrays]
        for r in self.rows:
            assert r % FLAT_ROWS == 0, r
        self.out_shape = [jax.ShapeDtypeStruct((N_CHIPS, r, a.shape[1]), a.dtype) for a, r in zip(arrays, self.rows)]
        n = len(arrays)
        self.sems = [pltpu.SemaphoreType.DMA((n, N_CHIPS)), pltpu.SemaphoreType.DMA((n, N_CHIPS))]

    def _copy(self, ins, outs, sems, i, j, me):
        sibling = _flip(me, 1)
        return pltpu.make_async_remote_copy(
            src_ref=_block_of(ins[i], 2 * j + sibling[2], self.rows[i], True), dst_ref=outs[i].at[j],
            send_sem=sems[0].at[i, j], recv_sem=sems[1].at[i, j], device_id=sibling, device_id_type=MESH)

    def before(self, ins, outs, sems):
        me = _mesh_position()

        @pl.when(pl.program_id(0) == 0)
        def _():
            for i in range(len(self.inputs)):
                for j in range(N_CHIPS):
                    self._copy(ins, outs, sems, i, j, me).start()

    def after(self, ins, outs, sems):
        me = _mesh_position()

        @pl.when(pl.program_id(0) == pl.num_programs(0) - 1)
        def _():
            for i in range(len(self.inputs)):
                for j in range(N_CHIPS):
                    self._copy(ins, outs, sems, i, j, me).wait()


class _ChipExchange:
    chips = (4, 2, 6)
    rels = frozenset(chips)

    def __init__(self, arrays, after=()):
        self.arrays = len(arrays)
        self.inputs = list(arrays) + list(after)
        self.out_shape = [jax.ShapeDtypeStruct(a.shape, a.dtype) for a in arrays]
        n = len(arrays)
        self.sems = [pltpu.SemaphoreType.DMA((n, 3)), pltpu.SemaphoreType.DMA((n, 3)), pltpu.SemaphoreType.DMA((n,))]

    def _send(self, ins, outs, sems, i, k, me):
        peer = _flip(me, self.chips[k])
        return pltpu.make_async_remote_copy(
            src_ref=ins[i].at[_chip(peer)], dst_ref=outs[i].at[_chip(me)], send_sem=sems[0].at[i, k],
            recv_sem=sems[1].at[i, k], device_id=peer, device_id_type=MESH)

    def _arrival(self, ins, outs, sems, i, k, me):
        peer = _flip(me, self.chips[k])
        return pltpu.make_async_remote_copy(
            src_ref=ins[i].at[_chip(me)], dst_ref=outs[i].at[_chip(peer)], send_sem=sems[0].at[i, k],
            recv_sem=sems[1].at[i, k], device_id=peer, device_id_type=MESH)

    def _local(self, ins, outs, sems, i, me):
        return pltpu.make_async_copy(ins[i].at[_chip(me)], outs[i].at[_chip(me)], sems[2].at[i])

    def before(self, ins, outs, sems):
        me = _mesh_position()

        @pl.when(pl.program_id(0) == 0)
        def _():
            for i in range(self.arrays):
                self._local(ins, outs, sems, i, me).start()
                for k in range(len(self.chips)):
                    self._send(ins, outs, sems, i, k, me).start()

    def after(self, ins, outs, sems):
        me = _mesh_position()

        @pl.when(pl.program_id(0) == pl.num_programs(0) - 1)
        def _():
            for i in range(self.arrays):
                for k in range(len(self.chips)):
                    self._arrival(ins, outs, sems, i, k, me).wait_recv()
            for i in range(self.arrays):
                for k in range(len(self.chips)):
                    self._send(ins, outs, sems, i, k, me).wait_send()
                self._local(ins, outs, sems, i, me).wait()


_HBM = pl.BlockSpec(memory_space=pltpu.HBM)
_SEM = pl.BlockSpec(memory_space=pltpu.SEMAPHORE)
_DATAFLOW = pltpu.SideEffectType.DATAFLOW_SIDE_EFFECTING
_CHIP_RELS = (4, 2, 6)
_SPLIT_COLLECTIVE_ID = 3


def _chips_copy(src_ref, dst_ref, send_sems, recv_sems, k, me):
    peer = _flip(me, _CHIP_RELS[k])
    return pltpu.make_async_remote_copy(
        src_ref=src_ref.at[_chip(peer)], dst_ref=dst_ref.at[_chip(me)], send_sem=send_sems.at[k],
        recv_sem=recv_sems.at[k], device_id=peer, device_id_type=MESH)


def _chips_start(pairs, name):
    def body(p_ref, land_ref, send_sems, recv_sems, p_thru, land_thru, token):
        me = _mesh_position()
        sem = pltpu.get_barrier_semaphore()
        for rel in _CHIP_RELS:
            pl.semaphore_signal(sem, inc=1, device_id=_flip(me, rel), device_id_type=MESH)
        pl.semaphore_wait(sem, len(_CHIP_RELS))
        for k in range(len(_CHIP_RELS)):
            _chips_copy(p_ref, land_ref, send_sems, recv_sems, k, me).start()
        token[...] = jnp.zeros_like(token)

    n = len(_CHIP_RELS)
    return pl.pallas_call(
        body, name="chips_start_" + name,
        out_shape=(pltpu.SemaphoreType.DMA((n,)), pltpu.SemaphoreType.DMA((n,)), pltpu.HBM(pairs.shape, pairs.dtype),
                   pltpu.HBM(pairs.shape, pairs.dtype), jax.ShapeDtypeStruct((SUBLANES, LANES), F32)),
        in_specs=(_HBM, _HBM), out_specs=(_SEM, _SEM, _HBM, _HBM, pl.BlockSpec(memory_space=pltpu.VMEM)),
        input_output_aliases={0: 2, 1: 3},
        compiler_params=pltpu.CompilerParams(has_side_effects=_DATAFLOW, collective_id=_SPLIT_COLLECTIVE_ID),
    )(pltpu.with_memory_space_constraint(pairs, pltpu.HBM),
      pltpu.with_memory_space_constraint(lax.empty(pairs.shape, pairs.dtype), pltpu.HBM))


def _chips_wait(send_sems, recv_sems, p_thru, land_thru, after, name):
    def body(p_ref, land_ref, send_sems, recv_sems, *rest):
        me = _mesh_position()
        for k in range(len(_CHIP_RELS)):
            copy = _chips_copy(p_ref, land_ref, send_sems, recv_sems, k, me)
            copy.wait_send()
            copy.wait_recv()

    return pl.pallas_call(
        body, name="chips_wait_" + name,
        out_shape=(pltpu.HBM(p_thru.shape, p_thru.dtype), pltpu.HBM(p_thru.shape, p_thru.dtype)),
        in_specs=(_HBM, _HBM, _SEM, _SEM) + (_ANY,) * len(after), out_specs=(_HBM, _HBM),
        input_output_aliases={0: 0, 1: 1}, compiler_params=pltpu.CompilerParams(has_side_effects=_DATAFLOW),
    )(p_thru, land_thru, send_sems, recv_sems, *after)


def _pair_add(grads, halves, name):
    n = len(grads)

    def body(*refs):
        g_refs, h_refs, o_refs = refs[:n], refs[n:2 * n], refs[2 * n:]
        c = lax.axis_index("c")
        for i in range(n):
            r = h_refs[i].shape[1]
            for j in range(N_CHIPS):
                own = g_refs[i][pl.ds(pl.multiple_of((2 * j + c) * r, FLAT_ROWS), r), :]
                o_refs[i][j] = (own.astype(F32) + h_refs[i][j].astype(F32)).astype(o_refs[i].dtype)

    return pl.pallas_call(
        body, name=name, out_shape=[jax.ShapeDtypeStruct(h.shape, h.dtype) for h in halves],
        compiler_params=pltpu.CompilerParams(vmem_limit_bytes=VMEM_LIMIT),
    )(*grads, *halves)


class _Both:
    def __init__(self, first, second):
        self.jobs = (first, second)
        self.rels = first.rels | second.rels
        self.inputs = first.inputs + second.inputs
        self.out_shape = first.out_shape + second.out_shape
        self.sems = first.sems + second.sems

    def _each(self, ins, outs, sems):
        a = self.jobs[0]
        i, o, s = len(a.inputs), len(a.out_shape), len(a.sems)
        return ((a, ins[:i], outs[:o], sems[:s]), (self.jobs[1], ins[i:], outs[o:], sems[s:]))

    def before(self, ins, outs, sems):
        for job, i, o, s in self._each(ins, outs, sems):
            job.before(i, o, s)

    def after(self, ins, outs, sems):
        for job, i, o, s in self._each(ins, outs, sems):
            job.after(i, o, s)


_COLLECTIVE_IDS = {(1,): 0, (2, 4, 6): 1, (1, 2, 4, 6): 2}


def _entry_barrier(rels):
    @pl.when(pl.program_id(0) == 0)
    def _():
        me = _mesh_position()
        sem = pltpu.get_barrier_semaphore()
        for rel in rels:
            pl.semaphore_signal(sem, inc=1, device_id=_flip(me, rel), device_id_type=MESH)
        pl.semaphore_wait(sem, len(rels))


def _run(body, comm, *, semantics, out_shape, in_specs, out_specs, scratch_shapes=(), **kw):
    if comm is None:
        return pl.pallas_call(body, out_shape=out_shape, in_specs=in_specs, out_specs=out_specs,
                              scratch_shapes=list(scratch_shapes), compiler_params=_params(semantics), **kw)
    single = not isinstance(out_shape, (list, tuple))
    outs = [out_shape] if single else list(out_shape)
    ospecs = [out_specs] if single else list(out_specs)
    counts = [len(in_specs), len(comm.inputs), len(outs), len(comm.out_shape), len(scratch_shapes), len(comm.sems)]
    rels = tuple(sorted(comm.rels))

    def carrying(*refs):
        groups, pos = [], 0
        for c in counts:
            groups.append(refs[pos:pos + c])
            pos += c
        main_in, comm_in, main_out, comm_out, main_scratch, comm_sems = groups
        _entry_barrier(rels)
        comm.before(comm_in, comm_out, comm_sems)
        body(*main_in, *main_out, *main_scratch)
        comm.after(comm_in, comm_out, comm_sems)

    call = pl.pallas_call(
        carrying, out_shape=outs + list(comm.out_shape), in_specs=list(in_specs) + [_ANY] * len(comm.inputs),
        out_specs=ospecs + [_ANY] * len(comm.out_shape), scratch_shapes=list(scratch_shapes) + list(comm.sems),
        compiler_params=pltpu.CompilerParams(dimension_semantics=("arbitrary",), vmem_limit_bytes=VMEM_LIMIT,
                                             collective_id=_COLLECTIVE_IDS[rels]), **kw)

    def apply(*args):
        res = call(*args, *comm.inputs)
        main = res[:len(outs)]
        return (main[0] if single else list(main)), list(res[len(outs):])

    return apply


def _alone(comm, name):
    return _run(lambda: None, comm, semantics="arbitrary", name=name, grid=(1,), out_shape=[], in_specs=[], out_specs=[])()[1]


SHARDED = {"w_in": 1, "w_glu": 0, "conv_w": 1, "w_a_out": 1, "w_b_out": 0, "w_o": 0, "w_ffn_gate": 1, "w_ffn_up": 1,
           "w_ffn_down": 0, "w_ple_gate": 0, "w_ple": 1}
TRANSPOSED = ("w_in", "w_a_out", "w_ffn_gate", "w_ffn_up", "w_ple")
LONG_AXIS_MINOR = ("w_in", "w_ffn_gate", "w_ffn_up")
NARROW_LAST = ("s5_b_re", "s5_b_im", "s5_d")
ADAMW_GROUPS = (("w_in",), ("w_glu",), ("conv_w",), ("w_a_out",), ("w_b_out", "w_o", "w_ple_gate"),
                ("w_ffn_gate", "w_ffn_up"), ("w_ffn_down",), ("w_ple",))
SMALL = ["g_mix", "b_in", "lam_re", "lam_im", "log_dt", "s5_b_re", "s5_b_im", "s5_c_re", "s5_c_im", "s5_d", "b_glu",
         "conv_b", "w_r", "b_r", "w_i", "b_i", "lru_lambda", "g_ffn", "g_ple_gate", "b_ple_gate", "g_ple", "g_final"]
WEIGHTS = ["g_mix", "w_in", "b_in", "lam_re", "lam_im", "log_dt", "s5_b_re", "s5_b_im", "s5_c_re", "s5_c_im", "s5_d",
           "w_glu", "b_glu", "conv_w", "conv_b", "w_r", "b_r", "w_i", "b_i", "lru_lambda", "w_a_out", "w_b_out", "w_o",
           "g_ffn", "w_ffn_gate", "w_ffn_up", "w_ffn_down", "g_ple_gate", "w_ple_gate", "b_ple_gate", "w_ple", "g_ple",
           "g_final"]


def _join_columns(gathered):
    nb, r, c = gathered.shape
    return jnp.transpose(gathered, (1, 0, 2)).reshape(r, nb * c)


def _disc_scalars(lr, li, ldt):
    dt = jnp.exp(ldt)
    mag = jnp.exp(lr * dt)
    ar = mag * jnp.cos(li * dt)
    ai = mag * jnp.sin(li * dt)
    den = lr * lr + li * li
    nr = ar - 1.0
    fr = (nr * lr + ai * li) / den
    fi = (ai * lr - nr * li) / den
    return ar, ai, fr, fi


def _disc_cols(lr, li, ldt, b_re, b_im):
    ar, ai, fr, fi = _disc_scalars(lr, li, ldt)
    return ar, ai, fr * b_re - fi * b_im, fr * b_im + fi * b_re


def _local_step(x, p, target, src, small, disc, distributed=True):
    full = {} if distributed else dict(src)
    gw, halves, pairs, got = {}, {}, {}, {}

    def gather(keys):
        return (_Gather([src[k] for k in keys]), keys, full) if distributed else None

    def swap(keys):
        return (_PairSwap([gw[k] for k in keys]), keys, halves) if distributed else None

    def chips(keys):
        return (_ChipExchange([pairs[k] for k in keys]), keys, got) if distributed else None

    def add_pairs(keys):
        if distributed:
            pairs.update(zip(keys, _pair_add([gw[k] for k in keys], [halves[k] for k in keys], "pair_add_" + keys[0])))

    def carry(fn, *args, jobs=()):
        jobs = [j for j in jobs if j is not None]
        if not jobs:
            return fn(*args)
        comm = jobs[0][0]
        for j in jobs[1:]:
            comm = _Both(comm, j[0])
        res, extra = fn(*args, comm=comm)
        for job, keys, sink in jobs:
            sink.update(zip(keys, extra[:len(job.out_shape)]))
            extra = extra[len(job.out_shape):]
        return res

    d = x.shape[1]
    g, n, pch = small["s5_b_re"].shape
    sa, lw = g * pch, small["lru_lambda"].shape[-1]
    widths = [sa, lw, d, d]
    row = lambda v: v.reshape(1, -1)

    hd = small["w_r"].shape[-1]
    ar_row, ai_row, bbr_blk, bbi_blk, cre_blk, cimn_blk, wr_blk, wi_blk = carry(
        _prep, *disc["rows"], *disc["cols"], disc["b_re"], disc["b_im"], small["s5_c_re"].reshape(sa, n),
        small["s5_c_im"].reshape(sa, n), small["w_r"].reshape(lw, hd), small["w_i"].reshape(lw, hd),
        jobs=[gather(["w_in"])])
    d_row = row(small["s5_d"])
    u_a, u_b, za, zb = carry(_inproj_fwd, x, row(small["g_mix"]), full["w_in"], row(small["b_in"]), widths,
                             jobs=[gather(["w_glu", "conv_w", "w_a_out", "w_b_out"])])
    conv_w = _join_columns(full["conv_w"]) if distributed else full["conv_w"]
    branches = [_s5_fwd(u_a, bbr_blk, bbi_blk, ar_row, ai_row, cre_blk, cimn_blk, d_row, full["w_glu"],
                        row(small["b_glu"])),
                _lru_fwd(u_b, conv_w, row(small["conv_b"]), wr_blk, row(small["b_r"]), wi_blk, row(small["b_i"]),
                         row(small["lru_lambda"]))]
    (sr, si, y, y_a), (xc, h, hprev) = carry(_stages, branches, "branches_fwd",
                                             jobs=[gather(["w_ffn_gate", "w_ffn_up", "w_o"])])
    x1, merged, ma, mb, fg, fu = carry(
        _merge_ffn_up_fwd, y_a, h, za, zb, x, full["w_a_out"], full["w_b_out"], full["w_o"], row(small["g_ffn"]),
        full["w_ffn_gate"], full["w_ffn_up"], jobs=[gather(["w_ffn_down"])])
    x2 = carry(_ffn_down_fwd, fg, fu, x1, full["w_ffn_down"], jobs=[gather(["w_ple_gate", "w_ple"])])

    dx2, loss_blk, gw["w_ple_gate"], gw["w_ple"], vec_tail = _tail_fwd_bwd(
        x2, p, target, row(small["g_ple_gate"]), full["w_ple_gate"], row(small["b_ple_gate"]), full["w_ple"],
        row(small["g_ple"]), row(small["g_final"]))
    dfg, dfu, act = carry(_ffn_bwd_a, dx2, fg, fu, full["w_ffn_down"], jobs=[swap(["w_ple_gate", "w_ple"])])
    tn_d = min(d, 512)
    gw["w_ffn_down"] = _matmul_tn(act, dx2, tn_d, "dw_ffn_down", BF16)
    add_pairs(["w_ple_gate", "w_ple"])
    dx1, h2, vec_ffn = carry(_ffn_bwd_b, dfg, dfu, x1, dx2, row(small["g_ffn"]), full["w_ffn_gate"], full["w_ffn_up"],
                             jobs=[chips(["w_ple_gate", "w_ple"]), swap(["w_ffn_down"])])
    gw["w_ffn_gate"] = _matmul_tn(dfg, h2, tn_d, "dw_ffn_gate", BF16)
    gw["w_ffn_up"] = _matmul_tn(dfu, h2, tn_d, "dw_ffn_up", BF16)
    add_pairs(["w_ffn_down"])
    dza, dzb, dya, dyb, gw["w_o"], gw["w_a_out"], gw["w_b_out"] = carry(
        _merge_bwd, dx1, merged, ma, mb, za, zb, y_a, h, full["w_o"], full["w_a_out"], full["w_b_out"],
        jobs=[chips(["w_ffn_down"]), swap(["w_ffn_gate", "w_ffn_up"])])
    add_pairs(["w_ffn_gate", "w_ffn_up"])
    own, gate_job, up_flight = {}, chips(["w_ffn_gate"]), None
    if distributed:
        up_flight = _chips_start(pairs["w_ffn_up"], "w_ffn_up")
        gate_job = (_ChipExchange([pairs["w_ffn_gate"]], after=up_flight[4:]), ["w_ffn_gate"], got)
    du_b, dw_r, dw_i, vec_lru = carry(
        _lru_bwd, dyb, xc, hprev, u_b, conv_w, wr_blk, row(small["b_r"]), wi_blk, row(small["b_i"]),
        row(small["lru_lambda"]), hd, jobs=[gate_job, swap(["w_o", "w_a_out", "w_b_out"])])
    add_pairs(["w_o", "w_a_out", "w_b_out"])
    du_a, lam_r, lam_i, dy16, gw["w_glu"], vec_sa, vec_gn = carry(
        _s5_bwd, dya, y, sr, si, u_a, full["w_glu"], row(small["b_glu"]), cre_blk, cimn_blk, bbr_blk, bbi_blk, ar_row,
        ai_row, d_row, jobs=[chips(["w_o", "w_a_out", "w_b_out"])])
    if distributed:
        own["w_ffn_up"], got["w_ffn_up"] = _chips_wait(*up_flight[:4], [du_a], "w_ffn_up")
        own["w_ffn_gate"] = pairs["w_ffn_gate"]
    smalls = {"vec_tail": vec_tail, "vec_ffn": vec_ffn, "vec_lru": vec_lru, "vec_sa": vec_sa, "vec_gn": vec_gn,
              "dw_r": dw_r, "dw_i": dw_i, "loss": loss_blk}
    everyones = {}

    def gather_smalls(keys):
        return (_Gather([smalls[k] for k in keys]), keys, everyones) if distributed else None

    smalls["dbb_re"], smalls["dbb_im"], smalls["dc_re"], smalls["dc_imn"] = carry(
        _s5_param_grads, lam_r, lam_i, sr, si, u_a, dy16, pch, n, jobs=[swap(["w_glu"]), gather_smalls(list(smalls))])
    add_pairs(["w_glu"])
    dz = [du_a, du_b, dza, dzb]
    grad_x, h0, smalls["vec_mix"], smalls["vec_bin"] = _inproj_bwd(
        dz, x, dx1, row(small["g_mix"]), full["w_in"])
    shapes = {k: a.shape for k, a in smalls.items()}
    gw["w_in"] = carry(_dw_from_parts, dz, h0, tn_d, "dw_in",
                       jobs=[chips(["w_glu"]),
                             gather_smalls(["dbb_re", "dbb_im", "dc_re", "dc_imn", "vec_mix", "vec_bin"])])
    smalls.update(everyones)
    if distributed:
        got["w_in"] = gw["w_in"]
        gw = got
    return grad_x, gw, smalls, shapes, own


def _disc_inputs(small):
    g, n, pch = small["s5_b_re"].shape
    srcs = (small["lam_re"], small["lam_im"], jnp.repeat(small["log_dt"], n))
    return {"rows": [a.reshape(1, g * n) for a in srcs], "cols": [a.reshape(g * n, 1) for a in srcs],
            "b_re": small["s5_b_re"].reshape(g * n, pch), "b_im": small["s5_b_im"].reshape(g * n, pch)}


def kernel(x, p, g_mix, w_in, b_in, lam_re, lam_im, log_dt, s5_b_re, s5_b_im, s5_c_re, s5_c_im, s5_d, w_glu, b_glu, conv_w, conv_b, w_r, b_r, w_i, b_i, lru_lambda, w_a_out, w_b_out, w_o, g_ffn, w_ffn_gate, w_ffn_up, w_ffn_down, g_ple_gate, w_ple_gate, b_ple_gate, w_ple, g_ple, g_final, loss_target, m_g_mix, m_w_in, m_b_in, m_lam_re, m_lam_im, m_log_dt, m_s5_b_re, m_s5_b_im, m_s5_c_re, m_s5_c_im, m_s5_d, m_w_glu, m_b_glu, m_conv_w, m_conv_b, m_w_r, m_b_r, m_w_i, m_b_i, m_lru_lambda, m_w_a_out, m_w_b_out, m_w_o, m_g_ffn, m_w_ffn_gate, m_w_ffn_up, m_w_ffn_down, m_g_ple_gate, m_w_ple_gate, m_b_ple_gate, m_w_ple, m_g_ple, m_g_final, v_g_mix, v_w_in, v_b_in, v_lam_re, v_lam_im, v_log_dt, v_s5_b_re, v_s5_b_im, v_s5_c_re, v_s5_c_im, v_s5_d, v_w_glu, v_b_glu, v_conv_w, v_conv_b, v_w_r, v_b_r, v_w_i, v_b_i, v_lru_lambda, v_w_a_out, v_w_b_out, v_w_o, v_g_ffn, v_w_ffn_gate, v_w_ffn_up, v_w_ffn_down, v_g_ple_gate, v_w_ple_gate, v_b_ple_gate, v_w_ple, v_g_ple, v_g_final):
    given = dict(locals())
    wts = {k: given[k] for k in WEIGHTS}
    moms = {k: given["m_" + k] for k in WEIGHTS}
    vels = {k: given["v_" + k] for k in WEIGHTS}

    def drop_depth(k, a):
        return a if k == "g_final" else a[0]

    small = {k: drop_depth(k, wts[k]) for k in SMALL}
    shard = {k: wts[k][0] for k in SHARDED}
    names = list(SHARDED)

    def wire(k):
        if k == "conv_w":
            return shard[k]
        return (shard[k].T if k in TRANSPOSED else shard[k]).astype(BF16)

    disc = _disc_inputs(small)
    grad_x, parts, smalls, shapes, own = _local_step(x[0], p[0, 0], loss_target[0], {k: wire(k) for k in names}, small,
                                                     disc)

    (pair_in,) = _pair_add([parts["w_in"]], _alone(_PairSwap([parts["w_in"]]), "swap_last"), "pair_add_w_in")
    send_sems, recv_sems, pair_in, landing, token = _chips_start(pair_in, "w_in")
    ordered = (token,)

    g_small = _small_reduce(smalls, shapes, *disc["cols"], disc["b_re"], disc["b_im"], small["s5_b_re"].shape[0],
                            after=ordered)
    loss = g_small.pop("loss")[0, 0]
    cols = shard["conv_w"].shape[1]
    mine = _index((lax.axis_index("x"), lax.axis_index("y"), lax.axis_index("c")))
    parts["conv_w"] = lax.dynamic_slice_in_dim(g_small.pop("conv_w"), mine * cols, cols, axis=1)[None]

    def view(k, a):
        a = a.reshape((1, -1) if k == "g_final" else wts[k].shape)
        return jnp.swapaxes(a, -1, -2) if k in NARROW_LAST else a

    def unview(k, a):
        return (jnp.swapaxes(a, -1, -2) if k in NARROW_LAST else a).reshape(wts[k].shape)

    slots = _adamw_small([view(k, wts[k]) for k in SMALL], [view(k, g_small[k]) for k in SMALL],
                         [view(k, moms[k]) for k in SMALL], [view(k, vels[k]) for k in SMALL])
    small_out = [dict(zip(SMALL, [unview(k, a) for k, a in zip(SMALL, slot)])) for slot in slots]

    big_out = {}
    between = [slots[0][0]]
    for group in sorted(ADAMW_GROUPS, key=lambda g: g[0] == "w_in"):
        flip = group[0] in LONG_AXIS_MINOR
        look = (lambda a: a.T) if flip else (lambda a: a)
        last = group[0] == "w_in"
        if last:
            own["w_in"], parts["w_in"] = _chips_wait(send_sems, recv_sems, pair_in, landing, between, "w_in")
        res = _adamw([parts[k] for k in group], [look(shard[k]) for k in group], [look(moms[k][0]) for k in group],
                     [look(vels[k][0]) for k in group], "adamw_" + group[0],
                     transposed=group[0] in TRANSPOSED and not flip,
                     own=[own[k] for k in group] if group[0] in own else None, after=() if last else ordered)
        between.append(res[0][0])
        for k, outs4 in zip(group, res):
            big_out[k] = [look(a) for a in outs4]

    outs = [loss, grad_x[None]]
    for slot in range(4):
        for k in WEIGHTS:
            if k in SHARDED:
                outs.append(big_out[k][slot][None])
            else:
                outs.append(small_out[slot][k])
    return tuple(outs)
```

```python
import math

import jax
import jax.numpy as jnp
from jax import lax
from jax.experimental import pallas as pl
from jax.experimental.pallas import tpu as pltpu

F32 = jnp.float32
BF16 = jnp.bfloat16

EPS = 1e-6
LRU_C = 8.0
CONV_WIDTH = 4
ADAM_LR = 0.001
ADAM_B1 = 0.9
ADAM_B2 = 0.999
ADAM_EPS = 1e-08
ADAM_WD = 0.01
ADAM_STEP = 10

N_DEV = 8
MESH = pl.DeviceIdType.MESH
SUBLANES = 8
LANES = 128
VMEM_LIMIT = 56 * 1024 * 1024
TOKEN_TILE = 256
TIME_CHUNK = 256
S5_SLAB = 128
LRU_SLAB = 256


def _dot(a, b):
    return jnp.dot(a.astype(BF16), b.astype(BF16), preferred_element_type=F32)


def _dot_nt(a, b):
    return lax.dot_general(a.astype(BF16), b.astype(BF16), (((1,), (1,)), ((), ())), preferred_element_type=F32)


def _dot_tn(a, b):
    return lax.dot_general(a.astype(BF16), b.astype(BF16), (((0,), (0,)), ((), ())), preferred_element_type=F32)


def _sigmoid(x):
    return jax.nn.sigmoid(x)


def _rms_stats(x):
    r = lax.rsqrt(jnp.mean(x * x, axis=-1, keepdims=True) + EPS)
    return x * r, r


def _rms_bwd(dy, xhat, r, g):
    dxn = dy * g
    dx = r * (dxn - xhat * jnp.mean(dxn * xhat, axis=-1, keepdims=True))
    return dx, dy * xhat


def _rowsum(v):
    return jnp.sum(v, axis=0, keepdims=True)


def _expm1(x):
    u = jnp.exp(x)
    um1 = u - 1.0
    safe = jnp.where(um1 == 0.0, 1.0, jnp.log(u))
    return jnp.where(um1 == 0.0, x, um1 * x / safe)


def _softplus(x):
    e = jnp.exp(-jnp.abs(x))
    u = 1.0 + e
    um1 = u - 1.0
    safe = jnp.where(um1 == 0.0, 1.0, um1)
    log1p_e = jnp.where(um1 == 0.0, e, jnp.log(u) * e / safe)
    return jnp.maximum(x, 0.0) + log1p_e


_GELU_K = math.sqrt(2.0 / math.pi)
_GELU_C = 0.044715


def _gelu(x):
    return 0.5 * x * (1.0 + jnp.tanh(_GELU_K * (x + _GELU_C * x * x * x)))


def _gelu_grad(x):
    th = jnp.tanh(_GELU_K * (x + _GELU_C * x * x * x))
    return 0.5 * (1.0 + th) + 0.5 * x * (1.0 - th * th) * _GELU_K * (1.0 + 3.0 * _GELU_C * x * x)


def _params(*sem):
    return pltpu.CompilerParams(dimension_semantics=sem, vmem_limit_bytes=VMEM_LIMIT)


def _rows(tm, n):
    return pl.BlockSpec((tm, n), lambda i: (i, 0))


def _rows_rev(tm, n, steps):
    return pl.BlockSpec((tm, n), lambda i: (steps - 1 - i, 0))


def _whole(shape):
    nd = len(shape)
    return pl.BlockSpec(shape, lambda i: (0,) * nd, pipeline_mode=pl.Buffered(1))


def _acc(shape):
    nd = len(shape)
    return pl.BlockSpec(shape, lambda i: (0,) * nd)


def _zero_on_first(*refs):
    @pl.when(pl.program_id(0) == 0)
    def _():
        for r in refs:
            r[...] = jnp.zeros_like(r)


def _inproj_fwd(x, g_mix, w_in_t, b_in, widths, comm=None):
    t, d = x.shape
    n = w_in_t.shape[0]
    tm = min(TOKEN_TILE, t)
    offs = [sum(widths[:i]) for i in range(len(widths) + 1)]

    def body(x_ref, g_ref, w_ref, b_ref, *outs):
        xhat, _ = _rms_stats(x_ref[...])
        h = (xhat * g_ref[...]).astype(BF16)
        for k, o_ref in enumerate(outs):
            lo, hi = offs[k], offs[k + 1]
            o_ref[...] = _dot_nt(h, w_ref[lo:hi, :]) + b_ref[:, lo:hi]

    return _run(
        body, comm, name="inproj_fwd", grid=(t // tm,),
        out_shape=[jax.ShapeDtypeStruct((t, w), F32) for w in widths],
        in_specs=[_rows(tm, d), _whole((1, d)), _whole((n, d)), _whole((1, n))],
        out_specs=[_rows(tm, w) for w in widths],
        semantics="parallel",
    )(x, g_mix, w_in_t, b_in)


def _s5_fwd(u, bbr_blk, bbi_blk, ar, ai, cre_blk, cimn_blk, d_skip, w_glu, b_glu):
    t, sa = u.shape
    ns, _, sw = bbr_blk.shape
    gn = ns * sw
    tc = min(TIME_CHUNK, t)

    def body(u_ref, bbr_ref, bbi_ref, ar_ref, ai_ref, cre_ref, cim_ref, d_ref, wg_ref, bg_ref,
             sr_ref, si_ref, y_ref, ya_ref, cr_s, ci_s, sr_s, si_s):
        _zero_on_first(cr_s, ci_s)
        uv = u_ref[...]
        ub = uv.astype(BF16)
        for m in range(ns):
            um = ub[:, m * S5_SLAB:(m + 1) * S5_SLAB]
            sr_s[:, m * sw:(m + 1) * sw] = _dot(um, bbr_ref[m])
            si_s[:, m * sw:(m + 1) * sw] = _dot(um, bbi_ref[m])
        a_r = ar_ref[...]
        a_i = ai_ref[...]

        def step(row, carry):
            c_r, c_i = carry
            at = pl.ds(row, 1)
            n_r = a_r * c_r - a_i * c_i + sr_s[at, :]
            n_i = a_r * c_i + a_i * c_r + si_s[at, :]
            sr_s[at, :] = n_r
            si_s[at, :] = n_i
            return n_r, n_i

        c_r, c_i = lax.fori_loop(0, tc, step, (cr_s[0:1, :], ci_s[0:1, :]), unroll=8)
        cr_s[0:1, :] = c_r
        ci_s[0:1, :] = c_i
        for m in range(ns):
            states, chans = slice(m * sw, (m + 1) * sw), slice(m * S5_SLAB, (m + 1) * S5_SLAB)
            s_r, s_i = sr_s[:, states].astype(BF16), si_s[:, states].astype(BF16)
            sr_ref[:, states] = s_r.astype(sr_ref.dtype)
            si_ref[:, states] = s_i.astype(si_ref.dtype)
            y_ref[:, chans] = _dot(s_r, cre_ref[m]) + _dot(s_i, cim_ref[m]) + d_ref[:, chans] * uv[:, chans]
        y = y_ref[...]
        zz = _gelu(y)
        q = _dot(zz, wg_ref[...]) + bg_ref[...]
        ya_ref[...] = zz * _sigmoid(q)

    return dict(
        body=body, steps=t // tc, args=(u, bbr_blk, bbi_blk, ar, ai, cre_blk, cimn_blk, d_skip, w_glu, b_glu),
        out_shape=[jax.ShapeDtypeStruct((t, gn), BF16), jax.ShapeDtypeStruct((t, gn), BF16),
                   jax.ShapeDtypeStruct((t, sa), F32), jax.ShapeDtypeStruct((t, sa), F32)],
        in_specs=[_rows(tc, sa), _whole(bbr_blk.shape), _whole(bbi_blk.shape), _whole((1, gn)), _whole((1, gn)),
                  _whole(cre_blk.shape), _whole(cimn_blk.shape), _whole((1, sa)), _whole((sa, sa)), _whole((1, sa))],
        out_specs=[_rows(tc, gn), _rows(tc, gn), _rows(tc, sa), _rows(tc, sa)],
        scratch=[pltpu.VMEM((SUBLANES, gn), F32), pltpu.VMEM((SUBLANES, gn), F32),
                 pltpu.VMEM((tc, gn), F32), pltpu.VMEM((tc, gn), F32)])


def _stages(stages, name, comm=None):
    counts = [[len(s[k]) for s in stages] for k in ("args", "out_shape", "scratch")]

    def body(*refs):
        groups, pos = [], 0
        for kind in counts:
            per_stage = []
            for c in kind:
                per_stage.append(refs[pos:pos + c])
                pos += c
            groups.append(per_stage)
        for s, ins, outs, scratch in zip(stages, *groups):
            s["body"](*ins, *outs, *scratch)

    res = _run(
        body, comm, name=name, grid=(stages[0]["steps"],),
        out_shape=[o for s in stages for o in s["out_shape"]], in_specs=[i for s in stages for i in s["in_specs"]],
        out_specs=[o for s in stages for o in s["out_specs"]], scratch_shapes=[c for s in stages for c in s["scratch"]],
        semantics="arbitrary",
    )(*[a for s in stages for a in s["args"]])
    extra = None
    if comm is not None:
        res, extra = res
    per_stage, pos = [], 0
    for c in counts[1]:
        per_stage.append(list(res[pos:pos + c]))
        pos += c
    return per_stage if comm is None else (per_stage, extra)


def _slab_dot(x, w_ref, transposed=False):
    dot = _dot_nt if transposed else _dot
    xb = x.astype(BF16)
    return jnp.concatenate([dot(xb[:, j * LRU_SLAB:(j + 1) * LRU_SLAB], w_ref[j]) for j in range(w_ref.shape[0])], axis=1)


def _lru_gates(xc, wr_ref, br_ref, wi_ref, bi_ref, lam_ref):
    r = _sigmoid(_slab_dot(xc, wr_ref) + br_ref[...])
    ig = _sigmoid(_slab_dot(xc, wi_ref) + bi_ref[...])
    sp = _softplus(-lam_ref[...])
    log_a = (-LRU_C * r) * sp
    return r, ig, sp, log_a


def _lru_fwd(u, conv_w, conv_b, wr_blk, b_r, wi_blk, b_i, lru_lambda):
    t, w = u.shape
    tc = min(TIME_CHUNK, t)
    halo = SUBLANES

    def body(u_ref, cw_ref, cb_ref, wr_ref, br_ref, wi_ref, bi_ref, lam_ref,
             xc_ref, h_ref, hp_ref, ext_s, a_s, carry_s):
        @pl.when(pl.program_id(0) == 0)
        def _():
            ext_s[0:halo, :] = jnp.zeros((halo, w), F32)
            carry_s[...] = jnp.zeros_like(carry_s)

        ext_s[halo:halo + tc, :] = u_ref[...]
        xc = cb_ref[...]
        for k in range(CONV_WIDTH):
            off = halo - (CONV_WIDTH - 1) + k
            xc = xc + cw_ref[k:k + 1, :] * ext_s[off:off + tc, :]
        ext_s[0:halo, :] = ext_s[tc:tc + halo, :]
        xc_ref[...] = xc
        r, ig, sp, log_a = _lru_gates(xc, wr_ref, br_ref, wi_ref, bi_ref, lam_ref)
        a_s[...] = jnp.exp(log_a)
        h_ref[...] = jnp.sqrt(-_expm1(2.0 * log_a)) * ig * xc

        def step(row, carry):
            at = pl.ds(row, 1)
            hp_ref[at, :] = carry
            nxt = a_s[at, :] * carry + h_ref[at, :]
            h_ref[at, :] = nxt
            return nxt

        carry_s[0:1, :] = lax.fori_loop(0, tc, step, carry_s[0:1, :], unroll=8)

    return dict(
        body=body, steps=t // tc, args=(u, conv_w, conv_b, wr_blk, b_r, wi_blk, b_i, lru_lambda),
        out_shape=[jax.ShapeDtypeStruct((t, w), F32)] * 3,
        in_specs=[_rows(tc, w), _whole((CONV_WIDTH, w)), _whole((1, w)), _whole(wr_blk.shape), _whole((1, w)),
                  _whole(wi_blk.shape), _whole((1, w)), _whole((1, w))],
        out_specs=[_rows(tc, w)] * 3,
        scratch=[pltpu.VMEM((halo + tc, w), F32), pltpu.VMEM((tc, w), F32), pltpu.VMEM((SUBLANES, w), F32)])


def _merge_ffn_up_fwd(y_a, h, za, zb, x, w_a_out_t, w_b_out, w_o, g_ffn, w_gate_t, w_up_t, comm=None):
    t, d = x.shape
    sa, lw = y_a.shape[1], h.shape[1]
    f = w_gate_t.shape[0]
    tm = min(TOKEN_TILE, t)

    def body(ya_ref, h_ref, za_ref, zb_ref, x_ref, wa_ref, wb_ref, wo_ref, g_ref, wg_ref, wu_ref,
             x1_ref, mg_ref, ma_ref, mb_ref, fg_ref, fu_ref):
        ma = _dot_nt(ya_ref[...], wa_ref[...])
        mb = _dot(h_ref[...], wb_ref[...])
        merged = _sigmoid(za_ref[...]) * ma + _sigmoid(zb_ref[...]) * mb
        ma_ref[...] = ma.astype(ma_ref.dtype)
        mb_ref[...] = mb.astype(mb_ref.dtype)
        mg_ref[...] = merged.astype(mg_ref.dtype)
        x1 = x_ref[...] + _dot(merged, wo_ref[...])
        x1_ref[...] = x1
        xhat, _ = _rms_stats(x1)
        h2 = (xhat * g_ref[...]).astype(BF16)
        fg_ref[...] = _dot_nt(h2, wg_ref[...]).astype(fg_ref.dtype)
        fu_ref[...] = _dot_nt(h2, wu_ref[...]).astype(fu_ref.dtype)

    return _run(
        body, comm, name="merge_ffn_up_fwd", grid=(t // tm,),
        out_shape=[jax.ShapeDtypeStruct((t, d), F32), jax.ShapeDtypeStruct((t, d), BF16),
                   jax.ShapeDtypeStruct((t, d), BF16), jax.ShapeDtypeStruct((t, d), BF16),
                   jax.ShapeDtypeStruct((t, f), BF16), jax.ShapeDtypeStruct((t, f), BF16)],
        in_specs=[_rows(tm, sa), _rows(tm, lw), _rows(tm, d), _rows(tm, d), _rows(tm, d),
                  _whole((d, sa)), _whole((lw, d)), _whole((d, d)), _whole((1, d)), _whole((f, d)), _whole((f, d))],
        out_specs=[_rows(tm, d)] * 4 + [_rows(tm, f)] * 2,
        semantics="parallel",
    )(y_a, h, za, zb, x, w_a_out_t, w_b_out, w_o, g_ffn, w_gate_t, w_up_t)


def _ffn_down_fwd(fg, fu, x1, w_down, comm=None):
    t, d = x1.shape
    f = fg.shape[1]
    tm = min(TOKEN_TILE, t)

    def body(fg_ref, fu_ref, x_ref, wd_ref, x2_ref):
        fgv = fg_ref[...].astype(F32)
        act = fgv * _sigmoid(fgv) * fu_ref[...].astype(F32)
        x2_ref[...] = x_ref[...] + _dot(act, wd_ref[...])

    return _run(
        body, comm, name="ffn_down_fwd", grid=(t // tm,),
        out_shape=jax.ShapeDtypeStruct((t, d), F32),
        in_specs=[_rows(tm, f), _rows(tm, f), _rows(tm, d), _whole((f, d))],
        out_specs=_rows(tm, d),
        semantics="parallel",
    )(fg, fu, x1, w_down)


def _store_on_last(pairs):
    @pl.when(pl.program_id(0) == pl.num_programs(0) - 1)
    def _():
        for acc, out in pairs:
            out[...] = acc[...].astype(out.dtype)


def _tail_fwd_bwd(x2, p, target, g_pg, w_pg, b_pg, w_ple_t, g_ple, g_final):
    t, d = x2.shape
    pd = p.shape[1]
    tm = min(TOKEN_TILE, t)

    def body(x2_ref, p_ref, tg_ref, gpg_ref, wpg_ref, bpg_ref, wple_ref, gple_ref, gfin_ref,
             dx2_ref, loss_ref, dwpg_out, dwple_out, vec_ref, dwpg_ref, dwple_ref):
        _zero_on_first(loss_ref, dwpg_ref, dwple_ref, vec_ref)
        x2v = x2_ref[...]
        xh2, r2 = _rms_stats(x2v)
        h3 = xh2 * gpg_ref[...]
        gp = _sigmoid(_dot(h3, wpg_ref[...]) + bpg_ref[...])
        pe = _dot_nt(p_ref[...], wple_ref[...])
        peh, r3 = _rms_stats(pe)
        e = peh * gple_ref[...]
        x3 = x2v + gp * e
        xh3, r4 = _rms_stats(x3)
        diff = xh3 * gfin_ref[...] - tg_ref[...]
        loss_ref[...] += 0.5 * jnp.sum(jnp.mean(diff * diff, axis=-1, keepdims=True))
        dy = diff * (1.0 / d)
        dx3, dgfin = _rms_bwd(dy, xh3, r4, gfin_ref[...])
        d_gp = dx3 * e
        d_e = dx3 * gp
        dpe, dgple = _rms_bwd(d_e, peh, r3, gple_ref[...])
        dwple_ref[...] += _dot_tn(dpe, p_ref[...])
        dpre = d_gp * gp * (1.0 - gp)
        dwpg_ref[...] += _dot_tn(h3, dpre)
        dh3 = _dot_nt(dpre, wpg_ref[...])
        dx2n, dgpg = _rms_bwd(dh3, xh2, r2, gpg_ref[...])
        dx2_ref[...] = dx3 + dx2n
        vec_ref[0:1, :] += _rowsum(dpre)
        vec_ref[1:2, :] += _rowsum(dgpg)
        vec_ref[2:3, :] += _rowsum(dgple)
        vec_ref[3:4, :] += _rowsum(dgfin)
        _store_on_last([(dwpg_ref, dwpg_out), (dwple_ref, dwple_out)])

    return pl.pallas_call(
        body, name="tail_fwd_bwd", grid=(t // tm,),
        out_shape=[jax.ShapeDtypeStruct((t, d), F32), jax.ShapeDtypeStruct((SUBLANES, LANES), F32),
                   jax.ShapeDtypeStruct((d, d), BF16), jax.ShapeDtypeStruct((d, pd), BF16),
                   jax.ShapeDtypeStruct((SUBLANES, d), F32)],
        in_specs=[_rows(tm, d), _rows(tm, pd), _rows(tm, d), _whole((1, d)), _whole((d, d)), _whole((1, d)),
                  _whole((d, pd)), _whole((1, d)), _whole((1, d))],
        out_specs=[_rows(tm, d), _acc((SUBLANES, LANES)), _acc((d, d)), _acc((d, pd)), _acc((SUBLANES, d))],
        scratch_shapes=[pltpu.VMEM((d, d), F32), pltpu.VMEM((d, pd), F32)],
        compiler_params=_params("arbitrary"),
    )(x2, p, target, g_pg, w_pg, b_pg, w_ple_t, g_ple, g_final)


def _ffn_bwd_a(dx2, fg, fu, w_down, comm=None):
    t, d = dx2.shape
    f = fg.shape[1]
    tm = min(TOKEN_TILE, t)

    def body(dx_ref, fg_ref, fu_ref, wd_ref, dfg_ref, dfu_ref, act_ref):
        dact = _dot_nt(dx_ref[...], wd_ref[...])
        fgv = fg_ref[...].astype(F32)
        fuv = fu_ref[...].astype(F32)
        sg = _sigmoid(fgv)
        silu = fgv * sg
        dfu_ref[...] = (dact * silu).astype(dfu_ref.dtype)
        dfg_ref[...] = (dact * fuv * (sg * (1.0 + fgv * (1.0 - sg)))).astype(dfg_ref.dtype)
        act_ref[...] = (silu * fuv).astype(act_ref.dtype)

    return _run(
        body, comm, name="ffn_bwd_a", grid=(t // tm,),
        out_shape=[jax.ShapeDtypeStruct((t, f), BF16)] * 3,
        in_specs=[_rows(tm, d), _rows(tm, f), _rows(tm, f), _whole((f, d))],
        out_specs=[_rows(tm, f)] * 3,
        semantics="parallel",
    )(dx2, fg, fu, w_down)


def _ffn_bwd_b(dfg, dfu, x1, dx2, g_ffn, w_gate_t, w_up_t, comm=None):
    t, d = x1.shape
    f = dfg.shape[1]
    tm = min(TOKEN_TILE, t)

    def body(dfg_ref, dfu_ref, x_ref, dx2_ref, g_ref, wg_ref, wu_ref, dx1_ref, h2_ref, vec_ref):
        _zero_on_first(vec_ref)
        dh2 = _dot(dfg_ref[...], wg_ref[...]) + _dot(dfu_ref[...], wu_ref[...])
        xhat, r = _rms_stats(x_ref[...])
        h2_ref[...] = (xhat * g_ref[...]).astype(h2_ref.dtype)
        dxn, dg = _rms_bwd(dh2, xhat, r, g_ref[...])
        dx1_ref[...] = dx2_ref[...] + dxn
        vec_ref[0:1, :] += _rowsum(dg)

    return _run(
        body, comm, name="ffn_bwd_b", grid=(t // tm,),
        out_shape=[jax.ShapeDtypeStruct((t, d), F32), jax.ShapeDtypeStruct((t, d), BF16),
                   jax.ShapeDtypeStruct((SUBLANES, d), F32)],
        in_specs=[_rows(tm, f), _rows(tm, f), _rows(tm, d), _rows(tm, d), _whole((1, d)), _whole((f, d)), _whole((f, d))],
        out_specs=[_rows(tm, d), _rows(tm, d), _acc((SUBLANES, d))],
        semantics="arbitrary",
    )(dfg, dfu, x1, dx2, g_ffn, w_gate_t, w_up_t)


def _matmul_tn(a, b, tn, name, dtype=F32):
    t, k = a.shape
    n = b.shape[1]

    def body(a_ref, b_ref, o_ref):
        o_ref[...] = _dot_tn(a_ref[...], b_ref[...]).astype(o_ref.dtype)

    return _run(
        body, None, name=name, grid=(n // tn,),
        out_shape=jax.ShapeDtypeStruct((k, n), dtype),
        in_specs=[_whole((t, k)), pl.BlockSpec((t, tn), lambda j: (0, j))],
        out_specs=pl.BlockSpec((k, tn), lambda j: (0, j)),
        semantics="parallel",
    )(a, b)


def _merge_bwd(dx1, merged, ma, mb, za, zb, y_a, h, w_o, w_a_out_t, w_b_out, comm=None):
    t, d = dx1.shape
    sa, lw = y_a.shape[1], h.shape[1]
    tm = min(TOKEN_TILE, t)

    def body(dx1_ref, mg_ref, ma_ref, mb_ref, za_ref, zb_ref, ya_ref, h_ref, wo_ref, wa_ref, wb_ref,
             dza_ref, dzb_ref, dya_ref, dyb_ref, dwo_out, dwa_out, dwb_out, dwo_ref, dwa_ref, dwb_ref):
        _zero_on_first(dwo_ref, dwa_ref, dwb_ref)
        dx1v = dx1_ref[...].astype(BF16)
        dmg = _dot_nt(dx1v, wo_ref[...])
        ga = _sigmoid(za_ref[...])
        gb = _sigmoid(zb_ref[...])
        dza_ref[...] = (dmg * ma_ref[...].astype(F32) * ga * (1.0 - ga)).astype(dza_ref.dtype)
        dzb_ref[...] = (dmg * mb_ref[...].astype(F32) * gb * (1.0 - gb)).astype(dzb_ref.dtype)
        dma = (dmg * ga).astype(BF16)
        dmb = (dmg * gb).astype(BF16)
        dya_ref[...] = _dot(dma, wa_ref[...])
        dyb_ref[...] = _dot_nt(dmb, wb_ref[...])
        dwo_ref[...] += _dot_tn(mg_ref[...], dx1v)
        dwa_ref[...] += _dot_tn(dma, ya_ref[...])
        dwb_ref[...] += _dot_tn(h_ref[...], dmb)
        _store_on_last([(dwo_ref, dwo_out), (dwa_ref, dwa_out), (dwb_ref, dwb_out)])

    return _run(
        body, comm, name="merge_bwd", grid=(t // tm,),
        out_shape=[jax.ShapeDtypeStruct((t, d), BF16), jax.ShapeDtypeStruct((t, d), BF16),
                   jax.ShapeDtypeStruct((t, sa), F32), jax.ShapeDtypeStruct((t, lw), F32),
                   jax.ShapeDtypeStruct((d, d), BF16), jax.ShapeDtypeStruct((d, sa), BF16),
                   jax.ShapeDtypeStruct((lw, d), BF16)],
        in_specs=[_rows(tm, d), _rows(tm, d), _rows(tm, d), _rows(tm, d), _rows(tm, d), _rows(tm, d),
                  _rows(tm, sa), _rows(tm, lw), _whole((d, d)), _whole((d, sa)), _whole((lw, d))],
        out_specs=[_rows(tm, d), _rows(tm, d), _rows(tm, sa), _rows(tm, lw), _acc((d, d)), _acc((d, sa)), _acc((lw, d))],
        scratch_shapes=[pltpu.VMEM((d, d), F32), pltpu.VMEM((d, sa), F32), pltpu.VMEM((lw, d), F32)],
        semantics="arbitrary",
    )(dx1, merged, ma, mb, za, zb, y_a, h, w_o, w_a_out_t, w_b_out)


def _fold_diag_blocks(dense, row_group, col_group):
    r, c = dense.shape
    rows = lax.broadcasted_iota(jnp.int32, (r, c), 0)
    cols = lax.broadcasted_iota(jnp.int32, (r, c), 1)
    kept = jnp.where(rows // row_group == cols // col_group, dense, 0.0)
    pick = (lax.broadcasted_iota(jnp.int32, (row_group, r), 0)
            == lax.broadcasted_iota(jnp.int32, (row_group, r), 1) % row_group).astype(F32)
    return jnp.dot(pick, kept, preferred_element_type=F32, precision=lax.Precision.HIGHEST)


def _lru_bwd(dh, xc, hprev, u, conv_w, wr_blk, b_r, wi_blk, b_i, lru_lambda, head_dim, comm=None):
    t, w = dh.shape
    tc = min(TIME_CHUNK, t)
    steps = t // tc
    halo = SUBLANES
    sub_per_chunk = tc // halo
    slabs = w // LRU_SLAB

    def body(dh_ref, xc_ref, hp_ref, u_ref, uh_ref, cw_ref, wr_ref, br_ref, wi_ref, bi_ref, lam_ref,
             du_ref, dwr_out, dwi_out, vec_ref, lam_s, a_s, dxc_s, uext_s, carry_s, dwr_ref, dwi_ref):
        chunk = steps - 1 - pl.program_id(0)

        @pl.when(pl.program_id(0) == 0)
        def _():
            carry_s[...] = jnp.zeros_like(carry_s)
            dxc_s[tc:tc + halo, :] = jnp.zeros((halo, w), F32)
            dwr_ref[...] = jnp.zeros_like(dwr_ref)
            dwi_ref[...] = jnp.zeros_like(dwi_ref)
            vec_ref[...] = jnp.zeros_like(vec_ref)

        xc = xc_ref[...]
        r, ig, sp, log_a = _lru_gates(xc, wr_ref, br_ref, wi_ref, bi_ref, lam_ref)
        a = jnp.exp(log_a)
        a_s[...] = a

        def step(i, q):
            at = pl.ds(tc - 1 - i, 1)
            lam_row = dh_ref[at, :] + q
            lam_s[at, :] = lam_row
            return a_s[at, :] * lam_row

        carry_s[0:1, :] = lax.fori_loop(0, tc, step, carry_s[0:1, :], unroll=8)
        lam = lam_s[...]
        mult = jnp.sqrt(-_expm1(2.0 * log_a))
        d_log_a = lam * hp_ref[...] * a - (lam * ig * xc) * (a * a) / mult
        d_ig = lam * mult * xc
        dpre_r = (d_log_a * (-LRU_C * sp)) * r * (1.0 - r)
        dpre_i = d_ig * ig * (1.0 - ig)
        dxc = lam * mult * ig + _slab_dot(dpre_r, wr_ref, transposed=True) + _slab_dot(dpre_i, wi_ref, transposed=True)
        xcb, drb, dib = xc.astype(BF16), dpre_r.astype(BF16), dpre_i.astype(BF16)
        for j in range(slabs):
            cols = slice(j * LRU_SLAB, (j + 1) * LRU_SLAB)
            dwr_ref[j] += _dot_tn(drb[:, cols], xcb[:, cols])
            dwi_ref[j] += _dot_tn(dib[:, cols], xcb[:, cols])
        vec_ref[0:1, :] += _rowsum(dxc)
        vec_ref[1:2, :] += _rowsum(dpre_r)
        vec_ref[2:3, :] += _rowsum(dpre_i)
        vec_ref[3:4, :] += _rowsum(d_log_a * (-LRU_C * r)) * (-_sigmoid(-lam_ref[...]))
        dxc_s[0:tc, :] = dxc
        du = cw_ref[CONV_WIDTH - 1:CONV_WIDTH, :] * dxc
        for k in range(CONV_WIDTH - 1):
            off = CONV_WIDTH - 1 - k
            du = du + cw_ref[k:k + 1, :] * dxc_s[off:off + tc, :]
        du_ref[...] = du.astype(du_ref.dtype)
        dxc_s[tc:tc + halo, :] = dxc_s[0:halo, :]
        uext_s[0:halo, :] = jnp.where(chunk > 0, uh_ref[...], 0.0)
        uext_s[halo:halo + tc, :] = u_ref[...]
        for k in range(CONV_WIDTH):
            off = halo - (CONV_WIDTH - 1) + k
            vec_ref[4 + k:5 + k, :] += _rowsum(dxc * uext_s[off:off + tc, :])

        @pl.when(pl.program_id(0) == steps - 1)
        def _():
            for j in range(slabs):
                cols = slice(j * LRU_SLAB, (j + 1) * LRU_SLAB)
                dwr_out[:, cols] = _fold_diag_blocks(dwr_ref[j], head_dim, head_dim).astype(dwr_out.dtype)
                dwi_out[:, cols] = _fold_diag_blocks(dwi_ref[j], head_dim, head_dim).astype(dwi_out.dtype)

    halo_spec = pl.BlockSpec((halo, w), lambda i: (jnp.maximum((steps - 1 - i) * sub_per_chunk - 1, 0), 0))
    return _run(
        body, comm, name="lru_bwd", grid=(steps,),
        out_shape=[jax.ShapeDtypeStruct((t, w), BF16), jax.ShapeDtypeStruct((head_dim, w), BF16),
                   jax.ShapeDtypeStruct((head_dim, w), BF16), jax.ShapeDtypeStruct((SUBLANES, w), F32)],
        in_specs=[_rows_rev(tc, w, steps)] * 4 + [halo_spec, _whole((CONV_WIDTH, w)), _whole(wr_blk.shape),
                                                  _whole((1, w)), _whole(wi_blk.shape), _whole((1, w)), _whole((1, w))],
        out_specs=[_rows_rev(tc, w, steps), _acc((head_dim, w)), _acc((head_dim, w)), _acc((SUBLANES, w))],
        scratch_shapes=[pltpu.VMEM((tc, w), F32), pltpu.VMEM((tc, w), F32), pltpu.VMEM((tc + halo, w), F32),
                        pltpu.VMEM((halo + tc, w), F32), pltpu.VMEM((SUBLANES, w), F32),
                        pltpu.VMEM((slabs, LRU_SLAB, LRU_SLAB), F32), pltpu.VMEM((slabs, LRU_SLAB, LRU_SLAB), F32)],
        semantics="arbitrary",
    )(dh, xc, hprev, u, u, conv_w, wr_blk, b_r, wi_blk, b_i, lru_lambda)


def _s5_bwd(dya, y, sr, si, u, w_glu, b_glu, cre_blk, cimn_blk, bbr_blk, bbi_blk, ar, ai, d_skip, comm=None):
    t, sa = dya.shape
    gn = sr.shape[1]
    ns, _, sw = bbr_blk.shape
    tc = min(TIME_CHUNK, t)
    steps = t // tc
    halo = SUBLANES

    def body(dya_ref, y_ref, sr_ref, si_ref, u_ref, wg_ref, bg_ref, cre_ref, cim_ref, bbr_ref, bbi_ref,
             ar_ref, ai_ref, d_ref, du_ref, lr_ref, li_ref, dy_ref, dwg_out, vsa_ref, vgn_ref, gr_s, gi_s, cr_s, ci_s,
             dwg_ref):
        @pl.when(pl.program_id(0) == 0)
        def _():
            cr_s[...] = jnp.zeros_like(cr_s)
            ci_s[...] = jnp.zeros_like(ci_s)
            gr_s[tc:tc + halo, :] = jnp.zeros((halo, gn), F32)
            gi_s[tc:tc + halo, :] = jnp.zeros((halo, gn), F32)
            dwg_ref[...] = jnp.zeros_like(dwg_ref)
            vsa_ref[...] = jnp.zeros_like(vsa_ref)
            vgn_ref[...] = jnp.zeros_like(vgn_ref)

        yv = y_ref[...]
        uv = u_ref[...]
        zz = _gelu(yv)
        sg = _sigmoid(_dot(zz, wg_ref[...]) + bg_ref[...])
        dyav = dya_ref[...]
        dq = dyav * zz * sg * (1.0 - sg)
        dzz = dyav * sg + _dot_nt(dq, wg_ref[...])
        dwg_ref[...] += _dot_tn(zz, dq)
        dy = dzz * _gelu_grad(yv)
        dyb = dy.astype(BF16)
        dy_ref[...] = dyb.astype(dy_ref.dtype)
        vsa_ref[0:1, :] += _rowsum(dq)
        vsa_ref[1:2, :] += _rowsum(dy * uv)
        for m in range(ns):
            dym = dyb[:, m * S5_SLAB:(m + 1) * S5_SLAB]
            gr_s[0:tc, m * sw:(m + 1) * sw] = _dot_nt(dym, cre_ref[m])
            gi_s[0:tc, m * sw:(m + 1) * sw] = _dot_nt(dym, cim_ref[m])
        a_r = ar_ref[...]
        a_i = ai_ref[...]

        def step(i, carry):
            l_r, l_i = carry
            at = pl.ds(tc - 1 - i, 1)
            n_r = gr_s[at, :] + a_r * l_r + a_i * l_i
            n_i = gi_s[at, :] + a_r * l_i - a_i * l_r
            gr_s[at, :] = n_r
            gi_s[at, :] = n_i
            return n_r, n_i

        l_r, l_i = lax.fori_loop(0, tc, step, (cr_s[0:1, :], ci_s[0:1, :]), unroll=8)
        cr_s[0:1, :] = l_r
        ci_s[0:1, :] = l_i
        nxt_r = gr_s[1:tc + 1, :]
        nxt_i = gi_s[1:tc + 1, :]
        srv = sr_ref[...].astype(F32)
        siv = si_ref[...].astype(F32)
        vgn_ref[0:1, :] += _rowsum(nxt_r * srv + nxt_i * siv)
        vgn_ref[1:2, :] += _rowsum(nxt_i * srv - nxt_r * siv)
        lam_r = gr_s[0:tc, :]
        lam_i = gi_s[0:tc, :]
        gr_s[tc:tc + halo, :] = gr_s[0:halo, :]
        gi_s[tc:tc + halo, :] = gi_s[0:halo, :]
        lrb = lam_r.astype(BF16)
        lib = lam_i.astype(BF16)
        lr_ref[...] = lrb.astype(lr_ref.dtype)
        li_ref[...] = lib.astype(li_ref.dtype)
        for m in range(ns):
            states, chans = slice(m * sw, (m + 1) * sw), slice(m * S5_SLAB, (m + 1) * S5_SLAB)
            du_ref[:, chans] = (_dot_nt(lrb[:, states], bbr_ref[m]) + _dot_nt(lib[:, states], bbi_ref[m])
                                + dy[:, chans] * d_ref[:, chans]).astype(du_ref.dtype)
        _store_on_last([(dwg_ref, dwg_out)])

    return _run(
        body, comm, name="s5_bwd", grid=(steps,),
        out_shape=[jax.ShapeDtypeStruct((t, sa), BF16), jax.ShapeDtypeStruct((t, gn), BF16),
                   jax.ShapeDtypeStruct((t, gn), BF16), jax.ShapeDtypeStruct((t, sa), BF16),
                   jax.ShapeDtypeStruct((sa, sa), BF16), jax.ShapeDtypeStruct((SUBLANES, sa), F32),
                   jax.ShapeDtypeStruct((SUBLANES, gn), F32)],
        in_specs=[_rows_rev(tc, sa, steps), _rows_rev(tc, sa, steps), _rows_rev(tc, gn, steps), _rows_rev(tc, gn, steps),
                  _rows_rev(tc, sa, steps), _whole((sa, sa)), _whole((1, sa)), _whole(cre_blk.shape),
                  _whole(cimn_blk.shape), _whole(bbr_blk.shape), _whole(bbi_blk.shape), _whole((1, gn)), _whole((1, gn)),
                  _whole((1, sa))],
        out_specs=[_rows_rev(tc, sa, steps), _rows_rev(tc, gn, steps), _rows_rev(tc, gn, steps), _rows_rev(tc, sa, steps),
                   _acc((sa, sa)), _acc((SUBLANES, sa)), _acc((SUBLANES, gn))],
        scratch_shapes=[pltpu.VMEM((tc + halo, gn), F32), pltpu.VMEM((tc + halo, gn), F32),
                        pltpu.VMEM((SUBLANES, gn), F32), pltpu.VMEM((SUBLANES, gn), F32), pltpu.VMEM((sa, sa), F32)],
        semantics="arbitrary",
    )(dya, y, sr, si, u, w_glu, b_glu, cre_blk, cimn_blk, bbr_blk, bbi_blk, ar, ai, d_skip)


def _inproj_bwd(dparts, x, dx1, g_mix, w_in_t, comm=None):
    t, d = x.shape
    n = w_in_t.shape[0]
    widths = [p.shape[1] for p in dparts]
    offs = [sum(widths[:i]) for i in range(len(widths) + 1)]
    tm = min(TOKEN_TILE, t)
    np_ = len(dparts)

    def body(*refs):
        dz_refs = refs[:np_]
        x_ref, dx1_ref, g_ref, w_ref, gx_ref, h_ref, vd_ref, vn_ref = refs[np_:]
        _zero_on_first(vd_ref, vn_ref)
        dh = jnp.zeros((tm, d), F32)
        for k, r in enumerate(dz_refs):
            lo, hi = offs[k], offs[k + 1]
            dzk = r[...]
            dh = dh + _dot(dzk, w_ref[lo:hi, :])
            vn_ref[0:1, lo:hi] += _rowsum(dzk.astype(F32))
        xhat, r0 = _rms_stats(x_ref[...])
        h_ref[...] = (xhat * g_ref[...]).astype(h_ref.dtype)
        dxn, dg = _rms_bwd(dh, xhat, r0, g_ref[...])
        gx_ref[...] = dx1_ref[...] + dxn
        vd_ref[0:1, :] += _rowsum(dg)

    return _run(
        body, comm, name="inproj_bwd", grid=(t // tm,),
        out_shape=[jax.ShapeDtypeStruct((t, d), F32), jax.ShapeDtypeStruct((t, d), BF16),
                   jax.ShapeDtypeStruct((SUBLANES, d), F32), jax.ShapeDtypeStruct((SUBLANES, n), F32)],
        in_specs=[_rows(tm, w) for w in widths] + [_rows(tm, d), _rows(tm, d), _whole((1, d)), _whole((n, d))],
        out_specs=[_rows(tm, d), _rows(tm, d), _acc((SUBLANES, d)), _acc((SUBLANES, n))],
        semantics="arbitrary",
    )(*dparts, x, dx1, g_mix, w_in_t)


def _dw_from_parts(dparts, h, tn, name, comm=None):
    t, d = h.shape
    widths = [p.shape[1] for p in dparts]
    offs = [sum(widths[:i]) for i in range(len(widths) + 1)]
    np_ = len(dparts)

    def body(*refs):
        h_ref, o_ref = refs[np_], refs[np_ + 1]
        hv = h_ref[...]
        for k, r in enumerate(refs[:np_]):
            o_ref[offs[k]:offs[k + 1], :] = _dot_tn(r[...], hv).astype(o_ref.dtype)

    return _run(
        body, comm, name=name, grid=(d // tn,),
        out_shape=jax.ShapeDtypeStruct((offs[-1], d), BF16),
        in_specs=[_whole(p.shape) for p in dparts] + [pl.BlockSpec((t, tn), lambda j: (0, j))],
        out_specs=pl.BlockSpec((offs[-1], tn), lambda j: (0, j)),
        semantics="parallel",
    )(*dparts, h)


def _prep(lr_row, li_row, ldt_row, lr_col, li_col, ldt_col, b_re, b_im, c_re, c_im, w_r, w_i, comm=None):
    gn, pch = b_re.shape
    sa, n = c_re.shape
    w, hd = w_r.shape
    ns, sw, lsl = sa // S5_SLAB, S5_SLAB * n // pch, w // LRU_SLAB

    def spread_cols(vals, row_group, col_group, width):
        r, k = vals.shape
        tile = (lax.broadcasted_iota(jnp.int32, (k, width), 0) == lax.broadcasted_iota(jnp.int32, (k, width), 1) % k)
        rows = lax.broadcasted_iota(jnp.int32, (r, width), 0) // row_group
        cols = lax.broadcasted_iota(jnp.int32, (r, width), 1) // col_group
        return jnp.where(rows == cols, _dot(vals, tile.astype(BF16)), 0.0)

    def spread_rows(vals, row_group, col_group, height):
        k, c = vals.shape
        tile = (lax.broadcasted_iota(jnp.int32, (height, k), 0) % k == lax.broadcasted_iota(jnp.int32, (height, k), 1))
        rows = lax.broadcasted_iota(jnp.int32, (height, c), 0) // row_group
        cols = lax.broadcasted_iota(jnp.int32, (height, c), 1) // col_group
        return jnp.where(rows == cols, _dot(tile.astype(BF16), vals), 0.0)

    def body(lrr, lir, ldr, lrc, lic, ldc, bre, bim, cre, cim, wr, wi,
             ar_o, ai_o, bbr_o, bbi_o, cre_o, cim_o, wr_o, wi_o):
        ar, ai, _, _ = _disc_scalars(lrr[...], lir[...], ldr[...])
        ar_o[...] = ar
        ai_o[...] = ai
        _, _, bbr, bbi = _disc_cols(lrc[...], lic[...], ldc[...], bre[...], bim[...])
        bbr_t, bbi_t = bbr.T, bbi.T
        for m in range(ns):
            bbr_o[m] = spread_rows(bbr_t[:, m * sw:(m + 1) * sw], pch, n, S5_SLAB).astype(bbr_o.dtype)
            bbi_o[m] = spread_rows(bbi_t[:, m * sw:(m + 1) * sw], pch, n, S5_SLAB).astype(bbi_o.dtype)
            rows = slice(m * S5_SLAB, (m + 1) * S5_SLAB)
            cre_o[m] = spread_rows(cre[rows, :].T, n, pch, sw).astype(cre_o.dtype)
            cim_o[m] = spread_rows(-cim[rows, :].T, n, pch, sw).astype(cim_o.dtype)
        for j in range(lsl):
            rows = slice(j * LRU_SLAB, (j + 1) * LRU_SLAB)
            wr_o[j] = spread_cols(wr[rows, :], hd, hd, LRU_SLAB).astype(wr_o.dtype)
            wi_o[j] = spread_cols(wi[rows, :], hd, hd, LRU_SLAB).astype(wi_o.dtype)

    args = (lr_row, li_row, ldt_row, lr_col, li_col, ldt_col, b_re, b_im, c_re, c_im, w_r, w_i)
    out_shape = [jax.ShapeDtypeStruct((1, gn), F32), jax.ShapeDtypeStruct((1, gn), F32),
                 jax.ShapeDtypeStruct((ns, S5_SLAB, sw), BF16), jax.ShapeDtypeStruct((ns, S5_SLAB, sw), BF16),
                 jax.ShapeDtypeStruct((ns, sw, S5_SLAB), BF16), jax.ShapeDtypeStruct((ns, sw, S5_SLAB), BF16),
                 jax.ShapeDtypeStruct((lsl, LRU_SLAB, LRU_SLAB), BF16),
                 jax.ShapeDtypeStruct((lsl, LRU_SLAB, LRU_SLAB), BF16)]
    return _run(
        body, comm, name="prep", grid=(1,), out_shape=out_shape, in_specs=[_whole(a.shape) for a in args],
        out_specs=[_acc(s.shape) for s in out_shape], semantics="arbitrary",
    )(*args)


def _s5_param_grads(lam_r, lam_i, sr, si, u, dy, pch, n, comm=None):
    t, gn = lam_r.shape
    sa = u.shape[1]
    sw = S5_SLAB * n // pch

    def body(lr_ref, li_ref, sr_ref, si_ref, u_ref, dy_ref, dbr_ref, dbi_ref, dcr_ref, dci_ref):
        uv = u_ref[...]
        dyv = dy_ref[...]
        dbr_ref[...] = _fold_diag_blocks(_dot_tn(uv, lr_ref[...]), pch, n).astype(dbr_ref.dtype)
        dbi_ref[...] = _fold_diag_blocks(_dot_tn(uv, li_ref[...]), pch, n).astype(dbi_ref.dtype)
        dcr_ref[...] = _fold_diag_blocks(_dot_tn(sr_ref[...], dyv), n, pch).astype(dcr_ref.dtype)
        dci_ref[...] = _fold_diag_blocks(_dot_tn(si_ref[...], dyv), n, pch).astype(dci_ref.dtype)

    states = pl.BlockSpec((t, sw), lambda m: (0, m))
    chans = pl.BlockSpec((t, S5_SLAB), lambda m: (0, m))
    return _run(
        body, comm, name="s5_param_grads", grid=(sa // S5_SLAB,),
        out_shape=[jax.ShapeDtypeStruct((pch, gn), BF16), jax.ShapeDtypeStruct((pch, gn), BF16),
                   jax.ShapeDtypeStruct((n, sa), BF16), jax.ShapeDtypeStruct((n, sa), BF16)],
        in_specs=[states, states, states, states, chans, chans],
        out_specs=[pl.BlockSpec((pch, sw), lambda m: (0, m)), pl.BlockSpec((pch, sw), lambda m: (0, m)),
                   pl.BlockSpec((n, S5_SLAB), lambda m: (0, m)), pl.BlockSpec((n, S5_SLAB), lambda m: (0, m))],
        semantics="parallel",
    )(lam_r, lam_i, sr, si, u, dy)


SMALL_PARTS = ["vec_tail", "vec_ffn", "vec_lru", "vec_mix", "vec_bin", "vec_sa", "vec_gn", "dw_r", "dw_i", "dbb_re",
               "dbb_im", "dc_re", "dc_imn", "loss"]


def _small_reduce(parts, shapes, lr_col, li_col, ldt_col, b_re, b_im, groups, after=()):
    gn, pch = b_re.shape
    n = gn // groups
    nparts = parts[SMALL_PARTS[0]].size // math.prod(shapes[SMALL_PARTS[0]])
    np_, nout, first_out = len(SMALL_PARTS), 24, len(SMALL_PARTS) + 5 + len(after)

    def body(*refs):
        ins = refs[:np_]
        lr, li, ldt, bre, bim = refs[np_:np_ + 5]
        outs = refs[first_out:first_out + nout]
        sums = dict(zip(SMALL_PARTS, refs[first_out + nout:]))

        @pl.when(pl.program_id(0) == 0)
        def _():
            for k, r in zip(SMALL_PARTS, ins):
                sums[k][...] = r[...].astype(F32)

        @pl.when(pl.program_id(0) > 0)
        def _():
            for k, r in zip(SMALL_PARTS, ins):
                sums[k][...] += r[...].astype(F32)

        @pl.when(pl.program_id(0) == nparts - 1)
        def _():
            finish({k: s[...] for k, s in sums.items()}, lr, li, ldt, bre, bim, *outs)

    def finish(tot, lr, li, ldt, bre, bim, o_loss, o_gmix, o_bin, o_bglu, o_s5d, o_convb, o_br, o_bi, o_lam, o_gffn,
               o_gpg, o_bpg, o_gple, o_gfin, o_wr, o_wi, o_cre, o_cim, o_bre, o_bim, o_lre, o_lim, o_ldt, o_convw):
        o_loss[...] = tot["loss"]
        o_convw[...] = tot["vec_lru"][SUBLANES - CONV_WIDTH:SUBLANES]
        o_bpg[...] = tot["vec_tail"][0:1]
        o_gpg[...] = tot["vec_tail"][1:2]
        o_gple[...] = tot["vec_tail"][2:3]
        o_gfin[...] = tot["vec_tail"][3:4]
        o_gffn[...] = tot["vec_ffn"][0:1]
        o_convb[...] = tot["vec_lru"][0:1]
        o_br[...] = tot["vec_lru"][1:2]
        o_bi[...] = tot["vec_lru"][2:3]
        o_lam[...] = tot["vec_lru"][3:4]
        o_gmix[...] = tot["vec_mix"][0:1]
        o_bin[...] = tot["vec_bin"][0:1]
        o_bglu[...] = tot["vec_sa"][0:1]
        o_s5d[...] = tot["vec_sa"][1:2]
        o_wr[...] = tot["dw_r"].T
        o_wi[...] = tot["dw_i"].T
        o_cre[...] = tot["dc_re"].T
        o_cim[...] = -tot["dc_imn"].T
        d_a = tot["vec_gn"].T
        _, chain = jax.vjp(_disc_cols, lr[...], li[...], ldt[...], bre[...], bim[...])
        d_lr, d_li, d_ldt, d_bre, d_bim = chain((d_a[:, 0:1], d_a[:, 1:2], tot["dbb_re"].T, tot["dbb_im"].T))
        o_lre[...] = d_lr
        o_lim[...] = d_li
        o_bre[...] = d_bre
        o_bim[...] = d_bim
        same = (lax.broadcasted_iota(jnp.int32, (groups, gn), 0)
                == lax.broadcasted_iota(jnp.int32, (groups, gn), 1) // n).astype(F32)
        o_ldt[...] = jnp.dot(same, d_ldt * jnp.ones((1, LANES), F32), preferred_element_type=F32,
                             precision=lax.Precision.HIGHEST)[:, 0:1]

    d = shapes["vec_mix"][1]
    nz = shapes["vec_bin"][1]
    sa = shapes["vec_sa"][1]
    w = shapes["vec_lru"][1]
    row = lambda c: jax.ShapeDtypeStruct((1, c), F32)
    out_shape = [jax.ShapeDtypeStruct(shapes["loss"], F32), row(d), row(nz), row(sa), row(sa), row(w), row(w), row(w),
                 row(w), row(d), row(d), row(d), row(d), row(d),
                 jax.ShapeDtypeStruct(shapes["dw_r"][::-1], F32), jax.ShapeDtypeStruct(shapes["dw_i"][::-1], F32),
                 jax.ShapeDtypeStruct(shapes["dc_re"][::-1], F32), jax.ShapeDtypeStruct(shapes["dc_imn"][::-1], F32),
                 jax.ShapeDtypeStruct((gn, pch), F32), jax.ShapeDtypeStruct((gn, pch), F32),
                 jax.ShapeDtypeStruct((gn, 1), F32), jax.ShapeDtypeStruct((gn, 1), F32),
                 jax.ShapeDtypeStruct((groups, 1), F32), jax.ShapeDtypeStruct((CONV_WIDTH, w), F32)]
    def part_spec(k):
        r, c = shapes[k]
        if parts[k].ndim == 3:
            return pl.BlockSpec((None, r, c), lambda i: (i, 0, 0))
        return pl.BlockSpec((r, c), lambda i: (i, 0))

    outs = _run(
        body, None, name="small_reduce", grid=(nparts,), out_shape=out_shape,
        in_specs=([part_spec(k) for k in SMALL_PARTS] + [_whole(a.shape) for a in (lr_col, li_col, ldt_col, b_re, b_im)]
                  + [_ANY] * len(after)),
        out_specs=[_acc(s.shape) for s in out_shape],
        scratch_shapes=[pltpu.VMEM(shapes[k], F32) for k in SMALL_PARTS],
        semantics="arbitrary",
    )(*[parts[k] for k in SMALL_PARTS], lr_col, li_col, ldt_col, b_re, b_im, *after)
    names = ["loss", "g_mix", "b_in", "b_glu", "s5_d", "conv_b", "b_r", "b_i", "lru_lambda", "g_ffn", "g_ple_gate",
             "b_ple_gate", "g_ple", "g_final", "w_r", "w_i", "s5_c_re", "s5_c_im", "s5_b_re", "s5_b_im", "lam_re",
             "lam_im", "log_dt", "conv_w"]
    return dict(zip(names, outs))


def _adamw_small(ws, gs, ms, vs):
    n = len(ws)

    def body(*refs):
        w_r, g_r, m_r, v_r = (refs[i * n:(i + 1) * n] for i in range(4))
        g_o, d_o, m_o, v_o = (refs[(4 + i) * n:(5 + i) * n] for i in range(4))
        for i in range(n):
            g = g_r[i][...]
            delta, m_new, v_new = _adamw_math(w_r[i][...], g, m_r[i][...], v_r[i][...])
            g_o[i][...] = g
            d_o[i][...] = delta
            m_o[i][...] = m_new
            v_o[i][...] = v_new

    shapes = [jax.ShapeDtypeStruct(a.shape, F32) for a in ws]
    outs = pl.pallas_call(body, name="adamw_small", out_shape=shapes * 4)(*ws, *gs, *ms, *vs)
    return outs[:n], outs[n:2 * n], outs[2 * n:3 * n], outs[3 * n:]


def _adamw_math(w, g, m, v):
    m_new = ADAM_B1 * m + (1.0 - ADAM_B1) * g
    v_new = ADAM_B2 * v + (1.0 - ADAM_B2) * (g * g)
    m_hat = m_new / (1.0 - ADAM_B1 ** ADAM_STEP)
    v_hat = v_new / (1.0 - ADAM_B2 ** ADAM_STEP)
    delta = -ADAM_LR * (m_hat / (jnp.sqrt(v_hat) + ADAM_EPS) + ADAM_WD * w)
    return delta, m_new, v_new


def _row_tile(rows):
    for cand in range(256, 0, -16):
        if rows % cand == 0:
            return cand
    return rows


def _adamw(parts, w, m, v, name, transposed=False, own=None, after=()):
    nw = len(w)
    rows, cols = w[0].shape
    npart = parts[0].shape[0]
    tr = _row_tile(rows)
    if transposed:
        parts_spec = pl.BlockSpec((npart, cols, tr), lambda i: (0, 0, i))
    else:
        parts_spec = pl.BlockSpec((npart, tr, cols), lambda i: (0, i, 0))
    per = 4 if own is None else 5
    first_out = per * nw + len(after)

    def body(*refs):
        mine = None if own is None else _chip(_mesh_position())
        for i in range(nw):
            group = refs[per * i:per * i + per]
            p_ref, (w_ref, m_ref, v_ref) = group[0], group[-3:]
            g_ref, d_ref, mo_ref, vo_ref = refs[first_out + 4 * i:first_out + 4 * i + 4]

            def part(k):
                a = p_ref[k].astype(F32)
                return a if own is None else jnp.where(mine == k, group[1][k].astype(F32), a)

            g = part(0)
            for k in range(1, npart):
                g = g + part(k)
            if transposed:
                g = g.T
            delta, m_new, v_new = _adamw_math(w_ref[...], g, m_ref[...], v_ref[...])
            g_ref[...] = g
            d_ref[...] = delta
            mo_ref[...] = m_new
            vo_ref[...] = v_new

    groups = zip(parts, w, m, v) if own is None else zip(parts, own, w, m, v)
    res = pl.pallas_call(
        body, name=name, grid=(rows // tr,),
        out_shape=[jax.ShapeDtypeStruct((rows, cols), F32)] * (4 * nw),
        in_specs=([parts_spec] * (per - 3) + [_rows(tr, cols)] * 3) * nw + [_ANY] * len(after),
        out_specs=[_rows(tr, cols)] * (4 * nw),
        compiler_params=_params("parallel"),
    )(*[a for group in groups for a in group], *after)
    return [res[4 * i:4 * i + 4] for i in range(nw)]


def _mesh_position():
    return lax.axis_index("x"), lax.axis_index("y"), lax.axis_index("c")


def _flip(pos, rel):
    x, y, c = pos
    return (1 - x if rel & 4 else x, 1 - y if rel & 2 else y, 1 - c if rel & 1 else c)


def _index(pos):
    return 4 * pos[0] + 2 * pos[1] + pos[2]


_ANY = pl.BlockSpec(memory_space=pl.ANY)
FLAT_ROWS = 32


def _dma_sems(n):
    return [pltpu.SemaphoreType.DMA((n, N_DEV - 1)), pltpu.SemaphoreType.DMA((n, N_DEV - 1)), pltpu.SemaphoreType.DMA((n,))]


def _block_of(ref, idx, rows, flat):
    if flat:
        return ref.at[pl.ds(pl.multiple_of(idx * rows, FLAT_ROWS), rows), :]
    return ref.at[idx]


class _Gather:
    chips = (4, 2, 6)
    rels = frozenset((1, 4, 2, 6))

    def __init__(self, shards):
        self.inputs = list(shards)
        self.flat = [s.shape[0] % FLAT_ROWS == 0 for s in shards]
        self.out_shape = [
            jax.ShapeDtypeStruct((N_DEV * s.shape[0], s.shape[1]) if f else (N_DEV,) + s.shape, s.dtype)
            for s, f in zip(shards, self.flat)]
        self.sems = _dma_sems(len(shards))

    def _copy(self, ins, outs, sems, i, k, block, to, own=False, lands_index=None):
        a = i if lands_index is None else lands_index
        dst = _block_of(outs[a], _index(block), self.inputs[a].shape[0], self.flat[a])
        return pltpu.make_async_remote_copy(
            src_ref=ins[a] if own else dst, dst_ref=dst, send_sem=sems[0].at[i, k], recv_sem=sems[1].at[i, k],
            device_id=to, device_id_type=MESH)

    def _local(self, ins, outs, sems, i, me):
        dst = _block_of(outs[i], _index(me), self.inputs[i].shape[0], self.flat[i])
        return pltpu.make_async_copy(ins[i], dst, sems[2].at[i])

    def _first(self, ins, outs, sems, i, me):
        cps = [self._copy(ins, outs, sems, i, 0, me, _flip(me, 1), own=True)]
        cps += [self._copy(ins, outs, sems, i, 1 + j, me, _flip(me, rel), own=True) for j, rel in enumerate(self.chips)]
        return cps

    def _passed(self, ins, outs, sems, i, j, me):
        return self._copy(ins, outs, sems, i, 4 + j, _flip(me, self.chips[j]), _flip(me, 1))

    def before(self, ins, outs, sems):
        n = len(self.inputs)
        me = _mesh_position()

        @pl.when(pl.program_id(0) == 0)
        def _():
            for i in range(n):
                self._local(ins, outs, sems, i, me).start()
                for cp in self._first(ins, outs, sems, i, me):
                    cp.start()

        @pl.when(pl.program_id(0) == pl.num_programs(0) - 1)
        def _():
            for j, rel in enumerate(self.chips):
                for i in range(n):
                    self._copy(ins, outs, sems, i, 1 + j, _flip(me, rel), me).wait_recv()
                    self._passed(ins, outs, sems, i, j, me).start()

    def after(self, ins, outs, sems):
        n = len(self.inputs)
        me = _mesh_position()
        sibling = _flip(me, 1)

        @pl.when(pl.program_id(0) == pl.num_programs(0) - 1)
        def _():
            for i in range(n):
                self._copy(ins, outs, sems, i, 0, sibling, me).wait_recv()
                for j, rel in enumerate(self.chips):
                    self._copy(ins, outs, sems, i, 4 + j, _flip(sibling, rel), me).wait_recv()
            for i in range(n):
                for cp in self._first(ins, outs, sems, i, me):
                    cp.wait_send()
                for j in range(len(self.chips)):
                    self._passed(ins, outs, sems, i, j, me).wait_send()
                self._local(ins, outs, sems, i, me).wait()


N_CHIPS = N_DEV // 2


def _chip(pos):
    return 2 * pos[0] + pos[1]


class _PairSwap:
    rels = frozenset((1,))

    def __init__(self, arrays):
        self.inputs = list(arrays)
        self.rows = [a.shape[0] // N_DEV for a in arrays]
        for r in self.rows:
            assert r % FLAT_ROWS == 0, r
        self.out_shape = [jax.ShapeDtypeStruct((N_CHIPS, r, a.shape[1]), a.dtype) for a, r in zip(arrays, self.rows)]
        n = len(arrays)
        self.sems = [pltpu.SemaphoreType.DMA((n, N_CHIPS)), pltpu.SemaphoreType.DMA((n, N_CHIPS))]

    def _copy(self, ins, outs, sems, i, j, me):
        sibling = _flip(me, 1)
        return pltpu.make_async_remote_copy(
            src_ref=_block_of(ins[i], 2 * j + sibling[2], self.rows[i], True), dst_ref=outs[i].at[j],
            send_sem=sems[0].at[i, j], recv_sem=sems[1].at[i, j], device_id=sibling, device_id_type=MESH)

    def before(self, ins, outs, sems):
        me = _mesh_position()

        @pl.when(pl.program_id(0) == 0)
        def _():
            for i in range(len(self.inputs)):
                for j in range(N_CHIPS):
                    self._copy(ins, outs, sems, i, j, me).start()

    def after(self, ins, outs, sems):
        me = _mesh_position()

        @pl.when(pl.program_id(0) == pl.num_programs(0) - 1)
        def _():
            for i in range(len(self.inputs)):
                for j in range(N_CHIPS):
                    self._copy(ins, outs, sems, i, j, me).wait()


class _ChipExchange:
    chips = (4, 2, 6)
    rels = frozenset(chips)

    def __init__(self, arrays, after=()):
        self.arrays = len(arrays)
        self.inputs = list(arrays) + list(after)
        self.out_shape = [jax.ShapeDtypeStruct(a.shape, a.dtype) for a in arrays]
        n = len(arrays)
        self.sems = [pltpu.SemaphoreType.DMA((n, 3)), pltpu.SemaphoreType.DMA((n, 3)), pltpu.SemaphoreType.DMA((n,))]

    def _send(self, ins, outs, sems, i, k, me):
        peer = _flip(me, self.chips[k])
        return pltpu.make_async_remote_copy(
            src_ref=ins[i].at[_chip(peer)], dst_ref=outs[i].at[_chip(me)], send_sem=sems[0].at[i, k],
            recv_sem=sems[1].at[i, k], device_id=peer, device_id_type=MESH)

    def _arrival(self, ins, outs, sems, i, k, me):
        peer = _flip(me, self.chips[k])
        return pltpu.make_async_remote_copy(
            src_ref=ins[i].at[_chip(me)], dst_ref=outs[i].at[_chip(peer)], send_sem=sems[0].at[i, k],
            recv_sem=sems[1].at[i, k], device_id=peer, device_id_type=MESH)

    def _local(self, ins, outs, sems, i, me):
        return pltpu.make_async_copy(ins[i].at[_chip(me)], outs[i].at[_chip(me)], sems[2].at[i])

    def before(self, ins, outs, sems):
        me = _mesh_position()

        @pl.when(pl.program_id(0) == 0)
        def _():
            for i in range(self.arrays):
                self._local(ins, outs, sems, i, me).start()
                for k in range(len(self.chips)):
                    self._send(ins, outs, sems, i, k, me).start()

    def after(self, ins, outs, sems):
        me = _mesh_position()

        @pl.when(pl.program_id(0) == pl.num_programs(0) - 1)
        def _():
            for i in range(self.arrays):
                for k in range(len(self.chips)):
                    self._arrival(ins, outs, sems, i, k, me).wait_recv()
            for i in range(self.arrays):
                for k in range(len(self.chips)):
                    self._send(ins, outs, sems, i, k, me).wait_send()
                self._local(ins, outs, sems, i, me).wait()


_HBM = pl.BlockSpec(memory_space=pltpu.HBM)
_SEM = pl.BlockSpec(memory_space=pltpu.SEMAPHORE)
_DATAFLOW = pltpu.SideEffectType.DATAFLOW_SIDE_EFFECTING
_CHIP_RELS = (4, 2, 6)
_SPLIT_COLLECTIVE_ID = 3


def _chips_copy(src_ref, dst_ref, send_sems, recv_sems, k, me):
    peer = _flip(me, _CHIP_RELS[k])
    return pltpu.make_async_remote_copy(
        src_ref=src_ref.at[_chip(peer)], dst_ref=dst_ref.at[_chip(me)], send_sem=send_sems.at[k],
        recv_sem=recv_sems.at[k], device_id=peer, device_id_type=MESH)


def _chips_start(pairs, name):
    def body(p_ref, land_ref, send_sems, recv_sems, p_thru, land_thru, token):
        me = _mesh_position()
        sem = pltpu.get_barrier_semaphore()
        for rel in _CHIP_RELS:
            pl.semaphore_signal(sem, inc=1, device_id=_flip(me, rel), device_id_type=MESH)
        pl.semaphore_wait(sem, len(_CHIP_RELS))
        for k in range(len(_CHIP_RELS)):
            _chips_copy(p_ref, land_ref, send_sems, recv_sems, k, me).start()
        token[...] = jnp.zeros_like(token)

    n = len(_CHIP_RELS)
    return pl.pallas_call(
        body, name="chips_start_" + name,
        out_shape=(pltpu.SemaphoreType.DMA((n,)), pltpu.SemaphoreType.DMA((n,)), pltpu.HBM(pairs.shape, pairs.dtype),
                   pltpu.HBM(pairs.shape, pairs.dtype), jax.ShapeDtypeStruct((SUBLANES, LANES), F32)),
        in_specs=(_HBM, _HBM), out_specs=(_SEM, _SEM, _HBM, _HBM, pl.BlockSpec(memory_space=pltpu.VMEM)),
        input_output_aliases={0: 2, 1: 3},
        compiler_params=pltpu.CompilerParams(has_side_effects=_DATAFLOW, collective_id=_SPLIT_COLLECTIVE_ID),
    )(pltpu.with_memory_space_constraint(pairs, pltpu.HBM),
      pltpu.with_memory_space_constraint(lax.empty(pairs.shape, pairs.dtype), pltpu.HBM))


def _chips_wait(send_sems, recv_sems, p_thru, land_thru, after, name):
    def body(p_ref, land_ref, send_sems, recv_sems, *rest):
        me = _mesh_position()
        for k in range(len(_CHIP_RELS)):
            copy = _chips_copy(p_ref, land_ref, send_sems, recv_sems, k, me)
            copy.wait_send()
            copy.wait_recv()

    return pl.pallas_call(
        body, name="chips_wait_" + name,
        out_shape=(pltpu.HBM(p_thru.shape, p_thru.dtype), pltpu.HBM(p_thru.shape, p_thru.dtype)),
        in_specs=(_HBM, _HBM, _SEM, _SEM) + (_ANY,) * len(after), out_specs=(_HBM, _HBM),
        input_output_aliases={0: 0, 1: 1}, compiler_params=pltpu.CompilerParams(has_side_effects=_DATAFLOW),
    )(p_thru, land_thru, send_sems, recv_sems, *after)


_GATHER_START_ID, _GATHER_PASS_ID = 4, 5


class _Rows:
    def __init__(self, ref, width):
        self.ref, self.width, self.at = ref, width, self

    def __getitem__(self, idx):
        i, k = idx
        return self.ref.at[i * self.width + k]


def _handshake(rels):
    me = _mesh_position()
    sem = pltpu.get_barrier_semaphore()
    for rel in rels:
        pl.semaphore_signal(sem, inc=1, device_id=_flip(me, rel), device_id_type=MESH)
    pl.semaphore_wait(sem, len(rels))


def _gather_start(shards, name, after=()):
    job, n, na = _Gather(shards), len(shards), len(after)

    def body(*refs):
        ins, lands = refs[:n], refs[n:2 * n]
        send, recv, local = refs[2 * n + na:2 * n + na + 3]
        sems = (_Rows(send, N_DEV - 1), _Rows(recv, N_DEV - 1), local)
        token = refs[-1]
        me = _mesh_position()
        _handshake(sorted(job.rels))
        for i in range(n):
            job._local(ins, lands, sems, i, me).start()
            for cp in job._first(ins, lands, sems, i, me):
                cp.start()
        token[...] = jnp.zeros_like(token)

    res = pl.pallas_call(
        body, name=name,
        out_shape=(pltpu.SemaphoreType.DMA((n * (N_DEV - 1),)), pltpu.SemaphoreType.DMA((n * (N_DEV - 1),)),
                   pltpu.SemaphoreType.DMA((n,))) + tuple(pltpu.HBM(s.shape, s.dtype) for s in job.out_shape)
        + (jax.ShapeDtypeStruct((SUBLANES, LANES), F32),),
        in_specs=(_HBM,) * (2 * n) + (_ANY,) * na,
        out_specs=(_SEM,) * 3 + (_HBM,) * n + (pl.BlockSpec(memory_space=pltpu.VMEM),),
        input_output_aliases={n + i: 3 + i for i in range(n)},
        compiler_params=pltpu.CompilerParams(has_side_effects=_DATAFLOW, collective_id=_GATHER_START_ID),
    )(*[pltpu.with_memory_space_constraint(s, pltpu.HBM) for s in shards],
      *[pltpu.with_memory_space_constraint(lax.empty(s.shape, s.dtype), pltpu.HBM) for s in job.out_shape], *after)
    return list(res[:3]), list(res[3:3 + n]), res[-1]


def _gather_pass(shards, which, sems, lands, name, after=()):
    job, n, m, na = _Gather(shards), len(shards), len(which), len(after)

    def body(*refs):
        lands_r, local = refs[:m], refs[m + 2]
        send, recv = _Rows(refs[m], N_DEV - 1), _Rows(refs[m + 1], N_DEV - 1)
        send2, recv2 = _Rows(refs[m + 3 + na], 3), _Rows(refs[m + 3 + na + 1], 3)
        me = _mesh_position()
        _handshake((1,))
        by_index = {i: lands_r[q] for q, i in enumerate(which)}
        for j, rel in enumerate(job.chips):
            for q, i in enumerate(which):
                job._copy(None, by_index, (send, recv, local), i, 1 + j, _flip(me, rel), me).wait_recv()
                job._copy(None, by_index, (send2, recv2), q, j, _flip(me, rel), _flip(me, 1), lands_index=i).start()

    res = pl.pallas_call(
        body, name=name,
        out_shape=(pltpu.SemaphoreType.DMA((m * 3,)), pltpu.SemaphoreType.DMA((m * 3,)))
        + tuple(pltpu.HBM(lands[i].shape, lands[i].dtype) for i in which),
        in_specs=(_HBM,) * m + (_SEM,) * 3 + (_ANY,) * na, out_specs=(_SEM, _SEM) + (_HBM,) * m,
        input_output_aliases={q: 2 + q for q in range(m)},
        compiler_params=pltpu.CompilerParams(has_side_effects=_DATAFLOW, collective_id=_GATHER_PASS_ID),
    )(*[lands[i] for i in which], *sems, *after)
    return list(res[:2]), list(res[2:])


def _gather_finish(shards, which, sems, sems2, lands, after, name):
    job, n, m = _Gather(shards), len(shards), len(which)

    def body(*refs):
        ins, lands_r = refs[:m], refs[m:2 * m]
        send, recv, local = _Rows(refs[2 * m], N_DEV - 1), _Rows(refs[2 * m + 1], N_DEV - 1), refs[2 * m + 2]
        send2, recv2 = _Rows(refs[2 * m + 3], 3), _Rows(refs[2 * m + 4], 3)
        me = _mesh_position()
        sibling = _flip(me, 1)
        by_index = {i: lands_r[q] for q, i in enumerate(which)}
        own = {i: ins[q] for q, i in enumerate(which)}
        for q, i in enumerate(which):
            job._copy(None, by_index, (send, recv, local), i, 0, sibling, me).wait_recv()
            for j, rel in enumerate(job.chips):
                job._copy(None, by_index, (send2, recv2), q, j, _flip(sibling, rel), me, lands_index=i).wait_recv()
                job._copy(None, by_index, (send2, recv2), q, j, _flip(me, rel), sibling, lands_index=i).wait_send()
            for cp in job._first(own, by_index, (send, recv, local), i, me):
                cp.wait_send()
            job._local(own, by_index, (send, recv, local), i, me).wait()

    res = pl.pallas_call(
        body, name=name, out_shape=tuple(pltpu.HBM(lands[i].shape, lands[i].dtype) for i in which),
        in_specs=(_HBM,) * (2 * m) + (_SEM,) * 5 + (_ANY,) * len(after), out_specs=(_HBM,) * m,
        input_output_aliases={m + q: q for q in range(m)},
        compiler_params=pltpu.CompilerParams(has_side_effects=_DATAFLOW),
    )(*[pltpu.with_memory_space_constraint(shards[i], pltpu.HBM) for i in which], *[lands[i] for i in which],
      *sems, *sems2, *after)
    return list(res)


def _pair_add(grads, halves, name):
    n = len(grads)

    def body(*refs):
        g_refs, h_refs, o_refs = refs[:n], refs[n:2 * n], refs[2 * n:]
        c = lax.axis_index("c")
        for i in range(n):
            r = h_refs[i].shape[1]
            for j in range(N_CHIPS):
                own = g_refs[i][pl.ds(pl.multiple_of((2 * j + c) * r, FLAT_ROWS), r), :]
                o_refs[i][j] = (own.astype(F32) + h_refs[i][j].astype(F32)).astype(o_refs[i].dtype)

    return pl.pallas_call(
        body, name=name, out_shape=[jax.ShapeDtypeStruct(h.shape, h.dtype) for h in halves],
        compiler_params=pltpu.CompilerParams(vmem_limit_bytes=VMEM_LIMIT),
    )(*grads, *halves)


class _Both:
    def __init__(self, first, second):
        self.jobs = (first, second)
        self.rels = first.rels | second.rels
        self.inputs = first.inputs + second.inputs
        self.out_shape = first.out_shape + second.out_shape
        self.sems = first.sems + second.sems

    def _each(self, ins, outs, sems):
        a = self.jobs[0]
        i, o, s = len(a.inputs), len(a.out_shape), len(a.sems)
        return ((a, ins[:i], outs[:o], sems[:s]), (self.jobs[1], ins[i:], outs[o:], sems[s:]))

    def before(self, ins, outs, sems):
        for job, i, o, s in self._each(ins, outs, sems):
            job.before(i, o, s)

    def after(self, ins, outs, sems):
        for job, i, o, s in self._each(ins, outs, sems):
            job.after(i, o, s)


_COLLECTIVE_IDS = {(1,): 0, (2, 4, 6): 1, (1, 2, 4, 6): 2}


def _entry_barrier(rels):
    @pl.when(pl.program_id(0) == 0)
    def _():
        me = _mesh_position()
        sem = pltpu.get_barrier_semaphore()
        for rel in rels:
            pl.semaphore_signal(sem, inc=1, device_id=_flip(me, rel), device_id_type=MESH)
        pl.semaphore_wait(sem, len(rels))


def _run(body, comm, *, semantics, out_shape, in_specs, out_specs, scratch_shapes=(), **kw):
    if comm is None:
        return pl.pallas_call(body, out_shape=out_shape, in_specs=in_specs, out_specs=out_specs,
                              scratch_shapes=list(scratch_shapes), compiler_params=_params(semantics), **kw)
    single = not isinstance(out_shape, (list, tuple))
    outs = [out_shape] if single else list(out_shape)
    ospecs = [out_specs] if single else list(out_specs)
    counts = [len(in_specs), len(comm.inputs), len(outs), len(comm.out_shape), len(scratch_shapes), len(comm.sems)]
    rels = tuple(sorted(comm.rels))

    def carrying(*refs):
        groups, pos = [], 0
        for c in counts:
            groups.append(refs[pos:pos + c])
            pos += c
        main_in, comm_in, main_out, comm_out, main_scratch, comm_sems = groups
        _entry_barrier(rels)
        comm.before(comm_in, comm_out, comm_sems)
        body(*main_in, *main_out, *main_scratch)
        comm.after(comm_in, comm_out, comm_sems)

    call = pl.pallas_call(
        carrying, out_shape=outs + list(comm.out_shape), in_specs=list(in_specs) + [_ANY] * len(comm.inputs),
        out_specs=ospecs + [_ANY] * len(comm.out_shape), scratch_shapes=list(scratch_shapes) + list(comm.sems),
        compiler_params=pltpu.CompilerParams(dimension_semantics=("arbitrary",), vmem_limit_bytes=VMEM_LIMIT,
                                             collective_id=_COLLECTIVE_IDS[rels]), **kw)

    def apply(*args):
        res = call(*args, *comm.inputs)
        main = res[:len(outs)]
        return (main[0] if single else list(main)), list(res[len(outs):])

    return apply


def _alone(comm, name):
    return _run(lambda: None, comm, semantics="arbitrary", name=name, grid=(1,), out_shape=[], in_specs=[], out_specs=[])()[1]


SHARDED = {"w_in": 1, "w_glu": 0, "conv_w": 1, "w_a_out": 1, "w_b_out": 0, "w_o": 0, "w_ffn_gate": 1, "w_ffn_up": 1,
           "w_ffn_down": 0, "w_ple_gate": 0, "w_ple": 1}
TRANSPOSED = ("w_in", "w_a_out", "w_ffn_gate", "w_ffn_up", "w_ple")
LONG_AXIS_MINOR = ("w_in", "w_ffn_gate", "w_ffn_up")
NARROW_LAST = ("s5_b_re", "s5_b_im", "s5_d")
ADAMW_GROUPS = (("w_in",), ("w_glu",), ("conv_w",), ("w_a_out",), ("w_b_out", "w_o", "w_ple_gate"),
                ("w_ffn_gate", "w_ffn_up"), ("w_ffn_down",), ("w_ple",))
SMALL = ["g_mix", "b_in", "lam_re", "lam_im", "log_dt", "s5_b_re", "s5_b_im", "s5_c_re", "s5_c_im", "s5_d", "b_glu",
         "conv_b", "w_r", "b_r", "w_i", "b_i", "lru_lambda", "g_ffn", "g_ple_gate", "b_ple_gate", "g_ple", "g_final"]
WEIGHTS = ["g_mix", "w_in", "b_in", "lam_re", "lam_im", "log_dt", "s5_b_re", "s5_b_im", "s5_c_re", "s5_c_im", "s5_d",
           "w_glu", "b_glu", "conv_w", "conv_b", "w_r", "b_r", "w_i", "b_i", "lru_lambda", "w_a_out", "w_b_out", "w_o",
           "g_ffn", "w_ffn_gate", "w_ffn_up", "w_ffn_down", "g_ple_gate", "w_ple_gate", "b_ple_gate", "w_ple", "g_ple",
           "g_final"]


def _join_columns(gathered):
    nb, r, c = gathered.shape
    return jnp.transpose(gathered, (1, 0, 2)).reshape(r, nb * c)


def _disc_scalars(lr, li, ldt):
    dt = jnp.exp(ldt)
    mag = jnp.exp(lr * dt)
    ar = mag * jnp.cos(li * dt)
    ai = mag * jnp.sin(li * dt)
    den = lr * lr + li * li
    nr = ar - 1.0
    fr = (nr * lr + ai * li) / den
    fi = (ai * lr - nr * li) / den
    return ar, ai, fr, fi


def _disc_cols(lr, li, ldt, b_re, b_im):
    ar, ai, fr, fi = _disc_scalars(lr, li, ldt)
    return ar, ai, fr * b_re - fi * b_im, fr * b_im + fi * b_re


def _local_step(x, p, target, src, small, disc, distributed=True):
    full = {} if distributed else dict(src)
    gw, halves, pairs, got = {}, {}, {}, {}

    def gather(keys):
        return (_Gather([src[k] for k in keys]), keys, full) if distributed else None

    def swap(keys):
        return (_PairSwap([gw[k] for k in keys]), keys, halves) if distributed else None

    def chips(keys):
        return (_ChipExchange([pairs[k] for k in keys]), keys, got) if distributed else None

    def add_pairs(keys):
        if distributed:
            pairs.update(zip(keys, _pair_add([gw[k] for k in keys], [halves[k] for k in keys], "pair_add_" + keys[0])))

    def carry(fn, *args, jobs=()):
        jobs = [j for j in jobs if j is not None]
        if not jobs:
            return fn(*args)
        comm = jobs[0][0]
        for j in jobs[1:]:
            comm = _Both(comm, j[0])
        res, extra = fn(*args, comm=comm)
        for job, keys, sink in jobs:
            sink.update(zip(keys, extra[:len(job.out_shape)]))
            extra = extra[len(job.out_shape):]
        return res

    d = x.shape[1]
    g, n, pch = small["s5_b_re"].shape
    sa, lw = g * pch, small["lru_lambda"].shape[-1]
    widths = [sa, lw, d, d]
    row = lambda v: v.reshape(1, -1)

    hd = small["w_r"].shape[-1]
    ar_row, ai_row, bbr_blk, bbi_blk, cre_blk, cimn_blk, wr_blk, wi_blk = carry(
        _prep, *disc["rows"], *disc["cols"], disc["b_re"], disc["b_im"], small["s5_c_re"].reshape(sa, n),
        small["s5_c_im"].reshape(sa, n), small["w_r"].reshape(lw, hd), small["w_i"].reshape(lw, hd),
        jobs=[gather(["w_in"])])
    d_row = row(small["s5_d"])
    g_mix_row = row(small["g_mix"])
    if distributed:
        later = ["w_glu", "conv_w", "w_a_out", "w_b_out", "w_ffn_gate", "w_ffn_up", "w_o", "w_ffn_down", "w_ple_gate",
                 "w_ple"]
        wires = [src[k] for k in later]
        flight, lands, token = _gather_start(wires, "gather_start", after=[ar_row])
        g_mix_row = g_mix_row + token[0, 0]

        def arrive(keys, after):
            which = [later.index(k) for k in keys]
            passed, moved = _gather_pass(wires, which, flight, lands, "gather_pass_" + keys[0], after)
            for q, i in enumerate(which):
                lands[i] = moved[q]
            full.update(zip(keys, _gather_finish(wires, which, flight, passed, lands, (), "gather_finish_" + keys[0])))
    else:
        arrive = lambda keys, after: None

    u_a, u_b, za, zb = _inproj_fwd(x, g_mix_row, full["w_in"], row(small["b_in"]), widths)
    arrive(["w_glu", "conv_w", "w_a_out", "w_b_out"], [u_a])
    conv_w = _join_columns(full["conv_w"]) if distributed else full["conv_w"]
    branches = [_s5_fwd(u_a, bbr_blk, bbi_blk, ar_row, ai_row, cre_blk, cimn_blk, d_row, full["w_glu"],
                        row(small["b_glu"])),
                _lru_fwd(u_b, conv_w, row(small["conv_b"]), wr_blk, row(small["b_r"]), wi_blk, row(small["b_i"]),
                         row(small["lru_lambda"]))]
    (sr, si, y, y_a), (xc, h, hprev) = _stages(branches, "branches_fwd")
    arrive(["w_ffn_gate", "w_ffn_up", "w_o"], [y_a])
    x1, merged, ma, mb, fg, fu = _merge_ffn_up_fwd(y_a, h, za, zb, x, full["w_a_out"], full["w_b_out"], full["w_o"],
                                                   row(small["g_ffn"]), full["w_ffn_gate"], full["w_ffn_up"])
    arrive(["w_ffn_down", "w_ple_gate", "w_ple"], [x1])
    x2 = _ffn_down_fwd(fg, fu, x1, full["w_ffn_down"])

    dx2, loss_blk, gw["w_ple_gate"], gw["w_ple"], vec_tail = _tail_fwd_bwd(
        x2, p, target, row(small["g_ple_gate"]), full["w_ple_gate"], row(small["b_ple_gate"]), full["w_ple"],
        row(small["g_ple"]), row(small["g_final"]))
    dfg, dfu, act = carry(_ffn_bwd_a, dx2, fg, fu, full["w_ffn_down"], jobs=[swap(["w_ple_gate", "w_ple"])])
    tn_d = min(d, 512)
    gw["w_ffn_down"] = _matmul_tn(act, dx2, tn_d, "dw_ffn_down", BF16)
    add_pairs(["w_ple_gate", "w_ple"])
    dx1, h2, vec_ffn = carry(_ffn_bwd_b, dfg, dfu, x1, dx2, row(small["g_ffn"]), full["w_ffn_gate"], full["w_ffn_up"],
                             jobs=[chips(["w_ple_gate", "w_ple"]), swap(["w_ffn_down"])])
    gw["w_ffn_gate"] = _matmul_tn(dfg, h2, tn_d, "dw_ffn_gate", BF16)
    gw["w_ffn_up"] = _matmul_tn(dfu, h2, tn_d, "dw_ffn_up", BF16)
    add_pairs(["w_ffn_down"])
    dza, dzb, dya, dyb, gw["w_o"], gw["w_a_out"], gw["w_b_out"] = carry(
        _merge_bwd, dx1, merged, ma, mb, za, zb, y_a, h, full["w_o"], full["w_a_out"], full["w_b_out"],
        jobs=[chips(["w_ffn_down"]), swap(["w_ffn_gate", "w_ffn_up"])])
    add_pairs(["w_ffn_gate", "w_ffn_up"])
    own = {}
    du_b, dw_r, dw_i, vec_lru = carry(
        _lru_bwd, dyb, xc, hprev, u_b, conv_w, wr_blk, row(small["b_r"]), wi_blk, row(small["b_i"]),
        row(small["lru_lambda"]), hd, jobs=[chips(["w_ffn_gate"]), swap(["w_o", "w_a_out", "w_b_out"])])
    add_pairs(["w_o", "w_a_out", "w_b_out"])
    du_a, lam_r, lam_i, dy16, gw["w_glu"], vec_sa, vec_gn = carry(
        _s5_bwd, dya, y, sr, si, u_a, full["w_glu"], row(small["b_glu"]), cre_blk, cimn_blk, bbr_blk, bbi_blk, ar_row,
        ai_row, d_row, jobs=[chips(["w_ffn_up"]), chips(["w_o", "w_a_out", "w_b_out"])])
    smalls = {"vec_tail": vec_tail, "vec_ffn": vec_ffn, "vec_lru": vec_lru, "vec_sa": vec_sa, "vec_gn": vec_gn,
              "dw_r": dw_r, "dw_i": dw_i, "loss": loss_blk}
    everyones = {}

    def gather_smalls(keys):
        return (_Gather([smalls[k] for k in keys]), keys, everyones) if distributed else None

    smalls["dbb_re"], smalls["dbb_im"], smalls["dc_re"], smalls["dc_imn"] = carry(
        _s5_param_grads, lam_r, lam_i, sr, si, u_a, dy16, pch, n, jobs=[swap(["w_glu"]), gather_smalls(list(smalls))])
    add_pairs(["w_glu"])
    dz = [du_a, du_b, dza, dzb]
    grad_x, h0, smalls["vec_mix"], smalls["vec_bin"] = _inproj_bwd(
        dz, x, dx1, row(small["g_mix"]), full["w_in"])
    shapes = {k: a.shape for k, a in smalls.items()}
    gw["w_in"] = carry(_dw_from_parts, dz, h0, tn_d, "dw_in",
                       jobs=[chips(["w_glu"]),
                             gather_smalls(["dbb_re", "dbb_im", "dc_re", "dc_imn", "vec_mix", "vec_bin"])])
    smalls.update(everyones)
    if distributed:
        got["w_in"] = gw["w_in"]
        gw = got
    return grad_x, gw, smalls, shapes, own


def _disc_inputs(small):
    g, n, pch = small["s5_b_re"].shape
    srcs = (small["lam_re"], small["lam_im"], jnp.repeat(small["log_dt"], n))
    return {"rows": [a.reshape(1, g * n) for a in srcs], "cols": [a.reshape(g * n, 1) for a in srcs],
            "b_re": small["s5_b_re"].reshape(g * n, pch), "b_im": small["s5_b_im"].reshape(g * n, pch)}


def kernel(x, p, g_mix, w_in, b_in, lam_re, lam_im, log_dt, s5_b_re, s5_b_im, s5_c_re, s5_c_im, s5_d, w_glu, b_glu, conv_w, conv_b, w_r, b_r, w_i, b_i, lru_lambda, w_a_out, w_b_out, w_o, g_ffn, w_ffn_gate, w_ffn_up, w_ffn_down, g_ple_gate, w_ple_gate, b_ple_gate, w_ple, g_ple, g_final, loss_target, m_g_mix, m_w_in, m_b_in, m_lam_re, m_lam_im, m_log_dt, m_s5_b_re, m_s5_b_im, m_s5_c_re, m_s5_c_im, m_s5_d, m_w_glu, m_b_glu, m_conv_w, m_conv_b, m_w_r, m_b_r, m_w_i, m_b_i, m_lru_lambda, m_w_a_out, m_w_b_out, m_w_o, m_g_ffn, m_w_ffn_gate, m_w_ffn_up, m_w_ffn_down, m_g_ple_gate, m_w_ple_gate, m_b_ple_gate, m_w_ple, m_g_ple, m_g_final, v_g_mix, v_w_in, v_b_in, v_lam_re, v_lam_im, v_log_dt, v_s5_b_re, v_s5_b_im, v_s5_c_re, v_s5_c_im, v_s5_d, v_w_glu, v_b_glu, v_conv_w, v_conv_b, v_w_r, v_b_r, v_w_i, v_b_i, v_lru_lambda, v_w_a_out, v_w_b_out, v_w_o, v_g_ffn, v_w_ffn_gate, v_w_ffn_up, v_w_ffn_down, v_g_ple_gate, v_w_ple_gate, v_b_ple_gate, v_w_ple, v_g_ple, v_g_final):
    given = dict(locals())
    wts = {k: given[k] for k in WEIGHTS}
    moms = {k: given["m_" + k] for k in WEIGHTS}
    vels = {k: given["v_" + k] for k in WEIGHTS}

    def drop_depth(k, a):
        return a if k == "g_final" else a[0]

    small = {k: drop_depth(k, wts[k]) for k in SMALL}
    shard = {k: wts[k][0] for k in SHARDED}
    names = list(SHARDED)

    def wire(k):
        if k == "conv_w":
            return shard[k]
        return (shard[k].T if k in TRANSPOSED else shard[k]).astype(BF16)

    disc = _disc_inputs(small)
    grad_x, parts, smalls, shapes, own = _local_step(x[0], p[0, 0], loss_target[0], {k: wire(k) for k in names}, small,
                                                     disc)

    (pair_in,) = _pair_add([parts["w_in"]], _alone(_PairSwap([parts["w_in"]]), "swap_last"), "pair_add_w_in")
    send_sems, recv_sems, pair_in, landing, token = _chips_start(pair_in, "w_in")
    ordered = (token,)

    g_small = _small_reduce(smalls, shapes, *disc["cols"], disc["b_re"], disc["b_im"], small["s5_b_re"].shape[0],
                            after=ordered)
    loss = g_small.pop("loss")[0, 0]
    cols = shard["conv_w"].shape[1]
    mine = _index((lax.axis_index("x"), lax.axis_index("y"), lax.axis_index("c")))
    parts["conv_w"] = lax.dynamic_slice_in_dim(g_small.pop("conv_w"), mine * cols, cols, axis=1)[None]

    def view(k, a):
        a = a.reshape((1, -1) if k == "g_final" else wts[k].shape)
        return jnp.swapaxes(a, -1, -2) if k in NARROW_LAST else a

    def unview(k, a):
        return (jnp.swapaxes(a, -1, -2) if k in NARROW_LAST else a).reshape(wts[k].shape)

    slots = _adamw_small([view(k, wts[k]) for k in SMALL], [view(k, g_small[k]) for k in SMALL],
                         [view(k, moms[k]) for k in SMALL], [view(k, vels[k]) for k in SMALL])
    small_out = [dict(zip(SMALL, [unview(k, a) for k, a in zip(SMALL, slot)])) for slot in slots]

    big_out = {}
    between = [slots[0][0]]
    for group in sorted(ADAMW_GROUPS, key=lambda g: g[0] == "w_in"):
        flip = group[0] in LONG_AXIS_MINOR
        look = (lambda a: a.T) if flip else (lambda a: a)
        last = group[0] == "w_in"
        if last:
            own["w_in"], parts["w_in"] = _chips_wait(send_sems, recv_sems, pair_in, landing, between, "w_in")
        res = _adamw([parts[k] for k in group], [look(shard[k]) for k in group], [look(moms[k][0]) for k in group],
                     [look(vels[k][0]) for k in group], "adamw_" + group[0],
                     transposed=group[0] in TRANSPOSED and not flip,
                     own=[own[k] for k in group] if group[0] in own else None, after=() if last else ordered)
        between.append(res[0][0])
        for k, outs4 in zip(group, res):
            big_out[k] = [look(a) for a in outs4]

    outs = [loss, grad_x[None]]
    for slot in range(4):
        for k in WEIGHTS:
            if k in SHARDED:
                outs.append(big_out[k][slot][None])
            else:
                outs.append(small_out[slot][k])
    return tuple(outs)
```

```python
import math

import jax
import jax.numpy as jnp
from jax import lax
from jax.experimental import pallas as pl
from jax.experimental.pallas import tpu as pltpu

F32 = jnp.float32
BF16 = jnp.bfloat16

EPS = 1e-6
LRU_C = 8.0
CONV_WIDTH = 4
ADAM_LR = 0.001
ADAM_B1 = 0.9
ADAM_B2 = 0.999
ADAM_EPS = 1e-08
ADAM_WD = 0.01
ADAM_STEP = 10

N_DEV = 8
MESH = pl.DeviceIdType.MESH
SUBLANES = 8
LANES = 128
VMEM_LIMIT = 56 * 1024 * 1024
TOKEN_TILE = 256
TIME_CHUNK = 256
S5_SLAB = 128
LRU_SLAB = 256


def _dot(a, b):
    return jnp.dot(a.astype(BF16), b.astype(BF16), preferred_element_type=F32)


def _dot_nt(a, b):
    return lax.dot_general(a.astype(BF16), b.astype(BF16), (((1,), (1,)), ((), ())), preferred_element_type=F32)


def _dot_tn(a, b):
    return lax.dot_general(a.astype(BF16), b.astype(BF16), (((0,), (0,)), ((), ())), preferred_element_type=F32)


def _sigmoid(x):
    return jax.nn.sigmoid(x)


def _rms_stats(x):
    r = lax.rsqrt(jnp.mean(x * x, axis=-1, keepdims=True) + EPS)
    return x * r, r


def _rms_bwd(dy, xhat, r, g):
    dxn = dy * g
    dx = r * (dxn - xhat * jnp.mean(dxn * xhat, axis=-1, keepdims=True))
    return dx, dy * xhat


def _rowsum(v):
    return jnp.sum(v, axis=0, keepdims=True)


def _expm1(x):
    u = jnp.exp(x)
    um1 = u - 1.0
    safe = jnp.where(um1 == 0.0, 1.0, jnp.log(u))
    return jnp.where(um1 == 0.0, x, um1 * x / safe)


def _softplus(x):
    e = jnp.exp(-jnp.abs(x))
    u = 1.0 + e
    um1 = u - 1.0
    safe = jnp.where(um1 == 0.0, 1.0, um1)
    log1p_e = jnp.where(um1 == 0.0, e, jnp.log(u) * e / safe)
    return jnp.maximum(x, 0.0) + log1p_e


_GELU_K = math.sqrt(2.0 / math.pi)
_GELU_C = 0.044715


def _gelu(x):
    return 0.5 * x * (1.0 + jnp.tanh(_GELU_K * (x + _GELU_C * x * x * x)))


def _gelu_grad(x):
    th = jnp.tanh(_GELU_K * (x + _GELU_C * x * x * x))
    return 0.5 * (1.0 + th) + 0.5 * x * (1.0 - th * th) * _GELU_K * (1.0 + 3.0 * _GELU_C * x * x)


def _params(*sem):
    return pltpu.CompilerParams(dimension_semantics=sem, vmem_limit_bytes=VMEM_LIMIT)


def _rows(tm, n):
    return pl.BlockSpec((tm, n), lambda i: (i, 0))


def _rows_rev(tm, n, steps):
    return pl.BlockSpec((tm, n), lambda i: (steps - 1 - i, 0))


def _whole(shape):
    nd = len(shape)
    return pl.BlockSpec(shape, lambda i: (0,) * nd, pipeline_mode=pl.Buffered(1))


def _acc(shape):
    nd = len(shape)
    return pl.BlockSpec(shape, lambda i: (0,) * nd)


def _zero_on_first(*refs):
    @pl.when(pl.program_id(0) == 0)
    def _():
        for r in refs:
            r[...] = jnp.zeros_like(r)


def _inproj_fwd(x, g_mix, w_in_t, b_in, widths, comm=None):
    t, d = x.shape
    n = w_in_t.shape[0]
    tm = min(TOKEN_TILE, t)
    offs = [sum(widths[:i]) for i in range(len(widths) + 1)]

    def body(x_ref, g_ref, w_ref, b_ref, *outs):
        xhat, _ = _rms_stats(x_ref[...])
        h = (xhat * g_ref[...]).astype(BF16)
        for k, o_ref in enumerate(outs):
            lo, hi = offs[k], offs[k + 1]
            o_ref[...] = _dot_nt(h, w_ref[lo:hi, :]) + b_ref[:, lo:hi]

    return _run(
        body, comm, name="inproj_fwd", grid=(t // tm,),
        out_shape=[jax.ShapeDtypeStruct((t, w), F32) for w in widths],
        in_specs=[_rows(tm, d), _whole((1, d)), _whole((n, d)), _whole((1, n))],
        out_specs=[_rows(tm, w) for w in widths],
        semantics="parallel",
    )(x, g_mix, w_in_t, b_in)


def _s5_fwd(u, bbr_blk, bbi_blk, ar, ai, cre_blk, cimn_blk, d_skip, w_glu, b_glu):
    t, sa = u.shape
    ns, _, sw = bbr_blk.shape
    gn = ns * sw
    tc = min(TIME_CHUNK, t)

    def body(u_ref, bbr_ref, bbi_ref, ar_ref, ai_ref, cre_ref, cim_ref, d_ref, wg_ref, bg_ref,
             sr_ref, si_ref, y_ref, ya_ref, cr_s, ci_s, sr_s, si_s):
        _zero_on_first(cr_s, ci_s)
        uv = u_ref[...]
        ub = uv.astype(BF16)
        for m in range(ns):
            um = ub[:, m * S5_SLAB:(m + 1) * S5_SLAB]
            sr_s[:, m * sw:(m + 1) * sw] = _dot(um, bbr_ref[m])
            si_s[:, m * sw:(m + 1) * sw] = _dot(um, bbi_ref[m])
        a_r = ar_ref[...]
        a_i = ai_ref[...]

        def step(row, carry):
            c_r, c_i = carry
            at = pl.ds(row, 1)
            n_r = a_r * c_r - a_i * c_i + sr_s[at, :]
            n_i = a_r * c_i + a_i * c_r + si_s[at, :]
            sr_s[at, :] = n_r
            si_s[at, :] = n_i
            return n_r, n_i

        c_r, c_i = lax.fori_loop(0, tc, step, (cr_s[0:1, :], ci_s[0:1, :]), unroll=8)
        cr_s[0:1, :] = c_r
        ci_s[0:1, :] = c_i
        for m in range(ns):
            states, chans = slice(m * sw, (m + 1) * sw), slice(m * S5_SLAB, (m + 1) * S5_SLAB)
            s_r, s_i = sr_s[:, states].astype(BF16), si_s[:, states].astype(BF16)
            sr_ref[:, states] = s_r.astype(sr_ref.dtype)
            si_ref[:, states] = s_i.astype(si_ref.dtype)
            y_ref[:, chans] = _dot(s_r, cre_ref[m]) + _dot(s_i, cim_ref[m]) + d_ref[:, chans] * uv[:, chans]
        y = y_ref[...]
        zz = _gelu(y)
        q = _dot(zz, wg_ref[...]) + bg_ref[...]
        ya_ref[...] = zz * _sigmoid(q)

    return dict(
        body=body, steps=t // tc, args=(u, bbr_blk, bbi_blk, ar, ai, cre_blk, cimn_blk, d_skip, w_glu, b_glu),
        out_shape=[jax.ShapeDtypeStruct((t, gn), BF16), jax.ShapeDtypeStruct((t, gn), BF16),
                   jax.ShapeDtypeStruct((t, sa), F32), jax.ShapeDtypeStruct((t, sa), F32)],
        in_specs=[_rows(tc, sa), _whole(bbr_blk.shape), _whole(bbi_blk.shape), _whole((1, gn)), _whole((1, gn)),
                  _whole(cre_blk.shape), _whole(cimn_blk.shape), _whole((1, sa)), _whole((sa, sa)), _whole((1, sa))],
        out_specs=[_rows(tc, gn), _rows(tc, gn), _rows(tc, sa), _rows(tc, sa)],
        scratch=[pltpu.VMEM((SUBLANES, gn), F32), pltpu.VMEM((SUBLANES, gn), F32),
                 pltpu.VMEM((tc, gn), F32), pltpu.VMEM((tc, gn), F32)])


def _stages(stages, name, comm=None):
    counts = [[len(s[k]) for s in stages] for k in ("args", "out_shape", "scratch")]

    def body(*refs):
        groups, pos = [], 0
        for kind in counts:
            per_stage = []
            for c in kind:
                per_stage.append(refs[pos:pos + c])
                pos += c
            groups.append(per_stage)
        for s, ins, outs, scratch in zip(stages, *groups):
            s["body"](*ins, *outs, *scratch)

    res = _run(
        body, comm, name=name, grid=(stages[0]["steps"],),
        out_shape=[o for s in stages for o in s["out_shape"]], in_specs=[i for s in stages for i in s["in_specs"]],
        out_specs=[o for s in stages for o in s["out_specs"]], scratch_shapes=[c for s in stages for c in s["scratch"]],
        semantics="arbitrary",
    )(*[a for s in stages for a in s["args"]])
    extra = None
    if comm is not None:
        res, extra = res
    per_stage, pos = [], 0
    for c in counts[1]:
        per_stage.append(list(res[pos:pos + c]))
        pos += c
    return per_stage if comm is None else (per_stage, extra)


def _slab_dot(x, w_ref, transposed=False):
    dot = _dot_nt if transposed else _dot
    xb = x.astype(BF16)
    return jnp.concatenate([dot(xb[:, j * LRU_SLAB:(j + 1) * LRU_SLAB], w_ref[j]) for j in range(w_ref.shape[0])], axis=1)


def _lru_gates(xc, wr_ref, br_ref, wi_ref, bi_ref, lam_ref):
    r = _sigmoid(_slab_dot(xc, wr_ref) + br_ref[...])
    ig = _sigmoid(_slab_dot(xc, wi_ref) + bi_ref[...])
    sp = _softplus(-lam_ref[...])
    log_a = (-LRU_C * r) * sp
    return r, ig, sp, log_a


def _lru_fwd(u, conv_w, conv_b, wr_blk, b_r, wi_blk, b_i, lru_lambda):
    t, w = u.shape
    tc = min(TIME_CHUNK, t)
    halo = SUBLANES

    def body(u_ref, cw_ref, cb_ref, wr_ref, br_ref, wi_ref, bi_ref, lam_ref,
             xc_ref, h_ref, hp_ref, ext_s, a_s, carry_s):
        @pl.when(pl.program_id(0) == 0)
        def _():
            ext_s[0:halo, :] = jnp.zeros((halo, w), F32)
            carry_s[...] = jnp.zeros_like(carry_s)

        ext_s[halo:halo + tc, :] = u_ref[...]
        xc = cb_ref[...]
        for k in range(CONV_WIDTH):
            off = halo - (CONV_WIDTH - 1) + k
            xc = xc + cw_ref[k:k + 1, :] * ext_s[off:off + tc, :]
        ext_s[0:halo, :] = ext_s[tc:tc + halo, :]
        xc_ref[...] = xc
        r, ig, sp, log_a = _lru_gates(xc, wr_ref, br_ref, wi_ref, bi_ref, lam_ref)
        a_s[...] = jnp.exp(log_a)
        h_ref[...] = jnp.sqrt(-_expm1(2.0 * log_a)) * ig * xc

        def step(row, carry):
            at = pl.ds(row, 1)
            hp_ref[at, :] = carry
            nxt = a_s[at, :] * carry + h_ref[at, :]
            h_ref[at, :] = nxt
            return nxt

        carry_s[0:1, :] = lax.fori_loop(0, tc, step, carry_s[0:1, :], unroll=8)

    return dict(
        body=body, steps=t // tc, args=(u, conv_w, conv_b, wr_blk, b_r, wi_blk, b_i, lru_lambda),
        out_shape=[jax.ShapeDtypeStruct((t, w), F32)] * 3,
        in_specs=[_rows(tc, w), _whole((CONV_WIDTH, w)), _whole((1, w)), _whole(wr_blk.shape), _whole((1, w)),
                  _whole(wi_blk.shape), _whole((1, w)), _whole((1, w))],
        out_specs=[_rows(tc, w)] * 3,
        scratch=[pltpu.VMEM((halo + tc, w), F32), pltpu.VMEM((tc, w), F32), pltpu.VMEM((SUBLANES, w), F32)])


def _merge_ffn_up_fwd(y_a, h, za, zb, x, w_a_out_t, w_b_out, w_o, g_ffn, w_gate_t, w_up_t, comm=None):
    t, d = x.shape
    sa, lw = y_a.shape[1], h.shape[1]
    f = w_gate_t.shape[0]
    tm = min(TOKEN_TILE, t)

    def body(ya_ref, h_ref, za_ref, zb_ref, x_ref, wa_ref, wb_ref, wo_ref, g_ref, wg_ref, wu_ref,
             x1_ref, mg_ref, ma_ref, mb_ref, fg_ref, fu_ref):
        ma = _dot_nt(ya_ref[...], wa_ref[...])
        mb = _dot(h_ref[...], wb_ref[...])
        merged = _sigmoid(za_ref[...]) * ma + _sigmoid(zb_ref[...]) * mb
        ma_ref[...] = ma.astype(ma_ref.dtype)
        mb_ref[...] = mb.astype(mb_ref.dtype)
        mg_ref[...] = merged.astype(mg_ref.dtype)
        x1 = x_ref[...] + _dot(merged, wo_ref[...])
        x1_ref[...] = x1
        xhat, _ = _rms_stats(x1)
        h2 = (xhat * g_ref[...]).astype(BF16)
        fg_ref[...] = _dot_nt(h2, wg_ref[...]).astype(fg_ref.dtype)
        fu_ref[...] = _dot_nt(h2, wu_ref[...]).astype(fu_ref.dtype)

    return _run(
        body, comm, name="merge_ffn_up_fwd", grid=(t // tm,),
        out_shape=[jax.ShapeDtypeStruct((t, d), F32), jax.ShapeDtypeStruct((t, d), BF16),
                   jax.ShapeDtypeStruct((t, d), BF16), jax.ShapeDtypeStruct((t, d), BF16),
                   jax.ShapeDtypeStruct((t, f), BF16), jax.ShapeDtypeStruct((t, f), BF16)],
        in_specs=[_rows(tm, sa), _rows(tm, lw), _rows(tm, d), _rows(tm, d), _rows(tm, d),
                  _whole((d, sa)), _whole((lw, d)), _whole((d, d)), _whole((1, d)), _whole((f, d)), _whole((f, d))],
        out_specs=[_rows(tm, d)] * 4 + [_rows(tm, f)] * 2,
        semantics="parallel",
    )(y_a, h, za, zb, x, w_a_out_t, w_b_out, w_o, g_ffn, w_gate_t, w_up_t)


def _ffn_down_fwd(fg, fu, x1, w_down, comm=None):
    t, d = x1.shape
    f = fg.shape[1]
    tm = min(TOKEN_TILE, t)

    def body(fg_ref, fu_ref, x_ref, wd_ref, x2_ref):
        fgv = fg_ref[...].astype(F32)
        act = fgv * _sigmoid(fgv) * fu_ref[...].astype(F32)
        x2_ref[...] = x_ref[...] + _dot(act, wd_ref[...])

    return _run(
        body, comm, name="ffn_down_fwd", grid=(t // tm,),
        out_shape=jax.ShapeDtypeStruct((t, d), F32),
        in_specs=[_rows(tm, f), _rows(tm, f), _rows(tm, d), _whole((f, d))],
        out_specs=_rows(tm, d),
        semantics="parallel",
    )(fg, fu, x1, w_down)


def _store_on_last(pairs):
    @pl.when(pl.program_id(0) == pl.num_programs(0) - 1)
    def _():
        for acc, out in pairs:
            out[...] = acc[...].astype(out.dtype)


def _tail_fwd_bwd(x2, p, target, g_pg, w_pg, b_pg, w_ple_t, g_ple, g_final):
    t, d = x2.shape
    pd = p.shape[1]
    tm = min(TOKEN_TILE, t)

    def body(x2_ref, p_ref, tg_ref, gpg_ref, wpg_ref, bpg_ref, wple_ref, gple_ref, gfin_ref,
             dx2_ref, loss_ref, dwpg_out, dwple_out, vec_ref, dwpg_ref, dwple_ref):
        _zero_on_first(loss_ref, dwpg_ref, dwple_ref, vec_ref)
        x2v = x2_ref[...]
        xh2, r2 = _rms_stats(x2v)
        h3 = xh2 * gpg_ref[...]
        gp = _sigmoid(_dot(h3, wpg_ref[...]) + bpg_ref[...])
        pe = _dot_nt(p_ref[...], wple_ref[...])
        peh, r3 = _rms_stats(pe)
        e = peh * gple_ref[...]
        x3 = x2v + gp * e
        xh3, r4 = _rms_stats(x3)
        diff = xh3 * gfin_ref[...] - tg_ref[...]
        loss_ref[...] += 0.5 * jnp.sum(jnp.mean(diff * diff, axis=-1, keepdims=True))
        dy = diff * (1.0 / d)
        dx3, dgfin = _rms_bwd(dy, xh3, r4, gfin_ref[...])
        d_gp = dx3 * e
        d_e = dx3 * gp
        dpe, dgple = _rms_bwd(d_e, peh, r3, gple_ref[...])
        dwple_ref[...] += _dot_tn(dpe, p_ref[...])
        dpre = d_gp * gp * (1.0 - gp)
        dwpg_ref[...] += _dot_tn(h3, dpre)
        dh3 = _dot_nt(dpre, wpg_ref[...])
        dx2n, dgpg = _rms_bwd(dh3, xh2, r2, gpg_ref[...])
        dx2_ref[...] = dx3 + dx2n
        vec_ref[0:1, :] += _rowsum(dpre)
        vec_ref[1:2, :] += _rowsum(dgpg)
        vec_ref[2:3, :] += _rowsum(dgple)
        vec_ref[3:4, :] += _rowsum(dgfin)
        _store_on_last([(dwpg_ref, dwpg_out), (dwple_ref, dwple_out)])

    return pl.pallas_call(
        body, name="tail_fwd_bwd", grid=(t // tm,),
        out_shape=[jax.ShapeDtypeStruct((t, d), F32), jax.ShapeDtypeStruct((SUBLANES, LANES), F32),
                   jax.ShapeDtypeStruct((d, d), BF16), jax.ShapeDtypeStruct((d, pd), BF16),
                   jax.ShapeDtypeStruct((SUBLANES, d), F32)],
        in_specs=[_rows(tm, d), _rows(tm, pd), _rows(tm, d), _whole((1, d)), _whole((d, d)), _whole((1, d)),
                  _whole((d, pd)), _whole((1, d)), _whole((1, d))],
        out_specs=[_rows(tm, d), _acc((SUBLANES, LANES)), _acc((d, d)), _acc((d, pd)), _acc((SUBLANES, d))],
        scratch_shapes=[pltpu.VMEM((d, d), F32), pltpu.VMEM((d, pd), F32)],
        compiler_params=_params("arbitrary"),
    )(x2, p, target, g_pg, w_pg, b_pg, w_ple_t, g_ple, g_final)


def _ffn_bwd_a(dx2, fg, fu, w_down, comm=None):
    t, d = dx2.shape
    f = fg.shape[1]
    tm = min(TOKEN_TILE, t)

    def body(dx_ref, fg_ref, fu_ref, wd_ref, dfg_ref, dfu_ref, act_ref):
        dact = _dot_nt(dx_ref[...], wd_ref[...])
        fgv = fg_ref[...].astype(F32)
        fuv = fu_ref[...].astype(F32)
        sg = _sigmoid(fgv)
        silu = fgv * sg
        dfu_ref[...] = (dact * silu).astype(dfu_ref.dtype)
        dfg_ref[...] = (dact * fuv * (sg * (1.0 + fgv * (1.0 - sg)))).astype(dfg_ref.dtype)
        act_ref[...] = (silu * fuv).astype(act_ref.dtype)

    return _run(
        body, comm, name="ffn_bwd_a", grid=(t // tm,),
        out_shape=[jax.ShapeDtypeStruct((t, f), BF16)] * 3,
        in_specs=[_rows(tm, d), _rows(tm, f), _rows(tm, f), _whole((f, d))],
        out_specs=[_rows(tm, f)] * 3,
        semantics="parallel",
    )(dx2, fg, fu, w_down)


def _ffn_bwd_b(dfg, dfu, x1, dx2, g_ffn, w_gate_t, w_up_t, comm=None):
    t, d = x1.shape
    f = dfg.shape[1]
    tm = min(TOKEN_TILE, t)

    def body(dfg_ref, dfu_ref, x_ref, dx2_ref, g_ref, wg_ref, wu_ref, dx1_ref, h2_ref, vec_ref):
        _zero_on_first(vec_ref)
        dh2 = _dot(dfg_ref[...], wg_ref[...]) + _dot(dfu_ref[...], wu_ref[...])
        xhat, r = _rms_stats(x_ref[...])
        h2_ref[...] = (xhat * g_ref[...]).astype(h2_ref.dtype)
        dxn, dg = _rms_bwd(dh2, xhat, r, g_ref[...])
        dx1_ref[...] = dx2_ref[...] + dxn
        vec_ref[0:1, :] += _rowsum(dg)

    return _run(
        body, comm, name="ffn_bwd_b", grid=(t // tm,),
        out_shape=[jax.ShapeDtypeStruct((t, d), F32), jax.ShapeDtypeStruct((t, d), BF16),
                   jax.ShapeDtypeStruct((SUBLANES, d), F32)],
        in_specs=[_rows(tm, f), _rows(tm, f), _rows(tm, d), _rows(tm, d), _whole((1, d)), _whole((f, d)), _whole((f, d))],
        out_specs=[_rows(tm, d), _rows(tm, d), _acc((SUBLANES, d))],
        semantics="arbitrary",
    )(dfg, dfu, x1, dx2, g_ffn, w_gate_t, w_up_t)


def _matmul_tn(a, b, tn, name, dtype=F32):
    t, k = a.shape
    n = b.shape[1]

    def body(a_ref, b_ref, o_ref):
        o_ref[...] = _dot_tn(a_ref[...], b_ref[...]).astype(o_ref.dtype)

    return _run(
        body, None, name=name, grid=(n // tn,),
        out_shape=jax.ShapeDtypeStruct((k, n), dtype),
        in_specs=[_whole((t, k)), pl.BlockSpec((t, tn), lambda j: (0, j))],
        out_specs=pl.BlockSpec((k, tn), lambda j: (0, j)),
        semantics="parallel",
    )(a, b)


def _merge_bwd(dx1, merged, ma, mb, za, zb, y_a, h, w_o, w_a_out_t, w_b_out, comm=None):
    t, d = dx1.shape
    sa, lw = y_a.shape[1], h.shape[1]
    tm = min(TOKEN_TILE, t)

    def body(dx1_ref, mg_ref, ma_ref, mb_ref, za_ref, zb_ref, ya_ref, h_ref, wo_ref, wa_ref, wb_ref,
             dza_ref, dzb_ref, dya_ref, dyb_ref, dwo_out, dwa_out, dwb_out, dwo_ref, dwa_ref, dwb_ref):
        _zero_on_first(dwo_ref, dwa_ref, dwb_ref)
        dx1v = dx1_ref[...].astype(BF16)
        dmg = _dot_nt(dx1v, wo_ref[...])
        ga = _sigmoid(za_ref[...])
        gb = _sigmoid(zb_ref[...])
        dza_ref[...] = (dmg * ma_ref[...].astype(F32) * ga * (1.0 - ga)).astype(dza_ref.dtype)
        dzb_ref[...] = (dmg * mb_ref[...].astype(F32) * gb * (1.0 - gb)).astype(dzb_ref.dtype)
        dma = (dmg * ga).astype(BF16)
        dmb = (dmg * gb).astype(BF16)
        dya_ref[...] = _dot(dma, wa_ref[...])
        dyb_ref[...] = _dot_nt(dmb, wb_ref[...])
        dwo_ref[...] += _dot_tn(mg_ref[...], dx1v)
        dwa_ref[...] += _dot_tn(dma, ya_ref[...])
        dwb_ref[...] += _dot_tn(h_ref[...], dmb)
        _store_on_last([(dwo_ref, dwo_out), (dwa_ref, dwa_out), (dwb_ref, dwb_out)])

    return _run(
        body, comm, name="merge_bwd", grid=(t // tm,),
        out_shape=[jax.ShapeDtypeStruct((t, d), BF16), jax.ShapeDtypeStruct((t, d), BF16),
                   jax.ShapeDtypeStruct((t, sa), F32), jax.ShapeDtypeStruct((t, lw), F32),
                   jax.ShapeDtypeStruct((d, d), BF16), jax.ShapeDtypeStruct((d, sa), BF16),
                   jax.ShapeDtypeStruct((lw, d), BF16)],
        in_specs=[_rows(tm, d), _rows(tm, d), _rows(tm, d), _rows(tm, d), _rows(tm, d), _rows(tm, d),
                  _rows(tm, sa), _rows(tm, lw), _whole((d, d)), _whole((d, sa)), _whole((lw, d))],
        out_specs=[_rows(tm, d), _rows(tm, d), _rows(tm, sa), _rows(tm, lw), _acc((d, d)), _acc((d, sa)), _acc((lw, d))],
        scratch_shapes=[pltpu.VMEM((d, d), F32), pltpu.VMEM((d, sa), F32), pltpu.VMEM((lw, d), F32)],
        semantics="arbitrary",
    )(dx1, merged, ma, mb, za, zb, y_a, h, w_o, w_a_out_t, w_b_out)


def _fold_diag_blocks(dense, row_group, col_group):
    r, c = dense.shape
    rows = lax.broadcasted_iota(jnp.int32, (r, c), 0)
    cols = lax.broadcasted_iota(jnp.int32, (r, c), 1)
    kept = jnp.where(rows // row_group == cols // col_group, dense, 0.0)
    pick = (lax.broadcasted_iota(jnp.int32, (row_group, r), 0)
            == lax.broadcasted_iota(jnp.int32, (row_group, r), 1) % row_group).astype(F32)
    return jnp.dot(pick, kept, preferred_element_type=F32, precision=lax.Precision.HIGHEST)


def _lru_bwd(dh, xc, hprev, u, conv_w, wr_blk, b_r, wi_blk, b_i, lru_lambda, head_dim, comm=None):
    t, w = dh.shape
    tc = min(TIME_CHUNK, t)
    steps = t // tc
    halo = SUBLANES
    sub_per_chunk = tc // halo
    slabs = w // LRU_SLAB

    def body(dh_ref, xc_ref, hp_ref, u_ref, uh_ref, cw_ref, wr_ref, br_ref, wi_ref, bi_ref, lam_ref,
             du_ref, dwr_out, dwi_out, vec_ref, lam_s, a_s, dxc_s, uext_s, carry_s, dwr_ref, dwi_ref):
        chunk = steps - 1 - pl.program_id(0)

        @pl.when(pl.program_id(0) == 0)
        def _():
            carry_s[...] = jnp.zeros_like(carry_s)
            dxc_s[tc:tc + halo, :] = jnp.zeros((halo, w), F32)
            dwr_ref[...] = jnp.zeros_like(dwr_ref)
            dwi_ref[...] = jnp.zeros_like(dwi_ref)
            vec_ref[...] = jnp.zeros_like(vec_ref)

        xc = xc_ref[...]
        r, ig, sp, log_a = _lru_gates(xc, wr_ref, br_ref, wi_ref, bi_ref, lam_ref)
        a = jnp.exp(log_a)
        a_s[...] = a

        def step(i, q):
            at = pl.ds(tc - 1 - i, 1)
            lam_row = dh_ref[at, :] + q
            lam_s[at, :] = lam_row
            return a_s[at, :] * lam_row

        carry_s[0:1, :] = lax.fori_loop(0, tc, step, carry_s[0:1, :], unroll=8)
        lam = lam_s[...]
        mult = jnp.sqrt(-_expm1(2.0 * log_a))
        d_log_a = lam * hp_ref[...] * a - (lam * ig * xc) * (a * a) / mult
        d_ig = lam * mult * xc
        dpre_r = (d_log_a * (-LRU_C * sp)) * r * (1.0 - r)
        dpre_i = d_ig * ig * (1.0 - ig)
        dxc = lam * mult * ig + _slab_dot(dpre_r, wr_ref, transposed=True) + _slab_dot(dpre_i, wi_ref, transposed=True)
        xcb, drb, dib = xc.astype(BF16), dpre_r.astype(BF16), dpre_i.astype(BF16)
        for j in range(slabs):
            cols = slice(j * LRU_SLAB, (j + 1) * LRU_SLAB)
            dwr_ref[j] += _dot_tn(drb[:, cols], xcb[:, cols])
            dwi_ref[j] += _dot_tn(dib[:, cols], xcb[:, cols])
        vec_ref[0:1, :] += _rowsum(dxc)
        vec_ref[1:2, :] += _rowsum(dpre_r)
        vec_ref[2:3, :] += _rowsum(dpre_i)
        vec_ref[3:4, :] += _rowsum(d_log_a * (-LRU_C * r)) * (-_sigmoid(-lam_ref[...]))
        dxc_s[0:tc, :] = dxc
        du = cw_ref[CONV_WIDTH - 1:CONV_WIDTH, :] * dxc
        for k in range(CONV_WIDTH - 1):
            off = CONV_WIDTH - 1 - k
            du = du + cw_ref[k:k + 1, :] * dxc_s[off:off + tc, :]
        du_ref[...] = du.astype(du_ref.dtype)
        dxc_s[tc:tc + halo, :] = dxc_s[0:halo, :]
        uext_s[0:halo, :] = jnp.where(chunk > 0, uh_ref[...], 0.0)
        uext_s[halo:halo + tc, :] = u_ref[...]
        for k in range(CONV_WIDTH):
            off = halo - (CONV_WIDTH - 1) + k
            vec_ref[4 + k:5 + k, :] += _rowsum(dxc * uext_s[off:off + tc, :])

        @pl.when(pl.program_id(0) == steps - 1)
        def _():
            for j in range(slabs):
                cols = slice(j * LRU_SLAB, (j + 1) * LRU_SLAB)
                dwr_out[:, cols] = _fold_diag_blocks(dwr_ref[j], head_dim, head_dim).astype(dwr_out.dtype)
                dwi_out[:, cols] = _fold_diag_blocks(dwi_ref[j], head_dim, head_dim).astype(dwi_out.dtype)

    halo_spec = pl.BlockSpec((halo, w), lambda i: (jnp.maximum((steps - 1 - i) * sub_per_chunk - 1, 0), 0))
    return _run(
        body, comm, name="lru_bwd", grid=(steps,),
        out_shape=[jax.ShapeDtypeStruct((t, w), BF16), jax.ShapeDtypeStruct((head_dim, w), BF16),
                   jax.ShapeDtypeStruct((head_dim, w), BF16), jax.ShapeDtypeStruct((SUBLANES, w), F32)],
        in_specs=[_rows_rev(tc, w, steps)] * 4 + [halo_spec, _whole((CONV_WIDTH, w)), _whole(wr_blk.shape),
                                                  _whole((1, w)), _whole(wi_blk.shape), _whole((1, w)), _whole((1, w))],
        out_specs=[_rows_rev(tc, w, steps), _acc((head_dim, w)), _acc((head_dim, w)), _acc((SUBLANES, w))],
        scratch_shapes=[pltpu.VMEM((tc, w), F32), pltpu.VMEM((tc, w), F32), pltpu.VMEM((tc + halo, w), F32),
                        pltpu.VMEM((halo + tc, w), F32), pltpu.VMEM((SUBLANES, w), F32),
                        pltpu.VMEM((slabs, LRU_SLAB, LRU_SLAB), F32), pltpu.VMEM((slabs, LRU_SLAB, LRU_SLAB), F32)],
        semantics="arbitrary",
    )(dh, xc, hprev, u, u, conv_w, wr_blk, b_r, wi_blk, b_i, lru_lambda)


def _s5_bwd(dya, y, sr, si, u, w_glu, b_glu, cre_blk, cimn_blk, bbr_blk, bbi_blk, ar, ai, d_skip, comm=None):
    t, sa = dya.shape
    gn = sr.shape[1]
    ns, _, sw = bbr_blk.shape
    tc = min(TIME_CHUNK, t)
    steps = t // tc
    halo = SUBLANES

    def body(dya_ref, y_ref, sr_ref, si_ref, u_ref, wg_ref, bg_ref, cre_ref, cim_ref, bbr_ref, bbi_ref,
             ar_ref, ai_ref, d_ref, du_ref, lr_ref, li_ref, dy_ref, dwg_out, vsa_ref, vgn_ref, gr_s, gi_s, cr_s, ci_s,
             dwg_ref):
        @pl.when(pl.program_id(0) == 0)
        def _():
            cr_s[...] = jnp.zeros_like(cr_s)
            ci_s[...] = jnp.zeros_like(ci_s)
            gr_s[tc:tc + halo, :] = jnp.zeros((halo, gn), F32)
            gi_s[tc:tc + halo, :] = jnp.zeros((halo, gn), F32)
            dwg_ref[...] = jnp.zeros_like(dwg_ref)
            vsa_ref[...] = jnp.zeros_like(vsa_ref)
            vgn_ref[...] = jnp.zeros_like(vgn_ref)

        yv = y_ref[...]
        uv = u_ref[...]
        zz = _gelu(yv)
        sg = _sigmoid(_dot(zz, wg_ref[...]) + bg_ref[...])
        dyav = dya_ref[...]
        dq = dyav * zz * sg * (1.0 - sg)
        dzz = dyav * sg + _dot_nt(dq, wg_ref[...])
        dwg_ref[...] += _dot_tn(zz, dq)
        dy = dzz * _gelu_grad(yv)
        dyb = dy.astype(BF16)
        dy_ref[...] = dyb.astype(dy_ref.dtype)
        vsa_ref[0:1, :] += _rowsum(dq)
        vsa_ref[1:2, :] += _rowsum(dy * uv)
        for m in range(ns):
            dym = dyb[:, m * S5_SLAB:(m + 1) * S5_SLAB]
            gr_s[0:tc, m * sw:(m + 1) * sw] = _dot_nt(dym, cre_ref[m])
            gi_s[0:tc, m * sw:(m + 1) * sw] = _dot_nt(dym, cim_ref[m])
        a_r = ar_ref[...]
        a_i = ai_ref[...]

        def step(i, carry):
            l_r, l_i = carry
            at = pl.ds(tc - 1 - i, 1)
            n_r = gr_s[at, :] + a_r * l_r + a_i * l_i
            n_i = gi_s[at, :] + a_r * l_i - a_i * l_r
            gr_s[at, :] = n_r
            gi_s[at, :] = n_i
            return n_r, n_i

        l_r, l_i = lax.fori_loop(0, tc, step, (cr_s[0:1, :], ci_s[0:1, :]), unroll=8)
        cr_s[0:1, :] = l_r
        ci_s[0:1, :] = l_i
        nxt_r = gr_s[1:tc + 1, :]
        nxt_i = gi_s[1:tc + 1, :]
        srv = sr_ref[...].astype(F32)
        siv = si_ref[...].astype(F32)
        vgn_ref[0:1, :] += _rowsum(nxt_r * srv + nxt_i * siv)
        vgn_ref[1:2, :] += _rowsum(nxt_i * srv - nxt_r * siv)
        lam_r = gr_s[0:tc, :]
        lam_i = gi_s[0:tc, :]
        gr_s[tc:tc + halo, :] = gr_s[0:halo, :]
        gi_s[tc:tc + halo, :] = gi_s[0:halo, :]
        lrb = lam_r.astype(BF16)
        lib = lam_i.astype(BF16)
        lr_ref[...] = lrb.astype(lr_ref.dtype)
        li_ref[...] = lib.astype(li_ref.dtype)
        for m in range(ns):
            states, chans = slice(m * sw, (m + 1) * sw), slice(m * S5_SLAB, (m + 1) * S5_SLAB)
            du_ref[:, chans] = (_dot_nt(lrb[:, states], bbr_ref[m]) + _dot_nt(lib[:, states], bbi_ref[m])
                                + dy[:, chans] * d_ref[:, chans]).astype(du_ref.dtype)
        _store_on_last([(dwg_ref, dwg_out)])

    return _run(
        body, comm, name="s5_bwd", grid=(steps,),
        out_shape=[jax.ShapeDtypeStruct((t, sa), BF16), jax.ShapeDtypeStruct((t, gn), BF16),
                   jax.ShapeDtypeStruct((t, gn), BF16), jax.ShapeDtypeStruct((t, sa), BF16),
                   jax.ShapeDtypeStruct((sa, sa), BF16), jax.ShapeDtypeStruct((SUBLANES, sa), F32),
                   jax.ShapeDtypeStruct((SUBLANES, gn), F32)],
        in_specs=[_rows_rev(tc, sa, steps), _rows_rev(tc, sa, steps), _rows_rev(tc, gn, steps), _rows_rev(tc, gn, steps),
                  _rows_rev(tc, sa, steps), _whole((sa, sa)), _whole((1, sa)), _whole(cre_blk.shape),
                  _whole(cimn_blk.shape), _whole(bbr_blk.shape), _whole(bbi_blk.shape), _whole((1, gn)), _whole((1, gn)),
                  _whole((1, sa))],
        out_specs=[_rows_rev(tc, sa, steps), _rows_rev(tc, gn, steps), _rows_rev(tc, gn, steps), _rows_rev(tc, sa, steps),
                   _acc((sa, sa)), _acc((SUBLANES, sa)), _acc((SUBLANES, gn))],
        scratch_shapes=[pltpu.VMEM((tc + halo, gn), F32), pltpu.VMEM((tc + halo, gn), F32),
                        pltpu.VMEM((SUBLANES, gn), F32), pltpu.VMEM((SUBLANES, gn), F32), pltpu.VMEM((sa, sa), F32)],
        semantics="arbitrary",
    )(dya, y, sr, si, u, w_glu, b_glu, cre_blk, cimn_blk, bbr_blk, bbi_blk, ar, ai, d_skip)


def _inproj_bwd(dparts, x, dx1, g_mix, w_in_t, comm=None):
    t, d = x.shape
    n = w_in_t.shape[0]
    widths = [p.shape[1] for p in dparts]
    offs = [sum(widths[:i]) for i in range(len(widths) + 1)]
    tm = min(TOKEN_TILE, t)
    np_ = len(dparts)

    def body(*refs):
        dz_refs = refs[:np_]
        x_ref, dx1_ref, g_ref, w_ref, gx_ref, h_ref, vd_ref, vn_ref = refs[np_:]
        _zero_on_first(vd_ref, vn_ref)
        dh = jnp.zeros((tm, d), F32)
        for k, r in enumerate(dz_refs):
            lo, hi = offs[k], offs[k + 1]
            dzk = r[...]
            dh = dh + _dot(dzk, w_ref[lo:hi, :])
            vn_ref[0:1, lo:hi] += _rowsum(dzk.astype(F32))
        xhat, r0 = _rms_stats(x_ref[...])
        h_ref[...] = (xhat * g_ref[...]).astype(h_ref.dtype)
        dxn, dg = _rms_bwd(dh, xhat, r0, g_ref[...])
        gx_ref[...] = dx1_ref[...] + dxn
        vd_ref[0:1, :] += _rowsum(dg)

    return _run(
        body, comm, name="inproj_bwd", grid=(t // tm,),
        out_shape=[jax.ShapeDtypeStruct((t, d), F32), jax.ShapeDtypeStruct((t, d), BF16),
                   jax.ShapeDtypeStruct((SUBLANES, d), F32), jax.ShapeDtypeStruct((SUBLANES, n), F32)],
        in_specs=[_rows(tm, w) for w in widths] + [_rows(tm, d), _rows(tm, d), _whole((1, d)), _whole((n, d))],
        out_specs=[_rows(tm, d), _rows(tm, d), _acc((SUBLANES, d)), _acc((SUBLANES, n))],
        semantics="arbitrary",
    )(*dparts, x, dx1, g_mix, w_in_t)


def _dw_from_parts(dparts, h, tn, name, comm=None):
    t, d = h.shape
    widths = [p.shape[1] for p in dparts]
    offs = [sum(widths[:i]) for i in range(len(widths) + 1)]
    np_ = len(dparts)

    def body(*refs):
        h_ref, o_ref = refs[np_], refs[np_ + 1]
        hv = h_ref[...]
        for k, r in enumerate(refs[:np_]):
            o_ref[offs[k]:offs[k + 1], :] = _dot_tn(r[...], hv).astype(o_ref.dtype)

    return _run(
        body, comm, name=name, grid=(d // tn,),
        out_shape=jax.ShapeDtypeStruct((offs[-1], d), BF16),
        in_specs=[_whole(p.shape) for p in dparts] + [pl.BlockSpec((t, tn), lambda j: (0, j))],
        out_specs=pl.BlockSpec((offs[-1], tn), lambda j: (0, j)),
        semantics="parallel",
    )(*dparts, h)


def _prep(lr_row, li_row, ldt_row, lr_col, li_col, ldt_col, b_re, b_im, c_re, c_im, w_r, w_i, comm=None):
    gn, pch = b_re.shape
    sa, n = c_re.shape
    w, hd = w_r.shape
    ns, sw, lsl = sa // S5_SLAB, S5_SLAB * n // pch, w // LRU_SLAB

    def spread_cols(vals, row_group, col_group, width):
        r, k = vals.shape
        tile = (lax.broadcasted_iota(jnp.int32, (k, width), 0) == lax.broadcasted_iota(jnp.int32, (k, width), 1) % k)
        rows = lax.broadcasted_iota(jnp.int32, (r, width), 0) // row_group
        cols = lax.broadcasted_iota(jnp.int32, (r, width), 1) // col_group
        return jnp.where(rows == cols, _dot(vals, tile.astype(BF16)), 0.0)

    def spread_rows(vals, row_group, col_group, height):
        k, c = vals.shape
        tile = (lax.broadcasted_iota(jnp.int32, (height, k), 0) % k == lax.broadcasted_iota(jnp.int32, (height, k), 1))
        rows = lax.broadcasted_iota(jnp.int32, (height, c), 0) // row_group
        cols = lax.broadcasted_iota(jnp.int32, (height, c), 1) // col_group
        return jnp.where(rows == cols, _dot(tile.astype(BF16), vals), 0.0)

    def body(lrr, lir, ldr, lrc, lic, ldc, bre, bim, cre, cim, wr, wi,
             ar_o, ai_o, bbr_o, bbi_o, cre_o, cim_o, wr_o, wi_o):
        ar, ai, _, _ = _disc_scalars(lrr[...], lir[...], ldr[...])
        ar_o[...] = ar
        ai_o[...] = ai
        _, _, bbr, bbi = _disc_cols(lrc[...], lic[...], ldc[...], bre[...], bim[...])
        bbr_t, bbi_t = bbr.T, bbi.T
        for m in range(ns):
            bbr_o[m] = spread_rows(bbr_t[:, m * sw:(m + 1) * sw], pch, n, S5_SLAB).astype(bbr_o.dtype)
            bbi_o[m] = spread_rows(bbi_t[:, m * sw:(m + 1) * sw], pch, n, S5_SLAB).astype(bbi_o.dtype)
            rows = slice(m * S5_SLAB, (m + 1) * S5_SLAB)
            cre_o[m] = spread_rows(cre[rows, :].T, n, pch, sw).astype(cre_o.dtype)
            cim_o[m] = spread_rows(-cim[rows, :].T, n, pch, sw).astype(cim_o.dtype)
        for j in range(lsl):
            rows = slice(j * LRU_SLAB, (j + 1) * LRU_SLAB)
            wr_o[j] = spread_cols(wr[rows, :], hd, hd, LRU_SLAB).astype(wr_o.dtype)
            wi_o[j] = spread_cols(wi[rows, :], hd, hd, LRU_SLAB).astype(wi_o.dtype)

    args = (lr_row, li_row, ldt_row, lr_col, li_col, ldt_col, b_re, b_im, c_re, c_im, w_r, w_i)
    out_shape = [jax.ShapeDtypeStruct((1, gn), F32), jax.ShapeDtypeStruct((1, gn), F32),
                 jax.ShapeDtypeStruct((ns, S5_SLAB, sw), BF16), jax.ShapeDtypeStruct((ns, S5_SLAB, sw), BF16),
                 jax.ShapeDtypeStruct((ns, sw, S5_SLAB), BF16), jax.ShapeDtypeStruct((ns, sw, S5_SLAB), BF16),
                 jax.ShapeDtypeStruct((lsl, LRU_SLAB, LRU_SLAB), BF16),
                 jax.ShapeDtypeStruct((lsl, LRU_SLAB, LRU_SLAB), BF16)]
    return _run(
        body, comm, name="prep", grid=(1,), out_shape=out_shape, in_specs=[_whole(a.shape) for a in args],
        out_specs=[_acc(s.shape) for s in out_shape], semantics="arbitrary",
    )(*args)


def _s5_param_grads(lam_r, lam_i, sr, si, u, dy, pch, n, comm=None):
    t, gn = lam_r.shape
    sa = u.shape[1]
    sw = S5_SLAB * n // pch

    def body(lr_ref, li_ref, sr_ref, si_ref, u_ref, dy_ref, dbr_ref, dbi_ref, dcr_ref, dci_ref):
        uv = u_ref[...]
        dyv = dy_ref[...]
        dbr_ref[...] = _fold_diag_blocks(_dot_tn(uv, lr_ref[...]), pch, n).astype(dbr_ref.dtype)
        dbi_ref[...] = _fold_diag_blocks(_dot_tn(uv, li_ref[...]), pch, n).astype(dbi_ref.dtype)
        dcr_ref[...] = _fold_diag_blocks(_dot_tn(sr_ref[...], dyv), n, pch).astype(dcr_ref.dtype)
        dci_ref[...] = _fold_diag_blocks(_dot_tn(si_ref[...], dyv), n, pch).astype(dci_ref.dtype)

    states = pl.BlockSpec((t, sw), lambda m: (0, m))
    chans = pl.BlockSpec((t, S5_SLAB), lambda m: (0, m))
    return _run(
        body, comm, name="s5_param_grads", grid=(sa // S5_SLAB,),
        out_shape=[jax.ShapeDtypeStruct((pch, gn), BF16), jax.ShapeDtypeStruct((pch, gn), BF16),
                   jax.ShapeDtypeStruct((n, sa), BF16), jax.ShapeDtypeStruct((n, sa), BF16)],
        in_specs=[states, states, states, states, chans, chans],
        out_specs=[pl.BlockSpec((pch, sw), lambda m: (0, m)), pl.BlockSpec((pch, sw), lambda m: (0, m)),
                   pl.BlockSpec((n, S5_SLAB), lambda m: (0, m)), pl.BlockSpec((n, S5_SLAB), lambda m: (0, m))],
        semantics="parallel",
    )(lam_r, lam_i, sr, si, u, dy)


SMALL_PARTS = ["vec_tail", "vec_ffn", "vec_lru", "vec_mix", "vec_bin", "vec_sa", "vec_gn", "dw_r", "dw_i", "dbb_re",
               "dbb_im", "dc_re", "dc_imn", "loss"]


def _small_reduce(parts, shapes, lr_col, li_col, ldt_col, b_re, b_im, groups, after=()):
    gn, pch = b_re.shape
    n = gn // groups
    nparts = parts[SMALL_PARTS[0]].size // math.prod(shapes[SMALL_PARTS[0]])
    np_, nout, first_out = len(SMALL_PARTS), 24, len(SMALL_PARTS) + 5 + len(after)

    def body(*refs):
        ins = refs[:np_]
        lr, li, ldt, bre, bim = refs[np_:np_ + 5]
        outs = refs[first_out:first_out + nout]
        sums = dict(zip(SMALL_PARTS, refs[first_out + nout:]))

        @pl.when(pl.program_id(0) == 0)
        def _():
            for k, r in zip(SMALL_PARTS, ins):
                sums[k][...] = r[...].astype(F32)

        @pl.when(pl.program_id(0) > 0)
        def _():
            for k, r in zip(SMALL_PARTS, ins):
                sums[k][...] += r[...].astype(F32)

        @pl.when(pl.program_id(0) == nparts - 1)
        def _():
            finish({k: s[...] for k, s in sums.items()}, lr, li, ldt, bre, bim, *outs)

    def finish(tot, lr, li, ldt, bre, bim, o_loss, o_gmix, o_bin, o_bglu, o_s5d, o_convb, o_br, o_bi, o_lam, o_gffn,
               o_gpg, o_bpg, o_gple, o_gfin, o_wr, o_wi, o_cre, o_cim, o_bre, o_bim, o_lre, o_lim, o_ldt, o_convw):
        o_loss[...] = tot["loss"]
        o_convw[...] = tot["vec_lru"][SUBLANES - CONV_WIDTH:SUBLANES]
        o_bpg[...] = tot["vec_tail"][0:1]
        o_gpg[...] = tot["vec_tail"][1:2]
        o_gple[...] = tot["vec_tail"][2:3]
        o_gfin[...] = tot["vec_tail"][3:4]
        o_gffn[...] = tot["vec_ffn"][0:1]
        o_convb[...] = tot["vec_lru"][0:1]
        o_br[...] = tot["vec_lru"][1:2]
        o_bi[...] = tot["vec_lru"][2:3]
        o_lam[...] = tot["vec_lru"][3:4]
        o_gmix[...] = tot["vec_mix"][0:1]
        o_bin[...] = tot["vec_bin"][0:1]
        o_bglu[...] = tot["vec_sa"][0:1]
        o_s5d[...] = tot["vec_sa"][1:2]
        o_wr[...] = tot["dw_r"].T
        o_wi[...] = tot["dw_i"].T
        o_cre[...] = tot["dc_re"].T
        o_cim[...] = -tot["dc_imn"].T
        d_a = tot["vec_gn"].T
        _, chain = jax.vjp(_disc_cols, lr[...], li[...], ldt[...], bre[...], bim[...])
        d_lr, d_li, d_ldt, d_bre, d_bim = chain((d_a[:, 0:1], d_a[:, 1:2], tot["dbb_re"].T, tot["dbb_im"].T))
        o_lre[...] = d_lr
        o_lim[...] = d_li
        o_bre[...] = d_bre
        o_bim[...] = d_bim
        same = (lax.broadcasted_iota(jnp.int32, (groups, gn), 0)
                == lax.broadcasted_iota(jnp.int32, (groups, gn), 1) // n).astype(F32)
        o_ldt[...] = jnp.dot(same, d_ldt * jnp.ones((1, LANES), F32), preferred_element_type=F32,
                             precision=lax.Precision.HIGHEST)[:, 0:1]

    d = shapes["vec_mix"][1]
    nz = shapes["vec_bin"][1]
    sa = shapes["vec_sa"][1]
    w = shapes["vec_lru"][1]
    row = lambda c: jax.ShapeDtypeStruct((1, c), F32)
    out_shape = [jax.ShapeDtypeStruct(shapes["loss"], F32), row(d), row(nz), row(sa), row(sa), row(w), row(w), row(w),
                 row(w), row(d), row(d), row(d), row(d), row(d),
                 jax.ShapeDtypeStruct(shapes["dw_r"][::-1], F32), jax.ShapeDtypeStruct(shapes["dw_i"][::-1], F32),
                 jax.ShapeDtypeStruct(shapes["dc_re"][::-1], F32), jax.ShapeDtypeStruct(shapes["dc_imn"][::-1], F32),
                 jax.ShapeDtypeStruct((gn, pch), F32), jax.ShapeDtypeStruct((gn, pch), F32),
                 jax.ShapeDtypeStruct((gn, 1), F32), jax.ShapeDtypeStruct((gn, 1), F32),
                 jax.ShapeDtypeStruct((groups, 1), F32), jax.ShapeDtypeStruct((CONV_WIDTH, w), F32)]
    def part_spec(k):
        r, c = shapes[k]
        if parts[k].ndim == 3:
            return pl.BlockSpec((None, r, c), lambda i: (i, 0, 0))
        return pl.BlockSpec((r, c), lambda i: (i, 0))

    outs = _run(
        body, None, name="small_reduce", grid=(nparts,), out_shape=out_shape,
        in_specs=([part_spec(k) for k in SMALL_PARTS] + [_whole(a.shape) for a in (lr_col, li_col, ldt_col, b_re, b_im)]
                  + [_ANY] * len(after)),
        out_specs=[_acc(s.shape) for s in out_shape],
        scratch_shapes=[pltpu.VMEM(shapes[k], F32) for k in SMALL_PARTS],
        semantics="arbitrary",
    )(*[parts[k] for k in SMALL_PARTS], lr_col, li_col, ldt_col, b_re, b_im, *after)
    names = ["loss", "g_mix", "b_in", "b_glu", "s5_d", "conv_b", "b_r", "b_i", "lru_lambda", "g_ffn", "g_ple_gate",
             "b_ple_gate", "g_ple", "g_final", "w_r", "w_i", "s5_c_re", "s5_c_im", "s5_b_re", "s5_b_im", "lam_re",
             "lam_im", "log_dt", "conv_w"]
    return dict(zip(names, outs))


def _adamw_small(ws, gs, ms, vs):
    n = len(ws)

    def body(*refs):
        w_r, g_r, m_r, v_r = (refs[i * n:(i + 1) * n] for i in range(4))
        g_o, d_o, m_o, v_o = (refs[(4 + i) * n:(5 + i) * n] for i in range(4))
        for i in range(n):
            g = g_r[i][...]
            delta, m_new, v_new = _adamw_math(w_r[i][...], g, m_r[i][...], v_r[i][...])
            g_o[i][...] = g
            d_o[i][...] = delta
            m_o[i][...] = m_new
            v_o[i][...] = v_new

    shapes = [jax.ShapeDtypeStruct(a.shape, F32) for a in ws]
    outs = pl.pallas_call(body, name="adamw_small", out_shape=shapes * 4)(*ws, *gs, *ms, *vs)
    return outs[:n], outs[n:2 * n], outs[2 * n:3 * n], outs[3 * n:]


def _adamw_math(w, g, m, v):
    m_new = ADAM_B1 * m + (1.0 - ADAM_B1) * g
    v_new = ADAM_B2 * v + (1.0 - ADAM_B2) * (g * g)
    m_hat = m_new / (1.0 - ADAM_B1 ** ADAM_STEP)
    v_hat = v_new / (1.0 - ADAM_B2 ** ADAM_STEP)
    delta = -ADAM_LR * (m_hat / (jnp.sqrt(v_hat) + ADAM_EPS) + ADAM_WD * w)
    return delta, m_new, v_new


def _row_tile(rows):
    for cand in range(256, 0, -16):
        if rows % cand == 0:
            return cand
    return rows


def _adamw(parts, w, m, v, name, transposed=False, own=None, after=()):
    nw = len(w)
    rows, cols = w[0].shape
    npart = parts[0].shape[0]
    tr = _row_tile(rows)
    if transposed:
        parts_spec = pl.BlockSpec((npart, cols, tr), lambda i: (0, 0, i))
    else:
        parts_spec = pl.BlockSpec((npart, tr, cols), lambda i: (0, i, 0))
    per = 4 if own is None else 5
    first_out = per * nw + len(after)

    def body(*refs):
        mine = None if own is None else _chip(_mesh_position())
        for i in range(nw):
            group = refs[per * i:per * i + per]
            p_ref, (w_ref, m_ref, v_ref) = group[0], group[-3:]
            g_ref, d_ref, mo_ref, vo_ref = refs[first_out + 4 * i:first_out + 4 * i + 4]

            def part(k):
                a = p_ref[k].astype(F32)
                return a if own is None else jnp.where(mine == k, group[1][k].astype(F32), a)

            g = part(0)
            for k in range(1, npart):
                g = g + part(k)
            if transposed:
                g = g.T
            delta, m_new, v_new = _adamw_math(w_ref[...], g, m_ref[...], v_ref[...])
            g_ref[...] = g
            d_ref[...] = delta
            mo_ref[...] = m_new
            vo_ref[...] = v_new

    groups = zip(parts, w, m, v) if own is None else zip(parts, own, w, m, v)
    res = pl.pallas_call(
        body, name=name, grid=(rows // tr,),
        out_shape=[jax.ShapeDtypeStruct((rows, cols), F32)] * (4 * nw),
        in_specs=([parts_spec] * (per - 3) + [_rows(tr, cols)] * 3) * nw + [_ANY] * len(after),
        out_specs=[_rows(tr, cols)] * (4 * nw),
        compiler_params=_params("parallel"),
    )(*[a for group in groups for a in group], *after)
    return [res[4 * i:4 * i + 4] for i in range(nw)]


def _mesh_position():
    return lax.axis_index("x"), lax.axis_index("y"), lax.axis_index("c")


def _flip(pos, rel):
    x, y, c = pos
    return (1 - x if rel & 4 else x, 1 - y if rel & 2 else y, 1 - c if rel & 1 else c)


def _index(pos):
    return 4 * pos[0] + 2 * pos[1] + pos[2]


_ANY = pl.BlockSpec(memory_space=pl.ANY)
FLAT_ROWS = 32


def _dma_sems(n):
    return [pltpu.SemaphoreType.DMA((n, N_DEV - 1)), pltpu.SemaphoreType.DMA((n, N_DEV - 1)), pltpu.SemaphoreType.DMA((n,))]


def _block_of(ref, idx, rows, flat):
    if flat:
        return ref.at[pl.ds(pl.multiple_of(idx * rows, FLAT_ROWS), rows), :]
    return ref.at[idx]


class _Gather:
    chips = (4, 2, 6)
    rels = frozenset((1, 4, 2, 6))

    def __init__(self, shards):
        self.inputs = list(shards)
        self.flat = [s.shape[0] % FLAT_ROWS == 0 for s in shards]
        self.out_shape = [
            jax.ShapeDtypeStruct((N_DEV * s.shape[0], s.shape[1]) if f else (N_DEV,) + s.shape, s.dtype)
            for s, f in zip(shards, self.flat)]
        self.sems = _dma_sems(len(shards))

    def _copy(self, ins, outs, sems, i, k, block, to, own=False, lands_index=None):
        a = i if lands_index is None else lands_index
        dst = _block_of(outs[a], _index(block), self.inputs[a].shape[0], self.flat[a])
        return pltpu.make_async_remote_copy(
            src_ref=ins[a] if own else dst, dst_ref=dst, send_sem=sems[0].at[i, k], recv_sem=sems[1].at[i, k],
            device_id=to, device_id_type=MESH)

    def _local(self, ins, outs, sems, i, me):
        dst = _block_of(outs[i], _index(me), self.inputs[i].shape[0], self.flat[i])
        return pltpu.make_async_copy(ins[i], dst, sems[2].at[i])

    def _first(self, ins, outs, sems, i, me):
        cps = [self._copy(ins, outs, sems, i, 0, me, _flip(me, 1), own=True)]
        cps += [self._copy(ins, outs, sems, i, 1 + j, me, _flip(me, rel), own=True) for j, rel in enumerate(self.chips)]
        return cps

    def _passed(self, ins, outs, sems, i, j, me):
        return self._copy(ins, outs, sems, i, 4 + j, _flip(me, self.chips[j]), _flip(me, 1))

    def before(self, ins, outs, sems):
        n = len(self.inputs)
        me = _mesh_position()

        @pl.when(pl.program_id(0) == 0)
        def _():
            for i in range(n):
                self._local(ins, outs, sems, i, me).start()
                for cp in self._first(ins, outs, sems, i, me):
                    cp.start()

        @pl.when(pl.program_id(0) == pl.num_programs(0) - 1)
        def _():
            for j, rel in enumerate(self.chips):
                for i in range(n):
                    self._copy(ins, outs, sems, i, 1 + j, _flip(me, rel), me).wait_recv()
                    self._passed(ins, outs, sems, i, j, me).start()

    def after(self, ins, outs, sems):
        n = len(self.inputs)
        me = _mesh_position()
        sibling = _flip(me, 1)

        @pl.when(pl.program_id(0) == pl.num_programs(0) - 1)
        def _():
            for i in range(n):
                self._copy(ins, outs, sems, i, 0, sibling, me).wait_recv()
                for j, rel in enumerate(self.chips):
                    self._copy(ins, outs, sems, i, 4 + j, _flip(sibling, rel), me).wait_recv()
            for i in range(n):
                for cp in self._first(ins, outs, sems, i, me):
                    cp.wait_send()
                for j in range(len(self.chips)):
                    self._passed(ins, outs, sems, i, j, me).wait_send()
                self._local(ins, outs, sems, i, me).wait()


N_CHIPS = N_DEV // 2


def _chip(pos):
    return 2 * pos[0] + pos[1]


class _PairSwap:
    rels = frozenset((1,))

    def __init__(self, arrays):
        self.inputs = list(arrays)
        self.rows = [a.shape[0] // N_DEV for a in arrays]
        for r in self.rows:
            assert r % FLAT_ROWS == 0, r
        self.out_shape = [jax.ShapeDtypeStruct((N_CHIPS, r, a.shape[1]), a.dtype) for a, r in zip(arrays, self.rows)]
        n = len(arrays)
        self.sems = [pltpu.SemaphoreType.DMA((n, N_CHIPS)), pltpu.SemaphoreType.DMA((n, N_CHIPS))]

    def _copy(self, ins, outs, sems, i, j, me):
        sibling = _flip(me, 1)
        return pltpu.make_async_remote_copy(
            src_ref=_block_of(ins[i], 2 * j + sibling[2], self.rows[i], True), dst_ref=outs[i].at[j],
            send_sem=sems[0].at[i, j], recv_sem=sems[1].at[i, j], device_id=sibling, device_id_type=MESH)

    def before(self, ins, outs, sems):
        me = _mesh_position()

        @pl.when(pl.program_id(0) == 0)
        def _():
            for i in range(len(self.inputs)):
                for j in range(N_CHIPS):
                    self._copy(ins, outs, sems, i, j, me).start()

    def after(self, ins, outs, sems):
        me = _mesh_position()

        @pl.when(pl.program_id(0) == pl.num_programs(0) - 1)
        def _():
            for i in range(len(self.inputs)):
                for j in range(N_CHIPS):
                    self._copy(ins, outs, sems, i, j, me).wait()


_HBM = pl.BlockSpec(memory_space=pltpu.HBM)
_SEM = pl.BlockSpec(memory_space=pltpu.SEMAPHORE)
_DATAFLOW = pltpu.SideEffectType.DATAFLOW_SIDE_EFFECTING
_CHIP_RELS = (4, 2, 6)
_SPLIT_COLLECTIVE_ID = 3


def _chips_copy(src_ref, dst_ref, send_sems, recv_sems, i, k, me):
    peer = _flip(me, _CHIP_RELS[k])
    at = i * len(_CHIP_RELS) + k
    return pltpu.make_async_remote_copy(
        src_ref=src_ref.at[_chip(peer)], dst_ref=dst_ref.at[_chip(me)], send_sem=send_sems.at[at],
        recv_sem=recv_sems.at[at], device_id=peer, device_id_type=MESH)


def _chips_start(pairs, name, collective_id=_SPLIT_COLLECTIVE_ID):
    n = len(pairs)

    def body(*refs):
        p_refs, land_refs, send_sems, recv_sems = refs[:n], refs[n:2 * n], refs[2 * n], refs[2 * n + 1]
        token = refs[-1]
        me = _mesh_position()
        _handshake(_CHIP_RELS)
        for i in range(n):
            for k in range(len(_CHIP_RELS)):
                _chips_copy(p_refs[i], land_refs[i], send_sems, recv_sems, i, k, me).start()
        token[...] = jnp.zeros_like(token)

    nsem = n * len(_CHIP_RELS)
    hbm = [pltpu.HBM(p.shape, p.dtype) for p in pairs]
    res = pl.pallas_call(
        body, name="chips_start_" + name,
        out_shape=(pltpu.SemaphoreType.DMA((nsem,)), pltpu.SemaphoreType.DMA((nsem,)), *hbm, *hbm,
                   jax.ShapeDtypeStruct((SUBLANES, LANES), F32)),
        in_specs=(_HBM,) * (2 * n),
        out_specs=(_SEM, _SEM) + (_HBM,) * (2 * n) + (pl.BlockSpec(memory_space=pltpu.VMEM),),
        input_output_aliases={i: 2 + i for i in range(2 * n)},
        compiler_params=pltpu.CompilerParams(has_side_effects=_DATAFLOW, collective_id=collective_id),
    )(*[pltpu.with_memory_space_constraint(p, pltpu.HBM) for p in pairs],
      *[pltpu.with_memory_space_constraint(lax.empty(p.shape, p.dtype), pltpu.HBM) for p in pairs])
    return res[0], res[1], list(res[2:2 + n]), list(res[2 + n:2 + 2 * n]), res[-1]


def _chips_wait(send_sems, recv_sems, p_thru, land_thru, after, name):
    n = len(p_thru)

    def body(*refs):
        p_refs, land_refs, send_sems, recv_sems = refs[:n], refs[n:2 * n], refs[2 * n], refs[2 * n + 1]
        me = _mesh_position()
        for i in range(n):
            for k in range(len(_CHIP_RELS)):
                copy = _chips_copy(p_refs[i], land_refs[i], send_sems, recv_sems, i, k, me)
                copy.wait_send()
                copy.wait_recv()

    hbm = [pltpu.HBM(p.shape, p.dtype) for p in p_thru]
    res = pl.pallas_call(
        body, name="chips_wait_" + name, out_shape=(*hbm, *hbm),
        in_specs=(_HBM,) * (2 * n) + (_SEM, _SEM) + (_ANY,) * len(after), out_specs=(_HBM,) * (2 * n),
        input_output_aliases={i: i for i in range(2 * n)}, compiler_params=pltpu.CompilerParams(has_side_effects=_DATAFLOW),
    )(*p_thru, *land_thru, send_sems, recv_sems, *after)
    return list(res[:n]), list(res[n:])


_GATHER_START_ID, _GATHER_PASS_ID = 4, 5


class _Rows:
    def __init__(self, ref, width):
        self.ref, self.width, self.at = ref, width, self

    def __getitem__(self, idx):
        i, k = idx
        return self.ref.at[i * self.width + k]


def _handshake(rels):
    me = _mesh_position()
    sem = pltpu.get_barrier_semaphore()
    for rel in rels:
        pl.semaphore_signal(sem, inc=1, device_id=_flip(me, rel), device_id_type=MESH)
    pl.semaphore_wait(sem, len(rels))


def _gather_start(shards, name, after=()):
    job, n, na = _Gather(shards), len(shards), len(after)

    def body(*refs):
        ins, lands = refs[:n], refs[n:2 * n]
        send, recv, local = refs[2 * n + na:2 * n + na + 3]
        sems = (_Rows(send, N_DEV - 1), _Rows(recv, N_DEV - 1), local)
        token = refs[-1]
        me = _mesh_position()
        _handshake(sorted(job.rels))
        for i in range(n):
            job._local(ins, lands, sems, i, me).start()
            for cp in job._first(ins, lands, sems, i, me):
                cp.start()
        token[...] = jnp.zeros_like(token)

    res = pl.pallas_call(
        body, name=name,
        out_shape=(pltpu.SemaphoreType.DMA((n * (N_DEV - 1),)), pltpu.SemaphoreType.DMA((n * (N_DEV - 1),)),
                   pltpu.SemaphoreType.DMA((n,))) + tuple(pltpu.HBM(s.shape, s.dtype) for s in job.out_shape)
        + (jax.ShapeDtypeStruct((SUBLANES, LANES), F32),),
        in_specs=(_HBM,) * (2 * n) + (_ANY,) * na,
        out_specs=(_SEM,) * 3 + (_HBM,) * n + (pl.BlockSpec(memory_space=pltpu.VMEM),),
        input_output_aliases={n + i: 3 + i for i in range(n)},
        compiler_params=pltpu.CompilerParams(has_side_effects=_DATAFLOW, collective_id=_GATHER_START_ID),
    )(*[pltpu.with_memory_space_constraint(s, pltpu.HBM) for s in shards],
      *[pltpu.with_memory_space_constraint(lax.empty(s.shape, s.dtype), pltpu.HBM) for s in job.out_shape], *after)
    return list(res[:3]), list(res[3:3 + n]), res[-1]


def _gather_pass(shards, which, sems, lands, name, after=()):
    job, n, m, na = _Gather(shards), len(shards), len(which), len(after)

    def body(*refs):
        lands_r, local = refs[:m], refs[m + 2]
        send, recv = _Rows(refs[m], N_DEV - 1), _Rows(refs[m + 1], N_DEV - 1)
        send2, recv2 = _Rows(refs[m + 3 + na], 3), _Rows(refs[m + 3 + na + 1], 3)
        me = _mesh_position()
        _handshake((1,))
        by_index = {i: lands_r[q] for q, i in enumerate(which)}
        for j, rel in enumerate(job.chips):
            for q, i in enumerate(which):
                job._copy(None, by_index, (send, recv, local), i, 1 + j, _flip(me, rel), me).wait_recv()
                job._copy(None, by_index, (send2, recv2), q, j, _flip(me, rel), _flip(me, 1), lands_index=i).start()

    res = pl.pallas_call(
        body, name=name,
        out_shape=(pltpu.SemaphoreType.DMA((m * 3,)), pltpu.SemaphoreType.DMA((m * 3,)))
        + tuple(pltpu.HBM(lands[i].shape, lands[i].dtype) for i in which),
        in_specs=(_HBM,) * m + (_SEM,) * 3 + (_ANY,) * na, out_specs=(_SEM, _SEM) + (_HBM,) * m,
        input_output_aliases={q: 2 + q for q in range(m)},
        compiler_params=pltpu.CompilerParams(has_side_effects=_DATAFLOW, collective_id=_GATHER_PASS_ID),
    )(*[lands[i] for i in which], *sems, *after)
    return list(res[:2]), list(res[2:])


def _gather_finish(shards, which, sems, sems2, lands, after, name):
    job, n, m = _Gather(shards), len(shards), len(which)

    def body(*refs):
        ins, lands_r = refs[:m], refs[m:2 * m]
        send, recv, local = _Rows(refs[2 * m], N_DEV - 1), _Rows(refs[2 * m + 1], N_DEV - 1), refs[2 * m + 2]
        send2, recv2 = _Rows(refs[2 * m + 3], 3), _Rows(refs[2 * m + 4], 3)
        me = _mesh_position()
        sibling = _flip(me, 1)
        by_index = {i: lands_r[q] for q, i in enumerate(which)}
        own = {i: ins[q] for q, i in enumerate(which)}
        for q, i in enumerate(which):
            job._copy(None, by_index, (send, recv, local), i, 0, sibling, me).wait_recv()
            for j, rel in enumerate(job.chips):
                job._copy(None, by_index, (send2, recv2), q, j, _flip(sibling, rel), me, lands_index=i).wait_recv()
                job._copy(None, by_index, (send2, recv2), q, j, _flip(me, rel), sibling, lands_index=i).wait_send()
            for cp in job._first(own, by_index, (send, recv, local), i, me):
                cp.wait_send()
            job._local(own, by_index, (send, recv, local), i, me).wait()

    res = pl.pallas_call(
        body, name=name, out_shape=tuple(pltpu.HBM(lands[i].shape, lands[i].dtype) for i in which),
        in_specs=(_HBM,) * (2 * m) + (_SEM,) * 5 + (_ANY,) * len(after), out_specs=(_HBM,) * m,
        input_output_aliases={m + q: q for q in range(m)},
        compiler_params=pltpu.CompilerParams(has_side_effects=_DATAFLOW),
    )(*[pltpu.with_memory_space_constraint(shards[i], pltpu.HBM) for i in which], *[lands[i] for i in which],
      *sems, *sems2, *after)
    return list(res)


def _pair_add(grads, halves, name):
    n = len(grads)

    def body(*refs):
        g_refs, h_refs, o_refs = refs[:n], refs[n:2 * n], refs[2 * n:]
        c = lax.axis_index("c")
        for i in range(n):
            r = h_refs[i].shape[1]
            for j in range(N_CHIPS):
                own = g_refs[i][pl.ds(pl.multiple_of((2 * j + c) * r, FLAT_ROWS), r), :]
                o_refs[i][j] = (own.astype(F32) + h_refs[i][j].astype(F32)).astype(o_refs[i].dtype)

    return pl.pallas_call(
        body, name=name, out_shape=[jax.ShapeDtypeStruct(h.shape, h.dtype) for h in halves],
        compiler_params=pltpu.CompilerParams(vmem_limit_bytes=VMEM_LIMIT),
    )(*grads, *halves)


class _After:
    rels = frozenset()
    out_shape = []
    sems = []

    def __init__(self, arrays):
        self.inputs = list(arrays)

    def before(self, ins, outs, sems):
        pass

    def after(self, ins, outs, sems):
        pass


class _Both:
    def __init__(self, first, second):
        self.jobs = (first, second)
        self.rels = first.rels | second.rels
        self.inputs = first.inputs + second.inputs
        self.out_shape = first.out_shape + second.out_shape
        self.sems = first.sems + second.sems

    def _each(self, ins, outs, sems):
        a = self.jobs[0]
        i, o, s = len(a.inputs), len(a.out_shape), len(a.sems)
        return ((a, ins[:i], outs[:o], sems[:s]), (self.jobs[1], ins[i:], outs[o:], sems[s:]))

    def before(self, ins, outs, sems):
        for job, i, o, s in self._each(ins, outs, sems):
            job.before(i, o, s)

    def after(self, ins, outs, sems):
        for job, i, o, s in self._each(ins, outs, sems):
            job.after(i, o, s)


_COLLECTIVE_IDS = {(1,): 0, (1, 2, 4, 6): 2}


def _entry_barrier(rels):
    @pl.when(pl.program_id(0) == 0)
    def _():
        me = _mesh_position()
        sem = pltpu.get_barrier_semaphore()
        for rel in rels:
            pl.semaphore_signal(sem, inc=1, device_id=_flip(me, rel), device_id_type=MESH)
        pl.semaphore_wait(sem, len(rels))


def _run(body, comm, *, semantics, out_shape, in_specs, out_specs, scratch_shapes=(), **kw):
    if comm is None:
        return pl.pallas_call(body, out_shape=out_shape, in_specs=in_specs, out_specs=out_specs,
                              scratch_shapes=list(scratch_shapes), compiler_params=_params(semantics), **kw)
    single = not isinstance(out_shape, (list, tuple))
    outs = [out_shape] if single else list(out_shape)
    ospecs = [out_specs] if single else list(out_specs)
    counts = [len(in_specs), len(comm.inputs), len(outs), len(comm.out_shape), len(scratch_shapes), len(comm.sems)]
    rels = tuple(sorted(comm.rels))

    def carrying(*refs):
        groups, pos = [], 0
        for c in counts:
            groups.append(refs[pos:pos + c])
            pos += c
        main_in, comm_in, main_out, comm_out, main_scratch, comm_sems = groups
        _entry_barrier(rels)
        comm.before(comm_in, comm_out, comm_sems)
        body(*main_in, *main_out, *main_scratch)
        comm.after(comm_in, comm_out, comm_sems)

    call = pl.pallas_call(
        carrying, out_shape=outs + list(comm.out_shape), in_specs=list(in_specs) + [_ANY] * len(comm.inputs),
        out_specs=ospecs + [_ANY] * len(comm.out_shape), scratch_shapes=list(scratch_shapes) + list(comm.sems),
        compiler_params=pltpu.CompilerParams(dimension_semantics=("arbitrary",), vmem_limit_bytes=VMEM_LIMIT,
                                             collective_id=_COLLECTIVE_IDS[rels]), **kw)

    def apply(*args):
        res = call(*args, *comm.inputs)
        main = res[:len(outs)]
        return (main[0] if single else list(main)), list(res[len(outs):])

    return apply


def _alone(comm, name):
    return _run(lambda: None, comm, semantics="arbitrary", name=name, grid=(1,), out_shape=[], in_specs=[], out_specs=[])()[1]


SHARDED = {"w_in": 1, "w_glu": 0, "conv_w": 1, "w_a_out": 1, "w_b_out": 0, "w_o": 0, "w_ffn_gate": 1, "w_ffn_up": 1,
           "w_ffn_down": 0, "w_ple_gate": 0, "w_ple": 1}
TRANSPOSED = ("w_in", "w_a_out", "w_ffn_gate", "w_ffn_up", "w_ple")
LONG_AXIS_MINOR = ("w_in", "w_ffn_gate", "w_ffn_up")
NARROW_LAST = ("s5_b_re", "s5_b_im", "s5_d")
ADAMW_GROUPS = (("w_in",), ("w_glu",), ("conv_w",), ("w_a_out",), ("w_b_out", "w_o", "w_ple_gate"),
                ("w_ffn_gate", "w_ffn_up"), ("w_ffn_down",), ("w_ple",))
SMALL = ["g_mix", "b_in", "lam_re", "lam_im", "log_dt", "s5_b_re", "s5_b_im", "s5_c_re", "s5_c_im", "s5_d", "b_glu",
         "conv_b", "w_r", "b_r", "w_i", "b_i", "lru_lambda", "g_ffn", "g_ple_gate", "b_ple_gate", "g_ple", "g_final"]
WEIGHTS = ["g_mix", "w_in", "b_in", "lam_re", "lam_im", "log_dt", "s5_b_re", "s5_b_im", "s5_c_re", "s5_c_im", "s5_d",
           "w_glu", "b_glu", "conv_w", "conv_b", "w_r", "b_r", "w_i", "b_i", "lru_lambda", "w_a_out", "w_b_out", "w_o",
           "g_ffn", "w_ffn_gate", "w_ffn_up", "w_ffn_down", "g_ple_gate", "w_ple_gate", "b_ple_gate", "w_ple", "g_ple",
           "g_final"]


def _join_columns(gathered):
    nb, r, c = gathered.shape
    return jnp.transpose(gathered, (1, 0, 2)).reshape(r, nb * c)


def _disc_scalars(lr, li, ldt):
    dt = jnp.exp(ldt)
    mag = jnp.exp(lr * dt)
    ar = mag * jnp.cos(li * dt)
    ai = mag * jnp.sin(li * dt)
    den = lr * lr + li * li
    nr = ar - 1.0
    fr = (nr * lr + ai * li) / den
    fi = (ai * lr - nr * li) / den
    return ar, ai, fr, fi


def _disc_cols(lr, li, ldt, b_re, b_im):
    ar, ai, fr, fi = _disc_scalars(lr, li, ldt)
    return ar, ai, fr * b_re - fi * b_im, fr * b_im + fi * b_re


def _local_step(x, p, target, src, small, disc, distributed=True):
    full = {} if distributed else dict(src)
    gw, halves, pairs, got = {}, {}, {}, {}

    def gather(keys):
        return (_Gather([src[k] for k in keys]), keys, full) if distributed else None

    def swap(keys):
        return (_PairSwap([gw[k] for k in keys]), keys, halves) if distributed else None

    flights = []

    def add_pairs(keys, collective_id):
        if not distributed:
            return None
        sums = _pair_add([gw[k] for k in keys], [halves[k] for k in keys], "pair_add_" + keys[0])
        send, recv, thru, lands, token = _chips_start(sums, keys[0], collective_id)
        flights.append((keys, send, recv, thru, lands))
        return (_After([token]), [], {})

    def carry(fn, *args, jobs=()):
        jobs = [j for j in jobs if j is not None]
        if not jobs:
            return fn(*args)
        comm = jobs[0][0]
        for j in jobs[1:]:
            comm = _Both(comm, j[0])
        res, extra = fn(*args, comm=comm)
        for job, keys, sink in jobs:
            sink.update(zip(keys, extra[:len(job.out_shape)]))
            extra = extra[len(job.out_shape):]
        return res

    d = x.shape[1]
    g, n, pch = small["s5_b_re"].shape
    sa, lw = g * pch, small["lru_lambda"].shape[-1]
    widths = [sa, lw, d, d]
    row = lambda v: v.reshape(1, -1)

    hd = small["w_r"].shape[-1]
    ar_row, ai_row, bbr_blk, bbi_blk, cre_blk, cimn_blk, wr_blk, wi_blk = carry(
        _prep, *disc["rows"], *disc["cols"], disc["b_re"], disc["b_im"], small["s5_c_re"].reshape(sa, n),
        small["s5_c_im"].reshape(sa, n), small["w_r"].reshape(lw, hd), small["w_i"].reshape(lw, hd),
        jobs=[gather(["w_in"])])
    d_row = row(small["s5_d"])
    g_mix_row = row(small["g_mix"])
    if distributed:
        later = ["w_glu", "conv_w", "w_a_out", "w_b_out", "w_ffn_gate", "w_ffn_up", "w_o", "w_ffn_down", "w_ple_gate",
                 "w_ple"]
        wires = [src[k] for k in later]
        flight, lands, token = _gather_start(wires, "gather_start", after=[ar_row])
        g_mix_row = g_mix_row + token[0, 0]

        def arrive(keys, after):
            which = [later.index(k) for k in keys]
            passed, moved = _gather_pass(wires, which, flight, lands, "gather_pass_" + keys[0], after)
            for q, i in enumerate(which):
                lands[i] = moved[q]
            full.update(zip(keys, _gather_finish(wires, which, flight, passed, lands, (), "gather_finish_" + keys[0])))
    else:
        arrive = lambda keys, after: None

    u_a, u_b, za, zb = _inproj_fwd(x, g_mix_row, full["w_in"], row(small["b_in"]), widths)
    arrive(["w_glu", "conv_w", "w_a_out", "w_b_out"], [u_a])
    conv_w = _join_columns(full["conv_w"]) if distributed else full["conv_w"]
    branches = [_s5_fwd(u_a, bbr_blk, bbi_blk, ar_row, ai_row, cre_blk, cimn_blk, d_row, full["w_glu"],
                        row(small["b_glu"])),
                _lru_fwd(u_b, conv_w, row(small["conv_b"]), wr_blk, row(small["b_r"]), wi_blk, row(small["b_i"]),
                         row(small["lru_lambda"]))]
    (sr, si, y, y_a), (xc, h, hprev) = _stages(branches, "branches_fwd")
    arrive(["w_ffn_gate", "w_ffn_up", "w_o"], [y_a])
    x1, merged, ma, mb, fg, fu = _merge_ffn_up_fwd(y_a, h, za, zb, x, full["w_a_out"], full["w_b_out"], full["w_o"],
                                                   row(small["g_ffn"]), full["w_ffn_gate"], full["w_ffn_up"])
    arrive(["w_ffn_down", "w_ple_gate", "w_ple"], [x1])
    x2 = _ffn_down_fwd(fg, fu, x1, full["w_ffn_down"])

    dx2, loss_blk, gw["w_ple_gate"], gw["w_ple"], vec_tail = _tail_fwd_bwd(
        x2, p, target, row(small["g_ple_gate"]), full["w_ple_gate"], row(small["b_ple_gate"]), full["w_ple"],
        row(small["g_ple"]), row(small["g_final"]))
    dfg, dfu, act = carry(_ffn_bwd_a, dx2, fg, fu, full["w_ffn_down"], jobs=[swap(["w_ple_gate", "w_ple"])])
    tn_d = min(d, 512)
    gw["w_ffn_down"] = _matmul_tn(act, dx2, tn_d, "dw_ffn_down", BF16)
    started = add_pairs(["w_ple_gate", "w_ple"], 6)
    dx1, h2, vec_ffn = carry(_ffn_bwd_b, dfg, dfu, x1, dx2, row(small["g_ffn"]), full["w_ffn_gate"], full["w_ffn_up"],
                             jobs=[swap(["w_ffn_down"]), started])
    gw["w_ffn_gate"] = _matmul_tn(dfg, h2, tn_d, "dw_ffn_gate", BF16)
    gw["w_ffn_up"] = _matmul_tn(dfu, h2, tn_d, "dw_ffn_up", BF16)
    started = add_pairs(["w_ffn_down"], 7)
    dza, dzb, dya, dyb, gw["w_o"], gw["w_a_out"], gw["w_b_out"] = carry(
        _merge_bwd, dx1, merged, ma, mb, za, zb, y_a, h, full["w_o"], full["w_a_out"], full["w_b_out"],
        jobs=[swap(["w_ffn_gate", "w_ffn_up"]), started])
    started = add_pairs(["w_ffn_gate", "w_ffn_up"], 8)
    own = {}
    du_b, dw_r, dw_i, vec_lru = carry(
        _lru_bwd, dyb, xc, hprev, u_b, conv_w, wr_blk, row(small["b_r"]), wi_blk, row(small["b_i"]),
        row(small["lru_lambda"]), hd, jobs=[swap(["w_o", "w_a_out", "w_b_out"]), started])
    started = add_pairs(["w_o", "w_a_out", "w_b_out"], 9)
    du_a, lam_r, lam_i, dy16, gw["w_glu"], vec_sa, vec_gn = _s5_bwd(
        dya, y, sr, si, u_a, full["w_glu"], row(small["b_glu"]), cre_blk, cimn_blk, bbr_blk, bbi_blk, ar_row, ai_row, d_row)
    smalls = {"vec_tail": vec_tail, "vec_ffn": vec_ffn, "vec_lru": vec_lru, "vec_sa": vec_sa, "vec_gn": vec_gn,
              "dw_r": dw_r, "dw_i": dw_i, "loss": loss_blk}
    everyones = {}

    def gather_smalls(keys):
        return (_Gather([smalls[k] for k in keys]), keys, everyones) if distributed else None

    smalls["dbb_re"], smalls["dbb_im"], smalls["dc_re"], smalls["dc_imn"] = carry(
        _s5_param_grads, lam_r, lam_i, sr, si, u_a, dy16, pch, n,
        jobs=[swap(["w_glu"]), gather_smalls(list(smalls)), started])
    started = add_pairs(["w_glu"], 10)
    dz = [du_a, du_b, dza, dzb]
    grad_x, h0, smalls["vec_mix"], smalls["vec_bin"] = _inproj_bwd(
        dz, x, dx1, row(small["g_mix"]), full["w_in"])
    shapes = {k: a.shape for k, a in smalls.items()}
    gw["w_in"] = carry(_dw_from_parts, dz, h0, tn_d, "dw_in",
                       jobs=[gather_smalls(["dbb_re", "dbb_im", "dc_re", "dc_imn", "vec_mix", "vec_bin"]), started])
    smalls.update(everyones)
    if distributed:
        for keys, send, recv, thru, lands in flights:
            stayed, landed = _chips_wait(send, recv, thru, lands, [gw["w_in"]], keys[0])
            own.update(zip(keys, stayed))
            got.update(zip(keys, landed))
        got["w_in"] = gw["w_in"]
        gw = got
    return grad_x, gw, smalls, shapes, own


def _disc_inputs(small):
    g, n, pch = small["s5_b_re"].shape
    srcs = (small["lam_re"], small["lam_im"], jnp.repeat(small["log_dt"], n))
    return {"rows": [a.reshape(1, g * n) for a in srcs], "cols": [a.reshape(g * n, 1) for a in srcs],
            "b_re": small["s5_b_re"].reshape(g * n, pch), "b_im": small["s5_b_im"].reshape(g * n, pch)}


def kernel(x, p, g_mix, w_in, b_in, lam_re, lam_im, log_dt, s5_b_re, s5_b_im, s5_c_re, s5_c_im, s5_d, w_glu, b_glu, conv_w, conv_b, w_r, b_r, w_i, b_i, lru_lambda, w_a_out, w_b_out, w_o, g_ffn, w_ffn_gate, w_ffn_up, w_ffn_down, g_ple_gate, w_ple_gate, b_ple_gate, w_ple, g_ple, g_final, loss_target, m_g_mix, m_w_in, m_b_in, m_lam_re, m_lam_im, m_log_dt, m_s5_b_re, m_s5_b_im, m_s5_c_re, m_s5_c_im, m_s5_d, m_w_glu, m_b_glu, m_conv_w, m_conv_b, m_w_r, m_b_r, m_w_i, m_b_i, m_lru_lambda, m_w_a_out, m_w_b_out, m_w_o, m_g_ffn, m_w_ffn_gate, m_w_ffn_up, m_w_ffn_down, m_g_ple_gate, m_w_ple_gate, m_b_ple_gate, m_w_ple, m_g_ple, m_g_final, v_g_mix, v_w_in, v_b_in, v_lam_re, v_lam_im, v_log_dt, v_s5_b_re, v_s5_b_im, v_s5_c_re, v_s5_c_im, v_s5_d, v_w_glu, v_b_glu, v_conv_w, v_conv_b, v_w_r, v_b_r, v_w_i, v_b_i, v_lru_lambda, v_w_a_out, v_w_b_out, v_w_o, v_g_ffn, v_w_ffn_gate, v_w_ffn_up, v_w_ffn_down, v_g_ple_gate, v_w_ple_gate, v_b_ple_gate, v_w_ple, v_g_ple, v_g_final):
    given = dict(locals())
    wts = {k: given[k] for k in WEIGHTS}
    moms = {k: given["m_" + k] for k in WEIGHTS}
    vels = {k: given["v_" + k] for k in WEIGHTS}

    def drop_depth(k, a):
        return a if k == "g_final" else a[0]

    small = {k: drop_depth(k, wts[k]) for k in SMALL}
    shard = {k: wts[k][0] for k in SHARDED}
    names = list(SHARDED)

    def wire(k):
        if k == "conv_w":
            return shard[k]
        return (shard[k].T if k in TRANSPOSED else shard[k]).astype(BF16)

    disc = _disc_inputs(small)
    grad_x, parts, smalls, shapes, own = _local_step(x[0], p[0, 0], loss_target[0], {k: wire(k) for k in names}, small,
                                                     disc)

    (pair_in,) = _pair_add([parts["w_in"]], _alone(_PairSwap([parts["w_in"]]), "swap_last"), "pair_add_w_in")
    send_sems, recv_sems, pair_in, landing, token = _chips_start([pair_in], "w_in")
    ordered = (token,)

    g_small = _small_reduce(smalls, shapes, *disc["cols"], disc["b_re"], disc["b_im"], small["s5_b_re"].shape[0],
                            after=ordered)
    loss = g_small.pop("loss")[0, 0]
    cols = shard["conv_w"].shape[1]
    mine = _index((lax.axis_index("x"), lax.axis_index("y"), lax.axis_index("c")))
    parts["conv_w"] = lax.dynamic_slice_in_dim(g_small.pop("conv_w"), mine * cols, cols, axis=1)[None]

    def view(k, a):
        a = a.reshape((1, -1) if k == "g_final" else wts[k].shape)
        return jnp.swapaxes(a, -1, -2) if k in NARROW_LAST else a

    def unview(k, a):
        return (jnp.swapaxes(a, -1, -2) if k in NARROW_LAST else a).reshape(wts[k].shape)

    slots = _adamw_small([view(k, wts[k]) for k in SMALL], [view(k, g_small[k]) for k in SMALL],
                         [view(k, moms[k]) for k in SMALL], [view(k, vels[k]) for k in SMALL])
    small_out = [dict(zip(SMALL, [unview(k, a) for k, a in zip(SMALL, slot)])) for slot in slots]

    big_out = {}
    between = [slots[0][0]]
    for group in sorted(ADAMW_GROUPS, key=lambda g: g[0] == "w_in"):
        flip = group[0] in LONG_AXIS_MINOR
        look = (lambda a: a.T) if flip else (lambda a: a)
        last = group[0] == "w_in"
        if last:
            (own["w_in"],), (parts["w_in"],) = _chips_wait(send_sems, recv_sems, pair_in, landing, between, "w_in")
        res = _adamw([parts[k] for k in group], [look(shard[k]) for k in group], [look(moms[k][0]) for k in group],
                     [look(vels[k][0]) for k in group], "adamw_" + group[0],
                     transposed=group[0] in TRANSPOSED and not flip,
                     own=[own[k] for k in group] if group[0] in own else None, after=() if last else ordered)
        between.append(res[0][0])
        for k, outs4 in zip(group, res):
            big_out[k] = [look(a) for a in outs4]

    outs = [loss, grad_x[None]]
    for slot in range(4):
        for k in WEIGHTS:
            if k in SHARDED:
                outs.append(big_out[k][slot][None])
            else:
                outs.append(small_out[slot][k])
    return tuple(outs)
```

```python
import math

import jax
import jax.numpy as jnp
from jax import lax
from jax.experimental import pallas as pl
from jax.experimental.pallas import tpu as pltpu

F32 = jnp.float32
BF16 = jnp.bfloat16

EPS = 1e-6
LRU_C = 8.0
CONV_WIDTH = 4
ADAM_LR = 0.001
ADAM_B1 = 0.9
ADAM_B2 = 0.999
ADAM_EPS = 1e-08
ADAM_WD = 0.01
ADAM_STEP = 10

N_DEV = 8
MESH = pl.DeviceIdType.MESH
SUBLANES = 8
LANES = 128
VMEM_LIMIT = 56 * 1024 * 1024
TOKEN_TILE = 256
TIME_CHUNK = 256
S5_SLAB = 128
LRU_SLAB = 256


def _dot(a, b):
    return jnp.dot(a.astype(BF16), b.astype(BF16), preferred_element_type=F32)


def _dot_nt(a, b):
    return lax.dot_general(a.astype(BF16), b.astype(BF16), (((1,), (1,)), ((), ())), preferred_element_type=F32)


def _dot_tn(a, b):
    return lax.dot_general(a.astype(BF16), b.astype(BF16), (((0,), (0,)), ((), ())), preferred_element_type=F32)


def _sigmoid(x):
    return jax.nn.sigmoid(x)


def _rms_stats(x):
    r = lax.rsqrt(jnp.mean(x * x, axis=-1, keepdims=True) + EPS)
    return x * r, r


def _rms_bwd(dy, xhat, r, g):
    dxn = dy * g
    dx = r * (dxn - xhat * jnp.mean(dxn * xhat, axis=-1, keepdims=True))
    return dx, dy * xhat


def _rowsum(v):
    return jnp.sum(v, axis=0, keepdims=True)


def _expm1(x):
    u = jnp.exp(x)
    um1 = u - 1.0
    safe = jnp.where(um1 == 0.0, 1.0, jnp.log(u))
    return jnp.where(um1 == 0.0, x, um1 * x / safe)


def _softplus(x):
    e = jnp.exp(-jnp.abs(x))
    u = 1.0 + e
    um1 = u - 1.0
    safe = jnp.where(um1 == 0.0, 1.0, um1)
    log1p_e = jnp.where(um1 == 0.0, e, jnp.log(u) * e / safe)
    return jnp.maximum(x, 0.0) + log1p_e


_GELU_K = math.sqrt(2.0 / math.pi)
_GELU_C = 0.044715


def _gelu(x):
    return 0.5 * x * (1.0 + jnp.tanh(_GELU_K * (x + _GELU_C * x * x * x)))


def _gelu_grad(x):
    th = jnp.tanh(_GELU_K * (x + _GELU_C * x * x * x))
    return 0.5 * (1.0 + th) + 0.5 * x * (1.0 - th * th) * _GELU_K * (1.0 + 3.0 * _GELU_C * x * x)


def _params(*sem):
    return pltpu.CompilerParams(dimension_semantics=sem, vmem_limit_bytes=VMEM_LIMIT)


def _rows(tm, n):
    return pl.BlockSpec((tm, n), lambda i: (i, 0))


def _rows_rev(tm, n, steps):
    return pl.BlockSpec((tm, n), lambda i: (steps - 1 - i, 0))


def _whole(shape):
    nd = len(shape)
    return pl.BlockSpec(shape, lambda i: (0,) * nd, pipeline_mode=pl.Buffered(1))


def _acc(shape):
    nd = len(shape)
    return pl.BlockSpec(shape, lambda i: (0,) * nd)


def _zero_on_first(*refs):
    @pl.when(pl.program_id(0) == 0)
    def _():
        for r in refs:
            r[...] = jnp.zeros_like(r)


def _put_lanes(s, first_lane, value):
    for k in range(value.shape[1] // LANES):
        s[first_lane // LANES + k, 0:value.shape[0], :] = value[:, k * LANES:(k + 1) * LANES]


def _get_lanes(s, rows, lanes):
    return jnp.concatenate([s[k, rows, :] for k in range(lanes.start // LANES, lanes.stop // LANES)], axis=1)


def _scan_setup(ar_ref, ai_ref, ar_s, ai_s, pr_s, pi_s, cr_s, ci_s, reverse):
    @pl.when(pl.program_id(0) == 0)
    def _():
        cr_s[...] = jnp.zeros_like(cr_s)
        ci_s[...] = jnp.zeros_like(ci_s)
        for k in range(ar_s.shape[0]):
            lanes = slice(k * LANES, (k + 1) * LANES)
            a_i = ai_ref[:, lanes]
            ar_s[k] = jnp.broadcast_to(ar_ref[:, lanes], ar_s.shape[1:])
            ai_s[k] = jnp.broadcast_to(-a_i if reverse else a_i, ai_s.shape[1:])
        a_r, a_i = ar_s[:, 0:1, :], ai_s[:, 0:1, :]
        n = pr_s.shape[1]
        first = n - 1 if reverse else 0
        pr_s[:, first:first + 1, :] = a_r
        pi_s[:, first:first + 1, :] = a_i

        def step(k, carry):
            p_r, p_i = carry
            at = pl.ds(n - 1 - k if reverse else k, 1)
            n_r = a_r * p_r - a_i * p_i
            n_i = a_r * p_i + a_i * p_r
            pr_s[:, at, :] = n_r
            pi_s[:, at, :] = n_i
            return n_r, n_i

        lax.fori_loop(1, n, step, (a_r, a_i))


def _complex_scan(xr_s, xi_s, tc, ar_s, ai_s, pr_s, pi_s, cr_s, ci_s, reverse):
    seg = tc // SUBLANES

    def step(k, carry):
        c_r, c_i = carry
        a_r, a_i = ar_s[...], ai_s[...]
        at = pl.ds(seg - 1 - k if reverse else k, SUBLANES, stride=seg)
        n_r = a_r * c_r - a_i * c_i + xr_s[:, at, :]
        n_i = a_r * c_i + a_i * c_r + xi_s[:, at, :]
        xr_s[:, at, :] = n_r
        xi_s[:, at, :] = n_i
        return n_r, n_i

    zero = jnp.zeros(ar_s.shape, F32)
    lax.fori_loop(0, seg, step, (zero, zero), unroll=2)
    c_r, c_i = cr_s[:, 0:1, :], ci_s[:, 0:1, :]
    for j in (range(SUBLANES - 1, -1, -1) if reverse else range(SUBLANES)):
        rows = slice(j * seg, (j + 1) * seg)
        x_r, x_i, p_r, p_i = xr_s[:, rows, :], xi_s[:, rows, :], pr_s[...], pi_s[...]
        xr_s[:, rows, :] = x_r + p_r * c_r - p_i * c_i
        xi_s[:, rows, :] = x_i + p_r * c_i + p_i * c_r
        last = j * seg if reverse else (j + 1) * seg - 1
        c_r, c_i = xr_s[:, last:last + 1, :], xi_s[:, last:last + 1, :]
    cr_s[:, 0:1, :] = c_r
    ci_s[:, 0:1, :] = c_i


def _inproj_fwd(x, g_mix, w_in_t, b_in, widths, comm=None):
    t, d = x.shape
    n = w_in_t.shape[0]
    tm = min(TOKEN_TILE, t)
    offs = [sum(widths[:i]) for i in range(len(widths) + 1)]

    def body(x_ref, g_ref, w_ref, b_ref, *outs):
        xhat, _ = _rms_stats(x_ref[...])
        h = (xhat * g_ref[...]).astype(BF16)
        for k, o_ref in enumerate(outs):
            lo, hi = offs[k], offs[k + 1]
            o_ref[...] = _dot_nt(h, w_ref[lo:hi, :]) + b_ref[:, lo:hi]

    return _run(
        body, comm, name="inproj_fwd", grid=(t // tm,),
        out_shape=[jax.ShapeDtypeStruct((t, w), F32) for w in widths],
        in_specs=[_rows(tm, d), _whole((1, d)), _whole((n, d)), _whole((1, n))],
        out_specs=[_rows(tm, w) for w in widths],
        semantics="parallel",
    )(x, g_mix, w_in_t, b_in)


def _s5_fwd(u, bbr_blk, bbi_blk, ar, ai, cre_blk, cimn_blk, d_skip, w_glu, b_glu):
    t, sa = u.shape
    ns, _, sw = bbr_blk.shape
    gn = ns * sw
    tc = min(TIME_CHUNK, t)

    def body(u_ref, bbr_ref, bbi_ref, ar_ref, ai_ref, cre_ref, cim_ref, d_ref, wg_ref, bg_ref,
             sr_ref, si_ref, y_ref, ya_ref, cr_s, ci_s, sr_s, si_s, ar_s, ai_s, pr_s, pi_s):
        _scan_setup(ar_ref, ai_ref, ar_s, ai_s, pr_s, pi_s, cr_s, ci_s, reverse=False)
        uv = u_ref[...]
        ub = uv.astype(BF16)
        for m in range(ns):
            um = ub[:, m * S5_SLAB:(m + 1) * S5_SLAB]
            _put_lanes(sr_s, m * sw, _dot(um, bbr_ref[m]))
            _put_lanes(si_s, m * sw, _dot(um, bbi_ref[m]))
        _complex_scan(sr_s, si_s, tc, ar_s, ai_s, pr_s, pi_s, cr_s, ci_s, reverse=False)
        for m in range(ns):
            states, chans = slice(m * sw, (m + 1) * sw), slice(m * S5_SLAB, (m + 1) * S5_SLAB)
            s_r = _get_lanes(sr_s, slice(0, tc), states).astype(BF16)
            s_i = _get_lanes(si_s, slice(0, tc), states).astype(BF16)
            sr_ref[:, states] = s_r.astype(sr_ref.dtype)
            si_ref[:, states] = s_i.astype(si_ref.dtype)
            y_ref[:, chans] = _dot(s_r, cre_ref[m]) + _dot(s_i, cim_ref[m]) + d_ref[:, chans] * uv[:, chans]
        y = y_ref[...]
        zz = _gelu(y)
        q = _dot(zz, wg_ref[...]) + bg_ref[...]
        ya_ref[...] = zz * _sigmoid(q)

    return dict(
        body=body, steps=t // tc, args=(u, bbr_blk, bbi_blk, ar, ai, cre_blk, cimn_blk, d_skip, w_glu, b_glu),
        out_shape=[jax.ShapeDtypeStruct((t, gn), BF16), jax.ShapeDtypeStruct((t, gn), BF16),
                   jax.ShapeDtypeStruct((t, sa), F32), jax.ShapeDtypeStruct((t, sa), F32)],
        in_specs=[_rows(tc, sa), _whole(bbr_blk.shape), _whole(bbi_blk.shape), _whole((1, gn)), _whole((1, gn)),
                  _whole(cre_blk.shape), _whole(cimn_blk.shape), _whole((1, sa)), _whole((sa, sa)), _whole((1, sa))],
        out_specs=[_rows(tc, gn), _rows(tc, gn), _rows(tc, sa), _rows(tc, sa)],
        scratch=[pltpu.VMEM((gn // LANES, SUBLANES, LANES), F32)] * 2 + [pltpu.VMEM((gn // LANES, tc, LANES), F32)] * 2
        + [pltpu.VMEM((gn // LANES, SUBLANES, LANES), F32)] * 2
        + [pltpu.VMEM((gn // LANES, tc // SUBLANES, LANES), F32)] * 2)


def _stages(stages, name, comm=None):
    counts = [[len(s[k]) for s in stages] for k in ("args", "out_shape", "scratch")]

    def body(*refs):
        groups, pos = [], 0
        for kind in counts:
            per_stage = []
            for c in kind:
                per_stage.append(refs[pos:pos + c])
                pos += c
            groups.append(per_stage)
        for s, ins, outs, scratch in zip(stages, *groups):
            s["body"](*ins, *outs, *scratch)

    res = _run(
        body, comm, name=name, grid=(stages[0]["steps"],),
        out_shape=[o for s in stages for o in s["out_shape"]], in_specs=[i for s in stages for i in s["in_specs"]],
        out_specs=[o for s in stages for o in s["out_specs"]], scratch_shapes=[c for s in stages for c in s["scratch"]],
        semantics="arbitrary",
    )(*[a for s in stages for a in s["args"]])
    extra = None
    if comm is not None:
        res, extra = res
    per_stage, pos = [], 0
    for c in counts[1]:
        per_stage.append(list(res[pos:pos + c]))
        pos += c
    return per_stage if comm is None else (per_stage, extra)


def _slab_dot(x, w_ref, transposed=False):
    dot = _dot_nt if transposed else _dot
    xb = x.astype(BF16)
    return jnp.concatenate([dot(xb[:, j * LRU_SLAB:(j + 1) * LRU_SLAB], w_ref[j]) for j in range(w_ref.shape[0])], axis=1)


def _lru_gates(xc, wr_ref, br_ref, wi_ref, bi_ref, lam_ref):
    r = _sigmoid(_slab_dot(xc, wr_ref) + br_ref[...])
    ig = _sigmoid(_slab_dot(xc, wi_ref) + bi_ref[...])
    sp = _softplus(-lam_ref[...])
    log_a = (-LRU_C * r) * sp
    return r, ig, sp, log_a


def _lru_fwd(u, conv_w, conv_b, wr_blk, b_r, wi_blk, b_i, lru_lambda):
    t, w = u.shape
    tc = min(TIME_CHUNK, t)
    halo = SUBLANES

    def body(u_ref, cw_ref, cb_ref, wr_ref, br_ref, wi_ref, bi_ref, lam_ref,
             xc_ref, h_ref, hp_ref, ext_s, a_s, carry_s):
        @pl.when(pl.program_id(0) == 0)
        def _():
            ext_s[0:halo, :] = jnp.zeros((halo, w), F32)
            carry_s[...] = jnp.zeros_like(carry_s)

        ext_s[halo:halo + tc, :] = u_ref[...]
        xc = cb_ref[...]
        for k in range(CONV_WIDTH):
            off = halo - (CONV_WIDTH - 1) + k
            xc = xc + cw_ref[k:k + 1, :] * ext_s[off:off + tc, :]
        ext_s[0:halo, :] = ext_s[tc:tc + halo, :]
        xc_ref[...] = xc
        r, ig, sp, log_a = _lru_gates(xc, wr_ref, br_ref, wi_ref, bi_ref, lam_ref)
        a_s[...] = jnp.exp(log_a)
        h_ref[...] = jnp.sqrt(-_expm1(2.0 * log_a)) * ig * xc

        def step(row, carry):
            at = pl.ds(row, 1)
            hp_ref[at, :] = carry
            nxt = a_s[at, :] * carry + h_ref[at, :]
            h_ref[at, :] = nxt
            return nxt

        carry_s[0:1, :] = lax.fori_loop(0, tc, step, carry_s[0:1, :], unroll=8)

    return dict(
        body=body, steps=t // tc, args=(u, conv_w, conv_b, wr_blk, b_r, wi_blk, b_i, lru_lambda),
        out_shape=[jax.ShapeDtypeStruct((t, w), F32)] * 3,
        in_specs=[_rows(tc, w), _whole((CONV_WIDTH, w)), _whole((1, w)), _whole(wr_blk.shape), _whole((1, w)),
                  _whole(wi_blk.shape), _whole((1, w)), _whole((1, w))],
        out_specs=[_rows(tc, w)] * 3,
        scratch=[pltpu.VMEM((halo + tc, w), F32), pltpu.VMEM((tc, w), F32), pltpu.VMEM((SUBLANES, w), F32)])


def _merge_ffn_up_fwd(y_a, h, za, zb, x, w_a_out_t, w_b_out, w_o, g_ffn, w_gate_t, w_up_t, comm=None):
    t, d = x.shape
    sa, lw = y_a.shape[1], h.shape[1]
    f = w_gate_t.shape[0]
    tm = min(TOKEN_TILE, t)

    def body(ya_ref, h_ref, za_ref, zb_ref, x_ref, wa_ref, wb_ref, wo_ref, g_ref, wg_ref, wu_ref,
             x1_ref, mg_ref, ma_ref, mb_ref, fg_ref, fu_ref):
        ma = _dot_nt(ya_ref[...], wa_ref[...])
        mb = _dot(h_ref[...], wb_ref[...])
        merged = _sigmoid(za_ref[...]) * ma + _sigmoid(zb_ref[...]) * mb
        ma_ref[...] = ma.astype(ma_ref.dtype)
        mb_ref[...] = mb.astype(mb_ref.dtype)
        mg_ref[...] = merged.astype(mg_ref.dtype)
        x1 = x_ref[...] + _dot(merged, wo_ref[...])
        x1_ref[...] = x1
        xhat, _ = _rms_stats(x1)
        h2 = (xhat * g_ref[...]).astype(BF16)
        fg_ref[...] = _dot_nt(h2, wg_ref[...]).astype(fg_ref.dtype)
        fu_ref[...] = _dot_nt(h2, wu_ref[...]).astype(fu_ref.dtype)

    return _run(
        body, comm, name="merge_ffn_up_fwd", grid=(t // tm,),
        out_shape=[jax.ShapeDtypeStruct((t, d), F32), jax.ShapeDtypeStruct((t, d), BF16),
                   jax.ShapeDtypeStruct((t, d), BF16), jax.ShapeDtypeStruct((t, d), BF16),
                   jax.ShapeDtypeStruct((t, f), BF16), jax.ShapeDtypeStruct((t, f), BF16)],
        in_specs=[_rows(tm, sa), _rows(tm, lw), _rows(tm, d), _rows(tm, d), _rows(tm, d),
                  _whole((d, sa)), _whole((lw, d)), _whole((d, d)), _whole((1, d)), _whole((f, d)), _whole((f, d))],
        out_specs=[_rows(tm, d)] * 4 + [_rows(tm, f)] * 2,
        semantics="parallel",
    )(y_a, h, za, zb, x, w_a_out_t, w_b_out, w_o, g_ffn, w_gate_t, w_up_t)


def _ffn_down_fwd(fg, fu, x1, w_down, comm=None):
    t, d = x1.shape
    f = fg.shape[1]
    tm = min(TOKEN_TILE, t)

    def body(fg_ref, fu_ref, x_ref, wd_ref, x2_ref):
        fgv = fg_ref[...].astype(F32)
        act = fgv * _sigmoid(fgv) * fu_ref[...].astype(F32)
        x2_ref[...] = x_ref[...] + _dot(act, wd_ref[...])

    return _run(
        body, comm, name="ffn_down_fwd", grid=(t // tm,),
        out_shape=jax.ShapeDtypeStruct((t, d), F32),
        in_specs=[_rows(tm, f), _rows(tm, f), _rows(tm, d), _whole((f, d))],
        out_specs=_rows(tm, d),
        semantics="parallel",
    )(fg, fu, x1, w_down)


def _store_on_last(pairs):
    @pl.when(pl.program_id(0) == pl.num_programs(0) - 1)
    def _():
        for acc, out in pairs:
            out[...] = acc[...].astype(out.dtype)


def _tail_fwd_bwd(x2, p, target, g_pg, w_pg, b_pg, w_ple_t, g_ple, g_final):
    t, d = x2.shape
    pd = p.shape[1]
    tm = min(TOKEN_TILE, t)

    def body(x2_ref, p_ref, tg_ref, gpg_ref, wpg_ref, bpg_ref, wple_ref, gple_ref, gfin_ref,
             dx2_ref, loss_ref, dwpg_out, dwple_out, vec_ref, dwpg_ref, dwple_ref):
        _zero_on_first(loss_ref, dwpg_ref, dwple_ref, vec_ref)
        x2v = x2_ref[...]
        xh2, r2 = _rms_stats(x2v)
        h3 = xh2 * gpg_ref[...]
        gp = _sigmoid(_dot(h3, wpg_ref[...]) + bpg_ref[...])
        pe = _dot_nt(p_ref[...], wple_ref[...])
        peh, r3 = _rms_stats(pe)
        e = peh * gple_ref[...]
        x3 = x2v + gp * e
        xh3, r4 = _rms_stats(x3)
        diff = xh3 * gfin_ref[...] - tg_ref[...]
        loss_ref[...] += 0.5 * jnp.sum(jnp.mean(diff * diff, axis=-1, keepdims=True))
        dy = diff * (1.0 / d)
        dx3, dgfin = _rms_bwd(dy, xh3, r4, gfin_ref[...])
        d_gp = dx3 * e
        d_e = dx3 * gp
        dpe, dgple = _rms_bwd(d_e, peh, r3, gple_ref[...])
        dwple_ref[...] += _dot_tn(dpe, p_ref[...])
        dpre = d_gp * gp * (1.0 - gp)
        dwpg_ref[...] += _dot_tn(h3, dpre)
        dh3 = _dot_nt(dpre, wpg_ref[...])
        dx2n, dgpg = _rms_bwd(dh3, xh2, r2, gpg_ref[...])
        dx2_ref[...] = dx3 + dx2n
        vec_ref[0:1, :] += _rowsum(dpre)
        vec_ref[1:2, :] += _rowsum(dgpg)
        vec_ref[2:3, :] += _rowsum(dgple)
        vec_ref[3:4, :] += _rowsum(dgfin)
        _store_on_last([(dwpg_ref, dwpg_out), (dwple_ref, dwple_out)])

    return pl.pallas_call(
        body, name="tail_fwd_bwd", grid=(t // tm,),
        out_shape=[jax.ShapeDtypeStruct((t, d), F32), jax.ShapeDtypeStruct((SUBLANES, LANES), F32),
                   jax.ShapeDtypeStruct((d, d), BF16), jax.ShapeDtypeStruct((d, pd), BF16),
                   jax.ShapeDtypeStruct((SUBLANES, d), F32)],
        in_specs=[_rows(tm, d), _rows(tm, pd), _rows(tm, d), _whole((1, d)), _whole((d, d)), _whole((1, d)),
                  _whole((d, pd)), _whole((1, d)), _whole((1, d))],
        out_specs=[_rows(tm, d), _acc((SUBLANES, LANES)), _acc((d, d)), _acc((d, pd)), _acc((SUBLANES, d))],
        scratch_shapes=[pltpu.VMEM((d, d), F32), pltpu.VMEM((d, pd), F32)],
        compiler_params=_params("arbitrary"),
    )(x2, p, target, g_pg, w_pg, b_pg, w_ple_t, g_ple, g_final)


def _ffn_bwd_a(dx2, fg, fu, w_down, comm=None):
    t, d = dx2.shape
    f = fg.shape[1]
    tm = min(TOKEN_TILE, t)

    def body(dx_ref, fg_ref, fu_ref, wd_ref, dfg_ref, dfu_ref, act_ref):
        dact = _dot_nt(dx_ref[...], wd_ref[...])
        fgv = fg_ref[...].astype(F32)
        fuv = fu_ref[...].astype(F32)
        sg = _sigmoid(fgv)
        silu = fgv * sg
        dfu_ref[...] = (dact * silu).astype(dfu_ref.dtype)
        dfg_ref[...] = (dact * fuv * (sg * (1.0 + fgv * (1.0 - sg)))).astype(dfg_ref.dtype)
        act_ref[...] = (silu * fuv).astype(act_ref.dtype)

    return _run(
        body, comm, name="ffn_bwd_a", grid=(t // tm,),
        out_shape=[jax.ShapeDtypeStruct((t, f), BF16)] * 3,
        in_specs=[_rows(tm, d), _rows(tm, f), _rows(tm, f), _whole((f, d))],
        out_specs=[_rows(tm, f)] * 3,
        semantics="parallel",
    )(dx2, fg, fu, w_down)


def _ffn_bwd_b(dfg, dfu, x1, dx2, g_ffn, w_gate_t, w_up_t, comm=None):
    t, d = x1.shape
    f = dfg.shape[1]
    tm = min(TOKEN_TILE, t)

    def body(dfg_ref, dfu_ref, x_ref, dx2_ref, g_ref, wg_ref, wu_ref, dx1_ref, h2_ref, vec_ref):
        _zero_on_first(vec_ref)
        dh2 = _dot(dfg_ref[...], wg_ref[...]) + _dot(dfu_ref[...], wu_ref[...])
        xhat, r = _rms_stats(x_ref[...])
        h2_ref[...] = (xhat * g_ref[...]).astype(h2_ref.dtype)
        dxn, dg = _rms_bwd(dh2, xhat, r, g_ref[...])
        dx1_ref[...] = dx2_ref[...] + dxn
        vec_ref[0:1, :] += _rowsum(dg)

    return _run(
        body, comm, name="ffn_bwd_b", grid=(t // tm,),
        out_shape=[jax.ShapeDtypeStruct((t, d), F32), jax.ShapeDtypeStruct((t, d), BF16),
                   jax.ShapeDtypeStruct((SUBLANES, d), F32)],
        in_specs=[_rows(tm, f), _rows(tm, f), _rows(tm, d), _rows(tm, d), _whole((1, d)), _whole((f, d)), _whole((f, d))],
        out_specs=[_rows(tm, d), _rows(tm, d), _acc((SUBLANES, d))],
        semantics="arbitrary",
    )(dfg, dfu, x1, dx2, g_ffn, w_gate_t, w_up_t)


def _matmul_tn(a, b, tn, name, dtype=F32):
    t, k = a.shape
    n = b.shape[1]

    def body(a_ref, b_ref, o_ref):
        o_ref[...] = _dot_tn(a_ref[...], b_ref[...]).astype(o_ref.dtype)

    return _run(
        body, None, name=name, grid=(n // tn,),
        out_shape=jax.ShapeDtypeStruct((k, n), dtype),
        in_specs=[_whole((t, k)), pl.BlockSpec((t, tn), lambda j: (0, j))],
        out_specs=pl.BlockSpec((k, tn), lambda j: (0, j)),
        semantics="parallel",
    )(a, b)


def _merge_bwd(dx1, merged, ma, mb, za, zb, y_a, h, w_o, w_a_out_t, w_b_out, comm=None):
    t, d = dx1.shape
    sa, lw = y_a.shape[1], h.shape[1]
    tm = min(TOKEN_TILE, t)

    def body(dx1_ref, mg_ref, ma_ref, mb_ref, za_ref, zb_ref, ya_ref, h_ref, wo_ref, wa_ref, wb_ref,
             dza_ref, dzb_ref, dya_ref, dyb_ref, dwo_out, dwa_out, dwb_out, dwo_ref, dwa_ref, dwb_ref):
        _zero_on_first(dwo_ref, dwa_ref, dwb_ref)
        dx1v = dx1_ref[...].astype(BF16)
        dmg = _dot_nt(dx1v, wo_ref[...])
        ga = _sigmoid(za_ref[...])
        gb = _sigmoid(zb_ref[...])
        dza_ref[...] = (dmg * ma_ref[...].astype(F32) * ga * (1.0 - ga)).astype(dza_ref.dtype)
        dzb_ref[...] = (dmg * mb_ref[...].astype(F32) * gb * (1.0 - gb)).astype(dzb_ref.dtype)
        dma = (dmg * ga).astype(BF16)
        dmb = (dmg * gb).astype(BF16)
        dya_ref[...] = _dot(dma, wa_ref[...])
        dyb_ref[...] = _dot_nt(dmb, wb_ref[...])
        dwo_ref[...] += _dot_tn(mg_ref[...], dx1v)
        dwa_ref[...] += _dot_tn(dma, ya_ref[...])
        dwb_ref[...] += _dot_tn(h_ref[...], dmb)
        _store_on_last([(dwo_ref, dwo_out), (dwa_ref, dwa_out), (dwb_ref, dwb_out)])

    return _run(
        body, comm, name="merge_bwd", grid=(t // tm,),
        out_shape=[jax.ShapeDtypeStruct((t, d), BF16), jax.ShapeDtypeStruct((t, d), BF16),
                   jax.ShapeDtypeStruct((t, sa), F32), jax.ShapeDtypeStruct((t, lw), F32),
                   jax.ShapeDtypeStruct((d, d), BF16), jax.ShapeDtypeStruct((d, sa), BF16),
                   jax.ShapeDtypeStruct((lw, d), BF16)],
        in_specs=[_rows(tm, d), _rows(tm, d), _rows(tm, d), _rows(tm, d), _rows(tm, d), _rows(tm, d),
                  _rows(tm, sa), _rows(tm, lw), _whole((d, d)), _whole((d, sa)), _whole((lw, d))],
        out_specs=[_rows(tm, d), _rows(tm, d), _rows(tm, sa), _rows(tm, lw), _acc((d, d)), _acc((d, sa)), _acc((lw, d))],
        scratch_shapes=[pltpu.VMEM((d, d), F32), pltpu.VMEM((d, sa), F32), pltpu.VMEM((lw, d), F32)],
        semantics="arbitrary",
    )(dx1, merged, ma, mb, za, zb, y_a, h, w_o, w_a_out_t, w_b_out)


def _fold_diag_blocks(dense, row_group, col_group):
    r, c = dense.shape
    rows = lax.broadcasted_iota(jnp.int32, (r, c), 0)
    cols = lax.broadcasted_iota(jnp.int32, (r, c), 1)
    kept = jnp.where(rows // row_group == cols // col_group, dense, 0.0)
    pick = (lax.broadcasted_iota(jnp.int32, (row_group, r), 0)
            == lax.broadcasted_iota(jnp.int32, (row_group, r), 1) % row_group).astype(F32)
    return jnp.dot(pick, kept, preferred_element_type=F32, precision=lax.Precision.HIGHEST)


def _lru_bwd(dh, xc, hprev, u, conv_w, wr_blk, b_r, wi_blk, b_i, lru_lambda, head_dim, comm=None):
    t, w = dh.shape
    tc = min(TIME_CHUNK, t)
    steps = t // tc
    halo = SUBLANES
    sub_per_chunk = tc // halo
    slabs = w // LRU_SLAB

    def body(dh_ref, xc_ref, hp_ref, u_ref, uh_ref, cw_ref, wr_ref, br_ref, wi_ref, bi_ref, lam_ref,
             du_ref, dwr_out, dwi_out, vec_ref, lam_s, a_s, dxc_s, uext_s, carry_s, dwr_ref, dwi_ref):
        chunk = steps - 1 - pl.program_id(0)

        @pl.when(pl.program_id(0) == 0)
        def _():
            carry_s[...] = jnp.zeros_like(carry_s)
            dxc_s[tc:tc + halo, :] = jnp.zeros((halo, w), F32)
            dwr_ref[...] = jnp.zeros_like(dwr_ref)
            dwi_ref[...] = jnp.zeros_like(dwi_ref)
            vec_ref[...] = jnp.zeros_like(vec_ref)

        xc = xc_ref[...]
        r, ig, sp, log_a = _lru_gates(xc, wr_ref, br_ref, wi_ref, bi_ref, lam_ref)
        a = jnp.exp(log_a)
        a_s[...] = a

        def step(i, q):
            at = pl.ds(tc - 1 - i, 1)
            lam_row = dh_ref[at, :] + q
            lam_s[at, :] = lam_row
            return a_s[at, :] * lam_row

        carry_s[0:1, :] = lax.fori_loop(0, tc, step, carry_s[0:1, :], unroll=8)
        lam = lam_s[...]
        mult = jnp.sqrt(-_expm1(2.0 * log_a))
        d_log_a = lam * hp_ref[...] * a - (lam * ig * xc) * (a * a) / mult
        d_ig = lam * mult * xc
        dpre_r = (d_log_a * (-LRU_C * sp)) * r * (1.0 - r)
        dpre_i = d_ig * ig * (1.0 - ig)
        dxc = lam * mult * ig + _slab_dot(dpre_r, wr_ref, transposed=True) + _slab_dot(dpre_i, wi_ref, transposed=True)
        xcb, drb, dib = xc.astype(BF16), dpre_r.astype(BF16), dpre_i.astype(BF16)
        for j in range(slabs):
            cols = slice(j * LRU_SLAB, (j + 1) * LRU_SLAB)
            dwr_ref[j] += _dot_tn(drb[:, cols], xcb[:, cols])
            dwi_ref[j] += _dot_tn(dib[:, cols], xcb[:, cols])
        vec_ref[0:1, :] += _rowsum(dxc)
        vec_ref[1:2, :] += _rowsum(dpre_r)
        vec_ref[2:3, :] += _rowsum(dpre_i)
        vec_ref[3:4, :] += _rowsum(d_log_a * (-LRU_C * r)) * (-_sigmoid(-lam_ref[...]))
        dxc_s[0:tc, :] = dxc
        du = cw_ref[CONV_WIDTH - 1:CONV_WIDTH, :] * dxc
        for k in range(CONV_WIDTH - 1):
            off = CONV_WIDTH - 1 - k
            du = du + cw_ref[k:k + 1, :] * dxc_s[off:off + tc, :]
        du_ref[...] = du.astype(du_ref.dtype)
        dxc_s[tc:tc + halo, :] = dxc_s[0:halo, :]
        uext_s[0:halo, :] = jnp.where(chunk > 0, uh_ref[...], 0.0)
        uext_s[halo:halo + tc, :] = u_ref[...]
        for k in range(CONV_WIDTH):
            off = halo - (CONV_WIDTH - 1) + k
            vec_ref[4 + k:5 + k, :] += _rowsum(dxc * uext_s[off:off + tc, :])

        @pl.when(pl.program_id(0) == steps - 1)
        def _():
            for j in range(slabs):
                cols = slice(j * LRU_SLAB, (j + 1) * LRU_SLAB)
                dwr_out[:, cols] = _fold_diag_blocks(dwr_ref[j], head_dim, head_dim).astype(dwr_out.dtype)
                dwi_out[:, cols] = _fold_diag_blocks(dwi_ref[j], head_dim, head_dim).astype(dwi_out.dtype)

    halo_spec = pl.BlockSpec((halo, w), lambda i: (jnp.maximum((steps - 1 - i) * sub_per_chunk - 1, 0), 0))
    return _run(
        body, comm, name="lru_bwd", grid=(steps,),
        out_shape=[jax.ShapeDtypeStruct((t, w), BF16), jax.ShapeDtypeStruct((head_dim, w), BF16),
                   jax.ShapeDtypeStruct((head_dim, w), BF16), jax.ShapeDtypeStruct((SUBLANES, w), F32)],
        in_specs=[_rows_rev(tc, w, steps)] * 4 + [halo_spec, _whole((CONV_WIDTH, w)), _whole(wr_blk.shape),
                                                  _whole((1, w)), _whole(wi_blk.shape), _whole((1, w)), _whole((1, w))],
        out_specs=[_rows_rev(tc, w, steps), _acc((head_dim, w)), _acc((head_dim, w)), _acc((SUBLANES, w))],
        scratch_shapes=[pltpu.VMEM((tc, w), F32), pltpu.VMEM((tc, w), F32), pltpu.VMEM((tc + halo, w), F32),
                        pltpu.VMEM((halo + tc, w), F32), pltpu.VMEM((SUBLANES, w), F32),
                        pltpu.VMEM((slabs, LRU_SLAB, LRU_SLAB), F32), pltpu.VMEM((slabs, LRU_SLAB, LRU_SLAB), F32)],
        semantics="arbitrary",
    )(dh, xc, hprev, u, u, conv_w, wr_blk, b_r, wi_blk, b_i, lru_lambda)


def _s5_bwd(dya, y, sr, si, u, w_glu, b_glu, cre_blk, cimn_blk, bbr_blk, bbi_blk, ar, ai, d_skip, comm=None):
    t, sa = dya.shape
    gn = sr.shape[1]
    ns, _, sw = bbr_blk.shape
    tc = min(TIME_CHUNK, t)
    steps = t // tc
    halo = SUBLANES

    def body(dya_ref, y_ref, sr_ref, si_ref, u_ref, wg_ref, bg_ref, cre_ref, cim_ref, bbr_ref, bbi_ref,
             ar_ref, ai_ref, d_ref, du_ref, lr_ref, li_ref, dy_ref, dwg_out, vsa_ref, vgn_ref, gr_s, gi_s, cr_s, ci_s,
             dwg_ref, ar_s, ai_s, pr_s, pi_s):
        _scan_setup(ar_ref, ai_ref, ar_s, ai_s, pr_s, pi_s, cr_s, ci_s, reverse=True)

        @pl.when(pl.program_id(0) == 0)
        def _():
            gr_s[:, tc:tc + halo, :] = jnp.zeros((gn // LANES, halo, LANES), F32)
            gi_s[:, tc:tc + halo, :] = jnp.zeros((gn // LANES, halo, LANES), F32)
            dwg_ref[...] = jnp.zeros_like(dwg_ref)
            vsa_ref[...] = jnp.zeros_like(vsa_ref)
            vgn_ref[...] = jnp.zeros_like(vgn_ref)

        yv = y_ref[...]
        uv = u_ref[...]
        zz = _gelu(yv)
        sg = _sigmoid(_dot(zz, wg_ref[...]) + bg_ref[...])
        dyav = dya_ref[...]
        dq = dyav * zz * sg * (1.0 - sg)
        dzz = dyav * sg + _dot_nt(dq, wg_ref[...])
        dwg_ref[...] += _dot_tn(zz, dq)
        dy = dzz * _gelu_grad(yv)
        dyb = dy.astype(BF16)
        dy_ref[...] = dyb.astype(dy_ref.dtype)
        vsa_ref[0:1, :] += _rowsum(dq)
        vsa_ref[1:2, :] += _rowsum(dy * uv)
        for m in range(ns):
            dym = dyb[:, m * S5_SLAB:(m + 1) * S5_SLAB]
            _put_lanes(gr_s, m * sw, _dot_nt(dym, cre_ref[m]))
            _put_lanes(gi_s, m * sw, _dot_nt(dym, cim_ref[m]))
        _complex_scan(gr_s, gi_s, tc, ar_s, ai_s, pr_s, pi_s, cr_s, ci_s, reverse=True)
        everything = slice(0, gn)
        nxt_r = _get_lanes(gr_s, slice(1, tc + 1), everything)
        nxt_i = _get_lanes(gi_s, slice(1, tc + 1), everything)
        srv = sr_ref[...].astype(F32)
        siv = si_ref[...].astype(F32)
        vgn_ref[0:1, :] += _rowsum(nxt_r * srv + nxt_i * siv)
        vgn_ref[1:2, :] += _rowsum(nxt_i * srv - nxt_r * siv)
        lam_r = _get_lanes(gr_s, slice(0, tc), everything)
        lam_i = _get_lanes(gi_s, slice(0, tc), everything)
        gr_s[:, tc:tc + halo, :] = gr_s[:, 0:halo, :]
        gi_s[:, tc:tc + halo, :] = gi_s[:, 0:halo, :]
        lrb = lam_r.astype(BF16)
        lib = lam_i.astype(BF16)
        lr_ref[...] = lrb.astype(lr_ref.dtype)
        li_ref[...] = lib.astype(li_ref.dtype)
        for m in range(ns):
            states, chans = slice(m * sw, (m + 1) * sw), slice(m * S5_SLAB, (m + 1) * S5_SLAB)
            du_ref[:, chans] = (_dot_nt(lrb[:, states], bbr_ref[m]) + _dot_nt(lib[:, states], bbi_ref[m])
                                + dy[:, chans] * d_ref[:, chans]).astype(du_ref.dtype)
        _store_on_last([(dwg_ref, dwg_out)])

    return _run(
        body, comm, name="s5_bwd", grid=(steps,),
        out_shape=[jax.ShapeDtypeStruct((t, sa), BF16), jax.ShapeDtypeStruct((t, gn), BF16),
                   jax.ShapeDtypeStruct((t, gn), BF16), jax.ShapeDtypeStruct((t, sa), BF16),
                   jax.ShapeDtypeStruct((sa, sa), BF16), jax.ShapeDtypeStruct((SUBLANES, sa), F32),
                   jax.ShapeDtypeStruct((SUBLANES, gn), F32)],
        in_specs=[_rows_rev(tc, sa, steps), _rows_rev(tc, sa, steps), _rows_rev(tc, gn, steps), _rows_rev(tc, gn, steps),
                  _rows_rev(tc, sa, steps), _whole((sa, sa)), _whole((1, sa)), _whole(cre_blk.shape),
                  _whole(cimn_blk.shape), _whole(bbr_blk.shape), _whole(bbi_blk.shape), _whole((1, gn)), _whole((1, gn)),
                  _whole((1, sa))],
        out_specs=[_rows_rev(tc, sa, steps), _rows_rev(tc, gn, steps), _rows_rev(tc, gn, steps), _rows_rev(tc, sa, steps),
                   _acc((sa, sa)), _acc((SUBLANES, sa)), _acc((SUBLANES, gn))],
        scratch_shapes=[pltpu.VMEM((gn // LANES, tc + halo, LANES), F32)] * 2
        + [pltpu.VMEM((gn // LANES, SUBLANES, LANES), F32)] * 2 + [pltpu.VMEM((sa, sa), F32)]
        + [pltpu.VMEM((gn // LANES, SUBLANES, LANES), F32)] * 2
        + [pltpu.VMEM((gn // LANES, tc // SUBLANES, LANES), F32)] * 2,
        semantics="arbitrary",
    )(dya, y, sr, si, u, w_glu, b_glu, cre_blk, cimn_blk, bbr_blk, bbi_blk, ar, ai, d_skip)


def _inproj_bwd(dparts, x, dx1, g_mix, w_in_t, comm=None):
    t, d = x.shape
    n = w_in_t.shape[0]
    widths = [p.shape[1] for p in dparts]
    offs = [sum(widths[:i]) for i in range(len(widths) + 1)]
    tm = min(TOKEN_TILE, t)
    np_ = len(dparts)

    def body(*refs):
        dz_refs = refs[:np_]
        x_ref, dx1_ref, g_ref, w_ref, gx_ref, h_ref, vd_ref, vn_ref = refs[np_:]
        _zero_on_first(vd_ref, vn_ref)
        dh = jnp.zeros((tm, d), F32)
        for k, r in enumerate(dz_refs):
            lo, hi = offs[k], offs[k + 1]
            dzk = r[...]
            dh = dh + _dot(dzk, w_ref[lo:hi, :])
            vn_ref[0:1, lo:hi] += _rowsum(dzk.astype(F32))
        xhat, r0 = _rms_stats(x_ref[...])
        h_ref[...] = (xhat * g_ref[...]).astype(h_ref.dtype)
        dxn, dg = _rms_bwd(dh, xhat, r0, g_ref[...])
        gx_ref[...] = dx1_ref[...] + dxn
        vd_ref[0:1, :] += _rowsum(dg)

    return _run(
        body, comm, name="inproj_bwd", grid=(t // tm,),
        out_shape=[jax.ShapeDtypeStruct((t, d), F32), jax.ShapeDtypeStruct((t, d), BF16),
                   jax.ShapeDtypeStruct((SUBLANES, d), F32), jax.ShapeDtypeStruct((SUBLANES, n), F32)],
        in_specs=[_rows(tm, w) for w in widths] + [_rows(tm, d), _rows(tm, d), _whole((1, d)), _whole((n, d))],
        out_specs=[_rows(tm, d), _rows(tm, d), _acc((SUBLANES, d)), _acc((SUBLANES, n))],
        semantics="arbitrary",
    )(*dparts, x, dx1, g_mix, w_in_t)


def _dw_from_parts(dparts, h, tn, name, comm=None):
    t, d = h.shape
    widths = [p.shape[1] for p in dparts]
    offs = [sum(widths[:i]) for i in range(len(widths) + 1)]
    np_ = len(dparts)

    def body(*refs):
        h_ref, o_ref = refs[np_], refs[np_ + 1]
        hv = h_ref[...]
        for k, r in enumerate(refs[:np_]):
            o_ref[offs[k]:offs[k + 1], :] = _dot_tn(r[...], hv).astype(o_ref.dtype)

    return _run(
        body, comm, name=name, grid=(d // tn,),
        out_shape=jax.ShapeDtypeStruct((offs[-1], d), BF16),
        in_specs=[_whole(p.shape) for p in dparts] + [pl.BlockSpec((t, tn), lambda j: (0, j))],
        out_specs=pl.BlockSpec((offs[-1], tn), lambda j: (0, j)),
        semantics="parallel",
    )(*dparts, h)


def _prep(lr_row, li_row, ldt_row, lr_col, li_col, ldt_col, b_re, b_im, c_re, c_im, w_r, w_i, comm=None):
    gn, pch = b_re.shape
    sa, n = c_re.shape
    w, hd = w_r.shape
    ns, sw, lsl = sa // S5_SLAB, S5_SLAB * n // pch, w // LRU_SLAB

    def spread_cols(vals, row_group, col_group, width):
        r, k = vals.shape
        tile = (lax.broadcasted_iota(jnp.int32, (k, width), 0) == lax.broadcasted_iota(jnp.int32, (k, width), 1) % k)
        rows = lax.broadcasted_iota(jnp.int32, (r, width), 0) // row_group
        cols = lax.broadcasted_iota(jnp.int32, (r, width), 1) // col_group
        return jnp.where(rows == cols, _dot(vals, tile.astype(BF16)), 0.0)

    def spread_rows(vals, row_group, col_group, height):
        k, c = vals.shape
        tile = (lax.broadcasted_iota(jnp.int32, (height, k), 0) % k == lax.broadcasted_iota(jnp.int32, (height, k), 1))
        rows = lax.broadcasted_iota(jnp.int32, (height, c), 0) // row_group
        cols = lax.broadcasted_iota(jnp.int32, (height, c), 1) // col_group
        return jnp.where(rows == cols, _dot(tile.astype(BF16), vals), 0.0)

    def body(lrr, lir, ldr, lrc, lic, ldc, bre, bim, cre, cim, wr, wi,
             ar_o, ai_o, bbr_o, bbi_o, cre_o, cim_o, wr_o, wi_o):
        ar, ai, _, _ = _disc_scalars(lrr[...], lir[...], ldr[...])
        ar_o[...] = ar
        ai_o[...] = ai
        _, _, bbr, bbi = _disc_cols(lrc[...], lic[...], ldc[...], bre[...], bim[...])
        bbr_t, bbi_t = bbr.T, bbi.T
        for m in range(ns):
            bbr_o[m] = spread_rows(bbr_t[:, m * sw:(m + 1) * sw], pch, n, S5_SLAB).astype(bbr_o.dtype)
            bbi_o[m] = spread_rows(bbi_t[:, m * sw:(m + 1) * sw], pch, n, S5_SLAB).astype(bbi_o.dtype)
            rows = slice(m * S5_SLAB, (m + 1) * S5_SLAB)
            cre_o[m] = spread_rows(cre[rows, :].T, n, pch, sw).astype(cre_o.dtype)
            cim_o[m] = spread_rows(-cim[rows, :].T, n, pch, sw).astype(cim_o.dtype)
        for j in range(lsl):
            rows = slice(j * LRU_SLAB, (j + 1) * LRU_SLAB)
            wr_o[j] = spread_cols(wr[rows, :], hd, hd, LRU_SLAB).astype(wr_o.dtype)
            wi_o[j] = spread_cols(wi[rows, :], hd, hd, LRU_SLAB).astype(wi_o.dtype)

    args = (lr_row, li_row, ldt_row, lr_col, li_col, ldt_col, b_re, b_im, c_re, c_im, w_r, w_i)
    out_shape = [jax.ShapeDtypeStruct((1, gn), F32), jax.ShapeDtypeStruct((1, gn), F32),
                 jax.ShapeDtypeStruct((ns, S5_SLAB, sw), BF16), jax.ShapeDtypeStruct((ns, S5_SLAB, sw), BF16),
                 jax.ShapeDtypeStruct((ns, sw, S5_SLAB), BF16), jax.ShapeDtypeStruct((ns, sw, S5_SLAB), BF16),
                 jax.ShapeDtypeStruct((lsl, LRU_SLAB, LRU_SLAB), BF16),
                 jax.ShapeDtypeStruct((lsl, LRU_SLAB, LRU_SLAB), BF16)]
    return _run(
        body, comm, name="prep", grid=(1,), out_shape=out_shape, in_specs=[_whole(a.shape) for a in args],
        out_specs=[_acc(s.shape) for s in out_shape], semantics="arbitrary",
    )(*args)


def _s5_param_grads(lam_r, lam_i, sr, si, u, dy, pch, n, comm=None):
    t, gn = lam_r.shape
    sa = u.shape[1]
    sw = S5_SLAB * n // pch

    def body(lr_ref, li_ref, sr_ref, si_ref, u_ref, dy_ref, dbr_ref, dbi_ref, dcr_ref, dci_ref):
        uv = u_ref[...]
        dyv = dy_ref[...]
        dbr_ref[...] = _fold_diag_blocks(_dot_tn(uv, lr_ref[...]), pch, n).astype(dbr_ref.dtype)
        dbi_ref[...] = _fold_diag_blocks(_dot_tn(uv, li_ref[...]), pch, n).astype(dbi_ref.dtype)
        dcr_ref[...] = _fold_diag_blocks(_dot_tn(sr_ref[...], dyv), n, pch).astype(dcr_ref.dtype)
        dci_ref[...] = _fold_diag_blocks(_dot_tn(si_ref[...], dyv), n, pch).astype(dci_ref.dtype)

    states = pl.BlockSpec((t, sw), lambda m: (0, m))
    chans = pl.BlockSpec((t, S5_SLAB), lambda m: (0, m))
    return _run(
        body, comm, name="s5_param_grads", grid=(sa // S5_SLAB,),
        out_shape=[jax.ShapeDtypeStruct((pch, gn), BF16), jax.ShapeDtypeStruct((pch, gn), BF16),
                   jax.ShapeDtypeStruct((n, sa), BF16), jax.ShapeDtypeStruct((n, sa), BF16)],
        in_specs=[states, states, states, states, chans, chans],
        out_specs=[pl.BlockSpec((pch, sw), lambda m: (0, m)), pl.BlockSpec((pch, sw), lambda m: (0, m)),
                   pl.BlockSpec((n, S5_SLAB), lambda m: (0, m)), pl.BlockSpec((n, S5_SLAB), lambda m: (0, m))],
        semantics="parallel",
    )(lam_r, lam_i, sr, si, u, dy)


SMALL_PARTS = ["vec_tail", "vec_ffn", "vec_lru", "vec_mix", "vec_bin", "vec_sa", "vec_gn", "dw_r", "dw_i", "dbb_re",
               "dbb_im", "dc_re", "dc_imn", "loss"]


def _small_reduce(parts, shapes, lr_col, li_col, ldt_col, b_re, b_im, groups, after=()):
    gn, pch = b_re.shape
    n = gn // groups
    nparts = parts[SMALL_PARTS[0]].size // math.prod(shapes[SMALL_PARTS[0]])
    np_, nout, first_out = len(SMALL_PARTS), 24, len(SMALL_PARTS) + 5 + len(after)

    def body(*refs):
        ins = refs[:np_]
        lr, li, ldt, bre, bim = refs[np_:np_ + 5]
        outs = refs[first_out:first_out + nout]
        sums = dict(zip(SMALL_PARTS, refs[first_out + nout:]))

        @pl.when(pl.program_id(0) == 0)
        def _():
            for k, r in zip(SMALL_PARTS, ins):
                sums[k][...] = r[...].astype(F32)

        @pl.when(pl.program_id(0) > 0)
        def _():
            for k, r in zip(SMALL_PARTS, ins):
                sums[k][...] += r[...].astype(F32)

        @pl.when(pl.program_id(0) == nparts - 1)
        def _():
            finish({k: s[...] for k, s in sums.items()}, lr, li, ldt, bre, bim, *outs)

    def finish(tot, lr, li, ldt, bre, bim, o_loss, o_gmix, o_bin, o_bglu, o_s5d, o_convb, o_br, o_bi, o_lam, o_gffn,
               o_gpg, o_bpg, o_gple, o_gfin, o_wr, o_wi, o_cre, o_cim, o_bre, o_bim, o_lre, o_lim, o_ldt, o_convw):
        o_loss[...] = tot["loss"]
        o_convw[...] = tot["vec_lru"][SUBLANES - CONV_WIDTH:SUBLANES]
        o_bpg[...] = tot["vec_tail"][0:1]
        o_gpg[...] = tot["vec_tail"][1:2]
        o_gple[...] = tot["vec_tail"][2:3]
        o_gfin[...] = tot["vec_tail"][3:4]
        o_gffn[...] = tot["vec_ffn"][0:1]
        o_convb[...] = tot["vec_lru"][0:1]
        o_br[...] = tot["vec_lru"][1:2]
        o_bi[...] = tot["vec_lru"][2:3]
        o_lam[...] = tot["vec_lru"][3:4]
        o_gmix[...] = tot["vec_mix"][0:1]
        o_bin[...] = tot["vec_bin"][0:1]
        o_bglu[...] = tot["vec_sa"][0:1]
        o_s5d[...] = tot["vec_sa"][1:2]
        o_wr[...] = tot["dw_r"].T
        o_wi[...] = tot["dw_i"].T
        o_cre[...] = tot["dc_re"].T
        o_cim[...] = -tot["dc_imn"].T
        d_a = tot["vec_gn"].T
        _, chain = jax.vjp(_disc_cols, lr[...], li[...], ldt[...], bre[...], bim[...])
        d_lr, d_li, d_ldt, d_bre, d_bim = chain((d_a[:, 0:1], d_a[:, 1:2], tot["dbb_re"].T, tot["dbb_im"].T))
        o_lre[...] = d_lr
        o_lim[...] = d_li
        o_bre[...] = d_bre
        o_bim[...] = d_bim
        same = (lax.broadcasted_iota(jnp.int32, (groups, gn), 0)
                == lax.broadcasted_iota(jnp.int32, (groups, gn), 1) // n).astype(F32)
        o_ldt[...] = jnp.dot(same, d_ldt * jnp.ones((1, LANES), F32), preferred_element_type=F32,
                             precision=lax.Precision.HIGHEST)[:, 0:1]

    d = shapes["vec_mix"][1]
    nz = shapes["vec_bin"][1]
    sa = shapes["vec_sa"][1]
    w = shapes["vec_lru"][1]
    row = lambda c: jax.ShapeDtypeStruct((1, c), F32)
    out_shape = [jax.ShapeDtypeStruct(shapes["loss"], F32), row(d), row(nz), row(sa), row(sa), row(w), row(w), row(w),
                 row(w), row(d), row(d), row(d), row(d), row(d),
                 jax.ShapeDtypeStruct(shapes["dw_r"][::-1], F32), jax.ShapeDtypeStruct(shapes["dw_i"][::-1], F32),
                 jax.ShapeDtypeStruct(shapes["dc_re"][::-1], F32), jax.ShapeDtypeStruct(shapes["dc_imn"][::-1], F32),
                 jax.ShapeDtypeStruct((gn, pch), F32), jax.ShapeDtypeStruct((gn, pch), F32),
                 jax.ShapeDtypeStruct((gn, 1), F32), jax.ShapeDtypeStruct((gn, 1), F32),
                 jax.ShapeDtypeStruct((groups, 1), F32), jax.ShapeDtypeStruct((CONV_WIDTH, w), F32)]
    def part_spec(k):
        r, c = shapes[k]
        if parts[k].ndim == 3:
            return pl.BlockSpec((None, r, c), lambda i: (i, 0, 0))
        return pl.BlockSpec((r, c), lambda i: (i, 0))

    outs = _run(
        body, None, name="small_reduce", grid=(nparts,), out_shape=out_shape,
        in_specs=([part_spec(k) for k in SMALL_PARTS] + [_whole(a.shape) for a in (lr_col, li_col, ldt_col, b_re, b_im)]
                  + [_ANY] * len(after)),
        out_specs=[_acc(s.shape) for s in out_shape],
        scratch_shapes=[pltpu.VMEM(shapes[k], F32) for k in SMALL_PARTS],
        semantics="arbitrary",
    )(*[parts[k] for k in SMALL_PARTS], lr_col, li_col, ldt_col, b_re, b_im, *after)
    names = ["loss", "g_mix", "b_in", "b_glu", "s5_d", "conv_b", "b_r", "b_i", "lru_lambda", "g_ffn", "g_ple_gate",
             "b_ple_gate", "g_ple", "g_final", "w_r", "w_i", "s5_c_re", "s5_c_im", "s5_b_re", "s5_b_im", "lam_re",
             "lam_im", "log_dt", "conv_w"]
    return dict(zip(names, outs))


def _adamw_small(ws, gs, ms, vs):
    n = len(ws)

    def body(*refs):
        w_r, g_r, m_r, v_r = (refs[i * n:(i + 1) * n] for i in range(4))
        g_o, d_o, m_o, v_o = (refs[(4 + i) * n:(5 + i) * n] for i in range(4))
        for i in range(n):
            g = g_r[i][...]
            delta, m_new, v_new = _adamw_math(w_r[i][...], g, m_r[i][...], v_r[i][...])
            g_o[i][...] = g
            d_o[i][...] = delta
            m_o[i][...] = m_new
            v_o[i][...] = v_new

    shapes = [jax.ShapeDtypeStruct(a.shape, F32) for a in ws]
    outs = pl.pallas_call(body, name="adamw_small", out_shape=shapes * 4)(*ws, *gs, *ms, *vs)
    return outs[:n], outs[n:2 * n], outs[2 * n:3 * n], outs[3 * n:]


def _adamw_math(w, g, m, v):
    m_new = ADAM_B1 * m + (1.0 - ADAM_B1) * g
    v_new = ADAM_B2 * v + (1.0 - ADAM_B2) * (g * g)
    m_hat = m_new / (1.0 - ADAM_B1 ** ADAM_STEP)
    v_hat = v_new / (1.0 - ADAM_B2 ** ADAM_STEP)
    delta = -ADAM_LR * (m_hat / (jnp.sqrt(v_hat) + ADAM_EPS) + ADAM_WD * w)
    return delta, m_new, v_new


def _row_tile(rows):
    for cand in range(256, 0, -16):
        if rows % cand == 0:
            return cand
    return rows


def _adamw(parts, w, m, v, name, transposed=False, own=None, after=()):
    nw = len(w)
    rows, cols = w[0].shape
    npart = parts[0].shape[0]
    tr = _row_tile(rows)
    if transposed:
        parts_spec = pl.BlockSpec((npart, cols, tr), lambda i: (0, 0, i))
    else:
        parts_spec = pl.BlockSpec((npart, tr, cols), lambda i: (0, i, 0))
    per = 4 if own is None else 5
    first_out = per * nw + len(after)

    def body(*refs):
        mine = None if own is None else _chip(_mesh_position())
        for i in range(nw):
            group = refs[per * i:per * i + per]
            p_ref, (w_ref, m_ref, v_ref) = group[0], group[-3:]
            g_ref, d_ref, mo_ref, vo_ref = refs[first_out + 4 * i:first_out + 4 * i + 4]

            def part(k):
                a = p_ref[k].astype(F32)
                return a if own is None else jnp.where(mine == k, group[1][k].astype(F32), a)

            g = part(0)
            for k in range(1, npart):
                g = g + part(k)
            if transposed:
                g = g.T
            delta, m_new, v_new = _adamw_math(w_ref[...], g, m_ref[...], v_ref[...])
            g_ref[...] = g
            d_ref[...] = delta
            mo_ref[...] = m_new
            vo_ref[...] = v_new

    groups = zip(parts, w, m, v) if own is None else zip(parts, own, w, m, v)
    res = pl.pallas_call(
        body, name=name, grid=(rows // tr,),
        out_shape=[jax.ShapeDtypeStruct((rows, cols), F32)] * (4 * nw),
        in_specs=([parts_spec] * (per - 3) + [_rows(tr, cols)] * 3) * nw + [_ANY] * len(after),
        out_specs=[_rows(tr, cols)] * (4 * nw),
        compiler_params=_params("parallel"),
    )(*[a for group in groups for a in group], *after)
    return [res[4 * i:4 * i + 4] for i in range(nw)]


def _mesh_position():
    return lax.axis_index("x"), lax.axis_index("y"), lax.axis_index("c")


def _flip(pos, rel):
    x, y, c = pos
    return (1 - x if rel & 4 else x, 1 - y if rel & 2 else y, 1 - c if rel & 1 else c)


def _index(pos):
    return 4 * pos[0] + 2 * pos[1] + pos[2]


_ANY = pl.BlockSpec(memory_space=pl.ANY)
FLAT_ROWS = 32


def _dma_sems(n):
    return [pltpu.SemaphoreType.DMA((n, N_DEV - 1)), pltpu.SemaphoreType.DMA((n, N_DEV - 1)), pltpu.SemaphoreType.DMA((n,))]


def _block_of(ref, idx, rows, flat):
    if flat:
        return ref.at[pl.ds(pl.multiple_of(idx * rows, FLAT_ROWS), rows), :]
    return ref.at[idx]


class _Gather:
    chips = (4, 2, 6)
    rels = frozenset((1, 4, 2, 6))

    def __init__(self, shards):
        self.inputs = list(shards)
        self.flat = [s.shape[0] % FLAT_ROWS == 0 for s in shards]
        self.out_shape = [
            jax.ShapeDtypeStruct((N_DEV * s.shape[0], s.shape[1]) if f else (N_DEV,) + s.shape, s.dtype)
            for s, f in zip(shards, self.flat)]
        self.sems = _dma_sems(len(shards))

    def _copy(self, ins, outs, sems, i, k, block, to, own=False, lands_index=None):
        a = i if lands_index is None else lands_index
        dst = _block_of(outs[a], _index(block), self.inputs[a].shape[0], self.flat[a])
        return pltpu.make_async_remote_copy(
            src_ref=ins[a] if own else dst, dst_ref=dst, send_sem=sems[0].at[i, k], recv_sem=sems[1].at[i, k],
            device_id=to, device_id_type=MESH)

    def _local(self, ins, outs, sems, i, me):
        dst = _block_of(outs[i], _index(me), self.inputs[i].shape[0], self.flat[i])
        return pltpu.make_async_copy(ins[i], dst, sems[2].at[i])

    def _first(self, ins, outs, sems, i, me):
        cps = [self._copy(ins, outs, sems, i, 0, me, _flip(me, 1), own=True)]
        cps += [self._copy(ins, outs, sems, i, 1 + j, me, _flip(me, rel), own=True) for j, rel in enumerate(self.chips)]
        return cps

    def _passed(self, ins, outs, sems, i, j, me):
        return self._copy(ins, outs, sems, i, 4 + j, _flip(me, self.chips[j]), _flip(me, 1))

    def before(self, ins, outs, sems):
        n = len(self.inputs)
        me = _mesh_position()

        @pl.when(pl.program_id(0) == 0)
        def _():
            for i in range(n):
                self._local(ins, outs, sems, i, me).start()
                for cp in self._first(ins, outs, sems, i, me):
                    cp.start()

        @pl.when(pl.program_id(0) == pl.num_programs(0) - 1)
        def _():
            for j, rel in enumerate(self.chips):
                for i in range(n):
                    self._copy(ins, outs, sems, i, 1 + j, _flip(me, rel), me).wait_recv()
                    self._passed(ins, outs, sems, i, j, me).start()

    def after(self, ins, outs, sems):
        n = len(self.inputs)
        me = _mesh_position()
        sibling = _flip(me, 1)

        @pl.when(pl.program_id(0) == pl.num_programs(0) - 1)
        def _():
            for i in range(n):
                self._copy(ins, outs, sems, i, 0, sibling, me).wait_recv()
                for j, rel in enumerate(self.chips):
                    self._copy(ins, outs, sems, i, 4 + j, _flip(sibling, rel), me).wait_recv()
            for i in range(n):
                for cp in self._first(ins, outs, sems, i, me):
                    cp.wait_send()
                for j in range(len(self.chips)):
                    self._passed(ins, outs, sems, i, j, me).wait_send()
                self._local(ins, outs, sems, i, me).wait()


N_CHIPS = N_DEV // 2


def _chip(pos):
    return 2 * pos[0] + pos[1]


class _PairSwap:
    rels = frozenset((1,))

    def __init__(self, arrays):
        self.inputs = list(arrays)
        self.rows = [a.shape[0] // N_DEV for a in arrays]
        for r in self.rows:
            assert r % FLAT_ROWS == 0, r
        self.out_shape = [jax.ShapeDtypeStruct((N_CHIPS, r, a.shape[1]), a.dtype) for a, r in zip(arrays, self.rows)]
        n = len(arrays)
        self.sems = [pltpu.SemaphoreType.DMA((n, N_CHIPS)), pltpu.SemaphoreType.DMA((n, N_CHIPS))]

    def _copy(self, ins, outs, sems, i, j, me):
        sibling = _flip(me, 1)
        return pltpu.make_async_remote_copy(
            src_ref=_block_of(ins[i], 2 * j + sibling[2], self.rows[i], True), dst_ref=outs[i].at[j],
            send_sem=sems[0].at[i, j], recv_sem=sems[1].at[i, j], device_id=sibling, device_id_type=MESH)

    def before(self, ins, outs, sems):
        me = _mesh_position()

        @pl.when(pl.program_id(0) == 0)
        def _():
            for i in range(len(self.inputs)):
                for j in range(N_CHIPS):
                    self._copy(ins, outs, sems, i, j, me).start()

    def after(self, ins, outs, sems):
        me = _mesh_position()

        @pl.when(pl.program_id(0) == pl.num_programs(0) - 1)
        def _():
            for i in range(len(self.inputs)):
                for j in range(N_CHIPS):
                    self._copy(ins, outs, sems, i, j, me).wait()


_HBM = pl.BlockSpec(memory_space=pltpu.HBM)
_SEM = pl.BlockSpec(memory_space=pltpu.SEMAPHORE)
_DATAFLOW = pltpu.SideEffectType.DATAFLOW_SIDE_EFFECTING
_CHIP_RELS = (4, 2, 6)
_SPLIT_COLLECTIVE_ID = 3


def _chips_copy(src_ref, dst_ref, send_sems, recv_sems, i, k, me):
    peer = _flip(me, _CHIP_RELS[k])
    at = i * len(_CHIP_RELS) + k
    return pltpu.make_async_remote_copy(
        src_ref=src_ref.at[_chip(peer)], dst_ref=dst_ref.at[_chip(me)], send_sem=send_sems.at[at],
        recv_sem=recv_sems.at[at], device_id=peer, device_id_type=MESH)


def _chips_start(pairs, name, collective_id=_SPLIT_COLLECTIVE_ID):
    n = len(pairs)

    def body(*refs):
        p_refs, land_refs, send_sems, recv_sems = refs[:n], refs[n:2 * n], refs[2 * n], refs[2 * n + 1]
        token = refs[-1]
        me = _mesh_position()
        _handshake(_CHIP_RELS)
        for i in range(n):
            for k in range(len(_CHIP_RELS)):
                _chips_copy(p_refs[i], land_refs[i], send_sems, recv_sems, i, k, me).start()
        token[...] = jnp.zeros_like(token)

    nsem = n * len(_CHIP_RELS)
    hbm = [pltpu.HBM(p.shape, p.dtype) for p in pairs]
    res = pl.pallas_call(
        body, name="chips_start_" + name,
        out_shape=(pltpu.SemaphoreType.DMA((nsem,)), pltpu.SemaphoreType.DMA((nsem,)), *hbm, *hbm,
                   jax.ShapeDtypeStruct((SUBLANES, LANES), F32)),
        in_specs=(_HBM,) * (2 * n),
        out_specs=(_SEM, _SEM) + (_HBM,) * (2 * n) + (pl.BlockSpec(memory_space=pltpu.VMEM),),
        input_output_aliases={i: 2 + i for i in range(2 * n)},
        compiler_params=pltpu.CompilerParams(has_side_effects=_DATAFLOW, collective_id=collective_id),
    )(*[pltpu.with_memory_space_constraint(p, pltpu.HBM) for p in pairs],
      *[pltpu.with_memory_space_constraint(lax.empty(p.shape, p.dtype), pltpu.HBM) for p in pairs])
    return res[0], res[1], list(res[2:2 + n]), list(res[2 + n:2 + 2 * n]), res[-1]


def _chips_wait(send_sems, recv_sems, p_thru, land_thru, after, name):
    n = len(p_thru)

    def body(*refs):
        p_refs, land_refs, send_sems, recv_sems = refs[:n], refs[n:2 * n], refs[2 * n], refs[2 * n + 1]
        me = _mesh_position()
        for i in range(n):
            for k in range(len(_CHIP_RELS)):
                copy = _chips_copy(p_refs[i], land_refs[i], send_sems, recv_sems, i, k, me)
                copy.wait_send()
                copy.wait_recv()

    hbm = [pltpu.HBM(p.shape, p.dtype) for p in p_thru]
    res = pl.pallas_call(
        body, name="chips_wait_" + name, out_shape=(*hbm, *hbm),
        in_specs=(_HBM,) * (2 * n) + (_SEM, _SEM) + (_ANY,) * len(after), out_specs=(_HBM,) * (2 * n),
        input_output_aliases={i: i for i in range(2 * n)}, compiler_params=pltpu.CompilerParams(has_side_effects=_DATAFLOW),
    )(*p_thru, *land_thru, send_sems, recv_sems, *after)
    return list(res[:n]), list(res[n:])


_GATHER_START_ID, _GATHER_PASS_ID = 4, 5


class _Rows:
    def __init__(self, ref, width):
        self.ref, self.width, self.at = ref, width, self

    def __getitem__(self, idx):
        i, k = idx
        return self.ref.at[i * self.width + k]


def _handshake(rels):
    me = _mesh_position()
    sem = pltpu.get_barrier_semaphore()
    for rel in rels:
        pl.semaphore_signal(sem, inc=1, device_id=_flip(me, rel), device_id_type=MESH)
    pl.semaphore_wait(sem, len(rels))


def _gather_start(shards, name, after=()):
    job, n, na = _Gather(shards), len(shards), len(after)

    def body(*refs):
        ins, lands = refs[:n], refs[n:2 * n]
        send, recv, local = refs[2 * n + na:2 * n + na + 3]
        sems = (_Rows(send, N_DEV - 1), _Rows(recv, N_DEV - 1), local)
        token = refs[-1]
        me = _mesh_position()
        _handshake(sorted(job.rels))
        for i in range(n):
            job._local(ins, lands, sems, i, me).start()
            for cp in job._first(ins, lands, sems, i, me):
                cp.start()
        token[...] = jnp.zeros_like(token)

    res = pl.pallas_call(
        body, name=name,
        out_shape=(pltpu.SemaphoreType.DMA((n * (N_DEV - 1),)), pltpu.SemaphoreType.DMA((n * (N_DEV - 1),)),
                   pltpu.SemaphoreType.DMA((n,))) + tuple(pltpu.HBM(s.shape, s.dtype) for s in job.out_shape)
        + (jax.ShapeDtypeStruct((SUBLANES, LANES), F32),),
        in_specs=(_HBM,) * (2 * n) + (_ANY,) * na,
        out_specs=(_SEM,) * 3 + (_HBM,) * n + (pl.BlockSpec(memory_space=pltpu.VMEM),),
        input_output_aliases={n + i: 3 + i for i in range(n)},
        compiler_params=pltpu.CompilerParams(has_side_effects=_DATAFLOW, collective_id=_GATHER_START_ID),
    )(*[pltpu.with_memory_space_constraint(s, pltpu.HBM) for s in shards],
      *[pltpu.with_memory_space_constraint(lax.empty(s.shape, s.dtype), pltpu.HBM) for s in job.out_shape], *after)
    return list(res[:3]), list(res[3:3 + n]), res[-1]


def _gather_pass(shards, which, sems, lands, name, after=()):
    job, n, m, na = _Gather(shards), len(shards), len(which), len(after)

    def body(*refs):
        lands_r, local = refs[:m], refs[m + 2]
        send, recv = _Rows(refs[m], N_DEV - 1), _Rows(refs[m + 1], N_DEV - 1)
        send2, recv2 = _Rows(refs[m + 3 + na], 3), _Rows(refs[m + 3 + na + 1], 3)
        me = _mesh_position()
        _handshake((1,))
        by_index = {i: lands_r[q] for q, i in enumerate(which)}
        for j, rel in enumerate(job.chips):
            for q, i in enumerate(which):
                job._copy(None, by_index, (send, recv, local), i, 1 + j, _flip(me, rel), me).wait_recv()
                job._copy(None, by_index, (send2, recv2), q, j, _flip(me, rel), _flip(me, 1), lands_index=i).start()

    res = pl.pallas_call(
        body, name=name,
        out_shape=(pltpu.SemaphoreType.DMA((m * 3,)), pltpu.SemaphoreType.DMA((m * 3,)))
        + tuple(pltpu.HBM(lands[i].shape, lands[i].dtype) for i in which),
        in_specs=(_HBM,) * m + (_SEM,) * 3 + (_ANY,) * na, out_specs=(_SEM, _SEM) + (_HBM,) * m,
        input_output_aliases={q: 2 + q for q in range(m)},
        compiler_params=pltpu.CompilerParams(has_side_effects=_DATAFLOW, collective_id=_GATHER_PASS_ID),
    )(*[lands[i] for i in which], *sems, *after)
    return list(res[:2]), list(res[2:])


def _gather_finish(shards, which, sems, sems2, lands, after, name):
    job, n, m = _Gather(shards), len(shards), len(which)

    def body(*refs):
        ins, lands_r = refs[:m], refs[m:2 * m]
        send, recv, local = _Rows(refs[2 * m], N_DEV - 1), _Rows(refs[2 * m + 1], N_DEV - 1), refs[2 * m + 2]
        send2, recv2 = _Rows(refs[2 * m + 3], 3), _Rows(refs[2 * m + 4], 3)
        me = _mesh_position()
        sibling = _flip(me, 1)
        by_index = {i: lands_r[q] for q, i in enumerate(which)}
        own = {i: ins[q] for q, i in enumerate(which)}
        for q, i in enumerate(which):
            job._copy(None, by_index, (send, recv, local), i, 0, sibling, me).wait_recv()
            for j, rel in enumerate(job.chips):
                job._copy(None, by_index, (send2, recv2), q, j, _flip(sibling, rel), me, lands_index=i).wait_recv()
                job._copy(None, by_index, (send2, recv2), q, j, _flip(me, rel), sibling, lands_index=i).wait_send()
            for cp in job._first(own, by_index, (send, recv, local), i, me):
                cp.wait_send()
            job._local(own, by_index, (send, recv, local), i, me).wait()

    res = pl.pallas_call(
        body, name=name, out_shape=tuple(pltpu.HBM(lands[i].shape, lands[i].dtype) for i in which),
        in_specs=(_HBM,) * (2 * m) + (_SEM,) * 5 + (_ANY,) * len(after), out_specs=(_HBM,) * m,
        input_output_aliases={m + q: q for q in range(m)},
        compiler_params=pltpu.CompilerParams(has_side_effects=_DATAFLOW),
    )(*[pltpu.with_memory_space_constraint(shards[i], pltpu.HBM) for i in which], *[lands[i] for i in which],
      *sems, *sems2, *after)
    return list(res)


def _pair_add(grads, halves, name):
    n = len(grads)

    def body(*refs):
        g_refs, h_refs, o_refs = refs[:n], refs[n:2 * n], refs[2 * n:]
        c = lax.axis_index("c")
        for i in range(n):
            r = h_refs[i].shape[1]
            for j in range(N_CHIPS):
                own = g_refs[i][pl.ds(pl.multiple_of((2 * j + c) * r, FLAT_ROWS), r), :]
                o_refs[i][j] = (own.astype(F32) + h_refs[i][j].astype(F32)).astype(o_refs[i].dtype)

    return pl.pallas_call(
        body, name=name, out_shape=[jax.ShapeDtypeStruct(h.shape, h.dtype) for h in halves],
        compiler_params=pltpu.CompilerParams(vmem_limit_bytes=VMEM_LIMIT),
    )(*grads, *halves)


class _After:
    rels = frozenset()
    out_shape = []
    sems = []

    def __init__(self, arrays):
        self.inputs = list(arrays)

    def before(self, ins, outs, sems):
        pass

    def after(self, ins, outs, sems):
        pass


class _Both:
    def __init__(self, first, second):
        self.jobs = (first, second)
        self.rels = first.rels | second.rels
        self.inputs = first.inputs + second.inputs
        self.out_shape = first.out_shape + second.out_shape
        self.sems = first.sems + second.sems

    def _each(self, ins, outs, sems):
        a = self.jobs[0]
        i, o, s = len(a.inputs), len(a.out_shape), len(a.sems)
        return ((a, ins[:i], outs[:o], sems[:s]), (self.jobs[1], ins[i:], outs[o:], sems[s:]))

    def before(self, ins, outs, sems):
        for job, i, o, s in self._each(ins, outs, sems):
            job.before(i, o, s)

    def after(self, ins, outs, sems):
        for job, i, o, s in self._each(ins, outs, sems):
            job.after(i, o, s)


_COLLECTIVE_IDS = {(1,): 0, (1, 2, 4, 6): 2}


def _entry_barrier(rels):
    @pl.when(pl.program_id(0) == 0)
    def _():
        me = _mesh_position()
        sem = pltpu.get_barrier_semaphore()
        for rel in rels:
            pl.semaphore_signal(sem, inc=1, device_id=_flip(me, rel), device_id_type=MESH)
        pl.semaphore_wait(sem, len(rels))


def _run(body, comm, *, semantics, out_shape, in_specs, out_specs, scratch_shapes=(), **kw):
    if comm is None:
        return pl.pallas_call(body, out_shape=out_shape, in_specs=in_specs, out_specs=out_specs,
                              scratch_shapes=list(scratch_shapes), compiler_params=_params(semantics), **kw)
    single = not isinstance(out_shape, (list, tuple))
    outs = [out_shape] if single else list(out_shape)
    ospecs = [out_specs] if single else list(out_specs)
    counts = [len(in_specs), len(comm.inputs), len(outs), len(comm.out_shape), len(scratch_shapes), len(comm.sems)]
    rels = tuple(sorted(comm.rels))

    def carrying(*refs):
        groups, pos = [], 0
        for c in counts:
            groups.append(refs[pos:pos + c])
            pos += c
        main_in, comm_in, main_out, comm_out, main_scratch, comm_sems = groups
        _entry_barrier(rels)
        comm.before(comm_in, comm_out, comm_sems)
        body(*main_in, *main_out, *main_scratch)
        comm.after(comm_in, comm_out, comm_sems)

    call = pl.pallas_call(
        carrying, out_shape=outs + list(comm.out_shape), in_specs=list(in_specs) + [_ANY] * len(comm.inputs),
        out_specs=ospecs + [_ANY] * len(comm.out_shape), scratch_shapes=list(scratch_shapes) + list(comm.sems),
        compiler_params=pltpu.CompilerParams(dimension_semantics=("arbitrary",), vmem_limit_bytes=VMEM_LIMIT,
                                             collective_id=_COLLECTIVE_IDS[rels]), **kw)

    def apply(*args):
        res = call(*args, *comm.inputs)
        main = res[:len(outs)]
        return (main[0] if single else list(main)), list(res[len(outs):])

    return apply


def _alone(comm, name):
    return _run(lambda: None, comm, semantics="arbitrary", name=name, grid=(1,), out_shape=[], in_specs=[], out_specs=[])()[1]


SHARDED = {"w_in": 1, "w_glu": 0, "conv_w": 1, "w_a_out": 1, "w_b_out": 0, "w_o": 0, "w_ffn_gate": 1, "w_ffn_up": 1,
           "w_ffn_down": 0, "w_ple_gate": 0, "w_ple": 1}
TRANSPOSED = ("w_in", "w_a_out", "w_ffn_gate", "w_ffn_up", "w_ple")
LONG_AXIS_MINOR = ("w_in", "w_ffn_gate", "w_ffn_up")
NARROW_LAST = ("s5_b_re", "s5_b_im", "s5_d")
ADAMW_GROUPS = (("w_in",), ("w_glu",), ("conv_w",), ("w_a_out",), ("w_b_out", "w_o", "w_ple_gate"),
                ("w_ffn_gate", "w_ffn_up"), ("w_ffn_down",), ("w_ple",))
SMALL = ["g_mix", "b_in", "lam_re", "lam_im", "log_dt", "s5_b_re", "s5_b_im", "s5_c_re", "s5_c_im", "s5_d", "b_glu",
         "conv_b", "w_r", "b_r", "w_i", "b_i", "lru_lambda", "g_ffn", "g_ple_gate", "b_ple_gate", "g_ple", "g_final"]
WEIGHTS = ["g_mix", "w_in", "b_in", "lam_re", "lam_im", "log_dt", "s5_b_re", "s5_b_im", "s5_c_re", "s5_c_im", "s5_d",
           "w_glu", "b_glu", "conv_w", "conv_b", "w_r", "b_r", "w_i", "b_i", "lru_lambda", "w_a_out", "w_b_out", "w_o",
           "g_ffn", "w_ffn_gate", "w_ffn_up", "w_ffn_down", "g_ple_gate", "w_ple_gate", "b_ple_gate", "w_ple", "g_ple",
           "g_final"]


def _join_columns(gathered):
    nb, r, c = gathered.shape
    return jnp.transpose(gathered, (1, 0, 2)).reshape(r, nb * c)


def _disc_scalars(lr, li, ldt):
    dt = jnp.exp(ldt)
    mag = jnp.exp(lr * dt)
    ar = mag * jnp.cos(li * dt)
    ai = mag * jnp.sin(li * dt)
    den = lr * lr + li * li
    nr = ar - 1.0
    fr = (nr * lr + ai * li) / den
    fi = (ai * lr - nr * li) / den
    return ar, ai, fr, fi


def _disc_cols(lr, li, ldt, b_re, b_im):
    ar, ai, fr, fi = _disc_scalars(lr, li, ldt)
    return ar, ai, fr * b_re - fi * b_im, fr * b_im + fi * b_re


def _local_step(x, p, target, src, small, disc, distributed=True):
    full = {} if distributed else dict(src)
    gw, halves, pairs, got = {}, {}, {}, {}

    def gather(keys):
        return (_Gather([src[k] for k in keys]), keys, full) if distributed else None

    def swap(keys):
        return (_PairSwap([gw[k] for k in keys]), keys, halves) if distributed else None

    flights = []

    def add_pairs(keys, collective_id):
        if not distributed:
            return None
        sums = _pair_add([gw[k] for k in keys], [halves[k] for k in keys], "pair_add_" + keys[0])
        send, recv, thru, lands, token = _chips_start(sums, keys[0], collective_id)
        flights.append((keys, send, recv, thru, lands))
        return (_After([token]), [], {})

    def carry(fn, *args, jobs=()):
        jobs = [j for j in jobs if j is not None]
        if not jobs:
            return fn(*args)
        comm = jobs[0][0]
        for j in jobs[1:]:
            comm = _Both(comm, j[0])
        res, extra = fn(*args, comm=comm)
        for job, keys, sink in jobs:
            sink.update(zip(keys, extra[:len(job.out_shape)]))
            extra = extra[len(job.out_shape):]
        return res

    d = x.shape[1]
    g, n, pch = small["s5_b_re"].shape
    sa, lw = g * pch, small["lru_lambda"].shape[-1]
    widths = [sa, lw, d, d]
    row = lambda v: v.reshape(1, -1)

    hd = small["w_r"].shape[-1]
    ar_row, ai_row, bbr_blk, bbi_blk, cre_blk, cimn_blk, wr_blk, wi_blk = carry(
        _prep, *disc["rows"], *disc["cols"], disc["b_re"], disc["b_im"], small["s5_c_re"].reshape(sa, n),
        small["s5_c_im"].reshape(sa, n), small["w_r"].reshape(lw, hd), small["w_i"].reshape(lw, hd),
        jobs=[gather(["w_in"])])
    d_row = row(small["s5_d"])
    g_mix_row = row(small["g_mix"])
    if distributed:
        later = ["w_glu", "conv_w", "w_a_out", "w_b_out", "w_ffn_gate", "w_ffn_up", "w_o", "w_ffn_down", "w_ple_gate",
                 "w_ple"]
        wires = [src[k] for k in later]
        flight, lands, token = _gather_start(wires, "gather_start", after=[ar_row])
        g_mix_row = g_mix_row + token[0, 0]

        def arrive(keys, after):
            which = [later.index(k) for k in keys]
            passed, moved = _gather_pass(wires, which, flight, lands, "gather_pass_" + keys[0], after)
            for q, i in enumerate(which):
                lands[i] = moved[q]
            full.update(zip(keys, _gather_finish(wires, which, flight, passed, lands, (), "gather_finish_" + keys[0])))
    else:
        arrive = lambda keys, after: None

    u_a, u_b, za, zb = _inproj_fwd(x, g_mix_row, full["w_in"], row(small["b_in"]), widths)
    arrive(["w_glu", "conv_w", "w_a_out", "w_b_out"], [u_a])
    conv_w = _join_columns(full["conv_w"]) if distributed else full["conv_w"]
    branches = [_s5_fwd(u_a, bbr_blk, bbi_blk, ar_row, ai_row, cre_blk, cimn_blk, d_row, full["w_glu"],
                        row(small["b_glu"])),
                _lru_fwd(u_b, conv_w, row(small["conv_b"]), wr_blk, row(small["b_r"]), wi_blk, row(small["b_i"]),
                         row(small["lru_lambda"]))]
    (sr, si, y, y_a), (xc, h, hprev) = _stages(branches, "branches_fwd")
    arrive(["w_ffn_gate", "w_ffn_up", "w_o"], [y_a])
    x1, merged, ma, mb, fg, fu = _merge_ffn_up_fwd(y_a, h, za, zb, x, full["w_a_out"], full["w_b_out"], full["w_o"],
                                                   row(small["g_ffn"]), full["w_ffn_gate"], full["w_ffn_up"])
    arrive(["w_ffn_down", "w_ple_gate", "w_ple"], [x1])
    x2 = _ffn_down_fwd(fg, fu, x1, full["w_ffn_down"])

    dx2, loss_blk, gw["w_ple_gate"], gw["w_ple"], vec_tail = _tail_fwd_bwd(
        x2, p, target, row(small["g_ple_gate"]), full["w_ple_gate"], row(small["b_ple_gate"]), full["w_ple"],
        row(small["g_ple"]), row(small["g_final"]))
    dfg, dfu, act = carry(_ffn_bwd_a, dx2, fg, fu, full["w_ffn_down"], jobs=[swap(["w_ple_gate", "w_ple"])])
    tn_d = min(d, 512)
    gw["w_ffn_down"] = _matmul_tn(act, dx2, tn_d, "dw_ffn_down", BF16)
    started = add_pairs(["w_ple_gate", "w_ple"], 6)
    dx1, h2, vec_ffn = carry(_ffn_bwd_b, dfg, dfu, x1, dx2, row(small["g_ffn"]), full["w_ffn_gate"], full["w_ffn_up"],
                             jobs=[swap(["w_ffn_down"]), started])
    gw["w_ffn_gate"] = _matmul_tn(dfg, h2, tn_d, "dw_ffn_gate", BF16)
    gw["w_ffn_up"] = _matmul_tn(dfu, h2, tn_d, "dw_ffn_up", BF16)
    started = add_pairs(["w_ffn_down"], 7)
    dza, dzb, dya, dyb, gw["w_o"], gw["w_a_out"], gw["w_b_out"] = carry(
        _merge_bwd, dx1, merged, ma, mb, za, zb, y_a, h, full["w_o"], full["w_a_out"], full["w_b_out"],
        jobs=[swap(["w_ffn_gate", "w_ffn_up"]), started])
    started = add_pairs(["w_ffn_gate", "w_ffn_up"], 8)
    own = {}
    du_b, dw_r, dw_i, vec_lru = carry(
        _lru_bwd, dyb, xc, hprev, u_b, conv_w, wr_blk, row(small["b_r"]), wi_blk, row(small["b_i"]),
        row(small["lru_lambda"]), hd, jobs=[swap(["w_o", "w_a_out", "w_b_out"]), started])
    started = add_pairs(["w_o", "w_a_out", "w_b_out"], 9)
    du_a, lam_r, lam_i, dy16, gw["w_glu"], vec_sa, vec_gn = _s5_bwd(
        dya, y, sr, si, u_a, full["w_glu"], row(small["b_glu"]), cre_blk, cimn_blk, bbr_blk, bbi_blk, ar_row, ai_row, d_row)
    smalls = {"vec_tail": vec_tail, "vec_ffn": vec_ffn, "vec_lru": vec_lru, "vec_sa": vec_sa, "vec_gn": vec_gn,
              "dw_r": dw_r, "dw_i": dw_i, "loss": loss_blk}
    everyones = {}

    def gather_smalls(keys):
        return (_Gather([smalls[k] for k in keys]), keys, everyones) if distributed else None

    smalls["dbb_re"], smalls["dbb_im"], smalls["dc_re"], smalls["dc_imn"] = carry(
        _s5_param_grads, lam_r, lam_i, sr, si, u_a, dy16, pch, n,
        jobs=[swap(["w_glu"]), gather_smalls(list(smalls)), started])
    started = add_pairs(["w_glu"], 10)
    dz = [du_a, du_b, dza, dzb]
    grad_x, h0, smalls["vec_mix"], smalls["vec_bin"] = _inproj_bwd(
        dz, x, dx1, row(small["g_mix"]), full["w_in"])
    shapes = {k: a.shape for k, a in smalls.items()}
    gw["w_in"] = carry(_dw_from_parts, dz, h0, tn_d, "dw_in",
                       jobs=[gather_smalls(["dbb_re", "dbb_im", "dc_re", "dc_imn", "vec_mix", "vec_bin"]), started])
    smalls.update(everyones)
    if distributed:
        for keys, send, recv, thru, lands in flights:
            stayed, landed = _chips_wait(send, recv, thru, lands, [gw["w_in"]], keys[0])
            own.update(zip(keys, stayed))
            got.update(zip(keys, landed))
        got["w_in"] = gw["w_in"]
        gw = got
    return grad_x, gw, smalls, shapes, own


def _disc_inputs(small):
    g, n, pch = small["s5_b_re"].shape
    srcs = (small["lam_re"], small["lam_im"], jnp.repeat(small["log_dt"], n))
    return {"rows": [a.reshape(1, g * n) for a in srcs], "cols": [a.reshape(g * n, 1) for a in srcs],
            "b_re": small["s5_b_re"].reshape(g * n, pch), "b_im": small["s5_b_im"].reshape(g * n, pch)}


def kernel(x, p, g_mix, w_in, b_in, lam_re, lam_im, log_dt, s5_b_re, s5_b_im, s5_c_re, s5_c_im, s5_d, w_glu, b_glu, conv_w, conv_b, w_r, b_r, w_i, b_i, lru_lambda, w_a_out, w_b_out, w_o, g_ffn, w_ffn_gate, w_ffn_up, w_ffn_down, g_ple_gate, w_ple_gate, b_ple_gate, w_ple, g_ple, g_final, loss_target, m_g_mix, m_w_in, m_b_in, m_lam_re, m_lam_im, m_log_dt, m_s5_b_re, m_s5_b_im, m_s5_c_re, m_s5_c_im, m_s5_d, m_w_glu, m_b_glu, m_conv_w, m_conv_b, m_w_r, m_b_r, m_w_i, m_b_i, m_lru_lambda, m_w_a_out, m_w_b_out, m_w_o, m_g_ffn, m_w_ffn_gate, m_w_ffn_up, m_w_ffn_down, m_g_ple_gate, m_w_ple_gate, m_b_ple_gate, m_w_ple, m_g_ple, m_g_final, v_g_mix, v_w_in, v_b_in, v_lam_re, v_lam_im, v_log_dt, v_s5_b_re, v_s5_b_im, v_s5_c_re, v_s5_c_im, v_s5_d, v_w_glu, v_b_glu, v_conv_w, v_conv_b, v_w_r, v_b_r, v_w_i, v_b_i, v_lru_lambda, v_w_a_out, v_w_b_out, v_w_o, v_g_ffn, v_w_ffn_gate, v_w_ffn_up, v_w_ffn_down, v_g_ple_gate, v_w_ple_gate, v_b_ple_gate, v_w_ple, v_g_ple, v_g_final):
    given = dict(locals())
    wts = {k: given[k] for k in WEIGHTS}
    moms = {k: given["m_" + k] for k in WEIGHTS}
    vels = {k: given["v_" + k] for k in WEIGHTS}

    def drop_depth(k, a):
        return a if k == "g_final" else a[0]

    small = {k: drop_depth(k, wts[k]) for k in SMALL}
    shard = {k: wts[k][0] for k in SHARDED}
    names = list(SHARDED)

    def wire(k):
        if k == "conv_w":
            return shard[k]
        return (shard[k].T if k in TRANSPOSED else shard[k]).astype(BF16)

    disc = _disc_inputs(small)
    grad_x, parts, smalls, shapes, own = _local_step(x[0], p[0, 0], loss_target[0], {k: wire(k) for k in names}, small,
                                                     disc)

    (pair_in,) = _pair_add([parts["w_in"]], _alone(_PairSwap([parts["w_in"]]), "swap_last"), "pair_add_w_in")
    send_sems, recv_sems, pair_in, landing, token = _chips_start([pair_in], "w_in")
    ordered = (token,)

    g_small = _small_reduce(smalls, shapes, *disc["cols"], disc["b_re"], disc["b_im"], small["s5_b_re"].shape[0],
                            after=ordered)
    loss = g_small.pop("loss")[0, 0]
    cols = shard["conv_w"].shape[1]
    mine = _index((lax.axis_index("x"), lax.axis_index("y"), lax.axis_index("c")))
    parts["conv_w"] = lax.dynamic_slice_in_dim(g_small.pop("conv_w"), mine * cols, cols, axis=1)[None]

    def view(k, a):
        a = a.reshape((1, -1) if k == "g_final" else wts[k].shape)
        return jnp.swapaxes(a, -1, -2) if k in NARROW_LAST else a

    def unview(k, a):
        return (jnp.swapaxes(a, -1, -2) if k in NARROW_LAST else a).reshape(wts[k].shape)

    slots = _adamw_small([view(k, wts[k]) for k in SMALL], [view(k, g_small[k]) for k in SMALL],
                         [view(k, moms[k]) for k in SMALL], [view(k, vels[k]) for k in SMALL])
    small_out = [dict(zip(SMALL, [unview(k, a) for k, a in zip(SMALL, slot)])) for slot in slots]

    big_out = {}
    between = [slots[0][0]]
    for group in sorted(ADAMW_GROUPS, key=lambda g: g[0] == "w_in"):
        flip = group[0] in LONG_AXIS_MINOR
        look = (lambda a: a.T) if flip else (lambda a: a)
        last = group[0] == "w_in"
        if last:
            (own["w_in"],), (parts["w_in"],) = _chips_wait(send_sems, recv_sems, pair_in, landing, between, "w_in")
        res = _adamw([parts[k] for k in group], [look(shard[k]) for k in group], [look(moms[k][0]) for k in group],
                     [look(vels[k][0]) for k in group], "adamw_" + group[0],
                     transposed=group[0] in TRANSPOSED and not flip,
                     own=[own[k] for k in group] if group[0] in own else None, after=() if last else ordered)
        between.append(res[0][0])
        for k, outs4 in zip(group, res):
            big_out[k] = [look(a) for a in outs4]

    outs = [loss, grad_x[None]]
    for slot in range(4):
        for k in WEIGHTS:
            if k in SHARDED:
                outs.append(big_out[k][slot][None])
            else:
                outs.append(small_out[slot][k])
    return tuple(outs)
```

```python
import math

import jax
import jax.numpy as jnp
from jax import lax
from jax.experimental import pallas as pl
from jax.experimental.pallas import tpu as pltpu

F32 = jnp.float32
BF16 = jnp.bfloat16

EPS = 1e-6
LRU_C = 8.0
CONV_WIDTH = 4
ADAM_LR = 0.001
ADAM_B1 = 0.9
ADAM_B2 = 0.999
ADAM_EPS = 1e-08
ADAM_WD = 0.01
ADAM_STEP = 10

N_DEV = 8
MESH = pl.DeviceIdType.MESH
SUBLANES = 8
LANES = 128
VMEM_LIMIT = 56 * 1024 * 1024
TOKEN_TILE = 256
TIME_CHUNK = 256
S5_SLAB = 128
LRU_SLAB = 256


def _dot(a, b):
    return jnp.dot(a.astype(BF16), b.astype(BF16), preferred_element_type=F32)


def _dot_nt(a, b):
    return lax.dot_general(a.astype(BF16), b.astype(BF16), (((1,), (1,)), ((), ())), preferred_element_type=F32)


def _dot_tn(a, b):
    return lax.dot_general(a.astype(BF16), b.astype(BF16), (((0,), (0,)), ((), ())), preferred_element_type=F32)


def _sigmoid(x):
    return jax.nn.sigmoid(x)


def _rms_stats(x):
    r = lax.rsqrt(jnp.mean(x * x, axis=-1, keepdims=True) + EPS)
    return x * r, r


def _rms_bwd(dy, xhat, r, g):
    dxn = dy * g
    dx = r * (dxn - xhat * jnp.mean(dxn * xhat, axis=-1, keepdims=True))
    return dx, dy * xhat


def _rowsum(v):
    return jnp.sum(v, axis=0, keepdims=True)


def _expm1(x):
    u = jnp.exp(x)
    um1 = u - 1.0
    safe = jnp.where(um1 == 0.0, 1.0, jnp.log(u))
    return jnp.where(um1 == 0.0, x, um1 * x / safe)


def _softplus(x):
    e = jnp.exp(-jnp.abs(x))
    u = 1.0 + e
    um1 = u - 1.0
    safe = jnp.where(um1 == 0.0, 1.0, um1)
    log1p_e = jnp.where(um1 == 0.0, e, jnp.log(u) * e / safe)
    return jnp.maximum(x, 0.0) + log1p_e


_GELU_K = math.sqrt(2.0 / math.pi)
_GELU_C = 0.044715


def _gelu(x):
    return 0.5 * x * (1.0 + jnp.tanh(_GELU_K * (x + _GELU_C * x * x * x)))


def _gelu_grad(x):
    th = jnp.tanh(_GELU_K * (x + _GELU_C * x * x * x))
    return 0.5 * (1.0 + th) + 0.5 * x * (1.0 - th * th) * _GELU_K * (1.0 + 3.0 * _GELU_C * x * x)


def _params(*sem):
    return pltpu.CompilerParams(dimension_semantics=sem, vmem_limit_bytes=VMEM_LIMIT)


def _rows(tm, n):
    return pl.BlockSpec((tm, n), lambda i: (i, 0))


def _rows_rev(tm, n, steps):
    return pl.BlockSpec((tm, n), lambda i: (steps - 1 - i, 0))


def _whole(shape):
    nd = len(shape)
    return pl.BlockSpec(shape, lambda i: (0,) * nd, pipeline_mode=pl.Buffered(1))


def _acc(shape):
    nd = len(shape)
    return pl.BlockSpec(shape, lambda i: (0,) * nd)


def _zero_on_first(*refs):
    @pl.when(pl.program_id(0) == 0)
    def _():
        for r in refs:
            r[...] = jnp.zeros_like(r)


def _inproj_fwd(x, g_mix, w_in_t, b_in, widths, comm=None):
    t, d = x.shape
    n = w_in_t.shape[0]
    tm = min(TOKEN_TILE, t)
    offs = [sum(widths[:i]) for i in range(len(widths) + 1)]

    def body(x_ref, g_ref, w_ref, b_ref, *outs):
        xhat, _ = _rms_stats(x_ref[...])
        h = (xhat * g_ref[...]).astype(BF16)
        for k, o_ref in enumerate(outs):
            lo, hi = offs[k], offs[k + 1]
            o_ref[...] = _dot_nt(h, w_ref[lo:hi, :]) + b_ref[:, lo:hi]

    return _run(
        body, comm, name="inproj_fwd", grid=(t // tm,),
        out_shape=[jax.ShapeDtypeStruct((t, w), F32) for w in widths],
        in_specs=[_rows(tm, d), _whole((1, d)), _whole((n, d)), _whole((1, n))],
        out_specs=[_rows(tm, w) for w in widths],
        semantics="parallel",
    )(x, g_mix, w_in_t, b_in)


def _s5_fwd(u, bbr_blk, bbi_blk, ar, ai, cre_blk, cimn_blk, d_skip, w_glu, b_glu):
    t, sa = u.shape
    ns, _, sw = bbr_blk.shape
    gn = ns * sw
    tc = min(TIME_CHUNK, t)
    seg = tc // SUBLANES

    def body(u_ref, bbr_ref, bbi_ref, ar_ref, ai_ref, cre_ref, cim_ref, d_ref, wg_ref, bg_ref,
             sr_ref, si_ref, y_ref, ya_ref, cr_s, ci_s, sr_s, si_s, a8r_s, a8i_s, pr_s, pi_s, c8r_s, c8i_s):
        @pl.when(pl.program_id(0) == 0)
        def _():
            cr_s[...] = jnp.zeros_like(cr_s)
            ci_s[...] = jnp.zeros_like(ci_s)
            a_r, a_i = ar_ref[...], ai_ref[...]
            a8r_s[...] = jnp.broadcast_to(a_r, a8r_s.shape)
            a8i_s[...] = jnp.broadcast_to(a_i, a8i_s.shape)
            pr_s[0:1, :] = a_r
            pi_s[0:1, :] = a_i

            def power(k, carry):
                p_r, p_i = carry
                n_r = a_r * p_r - a_i * p_i
                n_i = a_r * p_i + a_i * p_r
                pr_s[pl.ds(k, 1), :] = n_r
                pi_s[pl.ds(k, 1), :] = n_i
                return n_r, n_i

            lax.fori_loop(1, seg, power, (a_r, a_i))

        at_row = lax.broadcasted_iota(jnp.int32, (tc, tc), 0)
        at_col = lax.broadcasted_iota(jnp.int32, (tc, tc), 1)
        regroup = (at_col == (at_row % SUBLANES) * seg + at_row // SUBLANES).astype(BF16)
        restore = (at_row == (at_col % SUBLANES) * seg + at_col // SUBLANES).astype(BF16)
        uv = u_ref[...]
        ub = _dot(regroup, uv.astype(BF16)).astype(BF16)
        for m in range(ns):
            um = ub[:, m * S5_SLAB:(m + 1) * S5_SLAB]
            sr_s[:, m * sw:(m + 1) * sw] = _dot(um, bbr_ref[m])
            si_s[:, m * sw:(m + 1) * sw] = _dot(um, bbi_ref[m])

        def step(k, carry):
            c_r, c_i = carry
            at = pl.ds(pl.multiple_of(k * SUBLANES, SUBLANES), SUBLANES)
            a_r, a_i = a8r_s[...], a8i_s[...]
            n_r = a_r * c_r - a_i * c_i + sr_s[at, :]
            n_i = a_r * c_i + a_i * c_r + si_s[at, :]
            sr_s[at, :] = n_r
            si_s[at, :] = n_i
            return n_r, n_i

        zero = jnp.zeros((SUBLANES, gn), F32)
        c8r_s[...], c8i_s[...] = lax.fori_loop(0, seg, step, (zero, zero), unroll=2)
        q_r, q_i = pr_s[seg - 1:seg, :], pi_s[seg - 1:seg, :]
        c_r, c_i = cr_s[0:1, :], ci_s[0:1, :]
        for j in range(SUBLANES):
            l_r, l_i = c8r_s[j:j + 1, :], c8i_s[j:j + 1, :]
            c8r_s[j:j + 1, :] = c_r
            c8i_s[j:j + 1, :] = c_i
            c_r, c_i = q_r * c_r - q_i * c_i + l_r, q_r * c_i + q_i * c_r + l_i
        cr_s[0:1, :] = c_r
        ci_s[0:1, :] = c_i

        def fix(k, _):
            at = pl.ds(pl.multiple_of(k * SUBLANES, SUBLANES), SUBLANES)
            p_r, p_i = pr_s[pl.ds(k, 1), :], pi_s[pl.ds(k, 1), :]
            b_r, b_i = c8r_s[...], c8i_s[...]
            sr_s[at, :] = sr_s[at, :] + p_r * b_r - p_i * b_i
            si_s[at, :] = si_s[at, :] + p_r * b_i + p_i * b_r
            return 0

        lax.fori_loop(0, seg, fix, 0, unroll=2)
        for m in range(ns):
            states, chans = slice(m * sw, (m + 1) * sw), slice(m * S5_SLAB, (m + 1) * S5_SLAB)
            s_r = _dot(restore, sr_s[:, states].astype(BF16)).astype(BF16)
            s_i = _dot(restore, si_s[:, states].astype(BF16)).astype(BF16)
            sr_ref[:, states] = s_r.astype(sr_ref.dtype)
            si_ref[:, states] = s_i.astype(si_ref.dtype)
            y_ref[:, chans] = _dot(s_r, cre_ref[m]) + _dot(s_i, cim_ref[m]) + d_ref[:, chans] * uv[:, chans]
        y = y_ref[...]
        zz = _gelu(y)
        q = _dot(zz, wg_ref[...]) + bg_ref[...]
        ya_ref[...] = zz * _sigmoid(q)

    return dict(
        body=body, steps=t // tc, args=(u, bbr_blk, bbi_blk, ar, ai, cre_blk, cimn_blk, d_skip, w_glu, b_glu),
        out_shape=[jax.ShapeDtypeStruct((t, gn), BF16), jax.ShapeDtypeStruct((t, gn), BF16),
                   jax.ShapeDtypeStruct((t, sa), F32), jax.ShapeDtypeStruct((t, sa), F32)],
        in_specs=[_rows(tc, sa), _whole(bbr_blk.shape), _whole(bbi_blk.shape), _whole((1, gn)), _whole((1, gn)),
                  _whole(cre_blk.shape), _whole(cimn_blk.shape), _whole((1, sa)), _whole((sa, sa)), _whole((1, sa))],
        out_specs=[_rows(tc, gn), _rows(tc, gn), _rows(tc, sa), _rows(tc, sa)],
        scratch=[pltpu.VMEM((SUBLANES, gn), F32), pltpu.VMEM((SUBLANES, gn), F32),
                 pltpu.VMEM((tc, gn), F32), pltpu.VMEM((tc, gn), F32),
                 pltpu.VMEM((SUBLANES, gn), F32), pltpu.VMEM((SUBLANES, gn), F32),
                 pltpu.VMEM((seg, gn), F32), pltpu.VMEM((seg, gn), F32),
                 pltpu.VMEM((SUBLANES, gn), F32), pltpu.VMEM((SUBLANES, gn), F32)])


def _stages(stages, name, comm=None):
    counts = [[len(s[k]) for s in stages] for k in ("args", "out_shape", "scratch")]

    def body(*refs):
        groups, pos = [], 0
        for kind in counts:
            per_stage = []
            for c in kind:
                per_stage.append(refs[pos:pos + c])
                pos += c
            groups.append(per_stage)
        for s, ins, outs, scratch in zip(stages, *groups):
            s["body"](*ins, *outs, *scratch)

    res = _run(
        body, comm, name=name, grid=(stages[0]["steps"],),
        out_shape=[o for s in stages for o in s["out_shape"]], in_specs=[i for s in stages for i in s["in_specs"]],
        out_specs=[o for s in stages for o in s["out_specs"]], scratch_shapes=[c for s in stages for c in s["scratch"]],
        semantics="arbitrary",
    )(*[a for s in stages for a in s["args"]])
    extra = None
    if comm is not None:
        res, extra = res
    per_stage, pos = [], 0
    for c in counts[1]:
        per_stage.append(list(res[pos:pos + c]))
        pos += c
    return per_stage if comm is None else (per_stage, extra)


def _slab_dot(x, w_ref, transposed=False):
    dot = _dot_nt if transposed else _dot
    xb = x.astype(BF16)
    return jnp.concatenate([dot(xb[:, j * LRU_SLAB:(j + 1) * LRU_SLAB], w_ref[j]) for j in range(w_ref.shape[0])], axis=1)


def _lru_gates(xc, wr_ref, br_ref, wi_ref, bi_ref, lam_ref):
    r = _sigmoid(_slab_dot(xc, wr_ref) + br_ref[...])
    ig = _sigmoid(_slab_dot(xc, wi_ref) + bi_ref[...])
    sp = _softplus(-lam_ref[...])
    log_a = (-LRU_C * r) * sp
    return r, ig, sp, log_a


def _lru_fwd(u, conv_w, conv_b, wr_blk, b_r, wi_blk, b_i, lru_lambda):
    t, w = u.shape
    tc = min(TIME_CHUNK, t)
    halo = SUBLANES

    def body(u_ref, cw_ref, cb_ref, wr_ref, br_ref, wi_ref, bi_ref, lam_ref,
             xc_ref, h_ref, hp_ref, ext_s, a_s, carry_s):
        @pl.when(pl.program_id(0) == 0)
        def _():
            ext_s[0:halo, :] = jnp.zeros((halo, w), F32)
            carry_s[...] = jnp.zeros_like(carry_s)

        ext_s[halo:halo + tc, :] = u_ref[...]
        xc = cb_ref[...]
        for k in range(CONV_WIDTH):
            off = halo - (CONV_WIDTH - 1) + k
            xc = xc + cw_ref[k:k + 1, :] * ext_s[off:off + tc, :]
        ext_s[0:halo, :] = ext_s[tc:tc + halo, :]
        xc_ref[...] = xc
        r, ig, sp, log_a = _lru_gates(xc, wr_ref, br_ref, wi_ref, bi_ref, lam_ref)
        a_s[...] = jnp.exp(log_a)
        h_ref[...] = jnp.sqrt(-_expm1(2.0 * log_a)) * ig * xc

        def step(row, carry):
            at = pl.ds(row, 1)
            hp_ref[at, :] = carry
            nxt = a_s[at, :] * carry + h_ref[at, :]
            h_ref[at, :] = nxt
            return nxt

        carry_s[0:1, :] = lax.fori_loop(0, tc, step, carry_s[0:1, :], unroll=8)

    return dict(
        body=body, steps=t // tc, args=(u, conv_w, conv_b, wr_blk, b_r, wi_blk, b_i, lru_lambda),
        out_shape=[jax.ShapeDtypeStruct((t, w), F32)] * 3,
        in_specs=[_rows(tc, w), _whole((CONV_WIDTH, w)), _whole((1, w)), _whole(wr_blk.shape), _whole((1, w)),
                  _whole(wi_blk.shape), _whole((1, w)), _whole((1, w))],
        out_specs=[_rows(tc, w)] * 3,
        scratch=[pltpu.VMEM((halo + tc, w), F32), pltpu.VMEM((tc, w), F32), pltpu.VMEM((SUBLANES, w), F32)])


def _merge_ffn_up_fwd(y_a, h, za, zb, x, w_a_out_t, w_b_out, w_o, g_ffn, w_gate_t, w_up_t, comm=None):
    t, d = x.shape
    sa, lw = y_a.shape[1], h.shape[1]
    f = w_gate_t.shape[0]
    tm = min(TOKEN_TILE, t)

    def body(ya_ref, h_ref, za_ref, zb_ref, x_ref, wa_ref, wb_ref, wo_ref, g_ref, wg_ref, wu_ref,
             x1_ref, mg_ref, ma_ref, mb_ref, fg_ref, fu_ref):
        ma = _dot_nt(ya_ref[...], wa_ref[...])
        mb = _dot(h_ref[...], wb_ref[...])
        merged = _sigmoid(za_ref[...]) * ma + _sigmoid(zb_ref[...]) * mb
        ma_ref[...] = ma.astype(ma_ref.dtype)
        mb_ref[...] = mb.astype(mb_ref.dtype)
        mg_ref[...] = merged.astype(mg_ref.dtype)
        x1 = x_ref[...] + _dot(merged, wo_ref[...])
        x1_ref[...] = x1
        xhat, _ = _rms_stats(x1)
        h2 = (xhat * g_ref[...]).astype(BF16)
        fg_ref[...] = _dot_nt(h2, wg_ref[...]).astype(fg_ref.dtype)
        fu_ref[...] = _dot_nt(h2, wu_ref[...]).astype(fu_ref.dtype)

    return _run(
        body, comm, name="merge_ffn_up_fwd", grid=(t // tm,),
        out_shape=[jax.ShapeDtypeStruct((t, d), F32), jax.ShapeDtypeStruct((t, d), BF16),
                   jax.ShapeDtypeStruct((t, d), BF16), jax.ShapeDtypeStruct((t, d), BF16),
                   jax.ShapeDtypeStruct((t, f), BF16), jax.ShapeDtypeStruct((t, f), BF16)],
        in_specs=[_rows(tm, sa), _rows(tm, lw), _rows(tm, d), _rows(tm, d), _rows(tm, d),
                  _whole((d, sa)), _whole((lw, d)), _whole((d, d)), _whole((1, d)), _whole((f, d)), _whole((f, d))],
        out_specs=[_rows(tm, d)] * 4 + [_rows(tm, f)] * 2,
        semantics="parallel",
    )(y_a, h, za, zb, x, w_a_out_t, w_b_out, w_o, g_ffn, w_gate_t, w_up_t)


def _ffn_down_fwd(fg, fu, x1, w_down, comm=None):
    t, d = x1.shape
    f = fg.shape[1]
    tm = min(TOKEN_TILE, t)

    def body(fg_ref, fu_ref, x_ref, wd_ref, x2_ref):
        fgv = fg_ref[...].astype(F32)
        act = fgv * _sigmoid(fgv) * fu_ref[...].astype(F32)
        x2_ref[...] = x_ref[...] + _dot(act, wd_ref[...])

    return _run(
        body, comm, name="ffn_down_fwd", grid=(t // tm,),
        out_shape=jax.ShapeDtypeStruct((t, d), F32),
        in_specs=[_rows(tm, f), _rows(tm, f), _rows(tm, d), _whole((f, d))],
        out_specs=_rows(tm, d),
        semantics="parallel",
    )(fg, fu, x1, w_down)


def _store_on_last(pairs):
    @pl.when(pl.program_id(0) == pl.num_programs(0) - 1)
    def _():
        for acc, out in pairs:
            out[...] = acc[...].astype(out.dtype)


def _tail_fwd_bwd(x2, p, target, g_pg, w_pg, b_pg, w_ple_t, g_ple, g_final):
    t, d = x2.shape
    pd = p.shape[1]
    tm = min(TOKEN_TILE, t)

    def body(x2_ref, p_ref, tg_ref, gpg_ref, wpg_ref, bpg_ref, wple_ref, gple_ref, gfin_ref,
             dx2_ref, loss_ref, dwpg_out, dwple_out, vec_ref, dwpg_ref, dwple_ref):
        _zero_on_first(loss_ref, dwpg_ref, dwple_ref, vec_ref)
        x2v = x2_ref[...]
        xh2, r2 = _rms_stats(x2v)
        h3 = xh2 * gpg_ref[...]
        gp = _sigmoid(_dot(h3, wpg_ref[...]) + bpg_ref[...])
        pe = _dot_nt(p_ref[...], wple_ref[...])
        peh, r3 = _rms_stats(pe)
        e = peh * gple_ref[...]
        x3 = x2v + gp * e
        xh3, r4 = _rms_stats(x3)
        diff = xh3 * gfin_ref[...] - tg_ref[...]
        loss_ref[...] += 0.5 * jnp.sum(jnp.mean(diff * diff, axis=-1, keepdims=True))
        dy = diff * (1.0 / d)
        dx3, dgfin = _rms_bwd(dy, xh3, r4, gfin_ref[...])
        d_gp = dx3 * e
        d_e = dx3 * gp
        dpe, dgple = _rms_bwd(d_e, peh, r3, gple_ref[...])
        dwple_ref[...] += _dot_tn(dpe, p_ref[...])
        dpre = d_gp * gp * (1.0 - gp)
        dwpg_ref[...] += _dot_tn(h3, dpre)
        dh3 = _dot_nt(dpre, wpg_ref[...])
        dx2n, dgpg = _rms_bwd(dh3, xh2, r2, gpg_ref[...])
        dx2_ref[...] = dx3 + dx2n
        vec_ref[0:1, :] += _rowsum(dpre)
        vec_ref[1:2, :] += _rowsum(dgpg)
        vec_ref[2:3, :] += _rowsum(dgple)
        vec_ref[3:4, :] += _rowsum(dgfin)
        _store_on_last([(dwpg_ref, dwpg_out), (dwple_ref, dwple_out)])

    return pl.pallas_call(
        body, name="tail_fwd_bwd", grid=(t // tm,),
        out_shape=[jax.ShapeDtypeStruct((t, d), F32), jax.ShapeDtypeStruct((SUBLANES, LANES), F32),
                   jax.ShapeDtypeStruct((d, d), BF16), jax.ShapeDtypeStruct((d, pd), BF16),
                   jax.ShapeDtypeStruct((SUBLANES, d), F32)],
        in_specs=[_rows(tm, d), _rows(tm, pd), _rows(tm, d), _whole((1, d)), _whole((d, d)), _whole((1, d)),
                  _whole((d, pd)), _whole((1, d)), _whole((1, d))],
        out_specs=[_rows(tm, d), _acc((SUBLANES, LANES)), _acc((d, d)), _acc((d, pd)), _acc((SUBLANES, d))],
        scratch_shapes=[pltpu.VMEM((d, d), F32), pltpu.VMEM((d, pd), F32)],
        compiler_params=_params("arbitrary"),
    )(x2, p, target, g_pg, w_pg, b_pg, w_ple_t, g_ple, g_final)


def _ffn_bwd_a(dx2, fg, fu, w_down, comm=None):
    t, d = dx2.shape
    f = fg.shape[1]
    tm = min(TOKEN_TILE, t)

    def body(dx_ref, fg_ref, fu_ref, wd_ref, dfg_ref, dfu_ref, act_ref):
        dact = _dot_nt(dx_ref[...], wd_ref[...])
        fgv = fg_ref[...].astype(F32)
        fuv = fu_ref[...].astype(F32)
        sg = _sigmoid(fgv)
        silu = fgv * sg
        dfu_ref[...] = (dact * silu).astype(dfu_ref.dtype)
        dfg_ref[...] = (dact * fuv * (sg * (1.0 + fgv * (1.0 - sg)))).astype(dfg_ref.dtype)
        act_ref[...] = (silu * fuv).astype(act_ref.dtype)

    return _run(
        body, comm, name="ffn_bwd_a", grid=(t // tm,),
        out_shape=[jax.ShapeDtypeStruct((t, f), BF16)] * 3,
        in_specs=[_rows(tm, d), _rows(tm, f), _rows(tm, f), _whole((f, d))],
        out_specs=[_rows(tm, f)] * 3,
        semantics="parallel",
    )(dx2, fg, fu, w_down)


def _ffn_bwd_b(dfg, dfu, x1, dx2, g_ffn, w_gate_t, w_up_t, comm=None):
    t, d = x1.shape
    f = dfg.shape[1]
    tm = min(TOKEN_TILE, t)

    def body(dfg_ref, dfu_ref, x_ref, dx2_ref, g_ref, wg_ref, wu_ref, dx1_ref, h2_ref, vec_ref):
        _zero_on_first(vec_ref)
        dh2 = _dot(dfg_ref[...], wg_ref[...]) + _dot(dfu_ref[...], wu_ref[...])
        xhat, r = _rms_stats(x_ref[...])
        h2_ref[...] = (xhat * g_ref[...]).astype(h2_ref.dtype)
        dxn, dg = _rms_bwd(dh2, xhat, r, g_ref[...])
        dx1_ref[...] = dx2_ref[...] + dxn
        vec_ref[0:1, :] += _rowsum(dg)

    return _run(
        body, comm, name="ffn_bwd_b", grid=(t // tm,),
        out_shape=[jax.ShapeDtypeStruct((t, d), F32), jax.ShapeDtypeStruct((t, d), BF16),
                   jax.ShapeDtypeStruct((SUBLANES, d), F32)],
        in_specs=[_rows(tm, f), _rows(tm, f), _rows(tm, d), _rows(tm, d), _whole((1, d)), _whole((f, d)), _whole((f, d))],
        out_specs=[_rows(tm, d), _rows(tm, d), _acc((SUBLANES, d))],
        semantics="arbitrary",
    )(dfg, dfu, x1, dx2, g_ffn, w_gate_t, w_up_t)


def _matmul_tn(a, b, tn, name, dtype=F32):
    t, k = a.shape
    n = b.shape[1]

    def body(a_ref, b_ref, o_ref):
        o_ref[...] = _dot_tn(a_ref[...], b_ref[...]).astype(o_ref.dtype)

    return _run(
        body, None, name=name, grid=(n // tn,),
        out_shape=jax.ShapeDtypeStruct((k, n), dtype),
        in_specs=[_whole((t, k)), pl.BlockSpec((t, tn), lambda j: (0, j))],
        out_specs=pl.BlockSpec((k, tn), lambda j: (0, j)),
        semantics="parallel",
    )(a, b)


def _merge_bwd(dx1, merged, ma, mb, za, zb, y_a, h, w_o, w_a_out_t, w_b_out, comm=None):
    t, d = dx1.shape
    sa, lw = y_a.shape[1], h.shape[1]
    tm = min(TOKEN_TILE, t)

    def body(dx1_ref, mg_ref, ma_ref, mb_ref, za_ref, zb_ref, ya_ref, h_ref, wo_ref, wa_ref, wb_ref,
             dza_ref, dzb_ref, dya_ref, dyb_ref, dwo_out, dwa_out, dwb_out, dwo_ref, dwa_ref, dwb_ref):
        _zero_on_first(dwo_ref, dwa_ref, dwb_ref)
        dx1v = dx1_ref[...].astype(BF16)
        dmg = _dot_nt(dx1v, wo_ref[...])
        ga = _sigmoid(za_ref[...])
        gb = _sigmoid(zb_ref[...])
        dza_ref[...] = (dmg * ma_ref[...].astype(F32) * ga * (1.0 - ga)).astype(dza_ref.dtype)
        dzb_ref[...] = (dmg * mb_ref[...].astype(F32) * gb * (1.0 - gb)).astype(dzb_ref.dtype)
        dma = (dmg * ga).astype(BF16)
        dmb = (dmg * gb).astype(BF16)
        dya_ref[...] = _dot(dma, wa_ref[...])
        dyb_ref[...] = _dot_nt(dmb, wb_ref[...])
        dwo_ref[...] += _dot_tn(mg_ref[...], dx1v)
        dwa_ref[...] += _dot_tn(dma, ya_ref[...])
        dwb_ref[...] += _dot_tn(h_ref[...], dmb)
        _store_on_last([(dwo_ref, dwo_out), (dwa_ref, dwa_out), (dwb_ref, dwb_out)])

    return _run(
        body, comm, name="merge_bwd", grid=(t // tm,),
        out_shape=[jax.ShapeDtypeStruct((t, d), BF16), jax.ShapeDtypeStruct((t, d), BF16),
                   jax.ShapeDtypeStruct((t, sa), F32), jax.ShapeDtypeStruct((t, lw), F32),
                   jax.ShapeDtypeStruct((d, d), BF16), jax.ShapeDtypeStruct((d, sa), BF16),
                   jax.ShapeDtypeStruct((lw, d), BF16)],
        in_specs=[_rows(tm, d), _rows(tm, d), _rows(tm, d), _rows(tm, d), _rows(tm, d), _rows(tm, d),
                  _rows(tm, sa), _rows(tm, lw), _whole((d, d)), _whole((d, sa)), _whole((lw, d))],
        out_specs=[_rows(tm, d), _rows(tm, d), _rows(tm, sa), _rows(tm, lw), _acc((d, d)), _acc((d, sa)), _acc((lw, d))],
        scratch_shapes=[pltpu.VMEM((d, d), F32), pltpu.VMEM((d, sa), F32), pltpu.VMEM((lw, d), F32)],
        semantics="arbitrary",
    )(dx1, merged, ma, mb, za, zb, y_a, h, w_o, w_a_out_t, w_b_out)


def _fold_diag_blocks(dense, row_group, col_group):
    r, c = dense.shape
    rows = lax.broadcasted_iota(jnp.int32, (r, c), 0)
    cols = lax.broadcasted_iota(jnp.int32, (r, c), 1)
    kept = jnp.where(rows // row_group == cols // col_group, dense, 0.0)
    pick = (lax.broadcasted_iota(jnp.int32, (row_group, r), 0)
            == lax.broadcasted_iota(jnp.int32, (row_group, r), 1) % row_group).astype(F32)
    return jnp.dot(pick, kept, preferred_element_type=F32, precision=lax.Precision.HIGHEST)


def _lru_bwd(dh, xc, hprev, u, conv_w, wr_blk, b_r, wi_blk, b_i, lru_lambda, head_dim, comm=None):
    t, w = dh.shape
    tc = min(TIME_CHUNK, t)
    steps = t // tc
    halo = SUBLANES
    sub_per_chunk = tc // halo
    slabs = w // LRU_SLAB

    def body(dh_ref, xc_ref, hp_ref, u_ref, uh_ref, cw_ref, wr_ref, br_ref, wi_ref, bi_ref, lam_ref,
             du_ref, dwr_out, dwi_out, vec_ref, lam_s, a_s, dxc_s, uext_s, carry_s, dwr_ref, dwi_ref):
        chunk = steps - 1 - pl.program_id(0)

        @pl.when(pl.program_id(0) == 0)
        def _():
            carry_s[...] = jnp.zeros_like(carry_s)
            dxc_s[tc:tc + halo, :] = jnp.zeros((halo, w), F32)
            dwr_ref[...] = jnp.zeros_like(dwr_ref)
            dwi_ref[...] = jnp.zeros_like(dwi_ref)
            vec_ref[...] = jnp.zeros_like(vec_ref)

        xc = xc_ref[...]
        r, ig, sp, log_a = _lru_gates(xc, wr_ref, br_ref, wi_ref, bi_ref, lam_ref)
        a = jnp.exp(log_a)
        a_s[...] = a

        def step(i, q):
            at = pl.ds(tc - 1 - i, 1)
            lam_row = dh_ref[at, :] + q
            lam_s[at, :] = lam_row
            return a_s[at, :] * lam_row

        carry_s[0:1, :] = lax.fori_loop(0, tc, step, carry_s[0:1, :], unroll=8)
        lam = lam_s[...]
        mult = jnp.sqrt(-_expm1(2.0 * log_a))
        d_log_a = lam * hp_ref[...] * a - (lam * ig * xc) * (a * a) / mult
        d_ig = lam * mult * xc
        dpre_r = (d_log_a * (-LRU_C * sp)) * r * (1.0 - r)
        dpre_i = d_ig * ig * (1.0 - ig)
        dxc = lam * mult * ig + _slab_dot(dpre_r, wr_ref, transposed=True) + _slab_dot(dpre_i, wi_ref, transposed=True)
        xcb, drb, dib = xc.astype(BF16), dpre_r.astype(BF16), dpre_i.astype(BF16)
        for j in range(slabs):
            cols = slice(j * LRU_SLAB, (j + 1) * LRU_SLAB)
            dwr_ref[j] += _dot_tn(drb[:, cols], xcb[:, cols])
            dwi_ref[j] += _dot_tn(dib[:, cols], xcb[:, cols])
        vec_ref[0:1, :] += _rowsum(dxc)
        vec_ref[1:2, :] += _rowsum(dpre_r)
        vec_ref[2:3, :] += _rowsum(dpre_i)
        vec_ref[3:4, :] += _rowsum(d_log_a * (-LRU_C * r)) * (-_sigmoid(-lam_ref[...]))
        dxc_s[0:tc, :] = dxc
        du = cw_ref[CONV_WIDTH - 1:CONV_WIDTH, :] * dxc
        for k in range(CONV_WIDTH - 1):
            off = CONV_WIDTH - 1 - k
            du = du + cw_ref[k:k + 1, :] * dxc_s[off:off + tc, :]
        du_ref[...] = du.astype(du_ref.dtype)
        dxc_s[tc:tc + halo, :] = dxc_s[0:halo, :]
        uext_s[0:halo, :] = jnp.where(chunk > 0, uh_ref[...], 0.0)
        uext_s[halo:halo + tc, :] = u_ref[...]
        for k in range(CONV_WIDTH):
            off = halo - (CONV_WIDTH - 1) + k
            vec_ref[4 + k:5 + k, :] += _rowsum(dxc * uext_s[off:off + tc, :])

        @pl.when(pl.program_id(0) == steps - 1)
        def _():
            for j in range(slabs):
                cols = slice(j * LRU_SLAB, (j + 1) * LRU_SLAB)
                dwr_out[:, cols] = _fold_diag_blocks(dwr_ref[j], head_dim, head_dim).astype(dwr_out.dtype)
                dwi_out[:, cols] = _fold_diag_blocks(dwi_ref[j], head_dim, head_dim).astype(dwi_out.dtype)

    halo_spec = pl.BlockSpec((halo, w), lambda i: (jnp.maximum((steps - 1 - i) * sub_per_chunk - 1, 0), 0))
    return _run(
        body, comm, name="lru_bwd", grid=(steps,),
        out_shape=[jax.ShapeDtypeStruct((t, w), BF16), jax.ShapeDtypeStruct((head_dim, w), BF16),
                   jax.ShapeDtypeStruct((head_dim, w), BF16), jax.ShapeDtypeStruct((SUBLANES, w), F32)],
        in_specs=[_rows_rev(tc, w, steps)] * 4 + [halo_spec, _whole((CONV_WIDTH, w)), _whole(wr_blk.shape),
                                                  _whole((1, w)), _whole(wi_blk.shape), _whole((1, w)), _whole((1, w))],
        out_specs=[_rows_rev(tc, w, steps), _acc((head_dim, w)), _acc((head_dim, w)), _acc((SUBLANES, w))],
        scratch_shapes=[pltpu.VMEM((tc, w), F32), pltpu.VMEM((tc, w), F32), pltpu.VMEM((tc + halo, w), F32),
                        pltpu.VMEM((halo + tc, w), F32), pltpu.VMEM((SUBLANES, w), F32),
                        pltpu.VMEM((slabs, LRU_SLAB, LRU_SLAB), F32), pltpu.VMEM((slabs, LRU_SLAB, LRU_SLAB), F32)],
        semantics="arbitrary",
    )(dh, xc, hprev, u, u, conv_w, wr_blk, b_r, wi_blk, b_i, lru_lambda)


def _s5_bwd(dya, y, sr, si, u, w_glu, b_glu, cre_blk, cimn_blk, bbr_blk, bbi_blk, ar, ai, d_skip, comm=None):
    t, sa = dya.shape
    gn = sr.shape[1]
    ns, _, sw = bbr_blk.shape
    tc = min(TIME_CHUNK, t)
    steps = t // tc
    halo = SUBLANES

    def body(dya_ref, y_ref, sr_ref, si_ref, u_ref, wg_ref, bg_ref, cre_ref, cim_ref, bbr_ref, bbi_ref,
             ar_ref, ai_ref, d_ref, du_ref, lr_ref, li_ref, dy_ref, dwg_out, vsa_ref, vgn_ref, gr_s, gi_s, cr_s, ci_s,
             dwg_ref):
        @pl.when(pl.program_id(0) == 0)
        def _():
            cr_s[...] = jnp.zeros_like(cr_s)
            ci_s[...] = jnp.zeros_like(ci_s)
            gr_s[tc:tc + halo, :] = jnp.zeros((halo, gn), F32)
            gi_s[tc:tc + halo, :] = jnp.zeros((halo, gn), F32)
            dwg_ref[...] = jnp.zeros_like(dwg_ref)
            vsa_ref[...] = jnp.zeros_like(vsa_ref)
            vgn_ref[...] = jnp.zeros_like(vgn_ref)

        yv = y_ref[...]
        uv = u_ref[...]
        zz = _gelu(yv)
        sg = _sigmoid(_dot(zz, wg_ref[...]) + bg_ref[...])
        dyav = dya_ref[...]
        dq = dyav * zz * sg * (1.0 - sg)
        dzz = dyav * sg + _dot_nt(dq, wg_ref[...])
        dwg_ref[...] += _dot_tn(zz, dq)
        dy = dzz * _gelu_grad(yv)
        dyb = dy.astype(BF16)
        dy_ref[...] = dyb.astype(dy_ref.dtype)
        vsa_ref[0:1, :] += _rowsum(dq)
        vsa_ref[1:2, :] += _rowsum(dy * uv)
        for m in range(ns):
            dym = dyb[:, m * S5_SLAB:(m + 1) * S5_SLAB]
            gr_s[0:tc, m * sw:(m + 1) * sw] = _dot_nt(dym, cre_ref[m])
            gi_s[0:tc, m * sw:(m + 1) * sw] = _dot_nt(dym, cim_ref[m])
        a_r = ar_ref[...]
        a_i = ai_ref[...]

        def step(i, carry):
            l_r, l_i = carry
            at = pl.ds(tc - 1 - i, 1)
            n_r = gr_s[at, :] + a_r * l_r + a_i * l_i
            n_i = gi_s[at, :] + a_r * l_i - a_i * l_r
            gr_s[at, :] = n_r
            gi_s[at, :] = n_i
            return n_r, n_i

        l_r, l_i = lax.fori_loop(0, tc, step, (cr_s[0:1, :], ci_s[0:1, :]), unroll=8)
        cr_s[0:1, :] = l_r
        ci_s[0:1, :] = l_i
        nxt_r = gr_s[1:tc + 1, :]
        nxt_i = gi_s[1:tc + 1, :]
        srv = sr_ref[...].astype(F32)
        siv = si_ref[...].astype(F32)
        vgn_ref[0:1, :] += _rowsum(nxt_r * srv + nxt_i * siv)
        vgn_ref[1:2, :] += _rowsum(nxt_i * srv - nxt_r * siv)
        lam_r = gr_s[0:tc, :]
        lam_i = gi_s[0:tc, :]
        gr_s[tc:tc + halo, :] = gr_s[0:halo, :]
        gi_s[tc:tc + halo, :] = gi_s[0:halo, :]
        lrb = lam_r.astype(BF16)
        lib = lam_i.astype(BF16)
        lr_ref[...] = lrb.astype(lr_ref.dtype)
        li_ref[...] = lib.astype(li_ref.dtype)
        for m in range(ns):
            states, chans = slice(m * sw, (m + 1) * sw), slice(m * S5_SLAB, (m + 1) * S5_SLAB)
            du_ref[:, chans] = (_dot_nt(lrb[:, states], bbr_ref[m]) + _dot_nt(lib[:, states], bbi_ref[m])
                                + dy[:, chans] * d_ref[:, chans]).astype(du_ref.dtype)
        _store_on_last([(dwg_ref, dwg_out)])

    return _run(
        body, comm, name="s5_bwd", grid=(steps,),
        out_shape=[jax.ShapeDtypeStruct((t, sa), BF16), jax.ShapeDtypeStruct((t, gn), BF16),
                   jax.ShapeDtypeStruct((t, gn), BF16), jax.ShapeDtypeStruct((t, sa), BF16),
                   jax.ShapeDtypeStruct((sa, sa), BF16), jax.ShapeDtypeStruct((SUBLANES, sa), F32),
                   jax.ShapeDtypeStruct((SUBLANES, gn), F32)],
        in_specs=[_rows_rev(tc, sa, steps), _rows_rev(tc, sa, steps), _rows_rev(tc, gn, steps), _rows_rev(tc, gn, steps),
                  _rows_rev(tc, sa, steps), _whole((sa, sa)), _whole((1, sa)), _whole(cre_blk.shape),
                  _whole(cimn_blk.shape), _whole(bbr_blk.shape), _whole(bbi_blk.shape), _whole((1, gn)), _whole((1, gn)),
                  _whole((1, sa))],
        out_specs=[_rows_rev(tc, sa, steps), _rows_rev(tc, gn, steps), _rows_rev(tc, gn, steps), _rows_rev(tc, sa, steps),
                   _acc((sa, sa)), _acc((SUBLANES, sa)), _acc((SUBLANES, gn))],
        scratch_shapes=[pltpu.VMEM((tc + halo, gn), F32), pltpu.VMEM((tc + halo, gn), F32),
                        pltpu.VMEM((SUBLANES, gn), F32), pltpu.VMEM((SUBLANES, gn), F32), pltpu.VMEM((sa, sa), F32)],
        semantics="arbitrary",
    )(dya, y, sr, si, u, w_glu, b_glu, cre_blk, cimn_blk, bbr_blk, bbi_blk, ar, ai, d_skip)


def _inproj_bwd(dparts, x, dx1, g_mix, w_in_t, comm=None):
    t, d = x.shape
    n = w_in_t.shape[0]
    widths = [p.shape[1] for p in dparts]
    offs = [sum(widths[:i]) for i in range(len(widths) + 1)]
    tm = min(TOKEN_TILE, t)
    np_ = len(dparts)

    def body(*refs):
        dz_refs = refs[:np_]
        x_ref, dx1_ref, g_ref, w_ref, gx_ref, h_ref, vd_ref, vn_ref = refs[np_:]
        _zero_on_first(vd_ref, vn_ref)
        dh = jnp.zeros((tm, d), F32)
        for k, r in enumerate(dz_refs):
            lo, hi = offs[k], offs[k + 1]
            dzk = r[...]
            dh = dh + _dot(dzk, w_ref[lo:hi, :])
            vn_ref[0:1, lo:hi] += _rowsum(dzk.astype(F32))
        xhat, r0 = _rms_stats(x_ref[...])
        h_ref[...] = (xhat * g_ref[...]).astype(h_ref.dtype)
        dxn, dg = _rms_bwd(dh, xhat, r0, g_ref[...])
        gx_ref[...] = dx1_ref[...] + dxn
        vd_ref[0:1, :] += _rowsum(dg)

    return _run(
        body, comm, name="inproj_bwd", grid=(t // tm,),
        out_shape=[jax.ShapeDtypeStruct((t, d), F32), jax.ShapeDtypeStruct((t, d), BF16),
                   jax.ShapeDtypeStruct((SUBLANES, d), F32), jax.ShapeDtypeStruct((SUBLANES, n), F32)],
        in_specs=[_rows(tm, w) for w in widths] + [_rows(tm, d), _rows(tm, d), _whole((1, d)), _whole((n, d))],
        out_specs=[_rows(tm, d), _rows(tm, d), _acc((SUBLANES, d)), _acc((SUBLANES, n))],
        semantics="arbitrary",
    )(*dparts, x, dx1, g_mix, w_in_t)


def _dw_from_parts(dparts, h, tn, name, comm=None):
    t, d = h.shape
    widths = [p.shape[1] for p in dparts]
    offs = [sum(widths[:i]) for i in range(len(widths) + 1)]
    np_ = len(dparts)

    def body(*refs):
        h_ref, o_ref = refs[np_], refs[np_ + 1]
        hv = h_ref[...]
        for k, r in enumerate(refs[:np_]):
            o_ref[offs[k]:offs[k + 1], :] = _dot_tn(r[...], hv).astype(o_ref.dtype)

    return _run(
        body, comm, name=name, grid=(d // tn,),
        out_shape=jax.ShapeDtypeStruct((offs[-1], d), BF16),
        in_specs=[_whole(p.shape) for p in dparts] + [pl.BlockSpec((t, tn), lambda j: (0, j))],
        out_specs=pl.BlockSpec((offs[-1], tn), lambda j: (0, j)),
        semantics="parallel",
    )(*dparts, h)


def _prep(lr_row, li_row, ldt_row, lr_col, li_col, ldt_col, b_re, b_im, c_re, c_im, w_r, w_i, comm=None):
    gn, pch = b_re.shape
    sa, n = c_re.shape
    w, hd = w_r.shape
    ns, sw, lsl = sa // S5_SLAB, S5_SLAB * n // pch, w // LRU_SLAB

    def spread_cols(vals, row_group, col_group, width):
        r, k = vals.shape
        tile = (lax.broadcasted_iota(jnp.int32, (k, width), 0) == lax.broadcasted_iota(jnp.int32, (k, width), 1) % k)
        rows = lax.broadcasted_iota(jnp.int32, (r, width), 0) // row_group
        cols = lax.broadcasted_iota(jnp.int32, (r, width), 1) // col_group
        return jnp.where(rows == cols, _dot(vals, tile.astype(BF16)), 0.0)

    def spread_rows(vals, row_group, col_group, height):
        k, c = vals.shape
        tile = (lax.broadcasted_iota(jnp.int32, (height, k), 0) % k == lax.broadcasted_iota(jnp.int32, (height, k), 1))
        rows = lax.broadcasted_iota(jnp.int32, (height, c), 0) // row_group
        cols = lax.broadcasted_iota(jnp.int32, (height, c), 1) // col_group
        return jnp.where(rows == cols, _dot(tile.astype(BF16), vals), 0.0)

    def body(lrr, lir, ldr, lrc, lic, ldc, bre, bim, cre, cim, wr, wi,
             ar_o, ai_o, bbr_o, bbi_o, cre_o, cim_o, wr_o, wi_o):
        ar, ai, _, _ = _disc_scalars(lrr[...], lir[...], ldr[...])
        ar_o[...] = ar
        ai_o[...] = ai
        _, _, bbr, bbi = _disc_cols(lrc[...], lic[...], ldc[...], bre[...], bim[...])
        bbr_t, bbi_t = bbr.T, bbi.T
        for m in range(ns):
            bbr_o[m] = spread_rows(bbr_t[:, m * sw:(m + 1) * sw], pch, n, S5_SLAB).astype(bbr_o.dtype)
            bbi_o[m] = spread_rows(bbi_t[:, m * sw:(m + 1) * sw], pch, n, S5_SLAB).astype(bbi_o.dtype)
            rows = slice(m * S5_SLAB, (m + 1) * S5_SLAB)
            cre_o[m] = spread_rows(cre[rows, :].T, n, pch, sw).astype(cre_o.dtype)
            cim_o[m] = spread_rows(-cim[rows, :].T, n, pch, sw).astype(cim_o.dtype)
        for j in range(lsl):
            rows = slice(j * LRU_SLAB, (j + 1) * LRU_SLAB)
            wr_o[j] = spread_cols(wr[rows, :], hd, hd, LRU_SLAB).astype(wr_o.dtype)
            wi_o[j] = spread_cols(wi[rows, :], hd, hd, LRU_SLAB).astype(wi_o.dtype)

    args = (lr_row, li_row, ldt_row, lr_col, li_col, ldt_col, b_re, b_im, c_re, c_im, w_r, w_i)
    out_shape = [jax.ShapeDtypeStruct((1, gn), F32), jax.ShapeDtypeStruct((1, gn), F32),
                 jax.ShapeDtypeStruct((ns, S5_SLAB, sw), BF16), jax.ShapeDtypeStruct((ns, S5_SLAB, sw), BF16),
                 jax.ShapeDtypeStruct((ns, sw, S5_SLAB), BF16), jax.ShapeDtypeStruct((ns, sw, S5_SLAB), BF16),
                 jax.ShapeDtypeStruct((lsl, LRU_SLAB, LRU_SLAB), BF16),
                 jax.ShapeDtypeStruct((lsl, LRU_SLAB, LRU_SLAB), BF16)]
    return _run(
        body, comm, name="prep", grid=(1,), out_shape=out_shape, in_specs=[_whole(a.shape) for a in args],
        out_specs=[_acc(s.shape) for s in out_shape], semantics="arbitrary",
    )(*args)


def _s5_param_grads(lam_r, lam_i, sr, si, u, dy, pch, n, comm=None):
    t, gn = lam_r.shape
    sa = u.shape[1]
    sw = S5_SLAB * n // pch

    def body(lr_ref, li_ref, sr_ref, si_ref, u_ref, dy_ref, dbr_ref, dbi_ref, dcr_ref, dci_ref):
        uv = u_ref[...]
        dyv = dy_ref[...]
        dbr_ref[...] = _fold_diag_blocks(_dot_tn(uv, lr_ref[...]), pch, n).astype(dbr_ref.dtype)
        dbi_ref[...] = _fold_diag_blocks(_dot_tn(uv, li_ref[...]), pch, n).astype(dbi_ref.dtype)
        dcr_ref[...] = _fold_diag_blocks(_dot_tn(sr_ref[...], dyv), n, pch).astype(dcr_ref.dtype)
        dci_ref[...] = _fold_diag_blocks(_dot_tn(si_ref[...], dyv), n, pch).astype(dci_ref.dtype)

    states = pl.BlockSpec((t, sw), lambda m: (0, m))
    chans = pl.BlockSpec((t, S5_SLAB), lambda m: (0, m))
    return _run(
        body, comm, name="s5_param_grads", grid=(sa // S5_SLAB,),
        out_shape=[jax.ShapeDtypeStruct((pch, gn), BF16), jax.ShapeDtypeStruct((pch, gn), BF16),
                   jax.ShapeDtypeStruct((n, sa), BF16), jax.ShapeDtypeStruct((n, sa), BF16)],
        in_specs=[states, states, states, states, chans, chans],
        out_specs=[pl.BlockSpec((pch, sw), lambda m: (0, m)), pl.BlockSpec((pch, sw), lambda m: (0, m)),
                   pl.BlockSpec((n, S5_SLAB), lambda m: (0, m)), pl.BlockSpec((n, S5_SLAB), lambda m: (0, m))],
        semantics="parallel",
    )(lam_r, lam_i, sr, si, u, dy)


SMALL_PARTS = ["vec_tail", "vec_ffn", "vec_lru", "vec_mix", "vec_bin", "vec_sa", "vec_gn", "dw_r", "dw_i", "dbb_re",
               "dbb_im", "dc_re", "dc_imn", "loss"]


def _small_reduce(parts, shapes, lr_col, li_col, ldt_col, b_re, b_im, groups, after=()):
    gn, pch = b_re.shape
    n = gn // groups
    nparts = parts[SMALL_PARTS[0]].size // math.prod(shapes[SMALL_PARTS[0]])
    np_, nout, first_out = len(SMALL_PARTS), 24, len(SMALL_PARTS) + 5 + len(after)

    def body(*refs):
        ins = refs[:np_]
        lr, li, ldt, bre, bim = refs[np_:np_ + 5]
        outs = refs[first_out:first_out + nout]
        sums = dict(zip(SMALL_PARTS, refs[first_out + nout:]))

        @pl.when(pl.program_id(0) == 0)
        def _():
            for k, r in zip(SMALL_PARTS, ins):
                sums[k][...] = r[...].astype(F32)

        @pl.when(pl.program_id(0) > 0)
        def _():
            for k, r in zip(SMALL_PARTS, ins):
                sums[k][...] += r[...].astype(F32)

        @pl.when(pl.program_id(0) == nparts - 1)
        def _():
            finish({k: s[...] for k, s in sums.items()}, lr, li, ldt, bre, bim, *outs)

    def finish(tot, lr, li, ldt, bre, bim, o_loss, o_gmix, o_bin, o_bglu, o_s5d, o_convb, o_br, o_bi, o_lam, o_gffn,
               o_gpg, o_bpg, o_gple, o_gfin, o_wr, o_wi, o_cre, o_cim, o_bre, o_bim, o_lre, o_lim, o_ldt, o_convw):
        o_loss[...] = tot["loss"]
        o_convw[...] = tot["vec_lru"][SUBLANES - CONV_WIDTH:SUBLANES]
        o_bpg[...] = tot["vec_tail"][0:1]
        o_gpg[...] = tot["vec_tail"][1:2]
        o_gple[...] = tot["vec_tail"][2:3]
        o_gfin[...] = tot["vec_tail"][3:4]
        o_gffn[...] = tot["vec_ffn"][0:1]
        o_convb[...] = tot["vec_lru"][0:1]
        o_br[...] = tot["vec_lru"][1:2]
        o_bi[...] = tot["vec_lru"][2:3]
        o_lam[...] = tot["vec_lru"][3:4]
        o_gmix[...] = tot["vec_mix"][0:1]
        o_bin[...] = tot["vec_bin"][0:1]
        o_bglu[...] = tot["vec_sa"][0:1]
        o_s5d[...] = tot["vec_sa"][1:2]
        o_wr[...] = tot["dw_r"].T
        o_wi[...] = tot["dw_i"].T
        o_cre[...] = tot["dc_re"].T
        o_cim[...] = -tot["dc_imn"].T
        d_a = tot["vec_gn"].T
        _, chain = jax.vjp(_disc_cols, lr[...], li[...], ldt[...], bre[...], bim[...])
        d_lr, d_li, d_ldt, d_bre, d_bim = chain((d_a[:, 0:1], d_a[:, 1:2], tot["dbb_re"].T, tot["dbb_im"].T))
        o_lre[...] = d_lr
        o_lim[...] = d_li
        o_bre[...] = d_bre
        o_bim[...] = d_bim
        same = (lax.broadcasted_iota(jnp.int32, (groups, gn), 0)
                == lax.broadcasted_iota(jnp.int32, (groups, gn), 1) // n).astype(F32)
        o_ldt[...] = jnp.dot(same, d_ldt * jnp.ones((1, LANES), F32), preferred_element_type=F32,
                             precision=lax.Precision.HIGHEST)[:, 0:1]

    d = shapes["vec_mix"][1]
    nz = shapes["vec_bin"][1]
    sa = shapes["vec_sa"][1]
    w = shapes["vec_lru"][1]
    row = lambda c: jax.ShapeDtypeStruct((1, c), F32)
    out_shape = [jax.ShapeDtypeStruct(shapes["loss"], F32), row(d), row(nz), row(sa), row(sa), row(w), row(w), row(w),
                 row(w), row(d), row(d), row(d), row(d), row(d),
                 jax.ShapeDtypeStruct(shapes["dw_r"][::-1], F32), jax.ShapeDtypeStruct(shapes["dw_i"][::-1], F32),
                 jax.ShapeDtypeStruct(shapes["dc_re"][::-1], F32), jax.ShapeDtypeStruct(shapes["dc_imn"][::-1], F32),
                 jax.ShapeDtypeStruct((gn, pch), F32), jax.ShapeDtypeStruct((gn, pch), F32),
                 jax.ShapeDtypeStruct((gn, 1), F32), jax.ShapeDtypeStruct((gn, 1), F32),
                 jax.ShapeDtypeStruct((groups, 1), F32), jax.ShapeDtypeStruct((CONV_WIDTH, w), F32)]
    def part_spec(k):
        r, c = shapes[k]
        if parts[k].ndim == 3:
            return pl.BlockSpec((None, r, c), lambda i: (i, 0, 0))
        return pl.BlockSpec((r, c), lambda i: (i, 0))

    outs = _run(
        body, None, name="small_reduce", grid=(nparts,), out_shape=out_shape,
        in_specs=([part_spec(k) for k in SMALL_PARTS] + [_whole(a.shape) for a in (lr_col, li_col, ldt_col, b_re, b_im)]
                  + [_ANY] * len(after)),
        out_specs=[_acc(s.shape) for s in out_shape],
        scratch_shapes=[pltpu.VMEM(shapes[k], F32) for k in SMALL_PARTS],
        semantics="arbitrary",
    )(*[parts[k] for k in SMALL_PARTS], lr_col, li_col, ldt_col, b_re, b_im, *after)
    names = ["loss", "g_mix", "b_in", "b_glu", "s5_d", "conv_b", "b_r", "b_i", "lru_lambda", "g_ffn", "g_ple_gate",
             "b_ple_gate", "g_ple", "g_final", "w_r", "w_i", "s5_c_re", "s5_c_im", "s5_b_re", "s5_b_im", "lam_re",
             "lam_im", "log_dt", "conv_w"]
    return dict(zip(names, outs))


def _adamw_small(ws, gs, ms, vs):
    n = len(ws)

    def body(*refs):
        w_r, g_r, m_r, v_r = (refs[i * n:(i + 1) * n] for i in range(4))
        g_o, d_o, m_o, v_o = (refs[(4 + i) * n:(5 + i) * n] for i in range(4))
        for i in range(n):
            g = g_r[i][...]
            delta, m_new, v_new = _adamw_math(w_r[i][...], g, m_r[i][...], v_r[i][...])
            g_o[i][...] = g
            d_o[i][...] = delta
            m_o[i][...] = m_new
            v_o[i][...] = v_new

    shapes = [jax.ShapeDtypeStruct(a.shape, F32) for a in ws]
    outs = pl.pallas_call(body, name="adamw_small", out_shape=shapes * 4)(*ws, *gs, *ms, *vs)
    return outs[:n], outs[n:2 * n], outs[2 * n:3 * n], outs[3 * n:]


def _adamw_math(w, g, m, v):
    m_new = ADAM_B1 * m + (1.0 - ADAM_B1) * g
    v_new = ADAM_B2 * v + (1.0 - ADAM_B2) * (g * g)
    m_hat = m_new / (1.0 - ADAM_B1 ** ADAM_STEP)
    v_hat = v_new / (1.0 - ADAM_B2 ** ADAM_STEP)
    delta = -ADAM_LR * (m_hat / (jnp.sqrt(v_hat) + ADAM_EPS) + ADAM_WD * w)
    return delta, m_new, v_new


def _row_tile(rows):
    for cand in range(256, 0, -16):
        if rows % cand == 0:
            return cand
    return rows


def _adamw(parts, w, m, v, name, transposed=False, own=None, after=()):
    nw = len(w)
    rows, cols = w[0].shape
    npart = parts[0].shape[0]
    tr = _row_tile(rows)
    if transposed:
        parts_spec = pl.BlockSpec((npart, cols, tr), lambda i: (0, 0, i))
    else:
        parts_spec = pl.BlockSpec((npart, tr, cols), lambda i: (0, i, 0))
    per = 4 if own is None else 5
    first_out = per * nw + len(after)

    def body(*refs):
        mine = None if own is None else _chip(_mesh_position())
        for i in range(nw):
            group = refs[per * i:per * i + per]
            p_ref, (w_ref, m_ref, v_ref) = group[0], group[-3:]
            g_ref, d_ref, mo_ref, vo_ref = refs[first_out + 4 * i:first_out + 4 * i + 4]

            def part(k):
                a = p_ref[k].astype(F32)
                return a if own is None else jnp.where(mine == k, group[1][k].astype(F32), a)

            g = part(0)
            for k in range(1, npart):
                g = g + part(k)
            if transposed:
                g = g.T
            delta, m_new, v_new = _adamw_math(w_ref[...], g, m_ref[...], v_ref[...])
            g_ref[...] = g
            d_ref[...] = delta
            mo_ref[...] = m_new
            vo_ref[...] = v_new

    groups = zip(parts, w, m, v) if own is None else zip(parts, own, w, m, v)
    res = pl.pallas_call(
        body, name=name, grid=(rows // tr,),
        out_shape=[jax.ShapeDtypeStruct((rows, cols), F32)] * (4 * nw),
        in_specs=([parts_spec] * (per - 3) + [_rows(tr, cols)] * 3) * nw + [_ANY] * len(after),
        out_specs=[_rows(tr, cols)] * (4 * nw),
        compiler_params=_params("parallel"),
    )(*[a for group in groups for a in group], *after)
    return [res[4 * i:4 * i + 4] for i in range(nw)]


def _mesh_position():
    return lax.axis_index("x"), lax.axis_index("y"), lax.axis_index("c")


def _flip(pos, rel):
    x, y, c = pos
    return (1 - x if rel & 4 else x, 1 - y if rel & 2 else y, 1 - c if rel & 1 else c)


def _index(pos):
    return 4 * pos[0] + 2 * pos[1] + pos[2]


_ANY = pl.BlockSpec(memory_space=pl.ANY)
FLAT_ROWS = 32


def _dma_sems(n):
    return [pltpu.SemaphoreType.DMA((n, N_DEV - 1)), pltpu.SemaphoreType.DMA((n, N_DEV - 1)), pltpu.SemaphoreType.DMA((n,))]


def _block_of(ref, idx, rows, flat):
    if flat:
        return ref.at[pl.ds(pl.multiple_of(idx * rows, FLAT_ROWS), rows), :]
    return ref.at[idx]


class _Gather:
    chips = (4, 2, 6)
    rels = frozenset((1, 4, 2, 6))

    def __init__(self, shards):
        self.inputs = list(shards)
        self.flat = [s.shape[0] % FLAT_ROWS == 0 for s in shards]
        self.out_shape = [
            jax.ShapeDtypeStruct((N_DEV * s.shape[0], s.shape[1]) if f else (N_DEV,) + s.shape, s.dtype)
            for s, f in zip(shards, self.flat)]
        self.sems = _dma_sems(len(shards))

    def _copy(self, ins, outs, sems, i, k, block, to, own=False, lands_index=None):
        a = i if lands_index is None else lands_index
        dst = _block_of(outs[a], _index(block), self.inputs[a].shape[0], self.flat[a])
        return pltpu.make_async_remote_copy(
            src_ref=ins[a] if own else dst, dst_ref=dst, send_sem=sems[0].at[i, k], recv_sem=sems[1].at[i, k],
            device_id=to, device_id_type=MESH)

    def _local(self, ins, outs, sems, i, me):
        dst = _block_of(outs[i], _index(me), self.inputs[i].shape[0], self.flat[i])
        return pltpu.make_async_copy(ins[i], dst, sems[2].at[i])

    def _first(self, ins, outs, sems, i, me):
        cps = [self._copy(ins, outs, sems, i, 0, me, _flip(me, 1), own=True)]
        cps += [self._copy(ins, outs, sems, i, 1 + j, me, _flip(me, rel), own=True) for j, rel in enumerate(self.chips)]
        return cps

    def _passed(self, ins, outs, sems, i, j, me):
        return self._copy(ins, outs, sems, i, 4 + j, _flip(me, self.chips[j]), _flip(me, 1))

    def before(self, ins, outs, sems):
        n = len(self.inputs)
        me = _mesh_position()

        @pl.when(pl.program_id(0) == 0)
        def _():
            for i in range(n):
                self._local(ins, outs, sems, i, me).start()
                for cp in self._first(ins, outs, sems, i, me):
                    cp.start()

        @pl.when(pl.program_id(0) == pl.num_programs(0) - 1)
        def _():
            for j, rel in enumerate(self.chips):
                for i in range(n):
                    self._copy(ins, outs, sems, i, 1 + j, _flip(me, rel), me).wait_recv()
                    self._passed(ins, outs, sems, i, j, me).start()

    def after(self, ins, outs, sems):
        n = len(self.inputs)
        me = _mesh_position()
        sibling = _flip(me, 1)

        @pl.when(pl.program_id(0) == pl.num_programs(0) - 1)
        def _():
            for i in range(n):
                self._copy(ins, outs, sems, i, 0, sibling, me).wait_recv()
                for j, rel in enumerate(self.chips):
                    self._copy(ins, outs, sems, i, 4 + j, _flip(sibling, rel), me).wait_recv()
            for i in range(n):
                for cp in self._first(ins, outs, sems, i, me):
                    cp.wait_send()
                for j in range(len(self.chips)):
                    self._passed(ins, outs, sems, i, j, me).wait_send()
                self._local(ins, outs, sems, i, me).wait()


N_CHIPS = N_DEV // 2


def _chip(pos):
    return 2 * pos[0] + pos[1]


class _PairSwap:
    rels = frozenset((1,))

    def __init__(self, arrays):
        self.inputs = list(arrays)
        self.rows = [a.shape[0] // N_DEV for a in arrays]
        for r in self.rows:
            assert r % FLAT_ROWS == 0, r
        self.out_shape = [jax.ShapeDtypeStruct((N_CHIPS, r, a.shape[1]), a.dtype) for a, r in zip(arrays, self.rows)]
        n = len(arrays)
        self.sems = [pltpu.SemaphoreType.DMA((n, N_CHIPS)), pltpu.SemaphoreType.DMA((n, N_CHIPS))]

    def _copy(self, ins, outs, sems, i, j, me):
        sibling = _flip(me, 1)
        return pltpu.make_async_remote_copy(
            src_ref=_block_of(ins[i], 2 * j + sibling[2], self.rows[i], True), dst_ref=outs[i].at[j],
            send_sem=sems[0].at[i, j], recv_sem=sems[1].at[i, j], device_id=sibling, device_id_type=MESH)

    def before(self, ins, outs, sems):
        me = _mesh_position()

        @pl.when(pl.program_id(0) == 0)
        def _():
            for i in range(len(self.inputs)):
                for j in range(N_CHIPS):
                    self._copy(ins, outs, sems, i, j, me).start()

    def after(self, ins, outs, sems):
        me = _mesh_position()

        @pl.when(pl.program_id(0) == pl.num_programs(0) - 1)
        def _():
            for i in range(len(self.inputs)):
                for j in range(N_CHIPS):
                    self._copy(ins, outs, sems, i, j, me).wait()


class _ChipExchange:
    chips = (4, 2, 6)
    rels = frozenset(chips)

    def __init__(self, arrays, after=()):
        self.arrays = len(arrays)
        self.inputs = list(arrays) + list(after)
        self.out_shape = [jax.ShapeDtypeStruct(a.shape, a.dtype) for a in arrays]
        n = len(arrays)
        self.sems = [pltpu.SemaphoreType.DMA((n, 3)), pltpu.SemaphoreType.DMA((n, 3)), pltpu.SemaphoreType.DMA((n,))]

    def _send(self, ins, outs, sems, i, k, me):
        peer = _flip(me, self.chips[k])
        return pltpu.make_async_remote_copy(
            src_ref=ins[i].at[_chip(peer)], dst_ref=outs[i].at[_chip(me)], send_sem=sems[0].at[i, k],
            recv_sem=sems[1].at[i, k], device_id=peer, device_id_type=MESH)

    def _arrival(self, ins, outs, sems, i, k, me):
        peer = _flip(me, self.chips[k])
        return pltpu.make_async_remote_copy(
            src_ref=ins[i].at[_chip(me)], dst_ref=outs[i].at[_chip(peer)], send_sem=sems[0].at[i, k],
            recv_sem=sems[1].at[i, k], device_id=peer, device_id_type=MESH)

    def _local(self, ins, outs, sems, i, me):
        return pltpu.make_async_copy(ins[i].at[_chip(me)], outs[i].at[_chip(me)], sems[2].at[i])

    def before(self, ins, outs, sems):
        me = _mesh_position()

        @pl.when(pl.program_id(0) == 0)
        def _():
            for i in range(self.arrays):
                self._local(ins, outs, sems, i, me).start()
                for k in range(len(self.chips)):
                    self._send(ins, outs, sems, i, k, me).start()

    def after(self, ins, outs, sems):
        me = _mesh_position()

        @pl.when(pl.program_id(0) == pl.num_programs(0) - 1)
        def _():
            for i in range(self.arrays):
                for k in range(len(self.chips)):
                    self._arrival(ins, outs, sems, i, k, me).wait_recv()
            for i in range(self.arrays):
                for k in range(len(self.chips)):
                    self._send(ins, outs, sems, i, k, me).wait_send()
                self._local(ins, outs, sems, i, me).wait()


_HBM = pl.BlockSpec(memory_space=pltpu.HBM)
_SEM = pl.BlockSpec(memory_space=pltpu.SEMAPHORE)
_DATAFLOW = pltpu.SideEffectType.DATAFLOW_SIDE_EFFECTING
_CHIP_RELS = (4, 2, 6)
_SPLIT_COLLECTIVE_ID = 3


def _chips_copy(src_ref, dst_ref, send_sems, recv_sems, k, me):
    peer = _flip(me, _CHIP_RELS[k])
    return pltpu.make_async_remote_copy(
        src_ref=src_ref.at[_chip(peer)], dst_ref=dst_ref.at[_chip(me)], send_sem=send_sems.at[k],
        recv_sem=recv_sems.at[k], device_id=peer, device_id_type=MESH)


def _chips_start(pairs, name):
    def body(p_ref, land_ref, send_sems, recv_sems, p_thru, land_thru, token):
        me = _mesh_position()
        sem = pltpu.get_barrier_semaphore()
        for rel in _CHIP_RELS:
            pl.semaphore_signal(sem, inc=1, device_id=_flip(me, rel), device_id_type=MESH)
        pl.semaphore_wait(sem, len(_CHIP_RELS))
        for k in range(len(_CHIP_RELS)):
            _chips_copy(p_ref, land_ref, send_sems, recv_sems, k, me).start()
        token[...] = jnp.zeros_like(token)

    n = len(_CHIP_RELS)
    return pl.pallas_call(
        body, name="chips_start_" + name,
        out_shape=(pltpu.SemaphoreType.DMA((n,)), pltpu.SemaphoreType.DMA((n,)), pltpu.HBM(pairs.shape, pairs.dtype),
                   pltpu.HBM(pairs.shape, pairs.dtype), jax.ShapeDtypeStruct((SUBLANES, LANES), F32)),
        in_specs=(_HBM, _HBM), out_specs=(_SEM, _SEM, _HBM, _HBM, pl.BlockSpec(memory_space=pltpu.VMEM)),
        input_output_aliases={0: 2, 1: 3},
        compiler_params=pltpu.CompilerParams(has_side_effects=_DATAFLOW, collective_id=_SPLIT_COLLECTIVE_ID),
    )(pltpu.with_memory_space_constraint(pairs, pltpu.HBM),
      pltpu.with_memory_space_constraint(lax.empty(pairs.shape, pairs.dtype), pltpu.HBM))


def _chips_wait(send_sems, recv_sems, p_thru, land_thru, after, name):
    def body(p_ref, land_ref, send_sems, recv_sems, *rest):
        me = _mesh_position()
        for k in range(len(_CHIP_RELS)):
            copy = _chips_copy(p_ref, land_ref, send_sems, recv_sems, k, me)
            copy.wait_send()
            copy.wait_recv()

    return pl.pallas_call(
        body, name="chips_wait_" + name,
        out_shape=(pltpu.HBM(p_thru.shape, p_thru.dtype), pltpu.HBM(p_thru.shape, p_thru.dtype)),
        in_specs=(_HBM, _HBM, _SEM, _SEM) + (_ANY,) * len(after), out_specs=(_HBM, _HBM),
        input_output_aliases={0: 0, 1: 1}, compiler_params=pltpu.CompilerParams(has_side_effects=_DATAFLOW),
    )(p_thru, land_thru, send_sems, recv_sems, *after)


_GATHER_START_ID, _GATHER_PASS_ID = 4, 5


class _Rows:
    def __init__(self, ref, width):
        self.ref, self.width, self.at = ref, width, self

    def __getitem__(self, idx):
        i, k = idx
        return self.ref.at[i * self.width + k]


def _handshake(rels):
    me = _mesh_position()
    sem = pltpu.get_barrier_semaphore()
    for rel in rels:
        pl.semaphore_signal(sem, inc=1, device_id=_flip(me, rel), device_id_type=MESH)
    pl.semaphore_wait(sem, len(rels))


def _gather_start(shards, name, after=()):
    job, n, na = _Gather(shards), len(shards), len(after)

    def body(*refs):
        ins, lands = refs[:n], refs[n:2 * n]
        send, recv, local = refs[2 * n + na:2 * n + na + 3]
        sems = (_Rows(send, N_DEV - 1), _Rows(recv, N_DEV - 1), local)
        token = refs[-1]
        me = _mesh_position()
        _handshake(sorted(job.rels))
        for i in range(n):
            job._local(ins, lands, sems, i, me).start()
            for cp in job._first(ins, lands, sems, i, me):
                cp.start()
        token[...] = jnp.zeros_like(token)

    res = pl.pallas_call(
        body, name=name,
        out_shape=(pltpu.SemaphoreType.DMA((n * (N_DEV - 1),)), pltpu.SemaphoreType.DMA((n * (N_DEV - 1),)),
                   pltpu.SemaphoreType.DMA((n,))) + tuple(pltpu.HBM(s.shape, s.dtype) for s in job.out_shape)
        + (jax.ShapeDtypeStruct((SUBLANES, LANES), F32),),
        in_specs=(_HBM,) * (2 * n) + (_ANY,) * na,
        out_specs=(_SEM,) * 3 + (_HBM,) * n + (pl.BlockSpec(memory_space=pltpu.VMEM),),
        input_output_aliases={n + i: 3 + i for i in range(n)},
        compiler_params=pltpu.CompilerParams(has_side_effects=_DATAFLOW, collective_id=_GATHER_START_ID),
    )(*[pltpu.with_memory_space_constraint(s, pltpu.HBM) for s in shards],
      *[pltpu.with_memory_space_constraint(lax.empty(s.shape, s.dtype), pltpu.HBM) for s in job.out_shape], *after)
    return list(res[:3]), list(res[3:3 + n]), res[-1]


def _gather_pass(shards, which, sems, lands, name, after=()):
    job, n, m, na = _Gather(shards), len(shards), len(which), len(after)

    def body(*refs):
        lands_r, local = refs[:m], refs[m + 2]
        send, recv = _Rows(refs[m], N_DEV - 1), _Rows(refs[m + 1], N_DEV - 1)
        send2, recv2 = _Rows(refs[m + 3 + na], 3), _Rows(refs[m + 3 + na + 1], 3)
        me = _mesh_position()
        _handshake((1,))
        by_index = {i: lands_r[q] for q, i in enumerate(which)}
        for j, rel in enumerate(job.chips):
            for q, i in enumerate(which):
                job._copy(None, by_index, (send, recv, local), i, 1 + j, _flip(me, rel), me).wait_recv()
                job._copy(None, by_index, (send2, recv2), q, j, _flip(me, rel), _flip(me, 1), lands_index=i).start()

    res = pl.pallas_call(
        body, name=name,
        out_shape=(pltpu.SemaphoreType.DMA((m * 3,)), pltpu.SemaphoreType.DMA((m * 3,)))
        + tuple(pltpu.HBM(lands[i].shape, lands[i].dtype) for i in which),
        in_specs=(_HBM,) * m + (_SEM,) * 3 + (_ANY,) * na, out_specs=(_SEM, _SEM) + (_HBM,) * m,
        input_output_aliases={q: 2 + q for q in range(m)},
        compiler_params=pltpu.CompilerParams(has_side_effects=_DATAFLOW, collective_id=_GATHER_PASS_ID),
    )(*[lands[i] for i in which], *sems, *after)
    return list(res[:2]), list(res[2:])


def _gather_finish(shards, which, sems, sems2, lands, after, name):
    job, n, m = _Gather(shards), len(shards), len(which)

    def body(*refs):
        ins, lands_r = refs[:m], refs[m:2 * m]
        send, recv, local = _Rows(refs[2 * m], N_DEV - 1), _Rows(refs[2 * m + 1], N_DEV - 1), refs[2 * m + 2]
        send2, recv2 = _Rows(refs[2 * m + 3], 3), _Rows(refs[2 * m + 4], 3)
        me = _mesh_position()
        sibling = _flip(me, 1)
        by_index = {i: lands_r[q] for q, i in enumerate(which)}
        own = {i: ins[q] for q, i in enumerate(which)}
        for q, i in enumerate(which):
            job._copy(None, by_index, (send, recv, local), i, 0, sibling, me).wait_recv()
            for j, rel in enumerate(job.chips):
                job._copy(None, by_index, (send2, recv2), q, j, _flip(sibling, rel), me, lands_index=i).wait_recv()
                job._copy(None, by_index, (send2, recv2), q, j, _flip(me, rel), sibling, lands_index=i).wait_send()
            for cp in job._first(own, by_index, (send, recv, local), i, me):
                cp.wait_send()
            job._local(own, by_index, (send, recv, local), i, me).wait()

    res = pl.pallas_call(
        body, name=name, out_shape=tuple(pltpu.HBM(lands[i].shape, lands[i].dtype) for i in which),
        in_specs=(_HBM,) * (2 * m) + (_SEM,) * 5 + (_ANY,) * len(after), out_specs=(_HBM,) * m,
        input_output_aliases={m + q: q for q in range(m)},
        compiler_params=pltpu.CompilerParams(has_side_effects=_DATAFLOW),
    )(*[pltpu.with_memory_space_constraint(shards[i], pltpu.HBM) for i in which], *[lands[i] for i in which],
      *sems, *sems2, *after)
    return list(res)


def _pair_add(grads, halves, name):
    n = len(grads)

    def body(*refs):
        g_refs, h_refs, o_refs = refs[:n], refs[n:2 * n], refs[2 * n:]
        c = lax.axis_index("c")
        for i in range(n):
            r = h_refs[i].shape[1]
            for j in range(N_CHIPS):
                own = g_refs[i][pl.ds(pl.multiple_of((2 * j + c) * r, FLAT_ROWS), r), :]
                o_refs[i][j] = (own.astype(F32) + h_refs[i][j].astype(F32)).astype(o_refs[i].dtype)

    return pl.pallas_call(
        body, name=name, out_shape=[jax.ShapeDtypeStruct(h.shape, h.dtype) for h in halves],
        compiler_params=pltpu.CompilerParams(vmem_limit_bytes=VMEM_LIMIT),
    )(*grads, *halves)


class _Both:
    def __init__(self, first, second):
        self.jobs = (first, second)
        self.rels = first.rels | second.rels
        self.inputs = first.inputs + second.inputs
        self.out_shape = first.out_shape + second.out_shape
        self.sems = first.sems + second.sems

    def _each(self, ins, outs, sems):
        a = self.jobs[0]
        i, o, s = len(a.inputs), len(a.out_shape), len(a.sems)
        return ((a, ins[:i], outs[:o], sems[:s]), (self.jobs[1], ins[i:], outs[o:], sems[s:]))

    def before(self, ins, outs, sems):
        for job, i, o, s in self._each(ins, outs, sems):
            job.before(i, o, s)

    def after(self, ins, outs, sems):
        for job, i, o, s in self._each(ins, outs, sems):
            job.after(i, o, s)


_COLLECTIVE_IDS = {(1,): 0, (2, 4, 6): 1, (1, 2, 4, 6): 2}


def _entry_barrier(rels):
    @pl.when(pl.program_id(0) == 0)
    def _():
        me = _mesh_position()
        sem = pltpu.get_barrier_semaphore()
        for rel in rels:
            pl.semaphore_signal(sem, inc=1, device_id=_flip(me, rel), device_id_type=MESH)
        pl.semaphore_wait(sem, len(rels))


def _run(body, comm, *, semantics, out_shape, in_specs, out_specs, scratch_shapes=(), **kw):
    if comm is None:
        return pl.pallas_call(body, out_shape=out_shape, in_specs=in_specs, out_specs=out_specs,
                              scratch_shapes=list(scratch_shapes), compiler_params=_params(semantics), **kw)
    single = not isinstance(out_shape, (list, tuple))
    outs = [out_shape] if single else list(out_shape)
    ospecs = [out_specs] if single else list(out_specs)
    counts = [len(in_specs), len(comm.inputs), len(outs), len(comm.out_shape), len(scratch_shapes), len(comm.sems)]
    rels = tuple(sorted(comm.rels))

    def carrying(*refs):
        groups, pos = [], 0
        for c in counts:
            groups.append(refs[pos:pos + c])
            pos += c
        main_in, comm_in, main_out, comm_out, main_scratch, comm_sems = groups
        _entry_barrier(rels)
        comm.before(comm_in, comm_out, comm_sems)
        body(*main_in, *main_out, *main_scratch)
        comm.after(comm_in, comm_out, comm_sems)

    call = pl.pallas_call(
        carrying, out_shape=outs + list(comm.out_shape), in_specs=list(in_specs) + [_ANY] * len(comm.inputs),
        out_specs=ospecs + [_ANY] * len(comm.out_shape), scratch_shapes=list(scratch_shapes) + list(comm.sems),
        compiler_params=pltpu.CompilerParams(dimension_semantics=("arbitrary",), vmem_limit_bytes=VMEM_LIMIT,
                                             collective_id=_COLLECTIVE_IDS[rels]), **kw)

    def apply(*args):
        res = call(*args, *comm.inputs)
        main = res[:len(outs)]
        return (main[0] if single else list(main)), list(res[len(outs):])

    return apply


def _alone(comm, name):
    return _run(lambda: None, comm, semantics="arbitrary", name=name, grid=(1,), out_shape=[], in_specs=[], out_specs=[])()[1]


SHARDED = {"w_in": 1, "w_glu": 0, "conv_w": 1, "w_a_out": 1, "w_b_out": 0, "w_o": 0, "w_ffn_gate": 1, "w_ffn_up": 1,
           "w_ffn_down": 0, "w_ple_gate": 0, "w_ple": 1}
TRANSPOSED = ("w_in", "w_a_out", "w_ffn_gate", "w_ffn_up", "w_ple")
LONG_AXIS_MINOR = ("w_in", "w_ffn_gate", "w_ffn_up")
NARROW_LAST = ("s5_b_re", "s5_b_im", "s5_d")
ADAMW_GROUPS = (("w_in",), ("w_glu",), ("conv_w",), ("w_a_out",), ("w_b_out", "w_o", "w_ple_gate"),
                ("w_ffn_gate", "w_ffn_up"), ("w_ffn_down",), ("w_ple",))
SMALL = ["g_mix", "b_in", "lam_re", "lam_im", "log_dt", "s5_b_re", "s5_b_im", "s5_c_re", "s5_c_im", "s5_d", "b_glu",
         "conv_b", "w_r", "b_r", "w_i", "b_i", "lru_lambda", "g_ffn", "g_ple_gate", "b_ple_gate", "g_ple", "g_final"]
WEIGHTS = ["g_mix", "w_in", "b_in", "lam_re", "lam_im", "log_dt", "s5_b_re", "s5_b_im", "s5_c_re", "s5_c_im", "s5_d",
           "w_glu", "b_glu", "conv_w", "conv_b", "w_r", "b_r", "w_i", "b_i", "lru_lambda", "w_a_out", "w_b_out", "w_o",
           "g_ffn", "w_ffn_gate", "w_ffn_up", "w_ffn_down", "g_ple_gate", "w_ple_gate", "b_ple_gate", "w_ple", "g_ple",
           "g_final"]


def _join_columns(gathered):
    nb, r, c = gathered.shape
    return jnp.transpose(gathered, (1, 0, 2)).reshape(r, nb * c)


def _disc_scalars(lr, li, ldt):
    dt = jnp.exp(ldt)
    mag = jnp.exp(lr * dt)
    ar = mag * jnp.cos(li * dt)
    ai = mag * jnp.sin(li * dt)
    den = lr * lr + li * li
    nr = ar - 1.0
    fr = (nr * lr + ai * li) / den
    fi = (ai * lr - nr * li) / den
    return ar, ai, fr, fi


def _disc_cols(lr, li, ldt, b_re, b_im):
    ar, ai, fr, fi = _disc_scalars(lr, li, ldt)
    return ar, ai, fr * b_re - fi * b_im, fr * b_im + fi * b_re


def _local_step(x, p, target, src, small, disc, distributed=True):
    full = {} if distributed else dict(src)
    gw, halves, pairs, got = {}, {}, {}, {}

    def gather(keys):
        return (_Gather([src[k] for k in keys]), keys, full) if distributed else None

    def swap(keys):
        return (_PairSwap([gw[k] for k in keys]), keys, halves) if distributed else None

    def chips(keys):
        return (_ChipExchange([pairs[k] for k in keys]), keys, got) if distributed else None

    def add_pairs(keys):
        if distributed:
            pairs.update(zip(keys, _pair_add([gw[k] for k in keys], [halves[k] for k in keys], "pair_add_" + keys[0])))

    def carry(fn, *args, jobs=()):
        jobs = [j for j in jobs if j is not None]
        if not jobs:
            return fn(*args)
        comm = jobs[0][0]
        for j in jobs[1:]:
            comm = _Both(comm, j[0])
        res, extra = fn(*args, comm=comm)
        for job, keys, sink in jobs:
            sink.update(zip(keys, extra[:len(job.out_shape)]))
            extra = extra[len(job.out_shape):]
        return res

    d = x.shape[1]
    g, n, pch = small["s5_b_re"].shape
    sa, lw = g * pch, small["lru_lambda"].shape[-1]
    widths = [sa, lw, d, d]
    row = lambda v: v.reshape(1, -1)

    hd = small["w_r"].shape[-1]
    ar_row, ai_row, bbr_blk, bbi_blk, cre_blk, cimn_blk, wr_blk, wi_blk = carry(
        _prep, *disc["rows"], *disc["cols"], disc["b_re"], disc["b_im"], small["s5_c_re"].reshape(sa, n),
        small["s5_c_im"].reshape(sa, n), small["w_r"].reshape(lw, hd), small["w_i"].reshape(lw, hd),
        jobs=[gather(["w_in"])])
    d_row = row(small["s5_d"])
    g_mix_row = row(small["g_mix"])
    if distributed:
        later = ["w_glu", "conv_w", "w_a_out", "w_b_out", "w_ffn_gate", "w_ffn_up", "w_o", "w_ffn_down", "w_ple_gate",
                 "w_ple"]
        wires = [src[k] for k in later]
        flight, lands, token = _gather_start(wires, "gather_start", after=[ar_row])
        g_mix_row = g_mix_row + token[0, 0]

        def arrive(keys, after):
            which = [later.index(k) for k in keys]
            passed, moved = _gather_pass(wires, which, flight, lands, "gather_pass_" + keys[0], after)
            for q, i in enumerate(which):
                lands[i] = moved[q]
            full.update(zip(keys, _gather_finish(wires, which, flight, passed, lands, (), "gather_finish_" + keys[0])))
    else:
        arrive = lambda keys, after: None

    u_a, u_b, za, zb = _inproj_fwd(x, g_mix_row, full["w_in"], row(small["b_in"]), widths)
    arrive(["w_glu", "conv_w", "w_a_out", "w_b_out"], [u_a])
    conv_w = _join_columns(full["conv_w"]) if distributed else full["conv_w"]
    branches = [_s5_fwd(u_a, bbr_blk, bbi_blk, ar_row, ai_row, cre_blk, cimn_blk, d_row, full["w_glu"],
                        row(small["b_glu"])),
                _lru_fwd(u_b, conv_w, row(small["conv_b"]), wr_blk, row(small["b_r"]), wi_blk, row(small["b_i"]),
                         row(small["lru_lambda"]))]
    (sr, si, y, y_a), (xc, h, hprev) = _stages(branches, "branches_fwd")
    arrive(["w_ffn_gate", "w_ffn_up", "w_o"], [y_a])
    x1, merged, ma, mb, fg, fu = _merge_ffn_up_fwd(y_a, h, za, zb, x, full["w_a_out"], full["w_b_out"], full["w_o"],
                                                   row(small["g_ffn"]), full["w_ffn_gate"], full["w_ffn_up"])
    arrive(["w_ffn_down", "w_ple_gate", "w_ple"], [x1])
    x2 = _ffn_down_fwd(fg, fu, x1, full["w_ffn_down"])

    dx2, loss_blk, gw["w_ple_gate"], gw["w_ple"], vec_tail = _tail_fwd_bwd(
        x2, p, target, row(small["g_ple_gate"]), full["w_ple_gate"], row(small["b_ple_gate"]), full["w_ple"],
        row(small["g_ple"]), row(small["g_final"]))
    dfg, dfu, act = carry(_ffn_bwd_a, dx2, fg, fu, full["w_ffn_down"], jobs=[swap(["w_ple_gate", "w_ple"])])
    tn_d = min(d, 512)
    gw["w_ffn_down"] = _matmul_tn(act, dx2, tn_d, "dw_ffn_down", BF16)
    add_pairs(["w_ple_gate", "w_ple"])
    dx1, h2, vec_ffn = carry(_ffn_bwd_b, dfg, dfu, x1, dx2, row(small["g_ffn"]), full["w_ffn_gate"], full["w_ffn_up"],
                             jobs=[chips(["w_ple_gate", "w_ple"]), swap(["w_ffn_down"])])
    gw["w_ffn_gate"] = _matmul_tn(dfg, h2, tn_d, "dw_ffn_gate", BF16)
    gw["w_ffn_up"] = _matmul_tn(dfu, h2, tn_d, "dw_ffn_up", BF16)
    add_pairs(["w_ffn_down"])
    dza, dzb, dya, dyb, gw["w_o"], gw["w_a_out"], gw["w_b_out"] = carry(
        _merge_bwd, dx1, merged, ma, mb, za, zb, y_a, h, full["w_o"], full["w_a_out"], full["w_b_out"],
        jobs=[chips(["w_ffn_down"]), swap(["w_ffn_gate", "w_ffn_up"])])
    add_pairs(["w_ffn_gate", "w_ffn_up"])
    own = {}
    du_b, dw_r, dw_i, vec_lru = carry(
        _lru_bwd, dyb, xc, hprev, u_b, conv_w, wr_blk, row(small["b_r"]), wi_blk, row(small["b_i"]),
        row(small["lru_lambda"]), hd, jobs=[chips(["w_ffn_gate"]), swap(["w_o", "w_a_out", "w_b_out"])])
    add_pairs(["w_o", "w_a_out", "w_b_out"])
    du_a, lam_r, lam_i, dy16, gw["w_glu"], vec_sa, vec_gn = carry(
        _s5_bwd, dya, y, sr, si, u_a, full["w_glu"], row(small["b_glu"]), cre_blk, cimn_blk, bbr_blk, bbi_blk, ar_row,
        ai_row, d_row, jobs=[chips(["w_ffn_up"]), chips(["w_o", "w_a_out", "w_b_out"])])
    smalls = {"vec_tail": vec_tail, "vec_ffn": vec_ffn, "vec_lru": vec_lru, "vec_sa": vec_sa, "vec_gn": vec_gn,
              "dw_r": dw_r, "dw_i": dw_i, "loss": loss_blk}
    everyones = {}

    def gather_smalls(keys):
        return (_Gather([smalls[k] for k in keys]), keys, everyones) if distributed else None

    smalls["dbb_re"], smalls["dbb_im"], smalls["dc_re"], smalls["dc_imn"] = carry(
        _s5_param_grads, lam_r, lam_i, sr, si, u_a, dy16, pch, n, jobs=[swap(["w_glu"]), gather_smalls(list(smalls))])
    add_pairs(["w_glu"])
    dz = [du_a, du_b, dza, dzb]
    grad_x, h0, smalls["vec_mix"], smalls["vec_bin"] = _inproj_bwd(
        dz, x, dx1, row(small["g_mix"]), full["w_in"])
    shapes = {k: a.shape for k, a in smalls.items()}
    gw["w_in"] = carry(_dw_from_parts, dz, h0, tn_d, "dw_in",
                       jobs=[chips(["w_glu"]),
                             gather_smalls(["dbb_re", "dbb_im", "dc_re", "dc_imn", "vec_mix", "vec_bin"])])
    smalls.update(everyones)
    if distributed:
        got["w_in"] = gw["w_in"]
        gw = got
    return grad_x, gw, smalls, shapes, own


def _disc_inputs(small):
    g, n, pch = small["s5_b_re"].shape
    srcs = (small["lam_re"], small["lam_im"], jnp.repeat(small["log_dt"], n))
    return {"rows": [a.reshape(1, g * n) for a in srcs], "cols": [a.reshape(g * n, 1) for a in srcs],
            "b_re": small["s5_b_re"].reshape(g * n, pch), "b_im": small["s5_b_im"].reshape(g * n, pch)}


def kernel(x, p, g_mix, w_in, b_in, lam_re, lam_im, log_dt, s5_b_re, s5_b_im, s5_c_re, s5_c_im, s5_d, w_glu, b_glu, conv_w, conv_b, w_r, b_r, w_i, b_i, lru_lambda, w_a_out, w_b_out, w_o, g_ffn, w_ffn_gate, w_ffn_up, w_ffn_down, g_ple_gate, w_ple_gate, b_ple_gate, w_ple, g_ple, g_final, loss_target, m_g_mix, m_w_in, m_b_in, m_lam_re, m_lam_im, m_log_dt, m_s5_b_re, m_s5_b_im, m_s5_c_re, m_s5_c_im, m_s5_d, m_w_glu, m_b_glu, m_conv_w, m_conv_b, m_w_r, m_b_r, m_w_i, m_b_i, m_lru_lambda, m_w_a_out, m_w_b_out, m_w_o, m_g_ffn, m_w_ffn_gate, m_w_ffn_up, m_w_ffn_down, m_g_ple_gate, m_w_ple_gate, m_b_ple_gate, m_w_ple, m_g_ple, m_g_final, v_g_mix, v_w_in, v_b_in, v_lam_re, v_lam_im, v_log_dt, v_s5_b_re, v_s5_b_im, v_s5_c_re, v_s5_c_im, v_s5_d, v_w_glu, v_b_glu, v_conv_w, v_conv_b, v_w_r, v_b_r, v_w_i, v_b_i, v_lru_lambda, v_w_a_out, v_w_b_out, v_w_o, v_g_ffn, v_w_ffn_gate, v_w_ffn_up, v_w_ffn_down, v_g_ple_gate, v_w_ple_gate, v_b_ple_gate, v_w_ple, v_g_ple, v_g_final):
    given = dict(locals())
    wts = {k: given[k] for k in WEIGHTS}
    moms = {k: given["m_" + k] for k in WEIGHTS}
    vels = {k: given["v_" + k] for k in WEIGHTS}

    def drop_depth(k, a):
        return a if k == "g_final" else a[0]

    small = {k: drop_depth(k, wts[k]) for k in SMALL}
    shard = {k: wts[k][0] for k in SHARDED}
    names = list(SHARDED)

    def wire(k):
        if k == "conv_w":
            return shard[k]
        return (shard[k].T if k in TRANSPOSED else shard[k]).astype(BF16)

    disc = _disc_inputs(small)
    grad_x, parts, smalls, shapes, own = _local_step(x[0], p[0, 0], loss_target[0], {k: wire(k) for k in names}, small,
                                                     disc)

    (pair_in,) = _pair_add([parts["w_in"]], _alone(_PairSwap([parts["w_in"]]), "swap_last"), "pair_add_w_in")
    send_sems, recv_sems, pair_in, landing, token = _chips_start(pair_in, "w_in")
    ordered = (token,)

    g_small = _small_reduce(smalls, shapes, *disc["cols"], disc["b_re"], disc["b_im"], small["s5_b_re"].shape[0],
                            after=ordered)
    loss = g_small.pop("loss")[0, 0]
    cols = shard["conv_w"].shape[1]
    mine = _index((lax.axis_index("x"), lax.axis_index("y"), lax.axis_index("c")))
    parts["conv_w"] = lax.dynamic_slice_in_dim(g_small.pop("conv_w"), mine * cols, cols, axis=1)[None]

    def view(k, a):
        a = a.reshape((1, -1) if k == "g_final" else wts[k].shape)
        return jnp.swapaxes(a, -1, -2) if k in NARROW_LAST else a

    def unview(k, a):
        return (jnp.swapaxes(a, -1, -2) if k in NARROW_LAST else a).reshape(wts[k].shape)

    slots = _adamw_small([view(k, wts[k]) for k in SMALL], [view(k, g_small[k]) for k in SMALL],
                         [view(k, moms[k]) for k in SMALL], [view(k, vels[k]) for k in SMALL])
    small_out = [dict(zip(SMALL, [unview(k, a) for k, a in zip(SMALL, slot)])) for slot in slots]

    big_out = {}
    between = [slots[0][0]]
    for group in sorted(ADAMW_GROUPS, key=lambda g: g[0] == "w_in"):
        flip = group[0] in LONG_AXIS_MINOR
        look = (lambda a: a.T) if flip else (lambda a: a)
        last = group[0] == "w_in"
        if last:
            own["w_in"], parts["w_in"] = _chips_wait(send_sems, recv_sems, pair_in, landing, between, "w_in")
        res = _adamw([parts[k] for k in group], [look(shard[k]) for k in group], [look(moms[k][0]) for k in group],
                     [look(vels[k][0]) for k in group], "adamw_" + group[0],
                     transposed=group[0] in TRANSPOSED and not flip,
                     own=[own[k] for k in group] if group[0] in own else None, after=() if last else ordered)
        between.append(res[0][0])
        for k, outs4 in zip(group, res):
            big_out[k] = [look(a) for a in outs4]

    outs = [loss, grad_x[None]]
    for slot in range(4):
        for k in WEIGHTS:
            if k in SHARDED:
                outs.append(big_out[k][slot][None])
            else:
                outs.append(small_out[slot][k])
    return tuple(outs)
```

```python
import math

import jax
import jax.numpy as jnp
from jax import lax
from jax.experimental import pallas as pl
from jax.experimental.pallas import tpu as pltpu

F32 = jnp.float32
BF16 = jnp.bfloat16

EPS = 1e-6
LRU_C = 8.0
CONV_WIDTH = 4
ADAM_LR = 0.001
ADAM_B1 = 0.9
ADAM_B2 = 0.999
ADAM_EPS = 1e-08
ADAM_WD = 0.01
ADAM_STEP = 10

N_DEV = 8
MESH = pl.DeviceIdType.MESH
SUBLANES = 8
LANES = 128
VMEM_LIMIT = 56 * 1024 * 1024
TOKEN_TILE = 256
TIME_CHUNK = 256
S5_SLAB = 128
LRU_SLAB = 256


def _dot(a, b):
    return jnp.dot(a.astype(BF16), b.astype(BF16), preferred_element_type=F32)


def _dot_nt(a, b):
    return lax.dot_general(a.astype(BF16), b.astype(BF16), (((1,), (1,)), ((), ())), preferred_element_type=F32)


def _dot_tn(a, b):
    return lax.dot_general(a.astype(BF16), b.astype(BF16), (((0,), (0,)), ((), ())), preferred_element_type=F32)


def _sigmoid(x):
    return jax.nn.sigmoid(x)


def _rms_stats(x):
    r = lax.rsqrt(jnp.mean(x * x, axis=-1, keepdims=True) + EPS)
    return x * r, r


def _rms_bwd(dy, xhat, r, g):
    dxn = dy * g
    dx = r * (dxn - xhat * jnp.mean(dxn * xhat, axis=-1, keepdims=True))
    return dx, dy * xhat


def _rowsum(v):
    return jnp.sum(v, axis=0, keepdims=True)


def _expm1(x):
    u = jnp.exp(x)
    um1 = u - 1.0
    safe = jnp.where(um1 == 0.0, 1.0, jnp.log(u))
    return jnp.where(um1 == 0.0, x, um1 * x / safe)


def _softplus(x):
    e = jnp.exp(-jnp.abs(x))
    u = 1.0 + e
    um1 = u - 1.0
    safe = jnp.where(um1 == 0.0, 1.0, um1)
    log1p_e = jnp.where(um1 == 0.0, e, jnp.log(u) * e / safe)
    return jnp.maximum(x, 0.0) + log1p_e


_GELU_K = math.sqrt(2.0 / math.pi)
_GELU_C = 0.044715


def _gelu(x):
    return 0.5 * x * (1.0 + jnp.tanh(_GELU_K * (x + _GELU_C * x * x * x)))


def _gelu_grad(x):
    th = jnp.tanh(_GELU_K * (x + _GELU_C * x * x * x))
    return 0.5 * (1.0 + th) + 0.5 * x * (1.0 - th * th) * _GELU_K * (1.0 + 3.0 * _GELU_C * x * x)


def _params(*sem):
    return pltpu.CompilerParams(dimension_semantics=sem, vmem_limit_bytes=VMEM_LIMIT)


def _rows(tm, n):
    return pl.BlockSpec((tm, n), lambda i: (i, 0))


def _rows_rev(tm, n, steps):
    return pl.BlockSpec((tm, n), lambda i: (steps - 1 - i, 0))


def _whole(shape):
    nd = len(shape)
    return pl.BlockSpec(shape, lambda i: (0,) * nd, pipeline_mode=pl.Buffered(1))


def _acc(shape):
    nd = len(shape)
    return pl.BlockSpec(shape, lambda i: (0,) * nd)


def _zero_on_first(*refs):
    @pl.when(pl.program_id(0) == 0)
    def _():
        for r in refs:
            r[...] = jnp.zeros_like(r)


def _inproj_fwd(x, g_mix, w_in_t, b_in, widths, comm=None):
    t, d = x.shape
    n = w_in_t.shape[0]
    tm = min(TOKEN_TILE, t)
    offs = [sum(widths[:i]) for i in range(len(widths) + 1)]

    def body(x_ref, g_ref, w_ref, b_ref, *outs):
        xhat, _ = _rms_stats(x_ref[...])
        h = (xhat * g_ref[...]).astype(BF16)
        for k, o_ref in enumerate(outs):
            lo, hi = offs[k], offs[k + 1]
            o_ref[...] = _dot_nt(h, w_ref[lo:hi, :]) + b_ref[:, lo:hi]

    return _run(
        body, comm, name="inproj_fwd", grid=(t // tm,),
        out_shape=[jax.ShapeDtypeStruct((t, w), F32) for w in widths],
        in_specs=[_rows(tm, d), _whole((1, d)), _whole((n, d)), _whole((1, n))],
        out_specs=[_rows(tm, w) for w in widths],
        semantics="parallel",
    )(x, g_mix, w_in_t, b_in)


def _s5_fwd(u, bbr_blk, bbi_blk, ar, ai, cre_blk, cimn_blk, d_skip, w_glu, b_glu):
    t, sa = u.shape
    ns, _, sw = bbr_blk.shape
    gn = ns * sw
    tc = min(TIME_CHUNK, t)

    def body(u_ref, bbr_ref, bbi_ref, ar_ref, ai_ref, cre_ref, cim_ref, d_ref, wg_ref, bg_ref,
             sr_ref, si_ref, y_ref, ya_ref, cr_s, ci_s, sr_s, si_s):
        _zero_on_first(cr_s, ci_s)
        uv = u_ref[...]
        ub = uv.astype(BF16)
        for m in range(ns):
            um = ub[:, m * S5_SLAB:(m + 1) * S5_SLAB]
            sr_s[:, m * sw:(m + 1) * sw] = _dot(um, bbr_ref[m])
            si_s[:, m * sw:(m + 1) * sw] = _dot(um, bbi_ref[m])
        a_r = ar_ref[...]
        a_i = ai_ref[...]

        def step(row, carry):
            c_r, c_i = carry
            at = pl.ds(row, 1)
            n_r = a_r * c_r - a_i * c_i + sr_s[at, :]
            n_i = a_r * c_i + a_i * c_r + si_s[at, :]
            sr_s[at, :] = n_r
            si_s[at, :] = n_i
            return n_r, n_i

        c_r, c_i = lax.fori_loop(0, tc, step, (cr_s[0:1, :], ci_s[0:1, :]), unroll=8)
        cr_s[0:1, :] = c_r
        ci_s[0:1, :] = c_i
        for m in range(ns):
            states, chans = slice(m * sw, (m + 1) * sw), slice(m * S5_SLAB, (m + 1) * S5_SLAB)
            s_r, s_i = sr_s[:, states].astype(BF16), si_s[:, states].astype(BF16)
            sr_ref[:, states] = s_r.astype(sr_ref.dtype)
            si_ref[:, states] = s_i.astype(si_ref.dtype)
            y_ref[:, chans] = _dot(s_r, cre_ref[m]) + _dot(s_i, cim_ref[m]) + d_ref[:, chans] * uv[:, chans]
        y = y_ref[...]
        zz = _gelu(y)
        q = _dot(zz, wg_ref[...]) + bg_ref[...]
        ya_ref[...] = zz * _sigmoid(q)

    return dict(
        body=body, steps=t // tc, args=(u, bbr_blk, bbi_blk, ar, ai, cre_blk, cimn_blk, d_skip, w_glu, b_glu),
        out_shape=[jax.ShapeDtypeStruct((t, gn), BF16), jax.ShapeDtypeStruct((t, gn), BF16),
                   jax.ShapeDtypeStruct((t, sa), F32), jax.ShapeDtypeStruct((t, sa), F32)],
        in_specs=[_rows(tc, sa), _whole(bbr_blk.shape), _whole(bbi_blk.shape), _whole((1, gn)), _whole((1, gn)),
                  _whole(cre_blk.shape), _whole(cimn_blk.shape), _whole((1, sa)), _whole((sa, sa)), _whole((1, sa))],
        out_specs=[_rows(tc, gn), _rows(tc, gn), _rows(tc, sa), _rows(tc, sa)],
        scratch=[pltpu.VMEM((SUBLANES, gn), F32), pltpu.VMEM((SUBLANES, gn), F32),
                 pltpu.VMEM((tc, gn), F32), pltpu.VMEM((tc, gn), F32)])


def _stages(stages, name, comm=None):
    counts = [[len(s[k]) for s in stages] for k in ("args", "out_shape", "scratch")]

    def body(*refs):
        groups, pos = [], 0
        for kind in counts:
            per_stage = []
            for c in kind:
                per_stage.append(refs[pos:pos + c])
                pos += c
            groups.append(per_stage)
        for s, ins, outs, scratch in zip(stages, *groups):
            s["body"](*ins, *outs, *scratch)

    res = _run(
        body, comm, name=name, grid=(stages[0]["steps"],),
        out_shape=[o for s in stages for o in s["out_shape"]], in_specs=[i for s in stages for i in s["in_specs"]],
        out_specs=[o for s in stages for o in s["out_specs"]], scratch_shapes=[c for s in stages for c in s["scratch"]],
        semantics="arbitrary",
    )(*[a for s in stages for a in s["args"]])
    extra = None
    if comm is not None:
        res, extra = res
    per_stage, pos = [], 0
    for c in counts[1]:
        per_stage.append(list(res[pos:pos + c]))
        pos += c
    return per_stage if comm is None else (per_stage, extra)


def _slab_dot(x, w_ref, transposed=False):
    dot = _dot_nt if transposed else _dot
    xb = x.astype(BF16)
    return jnp.concatenate([dot(xb[:, j * LRU_SLAB:(j + 1) * LRU_SLAB], w_ref[j]) for j in range(w_ref.shape[0])], axis=1)


def _lru_gates(xc, wr_ref, br_ref, wi_ref, bi_ref, lam_ref):
    r = _sigmoid(_slab_dot(xc, wr_ref) + br_ref[...])
    ig = _sigmoid(_slab_dot(xc, wi_ref) + bi_ref[...])
    sp = _softplus(-lam_ref[...])
    log_a = (-LRU_C * r) * sp
    return r, ig, sp, log_a


def _lru_fwd(u, conv_w, conv_b, wr_blk, b_r, wi_blk, b_i, lru_lambda):
    t, w = u.shape
    tc = min(TIME_CHUNK, t)
    halo = SUBLANES

    def body(u_ref, cw_ref, cb_ref, wr_ref, br_ref, wi_ref, bi_ref, lam_ref,
             xc_ref, h_ref, hp_ref, ext_s, a_s, carry_s):
        @pl.when(pl.program_id(0) == 0)
        def _():
            ext_s[0:halo, :] = jnp.zeros((halo, w), F32)
            carry_s[...] = jnp.zeros_like(carry_s)

        ext_s[halo:halo + tc, :] = u_ref[...]
        xc = cb_ref[...]
        for k in range(CONV_WIDTH):
            off = halo - (CONV_WIDTH - 1) + k
            xc = xc + cw_ref[k:k + 1, :] * ext_s[off:off + tc, :]
        ext_s[0:halo, :] = ext_s[tc:tc + halo, :]
        xc_ref[...] = xc
        r, ig, sp, log_a = _lru_gates(xc, wr_ref, br_ref, wi_ref, bi_ref, lam_ref)
        a_s[...] = jnp.exp(log_a)
        h_ref[...] = jnp.sqrt(-_expm1(2.0 * log_a)) * ig * xc

        def step(row, carry):
            at = pl.ds(row, 1)
            hp_ref[at, :] = carry
            nxt = a_s[at, :] * carry + h_ref[at, :]
            h_ref[at, :] = nxt
            return nxt

        carry_s[0:1, :] = lax.fori_loop(0, tc, step, carry_s[0:1, :], unroll=8)

    return dict(
        body=body, steps=t // tc, args=(u, conv_w, conv_b, wr_blk, b_r, wi_blk, b_i, lru_lambda),
        out_shape=[jax.ShapeDtypeStruct((t, w), F32)] * 3,
        in_specs=[_rows(tc, w), _whole((CONV_WIDTH, w)), _whole((1, w)), _whole(wr_blk.shape), _whole((1, w)),
                  _whole(wi_blk.shape), _whole((1, w)), _whole((1, w))],
        out_specs=[_rows(tc, w)] * 3,
        scratch=[pltpu.VMEM((halo + tc, w), F32), pltpu.VMEM((tc, w), F32), pltpu.VMEM((SUBLANES, w), F32)])


def _merge_ffn_up_fwd(y_a, h, za, zb, x, w_a_out_t, w_b_out, w_o, g_ffn, w_gate_t, w_up_t, comm=None):
    t, d = x.shape
    sa, lw = y_a.shape[1], h.shape[1]
    f = w_gate_t.shape[0]
    tm = min(TOKEN_TILE, t)

    def body(ya_ref, h_ref, za_ref, zb_ref, x_ref, wa_ref, wb_ref, wo_ref, g_ref, wg_ref, wu_ref,
             x1_ref, mg_ref, ma_ref, mb_ref, fg_ref, fu_ref):
        ma = _dot_nt(ya_ref[...], wa_ref[...])
        mb = _dot(h_ref[...], wb_ref[...])
        merged = _sigmoid(za_ref[...]) * ma + _sigmoid(zb_ref[...]) * mb
        ma_ref[...] = ma.astype(ma_ref.dtype)
        mb_ref[...] = mb.astype(mb_ref.dtype)
        mg_ref[...] = merged.astype(mg_ref.dtype)
        x1 = x_ref[...] + _dot(merged, wo_ref[...])
        x1_ref[...] = x1
        xhat, _ = _rms_stats(x1)
        h2 = (xhat * g_ref[...]).astype(BF16)
        fg_ref[...] = _dot_nt(h2, wg_ref[...]).astype(fg_ref.dtype)
        fu_ref[...] = _dot_nt(h2, wu_ref[...]).astype(fu_ref.dtype)

    return _run(
        body, comm, name="merge_ffn_up_fwd", grid=(t // tm,),
        out_shape=[jax.ShapeDtypeStruct((t, d), F32), jax.ShapeDtypeStruct((t, d), BF16),
                   jax.ShapeDtypeStruct((t, d), BF16), jax.ShapeDtypeStruct((t, d), BF16),
                   jax.ShapeDtypeStruct((t, f), BF16), jax.ShapeDtypeStruct((t, f), BF16)],
        in_specs=[_rows(tm, sa), _rows(tm, lw), _rows(tm, d), _rows(tm, d), _rows(tm, d),
                  _whole((d, sa)), _whole((lw, d)), _whole((d, d)), _whole((1, d)), _whole((f, d)), _whole((f, d))],
        out_specs=[_rows(tm, d)] * 4 + [_rows(tm, f)] * 2,
        semantics="parallel",
    )(y_a, h, za, zb, x, w_a_out_t, w_b_out, w_o, g_ffn, w_gate_t, w_up_t)


def _ffn_down_fwd(fg, fu, x1, w_down, comm=None):
    t, d = x1.shape
    f = fg.shape[1]
    tm = min(TOKEN_TILE, t)

    def body(fg_ref, fu_ref, x_ref, wd_ref, x2_ref):
        fgv = fg_ref[...].astype(F32)
        act = fgv * _sigmoid(fgv) * fu_ref[...].astype(F32)
        x2_ref[...] = x_ref[...] + _dot(act, wd_ref[...])

    return _run(
        body, comm, name="ffn_down_fwd", grid=(t // tm,),
        out_shape=jax.ShapeDtypeStruct((t, d), F32),
        in_specs=[_rows(tm, f), _rows(tm, f), _rows(tm, d), _whole((f, d))],
        out_specs=_rows(tm, d),
        semantics="parallel",
    )(fg, fu, x1, w_down)


def _store_on_last(pairs):
    @pl.when(pl.program_id(0) == pl.num_programs(0) - 1)
    def _():
        for acc, out in pairs:
            out[...] = acc[...].astype(out.dtype)


def _tail_fwd_bwd(x2, p, target, g_pg, w_pg, b_pg, w_ple_t, g_ple, g_final):
    t, d = x2.shape
    pd = p.shape[1]
    tm = min(TOKEN_TILE, t)

    def body(x2_ref, p_ref, tg_ref, gpg_ref, wpg_ref, bpg_ref, wple_ref, gple_ref, gfin_ref,
             dx2_ref, loss_ref, dwpg_out, dwple_out, vec_ref, dwpg_ref, dwple_ref):
        _zero_on_first(loss_ref, dwpg_ref, dwple_ref, vec_ref)
        x2v = x2_ref[...]
        xh2, r2 = _rms_stats(x2v)
        h3 = xh2 * gpg_ref[...]
        gp = _sigmoid(_dot(h3, wpg_ref[...]) + bpg_ref[...])
        pe = _dot_nt(p_ref[...], wple_ref[...])
        peh, r3 = _rms_stats(pe)
        e = peh * gple_ref[...]
        x3 = x2v + gp * e
        xh3, r4 = _rms_stats(x3)
        diff = xh3 * gfin_ref[...] - tg_ref[...]
        loss_ref[...] += 0.5 * jnp.sum(jnp.mean(diff * diff, axis=-1, keepdims=True))
        dy = diff * (1.0 / d)
        dx3, dgfin = _rms_bwd(dy, xh3, r4, gfin_ref[...])
        d_gp = dx3 * e
        d_e = dx3 * gp
        dpe, dgple = _rms_bwd(d_e, peh, r3, gple_ref[...])
        dwple_ref[...] += _dot_tn(dpe, p_ref[...])
        dpre = d_gp * gp * (1.0 - gp)
        dwpg_ref[...] += _dot_tn(h3, dpre)
        dh3 = _dot_nt(dpre, wpg_ref[...])
        dx2n, dgpg = _rms_bwd(dh3, xh2, r2, gpg_ref[...])
        dx2_ref[...] = dx3 + dx2n
        vec_ref[0:1, :] += _rowsum(dpre)
        vec_ref[1:2, :] += _rowsum(dgpg)
        vec_ref[2:3, :] += _rowsum(dgple)
        vec_ref[3:4, :] += _rowsum(dgfin)
        _store_on_last([(dwpg_ref, dwpg_out), (dwple_ref, dwple_out)])

    return pl.pallas_call(
        body, name="tail_fwd_bwd", grid=(t // tm,),
        out_shape=[jax.ShapeDtypeStruct((t, d), F32), jax.ShapeDtypeStruct((SUBLANES, LANES), F32),
                   jax.ShapeDtypeStruct((d, d), BF16), jax.ShapeDtypeStruct((d, pd), BF16),
                   jax.ShapeDtypeStruct((SUBLANES, d), F32)],
        in_specs=[_rows(tm, d), _rows(tm, pd), _rows(tm, d), _whole((1, d)), _whole((d, d)), _whole((1, d)),
                  _whole((d, pd)), _whole((1, d)), _whole((1, d))],
        out_specs=[_rows(tm, d), _acc((SUBLANES, LANES)), _acc((d, d)), _acc((d, pd)), _acc((SUBLANES, d))],
        scratch_shapes=[pltpu.VMEM((d, d), F32), pltpu.VMEM((d, pd), F32)],
        compiler_params=_params("arbitrary"),
    )(x2, p, target, g_pg, w_pg, b_pg, w_ple_t, g_ple, g_final)


def _ffn_bwd_a(dx2, fg, fu, w_down, comm=None):
    t, d = dx2.shape
    f = fg.shape[1]
    tm = min(TOKEN_TILE, t)

    def body(dx_ref, fg_ref, fu_ref, wd_ref, dfg_ref, dfu_ref, act_ref):
        dact = _dot_nt(dx_ref[...], wd_ref[...])
        fgv = fg_ref[...].astype(F32)
        fuv = fu_ref[...].astype(F32)
        sg = _sigmoid(fgv)
        silu = fgv * sg
        dfu_ref[...] = (dact * silu).astype(dfu_ref.dtype)
        dfg_ref[...] = (dact * fuv * (sg * (1.0 + fgv * (1.0 - sg)))).astype(dfg_ref.dtype)
        act_ref[...] = (silu * fuv).astype(act_ref.dtype)

    return _run(
        body, comm, name="ffn_bwd_a", grid=(t // tm,),
        out_shape=[jax.ShapeDtypeStruct((t, f), BF16)] * 3,
        in_specs=[_rows(tm, d), _rows(tm, f), _rows(tm, f), _whole((f, d))],
        out_specs=[_rows(tm, f)] * 3,
        semantics="parallel",
    )(dx2, fg, fu, w_down)


def _ffn_bwd_b(dfg, dfu, x1, dx2, g_ffn, w_gate_t, w_up_t, comm=None):
    t, d = x1.shape
    f = dfg.shape[1]
    tm = min(TOKEN_TILE, t)

    def body(dfg_ref, dfu_ref, x_ref, dx2_ref, g_ref, wg_ref, wu_ref, dx1_ref, h2_ref, vec_ref):
        _zero_on_first(vec_ref)
        dh2 = _dot(dfg_ref[...], wg_ref[...]) + _dot(dfu_ref[...], wu_ref[...])
        xhat, r = _rms_stats(x_ref[...])
        h2_ref[...] = (xhat * g_ref[...]).astype(h2_ref.dtype)
        dxn, dg = _rms_bwd(dh2, xhat, r, g_ref[...])
        dx1_ref[...] = dx2_ref[...] + dxn
        vec_ref[0:1, :] += _rowsum(dg)

    return _run(
        body, comm, name="ffn_bwd_b", grid=(t // tm,),
        out_shape=[jax.ShapeDtypeStruct((t, d), F32), jax.ShapeDtypeStruct((t, d), BF16),
                   jax.ShapeDtypeStruct((SUBLANES, d), F32)],
        in_specs=[_rows(tm, f), _rows(tm, f), _rows(tm, d), _rows(tm, d), _whole((1, d)), _whole((f, d)), _whole((f, d))],
        out_specs=[_rows(tm, d), _rows(tm, d), _acc((SUBLANES, d))],
        semantics="arbitrary",
    )(dfg, dfu, x1, dx2, g_ffn, w_gate_t, w_up_t)


def _matmul_tn(a, b, tn, name, dtype=F32):
    t, k = a.shape
    n = b.shape[1]

    def body(a_ref, b_ref, o_ref):
        o_ref[...] = _dot_tn(a_ref[...], b_ref[...]).astype(o_ref.dtype)

    return _run(
        body, None, name=name, grid=(n // tn,),
        out_shape=jax.ShapeDtypeStruct((k, n), dtype),
        in_specs=[_whole((t, k)), pl.BlockSpec((t, tn), lambda j: (0, j))],
        out_specs=pl.BlockSpec((k, tn), lambda j: (0, j)),
        semantics="parallel",
    )(a, b)


def _merge_bwd(dx1, merged, ma, mb, za, zb, y_a, h, w_o, w_a_out_t, w_b_out, comm=None):
    t, d = dx1.shape
    sa, lw = y_a.shape[1], h.shape[1]
    tm = min(TOKEN_TILE, t)

    def body(dx1_ref, mg_ref, ma_ref, mb_ref, za_ref, zb_ref, ya_ref, h_ref, wo_ref, wa_ref, wb_ref,
             dza_ref, dzb_ref, dya_ref, dyb_ref, dwo_out, dwa_out, dwb_out, dwo_ref, dwa_ref, dwb_ref):
        _zero_on_first(dwo_ref, dwa_ref, dwb_ref)
        dx1v = dx1_ref[...].astype(BF16)
        dmg = _dot_nt(dx1v, wo_ref[...])
        ga = _sigmoid(za_ref[...])
        gb = _sigmoid(zb_ref[...])
        dza_ref[...] = (dmg * ma_ref[...].astype(F32) * ga * (1.0 - ga)).astype(dza_ref.dtype)
        dzb_ref[...] = (dmg * mb_ref[...].astype(F32) * gb * (1.0 - gb)).astype(dzb_ref.dtype)
        dma = (dmg * ga).astype(BF16)
        dmb = (dmg * gb).astype(BF16)
        dya_ref[...] = _dot(dma, wa_ref[...])
        dyb_ref[...] = _dot_nt(dmb, wb_ref[...])
        dwo_ref[...] += _dot_tn(mg_ref[...], dx1v)
        dwa_ref[...] += _dot_tn(dma, ya_ref[...])
        dwb_ref[...] += _dot_tn(h_ref[...], dmb)
        _store_on_last([(dwo_ref, dwo_out), (dwa_ref, dwa_out), (dwb_ref, dwb_out)])

    return _run(
        body, comm, name="merge_bwd", grid=(t // tm,),
        out_shape=[jax.ShapeDtypeStruct((t, d), BF16), jax.ShapeDtypeStruct((t, d), BF16),
                   jax.ShapeDtypeStruct((t, sa), F32), jax.ShapeDtypeStruct((t, lw), F32),
                   jax.ShapeDtypeStruct((d, d), BF16), jax.ShapeDtypeStruct((d, sa), BF16),
                   jax.ShapeDtypeStruct((lw, d), BF16)],
        in_specs=[_rows(tm, d), _rows(tm, d), _rows(tm, d), _rows(tm, d), _rows(tm, d), _rows(tm, d),
                  _rows(tm, sa), _rows(tm, lw), _whole((d, d)), _whole((d, sa)), _whole((lw, d))],
        out_specs=[_rows(tm, d), _rows(tm, d), _rows(tm, sa), _rows(tm, lw), _acc((d, d)), _acc((d, sa)), _acc((lw, d))],
        scratch_shapes=[pltpu.VMEM((d, d), F32), pltpu.VMEM((d, sa), F32), pltpu.VMEM((lw, d), F32)],
        semantics="arbitrary",
    )(dx1, merged, ma, mb, za, zb, y_a, h, w_o, w_a_out_t, w_b_out)


def _fold_diag_blocks(dense, row_group, col_group):
    r, c = dense.shape
    rows = lax.broadcasted_iota(jnp.int32, (r, c), 0)
    cols = lax.broadcasted_iota(jnp.int32, (r, c), 1)
    kept = jnp.where(rows // row_group == cols // col_group, dense, 0.0)
    pick = (lax.broadcasted_iota(jnp.int32, (row_group, r), 0)
            == lax.broadcasted_iota(jnp.int32, (row_group, r), 1) % row_group).astype(F32)
    return jnp.dot(pick, kept, preferred_element_type=F32, precision=lax.Precision.HIGHEST)


def _lru_bwd(dh, xc, hprev, u, conv_w, wr_blk, b_r, wi_blk, b_i, lru_lambda, head_dim, comm=None):
    t, w = dh.shape
    tc = min(TIME_CHUNK, t)
    steps = t // tc
    halo = SUBLANES
    sub_per_chunk = tc // halo
    slabs = w // LRU_SLAB

    def body(dh_ref, xc_ref, hp_ref, u_ref, uh_ref, cw_ref, wr_ref, br_ref, wi_ref, bi_ref, lam_ref,
             du_ref, dwr_out, dwi_out, vec_ref, lam_s, a_s, dxc_s, uext_s, carry_s, dwr_ref, dwi_ref):
        chunk = steps - 1 - pl.program_id(0)

        @pl.when(pl.program_id(0) == 0)
        def _():
            carry_s[...] = jnp.zeros_like(carry_s)
            dxc_s[tc:tc + halo, :] = jnp.zeros((halo, w), F32)
            dwr_ref[...] = jnp.zeros_like(dwr_ref)
            dwi_ref[...] = jnp.zeros_like(dwi_ref)
            vec_ref[...] = jnp.zeros_like(vec_ref)

        xc = xc_ref[...]
        r, ig, sp, log_a = _lru_gates(xc, wr_ref, br_ref, wi_ref, bi_ref, lam_ref)
        a = jnp.exp(log_a)
        a_s[...] = a

        def step(i, q):
            at = pl.ds(tc - 1 - i, 1)
            lam_row = dh_ref[at, :] + q
            lam_s[at, :] = lam_row
            return a_s[at, :] * lam_row

        carry_s[0:1, :] = lax.fori_loop(0, tc, step, carry_s[0:1, :], unroll=8)
        lam = lam_s[...]
        mult = jnp.sqrt(-_expm1(2.0 * log_a))
        d_log_a = lam * hp_ref[...] * a - (lam * ig * xc) * (a * a) / mult
        d_ig = lam * mult * xc
        dpre_r = (d_log_a * (-LRU_C * sp)) * r * (1.0 - r)
        dpre_i = d_ig * ig * (1.0 - ig)
        dxc = lam * mult * ig + _slab_dot(dpre_r, wr_ref, transposed=True) + _slab_dot(dpre_i, wi_ref, transposed=True)
        xcb, drb, dib = xc.astype(BF16), dpre_r.astype(BF16), dpre_i.astype(BF16)
        for j in range(slabs):
            cols = slice(j * LRU_SLAB, (j + 1) * LRU_SLAB)
            dwr_ref[j] += _dot_tn(drb[:, cols], xcb[:, cols])
            dwi_ref[j] += _dot_tn(dib[:, cols], xcb[:, cols])
        vec_ref[0:1, :] += _rowsum(dxc)
        vec_ref[1:2, :] += _rowsum(dpre_r)
        vec_ref[2:3, :] += _rowsum(dpre_i)
        vec_ref[3:4, :] += _rowsum(d_log_a * (-LRU_C * r)) * (-_sigmoid(-lam_ref[...]))
        dxc_s[0:tc, :] = dxc
        du = cw_ref[CONV_WIDTH - 1:CONV_WIDTH, :] * dxc
        for k in range(CONV_WIDTH - 1):
            off = CONV_WIDTH - 1 - k
            du = du + cw_ref[k:k + 1, :] * dxc_s[off:off + tc, :]
        du_ref[...] = du.astype(du_ref.dtype)
        dxc_s[tc:tc + halo, :] = dxc_s[0:halo, :]
        uext_s[0:halo, :] = jnp.where(chunk > 0, uh_ref[...], 0.0)
        uext_s[halo:halo + tc, :] = u_ref[...]
        for k in range(CONV_WIDTH):
            off = halo - (CONV_WIDTH - 1) + k
            vec_ref[4 + k:5 + k, :] += _rowsum(dxc * uext_s[off:off + tc, :])

        @pl.when(pl.program_id(0) == steps - 1)
        def _():
            for j in range(slabs):
                cols = slice(j * LRU_SLAB, (j + 1) * LRU_SLAB)
                dwr_out[:, cols] = _fold_diag_blocks(dwr_ref[j], head_dim, head_dim).astype(dwr_out.dtype)
                dwi_out[:, cols] = _fold_diag_blocks(dwi_ref[j], head_dim, head_dim).astype(dwi_out.dtype)

    halo_spec = pl.BlockSpec((halo, w), lambda i: (jnp.maximum((steps - 1 - i) * sub_per_chunk - 1, 0), 0))
    return _run(
        body, comm, name="lru_bwd", grid=(steps,),
        out_shape=[jax.ShapeDtypeStruct((t, w), BF16), jax.ShapeDtypeStruct((head_dim, w), BF16),
                   jax.ShapeDtypeStruct((head_dim, w), BF16), jax.ShapeDtypeStruct((SUBLANES, w), F32)],
        in_specs=[_rows_rev(tc, w, steps)] * 4 + [halo_spec, _whole((CONV_WIDTH, w)), _whole(wr_blk.shape),
                                                  _whole((1, w)), _whole(wi_blk.shape), _whole((1, w)), _whole((1, w))],
        out_specs=[_rows_rev(tc, w, steps), _acc((head_dim, w)), _acc((head_dim, w)), _acc((SUBLANES, w))],
        scratch_shapes=[pltpu.VMEM((tc, w), F32), pltpu.VMEM((tc, w), F32), pltpu.VMEM((tc + halo, w), F32),
                        pltpu.VMEM((halo + tc, w), F32), pltpu.VMEM((SUBLANES, w), F32),
                        pltpu.VMEM((slabs, LRU_SLAB, LRU_SLAB), F32), pltpu.VMEM((slabs, LRU_SLAB, LRU_SLAB), F32)],
        semantics="arbitrary",
    )(dh, xc, hprev, u, u, conv_w, wr_blk, b_r, wi_blk, b_i, lru_lambda)


def _s5_bwd(dya, y, sr, si, u, w_glu, b_glu, cre_blk, cimn_blk, bbr_blk, bbi_blk, ar, ai, d_skip, comm=None):
    t, sa = dya.shape
    gn = sr.shape[1]
    ns, _, sw = bbr_blk.shape
    tc = min(TIME_CHUNK, t)
    steps = t // tc
    halo = SUBLANES

    def body(dya_ref, y_ref, sr_ref, si_ref, u_ref, wg_ref, bg_ref, cre_ref, cim_ref, bbr_ref, bbi_ref,
             ar_ref, ai_ref, d_ref, du_ref, lr_ref, li_ref, dy_ref, dwg_out, vsa_ref, vgn_ref, gr_s, gi_s, cr_s, ci_s,
             dwg_ref):
        @pl.when(pl.program_id(0) == 0)
        def _():
            cr_s[...] = jnp.zeros_like(cr_s)
            ci_s[...] = jnp.zeros_like(ci_s)
            gr_s[tc:tc + halo, :] = jnp.zeros((halo, gn), F32)
            gi_s[tc:tc + halo, :] = jnp.zeros((halo, gn), F32)
            dwg_ref[...] = jnp.zeros_like(dwg_ref)
            vsa_ref[...] = jnp.zeros_like(vsa_ref)
            vgn_ref[...] = jnp.zeros_like(vgn_ref)

        yv = y_ref[...]
        uv = u_ref[...]
        zz = _gelu(yv)
        sg = _sigmoid(_dot(zz, wg_ref[...]) + bg_ref[...])
        dyav = dya_ref[...]
        dq = dyav * zz * sg * (1.0 - sg)
        dzz = dyav * sg + _dot_nt(dq, wg_ref[...])
        dwg_ref[...] += _dot_tn(zz, dq)
        dy = dzz * _gelu_grad(yv)
        dyb = dy.astype(BF16)
        dy_ref[...] = dyb.astype(dy_ref.dtype)
        vsa_ref[0:1, :] += _rowsum(dq)
        vsa_ref[1:2, :] += _rowsum(dy * uv)
        for m in range(ns):
            dym = dyb[:, m * S5_SLAB:(m + 1) * S5_SLAB]
            gr_s[0:tc, m * sw:(m + 1) * sw] = _dot_nt(dym, cre_ref[m])
            gi_s[0:tc, m * sw:(m + 1) * sw] = _dot_nt(dym, cim_ref[m])
        a_r = ar_ref[...]
        a_i = ai_ref[...]

        def step(i, carry):
            l_r, l_i = carry
            at = pl.ds(tc - 1 - i, 1)
            n_r = gr_s[at, :] + a_r * l_r + a_i * l_i
            n_i = gi_s[at, :] + a_r * l_i - a_i * l_r
            gr_s[at, :] = n_r
            gi_s[at, :] = n_i
            return n_r, n_i

        l_r, l_i = lax.fori_loop(0, tc, step, (cr_s[0:1, :], ci_s[0:1, :]), unroll=8)
        cr_s[0:1, :] = l_r
        ci_s[0:1, :] = l_i
        nxt_r = gr_s[1:tc + 1, :]
        nxt_i = gi_s[1:tc + 1, :]
        srv = sr_ref[...].astype(F32)
        siv = si_ref[...].astype(F32)
        vgn_ref[0:1, :] += _rowsum(nxt_r * srv + nxt_i * siv)
        vgn_ref[1:2, :] += _rowsum(nxt_i * srv - nxt_r * siv)
        lam_r = gr_s[0:tc, :]
        lam_i = gi_s[0:tc, :]
        gr_s[tc:tc + halo, :] = gr_s[0:halo, :]
        gi_s[tc:tc + halo, :] = gi_s[0:halo, :]
        lrb = lam_r.astype(BF16)
        lib = lam_i.astype(BF16)
        lr_ref[...] = lrb.astype(lr_ref.dtype)
        li_ref[...] = lib.astype(li_ref.dtype)
        for m in range(ns):
            states, chans = slice(m * sw, (m + 1) * sw), slice(m * S5_SLAB, (m + 1) * S5_SLAB)
            du_ref[:, chans] = (_dot_nt(lrb[:, states], bbr_ref[m]) + _dot_nt(lib[:, states], bbi_ref[m])
                                + dy[:, chans] * d_ref[:, chans]).astype(du_ref.dtype)
        _store_on_last([(dwg_ref, dwg_out)])

    return _run(
        body, comm, name="s5_bwd", grid=(steps,),
        out_shape=[jax.ShapeDtypeStruct((t, sa), BF16), jax.ShapeDtypeStruct((t, gn), BF16),
                   jax.ShapeDtypeStruct((t, gn), BF16), jax.ShapeDtypeStruct((t, sa), BF16),
                   jax.ShapeDtypeStruct((sa, sa), BF16), jax.ShapeDtypeStruct((SUBLANES, sa), F32),
                   jax.ShapeDtypeStruct((SUBLANES, gn), F32)],
        in_specs=[_rows_rev(tc, sa, steps), _rows_rev(tc, sa, steps), _rows_rev(tc, gn, steps), _rows_rev(tc, gn, steps),
                  _rows_rev(tc, sa, steps), _whole((sa, sa)), _whole((1, sa)), _whole(cre_blk.shape),
                  _whole(cimn_blk.shape), _whole(bbr_blk.shape), _whole(bbi_blk.shape), _whole((1, gn)), _whole((1, gn)),
                  _whole((1, sa))],
        out_specs=[_rows_rev(tc, sa, steps), _rows_rev(tc, gn, steps), _rows_rev(tc, gn, steps), _rows_rev(tc, sa, steps),
                   _acc((sa, sa)), _acc((SUBLANES, sa)), _acc((SUBLANES, gn))],
        scratch_shapes=[pltpu.VMEM((tc + halo, gn), F32), pltpu.VMEM((tc + halo, gn), F32),
                        pltpu.VMEM((SUBLANES, gn), F32), pltpu.VMEM((SUBLANES, gn), F32), pltpu.VMEM((sa, sa), F32)],
        semantics="arbitrary",
    )(dya, y, sr, si, u, w_glu, b_glu, cre_blk, cimn_blk, bbr_blk, bbi_blk, ar, ai, d_skip)


def _inproj_bwd(dparts, x, dx1, g_mix, w_in_t, comm=None):
    t, d = x.shape
    n = w_in_t.shape[0]
    widths = [p.shape[1] for p in dparts]
    offs = [sum(widths[:i]) for i in range(len(widths) + 1)]
    tm = min(TOKEN_TILE, t)
    np_ = len(dparts)

    def body(*refs):
        dz_refs = refs[:np_]
        x_ref, dx1_ref, g_ref, w_ref, gx_ref, h_ref, vd_ref, vn_ref = refs[np_:]
        _zero_on_first(vd_ref, vn_ref)
        dh = jnp.zeros((tm, d), F32)
        for k, r in enumerate(dz_refs):
            lo, hi = offs[k], offs[k + 1]
            dzk = r[...]
            dh = dh + _dot(dzk, w_ref[lo:hi, :])
            vn_ref[0:1, lo:hi] += _rowsum(dzk.astype(F32))
        xhat, r0 = _rms_stats(x_ref[...])
        h_ref[...] = (xhat * g_ref[...]).astype(h_ref.dtype)
        dxn, dg = _rms_bwd(dh, xhat, r0, g_ref[...])
        gx_ref[...] = dx1_ref[...] + dxn
        vd_ref[0:1, :] += _rowsum(dg)

    return _run(
        body, comm, name="inproj_bwd", grid=(t // tm,),
        out_shape=[jax.ShapeDtypeStruct((t, d), F32), jax.ShapeDtypeStruct((t, d), BF16),
                   jax.ShapeDtypeStruct((SUBLANES, d), F32), jax.ShapeDtypeStruct((SUBLANES, n), F32)],
        in_specs=[_rows(tm, w) for w in widths] + [_rows(tm, d), _rows(tm, d), _whole((1, d)), _whole((n, d))],
        out_specs=[_rows(tm, d), _rows(tm, d), _acc((SUBLANES, d)), _acc((SUBLANES, n))],
        semantics="arbitrary",
    )(*dparts, x, dx1, g_mix, w_in_t)


def _dw_from_parts(dparts, h, tn, name, comm=None):
    t, d = h.shape
    widths = [p.shape[1] for p in dparts]
    offs = [sum(widths[:i]) for i in range(len(widths) + 1)]
    np_ = len(dparts)

    def body(*refs):
        h_ref, o_ref = refs[np_], refs[np_ + 1]
        hv = h_ref[...]
        for k, r in enumerate(refs[:np_]):
            o_ref[offs[k]:offs[k + 1], :] = _dot_tn(r[...], hv).astype(o_ref.dtype)

    return _run(
        body, comm, name=name, grid=(d // tn,),
        out_shape=jax.ShapeDtypeStruct((offs[-1], d), BF16),
        in_specs=[_whole(p.shape) for p in dparts] + [pl.BlockSpec((t, tn), lambda j: (0, j))],
        out_specs=pl.BlockSpec((offs[-1], tn), lambda j: (0, j)),
        semantics="parallel",
    )(*dparts, h)


def _prep(lr_row, li_row, ldt_row, lr_col, li_col, ldt_col, b_re, b_im, c_re, c_im, w_r, w_i, comm=None):
    gn, pch = b_re.shape
    sa, n = c_re.shape
    w, hd = w_r.shape
    ns, sw, lsl = sa // S5_SLAB, S5_SLAB * n // pch, w // LRU_SLAB

    def spread_cols(vals, row_group, col_group, width):
        r, k = vals.shape
        tile = (lax.broadcasted_iota(jnp.int32, (k, width), 0) == lax.broadcasted_iota(jnp.int32, (k, width), 1) % k)
        rows = lax.broadcasted_iota(jnp.int32, (r, width), 0) // row_group
        cols = lax.broadcasted_iota(jnp.int32, (r, width), 1) // col_group
        return jnp.where(rows == cols, _dot(vals, tile.astype(BF16)), 0.0)

    def spread_rows(vals, row_group, col_group, height):
        k, c = vals.shape
        tile = (lax.broadcasted_iota(jnp.int32, (height, k), 0) % k == lax.broadcasted_iota(jnp.int32, (height, k), 1))
        rows = lax.broadcasted_iota(jnp.int32, (height, c), 0) // row_group
        cols = lax.broadcasted_iota(jnp.int32, (height, c), 1) // col_group
        return jnp.where(rows == cols, _dot(tile.astype(BF16), vals), 0.0)

    def body(lrr, lir, ldr, lrc, lic, ldc, bre, bim, cre, cim, wr, wi,
             ar_o, ai_o, bbr_o, bbi_o, cre_o, cim_o, wr_o, wi_o):
        ar, ai, _, _ = _disc_scalars(lrr[...], lir[...], ldr[...])
        ar_o[...] = ar
        ai_o[...] = ai
        _, _, bbr, bbi = _disc_cols(lrc[...], lic[...], ldc[...], bre[...], bim[...])
        bbr_t, bbi_t = bbr.T, bbi.T
        for m in range(ns):
            bbr_o[m] = spread_rows(bbr_t[:, m * sw:(m + 1) * sw], pch, n, S5_SLAB).astype(bbr_o.dtype)
            bbi_o[m] = spread_rows(bbi_t[:, m * sw:(m + 1) * sw], pch, n, S5_SLAB).astype(bbi_o.dtype)
            rows = slice(m * S5_SLAB, (m + 1) * S5_SLAB)
            cre_o[m] = spread_rows(cre[rows, :].T, n, pch, sw).astype(cre_o.dtype)
            cim_o[m] = spread_rows(-cim[rows, :].T, n, pch, sw).astype(cim_o.dtype)
        for j in range(lsl):
            rows = slice(j * LRU_SLAB, (j + 1) * LRU_SLAB)
            wr_o[j] = spread_cols(wr[rows, :], hd, hd, LRU_SLAB).astype(wr_o.dtype)
            wi_o[j] = spread_cols(wi[rows, :], hd, hd, LRU_SLAB).astype(wi_o.dtype)

    args = (lr_row, li_row, ldt_row, lr_col, li_col, ldt_col, b_re, b_im, c_re, c_im, w_r, w_i)
    out_shape = [jax.ShapeDtypeStruct((1, gn), F32), jax.ShapeDtypeStruct((1, gn), F32),
                 jax.ShapeDtypeStruct((ns, S5_SLAB, sw), BF16), jax.ShapeDtypeStruct((ns, S5_SLAB, sw), BF16),
                 jax.ShapeDtypeStruct((ns, sw, S5_SLAB), BF16), jax.ShapeDtypeStruct((ns, sw, S5_SLAB), BF16),
                 jax.ShapeDtypeStruct((lsl, LRU_SLAB, LRU_SLAB), BF16),
                 jax.ShapeDtypeStruct((lsl, LRU_SLAB, LRU_SLAB), BF16)]
    return _run(
        body, comm, name="prep", grid=(1,), out_shape=out_shape, in_specs=[_whole(a.shape) for a in args],
        out_specs=[_acc(s.shape) for s in out_shape], semantics="arbitrary",
    )(*args)


def _s5_param_grads(lam_r, lam_i, sr, si, u, dy, pch, n, comm=None):
    t, gn = lam_r.shape
    sa = u.shape[1]
    sw = S5_SLAB * n // pch

    def body(lr_ref, li_ref, sr_ref, si_ref, u_ref, dy_ref, dbr_ref, dbi_ref, dcr_ref, dci_ref):
        uv = u_ref[...]
        dyv = dy_ref[...]
        dbr_ref[...] = _fold_diag_blocks(_dot_tn(uv, lr_ref[...]), pch, n).astype(dbr_ref.dtype)
        dbi_ref[...] = _fold_diag_blocks(_dot_tn(uv, li_ref[...]), pch, n).astype(dbi_ref.dtype)
        dcr_ref[...] = _fold_diag_blocks(_dot_tn(sr_ref[...], dyv), n, pch).astype(dcr_ref.dtype)
        dci_ref[...] = _fold_diag_blocks(_dot_tn(si_ref[...], dyv), n, pch).astype(dci_ref.dtype)

    states = pl.BlockSpec((t, sw), lambda m: (0, m))
    chans = pl.BlockSpec((t, S5_SLAB), lambda m: (0, m))
    return _run(
        body, comm, name="s5_param_grads", grid=(sa // S5_SLAB,),
        out_shape=[jax.ShapeDtypeStruct((pch, gn), BF16), jax.ShapeDtypeStruct((pch, gn), BF16),
                   jax.ShapeDtypeStruct((n, sa), BF16), jax.ShapeDtypeStruct((n, sa), BF16)],
        in_specs=[states, states, states, states, chans, chans],
        out_specs=[pl.BlockSpec((pch, sw), lambda m: (0, m)), pl.BlockSpec((pch, sw), lambda m: (0, m)),
                   pl.BlockSpec((n, S5_SLAB), lambda m: (0, m)), pl.BlockSpec((n, S5_SLAB), lambda m: (0, m))],
        semantics="parallel",
    )(lam_r, lam_i, sr, si, u, dy)


SMALL_PARTS = ["vec_tail", "vec_ffn", "vec_lru", "vec_mix", "vec_bin", "vec_sa", "vec_gn", "dw_r", "dw_i", "dbb_re",
               "dbb_im", "dc_re", "dc_imn", "loss"]


def _small_reduce(parts, shapes, lr_col, li_col, ldt_col, b_re, b_im, groups, after=()):
    gn, pch = b_re.shape
    n = gn // groups
    nparts = parts[SMALL_PARTS[0]].size // math.prod(shapes[SMALL_PARTS[0]])
    np_, nout, first_out = len(SMALL_PARTS), 24, len(SMALL_PARTS) + 5 + len(after)

    def body(*refs):
        ins = refs[:np_]
        lr, li, ldt, bre, bim = refs[np_:np_ + 5]
        outs = refs[first_out:first_out + nout]
        sums = dict(zip(SMALL_PARTS, refs[first_out + nout:]))

        @pl.when(pl.program_id(0) == 0)
        def _():
            for k, r in zip(SMALL_PARTS, ins):
                sums[k][...] = r[...].astype(F32)

        @pl.when(pl.program_id(0) > 0)
        def _():
            for k, r in zip(SMALL_PARTS, ins):
                sums[k][...] += r[...].astype(F32)

        @pl.when(pl.program_id(0) == nparts - 1)
        def _():
            finish({k: s[...] for k, s in sums.items()}, lr, li, ldt, bre, bim, *outs)

    def finish(tot, lr, li, ldt, bre, bim, o_loss, o_gmix, o_bin, o_bglu, o_s5d, o_convb, o_br, o_bi, o_lam, o_gffn,
               o_gpg, o_bpg, o_gple, o_gfin, o_wr, o_wi, o_cre, o_cim, o_bre, o_bim, o_lre, o_lim, o_ldt, o_convw):
        o_loss[...] = tot["loss"]
        o_convw[...] = tot["vec_lru"][SUBLANES - CONV_WIDTH:SUBLANES]
        o_bpg[...] = tot["vec_tail"][0:1]
        o_gpg[...] = tot["vec_tail"][1:2]
        o_gple[...] = tot["vec_tail"][2:3]
        o_gfin[...] = tot["vec_tail"][3:4]
        o_gffn[...] = tot["vec_ffn"][0:1]
        o_convb[...] = tot["vec_lru"][0:1]
        o_br[...] = tot["vec_lru"][1:2]
        o_bi[...] = tot["vec_lru"][2:3]
        o_lam[...] = tot["vec_lru"][3:4]
        o_gmix[...] = tot["vec_mix"][0:1]
        o_bin[...] = tot["vec_bin"][0:1]
        o_bglu[...] = tot["vec_sa"][0:1]
        o_s5d[...] = tot["vec_sa"][1:2]
        o_wr[...] = tot["dw_r"].T
        o_wi[...] = tot["dw_i"].T
        o_cre[...] = tot["dc_re"].T
        o_cim[...] = -tot["dc_imn"].T
        d_a = tot["vec_gn"].T
        _, chain = jax.vjp(_disc_cols, lr[...], li[...], ldt[...], bre[...], bim[...])
        d_lr, d_li, d_ldt, d_bre, d_bim = chain((d_a[:, 0:1], d_a[:, 1:2], tot["dbb_re"].T, tot["dbb_im"].T))
        o_lre[...] = d_lr
        o_lim[...] = d_li
        o_bre[...] = d_bre
        o_bim[...] = d_bim
        same = (lax.broadcasted_iota(jnp.int32, (groups, gn), 0)
                == lax.broadcasted_iota(jnp.int32, (groups, gn), 1) // n).astype(F32)
        o_ldt[...] = jnp.dot(same, d_ldt * jnp.ones((1, LANES), F32), preferred_element_type=F32,
                             precision=lax.Precision.HIGHEST)[:, 0:1]

    d = shapes["vec_mix"][1]
    nz = shapes["vec_bin"][1]
    sa = shapes["vec_sa"][1]
    w = shapes["vec_lru"][1]
    row = lambda c: jax.ShapeDtypeStruct((1, c), F32)
    out_shape = [jax.ShapeDtypeStruct(shapes["loss"], F32), row(d), row(nz), row(sa), row(sa), row(w), row(w), row(w),
                 row(w), row(d), row(d), row(d), row(d), row(d),
                 jax.ShapeDtypeStruct(shapes["dw_r"][::-1], F32), jax.ShapeDtypeStruct(shapes["dw_i"][::-1], F32),
                 jax.ShapeDtypeStruct(shapes["dc_re"][::-1], F32), jax.ShapeDtypeStruct(shapes["dc_imn"][::-1], F32),
                 jax.ShapeDtypeStruct((gn, pch), F32), jax.ShapeDtypeStruct((gn, pch), F32),
                 jax.ShapeDtypeStruct((gn, 1), F32), jax.ShapeDtypeStruct((gn, 1), F32),
                 jax.ShapeDtypeStruct((groups, 1), F32), jax.ShapeDtypeStruct((CONV_WIDTH, w), F32)]
    def part_spec(k):
        r, c = shapes[k]
        if parts[k].ndim == 3:
            return pl.BlockSpec((None, r, c), lambda i: (i, 0, 0))
        return pl.BlockSpec((r, c), lambda i: (i, 0))

    outs = _run(
        body, None, name="small_reduce", grid=(nparts,), out_shape=out_shape,
        in_specs=([part_spec(k) for k in SMALL_PARTS] + [_whole(a.shape) for a in (lr_col, li_col, ldt_col, b_re, b_im)]
                  + [_ANY] * len(after)),
        out_specs=[_acc(s.shape) for s in out_shape],
        scratch_shapes=[pltpu.VMEM(shapes[k], F32) for k in SMALL_PARTS],
        semantics="arbitrary",
    )(*[parts[k] for k in SMALL_PARTS], lr_col, li_col, ldt_col, b_re, b_im, *after)
    names = ["loss", "g_mix", "b_in", "b_glu", "s5_d", "conv_b", "b_r", "b_i", "lru_lambda", "g_ffn", "g_ple_gate",
             "b_ple_gate", "g_ple", "g_final", "w_r", "w_i", "s5_c_re", "s5_c_im", "s5_b_re", "s5_b_im", "lam_re",
             "lam_im", "log_dt", "conv_w"]
    return dict(zip(names, outs))


def _adamw_small(ws, gs, ms, vs):
    n = len(ws)

    def body(*refs):
        w_r, g_r, m_r, v_r = (refs[i * n:(i + 1) * n] for i in range(4))
        g_o, d_o, m_o, v_o = (refs[(4 + i) * n:(5 + i) * n] for i in range(4))
        for i in range(n):
            g = g_r[i][...]
            delta, m_new, v_new = _adamw_math(w_r[i][...], g, m_r[i][...], v_r[i][...])
            g_o[i][...] = g
            d_o[i][...] = delta
            m_o[i][...] = m_new
            v_o[i][...] = v_new

    shapes = [jax.ShapeDtypeStruct(a.shape, F32) for a in ws]
    outs = pl.pallas_call(body, name="adamw_small", out_shape=shapes * 4)(*ws, *gs, *ms, *vs)
    return outs[:n], outs[n:2 * n], outs[2 * n:3 * n], outs[3 * n:]


def _adamw_math(w, g, m, v):
    m_new = ADAM_B1 * m + (1.0 - ADAM_B1) * g
    v_new = ADAM_B2 * v + (1.0 - ADAM_B2) * (g * g)
    m_hat = m_new / (1.0 - ADAM_B1 ** ADAM_STEP)
    v_hat = v_new / (1.0 - ADAM_B2 ** ADAM_STEP)
    delta = -ADAM_LR * (m_hat / (jnp.sqrt(v_hat) + ADAM_EPS) + ADAM_WD * w)
    return delta, m_new, v_new


def _row_tile(rows):
    for cand in range(256, 0, -16):
        if rows % cand == 0:
            return cand
    return rows


def _adamw(parts, w, m, v, name, transposed=False, own=None, after=()):
    nw = len(w)
    rows, cols = w[0].shape
    npart = parts[0].shape[0]
    tr = _row_tile(rows)
    if transposed:
        parts_spec = pl.BlockSpec((npart, cols, tr), lambda i: (0, 0, i))
    else:
        parts_spec = pl.BlockSpec((npart, tr, cols), lambda i: (0, i, 0))
    per = 4 if own is None else 5
    first_out = per * nw + len(after)

    def body(*refs):
        mine = None if own is None else _chip(_mesh_position())
        for i in range(nw):
            group = refs[per * i:per * i + per]
            p_ref, (w_ref, m_ref, v_ref) = group[0], group[-3:]
            g_ref, d_ref, mo_ref, vo_ref = refs[first_out + 4 * i:first_out + 4 * i + 4]

            def part(k):
                a = p_ref[k].astype(F32)
                return a if own is None else jnp.where(mine == k, group[1][k].astype(F32), a)

            g = part(0)
            for k in range(1, npart):
                g = g + part(k)
            if transposed:
                g = g.T
            delta, m_new, v_new = _adamw_math(w_ref[...], g, m_ref[...], v_ref[...])
            g_ref[...] = g
            d_ref[...] = delta
            mo_ref[...] = m_new
            vo_ref[...] = v_new

    groups = zip(parts, w, m, v) if own is None else zip(parts, own, w, m, v)
    res = pl.pallas_call(
        body, name=name, grid=(rows // tr,),
        out_shape=[jax.ShapeDtypeStruct((rows, cols), F32)] * (4 * nw),
        in_specs=([parts_spec] * (per - 3) + [_rows(tr, cols)] * 3) * nw + [_ANY] * len(after),
        out_specs=[_rows(tr, cols)] * (4 * nw),
        compiler_params=_params("parallel"),
    )(*[a for group in groups for a in group], *after)
    return [res[4 * i:4 * i + 4] for i in range(nw)]


def _mesh_position():
    return lax.axis_index("x"), lax.axis_index("y"), lax.axis_index("c")


def _flip(pos, rel):
    x, y, c = pos
    return (1 - x if rel & 4 else x, 1 - y if rel & 2 else y, 1 - c if rel & 1 else c)


def _index(pos):
    return 4 * pos[0] + 2 * pos[1] + pos[2]


_ANY = pl.BlockSpec(memory_space=pl.ANY)
FLAT_ROWS = 32


def _dma_sems(n):
    return [pltpu.SemaphoreType.DMA((n, N_DEV - 1)), pltpu.SemaphoreType.DMA((n, N_DEV - 1)), pltpu.SemaphoreType.DMA((n,))]


def _block_of(ref, idx, rows, flat):
    if flat:
        return ref.at[pl.ds(pl.multiple_of(idx * rows, FLAT_ROWS), rows), :]
    return ref.at[idx]


class _Gather:
    chips = (4, 2, 6)
    rels = frozenset((1, 4, 2, 6))

    def __init__(self, shards):
        self.inputs = list(shards)
        self.flat = [s.shape[0] % FLAT_ROWS == 0 for s in shards]
        self.out_shape = [
            jax.ShapeDtypeStruct((N_DEV * s.shape[0], s.shape[1]) if f else (N_DEV,) + s.shape, s.dtype)
            for s, f in zip(shards, self.flat)]
        self.sems = _dma_sems(len(shards))

    def _copy(self, ins, outs, sems, i, k, block, to, own=False, lands_index=None):
        a = i if lands_index is None else lands_index
        dst = _block_of(outs[a], _index(block), self.inputs[a].shape[0], self.flat[a])
        return pltpu.make_async_remote_copy(
            src_ref=ins[a] if own else dst, dst_ref=dst, send_sem=sems[0].at[i, k], recv_sem=sems[1].at[i, k],
            device_id=to, device_id_type=MESH)

    def _local(self, ins, outs, sems, i, me):
        dst = _block_of(outs[i], _index(me), self.inputs[i].shape[0], self.flat[i])
        return pltpu.make_async_copy(ins[i], dst, sems[2].at[i])

    def _first(self, ins, outs, sems, i, me):
        cps = [self._copy(ins, outs, sems, i, 0, me, _flip(me, 1), own=True)]
        cps += [self._copy(ins, outs, sems, i, 1 + j, me, _flip(me, rel), own=True) for j, rel in enumerate(self.chips)]
        return cps

    def _passed(self, ins, outs, sems, i, j, me):
        return self._copy(ins, outs, sems, i, 4 + j, _flip(me, self.chips[j]), _flip(me, 1))

    def before(self, ins, outs, sems):
        n = len(self.inputs)
        me = _mesh_position()

        @pl.when(pl.program_id(0) == 0)
        def _():
            for i in range(n):
                self._local(ins, outs, sems, i, me).start()
                for cp in self._first(ins, outs, sems, i, me):
                    cp.start()

        @pl.when(pl.program_id(0) == pl.num_programs(0) - 1)
        def _():
            for j, rel in enumerate(self.chips):
                for i in range(n):
                    self._copy(ins, outs, sems, i, 1 + j, _flip(me, rel), me).wait_recv()
                    self._passed(ins, outs, sems, i, j, me).start()

    def after(self, ins, outs, sems):
        n = len(self.inputs)
        me = _mesh_position()
        sibling = _flip(me, 1)

        @pl.when(pl.program_id(0) == pl.num_programs(0) - 1)
        def _():
            for i in range(n):
                self._copy(ins, outs, sems, i, 0, sibling, me).wait_recv()
                for j, rel in enumerate(self.chips):
                    self._copy(ins, outs, sems, i, 4 + j, _flip(sibling, rel), me).wait_recv()
            for i in range(n):
                for cp in self._first(ins, outs, sems, i, me):
                    cp.wait_send()
                for j in range(len(self.chips)):
                    self._passed(ins, outs, sems, i, j, me).wait_send()
                self._local(ins, outs, sems, i, me).wait()


N_CHIPS = N_DEV // 2


def _chip(pos):
    return 2 * pos[0] + pos[1]


class _PairSwap:
    rels = frozenset((1,))

    def __init__(self, arrays):
        self.inputs = list(arrays)
        self.rows = [a.shape[0] // N_DEV for a in arrays]
        for r in self.rows:
            assert r % FLAT_ROWS == 0, r
        self.out_shape = [jax.ShapeDtypeStruct((N_CHIPS, r, a.shape[1]), a.dtype) for a, r in zip(arrays, self.rows)]
        n = len(arrays)
        self.sems = [pltpu.SemaphoreType.DMA((n, N_CHIPS)), pltpu.SemaphoreType.DMA((n, N_CHIPS))]

    def _copy(self, ins, outs, sems, i, j, me):
        sibling = _flip(me, 1)
        return pltpu.make_async_remote_copy(
            src_ref=_block_of(ins[i], 2 * j + sibling[2], self.rows[i], True), dst_ref=outs[i].at[j],
            send_sem=sems[0].at[i, j], recv_sem=sems[1].at[i, j], device_id=sibling, device_id_type=MESH)

    def before(self, ins, outs, sems):
        me = _mesh_position()

        @pl.when(pl.program_id(0) == 0)
        def _():
            for i in range(len(self.inputs)):
                for j in range(N_CHIPS):
                    self._copy(ins, outs, sems, i, j, me).start()

    def after(self, ins, outs, sems):
        me = _mesh_position()

        @pl.when(pl.program_id(0) == pl.num_programs(0) - 1)
        def _():
            for i in range(len(self.inputs)):
                for j in range(N_CHIPS):
                    self._copy(ins, outs, sems, i, j, me).wait()


class _ChipExchange:
    chips = (4, 2, 6)
    rels = frozenset(chips)

    def __init__(self, arrays, after=()):
        self.arrays = len(arrays)
        self.inputs = list(arrays) + list(after)
        self.out_shape = [jax.ShapeDtypeStruct(a.shape, a.dtype) for a in arrays]
        n = len(arrays)
        self.sems = [pltpu.SemaphoreType.DMA((n, 3)), pltpu.SemaphoreType.DMA((n, 3)), pltpu.SemaphoreType.DMA((n,))]

    def _send(self, ins, outs, sems, i, k, me):
        peer = _flip(me, self.chips[k])
        return pltpu.make_async_remote_copy(
            src_ref=ins[i].at[_chip(peer)], dst_ref=outs[i].at[_chip(me)], send_sem=sems[0].at[i, k],
            recv_sem=sems[1].at[i, k], device_id=peer, device_id_type=MESH)

    def _arrival(self, ins, outs, sems, i, k, me):
        peer = _flip(me, self.chips[k])
        return pltpu.make_async_remote_copy(
            src_ref=ins[i].at[_chip(me)], dst_ref=outs[i].at[_chip(peer)], send_sem=sems[0].at[i, k],
            recv_sem=sems[1].at[i, k], device_id=peer, device_id_type=MESH)

    def _local(self, ins, outs, sems, i, me):
        return pltpu.make_async_copy(ins[i].at[_chip(me)], outs[i].at[_chip(me)], sems[2].at[i])

    def before(self, ins, outs, sems):
        me = _mesh_position()

        @pl.when(pl.program_id(0) == 0)
        def _():
            for i in range(self.arrays):
                self._local(ins, outs, sems, i, me).start()
                for k in range(len(self.chips)):
                    self._send(ins, outs, sems, i, k, me).start()

    def after(self, ins, outs, sems):
        me = _mesh_position()

        @pl.when(pl.program_id(0) == pl.num_programs(0) - 1)
        def _():
            for i in range(self.arrays):
                for k in range(len(self.chips)):
                    self._arrival(ins, outs, sems, i, k, me).wait_recv()
            for i in range(self.arrays):
                for k in range(len(self.chips)):
                    self._send(ins, outs, sems, i, k, me).wait_send()
                self._local(ins, outs, sems, i, me).wait()


_HBM = pl.BlockSpec(memory_space=pltpu.HBM)
_SEM = pl.BlockSpec(memory_space=pltpu.SEMAPHORE)
_DATAFLOW = pltpu.SideEffectType.DATAFLOW_SIDE_EFFECTING
_CHIP_RELS = (4, 2, 6)
_SPLIT_COLLECTIVE_ID = 3


def _chips_copy(src_ref, dst_ref, send_sems, recv_sems, k, me):
    peer = _flip(me, _CHIP_RELS[k])
    return pltpu.make_async_remote_copy(
        src_ref=src_ref.at[_chip(peer)], dst_ref=dst_ref.at[_chip(me)], send_sem=send_sems.at[k],
        recv_sem=recv_sems.at[k], device_id=peer, device_id_type=MESH)


def _chips_start(pairs, name):
    def body(p_ref, land_ref, send_sems, recv_sems, p_thru, land_thru, token):
        me = _mesh_position()
        sem = pltpu.get_barrier_semaphore()
        for rel in _CHIP_RELS:
            pl.semaphore_signal(sem, inc=1, device_id=_flip(me, rel), device_id_type=MESH)
        pl.semaphore_wait(sem, len(_CHIP_RELS))
        for k in range(len(_CHIP_RELS)):
            _chips_copy(p_ref, land_ref, send_sems, recv_sems, k, me).start()
        token[...] = jnp.zeros_like(token)

    n = len(_CHIP_RELS)
    return pl.pallas_call(
        body, name="chips_start_" + name,
        out_shape=(pltpu.SemaphoreType.DMA((n,)), pltpu.SemaphoreType.DMA((n,)), pltpu.HBM(pairs.shape, pairs.dtype),
                   pltpu.HBM(pairs.shape, pairs.dtype), jax.ShapeDtypeStruct((SUBLANES, LANES), F32)),
        in_specs=(_HBM, _HBM), out_specs=(_SEM, _SEM, _HBM, _HBM, pl.BlockSpec(memory_space=pltpu.VMEM)),
        input_output_aliases={0: 2, 1: 3},
        compiler_params=pltpu.CompilerParams(has_side_effects=_DATAFLOW, collective_id=_SPLIT_COLLECTIVE_ID),
    )(pltpu.with_memory_space_constraint(pairs, pltpu.HBM),
      pltpu.with_memory_space_constraint(lax.empty(pairs.shape, pairs.dtype), pltpu.HBM))


def _chips_wait(send_sems, recv_sems, p_thru, land_thru, after, name):
    def body(p_ref, land_ref, send_sems, recv_sems, *rest):
        me = _mesh_position()
        for k in range(len(_CHIP_RELS)):
            copy = _chips_copy(p_ref, land_ref, send_sems, recv_sems, k, me)
            copy.wait_send()
            copy.wait_recv()

    return pl.pallas_call(
        body, name="chips_wait_" + name,
        out_shape=(pltpu.HBM(p_thru.shape, p_thru.dtype), pltpu.HBM(p_thru.shape, p_thru.dtype)),
        in_specs=(_HBM, _HBM, _SEM, _SEM) + (_ANY,) * len(after), out_specs=(_HBM, _HBM),
        input_output_aliases={0: 0, 1: 1}, compiler_params=pltpu.CompilerParams(has_side_effects=_DATAFLOW),
    )(p_thru, land_thru, send_sems, recv_sems, *after)


_GATHER_START_ID, _GATHER_PASS_ID = 4, 5


class _Rows:
    def __init__(self, ref, width):
        self.ref, self.width, self.at = ref, width, self

    def __getitem__(self, idx):
        i, k = idx
        return self.ref.at[i * self.width + k]


def _handshake(rels):
    me = _mesh_position()
    sem = pltpu.get_barrier_semaphore()
    for rel in rels:
        pl.semaphore_signal(sem, inc=1, device_id=_flip(me, rel), device_id_type=MESH)
    pl.semaphore_wait(sem, len(rels))


def _gather_start(shards, name, after=()):
    job, n, na = _Gather(shards), len(shards), len(after)

    def body(*refs):
        ins, lands = refs[:n], refs[n:2 * n]
        send, recv, local = refs[2 * n + na:2 * n + na + 3]
        sems = (_Rows(send, N_DEV - 1), _Rows(recv, N_DEV - 1), local)
        token = refs[-1]
        me = _mesh_position()
        _handshake(sorted(job.rels))
        for i in range(n):
            job._local(ins, lands, sems, i, me).start()
            for cp in job._first(ins, lands, sems, i, me):
                cp.start()
        token[...] = jnp.zeros_like(token)

    res = pl.pallas_call(
        body, name=name,
        out_shape=(pltpu.SemaphoreType.DMA((n * (N_DEV - 1),)), pltpu.SemaphoreType.DMA((n * (N_DEV - 1),)),
                   pltpu.SemaphoreType.DMA((n,))) + tuple(pltpu.HBM(s.shape, s.dtype) for s in job.out_shape)
        + (jax.ShapeDtypeStruct((SUBLANES, LANES), F32),),
        in_specs=(_HBM,) * (2 * n) + (_ANY,) * na,
        out_specs=(_SEM,) * 3 + (_HBM,) * n + (pl.BlockSpec(memory_space=pltpu.VMEM),),
        input_output_aliases={n + i: 3 + i for i in range(n)},
        compiler_params=pltpu.CompilerParams(has_side_effects=_DATAFLOW, collective_id=_GATHER_START_ID),
    )(*[pltpu.with_memory_space_constraint(s, pltpu.HBM) for s in shards],
      *[pltpu.with_memory_space_constraint(lax.empty(s.shape, s.dtype), pltpu.HBM) for s in job.out_shape], *after)
    return list(res[:3]), list(res[3:3 + n]), res[-1]


def _gather_pass(shards, which, sems, lands, name, after=()):
    job, n, m, na = _Gather(shards), len(shards), len(which), len(after)

    def body(*refs):
        lands_r, local = refs[:m], refs[m + 2]
        send, recv = _Rows(refs[m], N_DEV - 1), _Rows(refs[m + 1], N_DEV - 1)
        send2, recv2 = _Rows(refs[m + 3 + na], 3), _Rows(refs[m + 3 + na + 1], 3)
        me = _mesh_position()
        _handshake((1,))
        by_index = {i: lands_r[q] for q, i in enumerate(which)}
        for j, rel in enumerate(job.chips):
            for q, i in enumerate(which):
                job._copy(None, by_index, (send, recv, local), i, 1 + j, _flip(me, rel), me).wait_recv()
                job._copy(None, by_index, (send2, recv2), q, j, _flip(me, rel), _flip(me, 1), lands_index=i).start()

    res = pl.pallas_call(
        body, name=name,
        out_shape=(pltpu.SemaphoreType.DMA((m * 3,)), pltpu.SemaphoreType.DMA((m * 3,)))
        + tuple(pltpu.HBM(lands[i].shape, lands[i].dtype) for i in which),
        in_specs=(_HBM,) * m + (_SEM,) * 3 + (_ANY,) * na, out_specs=(_SEM, _SEM) + (_HBM,) * m,
        input_output_aliases={q: 2 + q for q in range(m)},
        compiler_params=pltpu.CompilerParams(has_side_effects=_DATAFLOW, collective_id=_GATHER_PASS_ID),
    )(*[lands[i] for i in which], *sems, *after)
    return list(res[:2]), list(res[2:])


def _gather_finish(shards, which, sems, sems2, lands, after, name):
    job, n, m = _Gather(shards), len(shards), len(which)

    def body(*refs):
        ins, lands_r = refs[:m], refs[m:2 * m]
        send, recv, local = _Rows(refs[2 * m], N_DEV - 1), _Rows(refs[2 * m + 1], N_DEV - 1), refs[2 * m + 2]
        send2, recv2 = _Rows(refs[2 * m + 3], 3), _Rows(refs[2 * m + 4], 3)
        me = _mesh_position()
        sibling = _flip(me, 1)
        by_index = {i: lands_r[q] for q, i in enumerate(which)}
        own = {i: ins[q] for q, i in enumerate(which)}
        for q, i in enumerate(which):
            job._copy(None, by_index, (send, recv, local), i, 0, sibling, me).wait_recv()
            for j, rel in enumerate(job.chips):
                job._copy(None, by_index, (send2, recv2), q, j, _flip(sibling, rel), me, lands_index=i).wait_recv()
                job._copy(None, by_index, (send2, recv2), q, j, _flip(me, rel), sibling, lands_index=i).wait_send()
            for cp in job._first(own, by_index, (send, recv, local), i, me):
                cp.wait_send()
            job._local(own, by_index, (send, recv, local), i, me).wait()

    res = pl.pallas_call(
        body, name=name, out_shape=tuple(pltpu.HBM(lands[i].shape, lands[i].dtype) for i in which),
        in_specs=(_HBM,) * (2 * m) + (_SEM,) * 5 + (_ANY,) * len(after), out_specs=(_HBM,) * m,
        input_output_aliases={m + q: q for q in range(m)},
        compiler_params=pltpu.CompilerParams(has_side_effects=_DATAFLOW),
    )(*[pltpu.with_memory_space_constraint(shards[i], pltpu.HBM) for i in which], *[lands[i] for i in which],
      *sems, *sems2, *after)
    return list(res)


def _pair_add(grads, halves, name):
    n = len(grads)

    def body(*refs):
        g_refs, h_refs, o_refs = refs[:n], refs[n:2 * n], refs[2 * n:]
        for i in range(n):
            o_refs[i][...] = (g_refs[i][...].astype(F32) + h_refs[i][...].astype(F32)).astype(o_refs[i].dtype)

    def own_block(h):
        return pl.BlockSpec(h.shape[1:], lambda j: (2 * j + lax.axis_index("c"), 0))

    def slot(h):
        return pl.BlockSpec((None,) + h.shape[1:], lambda j: (j, 0, 0))

    return pl.pallas_call(
        body, name=name, grid=(N_CHIPS,), out_shape=[jax.ShapeDtypeStruct(h.shape, h.dtype) for h in halves],
        in_specs=[own_block(h) for h in halves] + [slot(h) for h in halves], out_specs=[slot(h) for h in halves],
        compiler_params=_params("parallel"),
    )(*grads, *halves)


class _Both:
    def __init__(self, first, second):
        self.jobs = (first, second)
        self.rels = first.rels | second.rels
        self.inputs = first.inputs + second.inputs
        self.out_shape = first.out_shape + second.out_shape
        self.sems = first.sems + second.sems

    def _each(self, ins, outs, sems):
        a = self.jobs[0]
        i, o, s = len(a.inputs), len(a.out_shape), len(a.sems)
        return ((a, ins[:i], outs[:o], sems[:s]), (self.jobs[1], ins[i:], outs[o:], sems[s:]))

    def before(self, ins, outs, sems):
        for job, i, o, s in self._each(ins, outs, sems):
            job.before(i, o, s)

    def after(self, ins, outs, sems):
        for job, i, o, s in self._each(ins, outs, sems):
            job.after(i, o, s)


_COLLECTIVE_IDS = {(1,): 0, (2, 4, 6): 1, (1, 2, 4, 6): 2}


def _entry_barrier(rels):
    @pl.when(pl.program_id(0) == 0)
    def _():
        me = _mesh_position()
        sem = pltpu.get_barrier_semaphore()
        for rel in rels:
            pl.semaphore_signal(sem, inc=1, device_id=_flip(me, rel), device_id_type=MESH)
        pl.semaphore_wait(sem, len(rels))


def _run(body, comm, *, semantics, out_shape, in_specs, out_specs, scratch_shapes=(), **kw):
    if comm is None:
        return pl.pallas_call(body, out_shape=out_shape, in_specs=in_specs, out_specs=out_specs,
                              scratch_shapes=list(scratch_shapes), compiler_params=_params(semantics), **kw)
    single = not isinstance(out_shape, (list, tuple))
    outs = [out_shape] if single else list(out_shape)
    ospecs = [out_specs] if single else list(out_specs)
    counts = [len(in_specs), len(comm.inputs), len(outs), len(comm.out_shape), len(scratch_shapes), len(comm.sems)]
    rels = tuple(sorted(comm.rels))

    def carrying(*refs):
        groups, pos = [], 0
        for c in counts:
            groups.append(refs[pos:pos + c])
            pos += c
        main_in, comm_in, main_out, comm_out, main_scratch, comm_sems = groups
        _entry_barrier(rels)
        comm.before(comm_in, comm_out, comm_sems)
        body(*main_in, *main_out, *main_scratch)
        comm.after(comm_in, comm_out, comm_sems)

    call = pl.pallas_call(
        carrying, out_shape=outs + list(comm.out_shape), in_specs=list(in_specs) + [_ANY] * len(comm.inputs),
        out_specs=ospecs + [_ANY] * len(comm.out_shape), scratch_shapes=list(scratch_shapes) + list(comm.sems),
        compiler_params=pltpu.CompilerParams(dimension_semantics=("arbitrary",), vmem_limit_bytes=VMEM_LIMIT,
                                             collective_id=_COLLECTIVE_IDS[rels]), **kw)

    def apply(*args):
        res = call(*args, *comm.inputs)
        main = res[:len(outs)]
        return (main[0] if single else list(main)), list(res[len(outs):])

    return apply


def _alone(comm, name):
    return _run(lambda: None, comm, semantics="arbitrary", name=name, grid=(1,), out_shape=[], in_specs=[], out_specs=[])()[1]


SHARDED = {"w_in": 1, "w_glu": 0, "conv_w": 1, "w_a_out": 1, "w_b_out": 0, "w_o": 0, "w_ffn_gate": 1, "w_ffn_up": 1,
           "w_ffn_down": 0, "w_ple_gate": 0, "w_ple": 1}
TRANSPOSED = ("w_in", "w_a_out", "w_ffn_gate", "w_ffn_up", "w_ple")
LONG_AXIS_MINOR = ("w_in", "w_ffn_gate", "w_ffn_up")
NARROW_LAST = ("s5_b_re", "s5_b_im", "s5_d")
ADAMW_GROUPS = (("w_in",), ("w_glu",), ("conv_w",), ("w_a_out",), ("w_b_out", "w_o", "w_ple_gate"),
                ("w_ffn_gate", "w_ffn_up"), ("w_ffn_down",), ("w_ple",))
SMALL = ["g_mix", "b_in", "lam_re", "lam_im", "log_dt", "s5_b_re", "s5_b_im", "s5_c_re", "s5_c_im", "s5_d", "b_glu",
         "conv_b", "w_r", "b_r", "w_i", "b_i", "lru_lambda", "g_ffn", "g_ple_gate", "b_ple_gate", "g_ple", "g_final"]
WEIGHTS = ["g_mix", "w_in", "b_in", "lam_re", "lam_im", "log_dt", "s5_b_re", "s5_b_im", "s5_c_re", "s5_c_im", "s5_d",
           "w_glu", "b_glu", "conv_w", "conv_b", "w_r", "b_r", "w_i", "b_i", "lru_lambda", "w_a_out", "w_b_out", "w_o",
           "g_ffn", "w_ffn_gate", "w_ffn_up", "w_ffn_down", "g_ple_gate", "w_ple_gate", "b_ple_gate", "w_ple", "g_ple",
           "g_final"]


def _join_columns(gathered):
    nb, r, c = gathered.shape
    return jnp.transpose(gathered, (1, 0, 2)).reshape(r, nb * c)


def _disc_scalars(lr, li, ldt):
    dt = jnp.exp(ldt)
    mag = jnp.exp(lr * dt)
    ar = mag * jnp.cos(li * dt)
    ai = mag * jnp.sin(li * dt)
    den = lr * lr + li * li
    nr = ar - 1.0
    fr = (nr * lr + ai * li) / den
    fi = (ai * lr - nr * li) / den
    return ar, ai, fr, fi


def _disc_cols(lr, li, ldt, b_re, b_im):
    ar, ai, fr, fi = _disc_scalars(lr, li, ldt)
    return ar, ai, fr * b_re - fi * b_im, fr * b_im + fi * b_re


def _local_step(x, p, target, src, small, disc, distributed=True):
    full = {} if distributed else dict(src)
    gw, halves, pairs, got = {}, {}, {}, {}

    def gather(keys):
        return (_Gather([src[k] for k in keys]), keys, full) if distributed else None

    def swap(keys):
        return (_PairSwap([gw[k] for k in keys]), keys, halves) if distributed else None

    def chips(keys):
        return (_ChipExchange([pairs[k] for k in keys]), keys, got) if distributed else None

    def add_pairs(keys):
        if distributed:
            pairs.update(zip(keys, _pair_add([gw[k] for k in keys], [halves[k] for k in keys], "pair_add_" + keys[0])))

    def carry(fn, *args, jobs=()):
        jobs = [j for j in jobs if j is not None]
        if not jobs:
            return fn(*args)
        comm = jobs[0][0]
        for j in jobs[1:]:
            comm = _Both(comm, j[0])
        res, extra = fn(*args, comm=comm)
        for job, keys, sink in jobs:
            sink.update(zip(keys, extra[:len(job.out_shape)]))
            extra = extra[len(job.out_shape):]
        return res

    d = x.shape[1]
    g, n, pch = small["s5_b_re"].shape
    sa, lw = g * pch, small["lru_lambda"].shape[-1]
    widths = [sa, lw, d, d]
    row = lambda v: v.reshape(1, -1)

    hd = small["w_r"].shape[-1]
    ar_row, ai_row, bbr_blk, bbi_blk, cre_blk, cimn_blk, wr_blk, wi_blk = carry(
        _prep, *disc["rows"], *disc["cols"], disc["b_re"], disc["b_im"], small["s5_c_re"].reshape(sa, n),
        small["s5_c_im"].reshape(sa, n), small["w_r"].reshape(lw, hd), small["w_i"].reshape(lw, hd),
        jobs=[gather(["w_in"])])
    d_row = row(small["s5_d"])
    g_mix_row = row(small["g_mix"])
    if distributed:
        later = ["w_glu", "conv_w", "w_a_out", "w_b_out", "w_ffn_gate", "w_ffn_up", "w_o", "w_ffn_down", "w_ple_gate",
                 "w_ple"]
        wires = [src[k] for k in later]
        flight, lands, token = _gather_start(wires, "gather_start", after=[ar_row])
        g_mix_row = g_mix_row + token[0, 0]

        def arrive(keys, after):
            which = [later.index(k) for k in keys]
            passed, moved = _gather_pass(wires, which, flight, lands, "gather_pass_" + keys[0], after)
            for q, i in enumerate(which):
                lands[i] = moved[q]
            full.update(zip(keys, _gather_finish(wires, which, flight, passed, lands, (), "gather_finish_" + keys[0])))
    else:
        arrive = lambda keys, after: None

    u_a, u_b, za, zb = _inproj_fwd(x, g_mix_row, full["w_in"], row(small["b_in"]), widths)
    arrive(["w_glu", "conv_w", "w_a_out", "w_b_out"], [u_a])
    conv_w = _join_columns(full["conv_w"]) if distributed else full["conv_w"]
    branches = [_s5_fwd(u_a, bbr_blk, bbi_blk, ar_row, ai_row, cre_blk, cimn_blk, d_row, full["w_glu"],
                        row(small["b_glu"])),
                _lru_fwd(u_b, conv_w, row(small["conv_b"]), wr_blk, row(small["b_r"]), wi_blk, row(small["b_i"]),
                         row(small["lru_lambda"]))]
    (sr, si, y, y_a), (xc, h, hprev) = _stages(branches, "branches_fwd")
    arrive(["w_ffn_gate", "w_ffn_up", "w_o"], [y_a])
    x1, merged, ma, mb, fg, fu = _merge_ffn_up_fwd(y_a, h, za, zb, x, full["w_a_out"], full["w_b_out"], full["w_o"],
                                                   row(small["g_ffn"]), full["w_ffn_gate"], full["w_ffn_up"])
    arrive(["w_ffn_down", "w_ple_gate", "w_ple"], [x1])
    x2 = _ffn_down_fwd(fg, fu, x1, full["w_ffn_down"])

    dx2, loss_blk, gw["w_ple_gate"], gw["w_ple"], vec_tail = _tail_fwd_bwd(
        x2, p, target, row(small["g_ple_gate"]), full["w_ple_gate"], row(small["b_ple_gate"]), full["w_ple"],
        row(small["g_ple"]), row(small["g_final"]))
    dfg, dfu, act = carry(_ffn_bwd_a, dx2, fg, fu, full["w_ffn_down"], jobs=[swap(["w_ple_gate", "w_ple"])])
    tn_d = min(d, 512)
    gw["w_ffn_down"] = _matmul_tn(act, dx2, tn_d, "dw_ffn_down", BF16)
    add_pairs(["w_ple_gate", "w_ple"])
    dx1, h2, vec_ffn = carry(_ffn_bwd_b, dfg, dfu, x1, dx2, row(small["g_ffn"]), full["w_ffn_gate"], full["w_ffn_up"],
                             jobs=[chips(["w_ple_gate", "w_ple"]), swap(["w_ffn_down"])])
    gw["w_ffn_gate"] = _matmul_tn(dfg, h2, tn_d, "dw_ffn_gate", BF16)
    gw["w_ffn_up"] = _matmul_tn(dfu, h2, tn_d, "dw_ffn_up", BF16)
    add_pairs(["w_ffn_down"])
    dza, dzb, dya, dyb, gw["w_o"], gw["w_a_out"], gw["w_b_out"] = carry(
        _merge_bwd, dx1, merged, ma, mb, za, zb, y_a, h, full["w_o"], full["w_a_out"], full["w_b_out"],
        jobs=[chips(["w_ffn_down"]), swap(["w_ffn_gate", "w_ffn_up"])])
    add_pairs(["w_ffn_gate", "w_ffn_up"])
    own = {}
    du_b, dw_r, dw_i, vec_lru = carry(
        _lru_bwd, dyb, xc, hprev, u_b, conv_w, wr_blk, row(small["b_r"]), wi_blk, row(small["b_i"]),
        row(small["lru_lambda"]), hd, jobs=[chips(["w_ffn_gate"]), swap(["w_o", "w_a_out", "w_b_out"])])
    add_pairs(["w_o", "w_a_out", "w_b_out"])
    du_a, lam_r, lam_i, dy16, gw["w_glu"], vec_sa, vec_gn = carry(
        _s5_bwd, dya, y, sr, si, u_a, full["w_glu"], row(small["b_glu"]), cre_blk, cimn_blk, bbr_blk, bbi_blk, ar_row,
        ai_row, d_row, jobs=[chips(["w_ffn_up"]), chips(["w_o", "w_a_out", "w_b_out"])])
    smalls = {"vec_tail": vec_tail, "vec_ffn": vec_ffn, "vec_lru": vec_lru, "vec_sa": vec_sa, "vec_gn": vec_gn,
              "dw_r": dw_r, "dw_i": dw_i, "loss": loss_blk}
    everyones = {}

    def gather_smalls(keys):
        return (_Gather([smalls[k] for k in keys]), keys, everyones) if distributed else None

    smalls["dbb_re"], smalls["dbb_im"], smalls["dc_re"], smalls["dc_imn"] = carry(
        _s5_param_grads, lam_r, lam_i, sr, si, u_a, dy16, pch, n, jobs=[swap(["w_glu"]), gather_smalls(list(smalls))])
    add_pairs(["w_glu"])
    dz = [du_a, du_b, dza, dzb]
    grad_x, h0, smalls["vec_mix"], smalls["vec_bin"] = _inproj_bwd(
        dz, x, dx1, row(small["g_mix"]), full["w_in"])
    shapes = {k: a.shape for k, a in smalls.items()}
    gw["w_in"] = carry(_dw_from_parts, dz, h0, tn_d, "dw_in",
                       jobs=[chips(["w_glu"]),
                             gather_smalls(["dbb_re", "dbb_im", "dc_re", "dc_imn", "vec_mix", "vec_bin"])])
    smalls.update(everyones)
    if distributed:
        got["w_in"] = gw["w_in"]
        gw = got
    return grad_x, gw, smalls, shapes, own


def _disc_inputs(small):
    g, n, pch = small["s5_b_re"].shape
    srcs = (small["lam_re"], small["lam_im"], jnp.repeat(small["log_dt"], n))
    return {"rows": [a.reshape(1, g * n) for a in srcs], "cols": [a.reshape(g * n, 1) for a in srcs],
            "b_re": small["s5_b_re"].reshape(g * n, pch), "b_im": small["s5_b_im"].reshape(g * n, pch)}


def kernel(x, p, g_mix, w_in, b_in, lam_re, lam_im, log_dt, s5_b_re, s5_b_im, s5_c_re, s5_c_im, s5_d, w_glu, b_glu, conv_w, conv_b, w_r, b_r, w_i, b_i, lru_lambda, w_a_out, w_b_out, w_o, g_ffn, w_ffn_gate, w_ffn_up, w_ffn_down, g_ple_gate, w_ple_gate, b_ple_gate, w_ple, g_ple, g_final, loss_target, m_g_mix, m_w_in, m_b_in, m_lam_re, m_lam_im, m_log_dt, m_s5_b_re, m_s5_b_im, m_s5_c_re, m_s5_c_im, m_s5_d, m_w_glu, m_b_glu, m_conv_w, m_conv_b, m_w_r, m_b_r, m_w_i, m_b_i, m_lru_lambda, m_w_a_out, m_w_b_out, m_w_o, m_g_ffn, m_w_ffn_gate, m_w_ffn_up, m_w_ffn_down, m_g_ple_gate, m_w_ple_gate, m_b_ple_gate, m_w_ple, m_g_ple, m_g_final, v_g_mix, v_w_in, v_b_in, v_lam_re, v_lam_im, v_log_dt, v_s5_b_re, v_s5_b_im, v_s5_c_re, v_s5_c_im, v_s5_d, v_w_glu, v_b_glu, v_conv_w, v_conv_b, v_w_r, v_b_r, v_w_i, v_b_i, v_lru_lambda, v_w_a_out, v_w_b_out, v_w_o, v_g_ffn, v_w_ffn_gate, v_w_ffn_up, v_w_ffn_down, v_g_ple_gate, v_w_ple_gate, v_b_ple_gate, v_w_ple, v_g_ple, v_g_final):
    given = dict(locals())
    wts = {k: given[k] for k in WEIGHTS}
    moms = {k: given["m_" + k] for k in WEIGHTS}
    vels = {k: given["v_" + k] for k in WEIGHTS}

    def drop_depth(k, a):
        return a if k == "g_final" else a[0]

    small = {k: drop_depth(k, wts[k]) for k in SMALL}
    shard = {k: wts[k][0] for k in SHARDED}
    names = list(SHARDED)

    def wire(k):
        if k == "conv_w":
            return shard[k]
        return (shard[k].T if k in TRANSPOSED else shard[k]).astype(BF16)

    disc = _disc_inputs(small)
    grad_x, parts, smalls, shapes, own = _local_step(x[0], p[0, 0], loss_target[0], {k: wire(k) for k in names}, small,
                                                     disc)

    (pair_in,) = _pair_add([parts["w_in"]], _alone(_PairSwap([parts["w_in"]]), "swap_last"), "pair_add_w_in")
    send_sems, recv_sems, pair_in, landing, token = _chips_start(pair_in, "w_in")
    ordered = (token,)

    g_small = _small_reduce(smalls, shapes, *disc["cols"], disc["b_re"], disc["b_im"], small["s5_b_re"].shape[0],
                            after=ordered)
    loss = g_small.pop("loss")[0, 0]
    cols = shard["conv_w"].shape[1]
    mine = _index((lax.axis_index("x"), lax.axis_index("y"), lax.axis_index("c")))
    parts["conv_w"] = lax.dynamic_slice_in_dim(g_small.pop("conv_w"), mine * cols, cols, axis=1)[None]

    def view(k, a):
        a = a.reshape((1, -1) if k == "g_final" else wts[k].shape)
        return jnp.swapaxes(a, -1, -2) if k in NARROW_LAST else a

    def unview(k, a):
        return (jnp.swapaxes(a, -1, -2) if k in NARROW_LAST else a).reshape(wts[k].shape)

    slots = _adamw_small([view(k, wts[k]) for k in SMALL], [view(k, g_small[k]) for k in SMALL],
                         [view(k, moms[k]) for k in SMALL], [view(k, vels[k]) for k in SMALL])
    small_out = [dict(zip(SMALL, [unview(k, a) for k, a in zip(SMALL, slot)])) for slot in slots]

    big_out = {}
    between = [slots[0][0]]
    for group in sorted(ADAMW_GROUPS, key=lambda g: g[0] == "w_in"):
        flip = group[0] in LONG_AXIS_MINOR
        look = (lambda a: a.T) if flip else (lambda a: a)
        last = group[0] == "w_in"
        if last:
            own["w_in"], parts["w_in"] = _chips_wait(send_sems, recv_sems, pair_in, landing, between, "w_in")
        res = _adamw([parts[k] for k in group], [look(shard[k]) for k in group], [look(moms[k][0]) for k in group],
                     [look(vels[k][0]) for k in group], "adamw_" + group[0],
                     transposed=group[0] in TRANSPOSED and not flip,
                     own=[own[k] for k in group] if group[0] in own else None, after=() if last else ordered)
        between.append(res[0][0])
        for k, outs4 in zip(group, res):
            big_out[k] = [look(a) for a in outs4]

    outs = [loss, grad_x[None]]
    for slot in range(4):
        for k in WEIGHTS:
            if k in SHARDED:
                outs.append(big_out[k][slot][None])
            else:
                outs.append(small_out[slot][k])
    return tuple(outs)
```

```python
import math

import jax
import jax.numpy as jnp
from jax import lax
from jax.experimental import pallas as pl
from jax.experimental.pallas import tpu as pltpu

F32 = jnp.float32
BF16 = jnp.bfloat16

EPS = 1e-6
LRU_C = 8.0
CONV_WIDTH = 4
ADAM_LR = 0.001
ADAM_B1 = 0.9
ADAM_B2 = 0.999
ADAM_EPS = 1e-08
ADAM_WD = 0.01
ADAM_STEP = 10

N_DEV = 8
MESH = pl.DeviceIdType.MESH
SUBLANES = 8
LANES = 128
VMEM_LIMIT = 56 * 1024 * 1024
TOKEN_TILE = 256
TIME_CHUNK = 256
S5_SLAB = 128
LRU_SLAB = 256


def _dot(a, b):
    return jnp.dot(a.astype(BF16), b.astype(BF16), preferred_element_type=F32)


def _dot_nt(a, b):
    return lax.dot_general(a.astype(BF16), b.astype(BF16), (((1,), (1,)), ((), ())), preferred_element_type=F32)


def _dot_tn(a, b):
    return lax.dot_general(a.astype(BF16), b.astype(BF16), (((0,), (0,)), ((), ())), preferred_element_type=F32)


def _sigmoid(x):
    return jax.nn.sigmoid(x)


def _rms_stats(x):
    r = lax.rsqrt(jnp.mean(x * x, axis=-1, keepdims=True) + EPS)
    return x * r, r


def _rms_bwd(dy, xhat, r, g):
    dxn = dy * g
    dx = r * (dxn - xhat * jnp.mean(dxn * xhat, axis=-1, keepdims=True))
    return dx, dy * xhat


def _rowsum(v):
    return jnp.sum(v, axis=0, keepdims=True)


def _expm1(x):
    u = jnp.exp(x)
    um1 = u - 1.0
    safe = jnp.where(um1 == 0.0, 1.0, jnp.log(u))
    return jnp.where(um1 == 0.0, x, um1 * x / safe)


def _softplus(x):
    e = jnp.exp(-jnp.abs(x))
    u = 1.0 + e
    um1 = u - 1.0
    safe = jnp.where(um1 == 0.0, 1.0, um1)
    log1p_e = jnp.where(um1 == 0.0, e, jnp.log(u) * e / safe)
    return jnp.maximum(x, 0.0) + log1p_e


_GELU_K = math.sqrt(2.0 / math.pi)
_GELU_C = 0.044715


def _gelu(x):
    return 0.5 * x * (1.0 + jnp.tanh(_GELU_K * (x + _GELU_C * x * x * x)))


def _gelu_grad(x):
    th = jnp.tanh(_GELU_K * (x + _GELU_C * x * x * x))
    return 0.5 * (1.0 + th) + 0.5 * x * (1.0 - th * th) * _GELU_K * (1.0 + 3.0 * _GELU_C * x * x)


def _params(*sem):
    return pltpu.CompilerParams(dimension_semantics=sem, vmem_limit_bytes=VMEM_LIMIT)


def _rows(tm, n):
    return pl.BlockSpec((tm, n), lambda i: (i, 0))


def _rows_rev(tm, n, steps):
    return pl.BlockSpec((tm, n), lambda i: (steps - 1 - i, 0))


def _whole(shape):
    nd = len(shape)
    return pl.BlockSpec(shape, lambda i: (0,) * nd, pipeline_mode=pl.Buffered(1))


def _acc(shape):
    nd = len(shape)
    return pl.BlockSpec(shape, lambda i: (0,) * nd)


def _zero_on_first(*refs):
    @pl.when(pl.program_id(0) == 0)
    def _():
        for r in refs:
            r[...] = jnp.zeros_like(r)


def _inproj_fwd(x, g_mix, w_in_t, b_in, widths, comm=None):
    t, d = x.shape
    n = w_in_t.shape[0]
    tm = min(TOKEN_TILE, t)
    offs = [sum(widths[:i]) for i in range(len(widths) + 1)]

    def body(x_ref, g_ref, w_ref, b_ref, *outs):
        xhat, _ = _rms_stats(x_ref[...])
        h = (xhat * g_ref[...]).astype(BF16)
        for k, o_ref in enumerate(outs):
            lo, hi = offs[k], offs[k + 1]
            o_ref[...] = _dot_nt(h, w_ref[lo:hi, :]) + b_ref[:, lo:hi]

    return _run(
        body, comm, name="inproj_fwd", grid=(t // tm,),
        out_shape=[jax.ShapeDtypeStruct((t, w), F32) for w in widths],
        in_specs=[_rows(tm, d), _whole((1, d)), _whole((n, d)), _whole((1, n))],
        out_specs=[_rows(tm, w) for w in widths],
        semantics="parallel",
    )(x, g_mix, w_in_t, b_in)


def _s5_fwd(u, bbr_blk, bbi_blk, ar, ai, cre_blk, cimn_blk, d_skip, w_glu, b_glu):
    t, sa = u.shape
    ns, _, sw = bbr_blk.shape
    gn = ns * sw
    tc = min(TIME_CHUNK, t)

    def body(u_ref, bbr_ref, bbi_ref, ar_ref, ai_ref, cre_ref, cim_ref, d_ref, wg_ref, bg_ref,
             sr_ref, si_ref, y_ref, ya_ref, cr_s, ci_s, sr_s, si_s):
        _zero_on_first(cr_s, ci_s)
        uv = u_ref[...]
        ub = uv.astype(BF16)
        for m in range(ns):
            um = ub[:, m * S5_SLAB:(m + 1) * S5_SLAB]
            sr_s[:, m * sw:(m + 1) * sw] = _dot(um, bbr_ref[m])
            si_s[:, m * sw:(m + 1) * sw] = _dot(um, bbi_ref[m])
        a_r = ar_ref[...]
        a_i = ai_ref[...]

        def step(row, carry):
            c_r, c_i = carry
            at = pl.ds(row, 1)
            n_r = a_r * c_r - a_i * c_i + sr_s[at, :]
            n_i = a_r * c_i + a_i * c_r + si_s[at, :]
            sr_s[at, :] = n_r
            si_s[at, :] = n_i
            return n_r, n_i

        c_r, c_i = lax.fori_loop(0, tc, step, (cr_s[0:1, :], ci_s[0:1, :]), unroll=8)
        cr_s[0:1, :] = c_r
        ci_s[0:1, :] = c_i
        for m in range(ns):
            states, chans = slice(m * sw, (m + 1) * sw), slice(m * S5_SLAB, (m + 1) * S5_SLAB)
            s_r, s_i = sr_s[:, states].astype(BF16), si_s[:, states].astype(BF16)
            sr_ref[:, states] = s_r.astype(sr_ref.dtype)
            si_ref[:, states] = s_i.astype(si_ref.dtype)
            y_ref[:, chans] = _dot(s_r, cre_ref[m]) + _dot(s_i, cim_ref[m]) + d_ref[:, chans] * uv[:, chans]
        y = y_ref[...]
        zz = _gelu(y)
        q = _dot(zz, wg_ref[...]) + bg_ref[...]
        ya_ref[...] = zz * _sigmoid(q)

    return dict(
        body=body, steps=t // tc, args=(u, bbr_blk, bbi_blk, ar, ai, cre_blk, cimn_blk, d_skip, w_glu, b_glu),
        out_shape=[jax.ShapeDtypeStruct((t, gn), BF16), jax.ShapeDtypeStruct((t, gn), BF16),
                   jax.ShapeDtypeStruct((t, sa), F32), jax.ShapeDtypeStruct((t, sa), F32)],
        in_specs=[_rows(tc, sa), _whole(bbr_blk.shape), _whole(bbi_blk.shape), _whole((1, gn)), _whole((1, gn)),
                  _whole(cre_blk.shape), _whole(cimn_blk.shape), _whole((1, sa)), _whole((sa, sa)), _whole((1, sa))],
        out_specs=[_rows(tc, gn), _rows(tc, gn), _rows(tc, sa), _rows(tc, sa)],
        scratch=[pltpu.VMEM((SUBLANES, gn), F32), pltpu.VMEM((SUBLANES, gn), F32),
                 pltpu.VMEM((tc, gn), F32), pltpu.VMEM((tc, gn), F32)])


def _stages(stages, name, comm=None):
    counts = [[len(s[k]) for s in stages] for k in ("args", "out_shape", "scratch")]

    def body(*refs):
        groups, pos = [], 0
        for kind in counts:
            per_stage = []
            for c in kind:
                per_stage.append(refs[pos:pos + c])
                pos += c
            groups.append(per_stage)
        for s, ins, outs, scratch in zip(stages, *groups):
            s["body"](*ins, *outs, *scratch)

    res = _run(
        body, comm, name=name, grid=(stages[0]["steps"],),
        out_shape=[o for s in stages for o in s["out_shape"]], in_specs=[i for s in stages for i in s["in_specs"]],
        out_specs=[o for s in stages for o in s["out_specs"]], scratch_shapes=[c for s in stages for c in s["scratch"]],
        semantics="arbitrary",
    )(*[a for s in stages for a in s["args"]])
    extra = None
    if comm is not None:
        res, extra = res
    per_stage, pos = [], 0
    for c in counts[1]:
        per_stage.append(list(res[pos:pos + c]))
        pos += c
    return per_stage if comm is None else (per_stage, extra)


def _slab_dot(x, w_ref, transposed=False):
    dot = _dot_nt if transposed else _dot
    xb = x.astype(BF16)
    return jnp.concatenate([dot(xb[:, j * LRU_SLAB:(j + 1) * LRU_SLAB], w_ref[j]) for j in range(w_ref.shape[0])], axis=1)


def _lru_gates(xc, wr_ref, br_ref, wi_ref, bi_ref, lam_ref):
    r = _sigmoid(_slab_dot(xc, wr_ref) + br_ref[...])
    ig = _sigmoid(_slab_dot(xc, wi_ref) + bi_ref[...])
    sp = _softplus(-lam_ref[...])
    log_a = (-LRU_C * r) * sp
    return r, ig, sp, log_a


def _lru_fwd(u, conv_w, conv_b, wr_blk, b_r, wi_blk, b_i, lru_lambda):
    t, w = u.shape
    tc = min(TIME_CHUNK, t)
    halo = SUBLANES

    def body(u_ref, cw_ref, cb_ref, wr_ref, br_ref, wi_ref, bi_ref, lam_ref,
             xc_ref, h_ref, hp_ref, ext_s, a_s, carry_s):
        @pl.when(pl.program_id(0) == 0)
        def _():
            ext_s[0:halo, :] = jnp.zeros((halo, w), F32)
            carry_s[...] = jnp.zeros_like(carry_s)

        ext_s[halo:halo + tc, :] = u_ref[...]
        xc = cb_ref[...]
        for k in range(CONV_WIDTH):
            off = halo - (CONV_WIDTH - 1) + k
            xc = xc + cw_ref[k:k + 1, :] * ext_s[off:off + tc, :]
        ext_s[0:halo, :] = ext_s[tc:tc + halo, :]
        xc_ref[...] = xc
        r, ig, sp, log_a = _lru_gates(xc, wr_ref, br_ref, wi_ref, bi_ref, lam_ref)
        a_s[...] = jnp.exp(log_a)
        h_ref[...] = jnp.sqrt(-_expm1(2.0 * log_a)) * ig * xc

        def step(row, carry):
            at = pl.ds(row, 1)
            hp_ref[at, :] = carry
            nxt = a_s[at, :] * carry + h_ref[at, :]
            h_ref[at, :] = nxt
            return nxt

        carry_s[0:1, :] = lax.fori_loop(0, tc, step, carry_s[0:1, :], unroll=8)

    return dict(
        body=body, steps=t // tc, args=(u, conv_w, conv_b, wr_blk, b_r, wi_blk, b_i, lru_lambda),
        out_shape=[jax.ShapeDtypeStruct((t, w), F32)] * 3,
        in_specs=[_rows(tc, w), _whole((CONV_WIDTH, w)), _whole((1, w)), _whole(wr_blk.shape), _whole((1, w)),
                  _whole(wi_blk.shape), _whole((1, w)), _whole((1, w))],
        out_specs=[_rows(tc, w)] * 3,
        scratch=[pltpu.VMEM((halo + tc, w), F32), pltpu.VMEM((tc, w), F32), pltpu.VMEM((SUBLANES, w), F32)])


def _merge_ffn_up_fwd(y_a, h, za, zb, x, w_a_out_t, w_b_out, w_o, g_ffn, w_gate_t, w_up_t, comm=None):
    t, d = x.shape
    sa, lw = y_a.shape[1], h.shape[1]
    f = w_gate_t.shape[0]
    tm = min(TOKEN_TILE, t)

    def body(ya_ref, h_ref, za_ref, zb_ref, x_ref, wa_ref, wb_ref, wo_ref, g_ref, wg_ref, wu_ref,
             x1_ref, mg_ref, ma_ref, mb_ref, fg_ref, fu_ref):
        ma = _dot_nt(ya_ref[...], wa_ref[...])
        mb = _dot(h_ref[...], wb_ref[...])
        merged = _sigmoid(za_ref[...]) * ma + _sigmoid(zb_ref[...]) * mb
        ma_ref[...] = ma.astype(ma_ref.dtype)
        mb_ref[...] = mb.astype(mb_ref.dtype)
        mg_ref[...] = merged.astype(mg_ref.dtype)
        x1 = x_ref[...] + _dot(merged, wo_ref[...])
        x1_ref[...] = x1
        xhat, _ = _rms_stats(x1)
        h2 = (xhat * g_ref[...]).astype(BF16)
        fg_ref[...] = _dot_nt(h2, wg_ref[...]).astype(fg_ref.dtype)
        fu_ref[...] = _dot_nt(h2, wu_ref[...]).astype(fu_ref.dtype)

    return _run(
        body, comm, name="merge_ffn_up_fwd", grid=(t // tm,),
        out_shape=[jax.ShapeDtypeStruct((t, d), F32), jax.ShapeDtypeStruct((t, d), BF16),
                   jax.ShapeDtypeStruct((t, d), BF16), jax.ShapeDtypeStruct((t, d), BF16),
                   jax.ShapeDtypeStruct((t, f), BF16), jax.ShapeDtypeStruct((t, f), BF16)],
        in_specs=[_rows(tm, sa), _rows(tm, lw), _rows(tm, d), _rows(tm, d), _rows(tm, d),
                  _whole((d, sa)), _whole((lw, d)), _whole((d, d)), _whole((1, d)), _whole((f, d)), _whole((f, d))],
        out_specs=[_rows(tm, d)] * 4 + [_rows(tm, f)] * 2,
        semantics="parallel",
    )(y_a, h, za, zb, x, w_a_out_t, w_b_out, w_o, g_ffn, w_gate_t, w_up_t)


def _ffn_down_fwd(fg, fu, x1, w_down, comm=None):
    t, d = x1.shape
    f = fg.shape[1]
    tm = min(TOKEN_TILE, t)

    def body(fg_ref, fu_ref, x_ref, wd_ref, x2_ref):
        fgv = fg_ref[...].astype(F32)
        act = fgv * _sigmoid(fgv) * fu_ref[...].astype(F32)
        x2_ref[...] = x_ref[...] + _dot(act, wd_ref[...])

    return _run(
        body, comm, name="ffn_down_fwd", grid=(t // tm,),
        out_shape=jax.ShapeDtypeStruct((t, d), F32),
        in_specs=[_rows(tm, f), _rows(tm, f), _rows(tm, d), _whole((f, d))],
        out_specs=_rows(tm, d),
        semantics="parallel",
    )(fg, fu, x1, w_down)


def _store_on_last(pairs):
    @pl.when(pl.program_id(0) == pl.num_programs(0) - 1)
    def _():
        for acc, out in pairs:
            out[...] = acc[...].astype(out.dtype)


def _tail_fwd_bwd(x2, p, target, g_pg, w_pg, b_pg, w_ple_t, g_ple, g_final):
    t, d = x2.shape
    pd = p.shape[1]
    tm = min(TOKEN_TILE, t)

    def body(x2_ref, p_ref, tg_ref, gpg_ref, wpg_ref, bpg_ref, wple_ref, gple_ref, gfin_ref,
             dx2_ref, loss_ref, dwpg_out, dwple_out, vec_ref, dwpg_ref, dwple_ref):
        _zero_on_first(loss_ref, dwpg_ref, dwple_ref, vec_ref)
        x2v = x2_ref[...]
        xh2, r2 = _rms_stats(x2v)
        h3 = xh2 * gpg_ref[...]
        gp = _sigmoid(_dot(h3, wpg_ref[...]) + bpg_ref[...])
        pe = _dot_nt(p_ref[...], wple_ref[...])
        peh, r3 = _rms_stats(pe)
        e = peh * gple_ref[...]
        x3 = x2v + gp * e
        xh3, r4 = _rms_stats(x3)
        diff = xh3 * gfin_ref[...] - tg_ref[...]
        loss_ref[...] += 0.5 * jnp.sum(jnp.mean(diff * diff, axis=-1, keepdims=True))
        dy = diff * (1.0 / d)
        dx3, dgfin = _rms_bwd(dy, xh3, r4, gfin_ref[...])
        d_gp = dx3 * e
        d_e = dx3 * gp
        dpe, dgple = _rms_bwd(d_e, peh, r3, gple_ref[...])
        dwple_ref[...] += _dot_tn(dpe, p_ref[...])
        dpre = d_gp * gp * (1.0 - gp)
        dwpg_ref[...] += _dot_tn(h3, dpre)
        dh3 = _dot_nt(dpre, wpg_ref[...])
        dx2n, dgpg = _rms_bwd(dh3, xh2, r2, gpg_ref[...])
        dx2_ref[...] = dx3 + dx2n
        vec_ref[0:1, :] += _rowsum(dpre)
        vec_ref[1:2, :] += _rowsum(dgpg)
        vec_ref[2:3, :] += _rowsum(dgple)
        vec_ref[3:4, :] += _rowsum(dgfin)
        _store_on_last([(dwpg_ref, dwpg_out), (dwple_ref, dwple_out)])

    return pl.pallas_call(
        body, name="tail_fwd_bwd", grid=(t // tm,),
        out_shape=[jax.ShapeDtypeStruct((t, d), F32), jax.ShapeDtypeStruct((SUBLANES, LANES), F32),
                   jax.ShapeDtypeStruct((d, d), BF16), jax.ShapeDtypeStruct((d, pd), BF16),
                   jax.ShapeDtypeStruct((SUBLANES, d), F32)],
        in_specs=[_rows(tm, d), _rows(tm, pd), _rows(tm, d), _whole((1, d)), _whole((d, d)), _whole((1, d)),
                  _whole((d, pd)), _whole((1, d)), _whole((1, d))],
        out_specs=[_rows(tm, d), _acc((SUBLANES, LANES)), _acc((d, d)), _acc((d, pd)), _acc((SUBLANES, d))],
        scratch_shapes=[pltpu.VMEM((d, d), F32), pltpu.VMEM((d, pd), F32)],
        compiler_params=_params("arbitrary"),
    )(x2, p, target, g_pg, w_pg, b_pg, w_ple_t, g_ple, g_final)


def _ffn_bwd_a(dx2, fg, fu, w_down, comm=None):
    t, d = dx2.shape
    f = fg.shape[1]
    tm = min(TOKEN_TILE, t)

    def body(dx_ref, fg_ref, fu_ref, wd_ref, dfg_ref, dfu_ref, act_ref):
        dact = _dot_nt(dx_ref[...], wd_ref[...])
        fgv = fg_ref[...].astype(F32)
        fuv = fu_ref[...].astype(F32)
        sg = _sigmoid(fgv)
        silu = fgv * sg
        dfu_ref[...] = (dact * silu).astype(dfu_ref.dtype)
        dfg_ref[...] = (dact * fuv * (sg * (1.0 + fgv * (1.0 - sg)))).astype(dfg_ref.dtype)
        act_ref[...] = (silu * fuv).astype(act_ref.dtype)

    return _run(
        body, comm, name="ffn_bwd_a", grid=(t // tm,),
        out_shape=[jax.ShapeDtypeStruct((t, f), BF16)] * 3,
        in_specs=[_rows(tm, d), _rows(tm, f), _rows(tm, f), _whole((f, d))],
        out_specs=[_rows(tm, f)] * 3,
        semantics="parallel",
    )(dx2, fg, fu, w_down)


def _ffn_bwd_b(dfg, dfu, x1, dx2, g_ffn, w_gate_t, w_up_t, comm=None):
    t, d = x1.shape
    f = dfg.shape[1]
    tm = min(TOKEN_TILE, t)

    def body(dfg_ref, dfu_ref, x_ref, dx2_ref, g_ref, wg_ref, wu_ref, dx1_ref, h2_ref, vec_ref):
        _zero_on_first(vec_ref)
        dh2 = _dot(dfg_ref[...], wg_ref[...]) + _dot(dfu_ref[...], wu_ref[...])
        xhat, r = _rms_stats(x_ref[...])
        h2_ref[...] = (xhat * g_ref[...]).astype(h2_ref.dtype)
        dxn, dg = _rms_bwd(dh2, xhat, r, g_ref[...])
        dx1_ref[...] = dx2_ref[...] + dxn
        vec_ref[0:1, :] += _rowsum(dg)

    return _run(
        body, comm, name="ffn_bwd_b", grid=(t // tm,),
        out_shape=[jax.ShapeDtypeStruct((t, d), F32), jax.ShapeDtypeStruct((t, d), BF16),
                   jax.ShapeDtypeStruct((SUBLANES, d), F32)],
        in_specs=[_rows(tm, f), _rows(tm, f), _rows(tm, d), _rows(tm, d), _whole((1, d)), _whole((f, d)), _whole((f, d))],
        out_specs=[_rows(tm, d), _rows(tm, d), _acc((SUBLANES, d))],
        semantics="arbitrary",
    )(dfg, dfu, x1, dx2, g_ffn, w_gate_t, w_up_t)


def _matmul_tn(a, b, tn, name, dtype=F32):
    t, k = a.shape
    n = b.shape[1]

    def body(a_ref, b_ref, o_ref):
        o_ref[...] = _dot_tn(a_ref[...], b_ref[...]).astype(o_ref.dtype)

    return _run(
        body, None, name=name, grid=(n // tn,),
        out_shape=jax.ShapeDtypeStruct((k, n), dtype),
        in_specs=[_whole((t, k)), pl.BlockSpec((t, tn), lambda j: (0, j))],
        out_specs=pl.BlockSpec((k, tn), lambda j: (0, j)),
        semantics="parallel",
    )(a, b)


def _merge_bwd(dx1, merged, ma, mb, za, zb, y_a, h, w_o, w_a_out_t, w_b_out, comm=None):
    t, d = dx1.shape
    sa, lw = y_a.shape[1], h.shape[1]
    tm = min(TOKEN_TILE, t)

    def body(dx1_ref, mg_ref, ma_ref, mb_ref, za_ref, zb_ref, ya_ref, h_ref, wo_ref, wa_ref, wb_ref,
             dza_ref, dzb_ref, dya_ref, dyb_ref, dwo_out, dwa_out, dwb_out, dwo_ref, dwa_ref, dwb_ref):
        _zero_on_first(dwo_ref, dwa_ref, dwb_ref)
        dx1v = dx1_ref[...].astype(BF16)
        dmg = _dot_nt(dx1v, wo_ref[...])
        ga = _sigmoid(za_ref[...])
        gb = _sigmoid(zb_ref[...])
        dza_ref[...] = (dmg * ma_ref[...].astype(F32) * ga * (1.0 - ga)).astype(dza_ref.dtype)
        dzb_ref[...] = (dmg * mb_ref[...].astype(F32) * gb * (1.0 - gb)).astype(dzb_ref.dtype)
        dma = (dmg * ga).astype(BF16)
        dmb = (dmg * gb).astype(BF16)
        dya_ref[...] = _dot(dma, wa_ref[...])
        dyb_ref[...] = _dot_nt(dmb, wb_ref[...])
        dwo_ref[...] += _dot_tn(mg_ref[...], dx1v)
        dwa_ref[...] += _dot_tn(dma, ya_ref[...])
        dwb_ref[...] += _dot_tn(h_ref[...], dmb)
        _store_on_last([(dwo_ref, dwo_out), (dwa_ref, dwa_out), (dwb_ref, dwb_out)])

    return _run(
        body, comm, name="merge_bwd", grid=(t // tm,),
        out_shape=[jax.ShapeDtypeStruct((t, d), BF16), jax.ShapeDtypeStruct((t, d), BF16),
                   jax.ShapeDtypeStruct((t, sa), F32), jax.ShapeDtypeStruct((t, lw), F32),
                   jax.ShapeDtypeStruct((d, d), BF16), jax.ShapeDtypeStruct((d, sa), BF16),
                   jax.ShapeDtypeStruct((lw, d), BF16)],
        in_specs=[_rows(tm, d), _rows(tm, d), _rows(tm, d), _rows(tm, d), _rows(tm, d), _rows(tm, d),
                  _rows(tm, sa), _rows(tm, lw), _whole((d, d)), _whole((d, sa)), _whole((lw, d))],
        out_specs=[_rows(tm, d), _rows(tm, d), _rows(tm, sa), _rows(tm, lw), _acc((d, d)), _acc((d, sa)), _acc((lw, d))],
        scratch_shapes=[pltpu.VMEM((d, d), F32), pltpu.VMEM((d, sa), F32), pltpu.VMEM((lw, d), F32)],
        semantics="arbitrary",
    )(dx1, merged, ma, mb, za, zb, y_a, h, w_o, w_a_out_t, w_b_out)


def _fold_diag_blocks(dense, row_group, col_group):
    r, c = dense.shape
    rows = lax.broadcasted_iota(jnp.int32, (r, c), 0)
    cols = lax.broadcasted_iota(jnp.int32, (r, c), 1)
    kept = jnp.where(rows // row_group == cols // col_group, dense, 0.0)
    pick = (lax.broadcasted_iota(jnp.int32, (row_group, r), 0)
            == lax.broadcasted_iota(jnp.int32, (row_group, r), 1) % row_group).astype(F32)
    return jnp.dot(pick, kept, preferred_element_type=F32, precision=lax.Precision.HIGHEST)


def _lru_bwd(dh, xc, hprev, u, conv_w, wr_blk, b_r, wi_blk, b_i, lru_lambda, head_dim, comm=None):
    t, w = dh.shape
    tc = min(TIME_CHUNK, t)
    steps = t // tc
    halo = SUBLANES
    sub_per_chunk = tc // halo
    slabs = w // LRU_SLAB

    def body(dh_ref, xc_ref, hp_ref, u_ref, uh_ref, cw_ref, wr_ref, br_ref, wi_ref, bi_ref, lam_ref,
             du_ref, dwr_out, dwi_out, vec_ref, lam_s, a_s, dxc_s, uext_s, carry_s, dwr_ref, dwi_ref):
        chunk = steps - 1 - pl.program_id(0)

        @pl.when(pl.program_id(0) == 0)
        def _():
            carry_s[...] = jnp.zeros_like(carry_s)
            dxc_s[tc:tc + halo, :] = jnp.zeros((halo, w), F32)
            dwr_ref[...] = jnp.zeros_like(dwr_ref)
            dwi_ref[...] = jnp.zeros_like(dwi_ref)
            vec_ref[...] = jnp.zeros_like(vec_ref)

        xc = xc_ref[...]
        r, ig, sp, log_a = _lru_gates(xc, wr_ref, br_ref, wi_ref, bi_ref, lam_ref)
        a = jnp.exp(log_a)
        a_s[...] = a

        def step(i, q):
            at = pl.ds(tc - 1 - i, 1)
            lam_row = dh_ref[at, :] + q
            lam_s[at, :] = lam_row
            return a_s[at, :] * lam_row

        carry_s[0:1, :] = lax.fori_loop(0, tc, step, carry_s[0:1, :], unroll=8)
        lam = lam_s[...]
        mult = jnp.sqrt(-_expm1(2.0 * log_a))
        d_log_a = lam * hp_ref[...] * a - (lam * ig * xc) * (a * a) / mult
        d_ig = lam * mult * xc
        dpre_r = (d_log_a * (-LRU_C * sp)) * r * (1.0 - r)
        dpre_i = d_ig * ig * (1.0 - ig)
        dxc = lam * mult * ig + _slab_dot(dpre_r, wr_ref, transposed=True) + _slab_dot(dpre_i, wi_ref, transposed=True)
        xcb, drb, dib = xc.astype(BF16), dpre_r.astype(BF16), dpre_i.astype(BF16)
        for j in range(slabs):
            cols = slice(j * LRU_SLAB, (j + 1) * LRU_SLAB)
            dwr_ref[j] += _dot_tn(drb[:, cols], xcb[:, cols])
            dwi_ref[j] += _dot_tn(dib[:, cols], xcb[:, cols])
        vec_ref[0:1, :] += _rowsum(dxc)
        vec_ref[1:2, :] += _rowsum(dpre_r)
        vec_ref[2:3, :] += _rowsum(dpre_i)
        vec_ref[3:4, :] += _rowsum(d_log_a * (-LRU_C * r)) * (-_sigmoid(-lam_ref[...]))
        dxc_s[0:tc, :] = dxc
        du = cw_ref[CONV_WIDTH - 1:CONV_WIDTH, :] * dxc
        for k in range(CONV_WIDTH - 1):
            off = CONV_WIDTH - 1 - k
            du = du + cw_ref[k:k + 1, :] * dxc_s[off:off + tc, :]
        du_ref[...] = du.astype(du_ref.dtype)
        dxc_s[tc:tc + halo, :] = dxc_s[0:halo, :]
        uext_s[0:halo, :] = jnp.where(chunk > 0, uh_ref[...], 0.0)
        uext_s[halo:halo + tc, :] = u_ref[...]
        for k in range(CONV_WIDTH):
            off = halo - (CONV_WIDTH - 1) + k
            vec_ref[4 + k:5 + k, :] += _rowsum(dxc * uext_s[off:off + tc, :])

        @pl.when(pl.program_id(0) == steps - 1)
        def _():
            for j in range(slabs):
                cols = slice(j * LRU_SLAB, (j + 1) * LRU_SLAB)
                dwr_out[:, cols] = _fold_diag_blocks(dwr_ref[j], head_dim, head_dim).astype(dwr_out.dtype)
                dwi_out[:, cols] = _fold_diag_blocks(dwi_ref[j], head_dim, head_dim).astype(dwi_out.dtype)

    halo_spec = pl.BlockSpec((halo, w), lambda i: (jnp.maximum((steps - 1 - i) * sub_per_chunk - 1, 0), 0))
    return _run(
        body, comm, name="lru_bwd", grid=(steps,),
        out_shape=[jax.ShapeDtypeStruct((t, w), BF16), jax.ShapeDtypeStruct((head_dim, w), BF16),
                   jax.ShapeDtypeStruct((head_dim, w), BF16), jax.ShapeDtypeStruct((SUBLANES, w), F32)],
        in_specs=[_rows_rev(tc, w, steps)] * 4 + [halo_spec, _whole((CONV_WIDTH, w)), _whole(wr_blk.shape),
                                                  _whole((1, w)), _whole(wi_blk.shape), _whole((1, w)), _whole((1, w))],
        out_specs=[_rows_rev(tc, w, steps), _acc((head_dim, w)), _acc((head_dim, w)), _acc((SUBLANES, w))],
        scratch_shapes=[pltpu.VMEM((tc, w), F32), pltpu.VMEM((tc, w), F32), pltpu.VMEM((tc + halo, w), F32),
                        pltpu.VMEM((halo + tc, w), F32), pltpu.VMEM((SUBLANES, w), F32),
                        pltpu.VMEM((slabs, LRU_SLAB, LRU_SLAB), F32), pltpu.VMEM((slabs, LRU_SLAB, LRU_SLAB), F32)],
        semantics="arbitrary",
    )(dh, xc, hprev, u, u, conv_w, wr_blk, b_r, wi_blk, b_i, lru_lambda)


def _s5_bwd(dya, y, sr, si, u, w_glu, b_glu, cre_blk, cimn_blk, bbr_blk, bbi_blk, ar, ai, d_skip, comm=None):
    t, sa = dya.shape
    gn = sr.shape[1]
    ns, _, sw = bbr_blk.shape
    tc = min(TIME_CHUNK, t)
    steps = t // tc
    halo = SUBLANES

    def body(dya_ref, y_ref, sr_ref, si_ref, u_ref, wg_ref, bg_ref, cre_ref, cim_ref, bbr_ref, bbi_ref,
             ar_ref, ai_ref, d_ref, du_ref, lr_ref, li_ref, dy_ref, dwg_out, vsa_ref, vgn_ref, gr_s, gi_s, cr_s, ci_s,
             dwg_ref):
        @pl.when(pl.program_id(0) == 0)
        def _():
            cr_s[...] = jnp.zeros_like(cr_s)
            ci_s[...] = jnp.zeros_like(ci_s)
            gr_s[tc:tc + halo, :] = jnp.zeros((halo, gn), F32)
            gi_s[tc:tc + halo, :] = jnp.zeros((halo, gn), F32)
            dwg_ref[...] = jnp.zeros_like(dwg_ref)
            vsa_ref[...] = jnp.zeros_like(vsa_ref)
            vgn_ref[...] = jnp.zeros_like(vgn_ref)

        yv = y_ref[...]
        uv = u_ref[...]
        zz = _gelu(yv)
        sg = _sigmoid(_dot(zz, wg_ref[...]) + bg_ref[...])
        dyav = dya_ref[...]
        dq = dyav * zz * sg * (1.0 - sg)
        dzz = dyav * sg + _dot_nt(dq, wg_ref[...])
        dwg_ref[...] += _dot_tn(zz, dq)
        dy = dzz * _gelu_grad(yv)
        dyb = dy.astype(BF16)
        dy_ref[...] = dyb.astype(dy_ref.dtype)
        vsa_ref[0:1, :] += _rowsum(dq)
        vsa_ref[1:2, :] += _rowsum(dy * uv)
        for m in range(ns):
            dym = dyb[:, m * S5_SLAB:(m + 1) * S5_SLAB]
            gr_s[0:tc, m * sw:(m + 1) * sw] = _dot_nt(dym, cre_ref[m])
            gi_s[0:tc, m * sw:(m + 1) * sw] = _dot_nt(dym, cim_ref[m])
        a_r = ar_ref[...]
        a_i = ai_ref[...]

        def step(i, carry):
            l_r, l_i = carry
            at = pl.ds(tc - 1 - i, 1)
            n_r = gr_s[at, :] + a_r * l_r + a_i * l_i
            n_i = gi_s[at, :] + a_r * l_i - a_i * l_r
            gr_s[at, :] = n_r
            gi_s[at, :] = n_i
            return n_r, n_i

        l_r, l_i = lax.fori_loop(0, tc, step, (cr_s[0:1, :], ci_s[0:1, :]), unroll=8)
        cr_s[0:1, :] = l_r
        ci_s[0:1, :] = l_i
        nxt_r = gr_s[1:tc + 1, :]
        nxt_i = gi_s[1:tc + 1, :]
        srv = sr_ref[...].astype(F32)
        siv = si_ref[...].astype(F32)
        vgn_ref[0:1, :] += _rowsum(nxt_r * srv + nxt_i * siv)
        vgn_ref[1:2, :] += _rowsum(nxt_i * srv - nxt_r * siv)
        lam_r = gr_s[0:tc, :]
        lam_i = gi_s[0:tc, :]
        gr_s[tc:tc + halo, :] = gr_s[0:halo, :]
        gi_s[tc:tc + halo, :] = gi_s[0:halo, :]
        lrb = lam_r.astype(BF16)
        lib = lam_i.astype(BF16)
        lr_ref[...] = lrb.astype(lr_ref.dtype)
        li_ref[...] = lib.astype(li_ref.dtype)
        for m in range(ns):
            states, chans = slice(m * sw, (m + 1) * sw), slice(m * S5_SLAB, (m + 1) * S5_SLAB)
            du_ref[:, chans] = (_dot_nt(lrb[:, states], bbr_ref[m]) + _dot_nt(lib[:, states], bbi_ref[m])
                                + dy[:, chans] * d_ref[:, chans]).astype(du_ref.dtype)
        _store_on_last([(dwg_ref, dwg_out)])

    return _run(
        body, comm, name="s5_bwd", grid=(steps,),
        out_shape=[jax.ShapeDtypeStruct((t, sa), BF16), jax.ShapeDtypeStruct((t, gn), BF16),
                   jax.ShapeDtypeStruct((t, gn), BF16), jax.ShapeDtypeStruct((t, sa), BF16),
                   jax.ShapeDtypeStruct((sa, sa), BF16), jax.ShapeDtypeStruct((SUBLANES, sa), F32),
                   jax.ShapeDtypeStruct((SUBLANES, gn), F32)],
        in_specs=[_rows_rev(tc, sa, steps), _rows_rev(tc, sa, steps), _rows_rev(tc, gn, steps), _rows_rev(tc, gn, steps),
                  _rows_rev(tc, sa, steps), _whole((sa, sa)), _whole((1, sa)), _whole(cre_blk.shape),
                  _whole(cimn_blk.shape), _whole(bbr_blk.shape), _whole(bbi_blk.shape), _whole((1, gn)), _whole((1, gn)),
                  _whole((1, sa))],
        out_specs=[_rows_rev(tc, sa, steps), _rows_rev(tc, gn, steps), _rows_rev(tc, gn, steps), _rows_rev(tc, sa, steps),
                   _acc((sa, sa)), _acc((SUBLANES, sa)), _acc((SUBLANES, gn))],
        scratch_shapes=[pltpu.VMEM((tc + halo, gn), F32), pltpu.VMEM((tc + halo, gn), F32),
                        pltpu.VMEM((SUBLANES, gn), F32), pltpu.VMEM((SUBLANES, gn), F32), pltpu.VMEM((sa, sa), F32)],
        semantics="arbitrary",
    )(dya, y, sr, si, u, w_glu, b_glu, cre_blk, cimn_blk, bbr_blk, bbi_blk, ar, ai, d_skip)


def _inproj_bwd(dparts, x, dx1, g_mix, w_in_t, comm=None):
    t, d = x.shape
    n = w_in_t.shape[0]
    widths = [p.shape[1] for p in dparts]
    offs = [sum(widths[:i]) for i in range(len(widths) + 1)]
    tm = min(TOKEN_TILE, t)
    np_ = len(dparts)

    def body(*refs):
        dz_refs = refs[:np_]
        x_ref, dx1_ref, g_ref, w_ref, gx_ref, h_ref, vd_ref, vn_ref = refs[np_:]
        _zero_on_first(vd_ref, vn_ref)
        dh = jnp.zeros((tm, d), F32)
        for k, r in enumerate(dz_refs):
            lo, hi = offs[k], offs[k + 1]
            dzk = r[...]
            dh = dh + _dot(dzk, w_ref[lo:hi, :])
            vn_ref[0:1, lo:hi] += _rowsum(dzk.astype(F32))
        xhat, r0 = _rms_stats(x_ref[...])
        h_ref[...] = (xhat * g_ref[...]).astype(h_ref.dtype)
        dxn, dg = _rms_bwd(dh, xhat, r0, g_ref[...])
        gx_ref[...] = dx1_ref[...] + dxn
        vd_ref[0:1, :] += _rowsum(dg)

    return _run(
        body, comm, name="inproj_bwd", grid=(t // tm,),
        out_shape=[jax.ShapeDtypeStruct((t, d), F32), jax.ShapeDtypeStruct((t, d), BF16),
                   jax.ShapeDtypeStruct((SUBLANES, d), F32), jax.ShapeDtypeStruct((SUBLANES, n), F32)],
        in_specs=[_rows(tm, w) for w in widths] + [_rows(tm, d), _rows(tm, d), _whole((1, d)), _whole((n, d))],
        out_specs=[_rows(tm, d), _rows(tm, d), _acc((SUBLANES, d)), _acc((SUBLANES, n))],
        semantics="arbitrary",
    )(*dparts, x, dx1, g_mix, w_in_t)


def _dw_from_parts(dparts, h, tn, name, comm=None):
    t, d = h.shape
    widths = [p.shape[1] for p in dparts]
    offs = [sum(widths[:i]) for i in range(len(widths) + 1)]
    np_ = len(dparts)

    def body(*refs):
        h_ref, o_ref = refs[np_], refs[np_ + 1]
        hv = h_ref[...]
        for k, r in enumerate(refs[:np_]):
            o_ref[offs[k]:offs[k + 1], :] = _dot_tn(r[...], hv).astype(o_ref.dtype)

    return _run(
        body, comm, name=name, grid=(d // tn,),
        out_shape=jax.ShapeDtypeStruct((offs[-1], d), BF16),
        in_specs=[_whole(p.shape) for p in dparts] + [pl.BlockSpec((t, tn), lambda j: (0, j))],
        out_specs=pl.BlockSpec((offs[-1], tn), lambda j: (0, j)),
        semantics="parallel",
    )(*dparts, h)


def _prep(lr_row, li_row, ldt_row, lr_col, li_col, ldt_col, b_re, b_im, c_re, c_im, w_r, w_i, comm=None):
    gn, pch = b_re.shape
    sa, n = c_re.shape
    w, hd = w_r.shape
    ns, sw, lsl = sa // S5_SLAB, S5_SLAB * n // pch, w // LRU_SLAB

    def spread_cols(vals, row_group, col_group, width):
        r, k = vals.shape
        tile = (lax.broadcasted_iota(jnp.int32, (k, width), 0) == lax.broadcasted_iota(jnp.int32, (k, width), 1) % k)
        rows = lax.broadcasted_iota(jnp.int32, (r, width), 0) // row_group
        cols = lax.broadcasted_iota(jnp.int32, (r, width), 1) // col_group
        return jnp.where(rows == cols, _dot(vals, tile.astype(BF16)), 0.0)

    def spread_rows(vals, row_group, col_group, height):
        k, c = vals.shape
        tile = (lax.broadcasted_iota(jnp.int32, (height, k), 0) % k == lax.broadcasted_iota(jnp.int32, (height, k), 1))
        rows = lax.broadcasted_iota(jnp.int32, (height, c), 0) // row_group
        cols = lax.broadcasted_iota(jnp.int32, (height, c), 1) // col_group
        return jnp.where(rows == cols, _dot(tile.astype(BF16), vals), 0.0)

    def body(lrr, lir, ldr, lrc, lic, ldc, bre, bim, cre, cim, wr, wi,
             ar_o, ai_o, bbr_o, bbi_o, cre_o, cim_o, wr_o, wi_o):
        ar, ai, _, _ = _disc_scalars(lrr[...], lir[...], ldr[...])
        ar_o[...] = ar
        ai_o[...] = ai
        _, _, bbr, bbi = _disc_cols(lrc[...], lic[...], ldc[...], bre[...], bim[...])
        bbr_t, bbi_t = bbr.T, bbi.T
        for m in range(ns):
            bbr_o[m] = spread_rows(bbr_t[:, m * sw:(m + 1) * sw], pch, n, S5_SLAB).astype(bbr_o.dtype)
            bbi_o[m] = spread_rows(bbi_t[:, m * sw:(m + 1) * sw], pch, n, S5_SLAB).astype(bbi_o.dtype)
            rows = slice(m * S5_SLAB, (m + 1) * S5_SLAB)
            cre_o[m] = spread_rows(cre[rows, :].T, n, pch, sw).astype(cre_o.dtype)
            cim_o[m] = spread_rows(-cim[rows, :].T, n, pch, sw).astype(cim_o.dtype)
        for j in range(lsl):
            rows = slice(j * LRU_SLAB, (j + 1) * LRU_SLAB)
            wr_o[j] = spread_cols(wr[rows, :], hd, hd, LRU_SLAB).astype(wr_o.dtype)
            wi_o[j] = spread_cols(wi[rows, :], hd, hd, LRU_SLAB).astype(wi_o.dtype)

    args = (lr_row, li_row, ldt_row, lr_col, li_col, ldt_col, b_re, b_im, c_re, c_im, w_r, w_i)
    out_shape = [jax.ShapeDtypeStruct((1, gn), F32), jax.ShapeDtypeStruct((1, gn), F32),
                 jax.ShapeDtypeStruct((ns, S5_SLAB, sw), BF16), jax.ShapeDtypeStruct((ns, S5_SLAB, sw), BF16),
                 jax.ShapeDtypeStruct((ns, sw, S5_SLAB), BF16), jax.ShapeDtypeStruct((ns, sw, S5_SLAB), BF16),
                 jax.ShapeDtypeStruct((lsl, LRU_SLAB, LRU_SLAB), BF16),
                 jax.ShapeDtypeStruct((lsl, LRU_SLAB, LRU_SLAB), BF16)]
    return _run(
        body, comm, name="prep", grid=(1,), out_shape=out_shape, in_specs=[_whole(a.shape) for a in args],
        out_specs=[_acc(s.shape) for s in out_shape], semantics="arbitrary",
    )(*args)


def _s5_param_grads(lam_r, lam_i, sr, si, u, dy, pch, n, comm=None):
    t, gn = lam_r.shape
    sa = u.shape[1]
    sw = S5_SLAB * n // pch

    def body(lr_ref, li_ref, sr_ref, si_ref, u_ref, dy_ref, dbr_ref, dbi_ref, dcr_ref, dci_ref):
        uv = u_ref[...]
        dyv = dy_ref[...]
        dbr_ref[...] = _fold_diag_blocks(_dot_tn(uv, lr_ref[...]), pch, n).astype(dbr_ref.dtype)
        dbi_ref[...] = _fold_diag_blocks(_dot_tn(uv, li_ref[...]), pch, n).astype(dbi_ref.dtype)
        dcr_ref[...] = _fold_diag_blocks(_dot_tn(sr_ref[...], dyv), n, pch).astype(dcr_ref.dtype)
        dci_ref[...] = _fold_diag_blocks(_dot_tn(si_ref[...], dyv), n, pch).astype(dci_ref.dtype)

    states = pl.BlockSpec((t, sw), lambda m: (0, m))
    chans = pl.BlockSpec((t, S5_SLAB), lambda m: (0, m))
    return _run(
        body, comm, name="s5_param_grads", grid=(sa // S5_SLAB,),
        out_shape=[jax.ShapeDtypeStruct((pch, gn), BF16), jax.ShapeDtypeStruct((pch, gn), BF16),
                   jax.ShapeDtypeStruct((n, sa), BF16), jax.ShapeDtypeStruct((n, sa), BF16)],
        in_specs=[states, states, states, states, chans, chans],
        out_specs=[pl.BlockSpec((pch, sw), lambda m: (0, m)), pl.BlockSpec((pch, sw), lambda m: (0, m)),
                   pl.BlockSpec((n, S5_SLAB), lambda m: (0, m)), pl.BlockSpec((n, S5_SLAB), lambda m: (0, m))],
        semantics="parallel",
    )(lam_r, lam_i, sr, si, u, dy)


SMALL_PARTS = ["vec_tail", "vec_ffn", "vec_lru", "vec_mix", "vec_bin", "vec_sa", "vec_gn", "dw_r", "dw_i", "dbb_re",
               "dbb_im", "dc_re", "dc_imn", "loss"]


def _small_reduce(parts, shapes, lr_col, li_col, ldt_col, b_re, b_im, groups, after=()):
    gn, pch = b_re.shape
    n = gn // groups
    nparts = parts[SMALL_PARTS[0]].size // math.prod(shapes[SMALL_PARTS[0]])
    np_, nout, first_out = len(SMALL_PARTS), 24, len(SMALL_PARTS) + 5 + len(after)

    def body(*refs):
        ins = refs[:np_]
        lr, li, ldt, bre, bim = refs[np_:np_ + 5]
        outs = refs[first_out:first_out + nout]
        sums = dict(zip(SMALL_PARTS, refs[first_out + nout:]))

        @pl.when(pl.program_id(0) == 0)
        def _():
            for k, r in zip(SMALL_PARTS, ins):
                sums[k][...] = r[...].astype(F32)

        @pl.when(pl.program_id(0) > 0)
        def _():
            for k, r in zip(SMALL_PARTS, ins):
                sums[k][...] += r[...].astype(F32)

        @pl.when(pl.program_id(0) == nparts - 1)
        def _():
            finish({k: s[...] for k, s in sums.items()}, lr, li, ldt, bre, bim, *outs)

    def finish(tot, lr, li, ldt, bre, bim, o_loss, o_gmix, o_bin, o_bglu, o_s5d, o_convb, o_br, o_bi, o_lam, o_gffn,
               o_gpg, o_bpg, o_gple, o_gfin, o_wr, o_wi, o_cre, o_cim, o_bre, o_bim, o_lre, o_lim, o_ldt, o_convw):
        o_loss[...] = tot["loss"]
        o_convw[...] = tot["vec_lru"][SUBLANES - CONV_WIDTH:SUBLANES]
        o_bpg[...] = tot["vec_tail"][0:1]
        o_gpg[...] = tot["vec_tail"][1:2]
        o_gple[...] = tot["vec_tail"][2:3]
        o_gfin[...] = tot["vec_tail"][3:4]
        o_gffn[...] = tot["vec_ffn"][0:1]
        o_convb[...] = tot["vec_lru"][0:1]
        o_br[...] = tot["vec_lru"][1:2]
        o_bi[...] = tot["vec_lru"][2:3]
        o_lam[...] = tot["vec_lru"][3:4]
        o_gmix[...] = tot["vec_mix"][0:1]
        o_bin[...] = tot["vec_bin"][0:1]
        o_bglu[...] = tot["vec_sa"][0:1]
        o_s5d[...] = tot["vec_sa"][1:2]
        o_wr[...] = tot["dw_r"].T
        o_wi[...] = tot["dw_i"].T
        o_cre[...] = tot["dc_re"].T
        o_cim[...] = -tot["dc_imn"].T
        d_a = tot["vec_gn"].T
        _, chain = jax.vjp(_disc_cols, lr[...], li[...], ldt[...], bre[...], bim[...])
        d_lr, d_li, d_ldt, d_bre, d_bim = chain((d_a[:, 0:1], d_a[:, 1:2], tot["dbb_re"].T, tot["dbb_im"].T))
        o_lre[...] = d_lr
        o_lim[...] = d_li
        o_bre[...] = d_bre
        o_bim[...] = d_bim
        same = (lax.broadcasted_iota(jnp.int32, (groups, gn), 0)
                == lax.broadcasted_iota(jnp.int32, (groups, gn), 1) // n).astype(F32)
        o_ldt[...] = jnp.dot(same, d_ldt * jnp.ones((1, LANES), F32), preferred_element_type=F32,
                             precision=lax.Precision.HIGHEST)[:, 0:1]

    d = shapes["vec_mix"][1]
    nz = shapes["vec_bin"][1]
    sa = shapes["vec_sa"][1]
    w = shapes["vec_lru"][1]
    row = lambda c: jax.ShapeDtypeStruct((1, c), F32)
    out_shape = [jax.ShapeDtypeStruct(shapes["loss"], F32), row(d), row(nz), row(sa), row(sa), row(w), row(w), row(w),
                 row(w), row(d), row(d), row(d), row(d), row(d),
                 jax.ShapeDtypeStruct(shapes["dw_r"][::-1], F32), jax.ShapeDtypeStruct(shapes["dw_i"][::-1], F32),
                 jax.ShapeDtypeStruct(shapes["dc_re"][::-1], F32), jax.ShapeDtypeStruct(shapes["dc_imn"][::-1], F32),
                 jax.ShapeDtypeStruct((gn, pch), F32), jax.ShapeDtypeStruct((gn, pch), F32),
                 jax.ShapeDtypeStruct((gn, 1), F32), jax.ShapeDtypeStruct((gn, 1), F32),
                 jax.ShapeDtypeStruct((groups, 1), F32), jax.ShapeDtypeStruct((CONV_WIDTH, w), F32)]
    def part_spec(k):
        r, c = shapes[k]
        if parts[k].ndim == 3:
            return pl.BlockSpec((None, r, c), lambda i: (i, 0, 0))
        return pl.BlockSpec((r, c), lambda i: (i, 0))

    outs = _run(
        body, None, name="small_reduce", grid=(nparts,), out_shape=out_shape,
        in_specs=([part_spec(k) for k in SMALL_PARTS] + [_whole(a.shape) for a in (lr_col, li_col, ldt_col, b_re, b_im)]
                  + [_ANY] * len(after)),
        out_specs=[_acc(s.shape) for s in out_shape],
        scratch_shapes=[pltpu.VMEM(shapes[k], F32) for k in SMALL_PARTS],
        semantics="arbitrary",
    )(*[parts[k] for k in SMALL_PARTS], lr_col, li_col, ldt_col, b_re, b_im, *after)
    names = ["loss", "g_mix", "b_in", "b_glu", "s5_d", "conv_b", "b_r", "b_i", "lru_lambda", "g_ffn", "g_ple_gate",
             "b_ple_gate", "g_ple", "g_final", "w_r", "w_i", "s5_c_re", "s5_c_im", "s5_b_re", "s5_b_im", "lam_re",
             "lam_im", "log_dt", "conv_w"]
    return dict(zip(names, outs))


def _adamw_small(ws, gs, ms, vs):
    n = len(ws)

    def body(*refs):
        w_r, g_r, m_r, v_r = (refs[i * n:(i + 1) * n] for i in range(4))
        g_o, d_o, m_o, v_o = (refs[(4 + i) * n:(5 + i) * n] for i in range(4))
        for i in range(n):
            g = g_r[i][...]
            delta, m_new, v_new = _adamw_math(w_r[i][...], g, m_r[i][...], v_r[i][...])
            g_o[i][...] = g
            d_o[i][...] = delta
            m_o[i][...] = m_new
            v_o[i][...] = v_new

    shapes = [jax.ShapeDtypeStruct(a.shape, F32) for a in ws]
    outs = pl.pallas_call(body, name="adamw_small", out_shape=shapes * 4)(*ws, *gs, *ms, *vs)
    return outs[:n], outs[n:2 * n], outs[2 * n:3 * n], outs[3 * n:]


def _adamw_math(w, g, m, v):
    m_new = ADAM_B1 * m + (1.0 - ADAM_B1) * g
    v_new = ADAM_B2 * v + (1.0 - ADAM_B2) * (g * g)
    m_hat = m_new / (1.0 - ADAM_B1 ** ADAM_STEP)
    v_hat = v_new / (1.0 - ADAM_B2 ** ADAM_STEP)
    delta = -ADAM_LR * (m_hat / (jnp.sqrt(v_hat) + ADAM_EPS) + ADAM_WD * w)
    return delta, m_new, v_new


def _row_tile(rows):
    for cand in range(256, 0, -16):
        if rows % cand == 0:
            return cand
    return rows


def _adamw(parts, w, m, v, name, transposed=False, own=None, after=()):
    nw = len(w)
    rows, cols = w[0].shape
    npart = parts[0].shape[0]
    tr = _row_tile(rows)
    if transposed:
        parts_spec = pl.BlockSpec((npart, cols, tr), lambda i: (0, 0, i))
    else:
        parts_spec = pl.BlockSpec((npart, tr, cols), lambda i: (0, i, 0))
    per = 4 if own is None else 5
    first_out = per * nw + len(after)

    def body(*refs):
        mine = None if own is None else _chip(_mesh_position())
        for i in range(nw):
            group = refs[per * i:per * i + per]
            p_ref, (w_ref, m_ref, v_ref) = group[0], group[-3:]
            g_ref, d_ref, mo_ref, vo_ref = refs[first_out + 4 * i:first_out + 4 * i + 4]

            def part(k):
                a = p_ref[k].astype(F32)
                return a if own is None else jnp.where(mine == k, group[1][k].astype(F32), a)

            g = part(0)
            for k in range(1, npart):
                g = g + part(k)
            if transposed:
                g = g.T
            delta, m_new, v_new = _adamw_math(w_ref[...], g, m_ref[...], v_ref[...])
            g_ref[...] = g
            d_ref[...] = delta
            mo_ref[...] = m_new
            vo_ref[...] = v_new

    groups = zip(parts, w, m, v) if own is None else zip(parts, own, w, m, v)
    res = pl.pallas_call(
        body, name=name, grid=(rows // tr,),
        out_shape=[jax.ShapeDtypeStruct((rows, cols), F32)] * (4 * nw),
        in_specs=([parts_spec] * (per - 3) + [_rows(tr, cols)] * 3) * nw + [_ANY] * len(after),
        out_specs=[_rows(tr, cols)] * (4 * nw),
        compiler_params=_params("parallel"),
    )(*[a for group in groups for a in group], *after)
    return [res[4 * i:4 * i + 4] for i in range(nw)]


def _mesh_position():
    return lax.axis_index("x"), lax.axis_index("y"), lax.axis_index("c")


def _flip(pos, rel):
    x, y, c = pos
    return (1 - x if rel & 4 else x, 1 - y if rel & 2 else y, 1 - c if rel & 1 else c)


def _index(pos):
    return 4 * pos[0] + 2 * pos[1] + pos[2]


_ANY = pl.BlockSpec(memory_space=pl.ANY)
FLAT_ROWS = 32


def _dma_sems(n):
    return [pltpu.SemaphoreType.DMA((n, N_DEV - 1)), pltpu.SemaphoreType.DMA((n, N_DEV - 1)), pltpu.SemaphoreType.DMA((n,))]


def _block_of(ref, idx, rows, flat):
    if flat:
        return ref.at[pl.ds(pl.multiple_of(idx * rows, FLAT_ROWS), rows), :]
    return ref.at[idx]


class _Gather:
    chips = (4, 2, 6)
    rels = frozenset((1, 4, 2, 6))

    def __init__(self, shards):
        self.inputs = list(shards)
        self.flat = [s.shape[0] % FLAT_ROWS == 0 for s in shards]
        self.out_shape = [
            jax.ShapeDtypeStruct((N_DEV * s.shape[0], s.shape[1]) if f else (N_DEV,) + s.shape, s.dtype)
            for s, f in zip(shards, self.flat)]
        self.sems = _dma_sems(len(shards))

    def _copy(self, ins, outs, sems, i, k, block, to, own=False, lands_index=None):
        a = i if lands_index is None else lands_index
        dst = _block_of(outs[a], _index(block), self.inputs[a].shape[0], self.flat[a])
        return pltpu.make_async_remote_copy(
            src_ref=ins[a] if own else dst, dst_ref=dst, send_sem=sems[0].at[i, k], recv_sem=sems[1].at[i, k],
            device_id=to, device_id_type=MESH)

    def _local(self, ins, outs, sems, i, me):
        dst = _block_of(outs[i], _index(me), self.inputs[i].shape[0], self.flat[i])
        return pltpu.make_async_copy(ins[i], dst, sems[2].at[i])

    def _first(self, ins, outs, sems, i, me):
        cps = [self._copy(ins, outs, sems, i, 0, me, _flip(me, 1), own=True)]
        cps += [self._copy(ins, outs, sems, i, 1 + j, me, _flip(me, rel), own=True) for j, rel in enumerate(self.chips)]
        return cps

    def _passed(self, ins, outs, sems, i, j, me):
        return self._copy(ins, outs, sems, i, 4 + j, _flip(me, self.chips[j]), _flip(me, 1))

    def before(self, ins, outs, sems):
        n = len(self.inputs)
        me = _mesh_position()

        @pl.when(pl.program_id(0) == 0)
        def _():
            for i in range(n):
                self._local(ins, outs, sems, i, me).start()
                for cp in self._first(ins, outs, sems, i, me):
                    cp.start()

        @pl.when(pl.program_id(0) == pl.num_programs(0) - 1)
        def _():
            for j, rel in enumerate(self.chips):
                for i in range(n):
                    self._copy(ins, outs, sems, i, 1 + j, _flip(me, rel), me).wait_recv()
                    self._passed(ins, outs, sems, i, j, me).start()

    def after(self, ins, outs, sems):
        n = len(self.inputs)
        me = _mesh_position()
        sibling = _flip(me, 1)

        @pl.when(pl.program_id(0) == pl.num_programs(0) - 1)
        def _():
            for i in range(n):
                self._copy(ins, outs, sems, i, 0, sibling, me).wait_recv()
                for j, rel in enumerate(self.chips):
                    self._copy(ins, outs, sems, i, 4 + j, _flip(sibling, rel), me).wait_recv()
            for i in range(n):
                for cp in self._first(ins, outs, sems, i, me):
                    cp.wait_send()
                for j in range(len(self.chips)):
                    self._passed(ins, outs, sems, i, j, me).wait_send()
                self._local(ins, outs, sems, i, me).wait()


N_CHIPS = N_DEV // 2


def _chip(pos):
    return 2 * pos[0] + pos[1]


class _PairSwap:
    rels = frozenset((1,))

    def __init__(self, arrays):
        self.inputs = list(arrays)
        self.rows = [a.shape[0] // N_DEV for a in arrays]
        for r in self.rows:
            assert r % FLAT_ROWS == 0, r
        self.out_shape = [jax.ShapeDtypeStruct((N_CHIPS, r, a.shape[1]), a.dtype) for a, r in zip(arrays, self.rows)]
        n = len(arrays)
        self.sems = [pltpu.SemaphoreType.DMA((n, N_CHIPS)), pltpu.SemaphoreType.DMA((n, N_CHIPS))]

    def _copy(self, ins, outs, sems, i, j, me):
        sibling = _flip(me, 1)
        return pltpu.make_async_remote_copy(
            src_ref=_block_of(ins[i], 2 * j + sibling[2], self.rows[i], True), dst_ref=outs[i].at[j],
            send_sem=sems[0].at[i, j], recv_sem=sems[1].at[i, j], device_id=sibling, device_id_type=MESH)

    def before(self, ins, outs, sems):
        me = _mesh_position()

        @pl.when(pl.program_id(0) == 0)
        def _():
            for i in range(len(self.inputs)):
                for j in range(N_CHIPS):
                    self._copy(ins, outs, sems, i, j, me).start()

    def after(self, ins, outs, sems):
        me = _mesh_position()

        @pl.when(pl.program_id(0) == pl.num_programs(0) - 1)
        def _():
            for i in range(len(self.inputs)):
                for j in range(N_CHIPS):
                    self._copy(ins, outs, sems, i, j, me).wait()


class _ChipExchange:
    chips = (4, 2, 6)
    rels = frozenset(chips)

    def __init__(self, arrays, after=()):
        self.arrays = len(arrays)
        self.inputs = list(arrays) + list(after)
        self.out_shape = [jax.ShapeDtypeStruct(a.shape, a.dtype) for a in arrays]
        n = len(arrays)
        self.sems = [pltpu.SemaphoreType.DMA((n, 3)), pltpu.SemaphoreType.DMA((n, 3)), pltpu.SemaphoreType.DMA((n,))]

    def _send(self, ins, outs, sems, i, k, me):
        peer = _flip(me, self.chips[k])
        return pltpu.make_async_remote_copy(
            src_ref=ins[i].at[_chip(peer)], dst_ref=outs[i].at[_chip(me)], send_sem=sems[0].at[i, k],
            recv_sem=sems[1].at[i, k], device_id=peer, device_id_type=MESH)

    def _arrival(self, ins, outs, sems, i, k, me):
        peer = _flip(me, self.chips[k])
        return pltpu.make_async_remote_copy(
            src_ref=ins[i].at[_chip(me)], dst_ref=outs[i].at[_chip(peer)], send_sem=sems[0].at[i, k],
            recv_sem=sems[1].at[i, k], device_id=peer, device_id_type=MESH)

    def _local(self, ins, outs, sems, i, me):
        return pltpu.make_async_copy(ins[i].at[_chip(me)], outs[i].at[_chip(me)], sems[2].at[i])

    def before(self, ins, outs, sems):
        me = _mesh_position()

        @pl.when(pl.program_id(0) == 0)
        def _():
            for i in range(self.arrays):
                self._local(ins, outs, sems, i, me).start()
                for k in range(len(self.chips)):
                    self._send(ins, outs, sems, i, k, me).start()

    def after(self, ins, outs, sems):
        me = _mesh_position()

        @pl.when(pl.program_id(0) == pl.num_programs(0) - 1)
        def _():
            for i in range(self.arrays):
                for k in range(len(self.chips)):
                    self._arrival(ins, outs, sems, i, k, me).wait_recv()
            for i in range(self.arrays):
                for k in range(len(self.chips)):
                    self._send(ins, outs, sems, i, k, me).wait_send()
                self._local(ins, outs, sems, i, me).wait()


_HBM = pl.BlockSpec(memory_space=pltpu.HBM)
_SEM = pl.BlockSpec(memory_space=pltpu.SEMAPHORE)
_DATAFLOW = pltpu.SideEffectType.DATAFLOW_SIDE_EFFECTING
_CHIP_RELS = (4, 2, 6)
_SPLIT_COLLECTIVE_ID = 3


def _chips_copy(src_ref, dst_ref, send_sems, recv_sems, k, me):
    peer = _flip(me, _CHIP_RELS[k])
    return pltpu.make_async_remote_copy(
        src_ref=src_ref.at[_chip(peer)], dst_ref=dst_ref.at[_chip(me)], send_sem=send_sems.at[k],
        recv_sem=recv_sems.at[k], device_id=peer, device_id_type=MESH)


def _chips_start(pairs, name):
    def body(p_ref, land_ref, send_sems, recv_sems, p_thru, land_thru, token):
        me = _mesh_position()
        sem = pltpu.get_barrier_semaphore()
        for rel in _CHIP_RELS:
            pl.semaphore_signal(sem, inc=1, device_id=_flip(me, rel), device_id_type=MESH)
        pl.semaphore_wait(sem, len(_CHIP_RELS))
        for k in range(len(_CHIP_RELS)):
            _chips_copy(p_ref, land_ref, send_sems, recv_sems, k, me).start()
        token[...] = jnp.zeros_like(token)

    n = len(_CHIP_RELS)
    return pl.pallas_call(
        body, name="chips_start_" + name,
        out_shape=(pltpu.SemaphoreType.DMA((n,)), pltpu.SemaphoreType.DMA((n,)), pltpu.HBM(pairs.shape, pairs.dtype),
                   pltpu.HBM(pairs.shape, pairs.dtype), jax.ShapeDtypeStruct((SUBLANES, LANES), F32)),
        in_specs=(_HBM, _HBM), out_specs=(_SEM, _SEM, _HBM, _HBM, pl.BlockSpec(memory_space=pltpu.VMEM)),
        input_output_aliases={0: 2, 1: 3},
        compiler_params=pltpu.CompilerParams(has_side_effects=_DATAFLOW, collective_id=_SPLIT_COLLECTIVE_ID),
    )(pltpu.with_memory_space_constraint(pairs, pltpu.HBM),
      pltpu.with_memory_space_constraint(lax.empty(pairs.shape, pairs.dtype), pltpu.HBM))


def _chips_wait(send_sems, recv_sems, p_thru, land_thru, after, name):
    def body(p_ref, land_ref, send_sems, recv_sems, *rest):
        me = _mesh_position()
        for k in range(len(_CHIP_RELS)):
            copy = _chips_copy(p_ref, land_ref, send_sems, recv_sems, k, me)
            copy.wait_send()
            copy.wait_recv()

    return pl.pallas_call(
        body, name="chips_wait_" + name,
        out_shape=(pltpu.HBM(p_thru.shape, p_thru.dtype), pltpu.HBM(p_thru.shape, p_thru.dtype)),
        in_specs=(_HBM, _HBM, _SEM, _SEM) + (_ANY,) * len(after), out_specs=(_HBM, _HBM),
        input_output_aliases={0: 0, 1: 1}, compiler_params=pltpu.CompilerParams(has_side_effects=_DATAFLOW),
    )(p_thru, land_thru, send_sems, recv_sems, *after)


_GATHER_START_ID, _GATHER_PASS_ID = 4, 5


class _Rows:
    def __init__(self, ref, width):
        self.ref, self.width, self.at = ref, width, self

    def __getitem__(self, idx):
        i, k = idx
        return self.ref.at[i * self.width + k]


def _handshake(rels):
    me = _mesh_position()
    sem = pltpu.get_barrier_semaphore()
    for rel in rels:
        pl.semaphore_signal(sem, inc=1, device_id=_flip(me, rel), device_id_type=MESH)
    pl.semaphore_wait(sem, len(rels))


def _gather_start(shards, name, after=()):
    job, n, na = _Gather(shards), len(shards), len(after)

    def body(*refs):
        ins, lands = refs[:n], refs[n:2 * n]
        send, recv, local = refs[2 * n + na:2 * n + na + 3]
        sems = (_Rows(send, N_DEV - 1), _Rows(recv, N_DEV - 1), local)
        token = refs[-1]
        me = _mesh_position()
        _handshake(sorted(job.rels))
        for i in range(n):
            job._local(ins, lands, sems, i, me).start()
            for cp in job._first(ins, lands, sems, i, me):
                cp.start()
        token[...] = jnp.zeros_like(token)

    res = pl.pallas_call(
        body, name=name,
        out_shape=(pltpu.SemaphoreType.DMA((n * (N_DEV - 1),)), pltpu.SemaphoreType.DMA((n * (N_DEV - 1),)),
                   pltpu.SemaphoreType.DMA((n,))) + tuple(pltpu.HBM(s.shape, s.dtype) for s in job.out_shape)
        + (jax.ShapeDtypeStruct((SUBLANES, LANES), F32),),
        in_specs=(_HBM,) * (2 * n) + (_ANY,) * na,
        out_specs=(_SEM,) * 3 + (_HBM,) * n + (pl.BlockSpec(memory_space=pltpu.VMEM),),
        input_output_aliases={n + i: 3 + i for i in range(n)},
        compiler_params=pltpu.CompilerParams(has_side_effects=_DATAFLOW, collective_id=_GATHER_START_ID),
    )(*[pltpu.with_memory_space_constraint(s, pltpu.HBM) for s in shards],
      *[pltpu.with_memory_space_constraint(lax.empty(s.shape, s.dtype), pltpu.HBM) for s in job.out_shape], *after)
    return list(res[:3]), list(res[3:3 + n]), res[-1]


def _gather_pass(shards, which, sems, lands, name, after=()):
    job, n, m, na = _Gather(shards), len(shards), len(which), len(after)

    def body(*refs):
        lands_r, local = refs[:m], refs[m + 2]
        send, recv = _Rows(refs[m], N_DEV - 1), _Rows(refs[m + 1], N_DEV - 1)
        send2, recv2 = _Rows(refs[m + 3 + na], 3), _Rows(refs[m + 3 + na + 1], 3)
        me = _mesh_position()
        _handshake((1,))
        by_index = {i: lands_r[q] for q, i in enumerate(which)}
        for j, rel in enumerate(job.chips):
            for q, i in enumerate(which):
                job._copy(None, by_index, (send, recv, local), i, 1 + j, _flip(me, rel), me).wait_recv()
                job._copy(None, by_index, (send2, recv2), q, j, _flip(me, rel), _flip(me, 1), lands_index=i).start()

    res = pl.pallas_call(
        body, name=name,
        out_shape=(pltpu.SemaphoreType.DMA((m * 3,)), pltpu.SemaphoreType.DMA((m * 3,)))
        + tuple(pltpu.HBM(lands[i].shape, lands[i].dtype) for i in which),
        in_specs=(_HBM,) * m + (_SEM,) * 3 + (_ANY,) * na, out_specs=(_SEM, _SEM) + (_HBM,) * m,
        input_output_aliases={q: 2 + q for q in range(m)},
        compiler_params=pltpu.CompilerParams(has_side_effects=_DATAFLOW, collective_id=_GATHER_PASS_ID),
    )(*[lands[i] for i in which], *sems, *after)
    return list(res[:2]), list(res[2:])


def _gather_finish(shards, which, sems, sems2, lands, after, name):
    job, n, m = _Gather(shards), len(shards), len(which)

    def body(*refs):
        ins, lands_r = refs[:m], refs[m:2 * m]
        send, recv, local = _Rows(refs[2 * m], N_DEV - 1), _Rows(refs[2 * m + 1], N_DEV - 1), refs[2 * m + 2]
        send2, recv2 = _Rows(refs[2 * m + 3], 3), _Rows(refs[2 * m + 4], 3)
        me = _mesh_position()
        sibling = _flip(me, 1)
        by_index = {i: lands_r[q] for q, i in enumerate(which)}
        own = {i: ins[q] for q, i in enumerate(which)}
        for q, i in enumerate(which):
            job._copy(None, by_index, (send, recv, local), i, 0, sibling, me).wait_recv()
            for j, rel in enumerate(job.chips):
                job._copy(None, by_index, (send2, recv2), q, j, _flip(sibling, rel), me, lands_index=i).wait_recv()
                job._copy(None, by_index, (send2, recv2), q, j, _flip(me, rel), sibling, lands_index=i).wait_send()
            for cp in job._first(own, by_index, (send, recv, local), i, me):
                cp.wait_send()
            job._local(own, by_index, (send, recv, local), i, me).wait()

    res = pl.pallas_call(
        body, name=name, out_shape=tuple(pltpu.HBM(lands[i].shape, lands[i].dtype) for i in which),
        in_specs=(_HBM,) * (2 * m) + (_SEM,) * 5 + (_ANY,) * len(after), out_specs=(_HBM,) * m,
        input_output_aliases={m + q: q for q in range(m)},
        compiler_params=pltpu.CompilerParams(has_side_effects=_DATAFLOW),
    )(*[pltpu.with_memory_space_constraint(shards[i], pltpu.HBM) for i in which], *[lands[i] for i in which],
      *sems, *sems2, *after)
    return list(res)


def _pair_add(grads, halves, name):
    n = len(grads)

    def body(core_ref, *refs):
        g_refs, h_refs, o_refs = refs[:n], refs[n:2 * n], refs[2 * n:]
        for i in range(n):
            o_refs[i][...] = (g_refs[i][...].astype(F32) + h_refs[i][...].astype(F32)).astype(o_refs[i].dtype)

    def own_block(h):
        return pl.BlockSpec(h.shape[1:], lambda j, core: (2 * j + core[0], 0))

    def slot(h):
        return pl.BlockSpec((None,) + h.shape[1:], lambda j, core: (j, 0, 0))

    spec = pltpu.PrefetchScalarGridSpec(
        num_scalar_prefetch=1, grid=(N_CHIPS,),
        in_specs=[own_block(h) for h in halves] + [slot(h) for h in halves], out_specs=[slot(h) for h in halves])
    return pl.pallas_call(
        body, name=name, grid_spec=spec, out_shape=[jax.ShapeDtypeStruct(h.shape, h.dtype) for h in halves],
        compiler_params=_params("parallel"),
    )(lax.axis_index("c").astype(jnp.int32).reshape(1), *grads, *halves)


class _Both:
    def __init__(self, first, second):
        self.jobs = (first, second)
        self.rels = first.rels | second.rels
        self.inputs = first.inputs + second.inputs
        self.out_shape = first.out_shape + second.out_shape
        self.sems = first.sems + second.sems

    def _each(self, ins, outs, sems):
        a = self.jobs[0]
        i, o, s = len(a.inputs), len(a.out_shape), len(a.sems)
        return ((a, ins[:i], outs[:o], sems[:s]), (self.jobs[1], ins[i:], outs[o:], sems[s:]))

    def before(self, ins, outs, sems):
        for job, i, o, s in self._each(ins, outs, sems):
            job.before(i, o, s)

    def after(self, ins, outs, sems):
        for job, i, o, s in self._each(ins, outs, sems):
            job.after(i, o, s)


_COLLECTIVE_IDS = {(1,): 0, (2, 4, 6): 1, (1, 2, 4, 6): 2}


def _entry_barrier(rels):
    @pl.when(pl.program_id(0) == 0)
    def _():
        me = _mesh_position()
        sem = pltpu.get_barrier_semaphore()
        for rel in rels:
            pl.semaphore_signal(sem, inc=1, device_id=_flip(me, rel), device_id_type=MESH)
        pl.semaphore_wait(sem, len(rels))


def _run(body, comm, *, semantics, out_shape, in_specs, out_specs, scratch_shapes=(), **kw):
    if comm is None:
        return pl.pallas_call(body, out_shape=out_shape, in_specs=in_specs, out_specs=out_specs,
                              scratch_shapes=list(scratch_shapes), compiler_params=_params(semantics), **kw)
    single = not isinstance(out_shape, (list, tuple))
    outs = [out_shape] if single else list(out_shape)
    ospecs = [out_specs] if single else list(out_specs)
    counts = [len(in_specs), len(comm.inputs), len(outs), len(comm.out_shape), len(scratch_shapes), len(comm.sems)]
    rels = tuple(sorted(comm.rels))

    def carrying(*refs):
        groups, pos = [], 0
        for c in counts:
            groups.append(refs[pos:pos + c])
            pos += c
        main_in, comm_in, main_out, comm_out, main_scratch, comm_sems = groups
        _entry_barrier(rels)
        comm.before(comm_in, comm_out, comm_sems)
        body(*main_in, *main_out, *main_scratch)
        comm.after(comm_in, comm_out, comm_sems)

    call = pl.pallas_call(
        carrying, out_shape=outs + list(comm.out_shape), in_specs=list(in_specs) + [_ANY] * len(comm.inputs),
        out_specs=ospecs + [_ANY] * len(comm.out_shape), scratch_shapes=list(scratch_shapes) + list(comm.sems),
        compiler_params=pltpu.CompilerParams(dimension_semantics=("arbitrary",), vmem_limit_bytes=VMEM_LIMIT,
                                             collective_id=_COLLECTIVE_IDS[rels]), **kw)

    def apply(*args):
        res = call(*args, *comm.inputs)
        main = res[:len(outs)]
        return (main[0] if single else list(main)), list(res[len(outs):])

    return apply


def _alone(comm, name):
    return _run(lambda: None, comm, semantics="arbitrary", name=name, grid=(1,), out_shape=[], in_specs=[], out_specs=[])()[1]


SHARDED = {"w_in": 1, "w_glu": 0, "conv_w": 1, "w_a_out": 1, "w_b_out": 0, "w_o": 0, "w_ffn_gate": 1, "w_ffn_up": 1,
           "w_ffn_down": 0, "w_ple_gate": 0, "w_ple": 1}
TRANSPOSED = ("w_in", "w_a_out", "w_ffn_gate", "w_ffn_up", "w_ple")
LONG_AXIS_MINOR = ("w_in", "w_ffn_gate", "w_ffn_up")
NARROW_LAST = ("s5_b_re", "s5_b_im", "s5_d")
ADAMW_GROUPS = (("w_in",), ("w_glu",), ("conv_w",), ("w_a_out",), ("w_b_out", "w_o", "w_ple_gate"),
                ("w_ffn_gate", "w_ffn_up"), ("w_ffn_down",), ("w_ple",))
SMALL = ["g_mix", "b_in", "lam_re", "lam_im", "log_dt", "s5_b_re", "s5_b_im", "s5_c_re", "s5_c_im", "s5_d", "b_glu",
         "conv_b", "w_r", "b_r", "w_i", "b_i", "lru_lambda", "g_ffn", "g_ple_gate", "b_ple_gate", "g_ple", "g_final"]
WEIGHTS = ["g_mix", "w_in", "b_in", "lam_re", "lam_im", "log_dt", "s5_b_re", "s5_b_im", "s5_c_re", "s5_c_im", "s5_d",
           "w_glu", "b_glu", "conv_w", "conv_b", "w_r", "b_r", "w_i", "b_i", "lru_lambda", "w_a_out", "w_b_out", "w_o",
           "g_ffn", "w_ffn_gate", "w_ffn_up", "w_ffn_down", "g_ple_gate", "w_ple_gate", "b_ple_gate", "w_ple", "g_ple",
           "g_final"]


def _join_columns(gathered):
    nb, r, c = gathered.shape
    return jnp.transpose(gathered, (1, 0, 2)).reshape(r, nb * c)


def _disc_scalars(lr, li, ldt):
    dt = jnp.exp(ldt)
    mag = jnp.exp(lr * dt)
    ar = mag * jnp.cos(li * dt)
    ai = mag * jnp.sin(li * dt)
    den = lr * lr + li * li
    nr = ar - 1.0
    fr = (nr * lr + ai * li) / den
    fi = (ai * lr - nr * li) / den
    return ar, ai, fr, fi


def _disc_cols(lr, li, ldt, b_re, b_im):
    ar, ai, fr, fi = _disc_scalars(lr, li, ldt)
    return ar, ai, fr * b_re - fi * b_im, fr * b_im + fi * b_re


def _local_step(x, p, target, src, small, disc, distributed=True):
    full = {} if distributed else dict(src)
    gw, halves, pairs, got = {}, {}, {}, {}

    def gather(keys):
        return (_Gather([src[k] for k in keys]), keys, full) if distributed else None

    def swap(keys):
        return (_PairSwap([gw[k] for k in keys]), keys, halves) if distributed else None

    def chips(keys):
        return (_ChipExchange([pairs[k] for k in keys]), keys, got) if distributed else None

    def add_pairs(keys):
        if distributed:
            pairs.update(zip(keys, _pair_add([gw[k] for k in keys], [halves[k] for k in keys], "pair_add_" + keys[0])))

    def carry(fn, *args, jobs=()):
        jobs = [j for j in jobs if j is not None]
        if not jobs:
            return fn(*args)
        comm = jobs[0][0]
        for j in jobs[1:]:
            comm = _Both(comm, j[0])
        res, extra = fn(*args, comm=comm)
        for job, keys, sink in jobs:
            sink.update(zip(keys, extra[:len(job.out_shape)]))
            extra = extra[len(job.out_shape):]
        return res

    d = x.shape[1]
    g, n, pch = small["s5_b_re"].shape
    sa, lw = g * pch, small["lru_lambda"].shape[-1]
    widths = [sa, lw, d, d]
    row = lambda v: v.reshape(1, -1)

    hd = small["w_r"].shape[-1]
    ar_row, ai_row, bbr_blk, bbi_blk, cre_blk, cimn_blk, wr_blk, wi_blk = carry(
        _prep, *disc["rows"], *disc["cols"], disc["b_re"], disc["b_im"], small["s5_c_re"].reshape(sa, n),
        small["s5_c_im"].reshape(sa, n), small["w_r"].reshape(lw, hd), small["w_i"].reshape(lw, hd),
        jobs=[gather(["w_in"])])
    d_row = row(small["s5_d"])
    g_mix_row = row(small["g_mix"])
    if distributed:
        later = ["w_glu", "conv_w", "w_a_out", "w_b_out", "w_ffn_gate", "w_ffn_up", "w_o", "w_ffn_down", "w_ple_gate",
                 "w_ple"]
        wires = [src[k] for k in later]
        flight, lands, token = _gather_start(wires, "gather_start", after=[ar_row])
        g_mix_row = g_mix_row + token[0, 0]

        def arrive(keys, after):
            which = [later.index(k) for k in keys]
            passed, moved = _gather_pass(wires, which, flight, lands, "gather_pass_" + keys[0], after)
            for q, i in enumerate(which):
                lands[i] = moved[q]
            full.update(zip(keys, _gather_finish(wires, which, flight, passed, lands, (), "gather_finish_" + keys[0])))
    else:
        arrive = lambda keys, after: None

    u_a, u_b, za, zb = _inproj_fwd(x, g_mix_row, full["w_in"], row(small["b_in"]), widths)
    arrive(["w_glu", "conv_w", "w_a_out", "w_b_out"], [u_a])
    conv_w = _join_columns(full["conv_w"]) if distributed else full["conv_w"]
    branches = [_s5_fwd(u_a, bbr_blk, bbi_blk, ar_row, ai_row, cre_blk, cimn_blk, d_row, full["w_glu"],
                        row(small["b_glu"])),
                _lru_fwd(u_b, conv_w, row(small["conv_b"]), wr_blk, row(small["b_r"]), wi_blk, row(small["b_i"]),
                         row(small["lru_lambda"]))]
    (sr, si, y, y_a), (xc, h, hprev) = _stages(branches, "branches_fwd")
    arrive(["w_ffn_gate", "w_ffn_up", "w_o"], [y_a])
    x1, merged, ma, mb, fg, fu = _merge_ffn_up_fwd(y_a, h, za, zb, x, full["w_a_out"], full["w_b_out"], full["w_o"],
                                                   row(small["g_ffn"]), full["w_ffn_gate"], full["w_ffn_up"])
    arrive(["w_ffn_down", "w_ple_gate", "w_ple"], [x1])
    x2 = _ffn_down_fwd(fg, fu, x1, full["w_ffn_down"])

    dx2, loss_blk, gw["w_ple_gate"], gw["w_ple"], vec_tail = _tail_fwd_bwd(
        x2, p, target, row(small["g_ple_gate"]), full["w_ple_gate"], row(small["b_ple_gate"]), full["w_ple"],
        row(small["g_ple"]), row(small["g_final"]))
    dfg, dfu, act = carry(_ffn_bwd_a, dx2, fg, fu, full["w_ffn_down"], jobs=[swap(["w_ple_gate", "w_ple"])])
    tn_d = min(d, 512)
    gw["w_ffn_down"] = _matmul_tn(act, dx2, tn_d, "dw_ffn_down", BF16)
    add_pairs(["w_ple_gate", "w_ple"])
    dx1, h2, vec_ffn = carry(_ffn_bwd_b, dfg, dfu, x1, dx2, row(small["g_ffn"]), full["w_ffn_gate"], full["w_ffn_up"],
                             jobs=[chips(["w_ple_gate", "w_ple"]), swap(["w_ffn_down"])])
    gw["w_ffn_gate"] = _matmul_tn(dfg, h2, tn_d, "dw_ffn_gate", BF16)
    gw["w_ffn_up"] = _matmul_tn(dfu, h2, tn_d, "dw_ffn_up", BF16)
    add_pairs(["w_ffn_down"])
    dza, dzb, dya, dyb, gw["w_o"], gw["w_a_out"], gw["w_b_out"] = carry(
        _merge_bwd, dx1, merged, ma, mb, za, zb, y_a, h, full["w_o"], full["w_a_out"], full["w_b_out"],
        jobs=[chips(["w_ffn_down"]), swap(["w_ffn_gate", "w_ffn_up"])])
    add_pairs(["w_ffn_gate", "w_ffn_up"])
    own = {}
    du_b, dw_r, dw_i, vec_lru = carry(
        _lru_bwd, dyb, xc, hprev, u_b, conv_w, wr_blk, row(small["b_r"]), wi_blk, row(small["b_i"]),
        row(small["lru_lambda"]), hd, jobs=[chips(["w_ffn_gate"]), swap(["w_o", "w_a_out", "w_b_out"])])
    add_pairs(["w_o", "w_a_out", "w_b_out"])
    du_a, lam_r, lam_i, dy16, gw["w_glu"], vec_sa, vec_gn = carry(
        _s5_bwd, dya, y, sr, si, u_a, full["w_glu"], row(small["b_glu"]), cre_blk, cimn_blk, bbr_blk, bbi_blk, ar_row,
        ai_row, d_row, jobs=[chips(["w_ffn_up"]), chips(["w_o", "w_a_out", "w_b_out"])])
    smalls = {"vec_tail": vec_tail, "vec_ffn": vec_ffn, "vec_lru": vec_lru, "vec_sa": vec_sa, "vec_gn": vec_gn,
              "dw_r": dw_r, "dw_i": dw_i, "loss": loss_blk}
    everyones = {}

    def gather_smalls(keys):
        return (_Gather([smalls[k] for k in keys]), keys, everyones) if distributed else None

    smalls["dbb_re"], smalls["dbb_im"], smalls["dc_re"], smalls["dc_imn"] = carry(
        _s5_param_grads, lam_r, lam_i, sr, si, u_a, dy16, pch, n, jobs=[swap(["w_glu"]), gather_smalls(list(smalls))])
    add_pairs(["w_glu"])
    dz = [du_a, du_b, dza, dzb]
    grad_x, h0, smalls["vec_mix"], smalls["vec_bin"] = _inproj_bwd(
        dz, x, dx1, row(small["g_mix"]), full["w_in"])
    shapes = {k: a.shape for k, a in smalls.items()}
    gw["w_in"] = carry(_dw_from_parts, dz, h0, tn_d, "dw_in",
                       jobs=[chips(["w_glu"]),
                             gather_smalls(["dbb_re", "dbb_im", "dc_re", "dc_imn", "vec_mix", "vec_bin"])])
    smalls.update(everyones)
    if distributed:
        got["w_in"] = gw["w_in"]
        gw = got
    return grad_x, gw, smalls, shapes, own


def _disc_inputs(small):
    g, n, pch = small["s5_b_re"].shape
    srcs = (small["lam_re"], small["lam_im"], jnp.repeat(small["log_dt"], n))
    return {"rows": [a.reshape(1, g * n) for a in srcs], "cols": [a.reshape(g * n, 1) for a in srcs],
            "b_re": small["s5_b_re"].reshape(g * n, pch), "b_im": small["s5_b_im"].reshape(g * n, pch)}


def kernel(x, p, g_mix, w_in, b_in, lam_re, lam_im, log_dt, s5_b_re, s5_b_im, s5_c_re, s5_c_im, s5_d, w_glu, b_glu, conv_w, conv_b, w_r, b_r, w_i, b_i, lru_lambda, w_a_out, w_b_out, w_o, g_ffn, w_ffn_gate, w_ffn_up, w_ffn_down, g_ple_gate, w_ple_gate, b_ple_gate, w_ple, g_ple, g_final, loss_target, m_g_mix, m_w_in, m_b_in, m_lam_re, m_lam_im, m_log_dt, m_s5_b_re, m_s5_b_im, m_s5_c_re, m_s5_c_im, m_s5_d, m_w_glu, m_b_glu, m_conv_w, m_conv_b, m_w_r, m_b_r, m_w_i, m_b_i, m_lru_lambda, m_w_a_out, m_w_b_out, m_w_o, m_g_ffn, m_w_ffn_gate, m_w_ffn_up, m_w_ffn_down, m_g_ple_gate, m_w_ple_gate, m_b_ple_gate, m_w_ple, m_g_ple, m_g_final, v_g_mix, v_w_in, v_b_in, v_lam_re, v_lam_im, v_log_dt, v_s5_b_re, v_s5_b_im, v_s5_c_re, v_s5_c_im, v_s5_d, v_w_glu, v_b_glu, v_conv_w, v_conv_b, v_w_r, v_b_r, v_w_i, v_b_i, v_lru_lambda, v_w_a_out, v_w_b_out, v_w_o, v_g_ffn, v_w_ffn_gate, v_w_ffn_up, v_w_ffn_down, v_g_ple_gate, v_w_ple_gate, v_b_ple_gate, v_w_ple, v_g_ple, v_g_final):
    given = dict(locals())
    wts = {k: given[k] for k in WEIGHTS}
    moms = {k: given["m_" + k] for k in WEIGHTS}
    vels = {k: given["v_" + k] for k in WEIGHTS}

    def drop_depth(k, a):
        return a if k == "g_final" else a[0]

    small = {k: drop_depth(k, wts[k]) for k in SMALL}
    shard = {k: wts[k][0] for k in SHARDED}
    names = list(SHARDED)

    def wire(k):
        if k == "conv_w":
            return shard[k]
        return (shard[k].T if k in TRANSPOSED else shard[k]).astype(BF16)

    disc = _disc_inputs(small)
    grad_x, parts, smalls, shapes, own = _local_step(x[0], p[0, 0], loss_target[0], {k: wire(k) for k in names}, small,
                                                     disc)

    (pair_in,) = _pair_add([parts["w_in"]], _alone(_PairSwap([parts["w_in"]]), "swap_last"), "pair_add_w_in")
    send_sems, recv_sems, pair_in, landing, token = _chips_start(pair_in, "w_in")
    ordered = (token,)

    g_small = _small_reduce(smalls, shapes, *disc["cols"], disc["b_re"], disc["b_im"], small["s5_b_re"].shape[0],
                            after=ordered)
    loss = g_small.pop("loss")[0, 0]
    cols = shard["conv_w"].shape[1]
    mine = _index((lax.axis_index("x"), lax.axis_index("y"), lax.axis_index("c")))
    parts["conv_w"] = lax.dynamic_slice_in_dim(g_small.pop("conv_w"), mine * cols, cols, axis=1)[None]

    def view(k, a):
        a = a.reshape((1, -1) if k == "g_final" else wts[k].shape)
        return jnp.swapaxes(a, -1, -2) if k in NARROW_LAST else a

    def unview(k, a):
        return (jnp.swapaxes(a, -1, -2) if k in NARROW_LAST else a).reshape(wts[k].shape)

    slots = _adamw_small([view(k, wts[k]) for k in SMALL], [view(k, g_small[k]) for k in SMALL],
                         [view(k, moms[k]) for k in SMALL], [view(k, vels[k]) for k in SMALL])
    small_out = [dict(zip(SMALL, [unview(k, a) for k, a in zip(SMALL, slot)])) for slot in slots]

    big_out = {}
    between = [slots[0][0]]
    for group in sorted(ADAMW_GROUPS, key=lambda g: g[0] == "w_in"):
        flip = group[0] in LONG_AXIS_MINOR
        look = (lambda a: a.T) if flip else (lambda a: a)
        last = group[0] == "w_in"
        if last:
            own["w_in"], parts["w_in"] = _chips_wait(send_sems, recv_sems, pair_in, landing, between, "w_in")
        res = _adamw([parts[k] for k in group], [look(shard[k]) for k in group], [look(moms[k][0]) for k in group],
                     [look(vels[k][0]) for k in group], "adamw_" + group[0],
                     transposed=group[0] in TRANSPOSED and not flip,
                     own=[own[k] for k in group] if group[0] in own else None, after=() if last else ordered)
        between.append(res[0][0])
        for k, outs4 in zip(group, res):
            big_out[k] = [look(a) for a in outs4]

    outs = [loss, grad_x[None]]
    for slot in range(4):
        for k in WEIGHTS:
            if k in SHARDED:
                outs.append(big_out[k][slot][None])
            else:
                outs.append(small_out[slot][k])
    return tuple(outs)
```

```python
import math

import jax
import jax.numpy as jnp
from jax import lax
from jax.experimental import pallas as pl
from jax.experimental.pallas import tpu as pltpu

F32 = jnp.float32
BF16 = jnp.bfloat16

EPS = 1e-6
LRU_C = 8.0
CONV_WIDTH = 4
ADAM_LR = 0.001
ADAM_B1 = 0.9
ADAM_B2 = 0.999
ADAM_EPS = 1e-08
ADAM_WD = 0.01
ADAM_STEP = 10

N_DEV = 8
MESH = pl.DeviceIdType.MESH
SUBLANES = 8
LANES = 128
VMEM_LIMIT = 56 * 1024 * 1024
TOKEN_TILE = 256
TIME_CHUNK = 256
S5_SLAB = 128
LRU_SLAB = 256


def _dot(a, b):
    return jnp.dot(a.astype(BF16), b.astype(BF16), preferred_element_type=F32)


def _dot_nt(a, b):
    return lax.dot_general(a.astype(BF16), b.astype(BF16), (((1,), (1,)), ((), ())), preferred_element_type=F32)


def _dot_tn(a, b):
    return lax.dot_general(a.astype(BF16), b.astype(BF16), (((0,), (0,)), ((), ())), preferred_element_type=F32)


def _sigmoid(x):
    return jax.nn.sigmoid(x)


def _rms_stats(x):
    r = lax.rsqrt(jnp.mean(x * x, axis=-1, keepdims=True) + EPS)
    return x * r, r


def _rms_bwd(dy, xhat, r, g):
    dxn = dy * g
    dx = r * (dxn - xhat * jnp.mean(dxn * xhat, axis=-1, keepdims=True))
    return dx, dy * xhat


def _rowsum(v):
    return jnp.sum(v, axis=0, keepdims=True)


def _expm1(x):
    u = jnp.exp(x)
    um1 = u - 1.0
    safe = jnp.where(um1 == 0.0, 1.0, jnp.log(u))
    return jnp.where(um1 == 0.0, x, um1 * x / safe)


def _softplus(x):
    e = jnp.exp(-jnp.abs(x))
    u = 1.0 + e
    um1 = u - 1.0
    safe = jnp.where(um1 == 0.0, 1.0, um1)
    log1p_e = jnp.where(um1 == 0.0, e, jnp.log(u) * e / safe)
    return jnp.maximum(x, 0.0) + log1p_e


_GELU_K = math.sqrt(2.0 / math.pi)
_GELU_C = 0.044715


def _gelu(x):
    return 0.5 * x * (1.0 + jnp.tanh(_GELU_K * (x + _GELU_C * x * x * x)))


def _gelu_grad(x):
    th = jnp.tanh(_GELU_K * (x + _GELU_C * x * x * x))
    return 0.5 * (1.0 + th) + 0.5 * x * (1.0 - th * th) * _GELU_K * (1.0 + 3.0 * _GELU_C * x * x)


def _params(*sem):
    return pltpu.CompilerParams(dimension_semantics=sem, vmem_limit_bytes=VMEM_LIMIT)


def _rows(tm, n):
    return pl.BlockSpec((tm, n), lambda i: (i, 0))


def _rows_rev(tm, n, steps):
    return pl.BlockSpec((tm, n), lambda i: (steps - 1 - i, 0))


def _whole(shape):
    nd = len(shape)
    return pl.BlockSpec(shape, lambda i: (0,) * nd, pipeline_mode=pl.Buffered(1))


def _acc(shape):
    nd = len(shape)
    return pl.BlockSpec(shape, lambda i: (0,) * nd)


def _zero_on_first(*refs):
    @pl.when(pl.program_id(0) == 0)
    def _():
        for r in refs:
            r[...] = jnp.zeros_like(r)


def _inproj_fwd(x, g_mix, w_in_t, b_in, widths, comm=None):
    t, d = x.shape
    n = w_in_t.shape[0]
    tm = min(TOKEN_TILE, t)
    offs = [sum(widths[:i]) for i in range(len(widths) + 1)]

    def body(x_ref, g_ref, w_ref, b_ref, *outs):
        xhat, _ = _rms_stats(x_ref[...])
        h = (xhat * g_ref[...]).astype(BF16)
        for k, o_ref in enumerate(outs):
            lo, hi = offs[k], offs[k + 1]
            o_ref[...] = _dot_nt(h, w_ref[lo:hi, :]) + b_ref[:, lo:hi]

    return _run(
        body, comm, name="inproj_fwd", grid=(t // tm,),
        out_shape=[jax.ShapeDtypeStruct((t, w), F32) for w in widths],
        in_specs=[_rows(tm, d), _whole((1, d)), _whole((n, d)), _whole((1, n))],
        out_specs=[_rows(tm, w) for w in widths],
        semantics="parallel",
    )(x, g_mix, w_in_t, b_in)


def _s5_fwd(u, bbr_blk, bbi_blk, ar, ai, cre_blk, cimn_blk, d_skip, w_glu, b_glu):
    t, sa = u.shape
    ns, _, sw = bbr_blk.shape
    gn = ns * sw
    tc = min(TIME_CHUNK, t)

    def body(u_ref, bbr_ref, bbi_ref, ar_ref, ai_ref, cre_ref, cim_ref, d_ref, wg_ref, bg_ref,
             sr_ref, si_ref, y_ref, ya_ref, cr_s, ci_s, sr_s, si_s):
        _zero_on_first(cr_s, ci_s)
        uv = u_ref[...]
        ub = uv.astype(BF16)
        for m in range(ns):
            um = ub[:, m * S5_SLAB:(m + 1) * S5_SLAB]
            sr_s[:, m * sw:(m + 1) * sw] = _dot(um, bbr_ref[m])
            si_s[:, m * sw:(m + 1) * sw] = _dot(um, bbi_ref[m])
        a_r = ar_ref[...]
        a_i = ai_ref[...]

        def step(row, carry):
            c_r, c_i = carry
            at = pl.ds(row, 1)
            n_r = a_r * c_r - a_i * c_i + sr_s[at, :]
            n_i = a_r * c_i + a_i * c_r + si_s[at, :]
            sr_s[at, :] = n_r
            si_s[at, :] = n_i
            return n_r, n_i

        c_r, c_i = lax.fori_loop(0, tc, step, (cr_s[0:1, :], ci_s[0:1, :]), unroll=8)
        cr_s[0:1, :] = c_r
        ci_s[0:1, :] = c_i
        for m in range(ns):
            states, chans = slice(m * sw, (m + 1) * sw), slice(m * S5_SLAB, (m + 1) * S5_SLAB)
            s_r, s_i = sr_s[:, states].astype(BF16), si_s[:, states].astype(BF16)
            sr_ref[:, states] = s_r.astype(sr_ref.dtype)
            si_ref[:, states] = s_i.astype(si_ref.dtype)
            y_ref[:, chans] = _dot(s_r, cre_ref[m]) + _dot(s_i, cim_ref[m]) + d_ref[:, chans] * uv[:, chans]
        y = y_ref[...]
        zz = _gelu(y)
        q = _dot(zz, wg_ref[...]) + bg_ref[...]
        ya_ref[...] = zz * _sigmoid(q)

    return dict(
        body=body, steps=t // tc, args=(u, bbr_blk, bbi_blk, ar, ai, cre_blk, cimn_blk, d_skip, w_glu, b_glu),
        out_shape=[jax.ShapeDtypeStruct((t, gn), BF16), jax.ShapeDtypeStruct((t, gn), BF16),
                   jax.ShapeDtypeStruct((t, sa), F32), jax.ShapeDtypeStruct((t, sa), F32)],
        in_specs=[_rows(tc, sa), _whole(bbr_blk.shape), _whole(bbi_blk.shape), _whole((1, gn)), _whole((1, gn)),
                  _whole(cre_blk.shape), _whole(cimn_blk.shape), _whole((1, sa)), _whole((sa, sa)), _whole((1, sa))],
        out_specs=[_rows(tc, gn), _rows(tc, gn), _rows(tc, sa), _rows(tc, sa)],
        scratch=[pltpu.VMEM((SUBLANES, gn), F32), pltpu.VMEM((SUBLANES, gn), F32),
                 pltpu.VMEM((tc, gn), F32), pltpu.VMEM((tc, gn), F32)])


def _stages(stages, name, comm=None):
    counts = [[len(s[k]) for s in stages] for k in ("args", "out_shape", "scratch")]

    def body(*refs):
        groups, pos = [], 0
        for kind in counts:
            per_stage = []
            for c in kind:
                per_stage.append(refs[pos:pos + c])
                pos += c
            groups.append(per_stage)
        for s, ins, outs, scratch in zip(stages, *groups):
            s["body"](*ins, *outs, *scratch)

    res = _run(
        body, comm, name=name, grid=(stages[0]["steps"],),
        out_shape=[o for s in stages for o in s["out_shape"]], in_specs=[i for s in stages for i in s["in_specs"]],
        out_specs=[o for s in stages for o in s["out_specs"]], scratch_shapes=[c for s in stages for c in s["scratch"]],
        semantics="arbitrary",
    )(*[a for s in stages for a in s["args"]])
    extra = None
    if comm is not None:
        res, extra = res
    per_stage, pos = [], 0
    for c in counts[1]:
        per_stage.append(list(res[pos:pos + c]))
        pos += c
    return per_stage if comm is None else (per_stage, extra)


def _slab_dot(x, w_ref, transposed=False):
    dot = _dot_nt if transposed else _dot
    xb = x.astype(BF16)
    return jnp.concatenate([dot(xb[:, j * LRU_SLAB:(j + 1) * LRU_SLAB], w_ref[j]) for j in range(w_ref.shape[0])], axis=1)


def _lru_gates(xc, wr_ref, br_ref, wi_ref, bi_ref, lam_ref):
    r = _sigmoid(_slab_dot(xc, wr_ref) + br_ref[...])
    ig = _sigmoid(_slab_dot(xc, wi_ref) + bi_ref[...])
    sp = _softplus(-lam_ref[...])
    log_a = (-LRU_C * r) * sp
    return r, ig, sp, log_a


def _lru_fwd(u, conv_w, conv_b, wr_blk, b_r, wi_blk, b_i, lru_lambda):
    t, w = u.shape
    tc = min(TIME_CHUNK, t)
    halo = SUBLANES

    def body(u_ref, cw_ref, cb_ref, wr_ref, br_ref, wi_ref, bi_ref, lam_ref,
             xc_ref, h_ref, hp_ref, ext_s, a_s, carry_s):
        @pl.when(pl.program_id(0) == 0)
        def _():
            ext_s[0:halo, :] = jnp.zeros((halo, w), F32)
            carry_s[...] = jnp.zeros_like(carry_s)

        ext_s[halo:halo + tc, :] = u_ref[...]
        xc = cb_ref[...]
        for k in range(CONV_WIDTH):
            off = halo - (CONV_WIDTH - 1) + k
            xc = xc + cw_ref[k:k + 1, :] * ext_s[off:off + tc, :]
        ext_s[0:halo, :] = ext_s[tc:tc + halo, :]
        xc_ref[...] = xc
        r, ig, sp, log_a = _lru_gates(xc, wr_ref, br_ref, wi_ref, bi_ref, lam_ref)
        a_s[...] = jnp.exp(log_a)
        h_ref[...] = jnp.sqrt(-_expm1(2.0 * log_a)) * ig * xc

        def step(row, carry):
            at = pl.ds(row, 1)
            hp_ref[at, :] = carry
            nxt = a_s[at, :] * carry + h_ref[at, :]
            h_ref[at, :] = nxt
            return nxt

        carry_s[0:1, :] = lax.fori_loop(0, tc, step, carry_s[0:1, :], unroll=8)

    return dict(
        body=body, steps=t // tc, args=(u, conv_w, conv_b, wr_blk, b_r, wi_blk, b_i, lru_lambda),
        out_shape=[jax.ShapeDtypeStruct((t, w), F32)] * 3,
        in_specs=[_rows(tc, w), _whole((CONV_WIDTH, w)), _whole((1, w)), _whole(wr_blk.shape), _whole((1, w)),
                  _whole(wi_blk.shape), _whole((1, w)), _whole((1, w))],
        out_specs=[_rows(tc, w)] * 3,
        scratch=[pltpu.VMEM((halo + tc, w), F32), pltpu.VMEM((tc, w), F32), pltpu.VMEM((SUBLANES, w), F32)])


def _merge_ffn_up_fwd(y_a, h, za, zb, x, w_a_out_t, w_b_out, w_o, g_ffn, w_gate_t, w_up_t, comm=None):
    t, d = x.shape
    sa, lw = y_a.shape[1], h.shape[1]
    f = w_gate_t.shape[0]
    tm = min(TOKEN_TILE, t)

    def body(ya_ref, h_ref, za_ref, zb_ref, x_ref, wa_ref, wb_ref, wo_ref, g_ref, wg_ref, wu_ref,
             x1_ref, mg_ref, ma_ref, mb_ref, fg_ref, fu_ref):
        ma = _dot_nt(ya_ref[...], wa_ref[...])
        mb = _dot(h_ref[...], wb_ref[...])
        merged = _sigmoid(za_ref[...]) * ma + _sigmoid(zb_ref[...]) * mb
        ma_ref[...] = ma.astype(ma_ref.dtype)
        mb_ref[...] = mb.astype(mb_ref.dtype)
        mg_ref[...] = merged.astype(mg_ref.dtype)
        x1 = x_ref[...] + _dot(merged, wo_ref[...])
        x1_ref[...] = x1
        xhat, _ = _rms_stats(x1)
        h2 = (xhat * g_ref[...]).astype(BF16)
        fg_ref[...] = _dot_nt(h2, wg_ref[...]).astype(fg_ref.dtype)
        fu_ref[...] = _dot_nt(h2, wu_ref[...]).astype(fu_ref.dtype)

    return _run(
        body, comm, name="merge_ffn_up_fwd", grid=(t // tm,),
        out_shape=[jax.ShapeDtypeStruct((t, d), F32), jax.ShapeDtypeStruct((t, d), BF16),
                   jax.ShapeDtypeStruct((t, d), BF16), jax.ShapeDtypeStruct((t, d), BF16),
                   jax.ShapeDtypeStruct((t, f), BF16), jax.ShapeDtypeStruct((t, f), BF16)],
        in_specs=[_rows(tm, sa), _rows(tm, lw), _rows(tm, d), _rows(tm, d), _rows(tm, d),
                  _whole((d, sa)), _whole((lw, d)), _whole((d, d)), _whole((1, d)), _whole((f, d)), _whole((f, d))],
        out_specs=[_rows(tm, d)] * 4 + [_rows(tm, f)] * 2,
        semantics="parallel",
    )(y_a, h, za, zb, x, w_a_out_t, w_b_out, w_o, g_ffn, w_gate_t, w_up_t)


def _ffn_down_fwd(fg, fu, x1, w_down, comm=None):
    t, d = x1.shape
    f = fg.shape[1]
    tm = min(TOKEN_TILE, t)

    def body(fg_ref, fu_ref, x_ref, wd_ref, x2_ref):
        fgv = fg_ref[...].astype(F32)
        act = fgv * _sigmoid(fgv) * fu_ref[...].astype(F32)
        x2_ref[...] = x_ref[...] + _dot(act, wd_ref[...])

    return _run(
        body, comm, name="ffn_down_fwd", grid=(t // tm,),
        out_shape=jax.ShapeDtypeStruct((t, d), F32),
        in_specs=[_rows(tm, f), _rows(tm, f), _rows(tm, d), _whole((f, d))],
        out_specs=_rows(tm, d),
        semantics="parallel",
    )(fg, fu, x1, w_down)


def _store_on_last(pairs):
    @pl.when(pl.program_id(0) == pl.num_programs(0) - 1)
    def _():
        for acc, out in pairs:
            out[...] = acc[...].astype(out.dtype)


def _tail_fwd_bwd(x2, p, target, g_pg, w_pg, b_pg, w_ple_t, g_ple, g_final):
    t, d = x2.shape
    pd = p.shape[1]
    tm = min(TOKEN_TILE, t)

    def body(x2_ref, p_ref, tg_ref, gpg_ref, wpg_ref, bpg_ref, wple_ref, gple_ref, gfin_ref,
             dx2_ref, loss_ref, dwpg_out, dwple_out, vec_ref, dwpg_ref, dwple_ref):
        _zero_on_first(loss_ref, dwpg_ref, dwple_ref, vec_ref)
        x2v = x2_ref[...]
        xh2, r2 = _rms_stats(x2v)
        h3 = xh2 * gpg_ref[...]
        gp = _sigmoid(_dot(h3, wpg_ref[...]) + bpg_ref[...])
        pe = _dot_nt(p_ref[...], wple_ref[...])
        peh, r3 = _rms_stats(pe)
        e = peh * gple_ref[...]
        x3 = x2v + gp * e
        xh3, r4 = _rms_stats(x3)
        diff = xh3 * gfin_ref[...] - tg_ref[...]
        loss_ref[...] += 0.5 * jnp.sum(jnp.mean(diff * diff, axis=-1, keepdims=True))
        dy = diff * (1.0 / d)
        dx3, dgfin = _rms_bwd(dy, xh3, r4, gfin_ref[...])
        d_gp = dx3 * e
        d_e = dx3 * gp
        dpe, dgple = _rms_bwd(d_e, peh, r3, gple_ref[...])
        dwple_ref[...] += _dot_tn(dpe, p_ref[...])
        dpre = d_gp * gp * (1.0 - gp)
        dwpg_ref[...] += _dot_tn(h3, dpre)
        dh3 = _dot_nt(dpre, wpg_ref[...])
        dx2n, dgpg = _rms_bwd(dh3, xh2, r2, gpg_ref[...])
        dx2_ref[...] = dx3 + dx2n
        vec_ref[0:1, :] += _rowsum(dpre)
        vec_ref[1:2, :] += _rowsum(dgpg)
        vec_ref[2:3, :] += _rowsum(dgple)
        vec_ref[3:4, :] += _rowsum(dgfin)
        _store_on_last([(dwpg_ref, dwpg_out), (dwple_ref, dwple_out)])

    return pl.pallas_call(
        body, name="tail_fwd_bwd", grid=(t // tm,),
        out_shape=[jax.ShapeDtypeStruct((t, d), F32), jax.ShapeDtypeStruct((SUBLANES, LANES), F32),
                   jax.ShapeDtypeStruct((d, d), BF16), jax.ShapeDtypeStruct((d, pd), BF16),
                   jax.ShapeDtypeStruct((SUBLANES, d), F32)],
        in_specs=[_rows(tm, d), _rows(tm, pd), _rows(tm, d), _whole((1, d)), _whole((d, d)), _whole((1, d)),
                  _whole((d, pd)), _whole((1, d)), _whole((1, d))],
        out_specs=[_rows(tm, d), _acc((SUBLANES, LANES)), _acc((d, d)), _acc((d, pd)), _acc((SUBLANES, d))],
        scratch_shapes=[pltpu.VMEM((d, d), F32), pltpu.VMEM((d, pd), F32)],
        compiler_params=_params("arbitrary"),
    )(x2, p, target, g_pg, w_pg, b_pg, w_ple_t, g_ple, g_final)


def _ffn_bwd_a(dx2, fg, fu, w_down, comm=None):
    t, d = dx2.shape
    f = fg.shape[1]
    tm = min(TOKEN_TILE, t)

    def body(dx_ref, fg_ref, fu_ref, wd_ref, dfg_ref, dfu_ref, act_ref):
        dact = _dot_nt(dx_ref[...], wd_ref[...])
        fgv = fg_ref[...].astype(F32)
        fuv = fu_ref[...].astype(F32)
        sg = _sigmoid(fgv)
        silu = fgv * sg
        dfu_ref[...] = (dact * silu).astype(dfu_ref.dtype)
        dfg_ref[...] = (dact * fuv * (sg * (1.0 + fgv * (1.0 - sg)))).astype(dfg_ref.dtype)
        act_ref[...] = (silu * fuv).astype(act_ref.dtype)

    return _run(
        body, comm, name="ffn_bwd_a", grid=(t // tm,),
        out_shape=[jax.ShapeDtypeStruct((t, f), BF16)] * 3,
        in_specs=[_rows(tm, d), _rows(tm, f), _rows(tm, f), _whole((f, d))],
        out_specs=[_rows(tm, f)] * 3,
        semantics="parallel",
    )(dx2, fg, fu, w_down)


def _ffn_bwd_b(dfg, dfu, x1, dx2, g_ffn, w_gate_t, w_up_t, comm=None):
    t, d = x1.shape
    f = dfg.shape[1]
    tm = min(TOKEN_TILE, t)

    def body(dfg_ref, dfu_ref, x_ref, dx2_ref, g_ref, wg_ref, wu_ref, dx1_ref, h2_ref, vec_ref):
        _zero_on_first(vec_ref)
        dh2 = _dot(dfg_ref[...], wg_ref[...]) + _dot(dfu_ref[...], wu_ref[...])
        xhat, r = _rms_stats(x_ref[...])
        h2_ref[...] = (xhat * g_ref[...]).astype(h2_ref.dtype)
        dxn, dg = _rms_bwd(dh2, xhat, r, g_ref[...])
        dx1_ref[...] = dx2_ref[...] + dxn
        vec_ref[0:1, :] += _rowsum(dg)

    return _run(
        body, comm, name="ffn_bwd_b", grid=(t // tm,),
        out_shape=[jax.ShapeDtypeStruct((t, d), F32), jax.ShapeDtypeStruct((t, d), BF16),
                   jax.ShapeDtypeStruct((SUBLANES, d), F32)],
        in_specs=[_rows(tm, f), _rows(tm, f), _rows(tm, d), _rows(tm, d), _whole((1, d)), _whole((f, d)), _whole((f, d))],
        out_specs=[_rows(tm, d), _rows(tm, d), _acc((SUBLANES, d))],
        semantics="arbitrary",
    )(dfg, dfu, x1, dx2, g_ffn, w_gate_t, w_up_t)


def _matmul_tn(a, b, tk, name, dtype=F32):
    t, k = a.shape
    n = b.shape[1]
    assert k % tk == 0 and tk % LANES == 0, (k, tk)

    def body(a_ref, b_ref, o_ref):
        o_ref[...] = _dot_tn(a_ref[...], b_ref[...]).astype(o_ref.dtype)

    return _run(
        body, None, name=name, grid=(k // tk,),
        out_shape=jax.ShapeDtypeStruct((k, n), dtype),
        in_specs=[pl.BlockSpec((t, tk), lambda j: (0, j)), _whole((t, n))],
        out_specs=pl.BlockSpec((tk, n), lambda j: (j, 0)),
        semantics="parallel",
    )(a, b)


def _merge_bwd(dx1, merged, ma, mb, za, zb, y_a, h, w_o, w_a_out_t, w_b_out, comm=None):
    t, d = dx1.shape
    sa, lw = y_a.shape[1], h.shape[1]
    tm = min(TOKEN_TILE, t)

    def body(dx1_ref, mg_ref, ma_ref, mb_ref, za_ref, zb_ref, ya_ref, h_ref, wo_ref, wa_ref, wb_ref,
             dza_ref, dzb_ref, dya_ref, dyb_ref, dwo_out, dwa_out, dwb_out, dwo_ref, dwa_ref, dwb_ref):
        _zero_on_first(dwo_ref, dwa_ref, dwb_ref)
        dx1v = dx1_ref[...].astype(BF16)
        dmg = _dot_nt(dx1v, wo_ref[...])
        ga = _sigmoid(za_ref[...])
        gb = _sigmoid(zb_ref[...])
        dza_ref[...] = (dmg * ma_ref[...].astype(F32) * ga * (1.0 - ga)).astype(dza_ref.dtype)
        dzb_ref[...] = (dmg * mb_ref[...].astype(F32) * gb * (1.0 - gb)).astype(dzb_ref.dtype)
        dma = (dmg * ga).astype(BF16)
        dmb = (dmg * gb).astype(BF16)
        dya_ref[...] = _dot(dma, wa_ref[...])
        dyb_ref[...] = _dot_nt(dmb, wb_ref[...])
        dwo_ref[...] += _dot_tn(mg_ref[...], dx1v)
        dwa_ref[...] += _dot_tn(dma, ya_ref[...])
        dwb_ref[...] += _dot_tn(h_ref[...], dmb)
        _store_on_last([(dwo_ref, dwo_out), (dwa_ref, dwa_out), (dwb_ref, dwb_out)])

    return _run(
        body, comm, name="merge_bwd", grid=(t // tm,),
        out_shape=[jax.ShapeDtypeStruct((t, d), BF16), jax.ShapeDtypeStruct((t, d), BF16),
                   jax.ShapeDtypeStruct((t, sa), F32), jax.ShapeDtypeStruct((t, lw), F32),
                   jax.ShapeDtypeStruct((d, d), BF16), jax.ShapeDtypeStruct((d, sa), BF16),
                   jax.ShapeDtypeStruct((lw, d), BF16)],
        in_specs=[_rows(tm, d), _rows(tm, d), _rows(tm, d), _rows(tm, d), _rows(tm, d), _rows(tm, d),
                  _rows(tm, sa), _rows(tm, lw), _whole((d, d)), _whole((d, sa)), _whole((lw, d))],
        out_specs=[_rows(tm, d), _rows(tm, d), _rows(tm, sa), _rows(tm, lw), _acc((d, d)), _acc((d, sa)), _acc((lw, d))],
        scratch_shapes=[pltpu.VMEM((d, d), F32), pltpu.VMEM((d, sa), F32), pltpu.VMEM((lw, d), F32)],
        semantics="arbitrary",
    )(dx1, merged, ma, mb, za, zb, y_a, h, w_o, w_a_out_t, w_b_out)


def _fold_diag_blocks(dense, row_group, col_group):
    r, c = dense.shape
    rows = lax.broadcasted_iota(jnp.int32, (r, c), 0)
    cols = lax.broadcasted_iota(jnp.int32, (r, c), 1)
    kept = jnp.where(rows // row_group == cols // col_group, dense, 0.0)
    pick = (lax.broadcasted_iota(jnp.int32, (row_group, r), 0)
            == lax.broadcasted_iota(jnp.int32, (row_group, r), 1) % row_group).astype(F32)
    return jnp.dot(pick, kept, preferred_element_type=F32, precision=lax.Precision.HIGHEST)


def _lru_bwd(dh, xc, hprev, u, conv_w, wr_blk, b_r, wi_blk, b_i, lru_lambda, head_dim, comm=None):
    t, w = dh.shape
    tc = min(TIME_CHUNK, t)
    steps = t // tc
    halo = SUBLANES
    sub_per_chunk = tc // halo
    slabs = w // LRU_SLAB

    def body(dh_ref, xc_ref, hp_ref, u_ref, uh_ref, cw_ref, wr_ref, br_ref, wi_ref, bi_ref, lam_ref,
             du_ref, dwr_out, dwi_out, vec_ref, lam_s, a_s, dxc_s, uext_s, carry_s, dwr_ref, dwi_ref):
        chunk = steps - 1 - pl.program_id(0)

        @pl.when(pl.program_id(0) == 0)
        def _():
            carry_s[...] = jnp.zeros_like(carry_s)
            dxc_s[tc:tc + halo, :] = jnp.zeros((halo, w), F32)
            dwr_ref[...] = jnp.zeros_like(dwr_ref)
            dwi_ref[...] = jnp.zeros_like(dwi_ref)
            vec_ref[...] = jnp.zeros_like(vec_ref)

        xc = xc_ref[...]
        r, ig, sp, log_a = _lru_gates(xc, wr_ref, br_ref, wi_ref, bi_ref, lam_ref)
        a = jnp.exp(log_a)
        a_s[...] = a

        def step(i, q):
            at = pl.ds(tc - 1 - i, 1)
            lam_row = dh_ref[at, :] + q
            lam_s[at, :] = lam_row
            return a_s[at, :] * lam_row

        carry_s[0:1, :] = lax.fori_loop(0, tc, step, carry_s[0:1, :], unroll=8)
        lam = lam_s[...]
        mult = jnp.sqrt(-_expm1(2.0 * log_a))
        d_log_a = lam * hp_ref[...] * a - (lam * ig * xc) * (a * a) / mult
        d_ig = lam * mult * xc
        dpre_r = (d_log_a * (-LRU_C * sp)) * r * (1.0 - r)
        dpre_i = d_ig * ig * (1.0 - ig)
        dxc = lam * mult * ig + _slab_dot(dpre_r, wr_ref, transposed=True) + _slab_dot(dpre_i, wi_ref, transposed=True)
        xcb, drb, dib = xc.astype(BF16), dpre_r.astype(BF16), dpre_i.astype(BF16)
        for j in range(slabs):
            cols = slice(j * LRU_SLAB, (j + 1) * LRU_SLAB)
            dwr_ref[j] += _dot_tn(drb[:, cols], xcb[:, cols])
            dwi_ref[j] += _dot_tn(dib[:, cols], xcb[:, cols])
        vec_ref[0:1, :] += _rowsum(dxc)
        vec_ref[1:2, :] += _rowsum(dpre_r)
        vec_ref[2:3, :] += _rowsum(dpre_i)
        vec_ref[3:4, :] += _rowsum(d_log_a * (-LRU_C * r)) * (-_sigmoid(-lam_ref[...]))
        dxc_s[0:tc, :] = dxc
        du = cw_ref[CONV_WIDTH - 1:CONV_WIDTH, :] * dxc
        for k in range(CONV_WIDTH - 1):
            off = CONV_WIDTH - 1 - k
            du = du + cw_ref[k:k + 1, :] * dxc_s[off:off + tc, :]
        du_ref[...] = du.astype(du_ref.dtype)
        dxc_s[tc:tc + halo, :] = dxc_s[0:halo, :]
        uext_s[0:halo, :] = jnp.where(chunk > 0, uh_ref[...], 0.0)
        uext_s[halo:halo + tc, :] = u_ref[...]
        for k in range(CONV_WIDTH):
            off = halo - (CONV_WIDTH - 1) + k
            vec_ref[4 + k:5 + k, :] += _rowsum(dxc * uext_s[off:off + tc, :])

        @pl.when(pl.program_id(0) == steps - 1)
        def _():
            for j in range(slabs):
                cols = slice(j * LRU_SLAB, (j + 1) * LRU_SLAB)
                dwr_out[:, cols] = _fold_diag_blocks(dwr_ref[j], head_dim, head_dim).astype(dwr_out.dtype)
                dwi_out[:, cols] = _fold_diag_blocks(dwi_ref[j], head_dim, head_dim).astype(dwi_out.dtype)

    halo_spec = pl.BlockSpec((halo, w), lambda i: (jnp.maximum((steps - 1 - i) * sub_per_chunk - 1, 0), 0))
    return _run(
        body, comm, name="lru_bwd", grid=(steps,),
        out_shape=[jax.ShapeDtypeStruct((t, w), BF16), jax.ShapeDtypeStruct((head_dim, w), BF16),
                   jax.ShapeDtypeStruct((head_dim, w), BF16), jax.ShapeDtypeStruct((SUBLANES, w), F32)],
        in_specs=[_rows_rev(tc, w, steps)] * 4 + [halo_spec, _whole((CONV_WIDTH, w)), _whole(wr_blk.shape),
                                                  _whole((1, w)), _whole(wi_blk.shape), _whole((1, w)), _whole((1, w))],
        out_specs=[_rows_rev(tc, w, steps), _acc((head_dim, w)), _acc((head_dim, w)), _acc((SUBLANES, w))],
        scratch_shapes=[pltpu.VMEM((tc, w), F32), pltpu.VMEM((tc, w), F32), pltpu.VMEM((tc + halo, w), F32),
                        pltpu.VMEM((halo + tc, w), F32), pltpu.VMEM((SUBLANES, w), F32),
                        pltpu.VMEM((slabs, LRU_SLAB, LRU_SLAB), F32), pltpu.VMEM((slabs, LRU_SLAB, LRU_SLAB), F32)],
        semantics="arbitrary",
    )(dh, xc, hprev, u, u, conv_w, wr_blk, b_r, wi_blk, b_i, lru_lambda)


def _s5_bwd(dya, y, sr, si, u, w_glu, b_glu, cre_blk, cimn_blk, bbr_blk, bbi_blk, ar, ai, d_skip, comm=None):
    t, sa = dya.shape
    gn = sr.shape[1]
    ns, _, sw = bbr_blk.shape
    tc = min(TIME_CHUNK, t)
    steps = t // tc
    halo = SUBLANES

    def body(dya_ref, y_ref, sr_ref, si_ref, u_ref, wg_ref, bg_ref, cre_ref, cim_ref, bbr_ref, bbi_ref,
             ar_ref, ai_ref, d_ref, du_ref, lr_ref, li_ref, dy_ref, dwg_out, vsa_ref, vgn_ref, gr_s, gi_s, cr_s, ci_s,
             dwg_ref):
        @pl.when(pl.program_id(0) == 0)
        def _():
            cr_s[...] = jnp.zeros_like(cr_s)
            ci_s[...] = jnp.zeros_like(ci_s)
            gr_s[tc:tc + halo, :] = jnp.zeros((halo, gn), F32)
            gi_s[tc:tc + halo, :] = jnp.zeros((halo, gn), F32)
            dwg_ref[...] = jnp.zeros_like(dwg_ref)
            vsa_ref[...] = jnp.zeros_like(vsa_ref)
            vgn_ref[...] = jnp.zeros_like(vgn_ref)

        yv = y_ref[...]
        uv = u_ref[...]
        zz = _gelu(yv)
        sg = _sigmoid(_dot(zz, wg_ref[...]) + bg_ref[...])
        dyav = dya_ref[...]
        dq = dyav * zz * sg * (1.0 - sg)
        dzz = dyav * sg + _dot_nt(dq, wg_ref[...])
        dwg_ref[...] += _dot_tn(zz, dq)
        dy = dzz * _gelu_grad(yv)
        dyb = dy.astype(BF16)
        dy_ref[...] = dyb.astype(dy_ref.dtype)
        vsa_ref[0:1, :] += _rowsum(dq)
        vsa_ref[1:2, :] += _rowsum(dy * uv)
        for m in range(ns):
            dym = dyb[:, m * S5_SLAB:(m + 1) * S5_SLAB]
            gr_s[0:tc, m * sw:(m + 1) * sw] = _dot_nt(dym, cre_ref[m])
            gi_s[0:tc, m * sw:(m + 1) * sw] = _dot_nt(dym, cim_ref[m])
        a_r = ar_ref[...]
        a_i = ai_ref[...]

        def step(i, carry):
            l_r, l_i = carry
            at = pl.ds(tc - 1 - i, 1)
            n_r = gr_s[at, :] + a_r * l_r + a_i * l_i
            n_i = gi_s[at, :] + a_r * l_i - a_i * l_r
            gr_s[at, :] = n_r
            gi_s[at, :] = n_i
            return n_r, n_i

        l_r, l_i = lax.fori_loop(0, tc, step, (cr_s[0:1, :], ci_s[0:1, :]), unroll=8)
        cr_s[0:1, :] = l_r
        ci_s[0:1, :] = l_i
        nxt_r = gr_s[1:tc + 1, :]
        nxt_i = gi_s[1:tc + 1, :]
        srv = sr_ref[...].astype(F32)
        siv = si_ref[...].astype(F32)
        vgn_ref[0:1, :] += _rowsum(nxt_r * srv + nxt_i * siv)
        vgn_ref[1:2, :] += _rowsum(nxt_i * srv - nxt_r * siv)
        lam_r = gr_s[0:tc, :]
        lam_i = gi_s[0:tc, :]
        gr_s[tc:tc + halo, :] = gr_s[0:halo, :]
        gi_s[tc:tc + halo, :] = gi_s[0:halo, :]
        lrb = lam_r.astype(BF16)
        lib = lam_i.astype(BF16)
        lr_ref[...] = lrb.astype(lr_ref.dtype)
        li_ref[...] = lib.astype(li_ref.dtype)
        for m in range(ns):
            states, chans = slice(m * sw, (m + 1) * sw), slice(m * S5_SLAB, (m + 1) * S5_SLAB)
            du_ref[:, chans] = (_dot_nt(lrb[:, states], bbr_ref[m]) + _dot_nt(lib[:, states], bbi_ref[m])
                                + dy[:, chans] * d_ref[:, chans]).astype(du_ref.dtype)
        _store_on_last([(dwg_ref, dwg_out)])

    return _run(
        body, comm, name="s5_bwd", grid=(steps,),
        out_shape=[jax.ShapeDtypeStruct((t, sa), BF16), jax.ShapeDtypeStruct((t, gn), BF16),
                   jax.ShapeDtypeStruct((t, gn), BF16), jax.ShapeDtypeStruct((t, sa), BF16),
                   jax.ShapeDtypeStruct((sa, sa), BF16), jax.ShapeDtypeStruct((SUBLANES, sa), F32),
                   jax.ShapeDtypeStruct((SUBLANES, gn), F32)],
        in_specs=[_rows_rev(tc, sa, steps), _rows_rev(tc, sa, steps), _rows_rev(tc, gn, steps), _rows_rev(tc, gn, steps),
                  _rows_rev(tc, sa, steps), _whole((sa, sa)), _whole((1, sa)), _whole(cre_blk.shape),
                  _whole(cimn_blk.shape), _whole(bbr_blk.shape), _whole(bbi_blk.shape), _whole((1, gn)), _whole((1, gn)),
                  _whole((1, sa))],
        out_specs=[_rows_rev(tc, sa, steps), _rows_rev(tc, gn, steps), _rows_rev(tc, gn, steps), _rows_rev(tc, sa, steps),
                   _acc((sa, sa)), _acc((SUBLANES, sa)), _acc((SUBLANES, gn))],
        scratch_shapes=[pltpu.VMEM((tc + halo, gn), F32), pltpu.VMEM((tc + halo, gn), F32),
                        pltpu.VMEM((SUBLANES, gn), F32), pltpu.VMEM((SUBLANES, gn), F32), pltpu.VMEM((sa, sa), F32)],
        semantics="arbitrary",
    )(dya, y, sr, si, u, w_glu, b_glu, cre_blk, cimn_blk, bbr_blk, bbi_blk, ar, ai, d_skip)


def _inproj_bwd(dparts, x, dx1, g_mix, w_in_t, comm=None):
    t, d = x.shape
    n = w_in_t.shape[0]
    widths = [p.shape[1] for p in dparts]
    offs = [sum(widths[:i]) for i in range(len(widths) + 1)]
    tm = min(TOKEN_TILE, t)
    np_ = len(dparts)

    def body(*refs):
        dz_refs = refs[:np_]
        x_ref, dx1_ref, g_ref, w_ref, gx_ref, h_ref, vd_ref, vn_ref = refs[np_:]
        _zero_on_first(vd_ref, vn_ref)
        dh = jnp.zeros((tm, d), F32)
        for k, r in enumerate(dz_refs):
            lo, hi = offs[k], offs[k + 1]
            dzk = r[...]
            dh = dh + _dot(dzk, w_ref[lo:hi, :])
            vn_ref[0:1, lo:hi] += _rowsum(dzk.astype(F32))
        xhat, r0 = _rms_stats(x_ref[...])
        h_ref[...] = (xhat * g_ref[...]).astype(h_ref.dtype)
        dxn, dg = _rms_bwd(dh, xhat, r0, g_ref[...])
        gx_ref[...] = dx1_ref[...] + dxn
        vd_ref[0:1, :] += _rowsum(dg)

    return _run(
        body, comm, name="inproj_bwd", grid=(t // tm,),
        out_shape=[jax.ShapeDtypeStruct((t, d), F32), jax.ShapeDtypeStruct((t, d), BF16),
                   jax.ShapeDtypeStruct((SUBLANES, d), F32), jax.ShapeDtypeStruct((SUBLANES, n), F32)],
        in_specs=[_rows(tm, w) for w in widths] + [_rows(tm, d), _rows(tm, d), _whole((1, d)), _whole((n, d))],
        out_specs=[_rows(tm, d), _rows(tm, d), _acc((SUBLANES, d)), _acc((SUBLANES, n))],
        semantics="arbitrary",
    )(*dparts, x, dx1, g_mix, w_in_t)


def _dw_from_parts(dparts, h, tn, name, comm=None):
    t, d = h.shape
    widths = [p.shape[1] for p in dparts]
    offs = [sum(widths[:i]) for i in range(len(widths) + 1)]
    np_ = len(dparts)

    def body(*refs):
        h_ref, o_ref = refs[np_], refs[np_ + 1]
        hv = h_ref[...]
        for k, r in enumerate(refs[:np_]):
            o_ref[offs[k]:offs[k + 1], :] = _dot_tn(r[...], hv).astype(o_ref.dtype)

    return _run(
        body, comm, name=name, grid=(d // tn,),
        out_shape=jax.ShapeDtypeStruct((offs[-1], d), BF16),
        in_specs=[_whole(p.shape) for p in dparts] + [pl.BlockSpec((t, tn), lambda j: (0, j))],
        out_specs=pl.BlockSpec((offs[-1], tn), lambda j: (0, j)),
        semantics="parallel",
    )(*dparts, h)


def _prep(lr_row, li_row, ldt_row, lr_col, li_col, ldt_col, b_re, b_im, c_re, c_im, w_r, w_i, comm=None):
    gn, pch = b_re.shape
    sa, n = c_re.shape
    w, hd = w_r.shape
    ns, sw, lsl = sa // S5_SLAB, S5_SLAB * n // pch, w // LRU_SLAB

    def spread_cols(vals, row_group, col_group, width):
        r, k = vals.shape
        tile = (lax.broadcasted_iota(jnp.int32, (k, width), 0) == lax.broadcasted_iota(jnp.int32, (k, width), 1) % k)
        rows = lax.broadcasted_iota(jnp.int32, (r, width), 0) // row_group
        cols = lax.broadcasted_iota(jnp.int32, (r, width), 1) // col_group
        return jnp.where(rows == cols, _dot(vals, tile.astype(BF16)), 0.0)

    def spread_rows(vals, row_group, col_group, height):
        k, c = vals.shape
        tile = (lax.broadcasted_iota(jnp.int32, (height, k), 0) % k == lax.broadcasted_iota(jnp.int32, (height, k), 1))
        rows = lax.broadcasted_iota(jnp.int32, (height, c), 0) // row_group
        cols = lax.broadcasted_iota(jnp.int32, (height, c), 1) // col_group
        return jnp.where(rows == cols, _dot(tile.astype(BF16), vals), 0.0)

    def body(lrr, lir, ldr, lrc, lic, ldc, bre, bim, cre, cim, wr, wi,
             ar_o, ai_o, bbr_o, bbi_o, cre_o, cim_o, wr_o, wi_o):
        ar, ai, _, _ = _disc_scalars(lrr[...], lir[...], ldr[...])
        ar_o[...] = ar
        ai_o[...] = ai
        _, _, bbr, bbi = _disc_cols(lrc[...], lic[...], ldc[...], bre[...], bim[...])
        bbr_t, bbi_t = bbr.T, bbi.T
        for m in range(ns):
            bbr_o[m] = spread_rows(bbr_t[:, m * sw:(m + 1) * sw], pch, n, S5_SLAB).astype(bbr_o.dtype)
            bbi_o[m] = spread_rows(bbi_t[:, m * sw:(m + 1) * sw], pch, n, S5_SLAB).astype(bbi_o.dtype)
            rows = slice(m * S5_SLAB, (m + 1) * S5_SLAB)
            cre_o[m] = spread_rows(cre[rows, :].T, n, pch, sw).astype(cre_o.dtype)
            cim_o[m] = spread_rows(-cim[rows, :].T, n, pch, sw).astype(cim_o.dtype)
        for j in range(lsl):
            rows = slice(j * LRU_SLAB, (j + 1) * LRU_SLAB)
            wr_o[j] = spread_cols(wr[rows, :], hd, hd, LRU_SLAB).astype(wr_o.dtype)
            wi_o[j] = spread_cols(wi[rows, :], hd, hd, LRU_SLAB).astype(wi_o.dtype)

    args = (lr_row, li_row, ldt_row, lr_col, li_col, ldt_col, b_re, b_im, c_re, c_im, w_r, w_i)
    out_shape = [jax.ShapeDtypeStruct((1, gn), F32), jax.ShapeDtypeStruct((1, gn), F32),
                 jax.ShapeDtypeStruct((ns, S5_SLAB, sw), BF16), jax.ShapeDtypeStruct((ns, S5_SLAB, sw), BF16),
                 jax.ShapeDtypeStruct((ns, sw, S5_SLAB), BF16), jax.ShapeDtypeStruct((ns, sw, S5_SLAB), BF16),
                 jax.ShapeDtypeStruct((lsl, LRU_SLAB, LRU_SLAB), BF16),
                 jax.ShapeDtypeStruct((lsl, LRU_SLAB, LRU_SLAB), BF16)]
    return _run(
        body, comm, name="prep", grid=(1,), out_shape=out_shape, in_specs=[_whole(a.shape) for a in args],
        out_specs=[_acc(s.shape) for s in out_shape], semantics="arbitrary",
    )(*args)


def _s5_param_grads(lam_r, lam_i, sr, si, u, dy, pch, n, comm=None):
    t, gn = lam_r.shape
    sa = u.shape[1]
    sw = S5_SLAB * n // pch

    def body(lr_ref, li_ref, sr_ref, si_ref, u_ref, dy_ref, dbr_ref, dbi_ref, dcr_ref, dci_ref):
        uv = u_ref[...]
        dyv = dy_ref[...]
        dbr_ref[...] = _fold_diag_blocks(_dot_tn(uv, lr_ref[...]), pch, n).astype(dbr_ref.dtype)
        dbi_ref[...] = _fold_diag_blocks(_dot_tn(uv, li_ref[...]), pch, n).astype(dbi_ref.dtype)
        dcr_ref[...] = _fold_diag_blocks(_dot_tn(sr_ref[...], dyv), n, pch).astype(dcr_ref.dtype)
        dci_ref[...] = _fold_diag_blocks(_dot_tn(si_ref[...], dyv), n, pch).astype(dci_ref.dtype)

    states = pl.BlockSpec((t, sw), lambda m: (0, m))
    chans = pl.BlockSpec((t, S5_SLAB), lambda m: (0, m))
    return _run(
        body, comm, name="s5_param_grads", grid=(sa // S5_SLAB,),
        out_shape=[jax.ShapeDtypeStruct((pch, gn), BF16), jax.ShapeDtypeStruct((pch, gn), BF16),
                   jax.ShapeDtypeStruct((n, sa), BF16), jax.ShapeDtypeStruct((n, sa), BF16)],
        in_specs=[states, states, states, states, chans, chans],
        out_specs=[pl.BlockSpec((pch, sw), lambda m: (0, m)), pl.BlockSpec((pch, sw), lambda m: (0, m)),
                   pl.BlockSpec((n, S5_SLAB), lambda m: (0, m)), pl.BlockSpec((n, S5_SLAB), lambda m: (0, m))],
        semantics="parallel",
    )(lam_r, lam_i, sr, si, u, dy)


SMALL_PARTS = ["vec_tail", "vec_ffn", "vec_lru", "vec_mix", "vec_bin", "vec_sa", "vec_gn", "dw_r", "dw_i", "dbb_re",
               "dbb_im", "dc_re", "dc_imn", "loss"]


def _small_reduce(parts, shapes, lr_col, li_col, ldt_col, b_re, b_im, groups, after=()):
    gn, pch = b_re.shape
    n = gn // groups
    nparts = parts[SMALL_PARTS[0]].size // math.prod(shapes[SMALL_PARTS[0]])
    np_, nout, first_out = len(SMALL_PARTS), 24, len(SMALL_PARTS) + 5 + len(after)

    def body(*refs):
        ins = refs[:np_]
        lr, li, ldt, bre, bim = refs[np_:np_ + 5]
        outs = refs[first_out:first_out + nout]
        sums = dict(zip(SMALL_PARTS, refs[first_out + nout:]))

        @pl.when(pl.program_id(0) == 0)
        def _():
            for k, r in zip(SMALL_PARTS, ins):
                sums[k][...] = r[...].astype(F32)

        @pl.when(pl.program_id(0) > 0)
        def _():
            for k, r in zip(SMALL_PARTS, ins):
                sums[k][...] += r[...].astype(F32)

        @pl.when(pl.program_id(0) == nparts - 1)
        def _():
            finish({k: s[...] for k, s in sums.items()}, lr, li, ldt, bre, bim, *outs)

    def finish(tot, lr, li, ldt, bre, bim, o_loss, o_gmix, o_bin, o_bglu, o_s5d, o_convb, o_br, o_bi, o_lam, o_gffn,
               o_gpg, o_bpg, o_gple, o_gfin, o_wr, o_wi, o_cre, o_cim, o_bre, o_bim, o_lre, o_lim, o_ldt, o_convw):
        o_loss[...] = tot["loss"]
        o_convw[...] = tot["vec_lru"][SUBLANES - CONV_WIDTH:SUBLANES]
        o_bpg[...] = tot["vec_tail"][0:1]
        o_gpg[...] = tot["vec_tail"][1:2]
        o_gple[...] = tot["vec_tail"][2:3]
        o_gfin[...] = tot["vec_tail"][3:4]
        o_gffn[...] = tot["vec_ffn"][0:1]
        o_convb[...] = tot["vec_lru"][0:1]
        o_br[...] = tot["vec_lru"][1:2]
        o_bi[...] = tot["vec_lru"][2:3]
        o_lam[...] = tot["vec_lru"][3:4]
        o_gmix[...] = tot["vec_mix"][0:1]
        o_bin[...] = tot["vec_bin"][0:1]
        o_bglu[...] = tot["vec_sa"][0:1]
        o_s5d[...] = tot["vec_sa"][1:2]
        o_wr[...] = tot["dw_r"].T
        o_wi[...] = tot["dw_i"].T
        o_cre[...] = tot["dc_re"].T
        o_cim[...] = -tot["dc_imn"].T
        d_a = tot["vec_gn"].T
        _, chain = jax.vjp(_disc_cols, lr[...], li[...], ldt[...], bre[...], bim[...])
        d_lr, d_li, d_ldt, d_bre, d_bim = chain((d_a[:, 0:1], d_a[:, 1:2], tot["dbb_re"].T, tot["dbb_im"].T))
        o_lre[...] = d_lr
        o_lim[...] = d_li
        o_bre[...] = d_bre
        o_bim[...] = d_bim
        same = (lax.broadcasted_iota(jnp.int32, (groups, gn), 0)
                == lax.broadcasted_iota(jnp.int32, (groups, gn), 1) // n).astype(F32)
        o_ldt[...] = jnp.dot(same, d_ldt * jnp.ones((1, LANES), F32), preferred_element_type=F32,
                             precision=lax.Precision.HIGHEST)[:, 0:1]

    d = shapes["vec_mix"][1]
    nz = shapes["vec_bin"][1]
    sa = shapes["vec_sa"][1]
    w = shapes["vec_lru"][1]
    row = lambda c: jax.ShapeDtypeStruct((1, c), F32)
    out_shape = [jax.ShapeDtypeStruct(shapes["loss"], F32), row(d), row(nz), row(sa), row(sa), row(w), row(w), row(w),
                 row(w), row(d), row(d), row(d), row(d), row(d),
                 jax.ShapeDtypeStruct(shapes["dw_r"][::-1], F32), jax.ShapeDtypeStruct(shapes["dw_i"][::-1], F32),
                 jax.ShapeDtypeStruct(shapes["dc_re"][::-1], F32), jax.ShapeDtypeStruct(shapes["dc_imn"][::-1], F32),
                 jax.ShapeDtypeStruct((gn, pch), F32), jax.ShapeDtypeStruct((gn, pch), F32),
                 jax.ShapeDtypeStruct((gn, 1), F32), jax.ShapeDtypeStruct((gn, 1), F32),
                 jax.ShapeDtypeStruct((groups, 1), F32), jax.ShapeDtypeStruct((CONV_WIDTH, w), F32)]
    def part_spec(k):
        r, c = shapes[k]
        if parts[k].ndim == 3:
            return pl.BlockSpec((None, r, c), lambda i: (i, 0, 0))
        return pl.BlockSpec((r, c), lambda i: (i, 0))

    outs = _run(
        body, None, name="small_reduce", grid=(nparts,), out_shape=out_shape,
        in_specs=([part_spec(k) for k in SMALL_PARTS] + [_whole(a.shape) for a in (lr_col, li_col, ldt_col, b_re, b_im)]
                  + [_ANY] * len(after)),
        out_specs=[_acc(s.shape) for s in out_shape],
        scratch_shapes=[pltpu.VMEM(shapes[k], F32) for k in SMALL_PARTS],
        semantics="arbitrary",
    )(*[parts[k] for k in SMALL_PARTS], lr_col, li_col, ldt_col, b_re, b_im, *after)
    names = ["loss", "g_mix", "b_in", "b_glu", "s5_d", "conv_b", "b_r", "b_i", "lru_lambda", "g_ffn", "g_ple_gate",
             "b_ple_gate", "g_ple", "g_final", "w_r", "w_i", "s5_c_re", "s5_c_im", "s5_b_re", "s5_b_im", "lam_re",
             "lam_im", "log_dt", "conv_w"]
    return dict(zip(names, outs))


def _adamw_small(ws, gs, ms, vs):
    n = len(ws)

    def body(*refs):
        w_r, g_r, m_r, v_r = (refs[i * n:(i + 1) * n] for i in range(4))
        g_o, d_o, m_o, v_o = (refs[(4 + i) * n:(5 + i) * n] for i in range(4))
        for i in range(n):
            g = g_r[i][...]
            delta, m_new, v_new = _adamw_math(w_r[i][...], g, m_r[i][...], v_r[i][...])
            g_o[i][...] = g
            d_o[i][...] = delta
            m_o[i][...] = m_new
            v_o[i][...] = v_new

    shapes = [jax.ShapeDtypeStruct(a.shape, F32) for a in ws]
    outs = pl.pallas_call(body, name="adamw_small", out_shape=shapes * 4)(*ws, *gs, *ms, *vs)
    return outs[:n], outs[n:2 * n], outs[2 * n:3 * n], outs[3 * n:]


def _adamw_math(w, g, m, v):
    m_new = ADAM_B1 * m + (1.0 - ADAM_B1) * g
    v_new = ADAM_B2 * v + (1.0 - ADAM_B2) * (g * g)
    m_hat = m_new / (1.0 - ADAM_B1 ** ADAM_STEP)
    v_hat = v_new / (1.0 - ADAM_B2 ** ADAM_STEP)
    delta = -ADAM_LR * (m_hat / (jnp.sqrt(v_hat) + ADAM_EPS) + ADAM_WD * w)
    return delta, m_new, v_new


def _row_tile(rows):
    for cand in range(256, 0, -16):
        if rows % cand == 0:
            return cand
    return rows


def _adamw(parts, w, m, v, name, transposed=False, own=None, after=()):
    nw = len(w)
    rows, cols = w[0].shape
    npart = parts[0].shape[0]
    tr = _row_tile(rows)
    if transposed:
        parts_spec = pl.BlockSpec((npart, cols, tr), lambda i: (0, 0, i))
    else:
        parts_spec = pl.BlockSpec((npart, tr, cols), lambda i: (0, i, 0))
    per = 4 if own is None else 5
    first_out = per * nw + len(after)

    def body(*refs):
        mine = None if own is None else _chip(_mesh_position())
        for i in range(nw):
            group = refs[per * i:per * i + per]
            p_ref, (w_ref, m_ref, v_ref) = group[0], group[-3:]
            g_ref, d_ref, mo_ref, vo_ref = refs[first_out + 4 * i:first_out + 4 * i + 4]

            def part(k):
                a = p_ref[k].astype(F32)
                return a if own is None else jnp.where(mine == k, group[1][k].astype(F32), a)

            g = part(0)
            for k in range(1, npart):
                g = g + part(k)
            if transposed:
                g = g.T
            delta, m_new, v_new = _adamw_math(w_ref[...], g, m_ref[...], v_ref[...])
            g_ref[...] = g
            d_ref[...] = delta
            mo_ref[...] = m_new
            vo_ref[...] = v_new

    groups = zip(parts, w, m, v) if own is None else zip(parts, own, w, m, v)
    res = pl.pallas_call(
        body, name=name, grid=(rows // tr,),
        out_shape=[jax.ShapeDtypeStruct((rows, cols), F32)] * (4 * nw),
        in_specs=([parts_spec] * (per - 3) + [_rows(tr, cols)] * 3) * nw + [_ANY] * len(after),
        out_specs=[_rows(tr, cols)] * (4 * nw),
        compiler_params=_params("parallel"),
    )(*[a for group in groups for a in group], *after)
    return [res[4 * i:4 * i + 4] for i in range(nw)]


def _mesh_position():
    return lax.axis_index("x"), lax.axis_index("y"), lax.axis_index("c")


def _flip(pos, rel):
    x, y, c = pos
    return (1 - x if rel & 4 else x, 1 - y if rel & 2 else y, 1 - c if rel & 1 else c)


def _index(pos):
    return 4 * pos[0] + 2 * pos[1] + pos[2]


_ANY = pl.BlockSpec(memory_space=pl.ANY)
FLAT_ROWS = 32


def _dma_sems(n):
    return [pltpu.SemaphoreType.DMA((n, N_DEV - 1)), pltpu.SemaphoreType.DMA((n, N_DEV - 1)), pltpu.SemaphoreType.DMA((n,))]


def _block_of(ref, idx, rows, flat):
    if flat:
        return ref.at[pl.ds(pl.multiple_of(idx * rows, FLAT_ROWS), rows), :]
    return ref.at[idx]


class _Gather:
    chips = (4, 2, 6)
    rels = frozenset((1, 4, 2, 6))

    def __init__(self, shards):
        self.inputs = list(shards)
        self.flat = [s.shape[0] % FLAT_ROWS == 0 for s in shards]
        self.out_shape = [
            jax.ShapeDtypeStruct((N_DEV * s.shape[0], s.shape[1]) if f else (N_DEV,) + s.shape, s.dtype)
            for s, f in zip(shards, self.flat)]
        self.sems = _dma_sems(len(shards))

    def _copy(self, ins, outs, sems, i, k, block, to, own=False, lands_index=None):
        a = i if lands_index is None else lands_index
        dst = _block_of(outs[a], _index(block), self.inputs[a].shape[0], self.flat[a])
        return pltpu.make_async_remote_copy(
            src_ref=ins[a] if own else dst, dst_ref=dst, send_sem=sems[0].at[i, k], recv_sem=sems[1].at[i, k],
            device_id=to, device_id_type=MESH)

    def _local(self, ins, outs, sems, i, me):
        dst = _block_of(outs[i], _index(me), self.inputs[i].shape[0], self.flat[i])
        return pltpu.make_async_copy(ins[i], dst, sems[2].at[i])

    def _first(self, ins, outs, sems, i, me):
        cps = [self._copy(ins, outs, sems, i, 0, me, _flip(me, 1), own=True)]
        cps += [self._copy(ins, outs, sems, i, 1 + j, me, _flip(me, rel), own=True) for j, rel in enumerate(self.chips)]
        return cps

    def _passed(self, ins, outs, sems, i, j, me):
        return self._copy(ins, outs, sems, i, 4 + j, _flip(me, self.chips[j]), _flip(me, 1))

    def before(self, ins, outs, sems):
        n = len(self.inputs)
        me = _mesh_position()

        @pl.when(pl.program_id(0) == 0)
        def _():
            for i in range(n):
                self._local(ins, outs, sems, i, me).start()
                for cp in self._first(ins, outs, sems, i, me):
                    cp.start()

        @pl.when(pl.program_id(0) == pl.num_programs(0) - 1)
        def _():
            for j, rel in enumerate(self.chips):
                for i in range(n):
                    self._copy(ins, outs, sems, i, 1 + j, _flip(me, rel), me).wait_recv()
                    self._passed(ins, outs, sems, i, j, me).start()

    def after(self, ins, outs, sems):
        n = len(self.inputs)
        me = _mesh_position()
        sibling = _flip(me, 1)

        @pl.when(pl.program_id(0) == pl.num_programs(0) - 1)
        def _():
            for i in range(n):
                self._copy(ins, outs, sems, i, 0, sibling, me).wait_recv()
                for j, rel in enumerate(self.chips):
                    self._copy(ins, outs, sems, i, 4 + j, _flip(sibling, rel), me).wait_recv()
            for i in range(n):
                for cp in self._first(ins, outs, sems, i, me):
                    cp.wait_send()
                for j in range(len(self.chips)):
                    self._passed(ins, outs, sems, i, j, me).wait_send()
                self._local(ins, outs, sems, i, me).wait()


N_CHIPS = N_DEV // 2


def _chip(pos):
    return 2 * pos[0] + pos[1]


class _PairSwap:
    rels = frozenset((1,))

    def __init__(self, arrays):
        self.inputs = list(arrays)
        self.rows = [a.shape[0] // N_DEV for a in arrays]
        for r in self.rows:
            assert r % FLAT_ROWS == 0, r
        self.out_shape = [jax.ShapeDtypeStruct((N_CHIPS, r, a.shape[1]), a.dtype) for a, r in zip(arrays, self.rows)]
        n = len(arrays)
        self.sems = [pltpu.SemaphoreType.DMA((n, N_CHIPS)), pltpu.SemaphoreType.DMA((n, N_CHIPS))]

    def _copy(self, ins, outs, sems, i, j, me):
        sibling = _flip(me, 1)
        return pltpu.make_async_remote_copy(
            src_ref=_block_of(ins[i], 2 * j + sibling[2], self.rows[i], True), dst_ref=outs[i].at[j],
            send_sem=sems[0].at[i, j], recv_sem=sems[1].at[i, j], device_id=sibling, device_id_type=MESH)

    def before(self, ins, outs, sems):
        me = _mesh_position()

        @pl.when(pl.program_id(0) == 0)
        def _():
            for i in range(len(self.inputs)):
                for j in range(N_CHIPS):
                    self._copy(ins, outs, sems, i, j, me).start()

    def after(self, ins, outs, sems):
        me = _mesh_position()

        @pl.when(pl.program_id(0) == pl.num_programs(0) - 1)
        def _():
            for i in range(len(self.inputs)):
                for j in range(N_CHIPS):
                    self._copy(ins, outs, sems, i, j, me).wait()


class _ChipExchange:
    chips = (4, 2, 6)
    rels = frozenset(chips)

    def __init__(self, arrays, after=()):
        self.arrays = len(arrays)
        self.inputs = list(arrays) + list(after)
        self.out_shape = [jax.ShapeDtypeStruct(a.shape, a.dtype) for a in arrays]
        n = len(arrays)
        self.sems = [pltpu.SemaphoreType.DMA((n, 3)), pltpu.SemaphoreType.DMA((n, 3)), pltpu.SemaphoreType.DMA((n,))]

    def _send(self, ins, outs, sems, i, k, me):
        peer = _flip(me, self.chips[k])
        return pltpu.make_async_remote_copy(
            src_ref=ins[i].at[_chip(peer)], dst_ref=outs[i].at[_chip(me)], send_sem=sems[0].at[i, k],
            recv_sem=sems[1].at[i, k], device_id=peer, device_id_type=MESH)

    def _arrival(self, ins, outs, sems, i, k, me):
        peer = _flip(me, self.chips[k])
        return pltpu.make_async_remote_copy(
            src_ref=ins[i].at[_chip(me)], dst_ref=outs[i].at[_chip(peer)], send_sem=sems[0].at[i, k],
            recv_sem=sems[1].at[i, k], device_id=peer, device_id_type=MESH)

    def _local(self, ins, outs, sems, i, me):
        return pltpu.make_async_copy(ins[i].at[_chip(me)], outs[i].at[_chip(me)], sems[2].at[i])

    def before(self, ins, outs, sems):
        me = _mesh_position()

        @pl.when(pl.program_id(0) == 0)
        def _():
            for i in range(self.arrays):
                self._local(ins, outs, sems, i, me).start()
                for k in range(len(self.chips)):
                    self._send(ins, outs, sems, i, k, me).start()

    def after(self, ins, outs, sems):
        me = _mesh_position()

        @pl.when(pl.program_id(0) == pl.num_programs(0) - 1)
        def _():
            for i in range(self.arrays):
                for k in range(len(self.chips)):
                    self._arrival(ins, outs, sems, i, k, me).wait_recv()
            for i in range(self.arrays):
                for k in range(len(self.chips)):
                    self._send(ins, outs, sems, i, k, me).wait_send()
                self._local(ins, outs, sems, i, me).wait()


_HBM = pl.BlockSpec(memory_space=pltpu.HBM)
_SEM = pl.BlockSpec(memory_space=pltpu.SEMAPHORE)
_DATAFLOW = pltpu.SideEffectType.DATAFLOW_SIDE_EFFECTING
_CHIP_RELS = (4, 2, 6)
_SPLIT_COLLECTIVE_ID = 3


def _chips_copy(src_ref, dst_ref, send_sems, recv_sems, k, me):
    peer = _flip(me, _CHIP_RELS[k])
    return pltpu.make_async_remote_copy(
        src_ref=src_ref.at[_chip(peer)], dst_ref=dst_ref.at[_chip(me)], send_sem=send_sems.at[k],
        recv_sem=recv_sems.at[k], device_id=peer, device_id_type=MESH)


def _chips_start(pairs, name):
    def body(p_ref, land_ref, send_sems, recv_sems, p_thru, land_thru, token):
        me = _mesh_position()
        sem = pltpu.get_barrier_semaphore()
        for rel in _CHIP_RELS:
            pl.semaphore_signal(sem, inc=1, device_id=_flip(me, rel), device_id_type=MESH)
        pl.semaphore_wait(sem, len(_CHIP_RELS))
        for k in range(len(_CHIP_RELS)):
            _chips_copy(p_ref, land_ref, send_sems, recv_sems, k, me).start()
        token[...] = jnp.zeros_like(token)

    n = len(_CHIP_RELS)
    return pl.pallas_call(
        body, name="chips_start_" + name,
        out_shape=(pltpu.SemaphoreType.DMA((n,)), pltpu.SemaphoreType.DMA((n,)), pltpu.HBM(pairs.shape, pairs.dtype),
                   pltpu.HBM(pairs.shape, pairs.dtype), jax.ShapeDtypeStruct((SUBLANES, LANES), F32)),
        in_specs=(_HBM, _HBM), out_specs=(_SEM, _SEM, _HBM, _HBM, pl.BlockSpec(memory_space=pltpu.VMEM)),
        input_output_aliases={0: 2, 1: 3},
        compiler_params=pltpu.CompilerParams(has_side_effects=_DATAFLOW, collective_id=_SPLIT_COLLECTIVE_ID),
    )(pltpu.with_memory_space_constraint(pairs, pltpu.HBM),
      pltpu.with_memory_space_constraint(lax.empty(pairs.shape, pairs.dtype), pltpu.HBM))


def _chips_wait(send_sems, recv_sems, p_thru, land_thru, after, name):
    def body(p_ref, land_ref, send_sems, recv_sems, *rest):
        me = _mesh_position()
        for k in range(len(_CHIP_RELS)):
            copy = _chips_copy(p_ref, land_ref, send_sems, recv_sems, k, me)
            copy.wait_send()
            copy.wait_recv()

    return pl.pallas_call(
        body, name="chips_wait_" + name,
        out_shape=(pltpu.HBM(p_thru.shape, p_thru.dtype), pltpu.HBM(p_thru.shape, p_thru.dtype)),
        in_specs=(_HBM, _HBM, _SEM, _SEM) + (_ANY,) * len(after), out_specs=(_HBM, _HBM),
        input_output_aliases={0: 0, 1: 1}, compiler_params=pltpu.CompilerParams(has_side_effects=_DATAFLOW),
    )(p_thru, land_thru, send_sems, recv_sems, *after)


_GATHER_START_ID, _GATHER_PASS_ID = 4, 5


class _Rows:
    def __init__(self, ref, width):
        self.ref, self.width, self.at = ref, width, self

    def __getitem__(self, idx):
        i, k = idx
        return self.ref.at[i * self.width + k]


def _handshake(rels):
    me = _mesh_position()
    sem = pltpu.get_barrier_semaphore()
    for rel in rels:
        pl.semaphore_signal(sem, inc=1, device_id=_flip(me, rel), device_id_type=MESH)
    pl.semaphore_wait(sem, len(rels))


def _gather_start(shards, name, after=()):
    job, n, na = _Gather(shards), len(shards), len(after)

    def body(*refs):
        ins, lands = refs[:n], refs[n:2 * n]
        send, recv, local = refs[2 * n + na:2 * n + na + 3]
        sems = (_Rows(send, N_DEV - 1), _Rows(recv, N_DEV - 1), local)
        token = refs[-1]
        me = _mesh_position()
        _handshake(sorted(job.rels))
        for i in range(n):
            job._local(ins, lands, sems, i, me).start()
            for cp in job._first(ins, lands, sems, i, me):
                cp.start()
        token[...] = jnp.zeros_like(token)

    res = pl.pallas_call(
        body, name=name,
        out_shape=(pltpu.SemaphoreType.DMA((n * (N_DEV - 1),)), pltpu.SemaphoreType.DMA((n * (N_DEV - 1),)),
                   pltpu.SemaphoreType.DMA((n,))) + tuple(pltpu.HBM(s.shape, s.dtype) for s in job.out_shape)
        + (jax.ShapeDtypeStruct((SUBLANES, LANES), F32),),
        in_specs=(_HBM,) * (2 * n) + (_ANY,) * na,
        out_specs=(_SEM,) * 3 + (_HBM,) * n + (pl.BlockSpec(memory_space=pltpu.VMEM),),
        input_output_aliases={n + i: 3 + i for i in range(n)},
        compiler_params=pltpu.CompilerParams(has_side_effects=_DATAFLOW, collective_id=_GATHER_START_ID),
    )(*[pltpu.with_memory_space_constraint(s, pltpu.HBM) for s in shards],
      *[pltpu.with_memory_space_constraint(lax.empty(s.shape, s.dtype), pltpu.HBM) for s in job.out_shape], *after)
    return list(res[:3]), list(res[3:3 + n]), res[-1]


def _gather_pass(shards, which, sems, lands, name, after=()):
    job, n, m, na = _Gather(shards), len(shards), len(which), len(after)

    def body(*refs):
        lands_r, local = refs[:m], refs[m + 2]
        send, recv = _Rows(refs[m], N_DEV - 1), _Rows(refs[m + 1], N_DEV - 1)
        send2, recv2 = _Rows(refs[m + 3 + na], 3), _Rows(refs[m + 3 + na + 1], 3)
        me = _mesh_position()
        _handshake((1,))
        by_index = {i: lands_r[q] for q, i in enumerate(which)}
        for j, rel in enumerate(job.chips):
            for q, i in enumerate(which):
                job._copy(None, by_index, (send, recv, local), i, 1 + j, _flip(me, rel), me).wait_recv()
                job._copy(None, by_index, (send2, recv2), q, j, _flip(me, rel), _flip(me, 1), lands_index=i).start()

    res = pl.pallas_call(
        body, name=name,
        out_shape=(pltpu.SemaphoreType.DMA((m * 3,)), pltpu.SemaphoreType.DMA((m * 3,)))
        + tuple(pltpu.HBM(lands[i].shape, lands[i].dtype) for i in which),
        in_specs=(_HBM,) * m + (_SEM,) * 3 + (_ANY,) * na, out_specs=(_SEM, _SEM) + (_HBM,) * m,
        input_output_aliases={q: 2 + q for q in range(m)},
        compiler_params=pltpu.CompilerParams(has_side_effects=_DATAFLOW, collective_id=_GATHER_PASS_ID),
    )(*[lands[i] for i in which], *sems, *after)
    return list(res[:2]), list(res[2:])


def _gather_finish(shards, which, sems, sems2, lands, after, name):
    job, n, m = _Gather(shards), len(shards), len(which)

    def body(*refs):
        ins, lands_r = refs[:m], refs[m:2 * m]
        send, recv, local = _Rows(refs[2 * m], N_DEV - 1), _Rows(refs[2 * m + 1], N_DEV - 1), refs[2 * m + 2]
        send2, recv2 = _Rows(refs[2 * m + 3], 3), _Rows(refs[2 * m + 4], 3)
        me = _mesh_position()
        sibling = _flip(me, 1)
        by_index = {i: lands_r[q] for q, i in enumerate(which)}
        own = {i: ins[q] for q, i in enumerate(which)}
        for q, i in enumerate(which):
            job._copy(None, by_index, (send, recv, local), i, 0, sibling, me).wait_recv()
            for j, rel in enumerate(job.chips):
                job._copy(None, by_index, (send2, recv2), q, j, _flip(sibling, rel), me, lands_index=i).wait_recv()
                job._copy(None, by_index, (send2, recv2), q, j, _flip(me, rel), sibling, lands_index=i).wait_send()
            for cp in job._first(own, by_index, (send, recv, local), i, me):
                cp.wait_send()
            job._local(own, by_index, (send, recv, local), i, me).wait()

    res = pl.pallas_call(
        body, name=name, out_shape=tuple(pltpu.HBM(lands[i].shape, lands[i].dtype) for i in which),
        in_specs=(_HBM,) * (2 * m) + (_SEM,) * 5 + (_ANY,) * len(after), out_specs=(_HBM,) * m,
        input_output_aliases={m + q: q for q in range(m)},
        compiler_params=pltpu.CompilerParams(has_side_effects=_DATAFLOW),
    )(*[pltpu.with_memory_space_constraint(shards[i], pltpu.HBM) for i in which], *[lands[i] for i in which],
      *sems, *sems2, *after)
    return list(res)


def _pair_add(grads, halves, name):
    n = len(grads)

    def body(core_ref, *refs):
        g_refs, h_refs, o_refs = refs[:n], refs[n:2 * n], refs[2 * n:]
        for i in range(n):
            o_refs[i][...] = (g_refs[i][...].astype(F32) + h_refs[i][...].astype(F32)).astype(o_refs[i].dtype)

    def own_block(h):
        return pl.BlockSpec(h.shape[1:], lambda j, core: (2 * j + core[0], 0))

    def slot(h):
        return pl.BlockSpec((None,) + h.shape[1:], lambda j, core: (j, 0, 0))

    spec = pltpu.PrefetchScalarGridSpec(
        num_scalar_prefetch=1, grid=(N_CHIPS,),
        in_specs=[own_block(h) for h in halves] + [slot(h) for h in halves], out_specs=[slot(h) for h in halves])
    return pl.pallas_call(
        body, name=name, grid_spec=spec, out_shape=[jax.ShapeDtypeStruct(h.shape, h.dtype) for h in halves],
        compiler_params=_params("parallel"),
    )(lax.axis_index("c").astype(jnp.int32).reshape(1), *grads, *halves)


class _Both:
    def __init__(self, first, second):
        self.jobs = (first, second)
        self.rels = first.rels | second.rels
        self.inputs = first.inputs + second.inputs
        self.out_shape = first.out_shape + second.out_shape
        self.sems = first.sems + second.sems

    def _each(self, ins, outs, sems):
        a = self.jobs[0]
        i, o, s = len(a.inputs), len(a.out_shape), len(a.sems)
        return ((a, ins[:i], outs[:o], sems[:s]), (self.jobs[1], ins[i:], outs[o:], sems[s:]))

    def before(self, ins, outs, sems):
        for job, i, o, s in self._each(ins, outs, sems):
            job.before(i, o, s)

    def after(self, ins, outs, sems):
        for job, i, o, s in self._each(ins, outs, sems):
            job.after(i, o, s)


_COLLECTIVE_IDS = {(1,): 0, (2, 4, 6): 1, (1, 2, 4, 6): 2}


def _entry_barrier(rels):
    @pl.when(pl.program_id(0) == 0)
    def _():
        me = _mesh_position()
        sem = pltpu.get_barrier_semaphore()
        for rel in rels:
            pl.semaphore_signal(sem, inc=1, device_id=_flip(me, rel), device_id_type=MESH)
        pl.semaphore_wait(sem, len(rels))


def _run(body, comm, *, semantics, out_shape, in_specs, out_specs, scratch_shapes=(), **kw):
    if comm is None:
        return pl.pallas_call(body, out_shape=out_shape, in_specs=in_specs, out_specs=out_specs,
                              scratch_shapes=list(scratch_shapes), compiler_params=_params(semantics), **kw)
    single = not isinstance(out_shape, (list, tuple))
    outs = [out_shape] if single else list(out_shape)
    ospecs = [out_specs] if single else list(out_specs)
    counts = [len(in_specs), len(comm.inputs), len(outs), len(comm.out_shape), len(scratch_shapes), len(comm.sems)]
    rels = tuple(sorted(comm.rels))

    def carrying(*refs):
        groups, pos = [], 0
        for c in counts:
            groups.append(refs[pos:pos + c])
            pos += c
        main_in, comm_in, main_out, comm_out, main_scratch, comm_sems = groups
        _entry_barrier(rels)
        comm.before(comm_in, comm_out, comm_sems)
        body(*main_in, *main_out, *main_scratch)
        comm.after(comm_in, comm_out, comm_sems)

    call = pl.pallas_call(
        carrying, out_shape=outs + list(comm.out_shape), in_specs=list(in_specs) + [_ANY] * len(comm.inputs),
        out_specs=ospecs + [_ANY] * len(comm.out_shape), scratch_shapes=list(scratch_shapes) + list(comm.sems),
        compiler_params=pltpu.CompilerParams(dimension_semantics=("arbitrary",), vmem_limit_bytes=VMEM_LIMIT,
                                             collective_id=_COLLECTIVE_IDS[rels]), **kw)

    def apply(*args):
        res = call(*args, *comm.inputs)
        main = res[:len(outs)]
        return (main[0] if single else list(main)), list(res[len(outs):])

    return apply


def _alone(comm, name):
    return _run(lambda: None, comm, semantics="arbitrary", name=name, grid=(1,), out_shape=[], in_specs=[], out_specs=[])()[1]


SHARDED = {"w_in": 1, "w_glu": 0, "conv_w": 1, "w_a_out": 1, "w_b_out": 0, "w_o": 0, "w_ffn_gate": 1, "w_ffn_up": 1,
           "w_ffn_down": 0, "w_ple_gate": 0, "w_ple": 1}
TRANSPOSED = ("w_in", "w_a_out", "w_ffn_gate", "w_ffn_up", "w_ple")
LONG_AXIS_MINOR = ("w_in", "w_ffn_gate", "w_ffn_up")
NARROW_LAST = ("s5_b_re", "s5_b_im", "s5_d")
ADAMW_GROUPS = (("w_in",), ("w_glu",), ("conv_w",), ("w_a_out",), ("w_b_out", "w_o", "w_ple_gate"),
                ("w_ffn_gate", "w_ffn_up"), ("w_ffn_down",), ("w_ple",))
SMALL = ["g_mix", "b_in", "lam_re", "lam_im", "log_dt", "s5_b_re", "s5_b_im", "s5_c_re", "s5_c_im", "s5_d", "b_glu",
         "conv_b", "w_r", "b_r", "w_i", "b_i", "lru_lambda", "g_ffn", "g_ple_gate", "b_ple_gate", "g_ple", "g_final"]
WEIGHTS = ["g_mix", "w_in", "b_in", "lam_re", "lam_im", "log_dt", "s5_b_re", "s5_b_im", "s5_c_re", "s5_c_im", "s5_d",
           "w_glu", "b_glu", "conv_w", "conv_b", "w_r", "b_r", "w_i", "b_i", "lru_lambda", "w_a_out", "w_b_out", "w_o",
           "g_ffn", "w_ffn_gate", "w_ffn_up", "w_ffn_down", "g_ple_gate", "w_ple_gate", "b_ple_gate", "w_ple", "g_ple",
           "g_final"]


def _join_columns(gathered):
    nb, r, c = gathered.shape
    return jnp.transpose(gathered, (1, 0, 2)).reshape(r, nb * c)


def _disc_scalars(lr, li, ldt):
    dt = jnp.exp(ldt)
    mag = jnp.exp(lr * dt)
    ar = mag * jnp.cos(li * dt)
    ai = mag * jnp.sin(li * dt)
    den = lr * lr + li * li
    nr = ar - 1.0
    fr = (nr * lr + ai * li) / den
    fi = (ai * lr - nr * li) / den
    return ar, ai, fr, fi


def _disc_cols(lr, li, ldt, b_re, b_im):
    ar, ai, fr, fi = _disc_scalars(lr, li, ldt)
    return ar, ai, fr * b_re - fi * b_im, fr * b_im + fi * b_re


def _local_step(x, p, target, src, small, disc, distributed=True):
    full = {} if distributed else dict(src)
    gw, halves, pairs, got = {}, {}, {}, {}

    def gather(keys):
        return (_Gather([src[k] for k in keys]), keys, full) if distributed else None

    def swap(keys):
        return (_PairSwap([gw[k] for k in keys]), keys, halves) if distributed else None

    def chips(keys):
        return (_ChipExchange([pairs[k] for k in keys]), keys, got) if distributed else None

    def add_pairs(keys):
        if distributed:
            pairs.update(zip(keys, _pair_add([gw[k] for k in keys], [halves[k] for k in keys], "pair_add_" + keys[0])))

    def carry(fn, *args, jobs=()):
        jobs = [j for j in jobs if j is not None]
        if not jobs:
            return fn(*args)
        comm = jobs[0][0]
        for j in jobs[1:]:
            comm = _Both(comm, j[0])
        res, extra = fn(*args, comm=comm)
        for job, keys, sink in jobs:
            sink.update(zip(keys, extra[:len(job.out_shape)]))
            extra = extra[len(job.out_shape):]
        return res

    d = x.shape[1]
    g, n, pch = small["s5_b_re"].shape
    sa, lw = g * pch, small["lru_lambda"].shape[-1]
    widths = [sa, lw, d, d]
    row = lambda v: v.reshape(1, -1)

    hd = small["w_r"].shape[-1]
    ar_row, ai_row, bbr_blk, bbi_blk, cre_blk, cimn_blk, wr_blk, wi_blk = carry(
        _prep, *disc["rows"], *disc["cols"], disc["b_re"], disc["b_im"], small["s5_c_re"].reshape(sa, n),
        small["s5_c_im"].reshape(sa, n), small["w_r"].reshape(lw, hd), small["w_i"].reshape(lw, hd),
        jobs=[gather(["w_in"])])
    d_row = row(small["s5_d"])
    g_mix_row = row(small["g_mix"])
    if distributed:
        later = ["w_glu", "conv_w", "w_a_out", "w_b_out", "w_ffn_gate", "w_ffn_up", "w_o", "w_ffn_down", "w_ple_gate",
                 "w_ple"]
        wires = [src[k] for k in later]
        flight, lands, token = _gather_start(wires, "gather_start", after=[ar_row])
        g_mix_row = g_mix_row + token[0, 0]

        def arrive(keys, after):
            which = [later.index(k) for k in keys]
            passed, moved = _gather_pass(wires, which, flight, lands, "gather_pass_" + keys[0], after)
            for q, i in enumerate(which):
                lands[i] = moved[q]
            full.update(zip(keys, _gather_finish(wires, which, flight, passed, lands, (), "gather_finish_" + keys[0])))
    else:
        arrive = lambda keys, after: None

    u_a, u_b, za, zb = _inproj_fwd(x, g_mix_row, full["w_in"], row(small["b_in"]), widths)
    arrive(["w_glu", "conv_w", "w_a_out", "w_b_out"], [u_a])
    conv_w = _join_columns(full["conv_w"]) if distributed else full["conv_w"]
    branches = [_s5_fwd(u_a, bbr_blk, bbi_blk, ar_row, ai_row, cre_blk, cimn_blk, d_row, full["w_glu"],
                        row(small["b_glu"])),
                _lru_fwd(u_b, conv_w, row(small["conv_b"]), wr_blk, row(small["b_r"]), wi_blk, row(small["b_i"]),
                         row(small["lru_lambda"]))]
    (sr, si, y, y_a), (xc, h, hprev) = _stages(branches, "branches_fwd")
    arrive(["w_ffn_gate", "w_ffn_up", "w_o"], [y_a])
    x1, merged, ma, mb, fg, fu = _merge_ffn_up_fwd(y_a, h, za, zb, x, full["w_a_out"], full["w_b_out"], full["w_o"],
                                                   row(small["g_ffn"]), full["w_ffn_gate"], full["w_ffn_up"])
    arrive(["w_ffn_down", "w_ple_gate", "w_ple"], [x1])
    x2 = _ffn_down_fwd(fg, fu, x1, full["w_ffn_down"])

    dx2, loss_blk, gw["w_ple_gate"], gw["w_ple"], vec_tail = _tail_fwd_bwd(
        x2, p, target, row(small["g_ple_gate"]), full["w_ple_gate"], row(small["b_ple_gate"]), full["w_ple"],
        row(small["g_ple"]), row(small["g_final"]))
    dfg, dfu, act = carry(_ffn_bwd_a, dx2, fg, fu, full["w_ffn_down"], jobs=[swap(["w_ple_gate", "w_ple"])])
    tn_d = min(d, 512)
    tk_f = _row_tile(act.shape[1])
    gw["w_ffn_down"] = _matmul_tn(act, dx2, tk_f, "dw_ffn_down", BF16)
    add_pairs(["w_ple_gate", "w_ple"])
    dx1, h2, vec_ffn = carry(_ffn_bwd_b, dfg, dfu, x1, dx2, row(small["g_ffn"]), full["w_ffn_gate"], full["w_ffn_up"],
                             jobs=[chips(["w_ple_gate", "w_ple"]), swap(["w_ffn_down"])])
    gw["w_ffn_gate"] = _matmul_tn(dfg, h2, tk_f, "dw_ffn_gate", BF16)
    gw["w_ffn_up"] = _matmul_tn(dfu, h2, tk_f, "dw_ffn_up", BF16)
    add_pairs(["w_ffn_down"])
    dza, dzb, dya, dyb, gw["w_o"], gw["w_a_out"], gw["w_b_out"] = carry(
        _merge_bwd, dx1, merged, ma, mb, za, zb, y_a, h, full["w_o"], full["w_a_out"], full["w_b_out"],
        jobs=[chips(["w_ffn_down"]), swap(["w_ffn_gate", "w_ffn_up"])])
    add_pairs(["w_ffn_gate", "w_ffn_up"])
    own = {}
    du_b, dw_r, dw_i, vec_lru = carry(
        _lru_bwd, dyb, xc, hprev, u_b, conv_w, wr_blk, row(small["b_r"]), wi_blk, row(small["b_i"]),
        row(small["lru_lambda"]), hd, jobs=[chips(["w_ffn_gate"]), swap(["w_o", "w_a_out", "w_b_out"])])
    add_pairs(["w_o", "w_a_out", "w_b_out"])
    du_a, lam_r, lam_i, dy16, gw["w_glu"], vec_sa, vec_gn = carry(
        _s5_bwd, dya, y, sr, si, u_a, full["w_glu"], row(small["b_glu"]), cre_blk, cimn_blk, bbr_blk, bbi_blk, ar_row,
        ai_row, d_row, jobs=[chips(["w_ffn_up"]), chips(["w_o", "w_a_out", "w_b_out"])])
    smalls = {"vec_tail": vec_tail, "vec_ffn": vec_ffn, "vec_lru": vec_lru, "vec_sa": vec_sa, "vec_gn": vec_gn,
              "dw_r": dw_r, "dw_i": dw_i, "loss": loss_blk}
    everyones = {}

    def gather_smalls(keys):
        return (_Gather([smalls[k] for k in keys]), keys, everyones) if distributed else None

    smalls["dbb_re"], smalls["dbb_im"], smalls["dc_re"], smalls["dc_imn"] = carry(
        _s5_param_grads, lam_r, lam_i, sr, si, u_a, dy16, pch, n, jobs=[swap(["w_glu"]), gather_smalls(list(smalls))])
    add_pairs(["w_glu"])
    dz = [du_a, du_b, dza, dzb]
    grad_x, h0, smalls["vec_mix"], smalls["vec_bin"] = _inproj_bwd(
        dz, x, dx1, row(small["g_mix"]), full["w_in"])
    shapes = {k: a.shape for k, a in smalls.items()}
    gw["w_in"] = carry(_dw_from_parts, dz, h0, tn_d, "dw_in",
                       jobs=[chips(["w_glu"]),
                             gather_smalls(["dbb_re", "dbb_im", "dc_re", "dc_imn", "vec_mix", "vec_bin"])])
    smalls.update(everyones)
    if distributed:
        got["w_in"] = gw["w_in"]
        gw = got
    return grad_x, gw, smalls, shapes, own


def _disc_inputs(small):
    g, n, pch = small["s5_b_re"].shape
    srcs = (small["lam_re"], small["lam_im"], jnp.repeat(small["log_dt"], n))
    return {"rows": [a.reshape(1, g * n) for a in srcs], "cols": [a.reshape(g * n, 1) for a in srcs],
            "b_re": small["s5_b_re"].reshape(g * n, pch), "b_im": small["s5_b_im"].reshape(g * n, pch)}


def kernel(x, p, g_mix, w_in, b_in, lam_re, lam_im, log_dt, s5_b_re, s5_b_im, s5_c_re, s5_c_im, s5_d, w_glu, b_glu, conv_w, conv_b, w_r, b_r, w_i, b_i, lru_lambda, w_a_out, w_b_out, w_o, g_ffn, w_ffn_gate, w_ffn_up, w_ffn_down, g_ple_gate, w_ple_gate, b_ple_gate, w_ple, g_ple, g_final, loss_target, m_g_mix, m_w_in, m_b_in, m_lam_re, m_lam_im, m_log_dt, m_s5_b_re, m_s5_b_im, m_s5_c_re, m_s5_c_im, m_s5_d, m_w_glu, m_b_glu, m_conv_w, m_conv_b, m_w_r, m_b_r, m_w_i, m_b_i, m_lru_lambda, m_w_a_out, m_w_b_out, m_w_o, m_g_ffn, m_w_ffn_gate, m_w_ffn_up, m_w_ffn_down, m_g_ple_gate, m_w_ple_gate, m_b_ple_gate, m_w_ple, m_g_ple, m_g_final, v_g_mix, v_w_in, v_b_in, v_lam_re, v_lam_im, v_log_dt, v_s5_b_re, v_s5_b_im, v_s5_c_re, v_s5_c_im, v_s5_d, v_w_glu, v_b_glu, v_conv_w, v_conv_b, v_w_r, v_b_r, v_w_i, v_b_i, v_lru_lambda, v_w_a_out, v_w_b_out, v_w_o, v_g_ffn, v_w_ffn_gate, v_w_ffn_up, v_w_ffn_down, v_g_ple_gate, v_w_ple_gate, v_b_ple_gate, v_w_ple, v_g_ple, v_g_final):
    given = dict(locals())
    wts = {k: given[k] for k in WEIGHTS}
    moms = {k: given["m_" + k] for k in WEIGHTS}
    vels = {k: given["v_" + k] for k in WEIGHTS}

    def drop_depth(k, a):
        return a if k == "g_final" else a[0]

    small = {k: drop_depth(k, wts[k]) for k in SMALL}
    shard = {k: wts[k][0] for k in SHARDED}
    names = list(SHARDED)

    def wire(k):
        if k == "conv_w":
            return shard[k]
        return (shard[k].T if k in TRANSPOSED else shard[k]).astype(BF16)

    disc = _disc_inputs(small)
    grad_x, parts, smalls, shapes, own = _local_step(x[0], p[0, 0], loss_target[0], {k: wire(k) for k in names}, small,
                                                     disc)

    (pair_in,) = _pair_add([parts["w_in"]], _alone(_PairSwap([parts["w_in"]]), "swap_last"), "pair_add_w_in")
    send_sems, recv_sems, pair_in, landing, token = _chips_start(pair_in, "w_in")
    ordered = (token,)

    g_small = _small_reduce(smalls, shapes, *disc["cols"], disc["b_re"], disc["b_im"], small["s5_b_re"].shape[0],
                            after=ordered)
    loss = g_small.pop("loss")[0, 0]
    cols = shard["conv_w"].shape[1]
    mine = _index((lax.axis_index("x"), lax.axis_index("y"), lax.axis_index("c")))
    parts["conv_w"] = lax.dynamic_slice_in_dim(g_small.pop("conv_w"), mine * cols, cols, axis=1)[None]

    def view(k, a):
        a = a.reshape((1, -1) if k == "g_final" else wts[k].shape)
        return jnp.swapaxes(a, -1, -2) if k in NARROW_LAST else a

    def unview(k, a):
        return (jnp.swapaxes(a, -1, -2) if k in NARROW_LAST else a).reshape(wts[k].shape)

    slots = _adamw_small([view(k, wts[k]) for k in SMALL], [view(k, g_small[k]) for k in SMALL],
                         [view(k, moms[k]) for k in SMALL], [view(k, vels[k]) for k in SMALL])
    small_out = [dict(zip(SMALL, [unview(k, a) for k, a in zip(SMALL, slot)])) for slot in slots]

    big_out = {}
    between = [slots[0][0]]
    for group in sorted(ADAMW_GROUPS, key=lambda g: g[0] == "w_in"):
        flip = group[0] in LONG_AXIS_MINOR
        look = (lambda a: a.T) if flip else (lambda a: a)
        last = group[0] == "w_in"
        if last:
            own["w_in"], parts["w_in"] = _chips_wait(send_sems, recv_sems, pair_in, landing, between, "w_in")
        res = _adamw([parts[k] for k in group], [look(shard[k]) for k in group], [look(moms[k][0]) for k in group],
                     [look(vels[k][0]) for k in group], "adamw_" + group[0],
                     transposed=group[0] in TRANSPOSED and not flip,
                     own=[own[k] for k in group] if group[0] in own else None, after=() if last else ordered)
        between.append(res[0][0])
        for k, outs4 in zip(group, res):
            big_out[k] = [look(a) for a in outs4]

    outs = [loss, grad_x[None]]
    for slot in range(4):
        for k in WEIGHTS:
            if k in SHARDED:
                outs.append(big_out[k][slot][None])
            else:
                outs.append(small_out[slot][k])
    return tuple(outs)
```

```python
import math

import jax
import jax.numpy as jnp
from jax import lax
from jax.experimental import pallas as pl
from jax.experimental.pallas import tpu as pltpu

F32 = jnp.float32
BF16 = jnp.bfloat16

EPS = 1e-6
LRU_C = 8.0
CONV_WIDTH = 4
ADAM_LR = 0.001
ADAM_B1 = 0.9
ADAM_B2 = 0.999
ADAM_EPS = 1e-08
ADAM_WD = 0.01
ADAM_STEP = 10

N_DEV = 8
MESH = pl.DeviceIdType.MESH
SUBLANES = 8
LANES = 128
VMEM_LIMIT = 56 * 1024 * 1024
TOKEN_TILE = 256
TIME_CHUNK = 256
S5_SLAB = 128
LRU_SLAB = 256


def _dot(a, b):
    return jnp.dot(a.astype(BF16), b.astype(BF16), preferred_element_type=F32)


def _dot_nt(a, b):
    return lax.dot_general(a.astype(BF16), b.astype(BF16), (((1,), (1,)), ((), ())), preferred_element_type=F32)


def _dot_tn(a, b):
    return lax.dot_general(a.astype(BF16), b.astype(BF16), (((0,), (0,)), ((), ())), preferred_element_type=F32)


def _sigmoid(x):
    return jax.nn.sigmoid(x)


def _rms_stats(x):
    r = lax.rsqrt(jnp.mean(x * x, axis=-1, keepdims=True) + EPS)
    return x * r, r


def _rms_bwd(dy, xhat, r, g):
    dxn = dy * g
    dx = r * (dxn - xhat * jnp.mean(dxn * xhat, axis=-1, keepdims=True))
    return dx, dy * xhat


def _rowsum(v):
    return jnp.sum(v, axis=0, keepdims=True)


def _expm1(x):
    u = jnp.exp(x)
    um1 = u - 1.0
    safe = jnp.where(um1 == 0.0, 1.0, jnp.log(u))
    return jnp.where(um1 == 0.0, x, um1 * x / safe)


def _softplus(x):
    e = jnp.exp(-jnp.abs(x))
    u = 1.0 + e
    um1 = u - 1.0
    safe = jnp.where(um1 == 0.0, 1.0, um1)
    log1p_e = jnp.where(um1 == 0.0, e, jnp.log(u) * e / safe)
    return jnp.maximum(x, 0.0) + log1p_e


_GELU_K = math.sqrt(2.0 / math.pi)
_GELU_C = 0.044715


def _gelu(x):
    return 0.5 * x * (1.0 + jnp.tanh(_GELU_K * (x + _GELU_C * x * x * x)))


def _gelu_grad(x):
    th = jnp.tanh(_GELU_K * (x + _GELU_C * x * x * x))
    return 0.5 * (1.0 + th) + 0.5 * x * (1.0 - th * th) * _GELU_K * (1.0 + 3.0 * _GELU_C * x * x)


def _params(*sem):
    return pltpu.CompilerParams(dimension_semantics=sem, vmem_limit_bytes=VMEM_LIMIT)


def _rows(tm, n):
    return pl.BlockSpec((tm, n), lambda i: (i, 0))


def _rows_rev(tm, n, steps):
    return pl.BlockSpec((tm, n), lambda i: (steps - 1 - i, 0))


def _whole(shape):
    nd = len(shape)
    return pl.BlockSpec(shape, lambda i: (0,) * nd, pipeline_mode=pl.Buffered(1))


def _acc(shape):
    nd = len(shape)
    return pl.BlockSpec(shape, lambda i: (0,) * nd)


def _zero_on_first(*refs):
    @pl.when(pl.program_id(0) == 0)
    def _():
        for r in refs:
            r[...] = jnp.zeros_like(r)


def _inproj_fwd(x, g_mix, w_in_t, b_in, widths, comm=None):
    t, d = x.shape
    n = w_in_t.shape[0]
    tm = min(TOKEN_TILE, t)
    offs = [sum(widths[:i]) for i in range(len(widths) + 1)]

    def body(x_ref, g_ref, w_ref, b_ref, *outs):
        xhat, _ = _rms_stats(x_ref[...])
        h = (xhat * g_ref[...]).astype(BF16)
        for k, o_ref in enumerate(outs):
            lo, hi = offs[k], offs[k + 1]
            o_ref[...] = _dot_nt(h, w_ref[lo:hi, :]) + b_ref[:, lo:hi]

    return _run(
        body, comm, name="inproj_fwd", grid=(t // tm,),
        out_shape=[jax.ShapeDtypeStruct((t, w), F32) for w in widths],
        in_specs=[_rows(tm, d), _whole((1, d)), _whole((n, d)), _whole((1, n))],
        out_specs=[_rows(tm, w) for w in widths],
        semantics="parallel",
    )(x, g_mix, w_in_t, b_in)


def _s5_fwd(u, bbr_blk, bbi_blk, ar, ai, cre_blk, cimn_blk, d_skip, w_glu, b_glu):
    t, sa = u.shape
    ns, _, sw = bbr_blk.shape
    gn = ns * sw
    tc = min(TIME_CHUNK, t)

    def body(u_ref, bbr_ref, bbi_ref, ar_ref, ai_ref, cre_ref, cim_ref, d_ref, wg_ref, bg_ref,
             sr_ref, si_ref, y_ref, ya_ref, cr_s, ci_s, sr_s, si_s):
        _zero_on_first(cr_s, ci_s)
        uv = u_ref[...]
        ub = uv.astype(BF16)
        for m in range(ns):
            um = ub[:, m * S5_SLAB:(m + 1) * S5_SLAB]
            sr_s[:, m * sw:(m + 1) * sw] = _dot(um, bbr_ref[m])
            si_s[:, m * sw:(m + 1) * sw] = _dot(um, bbi_ref[m])
        a_r = ar_ref[...]
        a_i = ai_ref[...]

        def step(row, carry):
            c_r, c_i = carry
            at = pl.ds(row, 1)
            n_r = a_r * c_r - a_i * c_i + sr_s[at, :]
            n_i = a_r * c_i + a_i * c_r + si_s[at, :]
            sr_s[at, :] = n_r
            si_s[at, :] = n_i
            return n_r, n_i

        c_r, c_i = lax.fori_loop(0, tc, step, (cr_s[0:1, :], ci_s[0:1, :]), unroll=16)
        cr_s[0:1, :] = c_r
        ci_s[0:1, :] = c_i
        for m in range(ns):
            states, chans = slice(m * sw, (m + 1) * sw), slice(m * S5_SLAB, (m + 1) * S5_SLAB)
            s_r, s_i = sr_s[:, states].astype(BF16), si_s[:, states].astype(BF16)
            sr_ref[:, states] = s_r.astype(sr_ref.dtype)
            si_ref[:, states] = s_i.astype(si_ref.dtype)
            y_ref[:, chans] = _dot(s_r, cre_ref[m]) + _dot(s_i, cim_ref[m]) + d_ref[:, chans] * uv[:, chans]
        y = y_ref[...]
        zz = _gelu(y)
        q = _dot(zz, wg_ref[...]) + bg_ref[...]
        ya_ref[...] = zz * _sigmoid(q)

    return dict(
        body=body, steps=t // tc, args=(u, bbr_blk, bbi_blk, ar, ai, cre_blk, cimn_blk, d_skip, w_glu, b_glu),
        out_shape=[jax.ShapeDtypeStruct((t, gn), BF16), jax.ShapeDtypeStruct((t, gn), BF16),
                   jax.ShapeDtypeStruct((t, sa), F32), jax.ShapeDtypeStruct((t, sa), F32)],
        in_specs=[_rows(tc, sa), _whole(bbr_blk.shape), _whole(bbi_blk.shape), _whole((1, gn)), _whole((1, gn)),
                  _whole(cre_blk.shape), _whole(cimn_blk.shape), _whole((1, sa)), _whole((sa, sa)), _whole((1, sa))],
        out_specs=[_rows(tc, gn), _rows(tc, gn), _rows(tc, sa), _rows(tc, sa)],
        scratch=[pltpu.VMEM((SUBLANES, gn), F32), pltpu.VMEM((SUBLANES, gn), F32),
                 pltpu.VMEM((tc, gn), F32), pltpu.VMEM((tc, gn), F32)])


def _stages(stages, name, comm=None):
    counts = [[len(s[k]) for s in stages] for k in ("args", "out_shape", "scratch")]

    def body(*refs):
        groups, pos = [], 0
        for kind in counts:
            per_stage = []
            for c in kind:
                per_stage.append(refs[pos:pos + c])
                pos += c
            groups.append(per_stage)
        for s, ins, outs, scratch in zip(stages, *groups):
            s["body"](*ins, *outs, *scratch)

    res = _run(
        body, comm, name=name, grid=(stages[0]["steps"],),
        out_shape=[o for s in stages for o in s["out_shape"]], in_specs=[i for s in stages for i in s["in_specs"]],
        out_specs=[o for s in stages for o in s["out_specs"]], scratch_shapes=[c for s in stages for c in s["scratch"]],
        semantics="arbitrary",
    )(*[a for s in stages for a in s["args"]])
    extra = None
    if comm is not None:
        res, extra = res
    per_stage, pos = [], 0
    for c in counts[1]:
        per_stage.append(list(res[pos:pos + c]))
        pos += c
    return per_stage if comm is None else (per_stage, extra)


def _slab_dot(x, w_ref, transposed=False):
    dot = _dot_nt if transposed else _dot
    xb = x.astype(BF16)
    return jnp.concatenate([dot(xb[:, j * LRU_SLAB:(j + 1) * LRU_SLAB], w_ref[j]) for j in range(w_ref.shape[0])], axis=1)


def _lru_gates(xc, wr_ref, br_ref, wi_ref, bi_ref, lam_ref):
    r = _sigmoid(_slab_dot(xc, wr_ref) + br_ref[...])
    ig = _sigmoid(_slab_dot(xc, wi_ref) + bi_ref[...])
    sp = _softplus(-lam_ref[...])
    log_a = (-LRU_C * r) * sp
    return r, ig, sp, log_a


def _lru_fwd(u, conv_w, conv_b, wr_blk, b_r, wi_blk, b_i, lru_lambda):
    t, w = u.shape
    tc = min(TIME_CHUNK, t)
    halo = SUBLANES

    def body(u_ref, cw_ref, cb_ref, wr_ref, br_ref, wi_ref, bi_ref, lam_ref,
             xc_ref, h_ref, hp_ref, ext_s, a_s, carry_s):
        @pl.when(pl.program_id(0) == 0)
        def _():
            ext_s[0:halo, :] = jnp.zeros((halo, w), F32)
            carry_s[...] = jnp.zeros_like(carry_s)

        ext_s[halo:halo + tc, :] = u_ref[...]
        xc = cb_ref[...]
        for k in range(CONV_WIDTH):
            off = halo - (CONV_WIDTH - 1) + k
            xc = xc + cw_ref[k:k + 1, :] * ext_s[off:off + tc, :]
        ext_s[0:halo, :] = ext_s[tc:tc + halo, :]
        xc_ref[...] = xc
        r, ig, sp, log_a = _lru_gates(xc, wr_ref, br_ref, wi_ref, bi_ref, lam_ref)
        a_s[...] = jnp.exp(log_a)
        h_ref[...] = jnp.sqrt(-_expm1(2.0 * log_a)) * ig * xc

        def step(row, carry):
            at = pl.ds(row, 1)
            hp_ref[at, :] = carry
            nxt = a_s[at, :] * carry + h_ref[at, :]
            h_ref[at, :] = nxt
            return nxt

        carry_s[0:1, :] = lax.fori_loop(0, tc, step, carry_s[0:1, :], unroll=16)

    return dict(
        body=body, steps=t // tc, args=(u, conv_w, conv_b, wr_blk, b_r, wi_blk, b_i, lru_lambda),
        out_shape=[jax.ShapeDtypeStruct((t, w), F32)] * 3,
        in_specs=[_rows(tc, w), _whole((CONV_WIDTH, w)), _whole((1, w)), _whole(wr_blk.shape), _whole((1, w)),
                  _whole(wi_blk.shape), _whole((1, w)), _whole((1, w))],
        out_specs=[_rows(tc, w)] * 3,
        scratch=[pltpu.VMEM((halo + tc, w), F32), pltpu.VMEM((tc, w), F32), pltpu.VMEM((SUBLANES, w), F32)])


def _merge_ffn_up_fwd(y_a, h, za, zb, x, w_a_out_t, w_b_out, w_o, g_ffn, w_gate_t, w_up_t, comm=None):
    t, d = x.shape
    sa, lw = y_a.shape[1], h.shape[1]
    f = w_gate_t.shape[0]
    tm = min(TOKEN_TILE, t)

    def body(ya_ref, h_ref, za_ref, zb_ref, x_ref, wa_ref, wb_ref, wo_ref, g_ref, wg_ref, wu_ref,
             x1_ref, mg_ref, ma_ref, mb_ref, fg_ref, fu_ref):
        ma = _dot_nt(ya_ref[...], wa_ref[...])
        mb = _dot(h_ref[...], wb_ref[...])
        merged = _sigmoid(za_ref[...]) * ma + _sigmoid(zb_ref[...]) * mb
        ma_ref[...] = ma.astype(ma_ref.dtype)
        mb_ref[...] = mb.astype(mb_ref.dtype)
        mg_ref[...] = merged.astype(mg_ref.dtype)
        x1 = x_ref[...] + _dot(merged, wo_ref[...])
        x1_ref[...] = x1
        xhat, _ = _rms_stats(x1)
        h2 = (xhat * g_ref[...]).astype(BF16)
        fg_ref[...] = _dot_nt(h2, wg_ref[...]).astype(fg_ref.dtype)
        fu_ref[...] = _dot_nt(h2, wu_ref[...]).astype(fu_ref.dtype)

    return _run(
        body, comm, name="merge_ffn_up_fwd", grid=(t // tm,),
        out_shape=[jax.ShapeDtypeStruct((t, d), F32), jax.ShapeDtypeStruct((t, d), BF16),
                   jax.ShapeDtypeStruct((t, d), BF16), jax.ShapeDtypeStruct((t, d), BF16),
                   jax.ShapeDtypeStruct((t, f), BF16), jax.ShapeDtypeStruct((t, f), BF16)],
        in_specs=[_rows(tm, sa), _rows(tm, lw), _rows(tm, d), _rows(tm, d), _rows(tm, d),
                  _whole((d, sa)), _whole((lw, d)), _whole((d, d)), _whole((1, d)), _whole((f, d)), _whole((f, d))],
        out_specs=[_rows(tm, d)] * 4 + [_rows(tm, f)] * 2,
        semantics="parallel",
    )(y_a, h, za, zb, x, w_a_out_t, w_b_out, w_o, g_ffn, w_gate_t, w_up_t)


def _ffn_down_fwd(fg, fu, x1, w_down, comm=None):
    t, d = x1.shape
    f = fg.shape[1]
    tm = min(TOKEN_TILE, t)

    def body(fg_ref, fu_ref, x_ref, wd_ref, x2_ref):
        fgv = fg_ref[...].astype(F32)
        act = fgv * _sigmoid(fgv) * fu_ref[...].astype(F32)
        x2_ref[...] = x_ref[...] + _dot(act, wd_ref[...])

    return _run(
        body, comm, name="ffn_down_fwd", grid=(t // tm,),
        out_shape=jax.ShapeDtypeStruct((t, d), F32),
        in_specs=[_rows(tm, f), _rows(tm, f), _rows(tm, d), _whole((f, d))],
        out_specs=_rows(tm, d),
        semantics="parallel",
    )(fg, fu, x1, w_down)


def _store_on_last(pairs):
    @pl.when(pl.program_id(0) == pl.num_programs(0) - 1)
    def _():
        for acc, out in pairs:
            out[...] = acc[...].astype(out.dtype)


def _tail_fwd_bwd(x2, p, target, g_pg, w_pg, b_pg, w_ple_t, g_ple, g_final):
    t, d = x2.shape
    pd = p.shape[1]
    tm = min(TOKEN_TILE, t)

    def body(x2_ref, p_ref, tg_ref, gpg_ref, wpg_ref, bpg_ref, wple_ref, gple_ref, gfin_ref,
             dx2_ref, loss_ref, dwpg_out, dwple_out, vec_ref, dwpg_ref, dwple_ref):
        _zero_on_first(loss_ref, dwpg_ref, dwple_ref, vec_ref)
        x2v = x2_ref[...]
        xh2, r2 = _rms_stats(x2v)
        h3 = xh2 * gpg_ref[...]
        gp = _sigmoid(_dot(h3, wpg_ref[...]) + bpg_ref[...])
        pe = _dot_nt(p_ref[...], wple_ref[...])
        peh, r3 = _rms_stats(pe)
        e = peh * gple_ref[...]
        x3 = x2v + gp * e
        xh3, r4 = _rms_stats(x3)
        diff = xh3 * gfin_ref[...] - tg_ref[...]
        loss_ref[...] += 0.5 * jnp.sum(jnp.mean(diff * diff, axis=-1, keepdims=True))
        dy = diff * (1.0 / d)
        dx3, dgfin = _rms_bwd(dy, xh3, r4, gfin_ref[...])
        d_gp = dx3 * e
        d_e = dx3 * gp
        dpe, dgple = _rms_bwd(d_e, peh, r3, gple_ref[...])
        dwple_ref[...] += _dot_tn(dpe, p_ref[...])
        dpre = d_gp * gp * (1.0 - gp)
        dwpg_ref[...] += _dot_tn(h3, dpre)
        dh3 = _dot_nt(dpre, wpg_ref[...])
        dx2n, dgpg = _rms_bwd(dh3, xh2, r2, gpg_ref[...])
        dx2_ref[...] = dx3 + dx2n
        vec_ref[0:1, :] += _rowsum(dpre)
        vec_ref[1:2, :] += _rowsum(dgpg)
        vec_ref[2:3, :] += _rowsum(dgple)
        vec_ref[3:4, :] += _rowsum(dgfin)
        _store_on_last([(dwpg_ref, dwpg_out), (dwple_ref, dwple_out)])

    return pl.pallas_call(
        body, name="tail_fwd_bwd", grid=(t // tm,),
        out_shape=[jax.ShapeDtypeStruct((t, d), F32), jax.ShapeDtypeStruct((SUBLANES, LANES), F32),
                   jax.ShapeDtypeStruct((d, d), BF16), jax.ShapeDtypeStruct((d, pd), BF16),
                   jax.ShapeDtypeStruct((SUBLANES, d), F32)],
        in_specs=[_rows(tm, d), _rows(tm, pd), _rows(tm, d), _whole((1, d)), _whole((d, d)), _whole((1, d)),
                  _whole((d, pd)), _whole((1, d)), _whole((1, d))],
        out_specs=[_rows(tm, d), _acc((SUBLANES, LANES)), _acc((d, d)), _acc((d, pd)), _acc((SUBLANES, d))],
        scratch_shapes=[pltpu.VMEM((d, d), F32), pltpu.VMEM((d, pd), F32)],
        compiler_params=_params("arbitrary"),
    )(x2, p, target, g_pg, w_pg, b_pg, w_ple_t, g_ple, g_final)


def _ffn_bwd_a(dx2, fg, fu, w_down, comm=None):
    t, d = dx2.shape
    f = fg.shape[1]
    tm = min(TOKEN_TILE, t)

    def body(dx_ref, fg_ref, fu_ref, wd_ref, dfg_ref, dfu_ref, act_ref):
        dact = _dot_nt(dx_ref[...], wd_ref[...])
        fgv = fg_ref[...].astype(F32)
        fuv = fu_ref[...].astype(F32)
        sg = _sigmoid(fgv)
        silu = fgv * sg
        dfu_ref[...] = (dact * silu).astype(dfu_ref.dtype)
        dfg_ref[...] = (dact * fuv * (sg * (1.0 + fgv * (1.0 - sg)))).astype(dfg_ref.dtype)
        act_ref[...] = (silu * fuv).astype(act_ref.dtype)

    return _run(
        body, comm, name="ffn_bwd_a", grid=(t // tm,),
        out_shape=[jax.ShapeDtypeStruct((t, f), BF16)] * 3,
        in_specs=[_rows(tm, d), _rows(tm, f), _rows(tm, f), _whole((f, d))],
        out_specs=[_rows(tm, f)] * 3,
        semantics="parallel",
    )(dx2, fg, fu, w_down)


def _ffn_bwd_b(dfg, dfu, x1, dx2, g_ffn, w_gate_t, w_up_t, comm=None):
    t, d = x1.shape
    f = dfg.shape[1]
    tm = min(TOKEN_TILE, t)

    def body(dfg_ref, dfu_ref, x_ref, dx2_ref, g_ref, wg_ref, wu_ref, dx1_ref, h2_ref, vec_ref):
        _zero_on_first(vec_ref)
        dh2 = _dot(dfg_ref[...], wg_ref[...]) + _dot(dfu_ref[...], wu_ref[...])
        xhat, r = _rms_stats(x_ref[...])
        h2_ref[...] = (xhat * g_ref[...]).astype(h2_ref.dtype)
        dxn, dg = _rms_bwd(dh2, xhat, r, g_ref[...])
        dx1_ref[...] = dx2_ref[...] + dxn
        vec_ref[0:1, :] += _rowsum(dg)

    return _run(
        body, comm, name="ffn_bwd_b", grid=(t // tm,),
        out_shape=[jax.ShapeDtypeStruct((t, d), F32), jax.ShapeDtypeStruct((t, d), BF16),
                   jax.ShapeDtypeStruct((SUBLANES, d), F32)],
        in_specs=[_rows(tm, f), _rows(tm, f), _rows(tm, d), _rows(tm, d), _whole((1, d)), _whole((f, d)), _whole((f, d))],
        out_specs=[_rows(tm, d), _rows(tm, d), _acc((SUBLANES, d))],
        semantics="arbitrary",
    )(dfg, dfu, x1, dx2, g_ffn, w_gate_t, w_up_t)


def _matmul_tn(a, b, tn, name, dtype=F32):
    t, k = a.shape
    n = b.shape[1]

    def body(a_ref, b_ref, o_ref):
        o_ref[...] = _dot_tn(a_ref[...], b_ref[...]).astype(o_ref.dtype)

    return _run(
        body, None, name=name, grid=(n // tn,),
        out_shape=jax.ShapeDtypeStruct((k, n), dtype),
        in_specs=[_whole((t, k)), pl.BlockSpec((t, tn), lambda j: (0, j))],
        out_specs=pl.BlockSpec((k, tn), lambda j: (0, j)),
        semantics="parallel",
    )(a, b)


def _merge_bwd(dx1, merged, ma, mb, za, zb, y_a, h, w_o, w_a_out_t, w_b_out, comm=None):
    t, d = dx1.shape
    sa, lw = y_a.shape[1], h.shape[1]
    tm = min(TOKEN_TILE, t)

    def body(dx1_ref, mg_ref, ma_ref, mb_ref, za_ref, zb_ref, ya_ref, h_ref, wo_ref, wa_ref, wb_ref,
             dza_ref, dzb_ref, dya_ref, dyb_ref, dwo_out, dwa_out, dwb_out, dwo_ref, dwa_ref, dwb_ref):
        _zero_on_first(dwo_ref, dwa_ref, dwb_ref)
        dx1v = dx1_ref[...].astype(BF16)
        dmg = _dot_nt(dx1v, wo_ref[...])
        ga = _sigmoid(za_ref[...])
        gb = _sigmoid(zb_ref[...])
        dza_ref[...] = (dmg * ma_ref[...].astype(F32) * ga * (1.0 - ga)).astype(dza_ref.dtype)
        dzb_ref[...] = (dmg * mb_ref[...].astype(F32) * gb * (1.0 - gb)).astype(dzb_ref.dtype)
        dma = (dmg * ga).astype(BF16)
        dmb = (dmg * gb).astype(BF16)
        dya_ref[...] = _dot(dma, wa_ref[...])
        dyb_ref[...] = _dot_nt(dmb, wb_ref[...])
        dwo_ref[...] += _dot_tn(mg_ref[...], dx1v)
        dwa_ref[...] += _dot_tn(dma, ya_ref[...])
        dwb_ref[...] += _dot_tn(h_ref[...], dmb)
        _store_on_last([(dwo_ref, dwo_out), (dwa_ref, dwa_out), (dwb_ref, dwb_out)])

    return _run(
        body, comm, name="merge_bwd", grid=(t // tm,),
        out_shape=[jax.ShapeDtypeStruct((t, d), BF16), jax.ShapeDtypeStruct((t, d), BF16),
                   jax.ShapeDtypeStruct((t, sa), F32), jax.ShapeDtypeStruct((t, lw), F32),
                   jax.ShapeDtypeStruct((d, d), BF16), jax.ShapeDtypeStruct((d, sa), BF16),
                   jax.ShapeDtypeStruct((lw, d), BF16)],
        in_specs=[_rows(tm, d), _rows(tm, d), _rows(tm, d), _rows(tm, d), _rows(tm, d), _rows(tm, d),
                  _rows(tm, sa), _rows(tm, lw), _whole((d, d)), _whole((d, sa)), _whole((lw, d))],
        out_specs=[_rows(tm, d), _rows(tm, d), _rows(tm, sa), _rows(tm, lw), _acc((d, d)), _acc((d, sa)), _acc((lw, d))],
        scratch_shapes=[pltpu.VMEM((d, d), F32), pltpu.VMEM((d, sa), F32), pltpu.VMEM((lw, d), F32)],
        semantics="arbitrary",
    )(dx1, merged, ma, mb, za, zb, y_a, h, w_o, w_a_out_t, w_b_out)


def _fold_diag_blocks(dense, row_group, col_group):
    r, c = dense.shape
    rows = lax.broadcasted_iota(jnp.int32, (r, c), 0)
    cols = lax.broadcasted_iota(jnp.int32, (r, c), 1)
    kept = jnp.where(rows // row_group == cols // col_group, dense, 0.0)
    pick = (lax.broadcasted_iota(jnp.int32, (row_group, r), 0)
            == lax.broadcasted_iota(jnp.int32, (row_group, r), 1) % row_group).astype(F32)
    return jnp.dot(pick, kept, preferred_element_type=F32, precision=lax.Precision.HIGHEST)


def _lru_bwd(dh, xc, hprev, u, conv_w, wr_blk, b_r, wi_blk, b_i, lru_lambda, head_dim, comm=None):
    t, w = dh.shape
    tc = min(TIME_CHUNK, t)
    steps = t // tc
    halo = SUBLANES
    sub_per_chunk = tc // halo
    slabs = w // LRU_SLAB

    def body(dh_ref, xc_ref, hp_ref, u_ref, uh_ref, cw_ref, wr_ref, br_ref, wi_ref, bi_ref, lam_ref,
             du_ref, dwr_out, dwi_out, vec_ref, lam_s, a_s, dxc_s, uext_s, carry_s, dwr_ref, dwi_ref):
        chunk = steps - 1 - pl.program_id(0)

        @pl.when(pl.program_id(0) == 0)
        def _():
            carry_s[...] = jnp.zeros_like(carry_s)
            dxc_s[tc:tc + halo, :] = jnp.zeros((halo, w), F32)
            dwr_ref[...] = jnp.zeros_like(dwr_ref)
            dwi_ref[...] = jnp.zeros_like(dwi_ref)
            vec_ref[...] = jnp.zeros_like(vec_ref)

        xc = xc_ref[...]
        r, ig, sp, log_a = _lru_gates(xc, wr_ref, br_ref, wi_ref, bi_ref, lam_ref)
        a = jnp.exp(log_a)
        a_s[...] = a

        def step(i, q):
            at = pl.ds(tc - 1 - i, 1)
            lam_row = dh_ref[at, :] + q
            lam_s[at, :] = lam_row
            return a_s[at, :] * lam_row

        carry_s[0:1, :] = lax.fori_loop(0, tc, step, carry_s[0:1, :], unroll=16)
        lam = lam_s[...]
        mult = jnp.sqrt(-_expm1(2.0 * log_a))
        d_log_a = lam * hp_ref[...] * a - (lam * ig * xc) * (a * a) / mult
        d_ig = lam * mult * xc
        dpre_r = (d_log_a * (-LRU_C * sp)) * r * (1.0 - r)
        dpre_i = d_ig * ig * (1.0 - ig)
        dxc = lam * mult * ig + _slab_dot(dpre_r, wr_ref, transposed=True) + _slab_dot(dpre_i, wi_ref, transposed=True)
        xcb, drb, dib = xc.astype(BF16), dpre_r.astype(BF16), dpre_i.astype(BF16)
        for j in range(slabs):
            cols = slice(j * LRU_SLAB, (j + 1) * LRU_SLAB)
            dwr_ref[j] += _dot_tn(drb[:, cols], xcb[:, cols])
            dwi_ref[j] += _dot_tn(dib[:, cols], xcb[:, cols])
        vec_ref[0:1, :] += _rowsum(dxc)
        vec_ref[1:2, :] += _rowsum(dpre_r)
        vec_ref[2:3, :] += _rowsum(dpre_i)
        vec_ref[3:4, :] += _rowsum(d_log_a * (-LRU_C * r)) * (-_sigmoid(-lam_ref[...]))
        dxc_s[0:tc, :] = dxc
        du = cw_ref[CONV_WIDTH - 1:CONV_WIDTH, :] * dxc
        for k in range(CONV_WIDTH - 1):
            off = CONV_WIDTH - 1 - k
            du = du + cw_ref[k:k + 1, :] * dxc_s[off:off + tc, :]
        du_ref[...] = du.astype(du_ref.dtype)
        dxc_s[tc:tc + halo, :] = dxc_s[0:halo, :]
        uext_s[0:halo, :] = jnp.where(chunk > 0, uh_ref[...], 0.0)
        uext_s[halo:halo + tc, :] = u_ref[...]
        for k in range(CONV_WIDTH):
            off = halo - (CONV_WIDTH - 1) + k
            vec_ref[4 + k:5 + k, :] += _rowsum(dxc * uext_s[off:off + tc, :])

        @pl.when(pl.program_id(0) == steps - 1)
        def _():
            for j in range(slabs):
                cols = slice(j * LRU_SLAB, (j + 1) * LRU_SLAB)
                dwr_out[:, cols] = _fold_diag_blocks(dwr_ref[j], head_dim, head_dim).astype(dwr_out.dtype)
                dwi_out[:, cols] = _fold_diag_blocks(dwi_ref[j], head_dim, head_dim).astype(dwi_out.dtype)

    halo_spec = pl.BlockSpec((halo, w), lambda i: (jnp.maximum((steps - 1 - i) * sub_per_chunk - 1, 0), 0))
    return _run(
        body, comm, name="lru_bwd", grid=(steps,),
        out_shape=[jax.ShapeDtypeStruct((t, w), BF16), jax.ShapeDtypeStruct((head_dim, w), BF16),
                   jax.ShapeDtypeStruct((head_dim, w), BF16), jax.ShapeDtypeStruct((SUBLANES, w), F32)],
        in_specs=[_rows_rev(tc, w, steps)] * 4 + [halo_spec, _whole((CONV_WIDTH, w)), _whole(wr_blk.shape),
                                                  _whole((1, w)), _whole(wi_blk.shape), _whole((1, w)), _whole((1, w))],
        out_specs=[_rows_rev(tc, w, steps), _acc((head_dim, w)), _acc((head_dim, w)), _acc((SUBLANES, w))],
        scratch_shapes=[pltpu.VMEM((tc, w), F32), pltpu.VMEM((tc, w), F32), pltpu.VMEM((tc + halo, w), F32),
                        pltpu.VMEM((halo + tc, w), F32), pltpu.VMEM((SUBLANES, w), F32),
                        pltpu.VMEM((slabs, LRU_SLAB, LRU_SLAB), F32), pltpu.VMEM((slabs, LRU_SLAB, LRU_SLAB), F32)],
        semantics="arbitrary",
    )(dh, xc, hprev, u, u, conv_w, wr_blk, b_r, wi_blk, b_i, lru_lambda)


def _s5_bwd(dya, y, sr, si, u, w_glu, b_glu, cre_blk, cimn_blk, bbr_blk, bbi_blk, ar, ai, d_skip, comm=None):
    t, sa = dya.shape
    gn = sr.shape[1]
    ns, _, sw = bbr_blk.shape
    tc = min(TIME_CHUNK, t)
    steps = t // tc
    halo = SUBLANES

    def body(dya_ref, y_ref, sr_ref, si_ref, u_ref, wg_ref, bg_ref, cre_ref, cim_ref, bbr_ref, bbi_ref,
             ar_ref, ai_ref, d_ref, du_ref, lr_ref, li_ref, dy_ref, dwg_out, vsa_ref, vgn_ref, gr_s, gi_s, cr_s, ci_s,
             dwg_ref):
        @pl.when(pl.program_id(0) == 0)
        def _():
            cr_s[...] = jnp.zeros_like(cr_s)
            ci_s[...] = jnp.zeros_like(ci_s)
            gr_s[tc:tc + halo, :] = jnp.zeros((halo, gn), F32)
            gi_s[tc:tc + halo, :] = jnp.zeros((halo, gn), F32)
            dwg_ref[...] = jnp.zeros_like(dwg_ref)
            vsa_ref[...] = jnp.zeros_like(vsa_ref)
            vgn_ref[...] = jnp.zeros_like(vgn_ref)

        yv = y_ref[...]
        uv = u_ref[...]
        zz = _gelu(yv)
        sg = _sigmoid(_dot(zz, wg_ref[...]) + bg_ref[...])
        dyav = dya_ref[...]
        dq = dyav * zz * sg * (1.0 - sg)
        dzz = dyav * sg + _dot_nt(dq, wg_ref[...])
        dwg_ref[...] += _dot_tn(zz, dq)
        dy = dzz * _gelu_grad(yv)
        dyb = dy.astype(BF16)
        dy_ref[...] = dyb.astype(dy_ref.dtype)
        vsa_ref[0:1, :] += _rowsum(dq)
        vsa_ref[1:2, :] += _rowsum(dy * uv)
        for m in range(ns):
            dym = dyb[:, m * S5_SLAB:(m + 1) * S5_SLAB]
            gr_s[0:tc, m * sw:(m + 1) * sw] = _dot_nt(dym, cre_ref[m])
            gi_s[0:tc, m * sw:(m + 1) * sw] = _dot_nt(dym, cim_ref[m])
        a_r = ar_ref[...]
        a_i = ai_ref[...]

        def step(i, carry):
            l_r, l_i = carry
            at = pl.ds(tc - 1 - i, 1)
            n_r = gr_s[at, :] + a_r * l_r + a_i * l_i
            n_i = gi_s[at, :] + a_r * l_i - a_i * l_r
            gr_s[at, :] = n_r
            gi_s[at, :] = n_i
            return n_r, n_i

        l_r, l_i = lax.fori_loop(0, tc, step, (cr_s[0:1, :], ci_s[0:1, :]), unroll=16)
        cr_s[0:1, :] = l_r
        ci_s[0:1, :] = l_i
        nxt_r = gr_s[1:tc + 1, :]
        nxt_i = gi_s[1:tc + 1, :]
        srv = sr_ref[...].astype(F32)
        siv = si_ref[...].astype(F32)
        vgn_ref[0:1, :] += _rowsum(nxt_r * srv + nxt_i * siv)
        vgn_ref[1:2, :] += _rowsum(nxt_i * srv - nxt_r * siv)
        lam_r = gr_s[0:tc, :]
        lam_i = gi_s[0:tc, :]
        gr_s[tc:tc + halo, :] = gr_s[0:halo, :]
        gi_s[tc:tc + halo, :] = gi_s[0:halo, :]
        lrb = lam_r.astype(BF16)
        lib = lam_i.astype(BF16)
        lr_ref[...] = lrb.astype(lr_ref.dtype)
        li_ref[...] = lib.astype(li_ref.dtype)
        for m in range(ns):
            states, chans = slice(m * sw, (m + 1) * sw), slice(m * S5_SLAB, (m + 1) * S5_SLAB)
            du_ref[:, chans] = (_dot_nt(lrb[:, states], bbr_ref[m]) + _dot_nt(lib[:, states], bbi_ref[m])
                                + dy[:, chans] * d_ref[:, chans]).astype(du_ref.dtype)
        _store_on_last([(dwg_ref, dwg_out)])

    return _run(
        body, comm, name="s5_bwd", grid=(steps,),
        out_shape=[jax.ShapeDtypeStruct((t, sa), BF16), jax.ShapeDtypeStruct((t, gn), BF16),
                   jax.ShapeDtypeStruct((t, gn), BF16), jax.ShapeDtypeStruct((t, sa), BF16),
                   jax.ShapeDtypeStruct((sa, sa), BF16), jax.ShapeDtypeStruct((SUBLANES, sa), F32),
                   jax.ShapeDtypeStruct((SUBLANES, gn), F32)],
        in_specs=[_rows_rev(tc, sa, steps), _rows_rev(tc, sa, steps), _rows_rev(tc, gn, steps), _rows_rev(tc, gn, steps),
                  _rows_rev(tc, sa, steps), _whole((sa, sa)), _whole((1, sa)), _whole(cre_blk.shape),
                  _whole(cimn_blk.shape), _whole(bbr_blk.shape), _whole(bbi_blk.shape), _whole((1, gn)), _whole((1, gn)),
                  _whole((1, sa))],
        out_specs=[_rows_rev(tc, sa, steps), _rows_rev(tc, gn, steps), _rows_rev(tc, gn, steps), _rows_rev(tc, sa, steps),
                   _acc((sa, sa)), _acc((SUBLANES, sa)), _acc((SUBLANES, gn))],
        scratch_shapes=[pltpu.VMEM((tc + halo, gn), F32), pltpu.VMEM((tc + halo, gn), F32),
                        pltpu.VMEM((SUBLANES, gn), F32), pltpu.VMEM((SUBLANES, gn), F32), pltpu.VMEM((sa, sa), F32)],
        semantics="arbitrary",
    )(dya, y, sr, si, u, w_glu, b_glu, cre_blk, cimn_blk, bbr_blk, bbi_blk, ar, ai, d_skip)


def _inproj_bwd(dparts, x, dx1, g_mix, w_in_t, comm=None):
    t, d = x.shape
    n = w_in_t.shape[0]
    widths = [p.shape[1] for p in dparts]
    offs = [sum(widths[:i]) for i in range(len(widths) + 1)]
    tm = min(TOKEN_TILE, t)
    np_ = len(dparts)

    def body(*refs):
        dz_refs = refs[:np_]
        x_ref, dx1_ref, g_ref, w_ref, gx_ref, h_ref, vd_ref, vn_ref = refs[np_:]
        _zero_on_first(vd_ref, vn_ref)
        dh = jnp.zeros((tm, d), F32)
        for k, r in enumerate(dz_refs):
            lo, hi = offs[k], offs[k + 1]
            dzk = r[...]
            dh = dh + _dot(dzk, w_ref[lo:hi, :])
            vn_ref[0:1, lo:hi] += _rowsum(dzk.astype(F32))
        xhat, r0 = _rms_stats(x_ref[...])
        h_ref[...] = (xhat * g_ref[...]).astype(h_ref.dtype)
        dxn, dg = _rms_bwd(dh, xhat, r0, g_ref[...])
        gx_ref[...] = dx1_ref[...] + dxn
        vd_ref[0:1, :] += _rowsum(dg)

    return _run(
        body, comm, name="inproj_bwd", grid=(t // tm,),
        out_shape=[jax.ShapeDtypeStruct((t, d), F32), jax.ShapeDtypeStruct((t, d), BF16),
                   jax.ShapeDtypeStruct((SUBLANES, d), F32), jax.ShapeDtypeStruct((SUBLANES, n), F32)],
        in_specs=[_rows(tm, w) for w in widths] + [_rows(tm, d), _rows(tm, d), _whole((1, d)), _whole((n, d))],
        out_specs=[_rows(tm, d), _rows(tm, d), _acc((SUBLANES, d)), _acc((SUBLANES, n))],
        semantics="arbitrary",
    )(*dparts, x, dx1, g_mix, w_in_t)


def _dw_from_parts(dparts, h, tn, name, comm=None):
    t, d = h.shape
    widths = [p.shape[1] for p in dparts]
    offs = [sum(widths[:i]) for i in range(len(widths) + 1)]
    np_ = len(dparts)

    def body(*refs):
        h_ref, o_ref = refs[np_], refs[np_ + 1]
        hv = h_ref[...]
        for k, r in enumerate(refs[:np_]):
            o_ref[offs[k]:offs[k + 1], :] = _dot_tn(r[...], hv).astype(o_ref.dtype)

    return _run(
        body, comm, name=name, grid=(d // tn,),
        out_shape=jax.ShapeDtypeStruct((offs[-1], d), BF16),
        in_specs=[_whole(p.shape) for p in dparts] + [pl.BlockSpec((t, tn), lambda j: (0, j))],
        out_specs=pl.BlockSpec((offs[-1], tn), lambda j: (0, j)),
        semantics="parallel",
    )(*dparts, h)


def _prep(lr_row, li_row, ldt_row, lr_col, li_col, ldt_col, b_re, b_im, c_re, c_im, w_r, w_i, comm=None):
    gn, pch = b_re.shape
    sa, n = c_re.shape
    w, hd = w_r.shape
    ns, sw, lsl = sa // S5_SLAB, S5_SLAB * n // pch, w // LRU_SLAB

    def spread_cols(vals, row_group, col_group, width):
        r, k = vals.shape
        tile = (lax.broadcasted_iota(jnp.int32, (k, width), 0) == lax.broadcasted_iota(jnp.int32, (k, width), 1) % k)
        rows = lax.broadcasted_iota(jnp.int32, (r, width), 0) // row_group
        cols = lax.broadcasted_iota(jnp.int32, (r, width), 1) // col_group
        return jnp.where(rows == cols, _dot(vals, tile.astype(BF16)), 0.0)

    def spread_rows(vals, row_group, col_group, height):
        k, c = vals.shape
        tile = (lax.broadcasted_iota(jnp.int32, (height, k), 0) % k == lax.broadcasted_iota(jnp.int32, (height, k), 1))
        rows = lax.broadcasted_iota(jnp.int32, (height, c), 0) // row_group
        cols = lax.broadcasted_iota(jnp.int32, (height, c), 1) // col_group
        return jnp.where(rows == cols, _dot(tile.astype(BF16), vals), 0.0)

    def body(lrr, lir, ldr, lrc, lic, ldc, bre, bim, cre, cim, wr, wi,
             ar_o, ai_o, bbr_o, bbi_o, cre_o, cim_o, wr_o, wi_o):
        ar, ai, _, _ = _disc_scalars(lrr[...], lir[...], ldr[...])
        ar_o[...] = ar
        ai_o[...] = ai
        _, _, bbr, bbi = _disc_cols(lrc[...], lic[...], ldc[...], bre[...], bim[...])
        bbr_t, bbi_t = bbr.T, bbi.T
        for m in range(ns):
            bbr_o[m] = spread_rows(bbr_t[:, m * sw:(m + 1) * sw], pch, n, S5_SLAB).astype(bbr_o.dtype)
            bbi_o[m] = spread_rows(bbi_t[:, m * sw:(m + 1) * sw], pch, n, S5_SLAB).astype(bbi_o.dtype)
            rows = slice(m * S5_SLAB, (m + 1) * S5_SLAB)
            cre_o[m] = spread_rows(cre[rows, :].T, n, pch, sw).astype(cre_o.dtype)
            cim_o[m] = spread_rows(-cim[rows, :].T, n, pch, sw).astype(cim_o.dtype)
        for j in range(lsl):
            rows = slice(j * LRU_SLAB, (j + 1) * LRU_SLAB)
            wr_o[j] = spread_cols(wr[rows, :], hd, hd, LRU_SLAB).astype(wr_o.dtype)
            wi_o[j] = spread_cols(wi[rows, :], hd, hd, LRU_SLAB).astype(wi_o.dtype)

    args = (lr_row, li_row, ldt_row, lr_col, li_col, ldt_col, b_re, b_im, c_re, c_im, w_r, w_i)
    out_shape = [jax.ShapeDtypeStruct((1, gn), F32), jax.ShapeDtypeStruct((1, gn), F32),
                 jax.ShapeDtypeStruct((ns, S5_SLAB, sw), BF16), jax.ShapeDtypeStruct((ns, S5_SLAB, sw), BF16),
                 jax.ShapeDtypeStruct((ns, sw, S5_SLAB), BF16), jax.ShapeDtypeStruct((ns, sw, S5_SLAB), BF16),
                 jax.ShapeDtypeStruct((lsl, LRU_SLAB, LRU_SLAB), BF16),
                 jax.ShapeDtypeStruct((lsl, LRU_SLAB, LRU_SLAB), BF16)]
    return _run(
        body, comm, name="prep", grid=(1,), out_shape=out_shape, in_specs=[_whole(a.shape) for a in args],
        out_specs=[_acc(s.shape) for s in out_shape], semantics="arbitrary",
    )(*args)


def _s5_param_grads(lam_r, lam_i, sr, si, u, dy, pch, n, comm=None):
    t, gn = lam_r.shape
    sa = u.shape[1]
    sw = S5_SLAB * n // pch

    def body(lr_ref, li_ref, sr_ref, si_ref, u_ref, dy_ref, dbr_ref, dbi_ref, dcr_ref, dci_ref):
        uv = u_ref[...]
        dyv = dy_ref[...]
        dbr_ref[...] = _fold_diag_blocks(_dot_tn(uv, lr_ref[...]), pch, n).astype(dbr_ref.dtype)
        dbi_ref[...] = _fold_diag_blocks(_dot_tn(uv, li_ref[...]), pch, n).astype(dbi_ref.dtype)
        dcr_ref[...] = _fold_diag_blocks(_dot_tn(sr_ref[...], dyv), n, pch).astype(dcr_ref.dtype)
        dci_ref[...] = _fold_diag_blocks(_dot_tn(si_ref[...], dyv), n, pch).astype(dci_ref.dtype)

    states = pl.BlockSpec((t, sw), lambda m: (0, m))
    chans = pl.BlockSpec((t, S5_SLAB), lambda m: (0, m))
    return _run(
        body, comm, name="s5_param_grads", grid=(sa // S5_SLAB,),
        out_shape=[jax.ShapeDtypeStruct((pch, gn), BF16), jax.ShapeDtypeStruct((pch, gn), BF16),
                   jax.ShapeDtypeStruct((n, sa), BF16), jax.ShapeDtypeStruct((n, sa), BF16)],
        in_specs=[states, states, states, states, chans, chans],
        out_specs=[pl.BlockSpec((pch, sw), lambda m: (0, m)), pl.BlockSpec((pch, sw), lambda m: (0, m)),
                   pl.BlockSpec((n, S5_SLAB), lambda m: (0, m)), pl.BlockSpec((n, S5_SLAB), lambda m: (0, m))],
        semantics="parallel",
    )(lam_r, lam_i, sr, si, u, dy)


SMALL_PARTS = ["vec_tail", "vec_ffn", "vec_lru", "vec_mix", "vec_bin", "vec_sa", "vec_gn", "dw_r", "dw_i", "dbb_re",
               "dbb_im", "dc_re", "dc_imn", "loss"]


def _small_reduce(parts, shapes, lr_col, li_col, ldt_col, b_re, b_im, groups, after=()):
    gn, pch = b_re.shape
    n = gn // groups
    nparts = parts[SMALL_PARTS[0]].size // math.prod(shapes[SMALL_PARTS[0]])
    np_, nout, first_out = len(SMALL_PARTS), 24, len(SMALL_PARTS) + 5 + len(after)

    def body(*refs):
        ins = refs[:np_]
        lr, li, ldt, bre, bim = refs[np_:np_ + 5]
        outs = refs[first_out:first_out + nout]
        sums = dict(zip(SMALL_PARTS, refs[first_out + nout:]))

        @pl.when(pl.program_id(0) == 0)
        def _():
            for k, r in zip(SMALL_PARTS, ins):
                sums[k][...] = r[...].astype(F32)

        @pl.when(pl.program_id(0) > 0)
        def _():
            for k, r in zip(SMALL_PARTS, ins):
                sums[k][...] += r[...].astype(F32)

        @pl.when(pl.program_id(0) == nparts - 1)
        def _():
            finish({k: s[...] for k, s in sums.items()}, lr, li, ldt, bre, bim, *outs)

    def finish(tot, lr, li, ldt, bre, bim, o_loss, o_gmix, o_bin, o_bglu, o_s5d, o_convb, o_br, o_bi, o_lam, o_gffn,
               o_gpg, o_bpg, o_gple, o_gfin, o_wr, o_wi, o_cre, o_cim, o_bre, o_bim, o_lre, o_lim, o_ldt, o_convw):
        o_loss[...] = tot["loss"]
        o_convw[...] = tot["vec_lru"][SUBLANES - CONV_WIDTH:SUBLANES]
        o_bpg[...] = tot["vec_tail"][0:1]
        o_gpg[...] = tot["vec_tail"][1:2]
        o_gple[...] = tot["vec_tail"][2:3]
        o_gfin[...] = tot["vec_tail"][3:4]
        o_gffn[...] = tot["vec_ffn"][0:1]
        o_convb[...] = tot["vec_lru"][0:1]
        o_br[...] = tot["vec_lru"][1:2]
        o_bi[...] = tot["vec_lru"][2:3]
        o_lam[...] = tot["vec_lru"][3:4]
        o_gmix[...] = tot["vec_mix"][0:1]
        o_bin[...] = tot["vec_bin"][0:1]
        o_bglu[...] = tot["vec_sa"][0:1]
        o_s5d[...] = tot["vec_sa"][1:2]
        o_wr[...] = tot["dw_r"].T
        o_wi[...] = tot["dw_i"].T
        o_cre[...] = tot["dc_re"].T
        o_cim[...] = -tot["dc_imn"].T
        d_a = tot["vec_gn"].T
        _, chain = jax.vjp(_disc_cols, lr[...], li[...], ldt[...], bre[...], bim[...])
        d_lr, d_li, d_ldt, d_bre, d_bim = chain((d_a[:, 0:1], d_a[:, 1:2], tot["dbb_re"].T, tot["dbb_im"].T))
        o_lre[...] = d_lr
        o_lim[...] = d_li
        o_bre[...] = d_bre
        o_bim[...] = d_bim
        same = (lax.broadcasted_iota(jnp.int32, (groups, gn), 0)
                == lax.broadcasted_iota(jnp.int32, (groups, gn), 1) // n).astype(F32)
        o_ldt[...] = jnp.dot(same, d_ldt * jnp.ones((1, LANES), F32), preferred_element_type=F32,
                             precision=lax.Precision.HIGHEST)[:, 0:1]

    d = shapes["vec_mix"][1]
    nz = shapes["vec_bin"][1]
    sa = shapes["vec_sa"][1]
    w = shapes["vec_lru"][1]
    row = lambda c: jax.ShapeDtypeStruct((1, c), F32)
    out_shape = [jax.ShapeDtypeStruct(shapes["loss"], F32), row(d), row(nz), row(sa), row(sa), row(w), row(w), row(w),
                 row(w), row(d), row(d), row(d), row(d), row(d),
                 jax.ShapeDtypeStruct(shapes["dw_r"][::-1], F32), jax.ShapeDtypeStruct(shapes["dw_i"][::-1], F32),
                 jax.ShapeDtypeStruct(shapes["dc_re"][::-1], F32), jax.ShapeDtypeStruct(shapes["dc_imn"][::-1], F32),
                 jax.ShapeDtypeStruct((gn, pch), F32), jax.ShapeDtypeStruct((gn, pch), F32),
                 jax.ShapeDtypeStruct((gn, 1), F32), jax.ShapeDtypeStruct((gn, 1), F32),
                 jax.ShapeDtypeStruct((groups, 1), F32), jax.ShapeDtypeStruct((CONV_WIDTH, w), F32)]
    def part_spec(k):
        r, c = shapes[k]
        if parts[k].ndim == 3:
            return pl.BlockSpec((None, r, c), lambda i: (i, 0, 0))
        return pl.BlockSpec((r, c), lambda i: (i, 0))

    outs = _run(
        body, None, name="small_reduce", grid=(nparts,), out_shape=out_shape,
        in_specs=([part_spec(k) for k in SMALL_PARTS] + [_whole(a.shape) for a in (lr_col, li_col, ldt_col, b_re, b_im)]
                  + [_ANY] * len(after)),
        out_specs=[_acc(s.shape) for s in out_shape],
        scratch_shapes=[pltpu.VMEM(shapes[k], F32) for k in SMALL_PARTS],
        semantics="arbitrary",
    )(*[parts[k] for k in SMALL_PARTS], lr_col, li_col, ldt_col, b_re, b_im, *after)
    names = ["loss", "g_mix", "b_in", "b_glu", "s5_d", "conv_b", "b_r", "b_i", "lru_lambda", "g_ffn", "g_ple_gate",
             "b_ple_gate", "g_ple", "g_final", "w_r", "w_i", "s5_c_re", "s5_c_im", "s5_b_re", "s5_b_im", "lam_re",
             "lam_im", "log_dt", "conv_w"]
    return dict(zip(names, outs))


def _adamw_small(ws, gs, ms, vs):
    n = len(ws)

    def body(*refs):
        w_r, g_r, m_r, v_r = (refs[i * n:(i + 1) * n] for i in range(4))
        g_o, d_o, m_o, v_o = (refs[(4 + i) * n:(5 + i) * n] for i in range(4))
        for i in range(n):
            g = g_r[i][...]
            delta, m_new, v_new = _adamw_math(w_r[i][...], g, m_r[i][...], v_r[i][...])
            g_o[i][...] = g
            d_o[i][...] = delta
            m_o[i][...] = m_new
            v_o[i][...] = v_new

    shapes = [jax.ShapeDtypeStruct(a.shape, F32) for a in ws]
    outs = pl.pallas_call(body, name="adamw_small", out_shape=shapes * 4)(*ws, *gs, *ms, *vs)
    return outs[:n], outs[n:2 * n], outs[2 * n:3 * n], outs[3 * n:]


def _adamw_math(w, g, m, v):
    m_new = ADAM_B1 * m + (1.0 - ADAM_B1) * g
    v_new = ADAM_B2 * v + (1.0 - ADAM_B2) * (g * g)
    m_hat = m_new / (1.0 - ADAM_B1 ** ADAM_STEP)
    v_hat = v_new / (1.0 - ADAM_B2 ** ADAM_STEP)
    delta = -ADAM_LR * (m_hat / (jnp.sqrt(v_hat) + ADAM_EPS) + ADAM_WD * w)
    return delta, m_new, v_new


def _row_tile(rows):
    for cand in range(256, 0, -16):
        if rows % cand == 0:
            return cand
    return rows


def _adamw(parts, w, m, v, name, transposed=False, own=None, after=()):
    nw = len(w)
    rows, cols = w[0].shape
    npart = parts[0].shape[0]
    tr = _row_tile(rows)
    if transposed:
        parts_spec = pl.BlockSpec((npart, cols, tr), lambda i: (0, 0, i))
    else:
        parts_spec = pl.BlockSpec((npart, tr, cols), lambda i: (0, i, 0))
    per = 4 if own is None else 5
    first_out = per * nw + len(after)

    def body(*refs):
        mine = None if own is None else _chip(_mesh_position())
        for i in range(nw):
            group = refs[per * i:per * i + per]
            p_ref, (w_ref, m_ref, v_ref) = group[0], group[-3:]
            g_ref, d_ref, mo_ref, vo_ref = refs[first_out + 4 * i:first_out + 4 * i + 4]

            def part(k):
                a = p_ref[k].astype(F32)
                return a if own is None else jnp.where(mine == k, group[1][k].astype(F32), a)

            g = part(0)
            for k in range(1, npart):
                g = g + part(k)
            if transposed:
                g = g.T
            delta, m_new, v_new = _adamw_math(w_ref[...], g, m_ref[...], v_ref[...])
            g_ref[...] = g
            d_ref[...] = delta
            mo_ref[...] = m_new
            vo_ref[...] = v_new

    groups = zip(parts, w, m, v) if own is None else zip(parts, own, w, m, v)
    res = pl.pallas_call(
        body, name=name, grid=(rows // tr,),
        out_shape=[jax.ShapeDtypeStruct((rows, cols), F32)] * (4 * nw),
        in_specs=([parts_spec] * (per - 3) + [_rows(tr, cols)] * 3) * nw + [_ANY] * len(after),
        out_specs=[_rows(tr, cols)] * (4 * nw),
        compiler_params=_params("parallel"),
    )(*[a for group in groups for a in group], *after)
    return [res[4 * i:4 * i + 4] for i in range(nw)]


def _mesh_position():
    return lax.axis_index("x"), lax.axis_index("y"), lax.axis_index("c")


def _flip(pos, rel):
    x, y, c = pos
    return (1 - x if rel & 4 else x, 1 - y if rel & 2 else y, 1 - c if rel & 1 else c)


def _index(pos):
    return 4 * pos[0] + 2 * pos[1] + pos[2]


_ANY = pl.BlockSpec(memory_space=pl.ANY)
FLAT_ROWS = 32


def _dma_sems(n):
    return [pltpu.SemaphoreType.DMA((n, N_DEV - 1)), pltpu.SemaphoreType.DMA((n, N_DEV - 1)), pltpu.SemaphoreType.DMA((n,))]


def _block_of(ref, idx, rows, flat):
    if flat:
        return ref.at[pl.ds(pl.multiple_of(idx * rows, FLAT_ROWS), rows), :]
    return ref.at[idx]


class _Gather:
    chips = (4, 2, 6)
    rels = frozenset((1, 4, 2, 6))

    def __init__(self, shards):
        self.inputs = list(shards)
        self.flat = [s.shape[0] % FLAT_ROWS == 0 for s in shards]
        self.out_shape = [
            jax.ShapeDtypeStruct((N_DEV * s.shape[0], s.shape[1]) if f else (N_DEV,) + s.shape, s.dtype)
            for s, f in zip(shards, self.flat)]
        self.sems = _dma_sems(len(shards))

    def _copy(self, ins, outs, sems, i, k, block, to, own=False, lands_index=None):
        a = i if lands_index is None else lands_index
        dst = _block_of(outs[a], _index(block), self.inputs[a].shape[0], self.flat[a])
        return pltpu.make_async_remote_copy(
            src_ref=ins[a] if own else dst, dst_ref=dst, send_sem=sems[0].at[i, k], recv_sem=sems[1].at[i, k],
            device_id=to, device_id_type=MESH)

    def _local(self, ins, outs, sems, i, me):
        dst = _block_of(outs[i], _index(me), self.inputs[i].shape[0], self.flat[i])
        return pltpu.make_async_copy(ins[i], dst, sems[2].at[i])

    def _first(self, ins, outs, sems, i, me):
        cps = [self._copy(ins, outs, sems, i, 0, me, _flip(me, 1), own=True)]
        cps += [self._copy(ins, outs, sems, i, 1 + j, me, _flip(me, rel), own=True) for j, rel in enumerate(self.chips)]
        return cps

    def _passed(self, ins, outs, sems, i, j, me):
        return self._copy(ins, outs, sems, i, 4 + j, _flip(me, self.chips[j]), _flip(me, 1))

    def before(self, ins, outs, sems):
        n = len(self.inputs)
        me = _mesh_position()

        @pl.when(pl.program_id(0) == 0)
        def _():
            for i in range(n):
                self._local(ins, outs, sems, i, me).start()
                for cp in self._first(ins, outs, sems, i, me):
                    cp.start()

        @pl.when(pl.program_id(0) == pl.num_programs(0) - 1)
        def _():
            for j, rel in enumerate(self.chips):
                for i in range(n):
                    self._copy(ins, outs, sems, i, 1 + j, _flip(me, rel), me).wait_recv()
                    self._passed(ins, outs, sems, i, j, me).start()

    def after(self, ins, outs, sems):
        n = len(self.inputs)
        me = _mesh_position()
        sibling = _flip(me, 1)

        @pl.when(pl.program_id(0) == pl.num_programs(0) - 1)
        def _():
            for i in range(n):
                self._copy(ins, outs, sems, i, 0, sibling, me).wait_recv()
                for j, rel in enumerate(self.chips):
                    self._copy(ins, outs, sems, i, 4 + j, _flip(sibling, rel), me).wait_recv()
            for i in range(n):
                for cp in self._first(ins, outs, sems, i, me):
                    cp.wait_send()
                for j in range(len(self.chips)):
                    self._passed(ins, outs, sems, i, j, me).wait_send()
                self._local(ins, outs, sems, i, me).wait()


N_CHIPS = N_DEV // 2


def _chip(pos):
    return 2 * pos[0] + pos[1]


class _PairSwap:
    rels = frozenset((1,))

    def __init__(self, arrays):
        self.inputs = list(arrays)
        self.rows = [a.shape[0] // N_DEV for a in arrays]
        for r in self.rows:
            assert r % FLAT_ROWS == 0, r
        self.out_shape = [jax.ShapeDtypeStruct((N_CHIPS, r, a.shape[1]), a.dtype) for a, r in zip(arrays, self.rows)]
        n = len(arrays)
        self.sems = [pltpu.SemaphoreType.DMA((n, N_CHIPS)), pltpu.SemaphoreType.DMA((n, N_CHIPS))]

    def _copy(self, ins, outs, sems, i, j, me):
        sibling = _flip(me, 1)
        return pltpu.make_async_remote_copy(
            src_ref=_block_of(ins[i], 2 * j + sibling[2], self.rows[i], True), dst_ref=outs[i].at[j],
            send_sem=sems[0].at[i, j], recv_sem=sems[1].at[i, j], device_id=sibling, device_id_type=MESH)

    def before(self, ins, outs, sems):
        me = _mesh_position()

        @pl.when(pl.program_id(0) == 0)
        def _():
            for i in range(len(self.inputs)):
                for j in range(N_CHIPS):
                    self._copy(ins, outs, sems, i, j, me).start()

    def after(self, ins, outs, sems):
        me = _mesh_position()

        @pl.when(pl.program_id(0) == pl.num_programs(0) - 1)
        def _():
            for i in range(len(self.inputs)):
                for j in range(N_CHIPS):
                    self._copy(ins, outs, sems, i, j, me).wait()


class _ChipExchange:
    chips = (4, 2, 6)
    rels = frozenset(chips)

    def __init__(self, arrays, after=()):
        self.arrays = len(arrays)
        self.inputs = list(arrays) + list(after)
        self.out_shape = [jax.ShapeDtypeStruct(a.shape, a.dtype) for a in arrays]
        n = len(arrays)
        self.sems = [pltpu.SemaphoreType.DMA((n, 3)), pltpu.SemaphoreType.DMA((n, 3)), pltpu.SemaphoreType.DMA((n,))]

    def _send(self, ins, outs, sems, i, k, me):
        peer = _flip(me, self.chips[k])
        return pltpu.make_async_remote_copy(
            src_ref=ins[i].at[_chip(peer)], dst_ref=outs[i].at[_chip(me)], send_sem=sems[0].at[i, k],
            recv_sem=sems[1].at[i, k], device_id=peer, device_id_type=MESH)

    def _arrival(self, ins, outs, sems, i, k, me):
        peer = _flip(me, self.chips[k])
        return pltpu.make_async_remote_copy(
            src_ref=ins[i].at[_chip(me)], dst_ref=outs[i].at[_chip(peer)], send_sem=sems[0].at[i, k],
            recv_sem=sems[1].at[i, k], device_id=peer, device_id_type=MESH)

    def _local(self, ins, outs, sems, i, me):
        return pltpu.make_async_copy(ins[i].at[_chip(me)], outs[i].at[_chip(me)], sems[2].at[i])

    def before(self, ins, outs, sems):
        me = _mesh_position()

        @pl.when(pl.program_id(0) == 0)
        def _():
            for i in range(self.arrays):
                self._local(ins, outs, sems, i, me).start()
                for k in range(len(self.chips)):
                    self._send(ins, outs, sems, i, k, me).start()

    def after(self, ins, outs, sems):
        me = _mesh_position()

        @pl.when(pl.program_id(0) == pl.num_programs(0) - 1)
        def _():
            for i in range(self.arrays):
                for k in range(len(self.chips)):
                    self._arrival(ins, outs, sems, i, k, me).wait_recv()
            for i in range(self.arrays):
                for k in range(len(self.chips)):
                    self._send(ins, outs, sems, i, k, me).wait_send()
                self._local(ins, outs, sems, i, me).wait()


_HBM = pl.BlockSpec(memory_space=pltpu.HBM)
_SEM = pl.BlockSpec(memory_space=pltpu.SEMAPHORE)
_DATAFLOW = pltpu.SideEffectType.DATAFLOW_SIDE_EFFECTING
_CHIP_RELS = (4, 2, 6)
_SPLIT_COLLECTIVE_ID = 3


def _chips_copy(src_ref, dst_ref, send_sems, recv_sems, k, me):
    peer = _flip(me, _CHIP_RELS[k])
    return pltpu.make_async_remote_copy(
        src_ref=src_ref.at[_chip(peer)], dst_ref=dst_ref.at[_chip(me)], send_sem=send_sems.at[k],
        recv_sem=recv_sems.at[k], device_id=peer, device_id_type=MESH)


def _chips_start(pairs, name):
    def body(p_ref, land_ref, send_sems, recv_sems, p_thru, land_thru, token):
        me = _mesh_position()
        sem = pltpu.get_barrier_semaphore()
        for rel in _CHIP_RELS:
            pl.semaphore_signal(sem, inc=1, device_id=_flip(me, rel), device_id_type=MESH)
        pl.semaphore_wait(sem, len(_CHIP_RELS))
        for k in range(len(_CHIP_RELS)):
            _chips_copy(p_ref, land_ref, send_sems, recv_sems, k, me).start()
        token[...] = jnp.zeros_like(token)

    n = len(_CHIP_RELS)
    return pl.pallas_call(
        body, name="chips_start_" + name,
        out_shape=(pltpu.SemaphoreType.DMA((n,)), pltpu.SemaphoreType.DMA((n,)), pltpu.HBM(pairs.shape, pairs.dtype),
                   pltpu.HBM(pairs.shape, pairs.dtype), jax.ShapeDtypeStruct((SUBLANES, LANES), F32)),
        in_specs=(_HBM, _HBM), out_specs=(_SEM, _SEM, _HBM, _HBM, pl.BlockSpec(memory_space=pltpu.VMEM)),
        input_output_aliases={0: 2, 1: 3},
        compiler_params=pltpu.CompilerParams(has_side_effects=_DATAFLOW, collective_id=_SPLIT_COLLECTIVE_ID),
    )(pltpu.with_memory_space_constraint(pairs, pltpu.HBM),
      pltpu.with_memory_space_constraint(lax.empty(pairs.shape, pairs.dtype), pltpu.HBM))


def _chips_wait(send_sems, recv_sems, p_thru, land_thru, after, name):
    def body(p_ref, land_ref, send_sems, recv_sems, *rest):
        me = _mesh_position()
        for k in range(len(_CHIP_RELS)):
            copy = _chips_copy(p_ref, land_ref, send_sems, recv_sems, k, me)
            copy.wait_send()
            copy.wait_recv()

    return pl.pallas_call(
        body, name="chips_wait_" + name,
        out_shape=(pltpu.HBM(p_thru.shape, p_thru.dtype), pltpu.HBM(p_thru.shape, p_thru.dtype)),
        in_specs=(_HBM, _HBM, _SEM, _SEM) + (_ANY,) * len(after), out_specs=(_HBM, _HBM),
        input_output_aliases={0: 0, 1: 1}, compiler_params=pltpu.CompilerParams(has_side_effects=_DATAFLOW),
    )(p_thru, land_thru, send_sems, recv_sems, *after)


_GATHER_START_ID, _GATHER_PASS_ID = 4, 5


class _Rows:
    def __init__(self, ref, width):
        self.ref, self.width, self.at = ref, width, self

    def __getitem__(self, idx):
        i, k = idx
        return self.ref.at[i * self.width + k]


def _handshake(rels):
    me = _mesh_position()
    sem = pltpu.get_barrier_semaphore()
    for rel in rels:
        pl.semaphore_signal(sem, inc=1, device_id=_flip(me, rel), device_id_type=MESH)
    pl.semaphore_wait(sem, len(rels))


def _gather_start(shards, name, after=()):
    job, n, na = _Gather(shards), len(shards), len(after)

    def body(*refs):
        ins, lands = refs[:n], refs[n:2 * n]
        send, recv, local = refs[2 * n + na:2 * n + na + 3]
        sems = (_Rows(send, N_DEV - 1), _Rows(recv, N_DEV - 1), local)
        token = refs[-1]
        me = _mesh_position()
        _handshake(sorted(job.rels))
        for i in range(n):
            job._local(ins, lands, sems, i, me).start()
            for cp in job._first(ins, lands, sems, i, me):
                cp.start()
        token[...] = jnp.zeros_like(token)

    res = pl.pallas_call(
        body, name=name,
        out_shape=(pltpu.SemaphoreType.DMA((n * (N_DEV - 1),)), pltpu.SemaphoreType.DMA((n * (N_DEV - 1),)),
                   pltpu.SemaphoreType.DMA((n,))) + tuple(pltpu.HBM(s.shape, s.dtype) for s in job.out_shape)
        + (jax.ShapeDtypeStruct((SUBLANES, LANES), F32),),
        in_specs=(_HBM,) * (2 * n) + (_ANY,) * na,
        out_specs=(_SEM,) * 3 + (_HBM,) * n + (pl.BlockSpec(memory_space=pltpu.VMEM),),
        input_output_aliases={n + i: 3 + i for i in range(n)},
        compiler_params=pltpu.CompilerParams(has_side_effects=_DATAFLOW, collective_id=_GATHER_START_ID),
    )(*[pltpu.with_memory_space_constraint(s, pltpu.HBM) for s in shards],
      *[pltpu.with_memory_space_constraint(lax.empty(s.shape, s.dtype), pltpu.HBM) for s in job.out_shape], *after)
    return list(res[:3]), list(res[3:3 + n]), res[-1]


def _gather_pass(shards, which, sems, lands, name, after=()):
    job, n, m, na = _Gather(shards), len(shards), len(which), len(after)

    def body(*refs):
        lands_r, local = refs[:m], refs[m + 2]
        send, recv = _Rows(refs[m], N_DEV - 1), _Rows(refs[m + 1], N_DEV - 1)
        send2, recv2 = _Rows(refs[m + 3 + na], 3), _Rows(refs[m + 3 + na + 1], 3)
        me = _mesh_position()
        _handshake((1,))
        by_index = {i: lands_r[q] for q, i in enumerate(which)}
        for j, rel in enumerate(job.chips):
            for q, i in enumerate(which):
                job._copy(None, by_index, (send, recv, local), i, 1 + j, _flip(me, rel), me).wait_recv()
                job._copy(None, by_index, (send2, recv2), q, j, _flip(me, rel), _flip(me, 1), lands_index=i).start()

    res = pl.pallas_call(
        body, name=name,
        out_shape=(pltpu.SemaphoreType.DMA((m * 3,)), pltpu.SemaphoreType.DMA((m * 3,)))
        + tuple(pltpu.HBM(lands[i].shape, lands[i].dtype) for i in which),
        in_specs=(_HBM,) * m + (_SEM,) * 3 + (_ANY,) * na, out_specs=(_SEM, _SEM) + (_HBM,) * m,
        input_output_aliases={q: 2 + q for q in range(m)},
        compiler_params=pltpu.CompilerParams(has_side_effects=_DATAFLOW, collective_id=_GATHER_PASS_ID),
    )(*[lands[i] for i in which], *sems, *after)
    return list(res[:2]), list(res[2:])


def _gather_finish(shards, which, sems, sems2, lands, after, name):
    job, n, m = _Gather(shards), len(shards), len(which)

    def body(*refs):
        ins, lands_r = refs[:m], refs[m:2 * m]
        send, recv, local = _Rows(refs[2 * m], N_DEV - 1), _Rows(refs[2 * m + 1], N_DEV - 1), refs[2 * m + 2]
        send2, recv2 = _Rows(refs[2 * m + 3], 3), _Rows(refs[2 * m + 4], 3)
        me = _mesh_position()
        sibling = _flip(me, 1)
        by_index = {i: lands_r[q] for q, i in enumerate(which)}
        own = {i: ins[q] for q, i in enumerate(which)}
        for q, i in enumerate(which):
            job._copy(None, by_index, (send, recv, local), i, 0, sibling, me).wait_recv()
            for j, rel in enumerate(job.chips):
                job._copy(None, by_index, (send2, recv2), q, j, _flip(sibling, rel), me, lands_index=i).wait_recv()
                job._copy(None, by_index, (send2, recv2), q, j, _flip(me, rel), sibling, lands_index=i).wait_send()
            for cp in job._first(own, by_index, (send, recv, local), i, me):
                cp.wait_send()
            job._local(own, by_index, (send, recv, local), i, me).wait()

    res = pl.pallas_call(
        body, name=name, out_shape=tuple(pltpu.HBM(lands[i].shape, lands[i].dtype) for i in which),
        in_specs=(_HBM,) * (2 * m) + (_SEM,) * 5 + (_ANY,) * len(after), out_specs=(_HBM,) * m,
        input_output_aliases={m + q: q for q in range(m)},
        compiler_params=pltpu.CompilerParams(has_side_effects=_DATAFLOW),
    )(*[pltpu.with_memory_space_constraint(shards[i], pltpu.HBM) for i in which], *[lands[i] for i in which],
      *sems, *sems2, *after)
    return list(res)


def _pair_add(grads, halves, name):
    n = len(grads)

    def body(core_ref, *refs):
        g_refs, h_refs, o_refs = refs[:n], refs[n:2 * n], refs[2 * n:]
        for i in range(n):
            o_refs[i][...] = (g_refs[i][...].astype(F32) + h_refs[i][...].astype(F32)).astype(o_refs[i].dtype)

    def own_block(h):
        return pl.BlockSpec(h.shape[1:], lambda j, core: (2 * j + core[0], 0))

    def slot(h):
        return pl.BlockSpec((None,) + h.shape[1:], lambda j, core: (j, 0, 0))

    spec = pltpu.PrefetchScalarGridSpec(
        num_scalar_prefetch=1, grid=(N_CHIPS,),
        in_specs=[own_block(h) for h in halves] + [slot(h) for h in halves], out_specs=[slot(h) for h in halves])
    return pl.pallas_call(
        body, name=name, grid_spec=spec, out_shape=[jax.ShapeDtypeStruct(h.shape, h.dtype) for h in halves],
        compiler_params=_params("parallel"),
    )(lax.axis_index("c").astype(jnp.int32).reshape(1), *grads, *halves)


class _Both:
    def __init__(self, first, second):
        self.jobs = (first, second)
        self.rels = first.rels | second.rels
        self.inputs = first.inputs + second.inputs
        self.out_shape = first.out_shape + second.out_shape
        self.sems = first.sems + second.sems

    def _each(self, ins, outs, sems):
        a = self.jobs[0]
        i, o, s = len(a.inputs), len(a.out_shape), len(a.sems)
        return ((a, ins[:i], outs[:o], sems[:s]), (self.jobs[1], ins[i:], outs[o:], sems[s:]))

    def before(self, ins, outs, sems):
        for job, i, o, s in self._each(ins, outs, sems):
            job.before(i, o, s)

    def after(self, ins, outs, sems):
        for job, i, o, s in self._each(ins, outs, sems):
            job.after(i, o, s)


_COLLECTIVE_IDS = {(1,): 0, (2, 4, 6): 1, (1, 2, 4, 6): 2}


def _entry_barrier(rels):
    @pl.when(pl.program_id(0) == 0)
    def _():
        me = _mesh_position()
        sem = pltpu.get_barrier_semaphore()
        for rel in rels:
            pl.semaphore_signal(sem, inc=1, device_id=_flip(me, rel), device_id_type=MESH)
        pl.semaphore_wait(sem, len(rels))


def _run(body, comm, *, semantics, out_shape, in_specs, out_specs, scratch_shapes=(), **kw):
    if comm is None:
        return pl.pallas_call(body, out_shape=out_shape, in_specs=in_specs, out_specs=out_specs,
                              scratch_shapes=list(scratch_shapes), compiler_params=_params(semantics), **kw)
    single = not isinstance(out_shape, (list, tuple))
    outs = [out_shape] if single else list(out_shape)
    ospecs = [out_specs] if single else list(out_specs)
    counts = [len(in_specs), len(comm.inputs), len(outs), len(comm.out_shape), len(scratch_shapes), len(comm.sems)]
    rels = tuple(sorted(comm.rels))

    def carrying(*refs):
        groups, pos = [], 0
        for c in counts:
            groups.append(refs[pos:pos + c])
            pos += c
        main_in, comm_in, main_out, comm_out, main_scratch, comm_sems = groups
        _entry_barrier(rels)
        comm.before(comm_in, comm_out, comm_sems)
        body(*main_in, *main_out, *main_scratch)
        comm.after(comm_in, comm_out, comm_sems)

    call = pl.pallas_call(
        carrying, out_shape=outs + list(comm.out_shape), in_specs=list(in_specs) + [_ANY] * len(comm.inputs),
        out_specs=ospecs + [_ANY] * len(comm.out_shape), scratch_shapes=list(scratch_shapes) + list(comm.sems),
        compiler_params=pltpu.CompilerParams(dimension_semantics=("arbitrary",), vmem_limit_bytes=VMEM_LIMIT,
                                             collective_id=_COLLECTIVE_IDS[rels]), **kw)

    def apply(*args):
        res = call(*args, *comm.inputs)
        main = res[:len(outs)]
        return (main[0] if single else list(main)), list(res[len(outs):])

    return apply


def _alone(comm, name):
    return _run(lambda: None, comm, semantics="arbitrary", name=name, grid=(1,), out_shape=[], in_specs=[], out_specs=[])()[1]


SHARDED = {"w_in": 1, "w_glu": 0, "conv_w": 1, "w_a_out": 1, "w_b_out": 0, "w_o": 0, "w_ffn_gate": 1, "w_ffn_up": 1,
           "w_ffn_down": 0, "w_ple_gate": 0, "w_ple": 1}
TRANSPOSED = ("w_in", "w_a_out", "w_ffn_gate", "w_ffn_up", "w_ple")
LONG_AXIS_MINOR = ("w_in", "w_ffn_gate", "w_ffn_up")
NARROW_LAST = ("s5_b_re", "s5_b_im", "s5_d")
ADAMW_GROUPS = (("w_in",), ("w_glu",), ("conv_w",), ("w_a_out",), ("w_b_out", "w_o", "w_ple_gate"),
                ("w_ffn_gate", "w_ffn_up"), ("w_ffn_down",), ("w_ple",))
SMALL = ["g_mix", "b_in", "lam_re", "lam_im", "log_dt", "s5_b_re", "s5_b_im", "s5_c_re", "s5_c_im", "s5_d", "b_glu",
         "conv_b", "w_r", "b_r", "w_i", "b_i", "lru_lambda", "g_ffn", "g_ple_gate", "b_ple_gate", "g_ple", "g_final"]
WEIGHTS = ["g_mix", "w_in", "b_in", "lam_re", "lam_im", "log_dt", "s5_b_re", "s5_b_im", "s5_c_re", "s5_c_im", "s5_d",
           "w_glu", "b_glu", "conv_w", "conv_b", "w_r", "b_r", "w_i", "b_i", "lru_lambda", "w_a_out", "w_b_out", "w_o",
           "g_ffn", "w_ffn_gate", "w_ffn_up", "w_ffn_down", "g_ple_gate", "w_ple_gate", "b_ple_gate", "w_ple", "g_ple",
           "g_final"]


def _join_columns(gathered):
    nb, r, c = gathered.shape
    return jnp.transpose(gathered, (1, 0, 2)).reshape(r, nb * c)


def _disc_scalars(lr, li, ldt):
    dt = jnp.exp(ldt)
    mag = jnp.exp(lr * dt)
    ar = mag * jnp.cos(li * dt)
    ai = mag * jnp.sin(li * dt)
    den = lr * lr + li * li
    nr = ar - 1.0
    fr = (nr * lr + ai * li) / den
    fi = (ai * lr - nr * li) / den
    return ar, ai, fr, fi


def _disc_cols(lr, li, ldt, b_re, b_im):
    ar, ai, fr, fi = _disc_scalars(lr, li, ldt)
    return ar, ai, fr * b_re - fi * b_im, fr * b_im + fi * b_re


def _local_step(x, p, target, src, small, disc, distributed=True):
    full = {} if distributed else dict(src)
    gw, halves, pairs, got = {}, {}, {}, {}

    def gather(keys):
        return (_Gather([src[k] for k in keys]), keys, full) if distributed else None

    def swap(keys):
        return (_PairSwap([gw[k] for k in keys]), keys, halves) if distributed else None

    def chips(keys):
        return (_ChipExchange([pairs[k] for k in keys]), keys, got) if distributed else None

    def add_pairs(keys):
        if distributed:
            pairs.update(zip(keys, _pair_add([gw[k] for k in keys], [halves[k] for k in keys], "pair_add_" + keys[0])))

    def carry(fn, *args, jobs=()):
        jobs = [j for j in jobs if j is not None]
        if not jobs:
            return fn(*args)
        comm = jobs[0][0]
        for j in jobs[1:]:
            comm = _Both(comm, j[0])
        res, extra = fn(*args, comm=comm)
        for job, keys, sink in jobs:
            sink.update(zip(keys, extra[:len(job.out_shape)]))
            extra = extra[len(job.out_shape):]
        return res

    d = x.shape[1]
    g, n, pch = small["s5_b_re"].shape
    sa, lw = g * pch, small["lru_lambda"].shape[-1]
    widths = [sa, lw, d, d]
    row = lambda v: v.reshape(1, -1)

    hd = small["w_r"].shape[-1]
    ar_row, ai_row, bbr_blk, bbi_blk, cre_blk, cimn_blk, wr_blk, wi_blk = carry(
        _prep, *disc["rows"], *disc["cols"], disc["b_re"], disc["b_im"], small["s5_c_re"].reshape(sa, n),
        small["s5_c_im"].reshape(sa, n), small["w_r"].reshape(lw, hd), small["w_i"].reshape(lw, hd),
        jobs=[gather(["w_in"])])
    d_row = row(small["s5_d"])
    g_mix_row = row(small["g_mix"])
    if distributed:
        later = ["w_glu", "conv_w", "w_a_out", "w_b_out", "w_ffn_gate", "w_ffn_up", "w_o", "w_ffn_down", "w_ple_gate",
                 "w_ple"]
        wires = [src[k] for k in later]
        flight, lands, token = _gather_start(wires, "gather_start", after=[ar_row])
        g_mix_row = g_mix_row + token[0, 0]

        def arrive(keys, after):
            which = [later.index(k) for k in keys]
            passed, moved = _gather_pass(wires, which, flight, lands, "gather_pass_" + keys[0], after)
            for q, i in enumerate(which):
                lands[i] = moved[q]
            full.update(zip(keys, _gather_finish(wires, which, flight, passed, lands, (), "gather_finish_" + keys[0])))
    else:
        arrive = lambda keys, after: None

    u_a, u_b, za, zb = _inproj_fwd(x, g_mix_row, full["w_in"], row(small["b_in"]), widths)
    arrive(["w_glu", "conv_w", "w_a_out", "w_b_out"], [u_a])
    conv_w = _join_columns(full["conv_w"]) if distributed else full["conv_w"]
    branches = [_s5_fwd(u_a, bbr_blk, bbi_blk, ar_row, ai_row, cre_blk, cimn_blk, d_row, full["w_glu"],
                        row(small["b_glu"])),
                _lru_fwd(u_b, conv_w, row(small["conv_b"]), wr_blk, row(small["b_r"]), wi_blk, row(small["b_i"]),
                         row(small["lru_lambda"]))]
    (sr, si, y, y_a), (xc, h, hprev) = _stages(branches, "branches_fwd")
    arrive(["w_ffn_gate", "w_ffn_up", "w_o"], [y_a])
    x1, merged, ma, mb, fg, fu = _merge_ffn_up_fwd(y_a, h, za, zb, x, full["w_a_out"], full["w_b_out"], full["w_o"],
                                                   row(small["g_ffn"]), full["w_ffn_gate"], full["w_ffn_up"])
    arrive(["w_ffn_down", "w_ple_gate", "w_ple"], [x1])
    x2 = _ffn_down_fwd(fg, fu, x1, full["w_ffn_down"])

    dx2, loss_blk, gw["w_ple_gate"], gw["w_ple"], vec_tail = _tail_fwd_bwd(
        x2, p, target, row(small["g_ple_gate"]), full["w_ple_gate"], row(small["b_ple_gate"]), full["w_ple"],
        row(small["g_ple"]), row(small["g_final"]))
    dfg, dfu, act = carry(_ffn_bwd_a, dx2, fg, fu, full["w_ffn_down"], jobs=[swap(["w_ple_gate", "w_ple"])])
    tn_d = min(d, 512)
    gw["w_ffn_down"] = _matmul_tn(act, dx2, tn_d, "dw_ffn_down", BF16)
    add_pairs(["w_ple_gate", "w_ple"])
    dx1, h2, vec_ffn = carry(_ffn_bwd_b, dfg, dfu, x1, dx2, row(small["g_ffn"]), full["w_ffn_gate"], full["w_ffn_up"],
                             jobs=[chips(["w_ple_gate", "w_ple"]), swap(["w_ffn_down"])])
    gw["w_ffn_gate"] = _matmul_tn(dfg, h2, tn_d, "dw_ffn_gate", BF16)
    gw["w_ffn_up"] = _matmul_tn(dfu, h2, tn_d, "dw_ffn_up", BF16)
    add_pairs(["w_ffn_down"])
    dza, dzb, dya, dyb, gw["w_o"], gw["w_a_out"], gw["w_b_out"] = carry(
        _merge_bwd, dx1, merged, ma, mb, za, zb, y_a, h, full["w_o"], full["w_a_out"], full["w_b_out"],
        jobs=[chips(["w_ffn_down"]), swap(["w_ffn_gate", "w_ffn_up"])])
    add_pairs(["w_ffn_gate", "w_ffn_up"])
    own = {}
    du_b, dw_r, dw_i, vec_lru = carry(
        _lru_bwd, dyb, xc, hprev, u_b, conv_w, wr_blk, row(small["b_r"]), wi_blk, row(small["b_i"]),
        row(small["lru_lambda"]), hd, jobs=[chips(["w_ffn_gate"]), swap(["w_o", "w_a_out", "w_b_out"])])
    add_pairs(["w_o", "w_a_out", "w_b_out"])
    du_a, lam_r, lam_i, dy16, gw["w_glu"], vec_sa, vec_gn = carry(
        _s5_bwd, dya, y, sr, si, u_a, full["w_glu"], row(small["b_glu"]), cre_blk, cimn_blk, bbr_blk, bbi_blk, ar_row,
        ai_row, d_row, jobs=[chips(["w_ffn_up"]), chips(["w_o", "w_a_out", "w_b_out"])])
    smalls = {"vec_tail": vec_tail, "vec_ffn": vec_ffn, "vec_lru": vec_lru, "vec_sa": vec_sa, "vec_gn": vec_gn,
              "dw_r": dw_r, "dw_i": dw_i, "loss": loss_blk}
    everyones = {}

    def gather_smalls(keys):
        return (_Gather([smalls[k] for k in keys]), keys, everyones) if distributed else None

    smalls["dbb_re"], smalls["dbb_im"], smalls["dc_re"], smalls["dc_imn"] = carry(
        _s5_param_grads, lam_r, lam_i, sr, si, u_a, dy16, pch, n, jobs=[swap(["w_glu"]), gather_smalls(list(smalls))])
    add_pairs(["w_glu"])
    dz = [du_a, du_b, dza, dzb]
    grad_x, h0, smalls["vec_mix"], smalls["vec_bin"] = _inproj_bwd(
        dz, x, dx1, row(small["g_mix"]), full["w_in"])
    shapes = {k: a.shape for k, a in smalls.items()}
    gw["w_in"] = carry(_dw_from_parts, dz, h0, tn_d, "dw_in",
                       jobs=[chips(["w_glu"]),
                             gather_smalls(["dbb_re", "dbb_im", "dc_re", "dc_imn", "vec_mix", "vec_bin"])])
    smalls.update(everyones)
    if distributed:
        got["w_in"] = gw["w_in"]
        gw = got
    return grad_x, gw, smalls, shapes, own


def _disc_inputs(small):
    g, n, pch = small["s5_b_re"].shape
    srcs = (small["lam_re"], small["lam_im"], jnp.repeat(small["log_dt"], n))
    return {"rows": [a.reshape(1, g * n) for a in srcs], "cols": [a.reshape(g * n, 1) for a in srcs],
            "b_re": small["s5_b_re"].reshape(g * n, pch), "b_im": small["s5_b_im"].reshape(g * n, pch)}


def kernel(x, p, g_mix, w_in, b_in, lam_re, lam_im, log_dt, s5_b_re, s5_b_im, s5_c_re, s5_c_im, s5_d, w_glu, b_glu, conv_w, conv_b, w_r, b_r, w_i, b_i, lru_lambda, w_a_out, w_b_out, w_o, g_ffn, w_ffn_gate, w_ffn_up, w_ffn_down, g_ple_gate, w_ple_gate, b_ple_gate, w_ple, g_ple, g_final, loss_target, m_g_mix, m_w_in, m_b_in, m_lam_re, m_lam_im, m_log_dt, m_s5_b_re, m_s5_b_im, m_s5_c_re, m_s5_c_im, m_s5_d, m_w_glu, m_b_glu, m_conv_w, m_conv_b, m_w_r, m_b_r, m_w_i, m_b_i, m_lru_lambda, m_w_a_out, m_w_b_out, m_w_o, m_g_ffn, m_w_ffn_gate, m_w_ffn_up, m_w_ffn_down, m_g_ple_gate, m_w_ple_gate, m_b_ple_gate, m_w_ple, m_g_ple, m_g_final, v_g_mix, v_w_in, v_b_in, v_lam_re, v_lam_im, v_log_dt, v_s5_b_re, v_s5_b_im, v_s5_c_re, v_s5_c_im, v_s5_d, v_w_glu, v_b_glu, v_conv_w, v_conv_b, v_w_r, v_b_r, v_w_i, v_b_i, v_lru_lambda, v_w_a_out, v_w_b_out, v_w_o, v_g_ffn, v_w_ffn_gate, v_w_ffn_up, v_w_ffn_down, v_g_ple_gate, v_w_ple_gate, v_b_ple_gate, v_w_ple, v_g_ple, v_g_final):
    given = dict(locals())
    wts = {k: given[k] for k in WEIGHTS}
    moms = {k: given["m_" + k] for k in WEIGHTS}
    vels = {k: given["v_" + k] for k in WEIGHTS}

    def drop_depth(k, a):
        return a if k == "g_final" else a[0]

    small = {k: drop_depth(k, wts[k]) for k in SMALL}
    shard = {k: wts[k][0] for k in SHARDED}
    names = list(SHARDED)

    def wire(k):
        if k == "conv_w":
            return shard[k]
        return (shard[k].T if k in TRANSPOSED else shard[k]).astype(BF16)

    disc = _disc_inputs(small)
    grad_x, parts, smalls, shapes, own = _local_step(x[0], p[0, 0], loss_target[0], {k: wire(k) for k in names}, small,
                                                     disc)

    (pair_in,) = _pair_add([parts["w_in"]], _alone(_PairSwap([parts["w_in"]]), "swap_last"), "pair_add_w_in")
    send_sems, recv_sems, pair_in, landing, token = _chips_start(pair_in, "w_in")
    ordered = (token,)

    g_small = _small_reduce(smalls, shapes, *disc["cols"], disc["b_re"], disc["b_im"], small["s5_b_re"].shape[0],
                            after=ordered)
    loss = g_small.pop("loss")[0, 0]
    cols = shard["conv_w"].shape[1]
    mine = _index((lax.axis_index("x"), lax.axis_index("y"), lax.axis_index("c")))
    parts["conv_w"] = lax.dynamic_slice_in_dim(g_small.pop("conv_w"), mine * cols, cols, axis=1)[None]

    def view(k, a):
        a = a.reshape((1, -1) if k == "g_final" else wts[k].shape)
        return jnp.swapaxes(a, -1, -2) if k in NARROW_LAST else a

    def unview(k, a):
        return (jnp.swapaxes(a, -1, -2) if k in NARROW_LAST else a).reshape(wts[k].shape)

    slots = _adamw_small([view(k, wts[k]) for k in SMALL], [view(k, g_small[k]) for k in SMALL],
                         [view(k, moms[k]) for k in SMALL], [view(k, vels[k]) for k in SMALL])
    small_out = [dict(zip(SMALL, [unview(k, a) for k, a in zip(SMALL, slot)])) for slot in slots]

    big_out = {}
    between = [slots[0][0]]
    for group in sorted(ADAMW_GROUPS, key=lambda g: g[0] == "w_in"):
        flip = group[0] in LONG_AXIS_MINOR
        look = (lambda a: a.T) if flip else (lambda a: a)
        last = group[0] == "w_in"
        if last:
            own["w_in"], parts["w_in"] = _chips_wait(send_sems, recv_sems, pair_in, landing, between, "w_in")
        res = _adamw([parts[k] for k in group], [look(shard[k]) for k in group], [look(moms[k][0]) for k in group],
                     [look(vels[k][0]) for k in group], "adamw_" + group[0],
                     transposed=group[0] in TRANSPOSED and not flip,
                     own=[own[k] for k in group] if group[0] in own else None, after=() if last else ordered)
        between.append(res[0][0])
        for k, outs4 in zip(group, res):
            big_out[k] = [look(a) for a in outs4]

    outs = [loss, grad_x[None]]
    for slot in range(4):
        for k in WEIGHTS:
            if k in SHARDED:
                outs.append(big_out[k][slot][None])
            else:
                outs.append(small_out[slot][k])
    return tuple(outs)
```
